```python
import jax, jax.numpy as jnp
from jax import lax
import numpy as np

D_MODEL = 1024
BATCH = 8
SEQ = 4096
DEPTH = 2

CHUNK = 64
D_RNN = 1280
RNN_HEADS = 20
RNN_HEAD_DIM = D_RNN // RNN_HEADS
CONV_WIDTH = 4
LRU_C = 8.0
D_SGU = 1024
SGU_GROUPS = 8
SGU_GROUP_DIM = D_SGU // SGU_GROUPS
SGU_BLOCK = 128
N_BRANCH = 2
D_FF = 4 * D_MODEL
D_IN = 2 * D_RNN + 2 * D_SGU + N_BRANCH * D_MODEL
EPS = 1e-6

kernel_name = "hybrid_rglru_sgu_gated_encoder"


def rmsnorm(x, g):
    xf = x.astype(jnp.float32)
    y = xf * lax.rsqrt(jnp.mean(xf * xf, axis=-1, keepdims=True) + EPS)
    return (y * g.astype(jnp.float32)).astype(x.dtype)


def layernorm(x, g, b):
    xf = x.astype(jnp.float32)
    mu = jnp.mean(xf, axis=-1, keepdims=True)
    xc = xf - mu
    y = xc * lax.rsqrt(jnp.mean(xc * xc, axis=-1, keepdims=True) + EPS)
    return (y * g.astype(jnp.float32) + b.astype(jnp.float32)).astype(x.dtype)


def causal_depthwise_conv(x, w, b):
    s = x.shape[1]
    xp = jnp.pad(x, ((0, 0), (CONV_WIDTH - 1, 0), (0, 0)))
    y = b
    for k in range(CONV_WIDTH):
        y = y + xp[:, k:k + s, :] * w[k]
    return y


def rg_lru(x, w_a, b_a, w_x, b_x, lam):
    bsz, s, _ = x.shape
    xh = x.reshape(bsz, s, RNN_HEADS, RNN_HEAD_DIM)
    r = jax.nn.sigmoid(jnp.einsum('bshi,hij->bshj', xh, w_a) + b_a).reshape(bsz, s, D_RNN)
    i = jax.nn.sigmoid(jnp.einsum('bshi,hij->bshj', xh, w_x) + b_x).reshape(bsz, s, D_RNN)
    log_a = (-LRU_C * r.astype(jnp.float32)) * jax.nn.softplus(-lam.astype(jnp.float32))
    a = jnp.exp(log_a)
    norm = jnp.sqrt(-jnp.expm1(2.0 * log_a))
    u = norm * (i * x).astype(jnp.float32)

    def combine(left, right):
        a_l, b_l = left
        a_r, b_r = right
        return a_l * a_r, a_r * b_l + b_r

    _, h = lax.associative_scan(combine, (a, u), axis=1)
    return h.astype(x.dtype)


def spatial_gating(u, v, w_s, b_s, ln_g, ln_b):
    bsz, s, _ = u.shape
    nblk = s // SGU_BLOCK
    v = layernorm(v, ln_g, ln_b)
    vb = v.reshape(bsz, nblk, SGU_BLOCK, SGU_GROUPS, SGU_GROUP_DIM)
    chunk_id = jnp.arange(SGU_BLOCK) // CHUNK
    mask = (chunk_id[:, None] >= chunk_id[None, :]).astype(w_s.dtype)
    mixed = jnp.einsum('gts,bnsgc->bntgc', w_s * mask, vb)
    mixed = mixed + jnp.transpose(b_s)[None, None, :, :, None]
    return u * mixed.reshape(bsz, s, D_SGU)


def hybrid_layer(x, norm_mix_g, w_in, conv_w, conv_b, lru_w_a, lru_b_a, lru_w_x, lru_b_x,
                 lru_lambda, sgu_ln_g, sgu_ln_b, sgu_w_s, sgu_b_s, w_branch_a, w_branch_b,
                 w_out, norm_ffn_g, w_up, w_down):
    h = rmsnorm(x, norm_mix_g)
    proj = jnp.einsum('bsd,de->bse', h, w_in)
    cuts = np.cumsum([D_RNN, D_RNN, D_SGU, D_SGU, D_MODEL])
    x_rnn, g_rnn, u, v, gate_a, gate_b = jnp.split(proj, cuts, axis=-1)

    xr = causal_depthwise_conv(x_rnn, conv_w, conv_b)
    ya = rg_lru(xr, lru_w_a, lru_b_a, lru_w_x, lru_b_x, lru_lambda) * jax.nn.gelu(g_rnn)
    ya = jnp.einsum('bsr,rd->bsd', ya, w_branch_a)

    yb = spatial_gating(jax.nn.gelu(u), jax.nn.gelu(v), sgu_w_s, sgu_b_s, sgu_ln_g, sgu_ln_b)
    yb = jnp.einsum('bsc,cd->bsd', yb, w_branch_b)

    merged = jax.nn.sigmoid(gate_a) * ya + jax.nn.sigmoid(gate_b) * yb
    x = x + jnp.einsum('bsd,de->bse', merged, w_out)

    h2 = rmsnorm(x, norm_ffn_g)
    f = jnp.square(jax.nn.relu(jnp.einsum('bsd,df->bsf', h2, w_up)))
    return x + jnp.einsum('bsf,fd->bsd', f, w_down)


def _fwd_setup_inputs(seed: int = 0) -> dict:
    key = jax.random.key(seed)
    ks = jax.random.split(key, 24)
    f32 = jnp.float32

    def nrm(k, shape, scale):
        return jax.random.normal(k, shape, f32) * scale

    a_c = jax.random.uniform(ks[7], (DEPTH, D_RNN), f32, 0.9, 0.999)
    a0 = a_c ** (1.0 / LRU_C)
    lru_lambda = jnp.log(a0) - jnp.log1p(-a0)

    return {
        "x": nrm(ks[0], (BATCH, SEQ, D_MODEL), 1.0),
        "norm_mix_g": 1.0 + nrm(ks[1], (DEPTH, D_MODEL), 0.02),
        "w_in": nrm(ks[2], (DEPTH, D_MODEL, D_IN), D_MODEL ** -0.5),
        "conv_w": nrm(ks[3], (DEPTH, CONV_WIDTH, D_RNN), CONV_WIDTH ** -0.5),
        "conv_b": nrm(ks[4], (DEPTH, D_RNN), 0.02),
        "lru_w_a": nrm(ks[5], (DEPTH, RNN_HEADS, RNN_HEAD_DIM, RNN_HEAD_DIM), RNN_HEAD_DIM ** -0.5),
        "lru_b_a": nrm(ks[6], (DEPTH, RNN_HEADS, RNN_HEAD_DIM), 0.02),
        "lru_w_x": nrm(ks[8], (DEPTH, RNN_HEADS, RNN_HEAD_DIM, RNN_HEAD_DIM), RNN_HEAD_DIM ** -0.5),
        "lru_b_x": nrm(ks[9], (DEPTH, RNN_HEADS, RNN_HEAD_DIM), 0.02),
        "lru_lambda": lru_lambda,
        "sgu_ln_g": 1.0 + nrm(ks[10], (DEPTH, D_SGU), 0.02),
        "sgu_ln_b": nrm(ks[11], (DEPTH, D_SGU), 0.02),
        "sgu_w_s": nrm(ks[12], (DEPTH, SGU_GROUPS, SGU_BLOCK, SGU_BLOCK), SGU_BLOCK ** -0.5),
        "sgu_b_s": 1.0 + nrm(ks[13], (DEPTH, SGU_GROUPS, SGU_BLOCK), 0.02),
        "w_branch_a": nrm(ks[14], (DEPTH, D_RNN, D_MODEL), D_RNN ** -0.5),
        "w_branch_b": nrm(ks[15], (DEPTH, D_SGU, D_MODEL), D_SGU ** -0.5),
        "w_out": nrm(ks[16], (DEPTH, D_MODEL, D_MODEL), D_MODEL ** -0.5),
        "norm_ffn_g": 1.0 + nrm(ks[17], (DEPTH, D_MODEL), 0.02),
        "w_up": nrm(ks[18], (DEPTH, D_MODEL, D_FF), D_MODEL ** -0.5),
        "w_down": nrm(ks[19], (DEPTH, D_FF, D_MODEL), D_FF ** -0.5),
        "final_norm_g": 1.0 + nrm(ks[20], (D_MODEL,), 0.02),
    }


def _fwd_reference(x, norm_mix_g, w_in, conv_w, conv_b, lru_w_a, lru_b_a, lru_w_x, lru_b_x,
              lru_lambda, sgu_ln_g, sgu_ln_b, sgu_w_s, sgu_b_s, w_branch_a, w_branch_b,
              w_out, norm_ffn_g, w_up, w_down, final_norm_g):
    for l in range(DEPTH):
        x = hybrid_layer(x, norm_mix_g[l], w_in[l], conv_w[l], conv_b[l], lru_w_a[l], lru_b_a[l],
                         lru_w_x[l], lru_b_x[l], lru_lambda[l], sgu_ln_g[l], sgu_ln_b[l],
                         sgu_w_s[l], sgu_b_s[l], w_branch_a[l], w_branch_b[l], w_out[l],
                         norm_ffn_g[l], w_up[l], w_down[l])
    return rmsnorm(x, final_norm_g)


import jax as _jax
import jax.numpy as _jnp

TWIN_FORMAT = 'train_step'
FWD_PARAMS = ['x', 'norm_mix_g', 'w_in', 'conv_w', 'conv_b', 'lru_w_a', 'lru_b_a', 'lru_w_x', 'lru_b_x', 'lru_lambda', 'sgu_ln_g', 'sgu_ln_b', 'sgu_w_s', 'sgu_b_s', 'w_branch_a', 'w_branch_b', 'w_out', 'norm_ffn_g', 'w_up', 'w_down', 'final_norm_g']
TWIN_WEIGHTS = ['norm_mix_g', 'w_in', 'conv_w', 'conv_b', 'lru_w_a', 'lru_b_a', 'lru_w_x', 'lru_b_x', 'lru_lambda', 'sgu_ln_g', 'sgu_ln_b', 'sgu_w_s', 'sgu_b_s', 'w_branch_a', 'w_branch_b', 'w_out', 'norm_ffn_g', 'w_up', 'w_down', 'final_norm_g']
TWIN_DIFF_INPUT = 'x'
TWIN_INPUTS = ['x', 'norm_mix_g', 'w_in', 'conv_w', 'conv_b', 'lru_w_a', 'lru_b_a', 'lru_w_x', 'lru_b_x', 'lru_lambda', 'sgu_ln_g', 'sgu_ln_b', 'sgu_w_s', 'sgu_b_s', 'w_branch_a', 'w_branch_b', 'w_out', 'norm_ffn_g', 'w_up', 'w_down', 'final_norm_g', 'loss_target', 'm_norm_mix_g', 'm_w_in', 'm_conv_w', 'm_conv_b', 'm_lru_w_a', 'm_lru_b_a', 'm_lru_w_x', 'm_lru_b_x', 'm_lru_lambda', 'm_sgu_ln_g', 'm_sgu_ln_b', 'm_sgu_w_s', 'm_sgu_b_s', 'm_w_branch_a', 'm_w_branch_b', 'm_w_out', 'm_norm_ffn_g', 'm_w_up', 'm_w_down', 'm_final_norm_g', 'v_norm_mix_g', 'v_w_in', 'v_conv_w', 'v_conv_b', 'v_lru_w_a', 'v_lru_b_a', 'v_lru_w_x', 'v_lru_b_x', 'v_lru_lambda', 'v_sgu_ln_g', 'v_sgu_ln_b', 'v_sgu_w_s', 'v_sgu_b_s', 'v_w_branch_a', 'v_w_branch_b', 'v_w_out', 'v_norm_ffn_g', 'v_w_up', 'v_w_down', 'v_final_norm_g']
TWIN_OUTPUTS = ['loss', 'grad_x', 'grad_norm_mix_g', 'grad_w_in', 'grad_conv_w', 'grad_conv_b', 'grad_lru_w_a', 'grad_lru_b_a', 'grad_lru_w_x', 'grad_lru_b_x', 'grad_lru_lambda', 'grad_sgu_ln_g', 'grad_sgu_ln_b', 'grad_sgu_w_s', 'grad_sgu_b_s', 'grad_w_branch_a', 'grad_w_branch_b', 'grad_w_out', 'grad_norm_ffn_g', 'grad_w_up', 'grad_w_down', 'grad_final_norm_g', 'delta_norm_mix_g', 'delta_w_in', 'delta_conv_w', 'delta_conv_b', 'delta_lru_w_a', 'delta_lru_b_a', 'delta_lru_w_x', 'delta_lru_b_x', 'delta_lru_lambda', 'delta_sgu_ln_g', 'delta_sgu_ln_b', 'delta_sgu_w_s', 'delta_sgu_b_s', 'delta_w_branch_a', 'delta_w_branch_b', 'delta_w_out', 'delta_norm_ffn_g', 'delta_w_up', 'delta_w_down', 'delta_final_norm_g', 'new_m_norm_mix_g', 'new_m_w_in', 'new_m_conv_w', 'new_m_conv_b', 'new_m_lru_w_a', 'new_m_lru_b_a', 'new_m_lru_w_x', 'new_m_lru_b_x', 'new_m_lru_lambda', 'new_m_sgu_ln_g', 'new_m_sgu_ln_b', 'new_m_sgu_w_s', 'new_m_sgu_b_s', 'new_m_w_branch_a', 'new_m_w_branch_b', 'new_m_w_out', 'new_m_norm_ffn_g', 'new_m_w_up', 'new_m_w_down', 'new_m_final_norm_g', 'new_v_norm_mix_g', 'new_v_w_in', 'new_v_conv_w', 'new_v_conv_b', 'new_v_lru_w_a', 'new_v_lru_b_a', 'new_v_lru_w_x', 'new_v_lru_b_x', 'new_v_lru_lambda', 'new_v_sgu_ln_g', 'new_v_sgu_ln_b', 'new_v_sgu_w_s', 'new_v_sgu_b_s', 'new_v_w_branch_a', 'new_v_w_branch_b', 'new_v_w_out', 'new_v_norm_ffn_g', 'new_v_w_up', 'new_v_w_down', 'new_v_final_norm_g']
TWIN_LEAF_KINDS = {'loss': 'loss', 'grad_x': 'grad_x', 'grad_norm_mix_g': 'grad_w', 'grad_w_in': 'grad_w', 'grad_conv_w': 'grad_w', 'grad_conv_b': 'grad_w', 'grad_lru_w_a': 'grad_w', 'grad_lru_b_a': 'grad_w', 'grad_lru_w_x': 'grad_w', 'grad_lru_b_x': 'grad_w', 'grad_lru_lambda': 'grad_w', 'grad_sgu_ln_g': 'grad_w', 'grad_sgu_ln_b': 'grad_w', 'grad_sgu_w_s': 'grad_w', 'grad_sgu_b_s': 'grad_w', 'grad_w_branch_a': 'grad_w', 'grad_w_branch_b': 'grad_w', 'grad_w_out': 'grad_w', 'grad_norm_ffn_g': 'grad_w', 'grad_w_up': 'grad_w', 'grad_w_down': 'grad_w', 'grad_final_norm_g': 'grad_w', 'delta_norm_mix_g': 'delta_w', 'delta_w_in': 'delta_w', 'delta_conv_w': 'delta_w', 'delta_conv_b': 'delta_w', 'delta_lru_w_a': 'delta_w', 'delta_lru_b_a': 'delta_w', 'delta_lru_w_x': 'delta_w', 'delta_lru_b_x': 'delta_w', 'delta_lru_lambda': 'delta_w', 'delta_sgu_ln_g': 'delta_w', 'delta_sgu_ln_b': 'delta_w', 'delta_sgu_w_s': 'delta_w', 'delta_sgu_b_s': 'delta_w', 'delta_w_branch_a': 'delta_w', 'delta_w_branch_b': 'delta_w', 'delta_w_out': 'delta_w', 'delta_norm_ffn_g': 'delta_w', 'delta_w_up': 'delta_w', 'delta_w_down': 'delta_w', 'delta_final_norm_g': 'delta_w', 'new_m_norm_mix_g': 'new_m', 'new_m_w_in': 'new_m', 'new_m_conv_w': 'new_m', 'new_m_conv_b': 'new_m', 'new_m_lru_w_a': 'new_m', 'new_m_lru_b_a': 'new_m', 'new_m_lru_w_x': 'new_m', 'new_m_lru_b_x': 'new_m', 'new_m_lru_lambda': 'new_m', 'new_m_sgu_ln_g': 'new_m', 'new_m_sgu_ln_b': 'new_m', 'new_m_sgu_w_s': 'new_m', 'new_m_sgu_b_s': 'new_m', 'new_m_w_branch_a': 'new_m', 'new_m_w_branch_b': 'new_m', 'new_m_w_out': 'new_m', 'new_m_norm_ffn_g': 'new_m', 'new_m_w_up': 'new_m', 'new_m_w_down': 'new_m', 'new_m_final_norm_g': 'new_m', 'new_v_norm_mix_g': 'new_v', 'new_v_w_in': 'new_v', 'new_v_conv_w': 'new_v', 'new_v_conv_b': 'new_v', 'new_v_lru_w_a': 'new_v', 'new_v_lru_b_a': 'new_v', 'new_v_lru_w_x': 'new_v', 'new_v_lru_b_x': 'new_v', 'new_v_lru_lambda': 'new_v', 'new_v_sgu_ln_g': 'new_v', 'new_v_sgu_ln_b': 'new_v', 'new_v_sgu_w_s': 'new_v', 'new_v_sgu_b_s': 'new_v', 'new_v_w_branch_a': 'new_v', 'new_v_w_branch_b': 'new_v', 'new_v_w_out': 'new_v', 'new_v_norm_ffn_g': 'new_v', 'new_v_w_up': 'new_v', 'new_v_w_down': 'new_v', 'new_v_final_norm_g': 'new_v'}


def _forward(args):
    return _fwd_reference(*[args[k] for k in FWD_PARAMS])


def _output_shape():
    def fwd():
        inp = _fwd_setup_inputs(0)
        return _fwd_reference(*[inp[k] for k in FWD_PARAMS])
    out = _jax.eval_shape(fwd)
    return out.shape, out.dtype

N_MICROBATCH = 1
ADAM_LR = 0.001
ADAM_B1 = 0.9
ADAM_B2 = 0.999
ADAM_EPS = 1e-08
ADAM_WD = 0.01
ADAM_STEP = 10
PER_EXAMPLE_BATCH_AXIS = {'x': 0, 'loss_target': 0}
SHARED_INPUTS = []
_WEIGHT_DTYPES = {'norm_mix_g': _jnp.float32, 'w_in': _jnp.float32, 'conv_w': _jnp.float32, 'conv_b': _jnp.float32, 'lru_w_a': _jnp.float32, 'lru_b_a': _jnp.float32, 'lru_w_x': _jnp.float32, 'lru_b_x': _jnp.float32, 'lru_lambda': _jnp.float32, 'sgu_ln_g': _jnp.float32, 'sgu_ln_b': _jnp.float32, 'sgu_w_s': _jnp.float32, 'sgu_b_s': _jnp.float32, 'w_branch_a': _jnp.float32, 'w_branch_b': _jnp.float32, 'w_out': _jnp.float32, 'norm_ffn_g': _jnp.float32, 'w_up': _jnp.float32, 'w_down': _jnp.float32, 'final_norm_g': _jnp.float32}
MOMENT_SCALE = {'norm_mix_g': 1.402411e-01, 'w_in': 5.561201e-02, 'conv_w': 7.391083e-02, 'conv_b': 4.029327e-01, 'lru_w_a': 1.441700e-02, 'lru_b_a': 1.505660e-02, 'lru_w_x': 2.782154e-02, 'lru_b_x': 2.954580e-02, 'lru_lambda': 3.236469e-02, 'sgu_ln_g': 4.990572e-02, 'sgu_ln_b': 4.644510e-02, 'sgu_w_s': 4.629870e-02, 'sgu_b_s': 5.356691e-02, 'w_branch_a': 7.758717e-02, 'w_branch_b': 7.945838e-02, 'w_out': 1.071202e-01, 'norm_ffn_g': 1.438643e-01, 'w_up': 7.288735e-02, 'w_down': 1.624651e-01, 'final_norm_g': 3.275926e+01}


def _to_microbatches(a, axis):
    t = _jnp.moveaxis(a, axis, 0)
    t = t.reshape((N_MICROBATCH, t.shape[0] // N_MICROBATCH) + t.shape[1:])
    return _jnp.moveaxis(t, 1, axis + 1)


def setup_inputs(seed: int = 0) -> dict:
    inp = _fwd_setup_inputs(seed)
    key = _jax.random.fold_in(_jax.random.key(seed), 7919)
    shape, _ = _output_shape()
    out = dict(inp)
    out["loss_target"] = _jax.random.normal(_jax.random.fold_in(key, 0), shape, _jnp.float32)
    for i, name in enumerate(TWIN_WEIGHTS):
        w = inp[name].astype(_jnp.float32)
        if MOMENT_SCALE is None:
            s = _jnp.sqrt(_jnp.mean(_jnp.square(w)) + 1e-30)
        else:
            s = MOMENT_SCALE[name]
        km, kv = _jax.random.split(_jax.random.fold_in(key, i + 1))
        out[name] = w
        out["m_" + name] = s * _jax.random.normal(km, w.shape, _jnp.float32)
        out["v_" + name] = (s * s) * _jax.random.uniform(kv, w.shape, _jnp.float32, 0.5, 1.5)
    if N_MICROBATCH > 1:
        for name, axis in PER_EXAMPLE_BATCH_AXIS.items():
            out[name] = _to_microbatches(out[name], axis)
    return {'x': out['x'], 'norm_mix_g': out['norm_mix_g'], 'w_in': out['w_in'], 'conv_w': out['conv_w'], 'conv_b': out['conv_b'], 'lru_w_a': out['lru_w_a'], 'lru_b_a': out['lru_b_a'], 'lru_w_x': out['lru_w_x'], 'lru_b_x': out['lru_b_x'], 'lru_lambda': out['lru_lambda'], 'sgu_ln_g': out['sgu_ln_g'], 'sgu_ln_b': out['sgu_ln_b'], 'sgu_w_s': out['sgu_w_s'], 'sgu_b_s': out['sgu_b_s'], 'w_branch_a': out['w_branch_a'], 'w_branch_b': out['w_branch_b'], 'w_out': out['w_out'], 'norm_ffn_g': out['norm_ffn_g'], 'w_up': out['w_up'], 'w_down': out['w_down'], 'final_norm_g': out['final_norm_g'], 'loss_target': out['loss_target'], 'm_norm_mix_g': out['m_norm_mix_g'], 'm_w_in': out['m_w_in'], 'm_conv_w': out['m_conv_w'], 'm_conv_b': out['m_conv_b'], 'm_lru_w_a': out['m_lru_w_a'], 'm_lru_b_a': out['m_lru_b_a'], 'm_lru_w_x': out['m_lru_w_x'], 'm_lru_b_x': out['m_lru_b_x'], 'm_lru_lambda': out['m_lru_lambda'], 'm_sgu_ln_g': out['m_sgu_ln_g'], 'm_sgu_ln_b': out['m_sgu_ln_b'], 'm_sgu_w_s': out['m_sgu_w_s'], 'm_sgu_b_s': out['m_sgu_b_s'], 'm_w_branch_a': out['m_w_branch_a'], 'm_w_branch_b': out['m_w_branch_b'], 'm_w_out': out['m_w_out'], 'm_norm_ffn_g': out['m_norm_ffn_g'], 'm_w_up': out['m_w_up'], 'm_w_down': out['m_w_down'], 'm_final_norm_g': out['m_final_norm_g'], 'v_norm_mix_g': out['v_norm_mix_g'], 'v_w_in': out['v_w_in'], 'v_conv_w': out['v_conv_w'], 'v_conv_b': out['v_conv_b'], 'v_lru_w_a': out['v_lru_w_a'], 'v_lru_b_a': out['v_lru_b_a'], 'v_lru_w_x': out['v_lru_w_x'], 'v_lru_b_x': out['v_lru_b_x'], 'v_lru_lambda': out['v_lru_lambda'], 'v_sgu_ln_g': out['v_sgu_ln_g'], 'v_sgu_ln_b': out['v_sgu_ln_b'], 'v_sgu_w_s': out['v_sgu_w_s'], 'v_sgu_b_s': out['v_sgu_b_s'], 'v_w_branch_a': out['v_w_branch_a'], 'v_w_branch_b': out['v_w_branch_b'], 'v_w_out': out['v_w_out'], 'v_norm_ffn_g': out['v_norm_ffn_g'], 'v_w_up': out['v_w_up'], 'v_w_down': out['v_w_down'], 'v_final_norm_g': out['v_final_norm_g']}


def _loss(weights, diff, rest, loss_target):
    with _jax.named_scope("forward"):
        args = {**rest, TWIN_DIFF_INPUT: diff, **{k: w.astype(_WEIGHT_DTYPES[k]) for k, w in weights.items()}}
        y = _forward(args)
    with _jax.named_scope("loss_head"):
        err = _jnp.square(y.astype(_jnp.float32) - loss_target)
        return 0.5 * _jnp.sum(_jnp.mean(err, axis=-1)) if err.ndim else 0.5 * err


def _adamw(w, g, m, v):
    m = ADAM_B1 * m + (1.0 - ADAM_B1) * g
    v = ADAM_B2 * v + (1.0 - ADAM_B2) * _jnp.square(g)
    m_hat = m / (1.0 - ADAM_B1 ** ADAM_STEP)
    v_hat = v / (1.0 - ADAM_B2 ** ADAM_STEP)
    delta = -ADAM_LR * (m_hat / (_jnp.sqrt(v_hat) + ADAM_EPS) + ADAM_WD * w)
    return delta, m, v


def reference(x, norm_mix_g, w_in, conv_w, conv_b, lru_w_a, lru_b_a, lru_w_x, lru_b_x, lru_lambda, sgu_ln_g, sgu_ln_b, sgu_w_s, sgu_b_s, w_branch_a, w_branch_b, w_out, norm_ffn_g, w_up, w_down, final_norm_g, loss_target, m_norm_mix_g, m_w_in, m_conv_w, m_conv_b, m_lru_w_a, m_lru_b_a, m_lru_w_x, m_lru_b_x, m_lru_lambda, m_sgu_ln_g, m_sgu_ln_b, m_sgu_w_s, m_sgu_b_s, m_w_branch_a, m_w_branch_b, m_w_out, m_norm_ffn_g, m_w_up, m_w_down, m_final_norm_g, v_norm_mix_g, v_w_in, v_conv_w, v_conv_b, v_lru_w_a, v_lru_b_a, v_lru_w_x, v_lru_b_x, v_lru_lambda, v_sgu_ln_g, v_sgu_ln_b, v_sgu_w_s, v_sgu_b_s, v_w_branch_a, v_w_branch_b, v_w_out, v_norm_ffn_g, v_w_up, v_w_down, v_final_norm_g):
    given = dict(x=x, norm_mix_g=norm_mix_g, w_in=w_in, conv_w=conv_w, conv_b=conv_b, lru_w_a=lru_w_a, lru_b_a=lru_b_a, lru_w_x=lru_w_x, lru_b_x=lru_b_x, lru_lambda=lru_lambda, sgu_ln_g=sgu_ln_g, sgu_ln_b=sgu_ln_b, sgu_w_s=sgu_w_s, sgu_b_s=sgu_b_s, w_branch_a=w_branch_a, w_branch_b=w_branch_b, w_out=w_out, norm_ffn_g=norm_ffn_g, w_up=w_up, w_down=w_down, final_norm_g=final_norm_g, loss_target=loss_target, m_norm_mix_g=m_norm_mix_g, m_w_in=m_w_in, m_conv_w=m_conv_w, m_conv_b=m_conv_b, m_lru_w_a=m_lru_w_a, m_lru_b_a=m_lru_b_a, m_lru_w_x=m_lru_w_x, m_lru_b_x=m_lru_b_x, m_lru_lambda=m_lru_lambda, m_sgu_ln_g=m_sgu_ln_g, m_sgu_ln_b=m_sgu_ln_b, m_sgu_w_s=m_sgu_w_s, m_sgu_b_s=m_sgu_b_s, m_w_branch_a=m_w_branch_a, m_w_branch_b=m_w_branch_b, m_w_out=m_w_out, m_norm_ffn_g=m_norm_ffn_g, m_w_up=m_w_up, m_w_down=m_w_down, m_final_norm_g=m_final_norm_g, v_norm_mix_g=v_norm_mix_g, v_w_in=v_w_in, v_conv_w=v_conv_w, v_conv_b=v_conv_b, v_lru_w_a=v_lru_w_a, v_lru_b_a=v_lru_b_a, v_lru_w_x=v_lru_w_x, v_lru_b_x=v_lru_b_x, v_lru_lambda=v_lru_lambda, v_sgu_ln_g=v_sgu_ln_g, v_sgu_ln_b=v_sgu_ln_b, v_sgu_w_s=v_sgu_w_s, v_sgu_b_s=v_sgu_b_s, v_w_branch_a=v_w_branch_a, v_w_branch_b=v_w_branch_b, v_w_out=v_w_out, v_norm_ffn_g=v_norm_ffn_g, v_w_up=v_w_up, v_w_down=v_w_down, v_final_norm_g=v_final_norm_g)
    weights = {n: given[n] for n in TWIN_WEIGHTS}
    shared = {n: given[n] for n in SHARED_INPUTS}
    per_example = {n: given[n] for n in ['x']}
    grad_fn = _jax.value_and_grad(_loss, argnums=(0, 1))

    def one_microbatch(ex, loss_target):
        ex = dict(ex)
        diff = ex.pop(TWIN_DIFF_INPUT)
        return grad_fn(weights, diff, {**shared, **ex}, loss_target)

    if N_MICROBATCH == 1:
        loss, (grad_w, grad_x) = one_microbatch(per_example, given["loss_target"])
    else:
        def body(carry, xs):
            loss_sum, grad_sum = carry
            l_k, (gw_k, gx_k) = one_microbatch(xs[0], xs[1])
            with _jax.named_scope("update"):
                return (loss_sum + l_k, _jax.tree.map(_jnp.add, grad_sum, gw_k)), gx_k

        init = (_jnp.zeros((), _jnp.float32), _jax.tree.map(_jnp.zeros_like, weights))
        (loss, grad_w), grad_x = _jax.lax.scan(body, init, (per_example, given["loss_target"]))
    with _jax.named_scope("update"):
        delta_w, new_m, new_v = {}, {}, {}
        for n in TWIN_WEIGHTS:
            delta_w[n], new_m[n], new_v[n] = _adamw(weights[n], grad_w[n], given["m_" + n], given["v_" + n])
    return (loss, grad_x, *[grad_w[n] for n in TWIN_WEIGHTS], *[delta_w[n] for n in TWIN_WEIGHTS],
            *[new_m[n] for n in TWIN_WEIGHTS], *[new_v[n] for n in TWIN_WEIGHTS])
```

```python
import functools
import math

import jax
import jax.numpy as jnp
from jax import lax
from jax.experimental import pallas as pl
from jax.experimental.pallas import tpu as pltpu

F32 = jnp.float32
BF = jnp.bfloat16

DEPTH = 2
D_MODEL = 1024
D_RNN = 1280
D_SGU = 1024
D_FF = 4096
D_IN = 2 * D_RNN + 2 * D_SGU + 2 * D_MODEL
N_QUARTERS = 4
Q_IN = D_IN // N_QUARTERS
Q_FF = D_FF // N_QUARTERS
RNN_HEADS = 20
RNN_HEAD_DIM = 64
LRU_GROUP = 256
N_LRU_GROUPS = D_RNN // LRU_GROUP
HEADS_PER_GROUP = LRU_GROUP // RNN_HEAD_DIM
CONV_WIDTH = 4
LRU_C = 8.0
SGU_GROUPS = 8
SGU_BLOCK = 128
CHUNK = 64
EPS = 1e-6

ADAM_LR = 0.001
ADAM_B1 = 0.9
ADAM_B2 = 0.999
ADAM_EPS = 1e-08
ADAM_WD = 0.01
ADAM_STEP = 10

SUBLANES = 8
TOKEN_TILE = 512
VMEM_LIMIT_BYTES = 56 * 1024 * 1024

MESH = pl.DeviceIdType.MESH


def _params(semantics=None, vmem=True, **kw):
    return pltpu.CompilerParams(
        dimension_semantics=semantics,
        vmem_limit_bytes=VMEM_LIMIT_BYTES if vmem else None,
        **kw,
    )


def _dot(a, b):
    return jnp.dot(a, b, preferred_element_type=F32)


def _dot_nt(a, b):
    return lax.dot_general(a, b, (((1,), (1,)), ((), ())), preferred_element_type=F32)


def _dot_tn(a, b):
    return lax.dot_general(a, b, (((0,), (0,)), ((), ())), preferred_element_type=F32)


_GELU_C = math.sqrt(2.0 / math.pi)
_GELU_A = 0.044715


def _gelu(x):
    return 0.5 * x * (1.0 + jnp.tanh(_GELU_C * (x + _GELU_A * x * x * x)))


def _gelu_and_grad(x):
    x2 = x * x
    t = jnp.tanh(_GELU_C * (x + _GELU_A * x2 * x))
    du = _GELU_C * (1.0 + 3.0 * _GELU_A * x2)
    return 0.5 * x * (1.0 + t), 0.5 * (1.0 + t) + 0.5 * x * (1.0 - t * t) * du


def _expm1(x):
    series = x * (1.0 + 0.5 * x * (1.0 + (1.0 / 3.0) * x * (1.0 + 0.25 * x)))
    return jnp.where(jnp.abs(x) < 0.05, series, jnp.exp(x) - 1.0)


def _rms_stats(x):
    return lax.rsqrt(jnp.mean(x * x, axis=-1, keepdims=True) + EPS)


def _rms_bwd(dy, x, g):
    rs = _rms_stats(x)
    n = x * rs
    dn = dy * g
    dx = rs * (dn - n * jnp.mean(dn * n, axis=-1, keepdims=True))
    return dx, dy * n


def _row_sum(x):
    return jnp.sum(x, axis=0, keepdims=True)


def _tile_spec(ts, width, col=0):
    return pl.BlockSpec((ts, width), lambda i, col=col: (i, col))


def _full_spec(shape):
    zeros = (0,) * len(shape)
    return pl.BlockSpec(shape, lambda *_: zeros)


def _layer_spec(w, layer):
    zeros = (0,) * (w.ndim - 1)
    return pl.BlockSpec((None,) + tuple(w.shape[1:]), lambda *_: (layer,) + zeros)


def _norm_call(x, g, ts):
    s = x.shape[0]

    def body(x_ref, g_ref, h_ref):
        xv = x_ref[...]
        h_ref[...] = (xv * _rms_stats(xv) * g_ref[...]).astype(BF)

    return pl.pallas_call(
        body, name="norm_fwd", grid=(s // ts,),
        in_specs=[_tile_spec(ts, D_MODEL), _full_spec((1, D_MODEL))],
        out_specs=_tile_spec(ts, D_MODEL),
        out_shape=jax.ShapeDtypeStruct((s, D_MODEL), BF),
        compiler_params=_params(("parallel",)),
    )(x, g)


def _inproj_call(h, w_in, layer, ts):
    s = h.shape[0]

    def body(h_ref, w_ref, o_ref):
        o_ref[...] = _dot(h_ref[...], w_ref[...]).astype(BF)

    return pl.pallas_call(
        body, name="inproj_fwd", grid=(N_QUARTERS, s // ts),
        in_specs=[
            pl.BlockSpec((ts, D_MODEL), lambda q, i: (i, 0)),
            pl.BlockSpec((None, None, D_MODEL, Q_IN), lambda q, i: (layer, q, 0, 0)),
        ],
        out_specs=pl.BlockSpec((ts, Q_IN), lambda q, i: (i, q)),
        out_shape=jax.ShapeDtypeStruct((s, D_IN), BF),
        compiler_params=_params(("parallel", "parallel")),
    )(h, w_in)


def _shift_down(x, tail, s):
    xr = pltpu.roll(x, s, 0)
    tr = pltpu.roll(tail, s, 0)
    row = lax.broadcasted_iota(jnp.int32, tail.shape, 0)
    top = jnp.where(row < s, tr, xr[0:SUBLANES])
    return jnp.concatenate([top, xr[SUBLANES:]], axis=0)


def _shift_up(x, head, s):
    t = x.shape[0]
    xr = pltpu.roll(x, t - s, 0)
    hr = pltpu.roll(head, SUBLANES - s, 0)
    row = lax.broadcasted_iota(jnp.int32, head.shape, 0)
    bottom = jnp.where(row >= SUBLANES - s, hr, xr[t - SUBLANES:])
    return jnp.concatenate([xr[: t - SUBLANES], bottom], axis=0)


def _conv_fwd(x, tail, cw_ref, cb_ref):
    out = cb_ref[...] + cw_ref[CONV_WIDTH - 1:CONV_WIDTH, :] * x
    for s in range(1, CONV_WIDTH):
        k = CONV_WIDTH - 1 - s
        out = out + cw_ref[k:k + 1, :] * _shift_down(x, tail, s)
    return out


def _group_dot(x_bf, w_ref, dot):
    cols = [dot(x_bf[:, g * LRU_GROUP:(g + 1) * LRU_GROUP], w_ref[g]) for g in range(N_LRU_GROUPS)]
    return jnp.concatenate(cols, axis=1)


def _lru_gates(xr, wa_ref, wx_ref, ba_ref, bx_ref, sp_ref):
    xb = xr.astype(BF)
    r = jax.nn.sigmoid(_group_dot(xb, wa_ref, _dot) + ba_ref[...])
    i = jax.nn.sigmoid(_group_dot(xb, wx_ref, _dot) + bx_ref[...])
    log_a = (-LRU_C * r) * sp_ref[...]
    a = jnp.exp(log_a)
    nrm = jnp.sqrt(-_expm1(2.0 * log_a))
    return r, i, a, nrm


def _linear_scan(a, b, carry, al_ref, bl_ref, h_ref, reverse):
    t, c = a.shape
    rowm = lax.broadcasted_iota(jnp.int32, (t, c), 0) & (SUBLANES - 1)
    for d in (1, 2, 4):
        if reverse:
            keep, sh = rowm < SUBLANES - d, t - d
        else:
            keep, sh = rowm >= d, d
        a_sh = jnp.where(keep, pltpu.roll(a, sh, 0), 1.0)
        b_sh = jnp.where(keep, pltpu.roll(b, sh, 0), 0.0)
        b = a * b_sh + b
        a = a * a_sh
    al_ref[...] = a
    bl_ref[...] = b
    groups = t // SUBLANES

    def step(j, state):
        jj = groups - 1 - j if reverse else j
        off = pl.multiple_of(jj * SUBLANES, SUBLANES)
        rows = bl_ref[pl.ds(off, SUBLANES), :] + al_ref[pl.ds(off, SUBLANES), :] * state
        h_ref[pl.ds(off, SUBLANES), :] = rows
        last = rows[0:1, :] if reverse else rows[SUBLANES - 1:SUBLANES, :]
        return jnp.broadcast_to(last, (SUBLANES, c))

    out = lax.fori_loop(0, groups, step, jnp.broadcast_to(carry, (SUBLANES, c)))
    return out[0:1, :]


def _rnn_fwd_call(proj, wa, wx, ba, bx, sp, cw, cb, ts):
    s = proj.shape[0]

    def body(xg_ref, wa_ref, wx_ref, ba_ref, bx_ref, sp_ref, cw_ref, cb_ref, hr_ref, ya_ref,
             tail_sc, carry_sc, al_sc, bl_sc, h_sc):
        @pl.when(pl.program_id(0) == 0)
        def _():
            tail_sc[...] = jnp.zeros_like(tail_sc)
            carry_sc[...] = jnp.zeros_like(carry_sc)

        x = xg_ref[:, :D_RNN].astype(F32)
        g = xg_ref[:, D_RNN:].astype(F32)
        xr = _conv_fwd(x, tail_sc[...], cw_ref, cb_ref)
        tail_sc[...] = x[ts - SUBLANES:, :]
        _, i, a, nrm = _lru_gates(xr, wa_ref, wx_ref, ba_ref, bx_ref, sp_ref)
        carry_sc[...] = _linear_scan(a, nrm * (i * xr), carry_sc[...], al_sc, bl_sc, h_sc, False)
        h = h_sc[...]
        hr_ref[...] = h.astype(BF)
        ya_ref[...] = (h * _gelu(g)).astype(BF)

    gw = (N_LRU_GROUPS, LRU_GROUP, LRU_GROUP)
    return pl.pallas_call(
        body, name="rnn_fwd", grid=(s // ts,),
        in_specs=[_tile_spec(ts, 2 * D_RNN), _full_spec(gw), _full_spec(gw),
                  _full_spec((1, D_RNN)), _full_spec((1, D_RNN)), _full_spec((1, D_RNN)),
                  _full_spec((CONV_WIDTH, D_RNN)), _full_spec((1, D_RNN))],
        out_specs=[_tile_spec(ts, D_RNN), _tile_spec(ts, D_RNN)],
        out_shape=[jax.ShapeDtypeStruct((s, D_RNN), BF), jax.ShapeDtypeStruct((s, D_RNN), BF)],
        scratch_shapes=[pltpu.VMEM((SUBLANES, D_RNN), F32), pltpu.VMEM((1, D_RNN), F32),
                        pltpu.VMEM((ts, D_RNN), F32), pltpu.VMEM((ts, D_RNN), F32),
                        pltpu.VMEM((ts, D_RNN), F32)],
        compiler_params=_params(("arbitrary",)),
    )(proj, wa, wx, ba, bx, sp, cw, cb)


def _layernorm_fwd(x):
    mu = jnp.mean(x, axis=-1, keepdims=True)
    xc = x - mu
    rstd = lax.rsqrt(jnp.mean(xc * xc, axis=-1, keepdims=True) + EPS)
    return xc * rstd, rstd


def _sgu_mix(vn_bf, wm_ref, bsb_ref, ts):
    rows = []
    for blk in range(ts // SGU_BLOCK):
        r0 = blk * SGU_BLOCK
        cols = [
            _dot(wm_ref[g], vn_bf[r0:r0 + SGU_BLOCK, g * SGU_BLOCK:(g + 1) * SGU_BLOCK]) + bsb_ref[g]
            for g in range(SGU_GROUPS)
        ]
        rows.append(jnp.concatenate(cols, axis=1))
    return jnp.concatenate(rows, axis=0)


def _sgu_fwd_call(proj, wm, bsb, lg, lb, ts):
    s = proj.shape[0]

    def body(uv_ref, wm_ref, bsb_ref, lg_ref, lb_ref, yb_ref):
        gu = _gelu(uv_ref[:, :D_SGU].astype(F32))
        gv = _gelu(uv_ref[:, D_SGU:2 * D_SGU].astype(F32))
        nh, _ = _layernorm_fwd(gv)
        vn = (nh * lg_ref[...] + lb_ref[...]).astype(BF)
        yb_ref[...] = (gu * _sgu_mix(vn, wm_ref, bsb_ref, ts)).astype(BF)

    sw = (SGU_GROUPS, SGU_BLOCK, SGU_BLOCK)
    return pl.pallas_call(
        body, name="sgu_fwd", grid=(s // ts,),
        in_specs=[_tile_spec(ts, 2 * D_RNN, 1), _full_spec(sw), _full_spec(sw),
                  _full_spec((1, D_SGU)), _full_spec((1, D_SGU))],
        out_specs=_tile_spec(ts, D_SGU),
        out_shape=jax.ShapeDtypeStruct((s, D_SGU), BF),
        compiler_params=_params(("parallel",)),
    )(proj, wm, bsb, lg, lb)


_GATE_COL0 = (2 * D_RNN + 2 * D_SGU) // 512


def _gate_specs(ts):
    return [_tile_spec(ts, 512, _GATE_COL0 + j) for j in range(4)]


def _merge_call(x, proj, ya_pre, yb_pre, w_ba, w_bb, w_out, g2, layer, ts):
    s = x.shape[0]

    def body(x_ref, ga0, ga1, gb0, gb1, ya_ref, yb_ref, wa_ref, wb_ref, wo_ref, g2_ref,
             x1_ref, yao_ref, ybo_ref, mg_ref, h2_ref):
        ya = _dot(ya_ref[...], wa_ref[...])
        yb = _dot(yb_ref[...], wb_ref[...])
        sa = jax.nn.sigmoid(jnp.concatenate([ga0[...], ga1[...]], axis=1).astype(F32))
        sb = jax.nn.sigmoid(jnp.concatenate([gb0[...], gb1[...]], axis=1).astype(F32))
        merged = (sa * ya + sb * yb).astype(BF)
        x1 = x_ref[...] + _dot(merged, wo_ref[...])
        x1_ref[...] = x1
        yao_ref[...] = ya.astype(BF)
        ybo_ref[...] = yb.astype(BF)
        mg_ref[...] = merged
        h2_ref[...] = (x1 * _rms_stats(x1) * g2_ref[...]).astype(BF)

    act = jax.ShapeDtypeStruct((s, D_MODEL), BF)
    return pl.pallas_call(
        body, name="merge_fwd", grid=(s // ts,),
        in_specs=[_tile_spec(ts, D_MODEL)] + _gate_specs(ts) + [
            _tile_spec(ts, D_RNN), _tile_spec(ts, D_SGU),
            _layer_spec(w_ba, layer), _layer_spec(w_bb, layer), _layer_spec(w_out, layer),
            _full_spec((1, D_MODEL))],
        out_specs=[_tile_spec(ts, D_MODEL)] * 5,
        out_shape=[jax.ShapeDtypeStruct((s, D_MODEL), F32), act, act, act, act],
        compiler_params=_params(("parallel",)),
    )(x, proj, proj, proj, proj, ya_pre, yb_pre, w_ba, w_bb, w_out, g2)


def _ffn_call(x1, h2, w_up, w_down, layer, ts):
    s = x1.shape[0]

    def body(x1_ref, h2_ref, wu_ref, wd_ref, x2_ref, p_ref):
        h2v = h2_ref[...]
        acc = x1_ref[...]
        for q in range(N_QUARTERS):
            p = _dot(h2v, wu_ref[q])
            p_ref[:, q * Q_FF:(q + 1) * Q_FF] = p.astype(BF)
            f = jnp.square(jnp.maximum(p, 0.0)).astype(BF)
            acc = acc + _dot(f, wd_ref[q * Q_FF:(q + 1) * Q_FF, :])
        x2_ref[...] = acc

    return pl.pallas_call(
        body, name="ffn_fwd", grid=(s // ts,),
        in_specs=[_tile_spec(ts, D_MODEL), _tile_spec(ts, D_MODEL),
                  pl.BlockSpec((None, N_QUARTERS, D_MODEL, Q_FF), lambda i: (layer, 0, 0, 0)),
                  pl.BlockSpec((None, D_FF, D_MODEL), lambda i: (layer, 0, 0))],
        out_specs=[_tile_spec(ts, D_MODEL), _tile_spec(ts, D_FF)],
        out_shape=[jax.ShapeDtypeStruct((s, D_MODEL), F32), jax.ShapeDtypeStruct((s, D_FF), BF)],
        compiler_params=_params(("parallel",)),
    )(x1, h2, w_up, w_down)


def _loss_call(x, target, gf, ts):
    s = x.shape[0]

    def body(x_ref, t_ref, g_ref, dx_ref, loss_ref, dg_ref):
        @pl.when(pl.program_id(0) == 0)
        def _():
            loss_ref[...] = jnp.zeros_like(loss_ref)
            dg_ref[...] = jnp.zeros_like(dg_ref)

        xv = x_ref[...]
        gv = g_ref[...]
        err = xv * _rms_stats(xv) * gv - t_ref[...]
        part = 0.5 * jnp.sum(jnp.mean(err * err, axis=-1, keepdims=True), axis=0, keepdims=True)
        loss_ref[...] += jnp.broadcast_to(part, loss_ref.shape)
        dx, dg = _rms_bwd(err * (1.0 / D_MODEL), xv, gv)
        dx_ref[...] = dx
        dg_ref[...] += _row_sum(dg)

    return pl.pallas_call(
        body, name="loss_head", grid=(s // ts,),
        in_specs=[_tile_spec(ts, D_MODEL), _tile_spec(ts, D_MODEL), _full_spec((1, D_MODEL))],
        out_specs=[_tile_spec(ts, D_MODEL), _full_spec((1, 128)), _full_spec((1, D_MODEL))],
        out_shape=[jax.ShapeDtypeStruct((s, D_MODEL), F32), jax.ShapeDtypeStruct((1, 128), F32),
                   jax.ShapeDtypeStruct((1, D_MODEL), F32)],
        compiler_params=_params(("arbitrary",)),
    )(x, target, gf)


def _ffn_bwd_call(dx2, p, x1, g2, w_up, w_down, layer, ts):
    s = dx2.shape[0]

    def body(dx2_ref, p_ref, x1_ref, g2_ref, wu_ref, wd_ref, dx1_ref, dp_ref, dg_ref):
        @pl.when(pl.program_id(0) == 0)
        def _():
            dg_ref[...] = jnp.zeros_like(dg_ref)

        dx2v = dx2_ref[...]
        dyb = dx2v.astype(BF)
        dh2 = jnp.zeros((ts, D_MODEL), F32)
        for q in range(N_QUARTERS):
            cols = slice(q * Q_FF, (q + 1) * Q_FF)
            df = _dot_nt(dyb, wd_ref[cols, :])
            dp = (df * (2.0 * jnp.maximum(p_ref[:, cols].astype(F32), 0.0))).astype(BF)
            dp_ref[:, cols] = dp
            dh2 = dh2 + _dot_nt(dp, wu_ref[q])
        dx, dg = _rms_bwd(dh2, x1_ref[...], g2_ref[...])
        dx1_ref[...] = dx2v + dx
        dg_ref[...] += _row_sum(dg)

    return pl.pallas_call(
        body, name="ffn_bwd", grid=(s // ts,),
        in_specs=[_tile_spec(ts, D_MODEL), _tile_spec(ts, D_FF), _tile_spec(ts, D_MODEL),
                  _full_spec((1, D_MODEL)),
                  pl.BlockSpec((None, N_QUARTERS, D_MODEL, Q_FF), lambda i: (layer, 0, 0, 0)),
                  pl.BlockSpec((None, D_FF, D_MODEL), lambda i: (layer, 0, 0))],
        out_specs=[_tile_spec(ts, D_MODEL), _tile_spec(ts, D_FF), _full_spec((1, D_MODEL))],
        out_shape=[jax.ShapeDtypeStruct((s, D_MODEL), F32), jax.ShapeDtypeStruct((s, D_FF), BF),
                   jax.ShapeDtypeStruct((1, D_MODEL), F32)],
        compiler_params=_params(("arbitrary",)),
    )(dx2, p, x1, g2, w_up, w_down)


def _merge_bwd_call(dx1, proj, ya, yb, w_ba, w_bb, w_out, layer, ts):
    s = dx1.shape[0]

    def body(dx1_ref, ga0, ga1, gb0, gb1, ya_ref, yb_ref, wa_ref, wb_ref, wo_ref,
             dya_ref, dyb_ref, dgate_ref, dyap_ref, dybp_ref):
        dm = _dot_nt(dx1_ref[...].astype(BF), wo_ref[...])
        sa = jax.nn.sigmoid(jnp.concatenate([ga0[...], ga1[...]], axis=1).astype(F32))
        sb = jax.nn.sigmoid(jnp.concatenate([gb0[...], gb1[...]], axis=1).astype(F32))
        dya = (dm * sa).astype(BF)
        dyb = (dm * sb).astype(BF)
        dya_ref[...] = dya
        dyb_ref[...] = dyb
        dgate_ref[:, :D_MODEL] = (dm * ya_ref[...].astype(F32) * sa * (1.0 - sa)).astype(BF)
        dgate_ref[:, D_MODEL:] = (dm * yb_ref[...].astype(F32) * sb * (1.0 - sb)).astype(BF)
        dyap_ref[...] = _dot_nt(dya, wa_ref[...]).astype(BF)
        dybp_ref[...] = _dot_nt(dyb, wb_ref[...]).astype(BF)

    act = jax.ShapeDtypeStruct((s, D_MODEL), BF)
    return pl.pallas_call(
        body, name="merge_bwd", grid=(s // ts,),
        in_specs=[_tile_spec(ts, D_MODEL)] + _gate_specs(ts) + [
            _tile_spec(ts, D_MODEL), _tile_spec(ts, D_MODEL),
            _layer_spec(w_ba, layer), _layer_spec(w_bb, layer), _layer_spec(w_out, layer)],
        out_specs=[_tile_spec(ts, D_MODEL), _tile_spec(ts, D_MODEL), _tile_spec(ts, 2 * D_MODEL),
                   _tile_spec(ts, D_RNN), _tile_spec(ts, D_SGU)],
        out_shape=[act, act, jax.ShapeDtypeStruct((s, 2 * D_MODEL), BF),
                   jax.ShapeDtypeStruct((s, D_RNN), BF), jax.ShapeDtypeStruct((s, D_SGU), BF)],
        compiler_params=_params(("parallel",)),
    )(dx1, proj, proj, proj, proj, ya, yb, w_ba, w_bb, w_out)


def _sgu_bwd_call(dyb_pre, proj, wm, bsb, mask, lg, lb, ts):
    s = proj.shape[0]

    def body(dy_ref, uv_ref, wm_ref, bsb_ref, mask_ref, lg_ref, lb_ref,
             duv_ref, dws_ref, dbs_ref, dlg_ref, dlb_ref, dm_sc):
        step = pl.program_id(0)

        @pl.when(step == 0)
        def _():
            dws_ref[...] = jnp.zeros_like(dws_ref)
            dlg_ref[...] = jnp.zeros_like(dlg_ref)
            dlb_ref[...] = jnp.zeros_like(dlb_ref)
            dm_sc[...] = jnp.zeros_like(dm_sc)

        gu, dgu_du = _gelu_and_grad(uv_ref[:, :D_SGU].astype(F32))
        gv, dgv_dv = _gelu_and_grad(uv_ref[:, D_SGU:2 * D_SGU].astype(F32))
        nh, rstd = _layernorm_fwd(gv)
        lgv = lg_ref[...]
        vn = (nh * lgv + lb_ref[...]).astype(BF)
        dy = dy_ref[...].astype(F32)
        du = dy * _sgu_mix(vn, wm_ref, bsb_ref, ts) * dgu_du
        dmix = dy * gu
        dmix_bf = dmix.astype(BF)
        dm_acc = dm_sc[...]
        rows = []
        for blk in range(ts // SGU_BLOCK):
            r0 = blk * SGU_BLOCK
            dm_acc = dm_acc + dmix[r0:r0 + SGU_BLOCK, :]
            cols = []
            for g in range(SGU_GROUPS):
                c0 = g * SGU_BLOCK
                dmg = dmix_bf[r0:r0 + SGU_BLOCK, c0:c0 + SGU_BLOCK]
                cols.append(_dot_tn(wm_ref[g], dmg))
                dws_ref[g] += mask_ref[...] * _dot_nt(dmg, vn[r0:r0 + SGU_BLOCK, c0:c0 + SGU_BLOCK])
            rows.append(jnp.concatenate(cols, axis=1))
        dm_sc[...] = dm_acc
        dvn = jnp.concatenate(rows, axis=0)
        dlg_ref[...] += _row_sum(dvn * nh)
        dlb_ref[...] += _row_sum(dvn)
        dnh = dvn * lgv
        dgv = rstd * (dnh - jnp.mean(dnh, axis=-1, keepdims=True)
                      - nh * jnp.mean(dnh * nh, axis=-1, keepdims=True))
        duv_ref[:, :D_SGU] = du.astype(BF)
        duv_ref[:, D_SGU:] = (dgv * dgv_dv).astype(BF)

        @pl.when(step == pl.num_programs(0) - 1)
        def _():
            for g in range(SGU_GROUPS):
                dbs_ref[:, g:g + 1] = jnp.sum(
                    dm_acc[:, g * SGU_BLOCK:(g + 1) * SGU_BLOCK], axis=1, keepdims=True)

    sw = (SGU_GROUPS, SGU_BLOCK, SGU_BLOCK)
    return pl.pallas_call(
        body, name="sgu_bwd", grid=(s // ts,),
        in_specs=[_tile_spec(ts, D_SGU), _tile_spec(ts, 2 * D_RNN, 1), _full_spec(sw), _full_spec(sw),
                  _full_spec((SGU_BLOCK, SGU_BLOCK)), _full_spec((1, D_SGU)), _full_spec((1, D_SGU))],
        out_specs=[_tile_spec(ts, 2 * D_SGU), _full_spec(sw), _full_spec((SGU_BLOCK, SGU_GROUPS)),
                   _full_spec((1, D_SGU)), _full_spec((1, D_SGU))],
        out_shape=[jax.ShapeDtypeStruct((s, 2 * D_SGU), BF), jax.ShapeDtypeStruct(sw, F32),
                   jax.ShapeDtypeStruct((SGU_BLOCK, SGU_GROUPS), F32),
                   jax.ShapeDtypeStruct((1, D_SGU), F32), jax.ShapeDtypeStruct((1, D_SGU), F32)],
        scratch_shapes=[pltpu.VMEM((SGU_BLOCK, D_SGU), F32)],
        compiler_params=_params(("arbitrary",)),
    )(dyb_pre, proj, wm, bsb, mask, lg, lb)


_ROW_DBA, _ROW_DBX, _ROW_DSP, _ROW_DCB, _ROW_DCW = 0, 1, 2, 3, 4
_PREV_ROWS = 16


def _rnn_bwd_call(dya_pre, proj, hr, wa, wx, ba, bx, sp, cw, cb, ts):
    s = proj.shape[0]
    nt = s // ts
    per = ts // _PREV_ROWS

    def tile(i):
        return nt - 1 - i

    def prev(i):
        return jnp.maximum(tile(i) * per - 1, 0)

    def body(dy_ref, xg_ref, xgp_ref, hr_ref, hrp_ref, wa_ref, wx_ref, ba_ref, bx_ref, sp_ref,
             cw_ref, cb_ref, dxg_ref, dwa_ref, dwx_ref, vec_ref,
             lam_carry, a_first, dxr_head, al_sc, bl_sc, lam_sc):
        step = pl.program_id(0)

        @pl.when(step == 0)
        def _():
            dwa_ref[...] = jnp.zeros_like(dwa_ref)
            dwx_ref[...] = jnp.zeros_like(dwx_ref)
            vec_ref[...] = jnp.zeros_like(vec_ref)
            lam_carry[...] = jnp.zeros_like(lam_carry)
            a_first[...] = jnp.zeros_like(a_first)
            dxr_head[...] = jnp.zeros_like(dxr_head)

        has_prev = (step < nt - 1).astype(F32)
        x = xg_ref[:, :D_RNN].astype(F32)
        g = xg_ref[:, D_RNN:].astype(F32)
        x_tail = xgp_ref[_PREV_ROWS - SUBLANES:, :D_RNN].astype(F32) * has_prev
        h_tail = hrp_ref[_PREV_ROWS - SUBLANES:, :].astype(F32) * has_prev
        xr = _conv_fwd(x, x_tail, cw_ref, cb_ref)
        r, i, a, nrm = _lru_gates(xr, wa_ref, wx_ref, ba_ref, bx_ref, sp_ref)
        h = hr_ref[...].astype(F32)
        dy = dy_ref[...].astype(F32)
        gg, dgg = _gelu_and_grad(g)

        coef = _shift_up(a, jnp.broadcast_to(a_first[...], (SUBLANES, D_RNN)), 1)
        lam_carry[...] = _linear_scan(coef, dy * gg, lam_carry[...], al_sc, bl_sc, lam_sc, True)
        a_first[...] = a[0:1, :]
        lam = lam_sc[...]

        da = lam * _shift_down(h, h_tail, 1)
        dnrm = lam * (i * xr)
        di = lam * nrm * xr
        dlog_a = da * a - dnrm * (a * a) / nrm
        spv = sp_ref[...]
        dza = (dlog_a * (-LRU_C * spv)) * (r * (1.0 - r))
        dzx = di * (i * (1.0 - i))
        vec_ref[_ROW_DSP:_ROW_DSP + 1, :] += _row_sum(dlog_a * (-LRU_C * r))
        vec_ref[_ROW_DBA:_ROW_DBA + 1, :] += _row_sum(dza)
        vec_ref[_ROW_DBX:_ROW_DBX + 1, :] += _row_sum(dzx)
        xb = xr.astype(BF)
        dza_bf = dza.astype(BF)
        dzx_bf = dzx.astype(BF)
        for grp in range(N_LRU_GROUPS):
            cols = slice(grp * LRU_GROUP, (grp + 1) * LRU_GROUP)
            dwa_ref[grp] += _dot_tn(xb[:, cols], dza_bf[:, cols])
            dwx_ref[grp] += _dot_tn(xb[:, cols], dzx_bf[:, cols])
        dxr = (lam * nrm * i + _group_dot(dza_bf, wa_ref, _dot_nt) + _group_dot(dzx_bf, wx_ref, _dot_nt))

        vec_ref[_ROW_DCB:_ROW_DCB + 1, :] += _row_sum(dxr)
        head = dxr_head[...]
        dx = cw_ref[CONV_WIDTH - 1:CONV_WIDTH, :] * dxr
        vec_ref[_ROW_DCW + 3:_ROW_DCW + 4, :] += _row_sum(dxr * x)
        for sft in range(1, CONV_WIDTH):
            k = CONV_WIDTH - 1 - sft
            dx = dx + cw_ref[k:k + 1, :] * _shift_up(dxr, head, sft)
            vec_ref[_ROW_DCW + k:_ROW_DCW + k + 1, :] += _row_sum(dxr * _shift_down(x, x_tail, sft))
        dxr_head[...] = dxr[0:SUBLANES, :]
        dxg_ref[:, :D_RNN] = dx.astype(BF)
        dxg_ref[:, D_RNN:] = (dy * h * dgg).astype(BF)

    gw = (N_LRU_GROUPS, LRU_GROUP, LRU_GROUP)
    rev = lambda width: pl.BlockSpec((ts, width), lambda i: (tile(i), 0))
    return pl.pallas_call(
        body, name="rnn_bwd", grid=(nt,),
        in_specs=[rev(D_RNN), rev(2 * D_RNN),
                  pl.BlockSpec((_PREV_ROWS, 2 * D_RNN), lambda i: (prev(i), 0)),
                  rev(D_RNN),
                  pl.BlockSpec((_PREV_ROWS, D_RNN), lambda i: (prev(i), 0)),
                  _full_spec(gw), _full_spec(gw),
                  _full_spec((1, D_RNN)), _full_spec((1, D_RNN)), _full_spec((1, D_RNN)),
                  _full_spec((CONV_WIDTH, D_RNN)), _full_spec((1, D_RNN))],
        out_specs=[rev(2 * D_RNN), _full_spec(gw), _full_spec(gw), _full_spec((SUBLANES, D_RNN))],
        out_shape=[jax.ShapeDtypeStruct((s, 2 * D_RNN), BF), jax.ShapeDtypeStruct(gw, F32),
                   jax.ShapeDtypeStruct(gw, F32), jax.ShapeDtypeStruct((SUBLANES, D_RNN), F32)],
        scratch_shapes=[pltpu.VMEM((1, D_RNN), F32), pltpu.VMEM((1, D_RNN), F32),
                        pltpu.VMEM((SUBLANES, D_RNN), F32),
                        pltpu.VMEM((ts, D_RNN), F32), pltpu.VMEM((ts, D_RNN), F32),
                        pltpu.VMEM((ts, D_RNN), F32)],
        compiler_params=_params(("arbitrary",)),
    )(dya_pre, proj, proj, hr, hr, wa, wx, ba, bx, sp, cw, cb)


def _inproj_bwd_call(dxg, duv, dgate, dx1, x, g1, w_in, layer, ts):
    s = x.shape[0]

    def body(dxg_ref, duv_ref, dgt_ref, dx1_ref, x_ref, g_ref, w_ref, dx_ref, dproj_ref, dg_ref):
        @pl.when(pl.program_id(0) == 0)
        def _():
            dg_ref[...] = jnp.zeros_like(dg_ref)

        dproj = jnp.concatenate([dxg_ref[...], duv_ref[...], dgt_ref[...]], axis=1)
        dproj_ref[...] = dproj
        dh = jnp.zeros((ts, D_MODEL), F32)
        for q in range(N_QUARTERS):
            dh = dh + _dot_nt(dproj[:, q * Q_IN:(q + 1) * Q_IN], w_ref[q])
        dx, dg = _rms_bwd(dh, x_ref[...], g_ref[...])
        dx_ref[...] = dx1_ref[...] + dx
        dg_ref[...] += _row_sum(dg)

    return pl.pallas_call(
        body, name="inproj_bwd", grid=(s // ts,),
        in_specs=[_tile_spec(ts, 2 * D_RNN), _tile_spec(ts, 2 * D_SGU), _tile_spec(ts, 2 * D_MODEL),
                  _tile_spec(ts, D_MODEL), _tile_spec(ts, D_MODEL), _full_spec((1, D_MODEL)),
                  pl.BlockSpec((None, N_QUARTERS, D_MODEL, Q_IN), lambda i: (layer, 0, 0, 0))],
        out_specs=[_tile_spec(ts, D_MODEL), _tile_spec(ts, D_IN), _full_spec((1, D_MODEL))],
        out_shape=[jax.ShapeDtypeStruct((s, D_MODEL), F32), jax.ShapeDtypeStruct((s, D_IN), BF),
                   jax.ShapeDtypeStruct((1, D_MODEL), F32)],
        compiler_params=_params(("arbitrary",)),
    )(dxg, duv, dgate, dx1, x, g1, w_in)


def _relu_sq(p):
    return jnp.square(jnp.maximum(p.astype(F32), 0.0))


def _wgrad_call(a, b, tm, tn, tk, col_blocked, name, layer, acc=None, a_fn=None):
    s, m = a.shape
    n = b.shape[1]

    def body(a_ref, b_ref, *rest):
        o_ref = rest[-1]
        av = a_ref[...]
        if a_fn is not None:
            av = a_fn(av)
        prod = _dot_tn(av.astype(BF), b_ref[...].astype(BF))

        @pl.when(pl.program_id(2) == 0)
        def _():
            o_ref[...] = prod

        @pl.when(pl.program_id(2) > 0)
        def _():
            o_ref[...] += prod

    if col_blocked:
        out_spec = pl.BlockSpec((None, None, tm, tn), lambda i, j, k: (layer, j, 0, 0))
        out_shape = jax.ShapeDtypeStruct((DEPTH, n // tn, m, tn), F32)
    else:
        out_spec = pl.BlockSpec((None, tm, tn), lambda i, j, k: (layer, i, j))
        out_shape = jax.ShapeDtypeStruct((DEPTH, m, n), F32)
    in_specs = [pl.BlockSpec((tk, tm), lambda i, j, k: (k, i)),
                pl.BlockSpec((tk, tn), lambda i, j, k: (k, j))]
    operands = [a, b]
    aliases = {}
    if acc is not None:
        in_specs.append(pl.BlockSpec(memory_space=pl.ANY))
        operands.append(acc)
        aliases = {2: 0}
    return pl.pallas_call(
        body, name=name, grid=(m // tm, n // tn, s // tk),
        in_specs=in_specs, out_specs=out_spec, out_shape=out_shape,
        input_output_aliases=aliases,
        compiler_params=_params(("parallel", "parallel", "arbitrary")),
    )(*operands)


BIG = ("w_in", "w_up", "w_down", "w_branch_a", "w_branch_b", "w_out")


def _block_diag(w):
    w4 = w.reshape(N_LRU_GROUPS, HEADS_PER_GROUP, RNN_HEAD_DIM, RNN_HEAD_DIM)
    eye = jnp.eye(HEADS_PER_GROUP, dtype=w.dtype)
    return jnp.einsum("gjio,jk->gjiko", w4, eye).reshape(N_LRU_GROUPS, LRU_GROUP, LRU_GROUP)


def _block_diag_extract(d):
    d5 = d.reshape(N_LRU_GROUPS, HEADS_PER_GROUP, RNN_HEAD_DIM, HEADS_PER_GROUP, RNN_HEAD_DIM)
    blocks = [d5[:, j, :, j, :] for j in range(HEADS_PER_GROUP)]
    return jnp.stack(blocks, axis=1).reshape(RNN_HEADS, RNN_HEAD_DIM, RNN_HEAD_DIM)


def _sgu_mask():
    chunk = jnp.arange(SGU_BLOCK) // CHUNK
    return (chunk[:, None] >= chunk[None, :]).astype(F32)


def _layer_small(sm, l):
    row = lambda v: v.reshape(1, -1)
    return dict(
        g1=row(sm["norm_mix_g"][l]), g2=row(sm["norm_ffn_g"][l]),
        wa=_block_diag(sm["lru_w_a"][l]).astype(BF), wx=_block_diag(sm["lru_w_x"][l]).astype(BF),
        ba=row(sm["lru_b_a"][l]), bx=row(sm["lru_b_x"][l]),
        sp=row(jax.nn.softplus(-sm["lru_lambda"][l])),
        cw=sm["conv_w"][l], cb=row(sm["conv_b"][l]),
        wm=(sm["sgu_w_s"][l] * _sgu_mask()).astype(BF),
        bsb=jnp.broadcast_to(sm["sgu_b_s"][l][:, :, None], (SGU_GROUPS, SGU_BLOCK, SGU_BLOCK)),
        lg=row(sm["sgu_ln_g"][l]), lb=row(sm["sgu_ln_b"][l]),
    )


def _local_step(x, target, big, sm, ts):
    saved = []
    for l in range(DEPTH):
        p = _layer_small(sm, l)
        h = _norm_call(x, p["g1"], ts)
        proj = _inproj_call(h, big["w_in"], l, ts)
        hr, ya_pre = _rnn_fwd_call(proj, p["wa"], p["wx"], p["ba"], p["bx"], p["sp"], p["cw"], p["cb"], ts)
        yb_pre = _sgu_fwd_call(proj, p["wm"], p["bsb"], p["lg"], p["lb"], ts)
        x1, ya, yb, merged, h2 = _merge_call(x, proj, ya_pre, yb_pre, big["w_branch_a"], big["w_branch_b"],
                                             big["w_out"], p["g2"], l, ts)
        x2, pre = _ffn_call(x1, h2, big["w_up"], big["w_down"], l, ts)
        saved.append(dict(p=p, x=x, h=h, proj=proj, hr=hr, ya_pre=ya_pre, yb_pre=yb_pre, x1=x1, ya=ya, yb=yb,
                          merged=merged, h2=h2, pre=pre))
        x = x2
    dx, loss, dgf = _loss_call(x, target, sm["final_norm_g"].reshape(1, -1), ts)

    gb = {k: None for k in BIG}
    gs = {k: [None] * DEPTH for k in (
        "norm_mix_g", "conv_w", "conv_b", "lru_w_a", "lru_b_a", "lru_w_x", "lru_b_x", "lru_lambda",
        "sgu_ln_g", "sgu_ln_b", "sgu_w_s", "sgu_b_s", "norm_ffn_g")}
    mask = _sgu_mask()
    for l in reversed(range(DEPTH)):
        sv = saved[l]
        p = sv["p"]
        dx1, dpre, dg2 = _ffn_bwd_call(dx, sv["pre"], sv["x1"], p["g2"], big["w_up"], big["w_down"], l, ts)
        gb["w_down"] = _wgrad_call(sv["pre"], dx, Q_FF, D_MODEL, ts, False, "wgrad_down", l, gb["w_down"],
                                   a_fn=_relu_sq)
        gb["w_up"] = _wgrad_call(sv["h2"], dpre, D_MODEL, Q_FF, ts, True, "wgrad_up", l, gb["w_up"])
        dya, dyb, dgate, dya_pre, dyb_pre = _merge_bwd_call(
            dx1, sv["proj"], sv["ya"], sv["yb"], big["w_branch_a"], big["w_branch_b"], big["w_out"], l, ts)
        gb["w_out"] = _wgrad_call(sv["merged"], dx1, D_MODEL, D_MODEL, ts, False, "wgrad_out", l, gb["w_out"])
        gb["w_branch_a"] = _wgrad_call(sv["ya_pre"], dya, D_RNN, D_MODEL, ts, False, "wgrad_branch_a", l,
                                       gb["w_branch_a"])
        gb["w_branch_b"] = _wgrad_call(sv["yb_pre"], dyb, D_SGU, D_MODEL, ts, False, "wgrad_branch_b", l,
                                       gb["w_branch_b"])
        duv, dws, dbs, dlg, dlb = _sgu_bwd_call(dyb_pre, sv["proj"], p["wm"], p["bsb"], mask, p["lg"], p["lb"], ts)
        dxg, dwa, dwx, vec = _rnn_bwd_call(dya_pre, sv["proj"], sv["hr"], p["wa"], p["wx"], p["ba"], p["bx"],
                                           p["sp"], p["cw"], p["cb"], ts)
        dx, dproj, dg1 = _inproj_bwd_call(dxg, duv, dgate, dx1, sv["x"], p["g1"], big["w_in"], l, ts)
        gb["w_in"] = _wgrad_call(sv["h"], dproj, D_MODEL, Q_IN, ts, True, "wgrad_in", l, gb["w_in"])

        gs["norm_mix_g"][l] = dg1[0]
        gs["norm_ffn_g"][l] = dg2[0]
        gs["conv_w"][l] = vec[_ROW_DCW:_ROW_DCW + CONV_WIDTH]
        gs["conv_b"][l] = vec[_ROW_DCB]
        gs["lru_w_a"][l] = _block_diag_extract(dwa)
        gs["lru_w_x"][l] = _block_diag_extract(dwx)
        gs["lru_b_a"][l] = vec[_ROW_DBA].reshape(RNN_HEADS, RNN_HEAD_DIM)
        gs["lru_b_x"][l] = vec[_ROW_DBX].reshape(RNN_HEADS, RNN_HEAD_DIM)
        gs["lru_lambda"][l] = -vec[_ROW_DSP] * jax.nn.sigmoid(-sm["lru_lambda"][l])
        gs["sgu_ln_g"][l] = dlg[0]
        gs["sgu_ln_b"][l] = dlb[0]
        gs["sgu_w_s"][l] = dws
        gs["sgu_b_s"][l] = dbs.T
    gs = {k: jnp.stack(v) for k, v in gs.items()}
    gs["final_norm_g"] = dgf[0]
    return loss, dx, gb, gs


EW_BLOCK_ELEMS = 384 * 1024


def _row_block(rows, cols):
    for br in range(min(rows, EW_BLOCK_ELEMS // cols), 0, -1):
        if rows % br == 0 and br % 16 == 0:
            return br
    return rows


def _ew_call(fn, name, operands, out_shapes, sel=None):
    rows, cols = out_shapes[0].shape
    br = _row_block(rows, cols)
    n_in = len(operands)

    def spec(lead):
        if lead is None:
            return pl.BlockSpec((br, cols), lambda i, s: (i, 0))
        if lead == "sel":
            return pl.BlockSpec((None, br, cols), lambda i, s: (s[0], i, 0))
        return pl.BlockSpec((None, br, cols), lambda i, s, lead=lead: (lead, i, 0))

    def body(sel_ref, *refs):
        outs = fn(*[r[...] for r in refs[:n_in]])
        for o_ref, o in zip(refs[n_in:], outs):
            o_ref[...] = o.astype(o_ref.dtype)

    if sel is None:
        sel = jnp.zeros((1,), jnp.int32)
    return pl.pallas_call(
        body, name=name, out_shape=list(out_shapes),
        grid_spec=pltpu.PrefetchScalarGridSpec(
            num_scalar_prefetch=1, grid=(rows // br,),
            in_specs=[spec(lead) for _, lead in operands],
            out_specs=[pl.BlockSpec((br, cols), lambda i, s: (i, 0)) for _ in out_shapes]),
        compiler_params=_params(("parallel",)),
    )(sel, *[a for a, _ in operands])


def _adamw(w, g, m, v):
    m = ADAM_B1 * m + (1.0 - ADAM_B1) * g
    v = ADAM_B2 * v + (1.0 - ADAM_B2) * jnp.square(g)
    m_hat = m / (1.0 - ADAM_B1 ** ADAM_STEP)
    v_hat = v / (1.0 - ADAM_B2 ** ADAM_STEP)
    delta = -ADAM_LR * (m_hat / (jnp.sqrt(v_hat) + ADAM_EPS) + ADAM_WD * w)
    return delta, m, v


def _small_adamw_call(ws, gs, ms, vs):
    n = len(ws)

    def body(*refs):
        for k in range(n):
            w, g, m, v = (refs[j * n + k][...] for j in range(4))
            outs = _adamw(w, g, m, v)
            for j in range(3):
                refs[(4 + j) * n + k][...] = outs[j]

    shapes = [jax.ShapeDtypeStruct(w.shape, F32) for w in ws]
    outs = pl.pallas_call(
        body, name="adamw_small", out_shape=shapes * 3,
        in_specs=[pl.BlockSpec(memory_space=pltpu.VMEM)] * (4 * n),
        out_specs=[pl.BlockSpec(memory_space=pltpu.VMEM)] * (3 * n),
        compiler_params=_params(),
    )(*ws, *gs, *ms, *vs)
    return outs[:n], outs[n:2 * n], outs[2 * n:]


ANY = pl.BlockSpec(memory_space=pl.ANY)


def _place():
    x, y, c = lax.axis_index("x"), lax.axis_index("y"), lax.axis_index("c")
    chips = [(1 - x, y), (x, 1 - y), (1 - x, 1 - y)]
    return x, y, c, chips


def _remote(src, dst, send_sem, recv_sem, to):
    return pltpu.make_async_remote_copy(src_ref=src, dst_ref=dst, send_sem=send_sem, recv_sem=recv_sem,
                                        device_id=to, device_id_type=MESH)


def _gather_call(own):
    n = len(own)

    def body(*refs):
        src, out = refs[:n], refs[n:2 * n]
        send_sems, recv_sems, local_sems = refs[2 * n:]
        x, y, c, chips = _place()
        me_q = 2 * x + y
        sibling = (x, y, 1 - c)
        local = [pltpu.make_async_copy(src[w], out[w].at[:, me_q], local_sems.at[w]) for w in range(n)]
        for cp in local:
            cp.start()
        first = []
        for w in range(n):
            for j, chip in enumerate(chips):
                first.append(_remote(src[w].at[c], out[w].at[c, me_q], send_sems.at[w * 3 + j],
                                     recv_sems.at[w * 3 + j], (*chip, c)))
        for cp in first:
            cp.start()
        passed = []
        for w in range(n):
            for j, (qx, qy) in enumerate(chips):
                landed = out[w].at[c, 2 * qx + qy]
                k = w * 3 + j
                _remote(landed, landed, send_sems.at[k], recv_sems.at[k], (qx, qy, c)).wait_recv()
                cp = _remote(landed, landed, send_sems.at[3 * n + k], recv_sems.at[3 * n + k], sibling)
                cp.start()
                passed.append(cp)
        for w in range(n):
            for j, (qx, qy) in enumerate(chips):
                landed = out[w].at[1 - c, 2 * qx + qy]
                k = 3 * n + w * 3 + j
                _remote(landed, landed, send_sems.at[k], recv_sems.at[k], sibling).wait_recv()
        for cp in first + passed:
            cp.wait_send()
        for cp in local:
            cp.wait()

    return pl.pallas_call(
        body, name="gather_weights",
        out_shape=[jax.ShapeDtypeStruct((a.shape[0], N_QUARTERS) + a.shape[1:], a.dtype) for a in own],
        in_specs=[ANY] * n, out_specs=[ANY] * n,
        scratch_shapes=[pltpu.SemaphoreType.DMA((6 * n,)), pltpu.SemaphoreType.DMA((6 * n,)),
                        pltpu.SemaphoreType.DMA((n,))],
        compiler_params=_params(vmem=False, has_side_effects=True),
    )(*own)


def _sibling_send_call(items):
    n = len(items)

    def body(*refs):
        src, out = refs[:n], refs[n:2 * n]
        send_sems, recv_sems = refs[2 * n:]
        x, y, c, _ = _place()
        copies = [_remote(src[w], out[w], send_sems.at[w], recv_sems.at[w], (x, y, 1 - c)) for w in range(n)]
        for cp in copies:
            cp.start()
        for cp in copies:
            cp.wait()

    return pl.pallas_call(
        body, name="grads_to_sibling",
        out_shape=[jax.ShapeDtypeStruct(a.shape, a.dtype) for a in items],
        in_specs=[ANY] * n, out_specs=[ANY] * n,
        scratch_shapes=[pltpu.SemaphoreType.DMA((n,)), pltpu.SemaphoreType.DMA((n,))],
        compiler_params=_params(vmem=False, has_side_effects=True),
    )(*items)


def _quarter_exchange_call(items):
    n = len(items)

    def body(*refs):
        src, out = refs[:n], refs[n:2 * n]
        send_sems, recv_sems, local_sems = refs[2 * n:]
        x, y, c, chips = _place()
        me_q = 2 * x + y
        local = [pltpu.make_async_copy(src[w].at[me_q], out[w].at[me_q], local_sems.at[w]) for w in range(n)]
        for cp in local:
            cp.start()
        sends = []
        for w in range(n):
            for j, (qx, qy) in enumerate(chips):
                k = w * 3 + j
                sends.append(_remote(src[w].at[2 * qx + qy], out[w].at[me_q], send_sems.at[k], recv_sems.at[k],
                                     (qx, qy, c)))
        for cp in sends:
            cp.start()
        for w in range(n):
            for j, (qx, qy) in enumerate(chips):
                k = w * 3 + j
                landed = out[w].at[2 * qx + qy]
                _remote(landed, landed, send_sems.at[k], recv_sems.at[k], (qx, qy, c)).wait_recv()
        for cp in sends:
            cp.wait_send()
        for cp in local:
            cp.wait()

    return pl.pallas_call(
        body, name="grads_to_owner",
        out_shape=[jax.ShapeDtypeStruct(a.shape, a.dtype) for a in items],
        in_specs=[ANY] * n, out_specs=[ANY] * n,
        scratch_shapes=[pltpu.SemaphoreType.DMA((3 * n,)), pltpu.SemaphoreType.DMA((3 * n,)),
                        pltpu.SemaphoreType.DMA((n,))],
        compiler_params=_params(vmem=False, has_side_effects=True),
    )(*items)


def _layer_swap_call(items):
    n = len(items)

    def body(*refs):
        src, out = refs[:n], refs[n:2 * n]
        send_sems, recv_sems, local_sems = refs[2 * n:]
        x, y, c, _ = _place()
        local = [pltpu.make_async_copy(src[w], out[w].at[c], local_sems.at[w]) for w in range(n)]
        sends = [_remote(src[w], out[w].at[c], send_sems.at[w], recv_sems.at[w], (x, y, 1 - c)) for w in range(n)]
        for cp in local + sends:
            cp.start()
        for w in range(n):
            landed = out[w].at[1 - c]
            _remote(landed, landed, send_sems.at[w], recv_sems.at[w], (x, y, 1 - c)).wait_recv()
        for cp in sends:
            cp.wait_send()
        for cp in local:
            cp.wait()

    return pl.pallas_call(
        body, name="grads_swap_layers",
        out_shape=[jax.ShapeDtypeStruct((DEPTH,) + a.shape, a.dtype) for a in items],
        in_specs=[ANY] * n, out_specs=[ANY] * n,
        scratch_shapes=[pltpu.SemaphoreType.DMA((n,)), pltpu.SemaphoreType.DMA((n,)),
                        pltpu.SemaphoreType.DMA((n,))],
        compiler_params=_params(vmem=False, has_side_effects=True),
    )(*items)


N_DEVICES = 8
SMALL_ROWS = 616


def _small_allreduce_call(buf):
    def body(in_ref, out_ref, recv_ref, red_ref, send_sems, recv_sems):
        x, y, c, _ = _place()
        me = 4 * x + 2 * y + c

        def peer(k):
            return (x ^ ((k >> 2) & 1), y ^ ((k >> 1) & 1), c ^ (k & 1))

        scatter = [_remote(in_ref.at[me ^ k], recv_ref.at[me], send_sems.at[k - 1], recv_sems.at[k - 1], peer(k))
                   for k in range(1, N_DEVICES)]
        for cp in scatter:
            cp.start()
        recv_ref[me] = in_ref[me]
        for k in range(1, N_DEVICES):
            landed = recv_ref.at[me ^ k]
            _remote(landed, landed, send_sems.at[k - 1], recv_sems.at[k - 1], peer(k)).wait_recv()
        total = recv_ref[0]
        for j in range(1, N_DEVICES):
            total = total + recv_ref[j]
        red_ref[...] = total
        out_ref[me] = total
        spread = [_remote(red_ref, out_ref.at[me], send_sems.at[6 + k], recv_sems.at[6 + k], peer(k))
                  for k in range(1, N_DEVICES)]
        for cp in spread:
            cp.start()
        for k in range(1, N_DEVICES):
            landed = out_ref.at[me ^ k]
            _remote(landed, landed, send_sems.at[6 + k], recv_sems.at[6 + k], peer(k)).wait_recv()
        for cp in scatter + spread:
            cp.wait_send()

    shape = (N_DEVICES, SMALL_ROWS, 128)
    return pl.pallas_call(
        body, name="allreduce_small",
        out_shape=jax.ShapeDtypeStruct(shape, F32),
        in_specs=[pl.BlockSpec(memory_space=pltpu.VMEM)],
        out_specs=pl.BlockSpec(memory_space=pltpu.VMEM),
        scratch_shapes=[pltpu.VMEM(shape, F32), pltpu.VMEM(shape[1:], F32),
                        pltpu.SemaphoreType.DMA((2 * (N_DEVICES - 1),)),
                        pltpu.SemaphoreType.DMA((2 * (N_DEVICES - 1),))],
        compiler_params=_params(has_side_effects=True),
    )(buf)


SMALL = ("norm_mix_g", "conv_w", "conv_b", "lru_w_a", "lru_b_a", "lru_w_x", "lru_b_x", "lru_lambda",
         "sgu_ln_g", "sgu_ln_b", "sgu_w_s", "sgu_b_s", "norm_ffn_g", "final_norm_g")
WEIGHTS = ("norm_mix_g", "w_in", "conv_w", "conv_b", "lru_w_a", "lru_b_a", "lru_w_x", "lru_b_x", "lru_lambda",
           "sgu_ln_g", "sgu_ln_b", "sgu_w_s", "sgu_b_s", "w_branch_a", "w_branch_b", "w_out", "norm_ffn_g",
           "w_up", "w_down", "final_norm_g")
PACK_ALIGN = SUBLANES * 128


def _pack_small(gs):
    parts = []
    for k in SMALL:
        flat = gs[k].reshape(-1)
        parts.append(jnp.pad(flat, (0, -flat.size % PACK_ALIGN)))
    flat = jnp.concatenate(parts)
    flat = jnp.pad(flat, (0, N_DEVICES * SMALL_ROWS * 128 - flat.size))
    return flat.reshape(N_DEVICES, SMALL_ROWS, 128)


def _unpack_small(buf, like):
    flat = buf.reshape(-1)
    out, off = {}, 0
    for k in SMALL:
        size = like[k].size
        out[k] = flat[off:off + size].reshape(like[k].shape)
        off += size + (-size % PACK_ALIGN)
    return out


def _as_rows(a):
    return a.reshape(-1, a.shape[-1])


def kernel(x, norm_mix_g, w_in, conv_w, conv_b, lru_w_a, lru_b_a, lru_w_x, lru_b_x, lru_lambda, sgu_ln_g, sgu_ln_b, sgu_w_s, sgu_b_s, w_branch_a, w_branch_b, w_out, norm_ffn_g, w_up, w_down, final_norm_g, loss_target, m_norm_mix_g, m_w_in, m_conv_w, m_conv_b, m_lru_w_a, m_lru_b_a, m_lru_w_x, m_lru_b_x, m_lru_lambda, m_sgu_ln_g, m_sgu_ln_b, m_sgu_w_s, m_sgu_b_s, m_w_branch_a, m_w_branch_b, m_w_out, m_norm_ffn_g, m_w_up, m_w_down, m_final_norm_g, v_norm_mix_g, v_w_in, v_conv_w, v_conv_b, v_lru_w_a, v_lru_b_a, v_lru_w_x, v_lru_b_x, v_lru_lambda, v_sgu_ln_g, v_sgu_ln_b, v_sgu_w_s, v_sgu_b_s, v_w_branch_a, v_w_branch_b, v_w_out, v_norm_ffn_g, v_w_up, v_w_down, v_final_norm_g):
    w = dict(norm_mix_g=norm_mix_g, w_in=w_in, conv_w=conv_w, conv_b=conv_b, lru_w_a=lru_w_a, lru_b_a=lru_b_a,
             lru_w_x=lru_w_x, lru_b_x=lru_b_x, lru_lambda=lru_lambda, sgu_ln_g=sgu_ln_g, sgu_ln_b=sgu_ln_b,
             sgu_w_s=sgu_w_s, sgu_b_s=sgu_b_s, w_branch_a=w_branch_a, w_branch_b=w_branch_b, w_out=w_out,
             norm_ffn_g=norm_ffn_g, w_up=w_up, w_down=w_down, final_norm_g=final_norm_g)
    m = dict(norm_mix_g=m_norm_mix_g, w_in=m_w_in, conv_w=m_conv_w, conv_b=m_conv_b, lru_w_a=m_lru_w_a,
             lru_b_a=m_lru_b_a, lru_w_x=m_lru_w_x, lru_b_x=m_lru_b_x, lru_lambda=m_lru_lambda,
             sgu_ln_g=m_sgu_ln_g, sgu_ln_b=m_sgu_ln_b, sgu_w_s=m_sgu_w_s, sgu_b_s=m_sgu_b_s,
             w_branch_a=m_w_branch_a, w_branch_b=m_w_branch_b, w_out=m_w_out, norm_ffn_g=m_norm_ffn_g,
             w_up=m_w_up, w_down=m_w_down, final_norm_g=m_final_norm_g)
    v = dict(norm_mix_g=v_norm_mix_g, w_in=v_w_in, conv_w=v_conv_w, conv_b=v_conv_b, lru_w_a=v_lru_w_a,
             lru_b_a=v_lru_b_a, lru_w_x=v_lru_w_x, lru_b_x=v_lru_b_x, lru_lambda=v_lru_lambda,
             sgu_ln_g=v_sgu_ln_g, sgu_ln_b=v_sgu_ln_b, sgu_w_s=v_sgu_w_s, sgu_b_s=v_sgu_b_s,
             w_branch_a=v_w_branch_a, w_branch_b=v_w_branch_b, w_out=v_w_out, norm_ffn_g=v_norm_ffn_g,
             w_up=v_w_up, w_down=v_w_down, final_norm_g=v_final_norm_g)
    core = lax.axis_index("c")
    chip = 2 * lax.axis_index("x") + lax.axis_index("y")
    sel_core = core.astype(jnp.int32).reshape(1)
    sel_other = (1 - core).astype(jnp.int32).reshape(1)

    own = []
    for k in BIG:
        rows = _as_rows(w[k])
        cast = _ew_call(lambda a: (a,), "cast_weights", [(rows, None)], [jax.ShapeDtypeStruct(rows.shape, BF)])[0]
        own.append(cast.reshape(w[k].shape))
    gathered = _gather_call(own + [conv_w])
    full = dict(zip(BIG, gathered[:-1]))
    big = {k: (full[k] if k in ("w_in", "w_up") else full[k].reshape(DEPTH, -1, D_MODEL)) for k in BIG}
    sm = {k: w[k] for k in SMALL}
    sm["conv_w"] = gathered[-1].transpose(0, 2, 1, 3).reshape(DEPTH, CONV_WIDTH, D_RNN)

    loss, grad_x, gb, gs = _local_step(x[0], loss_target[0], big, sm, TOKEN_TILE)

    flat = {k: gb[k].reshape(DEPTH, -1, gb[k].shape[-1]) for k in BIG}
    to_sibling = [
        _ew_call(lambda a: (a,), "cast_grads", [(flat[k], "sel")],
                 [jax.ShapeDtypeStruct(flat[k].shape[1:], BF)], sel_other)[0] for k in BIG]
    from_sibling = _sibling_send_call(to_sibling)
    pair = [
        _ew_call(lambda a, b: (a + b.astype(F32),), "pair_sum", [(flat[k], "sel"), (r, None)],
                 [jax.ShapeDtypeStruct(r.shape, BF)], sel_core)[0].reshape(N_QUARTERS, -1, r.shape[-1])
        for k, r in zip(BIG, from_sibling)]
    landed = _quarter_exchange_call(pair)
    mine = [
        _ew_call(lambda a, b, c, d: (((a.astype(F32) + b.astype(F32)) + c.astype(F32)) + d.astype(F32),),
                 "quarter_sum", [(r, q) for q in range(N_QUARTERS)],
                 [jax.ShapeDtypeStruct(r.shape[1:], F32)])[0] for r in landed]
    grads_big = dict(zip(BIG, _layer_swap_call(mine)))
    grads_big = {k: g.reshape(w[k].shape) for k, g in grads_big.items()}

    like = {k: jax.ShapeDtypeStruct(sm[k].shape, F32) for k in SMALL}
    grads_small = _unpack_small(_small_allreduce_call(_pack_small(gs)), like)
    conv_q = grads_small["conv_w"].reshape(DEPTH, CONV_WIDTH, N_QUARTERS, D_RNN // N_QUARTERS)
    grads_small["conv_w"] = lax.dynamic_index_in_dim(conv_q, chip, axis=2, keepdims=False)

    delta, new_m, new_v = {}, {}, {}
    for k in BIG:
        shape = jax.ShapeDtypeStruct(_as_rows(w[k]).shape, F32)
        outs = _ew_call(_adamw, "adamw_big", [(_as_rows(a), None) for a in (w[k], grads_big[k], m[k], v[k])],
                        [shape] * 3)
        delta[k], new_m[k], new_v[k] = (o.reshape(w[k].shape) for o in outs)
    outs = _small_adamw_call(*[[_as_rows(d[k]) for k in SMALL] for d in (w, grads_small, m, v)])
    for d, o in zip((delta, new_m, new_v), outs):
        for k, a in zip(SMALL, o):
            d[k] = a.reshape(w[k].shape)

    grads = {**grads_big, **grads_small}
    total = lax.psum(loss[0, 0], ("x", "y", "c"))
    return (total, grad_x[None], *[grads[k] for k in WEIGHTS], *[delta[k] for k in WEIGHTS],
            *[new_m[k] for k in WEIGHTS], *[new_v[k] for k in WEIGHTS])
```

```python
import functools
import math

import jax
import jax.numpy as jnp
from jax import lax
from jax.experimental import pallas as pl
from jax.experimental.pallas import tpu as pltpu

F32 = jnp.float32
BF = jnp.bfloat16

DEPTH = 2
D_MODEL = 1024
D_RNN = 1280
D_SGU = 1024
D_FF = 4096
D_IN = 2 * D_RNN + 2 * D_SGU + 2 * D_MODEL
N_QUARTERS = 4
Q_IN = D_IN // N_QUARTERS
Q_FF = D_FF // N_QUARTERS
RNN_HEADS = 20
RNN_HEAD_DIM = 64
LRU_GROUP = 256
N_LRU_GROUPS = D_RNN // LRU_GROUP
HEADS_PER_GROUP = LRU_GROUP // RNN_HEAD_DIM
CONV_WIDTH = 4
LRU_C = 8.0
SGU_GROUPS = 8
SGU_BLOCK = 128
CHUNK = 64
EPS = 1e-6

ADAM_LR = 0.001
ADAM_B1 = 0.9
ADAM_B2 = 0.999
ADAM_EPS = 1e-08
ADAM_WD = 0.01
ADAM_STEP = 10

SUBLANES = 8
TOKEN_TILE = 512
VMEM_LIMIT_BYTES = 56 * 1024 * 1024

MESH = pl.DeviceIdType.MESH


def _params(semantics=None, vmem=True, **kw):
    return pltpu.CompilerParams(
        dimension_semantics=semantics,
        vmem_limit_bytes=VMEM_LIMIT_BYTES if vmem else None,
        **kw,
    )


def _dot(a, b):
    return jnp.dot(a, b, preferred_element_type=F32)


def _dot_nt(a, b):
    return lax.dot_general(a, b, (((1,), (1,)), ((), ())), preferred_element_type=F32)


def _dot_tn(a, b):
    return lax.dot_general(a, b, (((0,), (0,)), ((), ())), preferred_element_type=F32)


_GELU_C = math.sqrt(2.0 / math.pi)
_GELU_A = 0.044715


def _gelu(x):
    return 0.5 * x * (1.0 + jnp.tanh(_GELU_C * (x + _GELU_A * x * x * x)))


def _gelu_and_grad(x):
    x2 = x * x
    t = jnp.tanh(_GELU_C * (x + _GELU_A * x2 * x))
    du = _GELU_C * (1.0 + 3.0 * _GELU_A * x2)
    return 0.5 * x * (1.0 + t), 0.5 * (1.0 + t) + 0.5 * x * (1.0 - t * t) * du


def _expm1(x):
    series = x * (1.0 + 0.5 * x * (1.0 + (1.0 / 3.0) * x * (1.0 + 0.25 * x)))
    return jnp.where(jnp.abs(x) < 0.05, series, jnp.exp(x) - 1.0)


def _rms_stats(x):
    return lax.rsqrt(jnp.mean(x * x, axis=-1, keepdims=True) + EPS)


def _rms_bwd(dy, x, g):
    rs = _rms_stats(x)
    n = x * rs
    dn = dy * g
    dx = rs * (dn - n * jnp.mean(dn * n, axis=-1, keepdims=True))
    return dx, dy * n


def _row_sum(x):
    return jnp.sum(x, axis=0, keepdims=True)


def _tile_spec(ts, width, col=0):
    return pl.BlockSpec((ts, width), lambda i, col=col: (i, col))


def _full_spec(shape):
    zeros = (0,) * len(shape)
    return pl.BlockSpec(shape, lambda *_: zeros)


def _layer_spec(w, layer):
    zeros = (0,) * (w.ndim - 1)
    return pl.BlockSpec((None,) + tuple(w.shape[1:]), lambda *_: (layer,) + zeros)


def _norm_call(x, g, ts):
    s = x.shape[0]

    def body(x_ref, g_ref, h_ref):
        xv = x_ref[...]
        h_ref[...] = (xv * _rms_stats(xv) * g_ref[...]).astype(BF)

    return pl.pallas_call(
        body, name="norm_fwd", grid=(s // ts,),
        in_specs=[_tile_spec(ts, D_MODEL), _full_spec((1, D_MODEL))],
        out_specs=_tile_spec(ts, D_MODEL),
        out_shape=jax.ShapeDtypeStruct((s, D_MODEL), BF),
        compiler_params=_params(("parallel",)),
    )(x, g)


def _inproj_call(h, w_in, layer, ts):
    s = h.shape[0]

    def body(h_ref, w_ref, o_ref):
        o_ref[...] = _dot(h_ref[...], w_ref[...]).astype(BF)

    return pl.pallas_call(
        body, name="inproj_fwd", grid=(N_QUARTERS, s // ts),
        in_specs=[
            pl.BlockSpec((ts, D_MODEL), lambda q, i: (i, 0)),
            pl.BlockSpec((None, None, D_MODEL, Q_IN), lambda q, i: (layer, q, 0, 0)),
        ],
        out_specs=pl.BlockSpec((ts, Q_IN), lambda q, i: (i, q)),
        out_shape=jax.ShapeDtypeStruct((s, D_IN), BF),
        compiler_params=_params(("parallel", "parallel")),
    )(h, w_in)


def _shift_down(x, tail, s):
    xr = pltpu.roll(x, s, 0)
    tr = pltpu.roll(tail, s, 0)
    row = lax.broadcasted_iota(jnp.int32, tail.shape, 0)
    top = jnp.where(row < s, tr, xr[0:SUBLANES])
    return jnp.concatenate([top, xr[SUBLANES:]], axis=0)


def _shift_up(x, head, s):
    t = x.shape[0]
    xr = pltpu.roll(x, t - s, 0)
    hr = pltpu.roll(head, SUBLANES - s, 0)
    row = lax.broadcasted_iota(jnp.int32, head.shape, 0)
    bottom = jnp.where(row >= SUBLANES - s, hr, xr[t - SUBLANES:])
    return jnp.concatenate([xr[: t - SUBLANES], bottom], axis=0)


def _conv_fwd(x, tail, cw_ref, cb_ref):
    out = cb_ref[...] + cw_ref[CONV_WIDTH - 1:CONV_WIDTH, :] * x
    for s in range(1, CONV_WIDTH):
        k = CONV_WIDTH - 1 - s
        out = out + cw_ref[k:k + 1, :] * _shift_down(x, tail, s)
    return out


def _group_dot(x_bf, w_ref, dot):
    cols = [dot(x_bf[:, g * LRU_GROUP:(g + 1) * LRU_GROUP], w_ref[g]) for g in range(N_LRU_GROUPS)]
    return jnp.concatenate(cols, axis=1)


def _lru_gates(xr, wa_ref, wx_ref, ba_ref, bx_ref, sp_ref):
    xb = xr.astype(BF)
    r = jax.nn.sigmoid(_group_dot(xb, wa_ref, _dot) + ba_ref[...])
    i = jax.nn.sigmoid(_group_dot(xb, wx_ref, _dot) + bx_ref[...])
    log_a = (-LRU_C * r) * sp_ref[...]
    a = jnp.exp(log_a)
    nrm = jnp.sqrt(-_expm1(2.0 * log_a))
    return r, i, a, nrm


def _linear_scan(a, b, carry, al_ref, bl_ref, h_ref, reverse):
    t, c = a.shape
    rowm = lax.broadcasted_iota(jnp.int32, (t, c), 0) & (SUBLANES - 1)
    for d in (1, 2, 4):
        if reverse:
            keep, sh = rowm < SUBLANES - d, t - d
        else:
            keep, sh = rowm >= d, d
        a_sh = jnp.where(keep, pltpu.roll(a, sh, 0), 1.0)
        b_sh = jnp.where(keep, pltpu.roll(b, sh, 0), 0.0)
        b = a * b_sh + b
        a = a * a_sh
    al_ref[...] = a
    bl_ref[...] = b
    groups = t // SUBLANES

    def step(j, state):
        jj = groups - 1 - j if reverse else j
        off = pl.multiple_of(jj * SUBLANES, SUBLANES)
        rows = bl_ref[pl.ds(off, SUBLANES), :] + al_ref[pl.ds(off, SUBLANES), :] * state
        h_ref[pl.ds(off, SUBLANES), :] = rows
        last = rows[0:1, :] if reverse else rows[SUBLANES - 1:SUBLANES, :]
        return jnp.broadcast_to(last, (SUBLANES, c))

    out = lax.fori_loop(0, groups, step, jnp.broadcast_to(carry, (SUBLANES, c)))
    return out[0:1, :]


def _rnn_fwd_call(proj, wa, wx, ba, bx, sp, cw, cb, ts):
    s = proj.shape[0]

    def body(xg_ref, wa_ref, wx_ref, ba_ref, bx_ref, sp_ref, cw_ref, cb_ref, hr_ref, ya_ref,
             tail_sc, carry_sc, al_sc, bl_sc, h_sc):
        @pl.when(pl.program_id(0) == 0)
        def _():
            tail_sc[...] = jnp.zeros_like(tail_sc)
            carry_sc[...] = jnp.zeros_like(carry_sc)

        x = xg_ref[:, :D_RNN].astype(F32)
        g = xg_ref[:, D_RNN:].astype(F32)
        xr = _conv_fwd(x, tail_sc[...], cw_ref, cb_ref)
        tail_sc[...] = x[ts - SUBLANES:, :]
        _, i, a, nrm = _lru_gates(xr, wa_ref, wx_ref, ba_ref, bx_ref, sp_ref)
        carry_sc[...] = _linear_scan(a, nrm * (i * xr), carry_sc[...], al_sc, bl_sc, h_sc, False)
        h = h_sc[...]
        hr_ref[...] = h.astype(BF)
        ya_ref[...] = (h * _gelu(g)).astype(BF)

    gw = (N_LRU_GROUPS, LRU_GROUP, LRU_GROUP)
    return pl.pallas_call(
        body, name="rnn_fwd", grid=(s // ts,),
        in_specs=[_tile_spec(ts, 2 * D_RNN), _full_spec(gw), _full_spec(gw),
                  _full_spec((1, D_RNN)), _full_spec((1, D_RNN)), _full_spec((1, D_RNN)),
                  _full_spec((CONV_WIDTH, D_RNN)), _full_spec((1, D_RNN))],
        out_specs=[_tile_spec(ts, D_RNN), _tile_spec(ts, D_RNN)],
        out_shape=[jax.ShapeDtypeStruct((s, D_RNN), BF), jax.ShapeDtypeStruct((s, D_RNN), BF)],
        scratch_shapes=[pltpu.VMEM((SUBLANES, D_RNN), F32), pltpu.VMEM((1, D_RNN), F32),
                        pltpu.VMEM((ts, D_RNN), F32), pltpu.VMEM((ts, D_RNN), F32),
                        pltpu.VMEM((ts, D_RNN), F32)],
        compiler_params=_params(("arbitrary",)),
    )(proj, wa, wx, ba, bx, sp, cw, cb)


def _layernorm_fwd(x):
    mu = jnp.mean(x, axis=-1, keepdims=True)
    xc = x - mu
    rstd = lax.rsqrt(jnp.mean(xc * xc, axis=-1, keepdims=True) + EPS)
    return xc * rstd, rstd


def _sgu_mix(vn_bf, wm_ref, bsb_ref, ts):
    rows = []
    for blk in range(ts // SGU_BLOCK):
        r0 = blk * SGU_BLOCK
        cols = [
            _dot(wm_ref[g], vn_bf[r0:r0 + SGU_BLOCK, g * SGU_BLOCK:(g + 1) * SGU_BLOCK]) + bsb_ref[g]
            for g in range(SGU_GROUPS)
        ]
        rows.append(jnp.concatenate(cols, axis=1))
    return jnp.concatenate(rows, axis=0)


def _sgu_fwd_call(proj, wm, bsb, lg, lb, ts):
    s = proj.shape[0]

    def body(uv_ref, wm_ref, bsb_ref, lg_ref, lb_ref, yb_ref):
        gu = _gelu(uv_ref[:, :D_SGU].astype(F32))
        gv = _gelu(uv_ref[:, D_SGU:2 * D_SGU].astype(F32))
        nh, _ = _layernorm_fwd(gv)
        vn = (nh * lg_ref[...] + lb_ref[...]).astype(BF)
        yb_ref[...] = (gu * _sgu_mix(vn, wm_ref, bsb_ref, ts)).astype(BF)

    sw = (SGU_GROUPS, SGU_BLOCK, SGU_BLOCK)
    return pl.pallas_call(
        body, name="sgu_fwd", grid=(s // ts,),
        in_specs=[_tile_spec(ts, 2 * D_RNN, 1), _full_spec(sw), _full_spec(sw),
                  _full_spec((1, D_SGU)), _full_spec((1, D_SGU))],
        out_specs=_tile_spec(ts, D_SGU),
        out_shape=jax.ShapeDtypeStruct((s, D_SGU), BF),
        compiler_params=_params(("parallel",)),
    )(proj, wm, bsb, lg, lb)


_GATE_COL0 = (2 * D_RNN + 2 * D_SGU) // 512


def _gate_specs(ts):
    return [_tile_spec(ts, 512, _GATE_COL0 + j) for j in range(4)]


def _merge_call(x, proj, ya_pre, yb_pre, w_ba, w_bb, w_out, g2, layer, ts):
    s = x.shape[0]

    def body(x_ref, ga0, ga1, gb0, gb1, ya_ref, yb_ref, wa_ref, wb_ref, wo_ref, g2_ref,
             x1_ref, yao_ref, ybo_ref, mg_ref, h2_ref):
        ya = _dot(ya_ref[...], wa_ref[...])
        yb = _dot(yb_ref[...], wb_ref[...])
        sa = jax.nn.sigmoid(jnp.concatenate([ga0[...], ga1[...]], axis=1).astype(F32))
        sb = jax.nn.sigmoid(jnp.concatenate([gb0[...], gb1[...]], axis=1).astype(F32))
        merged = (sa * ya + sb * yb).astype(BF)
        x1 = x_ref[...] + _dot(merged, wo_ref[...])
        x1_ref[...] = x1
        yao_ref[...] = ya.astype(BF)
        ybo_ref[...] = yb.astype(BF)
        mg_ref[...] = merged
        h2_ref[...] = (x1 * _rms_stats(x1) * g2_ref[...]).astype(BF)

    act = jax.ShapeDtypeStruct((s, D_MODEL), BF)
    return pl.pallas_call(
        body, name="merge_fwd", grid=(s // ts,),
        in_specs=[_tile_spec(ts, D_MODEL)] + _gate_specs(ts) + [
            _tile_spec(ts, D_RNN), _tile_spec(ts, D_SGU),
            _layer_spec(w_ba, layer), _layer_spec(w_bb, layer), _layer_spec(w_out, layer),
            _full_spec((1, D_MODEL))],
        out_specs=[_tile_spec(ts, D_MODEL)] * 5,
        out_shape=[jax.ShapeDtypeStruct((s, D_MODEL), F32), act, act, act, act],
        compiler_params=_params(("parallel",)),
    )(x, proj, proj, proj, proj, ya_pre, yb_pre, w_ba, w_bb, w_out, g2)


def _ffn_call(x1, h2, w_up, w_down, layer, ts):
    s = x1.shape[0]

    def body(x1_ref, h2_ref, wu_ref, wd_ref, x2_ref, p_ref):
        h2v = h2_ref[...]
        acc = x1_ref[...]
        for q in range(N_QUARTERS):
            p = _dot(h2v, wu_ref[q])
            p_ref[:, q * Q_FF:(q + 1) * Q_FF] = p.astype(BF)
            f = jnp.square(jnp.maximum(p, 0.0)).astype(BF)
            acc = acc + _dot(f, wd_ref[q * Q_FF:(q + 1) * Q_FF, :])
        x2_ref[...] = acc

    return pl.pallas_call(
        body, name="ffn_fwd", grid=(s // ts,),
        in_specs=[_tile_spec(ts, D_MODEL), _tile_spec(ts, D_MODEL),
                  pl.BlockSpec((None, N_QUARTERS, D_MODEL, Q_FF), lambda i: (layer, 0, 0, 0)),
                  pl.BlockSpec((None, D_FF, D_MODEL), lambda i: (layer, 0, 0))],
        out_specs=[_tile_spec(ts, D_MODEL), _tile_spec(ts, D_FF)],
        out_shape=[jax.ShapeDtypeStruct((s, D_MODEL), F32), jax.ShapeDtypeStruct((s, D_FF), BF)],
        compiler_params=_params(("parallel",)),
    )(x1, h2, w_up, w_down)


def _loss_call(x, target, gf, ts):
    s = x.shape[0]

    def body(x_ref, t_ref, g_ref, dx_ref, loss_ref, dg_ref):
        @pl.when(pl.program_id(0) == 0)
        def _():
            loss_ref[...] = jnp.zeros_like(loss_ref)
            dg_ref[...] = jnp.zeros_like(dg_ref)

        xv = x_ref[...]
        gv = g_ref[...]
        err = xv * _rms_stats(xv) * gv - t_ref[...]
        part = 0.5 * jnp.sum(jnp.mean(err * err, axis=-1, keepdims=True), axis=0, keepdims=True)
        loss_ref[...] += jnp.broadcast_to(part, loss_ref.shape)
        dx, dg = _rms_bwd(err * (1.0 / D_MODEL), xv, gv)
        dx_ref[...] = dx
        dg_ref[...] += _row_sum(dg)

    return pl.pallas_call(
        body, name="loss_head", grid=(s // ts,),
        in_specs=[_tile_spec(ts, D_MODEL), _tile_spec(ts, D_MODEL), _full_spec((1, D_MODEL))],
        out_specs=[_tile_spec(ts, D_MODEL), _full_spec((1, 128)), _full_spec((1, D_MODEL))],
        out_shape=[jax.ShapeDtypeStruct((s, D_MODEL), F32), jax.ShapeDtypeStruct((1, 128), F32),
                   jax.ShapeDtypeStruct((1, D_MODEL), F32)],
        compiler_params=_params(("arbitrary",)),
    )(x, target, gf)


def _ffn_bwd_call(dx2, p, x1, g2, w_up, w_down, layer, ts):
    s = dx2.shape[0]

    def body(dx2_ref, p_ref, x1_ref, g2_ref, wu_ref, wd_ref, dx1_ref, dp_ref, dg_ref):
        @pl.when(pl.program_id(0) == 0)
        def _():
            dg_ref[...] = jnp.zeros_like(dg_ref)

        dx2v = dx2_ref[...]
        dyb = dx2v.astype(BF)
        dh2 = jnp.zeros((ts, D_MODEL), F32)
        for q in range(N_QUARTERS):
            cols = slice(q * Q_FF, (q + 1) * Q_FF)
            df = _dot_nt(dyb, wd_ref[cols, :])
            dp = (df * (2.0 * jnp.maximum(p_ref[:, cols].astype(F32), 0.0))).astype(BF)
            dp_ref[:, cols] = dp
            dh2 = dh2 + _dot_nt(dp, wu_ref[q])
        dx, dg = _rms_bwd(dh2, x1_ref[...], g2_ref[...])
        dx1_ref[...] = dx2v + dx
        dg_ref[...] += _row_sum(dg)

    return pl.pallas_call(
        body, name="ffn_bwd", grid=(s // ts,),
        in_specs=[_tile_spec(ts, D_MODEL), _tile_spec(ts, D_FF), _tile_spec(ts, D_MODEL),
                  _full_spec((1, D_MODEL)),
                  pl.BlockSpec((None, N_QUARTERS, D_MODEL, Q_FF), lambda i: (layer, 0, 0, 0)),
                  pl.BlockSpec((None, D_FF, D_MODEL), lambda i: (layer, 0, 0))],
        out_specs=[_tile_spec(ts, D_MODEL), _tile_spec(ts, D_FF), _full_spec((1, D_MODEL))],
        out_shape=[jax.ShapeDtypeStruct((s, D_MODEL), F32), jax.ShapeDtypeStruct((s, D_FF), BF),
                   jax.ShapeDtypeStruct((1, D_MODEL), F32)],
        compiler_params=_params(("arbitrary",)),
    )(dx2, p, x1, g2, w_up, w_down)


def _merge_bwd_call(dx1, proj, ya, yb, w_ba, w_bb, w_out, layer, ts):
    s = dx1.shape[0]

    def body(dx1_ref, ga0, ga1, gb0, gb1, ya_ref, yb_ref, wa_ref, wb_ref, wo_ref,
             dya_ref, dyb_ref, dgate_ref, dyap_ref, dybp_ref):
        dm = _dot_nt(dx1_ref[...].astype(BF), wo_ref[...])
        sa = jax.nn.sigmoid(jnp.concatenate([ga0[...], ga1[...]], axis=1).astype(F32))
        sb = jax.nn.sigmoid(jnp.concatenate([gb0[...], gb1[...]], axis=1).astype(F32))
        dya = (dm * sa).astype(BF)
        dyb = (dm * sb).astype(BF)
        dya_ref[...] = dya
        dyb_ref[...] = dyb
        dgate_ref[:, :D_MODEL] = (dm * ya_ref[...].astype(F32) * sa * (1.0 - sa)).astype(BF)
        dgate_ref[:, D_MODEL:] = (dm * yb_ref[...].astype(F32) * sb * (1.0 - sb)).astype(BF)
        dyap_ref[...] = _dot_nt(dya, wa_ref[...]).astype(BF)
        dybp_ref[...] = _dot_nt(dyb, wb_ref[...]).astype(BF)

    act = jax.ShapeDtypeStruct((s, D_MODEL), BF)
    return pl.pallas_call(
        body, name="merge_bwd", grid=(s // ts,),
        in_specs=[_tile_spec(ts, D_MODEL)] + _gate_specs(ts) + [
            _tile_spec(ts, D_MODEL), _tile_spec(ts, D_MODEL),
            _layer_spec(w_ba, layer), _layer_spec(w_bb, layer), _layer_spec(w_out, layer)],
        out_specs=[_tile_spec(ts, D_MODEL), _tile_spec(ts, D_MODEL), _tile_spec(ts, 2 * D_MODEL),
                   _tile_spec(ts, D_RNN), _tile_spec(ts, D_SGU)],
        out_shape=[act, act, jax.ShapeDtypeStruct((s, 2 * D_MODEL), BF),
                   jax.ShapeDtypeStruct((s, D_RNN), BF), jax.ShapeDtypeStruct((s, D_SGU), BF)],
        compiler_params=_params(("parallel",)),
    )(dx1, proj, proj, proj, proj, ya, yb, w_ba, w_bb, w_out)


def _sgu_bwd_call(dyb_pre, proj, wm, bsb, mask, lg, lb, ts):
    s = proj.shape[0]

    def body(dy_ref, uv_ref, wm_ref, bsb_ref, mask_ref, lg_ref, lb_ref,
             duv_ref, dws_ref, dbs_ref, dlg_ref, dlb_ref, dm_sc):
        step = pl.program_id(0)

        @pl.when(step == 0)
        def _():
            dws_ref[...] = jnp.zeros_like(dws_ref)
            dlg_ref[...] = jnp.zeros_like(dlg_ref)
            dlb_ref[...] = jnp.zeros_like(dlb_ref)
            dm_sc[...] = jnp.zeros_like(dm_sc)

        gu, dgu_du = _gelu_and_grad(uv_ref[:, :D_SGU].astype(F32))
        gv, dgv_dv = _gelu_and_grad(uv_ref[:, D_SGU:2 * D_SGU].astype(F32))
        nh, rstd = _layernorm_fwd(gv)
        lgv = lg_ref[...]
        vn = (nh * lgv + lb_ref[...]).astype(BF)
        dy = dy_ref[...].astype(F32)
        du = dy * _sgu_mix(vn, wm_ref, bsb_ref, ts) * dgu_du
        dmix = dy * gu
        dmix_bf = dmix.astype(BF)
        dm_acc = dm_sc[...]
        rows = []
        for blk in range(ts // SGU_BLOCK):
            r0 = blk * SGU_BLOCK
            dm_acc = dm_acc + dmix[r0:r0 + SGU_BLOCK, :]
            cols = []
            for g in range(SGU_GROUPS):
                c0 = g * SGU_BLOCK
                dmg = dmix_bf[r0:r0 + SGU_BLOCK, c0:c0 + SGU_BLOCK]
                cols.append(_dot_tn(wm_ref[g], dmg))
                dws_ref[g] += mask_ref[...] * _dot_nt(dmg, vn[r0:r0 + SGU_BLOCK, c0:c0 + SGU_BLOCK])
            rows.append(jnp.concatenate(cols, axis=1))
        dm_sc[...] = dm_acc
        dvn = jnp.concatenate(rows, axis=0)
        dlg_ref[...] += _row_sum(dvn * nh)
        dlb_ref[...] += _row_sum(dvn)
        dnh = dvn * lgv
        dgv = rstd * (dnh - jnp.mean(dnh, axis=-1, keepdims=True)
                      - nh * jnp.mean(dnh * nh, axis=-1, keepdims=True))
        duv_ref[:, :D_SGU] = du.astype(BF)
        duv_ref[:, D_SGU:] = (dgv * dgv_dv).astype(BF)

        @pl.when(step == pl.num_programs(0) - 1)
        def _():
            for g in range(SGU_GROUPS):
                dbs_ref[:, g:g + 1] = jnp.sum(
                    dm_acc[:, g * SGU_BLOCK:(g + 1) * SGU_BLOCK], axis=1, keepdims=True)

    sw = (SGU_GROUPS, SGU_BLOCK, SGU_BLOCK)
    return pl.pallas_call(
        body, name="sgu_bwd", grid=(s // ts,),
        in_specs=[_tile_spec(ts, D_SGU), _tile_spec(ts, 2 * D_RNN, 1), _full_spec(sw), _full_spec(sw),
                  _full_spec((SGU_BLOCK, SGU_BLOCK)), _full_spec((1, D_SGU)), _full_spec((1, D_SGU))],
        out_specs=[_tile_spec(ts, 2 * D_SGU), _full_spec(sw), _full_spec((SGU_BLOCK, SGU_GROUPS)),
                   _full_spec((1, D_SGU)), _full_spec((1, D_SGU))],
        out_shape=[jax.ShapeDtypeStruct((s, 2 * D_SGU), BF), jax.ShapeDtypeStruct(sw, F32),
                   jax.ShapeDtypeStruct((SGU_BLOCK, SGU_GROUPS), F32),
                   jax.ShapeDtypeStruct((1, D_SGU), F32), jax.ShapeDtypeStruct((1, D_SGU), F32)],
        scratch_shapes=[pltpu.VMEM((SGU_BLOCK, D_SGU), F32)],
        compiler_params=_params(("arbitrary",)),
    )(dyb_pre, proj, wm, bsb, mask, lg, lb)


_ROW_DBA, _ROW_DBX, _ROW_DSP, _ROW_DCB, _ROW_DCW = 0, 1, 2, 3, 4
_PREV_ROWS = 16


def _rnn_bwd_call(dya_pre, proj, hr, wa, wx, ba, bx, sp, cw, cb, ts):
    s = proj.shape[0]
    nt = s // ts
    per = ts // _PREV_ROWS

    def tile(i):
        return nt - 1 - i

    def prev(i):
        return jnp.maximum(tile(i) * per - 1, 0)

    def body(dy_ref, xg_ref, xgp_ref, hr_ref, hrp_ref, wa_ref, wx_ref, ba_ref, bx_ref, sp_ref,
             cw_ref, cb_ref, dxg_ref, dwa_ref, dwx_ref, vec_ref,
             lam_carry, a_first, dxr_head, al_sc, bl_sc, lam_sc):
        step = pl.program_id(0)

        @pl.when(step == 0)
        def _():
            dwa_ref[...] = jnp.zeros_like(dwa_ref)
            dwx_ref[...] = jnp.zeros_like(dwx_ref)
            vec_ref[...] = jnp.zeros_like(vec_ref)
            lam_carry[...] = jnp.zeros_like(lam_carry)
            a_first[...] = jnp.zeros_like(a_first)
            dxr_head[...] = jnp.zeros_like(dxr_head)

        has_prev = (step < nt - 1).astype(F32)
        x = xg_ref[:, :D_RNN].astype(F32)
        g = xg_ref[:, D_RNN:].astype(F32)
        x_tail = xgp_ref[_PREV_ROWS - SUBLANES:, :D_RNN].astype(F32) * has_prev
        h_tail = hrp_ref[_PREV_ROWS - SUBLANES:, :].astype(F32) * has_prev
        xr = _conv_fwd(x, x_tail, cw_ref, cb_ref)
        r, i, a, nrm = _lru_gates(xr, wa_ref, wx_ref, ba_ref, bx_ref, sp_ref)
        h = hr_ref[...].astype(F32)
        dy = dy_ref[...].astype(F32)
        gg, dgg = _gelu_and_grad(g)

        coef = _shift_up(a, jnp.broadcast_to(a_first[...], (SUBLANES, D_RNN)), 1)
        lam_carry[...] = _linear_scan(coef, dy * gg, lam_carry[...], al_sc, bl_sc, lam_sc, True)
        a_first[...] = a[0:1, :]
        lam = lam_sc[...]

        da = lam * _shift_down(h, h_tail, 1)
        dnrm = lam * (i * xr)
        di = lam * nrm * xr
        dlog_a = da * a - dnrm * (a * a) / nrm
        spv = sp_ref[...]
        dza = (dlog_a * (-LRU_C * spv)) * (r * (1.0 - r))
        dzx = di * (i * (1.0 - i))
        vec_ref[_ROW_DSP:_ROW_DSP + 1, :] += _row_sum(dlog_a * (-LRU_C * r))
        vec_ref[_ROW_DBA:_ROW_DBA + 1, :] += _row_sum(dza)
        vec_ref[_ROW_DBX:_ROW_DBX + 1, :] += _row_sum(dzx)
        xb = xr.astype(BF)
        dza_bf = dza.astype(BF)
        dzx_bf = dzx.astype(BF)
        for grp in range(N_LRU_GROUPS):
            cols = slice(grp * LRU_GROUP, (grp + 1) * LRU_GROUP)
            dwa_ref[grp] += _dot_tn(xb[:, cols], dza_bf[:, cols])
            dwx_ref[grp] += _dot_tn(xb[:, cols], dzx_bf[:, cols])
        dxr = (lam * nrm * i + _group_dot(dza_bf, wa_ref, _dot_nt) + _group_dot(dzx_bf, wx_ref, _dot_nt))

        vec_ref[_ROW_DCB:_ROW_DCB + 1, :] += _row_sum(dxr)
        head = dxr_head[...]
        dx = cw_ref[CONV_WIDTH - 1:CONV_WIDTH, :] * dxr
        vec_ref[_ROW_DCW + 3:_ROW_DCW + 4, :] += _row_sum(dxr * x)
        for sft in range(1, CONV_WIDTH):
            k = CONV_WIDTH - 1 - sft
            dx = dx + cw_ref[k:k + 1, :] * _shift_up(dxr, head, sft)
            vec_ref[_ROW_DCW + k:_ROW_DCW + k + 1, :] += _row_sum(dxr * _shift_down(x, x_tail, sft))
        dxr_head[...] = dxr[0:SUBLANES, :]
        dxg_ref[:, :D_RNN] = dx.astype(BF)
        dxg_ref[:, D_RNN:] = (dy * h * dgg).astype(BF)

    gw = (N_LRU_GROUPS, LRU_GROUP, LRU_GROUP)
    rev = lambda width: pl.BlockSpec((ts, width), lambda i: (tile(i), 0))
    return pl.pallas_call(
        body, name="rnn_bwd", grid=(nt,),
        in_specs=[rev(D_RNN), rev(2 * D_RNN),
                  pl.BlockSpec((_PREV_ROWS, 2 * D_RNN), lambda i: (prev(i), 0)),
                  rev(D_RNN),
                  pl.BlockSpec((_PREV_ROWS, D_RNN), lambda i: (prev(i), 0)),
                  _full_spec(gw), _full_spec(gw),
                  _full_spec((1, D_RNN)), _full_spec((1, D_RNN)), _full_spec((1, D_RNN)),
                  _full_spec((CONV_WIDTH, D_RNN)), _full_spec((1, D_RNN))],
        out_specs=[rev(2 * D_RNN), _full_spec(gw), _full_spec(gw), _full_spec((SUBLANES, D_RNN))],
        out_shape=[jax.ShapeDtypeStruct((s, 2 * D_RNN), BF), jax.ShapeDtypeStruct(gw, F32),
                   jax.ShapeDtypeStruct(gw, F32), jax.ShapeDtypeStruct((SUBLANES, D_RNN), F32)],
        scratch_shapes=[pltpu.VMEM((1, D_RNN), F32), pltpu.VMEM((1, D_RNN), F32),
                        pltpu.VMEM((SUBLANES, D_RNN), F32),
                        pltpu.VMEM((ts, D_RNN), F32), pltpu.VMEM((ts, D_RNN), F32),
                        pltpu.VMEM((ts, D_RNN), F32)],
        compiler_params=_params(("arbitrary",)),
    )(dya_pre, proj, proj, hr, hr, wa, wx, ba, bx, sp, cw, cb)


def _inproj_bwd_call(dxg, duv, dgate, dx1, x, g1, w_in, layer, ts):
    s = x.shape[0]

    def body(dxg_ref, duv_ref, dgt_ref, dx1_ref, x_ref, g_ref, w_ref, dx_ref, dproj_ref, dg_ref):
        @pl.when(pl.program_id(0) == 0)
        def _():
            dg_ref[...] = jnp.zeros_like(dg_ref)

        dproj = jnp.concatenate([dxg_ref[...], duv_ref[...], dgt_ref[...]], axis=1)
        dproj_ref[...] = dproj
        dh = jnp.zeros((ts, D_MODEL), F32)
        for q in range(N_QUARTERS):
            dh = dh + _dot_nt(dproj[:, q * Q_IN:(q + 1) * Q_IN], w_ref[q])
        dx, dg = _rms_bwd(dh, x_ref[...], g_ref[...])
        dx_ref[...] = dx1_ref[...] + dx
        dg_ref[...] += _row_sum(dg)

    return pl.pallas_call(
        body, name="inproj_bwd", grid=(s // ts,),
        in_specs=[_tile_spec(ts, 2 * D_RNN), _tile_spec(ts, 2 * D_SGU), _tile_spec(ts, 2 * D_MODEL),
                  _tile_spec(ts, D_MODEL), _tile_spec(ts, D_MODEL), _full_spec((1, D_MODEL)),
                  pl.BlockSpec((None, N_QUARTERS, D_MODEL, Q_IN), lambda i: (layer, 0, 0, 0))],
        out_specs=[_tile_spec(ts, D_MODEL), _tile_spec(ts, D_IN), _full_spec((1, D_MODEL))],
        out_shape=[jax.ShapeDtypeStruct((s, D_MODEL), F32), jax.ShapeDtypeStruct((s, D_IN), BF),
                   jax.ShapeDtypeStruct((1, D_MODEL), F32)],
        compiler_params=_params(("arbitrary",)),
    )(dxg, duv, dgate, dx1, x, g1, w_in)


def _relu_sq(p):
    return jnp.square(jnp.maximum(p.astype(F32), 0.0))


def _wgrad_call(a, b, tm, tn, tk, col_blocked, name, layer, acc=None, a_fn=None):
    s, m = a.shape
    n = b.shape[1]

    def body(a_ref, b_ref, *rest):
        o_ref = rest[-1]
        av = a_ref[...]
        if a_fn is not None:
            av = a_fn(av)
        prod = _dot_tn(av.astype(BF), b_ref[...].astype(BF))

        @pl.when(pl.program_id(2) == 0)
        def _():
            o_ref[...] = prod

        @pl.when(pl.program_id(2) > 0)
        def _():
            o_ref[...] += prod

    if col_blocked:
        out_spec = pl.BlockSpec((None, None, tm, tn), lambda i, j, k: (layer, j, 0, 0))
        out_shape = jax.ShapeDtypeStruct((DEPTH, n // tn, m, tn), F32)
    else:
        out_spec = pl.BlockSpec((None, tm, tn), lambda i, j, k: (layer, i, j))
        out_shape = jax.ShapeDtypeStruct((DEPTH, m, n), F32)
    in_specs = [pl.BlockSpec((tk, tm), lambda i, j, k: (k, i)),
                pl.BlockSpec((tk, tn), lambda i, j, k: (k, j))]
    operands = [a, b]
    aliases = {}
    if acc is not None:
        in_specs.append(pl.BlockSpec(memory_space=pl.ANY))
        operands.append(acc)
        aliases = {2: 0}
    return pl.pallas_call(
        body, name=name, grid=(m // tm, n // tn, s // tk),
        in_specs=in_specs, out_specs=out_spec, out_shape=out_shape,
        input_output_aliases=aliases,
        compiler_params=_params(("parallel", "parallel", "arbitrary")),
    )(*operands)


BIG = ("w_in", "w_up", "w_down", "w_branch_a", "w_branch_b", "w_out")


def _block_diag(w):
    w4 = w.reshape(N_LRU_GROUPS, HEADS_PER_GROUP, RNN_HEAD_DIM, RNN_HEAD_DIM)
    eye = jnp.eye(HEADS_PER_GROUP, dtype=w.dtype)
    return jnp.einsum("gjio,jk->gjiko", w4, eye).reshape(N_LRU_GROUPS, LRU_GROUP, LRU_GROUP)


def _block_diag_extract(d):
    d5 = d.reshape(N_LRU_GROUPS, HEADS_PER_GROUP, RNN_HEAD_DIM, HEADS_PER_GROUP, RNN_HEAD_DIM)
    blocks = [d5[:, j, :, j, :] for j in range(HEADS_PER_GROUP)]
    return jnp.stack(blocks, axis=1).reshape(RNN_HEADS, RNN_HEAD_DIM, RNN_HEAD_DIM)


def _sgu_mask():
    chunk = jnp.arange(SGU_BLOCK) // CHUNK
    return (chunk[:, None] >= chunk[None, :]).astype(F32)


def _layer_small(sm, l):
    row = lambda v: v.reshape(1, -1)
    return dict(
        g1=row(sm["norm_mix_g"][l]), g2=row(sm["norm_ffn_g"][l]),
        wa=_block_diag(sm["lru_w_a"][l]).astype(BF), wx=_block_diag(sm["lru_w_x"][l]).astype(BF),
        ba=row(sm["lru_b_a"][l]), bx=row(sm["lru_b_x"][l]),
        sp=row(jax.nn.softplus(-sm["lru_lambda"][l])),
        cw=sm["conv_w"][l], cb=row(sm["conv_b"][l]),
        wm=(sm["sgu_w_s"][l] * _sgu_mask()).astype(BF),
        bsb=jnp.broadcast_to(sm["sgu_b_s"][l][:, :, None], (SGU_GROUPS, SGU_BLOCK, SGU_BLOCK)),
        lg=row(sm["sgu_ln_g"][l]), lb=row(sm["sgu_ln_b"][l]),
    )


def _local_step(x, target, big, sm, ts):
    saved = []
    for l in range(DEPTH):
        p = _layer_small(sm, l)
        h = _norm_call(x, p["g1"], ts)
        proj = _inproj_call(h, big["w_in"], l, ts)
        hr, ya_pre = _rnn_fwd_call(proj, p["wa"], p["wx"], p["ba"], p["bx"], p["sp"], p["cw"], p["cb"], ts)
        yb_pre = _sgu_fwd_call(proj, p["wm"], p["bsb"], p["lg"], p["lb"], ts)
        x1, ya, yb, merged, h2 = _merge_call(x, proj, ya_pre, yb_pre, big["w_branch_a"], big["w_branch_b"],
                                             big["w_out"], p["g2"], l, ts)
        x2, pre = _ffn_call(x1, h2, big["w_up"], big["w_down"], l, ts)
        saved.append(dict(p=p, x=x, h=h, proj=proj, hr=hr, ya_pre=ya_pre, yb_pre=yb_pre, x1=x1, ya=ya, yb=yb,
                          merged=merged, h2=h2, pre=pre))
        x = x2
    dx, loss, dgf = _loss_call(x, target, sm["final_norm_g"].reshape(1, -1), ts)

    gb = {k: None for k in BIG}
    gs = {k: [None] * DEPTH for k in (
        "norm_mix_g", "conv_w", "conv_b", "lru_w_a", "lru_b_a", "lru_w_x", "lru_b_x", "lru_lambda",
        "sgu_ln_g", "sgu_ln_b", "sgu_w_s", "sgu_b_s", "norm_ffn_g")}
    mask = _sgu_mask()
    for l in reversed(range(DEPTH)):
        sv = saved[l]
        p = sv["p"]
        dx1, dpre, dg2 = _ffn_bwd_call(dx, sv["pre"], sv["x1"], p["g2"], big["w_up"], big["w_down"], l, ts)
        gb["w_down"] = _wgrad_call(sv["pre"], dx, Q_FF, D_MODEL, ts, False, "wgrad_down", l, gb["w_down"],
                                   a_fn=_relu_sq)
        gb["w_up"] = _wgrad_call(sv["h2"], dpre, D_MODEL, Q_FF, ts, True, "wgrad_up", l, gb["w_up"])
        dya, dyb, dgate, dya_pre, dyb_pre = _merge_bwd_call(
            dx1, sv["proj"], sv["ya"], sv["yb"], big["w_branch_a"], big["w_branch_b"], big["w_out"], l, ts)
        gb["w_out"] = _wgrad_call(sv["merged"], dx1, D_MODEL, D_MODEL, ts, False, "wgrad_out", l, gb["w_out"])
        gb["w_branch_a"] = _wgrad_call(sv["ya_pre"], dya, D_RNN, D_MODEL, ts, False, "wgrad_branch_a", l,
                                       gb["w_branch_a"])
        gb["w_branch_b"] = _wgrad_call(sv["yb_pre"], dyb, D_SGU, D_MODEL, ts, False, "wgrad_branch_b", l,
                                       gb["w_branch_b"])
        duv, dws, dbs, dlg, dlb = _sgu_bwd_call(dyb_pre, sv["proj"], p["wm"], p["bsb"], mask, p["lg"], p["lb"], ts)
        dxg, dwa, dwx, vec = _rnn_bwd_call(dya_pre, sv["proj"], sv["hr"], p["wa"], p["wx"], p["ba"], p["bx"],
                                           p["sp"], p["cw"], p["cb"], ts)
        dx, dproj, dg1 = _inproj_bwd_call(dxg, duv, dgate, dx1, sv["x"], p["g1"], big["w_in"], l, ts)
        gb["w_in"] = _wgrad_call(sv["h"], dproj, D_MODEL, Q_IN, ts, True, "wgrad_in", l, gb["w_in"])

        gs["norm_mix_g"][l] = dg1[0]
        gs["norm_ffn_g"][l] = dg2[0]
        gs["conv_w"][l] = vec[_ROW_DCW:_ROW_DCW + CONV_WIDTH]
        gs["conv_b"][l] = vec[_ROW_DCB]
        gs["lru_w_a"][l] = _block_diag_extract(dwa)
        gs["lru_w_x"][l] = _block_diag_extract(dwx)
        gs["lru_b_a"][l] = vec[_ROW_DBA].reshape(RNN_HEADS, RNN_HEAD_DIM)
        gs["lru_b_x"][l] = vec[_ROW_DBX].reshape(RNN_HEADS, RNN_HEAD_DIM)
        gs["lru_lambda"][l] = -vec[_ROW_DSP] * jax.nn.sigmoid(-sm["lru_lambda"][l])
        gs["sgu_ln_g"][l] = dlg[0]
        gs["sgu_ln_b"][l] = dlb[0]
        gs["sgu_w_s"][l] = dws
        gs["sgu_b_s"][l] = dbs.T
    gs = {k: jnp.stack(v) for k, v in gs.items()}
    gs["final_norm_g"] = dgf[0]
    return loss, dx, gb, gs


EW_BLOCK_ELEMS = 384 * 1024


def _row_block(rows, cols):
    for br in range(min(rows, EW_BLOCK_ELEMS // cols), 0, -1):
        if rows % br == 0 and br % 16 == 0:
            return br
    return rows


def _ew_call(fn, name, operands, outputs, slabs=1, sel=None):
    rows, cols = outputs[0][0].shape[2:]
    br = _row_block(rows, cols)
    n_in = len(operands)

    def pick(tok, g, s):
        if tok == "g":
            return g
        if isinstance(tok, tuple):
            return s[tok[1]]
        return tok

    def spec(idx):
        return pl.BlockSpec((None, None, br, cols),
                            lambda g, i, s, idx=idx: (pick(idx[0], g, s), pick(idx[1], g, s), i, 0))

    def body(sel_ref, *refs):
        outs = fn(*[r[...] for r in refs[:n_in]])
        for o_ref, o in zip(refs[n_in:], outs):
            o_ref[...] = o.astype(o_ref.dtype)

    if sel is None:
        sel = jnp.zeros((1,), jnp.int32)
    return pl.pallas_call(
        body, name=name, out_shape=[s for s, _ in outputs],
        grid_spec=pltpu.PrefetchScalarGridSpec(
            num_scalar_prefetch=1, grid=(slabs, rows // br),
            in_specs=[spec(idx) for _, idx in operands],
            out_specs=[spec(idx) for _, idx in outputs]),
        compiler_params=_params(("parallel", "parallel")),
    )(sel, *[a for a, _ in operands])


def _as4(a):
    return a.reshape((1,) * (4 - a.ndim) + a.shape)


def _adamw(w, g, m, v):
    m = ADAM_B1 * m + (1.0 - ADAM_B1) * g
    v = ADAM_B2 * v + (1.0 - ADAM_B2) * jnp.square(g)
    m_hat = m / (1.0 - ADAM_B1 ** ADAM_STEP)
    v_hat = v / (1.0 - ADAM_B2 ** ADAM_STEP)
    delta = -ADAM_LR * (m_hat / (jnp.sqrt(v_hat) + ADAM_EPS) + ADAM_WD * w)
    return delta, m, v


def _small_adamw_call(ws, gs, ms, vs):
    n = len(ws)

    def body(*refs):
        for k in range(n):
            w, g, m, v = (refs[j * n + k][...] for j in range(4))
            outs = _adamw(w, g, m, v)
            for j in range(3):
                refs[(4 + j) * n + k][...] = outs[j]

    shapes = [jax.ShapeDtypeStruct(w.shape, F32) for w in ws]
    outs = pl.pallas_call(
        body, name="adamw_small", out_shape=shapes * 3,
        in_specs=[pl.BlockSpec(memory_space=pltpu.VMEM)] * (4 * n),
        out_specs=[pl.BlockSpec(memory_space=pltpu.VMEM)] * (3 * n),
        compiler_params=_params(),
    )(*ws, *gs, *ms, *vs)
    return outs[:n], outs[n:2 * n], outs[2 * n:]


ANY = pl.BlockSpec(memory_space=pl.ANY)


def _place():
    x, y, c = lax.axis_index("x"), lax.axis_index("y"), lax.axis_index("c")
    chips = [(1 - x, y), (x, 1 - y), (1 - x, 1 - y)]
    return x, y, c, chips


def _remote(src, dst, send_sem, recv_sem, to):
    return pltpu.make_async_remote_copy(src_ref=src, dst_ref=dst, send_sem=send_sem, recv_sem=recv_sem,
                                        device_id=to, device_id_type=MESH)


def _gather_call(bufs):
    n = len(bufs)

    def body(*refs):
        out = refs[n:2 * n]
        send_sems, recv_sems = refs[2 * n:]
        x, y, c, chips = _place()
        me_q = 2 * x + y
        sibling = (x, y, 1 - c)
        first = []
        for w in range(n):
            for j, chip in enumerate(chips):
                mine = out[w].at[c, me_q]
                first.append(_remote(mine, mine, send_sems.at[w * 3 + j], recv_sems.at[w * 3 + j], (*chip, c)))
        for cp in first:
            cp.start()
        passed = []
        for w in range(n):
            for j, (qx, qy) in enumerate(chips):
                landed = out[w].at[c, 2 * qx + qy]
                k = w * 3 + j
                _remote(landed, landed, send_sems.at[k], recv_sems.at[k], (qx, qy, c)).wait_recv()
                cp = _remote(landed, landed, send_sems.at[3 * n + k], recv_sems.at[3 * n + k], sibling)
                cp.start()
                passed.append(cp)
        for w in range(n):
            for j, (qx, qy) in enumerate(chips):
                landed = out[w].at[1 - c, 2 * qx + qy]
                k = 3 * n + w * 3 + j
                _remote(landed, landed, send_sems.at[k], recv_sems.at[k], sibling).wait_recv()
        for cp in first + passed:
            cp.wait_send()

    return pl.pallas_call(
        body, name="gather_weights",
        out_shape=[jax.ShapeDtypeStruct(a.shape, a.dtype) for a in bufs],
        in_specs=[ANY] * n, out_specs=[ANY] * n,
        input_output_aliases={w: w for w in range(n)},
        scratch_shapes=[pltpu.SemaphoreType.DMA((6 * n,)), pltpu.SemaphoreType.DMA((6 * n,))],
        compiler_params=_params(vmem=False, has_side_effects=True),
    )(*bufs)


def _sibling_send_call(items):
    n = len(items)

    def body(*refs):
        src, out = refs[:n], refs[n:2 * n]
        send_sems, recv_sems = refs[2 * n:]
        x, y, c, _ = _place()
        copies = [_remote(src[w], out[w], send_sems.at[w], recv_sems.at[w], (x, y, 1 - c)) for w in range(n)]
        for cp in copies:
            cp.start()
        for cp in copies:
            cp.wait()

    return pl.pallas_call(
        body, name="grads_to_sibling",
        out_shape=[jax.ShapeDtypeStruct(a.shape, a.dtype) for a in items],
        in_specs=[ANY] * n, out_specs=[ANY] * n,
        scratch_shapes=[pltpu.SemaphoreType.DMA((n,)), pltpu.SemaphoreType.DMA((n,))],
        compiler_params=_params(vmem=False, has_side_effects=True),
    )(*items)


def _quarter_exchange_call(items):
    n = len(items)

    def body(*refs):
        src, out = refs[:n], refs[n:2 * n]
        send_sems, recv_sems = refs[2 * n:]
        x, y, c, chips = _place()
        sends = []
        for w in range(n):
            for j, (qx, qy) in enumerate(chips):
                k = w * 3 + j
                sends.append(_remote(src[w].at[2 * qx + qy], out[w].at[j], send_sems.at[k], recv_sems.at[k],
                                     (qx, qy, c)))
        for cp in sends:
            cp.start()
        for w in range(n):
            for j, (qx, qy) in enumerate(chips):
                k = w * 3 + j
                landed = out[w].at[j]
                _remote(landed, landed, send_sems.at[k], recv_sems.at[k], (qx, qy, c)).wait_recv()
        for cp in sends:
            cp.wait_send()

    return pl.pallas_call(
        body, name="grads_to_owner",
        out_shape=[jax.ShapeDtypeStruct((3,) + a.shape[1:], a.dtype) for a in items],
        in_specs=[ANY] * n, out_specs=[ANY] * n,
        scratch_shapes=[pltpu.SemaphoreType.DMA((3 * n,)), pltpu.SemaphoreType.DMA((3 * n,))],
        compiler_params=_params(vmem=False, has_side_effects=True),
    )(*items)


def _layer_swap_call(bufs):
    n = len(bufs)

    def body(*refs):
        out = refs[n:2 * n]
        send_sems, recv_sems = refs[2 * n:]
        x, y, c, _ = _place()
        sends = [_remote(out[w].at[c], out[w].at[c], send_sems.at[w], recv_sems.at[w], (x, y, 1 - c))
                 for w in range(n)]
        for cp in sends:
            cp.start()
        for w in range(n):
            landed = out[w].at[1 - c]
            _remote(landed, landed, send_sems.at[w], recv_sems.at[w], (x, y, 1 - c)).wait_recv()
        for cp in sends:
            cp.wait_send()

    return pl.pallas_call(
        body, name="grads_swap_layers",
        out_shape=[jax.ShapeDtypeStruct(a.shape, a.dtype) for a in bufs],
        in_specs=[ANY] * n, out_specs=[ANY] * n,
        input_output_aliases={w: w for w in range(n)},
        scratch_shapes=[pltpu.SemaphoreType.DMA((n,)), pltpu.SemaphoreType.DMA((n,))],
        compiler_params=_params(vmem=False, has_side_effects=True),
    )(*bufs)


N_DEVICES = 8
SMALL_ROWS = 616


def _small_allreduce_call(buf):
    def body(in_ref, out_ref, recv_ref, red_ref, send_sems, recv_sems):
        x, y, c, _ = _place()
        me = 4 * x + 2 * y + c

        def peer(k):
            return (x ^ ((k >> 2) & 1), y ^ ((k >> 1) & 1), c ^ (k & 1))

        scatter = [_remote(in_ref.at[me ^ k], recv_ref.at[me], send_sems.at[k - 1], recv_sems.at[k - 1], peer(k))
                   for k in range(1, N_DEVICES)]
        for cp in scatter:
            cp.start()
        recv_ref[me] = in_ref[me]
        for k in range(1, N_DEVICES):
            landed = recv_ref.at[me ^ k]
            _remote(landed, landed, send_sems.at[k - 1], recv_sems.at[k - 1], peer(k)).wait_recv()
        total = recv_ref[0]
        for j in range(1, N_DEVICES):
            total = total + recv_ref[j]
        red_ref[...] = total
        out_ref[me] = total
        spread = [_remote(red_ref, out_ref.at[me], send_sems.at[6 + k], recv_sems.at[6 + k], peer(k))
                  for k in range(1, N_DEVICES)]
        for cp in spread:
            cp.start()
        for k in range(1, N_DEVICES):
            landed = out_ref.at[me ^ k]
            _remote(landed, landed, send_sems.at[6 + k], recv_sems.at[6 + k], peer(k)).wait_recv()
        for cp in scatter + spread:
            cp.wait_send()

    shape = (N_DEVICES, SMALL_ROWS, 128)
    return pl.pallas_call(
        body, name="allreduce_small",
        out_shape=jax.ShapeDtypeStruct(shape, F32),
        in_specs=[pl.BlockSpec(memory_space=pltpu.VMEM)],
        out_specs=pl.BlockSpec(memory_space=pltpu.VMEM),
        scratch_shapes=[pltpu.VMEM(shape, F32), pltpu.VMEM(shape[1:], F32),
                        pltpu.SemaphoreType.DMA((2 * (N_DEVICES - 1),)),
                        pltpu.SemaphoreType.DMA((2 * (N_DEVICES - 1),))],
        compiler_params=_params(has_side_effects=True),
    )(buf)


SMALL = ("norm_mix_g", "conv_w", "conv_b", "lru_w_a", "lru_b_a", "lru_w_x", "lru_b_x", "lru_lambda",
         "sgu_ln_g", "sgu_ln_b", "sgu_w_s", "sgu_b_s", "norm_ffn_g", "final_norm_g")
WEIGHTS = ("norm_mix_g", "w_in", "conv_w", "conv_b", "lru_w_a", "lru_b_a", "lru_w_x", "lru_b_x", "lru_lambda",
           "sgu_ln_g", "sgu_ln_b", "sgu_w_s", "sgu_b_s", "w_branch_a", "w_branch_b", "w_out", "norm_ffn_g",
           "w_up", "w_down", "final_norm_g")
PACK_ALIGN = SUBLANES * 128


def _pack_small(gs):
    parts = []
    for k in SMALL:
        flat = gs[k].reshape(-1)
        parts.append(jnp.pad(flat, (0, -flat.size % PACK_ALIGN)))
    flat = jnp.concatenate(parts)
    flat = jnp.pad(flat, (0, N_DEVICES * SMALL_ROWS * 128 - flat.size))
    return flat.reshape(N_DEVICES, SMALL_ROWS, 128)


def _unpack_small(buf, like):
    flat = buf.reshape(-1)
    out, off = {}, 0
    for k in SMALL:
        size = like[k].size
        out[k] = flat[off:off + size].reshape(like[k].shape)
        off += size + (-size % PACK_ALIGN)
    return out


def _as_rows(a):
    return a.reshape(-1, a.shape[-1])


def kernel(x, norm_mix_g, w_in, conv_w, conv_b, lru_w_a, lru_b_a, lru_w_x, lru_b_x, lru_lambda, sgu_ln_g, sgu_ln_b, sgu_w_s, sgu_b_s, w_branch_a, w_branch_b, w_out, norm_ffn_g, w_up, w_down, final_norm_g, loss_target, m_norm_mix_g, m_w_in, m_conv_w, m_conv_b, m_lru_w_a, m_lru_b_a, m_lru_w_x, m_lru_b_x, m_lru_lambda, m_sgu_ln_g, m_sgu_ln_b, m_sgu_w_s, m_sgu_b_s, m_w_branch_a, m_w_branch_b, m_w_out, m_norm_ffn_g, m_w_up, m_w_down, m_final_norm_g, v_norm_mix_g, v_w_in, v_conv_w, v_conv_b, v_lru_w_a, v_lru_b_a, v_lru_w_x, v_lru_b_x, v_lru_lambda, v_sgu_ln_g, v_sgu_ln_b, v_sgu_w_s, v_sgu_b_s, v_w_branch_a, v_w_branch_b, v_w_out, v_norm_ffn_g, v_w_up, v_w_down, v_final_norm_g):
    w = dict(norm_mix_g=norm_mix_g, w_in=w_in, conv_w=conv_w, conv_b=conv_b, lru_w_a=lru_w_a, lru_b_a=lru_b_a,
             lru_w_x=lru_w_x, lru_b_x=lru_b_x, lru_lambda=lru_lambda, sgu_ln_g=sgu_ln_g, sgu_ln_b=sgu_ln_b,
             sgu_w_s=sgu_w_s, sgu_b_s=sgu_b_s, w_branch_a=w_branch_a, w_branch_b=w_branch_b, w_out=w_out,
             norm_ffn_g=norm_ffn_g, w_up=w_up, w_down=w_down, final_norm_g=final_norm_g)
    m = dict(norm_mix_g=m_norm_mix_g, w_in=m_w_in, conv_w=m_conv_w, conv_b=m_conv_b, lru_w_a=m_lru_w_a,
             lru_b_a=m_lru_b_a, lru_w_x=m_lru_w_x, lru_b_x=m_lru_b_x, lru_lambda=m_lru_lambda,
             sgu_ln_g=m_sgu_ln_g, sgu_ln_b=m_sgu_ln_b, sgu_w_s=m_sgu_w_s, sgu_b_s=m_sgu_b_s,
             w_branch_a=m_w_branch_a, w_branch_b=m_w_branch_b, w_out=m_w_out, norm_ffn_g=m_norm_ffn_g,
             w_up=m_w_up, w_down=m_w_down, final_norm_g=m_final_norm_g)
    v = dict(norm_mix_g=v_norm_mix_g, w_in=v_w_in, conv_w=v_conv_w, conv_b=v_conv_b, lru_w_a=v_lru_w_a,
             lru_b_a=v_lru_b_a, lru_w_x=v_lru_w_x, lru_b_x=v_lru_b_x, lru_lambda=v_lru_lambda,
             sgu_ln_g=v_sgu_ln_g, sgu_ln_b=v_sgu_ln_b, sgu_w_s=v_sgu_w_s, sgu_b_s=v_sgu_b_s,
             w_branch_a=v_w_branch_a, w_branch_b=v_w_branch_b, w_out=v_w_out, norm_ffn_g=v_norm_ffn_g,
             w_up=v_w_up, w_down=v_w_down, final_norm_g=v_final_norm_g)
    core = lax.axis_index("c")
    chip = 2 * lax.axis_index("x") + lax.axis_index("y")
    sel = jnp.stack([core, 1 - core, chip]).astype(jnp.int32)
    this_core, other_core, this_chip = ("sel", 0), ("sel", 1), ("sel", 2)
    sds = jax.ShapeDtypeStruct

    bufs = []
    for k in BIG:
        _, r, cols = w[k].shape
        bufs.append(_ew_call(lambda a: (a,), "cast_weights", [(w[k].reshape(DEPTH, 1, r, cols), ("g", 0))],
                             [(sds((DEPTH, N_QUARTERS, r, cols), BF), ("g", this_chip))], DEPTH, sel)[0])
    conv_buf = lax.dynamic_update_slice_in_dim(
        jnp.zeros((DEPTH, N_QUARTERS) + conv_w.shape[1:], F32), conv_w[:, None], chip, axis=1)
    gathered = _gather_call(bufs + [conv_buf])
    full = dict(zip(BIG, gathered[:-1]))
    big = {k: (full[k] if k in ("w_in", "w_up") else full[k].reshape(DEPTH, -1, D_MODEL)) for k in BIG}
    sm = {k: w[k] for k in SMALL}
    sm["conv_w"] = gathered[-1].transpose(0, 2, 1, 3).reshape(DEPTH, CONV_WIDTH, D_RNN)

    loss, grad_x, gb, gs = _local_step(x[0], loss_target[0], big, sm, TOKEN_TILE)

    flat = {k: gb[k].reshape(DEPTH, 1, -1, gb[k].shape[-1]) for k in BIG}
    to_sibling = [
        _ew_call(lambda a: (a,), "cast_grads", [(flat[k], (other_core, 0))],
                 [(sds((1, 1) + flat[k].shape[2:], BF), (0, 0))], 1, sel)[0] for k in BIG]
    from_sibling = _sibling_send_call(to_sibling)
    pair = [
        _ew_call(lambda a, b: (a + b.astype(F32),), "pair_sum", [(flat[k], (this_core, 0)), (r, (0, 0))],
                 [(sds(r.shape, BF), (0, 0))], 1, sel)[0].reshape(N_QUARTERS, -1, r.shape[-1])
        for k, r in zip(BIG, from_sibling)]
    landed = _quarter_exchange_call(pair)
    mine = [
        _ew_call(lambda a, b, c, d: (((a.astype(F32) + b.astype(F32)) + c.astype(F32)) + d.astype(F32),),
                 "quarter_sum", [(p[None], (0, this_chip))] + [(r[None], (0, j)) for j in range(3)],
                 [(sds((1, DEPTH) + r.shape[1:], F32), (0, this_core))], 1, sel)[0][0]
        for p, r in zip(pair, landed)]
    grads_big = dict(zip(BIG, _layer_swap_call(mine)))
    grads_big = {k: g.reshape(w[k].shape) for k, g in grads_big.items()}

    like = {k: jax.ShapeDtypeStruct(sm[k].shape, F32) for k in SMALL}
    grads_small = _unpack_small(_small_allreduce_call(_pack_small(gs)), like)
    conv_q = grads_small["conv_w"].reshape(DEPTH, CONV_WIDTH, N_QUARTERS, D_RNN // N_QUARTERS)
    grads_small["conv_w"] = lax.dynamic_index_in_dim(conv_q, chip, axis=2, keepdims=False)

    delta, new_m, new_v = {}, {}, {}
    for k in BIG:
        views = [_as4(_as_rows(a)) for a in (w[k], grads_big[k], m[k], v[k])]
        outs = _ew_call(_adamw, "adamw_big", [(a, (0, 0)) for a in views],
                        [(sds(views[0].shape, F32), (0, 0))] * 3)
        delta[k], new_m[k], new_v[k] = (o.reshape(w[k].shape) for o in outs)
    outs = _small_adamw_call(*[[_as_rows(d[k]) for k in SMALL] for d in (w, grads_small, m, v)])
    for d, o in zip((delta, new_m, new_v), outs):
        for k, a in zip(SMALL, o):
            d[k] = a.reshape(w[k].shape)

    grads = {**grads_big, **grads_small}
    total = lax.psum(loss[0, 0], ("x", "y", "c"))
    return (total, grad_x[None], *[grads[k] for k in WEIGHTS], *[delta[k] for k in WEIGHTS],
            *[new_m[k] for k in WEIGHTS], *[new_v[k] for k in WEIGHTS])
```

```python
import functools
import math

import jax
import jax.numpy as jnp
from jax import lax
from jax.experimental import pallas as pl
from jax.experimental.pallas import tpu as pltpu

F32 = jnp.float32
BF = jnp.bfloat16

DEPTH = 2
D_MODEL = 1024
D_RNN = 1280
D_SGU = 1024
D_FF = 4096
D_IN = 2 * D_RNN + 2 * D_SGU + 2 * D_MODEL
N_QUARTERS = 4
Q_IN = D_IN // N_QUARTERS
Q_FF = D_FF // N_QUARTERS
RNN_HEADS = 20
RNN_HEAD_DIM = 64
LRU_GROUP = 256
N_LRU_GROUPS = D_RNN // LRU_GROUP
HEADS_PER_GROUP = LRU_GROUP // RNN_HEAD_DIM
CONV_WIDTH = 4
LRU_C = 8.0
SGU_GROUPS = 8
SGU_BLOCK = 128
CHUNK = 64
EPS = 1e-6

ADAM_LR = 0.001
ADAM_B1 = 0.9
ADAM_B2 = 0.999
ADAM_EPS = 1e-08
ADAM_WD = 0.01
ADAM_STEP = 10

SUBLANES = 8
TOKEN_TILE = 512
VMEM_LIMIT_BYTES = 56 * 1024 * 1024

MESH = pl.DeviceIdType.MESH


def _params(semantics=None, vmem=True, **kw):
    return pltpu.CompilerParams(
        dimension_semantics=semantics,
        vmem_limit_bytes=VMEM_LIMIT_BYTES if vmem else None,
        **kw,
    )


def _dot(a, b):
    return jnp.dot(a, b, preferred_element_type=F32)


def _dot_nt(a, b):
    return lax.dot_general(a, b, (((1,), (1,)), ((), ())), preferred_element_type=F32)


def _dot_tn(a, b):
    return lax.dot_general(a, b, (((0,), (0,)), ((), ())), preferred_element_type=F32)


_GELU_C = math.sqrt(2.0 / math.pi)
_GELU_A = 0.044715


def _gelu(x):
    return 0.5 * x * (1.0 + jnp.tanh(_GELU_C * (x + _GELU_A * x * x * x)))


def _gelu_and_grad(x):
    x2 = x * x
    t = jnp.tanh(_GELU_C * (x + _GELU_A * x2 * x))
    du = _GELU_C * (1.0 + 3.0 * _GELU_A * x2)
    return 0.5 * x * (1.0 + t), 0.5 * (1.0 + t) + 0.5 * x * (1.0 - t * t) * du


def _rms_stats(x):
    return lax.rsqrt(jnp.mean(x * x, axis=-1, keepdims=True) + EPS)


def _rms_bwd(dy, x, g):
    rs = _rms_stats(x)
    n = x * rs
    dn = dy * g
    dx = rs * (dn - n * jnp.mean(dn * n, axis=-1, keepdims=True))
    return dx, dy * n


def _row_sum(x):
    return jnp.sum(x, axis=0, keepdims=True)


def _tile_spec(ts, width, col=0):
    return pl.BlockSpec((ts, width), lambda i, col=col: (i, col))


def _full_spec(shape):
    zeros = (0,) * len(shape)
    return pl.BlockSpec(shape, lambda *_: zeros)


def _layer_spec(w, layer):
    zeros = (0,) * (w.ndim - 1)
    return pl.BlockSpec((None,) + tuple(w.shape[1:]), lambda *_: (layer,) + zeros)


def _norm_call(x, g, ts):
    s = x.shape[0]

    def body(x_ref, g_ref, h_ref):
        xv = x_ref[...]
        h_ref[...] = (xv * _rms_stats(xv) * g_ref[...]).astype(BF)

    return pl.pallas_call(
        body, name="norm_fwd", grid=(s // ts,),
        in_specs=[_tile_spec(ts, D_MODEL), _full_spec((1, D_MODEL))],
        out_specs=_tile_spec(ts, D_MODEL),
        out_shape=jax.ShapeDtypeStruct((s, D_MODEL), BF),
        compiler_params=_params(("parallel",)),
    )(x, g)


def _inproj_call(h, w_in, layer, ts):
    s = h.shape[0]

    def body(h_ref, w_ref, o_ref):
        o_ref[...] = _dot(h_ref[...], w_ref[...]).astype(BF)

    return pl.pallas_call(
        body, name="inproj_fwd", grid=(N_QUARTERS, s // ts),
        in_specs=[
            pl.BlockSpec((ts, D_MODEL), lambda q, i: (i, 0)),
            pl.BlockSpec((None, None, D_MODEL, Q_IN), lambda q, i: (layer, q, 0, 0)),
        ],
        out_specs=pl.BlockSpec((ts, Q_IN), lambda q, i: (i, q)),
        out_shape=jax.ShapeDtypeStruct((s, D_IN), BF),
        compiler_params=_params(("parallel", "parallel")),
    )(h, w_in)


def _shift_down(x, tail, s):
    xr = pltpu.roll(x, s, 0)
    tr = pltpu.roll(tail, s, 0)
    row = lax.broadcasted_iota(jnp.int32, tail.shape, 0)
    top = jnp.where(row < s, tr, xr[0:SUBLANES])
    return jnp.concatenate([top, xr[SUBLANES:]], axis=0)


def _shift_up(x, head, s):
    t = x.shape[0]
    xr = pltpu.roll(x, t - s, 0)
    hr = pltpu.roll(head, SUBLANES - s, 0)
    row = lax.broadcasted_iota(jnp.int32, head.shape, 0)
    bottom = jnp.where(row >= SUBLANES - s, hr, xr[t - SUBLANES:])
    return jnp.concatenate([xr[: t - SUBLANES], bottom], axis=0)


def _conv_fwd(x, tail, cw_ref, cb_ref):
    shifted = [x] + [_shift_down(x, tail, s) for s in range(1, CONV_WIDTH)]
    out = cb_ref[...] + cw_ref[CONV_WIDTH - 1:CONV_WIDTH, :] * x
    for s in range(1, CONV_WIDTH):
        k = CONV_WIDTH - 1 - s
        out = out + cw_ref[k:k + 1, :] * shifted[s]
    return out, shifted


def _group_dot(x_bf, w_ref, dot):
    cols = [dot(x_bf[:, g * LRU_GROUP:(g + 1) * LRU_GROUP], w_ref[g]) for g in range(N_LRU_GROUPS)]
    return jnp.concatenate(cols, axis=1)


def _lru_gates(xr, wa_ref, wx_ref, ba_ref, bx_ref, sp_ref):
    xb = xr.astype(BF)
    r = jax.nn.sigmoid(_group_dot(xb, wa_ref, _dot) + ba_ref[...])
    i = jax.nn.sigmoid(_group_dot(xb, wx_ref, _dot) + bx_ref[...])
    log_a = (-LRU_C * r) * sp_ref[...]
    a = jnp.exp(log_a)
    nrm = jnp.sqrt(-jnp.tanh(log_a) * (a * a + 1.0))
    return r, i, a, nrm


def _linear_scan(a, b, carry, al_ref, bl_ref, h_ref, reverse):
    t, c = a.shape
    rowm = lax.broadcasted_iota(jnp.int32, (t, c), 0) & (SUBLANES - 1)
    for d in (1, 2, 4):
        if reverse:
            keep, sh = rowm < SUBLANES - d, t - d
        else:
            keep, sh = rowm >= d, d
        a_sh = jnp.where(keep, pltpu.roll(a, sh, 0), 1.0)
        b_sh = jnp.where(keep, pltpu.roll(b, sh, 0), 0.0)
        b = a * b_sh + b
        a = a * a_sh
    al_ref[...] = a
    bl_ref[...] = b
    groups = t // SUBLANES

    def step(j, state):
        jj = groups - 1 - j if reverse else j
        off = pl.multiple_of(jj * SUBLANES, SUBLANES)
        rows = bl_ref[pl.ds(off, SUBLANES), :] + al_ref[pl.ds(off, SUBLANES), :] * state
        h_ref[pl.ds(off, SUBLANES), :] = rows
        last = rows[0:1, :] if reverse else rows[SUBLANES - 1:SUBLANES, :]
        return jnp.broadcast_to(last, (SUBLANES, c))

    out = lax.fori_loop(0, groups, step, jnp.broadcast_to(carry, (SUBLANES, c)))
    return out[0:1, :]


def _rnn_fwd_call(proj, wa, wx, ba, bx, sp, cw, cb, ts):
    s = proj.shape[0]

    def body(xg_ref, wa_ref, wx_ref, ba_ref, bx_ref, sp_ref, cw_ref, cb_ref, hr_ref, ya_ref,
             tail_sc, carry_sc, al_sc, bl_sc, h_sc):
        @pl.when(pl.program_id(0) == 0)
        def _():
            tail_sc[...] = jnp.zeros_like(tail_sc)
            carry_sc[...] = jnp.zeros_like(carry_sc)

        x = xg_ref[:, :D_RNN].astype(F32)
        g = xg_ref[:, D_RNN:].astype(F32)
        xr, _ = _conv_fwd(x, tail_sc[...], cw_ref, cb_ref)
        tail_sc[...] = x[ts - SUBLANES:, :]
        _, i, a, nrm = _lru_gates(xr, wa_ref, wx_ref, ba_ref, bx_ref, sp_ref)
        carry_sc[...] = _linear_scan(a, nrm * (i * xr), carry_sc[...], al_sc, bl_sc, h_sc, False)
        h = h_sc[...]
        hr_ref[...] = h.astype(BF)
        ya_ref[...] = (h * _gelu(g)).astype(BF)

    gw = (N_LRU_GROUPS, LRU_GROUP, LRU_GROUP)
    return pl.pallas_call(
        body, name="rnn_fwd", grid=(s // ts,),
        in_specs=[_tile_spec(ts, 2 * D_RNN), _full_spec(gw), _full_spec(gw),
                  _full_spec((1, D_RNN)), _full_spec((1, D_RNN)), _full_spec((1, D_RNN)),
                  _full_spec((CONV_WIDTH, D_RNN)), _full_spec((1, D_RNN))],
        out_specs=[_tile_spec(ts, D_RNN), _tile_spec(ts, D_RNN)],
        out_shape=[jax.ShapeDtypeStruct((s, D_RNN), BF), jax.ShapeDtypeStruct((s, D_RNN), BF)],
        scratch_shapes=[pltpu.VMEM((SUBLANES, D_RNN), F32), pltpu.VMEM((1, D_RNN), F32),
                        pltpu.VMEM((ts, D_RNN), F32), pltpu.VMEM((ts, D_RNN), F32),
                        pltpu.VMEM((ts, D_RNN), F32)],
        compiler_params=_params(("arbitrary",)),
    )(proj, wa, wx, ba, bx, sp, cw, cb)


def _layernorm_fwd(x):
    mu = jnp.mean(x, axis=-1, keepdims=True)
    xc = x - mu
    rstd = lax.rsqrt(jnp.mean(xc * xc, axis=-1, keepdims=True) + EPS)
    return xc * rstd, rstd


def _sgu_mix(vn_bf, wm_ref, bsb_ref, ts):
    rows = []
    for blk in range(ts // SGU_BLOCK):
        r0 = blk * SGU_BLOCK
        cols = [
            _dot(wm_ref[g], vn_bf[r0:r0 + SGU_BLOCK, g * SGU_BLOCK:(g + 1) * SGU_BLOCK]) + bsb_ref[g]
            for g in range(SGU_GROUPS)
        ]
        rows.append(jnp.concatenate(cols, axis=1))
    return jnp.concatenate(rows, axis=0)


def _sgu_fwd_call(proj, wm, bsb, lg, lb, ts):
    s = proj.shape[0]

    def body(uv_ref, wm_ref, bsb_ref, lg_ref, lb_ref, yb_ref):
        gu = _gelu(uv_ref[:, :D_SGU].astype(F32))
        gv = _gelu(uv_ref[:, D_SGU:2 * D_SGU].astype(F32))
        nh, _ = _layernorm_fwd(gv)
        vn = (nh * lg_ref[...] + lb_ref[...]).astype(BF)
        yb_ref[...] = (gu * _sgu_mix(vn, wm_ref, bsb_ref, ts)).astype(BF)

    sw = (SGU_GROUPS, SGU_BLOCK, SGU_BLOCK)
    return pl.pallas_call(
        body, name="sgu_fwd", grid=(s // ts,),
        in_specs=[_tile_spec(ts, 2 * D_RNN, 1), _full_spec(sw), _full_spec(sw),
                  _full_spec((1, D_SGU)), _full_spec((1, D_SGU))],
        out_specs=_tile_spec(ts, D_SGU),
        out_shape=jax.ShapeDtypeStruct((s, D_SGU), BF),
        compiler_params=_params(("parallel",)),
    )(proj, wm, bsb, lg, lb)


_GATE_COL0 = (2 * D_RNN + 2 * D_SGU) // 512


def _gate_specs(ts):
    return [_tile_spec(ts, 512, _GATE_COL0 + j) for j in range(4)]


def _merge_call(x, proj, ya_pre, yb_pre, w_ba, w_bb, w_out, g2, layer, ts):
    s = x.shape[0]

    def body(x_ref, ga0, ga1, gb0, gb1, ya_ref, yb_ref, wa_ref, wb_ref, wo_ref, g2_ref,
             x1_ref, yao_ref, ybo_ref, mg_ref, h2_ref):
        ya = _dot(ya_ref[...], wa_ref[...])
        yb = _dot(yb_ref[...], wb_ref[...])
        sa = jax.nn.sigmoid(jnp.concatenate([ga0[...], ga1[...]], axis=1).astype(F32))
        sb = jax.nn.sigmoid(jnp.concatenate([gb0[...], gb1[...]], axis=1).astype(F32))
        merged = (sa * ya + sb * yb).astype(BF)
        x1 = x_ref[...] + _dot(merged, wo_ref[...])
        x1_ref[...] = x1
        yao_ref[...] = ya.astype(BF)
        ybo_ref[...] = yb.astype(BF)
        mg_ref[...] = merged
        h2_ref[...] = (x1 * _rms_stats(x1) * g2_ref[...]).astype(BF)

    act = jax.ShapeDtypeStruct((s, D_MODEL), BF)
    return pl.pallas_call(
        body, name="merge_fwd", grid=(s // ts,),
        in_specs=[_tile_spec(ts, D_MODEL)] + _gate_specs(ts) + [
            _tile_spec(ts, D_RNN), _tile_spec(ts, D_SGU),
            _layer_spec(w_ba, layer), _layer_spec(w_bb, layer), _layer_spec(w_out, layer),
            _full_spec((1, D_MODEL))],
        out_specs=[_tile_spec(ts, D_MODEL)] * 5,
        out_shape=[jax.ShapeDtypeStruct((s, D_MODEL), F32), act, act, act, act],
        compiler_params=_params(("parallel",)),
    )(x, proj, proj, proj, proj, ya_pre, yb_pre, w_ba, w_bb, w_out, g2)


def _ffn_call(x1, h2, w_up, w_down, layer, ts):
    s = x1.shape[0]

    def body(x1_ref, h2_ref, wu_ref, wd_ref, x2_ref, p_ref):
        h2v = h2_ref[...]
        acc = x1_ref[...]
        for q in range(N_QUARTERS):
            p = _dot(h2v, wu_ref[q])
            p_ref[:, q * Q_FF:(q + 1) * Q_FF] = p.astype(BF)
            f = jnp.square(jnp.maximum(p, 0.0)).astype(BF)
            acc = acc + _dot(f, wd_ref[q * Q_FF:(q + 1) * Q_FF, :])
        x2_ref[...] = acc

    return pl.pallas_call(
        body, name="ffn_fwd", grid=(s // ts,),
        in_specs=[_tile_spec(ts, D_MODEL), _tile_spec(ts, D_MODEL),
                  pl.BlockSpec((None, N_QUARTERS, D_MODEL, Q_FF), lambda i: (layer, 0, 0, 0)),
                  pl.BlockSpec((None, D_FF, D_MODEL), lambda i: (layer, 0, 0))],
        out_specs=[_tile_spec(ts, D_MODEL), _tile_spec(ts, D_FF)],
        out_shape=[jax.ShapeDtypeStruct((s, D_MODEL), F32), jax.ShapeDtypeStruct((s, D_FF), BF)],
        compiler_params=_params(("parallel",)),
    )(x1, h2, w_up, w_down)


def _loss_call(x, target, gf, ts):
    s = x.shape[0]

    def body(x_ref, t_ref, g_ref, dx_ref, loss_ref, dg_ref):
        @pl.when(pl.program_id(0) == 0)
        def _():
            loss_ref[...] = jnp.zeros_like(loss_ref)
            dg_ref[...] = jnp.zeros_like(dg_ref)

        xv = x_ref[...]
        gv = g_ref[...]
        err = xv * _rms_stats(xv) * gv - t_ref[...]
        part = 0.5 * jnp.sum(jnp.mean(err * err, axis=-1, keepdims=True), axis=0, keepdims=True)
        loss_ref[...] += jnp.broadcast_to(part, loss_ref.shape)
        dx, dg = _rms_bwd(err * (1.0 / D_MODEL), xv, gv)
        dx_ref[...] = dx
        dg_ref[...] += _row_sum(dg)

    return pl.pallas_call(
        body, name="loss_head", grid=(s // ts,),
        in_specs=[_tile_spec(ts, D_MODEL), _tile_spec(ts, D_MODEL), _full_spec((1, D_MODEL))],
        out_specs=[_tile_spec(ts, D_MODEL), _full_spec((1, 128)), _full_spec((1, D_MODEL))],
        out_shape=[jax.ShapeDtypeStruct((s, D_MODEL), F32), jax.ShapeDtypeStruct((1, 128), F32),
                   jax.ShapeDtypeStruct((1, D_MODEL), F32)],
        compiler_params=_params(("arbitrary",)),
    )(x, target, gf)


def _ffn_bwd_call(dx2, p, x1, g2, w_up, w_down, layer, ts):
    s = dx2.shape[0]

    def body(dx2_ref, p_ref, x1_ref, g2_ref, wu_ref, wd_ref, dx1_ref, dp_ref, dg_ref):
        @pl.when(pl.program_id(0) == 0)
        def _():
            dg_ref[...] = jnp.zeros_like(dg_ref)

        dx2v = dx2_ref[...]
        dyb = dx2v.astype(BF)
        dh2 = jnp.zeros((ts, D_MODEL), F32)
        for q in range(N_QUARTERS):
            cols = slice(q * Q_FF, (q + 1) * Q_FF)
            df = _dot_nt(dyb, wd_ref[cols, :])
            dp = (df * (2.0 * jnp.maximum(p_ref[:, cols].astype(F32), 0.0))).astype(BF)
            dp_ref[:, cols] = dp
            dh2 = dh2 + _dot_nt(dp, wu_ref[q])
        dx, dg = _rms_bwd(dh2, x1_ref[...], g2_ref[...])
        dx1_ref[...] = dx2v + dx
        dg_ref[...] += _row_sum(dg)

    return pl.pallas_call(
        body, name="ffn_bwd", grid=(s // ts,),
        in_specs=[_tile_spec(ts, D_MODEL), _tile_spec(ts, D_FF), _tile_spec(ts, D_MODEL),
                  _full_spec((1, D_MODEL)),
                  pl.BlockSpec((None, N_QUARTERS, D_MODEL, Q_FF), lambda i: (layer, 0, 0, 0)),
                  pl.BlockSpec((None, D_FF, D_MODEL), lambda i: (layer, 0, 0))],
        out_specs=[_tile_spec(ts, D_MODEL), _tile_spec(ts, D_FF), _full_spec((1, D_MODEL))],
        out_shape=[jax.ShapeDtypeStruct((s, D_MODEL), F32), jax.ShapeDtypeStruct((s, D_FF), BF),
                   jax.ShapeDtypeStruct((1, D_MODEL), F32)],
        compiler_params=_params(("arbitrary",)),
    )(dx2, p, x1, g2, w_up, w_down)


def _merge_bwd_call(dx1, proj, ya, yb, w_ba, w_bb, w_out, layer, ts):
    s = dx1.shape[0]

    def body(dx1_ref, ga0, ga1, gb0, gb1, ya_ref, yb_ref, wa_ref, wb_ref, wo_ref,
             dya_ref, dyb_ref, dgate_ref, dyap_ref, dybp_ref):
        dm = _dot_nt(dx1_ref[...].astype(BF), wo_ref[...])
        sa = jax.nn.sigmoid(jnp.concatenate([ga0[...], ga1[...]], axis=1).astype(F32))
        sb = jax.nn.sigmoid(jnp.concatenate([gb0[...], gb1[...]], axis=1).astype(F32))
        dya = (dm * sa).astype(BF)
        dyb = (dm * sb).astype(BF)
        dya_ref[...] = dya
        dyb_ref[...] = dyb
        dgate_ref[:, :D_MODEL] = (dm * ya_ref[...].astype(F32) * sa * (1.0 - sa)).astype(BF)
        dgate_ref[:, D_MODEL:] = (dm * yb_ref[...].astype(F32) * sb * (1.0 - sb)).astype(BF)
        dyap_ref[...] = _dot_nt(dya, wa_ref[...]).astype(BF)
        dybp_ref[...] = _dot_nt(dyb, wb_ref[...]).astype(BF)

    act = jax.ShapeDtypeStruct((s, D_MODEL), BF)
    return pl.pallas_call(
        body, name="merge_bwd", grid=(s // ts,),
        in_specs=[_tile_spec(ts, D_MODEL)] + _gate_specs(ts) + [
            _tile_spec(ts, D_MODEL), _tile_spec(ts, D_MODEL),
            _layer_spec(w_ba, layer), _layer_spec(w_bb, layer), _layer_spec(w_out, layer)],
        out_specs=[_tile_spec(ts, D_MODEL), _tile_spec(ts, D_MODEL), _tile_spec(ts, 2 * D_MODEL),
                   _tile_spec(ts, D_RNN), _tile_spec(ts, D_SGU)],
        out_shape=[act, act, jax.ShapeDtypeStruct((s, 2 * D_MODEL), BF),
                   jax.ShapeDtypeStruct((s, D_RNN), BF), jax.ShapeDtypeStruct((s, D_SGU), BF)],
        compiler_params=_params(("parallel",)),
    )(dx1, proj, proj, proj, proj, ya, yb, w_ba, w_bb, w_out)


def _sgu_bwd_call(dyb_pre, proj, wm, bsb, mask, lg, lb, ts):
    s = proj.shape[0]

    def body(dy_ref, uv_ref, wm_ref, bsb_ref, mask_ref, lg_ref, lb_ref,
             duv_ref, dws_ref, dbs_ref, dlg_ref, dlb_ref, dm_sc):
        step = pl.program_id(0)

        @pl.when(step == 0)
        def _():
            dws_ref[...] = jnp.zeros_like(dws_ref)
            dlg_ref[...] = jnp.zeros_like(dlg_ref)
            dlb_ref[...] = jnp.zeros_like(dlb_ref)
            dm_sc[...] = jnp.zeros_like(dm_sc)

        gu, dgu_du = _gelu_and_grad(uv_ref[:, :D_SGU].astype(F32))
        gv, dgv_dv = _gelu_and_grad(uv_ref[:, D_SGU:2 * D_SGU].astype(F32))
        nh, rstd = _layernorm_fwd(gv)
        lgv = lg_ref[...]
        vn = (nh * lgv + lb_ref[...]).astype(BF)
        dy = dy_ref[...].astype(F32)
        du = dy * _sgu_mix(vn, wm_ref, bsb_ref, ts) * dgu_du
        dmix = dy * gu
        dmix_bf = dmix.astype(BF)
        dm_acc = dm_sc[...]
        rows = []
        for blk in range(ts // SGU_BLOCK):
            r0 = blk * SGU_BLOCK
            dm_acc = dm_acc + dmix[r0:r0 + SGU_BLOCK, :]
            cols = []
            for g in range(SGU_GROUPS):
                c0 = g * SGU_BLOCK
                dmg = dmix_bf[r0:r0 + SGU_BLOCK, c0:c0 + SGU_BLOCK]
                cols.append(_dot_tn(wm_ref[g], dmg))
                dws_ref[g] += mask_ref[...] * _dot_nt(dmg, vn[r0:r0 + SGU_BLOCK, c0:c0 + SGU_BLOCK])
            rows.append(jnp.concatenate(cols, axis=1))
        dm_sc[...] = dm_acc
        dvn = jnp.concatenate(rows, axis=0)
        dlg_ref[...] += _row_sum(dvn * nh)
        dlb_ref[...] += _row_sum(dvn)
        dnh = dvn * lgv
        dgv = rstd * (dnh - jnp.mean(dnh, axis=-1, keepdims=True)
                      - nh * jnp.mean(dnh * nh, axis=-1, keepdims=True))
        duv_ref[:, :D_SGU] = du.astype(BF)
        duv_ref[:, D_SGU:] = (dgv * dgv_dv).astype(BF)

        @pl.when(step == pl.num_programs(0) - 1)
        def _():
            for g in range(SGU_GROUPS):
                dbs_ref[:, g:g + 1] = jnp.sum(
                    dm_acc[:, g * SGU_BLOCK:(g + 1) * SGU_BLOCK], axis=1, keepdims=True)

    sw = (SGU_GROUPS, SGU_BLOCK, SGU_BLOCK)
    return pl.pallas_call(
        body, name="sgu_bwd", grid=(s // ts,),
        in_specs=[_tile_spec(ts, D_SGU), _tile_spec(ts, 2 * D_RNN, 1), _full_spec(sw), _full_spec(sw),
                  _full_spec((SGU_BLOCK, SGU_BLOCK)), _full_spec((1, D_SGU)), _full_spec((1, D_SGU))],
        out_specs=[_tile_spec(ts, 2 * D_SGU), _full_spec(sw), _full_spec((SGU_BLOCK, SGU_GROUPS)),
                   _full_spec((1, D_SGU)), _full_spec((1, D_SGU))],
        out_shape=[jax.ShapeDtypeStruct((s, 2 * D_SGU), BF), jax.ShapeDtypeStruct(sw, F32),
                   jax.ShapeDtypeStruct((SGU_BLOCK, SGU_GROUPS), F32),
                   jax.ShapeDtypeStruct((1, D_SGU), F32), jax.ShapeDtypeStruct((1, D_SGU), F32)],
        scratch_shapes=[pltpu.VMEM((SGU_BLOCK, D_SGU), F32)],
        compiler_params=_params(("arbitrary",)),
    )(dyb_pre, proj, wm, bsb, mask, lg, lb)


_ROW_DBA, _ROW_DBX, _ROW_DSP, _ROW_DCB, _ROW_DCW = 0, 1, 2, 3, 4
_PREV_ROWS = 16


def _rnn_bwd_call(dya_pre, proj, hr, wa, wx, ba, bx, sp, cw, cb, ts):
    s = proj.shape[0]
    nt = s // ts
    per = ts // _PREV_ROWS

    def tile(i):
        return nt - 1 - i

    def prev(i):
        return jnp.maximum(tile(i) * per - 1, 0)

    def body(dy_ref, xg_ref, xgp_ref, hr_ref, hrp_ref, wa_ref, wx_ref, ba_ref, bx_ref, sp_ref,
             cw_ref, cb_ref, dxg_ref, dwa_ref, dwx_ref, vec_ref,
             lam_carry, a_first, dxr_head, al_sc, bl_sc, lam_sc):
        step = pl.program_id(0)

        @pl.when(step == 0)
        def _():
            dwa_ref[...] = jnp.zeros_like(dwa_ref)
            dwx_ref[...] = jnp.zeros_like(dwx_ref)
            vec_ref[...] = jnp.zeros_like(vec_ref)
            lam_carry[...] = jnp.zeros_like(lam_carry)
            a_first[...] = jnp.zeros_like(a_first)
            dxr_head[...] = jnp.zeros_like(dxr_head)

        has_prev = (step < nt - 1).astype(F32)
        x = xg_ref[:, :D_RNN].astype(F32)
        g = xg_ref[:, D_RNN:].astype(F32)
        x_tail = xgp_ref[_PREV_ROWS - SUBLANES:, :D_RNN].astype(F32) * has_prev
        h_tail = hrp_ref[_PREV_ROWS - SUBLANES:, :].astype(F32) * has_prev
        xr, x_shifted = _conv_fwd(x, x_tail, cw_ref, cb_ref)
        r, i, a, nrm = _lru_gates(xr, wa_ref, wx_ref, ba_ref, bx_ref, sp_ref)
        h = hr_ref[...].astype(F32)
        dy = dy_ref[...].astype(F32)
        gg, dgg = _gelu_and_grad(g)

        coef = _shift_up(a, jnp.broadcast_to(a_first[...], (SUBLANES, D_RNN)), 1)
        lam_carry[...] = _linear_scan(coef, dy * gg, lam_carry[...], al_sc, bl_sc, lam_sc, True)
        a_first[...] = a[0:1, :]
        lam = lam_sc[...]

        da = lam * _shift_down(h, h_tail, 1)
        dnrm = lam * (i * xr)
        di = lam * nrm * xr
        dlog_a = da * a - dnrm * (a * a) / nrm
        spv = sp_ref[...]
        dza = (dlog_a * (-LRU_C * spv)) * (r * (1.0 - r))
        dzx = di * (i * (1.0 - i))
        vec_ref[_ROW_DSP:_ROW_DSP + 1, :] += _row_sum(dlog_a * (-LRU_C * r))
        vec_ref[_ROW_DBA:_ROW_DBA + 1, :] += _row_sum(dza)
        vec_ref[_ROW_DBX:_ROW_DBX + 1, :] += _row_sum(dzx)
        xb = xr.astype(BF)
        dza_bf = dza.astype(BF)
        dzx_bf = dzx.astype(BF)
        for grp in range(N_LRU_GROUPS):
            cols = slice(grp * LRU_GROUP, (grp + 1) * LRU_GROUP)
            dwa_ref[grp] += _dot_tn(xb[:, cols], dza_bf[:, cols])
            dwx_ref[grp] += _dot_tn(xb[:, cols], dzx_bf[:, cols])
        dxr = (lam * nrm * i + _group_dot(dza_bf, wa_ref, _dot_nt) + _group_dot(dzx_bf, wx_ref, _dot_nt))

        vec_ref[_ROW_DCB:_ROW_DCB + 1, :] += _row_sum(dxr)
        head = dxr_head[...]
        dx = cw_ref[CONV_WIDTH - 1:CONV_WIDTH, :] * dxr
        vec_ref[_ROW_DCW + 3:_ROW_DCW + 4, :] += _row_sum(dxr * x)
        for sft in range(1, CONV_WIDTH):
            k = CONV_WIDTH - 1 - sft
            dx = dx + cw_ref[k:k + 1, :] * _shift_up(dxr, head, sft)
            vec_ref[_ROW_DCW + k:_ROW_DCW + k + 1, :] += _row_sum(dxr * x_shifted[sft])
        dxr_head[...] = dxr[0:SUBLANES, :]
        dxg_ref[:, :D_RNN] = dx.astype(BF)
        dxg_ref[:, D_RNN:] = (dy * h * dgg).astype(BF)

    gw = (N_LRU_GROUPS, LRU_GROUP, LRU_GROUP)
    rev = lambda width: pl.BlockSpec((ts, width), lambda i: (tile(i), 0))
    return pl.pallas_call(
        body, name="rnn_bwd", grid=(nt,),
        in_specs=[rev(D_RNN), rev(2 * D_RNN),
                  pl.BlockSpec((_PREV_ROWS, 2 * D_RNN), lambda i: (prev(i), 0)),
                  rev(D_RNN),
                  pl.BlockSpec((_PREV_ROWS, D_RNN), lambda i: (prev(i), 0)),
                  _full_spec(gw), _full_spec(gw),
                  _full_spec((1, D_RNN)), _full_spec((1, D_RNN)), _full_spec((1, D_RNN)),
                  _full_spec((CONV_WIDTH, D_RNN)), _full_spec((1, D_RNN))],
        out_specs=[rev(2 * D_RNN), _full_spec(gw), _full_spec(gw), _full_spec((SUBLANES, D_RNN))],
        out_shape=[jax.ShapeDtypeStruct((s, 2 * D_RNN), BF), jax.ShapeDtypeStruct(gw, F32),
                   jax.ShapeDtypeStruct(gw, F32), jax.ShapeDtypeStruct((SUBLANES, D_RNN), F32)],
        scratch_shapes=[pltpu.VMEM((1, D_RNN), F32), pltpu.VMEM((1, D_RNN), F32),
                        pltpu.VMEM((SUBLANES, D_RNN), F32),
                        pltpu.VMEM((ts, D_RNN), F32), pltpu.VMEM((ts, D_RNN), F32),
                        pltpu.VMEM((ts, D_RNN), F32)],
        compiler_params=_params(("arbitrary",)),
    )(dya_pre, proj, proj, hr, hr, wa, wx, ba, bx, sp, cw, cb)


def _inproj_bwd_call(dxg, duv, dgate, dx1, x, g1, w_in, layer, ts):
    s = x.shape[0]

    def body(dxg_ref, duv_ref, dgt_ref, dx1_ref, x_ref, g_ref, w_ref, dx_ref, dproj_ref, dg_ref):
        @pl.when(pl.program_id(0) == 0)
        def _():
            dg_ref[...] = jnp.zeros_like(dg_ref)

        dproj = jnp.concatenate([dxg_ref[...], duv_ref[...], dgt_ref[...]], axis=1)
        dproj_ref[...] = dproj
        dh = jnp.zeros((ts, D_MODEL), F32)
        for q in range(N_QUARTERS):
            dh = dh + _dot_nt(dproj[:, q * Q_IN:(q + 1) * Q_IN], w_ref[q])
        dx, dg = _rms_bwd(dh, x_ref[...], g_ref[...])
        dx_ref[...] = dx1_ref[...] + dx
        dg_ref[...] += _row_sum(dg)

    return pl.pallas_call(
        body, name="inproj_bwd", grid=(s // ts,),
        in_specs=[_tile_spec(ts, 2 * D_RNN), _tile_spec(ts, 2 * D_SGU), _tile_spec(ts, 2 * D_MODEL),
                  _tile_spec(ts, D_MODEL), _tile_spec(ts, D_MODEL), _full_spec((1, D_MODEL)),
                  pl.BlockSpec((None, N_QUARTERS, D_MODEL, Q_IN), lambda i: (layer, 0, 0, 0))],
        out_specs=[_tile_spec(ts, D_MODEL), _tile_spec(ts, D_IN), _full_spec((1, D_MODEL))],
        out_shape=[jax.ShapeDtypeStruct((s, D_MODEL), F32), jax.ShapeDtypeStruct((s, D_IN), BF),
                   jax.ShapeDtypeStruct((1, D_MODEL), F32)],
        compiler_params=_params(("arbitrary",)),
    )(dxg, duv, dgate, dx1, x, g1, w_in)


def _relu_sq(p):
    return jnp.square(jnp.maximum(p.astype(F32), 0.0))


def _wgrad_call(a, b, tm, tn, tk, col_blocked, name, layer, acc=None, a_fn=None):
    s, m = a.shape
    n = b.shape[1]

    def body(a_ref, b_ref, *rest):
        o_ref = rest[-1]
        av = a_ref[...]
        if a_fn is not None:
            av = a_fn(av)
        prod = _dot_tn(av.astype(BF), b_ref[...].astype(BF))

        @pl.when(pl.program_id(2) == 0)
        def _():
            o_ref[...] = prod

        @pl.when(pl.program_id(2) > 0)
        def _():
            o_ref[...] += prod

    if col_blocked:
        per_q = n // N_QUARTERS // tn
        out_spec = pl.BlockSpec((None, None, tm, tn), lambda i, j, k: (layer, j // per_q, 0, j % per_q))
        out_shape = jax.ShapeDtypeStruct((DEPTH, N_QUARTERS, m, n // N_QUARTERS), F32)
    else:
        out_spec = pl.BlockSpec((None, tm, tn), lambda i, j, k: (layer, i, j))
        out_shape = jax.ShapeDtypeStruct((DEPTH, m, n), F32)
    in_specs = [pl.BlockSpec((tk, tm), lambda i, j, k: (k, i)),
                pl.BlockSpec((tk, tn), lambda i, j, k: (k, j))]
    operands = [a, b]
    aliases = {}
    if acc is not None:
        in_specs.append(pl.BlockSpec(memory_space=pl.ANY))
        operands.append(acc)
        aliases = {2: 0}
    return pl.pallas_call(
        body, name=name, grid=(m // tm, n // tn, s // tk),
        in_specs=in_specs, out_specs=out_spec, out_shape=out_shape,
        input_output_aliases=aliases,
        compiler_params=_params(("parallel", "parallel", "arbitrary")),
    )(*operands)


BIG = ("w_in", "w_up", "w_down", "w_branch_a", "w_branch_b", "w_out")


def _block_diag(w):
    w4 = w.reshape(N_LRU_GROUPS, HEADS_PER_GROUP, RNN_HEAD_DIM, RNN_HEAD_DIM)
    eye = jnp.eye(HEADS_PER_GROUP, dtype=w.dtype)
    return jnp.einsum("gjio,jk->gjiko", w4, eye).reshape(N_LRU_GROUPS, LRU_GROUP, LRU_GROUP)


def _block_diag_extract(d):
    d5 = d.reshape(N_LRU_GROUPS, HEADS_PER_GROUP, RNN_HEAD_DIM, HEADS_PER_GROUP, RNN_HEAD_DIM)
    blocks = [d5[:, j, :, j, :] for j in range(HEADS_PER_GROUP)]
    return jnp.stack(blocks, axis=1).reshape(RNN_HEADS, RNN_HEAD_DIM, RNN_HEAD_DIM)


def _sgu_mask():
    chunk = jnp.arange(SGU_BLOCK) // CHUNK
    return (chunk[:, None] >= chunk[None, :]).astype(F32)


def _layer_small(sm, l):
    row = lambda v: v.reshape(1, -1)
    return dict(
        g1=row(sm["norm_mix_g"][l]), g2=row(sm["norm_ffn_g"][l]),
        wa=_block_diag(sm["lru_w_a"][l]).astype(BF), wx=_block_diag(sm["lru_w_x"][l]).astype(BF),
        ba=row(sm["lru_b_a"][l]), bx=row(sm["lru_b_x"][l]),
        sp=row(jax.nn.softplus(-sm["lru_lambda"][l])),
        cw=sm["conv_w"][l], cb=row(sm["conv_b"][l]),
        wm=(sm["sgu_w_s"][l] * _sgu_mask()).astype(BF),
        bsb=jnp.broadcast_to(sm["sgu_b_s"][l][:, :, None], (SGU_GROUPS, SGU_BLOCK, SGU_BLOCK)),
        lg=row(sm["sgu_ln_g"][l]), lb=row(sm["sgu_ln_b"][l]),
    )


def _local_step(x, target, big, sm, ts):
    saved = []
    for l in range(DEPTH):
        p = _layer_small(sm, l)
        h = _norm_call(x, p["g1"], ts)
        proj = _inproj_call(h, big["w_in"], l, ts)
        hr, ya_pre = _rnn_fwd_call(proj, p["wa"], p["wx"], p["ba"], p["bx"], p["sp"], p["cw"], p["cb"], ts)
        yb_pre = _sgu_fwd_call(proj, p["wm"], p["bsb"], p["lg"], p["lb"], ts)
        x1, ya, yb, merged, h2 = _merge_call(x, proj, ya_pre, yb_pre, big["w_branch_a"], big["w_branch_b"],
                                             big["w_out"], p["g2"], l, ts)
        x2, pre = _ffn_call(x1, h2, big["w_up"], big["w_down"], l, ts)
        saved.append(dict(p=p, x=x, h=h, proj=proj, hr=hr, ya_pre=ya_pre, yb_pre=yb_pre, x1=x1, ya=ya, yb=yb,
                          merged=merged, h2=h2, pre=pre))
        x = x2
    dx, loss, dgf = _loss_call(x, target, sm["final_norm_g"].reshape(1, -1), ts)

    gb = {k: None for k in BIG}
    gs = {k: [None] * DEPTH for k in (
        "norm_mix_g", "conv_w", "conv_b", "lru_w_a", "lru_b_a", "lru_w_x", "lru_b_x", "lru_lambda",
        "sgu_ln_g", "sgu_ln_b", "sgu_w_s", "sgu_b_s", "norm_ffn_g")}
    mask = _sgu_mask()
    for l in reversed(range(DEPTH)):
        sv = saved[l]
        p = sv["p"]
        dx1, dpre, dg2 = _ffn_bwd_call(dx, sv["pre"], sv["x1"], p["g2"], big["w_up"], big["w_down"], l, ts)
        tk = dx.shape[0]
        gb["w_down"] = _wgrad_call(sv["pre"], dx, Q_FF, D_MODEL // 2, tk, False, "wgrad_down", l, gb["w_down"],
                                   a_fn=_relu_sq)
        gb["w_up"] = _wgrad_call(sv["h2"], dpre, D_MODEL, Q_FF, tk, True, "wgrad_up", l, gb["w_up"])
        dya, dyb, dgate, dya_pre, dyb_pre = _merge_bwd_call(
            dx1, sv["proj"], sv["ya"], sv["yb"], big["w_branch_a"], big["w_branch_b"], big["w_out"], l, ts)
        gb["w_out"] = _wgrad_call(sv["merged"], dx1, D_MODEL, D_MODEL // 2, tk, False, "wgrad_out", l, gb["w_out"])
        gb["w_branch_a"] = _wgrad_call(sv["ya_pre"], dya, D_RNN, D_MODEL // 2, tk, False, "wgrad_branch_a", l,
                                       gb["w_branch_a"])
        gb["w_branch_b"] = _wgrad_call(sv["yb_pre"], dyb, D_SGU, D_MODEL // 2, tk, False, "wgrad_branch_b", l,
                                       gb["w_branch_b"])
        duv, dws, dbs, dlg, dlb = _sgu_bwd_call(dyb_pre, sv["proj"], p["wm"], p["bsb"], mask, p["lg"], p["lb"], ts)
        dxg, dwa, dwx, vec = _rnn_bwd_call(dya_pre, sv["proj"], sv["hr"], p["wa"], p["wx"], p["ba"], p["bx"],
                                           p["sp"], p["cw"], p["cb"], ts // 2)
        dx, dproj, dg1 = _inproj_bwd_call(dxg, duv, dgate, dx1, sv["x"], p["g1"], big["w_in"], l, ts)
        gb["w_in"] = _wgrad_call(sv["h"], dproj, D_MODEL, Q_IN, tk // 2, True, "wgrad_in", l, gb["w_in"])

        gs["norm_mix_g"][l] = dg1[0]
        gs["norm_ffn_g"][l] = dg2[0]
        gs["conv_w"][l] = vec[_ROW_DCW:_ROW_DCW + CONV_WIDTH]
        gs["conv_b"][l] = vec[_ROW_DCB]
        gs["lru_w_a"][l] = _block_diag_extract(dwa)
        gs["lru_w_x"][l] = _block_diag_extract(dwx)
        gs["lru_b_a"][l] = vec[_ROW_DBA].reshape(RNN_HEADS, RNN_HEAD_DIM)
        gs["lru_b_x"][l] = vec[_ROW_DBX].reshape(RNN_HEADS, RNN_HEAD_DIM)
        gs["lru_lambda"][l] = -vec[_ROW_DSP] * jax.nn.sigmoid(-sm["lru_lambda"][l])
        gs["sgu_ln_g"][l] = dlg[0]
        gs["sgu_ln_b"][l] = dlb[0]
        gs["sgu_w_s"][l] = dws
        gs["sgu_b_s"][l] = dbs.T
    gs = {k: jnp.stack(v) for k, v in gs.items()}
    gs["final_norm_g"] = dgf[0]
    return loss, dx, gb, gs


EW_BLOCK_ELEMS = 384 * 1024


def _row_block(rows, cols):
    for br in range(min(rows, EW_BLOCK_ELEMS // cols), 0, -1):
        if rows % br == 0 and br % 16 == 0:
            return br
    return rows


def _ew_call(fn, name, operands, outputs, slabs=1, sel=None):
    rows, cols = outputs[0][0].shape[2:]
    br = _row_block(rows, cols)
    n_in = len(operands)

    def pick(tok, g, s):
        if tok == "g":
            return g
        if isinstance(tok, tuple):
            return s[tok[1]]
        return tok

    def spec(idx):
        return pl.BlockSpec((None, None, br, cols),
                            lambda g, i, s, idx=idx: (pick(idx[0], g, s), pick(idx[1], g, s), i, 0))

    def body(sel_ref, *refs):
        outs = fn(*[r[...] for r in refs[:n_in]])
        for o_ref, o in zip(refs[n_in:], outs):
            o_ref[...] = o.astype(o_ref.dtype)

    if sel is None:
        sel = jnp.zeros((1,), jnp.int32)
    return pl.pallas_call(
        body, name=name, out_shape=[s for s, _ in outputs],
        grid_spec=pltpu.PrefetchScalarGridSpec(
            num_scalar_prefetch=1, grid=(slabs, rows // br),
            in_specs=[spec(idx) for _, idx in operands],
            out_specs=[spec(idx) for _, idx in outputs]),
        compiler_params=_params(("parallel", "parallel")),
    )(sel, *[a for a, _ in operands])


def _as4(a):
    return a.reshape((1,) * (4 - a.ndim) + a.shape)


def _adamw(w, g, m, v):
    m = ADAM_B1 * m + (1.0 - ADAM_B1) * g
    v = ADAM_B2 * v + (1.0 - ADAM_B2) * jnp.square(g)
    m_hat = m / (1.0 - ADAM_B1 ** ADAM_STEP)
    v_hat = v / (1.0 - ADAM_B2 ** ADAM_STEP)
    delta = -ADAM_LR * (m_hat / (jnp.sqrt(v_hat) + ADAM_EPS) + ADAM_WD * w)
    return delta, m, v


def _small_adamw_call(ws, gs, ms, vs):
    n = len(ws)

    def body(*refs):
        for k in range(n):
            w, g, m, v = (refs[j * n + k][...] for j in range(4))
            outs = _adamw(w, g, m, v)
            for j in range(3):
                refs[(4 + j) * n + k][...] = outs[j]

    shapes = [jax.ShapeDtypeStruct(w.shape, F32) for w in ws]
    outs = pl.pallas_call(
        body, name="adamw_small", out_shape=shapes * 3,
        in_specs=[pl.BlockSpec(memory_space=pltpu.VMEM)] * (4 * n),
        out_specs=[pl.BlockSpec(memory_space=pltpu.VMEM)] * (3 * n),
        compiler_params=_params(),
    )(*ws, *gs, *ms, *vs)
    return outs[:n], outs[n:2 * n], outs[2 * n:]


ANY = pl.BlockSpec(memory_space=pl.ANY)


def _place():
    x, y, c = lax.axis_index("x"), lax.axis_index("y"), lax.axis_index("c")
    chips = [(1 - x, y), (x, 1 - y), (1 - x, 1 - y)]
    return x, y, c, chips


def _remote(src, dst, send_sem, recv_sem, to):
    return pltpu.make_async_remote_copy(src_ref=src, dst_ref=dst, send_sem=send_sem, recv_sem=recv_sem,
                                        device_id=to, device_id_type=MESH)


def _gather_call(bufs):
    n = len(bufs)

    def body(*refs):
        out = refs[n:2 * n]
        send_sems, recv_sems = refs[2 * n:]
        x, y, c, chips = _place()
        me_q = 2 * x + y
        sibling = (x, y, 1 - c)
        first = []
        for w in range(n):
            for j, chip in enumerate(chips):
                mine = out[w].at[c, me_q]
                first.append(_remote(mine, mine, send_sems.at[w * 3 + j], recv_sems.at[w * 3 + j], (*chip, c)))
        for cp in first:
            cp.start()
        passed = []
        for w in range(n):
            for j, (qx, qy) in enumerate(chips):
                landed = out[w].at[c, 2 * qx + qy]
                k = w * 3 + j
                _remote(landed, landed, send_sems.at[k], recv_sems.at[k], (qx, qy, c)).wait_recv()
                cp = _remote(landed, landed, send_sems.at[3 * n + k], recv_sems.at[3 * n + k], sibling)
                cp.start()
                passed.append(cp)
        for w in range(n):
            for j, (qx, qy) in enumerate(chips):
                landed = out[w].at[1 - c, 2 * qx + qy]
                k = 3 * n + w * 3 + j
                _remote(landed, landed, send_sems.at[k], recv_sems.at[k], sibling).wait_recv()
        for cp in first + passed:
            cp.wait_send()

    return pl.pallas_call(
        body, name="gather_weights",
        out_shape=[jax.ShapeDtypeStruct(a.shape, a.dtype) for a in bufs],
        in_specs=[ANY] * n, out_specs=[ANY] * n,
        input_output_aliases={w: w for w in range(n)},
        scratch_shapes=[pltpu.SemaphoreType.DMA((6 * n,)), pltpu.SemaphoreType.DMA((6 * n,))],
        compiler_params=_params(vmem=False, has_side_effects=True),
    )(*bufs)


def _sibling_send_call(items):
    n = len(items)

    def body(*refs):
        src, out = refs[:n], refs[n:2 * n]
        send_sems, recv_sems = refs[2 * n:]
        x, y, c, _ = _place()
        copies = [_remote(src[w], out[w], send_sems.at[w], recv_sems.at[w], (x, y, 1 - c)) for w in range(n)]
        for cp in copies:
            cp.start()
        for cp in copies:
            cp.wait()

    return pl.pallas_call(
        body, name="grads_to_sibling",
        out_shape=[jax.ShapeDtypeStruct(a.shape, a.dtype) for a in items],
        in_specs=[ANY] * n, out_specs=[ANY] * n,
        scratch_shapes=[pltpu.SemaphoreType.DMA((n,)), pltpu.SemaphoreType.DMA((n,))],
        compiler_params=_params(vmem=False, has_side_effects=True),
    )(*items)


def _quarter_exchange_call(items):
    n = len(items)

    def body(*refs):
        src, out = refs[:n], refs[n:2 * n]
        send_sems, recv_sems = refs[2 * n:]
        x, y, c, chips = _place()
        sends = []
        for w in range(n):
            for j, (qx, qy) in enumerate(chips):
                k = w * 3 + j
                sends.append(_remote(src[w].at[2 * qx + qy], out[w].at[j], send_sems.at[k], recv_sems.at[k],
                                     (qx, qy, c)))
        for cp in sends:
            cp.start()
        for w in range(n):
            for j, (qx, qy) in enumerate(chips):
                k = w * 3 + j
                landed = out[w].at[j]
                _remote(landed, landed, send_sems.at[k], recv_sems.at[k], (qx, qy, c)).wait_recv()
        for cp in sends:
            cp.wait_send()

    return pl.pallas_call(
        body, name="grads_to_owner",
        out_shape=[jax.ShapeDtypeStruct((3,) + a.shape[1:], a.dtype) for a in items],
        in_specs=[ANY] * n, out_specs=[ANY] * n,
        scratch_shapes=[pltpu.SemaphoreType.DMA((3 * n,)), pltpu.SemaphoreType.DMA((3 * n,))],
        compiler_params=_params(vmem=False, has_side_effects=True),
    )(*items)


def _layer_swap_call(bufs):
    n = len(bufs)

    def body(*refs):
        out = refs[n:2 * n]
        send_sems, recv_sems = refs[2 * n:]
        x, y, c, _ = _place()
        sends = [_remote(out[w].at[c], out[w].at[c], send_sems.at[w], recv_sems.at[w], (x, y, 1 - c))
                 for w in range(n)]
        for cp in sends:
            cp.start()
        for w in range(n):
            landed = out[w].at[1 - c]
            _remote(landed, landed, send_sems.at[w], recv_sems.at[w], (x, y, 1 - c)).wait_recv()
        for cp in sends:
            cp.wait_send()

    return pl.pallas_call(
        body, name="grads_swap_layers",
        out_shape=[jax.ShapeDtypeStruct(a.shape, a.dtype) for a in bufs],
        in_specs=[ANY] * n, out_specs=[ANY] * n,
        input_output_aliases={w: w for w in range(n)},
        scratch_shapes=[pltpu.SemaphoreType.DMA((n,)), pltpu.SemaphoreType.DMA((n,))],
        compiler_params=_params(vmem=False, has_side_effects=True),
    )(*bufs)


N_DEVICES = 8
SMALL_ROWS = 616


def _small_allreduce_call(buf):
    def body(in_ref, out_ref, recv_ref, red_ref, send_sems, recv_sems):
        x, y, c, _ = _place()
        me = 4 * x + 2 * y + c

        def peer(k):
            return (x ^ ((k >> 2) & 1), y ^ ((k >> 1) & 1), c ^ (k & 1))

        scatter = [_remote(in_ref.at[me ^ k], recv_ref.at[me], send_sems.at[k - 1], recv_sems.at[k - 1], peer(k))
                   for k in range(1, N_DEVICES)]
        for cp in scatter:
            cp.start()
        recv_ref[me] = in_ref[me]
        for k in range(1, N_DEVICES):
            landed = recv_ref.at[me ^ k]
            _remote(landed, landed, send_sems.at[k - 1], recv_sems.at[k - 1], peer(k)).wait_recv()
        total = recv_ref[0]
        for j in range(1, N_DEVICES):
            total = total + recv_ref[j]
        red_ref[...] = total
        out_ref[me] = total
        spread = [_remote(red_ref, out_ref.at[me], send_sems.at[6 + k], recv_sems.at[6 + k], peer(k))
                  for k in range(1, N_DEVICES)]
        for cp in spread:
            cp.start()
        for k in range(1, N_DEVICES):
            landed = out_ref.at[me ^ k]
            _remote(landed, landed, send_sems.at[6 + k], recv_sems.at[6 + k], peer(k)).wait_recv()
        for cp in scatter + spread:
            cp.wait_send()

    shape = (N_DEVICES, SMALL_ROWS, 128)
    return pl.pallas_call(
        body, name="allreduce_small",
        out_shape=jax.ShapeDtypeStruct(shape, F32),
        in_specs=[pl.BlockSpec(memory_space=pltpu.VMEM)],
        out_specs=pl.BlockSpec(memory_space=pltpu.VMEM),
        scratch_shapes=[pltpu.VMEM(shape, F32), pltpu.VMEM(shape[1:], F32),
                        pltpu.SemaphoreType.DMA((2 * (N_DEVICES - 1),)),
                        pltpu.SemaphoreType.DMA((2 * (N_DEVICES - 1),))],
        compiler_params=_params(has_side_effects=True),
    )(buf)


SMALL = ("norm_mix_g", "conv_w", "conv_b", "lru_w_a", "lru_b_a", "lru_w_x", "lru_b_x", "lru_lambda",
         "sgu_ln_g", "sgu_ln_b", "sgu_w_s", "sgu_b_s", "norm_ffn_g", "final_norm_g")
WEIGHTS = ("norm_mix_g", "w_in", "conv_w", "conv_b", "lru_w_a", "lru_b_a", "lru_w_x", "lru_b_x", "lru_lambda",
           "sgu_ln_g", "sgu_ln_b", "sgu_w_s", "sgu_b_s", "w_branch_a", "w_branch_b", "w_out", "norm_ffn_g",
           "w_up", "w_down", "final_norm_g")
PACK_ALIGN = SUBLANES * 128


def _pack_small(gs):
    parts = []
    for k in SMALL:
        flat = gs[k].reshape(-1)
        parts.append(jnp.pad(flat, (0, -flat.size % PACK_ALIGN)))
    flat = jnp.concatenate(parts)
    flat = jnp.pad(flat, (0, N_DEVICES * SMALL_ROWS * 128 - flat.size))
    return flat.reshape(N_DEVICES, SMALL_ROWS, 128)


def _unpack_small(buf, like):
    flat = buf.reshape(-1)
    out, off = {}, 0
    for k in SMALL:
        size = like[k].size
        out[k] = flat[off:off + size].reshape(like[k].shape)
        off += size + (-size % PACK_ALIGN)
    return out


def _as_rows(a):
    return a.reshape(-1, a.shape[-1])


def kernel(x, norm_mix_g, w_in, conv_w, conv_b, lru_w_a, lru_b_a, lru_w_x, lru_b_x, lru_lambda, sgu_ln_g, sgu_ln_b, sgu_w_s, sgu_b_s, w_branch_a, w_branch_b, w_out, norm_ffn_g, w_up, w_down, final_norm_g, loss_target, m_norm_mix_g, m_w_in, m_conv_w, m_conv_b, m_lru_w_a, m_lru_b_a, m_lru_w_x, m_lru_b_x, m_lru_lambda, m_sgu_ln_g, m_sgu_ln_b, m_sgu_w_s, m_sgu_b_s, m_w_branch_a, m_w_branch_b, m_w_out, m_norm_ffn_g, m_w_up, m_w_down, m_final_norm_g, v_norm_mix_g, v_w_in, v_conv_w, v_conv_b, v_lru_w_a, v_lru_b_a, v_lru_w_x, v_lru_b_x, v_lru_lambda, v_sgu_ln_g, v_sgu_ln_b, v_sgu_w_s, v_sgu_b_s, v_w_branch_a, v_w_branch_b, v_w_out, v_norm_ffn_g, v_w_up, v_w_down, v_final_norm_g):
    w = dict(norm_mix_g=norm_mix_g, w_in=w_in, conv_w=conv_w, conv_b=conv_b, lru_w_a=lru_w_a, lru_b_a=lru_b_a,
             lru_w_x=lru_w_x, lru_b_x=lru_b_x, lru_lambda=lru_lambda, sgu_ln_g=sgu_ln_g, sgu_ln_b=sgu_ln_b,
             sgu_w_s=sgu_w_s, sgu_b_s=sgu_b_s, w_branch_a=w_branch_a, w_branch_b=w_branch_b, w_out=w_out,
             norm_ffn_g=norm_ffn_g, w_up=w_up, w_down=w_down, final_norm_g=final_norm_g)
    m = dict(norm_mix_g=m_norm_mix_g, w_in=m_w_in, conv_w=m_conv_w, conv_b=m_conv_b, lru_w_a=m_lru_w_a,
             lru_b_a=m_lru_b_a, lru_w_x=m_lru_w_x, lru_b_x=m_lru_b_x, lru_lambda=m_lru_lambda,
             sgu_ln_g=m_sgu_ln_g, sgu_ln_b=m_sgu_ln_b, sgu_w_s=m_sgu_w_s, sgu_b_s=m_sgu_b_s,
             w_branch_a=m_w_branch_a, w_branch_b=m_w_branch_b, w_out=m_w_out, norm_ffn_g=m_norm_ffn_g,
             w_up=m_w_up, w_down=m_w_down, final_norm_g=m_final_norm_g)
    v = dict(norm_mix_g=v_norm_mix_g, w_in=v_w_in, conv_w=v_conv_w, conv_b=v_conv_b, lru_w_a=v_lru_w_a,
             lru_b_a=v_lru_b_a, lru_w_x=v_lru_w_x, lru_b_x=v_lru_b_x, lru_lambda=v_lru_lambda,
             sgu_ln_g=v_sgu_ln_g, sgu_ln_b=v_sgu_ln_b, sgu_w_s=v_sgu_w_s, sgu_b_s=v_sgu_b_s,
             w_branch_a=v_w_branch_a, w_branch_b=v_w_branch_b, w_out=v_w_out, norm_ffn_g=v_norm_ffn_g,
             w_up=v_w_up, w_down=v_w_down, final_norm_g=v_final_norm_g)
    core = lax.axis_index("c")
    chip = 2 * lax.axis_index("x") + lax.axis_index("y")
    sel = jnp.stack([core, 1 - core, chip]).astype(jnp.int32)
    this_core, other_core, this_chip = ("sel", 0), ("sel", 1), ("sel", 2)
    sds = jax.ShapeDtypeStruct

    bufs = []
    for k in BIG:
        _, r, cols = w[k].shape
        bufs.append(_ew_call(lambda a: (a,), "cast_weights", [(w[k].reshape(DEPTH, 1, r, cols), ("g", 0))],
                             [(sds((DEPTH, N_QUARTERS, r, cols), BF), ("g", this_chip))], DEPTH, sel)[0])
    conv_buf = lax.dynamic_update_slice_in_dim(
        jnp.zeros((DEPTH, N_QUARTERS) + conv_w.shape[1:], F32), conv_w[:, None], chip, axis=1)
    gathered = _gather_call(bufs + [conv_buf])
    full = dict(zip(BIG, gathered[:-1]))
    big = {k: (full[k] if k in ("w_in", "w_up") else full[k].reshape(DEPTH, -1, D_MODEL)) for k in BIG}
    sm = {k: w[k] for k in SMALL}
    sm["conv_w"] = gathered[-1].transpose(0, 2, 1, 3).reshape(DEPTH, CONV_WIDTH, D_RNN)

    loss, grad_x, gb, gs = _local_step(x[0], loss_target[0], big, sm, TOKEN_TILE)

    flat = {k: gb[k].reshape(DEPTH, 1, -1, gb[k].shape[-1]) for k in BIG}
    to_sibling = [
        _ew_call(lambda a: (a,), "cast_grads", [(flat[k], (other_core, 0))],
                 [(sds((1, 1) + flat[k].shape[2:], BF), (0, 0))], 1, sel)[0] for k in BIG]
    from_sibling = _sibling_send_call(to_sibling)
    pair = [
        _ew_call(lambda a, b: (a + b.astype(F32),), "pair_sum", [(flat[k], (this_core, 0)), (r, (0, 0))],
                 [(sds(r.shape, BF), (0, 0))], 1, sel)[0].reshape(N_QUARTERS, -1, r.shape[-1])
        for k, r in zip(BIG, from_sibling)]
    landed = _quarter_exchange_call(pair)
    mine = [
        _ew_call(lambda a, b, c, d: (((a.astype(F32) + b.astype(F32)) + c.astype(F32)) + d.astype(F32),),
                 "quarter_sum", [(p[None], (0, this_chip))] + [(r[None], (0, j)) for j in range(3)],
                 [(sds((1, DEPTH) + r.shape[1:], F32), (0, this_core))], 1, sel)[0][0]
        for p, r in zip(pair, landed)]
    grads_big = dict(zip(BIG, _layer_swap_call(mine)))
    grads_big = {k: g.reshape(w[k].shape) for k, g in grads_big.items()}

    like = {k: jax.ShapeDtypeStruct(sm[k].shape, F32) for k in SMALL}
    grads_small = _unpack_small(_small_allreduce_call(_pack_small(gs)), like)
    conv_q = grads_small["conv_w"].reshape(DEPTH, CONV_WIDTH, N_QUARTERS, D_RNN // N_QUARTERS)
    grads_small["conv_w"] = lax.dynamic_index_in_dim(conv_q, chip, axis=2, keepdims=False)

    delta, new_m, new_v = {}, {}, {}
    for k in BIG:
        views = [_as4(_as_rows(a)) for a in (w[k], grads_big[k], m[k], v[k])]
        outs = _ew_call(_adamw, "adamw_big", [(a, (0, 0)) for a in views],
                        [(sds(views[0].shape, F32), (0, 0))] * 3)
        delta[k], new_m[k], new_v[k] = (o.reshape(w[k].shape) for o in outs)
    outs = _small_adamw_call(*[[_as_rows(d[k]) for k in SMALL] for d in (w, grads_small, m, v)])
    for d, o in zip((delta, new_m, new_v), outs):
        for k, a in zip(SMALL, o):
            d[k] = a.reshape(w[k].shape)

    grads = {**grads_big, **grads_small}
    total = lax.psum(loss[0, 0], ("x", "y", "c"))
    return (total, grad_x[None], *[grads[k] for k in WEIGHTS], *[delta[k] for k in WEIGHTS],
            *[new_m[k] for k in WEIGHTS], *[new_v[k] for k in WEIGHTS])
```

```python
import functools
import math

import jax
import jax.numpy as jnp
from jax import lax
from jax.experimental import pallas as pl
from jax.experimental.pallas import tpu as pltpu

F32 = jnp.float32
BF = jnp.bfloat16

DEPTH = 2
D_MODEL = 1024
D_RNN = 1280
D_SGU = 1024
D_FF = 4096
D_IN = 2 * D_RNN + 2 * D_SGU + 2 * D_MODEL
N_QUARTERS = 4
Q_IN = D_IN // N_QUARTERS
Q_FF = D_FF // N_QUARTERS
RNN_HEADS = 20
RNN_HEAD_DIM = 64
LRU_GROUP = 256
N_LRU_GROUPS = D_RNN // LRU_GROUP
HEADS_PER_GROUP = LRU_GROUP // RNN_HEAD_DIM
CONV_WIDTH = 4
LRU_C = 8.0
SGU_GROUPS = 8
SGU_BLOCK = 128
CHUNK = 64
EPS = 1e-6

ADAM_LR = 0.001
ADAM_B1 = 0.9
ADAM_B2 = 0.999
ADAM_EPS = 1e-08
ADAM_WD = 0.01
ADAM_STEP = 10

SUBLANES = 8
TOKEN_TILE = 512
VMEM_LIMIT_BYTES = 56 * 1024 * 1024

MESH = pl.DeviceIdType.MESH


def _params(semantics=None, vmem=True, **kw):
    return pltpu.CompilerParams(
        dimension_semantics=semantics,
        vmem_limit_bytes=VMEM_LIMIT_BYTES if vmem else None,
        **kw,
    )


def _dot(a, b):
    return jnp.dot(a, b, preferred_element_type=F32)


def _dot_nt(a, b):
    return lax.dot_general(a, b, (((1,), (1,)), ((), ())), preferred_element_type=F32)


def _dot_tn(a, b):
    return lax.dot_general(a, b, (((0,), (0,)), ((), ())), preferred_element_type=F32)


_GELU_C = math.sqrt(2.0 / math.pi)
_GELU_A = 0.044715


def _gelu(x):
    return 0.5 * x * (1.0 + jnp.tanh(_GELU_C * (x + _GELU_A * x * x * x)))


def _gelu_and_grad(x):
    x2 = x * x
    t = jnp.tanh(_GELU_C * (x + _GELU_A * x2 * x))
    du = _GELU_C * (1.0 + 3.0 * _GELU_A * x2)
    return 0.5 * x * (1.0 + t), 0.5 * (1.0 + t) + 0.5 * x * (1.0 - t * t) * du


def _rms_stats(x):
    return lax.rsqrt(jnp.mean(x * x, axis=-1, keepdims=True) + EPS)


def _rms_bwd(dy, x, g):
    rs = _rms_stats(x)
    n = x * rs
    dn = dy * g
    dx = rs * (dn - n * jnp.mean(dn * n, axis=-1, keepdims=True))
    return dx, dy * n


def _row_sum(x):
    return jnp.sum(x, axis=0, keepdims=True)


def _tile_spec(ts, width, col=0):
    return pl.BlockSpec((ts, width), lambda i, col=col: (i, col))


def _full_spec(shape):
    zeros = (0,) * len(shape)
    return pl.BlockSpec(shape, lambda *_: zeros)


def _layer_spec(w, layer):
    zeros = (0,) * (w.ndim - 1)
    return pl.BlockSpec((None,) + tuple(w.shape[1:]), lambda *_: (layer,) + zeros)


def _norm_call(x, g, ts):
    s = x.shape[0]

    def body(x_ref, g_ref, h_ref):
        xv = x_ref[...]
        h_ref[...] = (xv * _rms_stats(xv) * g_ref[...]).astype(BF)

    return pl.pallas_call(
        body, name="norm_fwd", grid=(s // ts,),
        in_specs=[_tile_spec(ts, D_MODEL), _full_spec((1, D_MODEL))],
        out_specs=_tile_spec(ts, D_MODEL),
        out_shape=jax.ShapeDtypeStruct((s, D_MODEL), BF),
        compiler_params=_params(("parallel",)),
    )(x, g)


def _inproj_call(h, w_in, layer, ts):
    s = h.shape[0]

    def body(h_ref, w_ref, o_ref):
        o_ref[...] = _dot(h_ref[...], w_ref[...]).astype(BF)

    return pl.pallas_call(
        body, name="inproj_fwd", grid=(N_QUARTERS, s // ts),
        in_specs=[
            pl.BlockSpec((ts, D_MODEL), lambda q, i: (i, 0)),
            pl.BlockSpec((None, None, D_MODEL, Q_IN), lambda q, i: (layer, q, 0, 0)),
        ],
        out_specs=pl.BlockSpec((ts, Q_IN), lambda q, i: (i, q)),
        out_shape=jax.ShapeDtypeStruct((s, D_IN), BF),
        compiler_params=_params(("parallel", "parallel")),
    )(h, w_in)


def _shift_down(x, tail, s):
    xr = pltpu.roll(x, s, 0)
    tr = pltpu.roll(tail, s, 0)
    row = lax.broadcasted_iota(jnp.int32, tail.shape, 0)
    top = jnp.where(row < s, tr, xr[0:SUBLANES])
    return jnp.concatenate([top, xr[SUBLANES:]], axis=0)


def _shift_up(x, head, s):
    t = x.shape[0]
    xr = pltpu.roll(x, t - s, 0)
    hr = pltpu.roll(head, SUBLANES - s, 0)
    row = lax.broadcasted_iota(jnp.int32, head.shape, 0)
    bottom = jnp.where(row >= SUBLANES - s, hr, xr[t - SUBLANES:])
    return jnp.concatenate([xr[: t - SUBLANES], bottom], axis=0)


def _conv_fwd(x, tail, cw_ref, cb_ref):
    shifted = [x] + [_shift_down(x, tail, s) for s in range(1, CONV_WIDTH)]
    out = cb_ref[...] + cw_ref[CONV_WIDTH - 1:CONV_WIDTH, :] * x
    for s in range(1, CONV_WIDTH):
        k = CONV_WIDTH - 1 - s
        out = out + cw_ref[k:k + 1, :] * shifted[s]
    return out, shifted


def _group_dot(x_bf, w_ref, dot):
    cols = [dot(x_bf[:, g * LRU_GROUP:(g + 1) * LRU_GROUP], w_ref[g]) for g in range(N_LRU_GROUPS)]
    return jnp.concatenate(cols, axis=1)


def _lru_gates(xr, wa_ref, wx_ref, ba_ref, bx_ref, sp_ref):
    xb = xr.astype(BF)
    r = jax.nn.sigmoid(_group_dot(xb, wa_ref, _dot) + ba_ref[...])
    i = jax.nn.sigmoid(_group_dot(xb, wx_ref, _dot) + bx_ref[...])
    log_a = (-LRU_C * r) * sp_ref[...]
    a = jnp.exp(log_a)
    nrm = jnp.sqrt(-jnp.tanh(log_a) * (a * a + 1.0))
    return r, i, a, nrm


def _linear_scan(a, b, carry, al_ref, bl_ref, h_ref, reverse):
    t, c = a.shape
    rowm = lax.broadcasted_iota(jnp.int32, (t, c), 0) & (SUBLANES - 1)
    for d in (1, 2, 4):
        if reverse:
            keep, sh = rowm < SUBLANES - d, t - d
        else:
            keep, sh = rowm >= d, d
        a_sh = jnp.where(keep, pltpu.roll(a, sh, 0), 1.0)
        b_sh = jnp.where(keep, pltpu.roll(b, sh, 0), 0.0)
        b = a * b_sh + b
        a = a * a_sh
    al_ref[...] = a
    bl_ref[...] = b
    groups = t // SUBLANES

    def step(j, state):
        jj = groups - 1 - j if reverse else j
        off = pl.multiple_of(jj * SUBLANES, SUBLANES)
        rows = bl_ref[pl.ds(off, SUBLANES), :] + al_ref[pl.ds(off, SUBLANES), :] * state
        h_ref[pl.ds(off, SUBLANES), :] = rows
        last = rows[0:1, :] if reverse else rows[SUBLANES - 1:SUBLANES, :]
        return jnp.broadcast_to(last, (SUBLANES, c))

    out = lax.fori_loop(0, groups, step, jnp.broadcast_to(carry, (SUBLANES, c)))
    return out[0:1, :]


def _rnn_fwd_call(proj, wa, wx, ba, bx, sp, cw, cb, ts):
    s = proj.shape[0]

    def body(xg_ref, wa_ref, wx_ref, ba_ref, bx_ref, sp_ref, cw_ref, cb_ref, hr_ref, ya_ref,
             tail_sc, carry_sc, al_sc, bl_sc, h_sc):
        @pl.when(pl.program_id(0) == 0)
        def _():
            tail_sc[...] = jnp.zeros_like(tail_sc)
            carry_sc[...] = jnp.zeros_like(carry_sc)

        x = xg_ref[:, :D_RNN].astype(F32)
        g = xg_ref[:, D_RNN:].astype(F32)
        xr, _ = _conv_fwd(x, tail_sc[...], cw_ref, cb_ref)
        tail_sc[...] = x[ts - SUBLANES:, :]
        _, i, a, nrm = _lru_gates(xr, wa_ref, wx_ref, ba_ref, bx_ref, sp_ref)
        carry_sc[...] = _linear_scan(a, nrm * (i * xr), carry_sc[...], al_sc, bl_sc, h_sc, False)
        h = h_sc[...]
        hr_ref[...] = h.astype(BF)
        ya_ref[...] = (h * _gelu(g)).astype(BF)

    gw = (N_LRU_GROUPS, LRU_GROUP, LRU_GROUP)
    return pl.pallas_call(
        body, name="rnn_fwd", grid=(s // ts,),
        in_specs=[_tile_spec(ts, 2 * D_RNN), _full_spec(gw), _full_spec(gw),
                  _full_spec((1, D_RNN)), _full_spec((1, D_RNN)), _full_spec((1, D_RNN)),
                  _full_spec((CONV_WIDTH, D_RNN)), _full_spec((1, D_RNN))],
        out_specs=[_tile_spec(ts, D_RNN), _tile_spec(ts, D_RNN)],
        out_shape=[jax.ShapeDtypeStruct((s, D_RNN), BF), jax.ShapeDtypeStruct((s, D_RNN), BF)],
        scratch_shapes=[pltpu.VMEM((SUBLANES, D_RNN), F32), pltpu.VMEM((1, D_RNN), F32),
                        pltpu.VMEM((ts, D_RNN), F32), pltpu.VMEM((ts, D_RNN), F32),
                        pltpu.VMEM((ts, D_RNN), F32)],
        compiler_params=_params(("arbitrary",)),
    )(proj, wa, wx, ba, bx, sp, cw, cb)


def _layernorm_fwd(x):
    mu = jnp.mean(x, axis=-1, keepdims=True)
    xc = x - mu
    rstd = lax.rsqrt(jnp.mean(xc * xc, axis=-1, keepdims=True) + EPS)
    return xc * rstd, rstd


def _sgu_mix(vn_bf, wm_ref, bsb_ref, ts):
    rows = []
    for blk in range(ts // SGU_BLOCK):
        r0 = blk * SGU_BLOCK
        cols = [
            _dot(wm_ref[g], vn_bf[r0:r0 + SGU_BLOCK, g * SGU_BLOCK:(g + 1) * SGU_BLOCK]) + bsb_ref[g]
            for g in range(SGU_GROUPS)
        ]
        rows.append(jnp.concatenate(cols, axis=1))
    return jnp.concatenate(rows, axis=0)


def _sgu_fwd_call(proj, wm, bsb, lg, lb, ts):
    s = proj.shape[0]

    def body(uv_ref, wm_ref, bsb_ref, lg_ref, lb_ref, yb_ref):
        gu = _gelu(uv_ref[:, :D_SGU].astype(F32))
        gv = _gelu(uv_ref[:, D_SGU:2 * D_SGU].astype(F32))
        nh, _ = _layernorm_fwd(gv)
        vn = (nh * lg_ref[...] + lb_ref[...]).astype(BF)
        yb_ref[...] = (gu * _sgu_mix(vn, wm_ref, bsb_ref, ts)).astype(BF)

    sw = (SGU_GROUPS, SGU_BLOCK, SGU_BLOCK)
    return pl.pallas_call(
        body, name="sgu_fwd", grid=(s // ts,),
        in_specs=[_tile_spec(ts, 2 * D_RNN, 1), _full_spec(sw), _full_spec(sw),
                  _full_spec((1, D_SGU)), _full_spec((1, D_SGU))],
        out_specs=_tile_spec(ts, D_SGU),
        out_shape=jax.ShapeDtypeStruct((s, D_SGU), BF),
        compiler_params=_params(("parallel",)),
    )(proj, wm, bsb, lg, lb)


_GATE_COL0 = (2 * D_RNN + 2 * D_SGU) // 512


def _gate_specs(ts):
    return [_tile_spec(ts, 512, _GATE_COL0 + j) for j in range(4)]


def _merge_call(x, proj, ya_pre, yb_pre, w_ba, w_bb, w_out, g2, layer, ts):
    s = x.shape[0]

    def body(x_ref, ga0, ga1, gb0, gb1, ya_ref, yb_ref, wa_ref, wb_ref, wo_ref, g2_ref,
             x1_ref, yao_ref, ybo_ref, mg_ref, h2_ref):
        ya = _dot(ya_ref[...], wa_ref[...])
        yb = _dot(yb_ref[...], wb_ref[...])
        sa = jax.nn.sigmoid(jnp.concatenate([ga0[...], ga1[...]], axis=1).astype(F32))
        sb = jax.nn.sigmoid(jnp.concatenate([gb0[...], gb1[...]], axis=1).astype(F32))
        merged = (sa * ya + sb * yb).astype(BF)
        x1 = x_ref[...] + _dot(merged, wo_ref[...])
        x1_ref[...] = x1
        yao_ref[...] = ya.astype(BF)
        ybo_ref[...] = yb.astype(BF)
        mg_ref[...] = merged
        h2_ref[...] = (x1 * _rms_stats(x1) * g2_ref[...]).astype(BF)

    act = jax.ShapeDtypeStruct((s, D_MODEL), BF)
    return pl.pallas_call(
        body, name="merge_fwd", grid=(s // ts,),
        in_specs=[_tile_spec(ts, D_MODEL)] + _gate_specs(ts) + [
            _tile_spec(ts, D_RNN), _tile_spec(ts, D_SGU),
            _layer_spec(w_ba, layer), _layer_spec(w_bb, layer), _layer_spec(w_out, layer),
            _full_spec((1, D_MODEL))],
        out_specs=[_tile_spec(ts, D_MODEL)] * 5,
        out_shape=[jax.ShapeDtypeStruct((s, D_MODEL), F32), act, act, act, act],
        compiler_params=_params(("parallel",)),
    )(x, proj, proj, proj, proj, ya_pre, yb_pre, w_ba, w_bb, w_out, g2)


def _ffn_call(x1, h2, w_up, w_down, layer, ts):
    s = x1.shape[0]

    def body(x1_ref, h2_ref, wu_ref, wd_ref, x2_ref, p_ref):
        h2v = h2_ref[...]
        acc = x1_ref[...]
        for q in range(N_QUARTERS):
            p = _dot(h2v, wu_ref[q])
            p_ref[:, q * Q_FF:(q + 1) * Q_FF] = p.astype(BF)
            f = jnp.square(jnp.maximum(p, 0.0)).astype(BF)
            acc = acc + _dot(f, wd_ref[q * Q_FF:(q + 1) * Q_FF, :])
        x2_ref[...] = acc

    return pl.pallas_call(
        body, name="ffn_fwd", grid=(s // ts,),
        in_specs=[_tile_spec(ts, D_MODEL), _tile_spec(ts, D_MODEL),
                  pl.BlockSpec((None, N_QUARTERS, D_MODEL, Q_FF), lambda i: (layer, 0, 0, 0)),
                  pl.BlockSpec((None, D_FF, D_MODEL), lambda i: (layer, 0, 0))],
        out_specs=[_tile_spec(ts, D_MODEL), _tile_spec(ts, D_FF)],
        out_shape=[jax.ShapeDtypeStruct((s, D_MODEL), F32), jax.ShapeDtypeStruct((s, D_FF), BF)],
        compiler_params=_params(("parallel",)),
    )(x1, h2, w_up, w_down)


def _loss_call(x, target, gf, ts):
    s = x.shape[0]

    def body(x_ref, t_ref, g_ref, dx_ref, loss_ref, dg_ref):
        @pl.when(pl.program_id(0) == 0)
        def _():
            loss_ref[...] = jnp.zeros_like(loss_ref)
            dg_ref[...] = jnp.zeros_like(dg_ref)

        xv = x_ref[...]
        gv = g_ref[...]
        err = xv * _rms_stats(xv) * gv - t_ref[...]
        part = 0.5 * jnp.sum(jnp.mean(err * err, axis=-1, keepdims=True), axis=0, keepdims=True)
        loss_ref[...] += jnp.broadcast_to(part, loss_ref.shape)
        dx, dg = _rms_bwd(err * (1.0 / D_MODEL), xv, gv)
        dx_ref[...] = dx
        dg_ref[...] += _row_sum(dg)

    return pl.pallas_call(
        body, name="loss_head", grid=(s // ts,),
        in_specs=[_tile_spec(ts, D_MODEL), _tile_spec(ts, D_MODEL), _full_spec((1, D_MODEL))],
        out_specs=[_tile_spec(ts, D_MODEL), _full_spec((1, 128)), _full_spec((1, D_MODEL))],
        out_shape=[jax.ShapeDtypeStruct((s, D_MODEL), F32), jax.ShapeDtypeStruct((1, 128), F32),
                   jax.ShapeDtypeStruct((1, D_MODEL), F32)],
        compiler_params=_params(("arbitrary",)),
    )(x, target, gf)


def _ffn_bwd_call(dx2, p, x1, g2, w_up, w_down, layer, ts):
    s = dx2.shape[0]

    def body(dx2_ref, p_ref, x1_ref, g2_ref, wu_ref, wd_ref, dx1_ref, dp_ref, dg_ref):
        @pl.when(pl.program_id(0) == 0)
        def _():
            dg_ref[...] = jnp.zeros_like(dg_ref)

        dx2v = dx2_ref[...]
        dyb = dx2v.astype(BF)
        dh2 = jnp.zeros((ts, D_MODEL), F32)
        for q in range(N_QUARTERS):
            cols = slice(q * Q_FF, (q + 1) * Q_FF)
            df = _dot_nt(dyb, wd_ref[cols, :])
            dp = (df * (2.0 * jnp.maximum(p_ref[:, cols].astype(F32), 0.0))).astype(BF)
            dp_ref[:, cols] = dp
            dh2 = dh2 + _dot_nt(dp, wu_ref[q])
        dx, dg = _rms_bwd(dh2, x1_ref[...], g2_ref[...])
        dx1_ref[...] = dx2v + dx
        dg_ref[...] += _row_sum(dg)

    return pl.pallas_call(
        body, name="ffn_bwd", grid=(s // ts,),
        in_specs=[_tile_spec(ts, D_MODEL), _tile_spec(ts, D_FF), _tile_spec(ts, D_MODEL),
                  _full_spec((1, D_MODEL)),
                  pl.BlockSpec((None, N_QUARTERS, D_MODEL, Q_FF), lambda i: (layer, 0, 0, 0)),
                  pl.BlockSpec((None, D_FF, D_MODEL), lambda i: (layer, 0, 0))],
        out_specs=[_tile_spec(ts, D_MODEL), _tile_spec(ts, D_FF), _full_spec((1, D_MODEL))],
        out_shape=[jax.ShapeDtypeStruct((s, D_MODEL), F32), jax.ShapeDtypeStruct((s, D_FF), BF),
                   jax.ShapeDtypeStruct((1, D_MODEL), F32)],
        compiler_params=_params(("arbitrary",)),
    )(dx2, p, x1, g2, w_up, w_down)


def _merge_bwd_call(dx1, proj, ya, yb, w_ba, w_bb, w_out, layer, ts):
    s = dx1.shape[0]

    def body(dx1_ref, ga0, ga1, gb0, gb1, ya_ref, yb_ref, wa_ref, wb_ref, wo_ref,
             dya_ref, dyb_ref, dgate_ref, dyap_ref, dybp_ref):
        dm = _dot_nt(dx1_ref[...].astype(BF), wo_ref[...])
        sa = jax.nn.sigmoid(jnp.concatenate([ga0[...], ga1[...]], axis=1).astype(F32))
        sb = jax.nn.sigmoid(jnp.concatenate([gb0[...], gb1[...]], axis=1).astype(F32))
        dya = (dm * sa).astype(BF)
        dyb = (dm * sb).astype(BF)
        dya_ref[...] = dya
        dyb_ref[...] = dyb
        dgate_ref[:, :D_MODEL] = (dm * ya_ref[...].astype(F32) * sa * (1.0 - sa)).astype(BF)
        dgate_ref[:, D_MODEL:] = (dm * yb_ref[...].astype(F32) * sb * (1.0 - sb)).astype(BF)
        dyap_ref[...] = _dot_nt(dya, wa_ref[...]).astype(BF)
        dybp_ref[...] = _dot_nt(dyb, wb_ref[...]).astype(BF)

    act = jax.ShapeDtypeStruct((s, D_MODEL), BF)
    return pl.pallas_call(
        body, name="merge_bwd", grid=(s // ts,),
        in_specs=[_tile_spec(ts, D_MODEL)] + _gate_specs(ts) + [
            _tile_spec(ts, D_MODEL), _tile_spec(ts, D_MODEL),
            _layer_spec(w_ba, layer), _layer_spec(w_bb, layer), _layer_spec(w_out, layer)],
        out_specs=[_tile_spec(ts, D_MODEL), _tile_spec(ts, D_MODEL), _tile_spec(ts, 2 * D_MODEL),
                   _tile_spec(ts, D_RNN), _tile_spec(ts, D_SGU)],
        out_shape=[act, act, jax.ShapeDtypeStruct((s, 2 * D_MODEL), BF),
                   jax.ShapeDtypeStruct((s, D_RNN), BF), jax.ShapeDtypeStruct((s, D_SGU), BF)],
        compiler_params=_params(("parallel",)),
    )(dx1, proj, proj, proj, proj, ya, yb, w_ba, w_bb, w_out)


def _sgu_bwd_call(dyb_pre, proj, wm, bsb, mask, lg, lb, ts):
    s = proj.shape[0]

    def body(dy_ref, uv_ref, wm_ref, bsb_ref, mask_ref, lg_ref, lb_ref,
             duv_ref, dws_ref, dbs_ref, dlg_ref, dlb_ref, dm_sc):
        step = pl.program_id(0)

        @pl.when(step == 0)
        def _():
            dws_ref[...] = jnp.zeros_like(dws_ref)
            dlg_ref[...] = jnp.zeros_like(dlg_ref)
            dlb_ref[...] = jnp.zeros_like(dlb_ref)
            dm_sc[...] = jnp.zeros_like(dm_sc)

        gu, dgu_du = _gelu_and_grad(uv_ref[:, :D_SGU].astype(F32))
        gv, dgv_dv = _gelu_and_grad(uv_ref[:, D_SGU:2 * D_SGU].astype(F32))
        nh, rstd = _layernorm_fwd(gv)
        lgv = lg_ref[...]
        vn = (nh * lgv + lb_ref[...]).astype(BF)
        dy = dy_ref[...].astype(F32)
        du = dy * _sgu_mix(vn, wm_ref, bsb_ref, ts) * dgu_du
        dmix = dy * gu
        dmix_bf = dmix.astype(BF)
        dm_acc = dm_sc[...]
        rows = []
        for blk in range(ts // SGU_BLOCK):
            r0 = blk * SGU_BLOCK
            dm_acc = dm_acc + dmix[r0:r0 + SGU_BLOCK, :]
            cols = []
            for g in range(SGU_GROUPS):
                c0 = g * SGU_BLOCK
                dmg = dmix_bf[r0:r0 + SGU_BLOCK, c0:c0 + SGU_BLOCK]
                cols.append(_dot_tn(wm_ref[g], dmg))
                dws_ref[g] += mask_ref[...] * _dot_nt(dmg, vn[r0:r0 + SGU_BLOCK, c0:c0 + SGU_BLOCK])
            rows.append(jnp.concatenate(cols, axis=1))
        dm_sc[...] = dm_acc
        dvn = jnp.concatenate(rows, axis=0)
        dlg_ref[...] += _row_sum(dvn * nh)
        dlb_ref[...] += _row_sum(dvn)
        dnh = dvn * lgv
        dgv = rstd * (dnh - jnp.mean(dnh, axis=-1, keepdims=True)
                      - nh * jnp.mean(dnh * nh, axis=-1, keepdims=True))
        duv_ref[:, :D_SGU] = du.astype(BF)
        duv_ref[:, D_SGU:] = (dgv * dgv_dv).astype(BF)

        @pl.when(step == pl.num_programs(0) - 1)
        def _():
            for g in range(SGU_GROUPS):
                dbs_ref[:, g:g + 1] = jnp.sum(
                    dm_acc[:, g * SGU_BLOCK:(g + 1) * SGU_BLOCK], axis=1, keepdims=True)

    sw = (SGU_GROUPS, SGU_BLOCK, SGU_BLOCK)
    return pl.pallas_call(
        body, name="sgu_bwd", grid=(s // ts,),
        in_specs=[_tile_spec(ts, D_SGU), _tile_spec(ts, 2 * D_RNN, 1), _full_spec(sw), _full_spec(sw),
                  _full_spec((SGU_BLOCK, SGU_BLOCK)), _full_spec((1, D_SGU)), _full_spec((1, D_SGU))],
        out_specs=[_tile_spec(ts, 2 * D_SGU), _full_spec(sw), _full_spec((SGU_BLOCK, SGU_GROUPS)),
                   _full_spec((1, D_SGU)), _full_spec((1, D_SGU))],
        out_shape=[jax.ShapeDtypeStruct((s, 2 * D_SGU), BF), jax.ShapeDtypeStruct(sw, F32),
                   jax.ShapeDtypeStruct((SGU_BLOCK, SGU_GROUPS), F32),
                   jax.ShapeDtypeStruct((1, D_SGU), F32), jax.ShapeDtypeStruct((1, D_SGU), F32)],
        scratch_shapes=[pltpu.VMEM((SGU_BLOCK, D_SGU), F32)],
        compiler_params=_params(("arbitrary",)),
    )(dyb_pre, proj, wm, bsb, mask, lg, lb)


_ROW_DBA, _ROW_DBX, _ROW_DSP, _ROW_DCB, _ROW_DCW = 0, 1, 2, 3, 4
_PREV_ROWS = 16


def _rnn_bwd_call(dya_pre, proj, hr, wa, wx, ba, bx, sp, cw, cb, ts):
    s = proj.shape[0]
    nt = s // ts
    per = ts // _PREV_ROWS

    def tile(i):
        return nt - 1 - i

    def prev(i):
        return jnp.maximum(tile(i) * per - 1, 0)

    def body(dy_ref, xg_ref, xgp_ref, hr_ref, hrp_ref, wa_ref, wx_ref, ba_ref, bx_ref, sp_ref,
             cw_ref, cb_ref, dxg_ref, dwa_ref, dwx_ref, vec_ref,
             lam_carry, a_first, dxr_head, al_sc, bl_sc, lam_sc):
        step = pl.program_id(0)

        @pl.when(step == 0)
        def _():
            dwa_ref[...] = jnp.zeros_like(dwa_ref)
            dwx_ref[...] = jnp.zeros_like(dwx_ref)
            vec_ref[...] = jnp.zeros_like(vec_ref)
            lam_carry[...] = jnp.zeros_like(lam_carry)
            a_first[...] = jnp.zeros_like(a_first)
            dxr_head[...] = jnp.zeros_like(dxr_head)

        has_prev = (step < nt - 1).astype(F32)
        x = xg_ref[:, :D_RNN].astype(F32)
        g = xg_ref[:, D_RNN:].astype(F32)
        x_tail = xgp_ref[_PREV_ROWS - SUBLANES:, :D_RNN].astype(F32) * has_prev
        h_tail = hrp_ref[_PREV_ROWS - SUBLANES:, :].astype(F32) * has_prev
        xr, x_shifted = _conv_fwd(x, x_tail, cw_ref, cb_ref)
        r, i, a, nrm = _lru_gates(xr, wa_ref, wx_ref, ba_ref, bx_ref, sp_ref)
        h = hr_ref[...].astype(F32)
        dy = dy_ref[...].astype(F32)
        gg, dgg = _gelu_and_grad(g)

        coef = _shift_up(a, jnp.broadcast_to(a_first[...], (SUBLANES, D_RNN)), 1)
        lam_carry[...] = _linear_scan(coef, dy * gg, lam_carry[...], al_sc, bl_sc, lam_sc, True)
        a_first[...] = a[0:1, :]
        lam = lam_sc[...]

        da = lam * _shift_down(h, h_tail, 1)
        dnrm = lam * (i * xr)
        di = lam * nrm * xr
        dlog_a = da * a - dnrm * (a * a) / nrm
        spv = sp_ref[...]
        dza = (dlog_a * (-LRU_C * spv)) * (r * (1.0 - r))
        dzx = di * (i * (1.0 - i))
        vec_ref[_ROW_DSP:_ROW_DSP + 1, :] += _row_sum(dlog_a * (-LRU_C * r))
        vec_ref[_ROW_DBA:_ROW_DBA + 1, :] += _row_sum(dza)
        vec_ref[_ROW_DBX:_ROW_DBX + 1, :] += _row_sum(dzx)
        xb = xr.astype(BF)
        dza_bf = dza.astype(BF)
        dzx_bf = dzx.astype(BF)
        for grp in range(N_LRU_GROUPS):
            cols = slice(grp * LRU_GROUP, (grp + 1) * LRU_GROUP)
            dwa_ref[grp] += _dot_tn(xb[:, cols], dza_bf[:, cols])
            dwx_ref[grp] += _dot_tn(xb[:, cols], dzx_bf[:, cols])
        dxr = (lam * nrm * i + _group_dot(dza_bf, wa_ref, _dot_nt) + _group_dot(dzx_bf, wx_ref, _dot_nt))

        vec_ref[_ROW_DCB:_ROW_DCB + 1, :] += _row_sum(dxr)
        head = dxr_head[...]
        dx = cw_ref[CONV_WIDTH - 1:CONV_WIDTH, :] * dxr
        vec_ref[_ROW_DCW + 3:_ROW_DCW + 4, :] += _row_sum(dxr * x)
        for sft in range(1, CONV_WIDTH):
            k = CONV_WIDTH - 1 - sft
            dx = dx + cw_ref[k:k + 1, :] * _shift_up(dxr, head, sft)
            vec_ref[_ROW_DCW + k:_ROW_DCW + k + 1, :] += _row_sum(dxr * x_shifted[sft])
        dxr_head[...] = dxr[0:SUBLANES, :]
        dxg_ref[:, :D_RNN] = dx.astype(BF)
        dxg_ref[:, D_RNN:] = (dy * h * dgg).astype(BF)

    gw = (N_LRU_GROUPS, LRU_GROUP, LRU_GROUP)
    rev = lambda width: pl.BlockSpec((ts, width), lambda i: (tile(i), 0))
    return pl.pallas_call(
        body, name="rnn_bwd", grid=(nt,),
        in_specs=[rev(D_RNN), rev(2 * D_RNN),
                  pl.BlockSpec((_PREV_ROWS, 2 * D_RNN), lambda i: (prev(i), 0)),
                  rev(D_RNN),
                  pl.BlockSpec((_PREV_ROWS, D_RNN), lambda i: (prev(i), 0)),
                  _full_spec(gw), _full_spec(gw),
                  _full_spec((1, D_RNN)), _full_spec((1, D_RNN)), _full_spec((1, D_RNN)),
                  _full_spec((CONV_WIDTH, D_RNN)), _full_spec((1, D_RNN))],
        out_specs=[rev(2 * D_RNN), _full_spec(gw), _full_spec(gw), _full_spec((SUBLANES, D_RNN))],
        out_shape=[jax.ShapeDtypeStruct((s, 2 * D_RNN), BF), jax.ShapeDtypeStruct(gw, F32),
                   jax.ShapeDtypeStruct(gw, F32), jax.ShapeDtypeStruct((SUBLANES, D_RNN), F32)],
        scratch_shapes=[pltpu.VMEM((1, D_RNN), F32), pltpu.VMEM((1, D_RNN), F32),
                        pltpu.VMEM((SUBLANES, D_RNN), F32),
                        pltpu.VMEM((ts, D_RNN), F32), pltpu.VMEM((ts, D_RNN), F32),
                        pltpu.VMEM((ts, D_RNN), F32)],
        compiler_params=_params(("arbitrary",)),
    )(dya_pre, proj, proj, hr, hr, wa, wx, ba, bx, sp, cw, cb)


def _inproj_bwd_call(dxg, duv, dgate, dx1, x, g1, w_in, layer, ts):
    s = x.shape[0]

    def body(dxg_ref, duv_ref, dgt_ref, dx1_ref, x_ref, g_ref, w_ref, dx_ref, dproj_ref, dg_ref):
        @pl.when(pl.program_id(0) == 0)
        def _():
            dg_ref[...] = jnp.zeros_like(dg_ref)

        dproj = jnp.concatenate([dxg_ref[...], duv_ref[...], dgt_ref[...]], axis=1)
        dproj_ref[...] = dproj
        dh = jnp.zeros((ts, D_MODEL), F32)
        for q in range(N_QUARTERS):
            dh = dh + _dot_nt(dproj[:, q * Q_IN:(q + 1) * Q_IN], w_ref[q])
        dx, dg = _rms_bwd(dh, x_ref[...], g_ref[...])
        dx_ref[...] = dx1_ref[...] + dx
        dg_ref[...] += _row_sum(dg)

    return pl.pallas_call(
        body, name="inproj_bwd", grid=(s // ts,),
        in_specs=[_tile_spec(ts, 2 * D_RNN), _tile_spec(ts, 2 * D_SGU), _tile_spec(ts, 2 * D_MODEL),
                  _tile_spec(ts, D_MODEL), _tile_spec(ts, D_MODEL), _full_spec((1, D_MODEL)),
                  pl.BlockSpec((None, N_QUARTERS, D_MODEL, Q_IN), lambda i: (layer, 0, 0, 0))],
        out_specs=[_tile_spec(ts, D_MODEL), _tile_spec(ts, D_IN), _full_spec((1, D_MODEL))],
        out_shape=[jax.ShapeDtypeStruct((s, D_MODEL), F32), jax.ShapeDtypeStruct((s, D_IN), BF),
                   jax.ShapeDtypeStruct((1, D_MODEL), F32)],
        compiler_params=_params(("arbitrary",)),
    )(dxg, duv, dgate, dx1, x, g1, w_in)


def _relu_sq(p):
    return jnp.square(jnp.maximum(p.astype(F32), 0.0))


def _wgrad_call(a, b, tm, tn, tk, col_blocked, name, a_fn=None):
    s, m = a.shape
    n = b.shape[1]

    def body(a_ref, b_ref, o_ref):
        av = a_ref[...]
        if a_fn is not None:
            av = a_fn(av)
        prod = _dot_tn(av.astype(BF), b_ref[...].astype(BF))

        @pl.when(pl.program_id(2) == 0)
        def _():
            o_ref[...] = prod

        @pl.when(pl.program_id(2) > 0)
        def _():
            o_ref[...] += prod

    if col_blocked:
        per_q = n // N_QUARTERS // tn
        out_spec = pl.BlockSpec((None, tm, tn), lambda i, j, k: (j // per_q, 0, j % per_q))
        out_shape = jax.ShapeDtypeStruct((N_QUARTERS, m, n // N_QUARTERS), F32)
    else:
        out_spec = pl.BlockSpec((tm, tn), lambda i, j, k: (i, j))
        out_shape = jax.ShapeDtypeStruct((m, n), F32)
    return pl.pallas_call(
        body, name=name, grid=(m // tm, n // tn, s // tk),
        in_specs=[pl.BlockSpec((tk, tm), lambda i, j, k: (k, i)),
                  pl.BlockSpec((tk, tn), lambda i, j, k: (k, j))],
        out_specs=out_spec, out_shape=out_shape,
        compiler_params=_params(("parallel", "parallel", "arbitrary")),
    )(a, b)


BIG = ("w_in", "w_up", "w_down", "w_branch_a", "w_branch_b", "w_out")


def _block_diag(w):
    w4 = w.reshape(N_LRU_GROUPS, HEADS_PER_GROUP, RNN_HEAD_DIM, RNN_HEAD_DIM)
    eye = jnp.eye(HEADS_PER_GROUP, dtype=w.dtype)
    return jnp.einsum("gjio,jk->gjiko", w4, eye).reshape(N_LRU_GROUPS, LRU_GROUP, LRU_GROUP)


def _block_diag_extract(d):
    d5 = d.reshape(N_LRU_GROUPS, HEADS_PER_GROUP, RNN_HEAD_DIM, HEADS_PER_GROUP, RNN_HEAD_DIM)
    blocks = [d5[:, j, :, j, :] for j in range(HEADS_PER_GROUP)]
    return jnp.stack(blocks, axis=1).reshape(RNN_HEADS, RNN_HEAD_DIM, RNN_HEAD_DIM)


def _sgu_mask():
    chunk = jnp.arange(SGU_BLOCK) // CHUNK
    return (chunk[:, None] >= chunk[None, :]).astype(F32)


def _layer_small(sm, l):
    row = lambda v: v.reshape(1, -1)
    return dict(
        g1=row(sm["norm_mix_g"][l]), g2=row(sm["norm_ffn_g"][l]),
        wa=_block_diag(sm["lru_w_a"][l]).astype(BF), wx=_block_diag(sm["lru_w_x"][l]).astype(BF),
        ba=row(sm["lru_b_a"][l]), bx=row(sm["lru_b_x"][l]),
        sp=row(jax.nn.softplus(-sm["lru_lambda"][l])),
        cw=sm["conv_w"][l], cb=row(sm["conv_b"][l]),
        wm=(sm["sgu_w_s"][l] * _sgu_mask()).astype(BF),
        bsb=jnp.broadcast_to(sm["sgu_b_s"][l][:, :, None], (SGU_GROUPS, SGU_BLOCK, SGU_BLOCK)),
        lg=row(sm["sgu_ln_g"][l]), lb=row(sm["sgu_ln_b"][l]),
    )


def _layer_fwd_mix(x, big, p, ts):
    h = _norm_call(x, p["g1"], ts)
    proj = _inproj_call(h, big["w_in"], 0, ts)
    hr, ya_pre = _rnn_fwd_call(proj, p["wa"], p["wx"], p["ba"], p["bx"], p["sp"], p["cw"], p["cb"], ts)
    yb_pre = _sgu_fwd_call(proj, p["wm"], p["bsb"], p["lg"], p["lb"], ts)
    return dict(p=p, x=x, h=h, proj=proj, hr=hr, ya_pre=ya_pre, yb_pre=yb_pre)


def _layer_fwd_out(sv, big, ts):
    x1, ya, yb, merged, h2 = _merge_call(sv["x"], sv["proj"], sv["ya_pre"], sv["yb_pre"], big["w_branch_a"],
                                         big["w_branch_b"], big["w_out"], sv["p"]["g2"], 0, ts)
    x2, pre = _ffn_call(x1, h2, big["w_up"], big["w_down"], 0, ts)
    sv.update(x1=x1, ya=ya, yb=yb, merged=merged, h2=h2, pre=pre)
    return x2


def _layer_bwd_ffn(dx, sv, big, ts):
    p = sv["p"]
    dx1, dpre, dg2 = _ffn_bwd_call(dx, sv["pre"], sv["x1"], p["g2"], big["w_up"], big["w_down"], 0, ts)
    tk = dx.shape[0]
    gb = dict(
        w_down=_wgrad_call(sv["pre"], dx, Q_FF, D_MODEL // 2, tk, False, "wgrad_down", a_fn=_relu_sq),
        w_up=_wgrad_call(sv["h2"], dpre, D_MODEL, Q_FF, tk, True, "wgrad_up"))
    return dx1, gb, dict(norm_ffn_g=dg2[0])


def _layer_bwd_mix(dx1, sv, big, lam, ts):
    p = sv["p"]
    tk = dx1.shape[0]
    dya, dyb, dgate, dya_pre, dyb_pre = _merge_bwd_call(
        dx1, sv["proj"], sv["ya"], sv["yb"], big["w_branch_a"], big["w_branch_b"], big["w_out"], 0, ts)
    gb = dict(
        w_out=_wgrad_call(sv["merged"], dx1, D_MODEL, D_MODEL // 2, tk, False, "wgrad_out"),
        w_branch_a=_wgrad_call(sv["ya_pre"], dya, D_RNN, D_MODEL // 2, tk, False, "wgrad_branch_a"),
        w_branch_b=_wgrad_call(sv["yb_pre"], dyb, D_SGU, D_MODEL // 2, tk, False, "wgrad_branch_b"))
    duv, dws, dbs, dlg, dlb = _sgu_bwd_call(dyb_pre, sv["proj"], p["wm"], p["bsb"], _sgu_mask(), p["lg"], p["lb"],
                                            ts)
    dxg, dwa, dwx, vec = _rnn_bwd_call(dya_pre, sv["proj"], sv["hr"], p["wa"], p["wx"], p["ba"], p["bx"],
                                       p["sp"], p["cw"], p["cb"], ts // 2)
    dx, dproj, dg1 = _inproj_bwd_call(dxg, duv, dgate, dx1, sv["x"], p["g1"], big["w_in"], 0, ts)
    gb["w_in"] = _wgrad_call(sv["h"], dproj, D_MODEL, Q_IN, tk // 2, True, "wgrad_in")
    gs = dict(
        norm_mix_g=dg1[0], conv_w=vec[_ROW_DCW:_ROW_DCW + CONV_WIDTH], conv_b=vec[_ROW_DCB],
        lru_w_a=_block_diag_extract(dwa), lru_w_x=_block_diag_extract(dwx),
        lru_b_a=vec[_ROW_DBA].reshape(RNN_HEADS, RNN_HEAD_DIM), lru_b_x=vec[_ROW_DBX].reshape(RNN_HEADS, RNN_HEAD_DIM),
        lru_lambda=-vec[_ROW_DSP] * jax.nn.sigmoid(-lam),
        sgu_ln_g=dlg[0], sgu_ln_b=dlb[0], sgu_w_s=dws, sgu_b_s=dbs.T)
    return dx, gb, gs


def _local_step(x, target, big, sm, ts):
    saved = []
    for l in range(DEPTH):
        sv = _layer_fwd_mix(x, big[l], _layer_small(sm, l), ts)
        x = _layer_fwd_out(sv, big[l], ts)
        saved.append(sv)
    dx, loss, dgf = _loss_call(x, target, sm["final_norm_g"].reshape(1, -1), ts)
    gb, gs = [None] * DEPTH, [None] * DEPTH
    for l in reversed(range(DEPTH)):
        dx1, gb_ffn, gs_ffn = _layer_bwd_ffn(dx, saved[l], big[l], ts)
        dx, gb_mix, gs_mix = _layer_bwd_mix(dx1, saved[l], big[l], sm["lru_lambda"][l], ts)
        gb[l] = {**gb_ffn, **gb_mix}
        gs[l] = {**gs_ffn, **gs_mix}
    gs = {k: jnp.stack([g[k] for g in gs]) for k in gs[0]}
    gs["final_norm_g"] = dgf[0]
    return loss, dx, gb, gs


EW_BLOCK_ELEMS = 384 * 1024


def _row_block(rows, cols):
    for br in range(min(rows, EW_BLOCK_ELEMS // cols), 0, -1):
        if rows % br == 0 and br % 16 == 0:
            return br
    return rows


def _ew_call(fn, name, operands, outputs, slabs=1, sel=None, into=None):
    rows, cols = outputs[0][0].shape[2:]
    br = _row_block(rows, cols)
    n_in = len(operands)

    def pick(tok, g, s):
        if callable(tok):
            return tok(g, s)
        if tok == "g":
            return g
        if isinstance(tok, tuple):
            return s[tok[1]]
        return tok

    def spec(idx):
        return pl.BlockSpec((None, None, br, cols),
                            lambda g, i, s, idx=idx: (pick(idx[0], g, s), pick(idx[1], g, s), i, 0))

    if sel is None:
        sel = jnp.zeros((1,), jnp.int32)
    in_specs = [spec(idx) for _, idx in operands]
    arrays = [a for a, _ in operands]
    aliases = {}
    if into is not None:
        in_specs.append(pl.BlockSpec(memory_space=pl.ANY))
        arrays.append(into)
        aliases = {1 + n_in: 0}

    def body(sel_ref, *refs):
        outs = fn(*[r[...] for r in refs[:n_in]])
        for o_ref, o in zip(refs[len(arrays):], outs):
            o_ref[...] = o.astype(o_ref.dtype)

    return pl.pallas_call(
        body, name=name, out_shape=[s for s, _ in outputs],
        grid_spec=pltpu.PrefetchScalarGridSpec(
            num_scalar_prefetch=1, grid=(slabs, rows // br),
            in_specs=in_specs,
            out_specs=[spec(idx) for _, idx in outputs]),
        input_output_aliases=aliases,
        compiler_params=_params(("parallel", "parallel")),
    )(sel, *arrays)


def _as4(a):
    return a.reshape((1,) * (4 - a.ndim) + a.shape)


def _adamw(w, g, m, v):
    m = ADAM_B1 * m + (1.0 - ADAM_B1) * g
    v = ADAM_B2 * v + (1.0 - ADAM_B2) * jnp.square(g)
    m_hat = m / (1.0 - ADAM_B1 ** ADAM_STEP)
    v_hat = v / (1.0 - ADAM_B2 ** ADAM_STEP)
    delta = -ADAM_LR * (m_hat / (jnp.sqrt(v_hat) + ADAM_EPS) + ADAM_WD * w)
    return delta, m, v


def _small_adamw_call(ws, gs, ms, vs):
    n = len(ws)

    def body(*refs):
        for k in range(n):
            w, g, m, v = (refs[j * n + k][...] for j in range(4))
            outs = _adamw(w, g, m, v)
            for j in range(3):
                refs[(4 + j) * n + k][...] = outs[j]

    shapes = [jax.ShapeDtypeStruct(w.shape, F32) for w in ws]
    outs = pl.pallas_call(
        body, name="adamw_small", out_shape=shapes * 3,
        in_specs=[pl.BlockSpec(memory_space=pltpu.VMEM)] * (4 * n),
        out_specs=[pl.BlockSpec(memory_space=pltpu.VMEM)] * (3 * n),
        compiler_params=_params(),
    )(*ws, *gs, *ms, *vs)
    return outs[:n], outs[n:2 * n], outs[2 * n:]


ANY = pl.BlockSpec(memory_space=pl.ANY)


def _place():
    x, y, c = lax.axis_index("x"), lax.axis_index("y"), lax.axis_index("c")
    chips = [(1 - x, y), (x, 1 - y), (1 - x, 1 - y)]
    return x, y, c, chips


def _remote(src, dst, send_sem, recv_sem, to):
    return pltpu.make_async_remote_copy(src_ref=src, dst_ref=dst, send_sem=send_sem, recv_sem=recv_sem,
                                        device_id=to, device_id_type=MESH)


def _gather_call(bufs):
    n = len(bufs)

    def body(*refs):
        out = refs[n:2 * n]
        send_sems, recv_sems = refs[2 * n:]
        x, y, c, chips = _place()
        me_q = 2 * x + y
        sibling = (x, y, 1 - c)
        first = []
        for w in range(n):
            for j, chip in enumerate(chips):
                mine = out[w].at[c, me_q]
                first.append(_remote(mine, mine, send_sems.at[w * 3 + j], recv_sems.at[w * 3 + j], (*chip, c)))
        for cp in first:
            cp.start()
        passed = []
        for w in range(n):
            for j, (qx, qy) in enumerate(chips):
                landed = out[w].at[c, 2 * qx + qy]
                k = w * 3 + j
                _remote(landed, landed, send_sems.at[k], recv_sems.at[k], (qx, qy, c)).wait_recv()
                cp = _remote(landed, landed, send_sems.at[3 * n + k], recv_sems.at[3 * n + k], sibling)
                cp.start()
                passed.append(cp)
        for w in range(n):
            for j, (qx, qy) in enumerate(chips):
                landed = out[w].at[1 - c, 2 * qx + qy]
                k = 3 * n + w * 3 + j
                _remote(landed, landed, send_sems.at[k], recv_sems.at[k], sibling).wait_recv()
        for cp in first + passed:
            cp.wait_send()

    return pl.pallas_call(
        body, name="gather_weights",
        out_shape=[jax.ShapeDtypeStruct(a.shape, a.dtype) for a in bufs],
        in_specs=[ANY] * n, out_specs=[ANY] * n,
        input_output_aliases={w: w for w in range(n)},
        scratch_shapes=[pltpu.SemaphoreType.DMA((6 * n,)), pltpu.SemaphoreType.DMA((6 * n,))],
        compiler_params=_params(vmem=False, has_side_effects=True),
    )(*bufs)


def _sibling_send_call(items):
    n = len(items)

    def body(*refs):
        src, out = refs[:n], refs[n:2 * n]
        send_sems, recv_sems = refs[2 * n:]
        x, y, c, _ = _place()
        copies = [_remote(src[w], out[w], send_sems.at[w], recv_sems.at[w], (x, y, 1 - c)) for w in range(n)]
        for cp in copies:
            cp.start()
        for cp in copies:
            cp.wait()

    return pl.pallas_call(
        body, name="grads_to_sibling",
        out_shape=[jax.ShapeDtypeStruct(a.shape, a.dtype) for a in items],
        in_specs=[ANY] * n, out_specs=[ANY] * n,
        scratch_shapes=[pltpu.SemaphoreType.DMA((n,)), pltpu.SemaphoreType.DMA((n,))],
        compiler_params=_params(vmem=False, has_side_effects=True),
    )(*items)


def _sibling_inplace_call(name, bufs, slabs, per_buf):
    n = len(bufs)

    def body(*refs):
        out = refs[n:2 * n]
        send_sems, recv_sems = refs[2 * n:]
        x, y, c, _ = _place()
        sibling = (x, y, 1 - c)
        pairs = [pair for ref in out for pair in slabs(ref, c)]
        sends = [_remote(s, s, send_sems.at[k], recv_sems.at[k], sibling) for k, (s, _) in enumerate(pairs)]
        for cp in sends:
            cp.start()
        for k, (_, r) in enumerate(pairs):
            _remote(r, r, send_sems.at[k], recv_sems.at[k], sibling).wait_recv()
        for cp in sends:
            cp.wait_send()

    return pl.pallas_call(
        body, name=name,
        out_shape=[jax.ShapeDtypeStruct(a.shape, a.dtype) for a in bufs],
        in_specs=[ANY] * n, out_specs=[ANY] * n,
        input_output_aliases={w: w for w in range(n)},
        scratch_shapes=[pltpu.SemaphoreType.DMA((per_buf * n,)), pltpu.SemaphoreType.DMA((per_buf * n,))],
        compiler_params=_params(vmem=False, has_side_effects=True),
    )(*bufs)


HBM_SPEC = pl.BlockSpec(memory_space=pltpu.HBM)
SEM_SPEC = pl.BlockSpec(memory_space=pltpu.SEMAPHORE)
DATAFLOW_EFFECT = pltpu.SideEffectType.DATAFLOW_SIDE_EFFECTING


def _exchange_start(name, bufs, copies, n_copies, after):
    n = len(bufs)

    def body(*refs):
        ins, send_sems, recv_sems, token = refs[:n], refs[n + 1], refs[n + 2], refs[-1]
        for k, (src, dst, to) in enumerate(copies(ins)):
            _remote(src, dst, send_sems.at[k], recv_sems.at[k], to).start()
        token[...] = jnp.zeros_like(token)

    outs = pl.pallas_call(
        body, name=name,
        out_shape=(pltpu.SemaphoreType.DMA((n_copies,)), pltpu.SemaphoreType.DMA((n_copies,)),
                   *[pltpu.HBM(b.shape, b.dtype) for b in bufs], jax.ShapeDtypeStruct((SUBLANES, 128), F32)),
        in_specs=[HBM_SPEC] * n + [ANY],
        out_specs=(SEM_SPEC, SEM_SPEC, *[HBM_SPEC] * n, pl.BlockSpec(memory_space=pltpu.VMEM)),
        input_output_aliases={w: w + 2 for w in range(n)},
        compiler_params=pltpu.CompilerParams(has_side_effects=DATAFLOW_EFFECT),
    )(*[pltpu.with_memory_space_constraint(b, pltpu.HBM) for b in bufs], after)
    return outs[0], outs[1], list(outs[2:2 + n]), outs[-1]


def _exchange_wait(name, send_sems, recv_sems, bufs, copies, after):
    n = len(bufs)

    def body(*refs):
        ins, send_sems, recv_sems = refs[:n], refs[n], refs[n + 1]
        for k, (src, dst, to) in enumerate(copies(ins)):
            cp = _remote(src, dst, send_sems.at[k], recv_sems.at[k], to)
            cp.wait_send()
            cp.wait_recv()

    return pl.pallas_call(
        body, name=name,
        out_shape=[pltpu.HBM(b.shape, b.dtype) for b in bufs],
        in_specs=[HBM_SPEC] * n + [SEM_SPEC, SEM_SPEC, ANY],
        out_specs=[HBM_SPEC] * n,
        input_output_aliases={w: w for w in range(n)},
        compiler_params=pltpu.CompilerParams(has_side_effects=DATAFLOW_EFFECT),
    )(*bufs, send_sems, recv_sems, after)


def _gather_copies(refs):
    x, y, c, chips = _place()
    mine = 2 * (2 * x + y) + c
    return [(ref.at[mine], ref.at[mine], (qx, qy, c)) for ref in refs for qx, qy in chips]


def _gather_forward_slabs(ref, c):
    x, y, _, chips = _place()
    return [(ref.at[2 * (2 * qx + qy) + c], ref.at[2 * (2 * qx + qy) + 1 - c]) for qx, qy in chips]


def _owner_copies(refs):
    n = len(refs) // 2
    x, y, c, chips = _place()
    return [(refs[w].at[2 * qx + qy], refs[n + w].at[j], (qx, qy, c))
            for w in range(n) for j, (qx, qy) in enumerate(chips)]


N_DEVICES = 8
SMALL_ROWS = 616


def _small_allreduce_call(buf):
    def body(in_ref, out_ref, recv_ref, red_ref, send_sems, recv_sems):
        x, y, c, _ = _place()
        me = 4 * x + 2 * y + c

        def peer(k):
            return (x ^ ((k >> 2) & 1), y ^ ((k >> 1) & 1), c ^ (k & 1))

        scatter = [_remote(in_ref.at[me ^ k], recv_ref.at[me], send_sems.at[k - 1], recv_sems.at[k - 1], peer(k))
                   for k in range(1, N_DEVICES)]
        for cp in scatter:
            cp.start()
        recv_ref[me] = in_ref[me]
        for k in range(1, N_DEVICES):
            landed = recv_ref.at[me ^ k]
            _remote(landed, landed, send_sems.at[k - 1], recv_sems.at[k - 1], peer(k)).wait_recv()
        total = recv_ref[0]
        for j in range(1, N_DEVICES):
            total = total + recv_ref[j]
        red_ref[...] = total
        out_ref[me] = total
        spread = [_remote(red_ref, out_ref.at[me], send_sems.at[6 + k], recv_sems.at[6 + k], peer(k))
                  for k in range(1, N_DEVICES)]
        for cp in spread:
            cp.start()
        for k in range(1, N_DEVICES):
            landed = out_ref.at[me ^ k]
            _remote(landed, landed, send_sems.at[6 + k], recv_sems.at[6 + k], peer(k)).wait_recv()
        for cp in scatter + spread:
            cp.wait_send()

    shape = (N_DEVICES, SMALL_ROWS, 128)
    return pl.pallas_call(
        body, name="allreduce_small",
        out_shape=jax.ShapeDtypeStruct(shape, F32),
        in_specs=[pl.BlockSpec(memory_space=pltpu.VMEM)],
        out_specs=pl.BlockSpec(memory_space=pltpu.VMEM),
        scratch_shapes=[pltpu.VMEM(shape, F32), pltpu.VMEM(shape[1:], F32),
                        pltpu.SemaphoreType.DMA((2 * (N_DEVICES - 1),)),
                        pltpu.SemaphoreType.DMA((2 * (N_DEVICES - 1),))],
        compiler_params=_params(has_side_effects=True),
    )(buf)


SMALL = ("norm_mix_g", "conv_w", "conv_b", "lru_w_a", "lru_b_a", "lru_w_x", "lru_b_x", "lru_lambda",
         "sgu_ln_g", "sgu_ln_b", "sgu_w_s", "sgu_b_s", "norm_ffn_g", "final_norm_g")
WEIGHTS = ("norm_mix_g", "w_in", "conv_w", "conv_b", "lru_w_a", "lru_b_a", "lru_w_x", "lru_b_x", "lru_lambda",
           "sgu_ln_g", "sgu_ln_b", "sgu_w_s", "sgu_b_s", "w_branch_a", "w_branch_b", "w_out", "norm_ffn_g",
           "w_up", "w_down", "final_norm_g")
PACK_ALIGN = SUBLANES * 128


def _pack_small(gs):
    parts = []
    for k in SMALL:
        flat = gs[k].reshape(-1)
        parts.append(jnp.pad(flat, (0, -flat.size % PACK_ALIGN)))
    flat = jnp.concatenate(parts)
    flat = jnp.pad(flat, (0, N_DEVICES * SMALL_ROWS * 128 - flat.size))
    return flat.reshape(N_DEVICES, SMALL_ROWS, 128)


def _unpack_small(buf, like):
    flat = buf.reshape(-1)
    out, off = {}, 0
    for k in SMALL:
        size = like[k].size
        out[k] = flat[off:off + size].reshape(like[k].shape)
        off += size + (-size % PACK_ALIGN)
    return out


def _as_rows(a):
    return a.reshape(-1, a.shape[-1])


def kernel(x, norm_mix_g, w_in, conv_w, conv_b, lru_w_a, lru_b_a, lru_w_x, lru_b_x, lru_lambda, sgu_ln_g, sgu_ln_b, sgu_w_s, sgu_b_s, w_branch_a, w_branch_b, w_out, norm_ffn_g, w_up, w_down, final_norm_g, loss_target, m_norm_mix_g, m_w_in, m_conv_w, m_conv_b, m_lru_w_a, m_lru_b_a, m_lru_w_x, m_lru_b_x, m_lru_lambda, m_sgu_ln_g, m_sgu_ln_b, m_sgu_w_s, m_sgu_b_s, m_w_branch_a, m_w_branch_b, m_w_out, m_norm_ffn_g, m_w_up, m_w_down, m_final_norm_g, v_norm_mix_g, v_w_in, v_conv_w, v_conv_b, v_lru_w_a, v_lru_b_a, v_lru_w_x, v_lru_b_x, v_lru_lambda, v_sgu_ln_g, v_sgu_ln_b, v_sgu_w_s, v_sgu_b_s, v_w_branch_a, v_w_branch_b, v_w_out, v_norm_ffn_g, v_w_up, v_w_down, v_final_norm_g):
    w = dict(norm_mix_g=norm_mix_g, w_in=w_in, conv_w=conv_w, conv_b=conv_b, lru_w_a=lru_w_a, lru_b_a=lru_b_a,
             lru_w_x=lru_w_x, lru_b_x=lru_b_x, lru_lambda=lru_lambda, sgu_ln_g=sgu_ln_g, sgu_ln_b=sgu_ln_b,
             sgu_w_s=sgu_w_s, sgu_b_s=sgu_b_s, w_branch_a=w_branch_a, w_branch_b=w_branch_b, w_out=w_out,
             norm_ffn_g=norm_ffn_g, w_up=w_up, w_down=w_down, final_norm_g=final_norm_g)
    m = dict(norm_mix_g=m_norm_mix_g, w_in=m_w_in, conv_w=m_conv_w, conv_b=m_conv_b, lru_w_a=m_lru_w_a,
             lru_b_a=m_lru_b_a, lru_w_x=m_lru_w_x, lru_b_x=m_lru_b_x, lru_lambda=m_lru_lambda,
             sgu_ln_g=m_sgu_ln_g, sgu_ln_b=m_sgu_ln_b, sgu_w_s=m_sgu_w_s, sgu_b_s=m_sgu_b_s,
             w_branch_a=m_w_branch_a, w_branch_b=m_w_branch_b, w_out=m_w_out, norm_ffn_g=m_norm_ffn_g,
             w_up=m_w_up, w_down=m_w_down, final_norm_g=m_final_norm_g)
    v = dict(norm_mix_g=v_norm_mix_g, w_in=v_w_in, conv_w=v_conv_w, conv_b=v_conv_b, lru_w_a=v_lru_w_a,
             lru_b_a=v_lru_b_a, lru_w_x=v_lru_w_x, lru_b_x=v_lru_b_x, lru_lambda=v_lru_lambda,
             sgu_ln_g=v_sgu_ln_g, sgu_ln_b=v_sgu_ln_b, sgu_w_s=v_sgu_w_s, sgu_b_s=v_sgu_b_s,
             w_branch_a=v_w_branch_a, w_branch_b=v_w_branch_b, w_out=v_w_out, norm_ffn_g=v_norm_ffn_g,
             w_up=v_w_up, w_down=v_w_down, final_norm_g=v_final_norm_g)
    core = lax.axis_index("c")
    chip = 2 * lax.axis_index("x") + lax.axis_index("y")
    sel = jnp.stack([core, 1 - core, chip]).astype(jnp.int32)
    this_core, other_core, this_chip = ("sel", 0), ("sel", 1), ("sel", 2)
    sds = jax.ShapeDtypeStruct

    ts = TOKEN_TILE
    halves = {k: (w[k].shape[1] // 2, w[k].shape[2]) for k in BIG}

    def half_view(k, a):
        return a.reshape((2 * N_QUARTERS,) + halves[k])

    def full_view(k, a):
        r2, cols = halves[k]
        if k in ("w_in", "w_up"):
            return a.reshape(1, N_QUARTERS, 2 * r2, cols)
        return a.reshape(1, 2 * N_QUARTERS * r2, cols)

    layer_bufs = [[], []]
    for k in BIG:
        _, r, cols = w[k].shape
        w4 = w[k].reshape(DEPTH, 1, r, cols)
        outs = _ew_call(lambda a, b: (a, b), "cast_weights", [(w4, (0, 0)), (w4, (1, 0))],
                        [(sds((1, N_QUARTERS, r, cols), BF), (0, this_chip))] * DEPTH, 1, sel)
        for l in range(DEPTH):
            layer_bufs[l].append(half_view(k, outs[l]))
    conv_buf = lax.dynamic_update_slice_in_dim(
        jnp.zeros((DEPTH, N_QUARTERS) + conv_w.shape[1:], F32), conv_w[:, None], chip, axis=1)
    sm = {k: w[k] for k in SMALL}
    sm["conv_w"] = _gather_call([conv_buf])[0].transpose(0, 2, 1, 3).reshape(DEPTH, CONV_WIDTH, D_RNN)

    def gather_start(l, after):
        return _exchange_start(f"gather_start_{l}", layer_bufs[l], _gather_copies, 3 * len(BIG), after)

    def gather_finish(l, started, after):
        send_sems, recv_sems, thru, _ = started
        landed = _exchange_wait(f"gather_wait_{l}", send_sems, recv_sems, thru, _gather_copies, after)
        landed = _sibling_inplace_call("gather_forward", landed, _gather_forward_slabs, 3)
        return {k: full_view(k, a) for k, a in zip(BIG, landed)}

    started = gather_start(0, sel)
    big0 = gather_finish(0, started, started[3])
    started = gather_start(1, big0["w_in"])
    p0 = _layer_small(sm, 0)
    p0["g1"] = p0["g1"] + started[3][0, 0]
    sv0 = _layer_fwd_mix(x[0], big0, p0, ts)
    x_mid = _layer_fwd_out(sv0, big0, ts)
    big1 = gather_finish(1, started, x_mid)
    sv1 = _layer_fwd_mix(x_mid, big1, _layer_small(sm, 1), ts)
    x_out = _layer_fwd_out(sv1, big1, ts)
    dx, loss, dgf = _loss_call(x_out, loss_target[0], final_norm_g.reshape(1, -1), ts)

    def reduce_start(l, gb, after):
        pairs = []
        for k in BIG:
            r2, cols = halves[k]
            mine = half_view(k, gb[k])[None]
            to_sibling = _ew_call(lambda a: (a,), "cast_grads", [(mine, (0, lambda g, s: 2 * g + s[1]))],
                                  [(sds((1, N_QUARTERS, r2, cols), BF), (0, "g"))], N_QUARTERS, sel)[0]
            pairs.append((mine, to_sibling))
        from_sibling = _sibling_send_call([t for _, t in pairs])
        sums = [
            _ew_call(lambda a, b: (a + b.astype(F32),), "pair_sum",
                     [(mine, (0, lambda g, s: 2 * g + s[0])), (r, (0, "g"))],
                     [(sds(r.shape, BF), (0, "g"))], N_QUARTERS, sel)[0][0]
            for (mine, _), r in zip(pairs, from_sibling)]
        zones = [lax.empty((3,) + a.shape[1:], BF) for a in sums]
        return _exchange_start(f"reduce_start_{l}", sums + zones, _owner_copies, 3 * len(BIG), after)

    def reduce_finish(l, started, after, into):
        send_sems, recv_sems, thru, _ = started
        done = _exchange_wait(f"reduce_wait_{l}", send_sems, recv_sems, thru, _owner_copies, after)
        sums, zones = done[:len(BIG)], done[len(BIG):]
        out = []
        for i, k in enumerate(BIG):
            r2, cols = halves[k]
            out.append(_ew_call(
                lambda a, b, c, d: (((a.astype(F32) + b.astype(F32)) + c.astype(F32)) + d.astype(F32),),
                "quarter_sum", [(sums[i][None], (0, this_chip))] + [(zones[i][None], (0, j)) for j in range(3)],
                [(sds((DEPTH, 2, r2, cols), F32), (l, this_core))], 1, sel,
                into=None if into is None else into[i])[0])
        return out

    dx1, gb1, gs1 = _layer_bwd_ffn(dx, sv1, big1, ts)
    dx_mid, gb1_mix, gs1_mix = _layer_bwd_mix(dx1, sv1, big1, lru_lambda[1], ts)
    started = reduce_start(1, {**gb1, **gb1_mix}, dx_mid)
    sv0["p"] = dict(sv0["p"], g2=sv0["p"]["g2"] + started[3][0, 0])
    dx1, gb0, gs0 = _layer_bwd_ffn(dx_mid, sv0, big0, ts)
    grad_x, gb0_mix, gs0_mix = _layer_bwd_mix(dx1, sv0, big0, lru_lambda[0], ts)
    reduced = reduce_finish(1, started, grad_x, None)
    started = reduce_start(0, {**gb0, **gb0_mix}, reduced[0])
    reduced = reduce_finish(0, started, started[3], reduced)
    reduced = _sibling_inplace_call(
        "grads_swap_halves", reduced, lambda ref, c: [(ref.at[l, c], ref.at[l, 1 - c]) for l in range(DEPTH)], DEPTH)
    grads_big = {k: g.reshape(w[k].shape) for k, g in zip(BIG, reduced)}
    layer_gs = [{**gs0, **gs0_mix}, {**gs1, **gs1_mix}]
    gs = {k: jnp.stack([g[k] for g in layer_gs]) for k in layer_gs[0]}
    gs["final_norm_g"] = dgf[0]

    like = {k: jax.ShapeDtypeStruct(sm[k].shape, F32) for k in SMALL}
    grads_small = _unpack_small(_small_allreduce_call(_pack_small(gs)), like)
    conv_q = grads_small["conv_w"].reshape(DEPTH, CONV_WIDTH, N_QUARTERS, D_RNN // N_QUARTERS)
    grads_small["conv_w"] = lax.dynamic_index_in_dim(conv_q, chip, axis=2, keepdims=False)

    delta, new_m, new_v = {}, {}, {}
    for k in BIG:
        views = [_as4(_as_rows(a)) for a in (w[k], grads_big[k], m[k], v[k])]
        outs = _ew_call(_adamw, "adamw_big", [(a, (0, 0)) for a in views],
                        [(sds(views[0].shape, F32), (0, 0))] * 3)
        delta[k], new_m[k], new_v[k] = (o.reshape(w[k].shape) for o in outs)
    outs = _small_adamw_call(*[[_as_rows(d[k]) for k in SMALL] for d in (w, grads_small, m, v)])
    for d, o in zip((delta, new_m, new_v), outs):
        for k, a in zip(SMALL, o):
            d[k] = a.reshape(w[k].shape)

    grads = {**grads_big, **grads_small}
    total = lax.psum(loss[0, 0], ("x", "y", "c"))
    return (total, grad_x[None], *[grads[k] for k in WEIGHTS], *[delta[k] for k in WEIGHTS],
            *[new_m[k] for k in WEIGHTS], *[new_v[k] for k in WEIGHTS])
```

```python
import functools
import math

import jax
import jax.numpy as jnp
from jax import lax
from jax.experimental import pallas as pl
from jax.experimental.pallas import tpu as pltpu

F32 = jnp.float32
BF = jnp.bfloat16

DEPTH = 2
D_MODEL = 1024
D_RNN = 1280
D_SGU = 1024
D_FF = 4096
D_IN = 2 * D_RNN + 2 * D_SGU + 2 * D_MODEL
N_QUARTERS = 4
Q_IN = D_IN // N_QUARTERS
Q_FF = D_FF // N_QUARTERS
RNN_HEADS = 20
RNN_HEAD_DIM = 64
LRU_GROUP = 256
N_LRU_GROUPS = D_RNN // LRU_GROUP
HEADS_PER_GROUP = LRU_GROUP // RNN_HEAD_DIM
CONV_WIDTH = 4
LRU_C = 8.0
SGU_GROUPS = 8
SGU_BLOCK = 128
CHUNK = 64
EPS = 1e-6

ADAM_LR = 0.001
ADAM_B1 = 0.9
ADAM_B2 = 0.999
ADAM_EPS = 1e-08
ADAM_WD = 0.01
ADAM_STEP = 10

SUBLANES = 8
TOKEN_TILE = 512
VMEM_LIMIT_BYTES = 56 * 1024 * 1024

MESH = pl.DeviceIdType.MESH


def _params(semantics=None, vmem=True, **kw):
    return pltpu.CompilerParams(
        dimension_semantics=semantics,
        vmem_limit_bytes=VMEM_LIMIT_BYTES if vmem else None,
        **kw,
    )


def _dot(a, b):
    return jnp.dot(a, b, preferred_element_type=F32)


def _dot_nt(a, b):
    return lax.dot_general(a, b, (((1,), (1,)), ((), ())), preferred_element_type=F32)


def _dot_tn(a, b):
    return lax.dot_general(a, b, (((0,), (0,)), ((), ())), preferred_element_type=F32)


_GELU_C = math.sqrt(2.0 / math.pi)
_GELU_A = 0.044715


def _gelu(x):
    return 0.5 * x * (1.0 + jnp.tanh(_GELU_C * (x + _GELU_A * x * x * x)))


def _gelu_and_grad(x):
    x2 = x * x
    t = jnp.tanh(_GELU_C * (x + _GELU_A * x2 * x))
    du = _GELU_C * (1.0 + 3.0 * _GELU_A * x2)
    return 0.5 * x * (1.0 + t), 0.5 * (1.0 + t) + 0.5 * x * (1.0 - t * t) * du


def _rms_stats(x):
    return lax.rsqrt(jnp.mean(x * x, axis=-1, keepdims=True) + EPS)


def _rms_bwd(dy, x, g):
    rs = _rms_stats(x)
    n = x * rs
    dn = dy * g
    dx = rs * (dn - n * jnp.mean(dn * n, axis=-1, keepdims=True))
    return dx, dy * n


def _row_sum(x):
    return jnp.sum(x, axis=0, keepdims=True)


def _tile_spec(ts, width, col=0):
    return pl.BlockSpec((ts, width), lambda i, col=col: (i, col))


def _full_spec(shape):
    zeros = (0,) * len(shape)
    return pl.BlockSpec(shape, lambda *_: zeros)


def _layer_spec(w, layer):
    zeros = (0,) * (w.ndim - 1)
    return pl.BlockSpec((None,) + tuple(w.shape[1:]), lambda *_: (layer,) + zeros)


def _norm_call(x, g, ts):
    s = x.shape[0]

    def body(x_ref, g_ref, h_ref):
        xv = x_ref[...]
        h_ref[...] = (xv * _rms_stats(xv) * g_ref[...]).astype(BF)

    return pl.pallas_call(
        body, name="norm_fwd", grid=(s // ts,),
        in_specs=[_tile_spec(ts, D_MODEL), _full_spec((1, D_MODEL))],
        out_specs=_tile_spec(ts, D_MODEL),
        out_shape=jax.ShapeDtypeStruct((s, D_MODEL), BF),
        compiler_params=_params(("parallel",)),
    )(x, g)


def _inproj_call(h, w_in, layer, ts):
    s = h.shape[0]

    def body(h_ref, w_ref, o_ref):
        o_ref[...] = _dot(h_ref[...], w_ref[...]).astype(BF)

    return pl.pallas_call(
        body, name="inproj_fwd", grid=(N_QUARTERS, s // ts),
        in_specs=[
            pl.BlockSpec((ts, D_MODEL), lambda q, i: (i, 0)),
            pl.BlockSpec((None, None, D_MODEL, Q_IN), lambda q, i: (layer, q, 0, 0)),
        ],
        out_specs=pl.BlockSpec((ts, Q_IN), lambda q, i: (i, q)),
        out_shape=jax.ShapeDtypeStruct((s, D_IN), BF),
        compiler_params=_params(("parallel", "parallel")),
    )(h, w_in)


def _shift_down(x, tail, s):
    xr = pltpu.roll(x, s, 0)
    tr = pltpu.roll(tail, s, 0)
    row = lax.broadcasted_iota(jnp.int32, tail.shape, 0)
    top = jnp.where(row < s, tr, xr[0:SUBLANES])
    return jnp.concatenate([top, xr[SUBLANES:]], axis=0)


def _shift_up(x, head, s):
    t = x.shape[0]
    xr = pltpu.roll(x, t - s, 0)
    hr = pltpu.roll(head, SUBLANES - s, 0)
    row = lax.broadcasted_iota(jnp.int32, head.shape, 0)
    bottom = jnp.where(row >= SUBLANES - s, hr, xr[t - SUBLANES:])
    return jnp.concatenate([xr[: t - SUBLANES], bottom], axis=0)


def _conv_fwd(x, tail, cw_ref, cb_ref):
    shifted = [x] + [_shift_down(x, tail, s) for s in range(1, CONV_WIDTH)]
    out = cb_ref[...] + cw_ref[CONV_WIDTH - 1:CONV_WIDTH, :] * x
    for s in range(1, CONV_WIDTH):
        k = CONV_WIDTH - 1 - s
        out = out + cw_ref[k:k + 1, :] * shifted[s]
    return out, shifted


def _group_dot(x_bf, w_ref, dot):
    cols = [dot(x_bf[:, g * LRU_GROUP:(g + 1) * LRU_GROUP], w_ref[g]) for g in range(N_LRU_GROUPS)]
    return jnp.concatenate(cols, axis=1)


def _lru_gates(xr, wa_ref, wx_ref, ba_ref, bx_ref, sp_ref):
    xb = xr.astype(BF)
    r = jax.nn.sigmoid(_group_dot(xb, wa_ref, _dot) + ba_ref[...])
    i = jax.nn.sigmoid(_group_dot(xb, wx_ref, _dot) + bx_ref[...])
    log_a = (-LRU_C * r) * sp_ref[...]
    a = jnp.exp(log_a)
    nrm = jnp.sqrt(-jnp.tanh(log_a) * (a * a + 1.0))
    return r, i, a, nrm


def _linear_scan(a, b, carry, al_ref, bl_ref, h_ref, reverse):
    t, c = a.shape
    rowm = lax.broadcasted_iota(jnp.int32, (t, c), 0) & (SUBLANES - 1)
    for d in (1, 2, 4):
        if reverse:
            keep, sh = rowm < SUBLANES - d, t - d
        else:
            keep, sh = rowm >= d, d
        a_sh = jnp.where(keep, pltpu.roll(a, sh, 0), 1.0)
        b_sh = jnp.where(keep, pltpu.roll(b, sh, 0), 0.0)
        b = a * b_sh + b
        a = a * a_sh
    al_ref[...] = a
    bl_ref[...] = b
    groups = t // SUBLANES

    def step(j, state):
        jj = groups - 1 - j if reverse else j
        off = pl.multiple_of(jj * SUBLANES, SUBLANES)
        rows = bl_ref[pl.ds(off, SUBLANES), :] + al_ref[pl.ds(off, SUBLANES), :] * state
        h_ref[pl.ds(off, SUBLANES), :] = rows
        last = rows[0:1, :] if reverse else rows[SUBLANES - 1:SUBLANES, :]
        return jnp.broadcast_to(last, (SUBLANES, c))

    out = lax.fori_loop(0, groups, step, jnp.broadcast_to(carry, (SUBLANES, c)))
    return out[0:1, :]


def _rnn_fwd_call(proj, wa, wx, ba, bx, sp, cw, cb, ts):
    s = proj.shape[0]

    def body(xg_ref, wa_ref, wx_ref, ba_ref, bx_ref, sp_ref, cw_ref, cb_ref, hr_ref, ya_ref,
             tail_sc, carry_sc, al_sc, bl_sc, h_sc):
        @pl.when(pl.program_id(0) == 0)
        def _():
            tail_sc[...] = jnp.zeros_like(tail_sc)
            carry_sc[...] = jnp.zeros_like(carry_sc)

        x = xg_ref[:, :D_RNN].astype(F32)
        g = xg_ref[:, D_RNN:].astype(F32)
        xr, _ = _conv_fwd(x, tail_sc[...], cw_ref, cb_ref)
        tail_sc[...] = x[ts - SUBLANES:, :]
        _, i, a, nrm = _lru_gates(xr, wa_ref, wx_ref, ba_ref, bx_ref, sp_ref)
        carry_sc[...] = _linear_scan(a, nrm * (i * xr), carry_sc[...], al_sc, bl_sc, h_sc, False)
        h = h_sc[...]
        hr_ref[...] = h.astype(BF)
        ya_ref[...] = (h * _gelu(g)).astype(BF)

    gw = (N_LRU_GROUPS, LRU_GROUP, LRU_GROUP)
    return pl.pallas_call(
        body, name="rnn_fwd", grid=(s // ts,),
        in_specs=[_tile_spec(ts, 2 * D_RNN), _full_spec(gw), _full_spec(gw),
                  _full_spec((1, D_RNN)), _full_spec((1, D_RNN)), _full_spec((1, D_RNN)),
                  _full_spec((CONV_WIDTH, D_RNN)), _full_spec((1, D_RNN))],
        out_specs=[_tile_spec(ts, D_RNN), _tile_spec(ts, D_RNN)],
        out_shape=[jax.ShapeDtypeStruct((s, D_RNN), BF), jax.ShapeDtypeStruct((s, D_RNN), BF)],
        scratch_shapes=[pltpu.VMEM((SUBLANES, D_RNN), F32), pltpu.VMEM((1, D_RNN), F32),
                        pltpu.VMEM((ts, D_RNN), F32), pltpu.VMEM((ts, D_RNN), F32),
                        pltpu.VMEM((ts, D_RNN), F32)],
        compiler_params=_params(("arbitrary",)),
    )(proj, wa, wx, ba, bx, sp, cw, cb)


def _layernorm_fwd(x):
    mu = jnp.mean(x, axis=-1, keepdims=True)
    xc = x - mu
    rstd = lax.rsqrt(jnp.mean(xc * xc, axis=-1, keepdims=True) + EPS)
    return xc * rstd, rstd


def _sgu_mix(vn_bf, wm_ref, bsb_ref, ts):
    rows = []
    for blk in range(ts // SGU_BLOCK):
        r0 = blk * SGU_BLOCK
        cols = [
            _dot(wm_ref[g], vn_bf[r0:r0 + SGU_BLOCK, g * SGU_BLOCK:(g + 1) * SGU_BLOCK]) + bsb_ref[g]
            for g in range(SGU_GROUPS)
        ]
        rows.append(jnp.concatenate(cols, axis=1))
    return jnp.concatenate(rows, axis=0)


def _sgu_fwd_call(proj, wm, bsb, lg, lb, ts):
    s = proj.shape[0]

    def body(uv_ref, wm_ref, bsb_ref, lg_ref, lb_ref, yb_ref):
        gu = _gelu(uv_ref[:, :D_SGU].astype(F32))
        gv = _gelu(uv_ref[:, D_SGU:2 * D_SGU].astype(F32))
        nh, _ = _layernorm_fwd(gv)
        vn = (nh * lg_ref[...] + lb_ref[...]).astype(BF)
        yb_ref[...] = (gu * _sgu_mix(vn, wm_ref, bsb_ref, ts)).astype(BF)

    sw = (SGU_GROUPS, SGU_BLOCK, SGU_BLOCK)
    return pl.pallas_call(
        body, name="sgu_fwd", grid=(s // ts,),
        in_specs=[_tile_spec(ts, 2 * D_RNN, 1), _full_spec(sw), _full_spec(sw),
                  _full_spec((1, D_SGU)), _full_spec((1, D_SGU))],
        out_specs=_tile_spec(ts, D_SGU),
        out_shape=jax.ShapeDtypeStruct((s, D_SGU), BF),
        compiler_params=_params(("parallel",)),
    )(proj, wm, bsb, lg, lb)


_GATE_COL0 = (2 * D_RNN + 2 * D_SGU) // 512


def _gate_specs(ts):
    return [_tile_spec(ts, 512, _GATE_COL0 + j) for j in range(4)]


def _merge_call(x, proj, ya_pre, yb_pre, w_ba, w_bb, w_out, g2, layer, ts):
    s = x.shape[0]

    def body(x_ref, ga0, ga1, gb0, gb1, ya_ref, yb_ref, wa_ref, wb_ref, wo_ref, g2_ref,
             x1_ref, yao_ref, ybo_ref, mg_ref, h2_ref):
        ya = _dot(ya_ref[...], wa_ref[...])
        yb = _dot(yb_ref[...], wb_ref[...])
        sa = jax.nn.sigmoid(jnp.concatenate([ga0[...], ga1[...]], axis=1).astype(F32))
        sb = jax.nn.sigmoid(jnp.concatenate([gb0[...], gb1[...]], axis=1).astype(F32))
        merged = (sa * ya + sb * yb).astype(BF)
        x1 = x_ref[...] + _dot(merged, wo_ref[...])
        x1_ref[...] = x1
        yao_ref[...] = ya.astype(BF)
        ybo_ref[...] = yb.astype(BF)
        mg_ref[...] = merged
        h2_ref[...] = (x1 * _rms_stats(x1) * g2_ref[...]).astype(BF)

    act = jax.ShapeDtypeStruct((s, D_MODEL), BF)
    return pl.pallas_call(
        body, name="merge_fwd", grid=(s // ts,),
        in_specs=[_tile_spec(ts, D_MODEL)] + _gate_specs(ts) + [
            _tile_spec(ts, D_RNN), _tile_spec(ts, D_SGU),
            _layer_spec(w_ba, layer), _layer_spec(w_bb, layer), _layer_spec(w_out, layer),
            _full_spec((1, D_MODEL))],
        out_specs=[_tile_spec(ts, D_MODEL)] * 5,
        out_shape=[jax.ShapeDtypeStruct((s, D_MODEL), F32), act, act, act, act],
        compiler_params=_params(("parallel",)),
    )(x, proj, proj, proj, proj, ya_pre, yb_pre, w_ba, w_bb, w_out, g2)


def _ffn_call(x1, h2, w_up, w_down, layer, ts):
    s = x1.shape[0]

    def body(x1_ref, h2_ref, wu_ref, wd_ref, x2_ref, p_ref):
        h2v = h2_ref[...]
        acc = x1_ref[...]
        for q in range(N_QUARTERS):
            p = _dot(h2v, wu_ref[q])
            p_ref[:, q * Q_FF:(q + 1) * Q_FF] = p.astype(BF)
            f = jnp.square(jnp.maximum(p, 0.0)).astype(BF)
            acc = acc + _dot(f, wd_ref[q * Q_FF:(q + 1) * Q_FF, :])
        x2_ref[...] = acc

    return pl.pallas_call(
        body, name="ffn_fwd", grid=(s // ts,),
        in_specs=[_tile_spec(ts, D_MODEL), _tile_spec(ts, D_MODEL),
                  pl.BlockSpec((None, N_QUARTERS, D_MODEL, Q_FF), lambda i: (layer, 0, 0, 0)),
                  pl.BlockSpec((None, D_FF, D_MODEL), lambda i: (layer, 0, 0))],
        out_specs=[_tile_spec(ts, D_MODEL), _tile_spec(ts, D_FF)],
        out_shape=[jax.ShapeDtypeStruct((s, D_MODEL), F32), jax.ShapeDtypeStruct((s, D_FF), BF)],
        compiler_params=_params(("parallel",)),
    )(x1, h2, w_up, w_down)


def _loss_call(x, target, gf, ts):
    s = x.shape[0]

    def body(x_ref, t_ref, g_ref, dx_ref, loss_ref, dg_ref):
        @pl.when(pl.program_id(0) == 0)
        def _():
            loss_ref[...] = jnp.zeros_like(loss_ref)
            dg_ref[...] = jnp.zeros_like(dg_ref)

        xv = x_ref[...]
        gv = g_ref[...]
        err = xv * _rms_stats(xv) * gv - t_ref[...]
        part = 0.5 * jnp.sum(jnp.mean(err * err, axis=-1, keepdims=True), axis=0, keepdims=True)
        loss_ref[...] += jnp.broadcast_to(part, loss_ref.shape)
        dx, dg = _rms_bwd(err * (1.0 / D_MODEL), xv, gv)
        dx_ref[...] = dx
        dg_ref[...] += _row_sum(dg)

    return pl.pallas_call(
        body, name="loss_head", grid=(s // ts,),
        in_specs=[_tile_spec(ts, D_MODEL), _tile_spec(ts, D_MODEL), _full_spec((1, D_MODEL))],
        out_specs=[_tile_spec(ts, D_MODEL), _full_spec((1, 128)), _full_spec((1, D_MODEL))],
        out_shape=[jax.ShapeDtypeStruct((s, D_MODEL), F32), jax.ShapeDtypeStruct((1, 128), F32),
                   jax.ShapeDtypeStruct((1, D_MODEL), F32)],
        compiler_params=_params(("arbitrary",)),
    )(x, target, gf)


def _ffn_bwd_call(dx2, p, x1, g2, w_up, w_down, layer, ts):
    s = dx2.shape[0]

    def body(dx2_ref, p_ref, x1_ref, g2_ref, wu_ref, wd_ref, dx1_ref, dp_ref, dg_ref):
        @pl.when(pl.program_id(0) == 0)
        def _():
            dg_ref[...] = jnp.zeros_like(dg_ref)

        dx2v = dx2_ref[...]
        dyb = dx2v.astype(BF)
        dh2 = jnp.zeros((ts, D_MODEL), F32)
        for q in range(N_QUARTERS):
            cols = slice(q * Q_FF, (q + 1) * Q_FF)
            df = _dot_nt(dyb, wd_ref[cols, :])
            dp = (df * (2.0 * jnp.maximum(p_ref[:, cols].astype(F32), 0.0))).astype(BF)
            dp_ref[:, cols] = dp
            dh2 = dh2 + _dot_nt(dp, wu_ref[q])
        dx, dg = _rms_bwd(dh2, x1_ref[...], g2_ref[...])
        dx1_ref[...] = dx2v + dx
        dg_ref[...] += _row_sum(dg)

    return pl.pallas_call(
        body, name="ffn_bwd", grid=(s // ts,),
        in_specs=[_tile_spec(ts, D_MODEL), _tile_spec(ts, D_FF), _tile_spec(ts, D_MODEL),
                  _full_spec((1, D_MODEL)),
                  pl.BlockSpec((None, N_QUARTERS, D_MODEL, Q_FF), lambda i: (layer, 0, 0, 0)),
                  pl.BlockSpec((None, D_FF, D_MODEL), lambda i: (layer, 0, 0))],
        out_specs=[_tile_spec(ts, D_MODEL), _tile_spec(ts, D_FF), _full_spec((1, D_MODEL))],
        out_shape=[jax.ShapeDtypeStruct((s, D_MODEL), F32), jax.ShapeDtypeStruct((s, D_FF), BF),
                   jax.ShapeDtypeStruct((1, D_MODEL), F32)],
        compiler_params=_params(("arbitrary",)),
    )(dx2, p, x1, g2, w_up, w_down)


def _merge_bwd_call(dx1, proj, ya, yb, w_ba, w_bb, w_out, layer, ts):
    s = dx1.shape[0]

    def body(dx1_ref, ga0, ga1, gb0, gb1, ya_ref, yb_ref, wa_ref, wb_ref, wo_ref,
             dya_ref, dyb_ref, dgate_ref, dyap_ref, dybp_ref):
        dm = _dot_nt(dx1_ref[...].astype(BF), wo_ref[...])
        sa = jax.nn.sigmoid(jnp.concatenate([ga0[...], ga1[...]], axis=1).astype(F32))
        sb = jax.nn.sigmoid(jnp.concatenate([gb0[...], gb1[...]], axis=1).astype(F32))
        dya = (dm * sa).astype(BF)
        dyb = (dm * sb).astype(BF)
        dya_ref[...] = dya
        dyb_ref[...] = dyb
        dgate_ref[:, :D_MODEL] = (dm * ya_ref[...].astype(F32) * sa * (1.0 - sa)).astype(BF)
        dgate_ref[:, D_MODEL:] = (dm * yb_ref[...].astype(F32) * sb * (1.0 - sb)).astype(BF)
        dyap_ref[...] = _dot_nt(dya, wa_ref[...]).astype(BF)
        dybp_ref[...] = _dot_nt(dyb, wb_ref[...]).astype(BF)

    act = jax.ShapeDtypeStruct((s, D_MODEL), BF)
    return pl.pallas_call(
        body, name="merge_bwd", grid=(s // ts,),
        in_specs=[_tile_spec(ts, D_MODEL)] + _gate_specs(ts) + [
            _tile_spec(ts, D_MODEL), _tile_spec(ts, D_MODEL),
            _layer_spec(w_ba, layer), _layer_spec(w_bb, layer), _layer_spec(w_out, layer)],
        out_specs=[_tile_spec(ts, D_MODEL), _tile_spec(ts, D_MODEL), _tile_spec(ts, 2 * D_MODEL),
                   _tile_spec(ts, D_RNN), _tile_spec(ts, D_SGU)],
        out_shape=[act, act, jax.ShapeDtypeStruct((s, 2 * D_MODEL), BF),
                   jax.ShapeDtypeStruct((s, D_RNN), BF), jax.ShapeDtypeStruct((s, D_SGU), BF)],
        compiler_params=_params(("parallel",)),
    )(dx1, proj, proj, proj, proj, ya, yb, w_ba, w_bb, w_out)


def _sgu_bwd_call(dyb_pre, proj, wm, bsb, mask, lg, lb, ts):
    s = proj.shape[0]

    def body(dy_ref, uv_ref, wm_ref, bsb_ref, mask_ref, lg_ref, lb_ref,
             duv_ref, dws_ref, dbs_ref, dlg_ref, dlb_ref, dm_sc):
        step = pl.program_id(0)

        @pl.when(step == 0)
        def _():
            dws_ref[...] = jnp.zeros_like(dws_ref)
            dlg_ref[...] = jnp.zeros_like(dlg_ref)
            dlb_ref[...] = jnp.zeros_like(dlb_ref)
            dm_sc[...] = jnp.zeros_like(dm_sc)

        gu, dgu_du = _gelu_and_grad(uv_ref[:, :D_SGU].astype(F32))
        gv, dgv_dv = _gelu_and_grad(uv_ref[:, D_SGU:2 * D_SGU].astype(F32))
        nh, rstd = _layernorm_fwd(gv)
        lgv = lg_ref[...]
        vn = (nh * lgv + lb_ref[...]).astype(BF)
        dy = dy_ref[...].astype(F32)
        du = dy * _sgu_mix(vn, wm_ref, bsb_ref, ts) * dgu_du
        dmix = dy * gu
        dmix_bf = dmix.astype(BF)
        dm_acc = dm_sc[...]
        rows = []
        for blk in range(ts // SGU_BLOCK):
            r0 = blk * SGU_BLOCK
            dm_acc = dm_acc + dmix[r0:r0 + SGU_BLOCK, :]
            cols = []
            for g in range(SGU_GROUPS):
                c0 = g * SGU_BLOCK
                dmg = dmix_bf[r0:r0 + SGU_BLOCK, c0:c0 + SGU_BLOCK]
                cols.append(_dot_tn(wm_ref[g], dmg))
                dws_ref[g] += mask_ref[...] * _dot_nt(dmg, vn[r0:r0 + SGU_BLOCK, c0:c0 + SGU_BLOCK])
            rows.append(jnp.concatenate(cols, axis=1))
        dm_sc[...] = dm_acc
        dvn = jnp.concatenate(rows, axis=0)
        dlg_ref[...] += _row_sum(dvn * nh)
        dlb_ref[...] += _row_sum(dvn)
        dnh = dvn * lgv
        dgv = rstd * (dnh - jnp.mean(dnh, axis=-1, keepdims=True)
                      - nh * jnp.mean(dnh * nh, axis=-1, keepdims=True))
        duv_ref[:, :D_SGU] = du.astype(BF)
        duv_ref[:, D_SGU:] = (dgv * dgv_dv).astype(BF)

        @pl.when(step == pl.num_programs(0) - 1)
        def _():
            for g in range(SGU_GROUPS):
                dbs_ref[:, g:g + 1] = jnp.sum(
                    dm_acc[:, g * SGU_BLOCK:(g + 1) * SGU_BLOCK], axis=1, keepdims=True)

    sw = (SGU_GROUPS, SGU_BLOCK, SGU_BLOCK)
    return pl.pallas_call(
        body, name="sgu_bwd", grid=(s // ts,),
        in_specs=[_tile_spec(ts, D_SGU), _tile_spec(ts, 2 * D_RNN, 1), _full_spec(sw), _full_spec(sw),
                  _full_spec((SGU_BLOCK, SGU_BLOCK)), _full_spec((1, D_SGU)), _full_spec((1, D_SGU))],
        out_specs=[_tile_spec(ts, 2 * D_SGU), _full_spec(sw), _full_spec((SGU_BLOCK, SGU_GROUPS)),
                   _full_spec((1, D_SGU)), _full_spec((1, D_SGU))],
        out_shape=[jax.ShapeDtypeStruct((s, 2 * D_SGU), BF), jax.ShapeDtypeStruct(sw, F32),
                   jax.ShapeDtypeStruct((SGU_BLOCK, SGU_GROUPS), F32),
                   jax.ShapeDtypeStruct((1, D_SGU), F32), jax.ShapeDtypeStruct((1, D_SGU), F32)],
        scratch_shapes=[pltpu.VMEM((SGU_BLOCK, D_SGU), F32)],
        compiler_params=_params(("arbitrary",)),
    )(dyb_pre, proj, wm, bsb, mask, lg, lb)


_ROW_DBA, _ROW_DBX, _ROW_DSP, _ROW_DCB, _ROW_DCW = 0, 1, 2, 3, 4
_PREV_ROWS = 16


def _rnn_bwd_call(dya_pre, proj, hr, wa, wx, ba, bx, sp, cw, cb, ts):
    s = proj.shape[0]
    nt = s // ts
    per = ts // _PREV_ROWS

    def tile(i):
        return nt - 1 - i

    def prev(i):
        return jnp.maximum(tile(i) * per - 1, 0)

    def body(dy_ref, xg_ref, xgp_ref, hr_ref, hrp_ref, wa_ref, wx_ref, ba_ref, bx_ref, sp_ref,
             cw_ref, cb_ref, dxg_ref, dwa_ref, dwx_ref, vec_ref,
             lam_carry, a_first, dxr_head, al_sc, bl_sc, lam_sc):
        step = pl.program_id(0)

        @pl.when(step == 0)
        def _():
            dwa_ref[...] = jnp.zeros_like(dwa_ref)
            dwx_ref[...] = jnp.zeros_like(dwx_ref)
            vec_ref[...] = jnp.zeros_like(vec_ref)
            lam_carry[...] = jnp.zeros_like(lam_carry)
            a_first[...] = jnp.zeros_like(a_first)
            dxr_head[...] = jnp.zeros_like(dxr_head)

        has_prev = (step < nt - 1).astype(F32)
        x = xg_ref[:, :D_RNN].astype(F32)
        g = xg_ref[:, D_RNN:].astype(F32)
        x_tail = xgp_ref[_PREV_ROWS - SUBLANES:, :D_RNN].astype(F32) * has_prev
        h_tail = hrp_ref[_PREV_ROWS - SUBLANES:, :].astype(F32) * has_prev
        xr, x_shifted = _conv_fwd(x, x_tail, cw_ref, cb_ref)
        r, i, a, nrm = _lru_gates(xr, wa_ref, wx_ref, ba_ref, bx_ref, sp_ref)
        h = hr_ref[...].astype(F32)
        dy = dy_ref[...].astype(F32)
        gg, dgg = _gelu_and_grad(g)

        coef = _shift_up(a, jnp.broadcast_to(a_first[...], (SUBLANES, D_RNN)), 1)
        lam_carry[...] = _linear_scan(coef, dy * gg, lam_carry[...], al_sc, bl_sc, lam_sc, True)
        a_first[...] = a[0:1, :]
        lam = lam_sc[...]

        da = lam * _shift_down(h, h_tail, 1)
        dnrm = lam * (i * xr)
        di = lam * nrm * xr
        dlog_a = da * a - dnrm * (a * a) / nrm
        spv = sp_ref[...]
        dza = (dlog_a * (-LRU_C * spv)) * (r * (1.0 - r))
        dzx = di * (i * (1.0 - i))
        vec_ref[_ROW_DSP:_ROW_DSP + 1, :] += _row_sum(dlog_a * (-LRU_C * r))
        vec_ref[_ROW_DBA:_ROW_DBA + 1, :] += _row_sum(dza)
        vec_ref[_ROW_DBX:_ROW_DBX + 1, :] += _row_sum(dzx)
        xb = xr.astype(BF)
        dza_bf = dza.astype(BF)
        dzx_bf = dzx.astype(BF)
        for grp in range(N_LRU_GROUPS):
            cols = slice(grp * LRU_GROUP, (grp + 1) * LRU_GROUP)
            dwa_ref[grp] += _dot_tn(xb[:, cols], dza_bf[:, cols])
            dwx_ref[grp] += _dot_tn(xb[:, cols], dzx_bf[:, cols])
        dxr = (lam * nrm * i + _group_dot(dza_bf, wa_ref, _dot_nt) + _group_dot(dzx_bf, wx_ref, _dot_nt))

        vec_ref[_ROW_DCB:_ROW_DCB + 1, :] += _row_sum(dxr)
        head = dxr_head[...]
        dx = cw_ref[CONV_WIDTH - 1:CONV_WIDTH, :] * dxr
        vec_ref[_ROW_DCW + 3:_ROW_DCW + 4, :] += _row_sum(dxr * x)
        for sft in range(1, CONV_WIDTH):
            k = CONV_WIDTH - 1 - sft
            dx = dx + cw_ref[k:k + 1, :] * _shift_up(dxr, head, sft)
            vec_ref[_ROW_DCW + k:_ROW_DCW + k + 1, :] += _row_sum(dxr * x_shifted[sft])
        dxr_head[...] = dxr[0:SUBLANES, :]
        dxg_ref[:, :D_RNN] = dx.astype(BF)
        dxg_ref[:, D_RNN:] = (dy * h * dgg).astype(BF)

    gw = (N_LRU_GROUPS, LRU_GROUP, LRU_GROUP)
    rev = lambda width: pl.BlockSpec((ts, width), lambda i: (tile(i), 0))
    return pl.pallas_call(
        body, name="rnn_bwd", grid=(nt,),
        in_specs=[rev(D_RNN), rev(2 * D_RNN),
                  pl.BlockSpec((_PREV_ROWS, 2 * D_RNN), lambda i: (prev(i), 0)),
                  rev(D_RNN),
                  pl.BlockSpec((_PREV_ROWS, D_RNN), lambda i: (prev(i), 0)),
                  _full_spec(gw), _full_spec(gw),
                  _full_spec((1, D_RNN)), _full_spec((1, D_RNN)), _full_spec((1, D_RNN)),
                  _full_spec((CONV_WIDTH, D_RNN)), _full_spec((1, D_RNN))],
        out_specs=[rev(2 * D_RNN), _full_spec(gw), _full_spec(gw), _full_spec((SUBLANES, D_RNN))],
        out_shape=[jax.ShapeDtypeStruct((s, 2 * D_RNN), BF), jax.ShapeDtypeStruct(gw, F32),
                   jax.ShapeDtypeStruct(gw, F32), jax.ShapeDtypeStruct((SUBLANES, D_RNN), F32)],
        scratch_shapes=[pltpu.VMEM((1, D_RNN), F32), pltpu.VMEM((1, D_RNN), F32),
                        pltpu.VMEM((SUBLANES, D_RNN), F32),
                        pltpu.VMEM((ts, D_RNN), F32), pltpu.VMEM((ts, D_RNN), F32),
                        pltpu.VMEM((ts, D_RNN), F32)],
        compiler_params=_params(("arbitrary",)),
    )(dya_pre, proj, proj, hr, hr, wa, wx, ba, bx, sp, cw, cb)


def _inproj_bwd_call(dxg, duv, dgate, dx1, x, g1, w_in, layer, ts):
    s = x.shape[0]

    def body(dxg_ref, duv_ref, dgt_ref, dx1_ref, x_ref, g_ref, w_ref, dx_ref, dproj_ref, dg_ref):
        @pl.when(pl.program_id(0) == 0)
        def _():
            dg_ref[...] = jnp.zeros_like(dg_ref)

        dproj = jnp.concatenate([dxg_ref[...], duv_ref[...], dgt_ref[...]], axis=1)
        dproj_ref[...] = dproj
        dh = jnp.zeros((ts, D_MODEL), F32)
        for q in range(N_QUARTERS):
            dh = dh + _dot_nt(dproj[:, q * Q_IN:(q + 1) * Q_IN], w_ref[q])
        dx, dg = _rms_bwd(dh, x_ref[...], g_ref[...])
        dx_ref[...] = dx1_ref[...] + dx
        dg_ref[...] += _row_sum(dg)

    return pl.pallas_call(
        body, name="inproj_bwd", grid=(s // ts,),
        in_specs=[_tile_spec(ts, 2 * D_RNN), _tile_spec(ts, 2 * D_SGU), _tile_spec(ts, 2 * D_MODEL),
                  _tile_spec(ts, D_MODEL), _tile_spec(ts, D_MODEL), _full_spec((1, D_MODEL)),
                  pl.BlockSpec((None, N_QUARTERS, D_MODEL, Q_IN), lambda i: (layer, 0, 0, 0))],
        out_specs=[_tile_spec(ts, D_MODEL), _tile_spec(ts, D_IN), _full_spec((1, D_MODEL))],
        out_shape=[jax.ShapeDtypeStruct((s, D_MODEL), F32), jax.ShapeDtypeStruct((s, D_IN), BF),
                   jax.ShapeDtypeStruct((1, D_MODEL), F32)],
        compiler_params=_params(("arbitrary",)),
    )(dxg, duv, dgate, dx1, x, g1, w_in)


def _relu_sq(p):
    return jnp.square(jnp.maximum(p.astype(F32), 0.0))


def _wgrad_call(a, b, tm, tn, tk, col_blocked, name, a_fn=None):
    s, m = a.shape
    n = b.shape[1]

    def body(a_ref, b_ref, o_ref):
        av = a_ref[...]
        if a_fn is not None:
            av = a_fn(av)
        prod = _dot_tn(av.astype(BF), b_ref[...].astype(BF))

        @pl.when(pl.program_id(2) == 0)
        def _():
            o_ref[...] = prod

        @pl.when(pl.program_id(2) > 0)
        def _():
            o_ref[...] += prod

    if col_blocked:
        per_q = n // N_QUARTERS // tn
        out_spec = pl.BlockSpec((None, tm, tn), lambda i, j, k: (j // per_q, 0, j % per_q))
        out_shape = jax.ShapeDtypeStruct((N_QUARTERS, m, n // N_QUARTERS), F32)
    else:
        out_spec = pl.BlockSpec((tm, tn), lambda i, j, k: (i, j))
        out_shape = jax.ShapeDtypeStruct((m, n), F32)
    return pl.pallas_call(
        body, name=name, grid=(m // tm, n // tn, s // tk),
        in_specs=[pl.BlockSpec((tk, tm), lambda i, j, k: (k, i)),
                  pl.BlockSpec((tk, tn), lambda i, j, k: (k, j))],
        out_specs=out_spec, out_shape=out_shape,
        compiler_params=_params(("parallel", "parallel", "arbitrary")),
    )(a, b)


BIG = ("w_in", "w_up", "w_down", "w_branch_a", "w_branch_b", "w_out")


def _block_diag(w):
    w4 = w.reshape(N_LRU_GROUPS, HEADS_PER_GROUP, RNN_HEAD_DIM, RNN_HEAD_DIM)
    eye = jnp.eye(HEADS_PER_GROUP, dtype=w.dtype)
    return jnp.einsum("gjio,jk->gjiko", w4, eye).reshape(N_LRU_GROUPS, LRU_GROUP, LRU_GROUP)


def _block_diag_extract(d):
    d5 = d.reshape(N_LRU_GROUPS, HEADS_PER_GROUP, RNN_HEAD_DIM, HEADS_PER_GROUP, RNN_HEAD_DIM)
    blocks = [d5[:, j, :, j, :] for j in range(HEADS_PER_GROUP)]
    return jnp.stack(blocks, axis=1).reshape(RNN_HEADS, RNN_HEAD_DIM, RNN_HEAD_DIM)


def _sgu_mask():
    chunk = jnp.arange(SGU_BLOCK) // CHUNK
    return (chunk[:, None] >= chunk[None, :]).astype(F32)


def _layer_small(sm, l):
    row = lambda v: v.reshape(1, -1)
    return dict(
        g1=row(sm["norm_mix_g"][l]), g2=row(sm["norm_ffn_g"][l]),
        wa=_block_diag(sm["lru_w_a"][l]).astype(BF), wx=_block_diag(sm["lru_w_x"][l]).astype(BF),
        ba=row(sm["lru_b_a"][l]), bx=row(sm["lru_b_x"][l]),
        sp=row(jax.nn.softplus(-sm["lru_lambda"][l])),
        cw=sm["conv_w"][l], cb=row(sm["conv_b"][l]),
        wm=(sm["sgu_w_s"][l] * _sgu_mask()).astype(BF),
        bsb=jnp.broadcast_to(sm["sgu_b_s"][l][:, :, None], (SGU_GROUPS, SGU_BLOCK, SGU_BLOCK)),
        lg=row(sm["sgu_ln_g"][l]), lb=row(sm["sgu_ln_b"][l]),
    )


def _layer_fwd_mix(x, big, p, ts):
    h = _norm_call(x, p["g1"], ts)
    proj = _inproj_call(h, big["w_in"], 0, ts)
    hr, ya_pre = _rnn_fwd_call(proj, p["wa"], p["wx"], p["ba"], p["bx"], p["sp"], p["cw"], p["cb"], ts)
    yb_pre = _sgu_fwd_call(proj, p["wm"], p["bsb"], p["lg"], p["lb"], ts)
    return dict(p=p, x=x, h=h, proj=proj, hr=hr, ya_pre=ya_pre, yb_pre=yb_pre)


def _layer_fwd_out(sv, big, ts):
    x1, ya, yb, merged, h2 = _merge_call(sv["x"], sv["proj"], sv["ya_pre"], sv["yb_pre"], big["w_branch_a"],
                                         big["w_branch_b"], big["w_out"], sv["p"]["g2"], 0, ts)
    x2, pre = _ffn_call(x1, h2, big["w_up"], big["w_down"], 0, ts)
    sv.update(x1=x1, ya=ya, yb=yb, merged=merged, h2=h2, pre=pre)
    return x2


def _layer_bwd_ffn(dx, sv, big, ts):
    p = sv["p"]
    dx1, dpre, dg2 = _ffn_bwd_call(dx, sv["pre"], sv["x1"], p["g2"], big["w_up"], big["w_down"], 0, ts)
    tk = dx.shape[0]
    gb = dict(
        w_down=_wgrad_call(sv["pre"], dx, Q_FF, D_MODEL // 2, tk, False, "wgrad_down", a_fn=_relu_sq),
        w_up=_wgrad_call(sv["h2"], dpre, D_MODEL, Q_FF, tk, True, "wgrad_up"))
    return dx1, gb, dict(norm_ffn_g=dg2[0])


def _layer_bwd_merge(dx1, sv, big, ts):
    tk = dx1.shape[0]
    dya, dyb, dgate, dya_pre, dyb_pre = _merge_bwd_call(
        dx1, sv["proj"], sv["ya"], sv["yb"], big["w_branch_a"], big["w_branch_b"], big["w_out"], 0, ts)
    gb = dict(
        w_out=_wgrad_call(sv["merged"], dx1, D_MODEL, D_MODEL // 2, tk, False, "wgrad_out"),
        w_branch_a=_wgrad_call(sv["ya_pre"], dya, D_RNN, D_MODEL // 2, tk, False, "wgrad_branch_a"),
        w_branch_b=_wgrad_call(sv["yb_pre"], dyb, D_SGU, D_MODEL // 2, tk, False, "wgrad_branch_b"))
    return (dgate, dya_pre, dyb_pre), gb


def _layer_bwd_branches(dx1, merge_out, sv, big, lam, ts):
    p = sv["p"]
    tk = dx1.shape[0]
    dgate, dya_pre, dyb_pre = merge_out
    gb = {}
    duv, dws, dbs, dlg, dlb = _sgu_bwd_call(dyb_pre, sv["proj"], p["wm"], p["bsb"], _sgu_mask(), p["lg"], p["lb"],
                                            ts)
    dxg, dwa, dwx, vec = _rnn_bwd_call(dya_pre, sv["proj"], sv["hr"], p["wa"], p["wx"], p["ba"], p["bx"],
                                       p["sp"], p["cw"], p["cb"], ts // 2)
    dx, dproj, dg1 = _inproj_bwd_call(dxg, duv, dgate, dx1, sv["x"], p["g1"], big["w_in"], 0, ts)
    gb["w_in"] = _wgrad_call(sv["h"], dproj, D_MODEL, Q_IN, tk // 2, True, "wgrad_in")
    gs = dict(
        norm_mix_g=dg1[0], conv_w=vec[_ROW_DCW:_ROW_DCW + CONV_WIDTH], conv_b=vec[_ROW_DCB],
        lru_w_a=_block_diag_extract(dwa), lru_w_x=_block_diag_extract(dwx),
        lru_b_a=vec[_ROW_DBA].reshape(RNN_HEADS, RNN_HEAD_DIM), lru_b_x=vec[_ROW_DBX].reshape(RNN_HEADS, RNN_HEAD_DIM),
        lru_lambda=-vec[_ROW_DSP] * jax.nn.sigmoid(-lam),
        sgu_ln_g=dlg[0], sgu_ln_b=dlb[0], sgu_w_s=dws, sgu_b_s=dbs.T)
    return dx, gb, gs


def _local_step(x, target, big, sm, ts):
    saved = []
    for l in range(DEPTH):
        sv = _layer_fwd_mix(x, big[l], _layer_small(sm, l), ts)
        x = _layer_fwd_out(sv, big[l], ts)
        saved.append(sv)
    dx, loss, dgf = _loss_call(x, target, sm["final_norm_g"].reshape(1, -1), ts)
    gb, gs = [None] * DEPTH, [None] * DEPTH
    for l in reversed(range(DEPTH)):
        dx1, gb_ffn, gs_ffn = _layer_bwd_ffn(dx, saved[l], big[l], ts)
        merge_out, gb_merge = _layer_bwd_merge(dx1, saved[l], big[l], ts)
        dx, gb_mix, gs_mix = _layer_bwd_branches(dx1, merge_out, saved[l], big[l], sm["lru_lambda"][l], ts)
        gb[l] = {**gb_ffn, **gb_merge, **gb_mix}
        gs[l] = {**gs_ffn, **gs_mix}
    gs = {k: jnp.stack([g[k] for g in gs]) for k in gs[0]}
    gs["final_norm_g"] = dgf[0]
    return loss, dx, gb, gs


EW_BLOCK_ELEMS = 384 * 1024


def _row_block(rows, cols):
    for br in range(min(rows, EW_BLOCK_ELEMS // cols), 0, -1):
        if rows % br == 0 and br % 16 == 0:
            return br
    return rows


def _ew_call(fn, name, operands, outputs, slabs=1, sel=None, into=None):
    rows, cols = outputs[0][0].shape[2:]
    br = _row_block(rows, cols)
    n_in = len(operands)

    def pick(tok, g, s):
        if callable(tok):
            return tok(g, s)
        if tok == "g":
            return g
        if isinstance(tok, tuple):
            return s[tok[1]]
        return tok

    def spec(idx):
        return pl.BlockSpec((None, None, br, cols),
                            lambda g, i, s, idx=idx: (pick(idx[0], g, s), pick(idx[1], g, s), i, 0))

    if sel is None:
        sel = jnp.zeros((1,), jnp.int32)
    in_specs = [spec(idx) for _, idx in operands]
    arrays = [a for a, _ in operands]
    aliases = {}
    if into is not None:
        in_specs.append(pl.BlockSpec(memory_space=pl.ANY))
        arrays.append(into)
        aliases = {1 + n_in: 0}

    def body(sel_ref, *refs):
        outs = fn(*[r[...] for r in refs[:n_in]])
        for o_ref, o in zip(refs[len(arrays):], outs):
            o_ref[...] = o.astype(o_ref.dtype)

    return pl.pallas_call(
        body, name=name, out_shape=[s for s, _ in outputs],
        grid_spec=pltpu.PrefetchScalarGridSpec(
            num_scalar_prefetch=1, grid=(slabs, rows // br),
            in_specs=in_specs,
            out_specs=[spec(idx) for _, idx in outputs]),
        input_output_aliases=aliases,
        compiler_params=_params(("parallel", "parallel")),
    )(sel, *arrays)


def _as4(a):
    return a.reshape((1,) * (4 - a.ndim) + a.shape)


def _adamw(w, g, m, v):
    m = ADAM_B1 * m + (1.0 - ADAM_B1) * g
    v = ADAM_B2 * v + (1.0 - ADAM_B2) * jnp.square(g)
    m_hat = m / (1.0 - ADAM_B1 ** ADAM_STEP)
    v_hat = v / (1.0 - ADAM_B2 ** ADAM_STEP)
    delta = -ADAM_LR * (m_hat / (jnp.sqrt(v_hat) + ADAM_EPS) + ADAM_WD * w)
    return delta, m, v


def _small_adamw_call(ws, gs, ms, vs):
    n = len(ws)

    def body(*refs):
        for k in range(n):
            w, g, m, v = (refs[j * n + k][...] for j in range(4))
            outs = _adamw(w, g, m, v)
            for j in range(3):
                refs[(4 + j) * n + k][...] = outs[j]

    shapes = [jax.ShapeDtypeStruct(w.shape, F32) for w in ws]
    outs = pl.pallas_call(
        body, name="adamw_small", out_shape=shapes * 3,
        in_specs=[pl.BlockSpec(memory_space=pltpu.VMEM)] * (4 * n),
        out_specs=[pl.BlockSpec(memory_space=pltpu.VMEM)] * (3 * n),
        compiler_params=_params(),
    )(*ws, *gs, *ms, *vs)
    return outs[:n], outs[n:2 * n], outs[2 * n:]


ANY = pl.BlockSpec(memory_space=pl.ANY)


def _place():
    x, y, c = lax.axis_index("x"), lax.axis_index("y"), lax.axis_index("c")
    chips = [(1 - x, y), (x, 1 - y), (1 - x, 1 - y)]
    return x, y, c, chips


def _remote(src, dst, send_sem, recv_sem, to):
    return pltpu.make_async_remote_copy(src_ref=src, dst_ref=dst, send_sem=send_sem, recv_sem=recv_sem,
                                        device_id=to, device_id_type=MESH)


def _gather_call(bufs):
    n = len(bufs)

    def body(*refs):
        out = refs[n:2 * n]
        send_sems, recv_sems = refs[2 * n:]
        x, y, c, chips = _place()
        me_q = 2 * x + y
        sibling = (x, y, 1 - c)
        first = []
        for w in range(n):
            for j, chip in enumerate(chips):
                mine = out[w].at[c, me_q]
                first.append(_remote(mine, mine, send_sems.at[w * 3 + j], recv_sems.at[w * 3 + j], (*chip, c)))
        for cp in first:
            cp.start()
        passed = []
        for w in range(n):
            for j, (qx, qy) in enumerate(chips):
                landed = out[w].at[c, 2 * qx + qy]
                k = w * 3 + j
                _remote(landed, landed, send_sems.at[k], recv_sems.at[k], (qx, qy, c)).wait_recv()
                cp = _remote(landed, landed, send_sems.at[3 * n + k], recv_sems.at[3 * n + k], sibling)
                cp.start()
                passed.append(cp)
        for w in range(n):
            for j, (qx, qy) in enumerate(chips):
                landed = out[w].at[1 - c, 2 * qx + qy]
                k = 3 * n + w * 3 + j
                _remote(landed, landed, send_sems.at[k], recv_sems.at[k], sibling).wait_recv()
        for cp in first + passed:
            cp.wait_send()

    return pl.pallas_call(
        body, name="gather_weights",
        out_shape=[jax.ShapeDtypeStruct(a.shape, a.dtype) for a in bufs],
        in_specs=[ANY] * n, out_specs=[ANY] * n,
        input_output_aliases={w: w for w in range(n)},
        scratch_shapes=[pltpu.SemaphoreType.DMA((6 * n,)), pltpu.SemaphoreType.DMA((6 * n,))],
        compiler_params=_params(vmem=False, has_side_effects=True),
    )(*bufs)


def _sibling_send_call(items):
    n = len(items)

    def body(*refs):
        src, out = refs[:n], refs[n:2 * n]
        send_sems, recv_sems = refs[2 * n:]
        x, y, c, _ = _place()
        copies = [_remote(src[w], out[w], send_sems.at[w], recv_sems.at[w], (x, y, 1 - c)) for w in range(n)]
        for cp in copies:
            cp.start()
        for cp in copies:
            cp.wait()

    return pl.pallas_call(
        body, name="grads_to_sibling",
        out_shape=[jax.ShapeDtypeStruct(a.shape, a.dtype) for a in items],
        in_specs=[ANY] * n, out_specs=[ANY] * n,
        scratch_shapes=[pltpu.SemaphoreType.DMA((n,)), pltpu.SemaphoreType.DMA((n,))],
        compiler_params=_params(vmem=False, has_side_effects=True),
    )(*items)


def _sibling_inplace_call(name, bufs, slabs, per_buf):
    n = len(bufs)

    def body(*refs):
        out = refs[n:2 * n]
        send_sems, recv_sems = refs[2 * n:]
        x, y, c, _ = _place()
        sibling = (x, y, 1 - c)
        pairs = [pair for ref in out for pair in slabs(ref, c)]
        sends = [_remote(s, s, send_sems.at[k], recv_sems.at[k], sibling) for k, (s, _) in enumerate(pairs)]
        for cp in sends:
            cp.start()
        for k, (_, r) in enumerate(pairs):
            _remote(r, r, send_sems.at[k], recv_sems.at[k], sibling).wait_recv()
        for cp in sends:
            cp.wait_send()

    return pl.pallas_call(
        body, name=name,
        out_shape=[jax.ShapeDtypeStruct(a.shape, a.dtype) for a in bufs],
        in_specs=[ANY] * n, out_specs=[ANY] * n,
        input_output_aliases={w: w for w in range(n)},
        scratch_shapes=[pltpu.SemaphoreType.DMA((per_buf * n,)), pltpu.SemaphoreType.DMA((per_buf * n,))],
        compiler_params=_params(vmem=False, has_side_effects=True),
    )(*bufs)


HBM_SPEC = pl.BlockSpec(memory_space=pltpu.HBM)
SEM_SPEC = pl.BlockSpec(memory_space=pltpu.SEMAPHORE)
DATAFLOW_EFFECT = pltpu.SideEffectType.DATAFLOW_SIDE_EFFECTING


def _exchange_start(name, bufs, copies, n_copies, after):
    n = len(bufs)

    def body(*refs):
        ins, send_sems, recv_sems, token = refs[:n], refs[n + 1], refs[n + 2], refs[-1]
        for k, (src, dst, to) in enumerate(copies(ins)):
            _remote(src, dst, send_sems.at[k], recv_sems.at[k], to).start()
        token[...] = jnp.zeros_like(token)

    outs = pl.pallas_call(
        body, name=name,
        out_shape=(pltpu.SemaphoreType.DMA((n_copies,)), pltpu.SemaphoreType.DMA((n_copies,)),
                   *[pltpu.HBM(b.shape, b.dtype) for b in bufs], jax.ShapeDtypeStruct((SUBLANES, 128), F32)),
        in_specs=[HBM_SPEC] * n + [ANY],
        out_specs=(SEM_SPEC, SEM_SPEC, *[HBM_SPEC] * n, pl.BlockSpec(memory_space=pltpu.VMEM)),
        input_output_aliases={w: w + 2 for w in range(n)},
        compiler_params=pltpu.CompilerParams(has_side_effects=DATAFLOW_EFFECT),
    )(*[pltpu.with_memory_space_constraint(b, pltpu.HBM) for b in bufs], after)
    return outs[0], outs[1], list(outs[2:2 + n]), outs[-1]


def _exchange_wait(name, send_sems, recv_sems, bufs, copies, after):
    n = len(bufs)

    def body(*refs):
        ins, send_sems, recv_sems = refs[:n], refs[n], refs[n + 1]
        for k, (src, dst, to) in enumerate(copies(ins)):
            cp = _remote(src, dst, send_sems.at[k], recv_sems.at[k], to)
            cp.wait_send()
            cp.wait_recv()

    return pl.pallas_call(
        body, name=name,
        out_shape=[pltpu.HBM(b.shape, b.dtype) for b in bufs],
        in_specs=[HBM_SPEC] * n + [SEM_SPEC, SEM_SPEC, ANY],
        out_specs=[HBM_SPEC] * n,
        input_output_aliases={w: w for w in range(n)},
        compiler_params=pltpu.CompilerParams(has_side_effects=DATAFLOW_EFFECT),
    )(*bufs, send_sems, recv_sems, after)


def _gather_copies(refs):
    x, y, c, chips = _place()
    mine = 2 * (2 * x + y) + c
    return [(ref.at[mine], ref.at[mine], (qx, qy, c)) for ref in refs for qx, qy in chips]


def _gather_forward_slabs(ref, c):
    x, y, _, chips = _place()
    return [(ref.at[2 * (2 * qx + qy) + c], ref.at[2 * (2 * qx + qy) + 1 - c]) for qx, qy in chips]


def _owner_copies(refs):
    n = len(refs) // 2
    x, y, c, chips = _place()
    return [(refs[w].at[2 * qx + qy], refs[n + w].at[j], (qx, qy, c))
            for w in range(n) for j, (qx, qy) in enumerate(chips)]


N_DEVICES = 8
SMALL_ROWS = 616


def _small_allreduce_call(buf):
    def body(in_ref, out_ref, recv_ref, red_ref, send_sems, recv_sems):
        x, y, c, _ = _place()
        me = 4 * x + 2 * y + c

        def peer(k):
            return (x ^ ((k >> 2) & 1), y ^ ((k >> 1) & 1), c ^ (k & 1))

        scatter = [_remote(in_ref.at[me ^ k], recv_ref.at[me], send_sems.at[k - 1], recv_sems.at[k - 1], peer(k))
                   for k in range(1, N_DEVICES)]
        for cp in scatter:
            cp.start()
        recv_ref[me] = in_ref[me]
        for k in range(1, N_DEVICES):
            landed = recv_ref.at[me ^ k]
            _remote(landed, landed, send_sems.at[k - 1], recv_sems.at[k - 1], peer(k)).wait_recv()
        total = recv_ref[0]
        for j in range(1, N_DEVICES):
            total = total + recv_ref[j]
        red_ref[...] = total
        out_ref[me] = total
        spread = [_remote(red_ref, out_ref.at[me], send_sems.at[6 + k], recv_sems.at[6 + k], peer(k))
                  for k in range(1, N_DEVICES)]
        for cp in spread:
            cp.start()
        for k in range(1, N_DEVICES):
            landed = out_ref.at[me ^ k]
            _remote(landed, landed, send_sems.at[6 + k], recv_sems.at[6 + k], peer(k)).wait_recv()
        for cp in scatter + spread:
            cp.wait_send()

    shape = (N_DEVICES, SMALL_ROWS, 128)
    return pl.pallas_call(
        body, name="allreduce_small",
        out_shape=jax.ShapeDtypeStruct(shape, F32),
        in_specs=[pl.BlockSpec(memory_space=pltpu.VMEM)],
        out_specs=pl.BlockSpec(memory_space=pltpu.VMEM),
        scratch_shapes=[pltpu.VMEM(shape, F32), pltpu.VMEM(shape[1:], F32),
                        pltpu.SemaphoreType.DMA((2 * (N_DEVICES - 1),)),
                        pltpu.SemaphoreType.DMA((2 * (N_DEVICES - 1),))],
        compiler_params=_params(has_side_effects=True),
    )(buf)


SMALL = ("norm_mix_g", "conv_w", "conv_b", "lru_w_a", "lru_b_a", "lru_w_x", "lru_b_x", "lru_lambda",
         "sgu_ln_g", "sgu_ln_b", "sgu_w_s", "sgu_b_s", "norm_ffn_g", "final_norm_g")
WEIGHTS = ("norm_mix_g", "w_in", "conv_w", "conv_b", "lru_w_a", "lru_b_a", "lru_w_x", "lru_b_x", "lru_lambda",
           "sgu_ln_g", "sgu_ln_b", "sgu_w_s", "sgu_b_s", "w_branch_a", "w_branch_b", "w_out", "norm_ffn_g",
           "w_up", "w_down", "final_norm_g")
PACK_ALIGN = SUBLANES * 128


def _pack_small(gs):
    parts = []
    for k in SMALL:
        flat = gs[k].reshape(-1)
        parts.append(jnp.pad(flat, (0, -flat.size % PACK_ALIGN)))
    flat = jnp.concatenate(parts)
    flat = jnp.pad(flat, (0, N_DEVICES * SMALL_ROWS * 128 - flat.size))
    return flat.reshape(N_DEVICES, SMALL_ROWS, 128)


def _unpack_small(buf, like):
    flat = buf.reshape(-1)
    out, off = {}, 0
    for k in SMALL:
        size = like[k].size
        out[k] = flat[off:off + size].reshape(like[k].shape)
        off += size + (-size % PACK_ALIGN)
    return out


def _as_rows(a):
    return a.reshape(-1, a.shape[-1])


def kernel(x, norm_mix_g, w_in, conv_w, conv_b, lru_w_a, lru_b_a, lru_w_x, lru_b_x, lru_lambda, sgu_ln_g, sgu_ln_b, sgu_w_s, sgu_b_s, w_branch_a, w_branch_b, w_out, norm_ffn_g, w_up, w_down, final_norm_g, loss_target, m_norm_mix_g, m_w_in, m_conv_w, m_conv_b, m_lru_w_a, m_lru_b_a, m_lru_w_x, m_lru_b_x, m_lru_lambda, m_sgu_ln_g, m_sgu_ln_b, m_sgu_w_s, m_sgu_b_s, m_w_branch_a, m_w_branch_b, m_w_out, m_norm_ffn_g, m_w_up, m_w_down, m_final_norm_g, v_norm_mix_g, v_w_in, v_conv_w, v_conv_b, v_lru_w_a, v_lru_b_a, v_lru_w_x, v_lru_b_x, v_lru_lambda, v_sgu_ln_g, v_sgu_ln_b, v_sgu_w_s, v_sgu_b_s, v_w_branch_a, v_w_branch_b, v_w_out, v_norm_ffn_g, v_w_up, v_w_down, v_final_norm_g):
    w = dict(norm_mix_g=norm_mix_g, w_in=w_in, conv_w=conv_w, conv_b=conv_b, lru_w_a=lru_w_a, lru_b_a=lru_b_a,
             lru_w_x=lru_w_x, lru_b_x=lru_b_x, lru_lambda=lru_lambda, sgu_ln_g=sgu_ln_g, sgu_ln_b=sgu_ln_b,
             sgu_w_s=sgu_w_s, sgu_b_s=sgu_b_s, w_branch_a=w_branch_a, w_branch_b=w_branch_b, w_out=w_out,
             norm_ffn_g=norm_ffn_g, w_up=w_up, w_down=w_down, final_norm_g=final_norm_g)
    m = dict(norm_mix_g=m_norm_mix_g, w_in=m_w_in, conv_w=m_conv_w, conv_b=m_conv_b, lru_w_a=m_lru_w_a,
             lru_b_a=m_lru_b_a, lru_w_x=m_lru_w_x, lru_b_x=m_lru_b_x, lru_lambda=m_lru_lambda,
             sgu_ln_g=m_sgu_ln_g, sgu_ln_b=m_sgu_ln_b, sgu_w_s=m_sgu_w_s, sgu_b_s=m_sgu_b_s,
             w_branch_a=m_w_branch_a, w_branch_b=m_w_branch_b, w_out=m_w_out, norm_ffn_g=m_norm_ffn_g,
             w_up=m_w_up, w_down=m_w_down, final_norm_g=m_final_norm_g)
    v = dict(norm_mix_g=v_norm_mix_g, w_in=v_w_in, conv_w=v_conv_w, conv_b=v_conv_b, lru_w_a=v_lru_w_a,
             lru_b_a=v_lru_b_a, lru_w_x=v_lru_w_x, lru_b_x=v_lru_b_x, lru_lambda=v_lru_lambda,
             sgu_ln_g=v_sgu_ln_g, sgu_ln_b=v_sgu_ln_b, sgu_w_s=v_sgu_w_s, sgu_b_s=v_sgu_b_s,
             w_branch_a=v_w_branch_a, w_branch_b=v_w_branch_b, w_out=v_w_out, norm_ffn_g=v_norm_ffn_g,
             w_up=v_w_up, w_down=v_w_down, final_norm_g=v_final_norm_g)
    core = lax.axis_index("c")
    chip = 2 * lax.axis_index("x") + lax.axis_index("y")
    sel = jnp.stack([core, 1 - core, chip]).astype(jnp.int32)
    this_core, other_core, this_chip = ("sel", 0), ("sel", 1), ("sel", 2)
    sds = jax.ShapeDtypeStruct

    ts = TOKEN_TILE
    halves = {k: (w[k].shape[1] // 2, w[k].shape[2]) for k in BIG}

    def half_view(k, a):
        return a.reshape((2 * N_QUARTERS,) + halves[k])

    def full_view(k, a):
        r2, cols = halves[k]
        if k in ("w_in", "w_up"):
            return a.reshape(1, N_QUARTERS, 2 * r2, cols)
        return a.reshape(1, 2 * N_QUARTERS * r2, cols)

    layer_bufs = [[], []]
    for k in BIG:
        _, r, cols = w[k].shape
        w4 = w[k].reshape(DEPTH, 1, r, cols)
        outs = _ew_call(lambda a, b: (a, b), "cast_weights", [(w4, (0, 0)), (w4, (1, 0))],
                        [(sds((1, N_QUARTERS, r, cols), BF), (0, this_chip))] * DEPTH, 1, sel)
        for l in range(DEPTH):
            layer_bufs[l].append(half_view(k, outs[l]))
    conv_buf = lax.dynamic_update_slice_in_dim(
        jnp.zeros((DEPTH, N_QUARTERS) + conv_w.shape[1:], F32), conv_w[:, None], chip, axis=1)
    sm = {k: w[k] for k in SMALL}
    sm["conv_w"] = _gather_call([conv_buf])[0].transpose(0, 2, 1, 3).reshape(DEPTH, CONV_WIDTH, D_RNN)

    def gather_start(tag, l, keys, after):
        bufs = [layer_bufs[l][BIG.index(k)] for k in keys]
        return _exchange_start(f"gather_start_{tag}", bufs, _gather_copies, 3 * len(keys), after)

    def gather_finish(tag, keys, started, after):
        send_sems, recv_sems, thru, _ = started
        landed = _exchange_wait(f"gather_wait_{tag}", send_sems, recv_sems, thru, _gather_copies, after)
        landed = _sibling_inplace_call("gather_forward", landed, _gather_forward_slabs, 3)
        return {k: full_view(k, a) for k, a in zip(keys, landed)}

    first, rest = ("w_in",), tuple(k for k in BIG if k != "w_in")
    started_a = gather_start("0a", 0, first, sm["conv_w"])
    started_b = gather_start("0b", 0, rest, started_a[3])
    started_1 = gather_start("1", 1, BIG, started_b[3])
    big0 = gather_finish("0a", first, started_a, started_1[3])
    sv0 = _layer_fwd_mix(x[0], big0, _layer_small(sm, 0), ts)
    big0.update(gather_finish("0b", rest, started_b, sv0["yb_pre"]))
    x_mid = _layer_fwd_out(sv0, big0, ts)
    big1 = gather_finish("1", BIG, started_1, x_mid)
    sv1 = _layer_fwd_mix(x_mid, big1, _layer_small(sm, 1), ts)
    x_out = _layer_fwd_out(sv1, big1, ts)
    dx, loss, dgf = _loss_call(x_out, loss_target[0], final_norm_g.reshape(1, -1), ts)

    def reduce_start(tag, gb, after):
        keys = tuple(gb)
        pairs = []
        for k in keys:
            r2, cols = halves[k]
            mine = half_view(k, gb[k])[None]
            to_sibling = _ew_call(lambda a: (a,), "cast_grads", [(mine, (0, lambda g, s: 2 * g + s[1]))],
                                  [(sds((1, N_QUARTERS, r2, cols), BF), (0, "g"))], N_QUARTERS, sel)[0]
            pairs.append((mine, to_sibling))
        from_sibling = _sibling_send_call([t for _, t in pairs])
        sums = [
            _ew_call(lambda a, b: (a + b.astype(F32),), "pair_sum",
                     [(mine, (0, lambda g, s: 2 * g + s[0])), (r, (0, "g"))],
                     [(sds(r.shape, BF), (0, "g"))], N_QUARTERS, sel)[0][0]
            for (mine, _), r in zip(pairs, from_sibling)]
        zones = [lax.empty((3,) + a.shape[1:], BF) for a in sums]
        started = _exchange_start(f"reduce_start_{tag}", sums + zones, _owner_copies, 3 * len(keys), after)
        return keys, started

    def reduce_finish(tag, l, keys_started, after, reduced):
        keys, (send_sems, recv_sems, thru, _) = keys_started
        done = _exchange_wait(f"reduce_wait_{tag}", send_sems, recv_sems, thru, _owner_copies, after)
        sums, zones = done[:len(keys)], done[len(keys):]
        for i, k in enumerate(keys):
            r2, cols = halves[k]
            reduced[k] = _ew_call(
                lambda a, b, c, d: (((a.astype(F32) + b.astype(F32)) + c.astype(F32)) + d.astype(F32),),
                "quarter_sum", [(sums[i][None], (0, this_chip))] + [(zones[i][None], (0, j)) for j in range(3)],
                [(sds((DEPTH, 2, r2, cols), F32), (l, this_core))], 1, sel, into=reduced.get(k))[0]

    def behind(params, key, started):
        return dict(params, **{key: params[key] + started[1][3][0, 0]})

    dx1, gb_ffn, gs1 = _layer_bwd_ffn(dx, sv1, big1, ts)
    merge_out, gb_merge = _layer_bwd_merge(dx1, sv1, big1, ts)
    dx_mid, gb_in, gs1_mix = _layer_bwd_branches(dx1, merge_out, sv1, big1, lru_lambda[1], ts)
    exchange_1 = reduce_start("1", {**gb_ffn, **gb_merge, **gb_in}, dx_mid)
    sv0["p"] = behind(sv0["p"], "g2", exchange_1)
    dx1, gb_ffn, gs0 = _layer_bwd_ffn(dx_mid, sv0, big0, ts)
    exchange_0a = reduce_start("0a", gb_ffn, exchange_1[1][3])
    merge_out, gb_merge = _layer_bwd_merge(dx1, sv0, big0, ts)
    exchange_0b = reduce_start("0b", gb_merge, exchange_0a[1][3])
    sv0["p"] = behind(sv0["p"], "lg", exchange_0b)
    grad_x, gb_in, gs0_mix = _layer_bwd_branches(dx1, merge_out, sv0, big0, lru_lambda[0], ts)
    exchange_0c = reduce_start("0c", gb_in, exchange_0b[1][3])
    reduced = {}
    reduce_finish("1", 1, exchange_1, exchange_0c[1][3], reduced)
    reduce_finish("0a", 0, exchange_0a, reduced["w_in"], reduced)
    reduce_finish("0b", 0, exchange_0b, reduced["w_down"], reduced)
    reduce_finish("0c", 0, exchange_0c, reduced["w_out"], reduced)
    swapped = _sibling_inplace_call(
        "grads_swap_halves", [reduced[k] for k in BIG],
        lambda ref, c: [(ref.at[l, c], ref.at[l, 1 - c]) for l in range(DEPTH)], DEPTH)
    grads_big = {k: g.reshape(w[k].shape) for k, g in zip(BIG, swapped)}
    layer_gs = [{**gs0, **gs0_mix}, {**gs1, **gs1_mix}]
    gs = {k: jnp.stack([g[k] for g in layer_gs]) for k in layer_gs[0]}
    gs["final_norm_g"] = dgf[0]

    like = {k: jax.ShapeDtypeStruct(sm[k].shape, F32) for k in SMALL}
    grads_small = _unpack_small(_small_allreduce_call(_pack_small(gs)), like)
    conv_q = grads_small["conv_w"].reshape(DEPTH, CONV_WIDTH, N_QUARTERS, D_RNN // N_QUARTERS)
    grads_small["conv_w"] = lax.dynamic_index_in_dim(conv_q, chip, axis=2, keepdims=False)

    delta, new_m, new_v = {}, {}, {}
    for k in BIG:
        views = [_as4(_as_rows(a)) for a in (w[k], grads_big[k], m[k], v[k])]
        outs = _ew_call(_adamw, "adamw_big", [(a, (0, 0)) for a in views],
                        [(sds(views[0].shape, F32), (0, 0))] * 3)
        delta[k], new_m[k], new_v[k] = (o.reshape(w[k].shape) for o in outs)
    outs = _small_adamw_call(*[[_as_rows(d[k]) for k in SMALL] for d in (w, grads_small, m, v)])
    for d, o in zip((delta, new_m, new_v), outs):
        for k, a in zip(SMALL, o):
            d[k] = a.reshape(w[k].shape)

    grads = {**grads_big, **grads_small}
    total = lax.psum(loss[0, 0], ("x", "y", "c"))
    return (total, grad_x[None], *[grads[k] for k in WEIGHTS], *[delta[k] for k in WEIGHTS],
            *[new_m[k] for k in WEIGHTS], *[new_v[k] for k in WEIGHTS])
```

```python
import functools
import math

import jax
import jax.numpy as jnp
from jax import lax
from jax.experimental import pallas as pl
from jax.experimental.pallas import tpu as pltpu

F32 = jnp.float32
BF = jnp.bfloat16

DEPTH = 2
D_MODEL = 1024
D_RNN = 1280
D_SGU = 1024
D_FF = 4096
D_IN = 2 * D_RNN + 2 * D_SGU + 2 * D_MODEL
N_QUARTERS = 4
Q_IN = D_IN // N_QUARTERS
Q_FF = D_FF // N_QUARTERS
RNN_HEADS = 20
RNN_HEAD_DIM = 64
LRU_GROUP = 256
N_LRU_GROUPS = D_RNN // LRU_GROUP
HEADS_PER_GROUP = LRU_GROUP // RNN_HEAD_DIM
CONV_WIDTH = 4
LRU_C = 8.0
SGU_GROUPS = 8
SGU_BLOCK = 128
CHUNK = 64
EPS = 1e-6

ADAM_LR = 0.001
ADAM_B1 = 0.9
ADAM_B2 = 0.999
ADAM_EPS = 1e-08
ADAM_WD = 0.01
ADAM_STEP = 10

SUBLANES = 8
TOKEN_TILE = 512
VMEM_LIMIT_BYTES = 56 * 1024 * 1024

MESH = pl.DeviceIdType.MESH


def _params(semantics=None, vmem=True, **kw):
    return pltpu.CompilerParams(
        dimension_semantics=semantics,
        vmem_limit_bytes=VMEM_LIMIT_BYTES if vmem else None,
        **kw,
    )


def _dot(a, b):
    return jnp.dot(a, b, preferred_element_type=F32)


def _dot_nt(a, b):
    return lax.dot_general(a, b, (((1,), (1,)), ((), ())), preferred_element_type=F32)


def _dot_tn(a, b):
    return lax.dot_general(a, b, (((0,), (0,)), ((), ())), preferred_element_type=F32)


_GELU_C = math.sqrt(2.0 / math.pi)
_GELU_A = 0.044715


def _gelu(x):
    return 0.5 * x * (1.0 + jnp.tanh(_GELU_C * (x + _GELU_A * x * x * x)))


def _gelu_and_grad(x):
    x2 = x * x
    t = jnp.tanh(_GELU_C * (x + _GELU_A * x2 * x))
    du = _GELU_C * (1.0 + 3.0 * _GELU_A * x2)
    return 0.5 * x * (1.0 + t), 0.5 * (1.0 + t) + 0.5 * x * (1.0 - t * t) * du


def _rms_stats(x):
    return lax.rsqrt(jnp.mean(x * x, axis=-1, keepdims=True) + EPS)


def _rms_bwd(dy, x, g):
    rs = _rms_stats(x)
    n = x * rs
    dn = dy * g
    dx = rs * (dn - n * jnp.mean(dn * n, axis=-1, keepdims=True))
    return dx, dy * n


def _row_sum(x):
    return jnp.sum(x, axis=0, keepdims=True)


def _tile_spec(ts, width, col=0):
    return pl.BlockSpec((ts, width), lambda i, col=col: (i, col))


def _full_spec(shape):
    zeros = (0,) * len(shape)
    return pl.BlockSpec(shape, lambda *_: zeros)


def _layer_spec(w, layer):
    zeros = (0,) * (w.ndim - 1)
    return pl.BlockSpec((None,) + tuple(w.shape[1:]), lambda *_: (layer,) + zeros)


def _norm_call(x, g, ts):
    s = x.shape[0]

    def body(x_ref, g_ref, h_ref):
        xv = x_ref[...]
        h_ref[...] = (xv * _rms_stats(xv) * g_ref[...]).astype(BF)

    return pl.pallas_call(
        body, name="norm_fwd", grid=(s // ts,),
        in_specs=[_tile_spec(ts, D_MODEL), _full_spec((1, D_MODEL))],
        out_specs=_tile_spec(ts, D_MODEL),
        out_shape=jax.ShapeDtypeStruct((s, D_MODEL), BF),
        compiler_params=_params(("parallel",)),
    )(x, g)


def _inproj_call(h, w_in, layer, ts):
    s = h.shape[0]

    def body(h_ref, w_ref, o_ref):
        o_ref[...] = _dot(h_ref[...], w_ref[...]).astype(BF)

    return pl.pallas_call(
        body, name="inproj_fwd", grid=(N_QUARTERS, s // ts),
        in_specs=[
            pl.BlockSpec((ts, D_MODEL), lambda q, i: (i, 0)),
            pl.BlockSpec((None, None, D_MODEL, Q_IN), lambda q, i: (layer, q, 0, 0)),
        ],
        out_specs=pl.BlockSpec((ts, Q_IN), lambda q, i: (i, q)),
        out_shape=jax.ShapeDtypeStruct((s, D_IN), BF),
        compiler_params=_params(("parallel", "parallel")),
    )(h, w_in)


def _shift_down(x, tail, s):
    xr = pltpu.roll(x, s, 0)
    tr = pltpu.roll(tail, s, 0)
    row = lax.broadcasted_iota(jnp.int32, tail.shape, 0)
    top = jnp.where(row < s, tr, xr[0:SUBLANES])
    return jnp.concatenate([top, xr[SUBLANES:]], axis=0)


def _shift_up(x, head, s):
    t = x.shape[0]
    xr = pltpu.roll(x, t - s, 0)
    hr = pltpu.roll(head, SUBLANES - s, 0)
    row = lax.broadcasted_iota(jnp.int32, head.shape, 0)
    bottom = jnp.where(row >= SUBLANES - s, hr, xr[t - SUBLANES:])
    return jnp.concatenate([xr[: t - SUBLANES], bottom], axis=0)


def _conv_fwd(x, tail, cw_ref, cb_ref):
    shifted = [x] + [_shift_down(x, tail, s) for s in range(1, CONV_WIDTH)]
    out = cb_ref[...] + cw_ref[CONV_WIDTH - 1:CONV_WIDTH, :] * x
    for s in range(1, CONV_WIDTH):
        k = CONV_WIDTH - 1 - s
        out = out + cw_ref[k:k + 1, :] * shifted[s]
    return out, shifted


def _group_dot(x_bf, w_ref, dot):
    cols = [dot(x_bf[:, g * LRU_GROUP:(g + 1) * LRU_GROUP], w_ref[g]) for g in range(N_LRU_GROUPS)]
    return jnp.concatenate(cols, axis=1)


def _lru_gates(xr, wa_ref, wx_ref, ba_ref, bx_ref, sp_ref):
    xb = xr.astype(BF)
    r = jax.nn.sigmoid(_group_dot(xb, wa_ref, _dot) + ba_ref[...])
    i = jax.nn.sigmoid(_group_dot(xb, wx_ref, _dot) + bx_ref[...])
    log_a = (-LRU_C * r) * sp_ref[...]
    a = jnp.exp(log_a)
    nrm2 = -jnp.tanh(log_a) * (a * a + 1.0)
    inv_nrm = lax.rsqrt(jnp.maximum(nrm2, 1e-36))
    return r, i, a, nrm2 * inv_nrm, inv_nrm


def _linear_scan(a, b, carry, al_ref, bl_ref, h_ref, reverse):
    t, c = a.shape
    rowm = lax.broadcasted_iota(jnp.int32, (t, c), 0) & (SUBLANES - 1)
    for d in (1, 2, 4):
        if reverse:
            keep, sh = rowm < SUBLANES - d, t - d
        else:
            keep, sh = rowm >= d, d
        a_sh = jnp.where(keep, pltpu.roll(a, sh, 0), 1.0)
        b_sh = jnp.where(keep, pltpu.roll(b, sh, 0), 0.0)
        b = a * b_sh + b
        a = a * a_sh
    al_ref[...] = a
    bl_ref[...] = b
    groups = t // SUBLANES

    def step(j, state):
        jj = groups - 1 - j if reverse else j
        off = pl.multiple_of(jj * SUBLANES, SUBLANES)
        rows = bl_ref[pl.ds(off, SUBLANES), :] + al_ref[pl.ds(off, SUBLANES), :] * state
        h_ref[pl.ds(off, SUBLANES), :] = rows
        last = rows[0:1, :] if reverse else rows[SUBLANES - 1:SUBLANES, :]
        return jnp.broadcast_to(last, (SUBLANES, c))

    out = lax.fori_loop(0, groups, step, jnp.broadcast_to(carry, (SUBLANES, c)))
    return out[0:1, :]


def _rnn_fwd_call(proj, wa, wx, ba, bx, sp, cw, cb, ts):
    s = proj.shape[0]

    def body(xg_ref, wa_ref, wx_ref, ba_ref, bx_ref, sp_ref, cw_ref, cb_ref, hr_ref, ya_ref,
             tail_sc, carry_sc, al_sc, bl_sc, h_sc):
        @pl.when(pl.program_id(0) == 0)
        def _():
            tail_sc[...] = jnp.zeros_like(tail_sc)
            carry_sc[...] = jnp.zeros_like(carry_sc)

        x = xg_ref[:, :D_RNN].astype(F32)
        g = xg_ref[:, D_RNN:].astype(F32)
        xr, _ = _conv_fwd(x, tail_sc[...], cw_ref, cb_ref)
        tail_sc[...] = x[ts - SUBLANES:, :]
        _, i, a, nrm, _ = _lru_gates(xr, wa_ref, wx_ref, ba_ref, bx_ref, sp_ref)
        carry_sc[...] = _linear_scan(a, nrm * (i * xr), carry_sc[...], al_sc, bl_sc, h_sc, False)
        h = h_sc[...]
        hr_ref[...] = h.astype(BF)
        ya_ref[...] = (h * _gelu(g)).astype(BF)

    gw = (N_LRU_GROUPS, LRU_GROUP, LRU_GROUP)
    return pl.pallas_call(
        body, name="rnn_fwd", grid=(s // ts,),
        in_specs=[_tile_spec(ts, 2 * D_RNN), _full_spec(gw), _full_spec(gw),
                  _full_spec((1, D_RNN)), _full_spec((1, D_RNN)), _full_spec((1, D_RNN)),
                  _full_spec((CONV_WIDTH, D_RNN)), _full_spec((1, D_RNN))],
        out_specs=[_tile_spec(ts, D_RNN), _tile_spec(ts, D_RNN)],
        out_shape=[jax.ShapeDtypeStruct((s, D_RNN), BF), jax.ShapeDtypeStruct((s, D_RNN), BF)],
        scratch_shapes=[pltpu.VMEM((SUBLANES, D_RNN), F32), pltpu.VMEM((1, D_RNN), F32),
                        pltpu.VMEM((ts, D_RNN), F32), pltpu.VMEM((ts, D_RNN), F32),
                        pltpu.VMEM((ts, D_RNN), F32)],
        compiler_params=_params(("arbitrary",)),
    )(proj, wa, wx, ba, bx, sp, cw, cb)


def _layernorm_fwd(x):
    mu = jnp.mean(x, axis=-1, keepdims=True)
    xc = x - mu
    rstd = lax.rsqrt(jnp.mean(xc * xc, axis=-1, keepdims=True) + EPS)
    return xc * rstd, rstd


def _sgu_mix(vn_bf, wm_ref, bsb_ref, ts):
    rows = []
    for blk in range(ts // SGU_BLOCK):
        r0 = blk * SGU_BLOCK
        cols = [
            _dot(wm_ref[g], vn_bf[r0:r0 + SGU_BLOCK, g * SGU_BLOCK:(g + 1) * SGU_BLOCK]) + bsb_ref[g]
            for g in range(SGU_GROUPS)
        ]
        rows.append(jnp.concatenate(cols, axis=1))
    return jnp.concatenate(rows, axis=0)


def _sgu_fwd_call(proj, wm, bsb, lg, lb, ts):
    s = proj.shape[0]

    def body(uv_ref, wm_ref, bsb_ref, lg_ref, lb_ref, yb_ref):
        gu = _gelu(uv_ref[:, :D_SGU].astype(F32))
        gv = _gelu(uv_ref[:, D_SGU:2 * D_SGU].astype(F32))
        nh, _ = _layernorm_fwd(gv)
        vn = (nh * lg_ref[...] + lb_ref[...]).astype(BF)
        yb_ref[...] = (gu * _sgu_mix(vn, wm_ref, bsb_ref, ts)).astype(BF)

    sw = (SGU_GROUPS, SGU_BLOCK, SGU_BLOCK)
    return pl.pallas_call(
        body, name="sgu_fwd", grid=(s // ts,),
        in_specs=[_tile_spec(ts, 2 * D_RNN, 1), _full_spec(sw), _full_spec(sw),
                  _full_spec((1, D_SGU)), _full_spec((1, D_SGU))],
        out_specs=_tile_spec(ts, D_SGU),
        out_shape=jax.ShapeDtypeStruct((s, D_SGU), BF),
        compiler_params=_params(("parallel",)),
    )(proj, wm, bsb, lg, lb)


_GATE_COL0 = (2 * D_RNN + 2 * D_SGU) // 512


def _gate_specs(ts):
    return [_tile_spec(ts, 512, _GATE_COL0 + j) for j in range(4)]


def _merge_call(x, proj, ya_pre, yb_pre, w_ba, w_bb, w_out, g2, layer, ts):
    s = x.shape[0]

    def body(x_ref, ga0, ga1, gb0, gb1, ya_ref, yb_ref, wa_ref, wb_ref, wo_ref, g2_ref,
             x1_ref, yao_ref, ybo_ref, mg_ref, h2_ref):
        ya = _dot(ya_ref[...], wa_ref[...])
        yb = _dot(yb_ref[...], wb_ref[...])
        sa = jax.nn.sigmoid(jnp.concatenate([ga0[...], ga1[...]], axis=1).astype(F32))
        sb = jax.nn.sigmoid(jnp.concatenate([gb0[...], gb1[...]], axis=1).astype(F32))
        merged = (sa * ya + sb * yb).astype(BF)
        x1 = x_ref[...] + _dot(merged, wo_ref[...])
        x1_ref[...] = x1
        yao_ref[...] = ya.astype(BF)
        ybo_ref[...] = yb.astype(BF)
        mg_ref[...] = merged
        h2_ref[...] = (x1 * _rms_stats(x1) * g2_ref[...]).astype(BF)

    act = jax.ShapeDtypeStruct((s, D_MODEL), BF)
    return pl.pallas_call(
        body, name="merge_fwd", grid=(s // ts,),
        in_specs=[_tile_spec(ts, D_MODEL)] + _gate_specs(ts) + [
            _tile_spec(ts, D_RNN), _tile_spec(ts, D_SGU),
            _layer_spec(w_ba, layer), _layer_spec(w_bb, layer), _layer_spec(w_out, layer),
            _full_spec((1, D_MODEL))],
        out_specs=[_tile_spec(ts, D_MODEL)] * 5,
        out_shape=[jax.ShapeDtypeStruct((s, D_MODEL), F32), act, act, act, act],
        compiler_params=_params(("parallel",)),
    )(x, proj, proj, proj, proj, ya_pre, yb_pre, w_ba, w_bb, w_out, g2)


def _ffn_call(x1, h2, w_up, w_down, layer, ts):
    s = x1.shape[0]

    def body(x1_ref, h2_ref, wu_ref, wd_ref, x2_ref, p_ref):
        h2v = h2_ref[...]
        acc = x1_ref[...]
        for q in range(N_QUARTERS):
            p = _dot(h2v, wu_ref[q])
            p_ref[:, q * Q_FF:(q + 1) * Q_FF] = p.astype(BF)
            f = jnp.square(jnp.maximum(p, 0.0)).astype(BF)
            acc = acc + _dot(f, wd_ref[q * Q_FF:(q + 1) * Q_FF, :])
        x2_ref[...] = acc

    return pl.pallas_call(
        body, name="ffn_fwd", grid=(s // ts,),
        in_specs=[_tile_spec(ts, D_MODEL), _tile_spec(ts, D_MODEL),
                  pl.BlockSpec((None, N_QUARTERS, D_MODEL, Q_FF), lambda i: (layer, 0, 0, 0)),
                  pl.BlockSpec((None, D_FF, D_MODEL), lambda i: (layer, 0, 0))],
        out_specs=[_tile_spec(ts, D_MODEL), _tile_spec(ts, D_FF)],
        out_shape=[jax.ShapeDtypeStruct((s, D_MODEL), F32), jax.ShapeDtypeStruct((s, D_FF), BF)],
        compiler_params=_params(("parallel",)),
    )(x1, h2, w_up, w_down)


def _loss_call(x, target, gf, ts):
    s = x.shape[0]

    def body(x_ref, t_ref, g_ref, dx_ref, loss_ref, dg_ref):
        @pl.when(pl.program_id(0) == 0)
        def _():
            loss_ref[...] = jnp.zeros_like(loss_ref)
            dg_ref[...] = jnp.zeros_like(dg_ref)

        xv = x_ref[...]
        gv = g_ref[...]
        err = xv * _rms_stats(xv) * gv - t_ref[...]
        part = 0.5 * jnp.sum(jnp.mean(err * err, axis=-1, keepdims=True), axis=0, keepdims=True)
        loss_ref[...] += jnp.broadcast_to(part, loss_ref.shape)
        dx, dg = _rms_bwd(err * (1.0 / D_MODEL), xv, gv)
        dx_ref[...] = dx
        dg_ref[...] += _row_sum(dg)

    return pl.pallas_call(
        body, name="loss_head", grid=(s // ts,),
        in_specs=[_tile_spec(ts, D_MODEL), _tile_spec(ts, D_MODEL), _full_spec((1, D_MODEL))],
        out_specs=[_tile_spec(ts, D_MODEL), _full_spec((1, 128)), _full_spec((1, D_MODEL))],
        out_shape=[jax.ShapeDtypeStruct((s, D_MODEL), F32), jax.ShapeDtypeStruct((1, 128), F32),
                   jax.ShapeDtypeStruct((1, D_MODEL), F32)],
        compiler_params=_params(("arbitrary",)),
    )(x, target, gf)


def _ffn_bwd_call(dx2, p, x1, g2, w_up, w_down, layer, ts):
    s = dx2.shape[0]

    def body(dx2_ref, p_ref, x1_ref, g2_ref, wu_ref, wd_ref, dx1_ref, dp_ref, dg_ref):
        @pl.when(pl.program_id(0) == 0)
        def _():
            dg_ref[...] = jnp.zeros_like(dg_ref)

        dx2v = dx2_ref[...]
        dyb = dx2v.astype(BF)
        dh2 = jnp.zeros((ts, D_MODEL), F32)
        for q in range(N_QUARTERS):
            cols = slice(q * Q_FF, (q + 1) * Q_FF)
            df = _dot_nt(dyb, wd_ref[cols, :])
            dp = (df * (2.0 * jnp.maximum(p_ref[:, cols].astype(F32), 0.0))).astype(BF)
            dp_ref[:, cols] = dp
            dh2 = dh2 + _dot_nt(dp, wu_ref[q])
        dx, dg = _rms_bwd(dh2, x1_ref[...], g2_ref[...])
        dx1_ref[...] = dx2v + dx
        dg_ref[...] += _row_sum(dg)

    return pl.pallas_call(
        body, name="ffn_bwd", grid=(s // ts,),
        in_specs=[_tile_spec(ts, D_MODEL), _tile_spec(ts, D_FF), _tile_spec(ts, D_MODEL),
                  _full_spec((1, D_MODEL)),
                  pl.BlockSpec((None, N_QUARTERS, D_MODEL, Q_FF), lambda i: (layer, 0, 0, 0)),
                  pl.BlockSpec((None, D_FF, D_MODEL), lambda i: (layer, 0, 0))],
        out_specs=[_tile_spec(ts, D_MODEL), _tile_spec(ts, D_FF), _full_spec((1, D_MODEL))],
        out_shape=[jax.ShapeDtypeStruct((s, D_MODEL), F32), jax.ShapeDtypeStruct((s, D_FF), BF),
                   jax.ShapeDtypeStruct((1, D_MODEL), F32)],
        compiler_params=_params(("arbitrary",)),
    )(dx2, p, x1, g2, w_up, w_down)


def _merge_bwd_call(dx1, proj, ya, yb, w_ba, w_bb, w_out, layer, ts):
    s = dx1.shape[0]

    def body(dx1_ref, ga0, ga1, gb0, gb1, ya_ref, yb_ref, wa_ref, wb_ref, wo_ref,
             dya_ref, dyb_ref, dgate_ref, dyap_ref, dybp_ref):
        dm = _dot_nt(dx1_ref[...].astype(BF), wo_ref[...])
        sa = jax.nn.sigmoid(jnp.concatenate([ga0[...], ga1[...]], axis=1).astype(F32))
        sb = jax.nn.sigmoid(jnp.concatenate([gb0[...], gb1[...]], axis=1).astype(F32))
        dya = (dm * sa).astype(BF)
        dyb = (dm * sb).astype(BF)
        dya_ref[...] = dya
        dyb_ref[...] = dyb
        dgate_ref[:, :D_MODEL] = (dm * ya_ref[...].astype(F32) * sa * (1.0 - sa)).astype(BF)
        dgate_ref[:, D_MODEL:] = (dm * yb_ref[...].astype(F32) * sb * (1.0 - sb)).astype(BF)
        dyap_ref[...] = _dot_nt(dya, wa_ref[...]).astype(BF)
        dybp_ref[...] = _dot_nt(dyb, wb_ref[...]).astype(BF)

    act = jax.ShapeDtypeStruct((s, D_MODEL), BF)
    return pl.pallas_call(
        body, name="merge_bwd", grid=(s // ts,),
        in_specs=[_tile_spec(ts, D_MODEL)] + _gate_specs(ts) + [
            _tile_spec(ts, D_MODEL), _tile_spec(ts, D_MODEL),
            _layer_spec(w_ba, layer), _layer_spec(w_bb, layer), _layer_spec(w_out, layer)],
        out_specs=[_tile_spec(ts, D_MODEL), _tile_spec(ts, D_MODEL), _tile_spec(ts, 2 * D_MODEL),
                   _tile_spec(ts, D_RNN), _tile_spec(ts, D_SGU)],
        out_shape=[act, act, jax.ShapeDtypeStruct((s, 2 * D_MODEL), BF),
                   jax.ShapeDtypeStruct((s, D_RNN), BF), jax.ShapeDtypeStruct((s, D_SGU), BF)],
        compiler_params=_params(("parallel",)),
    )(dx1, proj, proj, proj, proj, ya, yb, w_ba, w_bb, w_out)


def _sgu_bwd_call(dyb_pre, proj, wm, bsb, mask, lg, lb, ts):
    s = proj.shape[0]

    def body(dy_ref, uv_ref, wm_ref, bsb_ref, mask_ref, lg_ref, lb_ref,
             duv_ref, dws_ref, dbs_ref, dlg_ref, dlb_ref, dm_sc):
        step = pl.program_id(0)

        @pl.when(step == 0)
        def _():
            dws_ref[...] = jnp.zeros_like(dws_ref)
            dlg_ref[...] = jnp.zeros_like(dlg_ref)
            dlb_ref[...] = jnp.zeros_like(dlb_ref)
            dm_sc[...] = jnp.zeros_like(dm_sc)

        gu, dgu_du = _gelu_and_grad(uv_ref[:, :D_SGU].astype(F32))
        gv, dgv_dv = _gelu_and_grad(uv_ref[:, D_SGU:2 * D_SGU].astype(F32))
        nh, rstd = _layernorm_fwd(gv)
        lgv = lg_ref[...]
        vn = (nh * lgv + lb_ref[...]).astype(BF)
        dy = dy_ref[...].astype(F32)
        du = dy * _sgu_mix(vn, wm_ref, bsb_ref, ts) * dgu_du
        dmix = dy * gu
        dmix_bf = dmix.astype(BF)
        dm_acc = dm_sc[...]
        rows = []
        for blk in range(ts // SGU_BLOCK):
            r0 = blk * SGU_BLOCK
            dm_acc = dm_acc + dmix[r0:r0 + SGU_BLOCK, :]
            cols = []
            for g in range(SGU_GROUPS):
                c0 = g * SGU_BLOCK
                dmg = dmix_bf[r0:r0 + SGU_BLOCK, c0:c0 + SGU_BLOCK]
                cols.append(_dot_tn(wm_ref[g], dmg))
                dws_ref[g] += mask_ref[...] * _dot_nt(dmg, vn[r0:r0 + SGU_BLOCK, c0:c0 + SGU_BLOCK])
            rows.append(jnp.concatenate(cols, axis=1))
        dm_sc[...] = dm_acc
        dvn = jnp.concatenate(rows, axis=0)
        dlg_ref[...] += _row_sum(dvn * nh)
        dlb_ref[...] += _row_sum(dvn)
        dnh = dvn * lgv
        dgv = rstd * (dnh - jnp.mean(dnh, axis=-1, keepdims=True)
                      - nh * jnp.mean(dnh * nh, axis=-1, keepdims=True))
        duv_ref[:, :D_SGU] = du.astype(BF)
        duv_ref[:, D_SGU:] = (dgv * dgv_dv).astype(BF)

        @pl.when(step == pl.num_programs(0) - 1)
        def _():
            for g in range(SGU_GROUPS):
                dbs_ref[:, g:g + 1] = jnp.sum(
                    dm_acc[:, g * SGU_BLOCK:(g + 1) * SGU_BLOCK], axis=1, keepdims=True)

    sw = (SGU_GROUPS, SGU_BLOCK, SGU_BLOCK)
    return pl.pallas_call(
        body, name="sgu_bwd", grid=(s // ts,),
        in_specs=[_tile_spec(ts, D_SGU), _tile_spec(ts, 2 * D_RNN, 1), _full_spec(sw), _full_spec(sw),
                  _full_spec((SGU_BLOCK, SGU_BLOCK)), _full_spec((1, D_SGU)), _full_spec((1, D_SGU))],
        out_specs=[_tile_spec(ts, 2 * D_SGU), _full_spec(sw), _full_spec((SGU_BLOCK, SGU_GROUPS)),
                   _full_spec((1, D_SGU)), _full_spec((1, D_SGU))],
        out_shape=[jax.ShapeDtypeStruct((s, 2 * D_SGU), BF), jax.ShapeDtypeStruct(sw, F32),
                   jax.ShapeDtypeStruct((SGU_BLOCK, SGU_GROUPS), F32),
                   jax.ShapeDtypeStruct((1, D_SGU), F32), jax.ShapeDtypeStruct((1, D_SGU), F32)],
        scratch_shapes=[pltpu.VMEM((SGU_BLOCK, D_SGU), F32)],
        compiler_params=_params(("arbitrary",)),
    )(dyb_pre, proj, wm, bsb, mask, lg, lb)


_ROW_DBA, _ROW_DBX, _ROW_DSP, _ROW_DCB, _ROW_DCW = 0, 1, 2, 3, 4
_PREV_ROWS = 16


def _rnn_bwd_call(dya_pre, proj, hr, wa, wx, ba, bx, sp, cw, cb, ts):
    s = proj.shape[0]
    nt = s // ts
    per = ts // _PREV_ROWS

    def tile(i):
        return nt - 1 - i

    def prev(i):
        return jnp.maximum(tile(i) * per - 1, 0)

    def body(dy_ref, xg_ref, xgp_ref, hr_ref, hrp_ref, wa_ref, wx_ref, ba_ref, bx_ref, sp_ref,
             cw_ref, cb_ref, dxg_ref, dwa_ref, dwx_ref, vec_ref,
             lam_carry, a_first, dxr_head, al_sc, bl_sc, lam_sc):
        step = pl.program_id(0)

        @pl.when(step == 0)
        def _():
            dwa_ref[...] = jnp.zeros_like(dwa_ref)
            dwx_ref[...] = jnp.zeros_like(dwx_ref)
            vec_ref[...] = jnp.zeros_like(vec_ref)
            lam_carry[...] = jnp.zeros_like(lam_carry)
            a_first[...] = jnp.zeros_like(a_first)
            dxr_head[...] = jnp.zeros_like(dxr_head)

        has_prev = (step < nt - 1).astype(F32)
        x = xg_ref[:, :D_RNN].astype(F32)
        g = xg_ref[:, D_RNN:].astype(F32)
        x_tail = xgp_ref[_PREV_ROWS - SUBLANES:, :D_RNN].astype(F32) * has_prev
        h_tail = hrp_ref[_PREV_ROWS - SUBLANES:, :].astype(F32) * has_prev
        xr, x_shifted = _conv_fwd(x, x_tail, cw_ref, cb_ref)
        r, i, a, nrm, inv_nrm = _lru_gates(xr, wa_ref, wx_ref, ba_ref, bx_ref, sp_ref)
        h = hr_ref[...].astype(F32)
        dy = dy_ref[...].astype(F32)
        gg, dgg = _gelu_and_grad(g)

        coef = _shift_up(a, jnp.broadcast_to(a_first[...], (SUBLANES, D_RNN)), 1)
        lam_carry[...] = _linear_scan(coef, dy * gg, lam_carry[...], al_sc, bl_sc, lam_sc, True)
        a_first[...] = a[0:1, :]
        lam = lam_sc[...]

        da = lam * _shift_down(h, h_tail, 1)
        dnrm = lam * (i * xr)
        di = lam * nrm * xr
        dlog_a = da * a - dnrm * (a * a) * inv_nrm
        spv = sp_ref[...]
        dza = (dlog_a * (-LRU_C * spv)) * (r * (1.0 - r))
        dzx = di * (i * (1.0 - i))
        vec_ref[_ROW_DSP:_ROW_DSP + 1, :] += _row_sum(dlog_a * (-LRU_C * r))
        vec_ref[_ROW_DBA:_ROW_DBA + 1, :] += _row_sum(dza)
        vec_ref[_ROW_DBX:_ROW_DBX + 1, :] += _row_sum(dzx)
        xb = xr.astype(BF)
        dza_bf = dza.astype(BF)
        dzx_bf = dzx.astype(BF)
        for grp in range(N_LRU_GROUPS):
            cols = slice(grp * LRU_GROUP, (grp + 1) * LRU_GROUP)
            dwa_ref[grp] += _dot_tn(xb[:, cols], dza_bf[:, cols])
            dwx_ref[grp] += _dot_tn(xb[:, cols], dzx_bf[:, cols])
        dxr = (lam * nrm * i + _group_dot(dza_bf, wa_ref, _dot_nt) + _group_dot(dzx_bf, wx_ref, _dot_nt))

        vec_ref[_ROW_DCB:_ROW_DCB + 1, :] += _row_sum(dxr)
        head = dxr_head[...]
        dx = cw_ref[CONV_WIDTH - 1:CONV_WIDTH, :] * dxr
        vec_ref[_ROW_DCW + 3:_ROW_DCW + 4, :] += _row_sum(dxr * x)
        for sft in range(1, CONV_WIDTH):
            k = CONV_WIDTH - 1 - sft
            dx = dx + cw_ref[k:k + 1, :] * _shift_up(dxr, head, sft)
            vec_ref[_ROW_DCW + k:_ROW_DCW + k + 1, :] += _row_sum(dxr * x_shifted[sft])
        dxr_head[...] = dxr[0:SUBLANES, :]
        dxg_ref[:, :D_RNN] = dx.astype(BF)
        dxg_ref[:, D_RNN:] = (dy * h * dgg).astype(BF)

    gw = (N_LRU_GROUPS, LRU_GROUP, LRU_GROUP)
    rev = lambda width: pl.BlockSpec((ts, width), lambda i: (tile(i), 0))
    return pl.pallas_call(
        body, name="rnn_bwd", grid=(nt,),
        in_specs=[rev(D_RNN), rev(2 * D_RNN),
                  pl.BlockSpec((_PREV_ROWS, 2 * D_RNN), lambda i: (prev(i), 0)),
                  rev(D_RNN),
                  pl.BlockSpec((_PREV_ROWS, D_RNN), lambda i: (prev(i), 0)),
                  _full_spec(gw), _full_spec(gw),
                  _full_spec((1, D_RNN)), _full_spec((1, D_RNN)), _full_spec((1, D_RNN)),
                  _full_spec((CONV_WIDTH, D_RNN)), _full_spec((1, D_RNN))],
        out_specs=[rev(2 * D_RNN), _full_spec(gw), _full_spec(gw), _full_spec((SUBLANES, D_RNN))],
        out_shape=[jax.ShapeDtypeStruct((s, 2 * D_RNN), BF), jax.ShapeDtypeStruct(gw, F32),
                   jax.ShapeDtypeStruct(gw, F32), jax.ShapeDtypeStruct((SUBLANES, D_RNN), F32)],
        scratch_shapes=[pltpu.VMEM((1, D_RNN), F32), pltpu.VMEM((1, D_RNN), F32),
                        pltpu.VMEM((SUBLANES, D_RNN), F32),
                        pltpu.VMEM((ts, D_RNN), F32), pltpu.VMEM((ts, D_RNN), F32),
                        pltpu.VMEM((ts, D_RNN), F32)],
        compiler_params=_params(("arbitrary",)),
    )(dya_pre, proj, proj, hr, hr, wa, wx, ba, bx, sp, cw, cb)


def _inproj_bwd_call(dxg, duv, dgate, dx1, x, g1, w_in, layer, ts):
    s = x.shape[0]

    def body(dxg_ref, duv_ref, dgt_ref, dx1_ref, x_ref, g_ref, w_ref, dx_ref, dproj_ref, dg_ref):
        @pl.when(pl.program_id(0) == 0)
        def _():
            dg_ref[...] = jnp.zeros_like(dg_ref)

        dproj = jnp.concatenate([dxg_ref[...], duv_ref[...], dgt_ref[...]], axis=1)
        dproj_ref[...] = dproj
        dh = jnp.zeros((ts, D_MODEL), F32)
        for q in range(N_QUARTERS):
            dh = dh + _dot_nt(dproj[:, q * Q_IN:(q + 1) * Q_IN], w_ref[q])
        dx, dg = _rms_bwd(dh, x_ref[...], g_ref[...])
        dx_ref[...] = dx1_ref[...] + dx
        dg_ref[...] += _row_sum(dg)

    return pl.pallas_call(
        body, name="inproj_bwd", grid=(s // ts,),
        in_specs=[_tile_spec(ts, 2 * D_RNN), _tile_spec(ts, 2 * D_SGU), _tile_spec(ts, 2 * D_MODEL),
                  _tile_spec(ts, D_MODEL), _tile_spec(ts, D_MODEL), _full_spec((1, D_MODEL)),
                  pl.BlockSpec((None, N_QUARTERS, D_MODEL, Q_IN), lambda i: (layer, 0, 0, 0))],
        out_specs=[_tile_spec(ts, D_MODEL), _tile_spec(ts, D_IN), _full_spec((1, D_MODEL))],
        out_shape=[jax.ShapeDtypeStruct((s, D_MODEL), F32), jax.ShapeDtypeStruct((s, D_IN), BF),
                   jax.ShapeDtypeStruct((1, D_MODEL), F32)],
        compiler_params=_params(("arbitrary",)),
    )(dxg, duv, dgate, dx1, x, g1, w_in)


def _relu_sq(p):
    return jnp.square(jnp.maximum(p.astype(F32), 0.0))


def _wgrad_call(a, b, core, tm, tn, tk, col_blocked, name, a_fn=None):
    s, m = a.shape
    n = b.shape[1]
    r, cols = (m, n // N_QUARTERS) if col_blocked else (m // N_QUARTERS, n)
    r2 = r // 2
    per_tile = tm // r
    steps = s // tk

    def body(core_ref, a_ref, b_ref, keep_ref, send_ref, *acc):
        av = a_ref[...]
        if a_fn is not None:
            av = a_fn(av)
        prod = _dot_tn(av.astype(BF), b_ref[...].astype(BF))

        def emit(total):
            for h in range(2):
                @pl.when(core_ref[0] == h)
                def _():
                    for q in range(per_tile):
                        keep_ref[q] = total[q * r + h * r2:q * r + (h + 1) * r2]
                        send_ref[q] = total[q * r + (1 - h) * r2:q * r + (2 - h) * r2].astype(BF)

        if steps == 1:
            emit(prod)
        else:
            acc_ref, = acc
            step = pl.program_id(2)

            @pl.when(step == 0)
            def _():
                acc_ref[...] = prod

            @pl.when(jnp.logical_and(step > 0, step < steps - 1))
            def _():
                acc_ref[...] += prod

            @pl.when(step == steps - 1)
            def _():
                emit(acc_ref[...] + prod)

    if col_blocked:
        per_q = cols // tn
        out_spec = pl.BlockSpec((1, r2, tn), lambda i, j, k, c: (j // per_q, 0, j % per_q))
    else:
        out_spec = pl.BlockSpec((per_tile, r2, tn), lambda i, j, k, c: (i, 0, j))
    return pl.pallas_call(
        body, name=name,
        out_shape=[jax.ShapeDtypeStruct((N_QUARTERS, r2, cols), F32),
                   jax.ShapeDtypeStruct((N_QUARTERS, r2, cols), BF)],
        grid_spec=pltpu.PrefetchScalarGridSpec(
            num_scalar_prefetch=1, grid=(m // tm, n // tn, steps),
            in_specs=[pl.BlockSpec((tk, tm), lambda i, j, k, c: (k, i)),
                      pl.BlockSpec((tk, tn), lambda i, j, k, c: (k, j))],
            out_specs=[out_spec, out_spec],
            scratch_shapes=[] if steps == 1 else [pltpu.VMEM((tm, tn), F32)]),
        compiler_params=_params(("parallel", "parallel", "arbitrary")),
    )(core, a, b)


BIG = ("w_in", "w_up", "w_down", "w_branch_a", "w_branch_b", "w_out")


def _block_diag(w):
    w4 = w.reshape(N_LRU_GROUPS, HEADS_PER_GROUP, RNN_HEAD_DIM, RNN_HEAD_DIM)
    eye = jnp.eye(HEADS_PER_GROUP, dtype=w.dtype)
    return jnp.einsum("gjio,jk->gjiko", w4, eye).reshape(N_LRU_GROUPS, LRU_GROUP, LRU_GROUP)


def _block_diag_extract(d):
    d5 = d.reshape(N_LRU_GROUPS, HEADS_PER_GROUP, RNN_HEAD_DIM, HEADS_PER_GROUP, RNN_HEAD_DIM)
    blocks = [d5[:, j, :, j, :] for j in range(HEADS_PER_GROUP)]
    return jnp.stack(blocks, axis=1).reshape(RNN_HEADS, RNN_HEAD_DIM, RNN_HEAD_DIM)


def _sgu_mask():
    chunk = jnp.arange(SGU_BLOCK) // CHUNK
    return (chunk[:, None] >= chunk[None, :]).astype(F32)


def _layer_small(sm, l, core):
    row = lambda v: v.reshape(1, -1)
    return dict(
        core=core,
        g1=row(sm["norm_mix_g"][l]), g2=row(sm["norm_ffn_g"][l]),
        wa=_block_diag(sm["lru_w_a"][l]).astype(BF), wx=_block_diag(sm["lru_w_x"][l]).astype(BF),
        ba=row(sm["lru_b_a"][l]), bx=row(sm["lru_b_x"][l]),
        sp=row(jax.nn.softplus(-sm["lru_lambda"][l])),
        cw=sm["conv_w"][l], cb=row(sm["conv_b"][l]),
        wm=(sm["sgu_w_s"][l] * _sgu_mask()).astype(BF),
        bsb=jnp.broadcast_to(sm["sgu_b_s"][l][:, :, None], (SGU_GROUPS, SGU_BLOCK, SGU_BLOCK)),
        lg=row(sm["sgu_ln_g"][l]), lb=row(sm["sgu_ln_b"][l]),
    )


def _layer_fwd_mix(x, big, p, ts):
    h = _norm_call(x, p["g1"], ts)
    proj = _inproj_call(h, big["w_in"], 0, ts)
    hr, ya_pre = _rnn_fwd_call(proj, p["wa"], p["wx"], p["ba"], p["bx"], p["sp"], p["cw"], p["cb"], ts)
    yb_pre = _sgu_fwd_call(proj, p["wm"], p["bsb"], p["lg"], p["lb"], ts)
    return dict(p=p, x=x, h=h, proj=proj, hr=hr, ya_pre=ya_pre, yb_pre=yb_pre)


def _layer_fwd_out(sv, big, ts):
    x1, ya, yb, merged, h2 = _merge_call(sv["x"], sv["proj"], sv["ya_pre"], sv["yb_pre"], big["w_branch_a"],
                                         big["w_branch_b"], big["w_out"], sv["p"]["g2"], 0, ts)
    x2, pre = _ffn_call(x1, h2, big["w_up"], big["w_down"], 0, ts)
    sv.update(x1=x1, ya=ya, yb=yb, merged=merged, h2=h2, pre=pre)
    return x2


def _layer_bwd_ffn(dx, sv, big, ts):
    p = sv["p"]
    dx1, dpre, dg2 = _ffn_bwd_call(dx, sv["pre"], sv["x1"], p["g2"], big["w_up"], big["w_down"], 0, ts)
    tk = dx.shape[0]
    gb = dict(
        w_down=_wgrad_call(sv["pre"], dx, p["core"], Q_FF, D_MODEL // 2, tk, False, "wgrad_down", a_fn=_relu_sq),
        w_up=_wgrad_call(sv["h2"], dpre, p["core"], D_MODEL, Q_FF, tk, True, "wgrad_up"))
    return dx1, gb, dict(norm_ffn_g=dg2[0])


def _layer_bwd_merge(dx1, sv, big, ts):
    tk = dx1.shape[0]
    core = sv["p"]["core"]
    dya, dyb, dgate, dya_pre, dyb_pre = _merge_bwd_call(
        dx1, sv["proj"], sv["ya"], sv["yb"], big["w_branch_a"], big["w_branch_b"], big["w_out"], 0, ts)
    gb = dict(
        w_out=_wgrad_call(sv["merged"], dx1, core, D_MODEL, D_MODEL // 2, tk, False, "wgrad_out"),
        w_branch_a=_wgrad_call(sv["ya_pre"], dya, core, D_RNN, D_MODEL // 2, tk, False, "wgrad_branch_a"),
        w_branch_b=_wgrad_call(sv["yb_pre"], dyb, core, D_SGU, D_MODEL // 2, tk, False, "wgrad_branch_b"))
    return (dgate, dya_pre, dyb_pre), gb


def _layer_bwd_branches(dx1, merge_out, sv, big, lam, ts):
    p = sv["p"]
    tk = dx1.shape[0]
    dgate, dya_pre, dyb_pre = merge_out
    gb = {}
    duv, dws, dbs, dlg, dlb = _sgu_bwd_call(dyb_pre, sv["proj"], p["wm"], p["bsb"], _sgu_mask(), p["lg"], p["lb"],
                                            ts)
    dxg, dwa, dwx, vec = _rnn_bwd_call(dya_pre, sv["proj"], sv["hr"], p["wa"], p["wx"], p["ba"], p["bx"],
                                       p["sp"], p["cw"], p["cb"], ts // 2)
    dx, dproj, dg1 = _inproj_bwd_call(dxg, duv, dgate, dx1, sv["x"], p["g1"], big["w_in"], 0, ts)
    gb["w_in"] = _wgrad_call(sv["h"], dproj, p["core"], D_MODEL, Q_IN, tk // 2, True, "wgrad_in")
    gs = dict(
        norm_mix_g=dg1[0], conv_w=vec[_ROW_DCW:_ROW_DCW + CONV_WIDTH], conv_b=vec[_ROW_DCB],
        lru_w_a=_block_diag_extract(dwa), lru_w_x=_block_diag_extract(dwx),
        lru_b_a=vec[_ROW_DBA].reshape(RNN_HEADS, RNN_HEAD_DIM), lru_b_x=vec[_ROW_DBX].reshape(RNN_HEADS, RNN_HEAD_DIM),
        lru_lambda=-vec[_ROW_DSP] * jax.nn.sigmoid(-lam),
        sgu_ln_g=dlg[0], sgu_ln_b=dlb[0], sgu_w_s=dws, sgu_b_s=dbs.T)
    return dx, gb, gs


def _local_step(x, target, big, sm, ts):
    saved = []
    core = jnp.zeros((1,), jnp.int32)
    for l in range(DEPTH):
        sv = _layer_fwd_mix(x, big[l], _layer_small(sm, l, core), ts)
        x = _layer_fwd_out(sv, big[l], ts)
        saved.append(sv)
    dx, loss, dgf = _loss_call(x, target, sm["final_norm_g"].reshape(1, -1), ts)
    gb, gs = [None] * DEPTH, [None] * DEPTH
    for l in reversed(range(DEPTH)):
        dx1, gb_ffn, gs_ffn = _layer_bwd_ffn(dx, saved[l], big[l], ts)
        merge_out, gb_merge = _layer_bwd_merge(dx1, saved[l], big[l], ts)
        dx, gb_mix, gs_mix = _layer_bwd_branches(dx1, merge_out, saved[l], big[l], sm["lru_lambda"][l], ts)
        gb[l] = {**gb_ffn, **gb_merge, **gb_mix}
        gs[l] = {**gs_ffn, **gs_mix}
    gs = {k: jnp.stack([g[k] for g in gs]) for k in gs[0]}
    gs["final_norm_g"] = dgf[0]
    return loss, dx, gb, gs


EW_BLOCK_ELEMS = 384 * 1024


def _row_block(rows, cols):
    for br in range(min(rows, EW_BLOCK_ELEMS // cols), 0, -1):
        if rows % br == 0 and br % 16 == 0:
            return br
    return rows


def _ew_call(fn, name, operands, outputs, slabs=1, sel=None, into=None):
    rows, cols = outputs[0][0].shape[2:]
    br = _row_block(rows, cols)
    n_in = len(operands)

    def pick(tok, g, s):
        if callable(tok):
            return tok(g, s)
        if tok == "g":
            return g
        if isinstance(tok, tuple):
            return s[tok[1]]
        return tok

    def spec(idx):
        return pl.BlockSpec((None, None, br, cols),
                            lambda g, i, s, idx=idx: (pick(idx[0], g, s), pick(idx[1], g, s), i, 0))

    if sel is None:
        sel = jnp.zeros((1,), jnp.int32)
    in_specs = [spec(idx) for _, idx in operands]
    arrays = [a for a, _ in operands]
    aliases = {}
    if into is not None:
        in_specs.append(pl.BlockSpec(memory_space=pl.ANY))
        arrays.append(into)
        aliases = {1 + n_in: 0}

    def body(sel_ref, *refs):
        outs = fn(*[r[...] for r in refs[:n_in]])
        for o_ref, o in zip(refs[len(arrays):], outs):
            o_ref[...] = o.astype(o_ref.dtype)

    return pl.pallas_call(
        body, name=name, out_shape=[s for s, _ in outputs],
        grid_spec=pltpu.PrefetchScalarGridSpec(
            num_scalar_prefetch=1, grid=(slabs, rows // br),
            in_specs=in_specs,
            out_specs=[spec(idx) for _, idx in outputs]),
        input_output_aliases=aliases,
        compiler_params=_params(("parallel", "parallel")),
    )(sel, *arrays)


def _as4(a):
    return a.reshape((1,) * (4 - a.ndim) + a.shape)


def _adamw(w, g, m, v):
    m = ADAM_B1 * m + (1.0 - ADAM_B1) * g
    v = ADAM_B2 * v + (1.0 - ADAM_B2) * jnp.square(g)
    m_hat = m / (1.0 - ADAM_B1 ** ADAM_STEP)
    v_hat = v / (1.0 - ADAM_B2 ** ADAM_STEP)
    delta = -ADAM_LR * (m_hat / (jnp.sqrt(v_hat) + ADAM_EPS) + ADAM_WD * w)
    return delta, m, v


def _small_adamw_call(ws, gs, ms, vs):
    n = len(ws)

    def body(*refs):
        for k in range(n):
            w, g, m, v = (refs[j * n + k][...] for j in range(4))
            outs = _adamw(w, g, m, v)
            for j in range(3):
                refs[(4 + j) * n + k][...] = outs[j]

    shapes = [jax.ShapeDtypeStruct(w.shape, F32) for w in ws]
    outs = pl.pallas_call(
        body, name="adamw_small", out_shape=shapes * 3,
        in_specs=[pl.BlockSpec(memory_space=pltpu.VMEM)] * (4 * n),
        out_specs=[pl.BlockSpec(memory_space=pltpu.VMEM)] * (3 * n),
        compiler_params=_params(),
    )(*ws, *gs, *ms, *vs)
    return outs[:n], outs[n:2 * n], outs[2 * n:]


ANY = pl.BlockSpec(memory_space=pl.ANY)


def _place():
    x, y, c = lax.axis_index("x"), lax.axis_index("y"), lax.axis_index("c")
    chips = [(1 - x, y), (x, 1 - y), (1 - x, 1 - y)]
    return x, y, c, chips


def _remote(src, dst, send_sem, recv_sem, to):
    return pltpu.make_async_remote_copy(src_ref=src, dst_ref=dst, send_sem=send_sem, recv_sem=recv_sem,
                                        device_id=to, device_id_type=MESH)


def _gather_call(bufs):
    n = len(bufs)

    def body(*refs):
        out = refs[n:2 * n]
        send_sems, recv_sems = refs[2 * n:]
        x, y, c, chips = _place()
        me_q = 2 * x + y
        sibling = (x, y, 1 - c)
        first = []
        for w in range(n):
            for j, chip in enumerate(chips):
                mine = out[w].at[c, me_q]
                first.append(_remote(mine, mine, send_sems.at[w * 3 + j], recv_sems.at[w * 3 + j], (*chip, c)))
        for cp in first:
            cp.start()
        passed = []
        for w in range(n):
            for j, (qx, qy) in enumerate(chips):
                landed = out[w].at[c, 2 * qx + qy]
                k = w * 3 + j
                _remote(landed, landed, send_sems.at[k], recv_sems.at[k], (qx, qy, c)).wait_recv()
                cp = _remote(landed, landed, send_sems.at[3 * n + k], recv_sems.at[3 * n + k], sibling)
                cp.start()
                passed.append(cp)
        for w in range(n):
            for j, (qx, qy) in enumerate(chips):
                landed = out[w].at[1 - c, 2 * qx + qy]
                k = 3 * n + w * 3 + j
                _remote(landed, landed, send_sems.at[k], recv_sems.at[k], sibling).wait_recv()
        for cp in first + passed:
            cp.wait_send()

    return pl.pallas_call(
        body, name="gather_weights",
        out_shape=[jax.ShapeDtypeStruct(a.shape, a.dtype) for a in bufs],
        in_specs=[ANY] * n, out_specs=[ANY] * n,
        input_output_aliases={w: w for w in range(n)},
        scratch_shapes=[pltpu.SemaphoreType.DMA((6 * n,)), pltpu.SemaphoreType.DMA((6 * n,))],
        compiler_params=_params(vmem=False, has_side_effects=True),
    )(*bufs)


def _sibling_send_call(items):
    n = len(items)

    def body(*refs):
        src, out = refs[:n], refs[n:2 * n]
        send_sems, recv_sems = refs[2 * n:]
        x, y, c, _ = _place()
        copies = [_remote(src[w], out[w], send_sems.at[w], recv_sems.at[w], (x, y, 1 - c)) for w in range(n)]
        for cp in copies:
            cp.start()
        for cp in copies:
            cp.wait()

    return pl.pallas_call(
        body, name="grads_to_sibling",
        out_shape=[jax.ShapeDtypeStruct(a.shape, a.dtype) for a in items],
        in_specs=[ANY] * n, out_specs=[ANY] * n,
        scratch_shapes=[pltpu.SemaphoreType.DMA((n,)), pltpu.SemaphoreType.DMA((n,))],
        compiler_params=_params(vmem=False, has_side_effects=True),
    )(*items)


def _sibling_inplace_call(name, bufs, slabs, per_buf):
    n = len(bufs)

    def body(*refs):
        out = refs[n:2 * n]
        send_sems, recv_sems = refs[2 * n:]
        x, y, c, _ = _place()
        sibling = (x, y, 1 - c)
        pairs = [pair for ref in out for pair in slabs(ref, c)]
        sends = [_remote(s, s, send_sems.at[k], recv_sems.at[k], sibling) for k, (s, _) in enumerate(pairs)]
        for cp in sends:
            cp.start()
        for k, (_, r) in enumerate(pairs):
            _remote(r, r, send_sems.at[k], recv_sems.at[k], sibling).wait_recv()
        for cp in sends:
            cp.wait_send()

    return pl.pallas_call(
        body, name=name,
        out_shape=[jax.ShapeDtypeStruct(a.shape, a.dtype) for a in bufs],
        in_specs=[ANY] * n, out_specs=[ANY] * n,
        input_output_aliases={w: w for w in range(n)},
        scratch_shapes=[pltpu.SemaphoreType.DMA((per_buf * n,)), pltpu.SemaphoreType.DMA((per_buf * n,))],
        compiler_params=_params(vmem=False, has_side_effects=True),
    )(*bufs)


HBM_SPEC = pl.BlockSpec(memory_space=pltpu.HBM)
SEM_SPEC = pl.BlockSpec(memory_space=pltpu.SEMAPHORE)
DATAFLOW_EFFECT = pltpu.SideEffectType.DATAFLOW_SIDE_EFFECTING


def _exchange_start(name, bufs, copies, n_copies, after):
    n = len(bufs)

    def body(*refs):
        ins, send_sems, recv_sems, token = refs[:n], refs[n + 1], refs[n + 2], refs[-1]
        for k, (src, dst, to) in enumerate(copies(ins)):
            _remote(src, dst, send_sems.at[k], recv_sems.at[k], to).start()
        token[...] = jnp.zeros_like(token)

    outs = pl.pallas_call(
        body, name=name,
        out_shape=(pltpu.SemaphoreType.DMA((n_copies,)), pltpu.SemaphoreType.DMA((n_copies,)),
                   *[pltpu.HBM(b.shape, b.dtype) for b in bufs], jax.ShapeDtypeStruct((SUBLANES, 128), F32)),
        in_specs=[HBM_SPEC] * n + [ANY],
        out_specs=(SEM_SPEC, SEM_SPEC, *[HBM_SPEC] * n, pl.BlockSpec(memory_space=pltpu.VMEM)),
        input_output_aliases={w: w + 2 for w in range(n)},
        compiler_params=pltpu.CompilerParams(has_side_effects=DATAFLOW_EFFECT),
    )(*[pltpu.with_memory_space_constraint(b, pltpu.HBM) for b in bufs], after)
    return outs[0], outs[1], list(outs[2:2 + n]), outs[-1]


def _exchange_wait(name, send_sems, recv_sems, bufs, copies, after):
    n = len(bufs)

    def body(*refs):
        ins, send_sems, recv_sems = refs[:n], refs[n], refs[n + 1]
        for k, (src, dst, to) in enumerate(copies(ins)):
            cp = _remote(src, dst, send_sems.at[k], recv_sems.at[k], to)
            cp.wait_send()
            cp.wait_recv()

    return pl.pallas_call(
        body, name=name,
        out_shape=[pltpu.HBM(b.shape, b.dtype) for b in bufs],
        in_specs=[HBM_SPEC] * n + [SEM_SPEC, SEM_SPEC, ANY],
        out_specs=[HBM_SPEC] * n,
        input_output_aliases={w: w for w in range(n)},
        compiler_params=pltpu.CompilerParams(has_side_effects=DATAFLOW_EFFECT),
    )(*bufs, send_sems, recv_sems, after)


def _gather_copies(refs):
    x, y, c, chips = _place()
    mine = 2 * (2 * x + y) + c
    return [(ref.at[mine], ref.at[mine], (qx, qy, c)) for ref in refs for qx, qy in chips]


def _gather_forward_slabs(ref, c):
    x, y, _, chips = _place()
    return [(ref.at[2 * (2 * qx + qy) + c], ref.at[2 * (2 * qx + qy) + 1 - c]) for qx, qy in chips]


def _owner_copies(refs):
    n = len(refs) // 2
    x, y, c, chips = _place()
    return [(refs[w].at[2 * qx + qy], refs[n + w].at[j], (qx, qy, c))
            for w in range(n) for j, (qx, qy) in enumerate(chips)]


N_DEVICES = 8
SMALL_ROWS = 616


def _small_allreduce_call(buf):
    def body(in_ref, out_ref, recv_ref, red_ref, send_sems, recv_sems):
        x, y, c, _ = _place()
        me = 4 * x + 2 * y + c

        def peer(k):
            return (x ^ ((k >> 2) & 1), y ^ ((k >> 1) & 1), c ^ (k & 1))

        scatter = [_remote(in_ref.at[me ^ k], recv_ref.at[me], send_sems.at[k - 1], recv_sems.at[k - 1], peer(k))
                   for k in range(1, N_DEVICES)]
        for cp in scatter:
            cp.start()
        recv_ref[me] = in_ref[me]
        for k in range(1, N_DEVICES):
            landed = recv_ref.at[me ^ k]
            _remote(landed, landed, send_sems.at[k - 1], recv_sems.at[k - 1], peer(k)).wait_recv()
        total = recv_ref[0]
        for j in range(1, N_DEVICES):
            total = total + recv_ref[j]
        red_ref[...] = total
        out_ref[me] = total
        spread = [_remote(red_ref, out_ref.at[me], send_sems.at[6 + k], recv_sems.at[6 + k], peer(k))
                  for k in range(1, N_DEVICES)]
        for cp in spread:
            cp.start()
        for k in range(1, N_DEVICES):
            landed = out_ref.at[me ^ k]
            _remote(landed, landed, send_sems.at[6 + k], recv_sems.at[6 + k], peer(k)).wait_recv()
        for cp in scatter + spread:
            cp.wait_send()

    shape = (N_DEVICES, SMALL_ROWS, 128)
    return pl.pallas_call(
        body, name="allreduce_small",
        out_shape=jax.ShapeDtypeStruct(shape, F32),
        in_specs=[pl.BlockSpec(memory_space=pltpu.VMEM)],
        out_specs=pl.BlockSpec(memory_space=pltpu.VMEM),
        scratch_shapes=[pltpu.VMEM(shape, F32), pltpu.VMEM(shape[1:], F32),
                        pltpu.SemaphoreType.DMA((2 * (N_DEVICES - 1),)),
                        pltpu.SemaphoreType.DMA((2 * (N_DEVICES - 1),))],
        compiler_params=_params(has_side_effects=True),
    )(buf)


SMALL = ("norm_mix_g", "conv_w", "conv_b", "lru_w_a", "lru_b_a", "lru_w_x", "lru_b_x", "lru_lambda",
         "sgu_ln_g", "sgu_ln_b", "sgu_w_s", "sgu_b_s", "norm_ffn_g", "final_norm_g")
WEIGHTS = ("norm_mix_g", "w_in", "conv_w", "conv_b", "lru_w_a", "lru_b_a", "lru_w_x", "lru_b_x", "lru_lambda",
           "sgu_ln_g", "sgu_ln_b", "sgu_w_s", "sgu_b_s", "w_branch_a", "w_branch_b", "w_out", "norm_ffn_g",
           "w_up", "w_down", "final_norm_g")
PACK_ALIGN = SUBLANES * 128


def _pack_small(gs):
    parts = []
    for k in SMALL:
        flat = gs[k].reshape(-1)
        parts.append(jnp.pad(flat, (0, -flat.size % PACK_ALIGN)))
    flat = jnp.concatenate(parts)
    flat = jnp.pad(flat, (0, N_DEVICES * SMALL_ROWS * 128 - flat.size))
    return flat.reshape(N_DEVICES, SMALL_ROWS, 128)


def _unpack_small(buf, like):
    flat = buf.reshape(-1)
    out, off = {}, 0
    for k in SMALL:
        size = like[k].size
        out[k] = flat[off:off + size].reshape(like[k].shape)
        off += size + (-size % PACK_ALIGN)
    return out


def _as_rows(a):
    return a.reshape(-1, a.shape[-1])


def kernel(x, norm_mix_g, w_in, conv_w, conv_b, lru_w_a, lru_b_a, lru_w_x, lru_b_x, lru_lambda, sgu_ln_g, sgu_ln_b, sgu_w_s, sgu_b_s, w_branch_a, w_branch_b, w_out, norm_ffn_g, w_up, w_down, final_norm_g, loss_target, m_norm_mix_g, m_w_in, m_conv_w, m_conv_b, m_lru_w_a, m_lru_b_a, m_lru_w_x, m_lru_b_x, m_lru_lambda, m_sgu_ln_g, m_sgu_ln_b, m_sgu_w_s, m_sgu_b_s, m_w_branch_a, m_w_branch_b, m_w_out, m_norm_ffn_g, m_w_up, m_w_down, m_final_norm_g, v_norm_mix_g, v_w_in, v_conv_w, v_conv_b, v_lru_w_a, v_lru_b_a, v_lru_w_x, v_lru_b_x, v_lru_lambda, v_sgu_ln_g, v_sgu_ln_b, v_sgu_w_s, v_sgu_b_s, v_w_branch_a, v_w_branch_b, v_w_out, v_norm_ffn_g, v_w_up, v_w_down, v_final_norm_g):
    w = dict(norm_mix_g=norm_mix_g, w_in=w_in, conv_w=conv_w, conv_b=conv_b, lru_w_a=lru_w_a, lru_b_a=lru_b_a,
             lru_w_x=lru_w_x, lru_b_x=lru_b_x, lru_lambda=lru_lambda, sgu_ln_g=sgu_ln_g, sgu_ln_b=sgu_ln_b,
             sgu_w_s=sgu_w_s, sgu_b_s=sgu_b_s, w_branch_a=w_branch_a, w_branch_b=w_branch_b, w_out=w_out,
             norm_ffn_g=norm_ffn_g, w_up=w_up, w_down=w_down, final_norm_g=final_norm_g)
    m = dict(norm_mix_g=m_norm_mix_g, w_in=m_w_in, conv_w=m_conv_w, conv_b=m_conv_b, lru_w_a=m_lru_w_a,
             lru_b_a=m_lru_b_a, lru_w_x=m_lru_w_x, lru_b_x=m_lru_b_x, lru_lambda=m_lru_lambda,
             sgu_ln_g=m_sgu_ln_g, sgu_ln_b=m_sgu_ln_b, sgu_w_s=m_sgu_w_s, sgu_b_s=m_sgu_b_s,
             w_branch_a=m_w_branch_a, w_branch_b=m_w_branch_b, w_out=m_w_out, norm_ffn_g=m_norm_ffn_g,
             w_up=m_w_up, w_down=m_w_down, final_norm_g=m_final_norm_g)
    v = dict(norm_mix_g=v_norm_mix_g, w_in=v_w_in, conv_w=v_conv_w, conv_b=v_conv_b, lru_w_a=v_lru_w_a,
             lru_b_a=v_lru_b_a, lru_w_x=v_lru_w_x, lru_b_x=v_lru_b_x, lru_lambda=v_lru_lambda,
             sgu_ln_g=v_sgu_ln_g, sgu_ln_b=v_sgu_ln_b, sgu_w_s=v_sgu_w_s, sgu_b_s=v_sgu_b_s,
             w_branch_a=v_w_branch_a, w_branch_b=v_w_branch_b, w_out=v_w_out, norm_ffn_g=v_norm_ffn_g,
             w_up=v_w_up, w_down=v_w_down, final_norm_g=v_final_norm_g)
    core = lax.axis_index("c")
    chip = 2 * lax.axis_index("x") + lax.axis_index("y")
    sel = jnp.stack([core, 1 - core, chip]).astype(jnp.int32)
    this_core, other_core, this_chip = ("sel", 0), ("sel", 1), ("sel", 2)
    sds = jax.ShapeDtypeStruct

    ts = TOKEN_TILE
    halves = {k: (w[k].shape[1] // 2, w[k].shape[2]) for k in BIG}

    def half_view(k, a):
        return a.reshape((2 * N_QUARTERS,) + halves[k])

    def full_view(k, a):
        r2, cols = halves[k]
        if k in ("w_in", "w_up"):
            return a.reshape(1, N_QUARTERS, 2 * r2, cols)
        return a.reshape(1, 2 * N_QUARTERS * r2, cols)

    layer_bufs = [[], []]
    for k in BIG:
        _, r, cols = w[k].shape
        w4 = w[k].reshape(DEPTH, 1, r, cols)
        outs = _ew_call(lambda a, b: (a, b), "cast_weights", [(w4, (0, 0)), (w4, (1, 0))],
                        [(sds((1, N_QUARTERS, r, cols), BF), (0, this_chip))] * DEPTH, 1, sel)
        for l in range(DEPTH):
            layer_bufs[l].append(half_view(k, outs[l]))
    conv_buf = lax.dynamic_update_slice_in_dim(
        jnp.zeros((DEPTH, N_QUARTERS) + conv_w.shape[1:], F32), conv_w[:, None], chip, axis=1)
    sm = {k: w[k] for k in SMALL}
    sm["conv_w"] = _gather_call([conv_buf])[0].transpose(0, 2, 1, 3).reshape(DEPTH, CONV_WIDTH, D_RNN)

    def gather_start(tag, l, keys, after):
        bufs = [layer_bufs[l][BIG.index(k)] for k in keys]
        return _exchange_start(f"gather_start_{tag}", bufs, _gather_copies, 3 * len(keys), after)

    def gather_finish(tag, keys, started, after):
        send_sems, recv_sems, thru, _ = started
        landed = _exchange_wait(f"gather_wait_{tag}", send_sems, recv_sems, thru, _gather_copies, after)
        landed = _sibling_inplace_call("gather_forward", landed, _gather_forward_slabs, 3)
        return {k: full_view(k, a) for k, a in zip(keys, landed)}

    first, rest = ("w_in",), tuple(k for k in BIG if k != "w_in")
    started_a = gather_start("0a", 0, first, sm["conv_w"])
    started_b = gather_start("0b", 0, rest, started_a[3])
    started_1 = gather_start("1", 1, BIG, started_b[3])
    big0 = gather_finish("0a", first, started_a, started_1[3])
    sv0 = _layer_fwd_mix(x[0], big0, _layer_small(sm, 0, sel[0:1]), ts)
    big0.update(gather_finish("0b", rest, started_b, sv0["yb_pre"]))
    x_mid = _layer_fwd_out(sv0, big0, ts)
    big1 = gather_finish("1", BIG, started_1, x_mid)
    sv1 = _layer_fwd_mix(x_mid, big1, _layer_small(sm, 1, sel[0:1]), ts)
    x_out = _layer_fwd_out(sv1, big1, ts)
    dx, loss, dgf = _loss_call(x_out, loss_target[0], final_norm_g.reshape(1, -1), ts)

    def reduce_start(tag, gb, after):
        keys = tuple(gb)
        from_sibling = _sibling_send_call([gb[k][1] for k in keys])
        sums = [
            _ew_call(lambda a, b: (a + b.astype(F32),), "pair_sum", [(gb[k][0][None], (0, "g")), (r[None], (0, "g"))],
                     [(sds((1,) + r.shape, BF), (0, "g"))], N_QUARTERS)[0][0]
            for k, r in zip(keys, from_sibling)]
        zones = [lax.empty((3,) + a.shape[1:], BF) for a in sums]
        started = _exchange_start(f"reduce_start_{tag}", sums + zones, _owner_copies, 3 * len(keys), after)
        return keys, started

    def reduce_finish(tag, l, keys_started, after, reduced):
        keys, (send_sems, recv_sems, thru, _) = keys_started
        done = _exchange_wait(f"reduce_wait_{tag}", send_sems, recv_sems, thru, _owner_copies, after)
        sums, zones = done[:len(keys)], done[len(keys):]
        for i, k in enumerate(keys):
            r2, cols = halves[k]
            reduced[k] = _ew_call(
                lambda a, b, c, d: (((a.astype(F32) + b.astype(F32)) + c.astype(F32)) + d.astype(F32),),
                "quarter_sum", [(sums[i][None], (0, this_chip))] + [(zones[i][None], (0, j)) for j in range(3)],
                [(sds((DEPTH, 2, r2, cols), F32), (l, this_core))], 1, sel, into=reduced.get(k))[0]

    def behind(params, key, started):
        return dict(params, **{key: params[key] + started[1][3][0, 0]})

    dx1, gb_ffn, gs1 = _layer_bwd_ffn(dx, sv1, big1, ts)
    merge_out, gb_merge = _layer_bwd_merge(dx1, sv1, big1, ts)
    dx_mid, gb_in, gs1_mix = _layer_bwd_branches(dx1, merge_out, sv1, big1, lru_lambda[1], ts)
    exchange_1 = reduce_start("1", {**gb_ffn, **gb_merge, **gb_in}, dx_mid)
    sv0["p"] = behind(sv0["p"], "g2", exchange_1)
    dx1, gb_ffn, gs0 = _layer_bwd_ffn(dx_mid, sv0, big0, ts)
    exchange_0a = reduce_start("0a", gb_ffn, exchange_1[1][3])
    merge_out, gb_merge = _layer_bwd_merge(dx1, sv0, big0, ts)
    exchange_0b = reduce_start("0b", gb_merge, exchange_0a[1][3])
    sv0["p"] = behind(sv0["p"], "lg", exchange_0b)
    grad_x, gb_in, gs0_mix = _layer_bwd_branches(dx1, merge_out, sv0, big0, lru_lambda[0], ts)
    exchange_0c = reduce_start("0c", gb_in, exchange_0b[1][3])
    reduced = {}
    reduce_finish("1", 1, exchange_1, exchange_0c[1][3], reduced)
    reduce_finish("0a", 0, exchange_0a, reduced["w_in"], reduced)
    reduce_finish("0b", 0, exchange_0b, reduced["w_down"], reduced)
    reduce_finish("0c", 0, exchange_0c, reduced["w_out"], reduced)
    swapped = _sibling_inplace_call(
        "grads_swap_halves", [reduced[k] for k in BIG],
        lambda ref, c: [(ref.at[l, c], ref.at[l, 1 - c]) for l in range(DEPTH)], DEPTH)
    grads_big = {k: g.reshape(w[k].shape) for k, g in zip(BIG, swapped)}
    layer_gs = [{**gs0, **gs0_mix}, {**gs1, **gs1_mix}]
    gs = {k: jnp.stack([g[k] for g in layer_gs]) for k in layer_gs[0]}
    gs["final_norm_g"] = dgf[0]

    like = {k: jax.ShapeDtypeStruct(sm[k].shape, F32) for k in SMALL}
    grads_small = _unpack_small(_small_allreduce_call(_pack_small(gs)), like)
    conv_q = grads_small["conv_w"].reshape(DEPTH, CONV_WIDTH, N_QUARTERS, D_RNN // N_QUARTERS)
    grads_small["conv_w"] = lax.dynamic_index_in_dim(conv_q, chip, axis=2, keepdims=False)

    delta, new_m, new_v = {}, {}, {}
    for k in BIG:
        views = [_as4(_as_rows(a)) for a in (w[k], grads_big[k], m[k], v[k])]
        outs = _ew_call(_adamw, "adamw_big", [(a, (0, 0)) for a in views],
                        [(sds(views[0].shape, F32), (0, 0))] * 3)
        delta[k], new_m[k], new_v[k] = (o.reshape(w[k].shape) for o in outs)
    outs = _small_adamw_call(*[[_as_rows(d[k]) for k in SMALL] for d in (w, grads_small, m, v)])
    for d, o in zip((delta, new_m, new_v), outs):
        for k, a in zip(SMALL, o):
            d[k] = a.reshape(w[k].shape)

    grads = {**grads_big, **grads_small}
    total = lax.psum(loss[0, 0], ("x", "y", "c"))
    return (total, grad_x[None], *[grads[k] for k in WEIGHTS], *[delta[k] for k in WEIGHTS],
            *[new_m[k] for k in WEIGHTS], *[new_v[k] for k in WEIGHTS])
```

```python
import functools
import math

import jax
import jax.numpy as jnp
from jax import lax
from jax.experimental import pallas as pl
from jax.experimental.pallas import tpu as pltpu

F32 = jnp.float32
BF = jnp.bfloat16

DEPTH = 2
D_MODEL = 1024
D_RNN = 1280
D_SGU = 1024
D_FF = 4096
D_IN = 2 * D_RNN + 2 * D_SGU + 2 * D_MODEL
N_QUARTERS = 4
Q_IN = D_IN // N_QUARTERS
Q_FF = D_FF // N_QUARTERS
RNN_HEADS = 20
RNN_HEAD_DIM = 64
LRU_GROUP = 256
N_LRU_GROUPS = D_RNN // LRU_GROUP
HEADS_PER_GROUP = LRU_GROUP // RNN_HEAD_DIM
CONV_WIDTH = 4
LRU_C = 8.0
SGU_GROUPS = 8
SGU_BLOCK = 128
CHUNK = 64
EPS = 1e-6

ADAM_LR = 0.001
ADAM_B1 = 0.9
ADAM_B2 = 0.999
ADAM_EPS = 1e-08
ADAM_WD = 0.01
ADAM_STEP = 10

SUBLANES = 8
TOKEN_TILE = 512
VMEM_LIMIT_BYTES = 56 * 1024 * 1024

MESH = pl.DeviceIdType.MESH


def _params(semantics=None, vmem=True, **kw):
    return pltpu.CompilerParams(
        dimension_semantics=semantics,
        vmem_limit_bytes=VMEM_LIMIT_BYTES if vmem else None,
        **kw,
    )


def _dot(a, b):
    return jnp.dot(a, b, preferred_element_type=F32)


def _dot_nt(a, b):
    return lax.dot_general(a, b, (((1,), (1,)), ((), ())), preferred_element_type=F32)


def _dot_tn(a, b):
    return lax.dot_general(a, b, (((0,), (0,)), ((), ())), preferred_element_type=F32)


_GELU_C = math.sqrt(2.0 / math.pi)
_GELU_A = 0.044715


def _gelu(x):
    return 0.5 * x * (1.0 + jnp.tanh(_GELU_C * (x + _GELU_A * x * x * x)))


def _gelu_and_grad(x):
    x2 = x * x
    t = jnp.tanh(_GELU_C * (x + _GELU_A * x2 * x))
    du = _GELU_C * (1.0 + 3.0 * _GELU_A * x2)
    return 0.5 * x * (1.0 + t), 0.5 * (1.0 + t) + 0.5 * x * (1.0 - t * t) * du


def _rms_stats(x):
    return lax.rsqrt(jnp.mean(x * x, axis=-1, keepdims=True) + EPS)


def _rms_bwd(dy, x, g):
    rs = _rms_stats(x)
    n = x * rs
    dn = dy * g
    dx = rs * (dn - n * jnp.mean(dn * n, axis=-1, keepdims=True))
    return dx, dy * n


def _row_sum(x):
    return jnp.sum(x, axis=0, keepdims=True)


def _tile_spec(ts, width, col=0):
    return pl.BlockSpec((ts, width), lambda i, col=col: (i, col))


def _full_spec(shape):
    zeros = (0,) * len(shape)
    return pl.BlockSpec(shape, lambda *_: zeros)


def _layer_spec(w, layer):
    zeros = (0,) * (w.ndim - 1)
    return pl.BlockSpec((None,) + tuple(w.shape[1:]), lambda *_: (layer,) + zeros)


def _norm_call(x, g, ts):
    s = x.shape[0]

    def body(x_ref, g_ref, h_ref):
        xv = x_ref[...]
        h_ref[...] = (xv * _rms_stats(xv) * g_ref[...]).astype(BF)

    return pl.pallas_call(
        body, name="norm_fwd", grid=(s // ts,),
        in_specs=[_tile_spec(ts, D_MODEL), _full_spec((1, D_MODEL))],
        out_specs=_tile_spec(ts, D_MODEL),
        out_shape=jax.ShapeDtypeStruct((s, D_MODEL), BF),
        compiler_params=_params(("parallel",)),
    )(x, g)


def _inproj_call(h, w_in, layer, ts):
    s = h.shape[0]

    def body(h_ref, w_ref, o_ref):
        o_ref[...] = _dot(h_ref[...], w_ref[...]).astype(BF)

    return pl.pallas_call(
        body, name="inproj_fwd", grid=(N_QUARTERS, s // ts),
        in_specs=[
            pl.BlockSpec((ts, D_MODEL), lambda q, i: (i, 0)),
            pl.BlockSpec((None, None, D_MODEL, Q_IN), lambda q, i: (layer, q, 0, 0)),
        ],
        out_specs=pl.BlockSpec((ts, Q_IN), lambda q, i: (i, q)),
        out_shape=jax.ShapeDtypeStruct((s, D_IN), BF),
        compiler_params=_params(("parallel", "parallel")),
    )(h, w_in)


def _shift_down(x, tail, s):
    xr = pltpu.roll(x, s, 0)
    tr = pltpu.roll(tail, s, 0)
    row = lax.broadcasted_iota(jnp.int32, tail.shape, 0)
    top = jnp.where(row < s, tr, xr[0:SUBLANES])
    return jnp.concatenate([top, xr[SUBLANES:]], axis=0)


def _shift_up(x, head, s):
    t = x.shape[0]
    xr = pltpu.roll(x, t - s, 0)
    hr = pltpu.roll(head, SUBLANES - s, 0)
    row = lax.broadcasted_iota(jnp.int32, head.shape, 0)
    bottom = jnp.where(row >= SUBLANES - s, hr, xr[t - SUBLANES:])
    return jnp.concatenate([xr[: t - SUBLANES], bottom], axis=0)


def _conv_fwd(x, tail, cw_ref, cb_ref):
    shifted = [x] + [_shift_down(x, tail, s) for s in range(1, CONV_WIDTH)]
    out = cb_ref[...] + cw_ref[CONV_WIDTH - 1:CONV_WIDTH, :] * x
    for s in range(1, CONV_WIDTH):
        k = CONV_WIDTH - 1 - s
        out = out + cw_ref[k:k + 1, :] * shifted[s]
    return out, shifted


def _group_dot(x_bf, w_ref, dot):
    cols = [dot(x_bf[:, g * LRU_GROUP:(g + 1) * LRU_GROUP], w_ref[g]) for g in range(N_LRU_GROUPS)]
    return jnp.concatenate(cols, axis=1)


def _lru_gates(xr, wa_ref, wx_ref, ba_ref, bx_ref, sp_ref):
    xb = xr.astype(BF)
    r = jax.nn.sigmoid(_group_dot(xb, wa_ref, _dot) + ba_ref[...])
    i = jax.nn.sigmoid(_group_dot(xb, wx_ref, _dot) + bx_ref[...])
    log_a = (-LRU_C * r) * sp_ref[...]
    a = jnp.exp(log_a)
    nrm2 = -jnp.tanh(log_a) * (a * a + 1.0)
    inv_nrm = lax.rsqrt(jnp.maximum(nrm2, 1e-36))
    return r, i, a, nrm2 * inv_nrm, inv_nrm


def _linear_scan(a, b, carry, al_ref, bl_ref, h_ref, reverse):
    t, c = a.shape
    rowm = lax.broadcasted_iota(jnp.int32, (t, c), 0) & (SUBLANES - 1)
    for d in (1, 2, 4):
        if reverse:
            keep, sh = rowm < SUBLANES - d, t - d
        else:
            keep, sh = rowm >= d, d
        a_sh = jnp.where(keep, pltpu.roll(a, sh, 0), 1.0)
        b_sh = jnp.where(keep, pltpu.roll(b, sh, 0), 0.0)
        b = a * b_sh + b
        a = a * a_sh
    al_ref[...] = a
    bl_ref[...] = b
    groups = t // SUBLANES

    def step(j, state):
        jj = groups - 1 - j if reverse else j
        off = pl.multiple_of(jj * SUBLANES, SUBLANES)
        rows = bl_ref[pl.ds(off, SUBLANES), :] + al_ref[pl.ds(off, SUBLANES), :] * state
        h_ref[pl.ds(off, SUBLANES), :] = rows
        last = rows[0:1, :] if reverse else rows[SUBLANES - 1:SUBLANES, :]
        return jnp.broadcast_to(last, (SUBLANES, c))

    out = lax.fori_loop(0, groups, step, jnp.broadcast_to(carry, (SUBLANES, c)))
    return out[0:1, :]


def _rnn_fwd_call(proj, wa, wx, ba, bx, sp, cw, cb, ts):
    s = proj.shape[0]

    def body(xg_ref, wa_ref, wx_ref, ba_ref, bx_ref, sp_ref, cw_ref, cb_ref, hr_ref, ya_ref,
             tail_sc, carry_sc, al_sc, bl_sc, h_sc):
        @pl.when(pl.program_id(0) == 0)
        def _():
            tail_sc[...] = jnp.zeros_like(tail_sc)
            carry_sc[...] = jnp.zeros_like(carry_sc)

        x = xg_ref[:, :D_RNN].astype(F32)
        g = xg_ref[:, D_RNN:].astype(F32)
        xr, _ = _conv_fwd(x, tail_sc[...], cw_ref, cb_ref)
        tail_sc[...] = x[ts - SUBLANES:, :]
        _, i, a, nrm, _ = _lru_gates(xr, wa_ref, wx_ref, ba_ref, bx_ref, sp_ref)
        carry_sc[...] = _linear_scan(a, nrm * (i * xr), carry_sc[...], al_sc, bl_sc, h_sc, False)
        h = h_sc[...]
        hr_ref[...] = h.astype(BF)
        ya_ref[...] = (h * _gelu(g)).astype(BF)

    gw = (N_LRU_GROUPS, LRU_GROUP, LRU_GROUP)
    return pl.pallas_call(
        body, name="rnn_fwd", grid=(s // ts,),
        in_specs=[_tile_spec(ts, 2 * D_RNN), _full_spec(gw), _full_spec(gw),
                  _full_spec((1, D_RNN)), _full_spec((1, D_RNN)), _full_spec((1, D_RNN)),
                  _full_spec((CONV_WIDTH, D_RNN)), _full_spec((1, D_RNN))],
        out_specs=[_tile_spec(ts, D_RNN), _tile_spec(ts, D_RNN)],
        out_shape=[jax.ShapeDtypeStruct((s, D_RNN), BF), jax.ShapeDtypeStruct((s, D_RNN), BF)],
        scratch_shapes=[pltpu.VMEM((SUBLANES, D_RNN), F32), pltpu.VMEM((1, D_RNN), F32),
                        pltpu.VMEM((ts, D_RNN), F32), pltpu.VMEM((ts, D_RNN), F32),
                        pltpu.VMEM((ts, D_RNN), F32)],
        compiler_params=_params(("arbitrary",)),
    )(proj, wa, wx, ba, bx, sp, cw, cb)


def _layernorm_fwd(x):
    mu = jnp.mean(x, axis=-1, keepdims=True)
    xc = x - mu
    rstd = lax.rsqrt(jnp.mean(xc * xc, axis=-1, keepdims=True) + EPS)
    return xc * rstd, rstd


def _sgu_mix(vn_bf, wm_ref, bsb_ref, ts):
    rows = []
    for blk in range(ts // SGU_BLOCK):
        r0 = blk * SGU_BLOCK
        cols = [
            _dot(wm_ref[g], vn_bf[r0:r0 + SGU_BLOCK, g * SGU_BLOCK:(g + 1) * SGU_BLOCK]) + bsb_ref[g]
            for g in range(SGU_GROUPS)
        ]
        rows.append(jnp.concatenate(cols, axis=1))
    return jnp.concatenate(rows, axis=0)


def _sgu_fwd_call(proj, wm, bsb, lg, lb, ts):
    s = proj.shape[0]

    def body(uv_ref, wm_ref, bsb_ref, lg_ref, lb_ref, yb_ref):
        gu = _gelu(uv_ref[:, :D_SGU].astype(F32))
        gv = _gelu(uv_ref[:, D_SGU:2 * D_SGU].astype(F32))
        nh, _ = _layernorm_fwd(gv)
        vn = (nh * lg_ref[...] + lb_ref[...]).astype(BF)
        yb_ref[...] = (gu * _sgu_mix(vn, wm_ref, bsb_ref, ts)).astype(BF)

    sw = (SGU_GROUPS, SGU_BLOCK, SGU_BLOCK)
    return pl.pallas_call(
        body, name="sgu_fwd", grid=(s // ts,),
        in_specs=[_tile_spec(ts, 2 * D_RNN, 1), _full_spec(sw), _full_spec(sw),
                  _full_spec((1, D_SGU)), _full_spec((1, D_SGU))],
        out_specs=_tile_spec(ts, D_SGU),
        out_shape=jax.ShapeDtypeStruct((s, D_SGU), BF),
        compiler_params=_params(("parallel",)),
    )(proj, wm, bsb, lg, lb)


_GATE_COL0 = (2 * D_RNN + 2 * D_SGU) // 512


def _gate_specs(ts):
    return [_tile_spec(ts, 512, _GATE_COL0 + j) for j in range(4)]


def _merge_call(x, proj, ya_pre, yb_pre, w_ba, w_bb, w_out, g2, layer, ts):
    s = x.shape[0]

    def body(x_ref, ga0, ga1, gb0, gb1, ya_ref, yb_ref, wa_ref, wb_ref, wo_ref, g2_ref,
             x1_ref, yao_ref, ybo_ref, mg_ref, h2_ref):
        ya = _dot(ya_ref[...], wa_ref[...])
        yb = _dot(yb_ref[...], wb_ref[...])
        sa = jax.nn.sigmoid(jnp.concatenate([ga0[...], ga1[...]], axis=1).astype(F32))
        sb = jax.nn.sigmoid(jnp.concatenate([gb0[...], gb1[...]], axis=1).astype(F32))
        merged = (sa * ya + sb * yb).astype(BF)
        x1 = x_ref[...] + _dot(merged, wo_ref[...])
        x1_ref[...] = x1
        yao_ref[...] = ya.astype(BF)
        ybo_ref[...] = yb.astype(BF)
        mg_ref[...] = merged
        h2_ref[...] = (x1 * _rms_stats(x1) * g2_ref[...]).astype(BF)

    act = jax.ShapeDtypeStruct((s, D_MODEL), BF)
    return pl.pallas_call(
        body, name="merge_fwd", grid=(s // ts,),
        in_specs=[_tile_spec(ts, D_MODEL)] + _gate_specs(ts) + [
            _tile_spec(ts, D_RNN), _tile_spec(ts, D_SGU),
            _layer_spec(w_ba, layer), _layer_spec(w_bb, layer), _layer_spec(w_out, layer),
            _full_spec((1, D_MODEL))],
        out_specs=[_tile_spec(ts, D_MODEL)] * 5,
        out_shape=[jax.ShapeDtypeStruct((s, D_MODEL), F32), act, act, act, act],
        compiler_params=_params(("parallel",)),
    )(x, proj, proj, proj, proj, ya_pre, yb_pre, w_ba, w_bb, w_out, g2)


def _ffn_call(x1, h2, w_up, w_down, layer, ts):
    s = x1.shape[0]

    def body(x1_ref, h2_ref, wu_ref, wd_ref, x2_ref, p_ref):
        h2v = h2_ref[...]
        acc = x1_ref[...]
        for q in range(N_QUARTERS):
            p = _dot(h2v, wu_ref[q])
            p_ref[:, q * Q_FF:(q + 1) * Q_FF] = p.astype(BF)
            f = jnp.square(jnp.maximum(p, 0.0)).astype(BF)
            acc = acc + _dot(f, wd_ref[q * Q_FF:(q + 1) * Q_FF, :])
        x2_ref[...] = acc

    return pl.pallas_call(
        body, name="ffn_fwd", grid=(s // ts,),
        in_specs=[_tile_spec(ts, D_MODEL), _tile_spec(ts, D_MODEL),
                  pl.BlockSpec((None, N_QUARTERS, D_MODEL, Q_FF), lambda i: (layer, 0, 0, 0)),
                  pl.BlockSpec((None, D_FF, D_MODEL), lambda i: (layer, 0, 0))],
        out_specs=[_tile_spec(ts, D_MODEL), _tile_spec(ts, D_FF)],
        out_shape=[jax.ShapeDtypeStruct((s, D_MODEL), F32), jax.ShapeDtypeStruct((s, D_FF), BF)],
        compiler_params=_params(("parallel",)),
    )(x1, h2, w_up, w_down)


def _loss_call(x, target, gf, ts):
    s = x.shape[0]

    def body(x_ref, t_ref, g_ref, dx_ref, loss_ref, dg_ref):
        @pl.when(pl.program_id(0) == 0)
        def _():
            loss_ref[...] = jnp.zeros_like(loss_ref)
            dg_ref[...] = jnp.zeros_like(dg_ref)

        xv = x_ref[...]
        gv = g_ref[...]
        err = xv * _rms_stats(xv) * gv - t_ref[...]
        part = 0.5 * jnp.sum(jnp.mean(err * err, axis=-1, keepdims=True), axis=0, keepdims=True)
        loss_ref[...] += jnp.broadcast_to(part, loss_ref.shape)
        dx, dg = _rms_bwd(err * (1.0 / D_MODEL), xv, gv)
        dx_ref[...] = dx
        dg_ref[...] += _row_sum(dg)

    return pl.pallas_call(
        body, name="loss_head", grid=(s // ts,),
        in_specs=[_tile_spec(ts, D_MODEL), _tile_spec(ts, D_MODEL), _full_spec((1, D_MODEL))],
        out_specs=[_tile_spec(ts, D_MODEL), _full_spec((1, 128)), _full_spec((1, D_MODEL))],
        out_shape=[jax.ShapeDtypeStruct((s, D_MODEL), F32), jax.ShapeDtypeStruct((1, 128), F32),
                   jax.ShapeDtypeStruct((1, D_MODEL), F32)],
        compiler_params=_params(("arbitrary",)),
    )(x, target, gf)


def _ffn_bwd_call(dx2, p, x1, g2, w_up, w_down, layer, ts):
    s = dx2.shape[0]

    def body(dx2_ref, p_ref, x1_ref, g2_ref, wu_ref, wd_ref, dx1_ref, dp_ref, dg_ref):
        @pl.when(pl.program_id(0) == 0)
        def _():
            dg_ref[...] = jnp.zeros_like(dg_ref)

        dx2v = dx2_ref[...]
        dyb = dx2v.astype(BF)
        dh2 = jnp.zeros((ts, D_MODEL), F32)
        for q in range(N_QUARTERS):
            cols = slice(q * Q_FF, (q + 1) * Q_FF)
            df = _dot_nt(dyb, wd_ref[cols, :])
            dp = (df * (2.0 * jnp.maximum(p_ref[:, cols].astype(F32), 0.0))).astype(BF)
            dp_ref[:, cols] = dp
            dh2 = dh2 + _dot_nt(dp, wu_ref[q])
        dx, dg = _rms_bwd(dh2, x1_ref[...], g2_ref[...])
        dx1_ref[...] = dx2v + dx
        dg_ref[...] += _row_sum(dg)

    return pl.pallas_call(
        body, name="ffn_bwd", grid=(s // ts,),
        in_specs=[_tile_spec(ts, D_MODEL), _tile_spec(ts, D_FF), _tile_spec(ts, D_MODEL),
                  _full_spec((1, D_MODEL)),
                  pl.BlockSpec((None, N_QUARTERS, D_MODEL, Q_FF), lambda i: (layer, 0, 0, 0)),
                  pl.BlockSpec((None, D_FF, D_MODEL), lambda i: (layer, 0, 0))],
        out_specs=[_tile_spec(ts, D_MODEL), _tile_spec(ts, D_FF), _full_spec((1, D_MODEL))],
        out_shape=[jax.ShapeDtypeStruct((s, D_MODEL), F32), jax.ShapeDtypeStruct((s, D_FF), BF),
                   jax.ShapeDtypeStruct((1, D_MODEL), F32)],
        compiler_params=_params(("arbitrary",)),
    )(dx2, p, x1, g2, w_up, w_down)


def _merge_bwd_call(dx1, proj, ya, yb, w_ba, w_bb, w_out, layer, ts):
    s = dx1.shape[0]

    def body(dx1_ref, ga0, ga1, gb0, gb1, ya_ref, yb_ref, wa_ref, wb_ref, wo_ref,
             dya_ref, dyb_ref, dgate_ref, dyap_ref, dybp_ref):
        dm = _dot_nt(dx1_ref[...].astype(BF), wo_ref[...])
        sa = jax.nn.sigmoid(jnp.concatenate([ga0[...], ga1[...]], axis=1).astype(F32))
        sb = jax.nn.sigmoid(jnp.concatenate([gb0[...], gb1[...]], axis=1).astype(F32))
        dya = (dm * sa).astype(BF)
        dyb = (dm * sb).astype(BF)
        dya_ref[...] = dya
        dyb_ref[...] = dyb
        dgate_ref[:, :D_MODEL] = (dm * ya_ref[...].astype(F32) * sa * (1.0 - sa)).astype(BF)
        dgate_ref[:, D_MODEL:] = (dm * yb_ref[...].astype(F32) * sb * (1.0 - sb)).astype(BF)
        dyap_ref[...] = _dot_nt(dya, wa_ref[...]).astype(BF)
        dybp_ref[...] = _dot_nt(dyb, wb_ref[...]).astype(BF)

    act = jax.ShapeDtypeStruct((s, D_MODEL), BF)
    return pl.pallas_call(
        body, name="merge_bwd", grid=(s // ts,),
        in_specs=[_tile_spec(ts, D_MODEL)] + _gate_specs(ts) + [
            _tile_spec(ts, D_MODEL), _tile_spec(ts, D_MODEL),
            _layer_spec(w_ba, layer), _layer_spec(w_bb, layer), _layer_spec(w_out, layer)],
        out_specs=[_tile_spec(ts, D_MODEL), _tile_spec(ts, D_MODEL), _tile_spec(ts, 2 * D_MODEL),
                   _tile_spec(ts, D_RNN), _tile_spec(ts, D_SGU)],
        out_shape=[act, act, jax.ShapeDtypeStruct((s, 2 * D_MODEL), BF),
                   jax.ShapeDtypeStruct((s, D_RNN), BF), jax.ShapeDtypeStruct((s, D_SGU), BF)],
        compiler_params=_params(("parallel",)),
    )(dx1, proj, proj, proj, proj, ya, yb, w_ba, w_bb, w_out)


def _sgu_bwd_call(dyb_pre, proj, wm, bsb, mask, lg, lb, ts):
    s = proj.shape[0]

    def body(dy_ref, uv_ref, wm_ref, bsb_ref, mask_ref, lg_ref, lb_ref,
             duv_ref, dws_ref, dbs_ref, dlg_ref, dlb_ref, dm_sc):
        step = pl.program_id(0)

        @pl.when(step == 0)
        def _():
            dws_ref[...] = jnp.zeros_like(dws_ref)
            dlg_ref[...] = jnp.zeros_like(dlg_ref)
            dlb_ref[...] = jnp.zeros_like(dlb_ref)
            dm_sc[...] = jnp.zeros_like(dm_sc)

        gu, dgu_du = _gelu_and_grad(uv_ref[:, :D_SGU].astype(F32))
        gv, dgv_dv = _gelu_and_grad(uv_ref[:, D_SGU:2 * D_SGU].astype(F32))
        nh, rstd = _layernorm_fwd(gv)
        lgv = lg_ref[...]
        vn = (nh * lgv + lb_ref[...]).astype(BF)
        dy = dy_ref[...].astype(F32)
        du = dy * _sgu_mix(vn, wm_ref, bsb_ref, ts) * dgu_du
        dmix = dy * gu
        dmix_bf = dmix.astype(BF)
        dm_acc = dm_sc[...]
        rows = []
        for blk in range(ts // SGU_BLOCK):
            r0 = blk * SGU_BLOCK
            dm_acc = dm_acc + dmix[r0:r0 + SGU_BLOCK, :]
            cols = []
            for g in range(SGU_GROUPS):
                c0 = g * SGU_BLOCK
                dmg = dmix_bf[r0:r0 + SGU_BLOCK, c0:c0 + SGU_BLOCK]
                cols.append(_dot_tn(wm_ref[g], dmg))
                dws_ref[g] += mask_ref[...] * _dot_nt(dmg, vn[r0:r0 + SGU_BLOCK, c0:c0 + SGU_BLOCK])
            rows.append(jnp.concatenate(cols, axis=1))
        dm_sc[...] = dm_acc
        dvn = jnp.concatenate(rows, axis=0)
        dlg_ref[...] += _row_sum(dvn * nh)
        dlb_ref[...] += _row_sum(dvn)
        dnh = dvn * lgv
        dgv = rstd * (dnh - jnp.mean(dnh, axis=-1, keepdims=True)
                      - nh * jnp.mean(dnh * nh, axis=-1, keepdims=True))
        duv_ref[:, :D_SGU] = du.astype(BF)
        duv_ref[:, D_SGU:] = (dgv * dgv_dv).astype(BF)

        @pl.when(step == pl.num_programs(0) - 1)
        def _():
            for g in range(SGU_GROUPS):
                dbs_ref[:, g:g + 1] = jnp.sum(
                    dm_acc[:, g * SGU_BLOCK:(g + 1) * SGU_BLOCK], axis=1, keepdims=True)

    sw = (SGU_GROUPS, SGU_BLOCK, SGU_BLOCK)
    return pl.pallas_call(
        body, name="sgu_bwd", grid=(s // ts,),
        in_specs=[_tile_spec(ts, D_SGU), _tile_spec(ts, 2 * D_RNN, 1), _full_spec(sw), _full_spec(sw),
                  _full_spec((SGU_BLOCK, SGU_BLOCK)), _full_spec((1, D_SGU)), _full_spec((1, D_SGU))],
        out_specs=[_tile_spec(ts, 2 * D_SGU), _full_spec(sw), _full_spec((SGU_BLOCK, SGU_GROUPS)),
                   _full_spec((1, D_SGU)), _full_spec((1, D_SGU))],
        out_shape=[jax.ShapeDtypeStruct((s, 2 * D_SGU), BF), jax.ShapeDtypeStruct(sw, F32),
                   jax.ShapeDtypeStruct((SGU_BLOCK, SGU_GROUPS), F32),
                   jax.ShapeDtypeStruct((1, D_SGU), F32), jax.ShapeDtypeStruct((1, D_SGU), F32)],
        scratch_shapes=[pltpu.VMEM((SGU_BLOCK, D_SGU), F32)],
        compiler_params=_params(("arbitrary",)),
    )(dyb_pre, proj, wm, bsb, mask, lg, lb)


_ROW_DBA, _ROW_DBX, _ROW_DSP, _ROW_DCB, _ROW_DCW = 0, 1, 2, 3, 4
_PREV_ROWS = 16


def _rnn_bwd_call(dya_pre, proj, hr, wa, wx, ba, bx, sp, cw, cb, ts):
    s = proj.shape[0]
    nt = s // ts
    per = ts // _PREV_ROWS

    def tile(i):
        return nt - 1 - i

    def prev(i):
        return jnp.maximum(tile(i) * per - 1, 0)

    def body(dy_ref, xg_ref, xgp_ref, hr_ref, hrp_ref, wa_ref, wx_ref, ba_ref, bx_ref, sp_ref,
             cw_ref, cb_ref, dxg_ref, dwa_ref, dwx_ref, vec_ref,
             lam_carry, a_first, dxr_head, al_sc, bl_sc, lam_sc):
        step = pl.program_id(0)

        @pl.when(step == 0)
        def _():
            dwa_ref[...] = jnp.zeros_like(dwa_ref)
            dwx_ref[...] = jnp.zeros_like(dwx_ref)
            vec_ref[...] = jnp.zeros_like(vec_ref)
            lam_carry[...] = jnp.zeros_like(lam_carry)
            a_first[...] = jnp.zeros_like(a_first)
            dxr_head[...] = jnp.zeros_like(dxr_head)

        has_prev = (step < nt - 1).astype(F32)
        x = xg_ref[:, :D_RNN].astype(F32)
        g = xg_ref[:, D_RNN:].astype(F32)
        x_tail = xgp_ref[_PREV_ROWS - SUBLANES:, :D_RNN].astype(F32) * has_prev
        h_tail = hrp_ref[_PREV_ROWS - SUBLANES:, :].astype(F32) * has_prev
        xr, x_shifted = _conv_fwd(x, x_tail, cw_ref, cb_ref)
        r, i, a, nrm, inv_nrm = _lru_gates(xr, wa_ref, wx_ref, ba_ref, bx_ref, sp_ref)
        h = hr_ref[...].astype(F32)
        dy = dy_ref[...].astype(F32)
        gg, dgg = _gelu_and_grad(g)

        coef = _shift_up(a, jnp.broadcast_to(a_first[...], (SUBLANES, D_RNN)), 1)
        lam_carry[...] = _linear_scan(coef, dy * gg, lam_carry[...], al_sc, bl_sc, lam_sc, True)
        a_first[...] = a[0:1, :]
        lam = lam_sc[...]

        da = lam * _shift_down(h, h_tail, 1)
        dnrm = lam * (i * xr)
        di = lam * nrm * xr
        dlog_a = da * a - dnrm * (a * a) * inv_nrm
        spv = sp_ref[...]
        dza = (dlog_a * (-LRU_C * spv)) * (r * (1.0 - r))
        dzx = di * (i * (1.0 - i))
        vec_ref[_ROW_DSP:_ROW_DSP + 1, :] += _row_sum(dlog_a * (-LRU_C * r))
        vec_ref[_ROW_DBA:_ROW_DBA + 1, :] += _row_sum(dza)
        vec_ref[_ROW_DBX:_ROW_DBX + 1, :] += _row_sum(dzx)
        xb = xr.astype(BF)
        dza_bf = dza.astype(BF)
        dzx_bf = dzx.astype(BF)
        for grp in range(N_LRU_GROUPS):
            cols = slice(grp * LRU_GROUP, (grp + 1) * LRU_GROUP)
            dwa_ref[grp] += _dot_tn(xb[:, cols], dza_bf[:, cols])
            dwx_ref[grp] += _dot_tn(xb[:, cols], dzx_bf[:, cols])
        dxr = (lam * nrm * i + _group_dot(dza_bf, wa_ref, _dot_nt) + _group_dot(dzx_bf, wx_ref, _dot_nt))

        vec_ref[_ROW_DCB:_ROW_DCB + 1, :] += _row_sum(dxr)
        head = dxr_head[...]
        dx = cw_ref[CONV_WIDTH - 1:CONV_WIDTH, :] * dxr
        vec_ref[_ROW_DCW + 3:_ROW_DCW + 4, :] += _row_sum(dxr * x)
        for sft in range(1, CONV_WIDTH):
            k = CONV_WIDTH - 1 - sft
            dx = dx + cw_ref[k:k + 1, :] * _shift_up(dxr, head, sft)
            vec_ref[_ROW_DCW + k:_ROW_DCW + k + 1, :] += _row_sum(dxr * x_shifted[sft])
        dxr_head[...] = dxr[0:SUBLANES, :]
        dxg_ref[:, :D_RNN] = dx.astype(BF)
        dxg_ref[:, D_RNN:] = (dy * h * dgg).astype(BF)

    gw = (N_LRU_GROUPS, LRU_GROUP, LRU_GROUP)
    rev = lambda width: pl.BlockSpec((ts, width), lambda i: (tile(i), 0))
    return pl.pallas_call(
        body, name="rnn_bwd", grid=(nt,),
        in_specs=[rev(D_RNN), rev(2 * D_RNN),
                  pl.BlockSpec((_PREV_ROWS, 2 * D_RNN), lambda i: (prev(i), 0)),
                  rev(D_RNN),
                  pl.BlockSpec((_PREV_ROWS, D_RNN), lambda i: (prev(i), 0)),
                  _full_spec(gw), _full_spec(gw),
                  _full_spec((1, D_RNN)), _full_spec((1, D_RNN)), _full_spec((1, D_RNN)),
                  _full_spec((CONV_WIDTH, D_RNN)), _full_spec((1, D_RNN))],
        out_specs=[rev(2 * D_RNN), _full_spec(gw), _full_spec(gw), _full_spec((SUBLANES, D_RNN))],
        out_shape=[jax.ShapeDtypeStruct((s, 2 * D_RNN), BF), jax.ShapeDtypeStruct(gw, F32),
                   jax.ShapeDtypeStruct(gw, F32), jax.ShapeDtypeStruct((SUBLANES, D_RNN), F32)],
        scratch_shapes=[pltpu.VMEM((1, D_RNN), F32), pltpu.VMEM((1, D_RNN), F32),
                        pltpu.VMEM((SUBLANES, D_RNN), F32),
                        pltpu.VMEM((ts, D_RNN), F32), pltpu.VMEM((ts, D_RNN), F32),
                        pltpu.VMEM((ts, D_RNN), F32)],
        compiler_params=_params(("arbitrary",)),
    )(dya_pre, proj, proj, hr, hr, wa, wx, ba, bx, sp, cw, cb)


def _inproj_bwd_call(dxg, duv, dgate, dx1, x, g1, w_in, layer, ts):
    s = x.shape[0]

    def body(dxg_ref, duv_ref, dgt_ref, dx1_ref, x_ref, g_ref, w_ref, dx_ref, dproj_ref, dg_ref):
        @pl.when(pl.program_id(0) == 0)
        def _():
            dg_ref[...] = jnp.zeros_like(dg_ref)

        dproj = jnp.concatenate([dxg_ref[...], duv_ref[...], dgt_ref[...]], axis=1)
        dproj_ref[...] = dproj
        dh = jnp.zeros((ts, D_MODEL), F32)
        for q in range(N_QUARTERS):
            dh = dh + _dot_nt(dproj[:, q * Q_IN:(q + 1) * Q_IN], w_ref[q])
        dx, dg = _rms_bwd(dh, x_ref[...], g_ref[...])
        dx_ref[...] = dx1_ref[...] + dx
        dg_ref[...] += _row_sum(dg)

    return pl.pallas_call(
        body, name="inproj_bwd", grid=(s // ts,),
        in_specs=[_tile_spec(ts, 2 * D_RNN), _tile_spec(ts, 2 * D_SGU), _tile_spec(ts, 2 * D_MODEL),
                  _tile_spec(ts, D_MODEL), _tile_spec(ts, D_MODEL), _full_spec((1, D_MODEL)),
                  pl.BlockSpec((None, N_QUARTERS, D_MODEL, Q_IN), lambda i: (layer, 0, 0, 0))],
        out_specs=[_tile_spec(ts, D_MODEL), _tile_spec(ts, D_IN), _full_spec((1, D_MODEL))],
        out_shape=[jax.ShapeDtypeStruct((s, D_MODEL), F32), jax.ShapeDtypeStruct((s, D_IN), BF),
                   jax.ShapeDtypeStruct((1, D_MODEL), F32)],
        compiler_params=_params(("arbitrary",)),
    )(dxg, duv, dgate, dx1, x, g1, w_in)


def _relu_sq(p):
    return jnp.square(jnp.maximum(p.astype(F32), 0.0))


def _wgrad_call(a, b, core, tm, tn, tk, col_blocked, name, a_fn=None):
    s, m = a.shape
    n = b.shape[1]
    r, cols = (m, n // N_QUARTERS) if col_blocked else (m // N_QUARTERS, n)
    r2 = r // 2
    per_tile = tm // r
    steps = s // tk

    def body(core_ref, a_ref, b_ref, keep_ref, send_ref, *acc):
        av = a_ref[...]
        if a_fn is not None:
            av = a_fn(av)
        prod = _dot_tn(av.astype(BF), b_ref[...].astype(BF))

        def emit(total):
            for h in range(2):
                @pl.when(core_ref[0] == h)
                def _():
                    for q in range(per_tile):
                        keep_ref[q] = total[q * r + h * r2:q * r + (h + 1) * r2]
                        send_ref[q] = total[q * r + (1 - h) * r2:q * r + (2 - h) * r2].astype(BF)

        if steps == 1:
            emit(prod)
        else:
            acc_ref, = acc
            step = pl.program_id(2)

            @pl.when(step == 0)
            def _():
                acc_ref[...] = prod

            @pl.when(jnp.logical_and(step > 0, step < steps - 1))
            def _():
                acc_ref[...] += prod

            @pl.when(step == steps - 1)
            def _():
                emit(acc_ref[...] + prod)

    if col_blocked:
        per_q = cols // tn
        out_spec = pl.BlockSpec((1, r2, tn), lambda i, j, k, c: (j // per_q, 0, j % per_q))
    else:
        out_spec = pl.BlockSpec((per_tile, r2, tn), lambda i, j, k, c: (i, 0, j))
    return pl.pallas_call(
        body, name=name,
        out_shape=[jax.ShapeDtypeStruct((N_QUARTERS, r2, cols), F32),
                   jax.ShapeDtypeStruct((N_QUARTERS, r2, cols), BF)],
        grid_spec=pltpu.PrefetchScalarGridSpec(
            num_scalar_prefetch=1, grid=(m // tm, n // tn, steps),
            in_specs=[pl.BlockSpec((tk, tm), lambda i, j, k, c: (k, i)),
                      pl.BlockSpec((tk, tn), lambda i, j, k, c: (k, j))],
            out_specs=[out_spec, out_spec],
            scratch_shapes=[] if steps == 1 else [pltpu.VMEM((tm, tn), F32)]),
        compiler_params=_params(("parallel", "parallel", "arbitrary")),
    )(core, a, b)


BIG = ("w_in", "w_up", "w_down", "w_branch_a", "w_branch_b", "w_out")


def _block_diag(w):
    w4 = w.reshape(N_LRU_GROUPS, HEADS_PER_GROUP, RNN_HEAD_DIM, RNN_HEAD_DIM)
    eye = jnp.eye(HEADS_PER_GROUP, dtype=w.dtype)
    return jnp.einsum("gjio,jk->gjiko", w4, eye).reshape(N_LRU_GROUPS, LRU_GROUP, LRU_GROUP)


def _block_diag_extract(d):
    d5 = d.reshape(N_LRU_GROUPS, HEADS_PER_GROUP, RNN_HEAD_DIM, HEADS_PER_GROUP, RNN_HEAD_DIM)
    blocks = [d5[:, j, :, j, :] for j in range(HEADS_PER_GROUP)]
    return jnp.stack(blocks, axis=1).reshape(RNN_HEADS, RNN_HEAD_DIM, RNN_HEAD_DIM)


def _sgu_mask():
    chunk = jnp.arange(SGU_BLOCK) // CHUNK
    return (chunk[:, None] >= chunk[None, :]).astype(F32)


def _layer_small(sm, l, core):
    row = lambda v: v.reshape(1, -1)
    return dict(
        core=core,
        g1=row(sm["norm_mix_g"][l]), g2=row(sm["norm_ffn_g"][l]),
        wa=_block_diag(sm["lru_w_a"][l]).astype(BF), wx=_block_diag(sm["lru_w_x"][l]).astype(BF),
        ba=row(sm["lru_b_a"][l]), bx=row(sm["lru_b_x"][l]),
        sp=row(jax.nn.softplus(-sm["lru_lambda"][l])),
        cw=sm["conv_w"][l], cb=row(sm["conv_b"][l]),
        wm=(sm["sgu_w_s"][l] * _sgu_mask()).astype(BF),
        bsb=jnp.broadcast_to(sm["sgu_b_s"][l][:, :, None], (SGU_GROUPS, SGU_BLOCK, SGU_BLOCK)),
        lg=row(sm["sgu_ln_g"][l]), lb=row(sm["sgu_ln_b"][l]),
    )


def _layer_fwd_mix(x, big, p, ts):
    h = _norm_call(x, p["g1"], ts)
    proj = _inproj_call(h, big["w_in"], 0, 2 * ts)
    hr, ya_pre = _rnn_fwd_call(proj, p["wa"], p["wx"], p["ba"], p["bx"], p["sp"], p["cw"], p["cb"], ts)
    yb_pre = _sgu_fwd_call(proj, p["wm"], p["bsb"], p["lg"], p["lb"], ts)
    return dict(p=p, x=x, h=h, proj=proj, hr=hr, ya_pre=ya_pre, yb_pre=yb_pre)


def _layer_fwd_out(sv, big, ts):
    x1, ya, yb, merged, h2 = _merge_call(sv["x"], sv["proj"], sv["ya_pre"], sv["yb_pre"], big["w_branch_a"],
                                         big["w_branch_b"], big["w_out"], sv["p"]["g2"], 0, ts)
    x2, pre = _ffn_call(x1, h2, big["w_up"], big["w_down"], 0, ts)
    sv.update(x1=x1, ya=ya, yb=yb, merged=merged, h2=h2, pre=pre)
    return x2


def _layer_bwd_ffn(dx, sv, big, ts):
    p = sv["p"]
    dx1, dpre, dg2 = _ffn_bwd_call(dx, sv["pre"], sv["x1"], p["g2"], big["w_up"], big["w_down"], 0, ts)
    tk = dx.shape[0]
    gb = dict(
        w_down=_wgrad_call(sv["pre"], dx, p["core"], Q_FF, D_MODEL // 2, tk, False, "wgrad_down", a_fn=_relu_sq),
        w_up=_wgrad_call(sv["h2"], dpre, p["core"], D_MODEL, Q_FF, tk, True, "wgrad_up"))
    return dx1, gb, dict(norm_ffn_g=dg2[0])


def _layer_bwd_merge(dx1, sv, big, ts):
    tk = dx1.shape[0]
    core = sv["p"]["core"]
    dya, dyb, dgate, dya_pre, dyb_pre = _merge_bwd_call(
        dx1, sv["proj"], sv["ya"], sv["yb"], big["w_branch_a"], big["w_branch_b"], big["w_out"], 0, ts)
    gb = dict(
        w_out=_wgrad_call(sv["merged"], dx1, core, D_MODEL, D_MODEL // 2, tk, False, "wgrad_out"),
        w_branch_a=_wgrad_call(sv["ya_pre"], dya, core, D_RNN, D_MODEL // 2, tk, False, "wgrad_branch_a"),
        w_branch_b=_wgrad_call(sv["yb_pre"], dyb, core, D_SGU, D_MODEL // 2, tk, False, "wgrad_branch_b"))
    return (dgate, dya_pre, dyb_pre), gb


def _layer_bwd_branches(dx1, merge_out, sv, big, lam, ts):
    p = sv["p"]
    tk = dx1.shape[0]
    dgate, dya_pre, dyb_pre = merge_out
    gb = {}
    duv, dws, dbs, dlg, dlb = _sgu_bwd_call(dyb_pre, sv["proj"], p["wm"], p["bsb"], _sgu_mask(), p["lg"], p["lb"],
                                            ts)
    dxg, dwa, dwx, vec = _rnn_bwd_call(dya_pre, sv["proj"], sv["hr"], p["wa"], p["wx"], p["ba"], p["bx"],
                                       p["sp"], p["cw"], p["cb"], ts // 2)
    dx, dproj, dg1 = _inproj_bwd_call(dxg, duv, dgate, dx1, sv["x"], p["g1"], big["w_in"], 0, ts)
    gb["w_in"] = _wgrad_call(sv["h"], dproj, p["core"], D_MODEL, Q_IN, tk // 2, True, "wgrad_in")
    gs = dict(
        norm_mix_g=dg1[0], conv_w=vec[_ROW_DCW:_ROW_DCW + CONV_WIDTH], conv_b=vec[_ROW_DCB],
        lru_w_a=_block_diag_extract(dwa), lru_w_x=_block_diag_extract(dwx),
        lru_b_a=vec[_ROW_DBA].reshape(RNN_HEADS, RNN_HEAD_DIM), lru_b_x=vec[_ROW_DBX].reshape(RNN_HEADS, RNN_HEAD_DIM),
        lru_lambda=-vec[_ROW_DSP] * jax.nn.sigmoid(-lam),
        sgu_ln_g=dlg[0], sgu_ln_b=dlb[0], sgu_w_s=dws, sgu_b_s=dbs.T)
    return dx, gb, gs


def _local_step(x, target, big, sm, ts):
    saved = []
    core = jnp.zeros((1,), jnp.int32)
    for l in range(DEPTH):
        sv = _layer_fwd_mix(x, big[l], _layer_small(sm, l, core), ts)
        x = _layer_fwd_out(sv, big[l], ts)
        saved.append(sv)
    dx, loss, dgf = _loss_call(x, target, sm["final_norm_g"].reshape(1, -1), ts)
    gb, gs = [None] * DEPTH, [None] * DEPTH
    for l in reversed(range(DEPTH)):
        dx1, gb_ffn, gs_ffn = _layer_bwd_ffn(dx, saved[l], big[l], ts)
        merge_out, gb_merge = _layer_bwd_merge(dx1, saved[l], big[l], ts)
        dx, gb_mix, gs_mix = _layer_bwd_branches(dx1, merge_out, saved[l], big[l], sm["lru_lambda"][l], ts)
        gb[l] = {**gb_ffn, **gb_merge, **gb_mix}
        gs[l] = {**gs_ffn, **gs_mix}
    gs = {k: jnp.stack([g[k] for g in gs]) for k in gs[0]}
    gs["final_norm_g"] = dgf[0]
    return loss, dx, gb, gs


EW_BLOCK_ELEMS = 384 * 1024


def _row_block(rows, cols):
    for br in range(min(rows, EW_BLOCK_ELEMS // cols), 0, -1):
        if rows % br == 0 and br % 16 == 0:
            return br
    return rows


def _ew_call(fn, name, operands, outputs, slabs=1, sel=None, into=None):
    if into is not None and not isinstance(into, (list, tuple)):
        into = [into]
    rows, cols = outputs[0][0].shape[2:]
    br = _row_block(rows, cols)
    n_in = len(operands)

    def pick(tok, g, s):
        if callable(tok):
            return tok(g, s)
        if tok == "g":
            return g
        if isinstance(tok, tuple):
            return s[tok[1]]
        return tok

    def spec(idx):
        return pl.BlockSpec((None, None, br, cols),
                            lambda g, i, s, idx=idx: (pick(idx[0], g, s), pick(idx[1], g, s), i, 0))

    if sel is None:
        sel = jnp.zeros((1,), jnp.int32)
    in_specs = [spec(idx) for _, idx in operands]
    arrays = [a for a, _ in operands]
    aliases = {}
    for j, buf in enumerate(into or ()):
        in_specs.append(pl.BlockSpec(memory_space=pl.ANY))
        arrays.append(buf)
        aliases[1 + n_in + j] = j

    def body(sel_ref, *refs):
        outs = fn(*[r[...] for r in refs[:n_in]])
        for o_ref, o in zip(refs[len(arrays):], outs):
            o_ref[...] = o.astype(o_ref.dtype)

    return pl.pallas_call(
        body, name=name, out_shape=[s for s, _ in outputs],
        grid_spec=pltpu.PrefetchScalarGridSpec(
            num_scalar_prefetch=1, grid=(slabs, rows // br),
            in_specs=in_specs,
            out_specs=[spec(idx) for _, idx in outputs]),
        input_output_aliases=aliases,
        compiler_params=_params(("parallel", "parallel")),
    )(sel, *arrays)


def _as4(a):
    return a.reshape((1,) * (4 - a.ndim) + a.shape)


def _adamw(w, g, m, v):
    m = ADAM_B1 * m + (1.0 - ADAM_B1) * g
    v = ADAM_B2 * v + (1.0 - ADAM_B2) * jnp.square(g)
    m_hat = m / (1.0 - ADAM_B1 ** ADAM_STEP)
    v_hat = v / (1.0 - ADAM_B2 ** ADAM_STEP)
    delta = -ADAM_LR * (m_hat / (jnp.sqrt(v_hat) + ADAM_EPS) + ADAM_WD * w)
    return delta, m, v


def _small_adamw_call(ws, gs, ms, vs):
    n = len(ws)

    def body(*refs):
        for k in range(n):
            w, g, m, v = (refs[j * n + k][...] for j in range(4))
            outs = _adamw(w, g, m, v)
            for j in range(3):
                refs[(4 + j) * n + k][...] = outs[j]

    shapes = [jax.ShapeDtypeStruct(w.shape, F32) for w in ws]
    outs = pl.pallas_call(
        body, name="adamw_small", out_shape=shapes * 3,
        in_specs=[pl.BlockSpec(memory_space=pltpu.VMEM)] * (4 * n),
        out_specs=[pl.BlockSpec(memory_space=pltpu.VMEM)] * (3 * n),
        compiler_params=_params(),
    )(*ws, *gs, *ms, *vs)
    return outs[:n], outs[n:2 * n], outs[2 * n:]


ANY = pl.BlockSpec(memory_space=pl.ANY)


def _place():
    x, y, c = lax.axis_index("x"), lax.axis_index("y"), lax.axis_index("c")
    chips = [(1 - x, y), (x, 1 - y), (1 - x, 1 - y)]
    return x, y, c, chips


def _remote(src, dst, send_sem, recv_sem, to):
    return pltpu.make_async_remote_copy(src_ref=src, dst_ref=dst, send_sem=send_sem, recv_sem=recv_sem,
                                        device_id=to, device_id_type=MESH)


def _gather_call(bufs):
    n = len(bufs)

    def body(*refs):
        out = refs[n:2 * n]
        send_sems, recv_sems = refs[2 * n:]
        x, y, c, chips = _place()
        me_q = 2 * x + y
        sibling = (x, y, 1 - c)
        first = []
        for w in range(n):
            for j, chip in enumerate(chips):
                mine = out[w].at[c, me_q]
                first.append(_remote(mine, mine, send_sems.at[w * 3 + j], recv_sems.at[w * 3 + j], (*chip, c)))
        for cp in first:
            cp.start()
        passed = []
        for w in range(n):
            for j, (qx, qy) in enumerate(chips):
                landed = out[w].at[c, 2 * qx + qy]
                k = w * 3 + j
                _remote(landed, landed, send_sems.at[k], recv_sems.at[k], (qx, qy, c)).wait_recv()
                cp = _remote(landed, landed, send_sems.at[3 * n + k], recv_sems.at[3 * n + k], sibling)
                cp.start()
                passed.append(cp)
        for w in range(n):
            for j, (qx, qy) in enumerate(chips):
                landed = out[w].at[1 - c, 2 * qx + qy]
                k = 3 * n + w * 3 + j
                _remote(landed, landed, send_sems.at[k], recv_sems.at[k], sibling).wait_recv()
        for cp in first + passed:
            cp.wait_send()

    return pl.pallas_call(
        body, name="gather_weights",
        out_shape=[jax.ShapeDtypeStruct(a.shape, a.dtype) for a in bufs],
        in_specs=[ANY] * n, out_specs=[ANY] * n,
        input_output_aliases={w: w for w in range(n)},
        scratch_shapes=[pltpu.SemaphoreType.DMA((6 * n,)), pltpu.SemaphoreType.DMA((6 * n,))],
        compiler_params=_params(vmem=False, has_side_effects=True),
    )(*bufs)


def _sibling_send_call(items):
    n = len(items)

    def body(*refs):
        src, out = refs[:n], refs[n:2 * n]
        send_sems, recv_sems = refs[2 * n:]
        x, y, c, _ = _place()
        copies = [_remote(src[w], out[w], send_sems.at[w], recv_sems.at[w], (x, y, 1 - c)) for w in range(n)]
        for cp in copies:
            cp.start()
        for cp in copies:
            cp.wait()

    return pl.pallas_call(
        body, name="grads_to_sibling",
        out_shape=[jax.ShapeDtypeStruct(a.shape, a.dtype) for a in items],
        in_specs=[ANY] * n, out_specs=[ANY] * n,
        scratch_shapes=[pltpu.SemaphoreType.DMA((n,)), pltpu.SemaphoreType.DMA((n,))],
        compiler_params=_params(vmem=False, has_side_effects=True),
    )(*items)


def _sibling_inplace_call(name, bufs, slabs, n_pairs):
    n = len(bufs)

    def body(*refs):
        out = refs[n:2 * n]
        send_sems, recv_sems = refs[2 * n:]
        x, y, c, _ = _place()
        sibling = (x, y, 1 - c)
        pairs = [pair for w, ref in enumerate(out) for pair in slabs(ref, c, w)]
        sends = [_remote(s, s, send_sems.at[k], recv_sems.at[k], sibling) for k, (s, _) in enumerate(pairs)]
        for cp in sends:
            cp.start()
        for k, (_, r) in enumerate(pairs):
            _remote(r, r, send_sems.at[k], recv_sems.at[k], sibling).wait_recv()
        for cp in sends:
            cp.wait_send()

    return pl.pallas_call(
        body, name=name,
        out_shape=[jax.ShapeDtypeStruct(a.shape, a.dtype) for a in bufs],
        in_specs=[ANY] * n, out_specs=[ANY] * n,
        input_output_aliases={w: w for w in range(n)},
        scratch_shapes=[pltpu.SemaphoreType.DMA((n_pairs,)), pltpu.SemaphoreType.DMA((n_pairs,))],
        compiler_params=_params(vmem=False, has_side_effects=True),
    )(*bufs)


HBM_SPEC = pl.BlockSpec(memory_space=pltpu.HBM)
SEM_SPEC = pl.BlockSpec(memory_space=pltpu.SEMAPHORE)
DATAFLOW_EFFECT = pltpu.SideEffectType.DATAFLOW_SIDE_EFFECTING


def _exchange_start(name, bufs, copies, n_copies, after):
    n = len(bufs)

    def body(*refs):
        ins, send_sems, recv_sems, token = refs[:n], refs[n + 1], refs[n + 2], refs[-1]
        for k, (src, dst, to) in enumerate(copies(ins)):
            _remote(src, dst, send_sems.at[k], recv_sems.at[k], to).start()
        token[...] = jnp.zeros_like(token)

    outs = pl.pallas_call(
        body, name=name,
        out_shape=(pltpu.SemaphoreType.DMA((n_copies,)), pltpu.SemaphoreType.DMA((n_copies,)),
                   *[pltpu.HBM(b.shape, b.dtype) for b in bufs], jax.ShapeDtypeStruct((SUBLANES, 128), F32)),
        in_specs=[HBM_SPEC] * n + [ANY],
        out_specs=(SEM_SPEC, SEM_SPEC, *[HBM_SPEC] * n, pl.BlockSpec(memory_space=pltpu.VMEM)),
        input_output_aliases={w: w + 2 for w in range(n)},
        compiler_params=pltpu.CompilerParams(has_side_effects=DATAFLOW_EFFECT),
    )(*[pltpu.with_memory_space_constraint(b, pltpu.HBM) for b in bufs], after)
    return outs[0], outs[1], list(outs[2:2 + n]), outs[-1]


def _exchange_wait(name, send_sems, recv_sems, bufs, copies, after):
    n = len(bufs)

    def body(*refs):
        ins, send_sems, recv_sems = refs[:n], refs[n], refs[n + 1]
        for k, (src, dst, to) in enumerate(copies(ins)):
            cp = _remote(src, dst, send_sems.at[k], recv_sems.at[k], to)
            cp.wait_send()
            cp.wait_recv()

    return pl.pallas_call(
        body, name=name,
        out_shape=[pltpu.HBM(b.shape, b.dtype) for b in bufs],
        in_specs=[HBM_SPEC] * n + [SEM_SPEC, SEM_SPEC, ANY],
        out_specs=[HBM_SPEC] * n,
        input_output_aliases={w: w for w in range(n)},
        compiler_params=pltpu.CompilerParams(has_side_effects=DATAFLOW_EFFECT),
    )(*bufs, send_sems, recv_sems, after)


def _gather_copies(refs):
    x, y, c, chips = _place()
    mine = 2 * (2 * x + y) + c
    return [(ref.at[mine], ref.at[mine], (qx, qy, c)) for ref in refs for qx, qy in chips]


def _gather_forward_slabs(ref, c, w):
    x, y, _, chips = _place()
    return [(ref.at[2 * (2 * qx + qy) + c], ref.at[2 * (2 * qx + qy) + 1 - c]) for qx, qy in chips]


def _device_peers():
    x, y, c, _ = _place()
    return 4 * x + 2 * y + c, [(k, (x ^ ((k >> 2) & 1), y ^ ((k >> 1) & 1), c ^ (k & 1))) for k in range(1, 8)]


def _small_scatter_copies(refs):
    me, peers = _device_peers()
    return [(refs[0].at[me ^ k], refs[1].at[me], to) for k, to in peers]


def _small_spread_copies(refs):
    me, peers = _device_peers()
    return [(refs[0].at[me], refs[0].at[me], to) for _, to in peers]


def _owner_copies(refs):
    n = len(refs) // 2
    x, y, c, chips = _place()
    return [(refs[w].at[2 * qx + qy], refs[n + w].at[j], (qx, qy, c))
            for w in range(n) for j, (qx, qy) in enumerate(chips)]


N_DEVICES = 8
SMALL_ROWS = 616


def _small_allreduce_call(buf):
    def body(in_ref, out_ref, recv_ref, red_ref, send_sems, recv_sems):
        x, y, c, _ = _place()
        me = 4 * x + 2 * y + c

        def peer(k):
            return (x ^ ((k >> 2) & 1), y ^ ((k >> 1) & 1), c ^ (k & 1))

        scatter = [_remote(in_ref.at[me ^ k], recv_ref.at[me], send_sems.at[k - 1], recv_sems.at[k - 1], peer(k))
                   for k in range(1, N_DEVICES)]
        for cp in scatter:
            cp.start()
        recv_ref[me] = in_ref[me]
        for k in range(1, N_DEVICES):
            landed = recv_ref.at[me ^ k]
            _remote(landed, landed, send_sems.at[k - 1], recv_sems.at[k - 1], peer(k)).wait_recv()
        total = recv_ref[0]
        for j in range(1, N_DEVICES):
            total = total + recv_ref[j]
        red_ref[...] = total
        out_ref[me] = total
        spread = [_remote(red_ref, out_ref.at[me], send_sems.at[6 + k], recv_sems.at[6 + k], peer(k))
                  for k in range(1, N_DEVICES)]
        for cp in spread:
            cp.start()
        for k in range(1, N_DEVICES):
            landed = out_ref.at[me ^ k]
            _remote(landed, landed, send_sems.at[6 + k], recv_sems.at[6 + k], peer(k)).wait_recv()
        for cp in scatter + spread:
            cp.wait_send()

    shape = (N_DEVICES, SMALL_ROWS, 128)
    return pl.pallas_call(
        body, name="allreduce_small",
        out_shape=jax.ShapeDtypeStruct(shape, F32),
        in_specs=[pl.BlockSpec(memory_space=pltpu.VMEM)],
        out_specs=pl.BlockSpec(memory_space=pltpu.VMEM),
        scratch_shapes=[pltpu.VMEM(shape, F32), pltpu.VMEM(shape[1:], F32),
                        pltpu.SemaphoreType.DMA((2 * (N_DEVICES - 1),)),
                        pltpu.SemaphoreType.DMA((2 * (N_DEVICES - 1),))],
        compiler_params=_params(has_side_effects=True),
    )(buf)


SMALL = ("norm_mix_g", "conv_w", "conv_b", "lru_w_a", "lru_b_a", "lru_w_x", "lru_b_x", "lru_lambda",
         "sgu_ln_g", "sgu_ln_b", "sgu_w_s", "sgu_b_s", "norm_ffn_g", "final_norm_g")
WEIGHTS = ("norm_mix_g", "w_in", "conv_w", "conv_b", "lru_w_a", "lru_b_a", "lru_w_x", "lru_b_x", "lru_lambda",
           "sgu_ln_g", "sgu_ln_b", "sgu_w_s", "sgu_b_s", "w_branch_a", "w_branch_b", "w_out", "norm_ffn_g",
           "w_up", "w_down", "final_norm_g")
PACK_ALIGN = SUBLANES * 128


def _pack_small(gs):
    parts = []
    for k in SMALL:
        flat = gs[k].reshape(-1)
        parts.append(jnp.pad(flat, (0, -flat.size % PACK_ALIGN)))
    flat = jnp.concatenate(parts)
    flat = jnp.pad(flat, (0, N_DEVICES * SMALL_ROWS * 128 - flat.size))
    return flat.reshape(N_DEVICES, SMALL_ROWS, 128)


def _unpack_small(buf, like):
    flat = buf.reshape(-1)
    out, off = {}, 0
    for k in SMALL:
        size = like[k].size
        out[k] = flat[off:off + size].reshape(like[k].shape)
        off += size + (-size % PACK_ALIGN)
    return out


def _as_rows(a):
    return a.reshape(-1, a.shape[-1])


def kernel(x, norm_mix_g, w_in, conv_w, conv_b, lru_w_a, lru_b_a, lru_w_x, lru_b_x, lru_lambda, sgu_ln_g, sgu_ln_b, sgu_w_s, sgu_b_s, w_branch_a, w_branch_b, w_out, norm_ffn_g, w_up, w_down, final_norm_g, loss_target, m_norm_mix_g, m_w_in, m_conv_w, m_conv_b, m_lru_w_a, m_lru_b_a, m_lru_w_x, m_lru_b_x, m_lru_lambda, m_sgu_ln_g, m_sgu_ln_b, m_sgu_w_s, m_sgu_b_s, m_w_branch_a, m_w_branch_b, m_w_out, m_norm_ffn_g, m_w_up, m_w_down, m_final_norm_g, v_norm_mix_g, v_w_in, v_conv_w, v_conv_b, v_lru_w_a, v_lru_b_a, v_lru_w_x, v_lru_b_x, v_lru_lambda, v_sgu_ln_g, v_sgu_ln_b, v_sgu_w_s, v_sgu_b_s, v_w_branch_a, v_w_branch_b, v_w_out, v_norm_ffn_g, v_w_up, v_w_down, v_final_norm_g):
    w = dict(norm_mix_g=norm_mix_g, w_in=w_in, conv_w=conv_w, conv_b=conv_b, lru_w_a=lru_w_a, lru_b_a=lru_b_a,
             lru_w_x=lru_w_x, lru_b_x=lru_b_x, lru_lambda=lru_lambda, sgu_ln_g=sgu_ln_g, sgu_ln_b=sgu_ln_b,
             sgu_w_s=sgu_w_s, sgu_b_s=sgu_b_s, w_branch_a=w_branch_a, w_branch_b=w_branch_b, w_out=w_out,
             norm_ffn_g=norm_ffn_g, w_up=w_up, w_down=w_down, final_norm_g=final_norm_g)
    m = dict(norm_mix_g=m_norm_mix_g, w_in=m_w_in, conv_w=m_conv_w, conv_b=m_conv_b, lru_w_a=m_lru_w_a,
             lru_b_a=m_lru_b_a, lru_w_x=m_lru_w_x, lru_b_x=m_lru_b_x, lru_lambda=m_lru_lambda,
             sgu_ln_g=m_sgu_ln_g, sgu_ln_b=m_sgu_ln_b, sgu_w_s=m_sgu_w_s, sgu_b_s=m_sgu_b_s,
             w_branch_a=m_w_branch_a, w_branch_b=m_w_branch_b, w_out=m_w_out, norm_ffn_g=m_norm_ffn_g,
             w_up=m_w_up, w_down=m_w_down, final_norm_g=m_final_norm_g)
    v = dict(norm_mix_g=v_norm_mix_g, w_in=v_w_in, conv_w=v_conv_w, conv_b=v_conv_b, lru_w_a=v_lru_w_a,
             lru_b_a=v_lru_b_a, lru_w_x=v_lru_w_x, lru_b_x=v_lru_b_x, lru_lambda=v_lru_lambda,
             sgu_ln_g=v_sgu_ln_g, sgu_ln_b=v_sgu_ln_b, sgu_w_s=v_sgu_w_s, sgu_b_s=v_sgu_b_s,
             w_branch_a=v_w_branch_a, w_branch_b=v_w_branch_b, w_out=v_w_out, norm_ffn_g=v_norm_ffn_g,
             w_up=v_w_up, w_down=v_w_down, final_norm_g=v_final_norm_g)
    core = lax.axis_index("c")
    chip = 2 * lax.axis_index("x") + lax.axis_index("y")
    sel = jnp.stack([core, 1 - core, chip, 2 * chip + core]).astype(jnp.int32)
    this_core, other_core, this_chip = ("sel", 0), ("sel", 1), ("sel", 2)
    sds = jax.ShapeDtypeStruct

    ts = TOKEN_TILE
    halves = {k: (w[k].shape[1] // 2, w[k].shape[2]) for k in BIG}

    def half_view(k, a):
        return a.reshape((2 * N_QUARTERS,) + halves[k])

    def full_view(k, a):
        r2, cols = halves[k]
        if k in ("w_in", "w_up"):
            return a.reshape(1, N_QUARTERS, 2 * r2, cols)
        return a.reshape(1, 2 * N_QUARTERS * r2, cols)

    layer_bufs = [[], []]
    for k in BIG:
        _, r, cols = w[k].shape
        w4 = w[k].reshape(DEPTH, 1, r, cols)
        outs = _ew_call(lambda a, b: (a, b), "cast_weights", [(w4, (0, 0)), (w4, (1, 0))],
                        [(sds((1, N_QUARTERS, r, cols), BF), (0, this_chip))] * DEPTH, 1, sel)
        for l in range(DEPTH):
            layer_bufs[l].append(half_view(k, outs[l]))
    conv_buf = lax.dynamic_update_slice_in_dim(
        jnp.zeros((DEPTH, N_QUARTERS) + conv_w.shape[1:], F32), conv_w[:, None], chip, axis=1)
    sm = {k: w[k] for k in SMALL}
    sm["conv_w"] = _gather_call([conv_buf])[0].transpose(0, 2, 1, 3).reshape(DEPTH, CONV_WIDTH, D_RNN)

    def gather_start(tag, l, keys, after):
        bufs = [layer_bufs[l][BIG.index(k)] for k in keys]
        return _exchange_start(f"gather_start_{tag}", bufs, _gather_copies, 3 * len(keys), after)

    def gather_finish(tag, keys, started, after):
        send_sems, recv_sems, thru, _ = started
        landed = _exchange_wait(f"gather_wait_{tag}", send_sems, recv_sems, thru, _gather_copies, after)
        landed = _sibling_inplace_call("gather_forward", landed, _gather_forward_slabs, 3 * len(keys))
        return {k: full_view(k, a) for k, a in zip(keys, landed)}

    first, rest = ("w_in",), tuple(k for k in BIG if k != "w_in")
    started_a = gather_start("0a", 0, first, sm["conv_w"])
    started_b = gather_start("0b", 0, rest, started_a[3])
    started_1 = gather_start("1", 1, BIG, started_b[3])
    big0 = gather_finish("0a", first, started_a, started_1[3])
    sv0 = _layer_fwd_mix(x[0], big0, _layer_small(sm, 0, sel[0:1]), ts)
    big0.update(gather_finish("0b", rest, started_b, sv0["yb_pre"]))
    x_mid = _layer_fwd_out(sv0, big0, ts)
    big1 = gather_finish("1", BIG, started_1, x_mid)
    sv1 = _layer_fwd_mix(x_mid, big1, _layer_small(sm, 1, sel[0:1]), ts)
    x_out = _layer_fwd_out(sv1, big1, ts)
    dx, loss, dgf = _loss_call(x_out, loss_target[0], final_norm_g.reshape(1, -1), ts)

    def reduce_start(tag, gb, after):
        keys = tuple(gb)
        from_sibling = _sibling_send_call([gb[k][1] for k in keys])
        sums = [
            _ew_call(lambda a, b: (a + b.astype(F32),), "pair_sum", [(gb[k][0][None], (0, "g")), (r[None], (0, "g"))],
                     [(sds((1,) + r.shape, BF), (0, "g"))], N_QUARTERS)[0][0]
            for k, r in zip(keys, from_sibling)]
        zones = [lax.empty((3,) + a.shape[1:], BF) for a in sums]
        started = _exchange_start(f"reduce_start_{tag}", sums + zones, _owner_copies, 3 * len(keys), after)
        return keys, started

    def reduce_finish(tag, l, keys_started, after, reduced):
        keys, (send_sems, recv_sems, thru, _) = keys_started
        done = _exchange_wait(f"reduce_wait_{tag}", send_sems, recv_sems, thru, _owner_copies, after)
        sums, zones = done[:len(keys)], done[len(keys):]
        for i, k in enumerate(keys):
            r2, cols = halves[k]
            reduced[k] = _ew_call(
                lambda a, b, c, d: (((a.astype(F32) + b.astype(F32)) + c.astype(F32)) + d.astype(F32),),
                "quarter_sum", [(sums[i][None], (0, this_chip))] + [(zones[i][None], (0, j)) for j in range(3)],
                [(sds((DEPTH, 2, r2, cols), F32), (l, this_core))], 1, sel, into=reduced.get(k))[0]

    def behind(params, key, started):
        return dict(params, **{key: params[key] + started[1][3][0, 0]})

    dx1, gb_ffn, gs1 = _layer_bwd_ffn(dx, sv1, big1, ts)
    merge_out, gb_merge = _layer_bwd_merge(dx1, sv1, big1, ts)
    dx_mid, gb_in, gs1_mix = _layer_bwd_branches(dx1, merge_out, sv1, big1, lru_lambda[1], ts)
    exchange_1 = reduce_start("1", {**gb_ffn, **gb_merge, **gb_in}, dx_mid)
    sv0["p"] = behind(sv0["p"], "g2", exchange_1)
    dx1, gb_ffn, gs0 = _layer_bwd_ffn(dx_mid, sv0, big0, ts)
    exchange_0a = reduce_start("0a", gb_ffn, exchange_1[1][3])
    merge_out, gb_merge = _layer_bwd_merge(dx1, sv0, big0, ts)
    exchange_0b = reduce_start("0b", gb_merge, exchange_0a[1][3])
    sv0["p"] = behind(sv0["p"], "lg", exchange_0b)
    grad_x, gb_in, gs0_mix = _layer_bwd_branches(dx1, merge_out, sv0, big0, lru_lambda[0], ts)
    exchange_0c = reduce_start("0c", gb_in, exchange_0b[1][3])
    layer_gs = [{**gs0, **gs0_mix}, {**gs1, **gs1_mix}]
    gs = {k: jnp.stack([g[k] for g in layer_gs]) for k in layer_gs[0]}
    gs["final_norm_g"] = dgf[0]

    me = ("sel", 3)
    piece = (1, N_DEVICES, SMALL_ROWS, 128)
    packed = _pack_small(gs).reshape(piece)
    scatter = _exchange_start("small_scatter_start", [packed[0], lax.empty(piece[1:], F32)], _small_scatter_copies,
                              N_DEVICES - 1, exchange_0c[1][3])
    reduced = {}
    reduce_finish("1", 1, exchange_1, scatter[3], reduced)
    reduce_finish("0a", 0, exchange_0a, reduced["w_in"], reduced)
    reduce_finish("0b", 0, exchange_0b, reduced["w_down"], reduced)

    def swap_slabs(ref, c, i):
        layers = (1,) if BIG[i] == "w_in" else range(DEPTH)
        return [(ref.at[l, c], ref.at[l, 1 - c]) for l in layers]

    swapped = dict(zip(BIG, _sibling_inplace_call("grads_swap_halves", [reduced[k] for k in BIG], swap_slabs,
                                                  DEPTH * len(BIG) - 1)))

    def adamw_layers(k, grad, layer, into):
        if layer is None:
            views = [_as4(_as_rows(a)) for a in (w[k], grad, m[k], v[k])]
            idx = (0, 0)
        else:
            views = [a.reshape((1,) + w[k].shape) for a in (w[k], grad, m[k], v[k])]
            idx = (0, layer)
        return _ew_call(_adamw, "adamw_big", [(a, idx) for a in views], [(sds(views[0].shape, F32), idx)] * 3,
                        into=into)

    delta, new_m, new_v = {}, {}, {}
    for k in BIG:
        outs = adamw_layers(k, swapped[k], 1 if k == "w_in" else None, None)
        delta[k], new_m[k], new_v[k] = outs
    scattered = _exchange_wait("small_scatter_wait", scatter[0], scatter[1], scatter[2], _small_scatter_copies,
                               delta["w_out"])
    summed = _ew_call(
        lambda *parts: (functools.reduce(lambda a, b: a + b, parts),), "small_sum",
        [(scattered[0][None], (0, me))]
        + [(scattered[1][None], (0, lambda g, s, k=k: s[3] ^ k)) for k in range(1, N_DEVICES)],
        [(sds(piece, F32), (0, me))], 1, sel)[0]
    spread = _exchange_start("small_spread_start", [summed[0]], _small_spread_copies, N_DEVICES - 1, summed)
    reduced["w_in"] = swapped["w_in"]
    reduce_finish("0c", 0, exchange_0c, spread[3], reduced)
    last = _sibling_inplace_call("grads_swap_last", [reduced["w_in"]],
                                 lambda ref, c, i: [(ref.at[0, c], ref.at[0, 1 - c])], 1)[0]
    swapped["w_in"] = last
    delta["w_in"], new_m["w_in"], new_v["w_in"] = adamw_layers(
        "w_in", last, 0, [delta["w_in"], new_m["w_in"], new_v["w_in"]])
    grads_big = {k: swapped[k].reshape(w[k].shape) for k in BIG}
    for d in (delta, new_m, new_v):
        for k in BIG:
            d[k] = d[k].reshape(w[k].shape)
    gathered_small = _exchange_wait("small_spread_wait", spread[0], spread[1], spread[2], _small_spread_copies,
                                    delta["w_in"])[0]

    like = {k: jax.ShapeDtypeStruct(sm[k].shape, F32) for k in SMALL}
    grads_small = _unpack_small(gathered_small, like)
    conv_q = grads_small["conv_w"].reshape(DEPTH, CONV_WIDTH, N_QUARTERS, D_RNN // N_QUARTERS)
    grads_small["conv_w"] = lax.dynamic_index_in_dim(conv_q, chip, axis=2, keepdims=False)
    outs = _small_adamw_call(*[[_as_rows(d[k]) for k in SMALL] for d in (w, grads_small, m, v)])
    for d, o in zip((delta, new_m, new_v), outs):
        for k, a in zip(SMALL, o):
            d[k] = a.reshape(w[k].shape)

    grads = {**grads_big, **grads_small}
    total = lax.psum(loss[0, 0], ("x", "y", "c"))
    return (total, grad_x[None], *[grads[k] for k in WEIGHTS], *[delta[k] for k in WEIGHTS],
            *[new_m[k] for k in WEIGHTS], *[new_v[k] for k in WEIGHTS])
```

```python
import functools
import math

import jax
import jax.numpy as jnp
from jax import lax
from jax.experimental import pallas as pl
from jax.experimental.pallas import tpu as pltpu

F32 = jnp.float32
BF = jnp.bfloat16

DEPTH = 2
D_MODEL = 1024
D_RNN = 1280
D_SGU = 1024
D_FF = 4096
D_IN = 2 * D_RNN + 2 * D_SGU + 2 * D_MODEL
N_QUARTERS = 4
Q_IN = D_IN // N_QUARTERS
Q_FF = D_FF // N_QUARTERS
RNN_HEADS = 20
RNN_HEAD_DIM = 64
LRU_GROUP = 256
N_LRU_GROUPS = D_RNN // LRU_GROUP
HEADS_PER_GROUP = LRU_GROUP // RNN_HEAD_DIM
CONV_WIDTH = 4
LRU_C = 8.0
SGU_GROUPS = 8
SGU_BLOCK = 128
CHUNK = 64
EPS = 1e-6

ADAM_LR = 0.001
ADAM_B1 = 0.9
ADAM_B2 = 0.999
ADAM_EPS = 1e-08
ADAM_WD = 0.01
ADAM_STEP = 10

SUBLANES = 8
TOKEN_TILE = 512
VMEM_LIMIT_BYTES = 56 * 1024 * 1024

MESH = pl.DeviceIdType.MESH


def _params(semantics=None, vmem=True, **kw):
    return pltpu.CompilerParams(
        dimension_semantics=semantics,
        vmem_limit_bytes=VMEM_LIMIT_BYTES if vmem else None,
        **kw,
    )


def _dot(a, b):
    return jnp.dot(a, b, preferred_element_type=F32)


def _dot_nt(a, b):
    return lax.dot_general(a, b, (((1,), (1,)), ((), ())), preferred_element_type=F32)


def _dot_tn(a, b):
    return lax.dot_general(a, b, (((0,), (0,)), ((), ())), preferred_element_type=F32)


_GELU_C = math.sqrt(2.0 / math.pi)
_GELU_A = 0.044715


def _gelu(x):
    return 0.5 * x * (1.0 + jnp.tanh(_GELU_C * (x + _GELU_A * x * x * x)))


def _gelu_and_grad(x):
    x2 = x * x
    t = jnp.tanh(_GELU_C * (x + _GELU_A * x2 * x))
    du = _GELU_C * (1.0 + 3.0 * _GELU_A * x2)
    return 0.5 * x * (1.0 + t), 0.5 * (1.0 + t) + 0.5 * x * (1.0 - t * t) * du


def _rms_stats(x):
    return lax.rsqrt(jnp.mean(x * x, axis=-1, keepdims=True) + EPS)


def _rms_bwd(dy, x, g):
    rs = _rms_stats(x)
    n = x * rs
    dn = dy * g
    dx = rs * (dn - n * jnp.mean(dn * n, axis=-1, keepdims=True))
    return dx, dy * n


def _row_sum(x):
    return jnp.sum(x, axis=0, keepdims=True)


def _tile_spec(ts, width, col=0):
    return pl.BlockSpec((ts, width), lambda i, col=col: (i, col))


def _full_spec(shape):
    zeros = (0,) * len(shape)
    return pl.BlockSpec(shape, lambda *_: zeros)


def _layer_spec(w, layer):
    zeros = (0,) * (w.ndim - 1)
    return pl.BlockSpec((None,) + tuple(w.shape[1:]), lambda *_: (layer,) + zeros)


def _norm_call(x, g, ts):
    s = x.shape[0]

    def body(x_ref, g_ref, h_ref):
        xv = x_ref[...]
        h_ref[...] = (xv * _rms_stats(xv) * g_ref[...]).astype(BF)

    return pl.pallas_call(
        body, name="norm_fwd", grid=(s // ts,),
        in_specs=[_tile_spec(ts, D_MODEL), _full_spec((1, D_MODEL))],
        out_specs=_tile_spec(ts, D_MODEL),
        out_shape=jax.ShapeDtypeStruct((s, D_MODEL), BF),
        compiler_params=_params(("parallel",)),
    )(x, g)


def _inproj_call(h, w_in, layer, ts):
    s = h.shape[0]

    def body(h_ref, w_ref, o_ref):
        o_ref[...] = _dot(h_ref[...], w_ref[...]).astype(BF)

    return pl.pallas_call(
        body, name="inproj_fwd", grid=(N_QUARTERS, s // ts),
        in_specs=[
            pl.BlockSpec((ts, D_MODEL), lambda q, i: (i, 0)),
            pl.BlockSpec((None, None, D_MODEL, Q_IN), lambda q, i: (layer, q, 0, 0)),
        ],
        out_specs=pl.BlockSpec((ts, Q_IN), lambda q, i: (i, q)),
        out_shape=jax.ShapeDtypeStruct((s, D_IN), BF),
        compiler_params=_params(("parallel", "parallel")),
    )(h, w_in)


def _shift_down(x, tail, s):
    xr = pltpu.roll(x, s, 0)
    tr = pltpu.roll(tail, s, 0)
    row = lax.broadcasted_iota(jnp.int32, tail.shape, 0)
    top = jnp.where(row < s, tr, xr[0:SUBLANES])
    return jnp.concatenate([top, xr[SUBLANES:]], axis=0)


def _shift_up(x, head, s):
    t = x.shape[0]
    xr = pltpu.roll(x, t - s, 0)
    hr = pltpu.roll(head, SUBLANES - s, 0)
    row = lax.broadcasted_iota(jnp.int32, head.shape, 0)
    bottom = jnp.where(row >= SUBLANES - s, hr, xr[t - SUBLANES:])
    return jnp.concatenate([xr[: t - SUBLANES], bottom], axis=0)


def _conv_fwd(x, tail, cw_ref, cb_ref):
    shifted = [x] + [_shift_down(x, tail, s) for s in range(1, CONV_WIDTH)]
    out = cb_ref[...] + cw_ref[CONV_WIDTH - 1:CONV_WIDTH, :] * x
    for s in range(1, CONV_WIDTH):
        k = CONV_WIDTH - 1 - s
        out = out + cw_ref[k:k + 1, :] * shifted[s]
    return out, shifted


def _group_dot(x_bf, w_ref, dot):
    cols = [dot(x_bf[:, g * LRU_GROUP:(g + 1) * LRU_GROUP], w_ref[g]) for g in range(N_LRU_GROUPS)]
    return jnp.concatenate(cols, axis=1)


def _lru_gates(xr, wa_ref, wx_ref, ba_ref, bx_ref, sp_ref):
    xb = xr.astype(BF)
    r = jax.nn.sigmoid(_group_dot(xb, wa_ref, _dot) + ba_ref[...])
    i = jax.nn.sigmoid(_group_dot(xb, wx_ref, _dot) + bx_ref[...])
    log_a = (-LRU_C * r) * sp_ref[...]
    a = jnp.exp(log_a)
    nrm2 = -jnp.tanh(log_a) * (a * a + 1.0)
    inv_nrm = lax.rsqrt(jnp.maximum(nrm2, 1e-36))
    return r, i, a, nrm2 * inv_nrm, inv_nrm


def _linear_scan(a, b, carry, al_ref, bl_ref, h_ref, reverse):
    t, c = a.shape
    rowm = lax.broadcasted_iota(jnp.int32, (t, c), 0) & (SUBLANES - 1)
    for d in (1, 2, 4):
        if reverse:
            keep, sh = rowm < SUBLANES - d, t - d
        else:
            keep, sh = rowm >= d, d
        a_sh = jnp.where(keep, pltpu.roll(a, sh, 0), 1.0)
        b_sh = jnp.where(keep, pltpu.roll(b, sh, 0), 0.0)
        b = a * b_sh + b
        a = a * a_sh
    al_ref[...] = a
    bl_ref[...] = b
    groups = t // SUBLANES

    def step(j, state):
        jj = groups - 1 - j if reverse else j
        off = pl.multiple_of(jj * SUBLANES, SUBLANES)
        rows = bl_ref[pl.ds(off, SUBLANES), :] + al_ref[pl.ds(off, SUBLANES), :] * state
        h_ref[pl.ds(off, SUBLANES), :] = rows
        last = rows[0:1, :] if reverse else rows[SUBLANES - 1:SUBLANES, :]
        return jnp.broadcast_to(last, (SUBLANES, c))

    out = lax.fori_loop(0, groups, step, jnp.broadcast_to(carry, (SUBLANES, c)))
    return out[0:1, :]


def _rnn_fwd_call(proj, wa, wx, ba, bx, sp, cw, cb, ts):
    s = proj.shape[0]

    def body(xg_ref, wa_ref, wx_ref, ba_ref, bx_ref, sp_ref, cw_ref, cb_ref, hr_ref, ya_ref,
             tail_sc, carry_sc, al_sc, bl_sc, h_sc):
        @pl.when(pl.program_id(0) == 0)
        def _():
            tail_sc[...] = jnp.zeros_like(tail_sc)
            carry_sc[...] = jnp.zeros_like(carry_sc)

        x = xg_ref[:, :D_RNN].astype(F32)
        g = xg_ref[:, D_RNN:].astype(F32)
        xr, _ = _conv_fwd(x, tail_sc[...], cw_ref, cb_ref)
        tail_sc[...] = x[ts - SUBLANES:, :]
        _, i, a, nrm, _ = _lru_gates(xr, wa_ref, wx_ref, ba_ref, bx_ref, sp_ref)
        carry_sc[...] = _linear_scan(a, nrm * (i * xr), carry_sc[...], al_sc, bl_sc, h_sc, False)
        h = h_sc[...]
        hr_ref[...] = h.astype(BF)
        ya_ref[...] = (h * _gelu(g)).astype(BF)

    gw = (N_LRU_GROUPS, LRU_GROUP, LRU_GROUP)
    return pl.pallas_call(
        body, name="rnn_fwd", grid=(s // ts,),
        in_specs=[_tile_spec(ts, 2 * D_RNN), _full_spec(gw), _full_spec(gw),
                  _full_spec((1, D_RNN)), _full_spec((1, D_RNN)), _full_spec((1, D_RNN)),
                  _full_spec((CONV_WIDTH, D_RNN)), _full_spec((1, D_RNN))],
        out_specs=[_tile_spec(ts, D_RNN), _tile_spec(ts, D_RNN)],
        out_shape=[jax.ShapeDtypeStruct((s, D_RNN), BF), jax.ShapeDtypeStruct((s, D_RNN), BF)],
        scratch_shapes=[pltpu.VMEM((SUBLANES, D_RNN), F32), pltpu.VMEM((1, D_RNN), F32),
                        pltpu.VMEM((ts, D_RNN), F32), pltpu.VMEM((ts, D_RNN), F32),
                        pltpu.VMEM((ts, D_RNN), F32)],
        compiler_params=_params(("arbitrary",)),
    )(proj, wa, wx, ba, bx, sp, cw, cb)


def _layernorm_fwd(x):
    mu = jnp.mean(x, axis=-1, keepdims=True)
    xc = x - mu
    rstd = lax.rsqrt(jnp.mean(xc * xc, axis=-1, keepdims=True) + EPS)
    return xc * rstd, rstd


def _sgu_mix(vn_bf, wm_ref, bsb_ref, ts):
    rows = []
    for blk in range(ts // SGU_BLOCK):
        r0 = blk * SGU_BLOCK
        cols = [
            _dot(wm_ref[g], vn_bf[r0:r0 + SGU_BLOCK, g * SGU_BLOCK:(g + 1) * SGU_BLOCK]) + bsb_ref[g]
            for g in range(SGU_GROUPS)
        ]
        rows.append(jnp.concatenate(cols, axis=1))
    return jnp.concatenate(rows, axis=0)


def _sgu_fwd_call(proj, wm, bsb, lg, lb, ts):
    s = proj.shape[0]

    def body(uv_ref, wm_ref, bsb_ref, lg_ref, lb_ref, yb_ref):
        gu = _gelu(uv_ref[:, :D_SGU].astype(F32))
        gv = _gelu(uv_ref[:, D_SGU:2 * D_SGU].astype(F32))
        nh, _ = _layernorm_fwd(gv)
        vn = (nh * lg_ref[...] + lb_ref[...]).astype(BF)
        yb_ref[...] = (gu * _sgu_mix(vn, wm_ref, bsb_ref, ts)).astype(BF)

    sw = (SGU_GROUPS, SGU_BLOCK, SGU_BLOCK)
    return pl.pallas_call(
        body, name="sgu_fwd", grid=(s // ts,),
        in_specs=[_tile_spec(ts, 2 * D_RNN, 1), _full_spec(sw), _full_spec(sw),
                  _full_spec((1, D_SGU)), _full_spec((1, D_SGU))],
        out_specs=_tile_spec(ts, D_SGU),
        out_shape=jax.ShapeDtypeStruct((s, D_SGU), BF),
        compiler_params=_params(("parallel",)),
    )(proj, wm, bsb, lg, lb)


_GATE_COL0 = (2 * D_RNN + 2 * D_SGU) // 512


def _gate_specs(ts):
    return [_tile_spec(ts, 512, _GATE_COL0 + j) for j in range(4)]


def _merge_call(x, proj, ya_pre, yb_pre, w_ba, w_bb, w_out, g2, layer, ts):
    s = x.shape[0]

    def body(x_ref, ga0, ga1, gb0, gb1, ya_ref, yb_ref, wa_ref, wb_ref, wo_ref, g2_ref,
             x1_ref, yao_ref, ybo_ref, mg_ref, h2_ref):
        ya = _dot(ya_ref[...], wa_ref[...])
        yb = _dot(yb_ref[...], wb_ref[...])
        sa = jax.nn.sigmoid(jnp.concatenate([ga0[...], ga1[...]], axis=1).astype(F32))
        sb = jax.nn.sigmoid(jnp.concatenate([gb0[...], gb1[...]], axis=1).astype(F32))
        merged = (sa * ya + sb * yb).astype(BF)
        x1 = x_ref[...] + _dot(merged, wo_ref[...])
        x1_ref[...] = x1
        yao_ref[...] = ya.astype(BF)
        ybo_ref[...] = yb.astype(BF)
        mg_ref[...] = merged
        h2_ref[...] = (x1 * _rms_stats(x1) * g2_ref[...]).astype(BF)

    act = jax.ShapeDtypeStruct((s, D_MODEL), BF)
    return pl.pallas_call(
        body, name="merge_fwd", grid=(s // ts,),
        in_specs=[_tile_spec(ts, D_MODEL)] + _gate_specs(ts) + [
            _tile_spec(ts, D_RNN), _tile_spec(ts, D_SGU),
            _layer_spec(w_ba, layer), _layer_spec(w_bb, layer), _layer_spec(w_out, layer),
            _full_spec((1, D_MODEL))],
        out_specs=[_tile_spec(ts, D_MODEL)] * 5,
        out_shape=[jax.ShapeDtypeStruct((s, D_MODEL), F32), act, act, act, act],
        compiler_params=_params(("parallel",)),
    )(x, proj, proj, proj, proj, ya_pre, yb_pre, w_ba, w_bb, w_out, g2)


def _ffn_call(x1, h2, w_up, w_down, layer, ts):
    s = x1.shape[0]

    def body(x1_ref, h2_ref, wu_ref, wd_ref, x2_ref, p_ref):
        h2v = h2_ref[...]
        acc = x1_ref[...]
        for q in range(N_QUARTERS):
            p = _dot(h2v, wu_ref[q])
            p_ref[:, q * Q_FF:(q + 1) * Q_FF] = p.astype(BF)
            f = jnp.square(jnp.maximum(p, 0.0)).astype(BF)
            acc = acc + _dot(f, wd_ref[q * Q_FF:(q + 1) * Q_FF, :])
        x2_ref[...] = acc

    return pl.pallas_call(
        body, name="ffn_fwd", grid=(s // ts,),
        in_specs=[_tile_spec(ts, D_MODEL), _tile_spec(ts, D_MODEL),
                  pl.BlockSpec((None, N_QUARTERS, D_MODEL, Q_FF), lambda i: (layer, 0, 0, 0)),
                  pl.BlockSpec((None, D_FF, D_MODEL), lambda i: (layer, 0, 0))],
        out_specs=[_tile_spec(ts, D_MODEL), _tile_spec(ts, D_FF)],
        out_shape=[jax.ShapeDtypeStruct((s, D_MODEL), F32), jax.ShapeDtypeStruct((s, D_FF), BF)],
        compiler_params=_params(("parallel",)),
    )(x1, h2, w_up, w_down)


def _loss_call(x, target, gf, ts):
    s = x.shape[0]

    def body(x_ref, t_ref, g_ref, dx_ref, loss_ref, dg_ref):
        @pl.when(pl.program_id(0) == 0)
        def _():
            loss_ref[...] = jnp.zeros_like(loss_ref)
            dg_ref[...] = jnp.zeros_like(dg_ref)

        xv = x_ref[...]
        gv = g_ref[...]
        err = xv * _rms_stats(xv) * gv - t_ref[...]
        part = 0.5 * jnp.sum(jnp.mean(err * err, axis=-1, keepdims=True), axis=0, keepdims=True)
        loss_ref[...] += jnp.broadcast_to(part, loss_ref.shape)
        dx, dg = _rms_bwd(err * (1.0 / D_MODEL), xv, gv)
        dx_ref[...] = dx
        dg_ref[...] += _row_sum(dg)

    return pl.pallas_call(
        body, name="loss_head", grid=(s // ts,),
        in_specs=[_tile_spec(ts, D_MODEL), _tile_spec(ts, D_MODEL), _full_spec((1, D_MODEL))],
        out_specs=[_tile_spec(ts, D_MODEL), _full_spec((1, 128)), _full_spec((1, D_MODEL))],
        out_shape=[jax.ShapeDtypeStruct((s, D_MODEL), F32), jax.ShapeDtypeStruct((1, 128), F32),
                   jax.ShapeDtypeStruct((1, D_MODEL), F32)],
        compiler_params=_params(("arbitrary",)),
    )(x, target, gf)


def _ffn_bwd_call(dx2, p, x1, g2, w_up, w_down, layer, ts):
    s = dx2.shape[0]

    def body(dx2_ref, p_ref, x1_ref, g2_ref, wu_ref, wd_ref, dx1_ref, dp_ref, dg_ref):
        @pl.when(pl.program_id(0) == 0)
        def _():
            dg_ref[...] = jnp.zeros_like(dg_ref)

        dx2v = dx2_ref[...]
        dyb = dx2v.astype(BF)
        dh2 = jnp.zeros((ts, D_MODEL), F32)
        for q in range(N_QUARTERS):
            cols = slice(q * Q_FF, (q + 1) * Q_FF)
            df = _dot_nt(dyb, wd_ref[cols, :])
            dp = (df * (2.0 * jnp.maximum(p_ref[:, cols].astype(F32), 0.0))).astype(BF)
            dp_ref[:, cols] = dp
            dh2 = dh2 + _dot_nt(dp, wu_ref[q])
        dx, dg = _rms_bwd(dh2, x1_ref[...], g2_ref[...])
        dx1_ref[...] = dx2v + dx
        dg_ref[...] += _row_sum(dg)

    return pl.pallas_call(
        body, name="ffn_bwd", grid=(s // ts,),
        in_specs=[_tile_spec(ts, D_MODEL), _tile_spec(ts, D_FF), _tile_spec(ts, D_MODEL),
                  _full_spec((1, D_MODEL)),
                  pl.BlockSpec((None, N_QUARTERS, D_MODEL, Q_FF), lambda i: (layer, 0, 0, 0)),
                  pl.BlockSpec((None, D_FF, D_MODEL), lambda i: (layer, 0, 0))],
        out_specs=[_tile_spec(ts, D_MODEL), _tile_spec(ts, D_FF), _full_spec((1, D_MODEL))],
        out_shape=[jax.ShapeDtypeStruct((s, D_MODEL), F32), jax.ShapeDtypeStruct((s, D_FF), BF),
                   jax.ShapeDtypeStruct((1, D_MODEL), F32)],
        compiler_params=_params(("arbitrary",)),
    )(dx2, p, x1, g2, w_up, w_down)


def _merge_bwd_call(dx1, proj, ya, yb, w_ba, w_bb, w_out, layer, ts):
    s = dx1.shape[0]

    def body(dx1_ref, ga0, ga1, gb0, gb1, ya_ref, yb_ref, wa_ref, wb_ref, wo_ref,
             dya_ref, dyb_ref, dgate_ref, dyap_ref, dybp_ref):
        dm = _dot_nt(dx1_ref[...].astype(BF), wo_ref[...])
        sa = jax.nn.sigmoid(jnp.concatenate([ga0[...], ga1[...]], axis=1).astype(F32))
        sb = jax.nn.sigmoid(jnp.concatenate([gb0[...], gb1[...]], axis=1).astype(F32))
        dya = (dm * sa).astype(BF)
        dyb = (dm * sb).astype(BF)
        dya_ref[...] = dya
        dyb_ref[...] = dyb
        dgate_ref[:, :D_MODEL] = (dm * ya_ref[...].astype(F32) * sa * (1.0 - sa)).astype(BF)
        dgate_ref[:, D_MODEL:] = (dm * yb_ref[...].astype(F32) * sb * (1.0 - sb)).astype(BF)
        dyap_ref[...] = _dot_nt(dya, wa_ref[...]).astype(BF)
        dybp_ref[...] = _dot_nt(dyb, wb_ref[...]).astype(BF)

    act = jax.ShapeDtypeStruct((s, D_MODEL), BF)
    return pl.pallas_call(
        body, name="merge_bwd", grid=(s // ts,),
        in_specs=[_tile_spec(ts, D_MODEL)] + _gate_specs(ts) + [
            _tile_spec(ts, D_MODEL), _tile_spec(ts, D_MODEL),
            _layer_spec(w_ba, layer), _layer_spec(w_bb, layer), _layer_spec(w_out, layer)],
        out_specs=[_tile_spec(ts, D_MODEL), _tile_spec(ts, D_MODEL), _tile_spec(ts, 2 * D_MODEL),
                   _tile_spec(ts, D_RNN), _tile_spec(ts, D_SGU)],
        out_shape=[act, act, jax.ShapeDtypeStruct((s, 2 * D_MODEL), BF),
                   jax.ShapeDtypeStruct((s, D_RNN), BF), jax.ShapeDtypeStruct((s, D_SGU), BF)],
        compiler_params=_params(("parallel",)),
    )(dx1, proj, proj, proj, proj, ya, yb, w_ba, w_bb, w_out)


def _sgu_bwd_call(dyb_pre, proj, wm, bsb, mask, lg, lb, ts):
    s = proj.shape[0]

    def body(dy_ref, uv_ref, wm_ref, bsb_ref, mask_ref, lg_ref, lb_ref,
             duv_ref, dws_ref, dbs_ref, dlg_ref, dlb_ref, dm_sc):
        step = pl.program_id(0)

        @pl.when(step == 0)
        def _():
            dws_ref[...] = jnp.zeros_like(dws_ref)
            dlg_ref[...] = jnp.zeros_like(dlg_ref)
            dlb_ref[...] = jnp.zeros_like(dlb_ref)
            dm_sc[...] = jnp.zeros_like(dm_sc)

        gu, dgu_du = _gelu_and_grad(uv_ref[:, :D_SGU].astype(F32))
        gv, dgv_dv = _gelu_and_grad(uv_ref[:, D_SGU:2 * D_SGU].astype(F32))
        nh, rstd = _layernorm_fwd(gv)
        lgv = lg_ref[...]
        vn = (nh * lgv + lb_ref[...]).astype(BF)
        dy = dy_ref[...].astype(F32)
        du = dy * _sgu_mix(vn, wm_ref, bsb_ref, ts) * dgu_du
        dmix = dy * gu
        dmix_bf = dmix.astype(BF)
        dm_acc = dm_sc[...]
        rows = []
        for blk in range(ts // SGU_BLOCK):
            r0 = blk * SGU_BLOCK
            dm_acc = dm_acc + dmix[r0:r0 + SGU_BLOCK, :]
            cols = []
            for g in range(SGU_GROUPS):
                c0 = g * SGU_BLOCK
                dmg = dmix_bf[r0:r0 + SGU_BLOCK, c0:c0 + SGU_BLOCK]
                cols.append(_dot_tn(wm_ref[g], dmg))
                dws_ref[g] += mask_ref[...] * _dot_nt(dmg, vn[r0:r0 + SGU_BLOCK, c0:c0 + SGU_BLOCK])
            rows.append(jnp.concatenate(cols, axis=1))
        dm_sc[...] = dm_acc
        dvn = jnp.concatenate(rows, axis=0)
        dlg_ref[...] += _row_sum(dvn * nh)
        dlb_ref[...] += _row_sum(dvn)
        dnh = dvn * lgv
        dgv = rstd * (dnh - jnp.mean(dnh, axis=-1, keepdims=True)
                      - nh * jnp.mean(dnh * nh, axis=-1, keepdims=True))
        duv_ref[:, :D_SGU] = du.astype(BF)
        duv_ref[:, D_SGU:] = (dgv * dgv_dv).astype(BF)

        @pl.when(step == pl.num_programs(0) - 1)
        def _():
            for g in range(SGU_GROUPS):
                dbs_ref[:, g:g + 1] = jnp.sum(
                    dm_acc[:, g * SGU_BLOCK:(g + 1) * SGU_BLOCK], axis=1, keepdims=True)

    sw = (SGU_GROUPS, SGU_BLOCK, SGU_BLOCK)
    return pl.pallas_call(
        body, name="sgu_bwd", grid=(s // ts,),
        in_specs=[_tile_spec(ts, D_SGU), _tile_spec(ts, 2 * D_RNN, 1), _full_spec(sw), _full_spec(sw),
                  _full_spec((SGU_BLOCK, SGU_BLOCK)), _full_spec((1, D_SGU)), _full_spec((1, D_SGU))],
        out_specs=[_tile_spec(ts, 2 * D_SGU), _full_spec(sw), _full_spec((SGU_BLOCK, SGU_GROUPS)),
                   _full_spec((1, D_SGU)), _full_spec((1, D_SGU))],
        out_shape=[jax.ShapeDtypeStruct((s, 2 * D_SGU), BF), jax.ShapeDtypeStruct(sw, F32),
                   jax.ShapeDtypeStruct((SGU_BLOCK, SGU_GROUPS), F32),
                   jax.ShapeDtypeStruct((1, D_SGU), F32), jax.ShapeDtypeStruct((1, D_SGU), F32)],
        scratch_shapes=[pltpu.VMEM((SGU_BLOCK, D_SGU), F32)],
        compiler_params=_params(("arbitrary",)),
    )(dyb_pre, proj, wm, bsb, mask, lg, lb)


_ROW_DBA, _ROW_DBX, _ROW_DSP, _ROW_DCB, _ROW_DCW = 0, 1, 2, 3, 4
_PREV_ROWS = 16


def _rnn_bwd_call(dya_pre, proj, hr, wa, wx, ba, bx, sp, cw, cb, ts):
    s = proj.shape[0]
    nt = s // ts
    per = ts // _PREV_ROWS

    def tile(i):
        return nt - 1 - i

    def prev(i):
        return jnp.maximum(tile(i) * per - 1, 0)

    def body(dy_ref, xg_ref, xgp_ref, hr_ref, hrp_ref, wa_ref, wx_ref, ba_ref, bx_ref, sp_ref,
             cw_ref, cb_ref, dxg_ref, dwa_ref, dwx_ref, vec_ref,
             lam_carry, a_first, dxr_head, al_sc, bl_sc, lam_sc):
        step = pl.program_id(0)

        @pl.when(step == 0)
        def _():
            dwa_ref[...] = jnp.zeros_like(dwa_ref)
            dwx_ref[...] = jnp.zeros_like(dwx_ref)
            vec_ref[...] = jnp.zeros_like(vec_ref)
            lam_carry[...] = jnp.zeros_like(lam_carry)
            a_first[...] = jnp.zeros_like(a_first)
            dxr_head[...] = jnp.zeros_like(dxr_head)

        has_prev = (step < nt - 1).astype(F32)
        x = xg_ref[:, :D_RNN].astype(F32)
        g = xg_ref[:, D_RNN:].astype(F32)
        x_tail = xgp_ref[_PREV_ROWS - SUBLANES:, :D_RNN].astype(F32) * has_prev
        h_tail = hrp_ref[_PREV_ROWS - SUBLANES:, :].astype(F32) * has_prev
        xr, x_shifted = _conv_fwd(x, x_tail, cw_ref, cb_ref)
        r, i, a, nrm, inv_nrm = _lru_gates(xr, wa_ref, wx_ref, ba_ref, bx_ref, sp_ref)
        h = hr_ref[...].astype(F32)
        dy = dy_ref[...].astype(F32)
        gg, dgg = _gelu_and_grad(g)

        coef = _shift_up(a, jnp.broadcast_to(a_first[...], (SUBLANES, D_RNN)), 1)
        lam_carry[...] = _linear_scan(coef, dy * gg, lam_carry[...], al_sc, bl_sc, lam_sc, True)
        a_first[...] = a[0:1, :]
        lam = lam_sc[...]

        da = lam * _shift_down(h, h_tail, 1)
        dnrm = lam * (i * xr)
        di = lam * nrm * xr
        dlog_a = da * a - dnrm * (a * a) * inv_nrm
        spv = sp_ref[...]
        dza = (dlog_a * (-LRU_C * spv)) * (r * (1.0 - r))
        dzx = di * (i * (1.0 - i))
        vec_ref[_ROW_DSP:_ROW_DSP + 1, :] += _row_sum(dlog_a * (-LRU_C * r))
        vec_ref[_ROW_DBA:_ROW_DBA + 1, :] += _row_sum(dza)
        vec_ref[_ROW_DBX:_ROW_DBX + 1, :] += _row_sum(dzx)
        xb = xr.astype(BF)
        dza_bf = dza.astype(BF)
        dzx_bf = dzx.astype(BF)
        for grp in range(N_LRU_GROUPS):
            cols = slice(grp * LRU_GROUP, (grp + 1) * LRU_GROUP)
            dwa_ref[grp] += _dot_tn(xb[:, cols], dza_bf[:, cols])
            dwx_ref[grp] += _dot_tn(xb[:, cols], dzx_bf[:, cols])
        dxr = (lam * nrm * i + _group_dot(dza_bf, wa_ref, _dot_nt) + _group_dot(dzx_bf, wx_ref, _dot_nt))

        vec_ref[_ROW_DCB:_ROW_DCB + 1, :] += _row_sum(dxr)
        head = dxr_head[...]
        dx = cw_ref[CONV_WIDTH - 1:CONV_WIDTH, :] * dxr
        vec_ref[_ROW_DCW + 3:_ROW_DCW + 4, :] += _row_sum(dxr * x)
        for sft in range(1, CONV_WIDTH):
            k = CONV_WIDTH - 1 - sft
            dx = dx + cw_ref[k:k + 1, :] * _shift_up(dxr, head, sft)
            vec_ref[_ROW_DCW + k:_ROW_DCW + k + 1, :] += _row_sum(dxr * x_shifted[sft])
        dxr_head[...] = dxr[0:SUBLANES, :]
        dxg_ref[:, :D_RNN] = dx.astype(BF)
        dxg_ref[:, D_RNN:] = (dy * h * dgg).astype(BF)

    gw = (N_LRU_GROUPS, LRU_GROUP, LRU_GROUP)
    rev = lambda width: pl.BlockSpec((ts, width), lambda i: (tile(i), 0))
    return pl.pallas_call(
        body, name="rnn_bwd", grid=(nt,),
        in_specs=[rev(D_RNN), rev(2 * D_RNN),
                  pl.BlockSpec((_PREV_ROWS, 2 * D_RNN), lambda i: (prev(i), 0)),
                  rev(D_RNN),
                  pl.BlockSpec((_PREV_ROWS, D_RNN), lambda i: (prev(i), 0)),
                  _full_spec(gw), _full_spec(gw),
                  _full_spec((1, D_RNN)), _full_spec((1, D_RNN)), _full_spec((1, D_RNN)),
                  _full_spec((CONV_WIDTH, D_RNN)), _full_spec((1, D_RNN))],
        out_specs=[rev(2 * D_RNN), _full_spec(gw), _full_spec(gw), _full_spec((SUBLANES, D_RNN))],
        out_shape=[jax.ShapeDtypeStruct((s, 2 * D_RNN), BF), jax.ShapeDtypeStruct(gw, F32),
                   jax.ShapeDtypeStruct(gw, F32), jax.ShapeDtypeStruct((SUBLANES, D_RNN), F32)],
        scratch_shapes=[pltpu.VMEM((1, D_RNN), F32), pltpu.VMEM((1, D_RNN), F32),
                        pltpu.VMEM((SUBLANES, D_RNN), F32),
                        pltpu.VMEM((ts, D_RNN), F32), pltpu.VMEM((ts, D_RNN), F32),
                        pltpu.VMEM((ts, D_RNN), F32)],
        compiler_params=_params(("arbitrary",)),
    )(dya_pre, proj, proj, hr, hr, wa, wx, ba, bx, sp, cw, cb)


def _inproj_bwd_call(dxg, duv, dgate, dx1, x, g1, w_in, layer, ts):
    s = x.shape[0]

    def body(dxg_ref, duv_ref, dgt_ref, dx1_ref, x_ref, g_ref, w_ref, dx_ref, dproj_ref, dg_ref):
        @pl.when(pl.program_id(0) == 0)
        def _():
            dg_ref[...] = jnp.zeros_like(dg_ref)

        dproj = jnp.concatenate([dxg_ref[...], duv_ref[...], dgt_ref[...]], axis=1)
        dproj_ref[...] = dproj
        dh = jnp.zeros((ts, D_MODEL), F32)
        for q in range(N_QUARTERS):
            dh = dh + _dot_nt(dproj[:, q * Q_IN:(q + 1) * Q_IN], w_ref[q])
        dx, dg = _rms_bwd(dh, x_ref[...], g_ref[...])
        dx_ref[...] = dx1_ref[...] + dx
        dg_ref[...] += _row_sum(dg)

    return pl.pallas_call(
        body, name="inproj_bwd", grid=(s // ts,),
        in_specs=[_tile_spec(ts, 2 * D_RNN), _tile_spec(ts, 2 * D_SGU), _tile_spec(ts, 2 * D_MODEL),
                  _tile_spec(ts, D_MODEL), _tile_spec(ts, D_MODEL), _full_spec((1, D_MODEL)),
                  pl.BlockSpec((None, N_QUARTERS, D_MODEL, Q_IN), lambda i: (layer, 0, 0, 0))],
        out_specs=[_tile_spec(ts, D_MODEL), _tile_spec(ts, D_IN), _full_spec((1, D_MODEL))],
        out_shape=[jax.ShapeDtypeStruct((s, D_MODEL), F32), jax.ShapeDtypeStruct((s, D_IN), BF),
                   jax.ShapeDtypeStruct((1, D_MODEL), F32)],
        compiler_params=_params(("arbitrary",)),
    )(dxg, duv, dgate, dx1, x, g1, w_in)


def _relu_sq(p):
    return jnp.square(jnp.maximum(p.astype(F32), 0.0))


def _wgrad_call(a, b, core, tm, tn, tk, col_blocked, name, a_fn=None):
    s, m = a.shape
    n = b.shape[1]
    r, cols = (m, n // N_QUARTERS) if col_blocked else (m // N_QUARTERS, n)
    r2 = r // 2
    per_tile = tm // r
    steps = s // tk

    def body(core_ref, a_ref, b_ref, keep_ref, send_ref, *acc):
        av = a_ref[...]
        if a_fn is not None:
            av = a_fn(av)
        prod = _dot_tn(av.astype(BF), b_ref[...].astype(BF))

        def emit(total):
            for h in range(2):
                @pl.when(core_ref[0] == h)
                def _():
                    for q in range(per_tile):
                        keep_ref[q] = total[q * r + h * r2:q * r + (h + 1) * r2]
                        send_ref[q] = total[q * r + (1 - h) * r2:q * r + (2 - h) * r2].astype(BF)

        if steps == 1:
            emit(prod)
        else:
            acc_ref, = acc
            step = pl.program_id(2)

            @pl.when(step == 0)
            def _():
                acc_ref[...] = prod

            @pl.when(jnp.logical_and(step > 0, step < steps - 1))
            def _():
                acc_ref[...] += prod

            @pl.when(step == steps - 1)
            def _():
                emit(acc_ref[...] + prod)

    if col_blocked:
        per_q = cols // tn
        out_spec = pl.BlockSpec((1, r2, tn), lambda i, j, k, c: (j // per_q, 0, j % per_q))
    else:
        out_spec = pl.BlockSpec((per_tile, r2, tn), lambda i, j, k, c: (i, 0, j))
    return pl.pallas_call(
        body, name=name,
        out_shape=[jax.ShapeDtypeStruct((N_QUARTERS, r2, cols), F32),
                   jax.ShapeDtypeStruct((N_QUARTERS, r2, cols), BF)],
        grid_spec=pltpu.PrefetchScalarGridSpec(
            num_scalar_prefetch=1, grid=(m // tm, n // tn, steps),
            in_specs=[pl.BlockSpec((tk, tm), lambda i, j, k, c: (k, i)),
                      pl.BlockSpec((tk, tn), lambda i, j, k, c: (k, j))],
            out_specs=[out_spec, out_spec],
            scratch_shapes=[] if steps == 1 else [pltpu.VMEM((tm, tn), F32)]),
        compiler_params=_params(("parallel", "parallel", "arbitrary")),
    )(core, a, b)


BIG = ("w_in", "w_up", "w_down", "w_branch_a", "w_branch_b", "w_out")


def _block_diag(w):
    w4 = w.reshape(N_LRU_GROUPS, HEADS_PER_GROUP, RNN_HEAD_DIM, RNN_HEAD_DIM)
    eye = jnp.eye(HEADS_PER_GROUP, dtype=w.dtype)
    return jnp.einsum("gjio,jk->gjiko", w4, eye).reshape(N_LRU_GROUPS, LRU_GROUP, LRU_GROUP)


def _block_diag_extract(d):
    d5 = d.reshape(N_LRU_GROUPS, HEADS_PER_GROUP, RNN_HEAD_DIM, HEADS_PER_GROUP, RNN_HEAD_DIM)
    blocks = [d5[:, j, :, j, :] for j in range(HEADS_PER_GROUP)]
    return jnp.stack(blocks, axis=1).reshape(RNN_HEADS, RNN_HEAD_DIM, RNN_HEAD_DIM)


def _sgu_mask():
    chunk = jnp.arange(SGU_BLOCK) // CHUNK
    return (chunk[:, None] >= chunk[None, :]).astype(F32)


def _layer_small(sm, l, core):
    row = lambda v: v.reshape(1, -1)
    return dict(
        core=core,
        g1=row(sm["norm_mix_g"][l]), g2=row(sm["norm_ffn_g"][l]),
        wa=_block_diag(sm["lru_w_a"][l]).astype(BF), wx=_block_diag(sm["lru_w_x"][l]).astype(BF),
        ba=row(sm["lru_b_a"][l]), bx=row(sm["lru_b_x"][l]),
        sp=row(jax.nn.softplus(-sm["lru_lambda"][l])),
        cw=sm["conv_w"][l], cb=row(sm["conv_b"][l]),
        wm=(sm["sgu_w_s"][l] * _sgu_mask()).astype(BF),
        bsb=jnp.broadcast_to(sm["sgu_b_s"][l][:, :, None], (SGU_GROUPS, SGU_BLOCK, SGU_BLOCK)),
        lg=row(sm["sgu_ln_g"][l]), lb=row(sm["sgu_ln_b"][l]),
    )


def _layer_fwd_mix(x, big, p, ts):
    h = _norm_call(x, p["g1"], ts)
    proj = _inproj_call(h, big["w_in"], 0, 2 * ts)
    hr, ya_pre = _rnn_fwd_call(proj, p["wa"], p["wx"], p["ba"], p["bx"], p["sp"], p["cw"], p["cb"], ts)
    yb_pre = _sgu_fwd_call(proj, p["wm"], p["bsb"], p["lg"], p["lb"], ts)
    return dict(p=p, x=x, h=h, proj=proj, hr=hr, ya_pre=ya_pre, yb_pre=yb_pre)


def _layer_fwd_out(sv, big, ts):
    x1, ya, yb, merged, h2 = _merge_call(sv["x"], sv["proj"], sv["ya_pre"], sv["yb_pre"], big["w_branch_a"],
                                         big["w_branch_b"], big["w_out"], sv["p"]["g2"], 0, ts)
    x2, pre = _ffn_call(x1, h2, big["w_up"], big["w_down"], 0, ts)
    sv.update(x1=x1, ya=ya, yb=yb, merged=merged, h2=h2, pre=pre)
    return x2


def _layer_bwd_ffn(dx, sv, big, ts):
    p = sv["p"]
    dx1, dpre, dg2 = _ffn_bwd_call(dx, sv["pre"], sv["x1"], p["g2"], big["w_up"], big["w_down"], 0, ts)
    tk = dx.shape[0]
    gb = dict(
        w_down=_wgrad_call(sv["pre"], dx, p["core"], Q_FF, D_MODEL // 2, tk, False, "wgrad_down", a_fn=_relu_sq),
        w_up=_wgrad_call(sv["h2"], dpre, p["core"], D_MODEL, Q_FF, tk, True, "wgrad_up"))
    return dx1, gb, dict(norm_ffn_g=dg2[0])


def _layer_bwd_merge(dx1, sv, big, ts):
    tk = dx1.shape[0]
    core = sv["p"]["core"]
    dya, dyb, dgate, dya_pre, dyb_pre = _merge_bwd_call(
        dx1, sv["proj"], sv["ya"], sv["yb"], big["w_branch_a"], big["w_branch_b"], big["w_out"], 0, ts)
    gb = dict(
        w_out=_wgrad_call(sv["merged"], dx1, core, D_MODEL, D_MODEL // 2, tk, False, "wgrad_out"),
        w_branch_a=_wgrad_call(sv["ya_pre"], dya, core, D_RNN, D_MODEL // 2, tk, False, "wgrad_branch_a"),
        w_branch_b=_wgrad_call(sv["yb_pre"], dyb, core, D_SGU, D_MODEL // 2, tk, False, "wgrad_branch_b"))
    return (dgate, dya_pre, dyb_pre), gb


def _layer_bwd_branches(dx1, merge_out, sv, big, lam, ts):
    p = sv["p"]
    tk = dx1.shape[0]
    dgate, dya_pre, dyb_pre = merge_out
    gb = {}
    duv, dws, dbs, dlg, dlb = _sgu_bwd_call(dyb_pre, sv["proj"], p["wm"], p["bsb"], _sgu_mask(), p["lg"], p["lb"],
                                            ts)
    dxg, dwa, dwx, vec = _rnn_bwd_call(dya_pre, sv["proj"], sv["hr"], p["wa"], p["wx"], p["ba"], p["bx"],
                                       p["sp"], p["cw"], p["cb"], ts // 2)
    dx, dproj, dg1 = _inproj_bwd_call(dxg, duv, dgate, dx1, sv["x"], p["g1"], big["w_in"], 0, ts)
    gb["w_in"] = _wgrad_call(sv["h"], dproj, p["core"], D_MODEL, Q_IN, tk // 2, True, "wgrad_in")
    gs = dict(
        norm_mix_g=dg1[0], conv_w=vec[_ROW_DCW:_ROW_DCW + CONV_WIDTH], conv_b=vec[_ROW_DCB],
        lru_w_a=_block_diag_extract(dwa), lru_w_x=_block_diag_extract(dwx),
        lru_b_a=vec[_ROW_DBA].reshape(RNN_HEADS, RNN_HEAD_DIM), lru_b_x=vec[_ROW_DBX].reshape(RNN_HEADS, RNN_HEAD_DIM),
        lru_lambda=-vec[_ROW_DSP] * jax.nn.sigmoid(-lam),
        sgu_ln_g=dlg[0], sgu_ln_b=dlb[0], sgu_w_s=dws, sgu_b_s=dbs.T)
    return dx, gb, gs


def _local_step(x, target, big, sm, ts):
    saved = []
    core = jnp.zeros((1,), jnp.int32)
    for l in range(DEPTH):
        sv = _layer_fwd_mix(x, big[l], _layer_small(sm, l, core), ts)
        x = _layer_fwd_out(sv, big[l], ts)
        saved.append(sv)
    dx, loss, dgf = _loss_call(x, target, sm["final_norm_g"].reshape(1, -1), ts)
    gb, gs = [None] * DEPTH, [None] * DEPTH
    for l in reversed(range(DEPTH)):
        dx1, gb_ffn, gs_ffn = _layer_bwd_ffn(dx, saved[l], big[l], ts)
        merge_out, gb_merge = _layer_bwd_merge(dx1, saved[l], big[l], ts)
        dx, gb_mix, gs_mix = _layer_bwd_branches(dx1, merge_out, saved[l], big[l], sm["lru_lambda"][l], ts)
        gb[l] = {**gb_ffn, **gb_merge, **gb_mix}
        gs[l] = {**gs_ffn, **gs_mix}
    gs = {k: jnp.stack([g[k] for g in gs]) for k in gs[0]}
    gs["final_norm_g"] = dgf[0]
    return loss, dx, gb, gs


EW_VMEM_BYTES = 24 * 1024 * 1024


def _row_block(rows, cols, bytes_per_elem):
    for br in range(min(rows, EW_VMEM_BYTES // (2 * bytes_per_elem * cols)), 0, -1):
        if rows % br == 0 and br % 16 == 0:
            return br
    return rows


def _ew_call(fn, name, operands, outputs, slabs=1, sel=None, into=None):
    if into is not None and not isinstance(into, (list, tuple)):
        into = [into]
    rows, cols = outputs[0][0].shape[2:]
    br = _row_block(rows, cols, sum(jnp.dtype(a.dtype).itemsize for a, _ in operands + outputs))
    n_in = len(operands)

    def pick(tok, g, s):
        if callable(tok):
            return tok(g, s)
        if tok == "g":
            return g
        if isinstance(tok, tuple):
            return s[tok[1]]
        return tok

    def spec(idx):
        return pl.BlockSpec((None, None, br, cols),
                            lambda g, i, s, idx=idx: (pick(idx[0], g, s), pick(idx[1], g, s), i, 0))

    if sel is None:
        sel = jnp.zeros((1,), jnp.int32)
    in_specs = [spec(idx) for _, idx in operands]
    arrays = [a for a, _ in operands]
    aliases = {}
    for j, buf in enumerate(into or ()):
        in_specs.append(pl.BlockSpec(memory_space=pl.ANY))
        arrays.append(buf)
        aliases[1 + n_in + j] = j

    def body(sel_ref, *refs):
        outs = fn(*[r[...] for r in refs[:n_in]])
        for o_ref, o in zip(refs[len(arrays):], outs):
            o_ref[...] = o.astype(o_ref.dtype)

    return pl.pallas_call(
        body, name=name, out_shape=[s for s, _ in outputs],
        grid_spec=pltpu.PrefetchScalarGridSpec(
            num_scalar_prefetch=1, grid=(slabs, rows // br),
            in_specs=in_specs,
            out_specs=[spec(idx) for _, idx in outputs]),
        input_output_aliases=aliases,
        compiler_params=_params(("parallel", "parallel")),
    )(sel, *arrays)


def _as4(a):
    return a.reshape((1,) * (4 - a.ndim) + a.shape)


def _adamw(w, g, m, v):
    m = ADAM_B1 * m + (1.0 - ADAM_B1) * g
    v = ADAM_B2 * v + (1.0 - ADAM_B2) * jnp.square(g)
    m_hat = m / (1.0 - ADAM_B1 ** ADAM_STEP)
    v_hat = v / (1.0 - ADAM_B2 ** ADAM_STEP)
    delta = -ADAM_LR * (m_hat / (jnp.sqrt(v_hat) + ADAM_EPS) + ADAM_WD * w)
    return delta, m, v


def _small_adamw_call(ws, gs, ms, vs):
    n = len(ws)

    def body(*refs):
        for k in range(n):
            w, g, m, v = (refs[j * n + k][...] for j in range(4))
            outs = _adamw(w, g, m, v)
            for j in range(3):
                refs[(4 + j) * n + k][...] = outs[j]

    shapes = [jax.ShapeDtypeStruct(w.shape, F32) for w in ws]
    outs = pl.pallas_call(
        body, name="adamw_small", out_shape=shapes * 3,
        in_specs=[pl.BlockSpec(memory_space=pltpu.VMEM)] * (4 * n),
        out_specs=[pl.BlockSpec(memory_space=pltpu.VMEM)] * (3 * n),
        compiler_params=_params(),
    )(*ws, *gs, *ms, *vs)
    return outs[:n], outs[n:2 * n], outs[2 * n:]


ANY = pl.BlockSpec(memory_space=pl.ANY)


def _place():
    x, y, c = lax.axis_index("x"), lax.axis_index("y"), lax.axis_index("c")
    chips = [(1 - x, y), (x, 1 - y), (1 - x, 1 - y)]
    return x, y, c, chips


def _remote(src, dst, send_sem, recv_sem, to):
    return pltpu.make_async_remote_copy(src_ref=src, dst_ref=dst, send_sem=send_sem, recv_sem=recv_sem,
                                        device_id=to, device_id_type=MESH)


def _gather_call(bufs):
    n = len(bufs)

    def body(*refs):
        out = refs[n:2 * n]
        send_sems, recv_sems = refs[2 * n:]
        x, y, c, chips = _place()
        me_q = 2 * x + y
        sibling = (x, y, 1 - c)
        first = []
        for w in range(n):
            for j, chip in enumerate(chips):
                mine = out[w].at[c, me_q]
                first.append(_remote(mine, mine, send_sems.at[w * 3 + j], recv_sems.at[w * 3 + j], (*chip, c)))
        for cp in first:
            cp.start()
        passed = []
        for w in range(n):
            for j, (qx, qy) in enumerate(chips):
                landed = out[w].at[c, 2 * qx + qy]
                k = w * 3 + j
                _remote(landed, landed, send_sems.at[k], recv_sems.at[k], (qx, qy, c)).wait_recv()
                cp = _remote(landed, landed, send_sems.at[3 * n + k], recv_sems.at[3 * n + k], sibling)
                cp.start()
                passed.append(cp)
        for w in range(n):
            for j, (qx, qy) in enumerate(chips):
                landed = out[w].at[1 - c, 2 * qx + qy]
                k = 3 * n + w * 3 + j
                _remote(landed, landed, send_sems.at[k], recv_sems.at[k], sibling).wait_recv()
        for cp in first + passed:
            cp.wait_send()

    return pl.pallas_call(
        body, name="gather_weights",
        out_shape=[jax.ShapeDtypeStruct(a.shape, a.dtype) for a in bufs],
        in_specs=[ANY] * n, out_specs=[ANY] * n,
        input_output_aliases={w: w for w in range(n)},
        scratch_shapes=[pltpu.SemaphoreType.DMA((6 * n,)), pltpu.SemaphoreType.DMA((6 * n,))],
        compiler_params=_params(vmem=False, has_side_effects=True),
    )(*bufs)


def _sibling_send_call(items):
    n = len(items)

    def body(*refs):
        src, out = refs[:n], refs[n:2 * n]
        send_sems, recv_sems = refs[2 * n:]
        x, y, c, _ = _place()
        copies = [_remote(src[w], out[w], send_sems.at[w], recv_sems.at[w], (x, y, 1 - c)) for w in range(n)]
        for cp in copies:
            cp.start()
        for cp in copies:
            cp.wait()

    return pl.pallas_call(
        body, name="grads_to_sibling",
        out_shape=[jax.ShapeDtypeStruct(a.shape, a.dtype) for a in items],
        in_specs=[ANY] * n, out_specs=[ANY] * n,
        scratch_shapes=[pltpu.SemaphoreType.DMA((n,)), pltpu.SemaphoreType.DMA((n,))],
        compiler_params=_params(vmem=False, has_side_effects=True),
    )(*items)


def _sibling_inplace_call(name, bufs, slabs, n_pairs):
    n = len(bufs)

    def body(*refs):
        out = refs[n:2 * n]
        send_sems, recv_sems = refs[2 * n:]
        x, y, c, _ = _place()
        sibling = (x, y, 1 - c)
        pairs = [pair for w, ref in enumerate(out) for pair in slabs(ref, c, w)]
        sends = [_remote(s, s, send_sems.at[k], recv_sems.at[k], sibling) for k, (s, _) in enumerate(pairs)]
        for cp in sends:
            cp.start()
        for k, (_, r) in enumerate(pairs):
            _remote(r, r, send_sems.at[k], recv_sems.at[k], sibling).wait_recv()
        for cp in sends:
            cp.wait_send()

    return pl.pallas_call(
        body, name=name,
        out_shape=[jax.ShapeDtypeStruct(a.shape, a.dtype) for a in bufs],
        in_specs=[ANY] * n, out_specs=[ANY] * n,
        input_output_aliases={w: w for w in range(n)},
        scratch_shapes=[pltpu.SemaphoreType.DMA((n_pairs,)), pltpu.SemaphoreType.DMA((n_pairs,))],
        compiler_params=_params(vmem=False, has_side_effects=True),
    )(*bufs)


HBM_SPEC = pl.BlockSpec(memory_space=pltpu.HBM)
SEM_SPEC = pl.BlockSpec(memory_space=pltpu.SEMAPHORE)
DATAFLOW_EFFECT = pltpu.SideEffectType.DATAFLOW_SIDE_EFFECTING


def _exchange_start(name, bufs, copies, n_copies, after):
    n = len(bufs)

    def body(*refs):
        ins, send_sems, recv_sems, token = refs[:n], refs[n + 1], refs[n + 2], refs[-1]
        for k, (src, dst, to) in enumerate(copies(ins)):
            _remote(src, dst, send_sems.at[k], recv_sems.at[k], to).start()
        token[...] = jnp.zeros_like(token)

    outs = pl.pallas_call(
        body, name=name,
        out_shape=(pltpu.SemaphoreType.DMA((n_copies,)), pltpu.SemaphoreType.DMA((n_copies,)),
                   *[pltpu.HBM(b.shape, b.dtype) for b in bufs], jax.ShapeDtypeStruct((SUBLANES, 128), F32)),
        in_specs=[HBM_SPEC] * n + [ANY],
        out_specs=(SEM_SPEC, SEM_SPEC, *[HBM_SPEC] * n, pl.BlockSpec(memory_space=pltpu.VMEM)),
        input_output_aliases={w: w + 2 for w in range(n)},
        compiler_params=pltpu.CompilerParams(has_side_effects=DATAFLOW_EFFECT),
    )(*[pltpu.with_memory_space_constraint(b, pltpu.HBM) for b in bufs], after)
    return outs[0], outs[1], list(outs[2:2 + n]), outs[-1]


def _exchange_wait(name, send_sems, recv_sems, bufs, copies, after):
    n = len(bufs)

    def body(*refs):
        ins, send_sems, recv_sems = refs[:n], refs[n], refs[n + 1]
        for k, (src, dst, to) in enumerate(copies(ins)):
            cp = _remote(src, dst, send_sems.at[k], recv_sems.at[k], to)
            cp.wait_send()
            cp.wait_recv()

    return pl.pallas_call(
        body, name=name,
        out_shape=[pltpu.HBM(b.shape, b.dtype) for b in bufs],
        in_specs=[HBM_SPEC] * n + [SEM_SPEC, SEM_SPEC, ANY],
        out_specs=[HBM_SPEC] * n,
        input_output_aliases={w: w for w in range(n)},
        compiler_params=pltpu.CompilerParams(has_side_effects=DATAFLOW_EFFECT),
    )(*bufs, send_sems, recv_sems, after)


def _gather_copies(refs):
    x, y, c, chips = _place()
    mine = 2 * (2 * x + y) + c
    return [(ref.at[mine], ref.at[mine], (qx, qy, c)) for ref in refs for qx, qy in chips]


def _gather_forward_slabs(ref, c, w):
    x, y, _, chips = _place()
    return [(ref.at[2 * (2 * qx + qy) + c], ref.at[2 * (2 * qx + qy) + 1 - c]) for qx, qy in chips]


def _device_peers():
    x, y, c, _ = _place()
    return 4 * x + 2 * y + c, [(k, (x ^ ((k >> 2) & 1), y ^ ((k >> 1) & 1), c ^ (k & 1))) for k in range(1, 8)]


def _small_scatter_copies(refs):
    me, peers = _device_peers()
    return [(refs[0].at[me ^ k], refs[1].at[me], to) for k, to in peers]


def _small_spread_copies(refs):
    me, peers = _device_peers()
    return [(refs[0].at[me], refs[0].at[me], to) for _, to in peers]


def _owner_copies(refs):
    n = len(refs) // 2
    x, y, c, chips = _place()
    return [(refs[w].at[2 * qx + qy], refs[n + w].at[j], (qx, qy, c))
            for w in range(n) for j, (qx, qy) in enumerate(chips)]


N_DEVICES = 8
SMALL_ROWS = 616


SMALL = ("norm_mix_g", "conv_w", "conv_b", "lru_w_a", "lru_b_a", "lru_w_x", "lru_b_x", "lru_lambda",
         "sgu_ln_g", "sgu_ln_b", "sgu_w_s", "sgu_b_s", "norm_ffn_g", "final_norm_g")
WEIGHTS = ("norm_mix_g", "w_in", "conv_w", "conv_b", "lru_w_a", "lru_b_a", "lru_w_x", "lru_b_x", "lru_lambda",
           "sgu_ln_g", "sgu_ln_b", "sgu_w_s", "sgu_b_s", "w_branch_a", "w_branch_b", "w_out", "norm_ffn_g",
           "w_up", "w_down", "final_norm_g")
PACK_ALIGN = SUBLANES * 128


def _pack_small(gs):
    parts = []
    for k in SMALL:
        flat = gs[k].reshape(-1)
        parts.append(jnp.pad(flat, (0, -flat.size % PACK_ALIGN)))
    flat = jnp.concatenate(parts)
    flat = jnp.pad(flat, (0, N_DEVICES * SMALL_ROWS * 128 - flat.size))
    return flat.reshape(N_DEVICES, SMALL_ROWS, 128)


def _unpack_small(buf, like):
    flat = buf.reshape(-1)
    out, off = {}, 0
    for k in SMALL:
        size = like[k].size
        out[k] = flat[off:off + size].reshape(like[k].shape)
        off += size + (-size % PACK_ALIGN)
    return out


def _as_rows(a):
    return a.reshape(-1, a.shape[-1])


def kernel(x, norm_mix_g, w_in, conv_w, conv_b, lru_w_a, lru_b_a, lru_w_x, lru_b_x, lru_lambda, sgu_ln_g, sgu_ln_b, sgu_w_s, sgu_b_s, w_branch_a, w_branch_b, w_out, norm_ffn_g, w_up, w_down, final_norm_g, loss_target, m_norm_mix_g, m_w_in, m_conv_w, m_conv_b, m_lru_w_a, m_lru_b_a, m_lru_w_x, m_lru_b_x, m_lru_lambda, m_sgu_ln_g, m_sgu_ln_b, m_sgu_w_s, m_sgu_b_s, m_w_branch_a, m_w_branch_b, m_w_out, m_norm_ffn_g, m_w_up, m_w_down, m_final_norm_g, v_norm_mix_g, v_w_in, v_conv_w, v_conv_b, v_lru_w_a, v_lru_b_a, v_lru_w_x, v_lru_b_x, v_lru_lambda, v_sgu_ln_g, v_sgu_ln_b, v_sgu_w_s, v_sgu_b_s, v_w_branch_a, v_w_branch_b, v_w_out, v_norm_ffn_g, v_w_up, v_w_down, v_final_norm_g):
    w = dict(norm_mix_g=norm_mix_g, w_in=w_in, conv_w=conv_w, conv_b=conv_b, lru_w_a=lru_w_a, lru_b_a=lru_b_a,
             lru_w_x=lru_w_x, lru_b_x=lru_b_x, lru_lambda=lru_lambda, sgu_ln_g=sgu_ln_g, sgu_ln_b=sgu_ln_b,
             sgu_w_s=sgu_w_s, sgu_b_s=sgu_b_s, w_branch_a=w_branch_a, w_branch_b=w_branch_b, w_out=w_out,
             norm_ffn_g=norm_ffn_g, w_up=w_up, w_down=w_down, final_norm_g=final_norm_g)
    m = dict(norm_mix_g=m_norm_mix_g, w_in=m_w_in, conv_w=m_conv_w, conv_b=m_conv_b, lru_w_a=m_lru_w_a,
             lru_b_a=m_lru_b_a, lru_w_x=m_lru_w_x, lru_b_x=m_lru_b_x, lru_lambda=m_lru_lambda,
             sgu_ln_g=m_sgu_ln_g, sgu_ln_b=m_sgu_ln_b, sgu_w_s=m_sgu_w_s, sgu_b_s=m_sgu_b_s,
             w_branch_a=m_w_branch_a, w_branch_b=m_w_branch_b, w_out=m_w_out, norm_ffn_g=m_norm_ffn_g,
             w_up=m_w_up, w_down=m_w_down, final_norm_g=m_final_norm_g)
    v = dict(norm_mix_g=v_norm_mix_g, w_in=v_w_in, conv_w=v_conv_w, conv_b=v_conv_b, lru_w_a=v_lru_w_a,
             lru_b_a=v_lru_b_a, lru_w_x=v_lru_w_x, lru_b_x=v_lru_b_x, lru_lambda=v_lru_lambda,
             sgu_ln_g=v_sgu_ln_g, sgu_ln_b=v_sgu_ln_b, sgu_w_s=v_sgu_w_s, sgu_b_s=v_sgu_b_s,
             w_branch_a=v_w_branch_a, w_branch_b=v_w_branch_b, w_out=v_w_out, norm_ffn_g=v_norm_ffn_g,
             w_up=v_w_up, w_down=v_w_down, final_norm_g=v_final_norm_g)
    core = lax.axis_index("c")
    chip = 2 * lax.axis_index("x") + lax.axis_index("y")
    sel = jnp.stack([core, 1 - core, chip, 2 * chip + core]).astype(jnp.int32)
    this_core, other_core, this_chip = ("sel", 0), ("sel", 1), ("sel", 2)
    sds = jax.ShapeDtypeStruct

    ts = TOKEN_TILE
    halves = {k: (w[k].shape[1] // 2, w[k].shape[2]) for k in BIG}

    def half_view(k, a):
        return a.reshape((2 * N_QUARTERS,) + halves[k])

    def full_view(k, a):
        r2, cols = halves[k]
        if k in ("w_in", "w_up"):
            return a.reshape(1, N_QUARTERS, 2 * r2, cols)
        return a.reshape(1, 2 * N_QUARTERS * r2, cols)

    layer_bufs = [[], []]
    for k in BIG:
        _, r, cols = w[k].shape
        w4 = w[k].reshape(DEPTH, 1, r, cols)
        outs = _ew_call(lambda a, b: (a, b), "cast_weights", [(w4, (0, 0)), (w4, (1, 0))],
                        [(sds((1, N_QUARTERS, r, cols), BF), (0, this_chip))] * DEPTH, 1, sel)
        for l in range(DEPTH):
            layer_bufs[l].append(half_view(k, outs[l]))
    conv_buf = lax.dynamic_update_slice_in_dim(
        jnp.zeros((DEPTH, N_QUARTERS) + conv_w.shape[1:], F32), conv_w[:, None], chip, axis=1)
    sm = {k: w[k] for k in SMALL}
    sm["conv_w"] = _gather_call([conv_buf])[0].transpose(0, 2, 1, 3).reshape(DEPTH, CONV_WIDTH, D_RNN)

    def gather_start(tag, l, keys, after):
        bufs = [layer_bufs[l][BIG.index(k)] for k in keys]
        return _exchange_start(f"gather_start_{tag}", bufs, _gather_copies, 3 * len(keys), after)

    def gather_finish(tag, keys, started, after):
        send_sems, recv_sems, thru, _ = started
        landed = _exchange_wait(f"gather_wait_{tag}", send_sems, recv_sems, thru, _gather_copies, after)
        landed = _sibling_inplace_call("gather_forward", landed, _gather_forward_slabs, 3 * len(keys))
        return {k: full_view(k, a) for k, a in zip(keys, landed)}

    first, rest = ("w_in",), tuple(k for k in BIG if k != "w_in")
    started_a = gather_start("0a", 0, first, sm["conv_w"])
    started_b = gather_start("0b", 0, rest, started_a[3])
    started_1 = gather_start("1", 1, BIG, started_b[3])
    big0 = gather_finish("0a", first, started_a, started_1[3])
    sv0 = _layer_fwd_mix(x[0], big0, _layer_small(sm, 0, sel[0:1]), ts)
    big0.update(gather_finish("0b", rest, started_b, sv0["yb_pre"]))
    x_mid = _layer_fwd_out(sv0, big0, ts)
    big1 = gather_finish("1", BIG, started_1, x_mid)
    sv1 = _layer_fwd_mix(x_mid, big1, _layer_small(sm, 1, sel[0:1]), ts)
    x_out = _layer_fwd_out(sv1, big1, ts)
    dx, loss, dgf = _loss_call(x_out, loss_target[0], final_norm_g.reshape(1, -1), ts)

    def reduce_start(tag, gb, after):
        keys = tuple(gb)
        from_sibling = _sibling_send_call([gb[k][1] for k in keys])
        sums = [
            _ew_call(lambda a, b: (a + b.astype(F32),), "pair_sum", [(gb[k][0][None], (0, "g")), (r[None], (0, "g"))],
                     [(sds((1,) + r.shape, BF), (0, "g"))], N_QUARTERS)[0][0]
            for k, r in zip(keys, from_sibling)]
        zones = [lax.empty((3,) + a.shape[1:], BF) for a in sums]
        started = _exchange_start(f"reduce_start_{tag}", sums + zones, _owner_copies, 3 * len(keys), after)
        return keys, started

    def reduce_finish(tag, l, keys_started, after, reduced):
        keys, (send_sems, recv_sems, thru, _) = keys_started
        done = _exchange_wait(f"reduce_wait_{tag}", send_sems, recv_sems, thru, _owner_copies, after)
        sums, zones = done[:len(keys)], done[len(keys):]
        for i, k in enumerate(keys):
            r2, cols = halves[k]
            reduced[k] = _ew_call(
                lambda a, b, c, d: (((a.astype(F32) + b.astype(F32)) + c.astype(F32)) + d.astype(F32),),
                "quarter_sum", [(sums[i][None], (0, this_chip))] + [(zones[i][None], (0, j)) for j in range(3)],
                [(sds((DEPTH, 2, r2, cols), F32), (l, this_core))], 1, sel, into=reduced.get(k))[0]

    def behind(params, key, started):
        return dict(params, **{key: params[key] + started[1][3][0, 0]})

    dx1, gb_ffn, gs1 = _layer_bwd_ffn(dx, sv1, big1, ts)
    merge_out, gb_merge = _layer_bwd_merge(dx1, sv1, big1, ts)
    dx_mid, gb_in, gs1_mix = _layer_bwd_branches(dx1, merge_out, sv1, big1, lru_lambda[1], ts)
    exchange_1 = reduce_start("1", {**gb_ffn, **gb_merge, **gb_in}, dx_mid)
    sv0["p"] = behind(sv0["p"], "g2", exchange_1)
    dx1, gb_ffn, gs0 = _layer_bwd_ffn(dx_mid, sv0, big0, ts)
    exchange_0a = reduce_start("0a", gb_ffn, exchange_1[1][3])
    merge_out, gb_merge = _layer_bwd_merge(dx1, sv0, big0, ts)
    exchange_0b = reduce_start("0b", gb_merge, exchange_0a[1][3])
    sv0["p"] = behind(sv0["p"], "lg", exchange_0b)
    grad_x, gb_in, gs0_mix = _layer_bwd_branches(dx1, merge_out, sv0, big0, lru_lambda[0], ts)
    exchange_0c = reduce_start("0c", gb_in, exchange_0b[1][3])
    layer_gs = [{**gs0, **gs0_mix}, {**gs1, **gs1_mix}]
    gs = {k: jnp.stack([g[k] for g in layer_gs]) for k in layer_gs[0]}
    gs["final_norm_g"] = dgf[0]

    me = ("sel", 3)
    piece = (1, N_DEVICES, SMALL_ROWS, 128)
    packed = _pack_small(gs).reshape(piece)
    scatter = _exchange_start("small_scatter_start", [packed[0], lax.empty(piece[1:], F32)], _small_scatter_copies,
                              N_DEVICES - 1, exchange_0c[1][3])
    reduced = {}
    reduce_finish("1", 1, exchange_1, scatter[3], reduced)
    reduce_finish("0a", 0, exchange_0a, reduced["w_in"], reduced)
    reduce_finish("0b", 0, exchange_0b, reduced["w_down"], reduced)

    def swap_slabs(ref, c, i):
        layers = (1,) if BIG[i] == "w_in" else range(DEPTH)
        return [(ref.at[l, c], ref.at[l, 1 - c]) for l in layers]

    swapped = dict(zip(BIG, _sibling_inplace_call("grads_swap_halves", [reduced[k] for k in BIG], swap_slabs,
                                                  DEPTH * len(BIG) - 1)))

    def adamw_layers(k, grad, layer, into):
        if layer is None:
            views = [_as4(_as_rows(a)) for a in (w[k], grad, m[k], v[k])]
            idx = (0, 0)
        else:
            views = [a.reshape((1,) + w[k].shape) for a in (w[k], grad, m[k], v[k])]
            idx = (0, layer)
        return _ew_call(lambda w_, g_, m_, v_: (g_,) + _adamw(w_, g_, m_, v_), "adamw_big",
                        [(a, idx) for a in views], [(sds(views[0].shape, F32), idx)] * 4, into=into)

    def after_all(arrays):
        return jnp.stack([a.reshape(-1)[0] for a in arrays])

    updated = {k: adamw_layers(k, swapped[k], 1 if k == "w_in" else None, None) for k in BIG}
    scattered = _exchange_wait("small_scatter_wait", scatter[0], scatter[1], scatter[2], _small_scatter_copies,
                               after_all([updated[k][1] for k in BIG]))
    summed = _ew_call(
        lambda *parts: (functools.reduce(lambda a, b: a + b, parts),), "small_sum",
        [(scattered[0][None], (0, me))]
        + [(scattered[1][None], (0, lambda g, s, k=k: s[3] ^ k)) for k in range(1, N_DEVICES)],
        [(sds(piece, F32), (0, me))], 1, sel)[0]
    spread = _exchange_start("small_spread_start", [summed[0]], _small_spread_copies, N_DEVICES - 1, summed)
    reduced["w_in"] = swapped["w_in"]
    reduce_finish("0c", 0, exchange_0c, spread[3], reduced)
    last = _sibling_inplace_call("grads_swap_last", [reduced["w_in"]],
                                 lambda ref, c, i: [(ref.at[0, c], ref.at[0, 1 - c])], 1)[0]
    updated["w_in"] = adamw_layers("w_in", last, 0, updated["w_in"])
    grads_big, delta, new_m, new_v = ({k: updated[k][j].reshape(w[k].shape) for k in BIG} for j in range(4))
    gathered_small = _exchange_wait("small_spread_wait", spread[0], spread[1], spread[2], _small_spread_copies,
                                    updated["w_in"][1])[0]

    like = {k: jax.ShapeDtypeStruct(sm[k].shape, F32) for k in SMALL}
    grads_small = _unpack_small(gathered_small, like)
    conv_q = grads_small["conv_w"].reshape(DEPTH, CONV_WIDTH, N_QUARTERS, D_RNN // N_QUARTERS)
    grads_small["conv_w"] = lax.dynamic_index_in_dim(conv_q, chip, axis=2, keepdims=False)
    outs = _small_adamw_call(*[[_as_rows(d[k]) for k in SMALL] for d in (w, grads_small, m, v)])
    for d, o in zip((delta, new_m, new_v), outs):
        for k, a in zip(SMALL, o):
            d[k] = a.reshape(w[k].shape)

    grads = {**grads_big, **grads_small}
    total = lax.psum(loss[0, 0], ("x", "y", "c"))
    return (total, grad_x[None], *[grads[k] for k in WEIGHTS], *[delta[k] for k in WEIGHTS],
            *[new_m[k] for k in WEIGHTS], *[new_v[k] for k in WEIGHTS])
```

```python
import functools
import math

import jax
import jax.numpy as jnp
from jax import lax
from jax.experimental import pallas as pl
from jax.experimental.pallas import tpu as pltpu

F32 = jnp.float32
BF = jnp.bfloat16

DEPTH = 2
D_MODEL = 1024
D_RNN = 1280
D_SGU = 1024
D_FF = 4096
D_IN = 2 * D_RNN + 2 * D_SGU + 2 * D_MODEL
N_QUARTERS = 4
Q_IN = D_IN // N_QUARTERS
Q_FF = D_FF // N_QUARTERS
RNN_HEADS = 20
RNN_HEAD_DIM = 64
LRU_GROUP = 256
N_LRU_GROUPS = D_RNN // LRU_GROUP
HEADS_PER_GROUP = LRU_GROUP // RNN_HEAD_DIM
CONV_WIDTH = 4
LRU_C = 8.0
SGU_GROUPS = 8
SGU_BLOCK = 128
CHUNK = 64
EPS = 1e-6

ADAM_LR = 0.001
ADAM_B1 = 0.9
ADAM_B2 = 0.999
ADAM_EPS = 1e-08
ADAM_WD = 0.01
ADAM_STEP = 10

SUBLANES = 8
TOKEN_TILE = 512
VMEM_LIMIT_BYTES = 56 * 1024 * 1024

MESH = pl.DeviceIdType.MESH


def _params(semantics=None, vmem=True, **kw):
    return pltpu.CompilerParams(
        dimension_semantics=semantics,
        vmem_limit_bytes=VMEM_LIMIT_BYTES if vmem else None,
        **kw,
    )


def _dot(a, b):
    return jnp.dot(a, b, preferred_element_type=F32)


def _dot_nt(a, b):
    return lax.dot_general(a, b, (((1,), (1,)), ((), ())), preferred_element_type=F32)


def _dot_tn(a, b):
    return lax.dot_general(a, b, (((0,), (0,)), ((), ())), preferred_element_type=F32)


_GELU_C = math.sqrt(2.0 / math.pi)
_GELU_A = 0.044715


def _gelu(x):
    return 0.5 * x * (1.0 + jnp.tanh(_GELU_C * (x + _GELU_A * x * x * x)))


def _gelu_and_grad(x):
    x2 = x * x
    t = jnp.tanh(_GELU_C * (x + _GELU_A * x2 * x))
    du = _GELU_C * (1.0 + 3.0 * _GELU_A * x2)
    return 0.5 * x * (1.0 + t), 0.5 * (1.0 + t) + 0.5 * x * (1.0 - t * t) * du


def _rms_stats(x):
    return lax.rsqrt(jnp.mean(x * x, axis=-1, keepdims=True) + EPS)


def _rms_bwd(dy, x, g):
    rs = _rms_stats(x)
    n = x * rs
    dn = dy * g
    dx = rs * (dn - n * jnp.mean(dn * n, axis=-1, keepdims=True))
    return dx, dy * n


def _row_sum(x):
    return jnp.sum(x, axis=0, keepdims=True)


def _tile_spec(ts, width, col=0):
    return pl.BlockSpec((ts, width), lambda i, col=col: (i, col))


def _full_spec(shape):
    zeros = (0,) * len(shape)
    return pl.BlockSpec(shape, lambda *_: zeros)


def _layer_spec(w, layer):
    zeros = (0,) * (w.ndim - 1)
    return pl.BlockSpec((None,) + tuple(w.shape[1:]), lambda *_: (layer,) + zeros)


def _norm_call(x, g, ts):
    s = x.shape[0]

    def body(x_ref, g_ref, h_ref):
        xv = x_ref[...]
        h_ref[...] = (xv * _rms_stats(xv) * g_ref[...]).astype(BF)

    return pl.pallas_call(
        body, name="norm_fwd", grid=(s // ts,),
        in_specs=[_tile_spec(ts, D_MODEL), _full_spec((1, D_MODEL))],
        out_specs=_tile_spec(ts, D_MODEL),
        out_shape=jax.ShapeDtypeStruct((s, D_MODEL), BF),
        compiler_params=_params(("parallel",)),
    )(x, g)


def _inproj_call(h, w_in, layer, ts):
    s = h.shape[0]

    def body(h_ref, w_ref, o_ref):
        o_ref[...] = _dot(h_ref[...], w_ref[...]).astype(BF)

    return pl.pallas_call(
        body, name="inproj_fwd", grid=(N_QUARTERS, s // ts),
        in_specs=[
            pl.BlockSpec((ts, D_MODEL), lambda q, i: (i, 0)),
            pl.BlockSpec((None, None, D_MODEL, Q_IN), lambda q, i: (layer, q, 0, 0)),
        ],
        out_specs=pl.BlockSpec((ts, Q_IN), lambda q, i: (i, q)),
        out_shape=jax.ShapeDtypeStruct((s, D_IN), BF),
        compiler_params=_params(("parallel", "parallel")),
    )(h, w_in)


def _shift_down(x, tail, s):
    xr = pltpu.roll(x, s, 0)
    tr = pltpu.roll(tail, s, 0)
    row = lax.broadcasted_iota(jnp.int32, tail.shape, 0)
    top = jnp.where(row < s, tr, xr[0:SUBLANES])
    return jnp.concatenate([top, xr[SUBLANES:]], axis=0)


def _shift_up(x, head, s):
    t = x.shape[0]
    xr = pltpu.roll(x, t - s, 0)
    hr = pltpu.roll(head, SUBLANES - s, 0)
    row = lax.broadcasted_iota(jnp.int32, head.shape, 0)
    bottom = jnp.where(row >= SUBLANES - s, hr, xr[t - SUBLANES:])
    return jnp.concatenate([xr[: t - SUBLANES], bottom], axis=0)


def _conv_fwd(x, tail, cw_ref, cb_ref):
    shifted = [x] + [_shift_down(x, tail, s) for s in range(1, CONV_WIDTH)]
    out = cb_ref[...] + cw_ref[CONV_WIDTH - 1:CONV_WIDTH, :] * x
    for s in range(1, CONV_WIDTH):
        k = CONV_WIDTH - 1 - s
        out = out + cw_ref[k:k + 1, :] * shifted[s]
    return out, shifted


def _group_dot(x_bf, w_ref, dot):
    cols = [dot(x_bf[:, g * LRU_GROUP:(g + 1) * LRU_GROUP], w_ref[g]) for g in range(N_LRU_GROUPS)]
    return jnp.concatenate(cols, axis=1)


def _lru_gates(xr, wa_ref, wx_ref, ba_ref, bx_ref, sp_ref):
    xb = xr.astype(BF)
    r = jax.nn.sigmoid(_group_dot(xb, wa_ref, _dot) + ba_ref[...])
    i = jax.nn.sigmoid(_group_dot(xb, wx_ref, _dot) + bx_ref[...])
    log_a = (-LRU_C * r) * sp_ref[...]
    a = jnp.exp(log_a)
    nrm2 = -jnp.tanh(log_a) * (a * a + 1.0)
    inv_nrm = lax.rsqrt(jnp.maximum(nrm2, 1e-36))
    return r, i, a, nrm2 * inv_nrm, inv_nrm


def _linear_scan(a, b, carry, al_ref, bl_ref, h_ref, reverse):
    t, c = a.shape
    rowm = lax.broadcasted_iota(jnp.int32, (t, c), 0) & (SUBLANES - 1)
    for d in (1, 2, 4):
        if reverse:
            keep, sh = rowm < SUBLANES - d, t - d
        else:
            keep, sh = rowm >= d, d
        a_sh = jnp.where(keep, pltpu.roll(a, sh, 0), 1.0)
        b_sh = jnp.where(keep, pltpu.roll(b, sh, 0), 0.0)
        b = a * b_sh + b
        a = a * a_sh
    al_ref[...] = a
    bl_ref[...] = b
    groups = t // SUBLANES

    def step(j, state):
        jj = groups - 1 - j if reverse else j
        off = pl.multiple_of(jj * SUBLANES, SUBLANES)
        rows = bl_ref[pl.ds(off, SUBLANES), :] + al_ref[pl.ds(off, SUBLANES), :] * state
        h_ref[pl.ds(off, SUBLANES), :] = rows
        last = rows[0:1, :] if reverse else rows[SUBLANES - 1:SUBLANES, :]
        return jnp.broadcast_to(last, (SUBLANES, c))

    out = lax.fori_loop(0, groups, step, jnp.broadcast_to(carry, (SUBLANES, c)))
    return out[0:1, :]


def _rnn_fwd_call(proj, wa, wx, ba, bx, sp, cw, cb, ts):
    s = proj.shape[0]

    def body(xg_ref, wa_ref, wx_ref, ba_ref, bx_ref, sp_ref, cw_ref, cb_ref, hr_ref, ya_ref,
             tail_sc, carry_sc, al_sc, bl_sc, h_sc):
        @pl.when(pl.program_id(0) == 0)
        def _():
            tail_sc[...] = jnp.zeros_like(tail_sc)
            carry_sc[...] = jnp.zeros_like(carry_sc)

        x = xg_ref[:, :D_RNN].astype(F32)
        g = xg_ref[:, D_RNN:].astype(F32)
        xr, _ = _conv_fwd(x, tail_sc[...], cw_ref, cb_ref)
        tail_sc[...] = x[ts - SUBLANES:, :]
        _, i, a, nrm, _ = _lru_gates(xr, wa_ref, wx_ref, ba_ref, bx_ref, sp_ref)
        carry_sc[...] = _linear_scan(a, nrm * (i * xr), carry_sc[...], al_sc, bl_sc, h_sc, False)
        h = h_sc[...]
        hr_ref[...] = h.astype(BF)
        ya_ref[...] = (h * _gelu(g)).astype(BF)

    gw = (N_LRU_GROUPS, LRU_GROUP, LRU_GROUP)
    return pl.pallas_call(
        body, name="rnn_fwd", grid=(s // ts,),
        in_specs=[_tile_spec(ts, 2 * D_RNN), _full_spec(gw), _full_spec(gw),
                  _full_spec((1, D_RNN)), _full_spec((1, D_RNN)), _full_spec((1, D_RNN)),
                  _full_spec((CONV_WIDTH, D_RNN)), _full_spec((1, D_RNN))],
        out_specs=[_tile_spec(ts, D_RNN), _tile_spec(ts, D_RNN)],
        out_shape=[jax.ShapeDtypeStruct((s, D_RNN), BF), jax.ShapeDtypeStruct((s, D_RNN), BF)],
        scratch_shapes=[pltpu.VMEM((SUBLANES, D_RNN), F32), pltpu.VMEM((1, D_RNN), F32),
                        pltpu.VMEM((ts, D_RNN), F32), pltpu.VMEM((ts, D_RNN), F32),
                        pltpu.VMEM((ts, D_RNN), F32)],
        compiler_params=_params(("arbitrary",)),
    )(proj, wa, wx, ba, bx, sp, cw, cb)


def _layernorm_fwd(x):
    mu = jnp.mean(x, axis=-1, keepdims=True)
    xc = x - mu
    rstd = lax.rsqrt(jnp.mean(xc * xc, axis=-1, keepdims=True) + EPS)
    return xc * rstd, rstd


def _sgu_mix(vn_bf, wm_ref, bsb_ref, ts):
    rows = []
    for blk in range(ts // SGU_BLOCK):
        r0 = blk * SGU_BLOCK
        cols = [
            _dot(wm_ref[g], vn_bf[r0:r0 + SGU_BLOCK, g * SGU_BLOCK:(g + 1) * SGU_BLOCK]) + bsb_ref[g]
            for g in range(SGU_GROUPS)
        ]
        rows.append(jnp.concatenate(cols, axis=1))
    return jnp.concatenate(rows, axis=0)


def _sgu_fwd_call(proj, wm, bsb, lg, lb, ts):
    s = proj.shape[0]

    def body(uv_ref, wm_ref, bsb_ref, lg_ref, lb_ref, yb_ref):
        gu = _gelu(uv_ref[:, :D_SGU].astype(F32))
        gv = _gelu(uv_ref[:, D_SGU:2 * D_SGU].astype(F32))
        nh, _ = _layernorm_fwd(gv)
        vn = (nh * lg_ref[...] + lb_ref[...]).astype(BF)
        yb_ref[...] = (gu * _sgu_mix(vn, wm_ref, bsb_ref, ts)).astype(BF)

    sw = (SGU_GROUPS, SGU_BLOCK, SGU_BLOCK)
    return pl.pallas_call(
        body, name="sgu_fwd", grid=(s // ts,),
        in_specs=[_tile_spec(ts, 2 * D_RNN, 1), _full_spec(sw), _full_spec(sw),
                  _full_spec((1, D_SGU)), _full_spec((1, D_SGU))],
        out_specs=_tile_spec(ts, D_SGU),
        out_shape=jax.ShapeDtypeStruct((s, D_SGU), BF),
        compiler_params=_params(("parallel",)),
    )(proj, wm, bsb, lg, lb)


_GATE_COL0 = (2 * D_RNN + 2 * D_SGU) // 512


def _gate_specs(ts):
    return [_tile_spec(ts, 512, _GATE_COL0 + j) for j in range(4)]


def _merge_call(x, proj, ya_pre, yb_pre, w_ba, w_bb, w_out, g2, layer, ts):
    s = x.shape[0]

    def body(x_ref, ga0, ga1, gb0, gb1, ya_ref, yb_ref, wa_ref, wb_ref, wo_ref, g2_ref,
             x1_ref, yao_ref, ybo_ref, mg_ref, h2_ref):
        ya = _dot(ya_ref[...], wa_ref[...])
        yb = _dot(yb_ref[...], wb_ref[...])
        sa = jax.nn.sigmoid(jnp.concatenate([ga0[...], ga1[...]], axis=1).astype(F32))
        sb = jax.nn.sigmoid(jnp.concatenate([gb0[...], gb1[...]], axis=1).astype(F32))
        merged = (sa * ya + sb * yb).astype(BF)
        x1 = x_ref[...] + _dot(merged, wo_ref[...])
        x1_ref[...] = x1
        yao_ref[...] = ya.astype(BF)
        ybo_ref[...] = yb.astype(BF)
        mg_ref[...] = merged
        h2_ref[...] = (x1 * _rms_stats(x1) * g2_ref[...]).astype(BF)

    act = jax.ShapeDtypeStruct((s, D_MODEL), BF)
    return pl.pallas_call(
        body, name="merge_fwd", grid=(s // ts,),
        in_specs=[_tile_spec(ts, D_MODEL)] + _gate_specs(ts) + [
            _tile_spec(ts, D_RNN), _tile_spec(ts, D_SGU),
            _layer_spec(w_ba, layer), _layer_spec(w_bb, layer), _layer_spec(w_out, layer),
            _full_spec((1, D_MODEL))],
        out_specs=[_tile_spec(ts, D_MODEL)] * 5,
        out_shape=[jax.ShapeDtypeStruct((s, D_MODEL), F32), act, act, act, act],
        compiler_params=_params(("parallel",)),
    )(x, proj, proj, proj, proj, ya_pre, yb_pre, w_ba, w_bb, w_out, g2)


def _ffn_call(x1, h2, w_up, w_down, layer, ts):
    s = x1.shape[0]

    def body(x1_ref, h2_ref, wu_ref, wd_ref, x2_ref, p_ref):
        h2v = h2_ref[...]
        acc = x1_ref[...]
        for q in range(N_QUARTERS):
            p = _dot(h2v, wu_ref[q])
            p_ref[:, q * Q_FF:(q + 1) * Q_FF] = p.astype(BF)
            f = jnp.square(jnp.maximum(p, 0.0)).astype(BF)
            acc = acc + _dot(f, wd_ref[q * Q_FF:(q + 1) * Q_FF, :])
        x2_ref[...] = acc

    return pl.pallas_call(
        body, name="ffn_fwd", grid=(s // ts,),
        in_specs=[_tile_spec(ts, D_MODEL), _tile_spec(ts, D_MODEL),
                  pl.BlockSpec((None, N_QUARTERS, D_MODEL, Q_FF), lambda i: (layer, 0, 0, 0)),
                  pl.BlockSpec((None, D_FF, D_MODEL), lambda i: (layer, 0, 0))],
        out_specs=[_tile_spec(ts, D_MODEL), _tile_spec(ts, D_FF)],
        out_shape=[jax.ShapeDtypeStruct((s, D_MODEL), F32), jax.ShapeDtypeStruct((s, D_FF), BF)],
        compiler_params=_params(("parallel",)),
    )(x1, h2, w_up, w_down)


def _loss_call(x, target, gf, ts):
    s = x.shape[0]

    def body(x_ref, t_ref, g_ref, dx_ref, loss_ref, dg_ref):
        @pl.when(pl.program_id(0) == 0)
        def _():
            loss_ref[...] = jnp.zeros_like(loss_ref)
            dg_ref[...] = jnp.zeros_like(dg_ref)

        xv = x_ref[...]
        gv = g_ref[...]
        err = xv * _rms_stats(xv) * gv - t_ref[...]
        part = 0.5 * jnp.sum(jnp.mean(err * err, axis=-1, keepdims=True), axis=0, keepdims=True)
        loss_ref[...] += jnp.broadcast_to(part, loss_ref.shape)
        dx, dg = _rms_bwd(err * (1.0 / D_MODEL), xv, gv)
        dx_ref[...] = dx
        dg_ref[...] += _row_sum(dg)

    return pl.pallas_call(
        body, name="loss_head", grid=(s // ts,),
        in_specs=[_tile_spec(ts, D_MODEL), _tile_spec(ts, D_MODEL), _full_spec((1, D_MODEL))],
        out_specs=[_tile_spec(ts, D_MODEL), _full_spec((1, 128)), _full_spec((1, D_MODEL))],
        out_shape=[jax.ShapeDtypeStruct((s, D_MODEL), F32), jax.ShapeDtypeStruct((1, 128), F32),
                   jax.ShapeDtypeStruct((1, D_MODEL), F32)],
        compiler_params=_params(("arbitrary",)),
    )(x, target, gf)


def _ffn_bwd_call(dx2, p, x1, g2, w_up, w_down, layer, ts):
    s = dx2.shape[0]

    def body(dx2_ref, p_ref, x1_ref, g2_ref, wu_ref, wd_ref, dx1_ref, dp_ref, dg_ref):
        @pl.when(pl.program_id(0) == 0)
        def _():
            dg_ref[...] = jnp.zeros_like(dg_ref)

        dx2v = dx2_ref[...]
        dyb = dx2v.astype(BF)
        dh2 = jnp.zeros((ts, D_MODEL), F32)
        for q in range(N_QUARTERS):
            cols = slice(q * Q_FF, (q + 1) * Q_FF)
            df = _dot_nt(dyb, wd_ref[cols, :])
            dp = (df * (2.0 * jnp.maximum(p_ref[:, cols].astype(F32), 0.0))).astype(BF)
            dp_ref[:, cols] = dp
            dh2 = dh2 + _dot_nt(dp, wu_ref[q])
        dx, dg = _rms_bwd(dh2, x1_ref[...], g2_ref[...])
        dx1_ref[...] = dx2v + dx
        dg_ref[...] += _row_sum(dg)

    return pl.pallas_call(
        body, name="ffn_bwd", grid=(s // ts,),
        in_specs=[_tile_spec(ts, D_MODEL), _tile_spec(ts, D_FF), _tile_spec(ts, D_MODEL),
                  _full_spec((1, D_MODEL)),
                  pl.BlockSpec((None, N_QUARTERS, D_MODEL, Q_FF), lambda i: (layer, 0, 0, 0)),
                  pl.BlockSpec((None, D_FF, D_MODEL), lambda i: (layer, 0, 0))],
        out_specs=[_tile_spec(ts, D_MODEL), _tile_spec(ts, D_FF), _full_spec((1, D_MODEL))],
        out_shape=[jax.ShapeDtypeStruct((s, D_MODEL), F32), jax.ShapeDtypeStruct((s, D_FF), BF),
                   jax.ShapeDtypeStruct((1, D_MODEL), F32)],
        compiler_params=_params(("arbitrary",)),
    )(dx2, p, x1, g2, w_up, w_down)


def _merge_bwd_call(dx1, proj, ya, yb, w_ba, w_bb, w_out, layer, ts):
    s = dx1.shape[0]

    def body(dx1_ref, ga0, ga1, gb0, gb1, ya_ref, yb_ref, wa_ref, wb_ref, wo_ref,
             dya_ref, dyb_ref, dgate_ref, dyap_ref, dybp_ref):
        dm = _dot_nt(dx1_ref[...].astype(BF), wo_ref[...])
        sa = jax.nn.sigmoid(jnp.concatenate([ga0[...], ga1[...]], axis=1).astype(F32))
        sb = jax.nn.sigmoid(jnp.concatenate([gb0[...], gb1[...]], axis=1).astype(F32))
        dya = (dm * sa).astype(BF)
        dyb = (dm * sb).astype(BF)
        dya_ref[...] = dya
        dyb_ref[...] = dyb
        dgate_ref[:, :D_MODEL] = (dm * ya_ref[...].astype(F32) * sa * (1.0 - sa)).astype(BF)
        dgate_ref[:, D_MODEL:] = (dm * yb_ref[...].astype(F32) * sb * (1.0 - sb)).astype(BF)
        dyap_ref[...] = _dot_nt(dya, wa_ref[...]).astype(BF)
        dybp_ref[...] = _dot_nt(dyb, wb_ref[...]).astype(BF)

    act = jax.ShapeDtypeStruct((s, D_MODEL), BF)
    return pl.pallas_call(
        body, name="merge_bwd", grid=(s // ts,),
        in_specs=[_tile_spec(ts, D_MODEL)] + _gate_specs(ts) + [
            _tile_spec(ts, D_MODEL), _tile_spec(ts, D_MODEL),
            _layer_spec(w_ba, layer), _layer_spec(w_bb, layer), _layer_spec(w_out, layer)],
        out_specs=[_tile_spec(ts, D_MODEL), _tile_spec(ts, D_MODEL), _tile_spec(ts, 2 * D_MODEL),
                   _tile_spec(ts, D_RNN), _tile_spec(ts, D_SGU)],
        out_shape=[act, act, jax.ShapeDtypeStruct((s, 2 * D_MODEL), BF),
                   jax.ShapeDtypeStruct((s, D_RNN), BF), jax.ShapeDtypeStruct((s, D_SGU), BF)],
        compiler_params=_params(("parallel",)),
    )(dx1, proj, proj, proj, proj, ya, yb, w_ba, w_bb, w_out)


def _sgu_bwd_call(dyb_pre, proj, wm, bsb, mask, lg, lb, ts):
    s = proj.shape[0]

    def body(dy_ref, uv_ref, wm_ref, bsb_ref, mask_ref, lg_ref, lb_ref,
             duv_ref, dws_ref, dbs_ref, dlg_ref, dlb_ref, dm_sc):
        step = pl.program_id(0)

        @pl.when(step == 0)
        def _():
            dws_ref[...] = jnp.zeros_like(dws_ref)
            dlg_ref[...] = jnp.zeros_like(dlg_ref)
            dlb_ref[...] = jnp.zeros_like(dlb_ref)
            dm_sc[...] = jnp.zeros_like(dm_sc)

        gu, dgu_du = _gelu_and_grad(uv_ref[:, :D_SGU].astype(F32))
        gv, dgv_dv = _gelu_and_grad(uv_ref[:, D_SGU:2 * D_SGU].astype(F32))
        nh, rstd = _layernorm_fwd(gv)
        lgv = lg_ref[...]
        vn = (nh * lgv + lb_ref[...]).astype(BF)
        dy = dy_ref[...].astype(F32)
        du = dy * _sgu_mix(vn, wm_ref, bsb_ref, ts) * dgu_du
        dmix = dy * gu
        dmix_bf = dmix.astype(BF)
        dm_acc = dm_sc[...]
        rows = []
        for blk in range(ts // SGU_BLOCK):
            r0 = blk * SGU_BLOCK
            dm_acc = dm_acc + dmix[r0:r0 + SGU_BLOCK, :]
            cols = []
            for g in range(SGU_GROUPS):
                c0 = g * SGU_BLOCK
                dmg = dmix_bf[r0:r0 + SGU_BLOCK, c0:c0 + SGU_BLOCK]
                cols.append(_dot_tn(wm_ref[g], dmg))
                dws_ref[g] += mask_ref[...] * _dot_nt(dmg, vn[r0:r0 + SGU_BLOCK, c0:c0 + SGU_BLOCK])
            rows.append(jnp.concatenate(cols, axis=1))
        dm_sc[...] = dm_acc
        dvn = jnp.concatenate(rows, axis=0)
        dlg_ref[...] += _row_sum(dvn * nh)
        dlb_ref[...] += _row_sum(dvn)
        dnh = dvn * lgv
        dgv = rstd * (dnh - jnp.mean(dnh, axis=-1, keepdims=True)
                      - nh * jnp.mean(dnh * nh, axis=-1, keepdims=True))
        duv_ref[:, :D_SGU] = du.astype(BF)
        duv_ref[:, D_SGU:] = (dgv * dgv_dv).astype(BF)

        @pl.when(step == pl.num_programs(0) - 1)
        def _():
            for g in range(SGU_GROUPS):
                dbs_ref[:, g:g + 1] = jnp.sum(
                    dm_acc[:, g * SGU_BLOCK:(g + 1) * SGU_BLOCK], axis=1, keepdims=True)

    sw = (SGU_GROUPS, SGU_BLOCK, SGU_BLOCK)
    return pl.pallas_call(
        body, name="sgu_bwd", grid=(s // ts,),
        in_specs=[_tile_spec(ts, D_SGU), _tile_spec(ts, 2 * D_RNN, 1), _full_spec(sw), _full_spec(sw),
                  _full_spec((SGU_BLOCK, SGU_BLOCK)), _full_spec((1, D_SGU)), _full_spec((1, D_SGU))],
        out_specs=[_tile_spec(ts, 2 * D_SGU), _full_spec(sw), _full_spec((SGU_BLOCK, SGU_GROUPS)),
                   _full_spec((1, D_SGU)), _full_spec((1, D_SGU))],
        out_shape=[jax.ShapeDtypeStruct((s, 2 * D_SGU), BF), jax.ShapeDtypeStruct(sw, F32),
                   jax.ShapeDtypeStruct((SGU_BLOCK, SGU_GROUPS), F32),
                   jax.ShapeDtypeStruct((1, D_SGU), F32), jax.ShapeDtypeStruct((1, D_SGU), F32)],
        scratch_shapes=[pltpu.VMEM((SGU_BLOCK, D_SGU), F32)],
        compiler_params=_params(("arbitrary",)),
    )(dyb_pre, proj, wm, bsb, mask, lg, lb)


_ROW_DBA, _ROW_DBX, _ROW_DSP, _ROW_DCB, _ROW_DCW = 0, 1, 2, 3, 4
_PREV_ROWS = 16


def _rnn_bwd_call(dya_pre, proj, hr, wa, wx, ba, bx, sp, cw, cb, ts):
    s = proj.shape[0]
    nt = s // ts
    per = ts // _PREV_ROWS

    def tile(i):
        return nt - 1 - i

    def prev(i):
        return jnp.maximum(tile(i) * per - 1, 0)

    def body(dy_ref, xg_ref, xgp_ref, hr_ref, hrp_ref, wa_ref, wx_ref, ba_ref, bx_ref, sp_ref,
             cw_ref, cb_ref, dxg_ref, dwa_ref, dwx_ref, vec_ref,
             lam_carry, a_first, dxr_head, al_sc, bl_sc, lam_sc):
        step = pl.program_id(0)

        @pl.when(step == 0)
        def _():
            dwa_ref[...] = jnp.zeros_like(dwa_ref)
            dwx_ref[...] = jnp.zeros_like(dwx_ref)
            vec_ref[...] = jnp.zeros_like(vec_ref)
            lam_carry[...] = jnp.zeros_like(lam_carry)
            a_first[...] = jnp.zeros_like(a_first)
            dxr_head[...] = jnp.zeros_like(dxr_head)

        has_prev = (step < nt - 1).astype(F32)
        x = xg_ref[:, :D_RNN].astype(F32)
        g = xg_ref[:, D_RNN:].astype(F32)
        x_tail = xgp_ref[_PREV_ROWS - SUBLANES:, :D_RNN].astype(F32) * has_prev
        h_tail = hrp_ref[_PREV_ROWS - SUBLANES:, :].astype(F32) * has_prev
        xr, x_shifted = _conv_fwd(x, x_tail, cw_ref, cb_ref)
        r, i, a, nrm, inv_nrm = _lru_gates(xr, wa_ref, wx_ref, ba_ref, bx_ref, sp_ref)
        h = hr_ref[...].astype(F32)
        dy = dy_ref[...].astype(F32)
        gg, dgg = _gelu_and_grad(g)

        coef = _shift_up(a, jnp.broadcast_to(a_first[...], (SUBLANES, D_RNN)), 1)
        lam_carry[...] = _linear_scan(coef, dy * gg, lam_carry[...], al_sc, bl_sc, lam_sc, True)
        a_first[...] = a[0:1, :]
        lam = lam_sc[...]

        da = lam * _shift_down(h, h_tail, 1)
        dnrm = lam * (i * xr)
        di = lam * nrm * xr
        dlog_a = da * a - dnrm * (a * a) * inv_nrm
        spv = sp_ref[...]
        dza = (dlog_a * (-LRU_C * spv)) * (r * (1.0 - r))
        dzx = di * (i * (1.0 - i))
        vec_ref[_ROW_DSP:_ROW_DSP + 1, :] += _row_sum(dlog_a * (-LRU_C * r))
        vec_ref[_ROW_DBA:_ROW_DBA + 1, :] += _row_sum(dza)
        vec_ref[_ROW_DBX:_ROW_DBX + 1, :] += _row_sum(dzx)
        xb = xr.astype(BF)
        dza_bf = dza.astype(BF)
        dzx_bf = dzx.astype(BF)
        for grp in range(N_LRU_GROUPS):
            cols = slice(grp * LRU_GROUP, (grp + 1) * LRU_GROUP)
            dwa_ref[grp] += _dot_tn(xb[:, cols], dza_bf[:, cols])
            dwx_ref[grp] += _dot_tn(xb[:, cols], dzx_bf[:, cols])
        dxr = (lam * nrm * i + _group_dot(dza_bf, wa_ref, _dot_nt) + _group_dot(dzx_bf, wx_ref, _dot_nt))

        vec_ref[_ROW_DCB:_ROW_DCB + 1, :] += _row_sum(dxr)
        head = dxr_head[...]
        dx = cw_ref[CONV_WIDTH - 1:CONV_WIDTH, :] * dxr
        vec_ref[_ROW_DCW + 3:_ROW_DCW + 4, :] += _row_sum(dxr * x)
        for sft in range(1, CONV_WIDTH):
            k = CONV_WIDTH - 1 - sft
            dx = dx + cw_ref[k:k + 1, :] * _shift_up(dxr, head, sft)
            vec_ref[_ROW_DCW + k:_ROW_DCW + k + 1, :] += _row_sum(dxr * x_shifted[sft])
        dxr_head[...] = dxr[0:SUBLANES, :]
        dxg_ref[:, :D_RNN] = dx.astype(BF)
        dxg_ref[:, D_RNN:] = (dy * h * dgg).astype(BF)

    gw = (N_LRU_GROUPS, LRU_GROUP, LRU_GROUP)
    rev = lambda width: pl.BlockSpec((ts, width), lambda i: (tile(i), 0))
    return pl.pallas_call(
        body, name="rnn_bwd", grid=(nt,),
        in_specs=[rev(D_RNN), rev(2 * D_RNN),
                  pl.BlockSpec((_PREV_ROWS, 2 * D_RNN), lambda i: (prev(i), 0)),
                  rev(D_RNN),
                  pl.BlockSpec((_PREV_ROWS, D_RNN), lambda i: (prev(i), 0)),
                  _full_spec(gw), _full_spec(gw),
                  _full_spec((1, D_RNN)), _full_spec((1, D_RNN)), _full_spec((1, D_RNN)),
                  _full_spec((CONV_WIDTH, D_RNN)), _full_spec((1, D_RNN))],
        out_specs=[rev(2 * D_RNN), _full_spec(gw), _full_spec(gw), _full_spec((SUBLANES, D_RNN))],
        out_shape=[jax.ShapeDtypeStruct((s, 2 * D_RNN), BF), jax.ShapeDtypeStruct(gw, F32),
                   jax.ShapeDtypeStruct(gw, F32), jax.ShapeDtypeStruct((SUBLANES, D_RNN), F32)],
        scratch_shapes=[pltpu.VMEM((1, D_RNN), F32), pltpu.VMEM((1, D_RNN), F32),
                        pltpu.VMEM((SUBLANES, D_RNN), F32),
                        pltpu.VMEM((ts, D_RNN), F32), pltpu.VMEM((ts, D_RNN), F32),
                        pltpu.VMEM((ts, D_RNN), F32)],
        compiler_params=_params(("arbitrary",)),
    )(dya_pre, proj, proj, hr, hr, wa, wx, ba, bx, sp, cw, cb)


def _inproj_bwd_call(dxg, duv, dgate, dx1, x, g1, w_in, layer, ts):
    s = x.shape[0]

    def body(dxg_ref, duv_ref, dgt_ref, dx1_ref, x_ref, g_ref, w_ref, dx_ref, dproj_ref, dg_ref):
        @pl.when(pl.program_id(0) == 0)
        def _():
            dg_ref[...] = jnp.zeros_like(dg_ref)

        dproj = jnp.concatenate([dxg_ref[...], duv_ref[...], dgt_ref[...]], axis=1)
        dproj_ref[...] = dproj
        dh = jnp.zeros((ts, D_MODEL), F32)
        for q in range(N_QUARTERS):
            dh = dh + _dot_nt(dproj[:, q * Q_IN:(q + 1) * Q_IN], w_ref[q])
        dx, dg = _rms_bwd(dh, x_ref[...], g_ref[...])
        dx_ref[...] = dx1_ref[...] + dx
        dg_ref[...] += _row_sum(dg)

    return pl.pallas_call(
        body, name="inproj_bwd", grid=(s // ts,),
        in_specs=[_tile_spec(ts, 2 * D_RNN), _tile_spec(ts, 2 * D_SGU), _tile_spec(ts, 2 * D_MODEL),
                  _tile_spec(ts, D_MODEL), _tile_spec(ts, D_MODEL), _full_spec((1, D_MODEL)),
                  pl.BlockSpec((None, N_QUARTERS, D_MODEL, Q_IN), lambda i: (layer, 0, 0, 0))],
        out_specs=[_tile_spec(ts, D_MODEL), _tile_spec(ts, D_IN), _full_spec((1, D_MODEL))],
        out_shape=[jax.ShapeDtypeStruct((s, D_MODEL), F32), jax.ShapeDtypeStruct((s, D_IN), BF),
                   jax.ShapeDtypeStruct((1, D_MODEL), F32)],
        compiler_params=_params(("arbitrary",)),
    )(dxg, duv, dgate, dx1, x, g1, w_in)


def _relu_sq(p):
    return jnp.square(jnp.maximum(p.astype(F32), 0.0))


def _wgrad_call(a, b, core, tm, tn, tk, col_blocked, name, a_fn=None):
    s, m = a.shape
    n = b.shape[1]
    r, cols = (m, n // N_QUARTERS) if col_blocked else (m // N_QUARTERS, n)
    r2 = r // 2
    per_tile = tm // r
    steps = s // tk

    def body(core_ref, a_ref, b_ref, keep_ref, send_ref, *acc):
        av = a_ref[...]
        if a_fn is not None:
            av = a_fn(av)
        prod = _dot_tn(av.astype(BF), b_ref[...].astype(BF))

        def emit(total):
            for h in range(2):
                @pl.when(core_ref[0] == h)
                def _():
                    for q in range(per_tile):
                        keep_ref[q] = total[q * r + h * r2:q * r + (h + 1) * r2]
                        send_ref[q] = total[q * r + (1 - h) * r2:q * r + (2 - h) * r2].astype(BF)

        if steps == 1:
            emit(prod)
        else:
            acc_ref, = acc
            step = pl.program_id(2)

            @pl.when(step == 0)
            def _():
                acc_ref[...] = prod

            @pl.when(jnp.logical_and(step > 0, step < steps - 1))
            def _():
                acc_ref[...] += prod

            @pl.when(step == steps - 1)
            def _():
                emit(acc_ref[...] + prod)

    if col_blocked:
        per_q = cols // tn
        out_spec = pl.BlockSpec((1, r2, tn), lambda i, j, k, c: (j // per_q, 0, j % per_q))
    else:
        out_spec = pl.BlockSpec((per_tile, r2, tn), lambda i, j, k, c: (i, 0, j))
    return pl.pallas_call(
        body, name=name,
        out_shape=[jax.ShapeDtypeStruct((N_QUARTERS, r2, cols), F32),
                   jax.ShapeDtypeStruct((N_QUARTERS, r2, cols), BF)],
        grid_spec=pltpu.PrefetchScalarGridSpec(
            num_scalar_prefetch=1, grid=(m // tm, n // tn, steps),
            in_specs=[pl.BlockSpec((tk, tm), lambda i, j, k, c: (k, i)),
                      pl.BlockSpec((tk, tn), lambda i, j, k, c: (k, j))],
            out_specs=[out_spec, out_spec],
            scratch_shapes=[] if steps == 1 else [pltpu.VMEM((tm, tn), F32)]),
        compiler_params=_params(("parallel", "parallel", "arbitrary")),
    )(core, a, b)


BIG = ("w_in", "w_up", "w_down", "w_branch_a", "w_branch_b", "w_out")


def _block_diag(w):
    w4 = w.reshape(N_LRU_GROUPS, HEADS_PER_GROUP, RNN_HEAD_DIM, RNN_HEAD_DIM)
    eye = jnp.eye(HEADS_PER_GROUP, dtype=w.dtype)
    return jnp.einsum("gjio,jk->gjiko", w4, eye).reshape(N_LRU_GROUPS, LRU_GROUP, LRU_GROUP)


def _block_diag_extract(d):
    d5 = d.reshape(N_LRU_GROUPS, HEADS_PER_GROUP, RNN_HEAD_DIM, HEADS_PER_GROUP, RNN_HEAD_DIM)
    blocks = [d5[:, j, :, j, :] for j in range(HEADS_PER_GROUP)]
    return jnp.stack(blocks, axis=1).reshape(RNN_HEADS, RNN_HEAD_DIM, RNN_HEAD_DIM)


def _sgu_mask():
    chunk = jnp.arange(SGU_BLOCK) // CHUNK
    return (chunk[:, None] >= chunk[None, :]).astype(F32)


def _layer_small(sm, l, core):
    row = lambda v: v.reshape(1, -1)
    return dict(
        core=core,
        g1=row(sm["norm_mix_g"][l]), g2=row(sm["norm_ffn_g"][l]),
        wa=_block_diag(sm["lru_w_a"][l]).astype(BF), wx=_block_diag(sm["lru_w_x"][l]).astype(BF),
        ba=row(sm["lru_b_a"][l]), bx=row(sm["lru_b_x"][l]),
        sp=row(jax.nn.softplus(-sm["lru_lambda"][l])),
        cw=sm["conv_w"][l], cb=row(sm["conv_b"][l]),
        wm=(sm["sgu_w_s"][l] * _sgu_mask()).astype(BF),
        bsb=jnp.broadcast_to(sm["sgu_b_s"][l][:, :, None], (SGU_GROUPS, SGU_BLOCK, SGU_BLOCK)),
        lg=row(sm["sgu_ln_g"][l]), lb=row(sm["sgu_ln_b"][l]),
    )


def _layer_fwd_mix(x, big, p, ts):
    h = _norm_call(x, p["g1"], ts)
    proj = _inproj_call(h, big["w_in"], 0, 2 * ts)
    hr, ya_pre = _rnn_fwd_call(proj, p["wa"], p["wx"], p["ba"], p["bx"], p["sp"], p["cw"], p["cb"], ts)
    yb_pre = _sgu_fwd_call(proj, p["wm"], p["bsb"], p["lg"], p["lb"], ts)
    return dict(p=p, x=x, h=h, proj=proj, hr=hr, ya_pre=ya_pre, yb_pre=yb_pre)


def _layer_fwd_out(sv, big, ts):
    x1, ya, yb, merged, h2 = _merge_call(sv["x"], sv["proj"], sv["ya_pre"], sv["yb_pre"], big["w_branch_a"],
                                         big["w_branch_b"], big["w_out"], sv["p"]["g2"], 0, ts)
    x2, pre = _ffn_call(x1, h2, big["w_up"], big["w_down"], 0, ts)
    sv.update(x1=x1, ya=ya, yb=yb, merged=merged, h2=h2, pre=pre)
    return x2


def _layer_bwd_ffn(dx, sv, big, ts):
    p = sv["p"]
    dx1, dpre, dg2 = _ffn_bwd_call(dx, sv["pre"], sv["x1"], p["g2"], big["w_up"], big["w_down"], 0, ts)
    tk = dx.shape[0]
    gb = dict(
        w_down=_wgrad_call(sv["pre"], dx, p["core"], Q_FF, D_MODEL // 2, tk, False, "wgrad_down", a_fn=_relu_sq),
        w_up=_wgrad_call(sv["h2"], dpre, p["core"], D_MODEL, Q_FF, tk, True, "wgrad_up"))
    return dx1, gb, dict(norm_ffn_g=dg2[0])


def _layer_bwd_merge(dx1, sv, big, ts):
    tk = dx1.shape[0]
    core = sv["p"]["core"]
    dya, dyb, dgate, dya_pre, dyb_pre = _merge_bwd_call(
        dx1, sv["proj"], sv["ya"], sv["yb"], big["w_branch_a"], big["w_branch_b"], big["w_out"], 0, ts)
    gb = dict(
        w_out=_wgrad_call(sv["merged"], dx1, core, D_MODEL, D_MODEL // 2, tk, False, "wgrad_out"),
        w_branch_a=_wgrad_call(sv["ya_pre"], dya, core, D_RNN, D_MODEL // 2, tk, False, "wgrad_branch_a"),
        w_branch_b=_wgrad_call(sv["yb_pre"], dyb, core, D_SGU, D_MODEL // 2, tk, False, "wgrad_branch_b"))
    return (dgate, dya_pre, dyb_pre), gb


def _layer_bwd_branches(dx1, merge_out, sv, big, lam, ts):
    p = sv["p"]
    tk = dx1.shape[0]
    dgate, dya_pre, dyb_pre = merge_out
    gb = {}
    duv, dws, dbs, dlg, dlb = _sgu_bwd_call(dyb_pre, sv["proj"], p["wm"], p["bsb"], _sgu_mask(), p["lg"], p["lb"],
                                            ts)
    dxg, dwa, dwx, vec = _rnn_bwd_call(dya_pre, sv["proj"], sv["hr"], p["wa"], p["wx"], p["ba"], p["bx"],
                                       p["sp"], p["cw"], p["cb"], ts // 2)
    dx, dproj, dg1 = _inproj_bwd_call(dxg, duv, dgate, dx1, sv["x"], p["g1"], big["w_in"], 0, ts)
    gb["w_in"] = _wgrad_call(sv["h"], dproj, p["core"], D_MODEL, Q_IN, tk // 2, True, "wgrad_in")
    gs = dict(
        norm_mix_g=dg1[0], conv_w=vec[_ROW_DCW:_ROW_DCW + CONV_WIDTH], conv_b=vec[_ROW_DCB],
        lru_w_a=_block_diag_extract(dwa), lru_w_x=_block_diag_extract(dwx),
        lru_b_a=vec[_ROW_DBA].reshape(RNN_HEADS, RNN_HEAD_DIM), lru_b_x=vec[_ROW_DBX].reshape(RNN_HEADS, RNN_HEAD_DIM),
        lru_lambda=-vec[_ROW_DSP] * jax.nn.sigmoid(-lam),
        sgu_ln_g=dlg[0], sgu_ln_b=dlb[0], sgu_w_s=dws, sgu_b_s=dbs.T)
    return dx, gb, gs


def _local_step(x, target, big, sm, ts):
    saved = []
    core = jnp.zeros((1,), jnp.int32)
    for l in range(DEPTH):
        sv = _layer_fwd_mix(x, big[l], _layer_small(sm, l, core), ts)
        x = _layer_fwd_out(sv, big[l], ts)
        saved.append(sv)
    dx, loss, dgf = _loss_call(x, target, sm["final_norm_g"].reshape(1, -1), ts)
    gb, gs = [None] * DEPTH, [None] * DEPTH
    for l in reversed(range(DEPTH)):
        dx1, gb_ffn, gs_ffn = _layer_bwd_ffn(dx, saved[l], big[l], ts)
        merge_out, gb_merge = _layer_bwd_merge(dx1, saved[l], big[l], ts)
        dx, gb_mix, gs_mix = _layer_bwd_branches(dx1, merge_out, saved[l], big[l], sm["lru_lambda"][l], ts)
        gb[l] = {**gb_ffn, **gb_merge, **gb_mix}
        gs[l] = {**gs_ffn, **gs_mix}
    gs = {k: jnp.stack([g[k] for g in gs]) for k in gs[0]}
    gs["final_norm_g"] = dgf[0]
    return loss, dx, gb, gs


EW_VMEM_BYTES = 24 * 1024 * 1024


def _row_block(rows, cols, bytes_per_elem):
    for br in range(min(rows, EW_VMEM_BYTES // (2 * bytes_per_elem * cols)), 0, -1):
        if rows % br == 0 and br % 16 == 0:
            return br
    return rows


def _ew_call(fn, name, operands, outputs, slabs=1, sel=None, into=None, after=None):
    if into is not None and not isinstance(into, (list, tuple)):
        into = [into]
    rows, cols = outputs[0][0].shape[2:]
    br = _row_block(rows, cols, sum(jnp.dtype(a.dtype).itemsize for a, _ in operands + outputs))
    n_in = len(operands)

    def pick(tok, g, s):
        if callable(tok):
            return tok(g, s)
        if tok == "g":
            return g
        if isinstance(tok, tuple):
            return s[tok[1]]
        return tok

    def spec(idx):
        return pl.BlockSpec((None, None, br, cols),
                            lambda g, i, s, idx=idx: (pick(idx[0], g, s), pick(idx[1], g, s), i, 0))

    if sel is None:
        sel = jnp.zeros((1,), jnp.int32)
    in_specs = [spec(idx) for _, idx in operands]
    arrays = [a for a, _ in operands]
    aliases = {}
    for j, buf in enumerate(into or ()):
        in_specs.append(pl.BlockSpec(memory_space=pl.ANY))
        arrays.append(buf)
        aliases[1 + n_in + j] = j
    if after is not None:
        in_specs.append(pl.BlockSpec(memory_space=pl.ANY))
        arrays.append(after)

    def body(sel_ref, *refs):
        outs = fn(*[r[...] for r in refs[:n_in]])
        for o_ref, o in zip(refs[len(arrays):], outs):
            o_ref[...] = o.astype(o_ref.dtype)

    return pl.pallas_call(
        body, name=name, out_shape=[s for s, _ in outputs],
        grid_spec=pltpu.PrefetchScalarGridSpec(
            num_scalar_prefetch=1, grid=(slabs, rows // br),
            in_specs=in_specs,
            out_specs=[spec(idx) for _, idx in outputs]),
        input_output_aliases=aliases,
        compiler_params=_params(("parallel", "parallel")),
    )(sel, *arrays)


def _as4(a):
    return a.reshape((1,) * (4 - a.ndim) + a.shape)


def _adamw(w, g, m, v):
    m = ADAM_B1 * m + (1.0 - ADAM_B1) * g
    v = ADAM_B2 * v + (1.0 - ADAM_B2) * jnp.square(g)
    m_hat = m / (1.0 - ADAM_B1 ** ADAM_STEP)
    v_hat = v / (1.0 - ADAM_B2 ** ADAM_STEP)
    delta = -ADAM_LR * (m_hat / (jnp.sqrt(v_hat) + ADAM_EPS) + ADAM_WD * w)
    return delta, m, v


def _small_adamw_call(ws, gs, ms, vs):
    n = len(ws)

    def body(*refs):
        for k in range(n):
            w, g, m, v = (refs[j * n + k][...] for j in range(4))
            outs = _adamw(w, g, m, v)
            for j in range(3):
                refs[(4 + j) * n + k][...] = outs[j]

    shapes = [jax.ShapeDtypeStruct(w.shape, F32) for w in ws]
    outs = pl.pallas_call(
        body, name="adamw_small", out_shape=shapes * 3,
        in_specs=[pl.BlockSpec(memory_space=pltpu.VMEM)] * (4 * n),
        out_specs=[pl.BlockSpec(memory_space=pltpu.VMEM)] * (3 * n),
        compiler_params=_params(),
    )(*ws, *gs, *ms, *vs)
    return outs[:n], outs[n:2 * n], outs[2 * n:]


ANY = pl.BlockSpec(memory_space=pl.ANY)


def _place():
    x, y, c = lax.axis_index("x"), lax.axis_index("y"), lax.axis_index("c")
    chips = [(1 - x, y), (x, 1 - y), (1 - x, 1 - y)]
    return x, y, c, chips


def _remote(src, dst, send_sem, recv_sem, to):
    return pltpu.make_async_remote_copy(src_ref=src, dst_ref=dst, send_sem=send_sem, recv_sem=recv_sem,
                                        device_id=to, device_id_type=MESH)


def _gather_call(bufs):
    n = len(bufs)

    def body(*refs):
        out = refs[n:2 * n]
        send_sems, recv_sems = refs[2 * n:]
        x, y, c, chips = _place()
        me_q = 2 * x + y
        sibling = (x, y, 1 - c)
        first = []
        for w in range(n):
            for j, chip in enumerate(chips):
                mine = out[w].at[c, me_q]
                first.append(_remote(mine, mine, send_sems.at[w * 3 + j], recv_sems.at[w * 3 + j], (*chip, c)))
        for cp in first:
            cp.start()
        passed = []
        for w in range(n):
            for j, (qx, qy) in enumerate(chips):
                landed = out[w].at[c, 2 * qx + qy]
                k = w * 3 + j
                _remote(landed, landed, send_sems.at[k], recv_sems.at[k], (qx, qy, c)).wait_recv()
                cp = _remote(landed, landed, send_sems.at[3 * n + k], recv_sems.at[3 * n + k], sibling)
                cp.start()
                passed.append(cp)
        for w in range(n):
            for j, (qx, qy) in enumerate(chips):
                landed = out[w].at[1 - c, 2 * qx + qy]
                k = 3 * n + w * 3 + j
                _remote(landed, landed, send_sems.at[k], recv_sems.at[k], sibling).wait_recv()
        for cp in first + passed:
            cp.wait_send()

    return pl.pallas_call(
        body, name="gather_weights",
        out_shape=[jax.ShapeDtypeStruct(a.shape, a.dtype) for a in bufs],
        in_specs=[ANY] * n, out_specs=[ANY] * n,
        input_output_aliases={w: w for w in range(n)},
        scratch_shapes=[pltpu.SemaphoreType.DMA((6 * n,)), pltpu.SemaphoreType.DMA((6 * n,))],
        compiler_params=_params(vmem=False, has_side_effects=True),
    )(*bufs)


def _sibling_send_call(items):
    n = len(items)

    def body(*refs):
        src, out = refs[:n], refs[n:2 * n]
        send_sems, recv_sems = refs[2 * n:]
        x, y, c, _ = _place()
        copies = [_remote(src[w], out[w], send_sems.at[w], recv_sems.at[w], (x, y, 1 - c)) for w in range(n)]
        for cp in copies:
            cp.start()
        for cp in copies:
            cp.wait()

    return pl.pallas_call(
        body, name="grads_to_sibling",
        out_shape=[jax.ShapeDtypeStruct(a.shape, a.dtype) for a in items],
        in_specs=[ANY] * n, out_specs=[ANY] * n,
        scratch_shapes=[pltpu.SemaphoreType.DMA((n,)), pltpu.SemaphoreType.DMA((n,))],
        compiler_params=_params(vmem=False, has_side_effects=True),
    )(*items)


def _sibling_inplace_call(name, bufs, slabs, n_pairs):
    n = len(bufs)

    def body(*refs):
        out = refs[n:2 * n]
        send_sems, recv_sems = refs[2 * n:]
        x, y, c, _ = _place()
        sibling = (x, y, 1 - c)
        pairs = [pair for w, ref in enumerate(out) for pair in slabs(ref, c, w)]
        sends = [_remote(s, s, send_sems.at[k], recv_sems.at[k], sibling) for k, (s, _) in enumerate(pairs)]
        for cp in sends:
            cp.start()
        for k, (_, r) in enumerate(pairs):
            _remote(r, r, send_sems.at[k], recv_sems.at[k], sibling).wait_recv()
        for cp in sends:
            cp.wait_send()

    return pl.pallas_call(
        body, name=name,
        out_shape=[jax.ShapeDtypeStruct(a.shape, a.dtype) for a in bufs],
        in_specs=[ANY] * n, out_specs=[ANY] * n,
        input_output_aliases={w: w for w in range(n)},
        scratch_shapes=[pltpu.SemaphoreType.DMA((n_pairs,)), pltpu.SemaphoreType.DMA((n_pairs,))],
        compiler_params=_params(vmem=False, has_side_effects=True),
    )(*bufs)


HBM_SPEC = pl.BlockSpec(memory_space=pltpu.HBM)
SEM_SPEC = pl.BlockSpec(memory_space=pltpu.SEMAPHORE)
DATAFLOW_EFFECT = pltpu.SideEffectType.DATAFLOW_SIDE_EFFECTING


def _exchange_start(name, bufs, copies, n_copies, after):
    n = len(bufs)

    def body(*refs):
        ins, send_sems, recv_sems, token = refs[:n], refs[n + 1], refs[n + 2], refs[-1]
        for k, (src, dst, to) in enumerate(copies(ins)):
            _remote(src, dst, send_sems.at[k], recv_sems.at[k], to).start()
        token[...] = jnp.zeros_like(token)

    outs = pl.pallas_call(
        body, name=name,
        out_shape=(pltpu.SemaphoreType.DMA((n_copies,)), pltpu.SemaphoreType.DMA((n_copies,)),
                   *[pltpu.HBM(b.shape, b.dtype) for b in bufs], jax.ShapeDtypeStruct((SUBLANES, 128), F32)),
        in_specs=[HBM_SPEC] * n + [ANY],
        out_specs=(SEM_SPEC, SEM_SPEC, *[HBM_SPEC] * n, pl.BlockSpec(memory_space=pltpu.VMEM)),
        input_output_aliases={w: w + 2 for w in range(n)},
        compiler_params=pltpu.CompilerParams(has_side_effects=DATAFLOW_EFFECT),
    )(*[pltpu.with_memory_space_constraint(b, pltpu.HBM) for b in bufs], after)
    return outs[0], outs[1], list(outs[2:2 + n]), outs[-1]


def _exchange_wait(name, send_sems, recv_sems, bufs, copies, after):
    n = len(bufs)

    def body(*refs):
        ins, send_sems, recv_sems = refs[:n], refs[n], refs[n + 1]
        for k, (src, dst, to) in enumerate(copies(ins)):
            cp = _remote(src, dst, send_sems.at[k], recv_sems.at[k], to)
            cp.wait_send()
            cp.wait_recv()

    return pl.pallas_call(
        body, name=name,
        out_shape=[pltpu.HBM(b.shape, b.dtype) for b in bufs],
        in_specs=[HBM_SPEC] * n + [SEM_SPEC, SEM_SPEC, ANY],
        out_specs=[HBM_SPEC] * n,
        input_output_aliases={w: w for w in range(n)},
        compiler_params=pltpu.CompilerParams(has_side_effects=DATAFLOW_EFFECT),
    )(*bufs, send_sems, recv_sems, after)


def _gather_copies(refs):
    x, y, c, chips = _place()
    mine = 2 * (2 * x + y) + c
    return [(ref.at[mine], ref.at[mine], (qx, qy, c)) for ref in refs for qx, qy in chips]


def _gather_forward_slabs(ref, c, w):
    x, y, _, chips = _place()
    return [(ref.at[2 * (2 * qx + qy) + c], ref.at[2 * (2 * qx + qy) + 1 - c]) for qx, qy in chips]


def _device_peers():
    x, y, c, _ = _place()
    return 4 * x + 2 * y + c, [(k, (x ^ ((k >> 2) & 1), y ^ ((k >> 1) & 1), c ^ (k & 1))) for k in range(1, 8)]


def _small_scatter_copies(refs):
    me, peers = _device_peers()
    return [(refs[0].at[me ^ k], refs[1].at[me], to) for k, to in peers]


def _small_spread_copies(refs):
    me, peers = _device_peers()
    return [(refs[0].at[me], refs[0].at[me], to) for _, to in peers]


def _sibling_copies(refs):
    n = len(refs) // 2
    x, y, c, _ = _place()
    return [(refs[w], refs[n + w], (x, y, 1 - c)) for w in range(n)]


def _owner_copies(refs):
    n = len(refs) // 2
    x, y, c, chips = _place()
    return [(refs[w].at[2 * qx + qy], refs[n + w].at[j], (qx, qy, c))
            for w in range(n) for j, (qx, qy) in enumerate(chips)]


N_DEVICES = 8
SMALL_ROWS = 616


SMALL = ("norm_mix_g", "conv_w", "conv_b", "lru_w_a", "lru_b_a", "lru_w_x", "lru_b_x", "lru_lambda",
         "sgu_ln_g", "sgu_ln_b", "sgu_w_s", "sgu_b_s", "norm_ffn_g", "final_norm_g")
WEIGHTS = ("norm_mix_g", "w_in", "conv_w", "conv_b", "lru_w_a", "lru_b_a", "lru_w_x", "lru_b_x", "lru_lambda",
           "sgu_ln_g", "sgu_ln_b", "sgu_w_s", "sgu_b_s", "w_branch_a", "w_branch_b", "w_out", "norm_ffn_g",
           "w_up", "w_down", "final_norm_g")
PACK_ALIGN = SUBLANES * 128


def _pack_small(gs):
    parts = []
    for k in SMALL:
        flat = gs[k].reshape(-1)
        parts.append(jnp.pad(flat, (0, -flat.size % PACK_ALIGN)))
    flat = jnp.concatenate(parts)
    flat = jnp.pad(flat, (0, N_DEVICES * SMALL_ROWS * 128 - flat.size))
    return flat.reshape(N_DEVICES, SMALL_ROWS, 128)


def _unpack_small(buf, like):
    flat = buf.reshape(-1)
    out, off = {}, 0
    for k in SMALL:
        size = like[k].size
        out[k] = flat[off:off + size].reshape(like[k].shape)
        off += size + (-size % PACK_ALIGN)
    return out


def _as_rows(a):
    return a.reshape(-1, a.shape[-1])


def kernel(x, norm_mix_g, w_in, conv_w, conv_b, lru_w_a, lru_b_a, lru_w_x, lru_b_x, lru_lambda, sgu_ln_g, sgu_ln_b, sgu_w_s, sgu_b_s, w_branch_a, w_branch_b, w_out, norm_ffn_g, w_up, w_down, final_norm_g, loss_target, m_norm_mix_g, m_w_in, m_conv_w, m_conv_b, m_lru_w_a, m_lru_b_a, m_lru_w_x, m_lru_b_x, m_lru_lambda, m_sgu_ln_g, m_sgu_ln_b, m_sgu_w_s, m_sgu_b_s, m_w_branch_a, m_w_branch_b, m_w_out, m_norm_ffn_g, m_w_up, m_w_down, m_final_norm_g, v_norm_mix_g, v_w_in, v_conv_w, v_conv_b, v_lru_w_a, v_lru_b_a, v_lru_w_x, v_lru_b_x, v_lru_lambda, v_sgu_ln_g, v_sgu_ln_b, v_sgu_w_s, v_sgu_b_s, v_w_branch_a, v_w_branch_b, v_w_out, v_norm_ffn_g, v_w_up, v_w_down, v_final_norm_g):
    w = dict(norm_mix_g=norm_mix_g, w_in=w_in, conv_w=conv_w, conv_b=conv_b, lru_w_a=lru_w_a, lru_b_a=lru_b_a,
             lru_w_x=lru_w_x, lru_b_x=lru_b_x, lru_lambda=lru_lambda, sgu_ln_g=sgu_ln_g, sgu_ln_b=sgu_ln_b,
             sgu_w_s=sgu_w_s, sgu_b_s=sgu_b_s, w_branch_a=w_branch_a, w_branch_b=w_branch_b, w_out=w_out,
             norm_ffn_g=norm_ffn_g, w_up=w_up, w_down=w_down, final_norm_g=final_norm_g)
    m = dict(norm_mix_g=m_norm_mix_g, w_in=m_w_in, conv_w=m_conv_w, conv_b=m_conv_b, lru_w_a=m_lru_w_a,
             lru_b_a=m_lru_b_a, lru_w_x=m_lru_w_x, lru_b_x=m_lru_b_x, lru_lambda=m_lru_lambda,
             sgu_ln_g=m_sgu_ln_g, sgu_ln_b=m_sgu_ln_b, sgu_w_s=m_sgu_w_s, sgu_b_s=m_sgu_b_s,
             w_branch_a=m_w_branch_a, w_branch_b=m_w_branch_b, w_out=m_w_out, norm_ffn_g=m_norm_ffn_g,
             w_up=m_w_up, w_down=m_w_down, final_norm_g=m_final_norm_g)
    v = dict(norm_mix_g=v_norm_mix_g, w_in=v_w_in, conv_w=v_conv_w, conv_b=v_conv_b, lru_w_a=v_lru_w_a,
             lru_b_a=v_lru_b_a, lru_w_x=v_lru_w_x, lru_b_x=v_lru_b_x, lru_lambda=v_lru_lambda,
             sgu_ln_g=v_sgu_ln_g, sgu_ln_b=v_sgu_ln_b, sgu_w_s=v_sgu_w_s, sgu_b_s=v_sgu_b_s,
             w_branch_a=v_w_branch_a, w_branch_b=v_w_branch_b, w_out=v_w_out, norm_ffn_g=v_norm_ffn_g,
             w_up=v_w_up, w_down=v_w_down, final_norm_g=v_final_norm_g)
    core = lax.axis_index("c")
    chip = 2 * lax.axis_index("x") + lax.axis_index("y")
    sel = jnp.stack([core, 1 - core, chip, 2 * chip + core]).astype(jnp.int32)
    this_core, other_core, this_chip = ("sel", 0), ("sel", 1), ("sel", 2)
    sds = jax.ShapeDtypeStruct

    ts = TOKEN_TILE
    halves = {k: (w[k].shape[1] // 2, w[k].shape[2]) for k in BIG}

    def half_view(k, a):
        return a.reshape((2 * N_QUARTERS,) + halves[k])

    def full_view(k, a):
        r2, cols = halves[k]
        if k in ("w_in", "w_up"):
            return a.reshape(1, N_QUARTERS, 2 * r2, cols)
        return a.reshape(1, 2 * N_QUARTERS * r2, cols)

    layer_bufs = [{}, {}]

    def cast_weights(k, after):
        _, r, cols = w[k].shape
        w4 = w[k].reshape(DEPTH, 1, r, cols)
        outs = _ew_call(lambda a, b: (a, b), "cast_weights", [(w4, (0, 0)), (w4, (1, 0))],
                        [(sds((1, N_QUARTERS, r, cols), BF), (0, this_chip))] * DEPTH, 1, sel, after=after)
        for l in range(DEPTH):
            layer_bufs[l][k] = half_view(k, outs[l])

    conv_buf = lax.dynamic_update_slice_in_dim(
        jnp.zeros((DEPTH, N_QUARTERS) + conv_w.shape[1:], F32), conv_w[:, None], chip, axis=1)
    sm = {k: w[k] for k in SMALL}
    sm["conv_w"] = _gather_call([conv_buf])[0].transpose(0, 2, 1, 3).reshape(DEPTH, CONV_WIDTH, D_RNN)

    def gather_start(tag, l, keys, after):
        bufs = [layer_bufs[l][k] for k in keys]
        return _exchange_start(f"gather_start_{tag}", bufs, _gather_copies, 3 * len(keys), after)

    def gather_finish(tag, keys, started, after):
        send_sems, recv_sems, thru, _ = started
        landed = _exchange_wait(f"gather_wait_{tag}", send_sems, recv_sems, thru, _gather_copies, after)
        landed = _sibling_inplace_call("gather_forward", landed, _gather_forward_slabs, 3 * len(keys))
        return {k: full_view(k, a) for k, a in zip(keys, landed)}

    first, rest = ("w_in",), tuple(k for k in BIG if k != "w_in")
    cast_weights("w_in", None)
    started_a = gather_start("0a", 0, first, sm["conv_w"])
    for k in rest:
        cast_weights(k, started_a[3])
    started_b = gather_start("0b", 0, rest, started_a[3])
    started_1 = gather_start("1", 1, BIG, started_b[3])
    big0 = gather_finish("0a", first, started_a, started_1[3])
    sv0 = _layer_fwd_mix(x[0], big0, _layer_small(sm, 0, sel[0:1]), ts)
    big0.update(gather_finish("0b", rest, started_b, sv0["yb_pre"]))
    x_mid = _layer_fwd_out(sv0, big0, ts)
    big1 = gather_finish("1", BIG, started_1, x_mid)
    sv1 = _layer_fwd_mix(x_mid, big1, _layer_small(sm, 1, sel[0:1]), ts)
    x_out = _layer_fwd_out(sv1, big1, ts)
    dx, loss, dgf = _loss_call(x_out, loss_target[0], final_norm_g.reshape(1, -1), ts)

    def pair_start(tag, gb, after):
        sends = [gb[k][1] for k in gb]
        zones = [lax.empty(a.shape, BF) for a in sends]
        return _exchange_start(f"pair_start_{tag}", sends + zones, _sibling_copies, len(sends), after)

    def reduce_start(tag, gb, after, pair=None):
        keys = tuple(gb)
        if pair is None:
            from_sibling = _sibling_send_call([gb[k][1] for k in keys])
        else:
            done = _exchange_wait(f"pair_wait_{tag}", pair[0], pair[1], pair[2], _sibling_copies, after)
            from_sibling = done[len(keys):]
        sums = [
            _ew_call(lambda a, b: (a + b.astype(F32),), "pair_sum", [(gb[k][0][None], (0, "g")), (r[None], (0, "g"))],
                     [(sds((1,) + r.shape, BF), (0, "g"))], N_QUARTERS)[0][0]
            for k, r in zip(keys, from_sibling)]
        zones = [lax.empty((3,) + a.shape[1:], BF) for a in sums]
        started = _exchange_start(f"reduce_start_{tag}", sums + zones, _owner_copies, 3 * len(keys), after)
        return keys, started

    def reduce_finish(tag, l, keys_started, after, reduced):
        keys, (send_sems, recv_sems, thru, _) = keys_started
        done = _exchange_wait(f"reduce_wait_{tag}", send_sems, recv_sems, thru, _owner_copies, after)
        sums, zones = done[:len(keys)], done[len(keys):]
        for i, k in enumerate(keys):
            r2, cols = halves[k]
            reduced[k] = _ew_call(
                lambda a, b, c, d: (((a.astype(F32) + b.astype(F32)) + c.astype(F32)) + d.astype(F32),),
                "quarter_sum", [(sums[i][None], (0, this_chip))] + [(zones[i][None], (0, j)) for j in range(3)],
                [(sds((DEPTH, 2, r2, cols), F32), (l, this_core))], 1, sel, into=reduced.get(k))[0]

    def behind(params, key, started):
        return dict(params, **{key: params[key] + started[1][3][0, 0]})

    dx1, gb_ffn, gs1 = _layer_bwd_ffn(dx, sv1, big1, ts)
    merge_out, gb_merge = _layer_bwd_merge(dx1, sv1, big1, ts)
    dx_mid, gb_in, gs1_mix = _layer_bwd_branches(dx1, merge_out, sv1, big1, lru_lambda[1], ts)
    gb_1 = {**gb_ffn, **gb_merge, **gb_in}
    pair_1 = pair_start("1", gb_1, dx_mid)
    sv0["p"] = behind(sv0["p"], "g2", (None, pair_1))
    dx1, gb_ffn, gs0 = _layer_bwd_ffn(dx_mid, sv0, big0, ts)
    exchange_1 = reduce_start("1", gb_1, dx1, pair_1)
    exchange_0a = reduce_start("0a", gb_ffn, exchange_1[1][3])
    merge_out, gb_merge = _layer_bwd_merge(dx1, sv0, big0, ts)
    exchange_0b = reduce_start("0b", gb_merge, exchange_0a[1][3])
    sv0["p"] = behind(sv0["p"], "lg", exchange_0b)
    grad_x, gb_in, gs0_mix = _layer_bwd_branches(dx1, merge_out, sv0, big0, lru_lambda[0], ts)
    exchange_0c = reduce_start("0c", gb_in, exchange_0b[1][3])
    layer_gs = [{**gs0, **gs0_mix}, {**gs1, **gs1_mix}]
    gs = {k: jnp.stack([g[k] for g in layer_gs]) for k in layer_gs[0]}
    gs["final_norm_g"] = dgf[0]

    me = ("sel", 3)
    piece = (1, N_DEVICES, SMALL_ROWS, 128)
    packed = _pack_small(gs).reshape(piece)
    scatter = _exchange_start("small_scatter_start", [packed[0], lax.empty(piece[1:], F32)], _small_scatter_copies,
                              N_DEVICES - 1, exchange_0c[1][3])
    reduced = {}
    reduce_finish("1", 1, exchange_1, scatter[3], reduced)
    reduce_finish("0a", 0, exchange_0a, reduced["w_in"], reduced)
    reduce_finish("0b", 0, exchange_0b, reduced["w_down"], reduced)

    def swap_slabs(ref, c, i):
        layers = (1,) if BIG[i] == "w_in" else range(DEPTH)
        return [(ref.at[l, c], ref.at[l, 1 - c]) for l in layers]

    swapped = dict(zip(BIG, _sibling_inplace_call("grads_swap_halves", [reduced[k] for k in BIG], swap_slabs,
                                                  DEPTH * len(BIG) - 1)))

    def adamw_layers(k, grad, layer, into):
        if layer is None:
            views = [_as4(_as_rows(a)) for a in (w[k], grad, m[k], v[k])]
            idx = (0, 0)
        else:
            views = [a.reshape((1,) + w[k].shape) for a in (w[k], grad, m[k], v[k])]
            idx = (0, layer)
        return _ew_call(_adamw, "adamw_big", [(a, idx) for a in views], [(sds(views[0].shape, F32), idx)] * 3,
                        into=into)

    def after_all(arrays):
        return jnp.stack([a.reshape(-1)[0] for a in arrays])

    updated = {k: adamw_layers(k, swapped[k], 1 if k == "w_in" else None, None) for k in BIG}
    scattered = _exchange_wait("small_scatter_wait", scatter[0], scatter[1], scatter[2], _small_scatter_copies,
                               after_all([updated[k][0] for k in BIG]))
    summed = _ew_call(
        lambda *parts: (functools.reduce(lambda a, b: a + b, parts),), "small_sum",
        [(scattered[0][None], (0, me))]
        + [(scattered[1][None], (0, lambda g, s, k=k: s[3] ^ k)) for k in range(1, N_DEVICES)],
        [(sds(piece, F32), (0, me))], 1, sel)[0]
    spread = _exchange_start("small_spread_start", [summed[0]], _small_spread_copies, N_DEVICES - 1, summed)
    reduced["w_in"] = swapped["w_in"]
    reduce_finish("0c", 0, exchange_0c, spread[3], reduced)
    last = _sibling_inplace_call("grads_swap_last", [reduced["w_in"]],
                                 lambda ref, c, i: [(ref.at[0, c], ref.at[0, 1 - c])], 1)[0]
    swapped["w_in"] = last
    updated["w_in"] = adamw_layers("w_in", last, 0, updated["w_in"])
    grads_big = {k: swapped[k].reshape(w[k].shape) for k in BIG}
    delta, new_m, new_v = ({k: updated[k][j].reshape(w[k].shape) for k in BIG} for j in range(3))
    gathered_small = _exchange_wait("small_spread_wait", spread[0], spread[1], spread[2], _small_spread_copies,
                                    updated["w_in"][0])[0]

    like = {k: jax.ShapeDtypeStruct(sm[k].shape, F32) for k in SMALL}
    grads_small = _unpack_small(gathered_small, like)
    conv_q = grads_small["conv_w"].reshape(DEPTH, CONV_WIDTH, N_QUARTERS, D_RNN // N_QUARTERS)
    grads_small["conv_w"] = lax.dynamic_index_in_dim(conv_q, chip, axis=2, keepdims=False)
    outs = _small_adamw_call(*[[_as_rows(d[k]) for k in SMALL] for d in (w, grads_small, m, v)])
    for d, o in zip((delta, new_m, new_v), outs):
        for k, a in zip(SMALL, o):
            d[k] = a.reshape(w[k].shape)

    grads = {**grads_big, **grads_small}
    total = lax.psum(loss[0, 0], ("x", "y", "c"))
    return (total, grad_x[None], *[grads[k] for k in WEIGHTS], *[delta[k] for k in WEIGHTS],
            *[new_m[k] for k in WEIGHTS], *[new_v[k] for k in WEIGHTS])
```

```python
import functools
import math

import jax
import jax.numpy as jnp
from jax import lax
from jax.experimental import pallas as pl
from jax.experimental.pallas import tpu as pltpu

F32 = jnp.float32
BF = jnp.bfloat16

DEPTH = 2
D_MODEL = 1024
D_RNN = 1280
D_SGU = 1024
D_FF = 4096
D_IN = 2 * D_RNN + 2 * D_SGU + 2 * D_MODEL
N_QUARTERS = 4
Q_IN = D_IN // N_QUARTERS
Q_FF = D_FF // N_QUARTERS
RNN_HEADS = 20
RNN_HEAD_DIM = 64
LRU_GROUP = 256
N_LRU_GROUPS = D_RNN // LRU_GROUP
HEADS_PER_GROUP = LRU_GROUP // RNN_HEAD_DIM
CONV_WIDTH = 4
LRU_C = 8.0
SGU_GROUPS = 8
SGU_BLOCK = 128
CHUNK = 64
EPS = 1e-6

ADAM_LR = 0.001
ADAM_B1 = 0.9
ADAM_B2 = 0.999
ADAM_EPS = 1e-08
ADAM_WD = 0.01
ADAM_STEP = 10

SUBLANES = 8
TOKEN_TILE = 512
VMEM_LIMIT_BYTES = 56 * 1024 * 1024

MESH = pl.DeviceIdType.MESH


def _params(semantics=None, vmem=True, **kw):
    return pltpu.CompilerParams(
        dimension_semantics=semantics,
        vmem_limit_bytes=VMEM_LIMIT_BYTES if vmem else None,
        **kw,
    )


def _dot(a, b):
    return jnp.dot(a, b, preferred_element_type=F32)


def _dot_nt(a, b):
    return lax.dot_general(a, b, (((1,), (1,)), ((), ())), preferred_element_type=F32)


def _dot_tn(a, b):
    return lax.dot_general(a, b, (((0,), (0,)), ((), ())), preferred_element_type=F32)


_GELU_C = math.sqrt(2.0 / math.pi)
_GELU_A = 0.044715


def _gelu(x):
    return 0.5 * x * (1.0 + jnp.tanh(_GELU_C * (x + _GELU_A * x * x * x)))


def _gelu_and_grad(x):
    x2 = x * x
    t = jnp.tanh(_GELU_C * (x + _GELU_A * x2 * x))
    du = _GELU_C * (1.0 + 3.0 * _GELU_A * x2)
    return 0.5 * x * (1.0 + t), 0.5 * (1.0 + t) + 0.5 * x * (1.0 - t * t) * du


def _rms_stats(x):
    return lax.rsqrt(jnp.mean(x * x, axis=-1, keepdims=True) + EPS)


def _rms_bwd(dy, x, g):
    rs = _rms_stats(x)
    n = x * rs
    dn = dy * g
    dx = rs * (dn - n * jnp.mean(dn * n, axis=-1, keepdims=True))
    return dx, dy * n


def _row_sum(x):
    return jnp.sum(x, axis=0, keepdims=True)


def _tile_spec(ts, width, col=0):
    return pl.BlockSpec((ts, width), lambda i, col=col: (i, col))


def _full_spec(shape):
    zeros = (0,) * len(shape)
    return pl.BlockSpec(shape, lambda *_: zeros)


def _layer_spec(w, layer):
    zeros = (0,) * (w.ndim - 1)
    return pl.BlockSpec((None,) + tuple(w.shape[1:]), lambda *_: (layer,) + zeros)


def _norm_call(x, g, ts):
    s = x.shape[0]

    def body(x_ref, g_ref, h_ref):
        xv = x_ref[...]
        h_ref[...] = (xv * _rms_stats(xv) * g_ref[...]).astype(BF)

    return pl.pallas_call(
        body, name="norm_fwd", grid=(s // ts,),
        in_specs=[_tile_spec(ts, D_MODEL), _full_spec((1, D_MODEL))],
        out_specs=_tile_spec(ts, D_MODEL),
        out_shape=jax.ShapeDtypeStruct((s, D_MODEL), BF),
        compiler_params=_params(("parallel",)),
    )(x, g)


def _inproj_call(h, w_in, layer, ts):
    s = h.shape[0]

    def body(h_ref, w_ref, o_ref):
        o_ref[...] = _dot(h_ref[...], w_ref[...]).astype(BF)

    return pl.pallas_call(
        body, name="inproj_fwd", grid=(N_QUARTERS, s // ts),
        in_specs=[
            pl.BlockSpec((ts, D_MODEL), lambda q, i: (i, 0)),
            pl.BlockSpec((None, None, D_MODEL, Q_IN), lambda q, i: (layer, q, 0, 0)),
        ],
        out_specs=pl.BlockSpec((ts, Q_IN), lambda q, i: (i, q)),
        out_shape=jax.ShapeDtypeStruct((s, D_IN), BF),
        compiler_params=_params(("parallel", "parallel")),
    )(h, w_in)


def _shift_down(x, tail, s):
    xr = pltpu.roll(x, s, 0)
    tr = pltpu.roll(tail, s, 0)
    row = lax.broadcasted_iota(jnp.int32, tail.shape, 0)
    top = jnp.where(row < s, tr, xr[0:SUBLANES])
    return jnp.concatenate([top, xr[SUBLANES:]], axis=0)


def _shift_up(x, head, s):
    t = x.shape[0]
    xr = pltpu.roll(x, t - s, 0)
    hr = pltpu.roll(head, SUBLANES - s, 0)
    row = lax.broadcasted_iota(jnp.int32, head.shape, 0)
    bottom = jnp.where(row >= SUBLANES - s, hr, xr[t - SUBLANES:])
    return jnp.concatenate([xr[: t - SUBLANES], bottom], axis=0)


def _conv_fwd(x, tail, cw_ref, cb_ref):
    shifted = [x] + [_shift_down(x, tail, s) for s in range(1, CONV_WIDTH)]
    out = cb_ref[...] + cw_ref[CONV_WIDTH - 1:CONV_WIDTH, :] * x
    for s in range(1, CONV_WIDTH):
        k = CONV_WIDTH - 1 - s
        out = out + cw_ref[k:k + 1, :] * shifted[s]
    return out, shifted


def _group_dot(x_bf, w_ref, dot):
    cols = [dot(x_bf[:, g * LRU_GROUP:(g + 1) * LRU_GROUP], w_ref[g]) for g in range(N_LRU_GROUPS)]
    return jnp.concatenate(cols, axis=1)


def _lru_gates(xr, wa_ref, wx_ref, ba_ref, bx_ref, sp_ref):
    xb = xr.astype(BF)
    r = jax.nn.sigmoid(_group_dot(xb, wa_ref, _dot) + ba_ref[...])
    i = jax.nn.sigmoid(_group_dot(xb, wx_ref, _dot) + bx_ref[...])
    log_a = (-LRU_C * r) * sp_ref[...]
    a = jnp.exp(log_a)
    nrm2 = -jnp.tanh(log_a) * (a * a + 1.0)
    inv_nrm = lax.rsqrt(jnp.maximum(nrm2, 1e-36))
    return r, i, a, nrm2 * inv_nrm, inv_nrm


def _linear_scan(a, b, carry, al_ref, bl_ref, h_ref, reverse):
    t, c = a.shape
    rowm = lax.broadcasted_iota(jnp.int32, (t, c), 0) & (SUBLANES - 1)
    for d in (1, 2, 4):
        if reverse:
            keep, sh = rowm < SUBLANES - d, t - d
        else:
            keep, sh = rowm >= d, d
        a_sh = jnp.where(keep, pltpu.roll(a, sh, 0), 1.0)
        b_sh = jnp.where(keep, pltpu.roll(b, sh, 0), 0.0)
        b = a * b_sh + b
        a = a * a_sh
    al_ref[...] = a
    bl_ref[...] = b
    groups = t // SUBLANES

    def step(j, state):
        jj = groups - 1 - j if reverse else j
        off = pl.multiple_of(jj * SUBLANES, SUBLANES)
        rows = bl_ref[pl.ds(off, SUBLANES), :] + al_ref[pl.ds(off, SUBLANES), :] * state
        h_ref[pl.ds(off, SUBLANES), :] = rows
        last = rows[0:1, :] if reverse else rows[SUBLANES - 1:SUBLANES, :]
        return jnp.broadcast_to(last, (SUBLANES, c))

    out = lax.fori_loop(0, groups, step, jnp.broadcast_to(carry, (SUBLANES, c)))
    return out[0:1, :]


def _rnn_fwd_call(proj, wa, wx, ba, bx, sp, cw, cb, ts):
    s = proj.shape[0]

    def body(xg_ref, wa_ref, wx_ref, ba_ref, bx_ref, sp_ref, cw_ref, cb_ref, hr_ref, ya_ref,
             tail_sc, carry_sc, al_sc, bl_sc, h_sc):
        @pl.when(pl.program_id(0) == 0)
        def _():
            tail_sc[...] = jnp.zeros_like(tail_sc)
            carry_sc[...] = jnp.zeros_like(carry_sc)

        x = xg_ref[:, :D_RNN].astype(F32)
        g = xg_ref[:, D_RNN:].astype(F32)
        xr, _ = _conv_fwd(x, tail_sc[...], cw_ref, cb_ref)
        tail_sc[...] = x[ts - SUBLANES:, :]
        _, i, a, nrm, _ = _lru_gates(xr, wa_ref, wx_ref, ba_ref, bx_ref, sp_ref)
        carry_sc[...] = _linear_scan(a, nrm * (i * xr), carry_sc[...], al_sc, bl_sc, h_sc, False)
        h = h_sc[...]
        hr_ref[...] = h.astype(BF)
        ya_ref[...] = (h * _gelu(g)).astype(BF)

    gw = (N_LRU_GROUPS, LRU_GROUP, LRU_GROUP)
    return pl.pallas_call(
        body, name="rnn_fwd", grid=(s // ts,),
        in_specs=[_tile_spec(ts, 2 * D_RNN), _full_spec(gw), _full_spec(gw),
                  _full_spec((1, D_RNN)), _full_spec((1, D_RNN)), _full_spec((1, D_RNN)),
                  _full_spec((CONV_WIDTH, D_RNN)), _full_spec((1, D_RNN))],
        out_specs=[_tile_spec(ts, D_RNN), _tile_spec(ts, D_RNN)],
        out_shape=[jax.ShapeDtypeStruct((s, D_RNN), BF), jax.ShapeDtypeStruct((s, D_RNN), BF)],
        scratch_shapes=[pltpu.VMEM((SUBLANES, D_RNN), F32), pltpu.VMEM((1, D_RNN), F32),
                        pltpu.VMEM((ts, D_RNN), F32), pltpu.VMEM((ts, D_RNN), F32),
                        pltpu.VMEM((ts, D_RNN), F32)],
        compiler_params=_params(("arbitrary",)),
    )(proj, wa, wx, ba, bx, sp, cw, cb)


def _layernorm_fwd(x):
    mu = jnp.mean(x, axis=-1, keepdims=True)
    xc = x - mu
    rstd = lax.rsqrt(jnp.mean(xc * xc, axis=-1, keepdims=True) + EPS)
    return xc * rstd, rstd


def _sgu_mix(vn_bf, wm_ref, bsb_ref, ts):
    rows = []
    for blk in range(ts // SGU_BLOCK):
        r0 = blk * SGU_BLOCK
        cols = [
            _dot(wm_ref[g], vn_bf[r0:r0 + SGU_BLOCK, g * SGU_BLOCK:(g + 1) * SGU_BLOCK]) + bsb_ref[g]
            for g in range(SGU_GROUPS)
        ]
        rows.append(jnp.concatenate(cols, axis=1))
    return jnp.concatenate(rows, axis=0)


def _sgu_fwd_call(proj, wm, bsb, lg, lb, ts):
    s = proj.shape[0]

    def body(uv_ref, wm_ref, bsb_ref, lg_ref, lb_ref, yb_ref):
        gu = _gelu(uv_ref[:, :D_SGU].astype(F32))
        gv = _gelu(uv_ref[:, D_SGU:2 * D_SGU].astype(F32))
        nh, _ = _layernorm_fwd(gv)
        vn = (nh * lg_ref[...] + lb_ref[...]).astype(BF)
        yb_ref[...] = (gu * _sgu_mix(vn, wm_ref, bsb_ref, ts)).astype(BF)

    sw = (SGU_GROUPS, SGU_BLOCK, SGU_BLOCK)
    return pl.pallas_call(
        body, name="sgu_fwd", grid=(s // ts,),
        in_specs=[_tile_spec(ts, 2 * D_RNN, 1), _full_spec(sw), _full_spec(sw),
                  _full_spec((1, D_SGU)), _full_spec((1, D_SGU))],
        out_specs=_tile_spec(ts, D_SGU),
        out_shape=jax.ShapeDtypeStruct((s, D_SGU), BF),
        compiler_params=_params(("parallel",)),
    )(proj, wm, bsb, lg, lb)


_GATE_COL0 = (2 * D_RNN + 2 * D_SGU) // 512


def _gate_specs(ts):
    return [_tile_spec(ts, 512, _GATE_COL0 + j) for j in range(4)]


def _merge_call(x, proj, ya_pre, yb_pre, w_ba, w_bb, w_out, g2, layer, ts):
    s = x.shape[0]

    def body(x_ref, ga0, ga1, gb0, gb1, ya_ref, yb_ref, wa_ref, wb_ref, wo_ref, g2_ref,
             x1_ref, yao_ref, ybo_ref, mg_ref, h2_ref):
        ya = _dot(ya_ref[...], wa_ref[...])
        yb = _dot(yb_ref[...], wb_ref[...])
        sa = jax.nn.sigmoid(jnp.concatenate([ga0[...], ga1[...]], axis=1).astype(F32))
        sb = jax.nn.sigmoid(jnp.concatenate([gb0[...], gb1[...]], axis=1).astype(F32))
        merged = (sa * ya + sb * yb).astype(BF)
        x1 = x_ref[...] + _dot(merged, wo_ref[...])
        x1_ref[...] = x1
        yao_ref[...] = ya.astype(BF)
        ybo_ref[...] = yb.astype(BF)
        mg_ref[...] = merged
        h2_ref[...] = (x1 * _rms_stats(x1) * g2_ref[...]).astype(BF)

    act = jax.ShapeDtypeStruct((s, D_MODEL), BF)
    return pl.pallas_call(
        body, name="merge_fwd", grid=(s // ts,),
        in_specs=[_tile_spec(ts, D_MODEL)] + _gate_specs(ts) + [
            _tile_spec(ts, D_RNN), _tile_spec(ts, D_SGU),
            _layer_spec(w_ba, layer), _layer_spec(w_bb, layer), _layer_spec(w_out, layer),
            _full_spec((1, D_MODEL))],
        out_specs=[_tile_spec(ts, D_MODEL)] * 5,
        out_shape=[jax.ShapeDtypeStruct((s, D_MODEL), F32), act, act, act, act],
        compiler_params=_params(("parallel",)),
    )(x, proj, proj, proj, proj, ya_pre, yb_pre, w_ba, w_bb, w_out, g2)


def _ffn_call(x1, h2, w_up, w_down, layer, ts):
    s = x1.shape[0]

    def body(x1_ref, h2_ref, wu_ref, wd_ref, x2_ref, p_ref):
        h2v = h2_ref[...]
        acc = x1_ref[...]
        for q in range(N_QUARTERS):
            p = _dot(h2v, wu_ref[q])
            p_ref[:, q * Q_FF:(q + 1) * Q_FF] = p.astype(BF)
            f = jnp.square(jnp.maximum(p, 0.0)).astype(BF)
            acc = acc + _dot(f, wd_ref[q * Q_FF:(q + 1) * Q_FF, :])
        x2_ref[...] = acc

    return pl.pallas_call(
        body, name="ffn_fwd", grid=(s // ts,),
        in_specs=[_tile_spec(ts, D_MODEL), _tile_spec(ts, D_MODEL),
                  pl.BlockSpec((None, N_QUARTERS, D_MODEL, Q_FF), lambda i: (layer, 0, 0, 0)),
                  pl.BlockSpec((None, D_FF, D_MODEL), lambda i: (layer, 0, 0))],
        out_specs=[_tile_spec(ts, D_MODEL), _tile_spec(ts, D_FF)],
        out_shape=[jax.ShapeDtypeStruct((s, D_MODEL), F32), jax.ShapeDtypeStruct((s, D_FF), BF)],
        compiler_params=_params(("parallel",)),
    )(x1, h2, w_up, w_down)


def _loss_call(x, target, gf, ts):
    s = x.shape[0]

    def body(x_ref, t_ref, g_ref, dx_ref, loss_ref, dg_ref):
        @pl.when(pl.program_id(0) == 0)
        def _():
            loss_ref[...] = jnp.zeros_like(loss_ref)
            dg_ref[...] = jnp.zeros_like(dg_ref)

        xv = x_ref[...]
        gv = g_ref[...]
        err = xv * _rms_stats(xv) * gv - t_ref[...]
        part = 0.5 * jnp.sum(jnp.mean(err * err, axis=-1, keepdims=True), axis=0, keepdims=True)
        loss_ref[...] += jnp.broadcast_to(part, loss_ref.shape)
        dx, dg = _rms_bwd(err * (1.0 / D_MODEL), xv, gv)
        dx_ref[...] = dx
        dg_ref[...] += _row_sum(dg)

    return pl.pallas_call(
        body, name="loss_head", grid=(s // ts,),
        in_specs=[_tile_spec(ts, D_MODEL), _tile_spec(ts, D_MODEL), _full_spec((1, D_MODEL))],
        out_specs=[_tile_spec(ts, D_MODEL), _full_spec((1, 128)), _full_spec((1, D_MODEL))],
        out_shape=[jax.ShapeDtypeStruct((s, D_MODEL), F32), jax.ShapeDtypeStruct((1, 128), F32),
                   jax.ShapeDtypeStruct((1, D_MODEL), F32)],
        compiler_params=_params(("arbitrary",)),
    )(x, target, gf)


def _ffn_bwd_call(dx2, p, x1, g2, w_up, w_down, layer, ts):
    s = dx2.shape[0]

    def body(dx2_ref, p_ref, x1_ref, g2_ref, wu_ref, wd_ref, dx1_ref, dp_ref, dg_ref):
        @pl.when(pl.program_id(0) == 0)
        def _():
            dg_ref[...] = jnp.zeros_like(dg_ref)

        dx2v = dx2_ref[...]
        dyb = dx2v.astype(BF)
        dh2 = jnp.zeros((ts, D_MODEL), F32)
        for q in range(N_QUARTERS):
            cols = slice(q * Q_FF, (q + 1) * Q_FF)
            df = _dot_nt(dyb, wd_ref[cols, :])
            dp = (df * (2.0 * jnp.maximum(p_ref[:, cols].astype(F32), 0.0))).astype(BF)
            dp_ref[:, cols] = dp
            dh2 = dh2 + _dot_nt(dp, wu_ref[q])
        dx, dg = _rms_bwd(dh2, x1_ref[...], g2_ref[...])
        dx1_ref[...] = dx2v + dx
        dg_ref[...] += _row_sum(dg)

    return pl.pallas_call(
        body, name="ffn_bwd", grid=(s // ts,),
        in_specs=[_tile_spec(ts, D_MODEL), _tile_spec(ts, D_FF), _tile_spec(ts, D_MODEL),
                  _full_spec((1, D_MODEL)),
                  pl.BlockSpec((None, N_QUARTERS, D_MODEL, Q_FF), lambda i: (layer, 0, 0, 0)),
                  pl.BlockSpec((None, D_FF, D_MODEL), lambda i: (layer, 0, 0))],
        out_specs=[_tile_spec(ts, D_MODEL), _tile_spec(ts, D_FF), _full_spec((1, D_MODEL))],
        out_shape=[jax.ShapeDtypeStruct((s, D_MODEL), F32), jax.ShapeDtypeStruct((s, D_FF), BF),
                   jax.ShapeDtypeStruct((1, D_MODEL), F32)],
        compiler_params=_params(("arbitrary",)),
    )(dx2, p, x1, g2, w_up, w_down)


def _merge_bwd_call(dx1, proj, ya, yb, w_ba, w_bb, w_out, layer, ts):
    s = dx1.shape[0]

    def body(dx1_ref, ga0, ga1, gb0, gb1, ya_ref, yb_ref, wa_ref, wb_ref, wo_ref,
             dya_ref, dyb_ref, dgate_ref, dyap_ref, dybp_ref):
        dm = _dot_nt(dx1_ref[...].astype(BF), wo_ref[...])
        sa = jax.nn.sigmoid(jnp.concatenate([ga0[...], ga1[...]], axis=1).astype(F32))
        sb = jax.nn.sigmoid(jnp.concatenate([gb0[...], gb1[...]], axis=1).astype(F32))
        dya = (dm * sa).astype(BF)
        dyb = (dm * sb).astype(BF)
        dya_ref[...] = dya
        dyb_ref[...] = dyb
        dgate_ref[:, :D_MODEL] = (dm * ya_ref[...].astype(F32) * sa * (1.0 - sa)).astype(BF)
        dgate_ref[:, D_MODEL:] = (dm * yb_ref[...].astype(F32) * sb * (1.0 - sb)).astype(BF)
        dyap_ref[...] = _dot_nt(dya, wa_ref[...]).astype(BF)
        dybp_ref[...] = _dot_nt(dyb, wb_ref[...]).astype(BF)

    act = jax.ShapeDtypeStruct((s, D_MODEL), BF)
    return pl.pallas_call(
        body, name="merge_bwd", grid=(s // ts,),
        in_specs=[_tile_spec(ts, D_MODEL)] + _gate_specs(ts) + [
            _tile_spec(ts, D_MODEL), _tile_spec(ts, D_MODEL),
            _layer_spec(w_ba, layer), _layer_spec(w_bb, layer), _layer_spec(w_out, layer)],
        out_specs=[_tile_spec(ts, D_MODEL), _tile_spec(ts, D_MODEL), _tile_spec(ts, 2 * D_MODEL),
                   _tile_spec(ts, D_RNN), _tile_spec(ts, D_SGU)],
        out_shape=[act, act, jax.ShapeDtypeStruct((s, 2 * D_MODEL), BF),
                   jax.ShapeDtypeStruct((s, D_RNN), BF), jax.ShapeDtypeStruct((s, D_SGU), BF)],
        compiler_params=_params(("parallel",)),
    )(dx1, proj, proj, proj, proj, ya, yb, w_ba, w_bb, w_out)


def _sgu_bwd_call(dyb_pre, proj, wm, bsb, mask, lg, lb, ts):
    s = proj.shape[0]

    def body(dy_ref, uv_ref, wm_ref, bsb_ref, mask_ref, lg_ref, lb_ref,
             duv_ref, dws_ref, dbs_ref, dlg_ref, dlb_ref, dm_sc):
        step = pl.program_id(0)

        @pl.when(step == 0)
        def _():
            dws_ref[...] = jnp.zeros_like(dws_ref)
            dlg_ref[...] = jnp.zeros_like(dlg_ref)
            dlb_ref[...] = jnp.zeros_like(dlb_ref)
            dm_sc[...] = jnp.zeros_like(dm_sc)

        gu, dgu_du = _gelu_and_grad(uv_ref[:, :D_SGU].astype(F32))
        gv, dgv_dv = _gelu_and_grad(uv_ref[:, D_SGU:2 * D_SGU].astype(F32))
        nh, rstd = _layernorm_fwd(gv)
        lgv = lg_ref[...]
        vn = (nh * lgv + lb_ref[...]).astype(BF)
        dy = dy_ref[...].astype(F32)
        du = dy * _sgu_mix(vn, wm_ref, bsb_ref, ts) * dgu_du
        dmix = dy * gu
        dmix_bf = dmix.astype(BF)
        dm_acc = dm_sc[...]
        rows = []
        for blk in range(ts // SGU_BLOCK):
            r0 = blk * SGU_BLOCK
            dm_acc = dm_acc + dmix[r0:r0 + SGU_BLOCK, :]
            cols = []
            for g in range(SGU_GROUPS):
                c0 = g * SGU_BLOCK
                dmg = dmix_bf[r0:r0 + SGU_BLOCK, c0:c0 + SGU_BLOCK]
                cols.append(_dot_tn(wm_ref[g], dmg))
                dws_ref[g] += mask_ref[...] * _dot_nt(dmg, vn[r0:r0 + SGU_BLOCK, c0:c0 + SGU_BLOCK])
            rows.append(jnp.concatenate(cols, axis=1))
        dm_sc[...] = dm_acc
        dvn = jnp.concatenate(rows, axis=0)
        dlg_ref[...] += _row_sum(dvn * nh)
        dlb_ref[...] += _row_sum(dvn)
        dnh = dvn * lgv
        dgv = rstd * (dnh - jnp.mean(dnh, axis=-1, keepdims=True)
                      - nh * jnp.mean(dnh * nh, axis=-1, keepdims=True))
        duv_ref[:, :D_SGU] = du.astype(BF)
        duv_ref[:, D_SGU:] = (dgv * dgv_dv).astype(BF)

        @pl.when(step == pl.num_programs(0) - 1)
        def _():
            for g in range(SGU_GROUPS):
                dbs_ref[:, g:g + 1] = jnp.sum(
                    dm_acc[:, g * SGU_BLOCK:(g + 1) * SGU_BLOCK], axis=1, keepdims=True)

    sw = (SGU_GROUPS, SGU_BLOCK, SGU_BLOCK)
    return pl.pallas_call(
        body, name="sgu_bwd", grid=(s // ts,),
        in_specs=[_tile_spec(ts, D_SGU), _tile_spec(ts, 2 * D_RNN, 1), _full_spec(sw), _full_spec(sw),
                  _full_spec((SGU_BLOCK, SGU_BLOCK)), _full_spec((1, D_SGU)), _full_spec((1, D_SGU))],
        out_specs=[_tile_spec(ts, 2 * D_SGU), _full_spec(sw), _full_spec((SGU_BLOCK, SGU_GROUPS)),
                   _full_spec((1, D_SGU)), _full_spec((1, D_SGU))],
        out_shape=[jax.ShapeDtypeStruct((s, 2 * D_SGU), BF), jax.ShapeDtypeStruct(sw, F32),
                   jax.ShapeDtypeStruct((SGU_BLOCK, SGU_GROUPS), F32),
                   jax.ShapeDtypeStruct((1, D_SGU), F32), jax.ShapeDtypeStruct((1, D_SGU), F32)],
        scratch_shapes=[pltpu.VMEM((SGU_BLOCK, D_SGU), F32)],
        compiler_params=_params(("arbitrary",)),
    )(dyb_pre, proj, wm, bsb, mask, lg, lb)


_ROW_DBA, _ROW_DBX, _ROW_DSP, _ROW_DCB, _ROW_DCW = 0, 1, 2, 3, 4
_PREV_ROWS = 16


def _rnn_bwd_call(dya_pre, proj, hr, wa, wx, ba, bx, sp, cw, cb, ts):
    s = proj.shape[0]
    nt = s // ts
    per = ts // _PREV_ROWS

    def tile(i):
        return nt - 1 - i

    def prev(i):
        return jnp.maximum(tile(i) * per - 1, 0)

    def body(dy_ref, xg_ref, xgp_ref, hr_ref, hrp_ref, wa_ref, wx_ref, ba_ref, bx_ref, sp_ref,
             cw_ref, cb_ref, dxg_ref, dwa_ref, dwx_ref, vec_ref,
             lam_carry, a_first, dxr_head, al_sc, bl_sc, lam_sc):
        step = pl.program_id(0)

        @pl.when(step == 0)
        def _():
            dwa_ref[...] = jnp.zeros_like(dwa_ref)
            dwx_ref[...] = jnp.zeros_like(dwx_ref)
            vec_ref[...] = jnp.zeros_like(vec_ref)
            lam_carry[...] = jnp.zeros_like(lam_carry)
            a_first[...] = jnp.zeros_like(a_first)
            dxr_head[...] = jnp.zeros_like(dxr_head)

        has_prev = (step < nt - 1).astype(F32)
        x = xg_ref[:, :D_RNN].astype(F32)
        g = xg_ref[:, D_RNN:].astype(F32)
        x_tail = xgp_ref[_PREV_ROWS - SUBLANES:, :D_RNN].astype(F32) * has_prev
        h_tail = hrp_ref[_PREV_ROWS - SUBLANES:, :].astype(F32) * has_prev
        xr, x_shifted = _conv_fwd(x, x_tail, cw_ref, cb_ref)
        r, i, a, nrm, inv_nrm = _lru_gates(xr, wa_ref, wx_ref, ba_ref, bx_ref, sp_ref)
        h = hr_ref[...].astype(F32)
        dy = dy_ref[...].astype(F32)
        gg, dgg = _gelu_and_grad(g)

        coef = _shift_up(a, jnp.broadcast_to(a_first[...], (SUBLANES, D_RNN)), 1)
        lam_carry[...] = _linear_scan(coef, dy * gg, lam_carry[...], al_sc, bl_sc, lam_sc, True)
        a_first[...] = a[0:1, :]
        lam = lam_sc[...]

        da = lam * _shift_down(h, h_tail, 1)
        dnrm = lam * (i * xr)
        di = lam * nrm * xr
        dlog_a = da * a - dnrm * (a * a) * inv_nrm
        spv = sp_ref[...]
        dza = (dlog_a * (-LRU_C * spv)) * (r * (1.0 - r))
        dzx = di * (i * (1.0 - i))
        vec_ref[_ROW_DSP:_ROW_DSP + 1, :] += _row_sum(dlog_a * (-LRU_C * r))
        vec_ref[_ROW_DBA:_ROW_DBA + 1, :] += _row_sum(dza)
        vec_ref[_ROW_DBX:_ROW_DBX + 1, :] += _row_sum(dzx)
        xb = xr.astype(BF)
        dza_bf = dza.astype(BF)
        dzx_bf = dzx.astype(BF)
        for grp in range(N_LRU_GROUPS):
            cols = slice(grp * LRU_GROUP, (grp + 1) * LRU_GROUP)
            dwa_ref[grp] += _dot_tn(xb[:, cols], dza_bf[:, cols])
            dwx_ref[grp] += _dot_tn(xb[:, cols], dzx_bf[:, cols])
        dxr = (lam * nrm * i + _group_dot(dza_bf, wa_ref, _dot_nt) + _group_dot(dzx_bf, wx_ref, _dot_nt))

        vec_ref[_ROW_DCB:_ROW_DCB + 1, :] += _row_sum(dxr)
        head = dxr_head[...]
        dx = cw_ref[CONV_WIDTH - 1:CONV_WIDTH, :] * dxr
        vec_ref[_ROW_DCW + 3:_ROW_DCW + 4, :] += _row_sum(dxr * x)
        for sft in range(1, CONV_WIDTH):
            k = CONV_WIDTH - 1 - sft
            dx = dx + cw_ref[k:k + 1, :] * _shift_up(dxr, head, sft)
            vec_ref[_ROW_DCW + k:_ROW_DCW + k + 1, :] += _row_sum(dxr * x_shifted[sft])
        dxr_head[...] = dxr[0:SUBLANES, :]
        dxg_ref[:, :D_RNN] = dx.astype(BF)
        dxg_ref[:, D_RNN:] = (dy * h * dgg).astype(BF)

    gw = (N_LRU_GROUPS, LRU_GROUP, LRU_GROUP)
    rev = lambda width: pl.BlockSpec((ts, width), lambda i: (tile(i), 0))
    return pl.pallas_call(
        body, name="rnn_bwd", grid=(nt,),
        in_specs=[rev(D_RNN), rev(2 * D_RNN),
                  pl.BlockSpec((_PREV_ROWS, 2 * D_RNN), lambda i: (prev(i), 0)),
                  rev(D_RNN),
                  pl.BlockSpec((_PREV_ROWS, D_RNN), lambda i: (prev(i), 0)),
                  _full_spec(gw), _full_spec(gw),
                  _full_spec((1, D_RNN)), _full_spec((1, D_RNN)), _full_spec((1, D_RNN)),
                  _full_spec((CONV_WIDTH, D_RNN)), _full_spec((1, D_RNN))],
        out_specs=[rev(2 * D_RNN), _full_spec(gw), _full_spec(gw), _full_spec((SUBLANES, D_RNN))],
        out_shape=[jax.ShapeDtypeStruct((s, 2 * D_RNN), BF), jax.ShapeDtypeStruct(gw, F32),
                   jax.ShapeDtypeStruct(gw, F32), jax.ShapeDtypeStruct((SUBLANES, D_RNN), F32)],
        scratch_shapes=[pltpu.VMEM((1, D_RNN), F32), pltpu.VMEM((1, D_RNN), F32),
                        pltpu.VMEM((SUBLANES, D_RNN), F32),
                        pltpu.VMEM((ts, D_RNN), F32), pltpu.VMEM((ts, D_RNN), F32),
                        pltpu.VMEM((ts, D_RNN), F32)],
        compiler_params=_params(("arbitrary",)),
    )(dya_pre, proj, proj, hr, hr, wa, wx, ba, bx, sp, cw, cb)


def _inproj_bwd_call(dxg, duv, dgate, dx1, x, g1, w_in, layer, ts):
    s = x.shape[0]

    def body(dxg_ref, duv_ref, dgt_ref, dx1_ref, x_ref, g_ref, w_ref, dx_ref, dproj_ref, dg_ref):
        @pl.when(pl.program_id(0) == 0)
        def _():
            dg_ref[...] = jnp.zeros_like(dg_ref)

        dproj = jnp.concatenate([dxg_ref[...], duv_ref[...], dgt_ref[...]], axis=1)
        dproj_ref[...] = dproj
        dh = jnp.zeros((ts, D_MODEL), F32)
        for q in range(N_QUARTERS):
            dh = dh + _dot_nt(dproj[:, q * Q_IN:(q + 1) * Q_IN], w_ref[q])
        dx, dg = _rms_bwd(dh, x_ref[...], g_ref[...])
        dx_ref[...] = dx1_ref[...] + dx
        dg_ref[...] += _row_sum(dg)

    return pl.pallas_call(
        body, name="inproj_bwd", grid=(s // ts,),
        in_specs=[_tile_spec(ts, 2 * D_RNN), _tile_spec(ts, 2 * D_SGU), _tile_spec(ts, 2 * D_MODEL),
                  _tile_spec(ts, D_MODEL), _tile_spec(ts, D_MODEL), _full_spec((1, D_MODEL)),
                  pl.BlockSpec((None, N_QUARTERS, D_MODEL, Q_IN), lambda i: (layer, 0, 0, 0))],
        out_specs=[_tile_spec(ts, D_MODEL), _tile_spec(ts, D_IN), _full_spec((1, D_MODEL))],
        out_shape=[jax.ShapeDtypeStruct((s, D_MODEL), F32), jax.ShapeDtypeStruct((s, D_IN), BF),
                   jax.ShapeDtypeStruct((1, D_MODEL), F32)],
        compiler_params=_params(("arbitrary",)),
    )(dxg, duv, dgate, dx1, x, g1, w_in)


def _relu_sq(p):
    return jnp.square(jnp.maximum(p.astype(F32), 0.0))


def _wgrad_call(a, b, core, tm, tn, tk, col_blocked, name, a_fn=None):
    s, m = a.shape
    n = b.shape[1]
    r, cols = (m, n // N_QUARTERS) if col_blocked else (m // N_QUARTERS, n)
    r2 = r // 2
    per_tile = tm // r
    steps = s // tk

    def body(core_ref, a_ref, b_ref, keep_ref, send_ref, *acc):
        av = a_ref[...]
        if a_fn is not None:
            av = a_fn(av)
        prod = _dot_tn(av.astype(BF), b_ref[...].astype(BF))

        def emit(total):
            for h in range(2):
                @pl.when(core_ref[0] == h)
                def _():
                    for q in range(per_tile):
                        keep_ref[q] = total[q * r + h * r2:q * r + (h + 1) * r2]
                        send_ref[q] = total[q * r + (1 - h) * r2:q * r + (2 - h) * r2].astype(BF)

        if steps == 1:
            emit(prod)
        else:
            acc_ref, = acc
            step = pl.program_id(2)

            @pl.when(step == 0)
            def _():
                acc_ref[...] = prod

            @pl.when(jnp.logical_and(step > 0, step < steps - 1))
            def _():
                acc_ref[...] += prod

            @pl.when(step == steps - 1)
            def _():
                emit(acc_ref[...] + prod)

    if col_blocked:
        per_q = cols // tn
        out_spec = pl.BlockSpec((1, r2, tn), lambda i, j, k, c: (j // per_q, 0, j % per_q))
    else:
        out_spec = pl.BlockSpec((per_tile, r2, tn), lambda i, j, k, c: (i, 0, j))
    return pl.pallas_call(
        body, name=name,
        out_shape=[jax.ShapeDtypeStruct((N_QUARTERS, r2, cols), F32),
                   jax.ShapeDtypeStruct((N_QUARTERS, r2, cols), BF)],
        grid_spec=pltpu.PrefetchScalarGridSpec(
            num_scalar_prefetch=1, grid=(m // tm, n // tn, steps),
            in_specs=[pl.BlockSpec((tk, tm), lambda i, j, k, c: (k, i)),
                      pl.BlockSpec((tk, tn), lambda i, j, k, c: (k, j))],
            out_specs=[out_spec, out_spec],
            scratch_shapes=[] if steps == 1 else [pltpu.VMEM((tm, tn), F32)]),
        compiler_params=_params(("parallel", "parallel", "arbitrary")),
    )(core, a, b)


BIG = ("w_in", "w_up", "w_down", "w_branch_a", "w_branch_b", "w_out")


def _block_diag(w):
    w4 = w.reshape(N_LRU_GROUPS, HEADS_PER_GROUP, RNN_HEAD_DIM, RNN_HEAD_DIM)
    eye = jnp.eye(HEADS_PER_GROUP, dtype=w.dtype)
    return jnp.einsum("gjio,jk->gjiko", w4, eye).reshape(N_LRU_GROUPS, LRU_GROUP, LRU_GROUP)


def _block_diag_extract(d):
    d5 = d.reshape(N_LRU_GROUPS, HEADS_PER_GROUP, RNN_HEAD_DIM, HEADS_PER_GROUP, RNN_HEAD_DIM)
    blocks = [d5[:, j, :, j, :] for j in range(HEADS_PER_GROUP)]
    return jnp.stack(blocks, axis=1).reshape(RNN_HEADS, RNN_HEAD_DIM, RNN_HEAD_DIM)


def _sgu_mask():
    chunk = jnp.arange(SGU_BLOCK) // CHUNK
    return (chunk[:, None] >= chunk[None, :]).astype(F32)


def _layer_small(sm, l, core):
    row = lambda v: v.reshape(1, -1)
    return dict(
        core=core,
        g1=row(sm["norm_mix_g"][l]), g2=row(sm["norm_ffn_g"][l]),
        wa=_block_diag(sm["lru_w_a"][l]).astype(BF), wx=_block_diag(sm["lru_w_x"][l]).astype(BF),
        ba=row(sm["lru_b_a"][l]), bx=row(sm["lru_b_x"][l]),
        sp=row(jax.nn.softplus(-sm["lru_lambda"][l])),
        cw=sm["conv_w"][l], cb=row(sm["conv_b"][l]),
        wm=(sm["sgu_w_s"][l] * _sgu_mask()).astype(BF),
        bsb=jnp.broadcast_to(sm["sgu_b_s"][l][:, :, None], (SGU_GROUPS, SGU_BLOCK, SGU_BLOCK)),
        lg=row(sm["sgu_ln_g"][l]), lb=row(sm["sgu_ln_b"][l]),
    )


def _layer_fwd_mix(x, big, p, ts):
    h = _norm_call(x, p["g1"], ts)
    proj = _inproj_call(h, big["w_in"], 0, 2 * ts)
    hr, ya_pre = _rnn_fwd_call(proj, p["wa"], p["wx"], p["ba"], p["bx"], p["sp"], p["cw"], p["cb"], ts)
    yb_pre = _sgu_fwd_call(proj, p["wm"], p["bsb"], p["lg"], p["lb"], ts)
    return dict(p=p, x=x, h=h, proj=proj, hr=hr, ya_pre=ya_pre, yb_pre=yb_pre)


def _layer_fwd_out(sv, big, ts):
    x1, ya, yb, merged, h2 = _merge_call(sv["x"], sv["proj"], sv["ya_pre"], sv["yb_pre"], big["w_branch_a"],
                                         big["w_branch_b"], big["w_out"], sv["p"]["g2"], 0, ts)
    x2, pre = _ffn_call(x1, h2, big["w_up"], big["w_down"], 0, ts)
    sv.update(x1=x1, ya=ya, yb=yb, merged=merged, h2=h2, pre=pre)
    return x2


def _layer_bwd_ffn(dx, sv, big, ts):
    p = sv["p"]
    dx1, dpre, dg2 = _ffn_bwd_call(dx, sv["pre"], sv["x1"], p["g2"], big["w_up"], big["w_down"], 0, ts)
    tk = dx.shape[0]
    gb = dict(
        w_down=_wgrad_call(sv["pre"], dx, p["core"], Q_FF, D_MODEL // 2, tk, False, "wgrad_down", a_fn=_relu_sq),
        w_up=_wgrad_call(sv["h2"], dpre, p["core"], D_MODEL, Q_FF, tk, True, "wgrad_up"))
    return dx1, gb, dict(norm_ffn_g=dg2[0])


def _layer_bwd_merge(dx1, sv, big, ts):
    tk = dx1.shape[0]
    core = sv["p"]["core"]
    dya, dyb, dgate, dya_pre, dyb_pre = _merge_bwd_call(
        dx1, sv["proj"], sv["ya"], sv["yb"], big["w_branch_a"], big["w_branch_b"], big["w_out"], 0, ts)
    gb = dict(
        w_out=_wgrad_call(sv["merged"], dx1, core, D_MODEL, D_MODEL // 2, tk, False, "wgrad_out"),
        w_branch_a=_wgrad_call(sv["ya_pre"], dya, core, D_RNN, D_MODEL // 2, tk, False, "wgrad_branch_a"),
        w_branch_b=_wgrad_call(sv["yb_pre"], dyb, core, D_SGU, D_MODEL // 2, tk, False, "wgrad_branch_b"))
    return (dgate, dya_pre, dyb_pre), gb


def _layer_bwd_branches(dx1, merge_out, sv, big, lam, ts):
    p = sv["p"]
    tk = dx1.shape[0]
    dgate, dya_pre, dyb_pre = merge_out
    gb = {}
    duv, dws, dbs, dlg, dlb = _sgu_bwd_call(dyb_pre, sv["proj"], p["wm"], p["bsb"], _sgu_mask(), p["lg"], p["lb"],
                                            ts)
    dxg, dwa, dwx, vec = _rnn_bwd_call(dya_pre, sv["proj"], sv["hr"], p["wa"], p["wx"], p["ba"], p["bx"],
                                       p["sp"], p["cw"], p["cb"], ts // 2)
    dx, dproj, dg1 = _inproj_bwd_call(dxg, duv, dgate, dx1, sv["x"], p["g1"], big["w_in"], 0, ts)
    gb["w_in"] = _wgrad_call(sv["h"], dproj, p["core"], D_MODEL, Q_IN, tk // 2, True, "wgrad_in")
    gs = dict(
        norm_mix_g=dg1[0], conv_w=vec[_ROW_DCW:_ROW_DCW + CONV_WIDTH], conv_b=vec[_ROW_DCB],
        lru_w_a=_block_diag_extract(dwa), lru_w_x=_block_diag_extract(dwx),
        lru_b_a=vec[_ROW_DBA].reshape(RNN_HEADS, RNN_HEAD_DIM), lru_b_x=vec[_ROW_DBX].reshape(RNN_HEADS, RNN_HEAD_DIM),
        lru_lambda=-vec[_ROW_DSP] * jax.nn.sigmoid(-lam),
        sgu_ln_g=dlg[0], sgu_ln_b=dlb[0], sgu_w_s=dws, sgu_b_s=dbs.T)
    return dx, gb, gs


def _local_step(x, target, big, sm, ts):
    saved = []
    core = jnp.zeros((1,), jnp.int32)
    for l in range(DEPTH):
        sv = _layer_fwd_mix(x, big[l], _layer_small(sm, l, core), ts)
        x = _layer_fwd_out(sv, big[l], ts)
        saved.append(sv)
    dx, loss, dgf = _loss_call(x, target, sm["final_norm_g"].reshape(1, -1), ts)
    gb, gs = [None] * DEPTH, [None] * DEPTH
    for l in reversed(range(DEPTH)):
        dx1, gb_ffn, gs_ffn = _layer_bwd_ffn(dx, saved[l], big[l], ts)
        merge_out, gb_merge = _layer_bwd_merge(dx1, saved[l], big[l], ts)
        dx, gb_mix, gs_mix = _layer_bwd_branches(dx1, merge_out, saved[l], big[l], sm["lru_lambda"][l], ts)
        gb[l] = {**gb_ffn, **gb_merge, **gb_mix}
        gs[l] = {**gs_ffn, **gs_mix}
    gs = {k: jnp.stack([g[k] for g in gs]) for k in gs[0]}
    gs["final_norm_g"] = dgf[0]
    return loss, dx, gb, gs


EW_VMEM_BYTES = 24 * 1024 * 1024


def _row_block(rows, cols, bytes_per_elem):
    for br in range(min(rows, EW_VMEM_BYTES // (2 * bytes_per_elem * cols)), 0, -1):
        if rows % br == 0 and br % 16 == 0:
            return br
    return rows


def _ew_call(fn, name, operands, outputs, slabs=1, sel=None, into=None, after=None):
    if into is not None and not isinstance(into, (list, tuple)):
        into = [into]
    rows, cols = outputs[0][0].shape[2:]
    br = _row_block(rows, cols, sum(jnp.dtype(a.dtype).itemsize for a, _ in operands + outputs))
    n_in = len(operands)

    def pick(tok, g, s):
        if callable(tok):
            return tok(g, s)
        if tok == "g":
            return g
        if isinstance(tok, tuple):
            return s[tok[1]]
        return tok

    def spec(idx):
        return pl.BlockSpec((None, None, br, cols),
                            lambda g, i, s, idx=idx: (pick(idx[0], g, s), pick(idx[1], g, s), i, 0))

    if sel is None:
        sel = jnp.zeros((1,), jnp.int32)
    in_specs = [spec(idx) for _, idx in operands]
    arrays = [a for a, _ in operands]
    aliases = {}
    for j, buf in enumerate(into or ()):
        in_specs.append(pl.BlockSpec(memory_space=pl.ANY))
        arrays.append(buf)
        aliases[1 + n_in + j] = j
    if after is not None:
        in_specs.append(pl.BlockSpec(memory_space=pl.ANY))
        arrays.append(after)

    def body(sel_ref, *refs):
        outs = fn(*[r[...] for r in refs[:n_in]])
        for o_ref, o in zip(refs[len(arrays):], outs):
            o_ref[...] = o.astype(o_ref.dtype)

    return pl.pallas_call(
        body, name=name, out_shape=[s for s, _ in outputs],
        grid_spec=pltpu.PrefetchScalarGridSpec(
            num_scalar_prefetch=1, grid=(slabs, rows // br),
            in_specs=in_specs,
            out_specs=[spec(idx) for _, idx in outputs]),
        input_output_aliases=aliases,
        compiler_params=_params(("parallel", "parallel")),
    )(sel, *arrays)


def _as4(a):
    return a.reshape((1,) * (4 - a.ndim) + a.shape)


def _adamw(w, g, m, v):
    m = ADAM_B1 * m + (1.0 - ADAM_B1) * g
    v = ADAM_B2 * v + (1.0 - ADAM_B2) * jnp.square(g)
    m_hat = m / (1.0 - ADAM_B1 ** ADAM_STEP)
    v_hat = v / (1.0 - ADAM_B2 ** ADAM_STEP)
    delta = -ADAM_LR * (m_hat / (jnp.sqrt(v_hat) + ADAM_EPS) + ADAM_WD * w)
    return delta, m, v


def _small_adamw_call(ws, gs, ms, vs):
    n = len(ws)

    def body(*refs):
        for k in range(n):
            w, g, m, v = (refs[j * n + k][...] for j in range(4))
            outs = _adamw(w, g, m, v)
            for j in range(3):
                refs[(4 + j) * n + k][...] = outs[j]

    shapes = [jax.ShapeDtypeStruct(w.shape, F32) for w in ws]
    outs = pl.pallas_call(
        body, name="adamw_small", out_shape=shapes * 3,
        in_specs=[pl.BlockSpec(memory_space=pltpu.VMEM)] * (4 * n),
        out_specs=[pl.BlockSpec(memory_space=pltpu.VMEM)] * (3 * n),
        compiler_params=_params(),
    )(*ws, *gs, *ms, *vs)
    return outs[:n], outs[n:2 * n], outs[2 * n:]


ANY = pl.BlockSpec(memory_space=pl.ANY)


def _place():
    x, y, c = lax.axis_index("x"), lax.axis_index("y"), lax.axis_index("c")
    chips = [(1 - x, y), (x, 1 - y), (1 - x, 1 - y)]
    return x, y, c, chips


def _remote(src, dst, send_sem, recv_sem, to):
    return pltpu.make_async_remote_copy(src_ref=src, dst_ref=dst, send_sem=send_sem, recv_sem=recv_sem,
                                        device_id=to, device_id_type=MESH)


def _gather_call(bufs):
    n = len(bufs)

    def body(*refs):
        out = refs[n:2 * n]
        send_sems, recv_sems = refs[2 * n:]
        x, y, c, chips = _place()
        me_q = 2 * x + y
        sibling = (x, y, 1 - c)
        first = []
        for w in range(n):
            for j, chip in enumerate(chips):
                mine = out[w].at[c, me_q]
                first.append(_remote(mine, mine, send_sems.at[w * 3 + j], recv_sems.at[w * 3 + j], (*chip, c)))
        for cp in first:
            cp.start()
        passed = []
        for w in range(n):
            for j, (qx, qy) in enumerate(chips):
                landed = out[w].at[c, 2 * qx + qy]
                k = w * 3 + j
                _remote(landed, landed, send_sems.at[k], recv_sems.at[k], (qx, qy, c)).wait_recv()
                cp = _remote(landed, landed, send_sems.at[3 * n + k], recv_sems.at[3 * n + k], sibling)
                cp.start()
                passed.append(cp)
        for w in range(n):
            for j, (qx, qy) in enumerate(chips):
                landed = out[w].at[1 - c, 2 * qx + qy]
                k = 3 * n + w * 3 + j
                _remote(landed, landed, send_sems.at[k], recv_sems.at[k], sibling).wait_recv()
        for cp in first + passed:
            cp.wait_send()

    return pl.pallas_call(
        body, name="gather_weights",
        out_shape=[jax.ShapeDtypeStruct(a.shape, a.dtype) for a in bufs],
        in_specs=[ANY] * n, out_specs=[ANY] * n,
        input_output_aliases={w: w for w in range(n)},
        scratch_shapes=[pltpu.SemaphoreType.DMA((6 * n,)), pltpu.SemaphoreType.DMA((6 * n,))],
        compiler_params=_params(vmem=False, has_side_effects=True),
    )(*bufs)


def _sibling_send_call(items):
    n = len(items)

    def body(*refs):
        src, out = refs[:n], refs[n:2 * n]
        send_sems, recv_sems = refs[2 * n:]
        x, y, c, _ = _place()
        copies = [_remote(src[w], out[w], send_sems.at[w], recv_sems.at[w], (x, y, 1 - c)) for w in range(n)]
        for cp in copies:
            cp.start()
        for cp in copies:
            cp.wait()

    return pl.pallas_call(
        body, name="grads_to_sibling",
        out_shape=[jax.ShapeDtypeStruct(a.shape, a.dtype) for a in items],
        in_specs=[ANY] * n, out_specs=[ANY] * n,
        scratch_shapes=[pltpu.SemaphoreType.DMA((n,)), pltpu.SemaphoreType.DMA((n,))],
        compiler_params=_params(vmem=False, has_side_effects=True),
    )(*items)


def _sibling_inplace_call(name, bufs, slabs, n_pairs):
    n = len(bufs)

    def body(*refs):
        out = refs[n:2 * n]
        send_sems, recv_sems = refs[2 * n:]
        x, y, c, _ = _place()
        sibling = (x, y, 1 - c)
        pairs = [pair for w, ref in enumerate(out) for pair in slabs(ref, c, w)]
        sends = [_remote(s, s, send_sems.at[k], recv_sems.at[k], sibling) for k, (s, _) in enumerate(pairs)]
        for cp in sends:
            cp.start()
        for k, (_, r) in enumerate(pairs):
            _remote(r, r, send_sems.at[k], recv_sems.at[k], sibling).wait_recv()
        for cp in sends:
            cp.wait_send()

    return pl.pallas_call(
        body, name=name,
        out_shape=[jax.ShapeDtypeStruct(a.shape, a.dtype) for a in bufs],
        in_specs=[ANY] * n, out_specs=[ANY] * n,
        input_output_aliases={w: w for w in range(n)},
        scratch_shapes=[pltpu.SemaphoreType.DMA((n_pairs,)), pltpu.SemaphoreType.DMA((n_pairs,))],
        compiler_params=_params(vmem=False, has_side_effects=True),
    )(*bufs)


HBM_SPEC = pl.BlockSpec(memory_space=pltpu.HBM)
SEM_SPEC = pl.BlockSpec(memory_space=pltpu.SEMAPHORE)
DATAFLOW_EFFECT = pltpu.SideEffectType.DATAFLOW_SIDE_EFFECTING


def _exchange_start(name, bufs, copies, n_copies, after):
    n = len(bufs)

    def body(*refs):
        ins, send_sems, recv_sems, token = refs[:n], refs[n + 1], refs[n + 2], refs[-1]
        for k, (src, dst, to) in enumerate(copies(ins)):
            _remote(src, dst, send_sems.at[k], recv_sems.at[k], to).start()
        token[...] = jnp.zeros_like(token)

    outs = pl.pallas_call(
        body, name=name,
        out_shape=(pltpu.SemaphoreType.DMA((n_copies,)), pltpu.SemaphoreType.DMA((n_copies,)),
                   *[pltpu.HBM(b.shape, b.dtype) for b in bufs], jax.ShapeDtypeStruct((SUBLANES, 128), F32)),
        in_specs=[HBM_SPEC] * n + [ANY],
        out_specs=(SEM_SPEC, SEM_SPEC, *[HBM_SPEC] * n, pl.BlockSpec(memory_space=pltpu.VMEM)),
        input_output_aliases={w: w + 2 for w in range(n)},
        compiler_params=pltpu.CompilerParams(has_side_effects=DATAFLOW_EFFECT),
    )(*[pltpu.with_memory_space_constraint(b, pltpu.HBM) for b in bufs], after)
    return outs[0], outs[1], list(outs[2:2 + n]), outs[-1]


def _exchange_wait(name, send_sems, recv_sems, bufs, copies, after):
    n = len(bufs)

    def body(*refs):
        ins, send_sems, recv_sems = refs[:n], refs[n], refs[n + 1]
        for k, (src, dst, to) in enumerate(copies(ins)):
            cp = _remote(src, dst, send_sems.at[k], recv_sems.at[k], to)
            cp.wait_send()
            cp.wait_recv()

    return pl.pallas_call(
        body, name=name,
        out_shape=[pltpu.HBM(b.shape, b.dtype) for b in bufs],
        in_specs=[HBM_SPEC] * n + [SEM_SPEC, SEM_SPEC, ANY],
        out_specs=[HBM_SPEC] * n,
        input_output_aliases={w: w for w in range(n)},
        compiler_params=pltpu.CompilerParams(has_side_effects=DATAFLOW_EFFECT),
    )(*bufs, send_sems, recv_sems, after)


def _gather_copies(refs):
    x, y, c, chips = _place()
    mine = 2 * (2 * x + y) + c
    return [(ref.at[mine], ref.at[mine], (qx, qy, c)) for ref in refs for qx, qy in chips]


def _gather_forward_slabs(ref, c, w):
    x, y, _, chips = _place()
    return [(ref.at[2 * (2 * qx + qy) + c], ref.at[2 * (2 * qx + qy) + 1 - c]) for qx, qy in chips]


def _device_peers():
    x, y, c, _ = _place()
    return 4 * x + 2 * y + c, [(k, (x ^ ((k >> 2) & 1), y ^ ((k >> 1) & 1), c ^ (k & 1))) for k in range(1, 8)]


def _small_scatter_copies(refs):
    me, peers = _device_peers()
    return [(refs[0].at[me ^ k], refs[1].at[me], to) for k, to in peers]


def _small_spread_copies(refs):
    me, peers = _device_peers()
    return [(refs[0].at[me], refs[0].at[me], to) for _, to in peers]


def _sibling_copies(refs):
    n = len(refs) // 2
    x, y, c, _ = _place()
    return [(refs[w], refs[n + w], (x, y, 1 - c)) for w in range(n)]


def _owner_copies(refs):
    n = len(refs) // 2
    x, y, c, chips = _place()
    return [(refs[w].at[2 * qx + qy], refs[n + w].at[j], (qx, qy, c))
            for w in range(n) for j, (qx, qy) in enumerate(chips)]


N_DEVICES = 8
SMALL_ROWS = 616


SMALL = ("norm_mix_g", "conv_w", "conv_b", "lru_w_a", "lru_b_a", "lru_w_x", "lru_b_x", "lru_lambda",
         "sgu_ln_g", "sgu_ln_b", "sgu_w_s", "sgu_b_s", "norm_ffn_g", "final_norm_g")
WEIGHTS = ("norm_mix_g", "w_in", "conv_w", "conv_b", "lru_w_a", "lru_b_a", "lru_w_x", "lru_b_x", "lru_lambda",
           "sgu_ln_g", "sgu_ln_b", "sgu_w_s", "sgu_b_s", "w_branch_a", "w_branch_b", "w_out", "norm_ffn_g",
           "w_up", "w_down", "final_norm_g")
PACK_ALIGN = SUBLANES * 128


def _pack_small(gs):
    parts = []
    for k in SMALL:
        flat = gs[k].reshape(-1)
        parts.append(jnp.pad(flat, (0, -flat.size % PACK_ALIGN)))
    flat = jnp.concatenate(parts)
    flat = jnp.pad(flat, (0, N_DEVICES * SMALL_ROWS * 128 - flat.size))
    return flat.reshape(N_DEVICES, SMALL_ROWS, 128)


def _unpack_small(buf, like):
    flat = buf.reshape(-1)
    out, off = {}, 0
    for k in SMALL:
        size = like[k].size
        out[k] = flat[off:off + size].reshape(like[k].shape)
        off += size + (-size % PACK_ALIGN)
    return out


def _as_rows(a):
    return a.reshape(-1, a.shape[-1])


def kernel(x, norm_mix_g, w_in, conv_w, conv_b, lru_w_a, lru_b_a, lru_w_x, lru_b_x, lru_lambda, sgu_ln_g, sgu_ln_b, sgu_w_s, sgu_b_s, w_branch_a, w_branch_b, w_out, norm_ffn_g, w_up, w_down, final_norm_g, loss_target, m_norm_mix_g, m_w_in, m_conv_w, m_conv_b, m_lru_w_a, m_lru_b_a, m_lru_w_x, m_lru_b_x, m_lru_lambda, m_sgu_ln_g, m_sgu_ln_b, m_sgu_w_s, m_sgu_b_s, m_w_branch_a, m_w_branch_b, m_w_out, m_norm_ffn_g, m_w_up, m_w_down, m_final_norm_g, v_norm_mix_g, v_w_in, v_conv_w, v_conv_b, v_lru_w_a, v_lru_b_a, v_lru_w_x, v_lru_b_x, v_lru_lambda, v_sgu_ln_g, v_sgu_ln_b, v_sgu_w_s, v_sgu_b_s, v_w_branch_a, v_w_branch_b, v_w_out, v_norm_ffn_g, v_w_up, v_w_down, v_final_norm_g):
    w = dict(norm_mix_g=norm_mix_g, w_in=w_in, conv_w=conv_w, conv_b=conv_b, lru_w_a=lru_w_a, lru_b_a=lru_b_a,
             lru_w_x=lru_w_x, lru_b_x=lru_b_x, lru_lambda=lru_lambda, sgu_ln_g=sgu_ln_g, sgu_ln_b=sgu_ln_b,
             sgu_w_s=sgu_w_s, sgu_b_s=sgu_b_s, w_branch_a=w_branch_a, w_branch_b=w_branch_b, w_out=w_out,
             norm_ffn_g=norm_ffn_g, w_up=w_up, w_down=w_down, final_norm_g=final_norm_g)
    m = dict(norm_mix_g=m_norm_mix_g, w_in=m_w_in, conv_w=m_conv_w, conv_b=m_conv_b, lru_w_a=m_lru_w_a,
             lru_b_a=m_lru_b_a, lru_w_x=m_lru_w_x, lru_b_x=m_lru_b_x, lru_lambda=m_lru_lambda,
             sgu_ln_g=m_sgu_ln_g, sgu_ln_b=m_sgu_ln_b, sgu_w_s=m_sgu_w_s, sgu_b_s=m_sgu_b_s,
             w_branch_a=m_w_branch_a, w_branch_b=m_w_branch_b, w_out=m_w_out, norm_ffn_g=m_norm_ffn_g,
             w_up=m_w_up, w_down=m_w_down, final_norm_g=m_final_norm_g)
    v = dict(norm_mix_g=v_norm_mix_g, w_in=v_w_in, conv_w=v_conv_w, conv_b=v_conv_b, lru_w_a=v_lru_w_a,
             lru_b_a=v_lru_b_a, lru_w_x=v_lru_w_x, lru_b_x=v_lru_b_x, lru_lambda=v_lru_lambda,
             sgu_ln_g=v_sgu_ln_g, sgu_ln_b=v_sgu_ln_b, sgu_w_s=v_sgu_w_s, sgu_b_s=v_sgu_b_s,
             w_branch_a=v_w_branch_a, w_branch_b=v_w_branch_b, w_out=v_w_out, norm_ffn_g=v_norm_ffn_g,
             w_up=v_w_up, w_down=v_w_down, final_norm_g=v_final_norm_g)
    core = lax.axis_index("c")
    chip = 2 * lax.axis_index("x") + lax.axis_index("y")
    sel = jnp.stack([core, 1 - core, chip, 2 * chip + core]).astype(jnp.int32)
    this_core, other_core, this_chip = ("sel", 0), ("sel", 1), ("sel", 2)
    sds = jax.ShapeDtypeStruct

    ts = TOKEN_TILE
    halves = {k: (w[k].shape[1] // 2, w[k].shape[2]) for k in BIG}

    def half_view(k, a):
        return a.reshape((2 * N_QUARTERS,) + halves[k])

    def full_view(k, a):
        r2, cols = halves[k]
        if k in ("w_in", "w_up"):
            return a.reshape(1, N_QUARTERS, 2 * r2, cols)
        return a.reshape(1, 2 * N_QUARTERS * r2, cols)

    layer_bufs = [{}, {}]

    def cast_weights(k, after):
        _, r, cols = w[k].shape
        w4 = w[k].reshape(DEPTH, 1, r, cols)
        outs = _ew_call(lambda a, b: (a, b), "cast_weights", [(w4, (0, 0)), (w4, (1, 0))],
                        [(sds((1, N_QUARTERS, r, cols), BF), (0, this_chip))] * DEPTH, 1, sel, after=after)
        for l in range(DEPTH):
            layer_bufs[l][k] = half_view(k, outs[l])

    conv_buf = lax.dynamic_update_slice_in_dim(
        jnp.zeros((DEPTH, N_QUARTERS) + conv_w.shape[1:], F32), conv_w[:, None], chip, axis=1)
    sm = {k: w[k] for k in SMALL}
    sm["conv_w"] = _gather_call([conv_buf])[0].transpose(0, 2, 1, 3).reshape(DEPTH, CONV_WIDTH, D_RNN)

    def gather_start(tag, l, keys, after):
        bufs = [layer_bufs[l][k] for k in keys]
        return _exchange_start(f"gather_start_{tag}", bufs, _gather_copies, 3 * len(keys), after)

    def gather_finish(tag, keys, started, after):
        send_sems, recv_sems, thru, _ = started
        landed = _exchange_wait(f"gather_wait_{tag}", send_sems, recv_sems, thru, _gather_copies, after)
        landed = _sibling_inplace_call("gather_forward", landed, _gather_forward_slabs, 3 * len(keys))
        return {k: full_view(k, a) for k, a in zip(keys, landed)}

    first, rest = ("w_in",), tuple(k for k in BIG if k != "w_in")
    cast_weights("w_in", None)
    started_a = gather_start("0a", 0, first, sm["conv_w"])
    for k in rest:
        cast_weights(k, started_a[3])
    started_b = gather_start("0b", 0, rest, started_a[3])
    started_1 = gather_start("1", 1, BIG, started_b[3])
    big0 = gather_finish("0a", first, started_a, started_1[3])
    sv0 = _layer_fwd_mix(x[0], big0, _layer_small(sm, 0, sel[0:1]), ts)
    big0.update(gather_finish("0b", rest, started_b, sv0["yb_pre"]))
    x_mid = _layer_fwd_out(sv0, big0, ts)
    big1 = gather_finish("1", BIG, started_1, x_mid)
    sv1 = _layer_fwd_mix(x_mid, big1, _layer_small(sm, 1, sel[0:1]), ts)
    x_out = _layer_fwd_out(sv1, big1, ts)
    dx, loss, dgf = _loss_call(x_out, loss_target[0], final_norm_g.reshape(1, -1), ts)

    def pair_start(tag, gb, after):
        sends = [gb[k][1] for k in gb]
        zones = [lax.empty(a.shape, BF) for a in sends]
        return _exchange_start(f"pair_start_{tag}", sends + zones, _sibling_copies, len(sends), after)

    def reduce_start(tag, gb, after, pair=None):
        keys = tuple(gb)
        if pair is None:
            from_sibling = _sibling_send_call([gb[k][1] for k in keys])
        else:
            done = _exchange_wait(f"pair_wait_{tag}", pair[0], pair[1], pair[2], _sibling_copies, after)
            from_sibling = done[len(keys):]
        sums = [
            _ew_call(lambda a, b: (a + b.astype(F32),), "pair_sum", [(gb[k][0][None], (0, "g")), (r[None], (0, "g"))],
                     [(sds((1,) + r.shape, BF), (0, "g"))], N_QUARTERS)[0][0]
            for k, r in zip(keys, from_sibling)]
        zones = [lax.empty((3,) + a.shape[1:], BF) for a in sums]
        started = _exchange_start(f"reduce_start_{tag}", sums + zones, _owner_copies, 3 * len(keys), after)
        return keys, started

    def reduce_finish(tag, l, keys_started, after, reduced):
        keys, (send_sems, recv_sems, thru, _) = keys_started
        done = _exchange_wait(f"reduce_wait_{tag}", send_sems, recv_sems, thru, _owner_copies, after)
        sums, zones = done[:len(keys)], done[len(keys):]
        for i, k in enumerate(keys):
            r2, cols = halves[k]
            reduced[k] = _ew_call(
                lambda a, b, c, d: (((a.astype(F32) + b.astype(F32)) + c.astype(F32)) + d.astype(F32),),
                "quarter_sum", [(sums[i][None], (0, this_chip))] + [(zones[i][None], (0, j)) for j in range(3)],
                [(sds((DEPTH, 2, r2, cols), F32), (l, this_core))], 1, sel, into=reduced.get(k))[0]

    def behind(params, key, started):
        return dict(params, **{key: params[key] + started[1][3][0, 0]})

    dx1, gb_ffn, gs1 = _layer_bwd_ffn(dx, sv1, big1, ts)
    merge_out, gb_merge = _layer_bwd_merge(dx1, sv1, big1, ts)
    dx_mid, gb_in, gs1_mix = _layer_bwd_branches(dx1, merge_out, sv1, big1, lru_lambda[1], ts)
    gb_1 = {**gb_ffn, **gb_merge, **gb_in}
    pair_1 = pair_start("1", gb_1, dx_mid)
    sv0["p"] = behind(sv0["p"], "g2", (None, pair_1))
    dx1, gb_ffn, gs0 = _layer_bwd_ffn(dx_mid, sv0, big0, ts)
    exchange_1 = reduce_start("1", gb_1, dx1, pair_1)
    exchange_0a = reduce_start("0a", gb_ffn, exchange_1[1][3])
    merge_out, gb_merge = _layer_bwd_merge(dx1, sv0, big0, ts)
    exchange_0b = reduce_start("0b", gb_merge, exchange_0a[1][3])
    sv0["p"] = behind(sv0["p"], "lg", exchange_0b)
    grad_x, gb_in, gs0_mix = _layer_bwd_branches(dx1, merge_out, sv0, big0, lru_lambda[0], ts)
    exchange_0c = reduce_start("0c", gb_in, exchange_0b[1][3])
    layer_gs = [{**gs0, **gs0_mix}, {**gs1, **gs1_mix}]
    gs = {k: jnp.stack([g[k] for g in layer_gs]) for k in layer_gs[0]}
    gs["final_norm_g"] = dgf[0]

    me = ("sel", 3)
    piece = (1, N_DEVICES, SMALL_ROWS, 128)
    packed = _pack_small(gs).reshape(piece)
    scatter = _exchange_start("small_scatter_start", [packed[0], lax.empty(piece[1:], F32)], _small_scatter_copies,
                              N_DEVICES - 1, exchange_0c[1][3])
    reduced = {}
    reduce_finish("1", 1, exchange_1, scatter[3], reduced)
    reduce_finish("0a", 0, exchange_0a, reduced["w_in"], reduced)
    reduce_finish("0b", 0, exchange_0b, reduced["w_down"], reduced)

    def swap_slabs(ref, c, i):
        layers = (1,) if BIG[i] == "w_in" else range(DEPTH)
        return [(ref.at[l, c], ref.at[l, 1 - c]) for l in layers]

    swapped = dict(zip(BIG, _sibling_inplace_call("grads_swap_halves", [reduced[k] for k in BIG], swap_slabs,
                                                  DEPTH * len(BIG) - 1)))

    def adamw_layers(k, grad, layer, into):
        if layer is None:
            views = [_as4(_as_rows(a)) for a in (w[k], grad, m[k], v[k])]
            idx = (0, 0)
        else:
            views = [a.reshape((1,) + w[k].shape) for a in (w[k], grad, m[k], v[k])]
            idx = (0, layer)
        return _ew_call(_adamw, "adamw_big", [(a, idx) for a in views], [(sds(views[0].shape, F32), idx)] * 3,
                        into=into)

    def after_all(arrays):
        return jnp.stack([a[(0,) * a.ndim] for a in arrays])

    updated = {k: adamw_layers(k, swapped[k], 1 if k == "w_in" else None, None) for k in BIG}
    scattered = _exchange_wait("small_scatter_wait", scatter[0], scatter[1], scatter[2], _small_scatter_copies,
                               after_all([updated[k][0] for k in BIG]))
    summed = _ew_call(
        lambda *parts: (functools.reduce(lambda a, b: a + b, parts),), "small_sum",
        [(scattered[0][None], (0, me))]
        + [(scattered[1][None], (0, lambda g, s, k=k: s[3] ^ k)) for k in range(1, N_DEVICES)],
        [(sds(piece, F32), (0, me))], 1, sel)[0]
    spread = _exchange_start("small_spread_start", [summed[0]], _small_spread_copies, N_DEVICES - 1, summed)
    reduced["w_in"] = swapped["w_in"]
    reduce_finish("0c", 0, exchange_0c, spread[3], reduced)
    last = _sibling_inplace_call("grads_swap_last", [reduced["w_in"]],
                                 lambda ref, c, i: [(ref.at[0, c], ref.at[0, 1 - c])], 1)[0]
    swapped["w_in"] = last
    updated["w_in"] = adamw_layers("w_in", last, 0, updated["w_in"])
    grads_big = {k: swapped[k].reshape(w[k].shape) for k in BIG}
    delta, new_m, new_v = ({k: updated[k][j].reshape(w[k].shape) for k in BIG} for j in range(3))
    gathered_small = _exchange_wait("small_spread_wait", spread[0], spread[1], spread[2], _small_spread_copies,
                                    updated["w_in"][0])[0]

    like = {k: jax.ShapeDtypeStruct(sm[k].shape, F32) for k in SMALL}
    grads_small = _unpack_small(gathered_small, like)
    conv_q = grads_small["conv_w"].reshape(DEPTH, CONV_WIDTH, N_QUARTERS, D_RNN // N_QUARTERS)
    grads_small["conv_w"] = lax.dynamic_index_in_dim(conv_q, chip, axis=2, keepdims=False)
    outs = _small_adamw_call(*[[_as_rows(d[k]) for k in SMALL] for d in (w, grads_small, m, v)])
    for d, o in zip((delta, new_m, new_v), outs):
        for k, a in zip(SMALL, o):
            d[k] = a.reshape(w[k].shape)

    grads = {**grads_big, **grads_small}
    total = lax.psum(loss[0, 0], ("x", "y", "c"))
    return (total, grad_x[None], *[grads[k] for k in WEIGHTS], *[delta[k] for k in WEIGHTS],
            *[new_m[k] for k in WEIGHTS], *[new_v[k] for k in WEIGHTS])
```

```python
import functools
import math

import jax
import jax.numpy as jnp
from jax import lax
from jax.experimental import pallas as pl
from jax.experimental.pallas import tpu as pltpu

F32 = jnp.float32
BF = jnp.bfloat16

DEPTH = 2
D_MODEL = 1024
D_RNN = 1280
D_SGU = 1024
D_FF = 4096
D_IN = 2 * D_RNN + 2 * D_SGU + 2 * D_MODEL
N_QUARTERS = 4
Q_IN = D_IN // N_QUARTERS
Q_FF = D_FF // N_QUARTERS
RNN_HEADS = 20
RNN_HEAD_DIM = 64
LRU_GROUP = 256
N_LRU_GROUPS = D_RNN // LRU_GROUP
HEADS_PER_GROUP = LRU_GROUP // RNN_HEAD_DIM
CONV_WIDTH = 4
LRU_C = 8.0
SGU_GROUPS = 8
SGU_BLOCK = 128
CHUNK = 64
EPS = 1e-6

ADAM_LR = 0.001
ADAM_B1 = 0.9
ADAM_B2 = 0.999
ADAM_EPS = 1e-08
ADAM_WD = 0.01
ADAM_STEP = 10

SUBLANES = 8
TOKEN_TILE = 512
VMEM_LIMIT_BYTES = 56 * 1024 * 1024

MESH = pl.DeviceIdType.MESH


def _params(semantics=None, vmem=True, **kw):
    return pltpu.CompilerParams(
        dimension_semantics=semantics,
        vmem_limit_bytes=VMEM_LIMIT_BYTES if vmem else None,
        **kw,
    )


def _dot(a, b):
    return jnp.dot(a, b, preferred_element_type=F32)


def _dot_nt(a, b):
    return lax.dot_general(a, b, (((1,), (1,)), ((), ())), preferred_element_type=F32)


def _dot_tn(a, b):
    return lax.dot_general(a, b, (((0,), (0,)), ((), ())), preferred_element_type=F32)


_GELU_C = math.sqrt(2.0 / math.pi)
_GELU_A = 0.044715


def _gelu(x):
    return 0.5 * x * (1.0 + jnp.tanh(_GELU_C * (x + _GELU_A * x * x * x)))


def _gelu_and_grad(x):
    x2 = x * x
    t = jnp.tanh(_GELU_C * (x + _GELU_A * x2 * x))
    du = _GELU_C * (1.0 + 3.0 * _GELU_A * x2)
    return 0.5 * x * (1.0 + t), 0.5 * (1.0 + t) + 0.5 * x * (1.0 - t * t) * du


def _rms_stats(x):
    return lax.rsqrt(jnp.mean(x * x, axis=-1, keepdims=True) + EPS)


def _rms_bwd(dy, x, g):
    rs = _rms_stats(x)
    n = x * rs
    dn = dy * g
    dx = rs * (dn - n * jnp.mean(dn * n, axis=-1, keepdims=True))
    return dx, dy * n


def _row_sum(x):
    return jnp.sum(x, axis=0, keepdims=True)


def _tile_spec(ts, width, col=0):
    return pl.BlockSpec((ts, width), lambda i, col=col: (i, col))


def _full_spec(shape):
    zeros = (0,) * len(shape)
    return pl.BlockSpec(shape, lambda *_: zeros)


def _layer_spec(w, layer):
    zeros = (0,) * (w.ndim - 1)
    return pl.BlockSpec((None,) + tuple(w.shape[1:]), lambda *_: (layer,) + zeros)


def _norm_call(x, g, ts):
    s = x.shape[0]

    def body(x_ref, g_ref, h_ref):
        xv = x_ref[...]
        h_ref[...] = (xv * _rms_stats(xv) * g_ref[...]).astype(BF)

    return pl.pallas_call(
        body, name="norm_fwd", grid=(s // ts,),
        in_specs=[_tile_spec(ts, D_MODEL), _full_spec((1, D_MODEL))],
        out_specs=_tile_spec(ts, D_MODEL),
        out_shape=jax.ShapeDtypeStruct((s, D_MODEL), BF),
        compiler_params=_params(("parallel",)),
    )(x, g)


def _inproj_call(h, w_in, layer, ts):
    s = h.shape[0]

    def body(h_ref, w_ref, o_ref):
        o_ref[...] = _dot(h_ref[...], w_ref[...]).astype(BF)

    return pl.pallas_call(
        body, name="inproj_fwd", grid=(N_QUARTERS, s // ts),
        in_specs=[
            pl.BlockSpec((ts, D_MODEL), lambda q, i: (i, 0)),
            pl.BlockSpec((None, None, D_MODEL, Q_IN), lambda q, i: (layer, q, 0, 0)),
        ],
        out_specs=pl.BlockSpec((ts, Q_IN), lambda q, i: (i, q)),
        out_shape=jax.ShapeDtypeStruct((s, D_IN), BF),
        compiler_params=_params(("parallel", "parallel")),
    )(h, w_in)


def _shift_down(x, tail, s):
    xr = pltpu.roll(x, s, 0)
    tr = pltpu.roll(tail, s, 0)
    row = lax.broadcasted_iota(jnp.int32, tail.shape, 0)
    top = jnp.where(row < s, tr, xr[0:SUBLANES])
    return jnp.concatenate([top, xr[SUBLANES:]], axis=0)


def _shift_up(x, head, s):
    t = x.shape[0]
    xr = pltpu.roll(x, t - s, 0)
    hr = pltpu.roll(head, SUBLANES - s, 0)
    row = lax.broadcasted_iota(jnp.int32, head.shape, 0)
    bottom = jnp.where(row >= SUBLANES - s, hr, xr[t - SUBLANES:])
    return jnp.concatenate([xr[: t - SUBLANES], bottom], axis=0)


def _conv_fwd(x, tail, cw_ref, cb_ref):
    out = cb_ref[...] + cw_ref[CONV_WIDTH - 1:CONV_WIDTH, :] * x
    for s in range(1, CONV_WIDTH):
        k = CONV_WIDTH - 1 - s
        out = out + cw_ref[k:k + 1, :] * _shift_down(x, tail, s)
    return out


def _group_dot(x_bf, w_ref, dot):
    cols = [dot(x_bf[:, g * LRU_GROUP:(g + 1) * LRU_GROUP], w_ref[g]) for g in range(N_LRU_GROUPS)]
    return jnp.concatenate(cols, axis=1)


def _lru_gates(xr, wa_ref, wx_ref, ba_ref, bx_ref, sp_ref):
    xb = xr.astype(BF)
    r = jax.nn.sigmoid(_group_dot(xb, wa_ref, _dot) + ba_ref[...])
    i = jax.nn.sigmoid(_group_dot(xb, wx_ref, _dot) + bx_ref[...])
    log_a = (-LRU_C * r) * sp_ref[...]
    a = jnp.exp(log_a)
    nrm2 = -jnp.tanh(log_a) * (a * a + 1.0)
    inv_nrm = lax.rsqrt(jnp.maximum(nrm2, 1e-36))
    return r, i, a, nrm2 * inv_nrm, inv_nrm


def _linear_scan(a, b, carry, al_ref, bl_ref, h_ref, reverse):
    t, c = a.shape
    rowm = lax.broadcasted_iota(jnp.int32, (t, c), 0) & (SUBLANES - 1)
    for d in (1, 2, 4):
        if reverse:
            keep, sh = rowm < SUBLANES - d, t - d
        else:
            keep, sh = rowm >= d, d
        a_sh = jnp.where(keep, pltpu.roll(a, sh, 0), 1.0)
        b_sh = jnp.where(keep, pltpu.roll(b, sh, 0), 0.0)
        b = a * b_sh + b
        a = a * a_sh
    al_ref[...] = a
    bl_ref[...] = b
    groups = t // SUBLANES

    def step(j, state):
        jj = groups - 1 - j if reverse else j
        off = pl.multiple_of(jj * SUBLANES, SUBLANES)
        rows = bl_ref[pl.ds(off, SUBLANES), :] + al_ref[pl.ds(off, SUBLANES), :] * state
        h_ref[pl.ds(off, SUBLANES), :] = rows
        last = rows[0:1, :] if reverse else rows[SUBLANES - 1:SUBLANES, :]
        return jnp.broadcast_to(last, (SUBLANES, c))

    out = lax.fori_loop(0, groups, step, jnp.broadcast_to(carry, (SUBLANES, c)))
    return out[0:1, :]


def _rnn_fwd_call(proj, wa, wx, ba, bx, sp, cw, cb, ts):
    s = proj.shape[0]

    def body(xg_ref, wa_ref, wx_ref, ba_ref, bx_ref, sp_ref, cw_ref, cb_ref, xr_ref, hr_ref, ya_ref,
             tail_sc, carry_sc, al_sc, bl_sc, h_sc):
        @pl.when(pl.program_id(0) == 0)
        def _():
            tail_sc[...] = jnp.zeros_like(tail_sc)
            carry_sc[...] = jnp.zeros_like(carry_sc)

        x = xg_ref[:, :D_RNN].astype(F32)
        g = xg_ref[:, D_RNN:].astype(F32)
        xr = _conv_fwd(x, tail_sc[...], cw_ref, cb_ref)
        tail_sc[...] = x[ts - SUBLANES:, :]
        xr_ref[...] = xr.astype(BF)
        _, i, a, nrm, _ = _lru_gates(xr, wa_ref, wx_ref, ba_ref, bx_ref, sp_ref)
        carry_sc[...] = _linear_scan(a, nrm * (i * xr), carry_sc[...], al_sc, bl_sc, h_sc, False)
        h = h_sc[...]
        hr_ref[...] = h.astype(BF)
        ya_ref[...] = (h * _gelu(g)).astype(BF)

    gw = (N_LRU_GROUPS, LRU_GROUP, LRU_GROUP)
    return pl.pallas_call(
        body, name="rnn_fwd", grid=(s // ts,),
        in_specs=[_tile_spec(ts, 2 * D_RNN), _full_spec(gw), _full_spec(gw),
                  _full_spec((1, D_RNN)), _full_spec((1, D_RNN)), _full_spec((1, D_RNN)),
                  _full_spec((CONV_WIDTH, D_RNN)), _full_spec((1, D_RNN))],
        out_specs=[_tile_spec(ts, D_RNN)] * 3,
        out_shape=[jax.ShapeDtypeStruct((s, D_RNN), BF)] * 3,
        scratch_shapes=[pltpu.VMEM((SUBLANES, D_RNN), F32), pltpu.VMEM((1, D_RNN), F32),
                        pltpu.VMEM((ts, D_RNN), F32), pltpu.VMEM((ts, D_RNN), F32),
                        pltpu.VMEM((ts, D_RNN), F32)],
        compiler_params=_params(("arbitrary",)),
    )(proj, wa, wx, ba, bx, sp, cw, cb)


def _layernorm_fwd(x):
    mu = jnp.mean(x, axis=-1, keepdims=True)
    xc = x - mu
    rstd = lax.rsqrt(jnp.mean(xc * xc, axis=-1, keepdims=True) + EPS)
    return xc * rstd, rstd


def _sgu_mix(vn_bf, wm_ref, bsb_ref, ts):
    rows = []
    for blk in range(ts // SGU_BLOCK):
        r0 = blk * SGU_BLOCK
        cols = [
            _dot(wm_ref[g], vn_bf[r0:r0 + SGU_BLOCK, g * SGU_BLOCK:(g + 1) * SGU_BLOCK]) + bsb_ref[g]
            for g in range(SGU_GROUPS)
        ]
        rows.append(jnp.concatenate(cols, axis=1))
    return jnp.concatenate(rows, axis=0)


def _sgu_fwd_call(proj, wm, bsb, lg, lb, ts):
    s = proj.shape[0]

    def body(uv_ref, wm_ref, bsb_ref, lg_ref, lb_ref, yb_ref):
        gu = _gelu(uv_ref[:, :D_SGU].astype(F32))
        gv = _gelu(uv_ref[:, D_SGU:2 * D_SGU].astype(F32))
        nh, _ = _layernorm_fwd(gv)
        vn = (nh * lg_ref[...] + lb_ref[...]).astype(BF)
        yb_ref[...] = (gu * _sgu_mix(vn, wm_ref, bsb_ref, ts)).astype(BF)

    sw = (SGU_GROUPS, SGU_BLOCK, SGU_BLOCK)
    return pl.pallas_call(
        body, name="sgu_fwd", grid=(s // ts,),
        in_specs=[_tile_spec(ts, 2 * D_RNN, 1), _full_spec(sw), _full_spec(sw),
                  _full_spec((1, D_SGU)), _full_spec((1, D_SGU))],
        out_specs=_tile_spec(ts, D_SGU),
        out_shape=jax.ShapeDtypeStruct((s, D_SGU), BF),
        compiler_params=_params(("parallel",)),
    )(proj, wm, bsb, lg, lb)


_GATE_COL0 = (2 * D_RNN + 2 * D_SGU) // 512


def _gate_specs(ts):
    return [_tile_spec(ts, 512, _GATE_COL0 + j) for j in range(4)]


def _merge_call(x, proj, ya_pre, yb_pre, w_ba, w_bb, w_out, g2, layer, ts):
    s = x.shape[0]

    def body(x_ref, ga0, ga1, gb0, gb1, ya_ref, yb_ref, wa_ref, wb_ref, wo_ref, g2_ref,
             x1_ref, yao_ref, ybo_ref, mg_ref, h2_ref):
        ya = _dot(ya_ref[...], wa_ref[...])
        yb = _dot(yb_ref[...], wb_ref[...])
        sa = jax.nn.sigmoid(jnp.concatenate([ga0[...], ga1[...]], axis=1).astype(F32))
        sb = jax.nn.sigmoid(jnp.concatenate([gb0[...], gb1[...]], axis=1).astype(F32))
        merged = (sa * ya + sb * yb).astype(BF)
        x1 = x_ref[...] + _dot(merged, wo_ref[...])
        x1_ref[...] = x1
        yao_ref[...] = ya.astype(BF)
        ybo_ref[...] = yb.astype(BF)
        mg_ref[...] = merged
        h2_ref[...] = (x1 * _rms_stats(x1) * g2_ref[...]).astype(BF)

    act = jax.ShapeDtypeStruct((s, D_MODEL), BF)
    return pl.pallas_call(
        body, name="merge_fwd", grid=(s // ts,),
        in_specs=[_tile_spec(ts, D_MODEL)] + _gate_specs(ts) + [
            _tile_spec(ts, D_RNN), _tile_spec(ts, D_SGU),
            _layer_spec(w_ba, layer), _layer_spec(w_bb, layer), _layer_spec(w_out, layer),
            _full_spec((1, D_MODEL))],
        out_specs=[_tile_spec(ts, D_MODEL)] * 5,
        out_shape=[jax.ShapeDtypeStruct((s, D_MODEL), F32), act, act, act, act],
        compiler_params=_params(("parallel",)),
    )(x, proj, proj, proj, proj, ya_pre, yb_pre, w_ba, w_bb, w_out, g2)


def _ffn_call(x1, h2, w_up, w_down, layer, ts):
    s = x1.shape[0]

    def body(x1_ref, h2_ref, wu_ref, wd_ref, x2_ref, p_ref):
        h2v = h2_ref[...]
        acc = x1_ref[...]
        for q in range(N_QUARTERS):
            p = _dot(h2v, wu_ref[q])
            p_ref[:, q * Q_FF:(q + 1) * Q_FF] = p.astype(BF)
            f = jnp.square(jnp.maximum(p, 0.0)).astype(BF)
            acc = acc + _dot(f, wd_ref[q * Q_FF:(q + 1) * Q_FF, :])
        x2_ref[...] = acc

    return pl.pallas_call(
        body, name="ffn_fwd", grid=(s // ts,),
        in_specs=[_tile_spec(ts, D_MODEL), _tile_spec(ts, D_MODEL),
                  pl.BlockSpec((None, N_QUARTERS, D_MODEL, Q_FF), lambda i: (layer, 0, 0, 0)),
                  pl.BlockSpec((None, D_FF, D_MODEL), lambda i: (layer, 0, 0))],
        out_specs=[_tile_spec(ts, D_MODEL), _tile_spec(ts, D_FF)],
        out_shape=[jax.ShapeDtypeStruct((s, D_MODEL), F32), jax.ShapeDtypeStruct((s, D_FF), BF)],
        compiler_params=_params(("parallel",)),
    )(x1, h2, w_up, w_down)


def _loss_call(x, target, gf, ts):
    s = x.shape[0]

    def body(x_ref, t_ref, g_ref, dx_ref, loss_ref, dg_ref):
        @pl.when(pl.program_id(0) == 0)
        def _():
            loss_ref[...] = jnp.zeros_like(loss_ref)
            dg_ref[...] = jnp.zeros_like(dg_ref)

        xv = x_ref[...]
        gv = g_ref[...]
        err = xv * _rms_stats(xv) * gv - t_ref[...]
        part = 0.5 * jnp.sum(jnp.mean(err * err, axis=-1, keepdims=True), axis=0, keepdims=True)
        loss_ref[...] += jnp.broadcast_to(part, loss_ref.shape)
        dx, dg = _rms_bwd(err * (1.0 / D_MODEL), xv, gv)
        dx_ref[...] = dx
        dg_ref[...] += _row_sum(dg)

    return pl.pallas_call(
        body, name="loss_head", grid=(s // ts,),
        in_specs=[_tile_spec(ts, D_MODEL), _tile_spec(ts, D_MODEL), _full_spec((1, D_MODEL))],
        out_specs=[_tile_spec(ts, D_MODEL), _full_spec((1, 128)), _full_spec((1, D_MODEL))],
        out_shape=[jax.ShapeDtypeStruct((s, D_MODEL), F32), jax.ShapeDtypeStruct((1, 128), F32),
                   jax.ShapeDtypeStruct((1, D_MODEL), F32)],
        compiler_params=_params(("arbitrary",)),
    )(x, target, gf)


def _ffn_bwd_call(dx2, p, x1, g2, w_up, w_down, layer, ts):
    s = dx2.shape[0]

    def body(dx2_ref, p_ref, x1_ref, g2_ref, wu_ref, wd_ref, dx1_ref, dp_ref, dg_ref):
        @pl.when(pl.program_id(0) == 0)
        def _():
            dg_ref[...] = jnp.zeros_like(dg_ref)

        dx2v = dx2_ref[...]
        dyb = dx2v.astype(BF)
        dh2 = jnp.zeros((ts, D_MODEL), F32)
        for q in range(N_QUARTERS):
            cols = slice(q * Q_FF, (q + 1) * Q_FF)
            df = _dot_nt(dyb, wd_ref[cols, :])
            dp = (df * (2.0 * jnp.maximum(p_ref[:, cols].astype(F32), 0.0))).astype(BF)
            dp_ref[:, cols] = dp
            dh2 = dh2 + _dot_nt(dp, wu_ref[q])
        dx, dg = _rms_bwd(dh2, x1_ref[...], g2_ref[...])
        dx1_ref[...] = dx2v + dx
        dg_ref[...] += _row_sum(dg)

    return pl.pallas_call(
        body, name="ffn_bwd", grid=(s // ts,),
        in_specs=[_tile_spec(ts, D_MODEL), _tile_spec(ts, D_FF), _tile_spec(ts, D_MODEL),
                  _full_spec((1, D_MODEL)),
                  pl.BlockSpec((None, N_QUARTERS, D_MODEL, Q_FF), lambda i: (layer, 0, 0, 0)),
                  pl.BlockSpec((None, D_FF, D_MODEL), lambda i: (layer, 0, 0))],
        out_specs=[_tile_spec(ts, D_MODEL), _tile_spec(ts, D_FF), _full_spec((1, D_MODEL))],
        out_shape=[jax.ShapeDtypeStruct((s, D_MODEL), F32), jax.ShapeDtypeStruct((s, D_FF), BF),
                   jax.ShapeDtypeStruct((1, D_MODEL), F32)],
        compiler_params=_params(("arbitrary",)),
    )(dx2, p, x1, g2, w_up, w_down)


def _merge_bwd_call(dx1, proj, ya, yb, w_ba, w_bb, w_out, layer, ts):
    s = dx1.shape[0]

    def body(dx1_ref, ga0, ga1, gb0, gb1, ya_ref, yb_ref, wa_ref, wb_ref, wo_ref,
             dya_ref, dyb_ref, dgate_ref, dyap_ref, dybp_ref):
        dm = _dot_nt(dx1_ref[...].astype(BF), wo_ref[...])
        sa = jax.nn.sigmoid(jnp.concatenate([ga0[...], ga1[...]], axis=1).astype(F32))
        sb = jax.nn.sigmoid(jnp.concatenate([gb0[...], gb1[...]], axis=1).astype(F32))
        dya = (dm * sa).astype(BF)
        dyb = (dm * sb).astype(BF)
        dya_ref[...] = dya
        dyb_ref[...] = dyb
        dgate_ref[:, :D_MODEL] = (dm * ya_ref[...].astype(F32) * sa * (1.0 - sa)).astype(BF)
        dgate_ref[:, D_MODEL:] = (dm * yb_ref[...].astype(F32) * sb * (1.0 - sb)).astype(BF)
        dyap_ref[...] = _dot_nt(dya, wa_ref[...]).astype(BF)
        dybp_ref[...] = _dot_nt(dyb, wb_ref[...]).astype(BF)

    act = jax.ShapeDtypeStruct((s, D_MODEL), BF)
    return pl.pallas_call(
        body, name="merge_bwd", grid=(s // ts,),
        in_specs=[_tile_spec(ts, D_MODEL)] + _gate_specs(ts) + [
            _tile_spec(ts, D_MODEL), _tile_spec(ts, D_MODEL),
            _layer_spec(w_ba, layer), _layer_spec(w_bb, layer), _layer_spec(w_out, layer)],
        out_specs=[_tile_spec(ts, D_MODEL), _tile_spec(ts, D_MODEL), _tile_spec(ts, 2 * D_MODEL),
                   _tile_spec(ts, D_RNN), _tile_spec(ts, D_SGU)],
        out_shape=[act, act, jax.ShapeDtypeStruct((s, 2 * D_MODEL), BF),
                   jax.ShapeDtypeStruct((s, D_RNN), BF), jax.ShapeDtypeStruct((s, D_SGU), BF)],
        compiler_params=_params(("parallel",)),
    )(dx1, proj, proj, proj, proj, ya, yb, w_ba, w_bb, w_out)


def _sgu_bwd_call(dyb_pre, proj, wm, bsb, mask, lg, lb, ts):
    s = proj.shape[0]

    def body(dy_ref, uv_ref, wm_ref, bsb_ref, mask_ref, lg_ref, lb_ref,
             duv_ref, dws_ref, dbs_ref, dlg_ref, dlb_ref, dm_sc):
        step = pl.program_id(0)

        @pl.when(step == 0)
        def _():
            dws_ref[...] = jnp.zeros_like(dws_ref)
            dlg_ref[...] = jnp.zeros_like(dlg_ref)
            dlb_ref[...] = jnp.zeros_like(dlb_ref)
            dm_sc[...] = jnp.zeros_like(dm_sc)

        gu, dgu_du = _gelu_and_grad(uv_ref[:, :D_SGU].astype(F32))
        gv, dgv_dv = _gelu_and_grad(uv_ref[:, D_SGU:2 * D_SGU].astype(F32))
        nh, rstd = _layernorm_fwd(gv)
        lgv = lg_ref[...]
        vn = (nh * lgv + lb_ref[...]).astype(BF)
        dy = dy_ref[...].astype(F32)
        du = dy * _sgu_mix(vn, wm_ref, bsb_ref, ts) * dgu_du
        dmix = dy * gu
        dmix_bf = dmix.astype(BF)
        dm_acc = dm_sc[...]
        rows = []
        for blk in range(ts // SGU_BLOCK):
            r0 = blk * SGU_BLOCK
            dm_acc = dm_acc + dmix[r0:r0 + SGU_BLOCK, :]
            cols = []
            for g in range(SGU_GROUPS):
                c0 = g * SGU_BLOCK
                dmg = dmix_bf[r0:r0 + SGU_BLOCK, c0:c0 + SGU_BLOCK]
                cols.append(_dot_tn(wm_ref[g], dmg))
                dws_ref[g] += mask_ref[...] * _dot_nt(dmg, vn[r0:r0 + SGU_BLOCK, c0:c0 + SGU_BLOCK])
            rows.append(jnp.concatenate(cols, axis=1))
        dm_sc[...] = dm_acc
        dvn = jnp.concatenate(rows, axis=0)
        dlg_ref[...] += _row_sum(dvn * nh)
        dlb_ref[...] += _row_sum(dvn)
        dnh = dvn * lgv
        dgv = rstd * (dnh - jnp.mean(dnh, axis=-1, keepdims=True)
                      - nh * jnp.mean(dnh * nh, axis=-1, keepdims=True))
        duv_ref[:, :D_SGU] = du.astype(BF)
        duv_ref[:, D_SGU:] = (dgv * dgv_dv).astype(BF)

        @pl.when(step == pl.num_programs(0) - 1)
        def _():
            for g in range(SGU_GROUPS):
                dbs_ref[:, g:g + 1] = jnp.sum(
                    dm_acc[:, g * SGU_BLOCK:(g + 1) * SGU_BLOCK], axis=1, keepdims=True)

    sw = (SGU_GROUPS, SGU_BLOCK, SGU_BLOCK)
    return pl.pallas_call(
        body, name="sgu_bwd", grid=(s // ts,),
        in_specs=[_tile_spec(ts, D_SGU), _tile_spec(ts, 2 * D_RNN, 1), _full_spec(sw), _full_spec(sw),
                  _full_spec((SGU_BLOCK, SGU_BLOCK)), _full_spec((1, D_SGU)), _full_spec((1, D_SGU))],
        out_specs=[_tile_spec(ts, 2 * D_SGU), _full_spec(sw), _full_spec((SGU_BLOCK, SGU_GROUPS)),
                   _full_spec((1, D_SGU)), _full_spec((1, D_SGU))],
        out_shape=[jax.ShapeDtypeStruct((s, 2 * D_SGU), BF), jax.ShapeDtypeStruct(sw, F32),
                   jax.ShapeDtypeStruct((SGU_BLOCK, SGU_GROUPS), F32),
                   jax.ShapeDtypeStruct((1, D_SGU), F32), jax.ShapeDtypeStruct((1, D_SGU), F32)],
        scratch_shapes=[pltpu.VMEM((SGU_BLOCK, D_SGU), F32)],
        compiler_params=_params(("arbitrary",)),
    )(dyb_pre, proj, wm, bsb, mask, lg, lb)


_ROW_DBA, _ROW_DBX, _ROW_DSP, _ROW_DCB, _ROW_DCW = 0, 1, 2, 3, 4
_PREV_ROWS = 16


def _rnn_bwd_call(dya_pre, proj, xr_saved, hr, wa, wx, ba, bx, sp, cw, ts):
    s = proj.shape[0]
    nt = s // ts
    per = ts // _PREV_ROWS

    def tile(i):
        return nt - 1 - i

    def prev(i):
        return jnp.maximum(tile(i) * per - 1, 0)

    def body(dy_ref, xg_ref, xr_ref, hr_ref, hrp_ref, wa_ref, wx_ref, ba_ref, bx_ref, sp_ref,
             cw_ref, dxg_ref, dwa_ref, dwx_ref, vec_ref,
             lam_carry, a_first, dxr_head, al_sc, bl_sc, lam_sc):
        step = pl.program_id(0)

        @pl.when(step == 0)
        def _():
            dwa_ref[...] = jnp.zeros_like(dwa_ref)
            dwx_ref[...] = jnp.zeros_like(dwx_ref)
            vec_ref[...] = jnp.zeros_like(vec_ref)
            lam_carry[...] = jnp.zeros_like(lam_carry)
            a_first[...] = jnp.zeros_like(a_first)
            dxr_head[...] = jnp.zeros_like(dxr_head)

        has_prev = (step < nt - 1).astype(F32)
        x = xg_ref[:, :D_RNN].astype(F32)
        g = xg_ref[:, D_RNN:].astype(F32)
        h_tail = hrp_ref[_PREV_ROWS - SUBLANES:, :].astype(F32) * has_prev
        xr = xr_ref[...].astype(F32)
        r, i, a, nrm, inv_nrm = _lru_gates(xr, wa_ref, wx_ref, ba_ref, bx_ref, sp_ref)
        h = hr_ref[...].astype(F32)
        dy = dy_ref[...].astype(F32)
        gg, dgg = _gelu_and_grad(g)

        coef = _shift_up(a, jnp.broadcast_to(a_first[...], (SUBLANES, D_RNN)), 1)
        lam_carry[...] = _linear_scan(coef, dy * gg, lam_carry[...], al_sc, bl_sc, lam_sc, True)
        a_first[...] = a[0:1, :]
        lam = lam_sc[...]

        da = lam * _shift_down(h, h_tail, 1)
        dnrm = lam * (i * xr)
        di = lam * nrm * xr
        dlog_a = da * a - dnrm * (a * a) * inv_nrm
        spv = sp_ref[...]
        dza = (dlog_a * (-LRU_C * spv)) * (r * (1.0 - r))
        dzx = di * (i * (1.0 - i))
        vec_ref[_ROW_DSP:_ROW_DSP + 1, :] += _row_sum(dlog_a * (-LRU_C * r))
        vec_ref[_ROW_DBA:_ROW_DBA + 1, :] += _row_sum(dza)
        vec_ref[_ROW_DBX:_ROW_DBX + 1, :] += _row_sum(dzx)
        xb = xr.astype(BF)
        dza_bf = dza.astype(BF)
        dzx_bf = dzx.astype(BF)
        for grp in range(N_LRU_GROUPS):
            cols = slice(grp * LRU_GROUP, (grp + 1) * LRU_GROUP)
            dwa_ref[grp] += _dot_tn(xb[:, cols], dza_bf[:, cols])
            dwx_ref[grp] += _dot_tn(xb[:, cols], dzx_bf[:, cols])
        dxr = (lam * nrm * i + _group_dot(dza_bf, wa_ref, _dot_nt) + _group_dot(dzx_bf, wx_ref, _dot_nt))

        vec_ref[_ROW_DCB:_ROW_DCB + 1, :] += _row_sum(dxr)
        head = dxr_head[...]
        dx = cw_ref[CONV_WIDTH - 1:CONV_WIDTH, :] * dxr
        vec_ref[_ROW_DCW + 3:_ROW_DCW + 4, :] += _row_sum(dxr * x)
        for sft in range(1, CONV_WIDTH):
            k = CONV_WIDTH - 1 - sft
            ahead = _shift_up(dxr, head, sft)
            dx = dx + cw_ref[k:k + 1, :] * ahead
            vec_ref[_ROW_DCW + k:_ROW_DCW + k + 1, :] += _row_sum(ahead * x)
        dxr_head[...] = dxr[0:SUBLANES, :]
        dxg_ref[:, :D_RNN] = dx.astype(BF)
        dxg_ref[:, D_RNN:] = (dy * h * dgg).astype(BF)

    gw = (N_LRU_GROUPS, LRU_GROUP, LRU_GROUP)
    rev = lambda width: pl.BlockSpec((ts, width), lambda i: (tile(i), 0))
    return pl.pallas_call(
        body, name="rnn_bwd", grid=(nt,),
        in_specs=[rev(D_RNN), rev(2 * D_RNN), rev(D_RNN), rev(D_RNN),
                  pl.BlockSpec((_PREV_ROWS, D_RNN), lambda i: (prev(i), 0)),
                  _full_spec(gw), _full_spec(gw),
                  _full_spec((1, D_RNN)), _full_spec((1, D_RNN)), _full_spec((1, D_RNN)),
                  _full_spec((CONV_WIDTH, D_RNN))],
        out_specs=[rev(2 * D_RNN), _full_spec(gw), _full_spec(gw), _full_spec((SUBLANES, D_RNN))],
        out_shape=[jax.ShapeDtypeStruct((s, 2 * D_RNN), BF), jax.ShapeDtypeStruct(gw, F32),
                   jax.ShapeDtypeStruct(gw, F32), jax.ShapeDtypeStruct((SUBLANES, D_RNN), F32)],
        scratch_shapes=[pltpu.VMEM((1, D_RNN), F32), pltpu.VMEM((1, D_RNN), F32),
                        pltpu.VMEM((SUBLANES, D_RNN), F32),
                        pltpu.VMEM((ts, D_RNN), F32), pltpu.VMEM((ts, D_RNN), F32),
                        pltpu.VMEM((ts, D_RNN), F32)],
        compiler_params=_params(("arbitrary",)),
    )(dya_pre, proj, xr_saved, hr, hr, wa, wx, ba, bx, sp, cw)


def _inproj_bwd_call(dxg, duv, dgate, dx1, x, g1, w_in, layer, ts):
    s = x.shape[0]

    def body(dxg_ref, duv_ref, dgt_ref, dx1_ref, x_ref, g_ref, w_ref, dx_ref, dproj_ref, dg_ref):
        @pl.when(pl.program_id(0) == 0)
        def _():
            dg_ref[...] = jnp.zeros_like(dg_ref)

        dproj = jnp.concatenate([dxg_ref[...], duv_ref[...], dgt_ref[...]], axis=1)
        dproj_ref[...] = dproj
        dh = jnp.zeros((ts, D_MODEL), F32)
        for q in range(N_QUARTERS):
            dh = dh + _dot_nt(dproj[:, q * Q_IN:(q + 1) * Q_IN], w_ref[q])
        dx, dg = _rms_bwd(dh, x_ref[...], g_ref[...])
        dx_ref[...] = dx1_ref[...] + dx
        dg_ref[...] += _row_sum(dg)

    return pl.pallas_call(
        body, name="inproj_bwd", grid=(s // ts,),
        in_specs=[_tile_spec(ts, 2 * D_RNN), _tile_spec(ts, 2 * D_SGU), _tile_spec(ts, 2 * D_MODEL),
                  _tile_spec(ts, D_MODEL), _tile_spec(ts, D_MODEL), _full_spec((1, D_MODEL)),
                  pl.BlockSpec((None, N_QUARTERS, D_MODEL, Q_IN), lambda i: (layer, 0, 0, 0))],
        out_specs=[_tile_spec(ts, D_MODEL), _tile_spec(ts, D_IN), _full_spec((1, D_MODEL))],
        out_shape=[jax.ShapeDtypeStruct((s, D_MODEL), F32), jax.ShapeDtypeStruct((s, D_IN), BF),
                   jax.ShapeDtypeStruct((1, D_MODEL), F32)],
        compiler_params=_params(("arbitrary",)),
    )(dxg, duv, dgate, dx1, x, g1, w_in)


def _relu_sq(p):
    return jnp.square(jnp.maximum(p, 0))


def _wgrad_call(a, b, core, tm, tn, tk, col_blocked, name, a_fn=None):
    s, m = a.shape
    n = b.shape[1]
    r, cols = (m, n // N_QUARTERS) if col_blocked else (m // N_QUARTERS, n)
    r2 = r // 2
    per_tile = tm // r
    steps = s // tk

    def body(core_ref, a_ref, b_ref, keep_ref, send_ref, *acc):
        av = a_ref[...]
        if a_fn is not None:
            av = a_fn(av)
        prod = _dot_tn(av.astype(BF), b_ref[...].astype(BF))

        def emit(total):
            for h in range(2):
                @pl.when(core_ref[0] == h)
                def _():
                    for q in range(per_tile):
                        keep_ref[q] = total[q * r + h * r2:q * r + (h + 1) * r2]
                        send_ref[q] = total[q * r + (1 - h) * r2:q * r + (2 - h) * r2].astype(BF)

        if steps == 1:
            emit(prod)
        else:
            acc_ref, = acc
            step = pl.program_id(2)

            @pl.when(step == 0)
            def _():
                acc_ref[...] = prod

            @pl.when(jnp.logical_and(step > 0, step < steps - 1))
            def _():
                acc_ref[...] += prod

            @pl.when(step == steps - 1)
            def _():
                emit(acc_ref[...] + prod)

    if col_blocked:
        per_q = cols // tn
        out_spec = pl.BlockSpec((1, r2, tn), lambda i, j, k, c: (j // per_q, 0, j % per_q))
    else:
        out_spec = pl.BlockSpec((per_tile, r2, tn), lambda i, j, k, c: (i, 0, j))
    return pl.pallas_call(
        body, name=name,
        out_shape=[jax.ShapeDtypeStruct((N_QUARTERS, r2, cols), F32),
                   jax.ShapeDtypeStruct((N_QUARTERS, r2, cols), BF)],
        grid_spec=pltpu.PrefetchScalarGridSpec(
            num_scalar_prefetch=1, grid=(m // tm, n // tn, steps),
            in_specs=[pl.BlockSpec((tk, tm), lambda i, j, k, c: (k, i)),
                      pl.BlockSpec((tk, tn), lambda i, j, k, c: (k, j))],
            out_specs=[out_spec, out_spec],
            scratch_shapes=[] if steps == 1 else [pltpu.VMEM((tm, tn), F32)]),
        compiler_params=_params(("parallel", "parallel", "arbitrary")),
    )(core, a, b)


BIG = ("w_in", "w_up", "w_down", "w_branch_a", "w_branch_b", "w_out")


def _block_diag(w):
    w4 = w.reshape(N_LRU_GROUPS, HEADS_PER_GROUP, RNN_HEAD_DIM, RNN_HEAD_DIM)
    eye = jnp.eye(HEADS_PER_GROUP, dtype=w.dtype)
    return jnp.einsum("gjio,jk->gjiko", w4, eye).reshape(N_LRU_GROUPS, LRU_GROUP, LRU_GROUP)


def _block_diag_extract(d):
    d5 = d.reshape(N_LRU_GROUPS, HEADS_PER_GROUP, RNN_HEAD_DIM, HEADS_PER_GROUP, RNN_HEAD_DIM)
    blocks = [d5[:, j, :, j, :] for j in range(HEADS_PER_GROUP)]
    return jnp.stack(blocks, axis=1).reshape(RNN_HEADS, RNN_HEAD_DIM, RNN_HEAD_DIM)


def _sgu_mask():
    chunk = jnp.arange(SGU_BLOCK) // CHUNK
    return (chunk[:, None] >= chunk[None, :]).astype(F32)


def _layer_small(sm, l, core):
    row = lambda v: v.reshape(1, -1)
    return dict(
        core=core,
        g1=row(sm["norm_mix_g"][l]), g2=row(sm["norm_ffn_g"][l]),
        wa=_block_diag(sm["lru_w_a"][l]).astype(BF), wx=_block_diag(sm["lru_w_x"][l]).astype(BF),
        ba=row(sm["lru_b_a"][l]), bx=row(sm["lru_b_x"][l]),
        sp=row(jax.nn.softplus(-sm["lru_lambda"][l])),
        cw=sm["conv_w"][l], cb=row(sm["conv_b"][l]),
        wm=(sm["sgu_w_s"][l] * _sgu_mask()).astype(BF),
        bsb=jnp.broadcast_to(sm["sgu_b_s"][l][:, :, None], (SGU_GROUPS, SGU_BLOCK, SGU_BLOCK)),
        lg=row(sm["sgu_ln_g"][l]), lb=row(sm["sgu_ln_b"][l]),
    )


def _layer_fwd_mix(x, big, p, ts, h=None):
    if h is None:
        h = _norm_call(x, p["g1"], ts)
    proj = _inproj_call(h, big["w_in"], 0, 2 * ts)
    xr, hr, ya_pre = _rnn_fwd_call(proj, p["wa"], p["wx"], p["ba"], p["bx"], p["sp"], p["cw"], p["cb"], ts)
    yb_pre = _sgu_fwd_call(proj, p["wm"], p["bsb"], p["lg"], p["lb"], ts)
    return dict(p=p, x=x, h=h, proj=proj, xr=xr, hr=hr, ya_pre=ya_pre, yb_pre=yb_pre)


def _layer_fwd_out(sv, big, ts):
    x1, ya, yb, merged, h2 = _merge_call(sv["x"], sv["proj"], sv["ya_pre"], sv["yb_pre"], big["w_branch_a"],
                                         big["w_branch_b"], big["w_out"], sv["p"]["g2"], 0, ts)
    x2, pre = _ffn_call(x1, h2, big["w_up"], big["w_down"], 0, ts)
    sv.update(x1=x1, ya=ya, yb=yb, merged=merged, h2=h2, pre=pre)
    return x2


def _layer_bwd_ffn(dx, sv, big, ts):
    p = sv["p"]
    dx1, dpre, dg2 = _ffn_bwd_call(dx, sv["pre"], sv["x1"], p["g2"], big["w_up"], big["w_down"], 0, ts)
    tk = dx.shape[0]
    gb = dict(
        w_down=_wgrad_call(sv["pre"], dx, p["core"], Q_FF, D_MODEL // 2, tk, False, "wgrad_down", a_fn=_relu_sq),
        w_up=_wgrad_call(sv["h2"], dpre, p["core"], D_MODEL, Q_FF, tk, True, "wgrad_up"))
    return dx1, gb, dict(norm_ffn_g=dg2[0])


def _layer_bwd_merge(dx1, sv, big, ts):
    tk = dx1.shape[0]
    core = sv["p"]["core"]
    dya, dyb, dgate, dya_pre, dyb_pre = _merge_bwd_call(
        dx1, sv["proj"], sv["ya"], sv["yb"], big["w_branch_a"], big["w_branch_b"], big["w_out"], 0, ts)
    gb = dict(
        w_out=_wgrad_call(sv["merged"], dx1, core, D_MODEL, D_MODEL // 2, tk, False, "wgrad_out"),
        w_branch_a=_wgrad_call(sv["ya_pre"], dya, core, D_RNN, D_MODEL // 2, tk, False, "wgrad_branch_a"),
        w_branch_b=_wgrad_call(sv["yb_pre"], dyb, core, D_SGU, D_MODEL // 2, tk, False, "wgrad_branch_b"))
    return (dgate, dya_pre, dyb_pre), gb


def _layer_bwd_branches(dx1, merge_out, sv, big, lam, ts):
    p = sv["p"]
    tk = dx1.shape[0]
    dgate, dya_pre, dyb_pre = merge_out
    gb = {}
    duv, dws, dbs, dlg, dlb = _sgu_bwd_call(dyb_pre, sv["proj"], p["wm"], p["bsb"], _sgu_mask(), p["lg"], p["lb"],
                                            ts)
    dxg, dwa, dwx, vec = _rnn_bwd_call(dya_pre, sv["proj"], sv["xr"], sv["hr"], p["wa"], p["wx"], p["ba"], p["bx"],
                                       p["sp"], p["cw"], ts // 2)
    dx, dproj, dg1 = _inproj_bwd_call(dxg, duv, dgate, dx1, sv["x"], p["g1"], big["w_in"], 0, ts)
    gb["w_in"] = _wgrad_call(sv["h"], dproj, p["core"], D_MODEL, Q_IN, tk // 2, True, "wgrad_in")
    gs = dict(
        norm_mix_g=dg1[0], conv_w=vec[_ROW_DCW:_ROW_DCW + CONV_WIDTH], conv_b=vec[_ROW_DCB],
        lru_w_a=_block_diag_extract(dwa), lru_w_x=_block_diag_extract(dwx),
        lru_b_a=vec[_ROW_DBA].reshape(RNN_HEADS, RNN_HEAD_DIM), lru_b_x=vec[_ROW_DBX].reshape(RNN_HEADS, RNN_HEAD_DIM),
        lru_lambda=-vec[_ROW_DSP] * jax.nn.sigmoid(-lam),
        sgu_ln_g=dlg[0], sgu_ln_b=dlb[0], sgu_w_s=dws, sgu_b_s=dbs.T)
    return dx, gb, gs


def _local_step(x, target, big, sm, ts):
    saved = []
    core = jnp.zeros((1,), jnp.int32)
    for l in range(DEPTH):
        sv = _layer_fwd_mix(x, big[l], _layer_small(sm, l, core), ts)
        x = _layer_fwd_out(sv, big[l], ts)
        saved.append(sv)
    dx, loss, dgf = _loss_call(x, target, sm["final_norm_g"].reshape(1, -1), ts)
    gb, gs = [None] * DEPTH, [None] * DEPTH
    for l in reversed(range(DEPTH)):
        dx1, gb_ffn, gs_ffn = _layer_bwd_ffn(dx, saved[l], big[l], ts)
        merge_out, gb_merge = _layer_bwd_merge(dx1, saved[l], big[l], ts)
        dx, gb_mix, gs_mix = _layer_bwd_branches(dx1, merge_out, saved[l], big[l], sm["lru_lambda"][l], ts)
        gb[l] = {**gb_ffn, **gb_merge, **gb_mix}
        gs[l] = {**gs_ffn, **gs_mix}
    gs = {k: jnp.stack([g[k] for g in gs]) for k in gs[0]}
    gs["final_norm_g"] = dgf[0]
    return loss, dx, gb, gs


EW_VMEM_BYTES = 24 * 1024 * 1024


def _row_block(rows, cols, bytes_per_elem):
    for br in range(min(rows, EW_VMEM_BYTES // (2 * bytes_per_elem * cols)), 0, -1):
        if rows % br == 0 and br % 16 == 0:
            return br
    return rows


def _ew_call(fn, name, operands, outputs, slabs=1, sel=None, into=None, after=None):
    if into is not None and not isinstance(into, (list, tuple)):
        into = [into]
    rows, cols = outputs[0][0].shape[2:]
    br = _row_block(rows, cols, sum(jnp.dtype(a.dtype).itemsize for a, _ in operands + outputs))
    n_in = len(operands)

    def pick(tok, g, s):
        if callable(tok):
            return tok(g, s)
        if tok == "g":
            return g
        if isinstance(tok, tuple):
            return s[tok[1]]
        return tok

    def spec(idx):
        return pl.BlockSpec((None, None, br, cols),
                            lambda g, i, s, idx=idx: (pick(idx[0], g, s), pick(idx[1], g, s), i, 0))

    if sel is None:
        sel = jnp.zeros((1,), jnp.int32)
    in_specs = [spec(idx) for _, idx in operands]
    arrays = [a for a, _ in operands]
    aliases = {}
    for j, buf in enumerate(into or ()):
        in_specs.append(pl.BlockSpec(memory_space=pl.ANY))
        arrays.append(buf)
        aliases[1 + n_in + j] = j
    if after is not None:
        in_specs.append(pl.BlockSpec(memory_space=pl.ANY))
        arrays.append(after)

    def body(sel_ref, *refs):
        outs = fn(*[r[...] for r in refs[:n_in]])
        for o_ref, o in zip(refs[len(arrays):], outs):
            o_ref[...] = o.astype(o_ref.dtype)

    return pl.pallas_call(
        body, name=name, out_shape=[s for s, _ in outputs],
        grid_spec=pltpu.PrefetchScalarGridSpec(
            num_scalar_prefetch=1, grid=(slabs, rows // br),
            in_specs=in_specs,
            out_specs=[spec(idx) for _, idx in outputs]),
        input_output_aliases=aliases,
        compiler_params=_params(("parallel", "parallel")),
    )(sel, *arrays)


def _as4(a):
    return a.reshape((1,) * (4 - a.ndim) + a.shape)


def _adamw(w, g, m, v):
    m = ADAM_B1 * m + (1.0 - ADAM_B1) * g
    v = ADAM_B2 * v + (1.0 - ADAM_B2) * jnp.square(g)
    m_hat = m / (1.0 - ADAM_B1 ** ADAM_STEP)
    v_hat = v / (1.0 - ADAM_B2 ** ADAM_STEP)
    delta = -ADAM_LR * (m_hat / (jnp.sqrt(v_hat) + ADAM_EPS) + ADAM_WD * w)
    return delta, m, v


def _small_adamw_call(ws, gs, ms, vs):
    n = len(ws)

    def body(*refs):
        for k in range(n):
            w, g, m, v = (refs[j * n + k][...] for j in range(4))
            outs = _adamw(w, g, m, v)
            for j in range(3):
                refs[(4 + j) * n + k][...] = outs[j]

    shapes = [jax.ShapeDtypeStruct(w.shape, F32) for w in ws]
    outs = pl.pallas_call(
        body, name="adamw_small", out_shape=shapes * 3,
        in_specs=[pl.BlockSpec(memory_space=pltpu.VMEM)] * (4 * n),
        out_specs=[pl.BlockSpec(memory_space=pltpu.VMEM)] * (3 * n),
        compiler_params=_params(),
    )(*ws, *gs, *ms, *vs)
    return outs[:n], outs[n:2 * n], outs[2 * n:]


ANY = pl.BlockSpec(memory_space=pl.ANY)


def _place():
    x, y, c = lax.axis_index("x"), lax.axis_index("y"), lax.axis_index("c")
    chips = [(1 - x, y), (x, 1 - y), (1 - x, 1 - y)]
    return x, y, c, chips


def _remote(src, dst, send_sem, recv_sem, to):
    return pltpu.make_async_remote_copy(src_ref=src, dst_ref=dst, send_sem=send_sem, recv_sem=recv_sem,
                                        device_id=to, device_id_type=MESH)


def _gather_call(bufs):
    n = len(bufs)

    def body(*refs):
        out = refs[n:2 * n]
        send_sems, recv_sems = refs[2 * n:]
        x, y, c, chips = _place()
        me_q = 2 * x + y
        sibling = (x, y, 1 - c)
        first = []
        for w in range(n):
            for j, chip in enumerate(chips):
                mine = out[w].at[c, me_q]
                first.append(_remote(mine, mine, send_sems.at[w * 3 + j], recv_sems.at[w * 3 + j], (*chip, c)))
        for cp in first:
            cp.start()
        passed = []
        for w in range(n):
            for j, (qx, qy) in enumerate(chips):
                landed = out[w].at[c, 2 * qx + qy]
                k = w * 3 + j
                _remote(landed, landed, send_sems.at[k], recv_sems.at[k], (qx, qy, c)).wait_recv()
                cp = _remote(landed, landed, send_sems.at[3 * n + k], recv_sems.at[3 * n + k], sibling)
                cp.start()
                passed.append(cp)
        for w in range(n):
            for j, (qx, qy) in enumerate(chips):
                landed = out[w].at[1 - c, 2 * qx + qy]
                k = 3 * n + w * 3 + j
                _remote(landed, landed, send_sems.at[k], recv_sems.at[k], sibling).wait_recv()
        for cp in first + passed:
            cp.wait_send()

    return pl.pallas_call(
        body, name="gather_weights",
        out_shape=[jax.ShapeDtypeStruct(a.shape, a.dtype) for a in bufs],
        in_specs=[ANY] * n, out_specs=[ANY] * n,
        input_output_aliases={w: w for w in range(n)},
        scratch_shapes=[pltpu.SemaphoreType.DMA((6 * n,)), pltpu.SemaphoreType.DMA((6 * n,))],
        compiler_params=_params(vmem=False, has_side_effects=True),
    )(*bufs)


def _sibling_send_call(items):
    n = len(items)

    def body(*refs):
        src, out = refs[:n], refs[n:2 * n]
        send_sems, recv_sems = refs[2 * n:]
        x, y, c, _ = _place()
        copies = [_remote(src[w], out[w], send_sems.at[w], recv_sems.at[w], (x, y, 1 - c)) for w in range(n)]
        for cp in copies:
            cp.start()
        for cp in copies:
            cp.wait()

    return pl.pallas_call(
        body, name="grads_to_sibling",
        out_shape=[jax.ShapeDtypeStruct(a.shape, a.dtype) for a in items],
        in_specs=[ANY] * n, out_specs=[ANY] * n,
        scratch_shapes=[pltpu.SemaphoreType.DMA((n,)), pltpu.SemaphoreType.DMA((n,))],
        compiler_params=_params(vmem=False, has_side_effects=True),
    )(*items)


def _sibling_inplace_call(name, bufs, slabs, n_pairs):
    n = len(bufs)

    def body(*refs):
        out = refs[n:2 * n]
        send_sems, recv_sems = refs[2 * n:]
        x, y, c, _ = _place()
        sibling = (x, y, 1 - c)
        pairs = [pair for w, ref in enumerate(out) for pair in slabs(ref, c, w)]
        sends = [_remote(s, s, send_sems.at[k], recv_sems.at[k], sibling) for k, (s, _) in enumerate(pairs)]
        for cp in sends:
            cp.start()
        for k, (_, r) in enumerate(pairs):
            _remote(r, r, send_sems.at[k], recv_sems.at[k], sibling).wait_recv()
        for cp in sends:
            cp.wait_send()

    return pl.pallas_call(
        body, name=name,
        out_shape=[jax.ShapeDtypeStruct(a.shape, a.dtype) for a in bufs],
        in_specs=[ANY] * n, out_specs=[ANY] * n,
        input_output_aliases={w: w for w in range(n)},
        scratch_shapes=[pltpu.SemaphoreType.DMA((n_pairs,)), pltpu.SemaphoreType.DMA((n_pairs,))],
        compiler_params=_params(vmem=False, has_side_effects=True),
    )(*bufs)


HBM_SPEC = pl.BlockSpec(memory_space=pltpu.HBM)
SEM_SPEC = pl.BlockSpec(memory_space=pltpu.SEMAPHORE)
DATAFLOW_EFFECT = pltpu.SideEffectType.DATAFLOW_SIDE_EFFECTING


def _exchange_start(name, bufs, copies, n_copies, after):
    n = len(bufs)

    def body(*refs):
        ins, send_sems, recv_sems, token = refs[:n], refs[n + 1], refs[n + 2], refs[-1]
        for k, (src, dst, to) in enumerate(copies(ins)):
            _remote(src, dst, send_sems.at[k], recv_sems.at[k], to).start()
        token[...] = jnp.zeros_like(token)

    outs = pl.pallas_call(
        body, name=name,
        out_shape=(pltpu.SemaphoreType.DMA((n_copies,)), pltpu.SemaphoreType.DMA((n_copies,)),
                   *[pltpu.HBM(b.shape, b.dtype) for b in bufs], jax.ShapeDtypeStruct((SUBLANES, 128), F32)),
        in_specs=[HBM_SPEC] * n + [ANY],
        out_specs=(SEM_SPEC, SEM_SPEC, *[HBM_SPEC] * n, pl.BlockSpec(memory_space=pltpu.VMEM)),
        input_output_aliases={w: w + 2 for w in range(n)},
        compiler_params=pltpu.CompilerParams(has_side_effects=DATAFLOW_EFFECT),
    )(*[pltpu.with_memory_space_constraint(b, pltpu.HBM) for b in bufs], after)
    return outs[0], outs[1], list(outs[2:2 + n]), outs[-1]


def _exchange_wait(name, send_sems, recv_sems, bufs, copies, after):
    n = len(bufs)

    def body(*refs):
        ins, send_sems, recv_sems = refs[:n], refs[n], refs[n + 1]
        for k, (src, dst, to) in enumerate(copies(ins)):
            cp = _remote(src, dst, send_sems.at[k], recv_sems.at[k], to)
            cp.wait_send()
            cp.wait_recv()

    return pl.pallas_call(
        body, name=name,
        out_shape=[pltpu.HBM(b.shape, b.dtype) for b in bufs],
        in_specs=[HBM_SPEC] * n + [SEM_SPEC, SEM_SPEC, ANY],
        out_specs=[HBM_SPEC] * n,
        input_output_aliases={w: w for w in range(n)},
        compiler_params=pltpu.CompilerParams(has_side_effects=DATAFLOW_EFFECT),
    )(*bufs, send_sems, recv_sems, after)


def _gather_copies(refs):
    x, y, c, chips = _place()
    mine = 2 * (2 * x + y) + c
    return [(ref.at[mine], ref.at[mine], (qx, qy, c)) for ref in refs for qx, qy in chips]


def _gather_forward_slabs(ref, c, w):
    x, y, _, chips = _place()
    return [(ref.at[2 * (2 * qx + qy) + c], ref.at[2 * (2 * qx + qy) + 1 - c]) for qx, qy in chips]


def _device_peers():
    x, y, c, _ = _place()
    return 4 * x + 2 * y + c, [(k, (x ^ ((k >> 2) & 1), y ^ ((k >> 1) & 1), c ^ (k & 1))) for k in range(1, 8)]


def _small_scatter_copies(refs):
    me, peers = _device_peers()
    return [(refs[0].at[me ^ k], refs[1].at[me], to) for k, to in peers]


def _small_spread_copies(refs):
    me, peers = _device_peers()
    return [(refs[0].at[me], refs[0].at[me], to) for _, to in peers]


def _sibling_copies(refs):
    n = len(refs) // 2
    x, y, c, _ = _place()
    return [(refs[w], refs[n + w], (x, y, 1 - c)) for w in range(n)]


def _owner_copies(refs):
    n = len(refs) // 2
    x, y, c, chips = _place()
    return [(refs[w].at[2 * qx + qy], refs[n + w].at[j], (qx, qy, c))
            for w in range(n) for j, (qx, qy) in enumerate(chips)]


N_DEVICES = 8
SMALL_ROWS = 616


SMALL = ("norm_mix_g", "conv_w", "conv_b", "lru_w_a", "lru_b_a", "lru_w_x", "lru_b_x", "lru_lambda",
         "sgu_ln_g", "sgu_ln_b", "sgu_w_s", "sgu_b_s", "norm_ffn_g", "final_norm_g")
WEIGHTS = ("norm_mix_g", "w_in", "conv_w", "conv_b", "lru_w_a", "lru_b_a", "lru_w_x", "lru_b_x", "lru_lambda",
           "sgu_ln_g", "sgu_ln_b", "sgu_w_s", "sgu_b_s", "w_branch_a", "w_branch_b", "w_out", "norm_ffn_g",
           "w_up", "w_down", "final_norm_g")
PACK_ALIGN = SUBLANES * 128


def _pack_small(gs):
    parts = []
    for k in SMALL:
        flat = gs[k].reshape(-1)
        parts.append(jnp.pad(flat, (0, -flat.size % PACK_ALIGN)))
    flat = jnp.concatenate(parts)
    flat = jnp.pad(flat, (0, N_DEVICES * SMALL_ROWS * 128 - flat.size))
    return flat.reshape(N_DEVICES, SMALL_ROWS, 128)


def _unpack_small(buf, like):
    flat = buf.reshape(-1)
    out, off = {}, 0
    for k in SMALL:
        size = like[k].size
        out[k] = flat[off:off + size].reshape(like[k].shape)
        off += size + (-size % PACK_ALIGN)
    return out


def _as_rows(a):
    return a.reshape(-1, a.shape[-1])


def kernel(x, norm_mix_g, w_in, conv_w, conv_b, lru_w_a, lru_b_a, lru_w_x, lru_b_x, lru_lambda, sgu_ln_g, sgu_ln_b, sgu_w_s, sgu_b_s, w_branch_a, w_branch_b, w_out, norm_ffn_g, w_up, w_down, final_norm_g, loss_target, m_norm_mix_g, m_w_in, m_conv_w, m_conv_b, m_lru_w_a, m_lru_b_a, m_lru_w_x, m_lru_b_x, m_lru_lambda, m_sgu_ln_g, m_sgu_ln_b, m_sgu_w_s, m_sgu_b_s, m_w_branch_a, m_w_branch_b, m_w_out, m_norm_ffn_g, m_w_up, m_w_down, m_final_norm_g, v_norm_mix_g, v_w_in, v_conv_w, v_conv_b, v_lru_w_a, v_lru_b_a, v_lru_w_x, v_lru_b_x, v_lru_lambda, v_sgu_ln_g, v_sgu_ln_b, v_sgu_w_s, v_sgu_b_s, v_w_branch_a, v_w_branch_b, v_w_out, v_norm_ffn_g, v_w_up, v_w_down, v_final_norm_g):
    w = dict(norm_mix_g=norm_mix_g, w_in=w_in, conv_w=conv_w, conv_b=conv_b, lru_w_a=lru_w_a, lru_b_a=lru_b_a,
             lru_w_x=lru_w_x, lru_b_x=lru_b_x, lru_lambda=lru_lambda, sgu_ln_g=sgu_ln_g, sgu_ln_b=sgu_ln_b,
             sgu_w_s=sgu_w_s, sgu_b_s=sgu_b_s, w_branch_a=w_branch_a, w_branch_b=w_branch_b, w_out=w_out,
             norm_ffn_g=norm_ffn_g, w_up=w_up, w_down=w_down, final_norm_g=final_norm_g)
    m = dict(norm_mix_g=m_norm_mix_g, w_in=m_w_in, conv_w=m_conv_w, conv_b=m_conv_b, lru_w_a=m_lru_w_a,
             lru_b_a=m_lru_b_a, lru_w_x=m_lru_w_x, lru_b_x=m_lru_b_x, lru_lambda=m_lru_lambda,
             sgu_ln_g=m_sgu_ln_g, sgu_ln_b=m_sgu_ln_b, sgu_w_s=m_sgu_w_s, sgu_b_s=m_sgu_b_s,
             w_branch_a=m_w_branch_a, w_branch_b=m_w_branch_b, w_out=m_w_out, norm_ffn_g=m_norm_ffn_g,
             w_up=m_w_up, w_down=m_w_down, final_norm_g=m_final_norm_g)
    v = dict(norm_mix_g=v_norm_mix_g, w_in=v_w_in, conv_w=v_conv_w, conv_b=v_conv_b, lru_w_a=v_lru_w_a,
             lru_b_a=v_lru_b_a, lru_w_x=v_lru_w_x, lru_b_x=v_lru_b_x, lru_lambda=v_lru_lambda,
             sgu_ln_g=v_sgu_ln_g, sgu_ln_b=v_sgu_ln_b, sgu_w_s=v_sgu_w_s, sgu_b_s=v_sgu_b_s,
             w_branch_a=v_w_branch_a, w_branch_b=v_w_branch_b, w_out=v_w_out, norm_ffn_g=v_norm_ffn_g,
             w_up=v_w_up, w_down=v_w_down, final_norm_g=v_final_norm_g)
    core = lax.axis_index("c")
    chip = 2 * lax.axis_index("x") + lax.axis_index("y")
    sel = jnp.stack([core, 1 - core, chip, 2 * chip + core]).astype(jnp.int32)
    this_core, other_core, this_chip = ("sel", 0), ("sel", 1), ("sel", 2)
    sds = jax.ShapeDtypeStruct

    ts = TOKEN_TILE

    def after_all(arrays):
        return jnp.stack([a[(0,) * a.ndim].astype(F32) for a in arrays])

    halves ={k: (w[k].shape[1] // 2, w[k].shape[2]) for k in BIG}

    def half_view(k, a):
        return a.reshape((2 * N_QUARTERS,) + halves[k])

    def full_view(k, a):
        r2, cols = halves[k]
        if k in ("w_in", "w_up"):
            return a.reshape(1, N_QUARTERS, 2 * r2, cols)
        return a.reshape(1, 2 * N_QUARTERS * r2, cols)

    layer_bufs = [{}, {}]

    def cast_weights(k, after):
        _, r, cols = w[k].shape
        w4 = w[k].reshape(DEPTH, 1, r, cols)
        outs = _ew_call(lambda a, b: (a, b), "cast_weights", [(w4, (0, 0)), (w4, (1, 0))],
                        [(sds((1, N_QUARTERS, r, cols), BF), (0, this_chip))] * DEPTH, 1, sel, after=after)
        for l in range(DEPTH):
            layer_bufs[l][k] = half_view(k, outs[l])

    conv_buf = lax.dynamic_update_slice_in_dim(
        jnp.zeros((DEPTH, N_QUARTERS) + conv_w.shape[1:], F32), conv_w[:, None], chip, axis=1)
    sm = {k: w[k] for k in SMALL}
    sm["conv_w"] = _gather_call([conv_buf])[0].transpose(0, 2, 1, 3).reshape(DEPTH, CONV_WIDTH, D_RNN)

    def gather_start(tag, l, keys, after):
        bufs = [layer_bufs[l][k] for k in keys]
        return _exchange_start(f"gather_start_{tag}", bufs, _gather_copies, 3 * len(keys), after)

    def gather_finish(tag, keys, started, after):
        send_sems, recv_sems, thru, _ = started
        landed = _exchange_wait(f"gather_wait_{tag}", send_sems, recv_sems, thru, _gather_copies, after)
        landed = _sibling_inplace_call("gather_forward", landed, _gather_forward_slabs, 3 * len(keys))
        return {k: full_view(k, a) for k, a in zip(keys, landed)}

    first, rest = ("w_in",), tuple(k for k in BIG if k != "w_in")
    cast_weights("w_in", None)
    started_a = gather_start("0a", 0, first, sm["conv_w"])
    for k in rest:
        cast_weights(k, started_a[3])
    started_b = gather_start("0b", 0, rest, started_a[3])
    started_1 = gather_start("1", 1, BIG, started_b[3])
    p0, p1 = _layer_small(sm, 0, sel[0:1]), _layer_small(sm, 1, sel[0:1])
    h0 = _norm_call(x[0], p0["g1"], ts)
    ready = after_all([started_1[3], h0] + [p[k] for p in (p0, p1) for k in ("wa", "wx", "wm", "bsb", "sp")])
    big0 = gather_finish("0a", first, started_a, ready)
    sv0 = _layer_fwd_mix(x[0], big0, p0, ts, h0)
    big0.update(gather_finish("0b", rest, started_b, sv0["yb_pre"]))
    x_mid = _layer_fwd_out(sv0, big0, ts)
    big1 = gather_finish("1", BIG, started_1, x_mid)
    sv1 = _layer_fwd_mix(x_mid, big1, p1, ts)
    x_out = _layer_fwd_out(sv1, big1, ts)
    dx, loss, dgf = _loss_call(x_out, loss_target[0], final_norm_g.reshape(1, -1), ts)

    def pair_start(tag, gb, after):
        sends = [gb[k][1] for k in gb]
        zones = [lax.empty(a.shape, BF) for a in sends]
        return _exchange_start(f"pair_start_{tag}", sends + zones, _sibling_copies, len(sends), after)

    def reduce_start(tag, gb, after, pair=None):
        keys = tuple(gb)
        if pair is None:
            from_sibling = _sibling_send_call([gb[k][1] for k in keys])
        else:
            done = _exchange_wait(f"pair_wait_{tag}", pair[0], pair[1], pair[2], _sibling_copies, after)
            from_sibling = done[len(keys):]
        sums = [
            _ew_call(lambda a, b: (a + b.astype(F32),), "pair_sum", [(gb[k][0][None], (0, "g")), (r[None], (0, "g"))],
                     [(sds((1,) + r.shape, BF), (0, "g"))], N_QUARTERS)[0][0]
            for k, r in zip(keys, from_sibling)]
        zones = [lax.empty((3,) + a.shape[1:], BF) for a in sums]
        started = _exchange_start(f"reduce_start_{tag}", sums + zones, _owner_copies, 3 * len(keys), after)
        return keys, started

    def reduce_finish(tag, l, keys_started, after, reduced):
        keys, (send_sems, recv_sems, thru, _) = keys_started
        done = _exchange_wait(f"reduce_wait_{tag}", send_sems, recv_sems, thru, _owner_copies, after)
        sums, zones = done[:len(keys)], done[len(keys):]
        for i, k in enumerate(keys):
            r2, cols = halves[k]
            reduced[k] = _ew_call(
                lambda a, b, c, d: (((a.astype(F32) + b.astype(F32)) + c.astype(F32)) + d.astype(F32),),
                "quarter_sum", [(sums[i][None], (0, this_chip))] + [(zones[i][None], (0, j)) for j in range(3)],
                [(sds((DEPTH, 2, r2, cols), F32), (l, this_core))], 1, sel, into=reduced.get(k))[0]

    def behind(params, key, started):
        return dict(params, **{key: params[key] + started[1][3][0, 0]})

    dx1, gb_ffn, gs1 = _layer_bwd_ffn(dx, sv1, big1, ts)
    merge_out, gb_merge = _layer_bwd_merge(dx1, sv1, big1, ts)
    dx_mid, gb_in, gs1_mix = _layer_bwd_branches(dx1, merge_out, sv1, big1, lru_lambda[1], ts)
    gb_1 = {**gb_ffn, **gb_merge, **gb_in}
    pair_1 = pair_start("1", gb_1, dx_mid)
    sv0["p"] = behind(sv0["p"], "g2", (None, pair_1))
    dx1, gb_ffn, gs0 = _layer_bwd_ffn(dx_mid, sv0, big0, ts)
    exchange_1 = reduce_start("1", gb_1, dx1, pair_1)
    exchange_0a = reduce_start("0a", gb_ffn, exchange_1[1][3])
    merge_out, gb_merge = _layer_bwd_merge(dx1, sv0, big0, ts)
    exchange_0b = reduce_start("0b", gb_merge, exchange_0a[1][3])
    sv0["p"] = behind(sv0["p"], "lg", exchange_0b)
    grad_x, gb_in, gs0_mix = _layer_bwd_branches(dx1, merge_out, sv0, big0, lru_lambda[0], ts)
    exchange_0c = reduce_start("0c", gb_in, exchange_0b[1][3])
    layer_gs = [{**gs0, **gs0_mix}, {**gs1, **gs1_mix}]
    gs = {k: jnp.stack([g[k] for g in layer_gs]) for k in layer_gs[0]}
    gs["final_norm_g"] = dgf[0]

    me = ("sel", 3)
    piece = (1, N_DEVICES, SMALL_ROWS, 128)
    packed = _pack_small(gs).reshape(piece)
    scatter = _exchange_start("small_scatter_start", [packed[0], lax.empty(piece[1:], F32)], _small_scatter_copies,
                              N_DEVICES - 1, exchange_0c[1][3])
    reduced = {}
    reduce_finish("1", 1, exchange_1, scatter[3], reduced)
    reduce_finish("0a", 0, exchange_0a, reduced["w_in"], reduced)
    reduce_finish("0b", 0, exchange_0b, reduced["w_down"], reduced)

    def swap_slabs(ref, c, i):
        layers = (1,) if BIG[i] == "w_in" else range(DEPTH)
        return [(ref.at[l, c], ref.at[l, 1 - c]) for l in layers]

    swapped = dict(zip(BIG, _sibling_inplace_call("grads_swap_halves", [reduced[k] for k in BIG], swap_slabs,
                                                  DEPTH * len(BIG) - 1)))

    def adamw_layers(k, grad, layer, into, after=None):
        if layer is None:
            views = [_as4(_as_rows(a)) for a in (w[k], grad, m[k], v[k])]
            idx = (0, 0)
        else:
            views = [a.reshape((1,) + w[k].shape) for a in (w[k], grad, m[k], v[k])]
            idx = (0, layer)
        return _ew_call(_adamw, "adamw_big", [(a, idx) for a in views], [(sds(views[0].shape, F32), idx)] * 3,
                        into=into, after=after)

    updated, last_update = {}, None
    for k in BIG:
        updated[k] = adamw_layers(k, swapped[k], 1 if k == "w_in" else None, None, last_update)
        last_update = updated[k][0]
    scattered = _exchange_wait("small_scatter_wait", scatter[0], scatter[1], scatter[2], _small_scatter_copies,
                               last_update)
    summed = _ew_call(
        lambda *parts: (functools.reduce(lambda a, b: a + b, parts),), "small_sum",
        [(scattered[0][None], (0, me))]
        + [(scattered[1][None], (0, lambda g, s, k=k: s[3] ^ k)) for k in range(1, N_DEVICES)],
        [(sds(piece, F32), (0, me))], 1, sel)[0]
    spread = _exchange_start("small_spread_start", [summed[0]], _small_spread_copies, N_DEVICES - 1, summed)
    reduced["w_in"] = swapped["w_in"]
    reduce_finish("0c", 0, exchange_0c, spread[3], reduced)
    last = _sibling_inplace_call("grads_swap_last", [reduced["w_in"]],
                                 lambda ref, c, i: [(ref.at[0, c], ref.at[0, 1 - c])], 1)[0]
    swapped["w_in"] = last
    updated["w_in"] = adamw_layers("w_in", last, 0, updated["w_in"])
    grads_big = {k: swapped[k].reshape(w[k].shape) for k in BIG}
    delta, new_m, new_v = ({k: updated[k][j].reshape(w[k].shape) for k in BIG} for j in range(3))
    gathered_small = _exchange_wait("small_spread_wait", spread[0], spread[1], spread[2], _small_spread_copies,
                                    updated["w_in"][0])[0]

    like = {k: jax.ShapeDtypeStruct(sm[k].shape, F32) for k in SMALL}
    grads_small = _unpack_small(gathered_small, like)
    conv_q = grads_small["conv_w"].reshape(DEPTH, CONV_WIDTH, N_QUARTERS, D_RNN // N_QUARTERS)
    grads_small["conv_w"] = lax.dynamic_index_in_dim(conv_q, chip, axis=2, keepdims=False)
    outs = _small_adamw_call(*[[_as_rows(d[k]) for k in SMALL] for d in (w, grads_small, m, v)])
    for d, o in zip((delta, new_m, new_v), outs):
        for k, a in zip(SMALL, o):
            d[k] = a.reshape(w[k].shape)

    grads = {**grads_big, **grads_small}
    total = lax.psum(loss[0, 0], ("x", "y", "c"))
    return (total, grad_x[None], *[grads[k] for k in WEIGHTS], *[delta[k] for k in WEIGHTS],
            *[new_m[k] for k in WEIGHTS], *[new_v[k] for k in WEIGHTS])
```

```python
import functools
import math

import jax
import jax.numpy as jnp
from jax import lax
from jax.experimental import pallas as pl
from jax.experimental.pallas import tpu as pltpu

F32 = jnp.float32
BF = jnp.bfloat16

DEPTH = 2
D_MODEL = 1024
D_RNN = 1280
D_SGU = 1024
D_FF = 4096
D_IN = 2 * D_RNN + 2 * D_SGU + 2 * D_MODEL
N_QUARTERS = 4
Q_IN = D_IN // N_QUARTERS
Q_FF = D_FF // N_QUARTERS
RNN_HEADS = 20
RNN_HEAD_DIM = 64
LRU_GROUP = 256
N_LRU_GROUPS = D_RNN // LRU_GROUP
HEADS_PER_GROUP = LRU_GROUP // RNN_HEAD_DIM
CONV_WIDTH = 4
LRU_C = 8.0
SGU_GROUPS = 8
SGU_BLOCK = 128
CHUNK = 64
EPS = 1e-6

ADAM_LR = 0.001
ADAM_B1 = 0.9
ADAM_B2 = 0.999
ADAM_EPS = 1e-08
ADAM_WD = 0.01
ADAM_STEP = 10

SUBLANES = 8
TOKEN_TILE = 512
VMEM_LIMIT_BYTES = 56 * 1024 * 1024

MESH = pl.DeviceIdType.MESH


def _params(semantics=None, vmem=True, **kw):
    return pltpu.CompilerParams(
        dimension_semantics=semantics,
        vmem_limit_bytes=VMEM_LIMIT_BYTES if vmem else None,
        **kw,
    )


def _dot(a, b):
    return jnp.dot(a, b, preferred_element_type=F32)


def _dot_nt(a, b):
    return lax.dot_general(a, b, (((1,), (1,)), ((), ())), preferred_element_type=F32)


def _dot_tn(a, b):
    return lax.dot_general(a, b, (((0,), (0,)), ((), ())), preferred_element_type=F32)


_GELU_C = math.sqrt(2.0 / math.pi)
_GELU_A = 0.044715


def _gelu(x):
    return 0.5 * x * (1.0 + jnp.tanh(_GELU_C * (x + _GELU_A * x * x * x)))


def _gelu_and_grad(x):
    x2 = x * x
    t = jnp.tanh(_GELU_C * (x + _GELU_A * x2 * x))
    du = _GELU_C * (1.0 + 3.0 * _GELU_A * x2)
    return 0.5 * x * (1.0 + t), 0.5 * (1.0 + t) + 0.5 * x * (1.0 - t * t) * du


def _rms_stats(x):
    return lax.rsqrt(jnp.mean(x * x, axis=-1, keepdims=True) + EPS)


def _rms_bwd(dy, x, g):
    rs = _rms_stats(x)
    n = x * rs
    dn = dy * g
    dx = rs * (dn - n * jnp.mean(dn * n, axis=-1, keepdims=True))
    return dx, dy * n


def _row_sum(x):
    return jnp.sum(x, axis=0, keepdims=True)


def _tile_spec(ts, width, col=0):
    return pl.BlockSpec((ts, width), lambda i, col=col: (i, col))


def _full_spec(shape):
    zeros = (0,) * len(shape)
    return pl.BlockSpec(shape, lambda *_: zeros)


def _layer_spec(w, layer):
    zeros = (0,) * (w.ndim - 1)
    return pl.BlockSpec((None,) + tuple(w.shape[1:]), lambda *_: (layer,) + zeros)


def _norm_call(x, g, ts):
    s = x.shape[0]

    def body(x_ref, g_ref, h_ref):
        xv = x_ref[...]
        h_ref[...] = (xv * _rms_stats(xv) * g_ref[...]).astype(BF)

    return pl.pallas_call(
        body, name="norm_fwd", grid=(s // ts,),
        in_specs=[_tile_spec(ts, D_MODEL), _full_spec((1, D_MODEL))],
        out_specs=_tile_spec(ts, D_MODEL),
        out_shape=jax.ShapeDtypeStruct((s, D_MODEL), BF),
        compiler_params=_params(("parallel",)),
    )(x, g)


def _inproj_call(h, w_in, layer, ts):
    s = h.shape[0]

    def body(h_ref, w_ref, o_ref):
        o_ref[...] = _dot(h_ref[...], w_ref[...]).astype(BF)

    return pl.pallas_call(
        body, name="inproj_fwd", grid=(N_QUARTERS, s // ts),
        in_specs=[
            pl.BlockSpec((ts, D_MODEL), lambda q, i: (i, 0)),
            pl.BlockSpec((None, None, D_MODEL, Q_IN), lambda q, i: (layer, q, 0, 0)),
        ],
        out_specs=pl.BlockSpec((ts, Q_IN), lambda q, i: (i, q)),
        out_shape=jax.ShapeDtypeStruct((s, D_IN), BF),
        compiler_params=_params(("parallel", "parallel")),
    )(h, w_in)


def _shift_down(x, tail, s):
    xr = pltpu.roll(x, s, 0)
    tr = pltpu.roll(tail, s, 0)
    row = lax.broadcasted_iota(jnp.int32, tail.shape, 0)
    top = jnp.where(row < s, tr, xr[0:SUBLANES])
    return jnp.concatenate([top, xr[SUBLANES:]], axis=0)


def _shift_up(x, head, s):
    t = x.shape[0]
    xr = pltpu.roll(x, t - s, 0)
    hr = pltpu.roll(head, SUBLANES - s, 0)
    row = lax.broadcasted_iota(jnp.int32, head.shape, 0)
    bottom = jnp.where(row >= SUBLANES - s, hr, xr[t - SUBLANES:])
    return jnp.concatenate([xr[: t - SUBLANES], bottom], axis=0)


def _conv_fwd(x, tail, cw_ref, cb_ref):
    out = cb_ref[...] + cw_ref[CONV_WIDTH - 1:CONV_WIDTH, :] * x
    for s in range(1, CONV_WIDTH):
        k = CONV_WIDTH - 1 - s
        out = out + cw_ref[k:k + 1, :] * _shift_down(x, tail, s)
    return out


def _group_dot(x_bf, w_ref, dot):
    cols = [dot(x_bf[:, g * LRU_GROUP:(g + 1) * LRU_GROUP], w_ref[g]) for g in range(N_LRU_GROUPS)]
    return jnp.concatenate(cols, axis=1)


def _lru_gates(xr, wa_ref, wx_ref, ba_ref, bx_ref, sp_ref):
    xb = xr.astype(BF)
    r = jax.nn.sigmoid(_group_dot(xb, wa_ref, _dot) + ba_ref[...])
    i = jax.nn.sigmoid(_group_dot(xb, wx_ref, _dot) + bx_ref[...])
    log_a = (-LRU_C * r) * sp_ref[...]
    a = jnp.exp(log_a)
    nrm2 = -jnp.tanh(log_a) * (a * a + 1.0)
    inv_nrm = lax.rsqrt(jnp.maximum(nrm2, 1e-36))
    return r, i, a, nrm2 * inv_nrm, inv_nrm


def _linear_scan(a, b, carry, al_ref, bl_ref, h_ref, reverse):
    t, c = a.shape
    rowm = lax.broadcasted_iota(jnp.int32, (t, c), 0) & (SUBLANES - 1)
    for d in (1, 2, 4):
        if reverse:
            keep, sh = rowm < SUBLANES - d, t - d
        else:
            keep, sh = rowm >= d, d
        a_sh = jnp.where(keep, pltpu.roll(a, sh, 0), 1.0)
        b_sh = jnp.where(keep, pltpu.roll(b, sh, 0), 0.0)
        b = a * b_sh + b
        a = a * a_sh
    al_ref[...] = a
    bl_ref[...] = b
    groups = t // SUBLANES

    def step(j, state):
        jj = groups - 1 - j if reverse else j
        off = pl.multiple_of(jj * SUBLANES, SUBLANES)
        rows = bl_ref[pl.ds(off, SUBLANES), :] + al_ref[pl.ds(off, SUBLANES), :] * state
        h_ref[pl.ds(off, SUBLANES), :] = rows
        last = rows[0:1, :] if reverse else rows[SUBLANES - 1:SUBLANES, :]
        return jnp.broadcast_to(last, (SUBLANES, c))

    out = lax.fori_loop(0, groups, step, jnp.broadcast_to(carry, (SUBLANES, c)))
    return out[0:1, :]


def _rnn_fwd_call(proj, wa, wx, ba, bx, sp, cw, cb, ts):
    s = proj.shape[0]

    def body(xg_ref, wa_ref, wx_ref, ba_ref, bx_ref, sp_ref, cw_ref, cb_ref, xr_ref, hr_ref, ya_ref,
             tail_sc, carry_sc, al_sc, bl_sc, h_sc):
        @pl.when(pl.program_id(0) == 0)
        def _():
            tail_sc[...] = jnp.zeros_like(tail_sc)
            carry_sc[...] = jnp.zeros_like(carry_sc)

        x = xg_ref[:, :D_RNN].astype(F32)
        g = xg_ref[:, D_RNN:].astype(F32)
        xr = _conv_fwd(x, tail_sc[...], cw_ref, cb_ref)
        tail_sc[...] = x[ts - SUBLANES:, :]
        xr_ref[...] = xr.astype(BF)
        _, i, a, nrm, _ = _lru_gates(xr, wa_ref, wx_ref, ba_ref, bx_ref, sp_ref)
        carry_sc[...] = _linear_scan(a, nrm * (i * xr), carry_sc[...], al_sc, bl_sc, h_sc, False)
        h = h_sc[...]
        hr_ref[...] = h.astype(BF)
        ya_ref[...] = (h * _gelu(g)).astype(BF)

    gw = (N_LRU_GROUPS, LRU_GROUP, LRU_GROUP)
    return pl.pallas_call(
        body, name="rnn_fwd", grid=(s // ts,),
        in_specs=[_tile_spec(ts, 2 * D_RNN), _full_spec(gw), _full_spec(gw),
                  _full_spec((1, D_RNN)), _full_spec((1, D_RNN)), _full_spec((1, D_RNN)),
                  _full_spec((CONV_WIDTH, D_RNN)), _full_spec((1, D_RNN))],
        out_specs=[_tile_spec(ts, D_RNN)] * 3,
        out_shape=[jax.ShapeDtypeStruct((s, D_RNN), BF)] * 3,
        scratch_shapes=[pltpu.VMEM((SUBLANES, D_RNN), F32), pltpu.VMEM((1, D_RNN), F32),
                        pltpu.VMEM((ts, D_RNN), F32), pltpu.VMEM((ts, D_RNN), F32),
                        pltpu.VMEM((ts, D_RNN), F32)],
        compiler_params=_params(("arbitrary",)),
    )(proj, wa, wx, ba, bx, sp, cw, cb)


def _layernorm_fwd(x):
    mu = jnp.mean(x, axis=-1, keepdims=True)
    xc = x - mu
    rstd = lax.rsqrt(jnp.mean(xc * xc, axis=-1, keepdims=True) + EPS)
    return xc * rstd, rstd


def _sgu_mix(vn_bf, wm_ref, bsb_ref, ts):
    rows = []
    for blk in range(ts // SGU_BLOCK):
        r0 = blk * SGU_BLOCK
        cols = [
            _dot(wm_ref[g], vn_bf[r0:r0 + SGU_BLOCK, g * SGU_BLOCK:(g + 1) * SGU_BLOCK]) + bsb_ref[g]
            for g in range(SGU_GROUPS)
        ]
        rows.append(jnp.concatenate(cols, axis=1))
    return jnp.concatenate(rows, axis=0)


def _sgu_fwd_call(proj, wm, bsb, lg, lb, ts):
    s = proj.shape[0]

    def body(uv_ref, wm_ref, bsb_ref, lg_ref, lb_ref, yb_ref):
        gu = _gelu(uv_ref[:, :D_SGU].astype(F32))
        gv = _gelu(uv_ref[:, D_SGU:2 * D_SGU].astype(F32))
        nh, _ = _layernorm_fwd(gv)
        vn = (nh * lg_ref[...] + lb_ref[...]).astype(BF)
        yb_ref[...] = (gu * _sgu_mix(vn, wm_ref, bsb_ref, ts)).astype(BF)

    sw = (SGU_GROUPS, SGU_BLOCK, SGU_BLOCK)
    return pl.pallas_call(
        body, name="sgu_fwd", grid=(s // ts,),
        in_specs=[_tile_spec(ts, 2 * D_RNN, 1), _full_spec(sw), _full_spec(sw),
                  _full_spec((1, D_SGU)), _full_spec((1, D_SGU))],
        out_specs=_tile_spec(ts, D_SGU),
        out_shape=jax.ShapeDtypeStruct((s, D_SGU), BF),
        compiler_params=_params(("parallel",)),
    )(proj, wm, bsb, lg, lb)


_GATE_COL0 = (2 * D_RNN + 2 * D_SGU) // 512


def _gate_specs(ts):
    return [_tile_spec(ts, 512, _GATE_COL0 + j) for j in range(4)]


def _merge_call(x, proj, ya_pre, yb_pre, w_ba, w_bb, w_out, g2, layer, ts):
    s = x.shape[0]

    def body(x_ref, ga0, ga1, gb0, gb1, ya_ref, yb_ref, wa_ref, wb_ref, wo_ref, g2_ref,
             x1_ref, yao_ref, ybo_ref, mg_ref, h2_ref):
        ya = _dot(ya_ref[...], wa_ref[...])
        yb = _dot(yb_ref[...], wb_ref[...])
        sa = jax.nn.sigmoid(jnp.concatenate([ga0[...], ga1[...]], axis=1).astype(F32))
        sb = jax.nn.sigmoid(jnp.concatenate([gb0[...], gb1[...]], axis=1).astype(F32))
        merged = (sa * ya + sb * yb).astype(BF)
        x1 = x_ref[...] + _dot(merged, wo_ref[...])
        x1_ref[...] = x1
        yao_ref[...] = ya.astype(BF)
        ybo_ref[...] = yb.astype(BF)
        mg_ref[...] = merged
        h2_ref[...] = (x1 * _rms_stats(x1) * g2_ref[...]).astype(BF)

    act = jax.ShapeDtypeStruct((s, D_MODEL), BF)
    return pl.pallas_call(
        body, name="merge_fwd", grid=(s // ts,),
        in_specs=[_tile_spec(ts, D_MODEL)] + _gate_specs(ts) + [
            _tile_spec(ts, D_RNN), _tile_spec(ts, D_SGU),
            _layer_spec(w_ba, layer), _layer_spec(w_bb, layer), _layer_spec(w_out, layer),
            _full_spec((1, D_MODEL))],
        out_specs=[_tile_spec(ts, D_MODEL)] * 5,
        out_shape=[jax.ShapeDtypeStruct((s, D_MODEL), F32), act, act, act, act],
        compiler_params=_params(("parallel",)),
    )(x, proj, proj, proj, proj, ya_pre, yb_pre, w_ba, w_bb, w_out, g2)


def _ffn_call(x1, h2, w_up, w_down, layer, ts):
    s = x1.shape[0]

    def body(x1_ref, h2_ref, wu_ref, wd_ref, x2_ref, p_ref):
        h2v = h2_ref[...]
        acc = x1_ref[...]
        for q in range(N_QUARTERS):
            p = _dot(h2v, wu_ref[q])
            p_ref[:, q * Q_FF:(q + 1) * Q_FF] = p.astype(BF)
            f = jnp.square(jnp.maximum(p, 0.0)).astype(BF)
            acc = acc + _dot(f, wd_ref[q * Q_FF:(q + 1) * Q_FF, :])
        x2_ref[...] = acc

    return pl.pallas_call(
        body, name="ffn_fwd", grid=(s // ts,),
        in_specs=[_tile_spec(ts, D_MODEL), _tile_spec(ts, D_MODEL),
                  pl.BlockSpec((None, N_QUARTERS, D_MODEL, Q_FF), lambda i: (layer, 0, 0, 0)),
                  pl.BlockSpec((None, D_FF, D_MODEL), lambda i: (layer, 0, 0))],
        out_specs=[_tile_spec(ts, D_MODEL), _tile_spec(ts, D_FF)],
        out_shape=[jax.ShapeDtypeStruct((s, D_MODEL), F32), jax.ShapeDtypeStruct((s, D_FF), BF)],
        compiler_params=_params(("parallel",)),
    )(x1, h2, w_up, w_down)


def _loss_call(x, target, gf, ts):
    s = x.shape[0]

    def body(x_ref, t_ref, g_ref, dx_ref, loss_ref, dg_ref):
        @pl.when(pl.program_id(0) == 0)
        def _():
            loss_ref[...] = jnp.zeros_like(loss_ref)
            dg_ref[...] = jnp.zeros_like(dg_ref)

        xv = x_ref[...]
        gv = g_ref[...]
        err = xv * _rms_stats(xv) * gv - t_ref[...]
        part = 0.5 * jnp.sum(jnp.mean(err * err, axis=-1, keepdims=True), axis=0, keepdims=True)
        loss_ref[...] += jnp.broadcast_to(part, loss_ref.shape)
        dx, dg = _rms_bwd(err * (1.0 / D_MODEL), xv, gv)
        dx_ref[...] = dx
        dg_ref[...] += _row_sum(dg)

    return pl.pallas_call(
        body, name="loss_head", grid=(s // ts,),
        in_specs=[_tile_spec(ts, D_MODEL), _tile_spec(ts, D_MODEL), _full_spec((1, D_MODEL))],
        out_specs=[_tile_spec(ts, D_MODEL), _full_spec((1, 128)), _full_spec((1, D_MODEL))],
        out_shape=[jax.ShapeDtypeStruct((s, D_MODEL), F32), jax.ShapeDtypeStruct((1, 128), F32),
                   jax.ShapeDtypeStruct((1, D_MODEL), F32)],
        compiler_params=_params(("arbitrary",)),
    )(x, target, gf)


def _ffn_bwd_call(dx2, p, x1, g2, w_up, w_down, layer, ts):
    s = dx2.shape[0]

    def body(dx2_ref, p_ref, x1_ref, g2_ref, wu_ref, wd_ref, dx1_ref, dp_ref, dg_ref):
        @pl.when(pl.program_id(0) == 0)
        def _():
            dg_ref[...] = jnp.zeros_like(dg_ref)

        dx2v = dx2_ref[...]
        dyb = dx2v.astype(BF)
        dh2 = jnp.zeros((ts, D_MODEL), F32)
        for q in range(N_QUARTERS):
            cols = slice(q * Q_FF, (q + 1) * Q_FF)
            df = _dot_nt(dyb, wd_ref[cols, :])
            dp = (df * (2.0 * jnp.maximum(p_ref[:, cols].astype(F32), 0.0))).astype(BF)
            dp_ref[:, cols] = dp
            dh2 = dh2 + _dot_nt(dp, wu_ref[q])
        dx, dg = _rms_bwd(dh2, x1_ref[...], g2_ref[...])
        dx1_ref[...] = dx2v + dx
        dg_ref[...] += _row_sum(dg)

    return pl.pallas_call(
        body, name="ffn_bwd", grid=(s // ts,),
        in_specs=[_tile_spec(ts, D_MODEL), _tile_spec(ts, D_FF), _tile_spec(ts, D_MODEL),
                  _full_spec((1, D_MODEL)),
                  pl.BlockSpec((None, N_QUARTERS, D_MODEL, Q_FF), lambda i: (layer, 0, 0, 0)),
                  pl.BlockSpec((None, D_FF, D_MODEL), lambda i: (layer, 0, 0))],
        out_specs=[_tile_spec(ts, D_MODEL), _tile_spec(ts, D_FF), _full_spec((1, D_MODEL))],
        out_shape=[jax.ShapeDtypeStruct((s, D_MODEL), F32), jax.ShapeDtypeStruct((s, D_FF), BF),
                   jax.ShapeDtypeStruct((1, D_MODEL), F32)],
        compiler_params=_params(("arbitrary",)),
    )(dx2, p, x1, g2, w_up, w_down)


def _merge_bwd_call(dx1, proj, ya, yb, w_ba, w_bb, w_out, layer, ts, after=None):
    s = dx1.shape[0]

    def body(dx1_ref, ga0, ga1, gb0, gb1, ya_ref, yb_ref, wa_ref, wb_ref, wo_ref, *rest):
        dya_ref, dyb_ref, dgate_ref, dyap_ref, dybp_ref = rest[-5:]
        dm = _dot_nt(dx1_ref[...].astype(BF), wo_ref[...])
        sa = jax.nn.sigmoid(jnp.concatenate([ga0[...], ga1[...]], axis=1).astype(F32))
        sb = jax.nn.sigmoid(jnp.concatenate([gb0[...], gb1[...]], axis=1).astype(F32))
        dya = (dm * sa).astype(BF)
        dyb = (dm * sb).astype(BF)
        dya_ref[...] = dya
        dyb_ref[...] = dyb
        dgate_ref[:, :D_MODEL] = (dm * ya_ref[...].astype(F32) * sa * (1.0 - sa)).astype(BF)
        dgate_ref[:, D_MODEL:] = (dm * yb_ref[...].astype(F32) * sb * (1.0 - sb)).astype(BF)
        dyap_ref[...] = _dot_nt(dya, wa_ref[...]).astype(BF)
        dybp_ref[...] = _dot_nt(dyb, wb_ref[...]).astype(BF)

    act = jax.ShapeDtypeStruct((s, D_MODEL), BF)
    return pl.pallas_call(
        body, name="merge_bwd", grid=(s // ts,),
        in_specs=[_tile_spec(ts, D_MODEL)] + _gate_specs(ts) + [
            _tile_spec(ts, D_MODEL), _tile_spec(ts, D_MODEL),
            _layer_spec(w_ba, layer), _layer_spec(w_bb, layer), _layer_spec(w_out, layer)]
        + ([] if after is None else [pl.BlockSpec(memory_space=pl.ANY)]),
        out_specs=[_tile_spec(ts, D_MODEL), _tile_spec(ts, D_MODEL), _tile_spec(ts, 2 * D_MODEL),
                   _tile_spec(ts, D_RNN), _tile_spec(ts, D_SGU)],
        out_shape=[act, act, jax.ShapeDtypeStruct((s, 2 * D_MODEL), BF),
                   jax.ShapeDtypeStruct((s, D_RNN), BF), jax.ShapeDtypeStruct((s, D_SGU), BF)],
        compiler_params=_params(("parallel",)),
    )(dx1, proj, proj, proj, proj, ya, yb, w_ba, w_bb, w_out, *([] if after is None else [after]))


def _sgu_bwd_call(dyb_pre, proj, wm, bsb, mask, lg, lb, ts):
    s = proj.shape[0]

    def body(dy_ref, uv_ref, wm_ref, bsb_ref, mask_ref, lg_ref, lb_ref,
             duv_ref, dws_ref, dbs_ref, dlg_ref, dlb_ref, dm_sc):
        step = pl.program_id(0)

        @pl.when(step == 0)
        def _():
            dws_ref[...] = jnp.zeros_like(dws_ref)
            dlg_ref[...] = jnp.zeros_like(dlg_ref)
            dlb_ref[...] = jnp.zeros_like(dlb_ref)
            dm_sc[...] = jnp.zeros_like(dm_sc)

        gu, dgu_du = _gelu_and_grad(uv_ref[:, :D_SGU].astype(F32))
        gv, dgv_dv = _gelu_and_grad(uv_ref[:, D_SGU:2 * D_SGU].astype(F32))
        nh, rstd = _layernorm_fwd(gv)
        lgv = lg_ref[...]
        vn = (nh * lgv + lb_ref[...]).astype(BF)
        dy = dy_ref[...].astype(F32)
        du = dy * _sgu_mix(vn, wm_ref, bsb_ref, ts) * dgu_du
        dmix = dy * gu
        dmix_bf = dmix.astype(BF)
        dm_acc = dm_sc[...]
        rows = []
        for blk in range(ts // SGU_BLOCK):
            r0 = blk * SGU_BLOCK
            dm_acc = dm_acc + dmix[r0:r0 + SGU_BLOCK, :]
            cols = []
            for g in range(SGU_GROUPS):
                c0 = g * SGU_BLOCK
                dmg = dmix_bf[r0:r0 + SGU_BLOCK, c0:c0 + SGU_BLOCK]
                cols.append(_dot_tn(wm_ref[g], dmg))
                dws_ref[g] += mask_ref[...] * _dot_nt(dmg, vn[r0:r0 + SGU_BLOCK, c0:c0 + SGU_BLOCK])
            rows.append(jnp.concatenate(cols, axis=1))
        dm_sc[...] = dm_acc
        dvn = jnp.concatenate(rows, axis=0)
        dlg_ref[...] += _row_sum(dvn * nh)
        dlb_ref[...] += _row_sum(dvn)
        dnh = dvn * lgv
        dgv = rstd * (dnh - jnp.mean(dnh, axis=-1, keepdims=True)
                      - nh * jnp.mean(dnh * nh, axis=-1, keepdims=True))
        duv_ref[:, :D_SGU] = du.astype(BF)
        duv_ref[:, D_SGU:] = (dgv * dgv_dv).astype(BF)

        @pl.when(step == pl.num_programs(0) - 1)
        def _():
            for g in range(SGU_GROUPS):
                dbs_ref[:, g:g + 1] = jnp.sum(
                    dm_acc[:, g * SGU_BLOCK:(g + 1) * SGU_BLOCK], axis=1, keepdims=True)

    sw = (SGU_GROUPS, SGU_BLOCK, SGU_BLOCK)
    return pl.pallas_call(
        body, name="sgu_bwd", grid=(s // ts,),
        in_specs=[_tile_spec(ts, D_SGU), _tile_spec(ts, 2 * D_RNN, 1), _full_spec(sw), _full_spec(sw),
                  _full_spec((SGU_BLOCK, SGU_BLOCK)), _full_spec((1, D_SGU)), _full_spec((1, D_SGU))],
        out_specs=[_tile_spec(ts, 2 * D_SGU), _full_spec(sw), _full_spec((SGU_BLOCK, SGU_GROUPS)),
                   _full_spec((1, D_SGU)), _full_spec((1, D_SGU))],
        out_shape=[jax.ShapeDtypeStruct((s, 2 * D_SGU), BF), jax.ShapeDtypeStruct(sw, F32),
                   jax.ShapeDtypeStruct((SGU_BLOCK, SGU_GROUPS), F32),
                   jax.ShapeDtypeStruct((1, D_SGU), F32), jax.ShapeDtypeStruct((1, D_SGU), F32)],
        scratch_shapes=[pltpu.VMEM((SGU_BLOCK, D_SGU), F32)],
        compiler_params=_params(("arbitrary",)),
    )(dyb_pre, proj, wm, bsb, mask, lg, lb)


_ROW_DBA, _ROW_DBX, _ROW_DSP, _ROW_DCB, _ROW_DCW = 0, 1, 2, 3, 4
_PREV_ROWS = 16


def _rnn_bwd_call(dya_pre, proj, xr_saved, hr, wa, wx, ba, bx, sp, cw, ts):
    s = proj.shape[0]
    nt = s // ts
    per = ts // _PREV_ROWS

    def tile(i):
        return nt - 1 - i

    def prev(i):
        return jnp.maximum(tile(i) * per - 1, 0)

    def body(dy_ref, xg_ref, xr_ref, hr_ref, hrp_ref, wa_ref, wx_ref, ba_ref, bx_ref, sp_ref,
             cw_ref, dxg_ref, dwa_ref, dwx_ref, vec_ref,
             lam_carry, a_first, dxr_head, al_sc, bl_sc, lam_sc):
        step = pl.program_id(0)

        @pl.when(step == 0)
        def _():
            dwa_ref[...] = jnp.zeros_like(dwa_ref)
            dwx_ref[...] = jnp.zeros_like(dwx_ref)
            vec_ref[...] = jnp.zeros_like(vec_ref)
            lam_carry[...] = jnp.zeros_like(lam_carry)
            a_first[...] = jnp.zeros_like(a_first)
            dxr_head[...] = jnp.zeros_like(dxr_head)

        has_prev = (step < nt - 1).astype(F32)
        x = xg_ref[:, :D_RNN].astype(F32)
        g = xg_ref[:, D_RNN:].astype(F32)
        h_tail = hrp_ref[_PREV_ROWS - SUBLANES:, :].astype(F32) * has_prev
        xr = xr_ref[...].astype(F32)
        r, i, a, nrm, inv_nrm = _lru_gates(xr, wa_ref, wx_ref, ba_ref, bx_ref, sp_ref)
        h = hr_ref[...].astype(F32)
        dy = dy_ref[...].astype(F32)
        gg, dgg = _gelu_and_grad(g)

        coef = _shift_up(a, jnp.broadcast_to(a_first[...], (SUBLANES, D_RNN)), 1)
        lam_carry[...] = _linear_scan(coef, dy * gg, lam_carry[...], al_sc, bl_sc, lam_sc, True)
        a_first[...] = a[0:1, :]
        lam = lam_sc[...]

        da = lam * _shift_down(h, h_tail, 1)
        dnrm = lam * (i * xr)
        di = lam * nrm * xr
        dlog_a = da * a - dnrm * (a * a) * inv_nrm
        spv = sp_ref[...]
        dza = (dlog_a * (-LRU_C * spv)) * (r * (1.0 - r))
        dzx = di * (i * (1.0 - i))
        vec_ref[_ROW_DSP:_ROW_DSP + 1, :] += _row_sum(dlog_a * (-LRU_C * r))
        vec_ref[_ROW_DBA:_ROW_DBA + 1, :] += _row_sum(dza)
        vec_ref[_ROW_DBX:_ROW_DBX + 1, :] += _row_sum(dzx)
        xb = xr.astype(BF)
        dza_bf = dza.astype(BF)
        dzx_bf = dzx.astype(BF)
        for grp in range(N_LRU_GROUPS):
            cols = slice(grp * LRU_GROUP, (grp + 1) * LRU_GROUP)
            dwa_ref[grp] += _dot_tn(xb[:, cols], dza_bf[:, cols])
            dwx_ref[grp] += _dot_tn(xb[:, cols], dzx_bf[:, cols])
        dxr = (lam * nrm * i + _group_dot(dza_bf, wa_ref, _dot_nt) + _group_dot(dzx_bf, wx_ref, _dot_nt))

        vec_ref[_ROW_DCB:_ROW_DCB + 1, :] += _row_sum(dxr)
        head = dxr_head[...]
        dx = cw_ref[CONV_WIDTH - 1:CONV_WIDTH, :] * dxr
        vec_ref[_ROW_DCW + 3:_ROW_DCW + 4, :] += _row_sum(dxr * x)
        for sft in range(1, CONV_WIDTH):
            k = CONV_WIDTH - 1 - sft
            ahead = _shift_up(dxr, head, sft)
            dx = dx + cw_ref[k:k + 1, :] * ahead
            vec_ref[_ROW_DCW + k:_ROW_DCW + k + 1, :] += _row_sum(ahead * x)
        dxr_head[...] = dxr[0:SUBLANES, :]
        dxg_ref[:, :D_RNN] = dx.astype(BF)
        dxg_ref[:, D_RNN:] = (dy * h * dgg).astype(BF)

    gw = (N_LRU_GROUPS, LRU_GROUP, LRU_GROUP)
    rev = lambda width: pl.BlockSpec((ts, width), lambda i: (tile(i), 0))
    return pl.pallas_call(
        body, name="rnn_bwd", grid=(nt,),
        in_specs=[rev(D_RNN), rev(2 * D_RNN), rev(D_RNN), rev(D_RNN),
                  pl.BlockSpec((_PREV_ROWS, D_RNN), lambda i: (prev(i), 0)),
                  _full_spec(gw), _full_spec(gw),
                  _full_spec((1, D_RNN)), _full_spec((1, D_RNN)), _full_spec((1, D_RNN)),
                  _full_spec((CONV_WIDTH, D_RNN))],
        out_specs=[rev(2 * D_RNN), _full_spec(gw), _full_spec(gw), _full_spec((SUBLANES, D_RNN))],
        out_shape=[jax.ShapeDtypeStruct((s, 2 * D_RNN), BF), jax.ShapeDtypeStruct(gw, F32),
                   jax.ShapeDtypeStruct(gw, F32), jax.ShapeDtypeStruct((SUBLANES, D_RNN), F32)],
        scratch_shapes=[pltpu.VMEM((1, D_RNN), F32), pltpu.VMEM((1, D_RNN), F32),
                        pltpu.VMEM((SUBLANES, D_RNN), F32),
                        pltpu.VMEM((ts, D_RNN), F32), pltpu.VMEM((ts, D_RNN), F32),
                        pltpu.VMEM((ts, D_RNN), F32)],
        compiler_params=_params(("arbitrary",)),
    )(dya_pre, proj, xr_saved, hr, hr, wa, wx, ba, bx, sp, cw)


def _inproj_bwd_call(dxg, duv, dgate, dx1, x, g1, w_in, layer, ts):
    s = x.shape[0]

    def body(dxg_ref, duv_ref, dgt_ref, dx1_ref, x_ref, g_ref, w_ref, dx_ref, dproj_ref, dg_ref):
        @pl.when(pl.program_id(0) == 0)
        def _():
            dg_ref[...] = jnp.zeros_like(dg_ref)

        dproj = jnp.concatenate([dxg_ref[...], duv_ref[...], dgt_ref[...]], axis=1)
        dproj_ref[...] = dproj
        dh = jnp.zeros((ts, D_MODEL), F32)
        for q in range(N_QUARTERS):
            dh = dh + _dot_nt(dproj[:, q * Q_IN:(q + 1) * Q_IN], w_ref[q])
        dx, dg = _rms_bwd(dh, x_ref[...], g_ref[...])
        dx_ref[...] = dx1_ref[...] + dx
        dg_ref[...] += _row_sum(dg)

    return pl.pallas_call(
        body, name="inproj_bwd", grid=(s // ts,),
        in_specs=[_tile_spec(ts, 2 * D_RNN), _tile_spec(ts, 2 * D_SGU), _tile_spec(ts, 2 * D_MODEL),
                  _tile_spec(ts, D_MODEL), _tile_spec(ts, D_MODEL), _full_spec((1, D_MODEL)),
                  pl.BlockSpec((None, N_QUARTERS, D_MODEL, Q_IN), lambda i: (layer, 0, 0, 0))],
        out_specs=[_tile_spec(ts, D_MODEL), _tile_spec(ts, D_IN), _full_spec((1, D_MODEL))],
        out_shape=[jax.ShapeDtypeStruct((s, D_MODEL), F32), jax.ShapeDtypeStruct((s, D_IN), BF),
                   jax.ShapeDtypeStruct((1, D_MODEL), F32)],
        compiler_params=_params(("arbitrary",)),
    )(dxg, duv, dgate, dx1, x, g1, w_in)


def _relu_sq(p):
    return jnp.square(jnp.maximum(p, 0))


def _wgrad_call(a, b, core, tm, tn, tk, col_blocked, name, a_fn=None):
    s, m = a.shape
    n = b.shape[1]
    r, cols = (m, n // N_QUARTERS) if col_blocked else (m // N_QUARTERS, n)
    r2 = r // 2
    per_tile = tm // r
    steps = s // tk

    def body(core_ref, a_ref, b_ref, keep_ref, send_ref, *acc):
        av = a_ref[...]
        if a_fn is not None:
            av = a_fn(av)
        prod = _dot_tn(av.astype(BF), b_ref[...].astype(BF))

        def emit(total):
            for h in range(2):
                @pl.when(core_ref[0] == h)
                def _():
                    for q in range(per_tile):
                        keep_ref[q] = total[q * r + h * r2:q * r + (h + 1) * r2]
                        send_ref[q] = total[q * r + (1 - h) * r2:q * r + (2 - h) * r2].astype(BF)

        if steps == 1:
            emit(prod)
        else:
            acc_ref, = acc
            step = pl.program_id(2)

            @pl.when(step == 0)
            def _():
                acc_ref[...] = prod

            @pl.when(jnp.logical_and(step > 0, step < steps - 1))
            def _():
                acc_ref[...] += prod

            @pl.when(step == steps - 1)
            def _():
                emit(acc_ref[...] + prod)

    if col_blocked:
        per_q = cols // tn
        out_spec = pl.BlockSpec((1, r2, tn), lambda i, j, k, c: (j // per_q, 0, j % per_q))
    else:
        out_spec = pl.BlockSpec((per_tile, r2, tn), lambda i, j, k, c: (i, 0, j))
    return pl.pallas_call(
        body, name=name,
        out_shape=[jax.ShapeDtypeStruct((N_QUARTERS, r2, cols), F32),
                   jax.ShapeDtypeStruct((N_QUARTERS, r2, cols), BF)],
        grid_spec=pltpu.PrefetchScalarGridSpec(
            num_scalar_prefetch=1, grid=(m // tm, n // tn, steps),
            in_specs=[pl.BlockSpec((tk, tm), lambda i, j, k, c: (k, i)),
                      pl.BlockSpec((tk, tn), lambda i, j, k, c: (k, j))],
            out_specs=[out_spec, out_spec],
            scratch_shapes=[] if steps == 1 else [pltpu.VMEM((tm, tn), F32)]),
        compiler_params=_params(("parallel", "parallel", "arbitrary")),
    )(core, a, b)


BIG = ("w_in", "w_up", "w_down", "w_branch_a", "w_branch_b", "w_out")


def _block_diag(w):
    w4 = w.reshape(N_LRU_GROUPS, HEADS_PER_GROUP, RNN_HEAD_DIM, RNN_HEAD_DIM)
    eye = jnp.eye(HEADS_PER_GROUP, dtype=w.dtype)
    return jnp.einsum("gjio,jk->gjiko", w4, eye).reshape(N_LRU_GROUPS, LRU_GROUP, LRU_GROUP)


def _block_diag_extract(d):
    d5 = d.reshape(N_LRU_GROUPS, HEADS_PER_GROUP, RNN_HEAD_DIM, HEADS_PER_GROUP, RNN_HEAD_DIM)
    blocks = [d5[:, j, :, j, :] for j in range(HEADS_PER_GROUP)]
    return jnp.stack(blocks, axis=1).reshape(RNN_HEADS, RNN_HEAD_DIM, RNN_HEAD_DIM)


def _sgu_mask():
    chunk = jnp.arange(SGU_BLOCK) // CHUNK
    return (chunk[:, None] >= chunk[None, :]).astype(F32)


def _layer_small(sm, l, core):
    row = lambda v: v.reshape(1, -1)
    return dict(
        core=core,
        g1=row(sm["norm_mix_g"][l]), g2=row(sm["norm_ffn_g"][l]),
        wa=_block_diag(sm["lru_w_a"][l]).astype(BF), wx=_block_diag(sm["lru_w_x"][l]).astype(BF),
        ba=row(sm["lru_b_a"][l]), bx=row(sm["lru_b_x"][l]),
        sp=row(jax.nn.softplus(-sm["lru_lambda"][l])),
        cw=sm["conv_w"][l], cb=row(sm["conv_b"][l]),
        wm=(sm["sgu_w_s"][l] * _sgu_mask()).astype(BF),
        bsb=jnp.broadcast_to(sm["sgu_b_s"][l][:, :, None], (SGU_GROUPS, SGU_BLOCK, SGU_BLOCK)),
        lg=row(sm["sgu_ln_g"][l]), lb=row(sm["sgu_ln_b"][l]),
    )


def _layer_fwd_mix(x, big, p, ts, h=None):
    if h is None:
        h = _norm_call(x, p["g1"], ts)
    proj = _inproj_call(h, big["w_in"], 0, 2 * ts)
    xr, hr, ya_pre = _rnn_fwd_call(proj, p["wa"], p["wx"], p["ba"], p["bx"], p["sp"], p["cw"], p["cb"], ts)
    yb_pre = _sgu_fwd_call(proj, p["wm"], p["bsb"], p["lg"], p["lb"], ts)
    return dict(p=p, x=x, h=h, proj=proj, xr=xr, hr=hr, ya_pre=ya_pre, yb_pre=yb_pre)


def _layer_fwd_out(sv, big, ts):
    x1, ya, yb, merged, h2 = _merge_call(sv["x"], sv["proj"], sv["ya_pre"], sv["yb_pre"], big["w_branch_a"],
                                         big["w_branch_b"], big["w_out"], sv["p"]["g2"], 0, ts)
    x2, pre = _ffn_call(x1, h2, big["w_up"], big["w_down"], 0, ts)
    sv.update(x1=x1, ya=ya, yb=yb, merged=merged, h2=h2, pre=pre)
    return x2


def _layer_bwd_ffn(dx, sv, big, ts):
    p = sv["p"]
    dx1, dpre, dg2 = _ffn_bwd_call(dx, sv["pre"], sv["x1"], p["g2"], big["w_up"], big["w_down"], 0, ts)
    tk = dx.shape[0]
    gb = dict(
        w_down=_wgrad_call(sv["pre"], dx, p["core"], Q_FF, D_MODEL // 2, tk, False, "wgrad_down", a_fn=_relu_sq),
        w_up=_wgrad_call(sv["h2"], dpre, p["core"], D_MODEL, Q_FF, tk, True, "wgrad_up"))
    return dx1, gb, dict(norm_ffn_g=dg2[0])


def _layer_bwd_merge(dx1, sv, big, ts, after=None):
    tk = dx1.shape[0]
    core = sv["p"]["core"]
    dya, dyb, dgate, dya_pre, dyb_pre = _merge_bwd_call(
        dx1, sv["proj"], sv["ya"], sv["yb"], big["w_branch_a"], big["w_branch_b"], big["w_out"], 0, ts, after)
    gb = dict(
        w_out=_wgrad_call(sv["merged"], dx1, core, D_MODEL, D_MODEL // 2, tk, False, "wgrad_out"),
        w_branch_a=_wgrad_call(sv["ya_pre"], dya, core, D_RNN, D_MODEL // 2, tk, False, "wgrad_branch_a"),
        w_branch_b=_wgrad_call(sv["yb_pre"], dyb, core, D_SGU, D_MODEL // 2, tk, False, "wgrad_branch_b"))
    return (dgate, dya_pre, dyb_pre), gb


def _layer_bwd_branches(dx1, merge_out, sv, big, lam, ts):
    p = sv["p"]
    tk = dx1.shape[0]
    dgate, dya_pre, dyb_pre = merge_out
    gb = {}
    duv, dws, dbs, dlg, dlb = _sgu_bwd_call(dyb_pre, sv["proj"], p["wm"], p["bsb"], _sgu_mask(), p["lg"], p["lb"],
                                            ts)
    dxg, dwa, dwx, vec = _rnn_bwd_call(dya_pre, sv["proj"], sv["xr"], sv["hr"], p["wa"], p["wx"], p["ba"], p["bx"],
                                       p["sp"], p["cw"], ts // 2)
    dx, dproj, dg1 = _inproj_bwd_call(dxg, duv, dgate, dx1, sv["x"], p["g1"], big["w_in"], 0, ts)
    gb["w_in"] = _wgrad_call(sv["h"], dproj, p["core"], D_MODEL, Q_IN, tk // 2, True, "wgrad_in")
    gs = dict(
        norm_mix_g=dg1[0], conv_w=vec[_ROW_DCW:_ROW_DCW + CONV_WIDTH], conv_b=vec[_ROW_DCB],
        lru_w_a=_block_diag_extract(dwa), lru_w_x=_block_diag_extract(dwx),
        lru_b_a=vec[_ROW_DBA].reshape(RNN_HEADS, RNN_HEAD_DIM), lru_b_x=vec[_ROW_DBX].reshape(RNN_HEADS, RNN_HEAD_DIM),
        lru_lambda=-vec[_ROW_DSP] * jax.nn.sigmoid(-lam),
        sgu_ln_g=dlg[0], sgu_ln_b=dlb[0], sgu_w_s=dws, sgu_b_s=dbs.T)
    return dx, gb, gs


def _local_step(x, target, big, sm, ts):
    saved = []
    core = jnp.zeros((1,), jnp.int32)
    for l in range(DEPTH):
        sv = _layer_fwd_mix(x, big[l], _layer_small(sm, l, core), ts)
        x = _layer_fwd_out(sv, big[l], ts)
        saved.append(sv)
    dx, loss, dgf = _loss_call(x, target, sm["final_norm_g"].reshape(1, -1), ts)
    gb, gs = [None] * DEPTH, [None] * DEPTH
    for l in reversed(range(DEPTH)):
        dx1, gb_ffn, gs_ffn = _layer_bwd_ffn(dx, saved[l], big[l], ts)
        merge_out, gb_merge = _layer_bwd_merge(dx1, saved[l], big[l], ts)
        dx, gb_mix, gs_mix = _layer_bwd_branches(dx1, merge_out, saved[l], big[l], sm["lru_lambda"][l], ts)
        gb[l] = {**gb_ffn, **gb_merge, **gb_mix}
        gs[l] = {**gs_ffn, **gs_mix}
    gs = {k: jnp.stack([g[k] for g in gs]) for k in gs[0]}
    gs["final_norm_g"] = dgf[0]
    return loss, dx, gb, gs


EW_VMEM_BYTES = 24 * 1024 * 1024


def _row_block(rows, cols, bytes_per_elem):
    for br in range(min(rows, EW_VMEM_BYTES // (2 * bytes_per_elem * cols)), 0, -1):
        if rows % br == 0 and br % 16 == 0:
            return br
    return rows


def _ew_call(fn, name, operands, outputs, slabs=1, sel=None, into=None, after=None):
    if into is not None and not isinstance(into, (list, tuple)):
        into = [into]
    rows, cols = outputs[0][0].shape[2:]
    br = _row_block(rows, cols, sum(jnp.dtype(a.dtype).itemsize for a, _ in operands + outputs))
    n_in = len(operands)

    def pick(tok, g, s):
        if callable(tok):
            return tok(g, s)
        if tok == "g":
            return g
        if isinstance(tok, tuple):
            return s[tok[1]]
        return tok

    def spec(idx):
        return pl.BlockSpec((None, None, br, cols),
                            lambda g, i, s, idx=idx: (pick(idx[0], g, s), pick(idx[1], g, s), i, 0))

    if sel is None:
        sel = jnp.zeros((1,), jnp.int32)
    in_specs = [spec(idx) for _, idx in operands]
    arrays = [a for a, _ in operands]
    aliases = {}
    for j, buf in enumerate(into or ()):
        in_specs.append(pl.BlockSpec(memory_space=pl.ANY))
        arrays.append(buf)
        aliases[1 + n_in + j] = j
    if after is not None:
        in_specs.append(pl.BlockSpec(memory_space=pl.ANY))
        arrays.append(after)

    def body(sel_ref, *refs):
        outs = fn(*[r[...] for r in refs[:n_in]])
        for o_ref, o in zip(refs[len(arrays):], outs):
            o_ref[...] = o.astype(o_ref.dtype)

    return pl.pallas_call(
        body, name=name, out_shape=[s for s, _ in outputs],
        grid_spec=pltpu.PrefetchScalarGridSpec(
            num_scalar_prefetch=1, grid=(slabs, rows // br),
            in_specs=in_specs,
            out_specs=[spec(idx) for _, idx in outputs]),
        input_output_aliases=aliases,
        compiler_params=_params(("parallel", "parallel")),
    )(sel, *arrays)


def _as4(a):
    return a.reshape((1,) * (4 - a.ndim) + a.shape)


def _adamw(w, g, m, v):
    m = ADAM_B1 * m + (1.0 - ADAM_B1) * g
    v = ADAM_B2 * v + (1.0 - ADAM_B2) * jnp.square(g)
    m_hat = m / (1.0 - ADAM_B1 ** ADAM_STEP)
    v_hat = v / (1.0 - ADAM_B2 ** ADAM_STEP)
    delta = -ADAM_LR * (m_hat / (jnp.sqrt(v_hat) + ADAM_EPS) + ADAM_WD * w)
    return delta, m, v


def _small_adamw_call(ws, gs, ms, vs):
    n = len(ws)

    def body(*refs):
        for k in range(n):
            w, g, m, v = (refs[j * n + k][...] for j in range(4))
            outs = _adamw(w, g, m, v)
            for j in range(3):
                refs[(4 + j) * n + k][...] = outs[j]

    shapes = [jax.ShapeDtypeStruct(w.shape, F32) for w in ws]
    outs = pl.pallas_call(
        body, name="adamw_small", out_shape=shapes * 3,
        in_specs=[pl.BlockSpec(memory_space=pltpu.VMEM)] * (4 * n),
        out_specs=[pl.BlockSpec(memory_space=pltpu.VMEM)] * (3 * n),
        compiler_params=_params(),
    )(*ws, *gs, *ms, *vs)
    return outs[:n], outs[n:2 * n], outs[2 * n:]


ANY = pl.BlockSpec(memory_space=pl.ANY)


def _place():
    x, y, c = lax.axis_index("x"), lax.axis_index("y"), lax.axis_index("c")
    chips = [(1 - x, y), (x, 1 - y), (1 - x, 1 - y)]
    return x, y, c, chips


def _remote(src, dst, send_sem, recv_sem, to):
    return pltpu.make_async_remote_copy(src_ref=src, dst_ref=dst, send_sem=send_sem, recv_sem=recv_sem,
                                        device_id=to, device_id_type=MESH)


def _gather_call(bufs):
    n = len(bufs)

    def body(*refs):
        out = refs[n:2 * n]
        send_sems, recv_sems = refs[2 * n:]
        x, y, c, chips = _place()
        me_q = 2 * x + y
        sibling = (x, y, 1 - c)
        first = []
        for w in range(n):
            for j, chip in enumerate(chips):
                mine = out[w].at[c, me_q]
                first.append(_remote(mine, mine, send_sems.at[w * 3 + j], recv_sems.at[w * 3 + j], (*chip, c)))
        for cp in first:
            cp.start()
        passed = []
        for w in range(n):
            for j, (qx, qy) in enumerate(chips):
                landed = out[w].at[c, 2 * qx + qy]
                k = w * 3 + j
                _remote(landed, landed, send_sems.at[k], recv_sems.at[k], (qx, qy, c)).wait_recv()
                cp = _remote(landed, landed, send_sems.at[3 * n + k], recv_sems.at[3 * n + k], sibling)
                cp.start()
                passed.append(cp)
        for w in range(n):
            for j, (qx, qy) in enumerate(chips):
                landed = out[w].at[1 - c, 2 * qx + qy]
                k = 3 * n + w * 3 + j
                _remote(landed, landed, send_sems.at[k], recv_sems.at[k], sibling).wait_recv()
        for cp in first + passed:
            cp.wait_send()

    return pl.pallas_call(
        body, name="gather_weights",
        out_shape=[jax.ShapeDtypeStruct(a.shape, a.dtype) for a in bufs],
        in_specs=[ANY] * n, out_specs=[ANY] * n,
        input_output_aliases={w: w for w in range(n)},
        scratch_shapes=[pltpu.SemaphoreType.DMA((6 * n,)), pltpu.SemaphoreType.DMA((6 * n,))],
        compiler_params=_params(vmem=False, has_side_effects=True),
    )(*bufs)


def _sibling_send_call(items):
    n = len(items)

    def body(*refs):
        src, out = refs[:n], refs[n:2 * n]
        send_sems, recv_sems = refs[2 * n:]
        x, y, c, _ = _place()
        copies = [_remote(src[w], out[w], send_sems.at[w], recv_sems.at[w], (x, y, 1 - c)) for w in range(n)]
        for cp in copies:
            cp.start()
        for cp in copies:
            cp.wait()

    return pl.pallas_call(
        body, name="grads_to_sibling",
        out_shape=[jax.ShapeDtypeStruct(a.shape, a.dtype) for a in items],
        in_specs=[ANY] * n, out_specs=[ANY] * n,
        scratch_shapes=[pltpu.SemaphoreType.DMA((n,)), pltpu.SemaphoreType.DMA((n,))],
        compiler_params=_params(vmem=False, has_side_effects=True),
    )(*items)


def _sibling_inplace_call(name, bufs, slabs, n_pairs):
    n = len(bufs)

    def body(*refs):
        out = refs[n:2 * n]
        send_sems, recv_sems = refs[2 * n:]
        x, y, c, _ = _place()
        sibling = (x, y, 1 - c)
        pairs = [pair for w, ref in enumerate(out) for pair in slabs(ref, c, w)]
        sends = [_remote(s, s, send_sems.at[k], recv_sems.at[k], sibling) for k, (s, _) in enumerate(pairs)]
        for cp in sends:
            cp.start()
        for k, (_, r) in enumerate(pairs):
            _remote(r, r, send_sems.at[k], recv_sems.at[k], sibling).wait_recv()
        for cp in sends:
            cp.wait_send()

    return pl.pallas_call(
        body, name=name,
        out_shape=[jax.ShapeDtypeStruct(a.shape, a.dtype) for a in bufs],
        in_specs=[ANY] * n, out_specs=[ANY] * n,
        input_output_aliases={w: w for w in range(n)},
        scratch_shapes=[pltpu.SemaphoreType.DMA((n_pairs,)), pltpu.SemaphoreType.DMA((n_pairs,))],
        compiler_params=_params(vmem=False, has_side_effects=True),
    )(*bufs)


HBM_SPEC = pl.BlockSpec(memory_space=pltpu.HBM)
SEM_SPEC = pl.BlockSpec(memory_space=pltpu.SEMAPHORE)
DATAFLOW_EFFECT = pltpu.SideEffectType.DATAFLOW_SIDE_EFFECTING


def _exchange_start(name, bufs, copies, n_copies, after):
    n = len(bufs)

    def body(*refs):
        ins, send_sems, recv_sems, token = refs[:n], refs[n + 1], refs[n + 2], refs[-1]
        for k, (src, dst, to) in enumerate(copies(ins)):
            _remote(src, dst, send_sems.at[k], recv_sems.at[k], to).start()
        token[...] = jnp.zeros_like(token)

    outs = pl.pallas_call(
        body, name=name,
        out_shape=(pltpu.SemaphoreType.DMA((n_copies,)), pltpu.SemaphoreType.DMA((n_copies,)),
                   *[pltpu.HBM(b.shape, b.dtype) for b in bufs], jax.ShapeDtypeStruct((SUBLANES, 128), F32)),
        in_specs=[HBM_SPEC] * n + [ANY],
        out_specs=(SEM_SPEC, SEM_SPEC, *[HBM_SPEC] * n, pl.BlockSpec(memory_space=pltpu.VMEM)),
        input_output_aliases={w: w + 2 for w in range(n)},
        compiler_params=pltpu.CompilerParams(has_side_effects=DATAFLOW_EFFECT),
    )(*[pltpu.with_memory_space_constraint(b, pltpu.HBM) for b in bufs], after)
    return outs[0], outs[1], list(outs[2:2 + n]), outs[-1]


def _exchange_wait(name, send_sems, recv_sems, bufs, copies, after):
    n = len(bufs)

    def body(*refs):
        ins, send_sems, recv_sems = refs[:n], refs[n], refs[n + 1]
        for k, (src, dst, to) in enumerate(copies(ins)):
            cp = _remote(src, dst, send_sems.at[k], recv_sems.at[k], to)
            cp.wait_send()
            cp.wait_recv()

    return pl.pallas_call(
        body, name=name,
        out_shape=[pltpu.HBM(b.shape, b.dtype) for b in bufs],
        in_specs=[HBM_SPEC] * n + [SEM_SPEC, SEM_SPEC, ANY],
        out_specs=[HBM_SPEC] * n,
        input_output_aliases={w: w for w in range(n)},
        compiler_params=pltpu.CompilerParams(has_side_effects=DATAFLOW_EFFECT),
    )(*bufs, send_sems, recv_sems, after)


def _gather_copies(refs):
    x, y, c, chips = _place()
    mine = 2 * (2 * x + y) + c
    return [(ref.at[mine], ref.at[mine], (qx, qy, c)) for ref in refs for qx, qy in chips]


def _gather_forward_slabs(ref, c, w):
    x, y, _, chips = _place()
    return [(ref.at[2 * (2 * qx + qy) + c], ref.at[2 * (2 * qx + qy) + 1 - c]) for qx, qy in chips]


def _device_peers():
    x, y, c, _ = _place()
    return 4 * x + 2 * y + c, [(k, (x ^ ((k >> 2) & 1), y ^ ((k >> 1) & 1), c ^ (k & 1))) for k in range(1, 8)]


def _small_scatter_copies(refs):
    me, peers = _device_peers()
    return [(refs[0].at[me ^ k], refs[1].at[me], to) for k, to in peers]


def _small_spread_copies(refs):
    me, peers = _device_peers()
    return [(refs[0].at[me], refs[0].at[me], to) for _, to in peers]


def _sibling_copies(refs):
    n = len(refs) // 2
    x, y, c, _ = _place()
    return [(refs[w], refs[n + w], (x, y, 1 - c)) for w in range(n)]


def _owner_copies(refs):
    n = len(refs) // 2
    x, y, c, chips = _place()
    return [(refs[w].at[2 * qx + qy], refs[n + w].at[j], (qx, qy, c))
            for w in range(n) for j, (qx, qy) in enumerate(chips)]


N_DEVICES = 8
SMALL_ROWS = 616


SMALL = ("norm_mix_g", "conv_w", "conv_b", "lru_w_a", "lru_b_a", "lru_w_x", "lru_b_x", "lru_lambda",
         "sgu_ln_g", "sgu_ln_b", "sgu_w_s", "sgu_b_s", "norm_ffn_g", "final_norm_g")
WEIGHTS = ("norm_mix_g", "w_in", "conv_w", "conv_b", "lru_w_a", "lru_b_a", "lru_w_x", "lru_b_x", "lru_lambda",
           "sgu_ln_g", "sgu_ln_b", "sgu_w_s", "sgu_b_s", "w_branch_a", "w_branch_b", "w_out", "norm_ffn_g",
           "w_up", "w_down", "final_norm_g")
PACK_ALIGN = SUBLANES * 128


def _pack_small(gs):
    parts = []
    for k in SMALL:
        flat = gs[k].reshape(-1)
        parts.append(jnp.pad(flat, (0, -flat.size % PACK_ALIGN)))
    flat = jnp.concatenate(parts)
    flat = jnp.pad(flat, (0, N_DEVICES * SMALL_ROWS * 128 - flat.size))
    return flat.reshape(N_DEVICES, SMALL_ROWS, 128)


def _unpack_small(buf, like):
    flat = buf.reshape(-1)
    out, off = {}, 0
    for k in SMALL:
        size = like[k].size
        out[k] = flat[off:off + size].reshape(like[k].shape)
        off += size + (-size % PACK_ALIGN)
    return out


def _as_rows(a):
    return a.reshape(-1, a.shape[-1])


def kernel(x, norm_mix_g, w_in, conv_w, conv_b, lru_w_a, lru_b_a, lru_w_x, lru_b_x, lru_lambda, sgu_ln_g, sgu_ln_b, sgu_w_s, sgu_b_s, w_branch_a, w_branch_b, w_out, norm_ffn_g, w_up, w_down, final_norm_g, loss_target, m_norm_mix_g, m_w_in, m_conv_w, m_conv_b, m_lru_w_a, m_lru_b_a, m_lru_w_x, m_lru_b_x, m_lru_lambda, m_sgu_ln_g, m_sgu_ln_b, m_sgu_w_s, m_sgu_b_s, m_w_branch_a, m_w_branch_b, m_w_out, m_norm_ffn_g, m_w_up, m_w_down, m_final_norm_g, v_norm_mix_g, v_w_in, v_conv_w, v_conv_b, v_lru_w_a, v_lru_b_a, v_lru_w_x, v_lru_b_x, v_lru_lambda, v_sgu_ln_g, v_sgu_ln_b, v_sgu_w_s, v_sgu_b_s, v_w_branch_a, v_w_branch_b, v_w_out, v_norm_ffn_g, v_w_up, v_w_down, v_final_norm_g):
    w = dict(norm_mix_g=norm_mix_g, w_in=w_in, conv_w=conv_w, conv_b=conv_b, lru_w_a=lru_w_a, lru_b_a=lru_b_a,
             lru_w_x=lru_w_x, lru_b_x=lru_b_x, lru_lambda=lru_lambda, sgu_ln_g=sgu_ln_g, sgu_ln_b=sgu_ln_b,
             sgu_w_s=sgu_w_s, sgu_b_s=sgu_b_s, w_branch_a=w_branch_a, w_branch_b=w_branch_b, w_out=w_out,
             norm_ffn_g=norm_ffn_g, w_up=w_up, w_down=w_down, final_norm_g=final_norm_g)
    m = dict(norm_mix_g=m_norm_mix_g, w_in=m_w_in, conv_w=m_conv_w, conv_b=m_conv_b, lru_w_a=m_lru_w_a,
             lru_b_a=m_lru_b_a, lru_w_x=m_lru_w_x, lru_b_x=m_lru_b_x, lru_lambda=m_lru_lambda,
             sgu_ln_g=m_sgu_ln_g, sgu_ln_b=m_sgu_ln_b, sgu_w_s=m_sgu_w_s, sgu_b_s=m_sgu_b_s,
             w_branch_a=m_w_branch_a, w_branch_b=m_w_branch_b, w_out=m_w_out, norm_ffn_g=m_norm_ffn_g,
             w_up=m_w_up, w_down=m_w_down, final_norm_g=m_final_norm_g)
    v = dict(norm_mix_g=v_norm_mix_g, w_in=v_w_in, conv_w=v_conv_w, conv_b=v_conv_b, lru_w_a=v_lru_w_a,
             lru_b_a=v_lru_b_a, lru_w_x=v_lru_w_x, lru_b_x=v_lru_b_x, lru_lambda=v_lru_lambda,
             sgu_ln_g=v_sgu_ln_g, sgu_ln_b=v_sgu_ln_b, sgu_w_s=v_sgu_w_s, sgu_b_s=v_sgu_b_s,
             w_branch_a=v_w_branch_a, w_branch_b=v_w_branch_b, w_out=v_w_out, norm_ffn_g=v_norm_ffn_g,
             w_up=v_w_up, w_down=v_w_down, final_norm_g=v_final_norm_g)
    core = lax.axis_index("c")
    chip = 2 * lax.axis_index("x") + lax.axis_index("y")
    sel = jnp.stack([core, 1 - core, chip, 2 * chip + core]).astype(jnp.int32)
    this_core, other_core, this_chip = ("sel", 0), ("sel", 1), ("sel", 2)
    sds = jax.ShapeDtypeStruct

    ts = TOKEN_TILE

    def after_all(arrays):
        return jnp.stack([a[(0,) * a.ndim].astype(F32) for a in arrays])

    halves ={k: (w[k].shape[1] // 2, w[k].shape[2]) for k in BIG}

    def half_view(k, a):
        return a.reshape((2 * N_QUARTERS,) + halves[k])

    def full_view(k, a):
        r2, cols = halves[k]
        if k in ("w_in", "w_up"):
            return a.reshape(1, N_QUARTERS, 2 * r2, cols)
        return a.reshape(1, 2 * N_QUARTERS * r2, cols)

    layer_bufs = [{}, {}]

    def cast_weights(k, after):
        _, r, cols = w[k].shape
        w4 = w[k].reshape(DEPTH, 1, r, cols)
        outs = _ew_call(lambda a, b: (a, b), "cast_weights", [(w4, (0, 0)), (w4, (1, 0))],
                        [(sds((1, N_QUARTERS, r, cols), BF), (0, this_chip))] * DEPTH, 1, sel, after=after)
        for l in range(DEPTH):
            layer_bufs[l][k] = half_view(k, outs[l])

    conv_buf = lax.dynamic_update_slice_in_dim(
        jnp.zeros((DEPTH, N_QUARTERS) + conv_w.shape[1:], F32), conv_w[:, None], chip, axis=1)
    sm = {k: w[k] for k in SMALL}
    sm["conv_w"] = _gather_call([conv_buf])[0].transpose(0, 2, 1, 3).reshape(DEPTH, CONV_WIDTH, D_RNN)

    def gather_start(tag, l, keys, after):
        bufs = [layer_bufs[l][k] for k in keys]
        return _exchange_start(f"gather_start_{tag}", bufs, _gather_copies, 3 * len(keys), after)

    def gather_finish(tag, keys, started, after):
        send_sems, recv_sems, thru, _ = started
        landed = _exchange_wait(f"gather_wait_{tag}", send_sems, recv_sems, thru, _gather_copies, after)
        landed = _sibling_inplace_call("gather_forward", landed, _gather_forward_slabs, 3 * len(keys))
        return {k: full_view(k, a) for k, a in zip(keys, landed)}

    first, rest = ("w_in",), tuple(k for k in BIG if k != "w_in")
    cast_weights("w_in", None)
    started_a = gather_start("0a", 0, first, sm["conv_w"])
    for k in rest:
        cast_weights(k, started_a[3])
    started_b = gather_start("0b", 0, rest, started_a[3])
    started_1 = gather_start("1", 1, BIG, started_b[3])
    p0, p1 = _layer_small(sm, 0, sel[0:1]), _layer_small(sm, 1, sel[0:1])
    h0 = _norm_call(x[0], p0["g1"], ts)
    ready = after_all([started_1[3], h0] + [p[k] for p in (p0, p1) for k in ("wa", "wx", "wm")])
    big0 = gather_finish("0a", first, started_a, ready)
    sv0 = _layer_fwd_mix(x[0], big0, p0, ts, h0)
    big0.update(gather_finish("0b", rest, started_b, sv0["yb_pre"]))
    x_mid = _layer_fwd_out(sv0, big0, ts)
    big1 = gather_finish("1", BIG, started_1, x_mid)
    sv1 = _layer_fwd_mix(x_mid, big1, p1, ts)
    x_out = _layer_fwd_out(sv1, big1, ts)
    dx, loss, dgf = _loss_call(x_out, loss_target[0], final_norm_g.reshape(1, -1), ts)

    def pair_start(tag, gb, after):
        sends = [gb[k][1] for k in gb]
        zones = [lax.empty(a.shape, BF) for a in sends]
        return _exchange_start(f"pair_start_{tag}", sends + zones, _sibling_copies, len(sends), after)

    def reduce_start(tag, gb, after, pair=None):
        keys = tuple(gb)
        if pair is None:
            from_sibling = _sibling_send_call([gb[k][1] for k in keys])
        else:
            done = _exchange_wait(f"pair_wait_{tag}", pair[0], pair[1], pair[2], _sibling_copies, after)
            from_sibling = done[len(keys):]
        sums = [
            _ew_call(lambda a, b: (a + b.astype(F32),), "pair_sum", [(gb[k][0][None], (0, "g")), (r[None], (0, "g"))],
                     [(sds((1,) + r.shape, BF), (0, "g"))], N_QUARTERS)[0][0]
            for k, r in zip(keys, from_sibling)]
        zones = [lax.empty((3,) + a.shape[1:], BF) for a in sums]
        started = _exchange_start(f"reduce_start_{tag}", sums + zones, _owner_copies, 3 * len(keys), after)
        return keys, started

    def reduce_finish(tag, l, keys_started, after, reduced):
        keys, (send_sems, recv_sems, thru, _) = keys_started
        done = _exchange_wait(f"reduce_wait_{tag}", send_sems, recv_sems, thru, _owner_copies, after)
        sums, zones = done[:len(keys)], done[len(keys):]
        for i, k in enumerate(keys):
            r2, cols = halves[k]
            reduced[k] = _ew_call(
                lambda a, b, c, d: (((a.astype(F32) + b.astype(F32)) + c.astype(F32)) + d.astype(F32),),
                "quarter_sum", [(sums[i][None], (0, this_chip))] + [(zones[i][None], (0, j)) for j in range(3)],
                [(sds((DEPTH, 2, r2, cols), F32), (l, this_core))], 1, sel, into=reduced.get(k))[0]

    def behind(params, key, started):
        return dict(params, **{key: params[key] + started[1][3][0, 0]})

    dx1, gb_ffn, gs1 = _layer_bwd_ffn(dx, sv1, big1, ts)
    merge_out, gb_merge = _layer_bwd_merge(dx1, sv1, big1, ts)
    dx_mid, gb_in, gs1_mix = _layer_bwd_branches(dx1, merge_out, sv1, big1, lru_lambda[1], ts)
    gb_1 = {**gb_ffn, **gb_merge, **gb_in}
    pair_1 = pair_start("1", gb_1, dx_mid)
    sv0["p"] = behind(sv0["p"], "g2", (None, pair_1))
    dx1, gb_ffn, gs0 = _layer_bwd_ffn(dx_mid, sv0, big0, ts)
    exchange_1 = reduce_start("1", gb_1, dx1, pair_1)
    exchange_0a = reduce_start("0a", gb_ffn, exchange_1[1][3])
    merge_out, gb_merge = _layer_bwd_merge(dx1, sv0, big0, ts, exchange_0a[1][3])
    exchange_0b = reduce_start("0b", gb_merge, exchange_0a[1][3])
    sv0["p"] = behind(sv0["p"], "lg", exchange_0b)
    grad_x, gb_in, gs0_mix = _layer_bwd_branches(dx1, merge_out, sv0, big0, lru_lambda[0], ts)
    exchange_0c = reduce_start("0c", gb_in, exchange_0b[1][3])
    layer_gs = [{**gs0, **gs0_mix}, {**gs1, **gs1_mix}]
    gs = {k: jnp.stack([g[k] for g in layer_gs]) for k in layer_gs[0]}
    gs["final_norm_g"] = dgf[0]

    me = ("sel", 3)
    piece = (1, N_DEVICES, SMALL_ROWS, 128)
    packed = _pack_small(gs).reshape(piece)
    scatter = _exchange_start("small_scatter_start", [packed[0], lax.empty(piece[1:], F32)], _small_scatter_copies,
                              N_DEVICES - 1, exchange_0c[1][3])
    reduced = {}
    reduce_finish("1", 1, exchange_1, scatter[3], reduced)
    reduce_finish("0a", 0, exchange_0a, reduced["w_in"], reduced)
    reduce_finish("0b", 0, exchange_0b, reduced["w_down"], reduced)

    def swap_slabs(ref, c, i):
        layers = (1,) if BIG[i] == "w_in" else range(DEPTH)
        return [(ref.at[l, c], ref.at[l, 1 - c]) for l in layers]

    swapped = dict(zip(BIG, _sibling_inplace_call("grads_swap_halves", [reduced[k] for k in BIG], swap_slabs,
                                                  DEPTH * len(BIG) - 1)))

    def adamw_layers(k, grad, layer, into, after=None):
        if layer is None:
            views = [_as4(_as_rows(a)) for a in (w[k], grad, m[k], v[k])]
            idx = (0, 0)
        else:
            views = [a.reshape((1,) + w[k].shape) for a in (w[k], grad, m[k], v[k])]
            idx = (0, layer)
        return _ew_call(_adamw, "adamw_big", [(a, idx) for a in views], [(sds(views[0].shape, F32), idx)] * 3,
                        into=into, after=after)

    updated, last_update = {}, None
    for k in BIG:
        updated[k] = adamw_layers(k, swapped[k], 1 if k == "w_in" else None, None, last_update)
        last_update = updated[k][0]
    scattered = _exchange_wait("small_scatter_wait", scatter[0], scatter[1], scatter[2], _small_scatter_copies,
                               last_update)
    summed = _ew_call(
        lambda *parts: (functools.reduce(lambda a, b: a + b, parts),), "small_sum",
        [(scattered[0][None], (0, me))]
        + [(scattered[1][None], (0, lambda g, s, k=k: s[3] ^ k)) for k in range(1, N_DEVICES)],
        [(sds(piece, F32), (0, me))], 1, sel)[0]
    spread = _exchange_start("small_spread_start", [summed[0]], _small_spread_copies, N_DEVICES - 1, summed)
    reduced["w_in"] = swapped["w_in"]
    reduce_finish("0c", 0, exchange_0c, spread[3], reduced)
    last = _sibling_inplace_call("grads_swap_last", [reduced["w_in"]],
                                 lambda ref, c, i: [(ref.at[0, c], ref.at[0, 1 - c])], 1)[0]
    swapped["w_in"] = last
    updated["w_in"] = adamw_layers("w_in", last, 0, updated["w_in"])
    grads_big = {k: swapped[k].reshape(w[k].shape) for k in BIG}
    delta, new_m, new_v = ({k: updated[k][j].reshape(w[k].shape) for k in BIG} for j in range(3))
    gathered_small = _exchange_wait("small_spread_wait", spread[0], spread[1], spread[2], _small_spread_copies,
                                    updated["w_in"][0])[0]

    like = {k: jax.ShapeDtypeStruct(sm[k].shape, F32) for k in SMALL}
    grads_small = _unpack_small(gathered_small, like)
    conv_q = grads_small["conv_w"].reshape(DEPTH, CONV_WIDTH, N_QUARTERS, D_RNN // N_QUARTERS)
    grads_small["conv_w"] = lax.dynamic_index_in_dim(conv_q, chip, axis=2, keepdims=False)
    outs = _small_adamw_call(*[[_as_rows(d[k]) for k in SMALL] for d in (w, grads_small, m, v)])
    for d, o in zip((delta, new_m, new_v), outs):
        for k, a in zip(SMALL, o):
            d[k] = a.reshape(w[k].shape)

    grads = {**grads_big, **grads_small}
    total = lax.psum(loss[0, 0], ("x", "y", "c"))
    return (total, grad_x[None], *[grads[k] for k in WEIGHTS], *[delta[k] for k in WEIGHTS],
            *[new_m[k] for k in WEIGHTS], *[new_v[k] for k in WEIGHTS])
```

```python
import functools
import math

import jax
import jax.numpy as jnp
from jax import lax
from jax.experimental import pallas as pl
from jax.experimental.pallas import tpu as pltpu

F32 = jnp.float32
BF = jnp.bfloat16

DEPTH = 2
D_MODEL = 1024
D_RNN = 1280
D_SGU = 1024
D_FF = 4096
D_IN = 2 * D_RNN + 2 * D_SGU + 2 * D_MODEL
N_QUARTERS = 4
Q_IN = D_IN // N_QUARTERS
Q_FF = D_FF // N_QUARTERS
RNN_HEADS = 20
RNN_HEAD_DIM = 64
LRU_GROUP = 256
N_LRU_GROUPS = D_RNN // LRU_GROUP
HEADS_PER_GROUP = LRU_GROUP // RNN_HEAD_DIM
CONV_WIDTH = 4
LRU_C = 8.0
SGU_GROUPS = 8
SGU_BLOCK = 128
CHUNK = 64
EPS = 1e-6

ADAM_LR = 0.001
ADAM_B1 = 0.9
ADAM_B2 = 0.999
ADAM_EPS = 1e-08
ADAM_WD = 0.01
ADAM_STEP = 10

SUBLANES = 8
TOKEN_TILE = 512
VMEM_LIMIT_BYTES = 56 * 1024 * 1024

MESH = pl.DeviceIdType.MESH


def _params(semantics=None, vmem=True, **kw):
    return pltpu.CompilerParams(
        dimension_semantics=semantics,
        vmem_limit_bytes=VMEM_LIMIT_BYTES if vmem else None,
        **kw,
    )


def _dot(a, b):
    return jnp.dot(a, b, preferred_element_type=F32)


def _dot_nt(a, b):
    return lax.dot_general(a, b, (((1,), (1,)), ((), ())), preferred_element_type=F32)


def _dot_tn(a, b):
    return lax.dot_general(a, b, (((0,), (0,)), ((), ())), preferred_element_type=F32)


_GELU_C = math.sqrt(2.0 / math.pi)
_GELU_A = 0.044715


def _gelu(x):
    return 0.5 * x * (1.0 + jnp.tanh(_GELU_C * (x + _GELU_A * x * x * x)))


def _gelu_and_grad(x):
    x2 = x * x
    t = jnp.tanh(_GELU_C * (x + _GELU_A * x2 * x))
    du = _GELU_C * (1.0 + 3.0 * _GELU_A * x2)
    return 0.5 * x * (1.0 + t), 0.5 * (1.0 + t) + 0.5 * x * (1.0 - t * t) * du


def _rms_stats(x):
    return lax.rsqrt(jnp.mean(x * x, axis=-1, keepdims=True) + EPS)


def _rms_bwd(dy, x, g):
    rs = _rms_stats(x)
    n = x * rs
    dn = dy * g
    dx = rs * (dn - n * jnp.mean(dn * n, axis=-1, keepdims=True))
    return dx, dy * n


def _row_sum(x):
    return jnp.sum(x, axis=0, keepdims=True)


def _tile_spec(ts, width, col=0):
    return pl.BlockSpec((ts, width), lambda i, col=col: (i, col))


def _full_spec(shape):
    zeros = (0,) * len(shape)
    return pl.BlockSpec(shape, lambda *_: zeros)


def _layer_spec(w, layer):
    zeros = (0,) * (w.ndim - 1)
    return pl.BlockSpec((None,) + tuple(w.shape[1:]), lambda *_: (layer,) + zeros)


def _norm_call(x, g, ts):
    s = x.shape[0]

    def body(x_ref, g_ref, h_ref):
        xv = x_ref[...]
        h_ref[...] = (xv * _rms_stats(xv) * g_ref[...]).astype(BF)

    return pl.pallas_call(
        body, name="norm_fwd", grid=(s // ts,),
        in_specs=[_tile_spec(ts, D_MODEL), _full_spec((1, D_MODEL))],
        out_specs=_tile_spec(ts, D_MODEL),
        out_shape=jax.ShapeDtypeStruct((s, D_MODEL), BF),
        compiler_params=_params(("parallel",)),
    )(x, g)


def _inproj_call(h, w_in, layer, ts):
    s = h.shape[0]

    def body(h_ref, w_ref, o_ref):
        o_ref[...] = _dot(h_ref[...], w_ref[...]).astype(BF)

    return pl.pallas_call(
        body, name="inproj_fwd", grid=(N_QUARTERS, s // ts),
        in_specs=[
            pl.BlockSpec((ts, D_MODEL), lambda q, i: (i, 0)),
            pl.BlockSpec((None, None, D_MODEL, Q_IN), lambda q, i: (layer, q, 0, 0)),
        ],
        out_specs=pl.BlockSpec((ts, Q_IN), lambda q, i: (i, q)),
        out_shape=jax.ShapeDtypeStruct((s, D_IN), BF),
        compiler_params=_params(("parallel", "parallel")),
    )(h, w_in)


def _shift_down(x, tail, s):
    xr = pltpu.roll(x, s, 0)
    tr = pltpu.roll(tail, s, 0)
    row = lax.broadcasted_iota(jnp.int32, tail.shape, 0)
    top = jnp.where(row < s, tr, xr[0:SUBLANES])
    return jnp.concatenate([top, xr[SUBLANES:]], axis=0)


def _shift_up(x, head, s):
    t = x.shape[0]
    xr = pltpu.roll(x, t - s, 0)
    hr = pltpu.roll(head, SUBLANES - s, 0)
    row = lax.broadcasted_iota(jnp.int32, head.shape, 0)
    bottom = jnp.where(row >= SUBLANES - s, hr, xr[t - SUBLANES:])
    return jnp.concatenate([xr[: t - SUBLANES], bottom], axis=0)


def _conv_fwd(x, tail, cw_ref, cb_ref):
    out = cb_ref[...] + cw_ref[CONV_WIDTH - 1:CONV_WIDTH, :] * x
    for s in range(1, CONV_WIDTH):
        k = CONV_WIDTH - 1 - s
        out = out + cw_ref[k:k + 1, :] * _shift_down(x, tail, s)
    return out


def _group_dot(x_bf, w_ref, dot):
    cols = [dot(x_bf[:, g * LRU_GROUP:(g + 1) * LRU_GROUP], w_ref[g]) for g in range(N_LRU_GROUPS)]
    return jnp.concatenate(cols, axis=1)


def _lru_gates(xr, wa_ref, wx_ref, ba_ref, bx_ref, sp_ref):
    xb = xr.astype(BF)
    r = jax.nn.sigmoid(_group_dot(xb, wa_ref, _dot) + ba_ref[...])
    i = jax.nn.sigmoid(_group_dot(xb, wx_ref, _dot) + bx_ref[...])
    log_a = (-LRU_C * r) * sp_ref[...]
    a = jnp.exp(log_a)
    nrm2 = -jnp.tanh(log_a) * (a * a + 1.0)
    inv_nrm = lax.rsqrt(jnp.maximum(nrm2, 1e-36))
    return r, i, a, nrm2 * inv_nrm, inv_nrm


def _linear_scan(a, b, carry, al_ref, bl_ref, h_ref, reverse):
    t, c = a.shape
    rowm = lax.broadcasted_iota(jnp.int32, (t, c), 0) & (SUBLANES - 1)
    for d in (1, 2, 4):
        if reverse:
            keep, sh = rowm < SUBLANES - d, t - d
        else:
            keep, sh = rowm >= d, d
        a_sh = jnp.where(keep, pltpu.roll(a, sh, 0), 1.0)
        b_sh = jnp.where(keep, pltpu.roll(b, sh, 0), 0.0)
        b = a * b_sh + b
        a = a * a_sh
    al_ref[...] = a
    bl_ref[...] = b
    groups = t // SUBLANES

    def step(j, state):
        jj = groups - 1 - j if reverse else j
        off = pl.multiple_of(jj * SUBLANES, SUBLANES)
        rows = bl_ref[pl.ds(off, SUBLANES), :] + al_ref[pl.ds(off, SUBLANES), :] * state
        h_ref[pl.ds(off, SUBLANES), :] = rows
        last = rows[0:1, :] if reverse else rows[SUBLANES - 1:SUBLANES, :]
        return jnp.broadcast_to(last, (SUBLANES, c))

    out = lax.fori_loop(0, groups, step, jnp.broadcast_to(carry, (SUBLANES, c)))
    return out[0:1, :]


def _rnn_fwd_call(proj, wa, wx, ba, bx, sp, cw, cb, ts):
    s = proj.shape[0]

    def body(xg_ref, wa_ref, wx_ref, ba_ref, bx_ref, sp_ref, cw_ref, cb_ref, xr_ref, hr_ref, ya_ref,
             tail_sc, carry_sc, al_sc, bl_sc, h_sc):
        @pl.when(pl.program_id(0) == 0)
        def _():
            tail_sc[...] = jnp.zeros_like(tail_sc)
            carry_sc[...] = jnp.zeros_like(carry_sc)

        x = xg_ref[:, :D_RNN].astype(F32)
        g = xg_ref[:, D_RNN:].astype(F32)
        xr = _conv_fwd(x, tail_sc[...], cw_ref, cb_ref)
        tail_sc[...] = x[ts - SUBLANES:, :]
        xr_ref[...] = xr.astype(BF)
        _, i, a, nrm, _ = _lru_gates(xr, wa_ref, wx_ref, ba_ref, bx_ref, sp_ref)
        carry_sc[...] = _linear_scan(a, nrm * (i * xr), carry_sc[...], al_sc, bl_sc, h_sc, False)
        h = h_sc[...]
        hr_ref[...] = h.astype(BF)
        ya_ref[...] = (h * _gelu(g)).astype(BF)

    gw = (N_LRU_GROUPS, LRU_GROUP, LRU_GROUP)
    return pl.pallas_call(
        body, name="rnn_fwd", grid=(s // ts,),
        in_specs=[_tile_spec(ts, 2 * D_RNN), _full_spec(gw), _full_spec(gw),
                  _full_spec((1, D_RNN)), _full_spec((1, D_RNN)), _full_spec((1, D_RNN)),
                  _full_spec((CONV_WIDTH, D_RNN)), _full_spec((1, D_RNN))],
        out_specs=[_tile_spec(ts, D_RNN)] * 3,
        out_shape=[jax.ShapeDtypeStruct((s, D_RNN), BF)] * 3,
        scratch_shapes=[pltpu.VMEM((SUBLANES, D_RNN), F32), pltpu.VMEM((1, D_RNN), F32),
                        pltpu.VMEM((ts, D_RNN), F32), pltpu.VMEM((ts, D_RNN), F32),
                        pltpu.VMEM((ts, D_RNN), F32)],
        compiler_params=_params(("arbitrary",)),
    )(proj, wa, wx, ba, bx, sp, cw, cb)


def _layernorm_fwd(x):
    mu = jnp.mean(x, axis=-1, keepdims=True)
    xc = x - mu
    rstd = lax.rsqrt(jnp.mean(xc * xc, axis=-1, keepdims=True) + EPS)
    return xc * rstd, rstd


def _sgu_mix(vn_bf, wm_ref, bsb_ref, ts):
    rows = []
    for blk in range(ts // SGU_BLOCK):
        r0 = blk * SGU_BLOCK
        cols = [
            _dot(wm_ref[g], vn_bf[r0:r0 + SGU_BLOCK, g * SGU_BLOCK:(g + 1) * SGU_BLOCK]) + bsb_ref[g]
            for g in range(SGU_GROUPS)
        ]
        rows.append(jnp.concatenate(cols, axis=1))
    return jnp.concatenate(rows, axis=0)


def _sgu_fwd_call(proj, wm, bsb, lg, lb, ts):
    s = proj.shape[0]

    def body(uv_ref, wm_ref, bsb_ref, lg_ref, lb_ref, yb_ref):
        gu = _gelu(uv_ref[:, :D_SGU].astype(F32))
        gv = _gelu(uv_ref[:, D_SGU:2 * D_SGU].astype(F32))
        nh, _ = _layernorm_fwd(gv)
        vn = (nh * lg_ref[...] + lb_ref[...]).astype(BF)
        yb_ref[...] = (gu * _sgu_mix(vn, wm_ref, bsb_ref, ts)).astype(BF)

    sw = (SGU_GROUPS, SGU_BLOCK, SGU_BLOCK)
    return pl.pallas_call(
        body, name="sgu_fwd", grid=(s // ts,),
        in_specs=[_tile_spec(ts, 2 * D_RNN, 1), _full_spec(sw), _full_spec(sw),
                  _full_spec((1, D_SGU)), _full_spec((1, D_SGU))],
        out_specs=_tile_spec(ts, D_SGU),
        out_shape=jax.ShapeDtypeStruct((s, D_SGU), BF),
        compiler_params=_params(("parallel",)),
    )(proj, wm, bsb, lg, lb)


_GATE_COL0 = (2 * D_RNN + 2 * D_SGU) // 512


def _gate_specs(ts):
    return [_tile_spec(ts, 512, _GATE_COL0 + j) for j in range(4)]


def _merge_call(x, proj, ya_pre, yb_pre, w_ba, w_bb, w_out, g2, layer, ts):
    s = x.shape[0]

    def body(x_ref, ga0, ga1, gb0, gb1, ya_ref, yb_ref, wa_ref, wb_ref, wo_ref, g2_ref,
             x1_ref, yao_ref, ybo_ref, mg_ref, h2_ref):
        ya = _dot(ya_ref[...], wa_ref[...])
        yb = _dot(yb_ref[...], wb_ref[...])
        sa = jax.nn.sigmoid(jnp.concatenate([ga0[...], ga1[...]], axis=1).astype(F32))
        sb = jax.nn.sigmoid(jnp.concatenate([gb0[...], gb1[...]], axis=1).astype(F32))
        merged = (sa * ya + sb * yb).astype(BF)
        x1 = x_ref[...] + _dot(merged, wo_ref[...])
        x1_ref[...] = x1
        yao_ref[...] = ya.astype(BF)
        ybo_ref[...] = yb.astype(BF)
        mg_ref[...] = merged
        h2_ref[...] = (x1 * _rms_stats(x1) * g2_ref[...]).astype(BF)

    act = jax.ShapeDtypeStruct((s, D_MODEL), BF)
    return pl.pallas_call(
        body, name="merge_fwd", grid=(s // ts,),
        in_specs=[_tile_spec(ts, D_MODEL)] + _gate_specs(ts) + [
            _tile_spec(ts, D_RNN), _tile_spec(ts, D_SGU),
            _layer_spec(w_ba, layer), _layer_spec(w_bb, layer), _layer_spec(w_out, layer),
            _full_spec((1, D_MODEL))],
        out_specs=[_tile_spec(ts, D_MODEL)] * 5,
        out_shape=[jax.ShapeDtypeStruct((s, D_MODEL), F32), act, act, act, act],
        compiler_params=_params(("parallel",)),
    )(x, proj, proj, proj, proj, ya_pre, yb_pre, w_ba, w_bb, w_out, g2)


def _ffn_call(x1, h2, w_up, w_down, layer, ts):
    s = x1.shape[0]

    def body(x1_ref, h2_ref, wu_ref, wd_ref, x2_ref, p_ref):
        h2v = h2_ref[...]
        acc = x1_ref[...]
        for q in range(N_QUARTERS):
            p = _dot(h2v, wu_ref[q])
            p_ref[:, q * Q_FF:(q + 1) * Q_FF] = p.astype(BF)
            f = jnp.square(jnp.maximum(p, 0.0)).astype(BF)
            acc = acc + _dot(f, wd_ref[q * Q_FF:(q + 1) * Q_FF, :])
        x2_ref[...] = acc

    return pl.pallas_call(
        body, name="ffn_fwd", grid=(s // ts,),
        in_specs=[_tile_spec(ts, D_MODEL), _tile_spec(ts, D_MODEL),
                  pl.BlockSpec((None, N_QUARTERS, D_MODEL, Q_FF), lambda i: (layer, 0, 0, 0)),
                  pl.BlockSpec((None, D_FF, D_MODEL), lambda i: (layer, 0, 0))],
        out_specs=[_tile_spec(ts, D_MODEL), _tile_spec(ts, D_FF)],
        out_shape=[jax.ShapeDtypeStruct((s, D_MODEL), F32), jax.ShapeDtypeStruct((s, D_FF), BF)],
        compiler_params=_params(("parallel",)),
    )(x1, h2, w_up, w_down)


def _loss_call(x, target, gf, ts):
    s = x.shape[0]

    def body(x_ref, t_ref, g_ref, dx_ref, loss_ref, dg_ref):
        @pl.when(pl.program_id(0) == 0)
        def _():
            loss_ref[...] = jnp.zeros_like(loss_ref)
            dg_ref[...] = jnp.zeros_like(dg_ref)

        xv = x_ref[...]
        gv = g_ref[...]
        err = xv * _rms_stats(xv) * gv - t_ref[...]
        part = 0.5 * jnp.sum(jnp.mean(err * err, axis=-1, keepdims=True), axis=0, keepdims=True)
        loss_ref[...] += jnp.broadcast_to(part, loss_ref.shape)
        dx, dg = _rms_bwd(err * (1.0 / D_MODEL), xv, gv)
        dx_ref[...] = dx
        dg_ref[...] += _row_sum(dg)

    return pl.pallas_call(
        body, name="loss_head", grid=(s // ts,),
        in_specs=[_tile_spec(ts, D_MODEL), _tile_spec(ts, D_MODEL), _full_spec((1, D_MODEL))],
        out_specs=[_tile_spec(ts, D_MODEL), _full_spec((1, 128)), _full_spec((1, D_MODEL))],
        out_shape=[jax.ShapeDtypeStruct((s, D_MODEL), F32), jax.ShapeDtypeStruct((1, 128), F32),
                   jax.ShapeDtypeStruct((1, D_MODEL), F32)],
        compiler_params=_params(("arbitrary",)),
    )(x, target, gf)


def _ffn_bwd_call(dx2, p, x1, g2, w_up, w_down, layer, ts):
    s = dx2.shape[0]

    def body(dx2_ref, p_ref, x1_ref, g2_ref, wu_ref, wd_ref, dx1_ref, dp_ref, dg_ref):
        @pl.when(pl.program_id(0) == 0)
        def _():
            dg_ref[...] = jnp.zeros_like(dg_ref)

        dx2v = dx2_ref[...]
        dyb = dx2v.astype(BF)
        dh2 = jnp.zeros((ts, D_MODEL), F32)
        for q in range(N_QUARTERS):
            cols = slice(q * Q_FF, (q + 1) * Q_FF)
            df = _dot_nt(dyb, wd_ref[cols, :])
            dp = (df * (2.0 * jnp.maximum(p_ref[:, cols].astype(F32), 0.0))).astype(BF)
            dp_ref[:, cols] = dp
            dh2 = dh2 + _dot_nt(dp, wu_ref[q])
        dx, dg = _rms_bwd(dh2, x1_ref[...], g2_ref[...])
        dx1_ref[...] = dx2v + dx
        dg_ref[...] += _row_sum(dg)

    return pl.pallas_call(
        body, name="ffn_bwd", grid=(s // ts,),
        in_specs=[_tile_spec(ts, D_MODEL), _tile_spec(ts, D_FF), _tile_spec(ts, D_MODEL),
                  _full_spec((1, D_MODEL)),
                  pl.BlockSpec((None, N_QUARTERS, D_MODEL, Q_FF), lambda i: (layer, 0, 0, 0)),
                  pl.BlockSpec((None, D_FF, D_MODEL), lambda i: (layer, 0, 0))],
        out_specs=[_tile_spec(ts, D_MODEL), _tile_spec(ts, D_FF), _full_spec((1, D_MODEL))],
        out_shape=[jax.ShapeDtypeStruct((s, D_MODEL), F32), jax.ShapeDtypeStruct((s, D_FF), BF),
                   jax.ShapeDtypeStruct((1, D_MODEL), F32)],
        compiler_params=_params(("arbitrary",)),
    )(dx2, p, x1, g2, w_up, w_down)


def _merge_bwd_call(dx1, proj, ya, yb, w_ba, w_bb, w_out, layer, ts, after=None):
    s = dx1.shape[0]

    def body(dx1_ref, ga0, ga1, gb0, gb1, ya_ref, yb_ref, wa_ref, wb_ref, wo_ref, *rest):
        dya_ref, dyb_ref, dgate_ref, dyap_ref, dybp_ref = rest[-5:]
        dm = _dot_nt(dx1_ref[...].astype(BF), wo_ref[...])
        sa = jax.nn.sigmoid(jnp.concatenate([ga0[...], ga1[...]], axis=1).astype(F32))
        sb = jax.nn.sigmoid(jnp.concatenate([gb0[...], gb1[...]], axis=1).astype(F32))
        dya = (dm * sa).astype(BF)
        dyb = (dm * sb).astype(BF)
        dya_ref[...] = dya
        dyb_ref[...] = dyb
        dgate_ref[:, :D_MODEL] = (dm * ya_ref[...].astype(F32) * sa * (1.0 - sa)).astype(BF)
        dgate_ref[:, D_MODEL:] = (dm * yb_ref[...].astype(F32) * sb * (1.0 - sb)).astype(BF)
        dyap_ref[...] = _dot_nt(dya, wa_ref[...]).astype(BF)
        dybp_ref[...] = _dot_nt(dyb, wb_ref[...]).astype(BF)

    act = jax.ShapeDtypeStruct((s, D_MODEL), BF)
    return pl.pallas_call(
        body, name="merge_bwd", grid=(s // ts,),
        in_specs=[_tile_spec(ts, D_MODEL)] + _gate_specs(ts) + [
            _tile_spec(ts, D_MODEL), _tile_spec(ts, D_MODEL),
            _layer_spec(w_ba, layer), _layer_spec(w_bb, layer), _layer_spec(w_out, layer)]
        + ([] if after is None else [pl.BlockSpec(memory_space=pl.ANY)]),
        out_specs=[_tile_spec(ts, D_MODEL), _tile_spec(ts, D_MODEL), _tile_spec(ts, 2 * D_MODEL),
                   _tile_spec(ts, D_RNN), _tile_spec(ts, D_SGU)],
        out_shape=[act, act, jax.ShapeDtypeStruct((s, 2 * D_MODEL), BF),
                   jax.ShapeDtypeStruct((s, D_RNN), BF), jax.ShapeDtypeStruct((s, D_SGU), BF)],
        compiler_params=_params(("parallel",)),
    )(dx1, proj, proj, proj, proj, ya, yb, w_ba, w_bb, w_out, *([] if after is None else [after]))


def _sgu_bwd_call(dyb_pre, proj, wm, bsb, mask, lg, lb, ts):
    s = proj.shape[0]

    def body(dy_ref, uv_ref, wm_ref, bsb_ref, mask_ref, lg_ref, lb_ref,
             duv_ref, dws_ref, dbs_ref, dlg_ref, dlb_ref, dm_sc):
        step = pl.program_id(0)

        @pl.when(step == 0)
        def _():
            dws_ref[...] = jnp.zeros_like(dws_ref)
            dlg_ref[...] = jnp.zeros_like(dlg_ref)
            dlb_ref[...] = jnp.zeros_like(dlb_ref)
            dm_sc[...] = jnp.zeros_like(dm_sc)

        gu, dgu_du = _gelu_and_grad(uv_ref[:, :D_SGU].astype(F32))
        gv, dgv_dv = _gelu_and_grad(uv_ref[:, D_SGU:2 * D_SGU].astype(F32))
        nh, rstd = _layernorm_fwd(gv)
        lgv = lg_ref[...]
        vn = (nh * lgv + lb_ref[...]).astype(BF)
        dy = dy_ref[...].astype(F32)
        du = dy * _sgu_mix(vn, wm_ref, bsb_ref, ts) * dgu_du
        dmix = dy * gu
        dmix_bf = dmix.astype(BF)
        dm_acc = dm_sc[...]
        rows = []
        for blk in range(ts // SGU_BLOCK):
            r0 = blk * SGU_BLOCK
            dm_acc = dm_acc + dmix[r0:r0 + SGU_BLOCK, :]
            cols = []
            for g in range(SGU_GROUPS):
                c0 = g * SGU_BLOCK
                dmg = dmix_bf[r0:r0 + SGU_BLOCK, c0:c0 + SGU_BLOCK]
                cols.append(_dot_tn(wm_ref[g], dmg))
                dws_ref[g] += mask_ref[...] * _dot_nt(dmg, vn[r0:r0 + SGU_BLOCK, c0:c0 + SGU_BLOCK])
            rows.append(jnp.concatenate(cols, axis=1))
        dm_sc[...] = dm_acc
        dvn = jnp.concatenate(rows, axis=0)
        dlg_ref[...] += _row_sum(dvn * nh)
        dlb_ref[...] += _row_sum(dvn)
        dnh = dvn * lgv
        dgv = rstd * (dnh - jnp.mean(dnh, axis=-1, keepdims=True)
                      - nh * jnp.mean(dnh * nh, axis=-1, keepdims=True))
        duv_ref[:, :D_SGU] = du.astype(BF)
        duv_ref[:, D_SGU:] = (dgv * dgv_dv).astype(BF)

        @pl.when(step == pl.num_programs(0) - 1)
        def _():
            for g in range(SGU_GROUPS):
                dbs_ref[:, g:g + 1] = jnp.sum(
                    dm_acc[:, g * SGU_BLOCK:(g + 1) * SGU_BLOCK], axis=1, keepdims=True)

    sw = (SGU_GROUPS, SGU_BLOCK, SGU_BLOCK)
    return pl.pallas_call(
        body, name="sgu_bwd", grid=(s // ts,),
        in_specs=[_tile_spec(ts, D_SGU), _tile_spec(ts, 2 * D_RNN, 1), _full_spec(sw), _full_spec(sw),
                  _full_spec((SGU_BLOCK, SGU_BLOCK)), _full_spec((1, D_SGU)), _full_spec((1, D_SGU))],
        out_specs=[_tile_spec(ts, 2 * D_SGU), _full_spec(sw), _full_spec((SGU_BLOCK, SGU_GROUPS)),
                   _full_spec((1, D_SGU)), _full_spec((1, D_SGU))],
        out_shape=[jax.ShapeDtypeStruct((s, 2 * D_SGU), BF), jax.ShapeDtypeStruct(sw, F32),
                   jax.ShapeDtypeStruct((SGU_BLOCK, SGU_GROUPS), F32),
                   jax.ShapeDtypeStruct((1, D_SGU), F32), jax.ShapeDtypeStruct((1, D_SGU), F32)],
        scratch_shapes=[pltpu.VMEM((SGU_BLOCK, D_SGU), F32)],
        compiler_params=_params(("arbitrary",)),
    )(dyb_pre, proj, wm, bsb, mask, lg, lb)


_ROW_DBA, _ROW_DBX, _ROW_DSP, _ROW_DCB, _ROW_DCW = 0, 1, 2, 3, 4
_PREV_ROWS = 16


def _rnn_bwd_call(dya_pre, proj, xr_saved, hr, wa, wx, ba, bx, sp, cw, ts):
    s = proj.shape[0]
    nt = s // ts
    per = ts // _PREV_ROWS

    def tile(i):
        return nt - 1 - i

    def prev(i):
        return jnp.maximum(tile(i) * per - 1, 0)

    def body(dy_ref, xg_ref, xr_ref, hr_ref, hrp_ref, wa_ref, wx_ref, ba_ref, bx_ref, sp_ref,
             cw_ref, dxg_ref, dwa_ref, dwx_ref, vec_ref,
             lam_carry, a_first, dxr_head, al_sc, bl_sc, lam_sc):
        step = pl.program_id(0)

        @pl.when(step == 0)
        def _():
            dwa_ref[...] = jnp.zeros_like(dwa_ref)
            dwx_ref[...] = jnp.zeros_like(dwx_ref)
            vec_ref[...] = jnp.zeros_like(vec_ref)
            lam_carry[...] = jnp.zeros_like(lam_carry)
            a_first[...] = jnp.zeros_like(a_first)
            dxr_head[...] = jnp.zeros_like(dxr_head)

        has_prev = (step < nt - 1).astype(F32)
        x = xg_ref[:, :D_RNN].astype(F32)
        g = xg_ref[:, D_RNN:].astype(F32)
        h_tail = hrp_ref[_PREV_ROWS - SUBLANES:, :].astype(F32) * has_prev
        xr = xr_ref[...].astype(F32)
        r, i, a, nrm, inv_nrm = _lru_gates(xr, wa_ref, wx_ref, ba_ref, bx_ref, sp_ref)
        h = hr_ref[...].astype(F32)
        dy = dy_ref[...].astype(F32)
        gg, dgg = _gelu_and_grad(g)

        coef = _shift_up(a, jnp.broadcast_to(a_first[...], (SUBLANES, D_RNN)), 1)
        lam_carry[...] = _linear_scan(coef, dy * gg, lam_carry[...], al_sc, bl_sc, lam_sc, True)
        a_first[...] = a[0:1, :]
        lam = lam_sc[...]

        da = lam * _shift_down(h, h_tail, 1)
        dnrm = lam * (i * xr)
        di = lam * nrm * xr
        dlog_a = da * a - dnrm * (a * a) * inv_nrm
        spv = sp_ref[...]
        dza = (dlog_a * (-LRU_C * spv)) * (r * (1.0 - r))
        dzx = di * (i * (1.0 - i))
        vec_ref[_ROW_DSP:_ROW_DSP + 1, :] += _row_sum(dlog_a * (-LRU_C * r))
        vec_ref[_ROW_DBA:_ROW_DBA + 1, :] += _row_sum(dza)
        vec_ref[_ROW_DBX:_ROW_DBX + 1, :] += _row_sum(dzx)
        xb = xr.astype(BF)
        dza_bf = dza.astype(BF)
        dzx_bf = dzx.astype(BF)
        for grp in range(N_LRU_GROUPS):
            cols = slice(grp * LRU_GROUP, (grp + 1) * LRU_GROUP)
            dwa_ref[grp] += _dot_tn(xb[:, cols], dza_bf[:, cols])
            dwx_ref[grp] += _dot_tn(xb[:, cols], dzx_bf[:, cols])
        dxr = (lam * nrm * i + _group_dot(dza_bf, wa_ref, _dot_nt) + _group_dot(dzx_bf, wx_ref, _dot_nt))

        vec_ref[_ROW_DCB:_ROW_DCB + 1, :] += _row_sum(dxr)
        head = dxr_head[...]
        dx = cw_ref[CONV_WIDTH - 1:CONV_WIDTH, :] * dxr
        vec_ref[_ROW_DCW + 3:_ROW_DCW + 4, :] += _row_sum(dxr * x)
        for sft in range(1, CONV_WIDTH):
            k = CONV_WIDTH - 1 - sft
            ahead = _shift_up(dxr, head, sft)
            dx = dx + cw_ref[k:k + 1, :] * ahead
            vec_ref[_ROW_DCW + k:_ROW_DCW + k + 1, :] += _row_sum(ahead * x)
        dxr_head[...] = dxr[0:SUBLANES, :]
        dxg_ref[:, :D_RNN] = dx.astype(BF)
        dxg_ref[:, D_RNN:] = (dy * h * dgg).astype(BF)

    gw = (N_LRU_GROUPS, LRU_GROUP, LRU_GROUP)
    rev = lambda width: pl.BlockSpec((ts, width), lambda i: (tile(i), 0))
    return pl.pallas_call(
        body, name="rnn_bwd", grid=(nt,),
        in_specs=[rev(D_RNN), rev(2 * D_RNN), rev(D_RNN), rev(D_RNN),
                  pl.BlockSpec((_PREV_ROWS, D_RNN), lambda i: (prev(i), 0)),
                  _full_spec(gw), _full_spec(gw),
                  _full_spec((1, D_RNN)), _full_spec((1, D_RNN)), _full_spec((1, D_RNN)),
                  _full_spec((CONV_WIDTH, D_RNN))],
        out_specs=[rev(2 * D_RNN), _full_spec(gw), _full_spec(gw), _full_spec((SUBLANES, D_RNN))],
        out_shape=[jax.ShapeDtypeStruct((s, 2 * D_RNN), BF), jax.ShapeDtypeStruct(gw, F32),
                   jax.ShapeDtypeStruct(gw, F32), jax.ShapeDtypeStruct((SUBLANES, D_RNN), F32)],
        scratch_shapes=[pltpu.VMEM((1, D_RNN), F32), pltpu.VMEM((1, D_RNN), F32),
                        pltpu.VMEM((SUBLANES, D_RNN), F32),
                        pltpu.VMEM((ts, D_RNN), F32), pltpu.VMEM((ts, D_RNN), F32),
                        pltpu.VMEM((ts, D_RNN), F32)],
        compiler_params=_params(("arbitrary",)),
    )(dya_pre, proj, xr_saved, hr, hr, wa, wx, ba, bx, sp, cw)


def _inproj_bwd_call(dxg, duv, dgate, dx1, x, g1, w_in, layer, ts):
    s = x.shape[0]

    def body(dxg_ref, duv_ref, dgt_ref, dx1_ref, x_ref, g_ref, w_ref, dx_ref, dproj_ref, dg_ref):
        @pl.when(pl.program_id(0) == 0)
        def _():
            dg_ref[...] = jnp.zeros_like(dg_ref)

        dproj = jnp.concatenate([dxg_ref[...], duv_ref[...], dgt_ref[...]], axis=1)
        dproj_ref[...] = dproj
        dh = jnp.zeros((ts, D_MODEL), F32)
        for q in range(N_QUARTERS):
            dh = dh + _dot_nt(dproj[:, q * Q_IN:(q + 1) * Q_IN], w_ref[q])
        dx, dg = _rms_bwd(dh, x_ref[...], g_ref[...])
        dx_ref[...] = dx1_ref[...] + dx
        dg_ref[...] += _row_sum(dg)

    return pl.pallas_call(
        body, name="inproj_bwd", grid=(s // ts,),
        in_specs=[_tile_spec(ts, 2 * D_RNN), _tile_spec(ts, 2 * D_SGU), _tile_spec(ts, 2 * D_MODEL),
                  _tile_spec(ts, D_MODEL), _tile_spec(ts, D_MODEL), _full_spec((1, D_MODEL)),
                  pl.BlockSpec((None, N_QUARTERS, D_MODEL, Q_IN), lambda i: (layer, 0, 0, 0))],
        out_specs=[_tile_spec(ts, D_MODEL), _tile_spec(ts, D_IN), _full_spec((1, D_MODEL))],
        out_shape=[jax.ShapeDtypeStruct((s, D_MODEL), F32), jax.ShapeDtypeStruct((s, D_IN), BF),
                   jax.ShapeDtypeStruct((1, D_MODEL), F32)],
        compiler_params=_params(("arbitrary",)),
    )(dxg, duv, dgate, dx1, x, g1, w_in)


def _relu_sq(p):
    return jnp.square(jnp.maximum(p, 0))


def _wgrad_call(a, b, core, tm, tn, tk, col_blocked, name, a_fn=None):
    s, m = a.shape
    n = b.shape[1]
    r, cols = (m, n // N_QUARTERS) if col_blocked else (m // N_QUARTERS, n)
    r2 = r // 2
    per_tile = tm // r
    steps = s // tk

    def body(core_ref, a_ref, b_ref, keep_ref, send_ref, *acc):
        av = a_ref[...]
        if a_fn is not None:
            av = a_fn(av)
        prod = _dot_tn(av.astype(BF), b_ref[...].astype(BF))

        def emit(total):
            for h in range(2):
                @pl.when(core_ref[0] == h)
                def _():
                    for q in range(per_tile):
                        keep_ref[q] = total[q * r + h * r2:q * r + (h + 1) * r2]
                        send_ref[q] = total[q * r + (1 - h) * r2:q * r + (2 - h) * r2].astype(BF)

        if steps == 1:
            emit(prod)
        else:
            acc_ref, = acc
            step = pl.program_id(2)

            @pl.when(step == 0)
            def _():
                acc_ref[...] = prod

            @pl.when(jnp.logical_and(step > 0, step < steps - 1))
            def _():
                acc_ref[...] += prod

            @pl.when(step == steps - 1)
            def _():
                emit(acc_ref[...] + prod)

    if col_blocked:
        per_q = cols // tn
        out_spec = pl.BlockSpec((1, r2, tn), lambda i, j, k, c: (j // per_q, 0, j % per_q))
    else:
        out_spec = pl.BlockSpec((per_tile, r2, tn), lambda i, j, k, c: (i, 0, j))
    return pl.pallas_call(
        body, name=name,
        out_shape=[jax.ShapeDtypeStruct((N_QUARTERS, r2, cols), F32),
                   jax.ShapeDtypeStruct((N_QUARTERS, r2, cols), BF)],
        grid_spec=pltpu.PrefetchScalarGridSpec(
            num_scalar_prefetch=1, grid=(m // tm, n // tn, steps),
            in_specs=[pl.BlockSpec((tk, tm), lambda i, j, k, c: (k, i)),
                      pl.BlockSpec((tk, tn), lambda i, j, k, c: (k, j))],
            out_specs=[out_spec, out_spec],
            scratch_shapes=[] if steps == 1 else [pltpu.VMEM((tm, tn), F32)]),
        compiler_params=_params(("parallel", "parallel", "arbitrary")),
    )(core, a, b)


BIG = ("w_in", "w_up", "w_down", "w_branch_a", "w_branch_b", "w_out")


def _block_diag(w):
    w4 = w.reshape(N_LRU_GROUPS, HEADS_PER_GROUP, RNN_HEAD_DIM, RNN_HEAD_DIM)
    eye = jnp.eye(HEADS_PER_GROUP, dtype=w.dtype)
    return jnp.einsum("gjio,jk->gjiko", w4, eye).reshape(N_LRU_GROUPS, LRU_GROUP, LRU_GROUP)


def _block_diag_extract(d):
    d5 = d.reshape(N_LRU_GROUPS, HEADS_PER_GROUP, RNN_HEAD_DIM, HEADS_PER_GROUP, RNN_HEAD_DIM)
    blocks = [d5[:, j, :, j, :] for j in range(HEADS_PER_GROUP)]
    return jnp.stack(blocks, axis=1).reshape(RNN_HEADS, RNN_HEAD_DIM, RNN_HEAD_DIM)


def _sgu_mask():
    chunk = jnp.arange(SGU_BLOCK) // CHUNK
    return (chunk[:, None] >= chunk[None, :]).astype(F32)


def _layer_small(sm, l, core):
    row = lambda v: v.reshape(1, -1)
    return dict(
        core=core,
        g1=row(sm["norm_mix_g"][l]), g2=row(sm["norm_ffn_g"][l]),
        wa=_block_diag(sm["lru_w_a"][l]).astype(BF), wx=_block_diag(sm["lru_w_x"][l]).astype(BF),
        ba=row(sm["lru_b_a"][l]), bx=row(sm["lru_b_x"][l]),
        sp=row(jax.nn.softplus(-sm["lru_lambda"][l])),
        cw=sm["conv_w"][l], cb=row(sm["conv_b"][l]),
        wm=(sm["sgu_w_s"][l] * _sgu_mask()).astype(BF),
        bsb=jnp.broadcast_to(sm["sgu_b_s"][l][:, :, None], (SGU_GROUPS, SGU_BLOCK, SGU_BLOCK)),
        lg=row(sm["sgu_ln_g"][l]), lb=row(sm["sgu_ln_b"][l]),
    )


def _layer_fwd_mix(x, big, p, ts, h=None, before_sgu=None):
    if h is None:
        h = _norm_call(x, p["g1"], ts)
    proj = _inproj_call(h, big["w_in"], 0, 2 * ts)
    xr, hr, ya_pre = _rnn_fwd_call(proj, p["wa"], p["wx"], p["ba"], p["bx"], p["sp"], p["cw"], p["cb"], ts)
    lg = p["lg"] if before_sgu is None else p["lg"] + before_sgu(ya_pre)
    yb_pre = _sgu_fwd_call(proj, p["wm"], p["bsb"], lg, p["lb"], ts)
    return dict(p=p, x=x, h=h, proj=proj, xr=xr, hr=hr, ya_pre=ya_pre, yb_pre=yb_pre)


def _layer_fwd_out(sv, big, ts):
    x1, ya, yb, merged, h2 = _merge_call(sv["x"], sv["proj"], sv["ya_pre"], sv["yb_pre"], big["w_branch_a"],
                                         big["w_branch_b"], big["w_out"], sv["p"]["g2"], 0, ts)
    x2, pre = _ffn_call(x1, h2, big["w_up"], big["w_down"], 0, ts)
    sv.update(x1=x1, ya=ya, yb=yb, merged=merged, h2=h2, pre=pre)
    return x2


def _layer_bwd_ffn(dx, sv, big, ts):
    p = sv["p"]
    dx1, dpre, dg2 = _ffn_bwd_call(dx, sv["pre"], sv["x1"], p["g2"], big["w_up"], big["w_down"], 0, ts)
    tk = dx.shape[0]
    gb = dict(
        w_down=_wgrad_call(sv["pre"], dx, p["core"], Q_FF, D_MODEL // 2, tk, False, "wgrad_down", a_fn=_relu_sq),
        w_up=_wgrad_call(sv["h2"], dpre, p["core"], D_MODEL, Q_FF, tk, True, "wgrad_up"))
    return dx1, gb, dict(norm_ffn_g=dg2[0])


def _layer_bwd_merge(dx1, sv, big, ts, after=None):
    tk = dx1.shape[0]
    core = sv["p"]["core"]
    dya, dyb, dgate, dya_pre, dyb_pre = _merge_bwd_call(
        dx1, sv["proj"], sv["ya"], sv["yb"], big["w_branch_a"], big["w_branch_b"], big["w_out"], 0, ts, after)
    gb = dict(
        w_out=_wgrad_call(sv["merged"], dx1, core, D_MODEL, D_MODEL // 2, tk, False, "wgrad_out"),
        w_branch_a=_wgrad_call(sv["ya_pre"], dya, core, D_RNN, D_MODEL // 2, tk, False, "wgrad_branch_a"),
        w_branch_b=_wgrad_call(sv["yb_pre"], dyb, core, D_SGU, D_MODEL // 2, tk, False, "wgrad_branch_b"))
    return (dgate, dya_pre, dyb_pre), gb


def _layer_bwd_branches(dx1, merge_out, sv, big, lam, ts):
    p = sv["p"]
    tk = dx1.shape[0]
    dgate, dya_pre, dyb_pre = merge_out
    gb = {}
    duv, dws, dbs, dlg, dlb = _sgu_bwd_call(dyb_pre, sv["proj"], p["wm"], p["bsb"], _sgu_mask(), p["lg"], p["lb"],
                                            ts)
    dxg, dwa, dwx, vec = _rnn_bwd_call(dya_pre, sv["proj"], sv["xr"], sv["hr"], p["wa"], p["wx"], p["ba"], p["bx"],
                                       p["sp"], p["cw"], ts // 2)
    dx, dproj, dg1 = _inproj_bwd_call(dxg, duv, dgate, dx1, sv["x"], p["g1"], big["w_in"], 0, ts)
    gb["w_in"] = _wgrad_call(sv["h"], dproj, p["core"], D_MODEL, Q_IN, tk // 2, True, "wgrad_in")
    gs = dict(
        norm_mix_g=dg1[0], conv_w=vec[_ROW_DCW:_ROW_DCW + CONV_WIDTH], conv_b=vec[_ROW_DCB],
        lru_w_a=_block_diag_extract(dwa), lru_w_x=_block_diag_extract(dwx),
        lru_b_a=vec[_ROW_DBA].reshape(RNN_HEADS, RNN_HEAD_DIM), lru_b_x=vec[_ROW_DBX].reshape(RNN_HEADS, RNN_HEAD_DIM),
        lru_lambda=-vec[_ROW_DSP] * jax.nn.sigmoid(-lam),
        sgu_ln_g=dlg[0], sgu_ln_b=dlb[0], sgu_w_s=dws, sgu_b_s=dbs.T)
    return dx, gb, gs


def _local_step(x, target, big, sm, ts):
    saved = []
    core = jnp.zeros((1,), jnp.int32)
    for l in range(DEPTH):
        sv = _layer_fwd_mix(x, big[l], _layer_small(sm, l, core), ts)
        x = _layer_fwd_out(sv, big[l], ts)
        saved.append(sv)
    dx, loss, dgf = _loss_call(x, target, sm["final_norm_g"].reshape(1, -1), ts)
    gb, gs = [None] * DEPTH, [None] * DEPTH
    for l in reversed(range(DEPTH)):
        dx1, gb_ffn, gs_ffn = _layer_bwd_ffn(dx, saved[l], big[l], ts)
        merge_out, gb_merge = _layer_bwd_merge(dx1, saved[l], big[l], ts)
        dx, gb_mix, gs_mix = _layer_bwd_branches(dx1, merge_out, saved[l], big[l], sm["lru_lambda"][l], ts)
        gb[l] = {**gb_ffn, **gb_merge, **gb_mix}
        gs[l] = {**gs_ffn, **gs_mix}
    gs = {k: jnp.stack([g[k] for g in gs]) for k in gs[0]}
    gs["final_norm_g"] = dgf[0]
    return loss, dx, gb, gs


EW_VMEM_BYTES = 24 * 1024 * 1024


def _row_block(rows, cols, bytes_per_elem):
    for br in range(min(rows, EW_VMEM_BYTES // (2 * bytes_per_elem * cols)), 0, -1):
        if rows % br == 0 and br % 16 == 0:
            return br
    return rows


def _ew_call(fn, name, operands, outputs, slabs=1, sel=None, into=None, after=None):
    if into is not None and not isinstance(into, (list, tuple)):
        into = [into]
    rows, cols = outputs[0][0].shape[2:]
    br = _row_block(rows, cols, sum(jnp.dtype(a.dtype).itemsize for a, _ in operands + outputs))
    n_in = len(operands)

    def pick(tok, g, s):
        if callable(tok):
            return tok(g, s)
        if tok == "g":
            return g
        if isinstance(tok, tuple):
            return s[tok[1]]
        return tok

    def spec(idx):
        return pl.BlockSpec((None, None, br, cols),
                            lambda g, i, s, idx=idx: (pick(idx[0], g, s), pick(idx[1], g, s), i, 0))

    if sel is None:
        sel = jnp.zeros((1,), jnp.int32)
    in_specs = [spec(idx) for _, idx in operands]
    arrays = [a for a, _ in operands]
    aliases = {}
    for j, buf in enumerate(into or ()):
        in_specs.append(pl.BlockSpec(memory_space=pl.ANY))
        arrays.append(buf)
        aliases[1 + n_in + j] = j
    if after is not None:
        in_specs.append(pl.BlockSpec(memory_space=pl.ANY))
        arrays.append(after)

    def body(sel_ref, *refs):
        outs = fn(*[r[...] for r in refs[:n_in]])
        for o_ref, o in zip(refs[len(arrays):], outs):
            o_ref[...] = o.astype(o_ref.dtype)

    return pl.pallas_call(
        body, name=name, out_shape=[s for s, _ in outputs],
        grid_spec=pltpu.PrefetchScalarGridSpec(
            num_scalar_prefetch=1, grid=(slabs, rows // br),
            in_specs=in_specs,
            out_specs=[spec(idx) for _, idx in outputs]),
        input_output_aliases=aliases,
        compiler_params=_params(("parallel", "parallel")),
    )(sel, *arrays)


def _as4(a):
    return a.reshape((1,) * (4 - a.ndim) + a.shape)


def _adamw(w, g, m, v):
    m = ADAM_B1 * m + (1.0 - ADAM_B1) * g
    v = ADAM_B2 * v + (1.0 - ADAM_B2) * jnp.square(g)
    m_hat = m / (1.0 - ADAM_B1 ** ADAM_STEP)
    v_hat = v / (1.0 - ADAM_B2 ** ADAM_STEP)
    delta = -ADAM_LR * (m_hat / (jnp.sqrt(v_hat) + ADAM_EPS) + ADAM_WD * w)
    return delta, m, v


def _small_adamw_call(ws, gs, ms, vs):
    n = len(ws)

    def body(*refs):
        for k in range(n):
            w, g, m, v = (refs[j * n + k][...] for j in range(4))
            outs = _adamw(w, g, m, v)
            for j in range(3):
                refs[(4 + j) * n + k][...] = outs[j]

    shapes = [jax.ShapeDtypeStruct(w.shape, F32) for w in ws]
    outs = pl.pallas_call(
        body, name="adamw_small", out_shape=shapes * 3,
        in_specs=[pl.BlockSpec(memory_space=pltpu.VMEM)] * (4 * n),
        out_specs=[pl.BlockSpec(memory_space=pltpu.VMEM)] * (3 * n),
        compiler_params=_params(),
    )(*ws, *gs, *ms, *vs)
    return outs[:n], outs[n:2 * n], outs[2 * n:]


ANY = pl.BlockSpec(memory_space=pl.ANY)


def _place():
    x, y, c = lax.axis_index("x"), lax.axis_index("y"), lax.axis_index("c")
    chips = [(1 - x, y), (x, 1 - y), (1 - x, 1 - y)]
    return x, y, c, chips


def _remote(src, dst, send_sem, recv_sem, to):
    return pltpu.make_async_remote_copy(src_ref=src, dst_ref=dst, send_sem=send_sem, recv_sem=recv_sem,
                                        device_id=to, device_id_type=MESH)


def _gather_call(bufs):
    n = len(bufs)

    def body(*refs):
        out = refs[n:2 * n]
        send_sems, recv_sems = refs[2 * n:]
        x, y, c, chips = _place()
        me_q = 2 * x + y
        sibling = (x, y, 1 - c)
        first = []
        for w in range(n):
            for j, chip in enumerate(chips):
                mine = out[w].at[c, me_q]
                first.append(_remote(mine, mine, send_sems.at[w * 3 + j], recv_sems.at[w * 3 + j], (*chip, c)))
        for cp in first:
            cp.start()
        passed = []
        for w in range(n):
            for j, (qx, qy) in enumerate(chips):
                landed = out[w].at[c, 2 * qx + qy]
                k = w * 3 + j
                _remote(landed, landed, send_sems.at[k], recv_sems.at[k], (qx, qy, c)).wait_recv()
                cp = _remote(landed, landed, send_sems.at[3 * n + k], recv_sems.at[3 * n + k], sibling)
                cp.start()
                passed.append(cp)
        for w in range(n):
            for j, (qx, qy) in enumerate(chips):
                landed = out[w].at[1 - c, 2 * qx + qy]
                k = 3 * n + w * 3 + j
                _remote(landed, landed, send_sems.at[k], recv_sems.at[k], sibling).wait_recv()
        for cp in first + passed:
            cp.wait_send()

    return pl.pallas_call(
        body, name="gather_weights",
        out_shape=[jax.ShapeDtypeStruct(a.shape, a.dtype) for a in bufs],
        in_specs=[ANY] * n, out_specs=[ANY] * n,
        input_output_aliases={w: w for w in range(n)},
        scratch_shapes=[pltpu.SemaphoreType.DMA((6 * n,)), pltpu.SemaphoreType.DMA((6 * n,))],
        compiler_params=_params(vmem=False, has_side_effects=True),
    )(*bufs)


def _sibling_send_call(items):
    n = len(items)

    def body(*refs):
        src, out = refs[:n], refs[n:2 * n]
        send_sems, recv_sems = refs[2 * n:]
        x, y, c, _ = _place()
        copies = [_remote(src[w], out[w], send_sems.at[w], recv_sems.at[w], (x, y, 1 - c)) for w in range(n)]
        for cp in copies:
            cp.start()
        for cp in copies:
            cp.wait()

    return pl.pallas_call(
        body, name="grads_to_sibling",
        out_shape=[jax.ShapeDtypeStruct(a.shape, a.dtype) for a in items],
        in_specs=[ANY] * n, out_specs=[ANY] * n,
        scratch_shapes=[pltpu.SemaphoreType.DMA((n,)), pltpu.SemaphoreType.DMA((n,))],
        compiler_params=_params(vmem=False, has_side_effects=True),
    )(*items)


def _sibling_inplace_call(name, bufs, slabs, n_pairs):
    n = len(bufs)

    def body(*refs):
        out = refs[n:2 * n]
        send_sems, recv_sems = refs[2 * n:]
        x, y, c, _ = _place()
        sibling = (x, y, 1 - c)
        pairs = [pair for w, ref in enumerate(out) for pair in slabs(ref, c, w)]
        sends = [_remote(s, s, send_sems.at[k], recv_sems.at[k], sibling) for k, (s, _) in enumerate(pairs)]
        for cp in sends:
            cp.start()
        for k, (_, r) in enumerate(pairs):
            _remote(r, r, send_sems.at[k], recv_sems.at[k], sibling).wait_recv()
        for cp in sends:
            cp.wait_send()

    return pl.pallas_call(
        body, name=name,
        out_shape=[jax.ShapeDtypeStruct(a.shape, a.dtype) for a in bufs],
        in_specs=[ANY] * n, out_specs=[ANY] * n,
        input_output_aliases={w: w for w in range(n)},
        scratch_shapes=[pltpu.SemaphoreType.DMA((n_pairs,)), pltpu.SemaphoreType.DMA((n_pairs,))],
        compiler_params=_params(vmem=False, has_side_effects=True),
    )(*bufs)


HBM_SPEC = pl.BlockSpec(memory_space=pltpu.HBM)
SEM_SPEC = pl.BlockSpec(memory_space=pltpu.SEMAPHORE)
DATAFLOW_EFFECT = pltpu.SideEffectType.DATAFLOW_SIDE_EFFECTING


def _exchange_start(name, bufs, copies, n_copies, after):
    n = len(bufs)

    def body(*refs):
        ins, send_sems, recv_sems, token = refs[:n], refs[n + 1], refs[n + 2], refs[-1]
        for k, (src, dst, to) in enumerate(copies(ins)):
            _remote(src, dst, send_sems.at[k], recv_sems.at[k], to).start()
        token[...] = jnp.zeros_like(token)

    outs = pl.pallas_call(
        body, name=name,
        out_shape=(pltpu.SemaphoreType.DMA((n_copies,)), pltpu.SemaphoreType.DMA((n_copies,)),
                   *[pltpu.HBM(b.shape, b.dtype) for b in bufs], jax.ShapeDtypeStruct((SUBLANES, 128), F32)),
        in_specs=[HBM_SPEC] * n + [ANY],
        out_specs=(SEM_SPEC, SEM_SPEC, *[HBM_SPEC] * n, pl.BlockSpec(memory_space=pltpu.VMEM)),
        input_output_aliases={w: w + 2 for w in range(n)},
        compiler_params=pltpu.CompilerParams(has_side_effects=DATAFLOW_EFFECT),
    )(*[pltpu.with_memory_space_constraint(b, pltpu.HBM) for b in bufs], after)
    return outs[0], outs[1], list(outs[2:2 + n]), outs[-1]


def _exchange_wait(name, send_sems, recv_sems, bufs, copies, after):
    n = len(bufs)

    def body(*refs):
        ins, send_sems, recv_sems = refs[:n], refs[n], refs[n + 1]
        for k, (src, dst, to) in enumerate(copies(ins)):
            cp = _remote(src, dst, send_sems.at[k], recv_sems.at[k], to)
            cp.wait_send()
            cp.wait_recv()

    return pl.pallas_call(
        body, name=name,
        out_shape=[pltpu.HBM(b.shape, b.dtype) for b in bufs],
        in_specs=[HBM_SPEC] * n + [SEM_SPEC, SEM_SPEC, ANY],
        out_specs=[HBM_SPEC] * n,
        input_output_aliases={w: w for w in range(n)},
        compiler_params=pltpu.CompilerParams(has_side_effects=DATAFLOW_EFFECT),
    )(*bufs, send_sems, recv_sems, after)


def _gather_copies(refs):
    x, y, c, chips = _place()
    mine = 2 * (2 * x + y) + c
    return [(ref.at[mine], ref.at[mine], (qx, qy, c)) for ref in refs for qx, qy in chips]


def _forward_copies(refs):
    x, y, c, chips = _place()
    return [(ref.at[2 * (2 * qx + qy) + c], ref.at[2 * (2 * qx + qy) + c], (x, y, 1 - c))
            for ref in refs for qx, qy in chips]


def _gather_forward_slabs(ref, c, w):
    x, y, _, chips = _place()
    return [(ref.at[2 * (2 * qx + qy) + c], ref.at[2 * (2 * qx + qy) + 1 - c]) for qx, qy in chips]


def _device_peers():
    x, y, c, _ = _place()
    return 4 * x + 2 * y + c, [(k, (x ^ ((k >> 2) & 1), y ^ ((k >> 1) & 1), c ^ (k & 1))) for k in range(1, 8)]


def _small_scatter_copies(refs):
    me, peers = _device_peers()
    return [(refs[0].at[me ^ k], refs[1].at[me], to) for k, to in peers]


def _small_spread_copies(refs):
    me, peers = _device_peers()
    return [(refs[0].at[me], refs[0].at[me], to) for _, to in peers]


def _sibling_copies(refs):
    n = len(refs) // 2
    x, y, c, _ = _place()
    return [(refs[w], refs[n + w], (x, y, 1 - c)) for w in range(n)]


def _owner_copies(refs):
    n = len(refs) // 2
    x, y, c, chips = _place()
    return [(refs[w].at[2 * qx + qy], refs[n + w].at[j], (qx, qy, c))
            for w in range(n) for j, (qx, qy) in enumerate(chips)]


N_DEVICES = 8
SMALL_ROWS = 616


SMALL = ("norm_mix_g", "conv_w", "conv_b", "lru_w_a", "lru_b_a", "lru_w_x", "lru_b_x", "lru_lambda",
         "sgu_ln_g", "sgu_ln_b", "sgu_w_s", "sgu_b_s", "norm_ffn_g", "final_norm_g")
WEIGHTS = ("norm_mix_g", "w_in", "conv_w", "conv_b", "lru_w_a", "lru_b_a", "lru_w_x", "lru_b_x", "lru_lambda",
           "sgu_ln_g", "sgu_ln_b", "sgu_w_s", "sgu_b_s", "w_branch_a", "w_branch_b", "w_out", "norm_ffn_g",
           "w_up", "w_down", "final_norm_g")
PACK_ALIGN = SUBLANES * 128


def _pack_small(gs):
    parts = []
    for k in SMALL:
        flat = gs[k].reshape(-1)
        parts.append(jnp.pad(flat, (0, -flat.size % PACK_ALIGN)))
    flat = jnp.concatenate(parts)
    flat = jnp.pad(flat, (0, N_DEVICES * SMALL_ROWS * 128 - flat.size))
    return flat.reshape(N_DEVICES, SMALL_ROWS, 128)


def _unpack_small(buf, like):
    flat = buf.reshape(-1)
    out, off = {}, 0
    for k in SMALL:
        size = like[k].size
        out[k] = flat[off:off + size].reshape(like[k].shape)
        off += size + (-size % PACK_ALIGN)
    return out


def _as_rows(a):
    return a.reshape(-1, a.shape[-1])


def kernel(x, norm_mix_g, w_in, conv_w, conv_b, lru_w_a, lru_b_a, lru_w_x, lru_b_x, lru_lambda, sgu_ln_g, sgu_ln_b, sgu_w_s, sgu_b_s, w_branch_a, w_branch_b, w_out, norm_ffn_g, w_up, w_down, final_norm_g, loss_target, m_norm_mix_g, m_w_in, m_conv_w, m_conv_b, m_lru_w_a, m_lru_b_a, m_lru_w_x, m_lru_b_x, m_lru_lambda, m_sgu_ln_g, m_sgu_ln_b, m_sgu_w_s, m_sgu_b_s, m_w_branch_a, m_w_branch_b, m_w_out, m_norm_ffn_g, m_w_up, m_w_down, m_final_norm_g, v_norm_mix_g, v_w_in, v_conv_w, v_conv_b, v_lru_w_a, v_lru_b_a, v_lru_w_x, v_lru_b_x, v_lru_lambda, v_sgu_ln_g, v_sgu_ln_b, v_sgu_w_s, v_sgu_b_s, v_w_branch_a, v_w_branch_b, v_w_out, v_norm_ffn_g, v_w_up, v_w_down, v_final_norm_g):
    w = dict(norm_mix_g=norm_mix_g, w_in=w_in, conv_w=conv_w, conv_b=conv_b, lru_w_a=lru_w_a, lru_b_a=lru_b_a,
             lru_w_x=lru_w_x, lru_b_x=lru_b_x, lru_lambda=lru_lambda, sgu_ln_g=sgu_ln_g, sgu_ln_b=sgu_ln_b,
             sgu_w_s=sgu_w_s, sgu_b_s=sgu_b_s, w_branch_a=w_branch_a, w_branch_b=w_branch_b, w_out=w_out,
             norm_ffn_g=norm_ffn_g, w_up=w_up, w_down=w_down, final_norm_g=final_norm_g)
    m = dict(norm_mix_g=m_norm_mix_g, w_in=m_w_in, conv_w=m_conv_w, conv_b=m_conv_b, lru_w_a=m_lru_w_a,
             lru_b_a=m_lru_b_a, lru_w_x=m_lru_w_x, lru_b_x=m_lru_b_x, lru_lambda=m_lru_lambda,
             sgu_ln_g=m_sgu_ln_g, sgu_ln_b=m_sgu_ln_b, sgu_w_s=m_sgu_w_s, sgu_b_s=m_sgu_b_s,
             w_branch_a=m_w_branch_a, w_branch_b=m_w_branch_b, w_out=m_w_out, norm_ffn_g=m_norm_ffn_g,
             w_up=m_w_up, w_down=m_w_down, final_norm_g=m_final_norm_g)
    v = dict(norm_mix_g=v_norm_mix_g, w_in=v_w_in, conv_w=v_conv_w, conv_b=v_conv_b, lru_w_a=v_lru_w_a,
             lru_b_a=v_lru_b_a, lru_w_x=v_lru_w_x, lru_b_x=v_lru_b_x, lru_lambda=v_lru_lambda,
             sgu_ln_g=v_sgu_ln_g, sgu_ln_b=v_sgu_ln_b, sgu_w_s=v_sgu_w_s, sgu_b_s=v_sgu_b_s,
             w_branch_a=v_w_branch_a, w_branch_b=v_w_branch_b, w_out=v_w_out, norm_ffn_g=v_norm_ffn_g,
             w_up=v_w_up, w_down=v_w_down, final_norm_g=v_final_norm_g)
    core = lax.axis_index("c")
    chip = 2 * lax.axis_index("x") + lax.axis_index("y")
    sel = jnp.stack([core, 1 - core, chip, 2 * chip + core]).astype(jnp.int32)
    this_core, other_core, this_chip = ("sel", 0), ("sel", 1), ("sel", 2)
    sds = jax.ShapeDtypeStruct

    ts = TOKEN_TILE

    def after_all(arrays):
        return jnp.stack([a[(0,) * a.ndim].astype(F32) for a in arrays])

    halves ={k: (w[k].shape[1] // 2, w[k].shape[2]) for k in BIG}

    def half_view(k, a):
        return a.reshape((2 * N_QUARTERS,) + halves[k])

    def full_view(k, a):
        r2, cols = halves[k]
        if k in ("w_in", "w_up"):
            return a.reshape(1, N_QUARTERS, 2 * r2, cols)
        return a.reshape(1, 2 * N_QUARTERS * r2, cols)

    layer_bufs = [{}, {}]

    def cast_weights(k, after):
        _, r, cols = w[k].shape
        w4 = w[k].reshape(DEPTH, 1, r, cols)
        outs = _ew_call(lambda a, b: (a, b), "cast_weights", [(w4, (0, 0)), (w4, (1, 0))],
                        [(sds((1, N_QUARTERS, r, cols), BF), (0, this_chip))] * DEPTH, 1, sel, after=after)
        for l in range(DEPTH):
            layer_bufs[l][k] = half_view(k, outs[l])

    conv_buf = lax.dynamic_update_slice_in_dim(
        jnp.zeros((DEPTH, N_QUARTERS) + conv_w.shape[1:], F32), conv_w[:, None], chip, axis=1)
    sm = {k: w[k] for k in SMALL}
    sm["conv_w"] = _gather_call([conv_buf])[0].transpose(0, 2, 1, 3).reshape(DEPTH, CONV_WIDTH, D_RNN)

    def gather_start(tag, l, keys, after):
        bufs = [layer_bufs[l][k] for k in keys]
        return _exchange_start(f"gather_start_{tag}", bufs, _gather_copies, 3 * len(keys), after)

    def gather_finish(tag, keys, started, after):
        send_sems, recv_sems, thru, _ = started
        landed = _exchange_wait(f"gather_wait_{tag}", send_sems, recv_sems, thru, _gather_copies, after)
        landed = _sibling_inplace_call("gather_forward", landed, _gather_forward_slabs, 3 * len(keys))
        return {k: full_view(k, a) for k, a in zip(keys, landed)}

    first, rest = ("w_in",), tuple(k for k in BIG if k != "w_in")
    cast_weights("w_in", None)
    started_a = gather_start("0a", 0, first, sm["conv_w"])
    for k in rest:
        cast_weights(k, started_a[3])
    started_b = gather_start("0b", 0, rest, started_a[3])
    started_c = gather_start("1a", 1, first, started_b[3])
    started_d = gather_start("1b", 1, rest, started_c[3])

    def rest_arrives(tag, started):
        state = {}

        def hook(after):
            landed = _exchange_wait(f"gather_wait_{tag}", started[0], started[1], started[2], _gather_copies, after)
            state["forward"] = _exchange_start(f"forward_start_{tag}", landed, _forward_copies, 3 * len(rest), after)
            return state["forward"][3][0, 0]

        def finish(after):
            send_sems, recv_sems, thru, _ = state["forward"]
            done = _exchange_wait(f"forward_wait_{tag}", send_sems, recv_sems, thru, _forward_copies, after)
            return {k: full_view(k, a) for k, a in zip(rest, done)}

        return hook, finish

    p0, p1 = _layer_small(sm, 0, sel[0:1]), _layer_small(sm, 1, sel[0:1])
    h0 = _norm_call(x[0], p0["g1"], ts)
    ready = after_all([started_d[3], h0] + [p[k] for p in (p0, p1) for k in ("wa", "wx", "wm")])
    big0 = gather_finish("0a", first, started_a, ready)
    hook, finish = rest_arrives("0b", started_b)
    sv0 = _layer_fwd_mix(x[0], big0, p0, ts, h0, hook)
    big0.update(finish(sv0["yb_pre"]))
    x_mid = _layer_fwd_out(sv0, big0, ts)
    big1 = gather_finish("1a", first, started_c, x_mid)
    hook, finish = rest_arrives("1b", started_d)
    sv1 = _layer_fwd_mix(x_mid, big1, p1, ts, None, hook)
    big1.update(finish(sv1["yb_pre"]))
    x_out = _layer_fwd_out(sv1, big1, ts)
    dx, loss, dgf = _loss_call(x_out, loss_target[0], final_norm_g.reshape(1, -1), ts)

    def pair_start(tag, gb, after):
        sends = [gb[k][1] for k in gb]
        zones = [lax.empty(a.shape, BF) for a in sends]
        return _exchange_start(f"pair_start_{tag}", sends + zones, _sibling_copies, len(sends), after)

    def reduce_start(tag, gb, after, pair=None):
        keys = tuple(gb)
        if pair is None:
            from_sibling = _sibling_send_call([gb[k][1] for k in keys])
        else:
            done = _exchange_wait(f"pair_wait_{tag}", pair[0], pair[1], pair[2], _sibling_copies, after)
            from_sibling = done[len(keys):]
        sums = [
            _ew_call(lambda a, b: (a + b.astype(F32),), "pair_sum", [(gb[k][0][None], (0, "g")), (r[None], (0, "g"))],
                     [(sds((1,) + r.shape, BF), (0, "g"))], N_QUARTERS)[0][0]
            for k, r in zip(keys, from_sibling)]
        zones = [lax.empty((3,) + a.shape[1:], BF) for a in sums]
        started = _exchange_start(f"reduce_start_{tag}", sums + zones, _owner_copies, 3 * len(keys), after)
        return keys, started

    def reduce_finish(tag, l, keys_started, after, reduced):
        keys, (send_sems, recv_sems, thru, _) = keys_started
        done = _exchange_wait(f"reduce_wait_{tag}", send_sems, recv_sems, thru, _owner_copies, after)
        sums, zones = done[:len(keys)], done[len(keys):]
        for i, k in enumerate(keys):
            r2, cols = halves[k]
            reduced[k] = _ew_call(
                lambda a, b, c, d: (((a.astype(F32) + b.astype(F32)) + c.astype(F32)) + d.astype(F32),),
                "quarter_sum", [(sums[i][None], (0, this_chip))] + [(zones[i][None], (0, j)) for j in range(3)],
                [(sds((DEPTH, 2, r2, cols), F32), (l, this_core))], 1, sel, into=reduced.get(k))[0]

    def behind(params, key, started):
        return dict(params, **{key: params[key] + started[1][3][0, 0]})

    dx1, gb_ffn, gs1 = _layer_bwd_ffn(dx, sv1, big1, ts)
    merge_out, gb_merge = _layer_bwd_merge(dx1, sv1, big1, ts)
    dx_mid, gb_in, gs1_mix = _layer_bwd_branches(dx1, merge_out, sv1, big1, lru_lambda[1], ts)
    gb_1 = {**gb_ffn, **gb_merge, **gb_in}
    pair_1 = pair_start("1", gb_1, dx_mid)
    sv0["p"] = behind(sv0["p"], "g2", (None, pair_1))
    dx1, gb_ffn, gs0 = _layer_bwd_ffn(dx_mid, sv0, big0, ts)
    exchange_1 = reduce_start("1", gb_1, dx1, pair_1)
    exchange_0a = reduce_start("0a", gb_ffn, exchange_1[1][3])
    merge_out, gb_merge = _layer_bwd_merge(dx1, sv0, big0, ts, exchange_0a[1][3])
    exchange_0b = reduce_start("0b", gb_merge, exchange_0a[1][3])
    sv0["p"] = behind(sv0["p"], "lg", exchange_0b)
    grad_x, gb_in, gs0_mix = _layer_bwd_branches(dx1, merge_out, sv0, big0, lru_lambda[0], ts)
    exchange_0c = reduce_start("0c", gb_in, exchange_0b[1][3])
    layer_gs = [{**gs0, **gs0_mix}, {**gs1, **gs1_mix}]
    gs = {k: jnp.stack([g[k] for g in layer_gs]) for k in layer_gs[0]}
    gs["final_norm_g"] = dgf[0]

    me = ("sel", 3)
    piece = (1, N_DEVICES, SMALL_ROWS, 128)
    packed = _pack_small(gs).reshape(piece)
    scatter = _exchange_start("small_scatter_start", [packed[0], lax.empty(piece[1:], F32)], _small_scatter_copies,
                              N_DEVICES - 1, exchange_0c[1][3])
    reduced = {}
    reduce_finish("1", 1, exchange_1, scatter[3], reduced)
    reduce_finish("0a", 0, exchange_0a, reduced["w_in"], reduced)
    reduce_finish("0b", 0, exchange_0b, reduced["w_down"], reduced)

    def swap_slabs(ref, c, i):
        layers = (1,) if BIG[i] == "w_in" else range(DEPTH)
        return [(ref.at[l, c], ref.at[l, 1 - c]) for l in layers]

    swapped = dict(zip(BIG, _sibling_inplace_call("grads_swap_halves", [reduced[k] for k in BIG], swap_slabs,
                                                  DEPTH * len(BIG) - 1)))

    def adamw_layers(k, grad, layer, into, after=None):
        if layer is None:
            views = [_as4(_as_rows(a)) for a in (w[k], grad, m[k], v[k])]
            idx = (0, 0)
        else:
            views = [a.reshape((1,) + w[k].shape) for a in (w[k], grad, m[k], v[k])]
            idx = (0, layer)
        return _ew_call(_adamw, "adamw_big", [(a, idx) for a in views], [(sds(views[0].shape, F32), idx)] * 3,
                        into=into, after=after)

    updated, last_update = {}, None
    for k in BIG:
        updated[k] = adamw_layers(k, swapped[k], 1 if k == "w_in" else None, None, last_update)
        last_update = updated[k][0]
    scattered = _exchange_wait("small_scatter_wait", scatter[0], scatter[1], scatter[2], _small_scatter_copies,
                               last_update)
    summed = _ew_call(
        lambda *parts: (functools.reduce(lambda a, b: a + b, parts),), "small_sum",
        [(scattered[0][None], (0, me))]
        + [(scattered[1][None], (0, lambda g, s, k=k: s[3] ^ k)) for k in range(1, N_DEVICES)],
        [(sds(piece, F32), (0, me))], 1, sel)[0]
    spread = _exchange_start("small_spread_start", [summed[0]], _small_spread_copies, N_DEVICES - 1, summed)
    reduced["w_in"] = swapped["w_in"]
    reduce_finish("0c", 0, exchange_0c, spread[3], reduced)
    last = _sibling_inplace_call("grads_swap_last", [reduced["w_in"]],
                                 lambda ref, c, i: [(ref.at[0, c], ref.at[0, 1 - c])], 1)[0]
    swapped["w_in"] = last
    updated["w_in"] = adamw_layers("w_in", last, 0, updated["w_in"])
    grads_big = {k: swapped[k].reshape(w[k].shape) for k in BIG}
    delta, new_m, new_v = ({k: updated[k][j].reshape(w[k].shape) for k in BIG} for j in range(3))
    gathered_small = _exchange_wait("small_spread_wait", spread[0], spread[1], spread[2], _small_spread_copies,
                                    updated["w_in"][0])[0]

    like = {k: jax.ShapeDtypeStruct(sm[k].shape, F32) for k in SMALL}
    grads_small = _unpack_small(gathered_small, like)
    conv_q = grads_small["conv_w"].reshape(DEPTH, CONV_WIDTH, N_QUARTERS, D_RNN // N_QUARTERS)
    grads_small["conv_w"] = lax.dynamic_index_in_dim(conv_q, chip, axis=2, keepdims=False)
    outs = _small_adamw_call(*[[_as_rows(d[k]) for k in SMALL] for d in (w, grads_small, m, v)])
    for d, o in zip((delta, new_m, new_v), outs):
        for k, a in zip(SMALL, o):
            d[k] = a.reshape(w[k].shape)

    grads = {**grads_big, **grads_small}
    total = lax.psum(loss[0, 0], ("x", "y", "c"))
    return (total, grad_x[None], *[grads[k] for k in WEIGHTS], *[delta[k] for k in WEIGHTS],
            *[new_m[k] for k in WEIGHTS], *[new_v[k] for k in WEIGHTS])
```

```python
import functools
import math

import jax
import jax.numpy as jnp
from jax import lax
from jax.experimental import pallas as pl
from jax.experimental.pallas import tpu as pltpu

F32 = jnp.float32
BF = jnp.bfloat16

DEPTH = 2
D_MODEL = 1024
D_RNN = 1280
D_SGU = 1024
D_FF = 4096
D_IN = 2 * D_RNN + 2 * D_SGU + 2 * D_MODEL
N_QUARTERS = 4
Q_IN = D_IN // N_QUARTERS
Q_FF = D_FF // N_QUARTERS
RNN_HEADS = 20
RNN_HEAD_DIM = 64
LRU_GROUP = 256
N_LRU_GROUPS = D_RNN // LRU_GROUP
HEADS_PER_GROUP = LRU_GROUP // RNN_HEAD_DIM
CONV_WIDTH = 4
LRU_C = 8.0
SGU_GROUPS = 8
SGU_BLOCK = 128
CHUNK = 64
EPS = 1e-6

ADAM_LR = 0.001
ADAM_B1 = 0.9
ADAM_B2 = 0.999
ADAM_EPS = 1e-08
ADAM_WD = 0.01
ADAM_STEP = 10

SUBLANES = 8
TOKEN_TILE = 512
VMEM_LIMIT_BYTES = 56 * 1024 * 1024

MESH = pl.DeviceIdType.MESH


def _params(semantics=None, vmem=True, **kw):
    return pltpu.CompilerParams(
        dimension_semantics=semantics,
        vmem_limit_bytes=VMEM_LIMIT_BYTES if vmem else None,
        **kw,
    )


def _dot(a, b):
    return jnp.dot(a, b, preferred_element_type=F32)


def _dot_nt(a, b):
    return lax.dot_general(a, b, (((1,), (1,)), ((), ())), preferred_element_type=F32)


def _dot_tn(a, b):
    return lax.dot_general(a, b, (((0,), (0,)), ((), ())), preferred_element_type=F32)


_GELU_C = math.sqrt(2.0 / math.pi)
_GELU_A = 0.044715


def _gelu(x):
    return 0.5 * x * (1.0 + jnp.tanh(_GELU_C * (x + _GELU_A * x * x * x)))


def _gelu_and_grad(x):
    x2 = x * x
    t = jnp.tanh(_GELU_C * (x + _GELU_A * x2 * x))
    du = _GELU_C * (1.0 + 3.0 * _GELU_A * x2)
    return 0.5 * x * (1.0 + t), 0.5 * (1.0 + t) + 0.5 * x * (1.0 - t * t) * du


def _rms_stats(x):
    return lax.rsqrt(jnp.mean(x * x, axis=-1, keepdims=True) + EPS)


def _rms_bwd(dy, x, g):
    rs = _rms_stats(x)
    n = x * rs
    dn = dy * g
    dx = rs * (dn - n * jnp.mean(dn * n, axis=-1, keepdims=True))
    return dx, dy * n


def _row_sum(x):
    return jnp.sum(x, axis=0, keepdims=True)


def _tile_spec(ts, width, col=0):
    return pl.BlockSpec((ts, width), lambda i, col=col: (i, col))


def _full_spec(shape):
    zeros = (0,) * len(shape)
    return pl.BlockSpec(shape, lambda *_: zeros)


def _layer_spec(w, layer):
    zeros = (0,) * (w.ndim - 1)
    return pl.BlockSpec((None,) + tuple(w.shape[1:]), lambda *_: (layer,) + zeros)


def _norm_call(x, g, ts):
    s = x.shape[0]

    def body(x_ref, g_ref, h_ref):
        xv = x_ref[...]
        h_ref[...] = (xv * _rms_stats(xv) * g_ref[...]).astype(BF)

    return pl.pallas_call(
        body, name="norm_fwd", grid=(s // ts,),
        in_specs=[_tile_spec(ts, D_MODEL), _full_spec((1, D_MODEL))],
        out_specs=_tile_spec(ts, D_MODEL),
        out_shape=jax.ShapeDtypeStruct((s, D_MODEL), BF),
        compiler_params=_params(("parallel",)),
    )(x, g)


def _inproj_call(h, w_in, layer, ts):
    s = h.shape[0]

    def body(h_ref, w_ref, o_ref):
        o_ref[...] = _dot(h_ref[...], w_ref[...]).astype(BF)

    return pl.pallas_call(
        body, name="inproj_fwd", grid=(N_QUARTERS, s // ts),
        in_specs=[
            pl.BlockSpec((ts, D_MODEL), lambda q, i: (i, 0)),
            pl.BlockSpec((None, None, D_MODEL, Q_IN), lambda q, i: (layer, q, 0, 0)),
        ],
        out_specs=pl.BlockSpec((ts, Q_IN), lambda q, i: (i, q)),
        out_shape=jax.ShapeDtypeStruct((s, D_IN), BF),
        compiler_params=_params(("parallel", "parallel")),
    )(h, w_in)


def _shift_down(x, tail, s):
    xr = pltpu.roll(x, s, 0)
    tr = pltpu.roll(tail, s, 0)
    row = lax.broadcasted_iota(jnp.int32, tail.shape, 0)
    top = jnp.where(row < s, tr, xr[0:SUBLANES])
    return jnp.concatenate([top, xr[SUBLANES:]], axis=0)


def _shift_up(x, head, s):
    t = x.shape[0]
    xr = pltpu.roll(x, t - s, 0)
    hr = pltpu.roll(head, SUBLANES - s, 0)
    row = lax.broadcasted_iota(jnp.int32, head.shape, 0)
    bottom = jnp.where(row >= SUBLANES - s, hr, xr[t - SUBLANES:])
    return jnp.concatenate([xr[: t - SUBLANES], bottom], axis=0)


def _conv_fwd(x, tail, cw_ref, cb_ref):
    out = cb_ref[...] + cw_ref[CONV_WIDTH - 1:CONV_WIDTH, :] * x
    for s in range(1, CONV_WIDTH):
        k = CONV_WIDTH - 1 - s
        out = out + cw_ref[k:k + 1, :] * _shift_down(x, tail, s)
    return out


def _group_dot(x_bf, w_ref, dot):
    cols = [dot(x_bf[:, g * LRU_GROUP:(g + 1) * LRU_GROUP], w_ref[g]) for g in range(N_LRU_GROUPS)]
    return jnp.concatenate(cols, axis=1)


def _lru_gates(xr, wa_ref, wx_ref, ba_ref, bx_ref, sp_ref):
    xb = xr.astype(BF)
    r = jax.nn.sigmoid(_group_dot(xb, wa_ref, _dot) + ba_ref[...])
    i = jax.nn.sigmoid(_group_dot(xb, wx_ref, _dot) + bx_ref[...])
    log_a = (-LRU_C * r) * sp_ref[...]
    a = jnp.exp(log_a)
    nrm2 = -jnp.tanh(log_a) * (a * a + 1.0)
    inv_nrm = lax.rsqrt(jnp.maximum(nrm2, 1e-36))
    return r, i, a, nrm2 * inv_nrm, inv_nrm


def _linear_scan(a, b, carry, al_ref, bl_ref, h_ref, reverse):
    t, c = a.shape
    rowm = lax.broadcasted_iota(jnp.int32, (t, c), 0) & (SUBLANES - 1)
    for d in (1, 2, 4):
        if reverse:
            keep, sh = rowm < SUBLANES - d, t - d
        else:
            keep, sh = rowm >= d, d
        a_sh = jnp.where(keep, pltpu.roll(a, sh, 0), 1.0)
        b_sh = jnp.where(keep, pltpu.roll(b, sh, 0), 0.0)
        b = a * b_sh + b
        a = a * a_sh
    al_ref[...] = a
    bl_ref[...] = b
    groups = t // SUBLANES

    def step(j, state):
        jj = groups - 1 - j if reverse else j
        off = pl.multiple_of(jj * SUBLANES, SUBLANES)
        rows = bl_ref[pl.ds(off, SUBLANES), :] + al_ref[pl.ds(off, SUBLANES), :] * state
        h_ref[pl.ds(off, SUBLANES), :] = rows
        last = rows[0:1, :] if reverse else rows[SUBLANES - 1:SUBLANES, :]
        return jnp.broadcast_to(last, (SUBLANES, c))

    out = lax.fori_loop(0, groups, step, jnp.broadcast_to(carry, (SUBLANES, c)))
    return out[0:1, :]


def _rnn_fwd_call(proj, wa, wx, ba, bx, sp, cw, cb, ts):
    s = proj.shape[0]

    def body(xg_ref, wa_ref, wx_ref, ba_ref, bx_ref, sp_ref, cw_ref, cb_ref, xr_ref, hr_ref, ya_ref,
             tail_sc, carry_sc, al_sc, bl_sc, h_sc):
        @pl.when(pl.program_id(0) == 0)
        def _():
            tail_sc[...] = jnp.zeros_like(tail_sc)
            carry_sc[...] = jnp.zeros_like(carry_sc)

        x = xg_ref[:, :D_RNN].astype(F32)
        g = xg_ref[:, D_RNN:].astype(F32)
        xr = _conv_fwd(x, tail_sc[...], cw_ref, cb_ref)
        tail_sc[...] = x[ts - SUBLANES:, :]
        xr_ref[...] = xr.astype(BF)
        _, i, a, nrm, _ = _lru_gates(xr, wa_ref, wx_ref, ba_ref, bx_ref, sp_ref)
        carry_sc[...] = _linear_scan(a, nrm * (i * xr), carry_sc[...], al_sc, bl_sc, h_sc, False)
        h = h_sc[...]
        hr_ref[...] = h.astype(BF)
        ya_ref[...] = (h * _gelu(g)).astype(BF)

    gw = (N_LRU_GROUPS, LRU_GROUP, LRU_GROUP)
    return pl.pallas_call(
        body, name="rnn_fwd", grid=(s // ts,),
        in_specs=[_tile_spec(ts, 2 * D_RNN), _full_spec(gw), _full_spec(gw),
                  _full_spec((1, D_RNN)), _full_spec((1, D_RNN)), _full_spec((1, D_RNN)),
                  _full_spec((CONV_WIDTH, D_RNN)), _full_spec((1, D_RNN))],
        out_specs=[_tile_spec(ts, D_RNN)] * 3,
        out_shape=[jax.ShapeDtypeStruct((s, D_RNN), BF)] * 3,
        scratch_shapes=[pltpu.VMEM((SUBLANES, D_RNN), F32), pltpu.VMEM((1, D_RNN), F32),
                        pltpu.VMEM((ts, D_RNN), F32), pltpu.VMEM((ts, D_RNN), F32),
                        pltpu.VMEM((ts, D_RNN), F32)],
        compiler_params=_params(("arbitrary",)),
    )(proj, wa, wx, ba, bx, sp, cw, cb)


def _layernorm_fwd(x):
    mu = jnp.mean(x, axis=-1, keepdims=True)
    xc = x - mu
    rstd = lax.rsqrt(jnp.mean(xc * xc, axis=-1, keepdims=True) + EPS)
    return xc * rstd, rstd


def _sgu_mix(vn_bf, wm_ref, bsb_ref, ts):
    rows = []
    for blk in range(ts // SGU_BLOCK):
        r0 = blk * SGU_BLOCK
        cols = [
            _dot(wm_ref[g], vn_bf[r0:r0 + SGU_BLOCK, g * SGU_BLOCK:(g + 1) * SGU_BLOCK]) + bsb_ref[g]
            for g in range(SGU_GROUPS)
        ]
        rows.append(jnp.concatenate(cols, axis=1))
    return jnp.concatenate(rows, axis=0)


def _sgu_fwd_call(proj, wm, bsb, lg, lb, ts):
    s = proj.shape[0]

    def body(uv_ref, wm_ref, bsb_ref, lg_ref, lb_ref, yb_ref):
        gu = _gelu(uv_ref[:, :D_SGU].astype(F32))
        gv = _gelu(uv_ref[:, D_SGU:2 * D_SGU].astype(F32))
        nh, _ = _layernorm_fwd(gv)
        vn = (nh * lg_ref[...] + lb_ref[...]).astype(BF)
        yb_ref[...] = (gu * _sgu_mix(vn, wm_ref, bsb_ref, ts)).astype(BF)

    sw = (SGU_GROUPS, SGU_BLOCK, SGU_BLOCK)
    return pl.pallas_call(
        body, name="sgu_fwd", grid=(s // ts,),
        in_specs=[_tile_spec(ts, 2 * D_RNN, 1), _full_spec(sw), _full_spec(sw),
                  _full_spec((1, D_SGU)), _full_spec((1, D_SGU))],
        out_specs=_tile_spec(ts, D_SGU),
        out_shape=jax.ShapeDtypeStruct((s, D_SGU), BF),
        compiler_params=_params(("parallel",)),
    )(proj, wm, bsb, lg, lb)


_GATE_COL0 = (2 * D_RNN + 2 * D_SGU) // 512


def _gate_specs(ts):
    return [_tile_spec(ts, 512, _GATE_COL0 + j) for j in range(4)]


def _merge_call(x, proj, ya_pre, yb_pre, w_ba, w_bb, w_out, g2, layer, ts):
    s = x.shape[0]

    def body(x_ref, ga0, ga1, gb0, gb1, ya_ref, yb_ref, wa_ref, wb_ref, wo_ref, g2_ref,
             x1_ref, yao_ref, ybo_ref, mg_ref, h2_ref):
        ya = _dot(ya_ref[...], wa_ref[...])
        yb = _dot(yb_ref[...], wb_ref[...])
        sa = jax.nn.sigmoid(jnp.concatenate([ga0[...], ga1[...]], axis=1).astype(F32))
        sb = jax.nn.sigmoid(jnp.concatenate([gb0[...], gb1[...]], axis=1).astype(F32))
        merged = (sa * ya + sb * yb).astype(BF)
        x1 = x_ref[...] + _dot(merged, wo_ref[...])
        x1_ref[...] = x1
        yao_ref[...] = ya.astype(BF)
        ybo_ref[...] = yb.astype(BF)
        mg_ref[...] = merged
        h2_ref[...] = (x1 * _rms_stats(x1) * g2_ref[...]).astype(BF)

    act = jax.ShapeDtypeStruct((s, D_MODEL), BF)
    return pl.pallas_call(
        body, name="merge_fwd", grid=(s // ts,),
        in_specs=[_tile_spec(ts, D_MODEL)] + _gate_specs(ts) + [
            _tile_spec(ts, D_RNN), _tile_spec(ts, D_SGU),
            _layer_spec(w_ba, layer), _layer_spec(w_bb, layer), _layer_spec(w_out, layer),
            _full_spec((1, D_MODEL))],
        out_specs=[_tile_spec(ts, D_MODEL)] * 5,
        out_shape=[jax.ShapeDtypeStruct((s, D_MODEL), F32), act, act, act, act],
        compiler_params=_params(("parallel",)),
    )(x, proj, proj, proj, proj, ya_pre, yb_pre, w_ba, w_bb, w_out, g2)


def _ffn_call(x1, h2, w_up, w_down, layer, ts):
    s = x1.shape[0]

    def body(x1_ref, h2_ref, wu_ref, wd_ref, x2_ref, p_ref):
        h2v = h2_ref[...]
        acc = x1_ref[...]
        for q in range(N_QUARTERS):
            p = _dot(h2v, wu_ref[q])
            p_ref[:, q * Q_FF:(q + 1) * Q_FF] = p.astype(BF)
            f = jnp.square(jnp.maximum(p, 0.0)).astype(BF)
            acc = acc + _dot(f, wd_ref[q * Q_FF:(q + 1) * Q_FF, :])
        x2_ref[...] = acc

    return pl.pallas_call(
        body, name="ffn_fwd", grid=(s // ts,),
        in_specs=[_tile_spec(ts, D_MODEL), _tile_spec(ts, D_MODEL),
                  pl.BlockSpec((None, N_QUARTERS, D_MODEL, Q_FF), lambda i: (layer, 0, 0, 0)),
                  pl.BlockSpec((None, D_FF, D_MODEL), lambda i: (layer, 0, 0))],
        out_specs=[_tile_spec(ts, D_MODEL), _tile_spec(ts, D_FF)],
        out_shape=[jax.ShapeDtypeStruct((s, D_MODEL), F32), jax.ShapeDtypeStruct((s, D_FF), BF)],
        compiler_params=_params(("parallel",)),
    )(x1, h2, w_up, w_down)


def _loss_call(x, target, gf, ts):
    s = x.shape[0]

    def body(x_ref, t_ref, g_ref, dx_ref, loss_ref, dg_ref):
        @pl.when(pl.program_id(0) == 0)
        def _():
            loss_ref[...] = jnp.zeros_like(loss_ref)
            dg_ref[...] = jnp.zeros_like(dg_ref)

        xv = x_ref[...]
        gv = g_ref[...]
        err = xv * _rms_stats(xv) * gv - t_ref[...]
        part = 0.5 * jnp.sum(jnp.mean(err * err, axis=-1, keepdims=True), axis=0, keepdims=True)
        loss_ref[...] += jnp.broadcast_to(part, loss_ref.shape)
        dx, dg = _rms_bwd(err * (1.0 / D_MODEL), xv, gv)
        dx_ref[...] = dx
        dg_ref[...] += _row_sum(dg)

    return pl.pallas_call(
        body, name="loss_head", grid=(s // ts,),
        in_specs=[_tile_spec(ts, D_MODEL), _tile_spec(ts, D_MODEL), _full_spec((1, D_MODEL))],
        out_specs=[_tile_spec(ts, D_MODEL), _full_spec((1, 128)), _full_spec((1, D_MODEL))],
        out_shape=[jax.ShapeDtypeStruct((s, D_MODEL), F32), jax.ShapeDtypeStruct((1, 128), F32),
                   jax.ShapeDtypeStruct((1, D_MODEL), F32)],
        compiler_params=_params(("arbitrary",)),
    )(x, target, gf)


def _ffn_bwd_call(dx2, p, x1, g2, w_up, w_down, layer, ts):
    s = dx2.shape[0]

    def body(dx2_ref, p_ref, x1_ref, g2_ref, wu_ref, wd_ref, dx1_ref, dp_ref, dg_ref):
        @pl.when(pl.program_id(0) == 0)
        def _():
            dg_ref[...] = jnp.zeros_like(dg_ref)

        dx2v = dx2_ref[...]
        dyb = dx2v.astype(BF)
        dh2 = jnp.zeros((ts, D_MODEL), F32)
        for q in range(N_QUARTERS):
            cols = slice(q * Q_FF, (q + 1) * Q_FF)
            df = _dot_nt(dyb, wd_ref[cols, :])
            dp = (df * (2.0 * jnp.maximum(p_ref[:, cols].astype(F32), 0.0))).astype(BF)
            dp_ref[:, cols] = dp
            dh2 = dh2 + _dot_nt(dp, wu_ref[q])
        dx, dg = _rms_bwd(dh2, x1_ref[...], g2_ref[...])
        dx1_ref[...] = dx2v + dx
        dg_ref[...] += _row_sum(dg)

    return pl.pallas_call(
        body, name="ffn_bwd", grid=(s // ts,),
        in_specs=[_tile_spec(ts, D_MODEL), _tile_spec(ts, D_FF), _tile_spec(ts, D_MODEL),
                  _full_spec((1, D_MODEL)),
                  pl.BlockSpec((None, N_QUARTERS, D_MODEL, Q_FF), lambda i: (layer, 0, 0, 0)),
                  pl.BlockSpec((None, D_FF, D_MODEL), lambda i: (layer, 0, 0))],
        out_specs=[_tile_spec(ts, D_MODEL), _tile_spec(ts, D_FF), _full_spec((1, D_MODEL))],
        out_shape=[jax.ShapeDtypeStruct((s, D_MODEL), F32), jax.ShapeDtypeStruct((s, D_FF), BF),
                   jax.ShapeDtypeStruct((1, D_MODEL), F32)],
        compiler_params=_params(("arbitrary",)),
    )(dx2, p, x1, g2, w_up, w_down)


def _merge_bwd_call(dx1, proj, ya, yb, w_ba, w_bb, w_out, layer, ts, after=None):
    s = dx1.shape[0]

    def body(dx1_ref, ga0, ga1, gb0, gb1, ya_ref, yb_ref, wa_ref, wb_ref, wo_ref, *rest):
        dya_ref, dyb_ref, dgate_ref, dyap_ref, dybp_ref = rest[-5:]
        dm = _dot_nt(dx1_ref[...].astype(BF), wo_ref[...])
        sa = jax.nn.sigmoid(jnp.concatenate([ga0[...], ga1[...]], axis=1).astype(F32))
        sb = jax.nn.sigmoid(jnp.concatenate([gb0[...], gb1[...]], axis=1).astype(F32))
        dya = (dm * sa).astype(BF)
        dyb = (dm * sb).astype(BF)
        dya_ref[...] = dya
        dyb_ref[...] = dyb
        dgate_ref[:, :D_MODEL] = (dm * ya_ref[...].astype(F32) * sa * (1.0 - sa)).astype(BF)
        dgate_ref[:, D_MODEL:] = (dm * yb_ref[...].astype(F32) * sb * (1.0 - sb)).astype(BF)
        dyap_ref[...] = _dot_nt(dya, wa_ref[...]).astype(BF)
        dybp_ref[...] = _dot_nt(dyb, wb_ref[...]).astype(BF)

    act = jax.ShapeDtypeStruct((s, D_MODEL), BF)
    return pl.pallas_call(
        body, name="merge_bwd", grid=(s // ts,),
        in_specs=[_tile_spec(ts, D_MODEL)] + _gate_specs(ts) + [
            _tile_spec(ts, D_MODEL), _tile_spec(ts, D_MODEL),
            _layer_spec(w_ba, layer), _layer_spec(w_bb, layer), _layer_spec(w_out, layer)]
        + ([] if after is None else [pl.BlockSpec(memory_space=pl.ANY)]),
        out_specs=[_tile_spec(ts, D_MODEL), _tile_spec(ts, D_MODEL), _tile_spec(ts, 2 * D_MODEL),
                   _tile_spec(ts, D_RNN), _tile_spec(ts, D_SGU)],
        out_shape=[act, act, jax.ShapeDtypeStruct((s, 2 * D_MODEL), BF),
                   jax.ShapeDtypeStruct((s, D_RNN), BF), jax.ShapeDtypeStruct((s, D_SGU), BF)],
        compiler_params=_params(("parallel",)),
    )(dx1, proj, proj, proj, proj, ya, yb, w_ba, w_bb, w_out, *([] if after is None else [after]))


def _sgu_bwd_call(dyb_pre, proj, wm, bsb, mask, lg, lb, ts):
    s = proj.shape[0]

    def body(dy_ref, uv_ref, wm_ref, bsb_ref, mask_ref, lg_ref, lb_ref,
             duv_ref, dws_ref, dbs_ref, dlg_ref, dlb_ref, dm_sc):
        step = pl.program_id(0)

        @pl.when(step == 0)
        def _():
            dws_ref[...] = jnp.zeros_like(dws_ref)
            dlg_ref[...] = jnp.zeros_like(dlg_ref)
            dlb_ref[...] = jnp.zeros_like(dlb_ref)
            dm_sc[...] = jnp.zeros_like(dm_sc)

        gu, dgu_du = _gelu_and_grad(uv_ref[:, :D_SGU].astype(F32))
        gv, dgv_dv = _gelu_and_grad(uv_ref[:, D_SGU:2 * D_SGU].astype(F32))
        nh, rstd = _layernorm_fwd(gv)
        lgv = lg_ref[...]
        vn = (nh * lgv + lb_ref[...]).astype(BF)
        dy = dy_ref[...].astype(F32)
        du = dy * _sgu_mix(vn, wm_ref, bsb_ref, ts) * dgu_du
        dmix = dy * gu
        dmix_bf = dmix.astype(BF)
        dm_acc = dm_sc[...]
        rows = []
        for blk in range(ts // SGU_BLOCK):
            r0 = blk * SGU_BLOCK
            dm_acc = dm_acc + dmix[r0:r0 + SGU_BLOCK, :]
            cols = []
            for g in range(SGU_GROUPS):
                c0 = g * SGU_BLOCK
                dmg = dmix_bf[r0:r0 + SGU_BLOCK, c0:c0 + SGU_BLOCK]
                cols.append(_dot_tn(wm_ref[g], dmg))
                dws_ref[g] += mask_ref[...] * _dot_nt(dmg, vn[r0:r0 + SGU_BLOCK, c0:c0 + SGU_BLOCK])
            rows.append(jnp.concatenate(cols, axis=1))
        dm_sc[...] = dm_acc
        dvn = jnp.concatenate(rows, axis=0)
        dlg_ref[...] += _row_sum(dvn * nh)
        dlb_ref[...] += _row_sum(dvn)
        dnh = dvn * lgv
        dgv = rstd * (dnh - jnp.mean(dnh, axis=-1, keepdims=True)
                      - nh * jnp.mean(dnh * nh, axis=-1, keepdims=True))
        duv_ref[:, :D_SGU] = du.astype(BF)
        duv_ref[:, D_SGU:] = (dgv * dgv_dv).astype(BF)

        @pl.when(step == pl.num_programs(0) - 1)
        def _():
            for g in range(SGU_GROUPS):
                dbs_ref[:, g:g + 1] = jnp.sum(
                    dm_acc[:, g * SGU_BLOCK:(g + 1) * SGU_BLOCK], axis=1, keepdims=True)

    sw = (SGU_GROUPS, SGU_BLOCK, SGU_BLOCK)
    return pl.pallas_call(
        body, name="sgu_bwd", grid=(s // ts,),
        in_specs=[_tile_spec(ts, D_SGU), _tile_spec(ts, 2 * D_RNN, 1), _full_spec(sw), _full_spec(sw),
                  _full_spec((SGU_BLOCK, SGU_BLOCK)), _full_spec((1, D_SGU)), _full_spec((1, D_SGU))],
        out_specs=[_tile_spec(ts, 2 * D_SGU), _full_spec(sw), _full_spec((SGU_BLOCK, SGU_GROUPS)),
                   _full_spec((1, D_SGU)), _full_spec((1, D_SGU))],
        out_shape=[jax.ShapeDtypeStruct((s, 2 * D_SGU), BF), jax.ShapeDtypeStruct(sw, F32),
                   jax.ShapeDtypeStruct((SGU_BLOCK, SGU_GROUPS), F32),
                   jax.ShapeDtypeStruct((1, D_SGU), F32), jax.ShapeDtypeStruct((1, D_SGU), F32)],
        scratch_shapes=[pltpu.VMEM((SGU_BLOCK, D_SGU), F32)],
        compiler_params=_params(("arbitrary",)),
    )(dyb_pre, proj, wm, bsb, mask, lg, lb)


_ROW_DBA, _ROW_DBX, _ROW_DSP, _ROW_DCB, _ROW_DCW = 0, 1, 2, 3, 4
_PREV_ROWS = 16


def _rnn_bwd_call(dya_pre, proj, xr_saved, hr, wa, wx, ba, bx, sp, cw, ts):
    s = proj.shape[0]
    nt = s // ts
    per = ts // _PREV_ROWS

    def tile(i):
        return nt - 1 - i

    def prev(i):
        return jnp.maximum(tile(i) * per - 1, 0)

    def body(dy_ref, xg_ref, xr_ref, hr_ref, hrp_ref, wa_ref, wx_ref, ba_ref, bx_ref, sp_ref,
             cw_ref, dxg_ref, dwa_ref, dwx_ref, vec_ref,
             lam_carry, a_first, dxr_head, al_sc, bl_sc, lam_sc):
        step = pl.program_id(0)

        @pl.when(step == 0)
        def _():
            dwa_ref[...] = jnp.zeros_like(dwa_ref)
            dwx_ref[...] = jnp.zeros_like(dwx_ref)
            vec_ref[...] = jnp.zeros_like(vec_ref)
            lam_carry[...] = jnp.zeros_like(lam_carry)
            a_first[...] = jnp.zeros_like(a_first)
            dxr_head[...] = jnp.zeros_like(dxr_head)

        has_prev = (step < nt - 1).astype(F32)
        x = xg_ref[:, :D_RNN].astype(F32)
        g = xg_ref[:, D_RNN:].astype(F32)
        h_tail = hrp_ref[_PREV_ROWS - SUBLANES:, :].astype(F32) * has_prev
        xr = xr_ref[...].astype(F32)
        r, i, a, nrm, inv_nrm = _lru_gates(xr, wa_ref, wx_ref, ba_ref, bx_ref, sp_ref)
        h = hr_ref[...].astype(F32)
        dy = dy_ref[...].astype(F32)
        gg, dgg = _gelu_and_grad(g)

        coef = _shift_up(a, jnp.broadcast_to(a_first[...], (SUBLANES, D_RNN)), 1)
        lam_carry[...] = _linear_scan(coef, dy * gg, lam_carry[...], al_sc, bl_sc, lam_sc, True)
        a_first[...] = a[0:1, :]
        lam = lam_sc[...]

        da = lam * _shift_down(h, h_tail, 1)
        dnrm = lam * (i * xr)
        di = lam * nrm * xr
        dlog_a = da * a - dnrm * (a * a) * inv_nrm
        spv = sp_ref[...]
        dza = (dlog_a * (-LRU_C * spv)) * (r * (1.0 - r))
        dzx = di * (i * (1.0 - i))
        vec_ref[_ROW_DSP:_ROW_DSP + 1, :] += _row_sum(dlog_a * (-LRU_C * r))
        vec_ref[_ROW_DBA:_ROW_DBA + 1, :] += _row_sum(dza)
        vec_ref[_ROW_DBX:_ROW_DBX + 1, :] += _row_sum(dzx)
        xb = xr.astype(BF)
        dza_bf = dza.astype(BF)
        dzx_bf = dzx.astype(BF)
        for grp in range(N_LRU_GROUPS):
            cols = slice(grp * LRU_GROUP, (grp + 1) * LRU_GROUP)
            dwa_ref[grp] += _dot_tn(xb[:, cols], dza_bf[:, cols])
            dwx_ref[grp] += _dot_tn(xb[:, cols], dzx_bf[:, cols])
        dxr = (lam * nrm * i + _group_dot(dza_bf, wa_ref, _dot_nt) + _group_dot(dzx_bf, wx_ref, _dot_nt))

        vec_ref[_ROW_DCB:_ROW_DCB + 1, :] += _row_sum(dxr)
        head = dxr_head[...]
        dx = cw_ref[CONV_WIDTH - 1:CONV_WIDTH, :] * dxr
        vec_ref[_ROW_DCW + 3:_ROW_DCW + 4, :] += _row_sum(dxr * x)
        for sft in range(1, CONV_WIDTH):
            k = CONV_WIDTH - 1 - sft
            ahead = _shift_up(dxr, head, sft)
            dx = dx + cw_ref[k:k + 1, :] * ahead
            vec_ref[_ROW_DCW + k:_ROW_DCW + k + 1, :] += _row_sum(ahead * x)
        dxr_head[...] = dxr[0:SUBLANES, :]
        dxg_ref[:, :D_RNN] = dx.astype(BF)
        dxg_ref[:, D_RNN:] = (dy * h * dgg).astype(BF)

    gw = (N_LRU_GROUPS, LRU_GROUP, LRU_GROUP)
    rev = lambda width: pl.BlockSpec((ts, width), lambda i: (tile(i), 0))
    return pl.pallas_call(
        body, name="rnn_bwd", grid=(nt,),
        in_specs=[rev(D_RNN), rev(2 * D_RNN), rev(D_RNN), rev(D_RNN),
                  pl.BlockSpec((_PREV_ROWS, D_RNN), lambda i: (prev(i), 0)),
                  _full_spec(gw), _full_spec(gw),
                  _full_spec((1, D_RNN)), _full_spec((1, D_RNN)), _full_spec((1, D_RNN)),
                  _full_spec((CONV_WIDTH, D_RNN))],
        out_specs=[rev(2 * D_RNN), _full_spec(gw), _full_spec(gw), _full_spec((SUBLANES, D_RNN))],
        out_shape=[jax.ShapeDtypeStruct((s, 2 * D_RNN), BF), jax.ShapeDtypeStruct(gw, F32),
                   jax.ShapeDtypeStruct(gw, F32), jax.ShapeDtypeStruct((SUBLANES, D_RNN), F32)],
        scratch_shapes=[pltpu.VMEM((1, D_RNN), F32), pltpu.VMEM((1, D_RNN), F32),
                        pltpu.VMEM((SUBLANES, D_RNN), F32),
                        pltpu.VMEM((ts, D_RNN), F32), pltpu.VMEM((ts, D_RNN), F32),
                        pltpu.VMEM((ts, D_RNN), F32)],
        compiler_params=_params(("arbitrary",)),
    )(dya_pre, proj, xr_saved, hr, hr, wa, wx, ba, bx, sp, cw)


def _inproj_bwd_call(dxg, duv, dgate, dx1, x, g1, w_in, layer, ts):
    s = x.shape[0]

    def body(dxg_ref, duv_ref, dgt_ref, dx1_ref, x_ref, g_ref, w_ref, dx_ref, dproj_ref, dg_ref):
        @pl.when(pl.program_id(0) == 0)
        def _():
            dg_ref[...] = jnp.zeros_like(dg_ref)

        dproj = jnp.concatenate([dxg_ref[...], duv_ref[...], dgt_ref[...]], axis=1)
        dproj_ref[...] = dproj
        dh = jnp.zeros((ts, D_MODEL), F32)
        for q in range(N_QUARTERS):
            dh = dh + _dot_nt(dproj[:, q * Q_IN:(q + 1) * Q_IN], w_ref[q])
        dx, dg = _rms_bwd(dh, x_ref[...], g_ref[...])
        dx_ref[...] = dx1_ref[...] + dx
        dg_ref[...] += _row_sum(dg)

    return pl.pallas_call(
        body, name="inproj_bwd", grid=(s // ts,),
        in_specs=[_tile_spec(ts, 2 * D_RNN), _tile_spec(ts, 2 * D_SGU), _tile_spec(ts, 2 * D_MODEL),
                  _tile_spec(ts, D_MODEL), _tile_spec(ts, D_MODEL), _full_spec((1, D_MODEL)),
                  pl.BlockSpec((None, N_QUARTERS, D_MODEL, Q_IN), lambda i: (layer, 0, 0, 0))],
        out_specs=[_tile_spec(ts, D_MODEL), _tile_spec(ts, D_IN), _full_spec((1, D_MODEL))],
        out_shape=[jax.ShapeDtypeStruct((s, D_MODEL), F32), jax.ShapeDtypeStruct((s, D_IN), BF),
                   jax.ShapeDtypeStruct((1, D_MODEL), F32)],
        compiler_params=_params(("arbitrary",)),
    )(dxg, duv, dgate, dx1, x, g1, w_in)


def _relu_sq(p):
    return jnp.square(jnp.maximum(p, 0))


def _wgrad_call(a, b, core, tm, tn, tk, col_blocked, name, a_fn=None):
    s, m = a.shape
    n = b.shape[1]
    r, cols = (m, n // N_QUARTERS) if col_blocked else (m // N_QUARTERS, n)
    r2 = r // 2
    per_tile = tm // r
    steps = s // tk

    def body(core_ref, a_ref, b_ref, keep_ref, send_ref, *acc):
        av = a_ref[...]
        if a_fn is not None:
            av = a_fn(av)
        prod = _dot_tn(av.astype(BF), b_ref[...].astype(BF))

        def emit(total):
            for h in range(2):
                @pl.when(core_ref[0] == h)
                def _():
                    for q in range(per_tile):
                        keep_ref[q] = total[q * r + h * r2:q * r + (h + 1) * r2]
                        send_ref[q] = total[q * r + (1 - h) * r2:q * r + (2 - h) * r2].astype(BF)

        if steps == 1:
            emit(prod)
        else:
            acc_ref, = acc
            step = pl.program_id(2)

            @pl.when(step == 0)
            def _():
                acc_ref[...] = prod

            @pl.when(jnp.logical_and(step > 0, step < steps - 1))
            def _():
                acc_ref[...] += prod

            @pl.when(step == steps - 1)
            def _():
                emit(acc_ref[...] + prod)

    if col_blocked:
        per_q = cols // tn
        out_spec = pl.BlockSpec((1, r2, tn), lambda i, j, k, c: (j // per_q, 0, j % per_q))
    else:
        out_spec = pl.BlockSpec((per_tile, r2, tn), lambda i, j, k, c: (i, 0, j))
    return pl.pallas_call(
        body, name=name,
        out_shape=[jax.ShapeDtypeStruct((N_QUARTERS, r2, cols), F32),
                   jax.ShapeDtypeStruct((N_QUARTERS, r2, cols), BF)],
        grid_spec=pltpu.PrefetchScalarGridSpec(
            num_scalar_prefetch=1, grid=(m // tm, n // tn, steps),
            in_specs=[pl.BlockSpec((tk, tm), lambda i, j, k, c: (k, i)),
                      pl.BlockSpec((tk, tn), lambda i, j, k, c: (k, j))],
            out_specs=[out_spec, out_spec],
            scratch_shapes=[] if steps == 1 else [pltpu.VMEM((tm, tn), F32)]),
        compiler_params=_params(("parallel", "parallel", "arbitrary")),
    )(core, a, b)


BIG = ("w_in", "w_up", "w_down", "w_branch_a", "w_branch_b", "w_out")


def _block_diag(w):
    w4 = w.reshape(N_LRU_GROUPS, HEADS_PER_GROUP, RNN_HEAD_DIM, RNN_HEAD_DIM)
    eye = jnp.eye(HEADS_PER_GROUP, dtype=w.dtype)
    return jnp.einsum("gjio,jk->gjiko", w4, eye).reshape(N_LRU_GROUPS, LRU_GROUP, LRU_GROUP)


def _block_diag_extract(d):
    d5 = d.reshape(N_LRU_GROUPS, HEADS_PER_GROUP, RNN_HEAD_DIM, HEADS_PER_GROUP, RNN_HEAD_DIM)
    blocks = [d5[:, j, :, j, :] for j in range(HEADS_PER_GROUP)]
    return jnp.stack(blocks, axis=1).reshape(RNN_HEADS, RNN_HEAD_DIM, RNN_HEAD_DIM)


def _sgu_mask():
    chunk = jnp.arange(SGU_BLOCK) // CHUNK
    return (chunk[:, None] >= chunk[None, :]).astype(F32)


def _layer_small(sm, l, core):
    row = lambda v: v.reshape(1, -1)
    return dict(
        core=core,
        g1=row(sm["norm_mix_g"][l]), g2=row(sm["norm_ffn_g"][l]),
        wa=_block_diag(sm["lru_w_a"][l]).astype(BF), wx=_block_diag(sm["lru_w_x"][l]).astype(BF),
        ba=row(sm["lru_b_a"][l]), bx=row(sm["lru_b_x"][l]),
        sp=row(jax.nn.softplus(-sm["lru_lambda"][l])),
        cw=sm["conv_w"][l], cb=row(sm["conv_b"][l]),
        wm=(sm["sgu_w_s"][l] * _sgu_mask()).astype(BF),
        bsb=jnp.broadcast_to(sm["sgu_b_s"][l][:, :, None], (SGU_GROUPS, SGU_BLOCK, SGU_BLOCK)),
        lg=row(sm["sgu_ln_g"][l]), lb=row(sm["sgu_ln_b"][l]),
    )


def _layer_fwd_mix(x, big, p, ts, h=None, before_sgu=None):
    if h is None:
        h = _norm_call(x, p["g1"], ts)
    proj = _inproj_call(h, big["w_in"], 0, 2 * ts)
    xr, hr, ya_pre = _rnn_fwd_call(proj, p["wa"], p["wx"], p["ba"], p["bx"], p["sp"], p["cw"], p["cb"], ts)
    lg = p["lg"] if before_sgu is None else p["lg"] + before_sgu(ya_pre)
    yb_pre = _sgu_fwd_call(proj, p["wm"], p["bsb"], lg, p["lb"], ts)
    return dict(p=p, x=x, h=h, proj=proj, xr=xr, hr=hr, ya_pre=ya_pre, yb_pre=yb_pre)


def _layer_fwd_out(sv, big, ts):
    x1, ya, yb, merged, h2 = _merge_call(sv["x"], sv["proj"], sv["ya_pre"], sv["yb_pre"], big["w_branch_a"],
                                         big["w_branch_b"], big["w_out"], sv["p"]["g2"], 0, ts)
    x2, pre = _ffn_call(x1, h2, big["w_up"], big["w_down"], 0, ts)
    sv.update(x1=x1, ya=ya, yb=yb, merged=merged, h2=h2, pre=pre)
    return x2


def _layer_bwd_ffn(dx, sv, big, ts):
    p = sv["p"]
    dx1, dpre, dg2 = _ffn_bwd_call(dx, sv["pre"], sv["x1"], p["g2"], big["w_up"], big["w_down"], 0, ts)
    tk = dx.shape[0]
    gb = dict(
        w_down=_wgrad_call(sv["pre"], dx, p["core"], Q_FF, D_MODEL // 2, tk, False, "wgrad_down", a_fn=_relu_sq),
        w_up=_wgrad_call(sv["h2"], dpre, p["core"], D_MODEL, Q_FF, tk, True, "wgrad_up"))
    return dx1, gb, dict(norm_ffn_g=dg2[0])


def _layer_bwd_merge(dx1, sv, big, ts, after=None):
    tk = dx1.shape[0]
    core = sv["p"]["core"]
    dya, dyb, dgate, dya_pre, dyb_pre = _merge_bwd_call(
        dx1, sv["proj"], sv["ya"], sv["yb"], big["w_branch_a"], big["w_branch_b"], big["w_out"], 0, ts, after)
    gb = dict(
        w_out=_wgrad_call(sv["merged"], dx1, core, D_MODEL, D_MODEL // 2, tk, False, "wgrad_out"),
        w_branch_a=_wgrad_call(sv["ya_pre"], dya, core, D_RNN, D_MODEL // 2, tk, False, "wgrad_branch_a"),
        w_branch_b=_wgrad_call(sv["yb_pre"], dyb, core, D_SGU, D_MODEL // 2, tk, False, "wgrad_branch_b"))
    return (dgate, dya_pre, dyb_pre), gb


def _layer_bwd_branches(dx1, merge_out, sv, big, lam, ts):
    p = sv["p"]
    tk = dx1.shape[0]
    dgate, dya_pre, dyb_pre = merge_out
    gb = {}
    duv, dws, dbs, dlg, dlb = _sgu_bwd_call(dyb_pre, sv["proj"], p["wm"], p["bsb"], _sgu_mask(), p["lg"], p["lb"],
                                            ts)
    dxg, dwa, dwx, vec = _rnn_bwd_call(dya_pre, sv["proj"], sv["xr"], sv["hr"], p["wa"], p["wx"], p["ba"], p["bx"],
                                       p["sp"], p["cw"], ts // 2)
    dx, dproj, dg1 = _inproj_bwd_call(dxg, duv, dgate, dx1, sv["x"], p["g1"], big["w_in"], 0, ts)
    gb["w_in"] = _wgrad_call(sv["h"], dproj, p["core"], D_MODEL, Q_IN, tk // 2, True, "wgrad_in")
    gs = dict(
        norm_mix_g=dg1[0], conv_w=vec[_ROW_DCW:_ROW_DCW + CONV_WIDTH], conv_b=vec[_ROW_DCB],
        lru_w_a=_block_diag_extract(dwa), lru_w_x=_block_diag_extract(dwx),
        lru_b_a=vec[_ROW_DBA].reshape(RNN_HEADS, RNN_HEAD_DIM), lru_b_x=vec[_ROW_DBX].reshape(RNN_HEADS, RNN_HEAD_DIM),
        lru_lambda=-vec[_ROW_DSP] * jax.nn.sigmoid(-lam),
        sgu_ln_g=dlg[0], sgu_ln_b=dlb[0], sgu_w_s=dws, sgu_b_s=dbs.T)
    return dx, gb, gs


def _local_step(x, target, big, sm, ts):
    saved = []
    core = jnp.zeros((1,), jnp.int32)
    for l in range(DEPTH):
        sv = _layer_fwd_mix(x, big[l], _layer_small(sm, l, core), ts)
        x = _layer_fwd_out(sv, big[l], ts)
        saved.append(sv)
    dx, loss, dgf = _loss_call(x, target, sm["final_norm_g"].reshape(1, -1), ts)
    gb, gs = [None] * DEPTH, [None] * DEPTH
    for l in reversed(range(DEPTH)):
        dx1, gb_ffn, gs_ffn = _layer_bwd_ffn(dx, saved[l], big[l], ts)
        merge_out, gb_merge = _layer_bwd_merge(dx1, saved[l], big[l], ts)
        dx, gb_mix, gs_mix = _layer_bwd_branches(dx1, merge_out, saved[l], big[l], sm["lru_lambda"][l], ts)
        gb[l] = {**gb_ffn, **gb_merge, **gb_mix}
        gs[l] = {**gs_ffn, **gs_mix}
    gs = {k: jnp.stack([g[k] for g in gs]) for k in gs[0]}
    gs["final_norm_g"] = dgf[0]
    return loss, dx, gb, gs


EW_VMEM_BYTES = 24 * 1024 * 1024


def _row_block(rows, cols, bytes_per_elem):
    for br in range(min(rows, EW_VMEM_BYTES // (2 * bytes_per_elem * cols)), 0, -1):
        if rows % br == 0 and br % 16 == 0:
            return br
    return rows


def _ew_call(fn, name, operands, outputs, slabs=1, sel=None, into=None, after=None):
    if into is not None and not isinstance(into, (list, tuple)):
        into = [into]
    rows, cols = outputs[0][0].shape[2:]
    br = _row_block(rows, cols, sum(jnp.dtype(a.dtype).itemsize for a, _ in operands + outputs))
    n_in = len(operands)

    def pick(tok, g, s):
        if callable(tok):
            return tok(g, s)
        if tok == "g":
            return g
        if isinstance(tok, tuple):
            return s[tok[1]]
        return tok

    def spec(idx):
        return pl.BlockSpec((None, None, br, cols),
                            lambda g, i, s, idx=idx: (pick(idx[0], g, s), pick(idx[1], g, s), i, 0))

    if sel is None:
        sel = jnp.zeros((1,), jnp.int32)
    in_specs = [spec(idx) for _, idx in operands]
    arrays = [a for a, _ in operands]
    aliases = {}
    for j, buf in enumerate(into or ()):
        in_specs.append(pl.BlockSpec(memory_space=pl.ANY))
        arrays.append(buf)
        aliases[1 + n_in + j] = j
    if after is not None:
        in_specs.append(pl.BlockSpec(memory_space=pl.ANY))
        arrays.append(after)

    def body(sel_ref, *refs):
        outs = fn(*[r[...] for r in refs[:n_in]])
        for o_ref, o in zip(refs[len(arrays):], outs):
            o_ref[...] = o.astype(o_ref.dtype)

    return pl.pallas_call(
        body, name=name, out_shape=[s for s, _ in outputs],
        grid_spec=pltpu.PrefetchScalarGridSpec(
            num_scalar_prefetch=1, grid=(slabs, rows // br),
            in_specs=in_specs,
            out_specs=[spec(idx) for _, idx in outputs]),
        input_output_aliases=aliases,
        compiler_params=_params(("parallel", "parallel")),
    )(sel, *arrays)


def _as4(a):
    return a.reshape((1,) * (4 - a.ndim) + a.shape)


def _adamw(w, g, m, v):
    m = ADAM_B1 * m + (1.0 - ADAM_B1) * g
    v = ADAM_B2 * v + (1.0 - ADAM_B2) * jnp.square(g)
    m_hat = m / (1.0 - ADAM_B1 ** ADAM_STEP)
    v_hat = v / (1.0 - ADAM_B2 ** ADAM_STEP)
    delta = -ADAM_LR * (m_hat / (jnp.sqrt(v_hat) + ADAM_EPS) + ADAM_WD * w)
    return delta, m, v


def _small_adamw_call(ws, gs, ms, vs):
    n = len(ws)

    def body(*refs):
        for k in range(n):
            w, g, m, v = (refs[j * n + k][...] for j in range(4))
            outs = _adamw(w, g, m, v)
            for j in range(3):
                refs[(4 + j) * n + k][...] = outs[j]

    shapes = [jax.ShapeDtypeStruct(w.shape, F32) for w in ws]
    outs = pl.pallas_call(
        body, name="adamw_small", out_shape=shapes * 3,
        in_specs=[pl.BlockSpec(memory_space=pltpu.VMEM)] * (4 * n),
        out_specs=[pl.BlockSpec(memory_space=pltpu.VMEM)] * (3 * n),
        compiler_params=_params(),
    )(*ws, *gs, *ms, *vs)
    return outs[:n], outs[n:2 * n], outs[2 * n:]


ANY = pl.BlockSpec(memory_space=pl.ANY)


def _place():
    x, y, c = lax.axis_index("x"), lax.axis_index("y"), lax.axis_index("c")
    chips = [(1 - x, y), (x, 1 - y), (1 - x, 1 - y)]
    return x, y, c, chips


def _remote(src, dst, send_sem, recv_sem, to):
    return pltpu.make_async_remote_copy(src_ref=src, dst_ref=dst, send_sem=send_sem, recv_sem=recv_sem,
                                        device_id=to, device_id_type=MESH)


def _gather_call(bufs):
    n = len(bufs)

    def body(*refs):
        out = refs[n:2 * n]
        send_sems, recv_sems = refs[2 * n:]
        x, y, c, chips = _place()
        me_q = 2 * x + y
        sibling = (x, y, 1 - c)
        first = []
        for w in range(n):
            for j, chip in enumerate(chips):
                mine = out[w].at[c, me_q]
                first.append(_remote(mine, mine, send_sems.at[w * 3 + j], recv_sems.at[w * 3 + j], (*chip, c)))
        for cp in first:
            cp.start()
        passed = []
        for w in range(n):
            for j, (qx, qy) in enumerate(chips):
                landed = out[w].at[c, 2 * qx + qy]
                k = w * 3 + j
                _remote(landed, landed, send_sems.at[k], recv_sems.at[k], (qx, qy, c)).wait_recv()
                cp = _remote(landed, landed, send_sems.at[3 * n + k], recv_sems.at[3 * n + k], sibling)
                cp.start()
                passed.append(cp)
        for w in range(n):
            for j, (qx, qy) in enumerate(chips):
                landed = out[w].at[1 - c, 2 * qx + qy]
                k = 3 * n + w * 3 + j
                _remote(landed, landed, send_sems.at[k], recv_sems.at[k], sibling).wait_recv()
        for cp in first + passed:
            cp.wait_send()

    return pl.pallas_call(
        body, name="gather_weights",
        out_shape=[jax.ShapeDtypeStruct(a.shape, a.dtype) for a in bufs],
        in_specs=[ANY] * n, out_specs=[ANY] * n,
        input_output_aliases={w: w for w in range(n)},
        scratch_shapes=[pltpu.SemaphoreType.DMA((6 * n,)), pltpu.SemaphoreType.DMA((6 * n,))],
        compiler_params=_params(vmem=False, has_side_effects=True),
    )(*bufs)


def _sibling_send_call(items):
    n = len(items)

    def body(*refs):
        src, out = refs[:n], refs[n:2 * n]
        send_sems, recv_sems = refs[2 * n:]
        x, y, c, _ = _place()
        copies = [_remote(src[w], out[w], send_sems.at[w], recv_sems.at[w], (x, y, 1 - c)) for w in range(n)]
        for cp in copies:
            cp.start()
        for cp in copies:
            cp.wait()

    return pl.pallas_call(
        body, name="grads_to_sibling",
        out_shape=[jax.ShapeDtypeStruct(a.shape, a.dtype) for a in items],
        in_specs=[ANY] * n, out_specs=[ANY] * n,
        scratch_shapes=[pltpu.SemaphoreType.DMA((n,)), pltpu.SemaphoreType.DMA((n,))],
        compiler_params=_params(vmem=False, has_side_effects=True),
    )(*items)


def _sibling_inplace_call(name, bufs, slabs, n_pairs):
    n = len(bufs)

    def body(*refs):
        out = refs[n:2 * n]
        send_sems, recv_sems = refs[2 * n:]
        x, y, c, _ = _place()
        sibling = (x, y, 1 - c)
        pairs = [pair for w, ref in enumerate(out) for pair in slabs(ref, c, w)]
        sends = [_remote(s, s, send_sems.at[k], recv_sems.at[k], sibling) for k, (s, _) in enumerate(pairs)]
        for cp in sends:
            cp.start()
        for k, (_, r) in enumerate(pairs):
            _remote(r, r, send_sems.at[k], recv_sems.at[k], sibling).wait_recv()
        for cp in sends:
            cp.wait_send()

    return pl.pallas_call(
        body, name=name,
        out_shape=[jax.ShapeDtypeStruct(a.shape, a.dtype) for a in bufs],
        in_specs=[ANY] * n, out_specs=[ANY] * n,
        input_output_aliases={w: w for w in range(n)},
        scratch_shapes=[pltpu.SemaphoreType.DMA((n_pairs,)), pltpu.SemaphoreType.DMA((n_pairs,))],
        compiler_params=_params(vmem=False, has_side_effects=True),
    )(*bufs)


HBM_SPEC = pl.BlockSpec(memory_space=pltpu.HBM)
SEM_SPEC = pl.BlockSpec(memory_space=pltpu.SEMAPHORE)
DATAFLOW_EFFECT = pltpu.SideEffectType.DATAFLOW_SIDE_EFFECTING


def _exchange_start(name, bufs, copies, n_copies, after):
    n = len(bufs)

    def body(*refs):
        ins, send_sems, recv_sems, token = refs[:n], refs[n + 1], refs[n + 2], refs[-1]
        for k, (src, dst, to) in enumerate(copies(ins)):
            _remote(src, dst, send_sems.at[k], recv_sems.at[k], to).start()
        token[...] = jnp.zeros_like(token)

    outs = pl.pallas_call(
        body, name=name,
        out_shape=(pltpu.SemaphoreType.DMA((n_copies,)), pltpu.SemaphoreType.DMA((n_copies,)),
                   *[pltpu.HBM(b.shape, b.dtype) for b in bufs], jax.ShapeDtypeStruct((SUBLANES, 128), F32)),
        in_specs=[HBM_SPEC] * n + [ANY],
        out_specs=(SEM_SPEC, SEM_SPEC, *[HBM_SPEC] * n, pl.BlockSpec(memory_space=pltpu.VMEM)),
        input_output_aliases={w: w + 2 for w in range(n)},
        compiler_params=pltpu.CompilerParams(has_side_effects=DATAFLOW_EFFECT),
    )(*[pltpu.with_memory_space_constraint(b, pltpu.HBM) for b in bufs], after)
    return outs[0], outs[1], list(outs[2:2 + n]), outs[-1]


def _exchange_wait(name, send_sems, recv_sems, bufs, copies, after):
    n = len(bufs)

    def body(*refs):
        ins, send_sems, recv_sems = refs[:n], refs[n], refs[n + 1]
        for k, (src, dst, to) in enumerate(copies(ins)):
            cp = _remote(src, dst, send_sems.at[k], recv_sems.at[k], to)
            cp.wait_send()
            cp.wait_recv()

    return pl.pallas_call(
        body, name=name,
        out_shape=[pltpu.HBM(b.shape, b.dtype) for b in bufs],
        in_specs=[HBM_SPEC] * n + [SEM_SPEC, SEM_SPEC, ANY],
        out_specs=[HBM_SPEC] * n,
        input_output_aliases={w: w for w in range(n)},
        compiler_params=pltpu.CompilerParams(has_side_effects=DATAFLOW_EFFECT),
    )(*bufs, send_sems, recv_sems, after)


def _gather_copies(refs):
    x, y, c, chips = _place()
    mine = 2 * (2 * x + y) + c
    return [(ref.at[mine], ref.at[mine], (qx, qy, c)) for ref in refs for qx, qy in chips]


def _forward_copies(refs):
    x, y, c, chips = _place()
    return [(ref.at[2 * (2 * qx + qy) + c], ref.at[2 * (2 * qx + qy) + c], (x, y, 1 - c))
            for ref in refs for qx, qy in chips]


def _gather_forward_slabs(ref, c, w):
    x, y, _, chips = _place()
    return [(ref.at[2 * (2 * qx + qy) + c], ref.at[2 * (2 * qx + qy) + 1 - c]) for qx, qy in chips]


def _device_peers():
    x, y, c, _ = _place()
    return 4 * x + 2 * y + c, [(k, (x ^ ((k >> 2) & 1), y ^ ((k >> 1) & 1), c ^ (k & 1))) for k in range(1, 8)]


def _small_scatter_copies(refs):
    me, peers = _device_peers()
    return [(refs[0].at[me ^ k], refs[1].at[me], to) for k, to in peers]


def _small_spread_copies(refs):
    me, peers = _device_peers()
    return [(refs[0].at[me], refs[0].at[me], to) for _, to in peers]


def _sibling_copies(refs):
    n = len(refs) // 2
    x, y, c, _ = _place()
    return [(refs[w], refs[n + w], (x, y, 1 - c)) for w in range(n)]


def _owner_copies(refs):
    n = len(refs) // 2
    x, y, c, chips = _place()
    return [(refs[w].at[2 * qx + qy], refs[n + w].at[j], (qx, qy, c))
            for w in range(n) for j, (qx, qy) in enumerate(chips)]


N_DEVICES = 8
SMALL_ROWS = 616


SMALL = ("norm_mix_g", "conv_w", "conv_b", "lru_w_a", "lru_b_a", "lru_w_x", "lru_b_x", "lru_lambda",
         "sgu_ln_g", "sgu_ln_b", "sgu_w_s", "sgu_b_s", "norm_ffn_g", "final_norm_g")
WEIGHTS = ("norm_mix_g", "w_in", "conv_w", "conv_b", "lru_w_a", "lru_b_a", "lru_w_x", "lru_b_x", "lru_lambda",
           "sgu_ln_g", "sgu_ln_b", "sgu_w_s", "sgu_b_s", "w_branch_a", "w_branch_b", "w_out", "norm_ffn_g",
           "w_up", "w_down", "final_norm_g")
PACK_ALIGN = SUBLANES * 128


PACKED = SMALL + ("loss",)


def _pack_small(gs):
    parts = []
    for k in PACKED:
        flat = gs[k].reshape(-1)
        parts.append(jnp.pad(flat, (0, -flat.size % PACK_ALIGN)))
    flat = jnp.concatenate(parts)
    flat = jnp.pad(flat, (0, N_DEVICES * SMALL_ROWS * 128 - flat.size))
    return flat.reshape(N_DEVICES, SMALL_ROWS, 128)


def _unpack_small(buf, like):
    flat = buf.reshape(-1)
    out, off = {}, 0
    for k in PACKED:
        size = like[k].size
        out[k] = flat[off:off + size].reshape(like[k].shape)
        off += size + (-size % PACK_ALIGN)
    return out


def _as_rows(a):
    return a.reshape(-1, a.shape[-1])


def kernel(x, norm_mix_g, w_in, conv_w, conv_b, lru_w_a, lru_b_a, lru_w_x, lru_b_x, lru_lambda, sgu_ln_g, sgu_ln_b, sgu_w_s, sgu_b_s, w_branch_a, w_branch_b, w_out, norm_ffn_g, w_up, w_down, final_norm_g, loss_target, m_norm_mix_g, m_w_in, m_conv_w, m_conv_b, m_lru_w_a, m_lru_b_a, m_lru_w_x, m_lru_b_x, m_lru_lambda, m_sgu_ln_g, m_sgu_ln_b, m_sgu_w_s, m_sgu_b_s, m_w_branch_a, m_w_branch_b, m_w_out, m_norm_ffn_g, m_w_up, m_w_down, m_final_norm_g, v_norm_mix_g, v_w_in, v_conv_w, v_conv_b, v_lru_w_a, v_lru_b_a, v_lru_w_x, v_lru_b_x, v_lru_lambda, v_sgu_ln_g, v_sgu_ln_b, v_sgu_w_s, v_sgu_b_s, v_w_branch_a, v_w_branch_b, v_w_out, v_norm_ffn_g, v_w_up, v_w_down, v_final_norm_g):
    w = dict(norm_mix_g=norm_mix_g, w_in=w_in, conv_w=conv_w, conv_b=conv_b, lru_w_a=lru_w_a, lru_b_a=lru_b_a,
             lru_w_x=lru_w_x, lru_b_x=lru_b_x, lru_lambda=lru_lambda, sgu_ln_g=sgu_ln_g, sgu_ln_b=sgu_ln_b,
             sgu_w_s=sgu_w_s, sgu_b_s=sgu_b_s, w_branch_a=w_branch_a, w_branch_b=w_branch_b, w_out=w_out,
             norm_ffn_g=norm_ffn_g, w_up=w_up, w_down=w_down, final_norm_g=final_norm_g)
    m = dict(norm_mix_g=m_norm_mix_g, w_in=m_w_in, conv_w=m_conv_w, conv_b=m_conv_b, lru_w_a=m_lru_w_a,
             lru_b_a=m_lru_b_a, lru_w_x=m_lru_w_x, lru_b_x=m_lru_b_x, lru_lambda=m_lru_lambda,
             sgu_ln_g=m_sgu_ln_g, sgu_ln_b=m_sgu_ln_b, sgu_w_s=m_sgu_w_s, sgu_b_s=m_sgu_b_s,
             w_branch_a=m_w_branch_a, w_branch_b=m_w_branch_b, w_out=m_w_out, norm_ffn_g=m_norm_ffn_g,
             w_up=m_w_up, w_down=m_w_down, final_norm_g=m_final_norm_g)
    v = dict(norm_mix_g=v_norm_mix_g, w_in=v_w_in, conv_w=v_conv_w, conv_b=v_conv_b, lru_w_a=v_lru_w_a,
             lru_b_a=v_lru_b_a, lru_w_x=v_lru_w_x, lru_b_x=v_lru_b_x, lru_lambda=v_lru_lambda,
             sgu_ln_g=v_sgu_ln_g, sgu_ln_b=v_sgu_ln_b, sgu_w_s=v_sgu_w_s, sgu_b_s=v_sgu_b_s,
             w_branch_a=v_w_branch_a, w_branch_b=v_w_branch_b, w_out=v_w_out, norm_ffn_g=v_norm_ffn_g,
             w_up=v_w_up, w_down=v_w_down, final_norm_g=v_final_norm_g)
    core = lax.axis_index("c")
    chip = 2 * lax.axis_index("x") + lax.axis_index("y")
    sel = jnp.stack([core, 1 - core, chip, 2 * chip + core]).astype(jnp.int32)
    this_core, other_core, this_chip = ("sel", 0), ("sel", 1), ("sel", 2)
    sds = jax.ShapeDtypeStruct

    ts = TOKEN_TILE

    def after_all(arrays):
        return jnp.stack([a[(0,) * a.ndim].astype(F32) for a in arrays])

    halves ={k: (w[k].shape[1] // 2, w[k].shape[2]) for k in BIG}

    def half_view(k, a):
        return a.reshape((2 * N_QUARTERS,) + halves[k])

    def full_view(k, a):
        r2, cols = halves[k]
        if k in ("w_in", "w_up"):
            return a.reshape(1, N_QUARTERS, 2 * r2, cols)
        return a.reshape(1, 2 * N_QUARTERS * r2, cols)

    layer_bufs = [{}, {}]

    def cast_weights(k, after):
        _, r, cols = w[k].shape
        w4 = w[k].reshape(DEPTH, 1, r, cols)
        outs = _ew_call(lambda a, b: (a, b), "cast_weights", [(w4, (0, 0)), (w4, (1, 0))],
                        [(sds((1, N_QUARTERS, r, cols), BF), (0, this_chip))] * DEPTH, 1, sel, after=after)
        for l in range(DEPTH):
            layer_bufs[l][k] = half_view(k, outs[l])

    conv_buf = lax.dynamic_update_slice_in_dim(
        jnp.zeros((DEPTH, N_QUARTERS) + conv_w.shape[1:], F32), conv_w[:, None], chip, axis=1)
    sm = {k: w[k] for k in SMALL}
    sm["conv_w"] = _gather_call([conv_buf])[0].transpose(0, 2, 1, 3).reshape(DEPTH, CONV_WIDTH, D_RNN)

    def gather_start(tag, l, keys, after):
        bufs = [layer_bufs[l][k] for k in keys]
        return _exchange_start(f"gather_start_{tag}", bufs, _gather_copies, 3 * len(keys), after)

    def gather_finish(tag, keys, started, after):
        send_sems, recv_sems, thru, _ = started
        landed = _exchange_wait(f"gather_wait_{tag}", send_sems, recv_sems, thru, _gather_copies, after)
        landed = _sibling_inplace_call("gather_forward", landed, _gather_forward_slabs, 3 * len(keys))
        return {k: full_view(k, a) for k, a in zip(keys, landed)}

    first, rest = ("w_in",), tuple(k for k in BIG if k != "w_in")
    cast_weights("w_in", None)
    started_a = gather_start("0a", 0, first, sm["conv_w"])
    for k in rest:
        cast_weights(k, started_a[3])
    started_b = gather_start("0b", 0, rest, started_a[3])
    started_c = gather_start("1a", 1, first, started_b[3])
    started_d = gather_start("1b", 1, rest, started_c[3])

    def rest_arrives(tag, started):
        state = {}

        def hook(after):
            landed = _exchange_wait(f"gather_wait_{tag}", started[0], started[1], started[2], _gather_copies, after)
            state["forward"] = _exchange_start(f"forward_start_{tag}", landed, _forward_copies, 3 * len(rest), after)
            return state["forward"][3][0, 0]

        def finish(after):
            send_sems, recv_sems, thru, _ = state["forward"]
            done = _exchange_wait(f"forward_wait_{tag}", send_sems, recv_sems, thru, _forward_copies, after)
            return {k: full_view(k, a) for k, a in zip(rest, done)}

        return hook, finish

    p0, p1 = _layer_small(sm, 0, sel[0:1]), _layer_small(sm, 1, sel[0:1])
    h0 = _norm_call(x[0], p0["g1"], ts)
    ready = after_all([started_d[3], h0] + [p[k] for p in (p0, p1) for k in ("wa", "wx", "wm")])
    big0 = gather_finish("0a", first, started_a, ready)
    hook, finish = rest_arrives("0b", started_b)
    sv0 = _layer_fwd_mix(x[0], big0, p0, ts, h0, hook)
    big0.update(finish(sv0["yb_pre"]))
    x_mid = _layer_fwd_out(sv0, big0, ts)
    big1 = gather_finish("1a", first, started_c, x_mid)
    hook, finish = rest_arrives("1b", started_d)
    sv1 = _layer_fwd_mix(x_mid, big1, p1, ts, None, hook)
    big1.update(finish(sv1["yb_pre"]))
    x_out = _layer_fwd_out(sv1, big1, ts)
    dx, loss, dgf = _loss_call(x_out, loss_target[0], final_norm_g.reshape(1, -1), ts)

    def pair_start(tag, gb, after):
        sends = [gb[k][1] for k in gb]
        zones = [lax.empty(a.shape, BF) for a in sends]
        return _exchange_start(f"pair_start_{tag}", sends + zones, _sibling_copies, len(sends), after)

    def reduce_start(tag, gb, after, pair=None):
        keys = tuple(gb)
        if pair is None:
            from_sibling = _sibling_send_call([gb[k][1] for k in keys])
        else:
            done = _exchange_wait(f"pair_wait_{tag}", pair[0], pair[1], pair[2], _sibling_copies, after)
            from_sibling = done[len(keys):]
        sums = [
            _ew_call(lambda a, b: (a + b.astype(F32),), "pair_sum", [(gb[k][0][None], (0, "g")), (r[None], (0, "g"))],
                     [(sds((1,) + r.shape, BF), (0, "g"))], N_QUARTERS)[0][0]
            for k, r in zip(keys, from_sibling)]
        zones = [lax.empty((3,) + a.shape[1:], BF) for a in sums]
        started = _exchange_start(f"reduce_start_{tag}", sums + zones, _owner_copies, 3 * len(keys), after)
        return keys, started

    def reduce_finish(tag, l, keys_started, after, reduced):
        keys, (send_sems, recv_sems, thru, _) = keys_started
        done = _exchange_wait(f"reduce_wait_{tag}", send_sems, recv_sems, thru, _owner_copies, after)
        sums, zones = done[:len(keys)], done[len(keys):]
        for i, k in enumerate(keys):
            r2, cols = halves[k]
            reduced[k] = _ew_call(
                lambda a, b, c, d: (((a.astype(F32) + b.astype(F32)) + c.astype(F32)) + d.astype(F32),),
                "quarter_sum", [(sums[i][None], (0, this_chip))] + [(zones[i][None], (0, j)) for j in range(3)],
                [(sds((DEPTH, 2, r2, cols), F32), (l, this_core))], 1, sel, into=reduced.get(k))[0]

    def behind(params, key, started):
        return dict(params, **{key: params[key] + started[1][3][0, 0]})

    dx1, gb_ffn, gs1 = _layer_bwd_ffn(dx, sv1, big1, ts)
    merge_out, gb_merge = _layer_bwd_merge(dx1, sv1, big1, ts)
    dx_mid, gb_in, gs1_mix = _layer_bwd_branches(dx1, merge_out, sv1, big1, lru_lambda[1], ts)
    gb_1 = {**gb_ffn, **gb_merge, **gb_in}
    pair_1 = pair_start("1", gb_1, dx_mid)
    sv0["p"] = behind(sv0["p"], "g2", (None, pair_1))
    dx1, gb_ffn, gs0 = _layer_bwd_ffn(dx_mid, sv0, big0, ts)
    exchange_1 = reduce_start("1", gb_1, dx1, pair_1)
    exchange_0a = reduce_start("0a", gb_ffn, exchange_1[1][3])
    merge_out, gb_merge = _layer_bwd_merge(dx1, sv0, big0, ts, exchange_0a[1][3])
    exchange_0b = reduce_start("0b", gb_merge, exchange_0a[1][3])
    sv0["p"] = behind(sv0["p"], "lg", exchange_0b)
    grad_x, gb_in, gs0_mix = _layer_bwd_branches(dx1, merge_out, sv0, big0, lru_lambda[0], ts)
    exchange_0c = reduce_start("0c", gb_in, exchange_0b[1][3])
    layer_gs = [{**gs0, **gs0_mix}, {**gs1, **gs1_mix}]
    gs = {k: jnp.stack([g[k] for g in layer_gs]) for k in layer_gs[0]}
    gs["final_norm_g"] = dgf[0]
    gs["loss"] = loss[0, 0:1]

    me = ("sel", 3)
    piece = (1, N_DEVICES, SMALL_ROWS, 128)
    packed = _pack_small(gs).reshape(piece)
    scatter = _exchange_start("small_scatter_start", [packed[0], lax.empty(piece[1:], F32)], _small_scatter_copies,
                              N_DEVICES - 1, exchange_0c[1][3])
    reduced = {}
    reduce_finish("1", 1, exchange_1, scatter[3], reduced)
    reduce_finish("0a", 0, exchange_0a, reduced["w_in"], reduced)
    reduce_finish("0b", 0, exchange_0b, reduced["w_down"], reduced)

    def swap_slabs(ref, c, i):
        layers = (1,) if BIG[i] == "w_in" else range(DEPTH)
        return [(ref.at[l, c], ref.at[l, 1 - c]) for l in layers]

    swapped = dict(zip(BIG, _sibling_inplace_call("grads_swap_halves", [reduced[k] for k in BIG], swap_slabs,
                                                  DEPTH * len(BIG) - 1)))

    def adamw_layers(k, grad, layer, into, after=None):
        if layer is None:
            views = [_as4(_as_rows(a)) for a in (w[k], grad, m[k], v[k])]
            idx = (0, 0)
        else:
            views = [a.reshape((1,) + w[k].shape) for a in (w[k], grad, m[k], v[k])]
            idx = (0, layer)
        return _ew_call(_adamw, "adamw_big", [(a, idx) for a in views], [(sds(views[0].shape, F32), idx)] * 3,
                        into=into, after=after)

    updated, last_update = {}, None
    for k in BIG:
        updated[k] = adamw_layers(k, swapped[k], 1 if k == "w_in" else None, None, last_update)
        last_update = updated[k][0]
    scattered = _exchange_wait("small_scatter_wait", scatter[0], scatter[1], scatter[2], _small_scatter_copies,
                               last_update)
    summed = _ew_call(
        lambda *parts: (functools.reduce(lambda a, b: a + b, parts),), "small_sum",
        [(scattered[0][None], (0, me))]
        + [(scattered[1][None], (0, lambda g, s, k=k: s[3] ^ k)) for k in range(1, N_DEVICES)],
        [(sds(piece, F32), (0, me))], 1, sel)[0]
    spread = _exchange_start("small_spread_start", [summed[0]], _small_spread_copies, N_DEVICES - 1, summed)
    reduced["w_in"] = swapped["w_in"]
    reduce_finish("0c", 0, exchange_0c, spread[3], reduced)
    last = _sibling_inplace_call("grads_swap_last", [reduced["w_in"]],
                                 lambda ref, c, i: [(ref.at[0, c], ref.at[0, 1 - c])], 1)[0]
    swapped["w_in"] = last
    updated["w_in"] = adamw_layers("w_in", last, 0, updated["w_in"])
    grads_big = {k: swapped[k].reshape(w[k].shape) for k in BIG}
    delta, new_m, new_v = ({k: updated[k][j].reshape(w[k].shape) for k in BIG} for j in range(3))
    gathered_small = _exchange_wait("small_spread_wait", spread[0], spread[1], spread[2], _small_spread_copies,
                                    updated["w_in"][0])[0]

    like = {k: jax.ShapeDtypeStruct(sm[k].shape, F32) for k in SMALL}
    like["loss"] = jax.ShapeDtypeStruct((1,), F32)
    grads_small = _unpack_small(gathered_small, like)
    total = grads_small.pop("loss")[0]
    conv_q = grads_small["conv_w"].reshape(DEPTH, CONV_WIDTH, N_QUARTERS, D_RNN // N_QUARTERS)
    grads_small["conv_w"] = lax.dynamic_index_in_dim(conv_q, chip, axis=2, keepdims=False)
    outs = _small_adamw_call(*[[_as_rows(d[k]) for k in SMALL] for d in (w, grads_small, m, v)])
    for d, o in zip((delta, new_m, new_v), outs):
        for k, a in zip(SMALL, o):
            d[k] = a.reshape(w[k].shape)

    grads = {**grads_big, **grads_small}
    return (total, grad_x[None], *[grads[k] for k in WEIGHTS], *[delta[k] for k in WEIGHTS],
            *[new_m[k] for k in WEIGHTS], *[new_v[k] for k in WEIGHTS])
```

```python
import functools
import math

import jax
import jax.numpy as jnp
from jax import lax
from jax.experimental import pallas as pl
from jax.experimental.pallas import tpu as pltpu

F32 = jnp.float32
BF = jnp.bfloat16

DEPTH = 2
D_MODEL = 1024
D_RNN = 1280
D_SGU = 1024
D_FF = 4096
D_IN = 2 * D_RNN + 2 * D_SGU + 2 * D_MODEL
N_QUARTERS = 4
Q_IN = D_IN // N_QUARTERS
Q_FF = D_FF // N_QUARTERS
RNN_HEADS = 20
RNN_HEAD_DIM = 64
LRU_GROUP = 256
N_LRU_GROUPS = D_RNN // LRU_GROUP
HEADS_PER_GROUP = LRU_GROUP // RNN_HEAD_DIM
CONV_WIDTH = 4
LRU_C = 8.0
SGU_GROUPS = 8
SGU_BLOCK = 128
CHUNK = 64
EPS = 1e-6

ADAM_LR = 0.001
ADAM_B1 = 0.9
ADAM_B2 = 0.999
ADAM_EPS = 1e-08
ADAM_WD = 0.01
ADAM_STEP = 10

SUBLANES = 8
TOKEN_TILE = 512
VMEM_LIMIT_BYTES = 56 * 1024 * 1024

MESH = pl.DeviceIdType.MESH


def _params(semantics=None, vmem=True, **kw):
    return pltpu.CompilerParams(
        dimension_semantics=semantics,
        vmem_limit_bytes=VMEM_LIMIT_BYTES if vmem else None,
        **kw,
    )


def _dot(a, b):
    return jnp.dot(a, b, preferred_element_type=F32)


def _dot_nt(a, b):
    return lax.dot_general(a, b, (((1,), (1,)), ((), ())), preferred_element_type=F32)


def _dot_tn(a, b):
    return lax.dot_general(a, b, (((0,), (0,)), ((), ())), preferred_element_type=F32)


_GELU_C = math.sqrt(2.0 / math.pi)
_GELU_A = 0.044715


def _gelu(x):
    return 0.5 * x * (1.0 + jnp.tanh(_GELU_C * (x + _GELU_A * x * x * x)))


def _gelu_and_grad(x):
    x2 = x * x
    t = jnp.tanh(_GELU_C * (x + _GELU_A * x2 * x))
    du = _GELU_C * (1.0 + 3.0 * _GELU_A * x2)
    return 0.5 * x * (1.0 + t), 0.5 * (1.0 + t) + 0.5 * x * (1.0 - t * t) * du


def _rms_stats(x):
    return lax.rsqrt(jnp.mean(x * x, axis=-1, keepdims=True) + EPS)


def _rms_bwd(dy, x, g):
    rs = _rms_stats(x)
    n = x * rs
    dn = dy * g
    dx = rs * (dn - n * jnp.mean(dn * n, axis=-1, keepdims=True))
    return dx, dy * n


def _row_sum(x):
    return jnp.sum(x, axis=0, keepdims=True)


def _tile_spec(ts, width, col=0):
    return pl.BlockSpec((ts, width), lambda i, col=col: (i, col))


def _full_spec(shape):
    zeros = (0,) * len(shape)
    return pl.BlockSpec(shape, lambda *_: zeros)


def _layer_spec(w, layer):
    zeros = (0,) * (w.ndim - 1)
    return pl.BlockSpec((None,) + tuple(w.shape[1:]), lambda *_: (layer,) + zeros)


def _norm_call(x, g, ts):
    s = x.shape[0]

    def body(x_ref, g_ref, h_ref):
        xv = x_ref[...]
        h_ref[...] = (xv * _rms_stats(xv) * g_ref[...]).astype(BF)

    return pl.pallas_call(
        body, name="norm_fwd", grid=(s // ts,),
        in_specs=[_tile_spec(ts, D_MODEL), _full_spec((1, D_MODEL))],
        out_specs=_tile_spec(ts, D_MODEL),
        out_shape=jax.ShapeDtypeStruct((s, D_MODEL), BF),
        compiler_params=_params(("parallel",)),
    )(x, g)


def _inproj_call(h, w_in, layer, ts):
    s = h.shape[0]

    def body(h_ref, w_ref, o_ref):
        o_ref[...] = _dot(h_ref[...], w_ref[...]).astype(BF)

    return pl.pallas_call(
        body, name="inproj_fwd", grid=(N_QUARTERS, s // ts),
        in_specs=[
            pl.BlockSpec((ts, D_MODEL), lambda q, i: (i, 0)),
            pl.BlockSpec((None, None, D_MODEL, Q_IN), lambda q, i: (layer, q, 0, 0)),
        ],
        out_specs=pl.BlockSpec((ts, Q_IN), lambda q, i: (i, q)),
        out_shape=jax.ShapeDtypeStruct((s, D_IN), BF),
        compiler_params=_params(("parallel", "parallel")),
    )(h, w_in)


def _shift_down(x, tail, s):
    xr = pltpu.roll(x, s, 0)
    tr = pltpu.roll(tail, s, 0)
    row = lax.broadcasted_iota(jnp.int32, tail.shape, 0)
    top = jnp.where(row < s, tr, xr[0:SUBLANES])
    return jnp.concatenate([top, xr[SUBLANES:]], axis=0)


def _shift_up(x, head, s):
    t = x.shape[0]
    xr = pltpu.roll(x, t - s, 0)
    hr = pltpu.roll(head, SUBLANES - s, 0)
    row = lax.broadcasted_iota(jnp.int32, head.shape, 0)
    bottom = jnp.where(row >= SUBLANES - s, hr, xr[t - SUBLANES:])
    return jnp.concatenate([xr[: t - SUBLANES], bottom], axis=0)


def _conv_fwd(x, tail, cw_ref, cb_ref):
    out = cb_ref[...] + cw_ref[CONV_WIDTH - 1:CONV_WIDTH, :] * x
    for s in range(1, CONV_WIDTH):
        k = CONV_WIDTH - 1 - s
        out = out + cw_ref[k:k + 1, :] * _shift_down(x, tail, s)
    return out


def _group_dot(x_bf, w_ref, dot):
    cols = [dot(x_bf[:, g * LRU_GROUP:(g + 1) * LRU_GROUP], w_ref[g]) for g in range(N_LRU_GROUPS)]
    return jnp.concatenate(cols, axis=1)


def _lru_gates(xr, wa_ref, wx_ref, ba_ref, bx_ref, sp_ref):
    xb = xr.astype(BF)
    r = jax.nn.sigmoid(_group_dot(xb, wa_ref, _dot) + ba_ref[...])
    i = jax.nn.sigmoid(_group_dot(xb, wx_ref, _dot) + bx_ref[...])
    log_a = (-LRU_C * r) * sp_ref[...]
    a = jnp.exp(log_a)
    nrm2 = -jnp.tanh(log_a) * (a * a + 1.0)
    inv_nrm = lax.rsqrt(jnp.maximum(nrm2, 1e-36))
    return r, i, a, nrm2 * inv_nrm, inv_nrm


def _linear_scan(a, b, carry, al_ref, bl_ref, h_ref, reverse):
    t, c = a.shape
    rowm = lax.broadcasted_iota(jnp.int32, (t, c), 0) & (SUBLANES - 1)
    for d in (1, 2, 4):
        if reverse:
            keep, sh = rowm < SUBLANES - d, t - d
        else:
            keep, sh = rowm >= d, d
        a_sh = jnp.where(keep, pltpu.roll(a, sh, 0), 1.0)
        b_sh = jnp.where(keep, pltpu.roll(b, sh, 0), 0.0)
        b = a * b_sh + b
        a = a * a_sh
    al_ref[...] = a
    bl_ref[...] = b
    groups = t // SUBLANES

    def step(j, state):
        jj = groups - 1 - j if reverse else j
        off = pl.multiple_of(jj * SUBLANES, SUBLANES)
        rows = bl_ref[pl.ds(off, SUBLANES), :] + al_ref[pl.ds(off, SUBLANES), :] * state
        h_ref[pl.ds(off, SUBLANES), :] = rows
        last = rows[0:1, :] if reverse else rows[SUBLANES - 1:SUBLANES, :]
        return jnp.broadcast_to(last, (SUBLANES, c))

    out = lax.fori_loop(0, groups, step, jnp.broadcast_to(carry, (SUBLANES, c)))
    return out[0:1, :]


def _rnn_fwd_call(proj, wa, wx, ba, bx, sp, cw, cb, ts):
    s = proj.shape[0]

    def body(xg_ref, wa_ref, wx_ref, ba_ref, bx_ref, sp_ref, cw_ref, cb_ref, xr_ref, hr_ref, ya_ref,
             tail_sc, carry_sc, al_sc, bl_sc, h_sc):
        @pl.when(pl.program_id(0) == 0)
        def _():
            tail_sc[...] = jnp.zeros_like(tail_sc)
            carry_sc[...] = jnp.zeros_like(carry_sc)

        x = xg_ref[:, :D_RNN].astype(F32)
        g = xg_ref[:, D_RNN:]
        xr = _conv_fwd(x, tail_sc[...], cw_ref, cb_ref)
        tail_sc[...] = x[ts - SUBLANES:, :]
        xr_ref[...] = xr.astype(BF)
        _, i, a, nrm, _ = _lru_gates(xr, wa_ref, wx_ref, ba_ref, bx_ref, sp_ref)
        carry_sc[...] = _linear_scan(a, nrm * (i * xr), carry_sc[...], al_sc, bl_sc, h_sc, False)
        h = h_sc[...]
        hr_ref[...] = h.astype(BF)
        ya_ref[...] = (h * _gelu(g)).astype(BF)

    gw = (N_LRU_GROUPS, LRU_GROUP, LRU_GROUP)
    return pl.pallas_call(
        body, name="rnn_fwd", grid=(s // ts,),
        in_specs=[_tile_spec(ts, 2 * D_RNN), _full_spec(gw), _full_spec(gw),
                  _full_spec((1, D_RNN)), _full_spec((1, D_RNN)), _full_spec((1, D_RNN)),
                  _full_spec((CONV_WIDTH, D_RNN)), _full_spec((1, D_RNN))],
        out_specs=[_tile_spec(ts, D_RNN)] * 3,
        out_shape=[jax.ShapeDtypeStruct((s, D_RNN), BF)] * 3,
        scratch_shapes=[pltpu.VMEM((SUBLANES, D_RNN), F32), pltpu.VMEM((1, D_RNN), F32),
                        pltpu.VMEM((ts, D_RNN), F32), pltpu.VMEM((ts, D_RNN), F32),
                        pltpu.VMEM((ts, D_RNN), F32)],
        compiler_params=_params(("arbitrary",)),
    )(proj, wa, wx, ba, bx, sp, cw, cb)


def _layernorm_fwd(x):
    mu = jnp.mean(x, axis=-1, keepdims=True)
    xc = x - mu
    rstd = lax.rsqrt(jnp.mean(xc * xc, axis=-1, keepdims=True) + EPS)
    return xc * rstd, rstd


def _sgu_mix(vn_bf, wm_ref, bsb_ref, ts):
    rows = []
    for blk in range(ts // SGU_BLOCK):
        r0 = blk * SGU_BLOCK
        cols = [
            _dot(wm_ref[g], vn_bf[r0:r0 + SGU_BLOCK, g * SGU_BLOCK:(g + 1) * SGU_BLOCK]) + bsb_ref[g]
            for g in range(SGU_GROUPS)
        ]
        rows.append(jnp.concatenate(cols, axis=1))
    return jnp.concatenate(rows, axis=0)


def _sgu_fwd_call(proj, wm, bsb, lg, lb, ts):
    s = proj.shape[0]

    def body(uv_ref, wm_ref, bsb_ref, lg_ref, lb_ref, yb_ref):
        gu = _gelu(uv_ref[:, :D_SGU])
        gv = _gelu(uv_ref[:, D_SGU:2 * D_SGU]).astype(F32)
        nh, _ = _layernorm_fwd(gv)
        vn = (nh * lg_ref[...] + lb_ref[...]).astype(BF)
        yb_ref[...] = (gu * _sgu_mix(vn, wm_ref, bsb_ref, ts)).astype(BF)

    sw = (SGU_GROUPS, SGU_BLOCK, SGU_BLOCK)
    return pl.pallas_call(
        body, name="sgu_fwd", grid=(s // ts,),
        in_specs=[_tile_spec(ts, 2 * D_RNN, 1), _full_spec(sw), _full_spec(sw),
                  _full_spec((1, D_SGU)), _full_spec((1, D_SGU))],
        out_specs=_tile_spec(ts, D_SGU),
        out_shape=jax.ShapeDtypeStruct((s, D_SGU), BF),
        compiler_params=_params(("parallel",)),
    )(proj, wm, bsb, lg, lb)


_GATE_COL0 = (2 * D_RNN + 2 * D_SGU) // 512


def _gate_specs(ts):
    return [_tile_spec(ts, 512, _GATE_COL0 + j) for j in range(4)]


def _merge_call(x, proj, ya_pre, yb_pre, w_ba, w_bb, w_out, g2, layer, ts):
    s = x.shape[0]

    def body(x_ref, ga0, ga1, gb0, gb1, ya_ref, yb_ref, wa_ref, wb_ref, wo_ref, g2_ref,
             x1_ref, yao_ref, ybo_ref, mg_ref, h2_ref):
        ya = _dot(ya_ref[...], wa_ref[...])
        yb = _dot(yb_ref[...], wb_ref[...])
        sa = jax.nn.sigmoid(jnp.concatenate([ga0[...], ga1[...]], axis=1).astype(F32))
        sb = jax.nn.sigmoid(jnp.concatenate([gb0[...], gb1[...]], axis=1).astype(F32))
        merged = (sa * ya + sb * yb).astype(BF)
        x1 = x_ref[...] + _dot(merged, wo_ref[...])
        x1_ref[...] = x1
        yao_ref[...] = ya.astype(BF)
        ybo_ref[...] = yb.astype(BF)
        mg_ref[...] = merged
        h2_ref[...] = (x1 * _rms_stats(x1) * g2_ref[...]).astype(BF)

    act = jax.ShapeDtypeStruct((s, D_MODEL), BF)
    return pl.pallas_call(
        body, name="merge_fwd", grid=(s // ts,),
        in_specs=[_tile_spec(ts, D_MODEL)] + _gate_specs(ts) + [
            _tile_spec(ts, D_RNN), _tile_spec(ts, D_SGU),
            _layer_spec(w_ba, layer), _layer_spec(w_bb, layer), _layer_spec(w_out, layer),
            _full_spec((1, D_MODEL))],
        out_specs=[_tile_spec(ts, D_MODEL)] * 5,
        out_shape=[jax.ShapeDtypeStruct((s, D_MODEL), F32), act, act, act, act],
        compiler_params=_params(("parallel",)),
    )(x, proj, proj, proj, proj, ya_pre, yb_pre, w_ba, w_bb, w_out, g2)


def _ffn_call(x1, h2, w_up, w_down, layer, ts):
    s = x1.shape[0]

    def body(x1_ref, h2_ref, wu_ref, wd_ref, x2_ref, p_ref):
        h2v = h2_ref[...]
        acc = x1_ref[...]
        for q in range(N_QUARTERS):
            p = _dot(h2v, wu_ref[q])
            p_ref[:, q * Q_FF:(q + 1) * Q_FF] = p.astype(BF)
            f = jnp.square(jnp.maximum(p, 0.0)).astype(BF)
            acc = acc + _dot(f, wd_ref[q * Q_FF:(q + 1) * Q_FF, :])
        x2_ref[...] = acc

    return pl.pallas_call(
        body, name="ffn_fwd", grid=(s // ts,),
        in_specs=[_tile_spec(ts, D_MODEL), _tile_spec(ts, D_MODEL),
                  pl.BlockSpec((None, N_QUARTERS, D_MODEL, Q_FF), lambda i: (layer, 0, 0, 0)),
                  pl.BlockSpec((None, D_FF, D_MODEL), lambda i: (layer, 0, 0))],
        out_specs=[_tile_spec(ts, D_MODEL), _tile_spec(ts, D_FF)],
        out_shape=[jax.ShapeDtypeStruct((s, D_MODEL), F32), jax.ShapeDtypeStruct((s, D_FF), BF)],
        compiler_params=_params(("parallel",)),
    )(x1, h2, w_up, w_down)


def _loss_call(x, target, gf, ts):
    s = x.shape[0]

    def body(x_ref, t_ref, g_ref, dx_ref, loss_ref, dg_ref):
        @pl.when(pl.program_id(0) == 0)
        def _():
            loss_ref[...] = jnp.zeros_like(loss_ref)
            dg_ref[...] = jnp.zeros_like(dg_ref)

        xv = x_ref[...]
        gv = g_ref[...]
        err = xv * _rms_stats(xv) * gv - t_ref[...]
        part = 0.5 * jnp.sum(jnp.mean(err * err, axis=-1, keepdims=True), axis=0, keepdims=True)
        loss_ref[...] += jnp.broadcast_to(part, loss_ref.shape)
        dx, dg = _rms_bwd(err * (1.0 / D_MODEL), xv, gv)
        dx_ref[...] = dx
        dg_ref[...] += _row_sum(dg)

    return pl.pallas_call(
        body, name="loss_head", grid=(s // ts,),
        in_specs=[_tile_spec(ts, D_MODEL), _tile_spec(ts, D_MODEL), _full_spec((1, D_MODEL))],
        out_specs=[_tile_spec(ts, D_MODEL), _full_spec((1, 128)), _full_spec((1, D_MODEL))],
        out_shape=[jax.ShapeDtypeStruct((s, D_MODEL), F32), jax.ShapeDtypeStruct((1, 128), F32),
                   jax.ShapeDtypeStruct((1, D_MODEL), F32)],
        compiler_params=_params(("arbitrary",)),
    )(x, target, gf)


def _ffn_bwd_call(dx2, p, x1, g2, w_up, w_down, layer, ts):
    s = dx2.shape[0]

    def body(dx2_ref, p_ref, x1_ref, g2_ref, wu_ref, wd_ref, dx1_ref, dp_ref, dg_ref):
        @pl.when(pl.program_id(0) == 0)
        def _():
            dg_ref[...] = jnp.zeros_like(dg_ref)

        dx2v = dx2_ref[...]
        dyb = dx2v.astype(BF)
        dh2 = jnp.zeros((ts, D_MODEL), F32)
        for q in range(N_QUARTERS):
            cols = slice(q * Q_FF, (q + 1) * Q_FF)
            df = _dot_nt(dyb, wd_ref[cols, :])
            dp = (df * (2.0 * jnp.maximum(p_ref[:, cols].astype(F32), 0.0))).astype(BF)
            dp_ref[:, cols] = dp
            dh2 = dh2 + _dot_nt(dp, wu_ref[q])
        dx, dg = _rms_bwd(dh2, x1_ref[...], g2_ref[...])
        dx1_ref[...] = dx2v + dx
        dg_ref[...] += _row_sum(dg)

    return pl.pallas_call(
        body, name="ffn_bwd", grid=(s // ts,),
        in_specs=[_tile_spec(ts, D_MODEL), _tile_spec(ts, D_FF), _tile_spec(ts, D_MODEL),
                  _full_spec((1, D_MODEL)),
                  pl.BlockSpec((None, N_QUARTERS, D_MODEL, Q_FF), lambda i: (layer, 0, 0, 0)),
                  pl.BlockSpec((None, D_FF, D_MODEL), lambda i: (layer, 0, 0))],
        out_specs=[_tile_spec(ts, D_MODEL), _tile_spec(ts, D_FF), _full_spec((1, D_MODEL))],
        out_shape=[jax.ShapeDtypeStruct((s, D_MODEL), F32), jax.ShapeDtypeStruct((s, D_FF), BF),
                   jax.ShapeDtypeStruct((1, D_MODEL), F32)],
        compiler_params=_params(("arbitrary",)),
    )(dx2, p, x1, g2, w_up, w_down)


def _merge_bwd_call(dx1, proj, ya, yb, w_ba, w_bb, w_out, layer, ts, after=None):
    s = dx1.shape[0]

    def body(dx1_ref, ga0, ga1, gb0, gb1, ya_ref, yb_ref, wa_ref, wb_ref, wo_ref, *rest):
        dya_ref, dyb_ref, dgate_ref, dyap_ref, dybp_ref = rest[-5:]
        dm = _dot_nt(dx1_ref[...].astype(BF), wo_ref[...])
        sa = jax.nn.sigmoid(jnp.concatenate([ga0[...], ga1[...]], axis=1).astype(F32))
        sb = jax.nn.sigmoid(jnp.concatenate([gb0[...], gb1[...]], axis=1).astype(F32))
        dya = (dm * sa).astype(BF)
        dyb = (dm * sb).astype(BF)
        dya_ref[...] = dya
        dyb_ref[...] = dyb
        dgate_ref[:, :D_MODEL] = (dm * ya_ref[...].astype(F32) * sa * (1.0 - sa)).astype(BF)
        dgate_ref[:, D_MODEL:] = (dm * yb_ref[...].astype(F32) * sb * (1.0 - sb)).astype(BF)
        dyap_ref[...] = _dot_nt(dya, wa_ref[...]).astype(BF)
        dybp_ref[...] = _dot_nt(dyb, wb_ref[...]).astype(BF)

    act = jax.ShapeDtypeStruct((s, D_MODEL), BF)
    return pl.pallas_call(
        body, name="merge_bwd", grid=(s // ts,),
        in_specs=[_tile_spec(ts, D_MODEL)] + _gate_specs(ts) + [
            _tile_spec(ts, D_MODEL), _tile_spec(ts, D_MODEL),
            _layer_spec(w_ba, layer), _layer_spec(w_bb, layer), _layer_spec(w_out, layer)]
        + ([] if after is None else [pl.BlockSpec(memory_space=pl.ANY)]),
        out_specs=[_tile_spec(ts, D_MODEL), _tile_spec(ts, D_MODEL), _tile_spec(ts, 2 * D_MODEL),
                   _tile_spec(ts, D_RNN), _tile_spec(ts, D_SGU)],
        out_shape=[act, act, jax.ShapeDtypeStruct((s, 2 * D_MODEL), BF),
                   jax.ShapeDtypeStruct((s, D_RNN), BF), jax.ShapeDtypeStruct((s, D_SGU), BF)],
        compiler_params=_params(("parallel",)),
    )(dx1, proj, proj, proj, proj, ya, yb, w_ba, w_bb, w_out, *([] if after is None else [after]))


def _sgu_bwd_call(dyb_pre, proj, wm, bsb, mask, lg, lb, ts):
    s = proj.shape[0]

    def body(dy_ref, uv_ref, wm_ref, bsb_ref, mask_ref, lg_ref, lb_ref,
             duv_ref, dws_ref, dbs_ref, dlg_ref, dlb_ref, dm_sc):
        step = pl.program_id(0)

        @pl.when(step == 0)
        def _():
            dws_ref[...] = jnp.zeros_like(dws_ref)
            dlg_ref[...] = jnp.zeros_like(dlg_ref)
            dlb_ref[...] = jnp.zeros_like(dlb_ref)
            dm_sc[...] = jnp.zeros_like(dm_sc)

        gu, dgu_du = _gelu_and_grad(uv_ref[:, :D_SGU])
        gv, dgv_dv = _gelu_and_grad(uv_ref[:, D_SGU:2 * D_SGU])
        nh, rstd = _layernorm_fwd(gv.astype(F32))
        lgv = lg_ref[...]
        vn = (nh * lgv + lb_ref[...]).astype(BF)
        dy = dy_ref[...].astype(F32)
        du = dy * _sgu_mix(vn, wm_ref, bsb_ref, ts) * dgu_du
        dmix = dy * gu
        dmix_bf = dmix.astype(BF)
        dm_acc = dm_sc[...]
        rows = []
        for blk in range(ts // SGU_BLOCK):
            r0 = blk * SGU_BLOCK
            dm_acc = dm_acc + dmix[r0:r0 + SGU_BLOCK, :]
            cols = []
            for g in range(SGU_GROUPS):
                c0 = g * SGU_BLOCK
                dmg = dmix_bf[r0:r0 + SGU_BLOCK, c0:c0 + SGU_BLOCK]
                cols.append(_dot_tn(wm_ref[g], dmg))
                dws_ref[g] += mask_ref[...] * _dot_nt(dmg, vn[r0:r0 + SGU_BLOCK, c0:c0 + SGU_BLOCK])
            rows.append(jnp.concatenate(cols, axis=1))
        dm_sc[...] = dm_acc
        dvn = jnp.concatenate(rows, axis=0)
        dlg_ref[...] += _row_sum(dvn * nh)
        dlb_ref[...] += _row_sum(dvn)
        dnh = dvn * lgv
        dgv = rstd * (dnh - jnp.mean(dnh, axis=-1, keepdims=True)
                      - nh * jnp.mean(dnh * nh, axis=-1, keepdims=True))
        duv_ref[:, :D_SGU] = du.astype(BF)
        duv_ref[:, D_SGU:] = (dgv * dgv_dv).astype(BF)

        @pl.when(step == pl.num_programs(0) - 1)
        def _():
            for g in range(SGU_GROUPS):
                dbs_ref[:, g:g + 1] = jnp.sum(
                    dm_acc[:, g * SGU_BLOCK:(g + 1) * SGU_BLOCK], axis=1, keepdims=True)

    sw = (SGU_GROUPS, SGU_BLOCK, SGU_BLOCK)
    return pl.pallas_call(
        body, name="sgu_bwd", grid=(s // ts,),
        in_specs=[_tile_spec(ts, D_SGU), _tile_spec(ts, 2 * D_RNN, 1), _full_spec(sw), _full_spec(sw),
                  _full_spec((SGU_BLOCK, SGU_BLOCK)), _full_spec((1, D_SGU)), _full_spec((1, D_SGU))],
        out_specs=[_tile_spec(ts, 2 * D_SGU), _full_spec(sw), _full_spec((SGU_BLOCK, SGU_GROUPS)),
                   _full_spec((1, D_SGU)), _full_spec((1, D_SGU))],
        out_shape=[jax.ShapeDtypeStruct((s, 2 * D_SGU), BF), jax.ShapeDtypeStruct(sw, F32),
                   jax.ShapeDtypeStruct((SGU_BLOCK, SGU_GROUPS), F32),
                   jax.ShapeDtypeStruct((1, D_SGU), F32), jax.ShapeDtypeStruct((1, D_SGU), F32)],
        scratch_shapes=[pltpu.VMEM((SGU_BLOCK, D_SGU), F32)],
        compiler_params=_params(("arbitrary",)),
    )(dyb_pre, proj, wm, bsb, mask, lg, lb)


_ROW_DBA, _ROW_DBX, _ROW_DSP, _ROW_DCB, _ROW_DCW = 0, 1, 2, 3, 4
_PREV_ROWS = 16


def _rnn_bwd_call(dya_pre, proj, xr_saved, hr, wa, wx, ba, bx, sp, cw, ts):
    s = proj.shape[0]
    nt = s // ts
    per = ts // _PREV_ROWS

    def tile(i):
        return nt - 1 - i

    def prev(i):
        return jnp.maximum(tile(i) * per - 1, 0)

    def body(dy_ref, xg_ref, xr_ref, hr_ref, hrp_ref, wa_ref, wx_ref, ba_ref, bx_ref, sp_ref,
             cw_ref, dxg_ref, dwa_ref, dwx_ref, vec_ref,
             lam_carry, a_first, dxr_head, al_sc, bl_sc, lam_sc):
        step = pl.program_id(0)

        @pl.when(step == 0)
        def _():
            dwa_ref[...] = jnp.zeros_like(dwa_ref)
            dwx_ref[...] = jnp.zeros_like(dwx_ref)
            vec_ref[...] = jnp.zeros_like(vec_ref)
            lam_carry[...] = jnp.zeros_like(lam_carry)
            a_first[...] = jnp.zeros_like(a_first)
            dxr_head[...] = jnp.zeros_like(dxr_head)

        has_prev = (step < nt - 1).astype(F32)
        x = xg_ref[:, :D_RNN].astype(F32)
        g = xg_ref[:, D_RNN:]
        h_tail =hrp_ref[_PREV_ROWS - SUBLANES:, :].astype(F32) * has_prev
        xr = xr_ref[...].astype(F32)
        r, i, a, nrm, inv_nrm = _lru_gates(xr, wa_ref, wx_ref, ba_ref, bx_ref, sp_ref)
        h = hr_ref[...].astype(F32)
        dy = dy_ref[...].astype(F32)
        gg, dgg = _gelu_and_grad(g)

        coef = _shift_up(a, jnp.broadcast_to(a_first[...], (SUBLANES, D_RNN)), 1)
        lam_carry[...] = _linear_scan(coef, dy * gg, lam_carry[...], al_sc, bl_sc, lam_sc, True)
        a_first[...] = a[0:1, :]
        lam = lam_sc[...]

        da = lam * _shift_down(h, h_tail, 1)
        dnrm = lam * (i * xr)
        di = lam * nrm * xr
        dlog_a = da * a - dnrm * (a * a) * inv_nrm
        spv = sp_ref[...]
        dza = (dlog_a * (-LRU_C * spv)) * (r * (1.0 - r))
        dzx = di * (i * (1.0 - i))
        vec_ref[_ROW_DSP:_ROW_DSP + 1, :] += _row_sum(dlog_a * (-LRU_C * r))
        vec_ref[_ROW_DBA:_ROW_DBA + 1, :] += _row_sum(dza)
        vec_ref[_ROW_DBX:_ROW_DBX + 1, :] += _row_sum(dzx)
        xb = xr.astype(BF)
        dza_bf = dza.astype(BF)
        dzx_bf = dzx.astype(BF)
        for grp in range(N_LRU_GROUPS):
            cols = slice(grp * LRU_GROUP, (grp + 1) * LRU_GROUP)
            dwa_ref[grp] += _dot_tn(xb[:, cols], dza_bf[:, cols])
            dwx_ref[grp] += _dot_tn(xb[:, cols], dzx_bf[:, cols])
        dxr = (lam * nrm * i + _group_dot(dza_bf, wa_ref, _dot_nt) + _group_dot(dzx_bf, wx_ref, _dot_nt))

        vec_ref[_ROW_DCB:_ROW_DCB + 1, :] += _row_sum(dxr)
        head = dxr_head[...]
        dx = cw_ref[CONV_WIDTH - 1:CONV_WIDTH, :] * dxr
        vec_ref[_ROW_DCW + 3:_ROW_DCW + 4, :] += _row_sum(dxr * x)
        for sft in range(1, CONV_WIDTH):
            k = CONV_WIDTH - 1 - sft
            ahead = _shift_up(dxr, head, sft)
            dx = dx + cw_ref[k:k + 1, :] * ahead
            vec_ref[_ROW_DCW + k:_ROW_DCW + k + 1, :] += _row_sum(ahead * x)
        dxr_head[...] = dxr[0:SUBLANES, :]
        dxg_ref[:, :D_RNN] = dx.astype(BF)
        dxg_ref[:, D_RNN:] = (dy * h * dgg).astype(BF)

    gw = (N_LRU_GROUPS, LRU_GROUP, LRU_GROUP)
    rev = lambda width: pl.BlockSpec((ts, width), lambda i: (tile(i), 0))
    return pl.pallas_call(
        body, name="rnn_bwd", grid=(nt,),
        in_specs=[rev(D_RNN), rev(2 * D_RNN), rev(D_RNN), rev(D_RNN),
                  pl.BlockSpec((_PREV_ROWS, D_RNN), lambda i: (prev(i), 0)),
                  _full_spec(gw), _full_spec(gw),
                  _full_spec((1, D_RNN)), _full_spec((1, D_RNN)), _full_spec((1, D_RNN)),
                  _full_spec((CONV_WIDTH, D_RNN))],
        out_specs=[rev(2 * D_RNN), _full_spec(gw), _full_spec(gw), _full_spec((SUBLANES, D_RNN))],
        out_shape=[jax.ShapeDtypeStruct((s, 2 * D_RNN), BF), jax.ShapeDtypeStruct(gw, F32),
                   jax.ShapeDtypeStruct(gw, F32), jax.ShapeDtypeStruct((SUBLANES, D_RNN), F32)],
        scratch_shapes=[pltpu.VMEM((1, D_RNN), F32), pltpu.VMEM((1, D_RNN), F32),
                        pltpu.VMEM((SUBLANES, D_RNN), F32),
                        pltpu.VMEM((ts, D_RNN), F32), pltpu.VMEM((ts, D_RNN), F32),
                        pltpu.VMEM((ts, D_RNN), F32)],
        compiler_params=_params(("arbitrary",)),
    )(dya_pre, proj, xr_saved, hr, hr, wa, wx, ba, bx, sp, cw)


def _inproj_bwd_call(dxg, duv, dgate, dx1, x, g1, w_in, layer, ts):
    s = x.shape[0]

    def body(dxg_ref, duv_ref, dgt_ref, dx1_ref, x_ref, g_ref, w_ref, dx_ref, dproj_ref, dg_ref):
        @pl.when(pl.program_id(0) == 0)
        def _():
            dg_ref[...] = jnp.zeros_like(dg_ref)

        dproj = jnp.concatenate([dxg_ref[...], duv_ref[...], dgt_ref[...]], axis=1)
        dproj_ref[...] = dproj
        dh = jnp.zeros((ts, D_MODEL), F32)
        for q in range(N_QUARTERS):
            dh = dh + _dot_nt(dproj[:, q * Q_IN:(q + 1) * Q_IN], w_ref[q])
        dx, dg = _rms_bwd(dh, x_ref[...], g_ref[...])
        dx_ref[...] = dx1_ref[...] + dx
        dg_ref[...] += _row_sum(dg)

    return pl.pallas_call(
        body, name="inproj_bwd", grid=(s // ts,),
        in_specs=[_tile_spec(ts, 2 * D_RNN), _tile_spec(ts, 2 * D_SGU), _tile_spec(ts, 2 * D_MODEL),
                  _tile_spec(ts, D_MODEL), _tile_spec(ts, D_MODEL), _full_spec((1, D_MODEL)),
                  pl.BlockSpec((None, N_QUARTERS, D_MODEL, Q_IN), lambda i: (layer, 0, 0, 0))],
        out_specs=[_tile_spec(ts, D_MODEL), _tile_spec(ts, D_IN), _full_spec((1, D_MODEL))],
        out_shape=[jax.ShapeDtypeStruct((s, D_MODEL), F32), jax.ShapeDtypeStruct((s, D_IN), BF),
                   jax.ShapeDtypeStruct((1, D_MODEL), F32)],
        compiler_params=_params(("arbitrary",)),
    )(dxg, duv, dgate, dx1, x, g1, w_in)


def _relu_sq(p):
    return jnp.square(jnp.maximum(p, 0))


def _wgrad_call(a, b, core, tm, tn, tk, col_blocked, name, a_fn=None):
    s, m = a.shape
    n = b.shape[1]
    r, cols = (m, n // N_QUARTERS) if col_blocked else (m // N_QUARTERS, n)
    r2 = r // 2
    per_tile = tm // r
    steps = s // tk

    def body(core_ref, a_ref, b_ref, keep_ref, send_ref, *acc):
        av = a_ref[...]
        if a_fn is not None:
            av = a_fn(av)
        prod = _dot_tn(av.astype(BF), b_ref[...].astype(BF))

        def emit(total):
            for h in range(2):
                @pl.when(core_ref[0] == h)
                def _():
                    for q in range(per_tile):
                        keep_ref[q] = total[q * r + h * r2:q * r + (h + 1) * r2]
                        send_ref[q] = total[q * r + (1 - h) * r2:q * r + (2 - h) * r2].astype(BF)

        if steps == 1:
            emit(prod)
        else:
            acc_ref, = acc
            step = pl.program_id(2)

            @pl.when(step == 0)
            def _():
                acc_ref[...] = prod

            @pl.when(jnp.logical_and(step > 0, step < steps - 1))
            def _():
                acc_ref[...] += prod

            @pl.when(step == steps - 1)
            def _():
                emit(acc_ref[...] + prod)

    if col_blocked:
        per_q = cols // tn
        out_spec = pl.BlockSpec((1, r2, tn), lambda i, j, k, c: (j // per_q, 0, j % per_q))
    else:
        out_spec = pl.BlockSpec((per_tile, r2, tn), lambda i, j, k, c: (i, 0, j))
    return pl.pallas_call(
        body, name=name,
        out_shape=[jax.ShapeDtypeStruct((N_QUARTERS, r2, cols), F32),
                   jax.ShapeDtypeStruct((N_QUARTERS, r2, cols), BF)],
        grid_spec=pltpu.PrefetchScalarGridSpec(
            num_scalar_prefetch=1, grid=(m // tm, n // tn, steps),
            in_specs=[pl.BlockSpec((tk, tm), lambda i, j, k, c: (k, i)),
                      pl.BlockSpec((tk, tn), lambda i, j, k, c: (k, j))],
            out_specs=[out_spec, out_spec],
            scratch_shapes=[] if steps == 1 else [pltpu.VMEM((tm, tn), F32)]),
        compiler_params=_params(("parallel", "parallel", "arbitrary")),
    )(core, a, b)


BIG = ("w_in", "w_up", "w_down", "w_branch_a", "w_branch_b", "w_out")


def _block_diag(w):
    w4 = w.reshape(N_LRU_GROUPS, HEADS_PER_GROUP, RNN_HEAD_DIM, RNN_HEAD_DIM)
    eye = jnp.eye(HEADS_PER_GROUP, dtype=w.dtype)
    return jnp.einsum("gjio,jk->gjiko", w4, eye).reshape(N_LRU_GROUPS, LRU_GROUP, LRU_GROUP)


def _block_diag_extract(d):
    d5 = d.reshape(N_LRU_GROUPS, HEADS_PER_GROUP, RNN_HEAD_DIM, HEADS_PER_GROUP, RNN_HEAD_DIM)
    blocks = [d5[:, j, :, j, :] for j in range(HEADS_PER_GROUP)]
    return jnp.stack(blocks, axis=1).reshape(RNN_HEADS, RNN_HEAD_DIM, RNN_HEAD_DIM)


def _sgu_mask():
    chunk = jnp.arange(SGU_BLOCK) // CHUNK
    return (chunk[:, None] >= chunk[None, :]).astype(F32)


def _layer_small(sm, l, core):
    row = lambda v: v.reshape(1, -1)
    return dict(
        core=core,
        g1=row(sm["norm_mix_g"][l]), g2=row(sm["norm_ffn_g"][l]),
        wa=_block_diag(sm["lru_w_a"][l]).astype(BF), wx=_block_diag(sm["lru_w_x"][l]).astype(BF),
        ba=row(sm["lru_b_a"][l]), bx=row(sm["lru_b_x"][l]),
        sp=row(jax.nn.softplus(-sm["lru_lambda"][l])),
        cw=sm["conv_w"][l], cb=row(sm["conv_b"][l]),
        wm=(sm["sgu_w_s"][l] * _sgu_mask()).astype(BF),
        bsb=jnp.broadcast_to(sm["sgu_b_s"][l][:, :, None], (SGU_GROUPS, SGU_BLOCK, SGU_BLOCK)),
        lg=row(sm["sgu_ln_g"][l]), lb=row(sm["sgu_ln_b"][l]),
    )


def _layer_fwd_mix(x, big, p, ts, h=None, before_sgu=None):
    if h is None:
        h = _norm_call(x, p["g1"], ts)
    proj = _inproj_call(h, big["w_in"], 0, 2 * ts)
    xr, hr, ya_pre = _rnn_fwd_call(proj, p["wa"], p["wx"], p["ba"], p["bx"], p["sp"], p["cw"], p["cb"], ts)
    lg = p["lg"] if before_sgu is None else p["lg"] + before_sgu(ya_pre)
    yb_pre = _sgu_fwd_call(proj, p["wm"], p["bsb"], lg, p["lb"], ts)
    return dict(p=p, x=x, h=h, proj=proj, xr=xr, hr=hr, ya_pre=ya_pre, yb_pre=yb_pre)


def _layer_fwd_out(sv, big, ts):
    x1, ya, yb, merged, h2 = _merge_call(sv["x"], sv["proj"], sv["ya_pre"], sv["yb_pre"], big["w_branch_a"],
                                         big["w_branch_b"], big["w_out"], sv["p"]["g2"], 0, ts)
    x2, pre = _ffn_call(x1, h2, big["w_up"], big["w_down"], 0, ts)
    sv.update(x1=x1, ya=ya, yb=yb, merged=merged, h2=h2, pre=pre)
    return x2


def _layer_bwd_ffn(dx, sv, big, ts):
    p = sv["p"]
    dx1, dpre, dg2 = _ffn_bwd_call(dx, sv["pre"], sv["x1"], p["g2"], big["w_up"], big["w_down"], 0, ts)
    tk = dx.shape[0]
    gb = dict(
        w_down=_wgrad_call(sv["pre"], dx, p["core"], Q_FF, D_MODEL // 2, tk, False, "wgrad_down", a_fn=_relu_sq),
        w_up=_wgrad_call(sv["h2"], dpre, p["core"], D_MODEL, Q_FF, tk, True, "wgrad_up"))
    return dx1, gb, dict(norm_ffn_g=dg2[0])


def _layer_bwd_merge(dx1, sv, big, ts, after=None):
    tk = dx1.shape[0]
    core = sv["p"]["core"]
    dya, dyb, dgate, dya_pre, dyb_pre = _merge_bwd_call(
        dx1, sv["proj"], sv["ya"], sv["yb"], big["w_branch_a"], big["w_branch_b"], big["w_out"], 0, ts, after)
    gb = dict(
        w_out=_wgrad_call(sv["merged"], dx1, core, D_MODEL, D_MODEL // 2, tk, False, "wgrad_out"),
        w_branch_a=_wgrad_call(sv["ya_pre"], dya, core, D_RNN, D_MODEL // 2, tk, False, "wgrad_branch_a"),
        w_branch_b=_wgrad_call(sv["yb_pre"], dyb, core, D_SGU, D_MODEL // 2, tk, False, "wgrad_branch_b"))
    return (dgate, dya_pre, dyb_pre), gb


def _layer_bwd_branches(dx1, merge_out, sv, big, lam, ts):
    p = sv["p"]
    tk = dx1.shape[0]
    dgate, dya_pre, dyb_pre = merge_out
    gb = {}
    duv, dws, dbs, dlg, dlb = _sgu_bwd_call(dyb_pre, sv["proj"], p["wm"], p["bsb"], _sgu_mask(), p["lg"], p["lb"],
                                            ts)
    dxg, dwa, dwx, vec = _rnn_bwd_call(dya_pre, sv["proj"], sv["xr"], sv["hr"], p["wa"], p["wx"], p["ba"], p["bx"],
                                       p["sp"], p["cw"], ts // 2)
    dx, dproj, dg1 = _inproj_bwd_call(dxg, duv, dgate, dx1, sv["x"], p["g1"], big["w_in"], 0, ts)
    gb["w_in"] = _wgrad_call(sv["h"], dproj, p["core"], D_MODEL, Q_IN, tk // 2, True, "wgrad_in")
    gs = dict(
        norm_mix_g=dg1[0], conv_w=vec[_ROW_DCW:_ROW_DCW + CONV_WIDTH], conv_b=vec[_ROW_DCB],
        lru_w_a=_block_diag_extract(dwa), lru_w_x=_block_diag_extract(dwx),
        lru_b_a=vec[_ROW_DBA].reshape(RNN_HEADS, RNN_HEAD_DIM), lru_b_x=vec[_ROW_DBX].reshape(RNN_HEADS, RNN_HEAD_DIM),
        lru_lambda=-vec[_ROW_DSP] * jax.nn.sigmoid(-lam),
        sgu_ln_g=dlg[0], sgu_ln_b=dlb[0], sgu_w_s=dws, sgu_b_s=dbs.T)
    return dx, gb, gs


def _local_step(x, target, big, sm, ts):
    saved = []
    core = jnp.zeros((1,), jnp.int32)
    for l in range(DEPTH):
        sv = _layer_fwd_mix(x, big[l], _layer_small(sm, l, core), ts)
        x = _layer_fwd_out(sv, big[l], ts)
        saved.append(sv)
    dx, loss, dgf = _loss_call(x, target, sm["final_norm_g"].reshape(1, -1), ts)
    gb, gs = [None] * DEPTH, [None] * DEPTH
    for l in reversed(range(DEPTH)):
        dx1, gb_ffn, gs_ffn = _layer_bwd_ffn(dx, saved[l], big[l], ts)
        merge_out, gb_merge = _layer_bwd_merge(dx1, saved[l], big[l], ts)
        dx, gb_mix, gs_mix = _layer_bwd_branches(dx1, merge_out, saved[l], big[l], sm["lru_lambda"][l], ts)
        gb[l] = {**gb_ffn, **gb_merge, **gb_mix}
        gs[l] = {**gs_ffn, **gs_mix}
    gs = {k: jnp.stack([g[k] for g in gs]) for k in gs[0]}
    gs["final_norm_g"] = dgf[0]
    return loss, dx, gb, gs


EW_VMEM_BYTES = 24 * 1024 * 1024


def _row_block(rows, cols, bytes_per_elem):
    for br in range(min(rows, EW_VMEM_BYTES // (2 * bytes_per_elem * cols)), 0, -1):
        if rows % br == 0 and br % 16 == 0:
            return br
    return rows


def _ew_call(fn, name, operands, outputs, slabs=1, sel=None, into=None, after=None):
    if into is not None and not isinstance(into, (list, tuple)):
        into = [into]
    rows, cols = outputs[0][0].shape[2:]
    br = _row_block(rows, cols, sum(jnp.dtype(a.dtype).itemsize for a, _ in operands + outputs))
    n_in = len(operands)

    def pick(tok, g, s):
        if callable(tok):
            return tok(g, s)
        if tok == "g":
            return g
        if isinstance(tok, tuple):
            return s[tok[1]]
        return tok

    def spec(idx):
        return pl.BlockSpec((None, None, br, cols),
                            lambda g, i, s, idx=idx: (pick(idx[0], g, s), pick(idx[1], g, s), i, 0))

    if sel is None:
        sel = jnp.zeros((1,), jnp.int32)
    in_specs = [spec(idx) for _, idx in operands]
    arrays = [a for a, _ in operands]
    aliases = {}
    for j, buf in enumerate(into or ()):
        in_specs.append(pl.BlockSpec(memory_space=pl.ANY))
        arrays.append(buf)
        aliases[1 + n_in + j] = j
    if after is not None:
        in_specs.append(pl.BlockSpec(memory_space=pl.ANY))
        arrays.append(after)

    def body(sel_ref, *refs):
        outs = fn(*[r[...] for r in refs[:n_in]])
        for o_ref, o in zip(refs[len(arrays):], outs):
            o_ref[...] = o.astype(o_ref.dtype)

    return pl.pallas_call(
        body, name=name, out_shape=[s for s, _ in outputs],
        grid_spec=pltpu.PrefetchScalarGridSpec(
            num_scalar_prefetch=1, grid=(slabs, rows // br),
            in_specs=in_specs,
            out_specs=[spec(idx) for _, idx in outputs]),
        input_output_aliases=aliases,
        compiler_params=_params(("parallel", "parallel")),
    )(sel, *arrays)


def _as4(a):
    return a.reshape((1,) * (4 - a.ndim) + a.shape)


def _adamw(w, g, m, v):
    m = ADAM_B1 * m + (1.0 - ADAM_B1) * g
    v = ADAM_B2 * v + (1.0 - ADAM_B2) * jnp.square(g)
    m_hat = m / (1.0 - ADAM_B1 ** ADAM_STEP)
    v_hat = v / (1.0 - ADAM_B2 ** ADAM_STEP)
    delta = -ADAM_LR * (m_hat / (jnp.sqrt(v_hat) + ADAM_EPS) + ADAM_WD * w)
    return delta, m, v


def _small_adamw_call(ws, gs, ms, vs):
    n = len(ws)

    def body(*refs):
        for k in range(n):
            w, g, m, v = (refs[j * n + k][...] for j in range(4))
            outs = _adamw(w, g, m, v)
            for j in range(3):
                refs[(4 + j) * n + k][...] = outs[j]

    shapes = [jax.ShapeDtypeStruct(w.shape, F32) for w in ws]
    outs = pl.pallas_call(
        body, name="adamw_small", out_shape=shapes * 3,
        in_specs=[pl.BlockSpec(memory_space=pltpu.VMEM)] * (4 * n),
        out_specs=[pl.BlockSpec(memory_space=pltpu.VMEM)] * (3 * n),
        compiler_params=_params(),
    )(*ws, *gs, *ms, *vs)
    return outs[:n], outs[n:2 * n], outs[2 * n:]


ANY = pl.BlockSpec(memory_space=pl.ANY)


def _place():
    x, y, c = lax.axis_index("x"), lax.axis_index("y"), lax.axis_index("c")
    chips = [(1 - x, y), (x, 1 - y), (1 - x, 1 - y)]
    return x, y, c, chips


def _remote(src, dst, send_sem, recv_sem, to):
    return pltpu.make_async_remote_copy(src_ref=src, dst_ref=dst, send_sem=send_sem, recv_sem=recv_sem,
                                        device_id=to, device_id_type=MESH)


def _gather_call(bufs):
    n = len(bufs)

    def body(*refs):
        out = refs[n:2 * n]
        send_sems, recv_sems = refs[2 * n:]
        x, y, c, chips = _place()
        me_q = 2 * x + y
        sibling = (x, y, 1 - c)
        first = []
        for w in range(n):
            for j, chip in enumerate(chips):
                mine = out[w].at[c, me_q]
                first.append(_remote(mine, mine, send_sems.at[w * 3 + j], recv_sems.at[w * 3 + j], (*chip, c)))
        for cp in first:
            cp.start()
        passed = []
        for w in range(n):
            for j, (qx, qy) in enumerate(chips):
                landed = out[w].at[c, 2 * qx + qy]
                k = w * 3 + j
                _remote(landed, landed, send_sems.at[k], recv_sems.at[k], (qx, qy, c)).wait_recv()
                cp = _remote(landed, landed, send_sems.at[3 * n + k], recv_sems.at[3 * n + k], sibling)
                cp.start()
                passed.append(cp)
        for w in range(n):
            for j, (qx, qy) in enumerate(chips):
                landed = out[w].at[1 - c, 2 * qx + qy]
                k = 3 * n + w * 3 + j
                _remote(landed, landed, send_sems.at[k], recv_sems.at[k], sibling).wait_recv()
        for cp in first + passed:
            cp.wait_send()

    return pl.pallas_call(
        body, name="gather_weights",
        out_shape=[jax.ShapeDtypeStruct(a.shape, a.dtype) for a in bufs],
        in_specs=[ANY] * n, out_specs=[ANY] * n,
        input_output_aliases={w: w for w in range(n)},
        scratch_shapes=[pltpu.SemaphoreType.DMA((6 * n,)), pltpu.SemaphoreType.DMA((6 * n,))],
        compiler_params=_params(vmem=False, has_side_effects=True),
    )(*bufs)


def _sibling_send_call(items):
    n = len(items)

    def body(*refs):
        src, out = refs[:n], refs[n:2 * n]
        send_sems, recv_sems = refs[2 * n:]
        x, y, c, _ = _place()
        copies = [_remote(src[w], out[w], send_sems.at[w], recv_sems.at[w], (x, y, 1 - c)) for w in range(n)]
        for cp in copies:
            cp.start()
        for cp in copies:
            cp.wait()

    return pl.pallas_call(
        body, name="grads_to_sibling",
        out_shape=[jax.ShapeDtypeStruct(a.shape, a.dtype) for a in items],
        in_specs=[ANY] * n, out_specs=[ANY] * n,
        scratch_shapes=[pltpu.SemaphoreType.DMA((n,)), pltpu.SemaphoreType.DMA((n,))],
        compiler_params=_params(vmem=False, has_side_effects=True),
    )(*items)


def _sibling_inplace_call(name, bufs, slabs, n_pairs):
    n = len(bufs)

    def body(*refs):
        out = refs[n:2 * n]
        send_sems, recv_sems = refs[2 * n:]
        x, y, c, _ = _place()
        sibling = (x, y, 1 - c)
        pairs = [pair for w, ref in enumerate(out) for pair in slabs(ref, c, w)]
        sends = [_remote(s, s, send_sems.at[k], recv_sems.at[k], sibling) for k, (s, _) in enumerate(pairs)]
        for cp in sends:
            cp.start()
        for k, (_, r) in enumerate(pairs):
            _remote(r, r, send_sems.at[k], recv_sems.at[k], sibling).wait_recv()
        for cp in sends:
            cp.wait_send()

    return pl.pallas_call(
        body, name=name,
        out_shape=[jax.ShapeDtypeStruct(a.shape, a.dtype) for a in bufs],
        in_specs=[ANY] * n, out_specs=[ANY] * n,
        input_output_aliases={w: w for w in range(n)},
        scratch_shapes=[pltpu.SemaphoreType.DMA((n_pairs,)), pltpu.SemaphoreType.DMA((n_pairs,))],
        compiler_params=_params(vmem=False, has_side_effects=True),
    )(*bufs)


HBM_SPEC = pl.BlockSpec(memory_space=pltpu.HBM)
SEM_SPEC = pl.BlockSpec(memory_space=pltpu.SEMAPHORE)
DATAFLOW_EFFECT = pltpu.SideEffectType.DATAFLOW_SIDE_EFFECTING


def _exchange_start(name, bufs, copies, n_copies, after):
    n = len(bufs)

    def body(*refs):
        ins, send_sems, recv_sems, token = refs[:n], refs[n + 1], refs[n + 2], refs[-1]
        for k, (src, dst, to) in enumerate(copies(ins)):
            _remote(src, dst, send_sems.at[k], recv_sems.at[k], to).start()
        token[...] = jnp.zeros_like(token)

    outs = pl.pallas_call(
        body, name=name,
        out_shape=(pltpu.SemaphoreType.DMA((n_copies,)), pltpu.SemaphoreType.DMA((n_copies,)),
                   *[pltpu.HBM(b.shape, b.dtype) for b in bufs], jax.ShapeDtypeStruct((SUBLANES, 128), F32)),
        in_specs=[HBM_SPEC] * n + [ANY],
        out_specs=(SEM_SPEC, SEM_SPEC, *[HBM_SPEC] * n, pl.BlockSpec(memory_space=pltpu.VMEM)),
        input_output_aliases={w: w + 2 for w in range(n)},
        compiler_params=pltpu.CompilerParams(has_side_effects=DATAFLOW_EFFECT),
    )(*[pltpu.with_memory_space_constraint(b, pltpu.HBM) for b in bufs], after)
    return outs[0], outs[1], list(outs[2:2 + n]), outs[-1]


def _exchange_wait(name, send_sems, recv_sems, bufs, copies, after):
    n = len(bufs)

    def body(*refs):
        ins, send_sems, recv_sems = refs[:n], refs[n], refs[n + 1]
        for k, (src, dst, to) in enumerate(copies(ins)):
            cp = _remote(src, dst, send_sems.at[k], recv_sems.at[k], to)
            cp.wait_send()
            cp.wait_recv()

    return pl.pallas_call(
        body, name=name,
        out_shape=[pltpu.HBM(b.shape, b.dtype) for b in bufs],
        in_specs=[HBM_SPEC] * n + [SEM_SPEC, SEM_SPEC, ANY],
        out_specs=[HBM_SPEC] * n,
        input_output_aliases={w: w for w in range(n)},
        compiler_params=pltpu.CompilerParams(has_side_effects=DATAFLOW_EFFECT),
    )(*bufs, send_sems, recv_sems, after)


def _gather_copies(refs):
    x, y, c, chips = _place()
    mine = 2 * (2 * x + y) + c
    return [(ref.at[mine], ref.at[mine], (qx, qy, c)) for ref in refs for qx, qy in chips]


def _forward_copies(refs):
    x, y, c, chips = _place()
    return [(ref.at[2 * (2 * qx + qy) + c], ref.at[2 * (2 * qx + qy) + c], (x, y, 1 - c))
            for ref in refs for qx, qy in chips]


def _gather_forward_slabs(ref, c, w):
    x, y, _, chips = _place()
    return [(ref.at[2 * (2 * qx + qy) + c], ref.at[2 * (2 * qx + qy) + 1 - c]) for qx, qy in chips]


def _device_peers():
    x, y, c, _ = _place()
    return 4 * x + 2 * y + c, [(k, (x ^ ((k >> 2) & 1), y ^ ((k >> 1) & 1), c ^ (k & 1))) for k in range(1, 8)]


def _small_scatter_copies(refs):
    me, peers = _device_peers()
    return [(refs[0].at[me ^ k], refs[1].at[me], to) for k, to in peers]


def _small_spread_copies(refs):
    me, peers = _device_peers()
    return [(refs[0].at[me], refs[0].at[me], to) for _, to in peers]


def _sibling_copies(refs):
    n = len(refs) // 2
    x, y, c, _ = _place()
    return [(refs[w], refs[n + w], (x, y, 1 - c)) for w in range(n)]


def _owner_copies(refs):
    n = len(refs) // 2
    x, y, c, chips = _place()
    return [(refs[w].at[2 * qx + qy], refs[n + w].at[j], (qx, qy, c))
            for w in range(n) for j, (qx, qy) in enumerate(chips)]


N_DEVICES = 8
SMALL_ROWS = 616


SMALL = ("norm_mix_g", "conv_w", "conv_b", "lru_w_a", "lru_b_a", "lru_w_x", "lru_b_x", "lru_lambda",
         "sgu_ln_g", "sgu_ln_b", "sgu_w_s", "sgu_b_s", "norm_ffn_g", "final_norm_g")
WEIGHTS = ("norm_mix_g", "w_in", "conv_w", "conv_b", "lru_w_a", "lru_b_a", "lru_w_x", "lru_b_x", "lru_lambda",
           "sgu_ln_g", "sgu_ln_b", "sgu_w_s", "sgu_b_s", "w_branch_a", "w_branch_b", "w_out", "norm_ffn_g",
           "w_up", "w_down", "final_norm_g")
PACK_ALIGN = SUBLANES * 128


PACKED = SMALL + ("loss",)


def _pack_small(gs):
    parts = []
    for k in PACKED:
        flat = gs[k].reshape(-1)
        parts.append(jnp.pad(flat, (0, -flat.size % PACK_ALIGN)))
    flat = jnp.concatenate(parts)
    flat = jnp.pad(flat, (0, N_DEVICES * SMALL_ROWS * 128 - flat.size))
    return flat.reshape(N_DEVICES, SMALL_ROWS, 128)


def _unpack_small(buf, like):
    flat = buf.reshape(-1)
    out, off = {}, 0
    for k in PACKED:
        size = like[k].size
        out[k] = flat[off:off + size].reshape(like[k].shape)
        off += size + (-size % PACK_ALIGN)
    return out


def _as_rows(a):
    return a.reshape(-1, a.shape[-1])


def kernel(x, norm_mix_g, w_in, conv_w, conv_b, lru_w_a, lru_b_a, lru_w_x, lru_b_x, lru_lambda, sgu_ln_g, sgu_ln_b, sgu_w_s, sgu_b_s, w_branch_a, w_branch_b, w_out, norm_ffn_g, w_up, w_down, final_norm_g, loss_target, m_norm_mix_g, m_w_in, m_conv_w, m_conv_b, m_lru_w_a, m_lru_b_a, m_lru_w_x, m_lru_b_x, m_lru_lambda, m_sgu_ln_g, m_sgu_ln_b, m_sgu_w_s, m_sgu_b_s, m_w_branch_a, m_w_branch_b, m_w_out, m_norm_ffn_g, m_w_up, m_w_down, m_final_norm_g, v_norm_mix_g, v_w_in, v_conv_w, v_conv_b, v_lru_w_a, v_lru_b_a, v_lru_w_x, v_lru_b_x, v_lru_lambda, v_sgu_ln_g, v_sgu_ln_b, v_sgu_w_s, v_sgu_b_s, v_w_branch_a, v_w_branch_b, v_w_out, v_norm_ffn_g, v_w_up, v_w_down, v_final_norm_g):
    w = dict(norm_mix_g=norm_mix_g, w_in=w_in, conv_w=conv_w, conv_b=conv_b, lru_w_a=lru_w_a, lru_b_a=lru_b_a,
             lru_w_x=lru_w_x, lru_b_x=lru_b_x, lru_lambda=lru_lambda, sgu_ln_g=sgu_ln_g, sgu_ln_b=sgu_ln_b,
             sgu_w_s=sgu_w_s, sgu_b_s=sgu_b_s, w_branch_a=w_branch_a, w_branch_b=w_branch_b, w_out=w_out,
             norm_ffn_g=norm_ffn_g, w_up=w_up, w_down=w_down, final_norm_g=final_norm_g)
    m = dict(norm_mix_g=m_norm_mix_g, w_in=m_w_in, conv_w=m_conv_w, conv_b=m_conv_b, lru_w_a=m_lru_w_a,
             lru_b_a=m_lru_b_a, lru_w_x=m_lru_w_x, lru_b_x=m_lru_b_x, lru_lambda=m_lru_lambda,
             sgu_ln_g=m_sgu_ln_g, sgu_ln_b=m_sgu_ln_b, sgu_w_s=m_sgu_w_s, sgu_b_s=m_sgu_b_s,
             w_branch_a=m_w_branch_a, w_branch_b=m_w_branch_b, w_out=m_w_out, norm_ffn_g=m_norm_ffn_g,
             w_up=m_w_up, w_down=m_w_down, final_norm_g=m_final_norm_g)
    v = dict(norm_mix_g=v_norm_mix_g, w_in=v_w_in, conv_w=v_conv_w, conv_b=v_conv_b, lru_w_a=v_lru_w_a,
             lru_b_a=v_lru_b_a, lru_w_x=v_lru_w_x, lru_b_x=v_lru_b_x, lru_lambda=v_lru_lambda,
             sgu_ln_g=v_sgu_ln_g, sgu_ln_b=v_sgu_ln_b, sgu_w_s=v_sgu_w_s, sgu_b_s=v_sgu_b_s,
             w_branch_a=v_w_branch_a, w_branch_b=v_w_branch_b, w_out=v_w_out, norm_ffn_g=v_norm_ffn_g,
             w_up=v_w_up, w_down=v_w_down, final_norm_g=v_final_norm_g)
    core = lax.axis_index("c")
    chip = 2 * lax.axis_index("x") + lax.axis_index("y")
    sel = jnp.stack([core, 1 - core, chip, 2 * chip + core]).astype(jnp.int32)
    this_core, other_core, this_chip = ("sel", 0), ("sel", 1), ("sel", 2)
    sds = jax.ShapeDtypeStruct

    ts = TOKEN_TILE

    def after_all(arrays):
        return jnp.stack([a[(0,) * a.ndim].astype(F32) for a in arrays])

    halves ={k: (w[k].shape[1] // 2, w[k].shape[2]) for k in BIG}

    def half_view(k, a):
        return a.reshape((2 * N_QUARTERS,) + halves[k])

    def full_view(k, a):
        r2, cols = halves[k]
        if k in ("w_in", "w_up"):
            return a.reshape(1, N_QUARTERS, 2 * r2, cols)
        return a.reshape(1, 2 * N_QUARTERS * r2, cols)

    layer_bufs = [{}, {}]

    def cast_weights(k, after):
        _, r, cols = w[k].shape
        w4 = w[k].reshape(DEPTH, 1, r, cols)
        outs = _ew_call(lambda a, b: (a, b), "cast_weights", [(w4, (0, 0)), (w4, (1, 0))],
                        [(sds((1, N_QUARTERS, r, cols), BF), (0, this_chip))] * DEPTH, 1, sel, after=after)
        for l in range(DEPTH):
            layer_bufs[l][k] = half_view(k, outs[l])

    conv_buf = lax.dynamic_update_slice_in_dim(
        jnp.zeros((DEPTH, N_QUARTERS) + conv_w.shape[1:], F32), conv_w[:, None], chip, axis=1)
    sm = {k: w[k] for k in SMALL}
    sm["conv_w"] = _gather_call([conv_buf])[0].transpose(0, 2, 1, 3).reshape(DEPTH, CONV_WIDTH, D_RNN)

    def gather_start(tag, l, keys, after):
        bufs = [layer_bufs[l][k] for k in keys]
        return _exchange_start(f"gather_start_{tag}", bufs, _gather_copies, 3 * len(keys), after)

    def gather_finish(tag, keys, started, after):
        send_sems, recv_sems, thru, _ = started
        landed = _exchange_wait(f"gather_wait_{tag}", send_sems, recv_sems, thru, _gather_copies, after)
        landed = _sibling_inplace_call("gather_forward", landed, _gather_forward_slabs, 3 * len(keys))
        return {k: full_view(k, a) for k, a in zip(keys, landed)}

    first, rest = ("w_in",), tuple(k for k in BIG if k != "w_in")
    cast_weights("w_in", None)
    started_a = gather_start("0a", 0, first, sm["conv_w"])
    for k in rest:
        cast_weights(k, started_a[3])
    started_b = gather_start("0b", 0, rest, started_a[3])
    started_c = gather_start("1a", 1, first, started_b[3])
    started_d = gather_start("1b", 1, rest, started_c[3])

    def rest_arrives(tag, started):
        state = {}

        def hook(after):
            landed = _exchange_wait(f"gather_wait_{tag}", started[0], started[1], started[2], _gather_copies, after)
            state["forward"] = _exchange_start(f"forward_start_{tag}", landed, _forward_copies, 3 * len(rest), after)
            return state["forward"][3][0, 0]

        def finish(after):
            send_sems, recv_sems, thru, _ = state["forward"]
            done = _exchange_wait(f"forward_wait_{tag}", send_sems, recv_sems, thru, _forward_copies, after)
            return {k: full_view(k, a) for k, a in zip(rest, done)}

        return hook, finish

    p0, p1 = _layer_small(sm, 0, sel[0:1]), _layer_small(sm, 1, sel[0:1])
    h0 = _norm_call(x[0], p0["g1"], ts)
    ready = after_all([started_d[3], h0] + [p[k] for p in (p0, p1) for k in ("wa", "wx", "wm")])
    big0 = gather_finish("0a", first, started_a, ready)
    hook, finish = rest_arrives("0b", started_b)
    sv0 = _layer_fwd_mix(x[0], big0, p0, ts, h0, hook)
    big0.update(finish(sv0["yb_pre"]))
    x_mid = _layer_fwd_out(sv0, big0, ts)
    big1 = gather_finish("1a", first, started_c, x_mid)
    hook, finish = rest_arrives("1b", started_d)
    sv1 = _layer_fwd_mix(x_mid, big1, p1, ts, None, hook)
    big1.update(finish(sv1["yb_pre"]))
    x_out = _layer_fwd_out(sv1, big1, ts)
    dx, loss, dgf = _loss_call(x_out, loss_target[0], final_norm_g.reshape(1, -1), ts)

    def pair_start(tag, gb, after):
        sends = [gb[k][1] for k in gb]
        zones = [lax.empty(a.shape, BF) for a in sends]
        return _exchange_start(f"pair_start_{tag}", sends + zones, _sibling_copies, len(sends), after)

    def reduce_start(tag, gb, after, pair=None):
        keys = tuple(gb)
        if pair is None:
            from_sibling = _sibling_send_call([gb[k][1] for k in keys])
        else:
            done = _exchange_wait(f"pair_wait_{tag}", pair[0], pair[1], pair[2], _sibling_copies, after)
            from_sibling = done[len(keys):]
        sums = [
            _ew_call(lambda a, b: (a + b.astype(F32),), "pair_sum", [(gb[k][0][None], (0, "g")), (r[None], (0, "g"))],
                     [(sds((1,) + r.shape, BF), (0, "g"))], N_QUARTERS)[0][0]
            for k, r in zip(keys, from_sibling)]
        zones = [lax.empty((3,) + a.shape[1:], BF) for a in sums]
        started = _exchange_start(f"reduce_start_{tag}", sums + zones, _owner_copies, 3 * len(keys), after)
        return keys, started

    def reduce_finish(tag, l, keys_started, after, reduced):
        keys, (send_sems, recv_sems, thru, _) = keys_started
        done = _exchange_wait(f"reduce_wait_{tag}", send_sems, recv_sems, thru, _owner_copies, after)
        sums, zones = done[:len(keys)], done[len(keys):]
        for i, k in enumerate(keys):
            r2, cols = halves[k]
            reduced[k] = _ew_call(
                lambda a, b, c, d: (((a.astype(F32) + b.astype(F32)) + c.astype(F32)) + d.astype(F32),),
                "quarter_sum", [(sums[i][None], (0, this_chip))] + [(zones[i][None], (0, j)) for j in range(3)],
                [(sds((DEPTH, 2, r2, cols), F32), (l, this_core))], 1, sel, into=reduced.get(k))[0]

    def behind(params, key, started):
        return dict(params, **{key: params[key] + started[1][3][0, 0]})

    dx1, gb_ffn, gs1 = _layer_bwd_ffn(dx, sv1, big1, ts)
    merge_out, gb_merge = _layer_bwd_merge(dx1, sv1, big1, ts)
    dx_mid, gb_in, gs1_mix = _layer_bwd_branches(dx1, merge_out, sv1, big1, lru_lambda[1], ts)
    gb_1 = {**gb_ffn, **gb_merge, **gb_in}
    pair_1 = pair_start("1", gb_1, dx_mid)
    sv0["p"] = behind(sv0["p"], "g2", (None, pair_1))
    dx1, gb_ffn, gs0 = _layer_bwd_ffn(dx_mid, sv0, big0, ts)
    exchange_1 = reduce_start("1", gb_1, dx1, pair_1)
    exchange_0a = reduce_start("0a", gb_ffn, exchange_1[1][3])
    merge_out, gb_merge = _layer_bwd_merge(dx1, sv0, big0, ts, exchange_0a[1][3])
    exchange_0b = reduce_start("0b", gb_merge, exchange_0a[1][3])
    sv0["p"] = behind(sv0["p"], "lg", exchange_0b)
    grad_x, gb_in, gs0_mix = _layer_bwd_branches(dx1, merge_out, sv0, big0, lru_lambda[0], ts)
    exchange_0c = reduce_start("0c", gb_in, exchange_0b[1][3])
    layer_gs = [{**gs0, **gs0_mix}, {**gs1, **gs1_mix}]
    gs = {k: jnp.stack([g[k] for g in layer_gs]) for k in layer_gs[0]}
    gs["final_norm_g"] = dgf[0]
    gs["loss"] = loss[0, 0:1]

    me = ("sel", 3)
    piece = (1, N_DEVICES, SMALL_ROWS, 128)
    packed = _pack_small(gs).reshape(piece)
    scatter = _exchange_start("small_scatter_start", [packed[0], lax.empty(piece[1:], F32)], _small_scatter_copies,
                              N_DEVICES - 1, exchange_0c[1][3])
    reduced = {}
    reduce_finish("1", 1, exchange_1, scatter[3], reduced)
    reduce_finish("0a", 0, exchange_0a, reduced["w_in"], reduced)
    reduce_finish("0b", 0, exchange_0b, reduced["w_down"], reduced)

    def swap_slabs(ref, c, i):
        layers = (1,) if BIG[i] == "w_in" else range(DEPTH)
        return [(ref.at[l, c], ref.at[l, 1 - c]) for l in layers]

    swapped = dict(zip(BIG, _sibling_inplace_call("grads_swap_halves", [reduced[k] for k in BIG], swap_slabs,
                                                  DEPTH * len(BIG) - 1)))

    def adamw_layers(k, grad, layer, into, after=None):
        if layer is None:
            views = [_as4(_as_rows(a)) for a in (w[k], grad, m[k], v[k])]
            idx = (0, 0)
        else:
            views = [a.reshape((1,) + w[k].shape) for a in (w[k], grad, m[k], v[k])]
            idx = (0, layer)
        return _ew_call(_adamw, "adamw_big", [(a, idx) for a in views], [(sds(views[0].shape, F32), idx)] * 3,
                        into=into, after=after)

    updated, last_update = {}, None
    for k in BIG:
        updated[k] = adamw_layers(k, swapped[k], 1 if k == "w_in" else None, None, last_update)
        last_update = updated[k][0]
    scattered = _exchange_wait("small_scatter_wait", scatter[0], scatter[1], scatter[2], _small_scatter_copies,
                               last_update)
    summed = _ew_call(
        lambda *parts: (functools.reduce(lambda a, b: a + b, parts),), "small_sum",
        [(scattered[0][None], (0, me))]
        + [(scattered[1][None], (0, lambda g, s, k=k: s[3] ^ k)) for k in range(1, N_DEVICES)],
        [(sds(piece, F32), (0, me))], 1, sel)[0]
    spread = _exchange_start("small_spread_start", [summed[0]], _small_spread_copies, N_DEVICES - 1, summed)
    reduced["w_in"] = swapped["w_in"]
    reduce_finish("0c", 0, exchange_0c, spread[3], reduced)
    last = _sibling_inplace_call("grads_swap_last", [reduced["w_in"]],
                                 lambda ref, c, i: [(ref.at[0, c], ref.at[0, 1 - c])], 1)[0]
    swapped["w_in"] = last
    updated["w_in"] = adamw_layers("w_in", last, 0, updated["w_in"])
    grads_big = {k: swapped[k].reshape(w[k].shape) for k in BIG}
    delta, new_m, new_v = ({k: updated[k][j].reshape(w[k].shape) for k in BIG} for j in range(3))
    gathered_small = _exchange_wait("small_spread_wait", spread[0], spread[1], spread[2], _small_spread_copies,
                                    updated["w_in"][0])[0]

    like = {k: jax.ShapeDtypeStruct(sm[k].shape, F32) for k in SMALL}
    like["loss"] = jax.ShapeDtypeStruct((1,), F32)
    grads_small = _unpack_small(gathered_small, like)
    total = grads_small.pop("loss")[0]
    conv_q = grads_small["conv_w"].reshape(DEPTH, CONV_WIDTH, N_QUARTERS, D_RNN // N_QUARTERS)
    grads_small["conv_w"] = lax.dynamic_index_in_dim(conv_q, chip, axis=2, keepdims=False)
    outs = _small_adamw_call(*[[_as_rows(d[k]) for k in SMALL] for d in (w, grads_small, m, v)])
    for d, o in zip((delta, new_m, new_v), outs):
        for k, a in zip(SMALL, o):
            d[k] = a.reshape(w[k].shape)

    grads = {**grads_big, **grads_small}
    return (total, grad_x[None], *[grads[k] for k in WEIGHTS], *[delta[k] for k in WEIGHTS],
            *[new_m[k] for k in WEIGHTS], *[new_v[k] for k in WEIGHTS])
```

```python
import functools
import math

import jax
import jax.numpy as jnp
from jax import lax
from jax.experimental import pallas as pl
from jax.experimental.pallas import tpu as pltpu

F32 = jnp.float32
BF = jnp.bfloat16

DEPTH = 2
D_MODEL = 1024
D_RNN = 1280
D_SGU = 1024
D_FF = 4096
D_IN = 2 * D_RNN + 2 * D_SGU + 2 * D_MODEL
N_QUARTERS = 4
Q_IN = D_IN // N_QUARTERS
Q_FF = D_FF // N_QUARTERS
RNN_HEADS = 20
RNN_HEAD_DIM = 64
LRU_GROUP = 256
N_LRU_GROUPS = D_RNN // LRU_GROUP
HEADS_PER_GROUP = LRU_GROUP // RNN_HEAD_DIM
CONV_WIDTH = 4
LRU_C = 8.0
SGU_GROUPS = 8
SGU_BLOCK = 128
CHUNK = 64
EPS = 1e-6

ADAM_LR = 0.001
ADAM_B1 = 0.9
ADAM_B2 = 0.999
ADAM_EPS = 1e-08
ADAM_WD = 0.01
ADAM_STEP = 10

SUBLANES = 8
TOKEN_TILE = 512
VMEM_LIMIT_BYTES = 56 * 1024 * 1024

MESH = pl.DeviceIdType.MESH


def _params(semantics=None, vmem=True, **kw):
    return pltpu.CompilerParams(
        dimension_semantics=semantics,
        vmem_limit_bytes=VMEM_LIMIT_BYTES if vmem else None,
        **kw,
    )


def _dot(a, b):
    return jnp.dot(a, b, preferred_element_type=F32)


def _dot_nt(a, b):
    return lax.dot_general(a, b, (((1,), (1,)), ((), ())), preferred_element_type=F32)


def _dot_tn(a, b):
    return lax.dot_general(a, b, (((0,), (0,)), ((), ())), preferred_element_type=F32)


_GELU_C = math.sqrt(2.0 / math.pi)
_GELU_A = 0.044715


def _gelu(x):
    return 0.5 * x * (1.0 + jnp.tanh(_GELU_C * (x + _GELU_A * x * x * x)))


def _gelu_and_grad(x):
    x2 = x * x
    t = jnp.tanh(_GELU_C * (x + _GELU_A * x2 * x))
    du = _GELU_C * (1.0 + 3.0 * _GELU_A * x2)
    return 0.5 * x * (1.0 + t), 0.5 * (1.0 + t) + 0.5 * x * (1.0 - t * t) * du


def _rms_stats(x):
    return lax.rsqrt(jnp.mean(x * x, axis=-1, keepdims=True) + EPS)


def _rms_bwd(dy, x, g):
    rs = _rms_stats(x)
    n = x * rs
    dn = dy * g
    dx = rs * (dn - n * jnp.mean(dn * n, axis=-1, keepdims=True))
    return dx, dy * n


def _row_sum(x):
    return jnp.sum(x, axis=0, keepdims=True)


def _tile_spec(ts, width, col=0):
    return pl.BlockSpec((ts, width), lambda i, col=col: (i, col))


def _full_spec(shape):
    zeros = (0,) * len(shape)
    return pl.BlockSpec(shape, lambda *_: zeros)


def _layer_spec(w, layer):
    zeros = (0,) * (w.ndim - 1)
    return pl.BlockSpec((None,) + tuple(w.shape[1:]), lambda *_: (layer,) + zeros)


def _norm_call(x, g, ts):
    s = x.shape[0]

    def body(x_ref, g_ref, h_ref):
        xv = x_ref[...]
        h_ref[...] = (xv * _rms_stats(xv) * g_ref[...]).astype(BF)

    return pl.pallas_call(
        body, name="norm_fwd", grid=(s // ts,),
        in_specs=[_tile_spec(ts, D_MODEL), _full_spec((1, D_MODEL))],
        out_specs=_tile_spec(ts, D_MODEL),
        out_shape=jax.ShapeDtypeStruct((s, D_MODEL), BF),
        compiler_params=_params(("parallel",)),
    )(x, g)


def _inproj_call(h, w_in, layer, ts):
    s = h.shape[0]

    def body(h_ref, w_ref, o_ref):
        o_ref[...] = _dot(h_ref[...], w_ref[...]).astype(BF)

    return pl.pallas_call(
        body, name="inproj_fwd", grid=(N_QUARTERS, s // ts),
        in_specs=[
            pl.BlockSpec((ts, D_MODEL), lambda q, i: (i, 0)),
            pl.BlockSpec((None, None, D_MODEL, Q_IN), lambda q, i: (layer, q, 0, 0)),
        ],
        out_specs=pl.BlockSpec((ts, Q_IN), lambda q, i: (i, q)),
        out_shape=jax.ShapeDtypeStruct((s, D_IN), BF),
        compiler_params=_params(("parallel", "parallel")),
    )(h, w_in)


def _shift_down(x, tail, s):
    xr = pltpu.roll(x, s, 0)
    tr = pltpu.roll(tail, s, 0)
    row = lax.broadcasted_iota(jnp.int32, tail.shape, 0)
    top = jnp.where(row < s, tr, xr[0:SUBLANES])
    return jnp.concatenate([top, xr[SUBLANES:]], axis=0)


def _shift_up(x, head, s):
    t = x.shape[0]
    xr = pltpu.roll(x, t - s, 0)
    hr = pltpu.roll(head, SUBLANES - s, 0)
    row = lax.broadcasted_iota(jnp.int32, head.shape, 0)
    bottom = jnp.where(row >= SUBLANES - s, hr, xr[t - SUBLANES:])
    return jnp.concatenate([xr[: t - SUBLANES], bottom], axis=0)


def _conv_fwd(x, tail, cw_ref, cb_ref):
    out = cb_ref[...] + cw_ref[CONV_WIDTH - 1:CONV_WIDTH, :] * x
    for s in range(1, CONV_WIDTH):
        k = CONV_WIDTH - 1 - s
        out = out + cw_ref[k:k + 1, :] * _shift_down(x, tail, s)
    return out


def _group_dot(x_bf, w_ref, dot):
    cols = [dot(x_bf[:, g * LRU_GROUP:(g + 1) * LRU_GROUP], w_ref[g]) for g in range(N_LRU_GROUPS)]
    return jnp.concatenate(cols, axis=1)


def _lru_gates(xr, wa_ref, wx_ref, ba_ref, bx_ref, sp_ref):
    xb = xr.astype(BF)
    r = jax.nn.sigmoid(_group_dot(xb, wa_ref, _dot) + ba_ref[...])
    i = jax.nn.sigmoid(_group_dot(xb, wx_ref, _dot) + bx_ref[...])
    log_a = (-LRU_C * r) * sp_ref[...]
    a = jnp.exp(log_a)
    nrm2 = -jnp.tanh(log_a) * (a * a + 1.0)
    inv_nrm = lax.rsqrt(jnp.maximum(nrm2, 1e-36))
    return r, i, a, nrm2 * inv_nrm, inv_nrm


def _linear_scan(a, b, carry, al_ref, bl_ref, h_ref, reverse):
    t, c = a.shape
    rowm = lax.broadcasted_iota(jnp.int32, (t, c), 0) & (SUBLANES - 1)
    for d in (1, 2, 4):
        if reverse:
            keep, sh = rowm < SUBLANES - d, t - d
        else:
            keep, sh = rowm >= d, d
        a_sh = jnp.where(keep, pltpu.roll(a, sh, 0), 1.0)
        b_sh = jnp.where(keep, pltpu.roll(b, sh, 0), 0.0)
        b = a * b_sh + b
        a = a * a_sh
    al_ref[...] = a
    bl_ref[...] = b
    groups = t // SUBLANES

    def step(j, state):
        jj = groups - 1 - j if reverse else j
        off = pl.multiple_of(jj * SUBLANES, SUBLANES)
        rows = bl_ref[pl.ds(off, SUBLANES), :] + al_ref[pl.ds(off, SUBLANES), :] * state
        h_ref[pl.ds(off, SUBLANES), :] = rows
        last = rows[0:1, :] if reverse else rows[SUBLANES - 1:SUBLANES, :]
        return jnp.broadcast_to(last, (SUBLANES, c))

    out = lax.fori_loop(0, groups, step, jnp.broadcast_to(carry, (SUBLANES, c)))
    return out[0:1, :]


def _rnn_fwd_call(proj, wa, wx, ba, bx, sp, cw, cb, ts):
    s = proj.shape[0]

    def body(xg_ref, wa_ref, wx_ref, ba_ref, bx_ref, sp_ref, cw_ref, cb_ref, xr_ref, hr_ref, ya_ref,
             tail_sc, carry_sc, al_sc, bl_sc, h_sc):
        @pl.when(pl.program_id(0) == 0)
        def _():
            tail_sc[...] = jnp.zeros_like(tail_sc)
            carry_sc[...] = jnp.zeros_like(carry_sc)

        x = xg_ref[:, :D_RNN].astype(F32)
        g = xg_ref[:, D_RNN:]
        xr = _conv_fwd(x, tail_sc[...], cw_ref, cb_ref)
        tail_sc[...] = x[ts - SUBLANES:, :]
        xr_ref[...] = xr.astype(BF)
        _, i, a, nrm, _ = _lru_gates(xr, wa_ref, wx_ref, ba_ref, bx_ref, sp_ref)
        carry_sc[...] = _linear_scan(a, nrm * (i * xr), carry_sc[...], al_sc, bl_sc, h_sc, False)
        h = h_sc[...]
        hr_ref[...] = h.astype(BF)
        ya_ref[...] = (h * _gelu(g)).astype(BF)

    gw = (N_LRU_GROUPS, LRU_GROUP, LRU_GROUP)
    return pl.pallas_call(
        body, name="rnn_fwd", grid=(s // ts,),
        in_specs=[_tile_spec(ts, 2 * D_RNN), _full_spec(gw), _full_spec(gw),
                  _full_spec((1, D_RNN)), _full_spec((1, D_RNN)), _full_spec((1, D_RNN)),
                  _full_spec((CONV_WIDTH, D_RNN)), _full_spec((1, D_RNN))],
        out_specs=[_tile_spec(ts, D_RNN)] * 3,
        out_shape=[jax.ShapeDtypeStruct((s, D_RNN), BF)] * 3,
        scratch_shapes=[pltpu.VMEM((SUBLANES, D_RNN), F32), pltpu.VMEM((1, D_RNN), F32),
                        pltpu.VMEM((ts, D_RNN), F32), pltpu.VMEM((ts, D_RNN), F32),
                        pltpu.VMEM((ts, D_RNN), F32)],
        compiler_params=_params(("arbitrary",)),
    )(proj, wa, wx, ba, bx, sp, cw, cb)


def _layernorm_fwd(x):
    mu = jnp.mean(x, axis=-1, keepdims=True)
    xc = x - mu
    rstd = lax.rsqrt(jnp.mean(xc * xc, axis=-1, keepdims=True) + EPS)
    return xc * rstd, rstd


def _sgu_mix(vn_bf, wm_ref, bsb_ref, ts):
    rows = []
    for blk in range(ts // SGU_BLOCK):
        r0 = blk * SGU_BLOCK
        cols = [
            _dot(wm_ref[g], vn_bf[r0:r0 + SGU_BLOCK, g * SGU_BLOCK:(g + 1) * SGU_BLOCK]) + bsb_ref[g]
            for g in range(SGU_GROUPS)
        ]
        rows.append(jnp.concatenate(cols, axis=1))
    return jnp.concatenate(rows, axis=0)


def _sgu_fwd_call(proj, wm, bsb, lg, lb, ts):
    s = proj.shape[0]

    def body(uv_ref, wm_ref, bsb_ref, lg_ref, lb_ref, yb_ref):
        gu = _gelu(uv_ref[:, :D_SGU])
        gv = _gelu(uv_ref[:, D_SGU:2 * D_SGU]).astype(F32)
        nh, _ = _layernorm_fwd(gv)
        vn = (nh * lg_ref[...] + lb_ref[...]).astype(BF)
        yb_ref[...] = (gu * _sgu_mix(vn, wm_ref, bsb_ref, ts)).astype(BF)

    sw = (SGU_GROUPS, SGU_BLOCK, SGU_BLOCK)
    return pl.pallas_call(
        body, name="sgu_fwd", grid=(s // ts,),
        in_specs=[_tile_spec(ts, 2 * D_RNN, 1), _full_spec(sw), _full_spec(sw),
                  _full_spec((1, D_SGU)), _full_spec((1, D_SGU))],
        out_specs=_tile_spec(ts, D_SGU),
        out_shape=jax.ShapeDtypeStruct((s, D_SGU), BF),
        compiler_params=_params(("parallel",)),
    )(proj, wm, bsb, lg, lb)


_GATE_COL0 = (2 * D_RNN + 2 * D_SGU) // 512


def _gate_specs(ts):
    return [_tile_spec(ts, 512, _GATE_COL0 + j) for j in range(4)]


def _merge_call(x, proj, ya_pre, yb_pre, w_ba, w_bb, w_out, g2, layer, ts):
    s = x.shape[0]

    def body(x_ref, ga0, ga1, gb0, gb1, ya_ref, yb_ref, wa_ref, wb_ref, wo_ref, g2_ref,
             x1_ref, yao_ref, ybo_ref, mg_ref, h2_ref):
        ya = _dot(ya_ref[...], wa_ref[...])
        yb = _dot(yb_ref[...], wb_ref[...])
        sa = jax.nn.sigmoid(jnp.concatenate([ga0[...], ga1[...]], axis=1).astype(F32))
        sb = jax.nn.sigmoid(jnp.concatenate([gb0[...], gb1[...]], axis=1).astype(F32))
        merged = (sa * ya + sb * yb).astype(BF)
        x1 = x_ref[...] + _dot(merged, wo_ref[...])
        x1_ref[...] = x1
        yao_ref[...] = ya.astype(BF)
        ybo_ref[...] = yb.astype(BF)
        mg_ref[...] = merged
        h2_ref[...] = (x1 * _rms_stats(x1) * g2_ref[...]).astype(BF)

    act = jax.ShapeDtypeStruct((s, D_MODEL), BF)
    return pl.pallas_call(
        body, name="merge_fwd", grid=(s // ts,),
        in_specs=[_tile_spec(ts, D_MODEL)] + _gate_specs(ts) + [
            _tile_spec(ts, D_RNN), _tile_spec(ts, D_SGU),
            _layer_spec(w_ba, layer), _layer_spec(w_bb, layer), _layer_spec(w_out, layer),
            _full_spec((1, D_MODEL))],
        out_specs=[_tile_spec(ts, D_MODEL)] * 5,
        out_shape=[jax.ShapeDtypeStruct((s, D_MODEL), F32), act, act, act, act],
        compiler_params=_params(("parallel",)),
    )(x, proj, proj, proj, proj, ya_pre, yb_pre, w_ba, w_bb, w_out, g2)


def _ffn_call(x1, h2, w_up, w_down, layer, ts):
    s = x1.shape[0]

    def body(x1_ref, h2_ref, wu_ref, wd_ref, x2_ref, p_ref):
        h2v = h2_ref[...]
        acc = x1_ref[...]
        for q in range(N_QUARTERS):
            p = _dot(h2v, wu_ref[q])
            p_ref[:, q * Q_FF:(q + 1) * Q_FF] = p.astype(BF)
            f = jnp.square(jnp.maximum(p, 0.0)).astype(BF)
            acc = acc + _dot(f, wd_ref[q * Q_FF:(q + 1) * Q_FF, :])
        x2_ref[...] = acc

    return pl.pallas_call(
        body, name="ffn_fwd", grid=(s // ts,),
        in_specs=[_tile_spec(ts, D_MODEL), _tile_spec(ts, D_MODEL),
                  pl.BlockSpec((None, N_QUARTERS, D_MODEL, Q_FF), lambda i: (layer, 0, 0, 0)),
                  pl.BlockSpec((None, D_FF, D_MODEL), lambda i: (layer, 0, 0))],
        out_specs=[_tile_spec(ts, D_MODEL), _tile_spec(ts, D_FF)],
        out_shape=[jax.ShapeDtypeStruct((s, D_MODEL), F32), jax.ShapeDtypeStruct((s, D_FF), BF)],
        compiler_params=_params(("parallel",)),
    )(x1, h2, w_up, w_down)


def _loss_call(x, target, gf, ts):
    s = x.shape[0]

    def body(x_ref, t_ref, g_ref, dx_ref, loss_ref, dg_ref):
        @pl.when(pl.program_id(0) == 0)
        def _():
            loss_ref[...] = jnp.zeros_like(loss_ref)
            dg_ref[...] = jnp.zeros_like(dg_ref)

        xv = x_ref[...]
        gv = g_ref[...]
        err = xv * _rms_stats(xv) * gv - t_ref[...]
        part = 0.5 * jnp.sum(jnp.mean(err * err, axis=-1, keepdims=True), axis=0, keepdims=True)
        loss_ref[...] += jnp.broadcast_to(part, loss_ref.shape)
        dx, dg = _rms_bwd(err * (1.0 / D_MODEL), xv, gv)
        dx_ref[...] = dx
        dg_ref[...] += _row_sum(dg)

    return pl.pallas_call(
        body, name="loss_head", grid=(s // ts,),
        in_specs=[_tile_spec(ts, D_MODEL), _tile_spec(ts, D_MODEL), _full_spec((1, D_MODEL))],
        out_specs=[_tile_spec(ts, D_MODEL), _full_spec((1, 128)), _full_spec((1, D_MODEL))],
        out_shape=[jax.ShapeDtypeStruct((s, D_MODEL), F32), jax.ShapeDtypeStruct((1, 128), F32),
                   jax.ShapeDtypeStruct((1, D_MODEL), F32)],
        compiler_params=_params(("arbitrary",)),
    )(x, target, gf)


def _ffn_bwd_call(dx2, p, x1, g2, w_up, w_down, layer, ts):
    s = dx2.shape[0]

    def body(dx2_ref, p_ref, x1_ref, g2_ref, wu_ref, wd_ref, dx1_ref, dp_ref, dg_ref):
        @pl.when(pl.program_id(0) == 0)
        def _():
            dg_ref[...] = jnp.zeros_like(dg_ref)

        dx2v = dx2_ref[...]
        dyb = dx2v.astype(BF)
        dh2 = jnp.zeros((ts, D_MODEL), F32)
        for q in range(N_QUARTERS):
            cols = slice(q * Q_FF, (q + 1) * Q_FF)
            df = _dot_nt(dyb, wd_ref[cols, :])
            dp = (df * (2.0 * jnp.maximum(p_ref[:, cols].astype(F32), 0.0))).astype(BF)
            dp_ref[:, cols] = dp
            dh2 = dh2 + _dot_nt(dp, wu_ref[q])
        dx, dg = _rms_bwd(dh2, x1_ref[...], g2_ref[...])
        dx1_ref[...] = dx2v + dx
        dg_ref[...] += _row_sum(dg)

    return pl.pallas_call(
        body, name="ffn_bwd", grid=(s // ts,),
        in_specs=[_tile_spec(ts, D_MODEL), _tile_spec(ts, D_FF), _tile_spec(ts, D_MODEL),
                  _full_spec((1, D_MODEL)),
                  pl.BlockSpec((None, N_QUARTERS, D_MODEL, Q_FF), lambda i: (layer, 0, 0, 0)),
                  pl.BlockSpec((None, D_FF, D_MODEL), lambda i: (layer, 0, 0))],
        out_specs=[_tile_spec(ts, D_MODEL), _tile_spec(ts, D_FF), _full_spec((1, D_MODEL))],
        out_shape=[jax.ShapeDtypeStruct((s, D_MODEL), F32), jax.ShapeDtypeStruct((s, D_FF), BF),
                   jax.ShapeDtypeStruct((1, D_MODEL), F32)],
        compiler_params=_params(("arbitrary",)),
    )(dx2, p, x1, g2, w_up, w_down)


def _merge_bwd_call(dx1, proj, ya, yb, w_ba, w_bb, w_out, layer, ts, after=None):
    s = dx1.shape[0]

    def body(dx1_ref, ga0, ga1, gb0, gb1, ya_ref, yb_ref, wa_ref, wb_ref, wo_ref, *rest):
        dya_ref, dyb_ref, dgate_ref, dyap_ref, dybp_ref = rest[-5:]
        dm = _dot_nt(dx1_ref[...].astype(BF), wo_ref[...])
        sa = jax.nn.sigmoid(jnp.concatenate([ga0[...], ga1[...]], axis=1).astype(F32))
        sb = jax.nn.sigmoid(jnp.concatenate([gb0[...], gb1[...]], axis=1).astype(F32))
        dya = (dm * sa).astype(BF)
        dyb = (dm * sb).astype(BF)
        dya_ref[...] = dya
        dyb_ref[...] = dyb
        dgate_ref[:, :D_MODEL] = (dm * ya_ref[...].astype(F32) * sa * (1.0 - sa)).astype(BF)
        dgate_ref[:, D_MODEL:] = (dm * yb_ref[...].astype(F32) * sb * (1.0 - sb)).astype(BF)
        dyap_ref[...] = _dot_nt(dya, wa_ref[...]).astype(BF)
        dybp_ref[...] = _dot_nt(dyb, wb_ref[...]).astype(BF)

    act = jax.ShapeDtypeStruct((s, D_MODEL), BF)
    return pl.pallas_call(
        body, name="merge_bwd", grid=(s // ts,),
        in_specs=[_tile_spec(ts, D_MODEL)] + _gate_specs(ts) + [
            _tile_spec(ts, D_MODEL), _tile_spec(ts, D_MODEL),
            _layer_spec(w_ba, layer), _layer_spec(w_bb, layer), _layer_spec(w_out, layer)]
        + ([] if after is None else [pl.BlockSpec(memory_space=pl.ANY)]),
        out_specs=[_tile_spec(ts, D_MODEL), _tile_spec(ts, D_MODEL), _tile_spec(ts, 2 * D_MODEL),
                   _tile_spec(ts, D_RNN), _tile_spec(ts, D_SGU)],
        out_shape=[act, act, jax.ShapeDtypeStruct((s, 2 * D_MODEL), BF),
                   jax.ShapeDtypeStruct((s, D_RNN), BF), jax.ShapeDtypeStruct((s, D_SGU), BF)],
        compiler_params=_params(("parallel",)),
    )(dx1, proj, proj, proj, proj, ya, yb, w_ba, w_bb, w_out, *([] if after is None else [after]))


def _sgu_bwd_call(dyb_pre, proj, wm, bsb, mask, lg, lb, ts):
    s = proj.shape[0]

    def body(dy_ref, uv_ref, wm_ref, bsb_ref, mask_ref, lg_ref, lb_ref,
             duv_ref, dws_ref, dbs_ref, dlg_ref, dlb_ref, dm_sc):
        step = pl.program_id(0)

        @pl.when(step == 0)
        def _():
            dws_ref[...] = jnp.zeros_like(dws_ref)
            dlg_ref[...] = jnp.zeros_like(dlg_ref)
            dlb_ref[...] = jnp.zeros_like(dlb_ref)
            dm_sc[...] = jnp.zeros_like(dm_sc)

        gu, dgu_du = _gelu_and_grad(uv_ref[:, :D_SGU])
        gv, dgv_dv = _gelu_and_grad(uv_ref[:, D_SGU:2 * D_SGU])
        nh, rstd = _layernorm_fwd(gv.astype(F32))
        lgv = lg_ref[...]
        vn = (nh * lgv + lb_ref[...]).astype(BF)
        dy = dy_ref[...].astype(F32)
        du = dy * _sgu_mix(vn, wm_ref, bsb_ref, ts) * dgu_du
        dmix = dy * gu
        dmix_bf = dmix.astype(BF)
        dm_acc = dm_sc[...]
        rows = []
        for blk in range(ts // SGU_BLOCK):
            r0 = blk * SGU_BLOCK
            dm_acc = dm_acc + dmix[r0:r0 + SGU_BLOCK, :]
            cols = []
            for g in range(SGU_GROUPS):
                c0 = g * SGU_BLOCK
                dmg = dmix_bf[r0:r0 + SGU_BLOCK, c0:c0 + SGU_BLOCK]
                cols.append(_dot_tn(wm_ref[g], dmg))
                dws_ref[g] += mask_ref[...] * _dot_nt(dmg, vn[r0:r0 + SGU_BLOCK, c0:c0 + SGU_BLOCK])
            rows.append(jnp.concatenate(cols, axis=1))
        dm_sc[...] = dm_acc
        dvn = jnp.concatenate(rows, axis=0)
        dlg_ref[...] += _row_sum(dvn * nh)
        dlb_ref[...] += _row_sum(dvn)
        dnh = dvn * lgv
        dgv = rstd * (dnh - jnp.mean(dnh, axis=-1, keepdims=True)
                      - nh * jnp.mean(dnh * nh, axis=-1, keepdims=True))
        duv_ref[:, :D_SGU] = du.astype(BF)
        duv_ref[:, D_SGU:] = (dgv * dgv_dv).astype(BF)

        @pl.when(step == pl.num_programs(0) - 1)
        def _():
            for g in range(SGU_GROUPS):
                dbs_ref[:, g:g + 1] = jnp.sum(
                    dm_acc[:, g * SGU_BLOCK:(g + 1) * SGU_BLOCK], axis=1, keepdims=True)

    sw = (SGU_GROUPS, SGU_BLOCK, SGU_BLOCK)
    return pl.pallas_call(
        body, name="sgu_bwd", grid=(s // ts,),
        in_specs=[_tile_spec(ts, D_SGU), _tile_spec(ts, 2 * D_RNN, 1), _full_spec(sw), _full_spec(sw),
                  _full_spec((SGU_BLOCK, SGU_BLOCK)), _full_spec((1, D_SGU)), _full_spec((1, D_SGU))],
        out_specs=[_tile_spec(ts, 2 * D_SGU), _full_spec(sw), _full_spec((SGU_BLOCK, SGU_GROUPS)),
                   _full_spec((1, D_SGU)), _full_spec((1, D_SGU))],
        out_shape=[jax.ShapeDtypeStruct((s, 2 * D_SGU), BF), jax.ShapeDtypeStruct(sw, F32),
                   jax.ShapeDtypeStruct((SGU_BLOCK, SGU_GROUPS), F32),
                   jax.ShapeDtypeStruct((1, D_SGU), F32), jax.ShapeDtypeStruct((1, D_SGU), F32)],
        scratch_shapes=[pltpu.VMEM((SGU_BLOCK, D_SGU), F32)],
        compiler_params=_params(("arbitrary",)),
    )(dyb_pre, proj, wm, bsb, mask, lg, lb)


_ROW_DBA, _ROW_DBX, _ROW_DSP, _ROW_DCB, _ROW_DCW = 0, 1, 2, 3, 4
_PREV_ROWS = 16


def _rnn_bwd_call(dya_pre, proj, xr_saved, hr, wa, wx, ba, bx, sp, cw, ts):
    s = proj.shape[0]
    nt = s // ts
    per = ts // _PREV_ROWS

    def tile(i):
        return nt - 1 - i

    def prev(i):
        return jnp.maximum(tile(i) * per - 1, 0)

    def body(dy_ref, xg_ref, xr_ref, hr_ref, hrp_ref, wa_ref, wx_ref, ba_ref, bx_ref, sp_ref,
             cw_ref, dxg_ref, dwa_ref, dwx_ref, vec_ref,
             lam_carry, a_first, dxr_head, al_sc, bl_sc, lam_sc):
        step = pl.program_id(0)

        @pl.when(step == 0)
        def _():
            dwa_ref[...] = jnp.zeros_like(dwa_ref)
            dwx_ref[...] = jnp.zeros_like(dwx_ref)
            vec_ref[...] = jnp.zeros_like(vec_ref)
            lam_carry[...] = jnp.zeros_like(lam_carry)
            a_first[...] = jnp.zeros_like(a_first)
            dxr_head[...] = jnp.zeros_like(dxr_head)

        has_prev = (step < nt - 1).astype(F32)
        x = xg_ref[:, :D_RNN].astype(F32)
        g = xg_ref[:, D_RNN:]
        h_tail =hrp_ref[_PREV_ROWS - SUBLANES:, :].astype(F32) * has_prev
        xr = xr_ref[...].astype(F32)
        r, i, a, nrm, inv_nrm = _lru_gates(xr, wa_ref, wx_ref, ba_ref, bx_ref, sp_ref)
        h = hr_ref[...].astype(F32)
        dy = dy_ref[...].astype(F32)
        gg, dgg = _gelu_and_grad(g)

        coef = _shift_up(a, jnp.broadcast_to(a_first[...], (SUBLANES, D_RNN)), 1)
        lam_carry[...] = _linear_scan(coef, dy * gg, lam_carry[...], al_sc, bl_sc, lam_sc, True)
        a_first[...] = a[0:1, :]
        lam = lam_sc[...]

        da = lam * _shift_down(h, h_tail, 1)
        dnrm = lam * (i * xr)
        di = lam * nrm * xr
        dlog_a = da * a - dnrm * (a * a) * inv_nrm
        spv = sp_ref[...]
        dza = (dlog_a * (-LRU_C * spv)) * (r * (1.0 - r))
        dzx = di * (i * (1.0 - i))
        vec_ref[_ROW_DSP:_ROW_DSP + 1, :] += _row_sum(dlog_a * (-LRU_C * r))
        vec_ref[_ROW_DBA:_ROW_DBA + 1, :] += _row_sum(dza)
        vec_ref[_ROW_DBX:_ROW_DBX + 1, :] += _row_sum(dzx)
        xb = xr.astype(BF)
        dza_bf = dza.astype(BF)
        dzx_bf = dzx.astype(BF)
        for grp in range(N_LRU_GROUPS):
            cols = slice(grp * LRU_GROUP, (grp + 1) * LRU_GROUP)
            dwa_ref[grp] += _dot_tn(xb[:, cols], dza_bf[:, cols])
            dwx_ref[grp] += _dot_tn(xb[:, cols], dzx_bf[:, cols])
        dxr = (lam * nrm * i + _group_dot(dza_bf, wa_ref, _dot_nt) + _group_dot(dzx_bf, wx_ref, _dot_nt))

        vec_ref[_ROW_DCB:_ROW_DCB + 1, :] += _row_sum(dxr)
        head = dxr_head[...]
        dx = cw_ref[CONV_WIDTH - 1:CONV_WIDTH, :] * dxr
        vec_ref[_ROW_DCW + 3:_ROW_DCW + 4, :] += _row_sum(dxr * x)
        for sft in range(1, CONV_WIDTH):
            k = CONV_WIDTH - 1 - sft
            ahead = _shift_up(dxr, head, sft)
            dx = dx + cw_ref[k:k + 1, :] * ahead
            vec_ref[_ROW_DCW + k:_ROW_DCW + k + 1, :] += _row_sum(ahead * x)
        dxr_head[...] = dxr[0:SUBLANES, :]
        dxg_ref[:, :D_RNN] = dx.astype(BF)
        dxg_ref[:, D_RNN:] = (dy * h * dgg).astype(BF)

    gw = (N_LRU_GROUPS, LRU_GROUP, LRU_GROUP)
    rev = lambda width: pl.BlockSpec((ts, width), lambda i: (tile(i), 0))
    return pl.pallas_call(
        body, name="rnn_bwd", grid=(nt,),
        in_specs=[rev(D_RNN), rev(2 * D_RNN), rev(D_RNN), rev(D_RNN),
                  pl.BlockSpec((_PREV_ROWS, D_RNN), lambda i: (prev(i), 0)),
                  _full_spec(gw), _full_spec(gw),
                  _full_spec((1, D_RNN)), _full_spec((1, D_RNN)), _full_spec((1, D_RNN)),
                  _full_spec((CONV_WIDTH, D_RNN))],
        out_specs=[rev(2 * D_RNN), _full_spec(gw), _full_spec(gw), _full_spec((SUBLANES, D_RNN))],
        out_shape=[jax.ShapeDtypeStruct((s, 2 * D_RNN), BF), jax.ShapeDtypeStruct(gw, F32),
                   jax.ShapeDtypeStruct(gw, F32), jax.ShapeDtypeStruct((SUBLANES, D_RNN), F32)],
        scratch_shapes=[pltpu.VMEM((1, D_RNN), F32), pltpu.VMEM((1, D_RNN), F32),
                        pltpu.VMEM((SUBLANES, D_RNN), F32),
                        pltpu.VMEM((ts, D_RNN), F32), pltpu.VMEM((ts, D_RNN), F32),
                        pltpu.VMEM((ts, D_RNN), F32)],
        compiler_params=_params(("arbitrary",)),
    )(dya_pre, proj, xr_saved, hr, hr, wa, wx, ba, bx, sp, cw)


def _inproj_bwd_call(dxg, duv, dgate, dx1, x, g1, w_in, layer, ts):
    s = x.shape[0]

    def body(dxg_ref, duv_ref, dgt_ref, dx1_ref, x_ref, g_ref, w_ref, dx_ref, dproj_ref, dg_ref):
        @pl.when(pl.program_id(0) == 0)
        def _():
            dg_ref[...] = jnp.zeros_like(dg_ref)

        dproj = jnp.concatenate([dxg_ref[...], duv_ref[...], dgt_ref[...]], axis=1)
        dproj_ref[...] = dproj
        dh = jnp.zeros((ts, D_MODEL), F32)
        for q in range(N_QUARTERS):
            dh = dh + _dot_nt(dproj[:, q * Q_IN:(q + 1) * Q_IN], w_ref[q])
        dx, dg = _rms_bwd(dh, x_ref[...], g_ref[...])
        dx_ref[...] = dx1_ref[...] + dx
        dg_ref[...] += _row_sum(dg)

    return pl.pallas_call(
        body, name="inproj_bwd", grid=(s // ts,),
        in_specs=[_tile_spec(ts, 2 * D_RNN), _tile_spec(ts, 2 * D_SGU), _tile_spec(ts, 2 * D_MODEL),
                  _tile_spec(ts, D_MODEL), _tile_spec(ts, D_MODEL), _full_spec((1, D_MODEL)),
                  pl.BlockSpec((None, N_QUARTERS, D_MODEL, Q_IN), lambda i: (layer, 0, 0, 0))],
        out_specs=[_tile_spec(ts, D_MODEL), _tile_spec(ts, D_IN), _full_spec((1, D_MODEL))],
        out_shape=[jax.ShapeDtypeStruct((s, D_MODEL), F32), jax.ShapeDtypeStruct((s, D_IN), BF),
                   jax.ShapeDtypeStruct((1, D_MODEL), F32)],
        compiler_params=_params(("arbitrary",)),
    )(dxg, duv, dgate, dx1, x, g1, w_in)


def _relu_sq(p):
    return jnp.square(jnp.maximum(p, 0))


def _wgrad_call(a, b, core, tm, tn, tk, col_blocked, name, a_fn=None):
    s, m = a.shape
    n = b.shape[1]
    r, cols = (m, n // N_QUARTERS) if col_blocked else (m // N_QUARTERS, n)
    r2 = r // 2
    per_tile = tm // r
    steps = s // tk

    def body(core_ref, a_ref, b_ref, keep_ref, send_ref, *acc):
        av = a_ref[...]
        if a_fn is not None:
            av = a_fn(av)
        prod = _dot_tn(av.astype(BF), b_ref[...].astype(BF))

        def emit(total):
            for h in range(2):
                @pl.when(core_ref[0] == h)
                def _():
                    for q in range(per_tile):
                        keep_ref[q] = total[q * r + h * r2:q * r + (h + 1) * r2]
                        send_ref[q] = total[q * r + (1 - h) * r2:q * r + (2 - h) * r2].astype(BF)

        if steps == 1:
            emit(prod)
        else:
            acc_ref, = acc
            step = pl.program_id(2)

            @pl.when(step == 0)
            def _():
                acc_ref[...] = prod

            @pl.when(jnp.logical_and(step > 0, step < steps - 1))
            def _():
                acc_ref[...] += prod

            @pl.when(step == steps - 1)
            def _():
                emit(acc_ref[...] + prod)

    if col_blocked:
        per_q = cols // tn
        out_spec = pl.BlockSpec((1, r2, tn), lambda i, j, k, c: (j // per_q, 0, j % per_q))
    else:
        out_spec = pl.BlockSpec((per_tile, r2, tn), lambda i, j, k, c: (i, 0, j))
    return pl.pallas_call(
        body, name=name,
        out_shape=[jax.ShapeDtypeStruct((N_QUARTERS, r2, cols), F32),
                   jax.ShapeDtypeStruct((N_QUARTERS, r2, cols), BF)],
        grid_spec=pltpu.PrefetchScalarGridSpec(
            num_scalar_prefetch=1, grid=(m // tm, n // tn, steps),
            in_specs=[pl.BlockSpec((tk, tm), lambda i, j, k, c: (k, i)),
                      pl.BlockSpec((tk, tn), lambda i, j, k, c: (k, j))],
            out_specs=[out_spec, out_spec],
            scratch_shapes=[] if steps == 1 else [pltpu.VMEM((tm, tn), F32)]),
        compiler_params=_params(("parallel", "parallel", "arbitrary")),
    )(core, a, b)


BIG = ("w_in", "w_up", "w_down", "w_branch_a", "w_branch_b", "w_out")


def _block_diag(w):
    w4 = w.reshape(N_LRU_GROUPS, HEADS_PER_GROUP, RNN_HEAD_DIM, RNN_HEAD_DIM)
    eye = jnp.eye(HEADS_PER_GROUP, dtype=w.dtype)
    return jnp.einsum("gjio,jk->gjiko", w4, eye).reshape(N_LRU_GROUPS, LRU_GROUP, LRU_GROUP)


def _block_diag_extract(d):
    d5 = d.reshape(N_LRU_GROUPS, HEADS_PER_GROUP, RNN_HEAD_DIM, HEADS_PER_GROUP, RNN_HEAD_DIM)
    blocks = [d5[:, j, :, j, :] for j in range(HEADS_PER_GROUP)]
    return jnp.stack(blocks, axis=1).reshape(RNN_HEADS, RNN_HEAD_DIM, RNN_HEAD_DIM)


def _sgu_mask():
    chunk = jnp.arange(SGU_BLOCK) // CHUNK
    return (chunk[:, None] >= chunk[None, :]).astype(F32)


def _layer_small(sm, l, core):
    row = lambda v: v.reshape(1, -1)
    return dict(
        core=core,
        g1=row(sm["norm_mix_g"][l]), g2=row(sm["norm_ffn_g"][l]),
        wa=_block_diag(sm["lru_w_a"][l]).astype(BF), wx=_block_diag(sm["lru_w_x"][l]).astype(BF),
        ba=row(sm["lru_b_a"][l]), bx=row(sm["lru_b_x"][l]),
        sp=row(jax.nn.softplus(-sm["lru_lambda"][l])),
        cw=sm["conv_w"][l], cb=row(sm["conv_b"][l]),
        wm=(sm["sgu_w_s"][l] * _sgu_mask()).astype(BF),
        bsb=jnp.broadcast_to(sm["sgu_b_s"][l][:, :, None], (SGU_GROUPS, SGU_BLOCK, SGU_BLOCK)),
        lg=row(sm["sgu_ln_g"][l]), lb=row(sm["sgu_ln_b"][l]),
    )


def _layer_fwd_mix(x, big, p, ts, h=None, before_sgu=None):
    if h is None:
        h = _norm_call(x, p["g1"], ts)
    proj = _inproj_call(h, big["w_in"], 0, 2 * ts)
    xr, hr, ya_pre = _rnn_fwd_call(proj, p["wa"], p["wx"], p["ba"], p["bx"], p["sp"], p["cw"], p["cb"], ts)
    lg = p["lg"] if before_sgu is None else p["lg"] + before_sgu(ya_pre)
    yb_pre = _sgu_fwd_call(proj, p["wm"], p["bsb"], lg, p["lb"], ts)
    return dict(p=p, x=x, h=h, proj=proj, xr=xr, hr=hr, ya_pre=ya_pre, yb_pre=yb_pre)


def _layer_fwd_out(sv, big, ts):
    x1, ya, yb, merged, h2 = _merge_call(sv["x"], sv["proj"], sv["ya_pre"], sv["yb_pre"], big["w_branch_a"],
                                         big["w_branch_b"], big["w_out"], sv["p"]["g2"], 0, ts)
    x2, pre = _ffn_call(x1, h2, big["w_up"], big["w_down"], 0, ts)
    sv.update(x1=x1, ya=ya, yb=yb, merged=merged, h2=h2, pre=pre)
    return x2


def _layer_bwd_ffn(dx, sv, big, ts):
    p = sv["p"]
    dx1, dpre, dg2 = _ffn_bwd_call(dx, sv["pre"], sv["x1"], p["g2"], big["w_up"], big["w_down"], 0, ts)
    tk = dx.shape[0]
    gb = dict(
        w_down=_wgrad_call(sv["pre"], dx, p["core"], Q_FF, D_MODEL // 2, tk, False, "wgrad_down", a_fn=_relu_sq),
        w_up=_wgrad_call(sv["h2"], dpre, p["core"], D_MODEL, Q_FF, tk, True, "wgrad_up"))
    return dx1, gb, dict(norm_ffn_g=dg2[0])


def _layer_bwd_merge(dx1, sv, big, ts, after=None):
    tk = dx1.shape[0]
    core = sv["p"]["core"]
    dya, dyb, dgate, dya_pre, dyb_pre = _merge_bwd_call(
        dx1, sv["proj"], sv["ya"], sv["yb"], big["w_branch_a"], big["w_branch_b"], big["w_out"], 0, ts, after)
    gb = dict(
        w_out=_wgrad_call(sv["merged"], dx1, core, D_MODEL, D_MODEL // 2, tk, False, "wgrad_out"),
        w_branch_a=_wgrad_call(sv["ya_pre"], dya, core, D_RNN, D_MODEL // 2, tk, False, "wgrad_branch_a"),
        w_branch_b=_wgrad_call(sv["yb_pre"], dyb, core, D_SGU, D_MODEL // 2, tk, False, "wgrad_branch_b"))
    return (dgate, dya_pre, dyb_pre), gb


def _layer_bwd_branches(dx1, merge_out, sv, big, lam, ts, after_sgu=None):
    p = sv["p"]
    tk = dx1.shape[0]
    dgate, dya_pre, dyb_pre = merge_out
    gb = {}
    duv, dws, dbs, dlg, dlb = _sgu_bwd_call(dyb_pre, sv["proj"], p["wm"], p["bsb"], _sgu_mask(), p["lg"], p["lb"],
                                            ts)
    ba = p["ba"] if after_sgu is None else p["ba"] + after_sgu(duv)
    dxg, dwa, dwx, vec = _rnn_bwd_call(dya_pre, sv["proj"], sv["xr"], sv["hr"], p["wa"], p["wx"], ba, p["bx"],
                                       p["sp"], p["cw"], ts // 2)
    dx, dproj, dg1 = _inproj_bwd_call(dxg, duv, dgate, dx1, sv["x"], p["g1"], big["w_in"], 0, ts)
    gb["w_in"] = _wgrad_call(sv["h"], dproj, p["core"], D_MODEL, Q_IN, tk // 2, True, "wgrad_in")
    gs = dict(
        norm_mix_g=dg1[0], conv_w=vec[_ROW_DCW:_ROW_DCW + CONV_WIDTH], conv_b=vec[_ROW_DCB],
        lru_w_a=_block_diag_extract(dwa), lru_w_x=_block_diag_extract(dwx),
        lru_b_a=vec[_ROW_DBA].reshape(RNN_HEADS, RNN_HEAD_DIM), lru_b_x=vec[_ROW_DBX].reshape(RNN_HEADS, RNN_HEAD_DIM),
        lru_lambda=-vec[_ROW_DSP] * jax.nn.sigmoid(-lam),
        sgu_ln_g=dlg[0], sgu_ln_b=dlb[0], sgu_w_s=dws, sgu_b_s=dbs.T)
    return dx, gb, gs


def _local_step(x, target, big, sm, ts):
    saved = []
    core = jnp.zeros((1,), jnp.int32)
    for l in range(DEPTH):
        sv = _layer_fwd_mix(x, big[l], _layer_small(sm, l, core), ts)
        x = _layer_fwd_out(sv, big[l], ts)
        saved.append(sv)
    dx, loss, dgf = _loss_call(x, target, sm["final_norm_g"].reshape(1, -1), ts)
    gb, gs = [None] * DEPTH, [None] * DEPTH
    for l in reversed(range(DEPTH)):
        dx1, gb_ffn, gs_ffn = _layer_bwd_ffn(dx, saved[l], big[l], ts)
        merge_out, gb_merge = _layer_bwd_merge(dx1, saved[l], big[l], ts)
        dx, gb_mix, gs_mix = _layer_bwd_branches(dx1, merge_out, saved[l], big[l], sm["lru_lambda"][l], ts)
        gb[l] = {**gb_ffn, **gb_merge, **gb_mix}
        gs[l] = {**gs_ffn, **gs_mix}
    gs = {k: jnp.stack([g[k] for g in gs]) for k in gs[0]}
    gs["final_norm_g"] = dgf[0]
    return loss, dx, gb, gs


EW_VMEM_BYTES = 24 * 1024 * 1024


def _row_block(rows, cols, bytes_per_elem):
    for br in range(min(rows, EW_VMEM_BYTES // (2 * bytes_per_elem * cols)), 0, -1):
        if rows % br == 0 and br % 16 == 0:
            return br
    return rows


def _ew_call(fn, name, operands, outputs, slabs=1, sel=None, into=None, after=None):
    if into is not None and not isinstance(into, (list, tuple)):
        into = [into]
    rows, cols = outputs[0][0].shape[2:]
    br = _row_block(rows, cols, sum(jnp.dtype(a.dtype).itemsize for a, _ in operands + outputs))
    n_in = len(operands)

    def pick(tok, g, s):
        if callable(tok):
            return tok(g, s)
        if tok == "g":
            return g
        if isinstance(tok, tuple):
            return s[tok[1]]
        return tok

    def spec(idx):
        return pl.BlockSpec((None, None, br, cols),
                            lambda g, i, s, idx=idx: (pick(idx[0], g, s), pick(idx[1], g, s), i, 0))

    if sel is None:
        sel = jnp.zeros((1,), jnp.int32)
    in_specs = [spec(idx) for _, idx in operands]
    arrays = [a for a, _ in operands]
    aliases = {}
    for j, buf in enumerate(into or ()):
        in_specs.append(pl.BlockSpec(memory_space=pl.ANY))
        arrays.append(buf)
        aliases[1 + n_in + j] = j
    if after is not None:
        in_specs.append(pl.BlockSpec(memory_space=pl.ANY))
        arrays.append(after)

    def body(sel_ref, *refs):
        outs = fn(*[r[...] for r in refs[:n_in]])
        for o_ref, o in zip(refs[len(arrays):], outs):
            o_ref[...] = o.astype(o_ref.dtype)

    return pl.pallas_call(
        body, name=name, out_shape=[s for s, _ in outputs],
        grid_spec=pltpu.PrefetchScalarGridSpec(
            num_scalar_prefetch=1, grid=(slabs, rows // br),
            in_specs=in_specs,
            out_specs=[spec(idx) for _, idx in outputs]),
        input_output_aliases=aliases,
        compiler_params=_params(("parallel", "parallel")),
    )(sel, *arrays)


def _as4(a):
    return a.reshape((1,) * (4 - a.ndim) + a.shape)


def _adamw(w, g, m, v):
    m = ADAM_B1 * m + (1.0 - ADAM_B1) * g
    v = ADAM_B2 * v + (1.0 - ADAM_B2) * jnp.square(g)
    m_hat = m / (1.0 - ADAM_B1 ** ADAM_STEP)
    v_hat = v / (1.0 - ADAM_B2 ** ADAM_STEP)
    delta = -ADAM_LR * (m_hat / (jnp.sqrt(v_hat) + ADAM_EPS) + ADAM_WD * w)
    return delta, m, v


def _small_adamw_call(ws, gs, ms, vs):
    n = len(ws)

    def body(*refs):
        for k in range(n):
            w, g, m, v = (refs[j * n + k][...] for j in range(4))
            outs = _adamw(w, g, m, v)
            for j in range(3):
                refs[(4 + j) * n + k][...] = outs[j]

    shapes = [jax.ShapeDtypeStruct(w.shape, F32) for w in ws]
    outs = pl.pallas_call(
        body, name="adamw_small", out_shape=shapes * 3,
        in_specs=[pl.BlockSpec(memory_space=pltpu.VMEM)] * (4 * n),
        out_specs=[pl.BlockSpec(memory_space=pltpu.VMEM)] * (3 * n),
        compiler_params=_params(),
    )(*ws, *gs, *ms, *vs)
    return outs[:n], outs[n:2 * n], outs[2 * n:]


ANY = pl.BlockSpec(memory_space=pl.ANY)


def _place():
    x, y, c = lax.axis_index("x"), lax.axis_index("y"), lax.axis_index("c")
    chips = [(1 - x, y), (x, 1 - y), (1 - x, 1 - y)]
    return x, y, c, chips


def _remote(src, dst, send_sem, recv_sem, to):
    return pltpu.make_async_remote_copy(src_ref=src, dst_ref=dst, send_sem=send_sem, recv_sem=recv_sem,
                                        device_id=to, device_id_type=MESH)


def _gather_call(bufs):
    n = len(bufs)

    def body(*refs):
        out = refs[n:2 * n]
        send_sems, recv_sems = refs[2 * n:]
        x, y, c, chips = _place()
        me_q = 2 * x + y
        sibling = (x, y, 1 - c)
        first = []
        for w in range(n):
            for j, chip in enumerate(chips):
                mine = out[w].at[c, me_q]
                first.append(_remote(mine, mine, send_sems.at[w * 3 + j], recv_sems.at[w * 3 + j], (*chip, c)))
        for cp in first:
            cp.start()
        passed = []
        for w in range(n):
            for j, (qx, qy) in enumerate(chips):
                landed = out[w].at[c, 2 * qx + qy]
                k = w * 3 + j
                _remote(landed, landed, send_sems.at[k], recv_sems.at[k], (qx, qy, c)).wait_recv()
                cp = _remote(landed, landed, send_sems.at[3 * n + k], recv_sems.at[3 * n + k], sibling)
                cp.start()
                passed.append(cp)
        for w in range(n):
            for j, (qx, qy) in enumerate(chips):
                landed = out[w].at[1 - c, 2 * qx + qy]
                k = 3 * n + w * 3 + j
                _remote(landed, landed, send_sems.at[k], recv_sems.at[k], sibling).wait_recv()
        for cp in first + passed:
            cp.wait_send()

    return pl.pallas_call(
        body, name="gather_weights",
        out_shape=[jax.ShapeDtypeStruct(a.shape, a.dtype) for a in bufs],
        in_specs=[ANY] * n, out_specs=[ANY] * n,
        input_output_aliases={w: w for w in range(n)},
        scratch_shapes=[pltpu.SemaphoreType.DMA((6 * n,)), pltpu.SemaphoreType.DMA((6 * n,))],
        compiler_params=_params(vmem=False, has_side_effects=True),
    )(*bufs)


def _sibling_send_call(items):
    n = len(items)

    def body(*refs):
        src, out = refs[:n], refs[n:2 * n]
        send_sems, recv_sems = refs[2 * n:]
        x, y, c, _ = _place()
        copies = [_remote(src[w], out[w], send_sems.at[w], recv_sems.at[w], (x, y, 1 - c)) for w in range(n)]
        for cp in copies:
            cp.start()
        for cp in copies:
            cp.wait()

    return pl.pallas_call(
        body, name="grads_to_sibling",
        out_shape=[jax.ShapeDtypeStruct(a.shape, a.dtype) for a in items],
        in_specs=[ANY] * n, out_specs=[ANY] * n,
        scratch_shapes=[pltpu.SemaphoreType.DMA((n,)), pltpu.SemaphoreType.DMA((n,))],
        compiler_params=_params(vmem=False, has_side_effects=True),
    )(*items)


def _sibling_inplace_call(name, bufs, slabs, n_pairs):
    n = len(bufs)

    def body(*refs):
        out = refs[n:2 * n]
        send_sems, recv_sems = refs[2 * n:]
        x, y, c, _ = _place()
        sibling = (x, y, 1 - c)
        pairs = [pair for w, ref in enumerate(out) for pair in slabs(ref, c, w)]
        sends = [_remote(s, s, send_sems.at[k], recv_sems.at[k], sibling) for k, (s, _) in enumerate(pairs)]
        for cp in sends:
            cp.start()
        for k, (_, r) in enumerate(pairs):
            _remote(r, r, send_sems.at[k], recv_sems.at[k], sibling).wait_recv()
        for cp in sends:
            cp.wait_send()

    return pl.pallas_call(
        body, name=name,
        out_shape=[jax.ShapeDtypeStruct(a.shape, a.dtype) for a in bufs],
        in_specs=[ANY] * n, out_specs=[ANY] * n,
        input_output_aliases={w: w for w in range(n)},
        scratch_shapes=[pltpu.SemaphoreType.DMA((n_pairs,)), pltpu.SemaphoreType.DMA((n_pairs,))],
        compiler_params=_params(vmem=False, has_side_effects=True),
    )(*bufs)


HBM_SPEC = pl.BlockSpec(memory_space=pltpu.HBM)
SEM_SPEC = pl.BlockSpec(memory_space=pltpu.SEMAPHORE)
DATAFLOW_EFFECT = pltpu.SideEffectType.DATAFLOW_SIDE_EFFECTING


def _exchange_start(name, bufs, copies, n_copies, after):
    n = len(bufs)

    def body(*refs):
        ins, send_sems, recv_sems, token = refs[:n], refs[n + 1], refs[n + 2], refs[-1]
        for k, (src, dst, to) in enumerate(copies(ins)):
            _remote(src, dst, send_sems.at[k], recv_sems.at[k], to).start()
        token[...] = jnp.zeros_like(token)

    outs = pl.pallas_call(
        body, name=name,
        out_shape=(pltpu.SemaphoreType.DMA((n_copies,)), pltpu.SemaphoreType.DMA((n_copies,)),
                   *[pltpu.HBM(b.shape, b.dtype) for b in bufs], jax.ShapeDtypeStruct((SUBLANES, 128), F32)),
        in_specs=[HBM_SPEC] * n + [ANY],
        out_specs=(SEM_SPEC, SEM_SPEC, *[HBM_SPEC] * n, pl.BlockSpec(memory_space=pltpu.VMEM)),
        input_output_aliases={w: w + 2 for w in range(n)},
        compiler_params=pltpu.CompilerParams(has_side_effects=DATAFLOW_EFFECT),
    )(*[pltpu.with_memory_space_constraint(b, pltpu.HBM) for b in bufs], after)
    return outs[0], outs[1], list(outs[2:2 + n]), outs[-1]


def _exchange_wait(name, send_sems, recv_sems, bufs, copies, after):
    n = len(bufs)

    def body(*refs):
        ins, send_sems, recv_sems = refs[:n], refs[n], refs[n + 1]
        for k, (src, dst, to) in enumerate(copies(ins)):
            cp = _remote(src, dst, send_sems.at[k], recv_sems.at[k], to)
            cp.wait_send()
            cp.wait_recv()

    return pl.pallas_call(
        body, name=name,
        out_shape=[pltpu.HBM(b.shape, b.dtype) for b in bufs],
        in_specs=[HBM_SPEC] * n + [SEM_SPEC, SEM_SPEC, ANY],
        out_specs=[HBM_SPEC] * n,
        input_output_aliases={w: w for w in range(n)},
        compiler_params=pltpu.CompilerParams(has_side_effects=DATAFLOW_EFFECT),
    )(*bufs, send_sems, recv_sems, after)


def _gather_copies(refs):
    x, y, c, chips = _place()
    mine = 2 * (2 * x + y) + c
    return [(ref.at[mine], ref.at[mine], (qx, qy, c)) for ref in refs for qx, qy in chips]


def _forward_copies(refs):
    x, y, c, chips = _place()
    return [(ref.at[2 * (2 * qx + qy) + c], ref.at[2 * (2 * qx + qy) + c], (x, y, 1 - c))
            for ref in refs for qx, qy in chips]


def _gather_forward_slabs(ref, c, w):
    x, y, _, chips = _place()
    return [(ref.at[2 * (2 * qx + qy) + c], ref.at[2 * (2 * qx + qy) + 1 - c]) for qx, qy in chips]


def _device_peers():
    x, y, c, _ = _place()
    return 4 * x + 2 * y + c, [(k, (x ^ ((k >> 2) & 1), y ^ ((k >> 1) & 1), c ^ (k & 1))) for k in range(1, 8)]


def _small_scatter_copies(refs):
    me, peers = _device_peers()
    return [(refs[0].at[me ^ k], refs[1].at[me], to) for k, to in peers]


def _small_spread_copies(refs):
    me, peers = _device_peers()
    return [(refs[0].at[me], refs[0].at[me], to) for _, to in peers]


def _sibling_copies(refs):
    n = len(refs) // 2
    x, y, c, _ = _place()
    return [(refs[w], refs[n + w], (x, y, 1 - c)) for w in range(n)]


def _owner_copies(refs):
    n = len(refs) // 2
    x, y, c, chips = _place()
    return [(refs[w].at[2 * qx + qy], refs[n + w].at[j], (qx, qy, c))
            for w in range(n) for j, (qx, qy) in enumerate(chips)]


N_DEVICES = 8
SMALL_ROWS = 616


SMALL = ("norm_mix_g", "conv_w", "conv_b", "lru_w_a", "lru_b_a", "lru_w_x", "lru_b_x", "lru_lambda",
         "sgu_ln_g", "sgu_ln_b", "sgu_w_s", "sgu_b_s", "norm_ffn_g", "final_norm_g")
WEIGHTS = ("norm_mix_g", "w_in", "conv_w", "conv_b", "lru_w_a", "lru_b_a", "lru_w_x", "lru_b_x", "lru_lambda",
           "sgu_ln_g", "sgu_ln_b", "sgu_w_s", "sgu_b_s", "w_branch_a", "w_branch_b", "w_out", "norm_ffn_g",
           "w_up", "w_down", "final_norm_g")
PACK_ALIGN = SUBLANES * 128


PACKED = SMALL + ("loss",)


def _pack_small(gs):
    parts = []
    for k in PACKED:
        flat = gs[k].reshape(-1)
        parts.append(jnp.pad(flat, (0, -flat.size % PACK_ALIGN)))
    flat = jnp.concatenate(parts)
    flat = jnp.pad(flat, (0, N_DEVICES * SMALL_ROWS * 128 - flat.size))
    return flat.reshape(N_DEVICES, SMALL_ROWS, 128)


def _unpack_small(buf, like):
    flat = buf.reshape(-1)
    out, off = {}, 0
    for k in PACKED:
        size = like[k].size
        out[k] = flat[off:off + size].reshape(like[k].shape)
        off += size + (-size % PACK_ALIGN)
    return out


def _as_rows(a):
    return a.reshape(-1, a.shape[-1])


def kernel(x, norm_mix_g, w_in, conv_w, conv_b, lru_w_a, lru_b_a, lru_w_x, lru_b_x, lru_lambda, sgu_ln_g, sgu_ln_b, sgu_w_s, sgu_b_s, w_branch_a, w_branch_b, w_out, norm_ffn_g, w_up, w_down, final_norm_g, loss_target, m_norm_mix_g, m_w_in, m_conv_w, m_conv_b, m_lru_w_a, m_lru_b_a, m_lru_w_x, m_lru_b_x, m_lru_lambda, m_sgu_ln_g, m_sgu_ln_b, m_sgu_w_s, m_sgu_b_s, m_w_branch_a, m_w_branch_b, m_w_out, m_norm_ffn_g, m_w_up, m_w_down, m_final_norm_g, v_norm_mix_g, v_w_in, v_conv_w, v_conv_b, v_lru_w_a, v_lru_b_a, v_lru_w_x, v_lru_b_x, v_lru_lambda, v_sgu_ln_g, v_sgu_ln_b, v_sgu_w_s, v_sgu_b_s, v_w_branch_a, v_w_branch_b, v_w_out, v_norm_ffn_g, v_w_up, v_w_down, v_final_norm_g):
    w = dict(norm_mix_g=norm_mix_g, w_in=w_in, conv_w=conv_w, conv_b=conv_b, lru_w_a=lru_w_a, lru_b_a=lru_b_a,
             lru_w_x=lru_w_x, lru_b_x=lru_b_x, lru_lambda=lru_lambda, sgu_ln_g=sgu_ln_g, sgu_ln_b=sgu_ln_b,
             sgu_w_s=sgu_w_s, sgu_b_s=sgu_b_s, w_branch_a=w_branch_a, w_branch_b=w_branch_b, w_out=w_out,
             norm_ffn_g=norm_ffn_g, w_up=w_up, w_down=w_down, final_norm_g=final_norm_g)
    m = dict(norm_mix_g=m_norm_mix_g, w_in=m_w_in, conv_w=m_conv_w, conv_b=m_conv_b, lru_w_a=m_lru_w_a,
             lru_b_a=m_lru_b_a, lru_w_x=m_lru_w_x, lru_b_x=m_lru_b_x, lru_lambda=m_lru_lambda,
             sgu_ln_g=m_sgu_ln_g, sgu_ln_b=m_sgu_ln_b, sgu_w_s=m_sgu_w_s, sgu_b_s=m_sgu_b_s,
             w_branch_a=m_w_branch_a, w_branch_b=m_w_branch_b, w_out=m_w_out, norm_ffn_g=m_norm_ffn_g,
             w_up=m_w_up, w_down=m_w_down, final_norm_g=m_final_norm_g)
    v = dict(norm_mix_g=v_norm_mix_g, w_in=v_w_in, conv_w=v_conv_w, conv_b=v_conv_b, lru_w_a=v_lru_w_a,
             lru_b_a=v_lru_b_a, lru_w_x=v_lru_w_x, lru_b_x=v_lru_b_x, lru_lambda=v_lru_lambda,
             sgu_ln_g=v_sgu_ln_g, sgu_ln_b=v_sgu_ln_b, sgu_w_s=v_sgu_w_s, sgu_b_s=v_sgu_b_s,
             w_branch_a=v_w_branch_a, w_branch_b=v_w_branch_b, w_out=v_w_out, norm_ffn_g=v_norm_ffn_g,
             w_up=v_w_up, w_down=v_w_down, final_norm_g=v_final_norm_g)
    core = lax.axis_index("c")
    chip = 2 * lax.axis_index("x") + lax.axis_index("y")
    sel = jnp.stack([core, 1 - core, chip, 2 * chip + core]).astype(jnp.int32)
    this_core, other_core, this_chip = ("sel", 0), ("sel", 1), ("sel", 2)
    sds = jax.ShapeDtypeStruct

    ts = TOKEN_TILE

    def after_all(arrays):
        return jnp.stack([a[(0,) * a.ndim].astype(F32) for a in arrays])

    halves ={k: (w[k].shape[1] // 2, w[k].shape[2]) for k in BIG}

    def half_view(k, a):
        return a.reshape((2 * N_QUARTERS,) + halves[k])

    def full_view(k, a):
        r2, cols = halves[k]
        if k in ("w_in", "w_up"):
            return a.reshape(1, N_QUARTERS, 2 * r2, cols)
        return a.reshape(1, 2 * N_QUARTERS * r2, cols)

    layer_bufs = [{}, {}]

    def cast_weights(k, after):
        _, r, cols = w[k].shape
        w4 = w[k].reshape(DEPTH, 1, r, cols)
        outs = _ew_call(lambda a, b: (a, b), "cast_weights", [(w4, (0, 0)), (w4, (1, 0))],
                        [(sds((1, N_QUARTERS, r, cols), BF), (0, this_chip))] * DEPTH, 1, sel, after=after)
        for l in range(DEPTH):
            layer_bufs[l][k] = half_view(k, outs[l])

    conv_buf = lax.dynamic_update_slice_in_dim(
        jnp.zeros((DEPTH, N_QUARTERS) + conv_w.shape[1:], F32), conv_w[:, None], chip, axis=1)
    sm = {k: w[k] for k in SMALL}
    sm["conv_w"] = _gather_call([conv_buf])[0].transpose(0, 2, 1, 3).reshape(DEPTH, CONV_WIDTH, D_RNN)

    def gather_start(tag, l, keys, after):
        bufs = [layer_bufs[l][k] for k in keys]
        return _exchange_start(f"gather_start_{tag}", bufs, _gather_copies, 3 * len(keys), after)

    def gather_finish(tag, keys, started, after):
        send_sems, recv_sems, thru, _ = started
        landed = _exchange_wait(f"gather_wait_{tag}", send_sems, recv_sems, thru, _gather_copies, after)
        landed = _sibling_inplace_call("gather_forward", landed, _gather_forward_slabs, 3 * len(keys))
        return {k: full_view(k, a) for k, a in zip(keys, landed)}

    first, rest = ("w_in",), tuple(k for k in BIG if k != "w_in")
    cast_weights("w_in", None)
    started_a = gather_start("0a", 0, first, sm["conv_w"])
    for k in rest:
        cast_weights(k, started_a[3])
    started_b = gather_start("0b", 0, rest, started_a[3])
    started_c = gather_start("1a", 1, first, started_b[3])
    started_d = gather_start("1b", 1, rest, started_c[3])

    def rest_arrives(tag, started):
        state = {}

        def hook(after):
            landed = _exchange_wait(f"gather_wait_{tag}", started[0], started[1], started[2], _gather_copies, after)
            state["forward"] = _exchange_start(f"forward_start_{tag}", landed, _forward_copies, 3 * len(rest), after)
            return state["forward"][3][0, 0]

        def finish(after):
            send_sems, recv_sems, thru, _ = state["forward"]
            done = _exchange_wait(f"forward_wait_{tag}", send_sems, recv_sems, thru, _forward_copies, after)
            return {k: full_view(k, a) for k, a in zip(rest, done)}

        return hook, finish

    p0, p1 = _layer_small(sm, 0, sel[0:1]), _layer_small(sm, 1, sel[0:1])
    h0 = _norm_call(x[0], p0["g1"], ts)
    ready = after_all([started_d[3], h0] + [p[k] for p in (p0, p1) for k in ("wa", "wx", "wm")])
    big0 = gather_finish("0a", first, started_a, ready)
    hook, finish = rest_arrives("0b", started_b)
    sv0 = _layer_fwd_mix(x[0], big0, p0, ts, h0, hook)
    big0.update(finish(sv0["yb_pre"]))
    x_mid = _layer_fwd_out(sv0, big0, ts)
    big1 = gather_finish("1a", first, started_c, x_mid)
    hook, finish = rest_arrives("1b", started_d)
    sv1 = _layer_fwd_mix(x_mid, big1, p1, ts, None, hook)
    big1.update(finish(sv1["yb_pre"]))
    x_out = _layer_fwd_out(sv1, big1, ts)
    dx, loss, dgf = _loss_call(x_out, loss_target[0], final_norm_g.reshape(1, -1), ts)

    def pair_start(tag, gb, after):
        sends = [gb[k][1] for k in gb]
        zones = [lax.empty(a.shape, BF) for a in sends]
        return _exchange_start(f"pair_start_{tag}", sends + zones, _sibling_copies, len(sends), after)

    def reduce_start(tag, gb, after, pair=None):
        keys = tuple(gb)
        if pair is None:
            from_sibling = _sibling_send_call([gb[k][1] for k in keys])
        else:
            done = _exchange_wait(f"pair_wait_{tag}", pair[0], pair[1], pair[2], _sibling_copies, after)
            from_sibling = done[len(keys):]
        sums = [
            _ew_call(lambda a, b: (a + b.astype(F32),), "pair_sum", [(gb[k][0][None], (0, "g")), (r[None], (0, "g"))],
                     [(sds((1,) + r.shape, BF), (0, "g"))], N_QUARTERS)[0][0]
            for k, r in zip(keys, from_sibling)]
        zones = [lax.empty((3,) + a.shape[1:], BF) for a in sums]
        started = _exchange_start(f"reduce_start_{tag}", sums + zones, _owner_copies, 3 * len(keys), after)
        return keys, started

    def reduce_finish(tag, l, keys_started, after, reduced):
        keys, (send_sems, recv_sems, thru, _) = keys_started
        done = _exchange_wait(f"reduce_wait_{tag}", send_sems, recv_sems, thru, _owner_copies, after)
        sums, zones = done[:len(keys)], done[len(keys):]
        for i, k in enumerate(keys):
            r2, cols = halves[k]
            reduced[k] = _ew_call(
                lambda a, b, c, d: (((a.astype(F32) + b.astype(F32)) + c.astype(F32)) + d.astype(F32),),
                "quarter_sum", [(sums[i][None], (0, this_chip))] + [(zones[i][None], (0, j)) for j in range(3)],
                [(sds((DEPTH, 2, r2, cols), F32), (l, this_core))], 1, sel, into=reduced.get(k))[0]

    def behind(params, key, started):
        return dict(params, **{key: params[key] + started[1][3][0, 0]})

    dx1, gb_ffn, gs1 = _layer_bwd_ffn(dx, sv1, big1, ts)
    merge_out, gb_merge = _layer_bwd_merge(dx1, sv1, big1, ts)
    dx_mid, gb_in, gs1_mix = _layer_bwd_branches(dx1, merge_out, sv1, big1, lru_lambda[1], ts)
    gb_1 = {**gb_ffn, **gb_merge, **gb_in}
    pair_1 = pair_start("1", gb_1, dx_mid)
    sv0["p"] = behind(sv0["p"], "g2", (None, pair_1))
    dx1, gb_ffn, gs0 = _layer_bwd_ffn(dx_mid, sv0, big0, ts)
    exchange_1 = reduce_start("1", gb_1, dx1, pair_1)
    pair_0a = pair_start("0a", gb_ffn, exchange_1[1][3])
    merge_out, gb_merge = _layer_bwd_merge(dx1, sv0, big0, ts, pair_0a[3])
    exchange_0a = reduce_start("0a", gb_ffn, merge_out[0], pair_0a)
    pair_0b = pair_start("0b", gb_merge, exchange_0a[1][3])
    sv0["p"] = behind(sv0["p"], "lg", (None, pair_0b))
    started_0b = {}

    def after_sgu(duv):
        started_0b["exchange"] = reduce_start("0b", gb_merge, duv, pair_0b)
        return started_0b["exchange"][1][3][0, 0]

    grad_x, gb_in, gs0_mix = _layer_bwd_branches(dx1, merge_out, sv0, big0, lru_lambda[0], ts, after_sgu)
    exchange_0b = started_0b["exchange"]
    exchange_0c = reduce_start("0c", gb_in, exchange_0b[1][3])
    layer_gs = [{**gs0, **gs0_mix}, {**gs1, **gs1_mix}]
    gs = {k: jnp.stack([g[k] for g in layer_gs]) for k in layer_gs[0]}
    gs["final_norm_g"] = dgf[0]
    gs["loss"] = loss[0, 0:1]

    me = ("sel", 3)
    piece = (1, N_DEVICES, SMALL_ROWS, 128)
    packed = _pack_small(gs).reshape(piece)
    scatter = _exchange_start("small_scatter_start", [packed[0], lax.empty(piece[1:], F32)], _small_scatter_copies,
                              N_DEVICES - 1, exchange_0c[1][3])
    reduced = {}
    reduce_finish("1", 1, exchange_1, scatter[3], reduced)
    reduce_finish("0a", 0, exchange_0a, reduced["w_in"], reduced)
    reduce_finish("0b", 0, exchange_0b, reduced["w_down"], reduced)

    def swap_slabs(ref, c, i):
        layers = (1,) if BIG[i] == "w_in" else range(DEPTH)
        return [(ref.at[l, c], ref.at[l, 1 - c]) for l in layers]

    swapped = dict(zip(BIG, _sibling_inplace_call("grads_swap_halves", [reduced[k] for k in BIG], swap_slabs,
                                                  DEPTH * len(BIG) - 1)))

    def adamw_layers(k, grad, layer, into, after=None):
        if layer is None:
            views = [_as4(_as_rows(a)) for a in (w[k], grad, m[k], v[k])]
            idx = (0, 0)
        else:
            views = [a.reshape((1,) + w[k].shape) for a in (w[k], grad, m[k], v[k])]
            idx = (0, layer)
        return _ew_call(_adamw, "adamw_big", [(a, idx) for a in views], [(sds(views[0].shape, F32), idx)] * 3,
                        into=into, after=after)

    updated, last_update = {}, None
    for k in BIG:
        updated[k] = adamw_layers(k, swapped[k], 1 if k == "w_in" else None, None, last_update)
        last_update = updated[k][0]
    scattered = _exchange_wait("small_scatter_wait", scatter[0], scatter[1], scatter[2], _small_scatter_copies,
                               last_update)
    summed = _ew_call(
        lambda *parts: (functools.reduce(lambda a, b: a + b, parts),), "small_sum",
        [(scattered[0][None], (0, me))]
        + [(scattered[1][None], (0, lambda g, s, k=k: s[3] ^ k)) for k in range(1, N_DEVICES)],
        [(sds(piece, F32), (0, me))], 1, sel)[0]
    spread = _exchange_start("small_spread_start", [summed[0]], _small_spread_copies, N_DEVICES - 1, summed)
    reduced["w_in"] = swapped["w_in"]
    reduce_finish("0c", 0, exchange_0c, spread[3], reduced)
    last = _sibling_inplace_call("grads_swap_last", [reduced["w_in"]],
                                 lambda ref, c, i: [(ref.at[0, c], ref.at[0, 1 - c])], 1)[0]
    swapped["w_in"] = last
    updated["w_in"] = adamw_layers("w_in", last, 0, updated["w_in"])
    grads_big = {k: swapped[k].reshape(w[k].shape) for k in BIG}
    delta, new_m, new_v = ({k: updated[k][j].reshape(w[k].shape) for k in BIG} for j in range(3))
    gathered_small = _exchange_wait("small_spread_wait", spread[0], spread[1], spread[2], _small_spread_copies,
                                    updated["w_in"][0])[0]

    like = {k: jax.ShapeDtypeStruct(sm[k].shape, F32) for k in SMALL}
    like["loss"] = jax.ShapeDtypeStruct((1,), F32)
    grads_small = _unpack_small(gathered_small, like)
    total = grads_small.pop("loss")[0]
    conv_q = grads_small["conv_w"].reshape(DEPTH, CONV_WIDTH, N_QUARTERS, D_RNN // N_QUARTERS)
    grads_small["conv_w"] = lax.dynamic_index_in_dim(conv_q, chip, axis=2, keepdims=False)
    outs = _small_adamw_call(*[[_as_rows(d[k]) for k in SMALL] for d in (w, grads_small, m, v)])
    for d, o in zip((delta, new_m, new_v), outs):
        for k, a in zip(SMALL, o):
            d[k] = a.reshape(w[k].shape)

    grads = {**grads_big, **grads_small}
    return (total, grad_x[None], *[grads[k] for k in WEIGHTS], *[delta[k] for k in WEIGHTS],
            *[new_m[k] for k in WEIGHTS], *[new_v[k] for k in WEIGHTS])
```

```python
import functools
import math

import jax
import jax.numpy as jnp
from jax import lax
from jax.experimental import pallas as pl
from jax.experimental.pallas import tpu as pltpu

F32 = jnp.float32
BF = jnp.bfloat16

DEPTH = 2
D_MODEL = 1024
D_RNN = 1280
D_SGU = 1024
D_FF = 4096
D_IN = 2 * D_RNN + 2 * D_SGU + 2 * D_MODEL
N_QUARTERS = 4
Q_IN = D_IN // N_QUARTERS
Q_FF = D_FF // N_QUARTERS
RNN_HEADS = 20
RNN_HEAD_DIM = 64
LRU_GROUP = 256
N_LRU_GROUPS = D_RNN // LRU_GROUP
HEADS_PER_GROUP = LRU_GROUP // RNN_HEAD_DIM
CONV_WIDTH = 4
LRU_C = 8.0
SGU_GROUPS = 8
SGU_BLOCK = 128
CHUNK = 64
EPS = 1e-6

ADAM_LR = 0.001
ADAM_B1 = 0.9
ADAM_B2 = 0.999
ADAM_EPS = 1e-08
ADAM_WD = 0.01
ADAM_STEP = 10

SUBLANES = 8
TOKEN_TILE = 512
VMEM_LIMIT_BYTES = 56 * 1024 * 1024

MESH = pl.DeviceIdType.MESH


def _params(semantics=None, vmem=True, **kw):
    return pltpu.CompilerParams(
        dimension_semantics=semantics,
        vmem_limit_bytes=VMEM_LIMIT_BYTES if vmem else None,
        **kw,
    )


def _dot(a, b):
    return jnp.dot(a, b, preferred_element_type=F32)


def _dot_nt(a, b):
    return lax.dot_general(a, b, (((1,), (1,)), ((), ())), preferred_element_type=F32)


def _dot_tn(a, b):
    return lax.dot_general(a, b, (((0,), (0,)), ((), ())), preferred_element_type=F32)


_GELU_C = math.sqrt(2.0 / math.pi)
_GELU_A = 0.044715


def _gelu(x):
    return 0.5 * x * (1.0 + jnp.tanh(_GELU_C * (x + _GELU_A * x * x * x)))


def _gelu_and_grad(x):
    x2 = x * x
    t = jnp.tanh(_GELU_C * (x + _GELU_A * x2 * x))
    du = _GELU_C * (1.0 + 3.0 * _GELU_A * x2)
    return 0.5 * x * (1.0 + t), 0.5 * (1.0 + t) + 0.5 * x * (1.0 - t * t) * du


def _rms_stats(x):
    return lax.rsqrt(jnp.mean(x * x, axis=-1, keepdims=True) + EPS)


def _rms_bwd(dy, x, g):
    rs = _rms_stats(x)
    n = x * rs
    dn = dy * g
    dx = rs * (dn - n * jnp.mean(dn * n, axis=-1, keepdims=True))
    return dx, dy * n


def _row_sum(x):
    return jnp.sum(x, axis=0, keepdims=True)


def _tile_spec(ts, width, col=0):
    return pl.BlockSpec((ts, width), lambda i, col=col: (i, col))


def _full_spec(shape):
    zeros = (0,) * len(shape)
    return pl.BlockSpec(shape, lambda *_: zeros)


def _layer_spec(w, layer):
    zeros = (0,) * (w.ndim - 1)
    return pl.BlockSpec((None,) + tuple(w.shape[1:]), lambda *_: (layer,) + zeros)


def _norm_call(x, g, ts):
    s = x.shape[0]

    def body(x_ref, g_ref, h_ref):
        xv = x_ref[...]
        h_ref[...] = (xv * _rms_stats(xv) * g_ref[...]).astype(BF)

    return pl.pallas_call(
        body, name="norm_fwd", grid=(s // ts,),
        in_specs=[_tile_spec(ts, D_MODEL), _full_spec((1, D_MODEL))],
        out_specs=_tile_spec(ts, D_MODEL),
        out_shape=jax.ShapeDtypeStruct((s, D_MODEL), BF),
        compiler_params=_params(("parallel",)),
    )(x, g)


def _inproj_call(h, w_in, layer, ts):
    s = h.shape[0]

    def body(h_ref, w_ref, o_ref):
        o_ref[...] = _dot(h_ref[...], w_ref[...]).astype(BF)

    return pl.pallas_call(
        body, name="inproj_fwd", grid=(N_QUARTERS, s // ts),
        in_specs=[
            pl.BlockSpec((ts, D_MODEL), lambda q, i: (i, 0)),
            pl.BlockSpec((None, None, D_MODEL, Q_IN), lambda q, i: (layer, q, 0, 0)),
        ],
        out_specs=pl.BlockSpec((ts, Q_IN), lambda q, i: (i, q)),
        out_shape=jax.ShapeDtypeStruct((s, D_IN), BF),
        compiler_params=_params(("parallel", "parallel")),
    )(h, w_in)


def _inproj_part_call(h, w_in, ts, own, first, count, into=None):
    s = h.shape[0]

    def quarter(j, sel):
        return (sel[0] + first + j) % N_QUARTERS

    def body(sel_ref, h_ref, w_ref, *rest):
        rest[-1][...] = _dot(h_ref[...], w_ref[...]).astype(BF)

    in_specs = [pl.BlockSpec((ts, D_MODEL), lambda j, i, sel: (i, 0)),
                pl.BlockSpec((None, None, D_MODEL, Q_IN), lambda j, i, sel: (0, quarter(j, sel), 0, 0))]
    operands = [h, w_in]
    aliases = {}
    if into is not None:
        in_specs.append(pl.BlockSpec(memory_space=pl.ANY))
        operands.append(into)
        aliases = {3: 0}
    return pl.pallas_call(
        body, name="inproj_fwd_part", out_shape=jax.ShapeDtypeStruct((s, D_IN), BF),
        grid_spec=pltpu.PrefetchScalarGridSpec(
            num_scalar_prefetch=1, grid=(count, s // ts), in_specs=in_specs,
            out_specs=pl.BlockSpec((ts, Q_IN), lambda j, i, sel: (i, quarter(j, sel)))),
        input_output_aliases=aliases,
        compiler_params=_params(("parallel", "parallel")),
    )(own, *operands)


def _shift_down(x, tail, s):
    xr = pltpu.roll(x, s, 0)
    tr = pltpu.roll(tail, s, 0)
    row = lax.broadcasted_iota(jnp.int32, tail.shape, 0)
    top = jnp.where(row < s, tr, xr[0:SUBLANES])
    return jnp.concatenate([top, xr[SUBLANES:]], axis=0)


def _shift_up(x, head, s):
    t = x.shape[0]
    xr = pltpu.roll(x, t - s, 0)
    hr = pltpu.roll(head, SUBLANES - s, 0)
    row = lax.broadcasted_iota(jnp.int32, head.shape, 0)
    bottom = jnp.where(row >= SUBLANES - s, hr, xr[t - SUBLANES:])
    return jnp.concatenate([xr[: t - SUBLANES], bottom], axis=0)


def _conv_fwd(x, tail, cw_ref, cb_ref):
    out = cb_ref[...] + cw_ref[CONV_WIDTH - 1:CONV_WIDTH, :] * x
    for s in range(1, CONV_WIDTH):
        k = CONV_WIDTH - 1 - s
        out = out + cw_ref[k:k + 1, :] * _shift_down(x, tail, s)
    return out


def _group_dot(x_bf, w_ref, dot):
    cols = [dot(x_bf[:, g * LRU_GROUP:(g + 1) * LRU_GROUP], w_ref[g]) for g in range(N_LRU_GROUPS)]
    return jnp.concatenate(cols, axis=1)


def _lru_gates(xr, wa_ref, wx_ref, ba_ref, bx_ref, sp_ref):
    xb = xr.astype(BF)
    r = jax.nn.sigmoid(_group_dot(xb, wa_ref, _dot) + ba_ref[...])
    i = jax.nn.sigmoid(_group_dot(xb, wx_ref, _dot) + bx_ref[...])
    log_a = (-LRU_C * r) * sp_ref[...]
    a = jnp.exp(log_a)
    nrm2 = -jnp.tanh(log_a) * (a * a + 1.0)
    inv_nrm = lax.rsqrt(jnp.maximum(nrm2, 1e-36))
    return r, i, a, nrm2 * inv_nrm, inv_nrm


def _linear_scan(a, b, carry, al_ref, bl_ref, h_ref, reverse):
    t, c = a.shape
    rowm = lax.broadcasted_iota(jnp.int32, (t, c), 0) & (SUBLANES - 1)
    for d in (1, 2, 4):
        if reverse:
            keep, sh = rowm < SUBLANES - d, t - d
        else:
            keep, sh = rowm >= d, d
        a_sh = jnp.where(keep, pltpu.roll(a, sh, 0), 1.0)
        b_sh = jnp.where(keep, pltpu.roll(b, sh, 0), 0.0)
        b = a * b_sh + b
        a = a * a_sh
    al_ref[...] = a
    bl_ref[...] = b
    groups = t // SUBLANES

    def step(j, state):
        jj = groups - 1 - j if reverse else j
        off = pl.multiple_of(jj * SUBLANES, SUBLANES)
        rows = bl_ref[pl.ds(off, SUBLANES), :] + al_ref[pl.ds(off, SUBLANES), :] * state
        h_ref[pl.ds(off, SUBLANES), :] = rows
        last = rows[0:1, :] if reverse else rows[SUBLANES - 1:SUBLANES, :]
        return jnp.broadcast_to(last, (SUBLANES, c))

    out = lax.fori_loop(0, groups, step, jnp.broadcast_to(carry, (SUBLANES, c)))
    return out[0:1, :]


def _rnn_fwd_call(proj, wa, wx, ba, bx, sp, cw, cb, ts):
    s = proj.shape[0]

    def body(xg_ref, wa_ref, wx_ref, ba_ref, bx_ref, sp_ref, cw_ref, cb_ref, xr_ref, hr_ref, ya_ref,
             tail_sc, carry_sc, al_sc, bl_sc, h_sc):
        @pl.when(pl.program_id(0) == 0)
        def _():
            tail_sc[...] = jnp.zeros_like(tail_sc)
            carry_sc[...] = jnp.zeros_like(carry_sc)

        x = xg_ref[:, :D_RNN].astype(F32)
        g = xg_ref[:, D_RNN:]
        xr = _conv_fwd(x, tail_sc[...], cw_ref, cb_ref)
        tail_sc[...] = x[ts - SUBLANES:, :]
        xr_ref[...] = xr.astype(BF)
        _, i, a, nrm, _ = _lru_gates(xr, wa_ref, wx_ref, ba_ref, bx_ref, sp_ref)
        carry_sc[...] = _linear_scan(a, nrm * (i * xr), carry_sc[...], al_sc, bl_sc, h_sc, False)
        h = h_sc[...]
        hr_ref[...] = h.astype(BF)
        ya_ref[...] = (h * _gelu(g)).astype(BF)

    gw = (N_LRU_GROUPS, LRU_GROUP, LRU_GROUP)
    return pl.pallas_call(
        body, name="rnn_fwd", grid=(s // ts,),
        in_specs=[_tile_spec(ts, 2 * D_RNN), _full_spec(gw), _full_spec(gw),
                  _full_spec((1, D_RNN)), _full_spec((1, D_RNN)), _full_spec((1, D_RNN)),
                  _full_spec((CONV_WIDTH, D_RNN)), _full_spec((1, D_RNN))],
        out_specs=[_tile_spec(ts, D_RNN)] * 3,
        out_shape=[jax.ShapeDtypeStruct((s, D_RNN), BF)] * 3,
        scratch_shapes=[pltpu.VMEM((SUBLANES, D_RNN), F32), pltpu.VMEM((1, D_RNN), F32),
                        pltpu.VMEM((ts, D_RNN), F32), pltpu.VMEM((ts, D_RNN), F32),
                        pltpu.VMEM((ts, D_RNN), F32)],
        compiler_params=_params(("arbitrary",)),
    )(proj, wa, wx, ba, bx, sp, cw, cb)


def _layernorm_fwd(x):
    mu = jnp.mean(x, axis=-1, keepdims=True)
    xc = x - mu
    rstd = lax.rsqrt(jnp.mean(xc * xc, axis=-1, keepdims=True) + EPS)
    return xc * rstd, rstd


def _sgu_mix(vn_bf, wm_ref, bsb_ref, ts):
    rows = []
    for blk in range(ts // SGU_BLOCK):
        r0 = blk * SGU_BLOCK
        cols = [
            _dot(wm_ref[g], vn_bf[r0:r0 + SGU_BLOCK, g * SGU_BLOCK:(g + 1) * SGU_BLOCK]) + bsb_ref[g]
            for g in range(SGU_GROUPS)
        ]
        rows.append(jnp.concatenate(cols, axis=1))
    return jnp.concatenate(rows, axis=0)


def _sgu_fwd_call(proj, wm, bsb, lg, lb, ts):
    s = proj.shape[0]

    def body(uv_ref, wm_ref, bsb_ref, lg_ref, lb_ref, yb_ref):
        gu = _gelu(uv_ref[:, :D_SGU])
        gv = _gelu(uv_ref[:, D_SGU:2 * D_SGU]).astype(F32)
        nh, _ = _layernorm_fwd(gv)
        vn = (nh * lg_ref[...] + lb_ref[...]).astype(BF)
        yb_ref[...] = (gu * _sgu_mix(vn, wm_ref, bsb_ref, ts)).astype(BF)

    sw = (SGU_GROUPS, SGU_BLOCK, SGU_BLOCK)
    return pl.pallas_call(
        body, name="sgu_fwd", grid=(s // ts,),
        in_specs=[_tile_spec(ts, 2 * D_RNN, 1), _full_spec(sw), _full_spec(sw),
                  _full_spec((1, D_SGU)), _full_spec((1, D_SGU))],
        out_specs=_tile_spec(ts, D_SGU),
        out_shape=jax.ShapeDtypeStruct((s, D_SGU), BF),
        compiler_params=_params(("parallel",)),
    )(proj, wm, bsb, lg, lb)


_GATE_COL0 = (2 * D_RNN + 2 * D_SGU) // 512


def _gate_specs(ts):
    return [_tile_spec(ts, 512, _GATE_COL0 + j) for j in range(4)]


def _merge_call(x, proj, ya_pre, yb_pre, w_ba, w_bb, w_out, g2, layer, ts):
    s = x.shape[0]

    def body(x_ref, ga0, ga1, gb0, gb1, ya_ref, yb_ref, wa_ref, wb_ref, wo_ref, g2_ref,
             x1_ref, yao_ref, ybo_ref, mg_ref, h2_ref):
        ya = _dot(ya_ref[...], wa_ref[...])
        yb = _dot(yb_ref[...], wb_ref[...])
        sa = jax.nn.sigmoid(jnp.concatenate([ga0[...], ga1[...]], axis=1).astype(F32))
        sb = jax.nn.sigmoid(jnp.concatenate([gb0[...], gb1[...]], axis=1).astype(F32))
        merged = (sa * ya + sb * yb).astype(BF)
        x1 = x_ref[...] + _dot(merged, wo_ref[...])
        x1_ref[...] = x1
        yao_ref[...] = ya.astype(BF)
        ybo_ref[...] = yb.astype(BF)
        mg_ref[...] = merged
        h2_ref[...] = (x1 * _rms_stats(x1) * g2_ref[...]).astype(BF)

    act = jax.ShapeDtypeStruct((s, D_MODEL), BF)
    return pl.pallas_call(
        body, name="merge_fwd", grid=(s // ts,),
        in_specs=[_tile_spec(ts, D_MODEL)] + _gate_specs(ts) + [
            _tile_spec(ts, D_RNN), _tile_spec(ts, D_SGU),
            _layer_spec(w_ba, layer), _layer_spec(w_bb, layer), _layer_spec(w_out, layer),
            _full_spec((1, D_MODEL))],
        out_specs=[_tile_spec(ts, D_MODEL)] * 5,
        out_shape=[jax.ShapeDtypeStruct((s, D_MODEL), F32), act, act, act, act],
        compiler_params=_params(("parallel",)),
    )(x, proj, proj, proj, proj, ya_pre, yb_pre, w_ba, w_bb, w_out, g2)


def _ffn_call(x1, h2, w_up, w_down, layer, ts):
    s = x1.shape[0]

    def body(x1_ref, h2_ref, wu_ref, wd_ref, x2_ref, p_ref):
        h2v = h2_ref[...]
        acc = x1_ref[...]
        for q in range(N_QUARTERS):
            p = _dot(h2v, wu_ref[q])
            p_ref[:, q * Q_FF:(q + 1) * Q_FF] = p.astype(BF)
            f = jnp.square(jnp.maximum(p, 0.0)).astype(BF)
            acc = acc + _dot(f, wd_ref[q * Q_FF:(q + 1) * Q_FF, :])
        x2_ref[...] = acc

    return pl.pallas_call(
        body, name="ffn_fwd", grid=(s // ts,),
        in_specs=[_tile_spec(ts, D_MODEL), _tile_spec(ts, D_MODEL),
                  pl.BlockSpec((None, N_QUARTERS, D_MODEL, Q_FF), lambda i: (layer, 0, 0, 0)),
                  pl.BlockSpec((None, D_FF, D_MODEL), lambda i: (layer, 0, 0))],
        out_specs=[_tile_spec(ts, D_MODEL), _tile_spec(ts, D_FF)],
        out_shape=[jax.ShapeDtypeStruct((s, D_MODEL), F32), jax.ShapeDtypeStruct((s, D_FF), BF)],
        compiler_params=_params(("parallel",)),
    )(x1, h2, w_up, w_down)


def _loss_call(x, target, gf, ts):
    s = x.shape[0]

    def body(x_ref, t_ref, g_ref, dx_ref, loss_ref, dg_ref):
        @pl.when(pl.program_id(0) == 0)
        def _():
            loss_ref[...] = jnp.zeros_like(loss_ref)
            dg_ref[...] = jnp.zeros_like(dg_ref)

        xv = x_ref[...]
        gv = g_ref[...]
        err = xv * _rms_stats(xv) * gv - t_ref[...]
        part = 0.5 * jnp.sum(jnp.mean(err * err, axis=-1, keepdims=True), axis=0, keepdims=True)
        loss_ref[...] += jnp.broadcast_to(part, loss_ref.shape)
        dx, dg = _rms_bwd(err * (1.0 / D_MODEL), xv, gv)
        dx_ref[...] = dx
        dg_ref[...] += _row_sum(dg)

    return pl.pallas_call(
        body, name="loss_head", grid=(s // ts,),
        in_specs=[_tile_spec(ts, D_MODEL), _tile_spec(ts, D_MODEL), _full_spec((1, D_MODEL))],
        out_specs=[_tile_spec(ts, D_MODEL), _full_spec((1, 128)), _full_spec((1, D_MODEL))],
        out_shape=[jax.ShapeDtypeStruct((s, D_MODEL), F32), jax.ShapeDtypeStruct((1, 128), F32),
                   jax.ShapeDtypeStruct((1, D_MODEL), F32)],
        compiler_params=_params(("arbitrary",)),
    )(x, target, gf)


def _ffn_bwd_call(dx2, p, x1, g2, w_up, w_down, layer, ts):
    s = dx2.shape[0]

    def body(dx2_ref, p_ref, x1_ref, g2_ref, wu_ref, wd_ref, dx1_ref, dp_ref, dg_ref):
        @pl.when(pl.program_id(0) == 0)
        def _():
            dg_ref[...] = jnp.zeros_like(dg_ref)

        dx2v = dx2_ref[...]
        dyb = dx2v.astype(BF)
        dh2 = jnp.zeros((ts, D_MODEL), F32)
        for q in range(N_QUARTERS):
            cols = slice(q * Q_FF, (q + 1) * Q_FF)
            df = _dot_nt(dyb, wd_ref[cols, :])
            dp = (df * (2.0 * jnp.maximum(p_ref[:, cols].astype(F32), 0.0))).astype(BF)
            dp_ref[:, cols] = dp
            dh2 = dh2 + _dot_nt(dp, wu_ref[q])
        dx, dg = _rms_bwd(dh2, x1_ref[...], g2_ref[...])
        dx1_ref[...] = dx2v + dx
        dg_ref[...] += _row_sum(dg)

    return pl.pallas_call(
        body, name="ffn_bwd", grid=(s // ts,),
        in_specs=[_tile_spec(ts, D_MODEL), _tile_spec(ts, D_FF), _tile_spec(ts, D_MODEL),
                  _full_spec((1, D_MODEL)),
                  pl.BlockSpec((None, N_QUARTERS, D_MODEL, Q_FF), lambda i: (layer, 0, 0, 0)),
                  pl.BlockSpec((None, D_FF, D_MODEL), lambda i: (layer, 0, 0))],
        out_specs=[_tile_spec(ts, D_MODEL), _tile_spec(ts, D_FF), _full_spec((1, D_MODEL))],
        out_shape=[jax.ShapeDtypeStruct((s, D_MODEL), F32), jax.ShapeDtypeStruct((s, D_FF), BF),
                   jax.ShapeDtypeStruct((1, D_MODEL), F32)],
        compiler_params=_params(("arbitrary",)),
    )(dx2, p, x1, g2, w_up, w_down)


def _merge_bwd_call(dx1, proj, ya, yb, w_ba, w_bb, w_out, layer, ts, after=None):
    s = dx1.shape[0]

    def body(dx1_ref, ga0, ga1, gb0, gb1, ya_ref, yb_ref, wa_ref, wb_ref, wo_ref, *rest):
        dya_ref, dyb_ref, dgate_ref, dyap_ref, dybp_ref = rest[-5:]
        dm = _dot_nt(dx1_ref[...].astype(BF), wo_ref[...])
        sa = jax.nn.sigmoid(jnp.concatenate([ga0[...], ga1[...]], axis=1).astype(F32))
        sb = jax.nn.sigmoid(jnp.concatenate([gb0[...], gb1[...]], axis=1).astype(F32))
        dya = (dm * sa).astype(BF)
        dyb = (dm * sb).astype(BF)
        dya_ref[...] = dya
        dyb_ref[...] = dyb
        dgate_ref[:, :D_MODEL] = (dm * ya_ref[...].astype(F32) * sa * (1.0 - sa)).astype(BF)
        dgate_ref[:, D_MODEL:] = (dm * yb_ref[...].astype(F32) * sb * (1.0 - sb)).astype(BF)
        dyap_ref[...] = _dot_nt(dya, wa_ref[...]).astype(BF)
        dybp_ref[...] = _dot_nt(dyb, wb_ref[...]).astype(BF)

    act = jax.ShapeDtypeStruct((s, D_MODEL), BF)
    return pl.pallas_call(
        body, name="merge_bwd", grid=(s // ts,),
        in_specs=[_tile_spec(ts, D_MODEL)] + _gate_specs(ts) + [
            _tile_spec(ts, D_MODEL), _tile_spec(ts, D_MODEL),
            _layer_spec(w_ba, layer), _layer_spec(w_bb, layer), _layer_spec(w_out, layer)]
        + ([] if after is None else [pl.BlockSpec(memory_space=pl.ANY)]),
        out_specs=[_tile_spec(ts, D_MODEL), _tile_spec(ts, D_MODEL), _tile_spec(ts, 2 * D_MODEL),
                   _tile_spec(ts, D_RNN), _tile_spec(ts, D_SGU)],
        out_shape=[act, act, jax.ShapeDtypeStruct((s, 2 * D_MODEL), BF),
                   jax.ShapeDtypeStruct((s, D_RNN), BF), jax.ShapeDtypeStruct((s, D_SGU), BF)],
        compiler_params=_params(("parallel",)),
    )(dx1, proj, proj, proj, proj, ya, yb, w_ba, w_bb, w_out, *([] if after is None else [after]))


def _sgu_bwd_call(dyb_pre, proj, wm, bsb, mask, lg, lb, ts):
    s = proj.shape[0]

    def body(dy_ref, uv_ref, wm_ref, bsb_ref, mask_ref, lg_ref, lb_ref,
             duv_ref, dws_ref, dbs_ref, dlg_ref, dlb_ref, dm_sc):
        step = pl.program_id(0)

        @pl.when(step == 0)
        def _():
            dws_ref[...] = jnp.zeros_like(dws_ref)
            dlg_ref[...] = jnp.zeros_like(dlg_ref)
            dlb_ref[...] = jnp.zeros_like(dlb_ref)
            dm_sc[...] = jnp.zeros_like(dm_sc)

        gu, dgu_du = _gelu_and_grad(uv_ref[:, :D_SGU])
        gv, dgv_dv = _gelu_and_grad(uv_ref[:, D_SGU:2 * D_SGU])
        nh, rstd = _layernorm_fwd(gv.astype(F32))
        lgv = lg_ref[...]
        vn = (nh * lgv + lb_ref[...]).astype(BF)
        dy = dy_ref[...].astype(F32)
        du = dy * _sgu_mix(vn, wm_ref, bsb_ref, ts) * dgu_du
        dmix = dy * gu
        dmix_bf = dmix.astype(BF)
        dm_acc = dm_sc[...]
        rows = []
        for blk in range(ts // SGU_BLOCK):
            r0 = blk * SGU_BLOCK
            dm_acc = dm_acc + dmix[r0:r0 + SGU_BLOCK, :]
            cols = []
            for g in range(SGU_GROUPS):
                c0 = g * SGU_BLOCK
                dmg = dmix_bf[r0:r0 + SGU_BLOCK, c0:c0 + SGU_BLOCK]
                cols.append(_dot_tn(wm_ref[g], dmg))
                dws_ref[g] += mask_ref[...] * _dot_nt(dmg, vn[r0:r0 + SGU_BLOCK, c0:c0 + SGU_BLOCK])
            rows.append(jnp.concatenate(cols, axis=1))
        dm_sc[...] = dm_acc
        dvn = jnp.concatenate(rows, axis=0)
        dlg_ref[...] += _row_sum(dvn * nh)
        dlb_ref[...] += _row_sum(dvn)
        dnh = dvn * lgv
        dgv = rstd * (dnh - jnp.mean(dnh, axis=-1, keepdims=True)
                      - nh * jnp.mean(dnh * nh, axis=-1, keepdims=True))
        duv_ref[:, :D_SGU] = du.astype(BF)
        duv_ref[:, D_SGU:] = (dgv * dgv_dv).astype(BF)

        @pl.when(step == pl.num_programs(0) - 1)
        def _():
            for g in range(SGU_GROUPS):
                dbs_ref[:, g:g + 1] = jnp.sum(
                    dm_acc[:, g * SGU_BLOCK:(g + 1) * SGU_BLOCK], axis=1, keepdims=True)

    sw = (SGU_GROUPS, SGU_BLOCK, SGU_BLOCK)
    return pl.pallas_call(
        body, name="sgu_bwd", grid=(s // ts,),
        in_specs=[_tile_spec(ts, D_SGU), _tile_spec(ts, 2 * D_RNN, 1), _full_spec(sw), _full_spec(sw),
                  _full_spec((SGU_BLOCK, SGU_BLOCK)), _full_spec((1, D_SGU)), _full_spec((1, D_SGU))],
        out_specs=[_tile_spec(ts, 2 * D_SGU), _full_spec(sw), _full_spec((SGU_BLOCK, SGU_GROUPS)),
                   _full_spec((1, D_SGU)), _full_spec((1, D_SGU))],
        out_shape=[jax.ShapeDtypeStruct((s, 2 * D_SGU), BF), jax.ShapeDtypeStruct(sw, F32),
                   jax.ShapeDtypeStruct((SGU_BLOCK, SGU_GROUPS), F32),
                   jax.ShapeDtypeStruct((1, D_SGU), F32), jax.ShapeDtypeStruct((1, D_SGU), F32)],
        scratch_shapes=[pltpu.VMEM((SGU_BLOCK, D_SGU), F32)],
        compiler_params=_params(("arbitrary",)),
    )(dyb_pre, proj, wm, bsb, mask, lg, lb)


_ROW_DBA, _ROW_DBX, _ROW_DSP, _ROW_DCB, _ROW_DCW = 0, 1, 2, 3, 4
_PREV_ROWS = 16


def _rnn_bwd_call(dya_pre, proj, xr_saved, hr, wa, wx, ba, bx, sp, cw, ts):
    s = proj.shape[0]
    nt = s // ts
    per = ts // _PREV_ROWS

    def tile(i):
        return nt - 1 - i

    def prev(i):
        return jnp.maximum(tile(i) * per - 1, 0)

    def body(dy_ref, xg_ref, xr_ref, hr_ref, hrp_ref, wa_ref, wx_ref, ba_ref, bx_ref, sp_ref,
             cw_ref, dxg_ref, dwa_ref, dwx_ref, vec_ref,
             lam_carry, a_first, dxr_head, al_sc, bl_sc, lam_sc):
        step = pl.program_id(0)

        @pl.when(step == 0)
        def _():
            dwa_ref[...] = jnp.zeros_like(dwa_ref)
            dwx_ref[...] = jnp.zeros_like(dwx_ref)
            vec_ref[...] = jnp.zeros_like(vec_ref)
            lam_carry[...] = jnp.zeros_like(lam_carry)
            a_first[...] = jnp.zeros_like(a_first)
            dxr_head[...] = jnp.zeros_like(dxr_head)

        has_prev = (step < nt - 1).astype(F32)
        x = xg_ref[:, :D_RNN].astype(F32)
        g = xg_ref[:, D_RNN:]
        h_tail =hrp_ref[_PREV_ROWS - SUBLANES:, :].astype(F32) * has_prev
        xr = xr_ref[...].astype(F32)
        r, i, a, nrm, inv_nrm = _lru_gates(xr, wa_ref, wx_ref, ba_ref, bx_ref, sp_ref)
        h = hr_ref[...].astype(F32)
        dy = dy_ref[...].astype(F32)
        gg, dgg = _gelu_and_grad(g)

        coef = _shift_up(a, jnp.broadcast_to(a_first[...], (SUBLANES, D_RNN)), 1)
        lam_carry[...] = _linear_scan(coef, dy * gg, lam_carry[...], al_sc, bl_sc, lam_sc, True)
        a_first[...] = a[0:1, :]
        lam = lam_sc[...]

        da = lam * _shift_down(h, h_tail, 1)
        dnrm = lam * (i * xr)
        di = lam * nrm * xr
        dlog_a = da * a - dnrm * (a * a) * inv_nrm
        spv = sp_ref[...]
        dza = (dlog_a * (-LRU_C * spv)) * (r * (1.0 - r))
        dzx = di * (i * (1.0 - i))
        vec_ref[_ROW_DSP:_ROW_DSP + 1, :] += _row_sum(dlog_a * (-LRU_C * r))
        vec_ref[_ROW_DBA:_ROW_DBA + 1, :] += _row_sum(dza)
        vec_ref[_ROW_DBX:_ROW_DBX + 1, :] += _row_sum(dzx)
        xb = xr.astype(BF)
        dza_bf = dza.astype(BF)
        dzx_bf = dzx.astype(BF)
        for grp in range(N_LRU_GROUPS):
            cols = slice(grp * LRU_GROUP, (grp + 1) * LRU_GROUP)
            dwa_ref[grp] += _dot_tn(xb[:, cols], dza_bf[:, cols])
            dwx_ref[grp] += _dot_tn(xb[:, cols], dzx_bf[:, cols])
        dxr = (lam * nrm * i + _group_dot(dza_bf, wa_ref, _dot_nt) + _group_dot(dzx_bf, wx_ref, _dot_nt))

        vec_ref[_ROW_DCB:_ROW_DCB + 1, :] += _row_sum(dxr)
        head = dxr_head[...]
        dx = cw_ref[CONV_WIDTH - 1:CONV_WIDTH, :] * dxr
        vec_ref[_ROW_DCW + 3:_ROW_DCW + 4, :] += _row_sum(dxr * x)
        for sft in range(1, CONV_WIDTH):
            k = CONV_WIDTH - 1 - sft
            ahead = _shift_up(dxr, head, sft)
            dx = dx + cw_ref[k:k + 1, :] * ahead
            vec_ref[_ROW_DCW + k:_ROW_DCW + k + 1, :] += _row_sum(ahead * x)
        dxr_head[...] = dxr[0:SUBLANES, :]
        dxg_ref[:, :D_RNN] = dx.astype(BF)
        dxg_ref[:, D_RNN:] = (dy * h * dgg).astype(BF)

    gw = (N_LRU_GROUPS, LRU_GROUP, LRU_GROUP)
    rev = lambda width: pl.BlockSpec((ts, width), lambda i: (tile(i), 0))
    return pl.pallas_call(
        body, name="rnn_bwd", grid=(nt,),
        in_specs=[rev(D_RNN), rev(2 * D_RNN), rev(D_RNN), rev(D_RNN),
                  pl.BlockSpec((_PREV_ROWS, D_RNN), lambda i: (prev(i), 0)),
                  _full_spec(gw), _full_spec(gw),
                  _full_spec((1, D_RNN)), _full_spec((1, D_RNN)), _full_spec((1, D_RNN)),
                  _full_spec((CONV_WIDTH, D_RNN))],
        out_specs=[rev(2 * D_RNN), _full_spec(gw), _full_spec(gw), _full_spec((SUBLANES, D_RNN))],
        out_shape=[jax.ShapeDtypeStruct((s, 2 * D_RNN), BF), jax.ShapeDtypeStruct(gw, F32),
                   jax.ShapeDtypeStruct(gw, F32), jax.ShapeDtypeStruct((SUBLANES, D_RNN), F32)],
        scratch_shapes=[pltpu.VMEM((1, D_RNN), F32), pltpu.VMEM((1, D_RNN), F32),
                        pltpu.VMEM((SUBLANES, D_RNN), F32),
                        pltpu.VMEM((ts, D_RNN), F32), pltpu.VMEM((ts, D_RNN), F32),
                        pltpu.VMEM((ts, D_RNN), F32)],
        compiler_params=_params(("arbitrary",)),
    )(dya_pre, proj, xr_saved, hr, hr, wa, wx, ba, bx, sp, cw)


def _inproj_bwd_call(dxg, duv, dgate, dx1, x, g1, w_in, layer, ts):
    s = x.shape[0]

    def body(dxg_ref, duv_ref, dgt_ref, dx1_ref, x_ref, g_ref, w_ref, dx_ref, dproj_ref, dg_ref):
        @pl.when(pl.program_id(0) == 0)
        def _():
            dg_ref[...] = jnp.zeros_like(dg_ref)

        dproj = jnp.concatenate([dxg_ref[...], duv_ref[...], dgt_ref[...]], axis=1)
        dproj_ref[...] = dproj
        dh = jnp.zeros((ts, D_MODEL), F32)
        for q in range(N_QUARTERS):
            dh = dh + _dot_nt(dproj[:, q * Q_IN:(q + 1) * Q_IN], w_ref[q])
        dx, dg = _rms_bwd(dh, x_ref[...], g_ref[...])
        dx_ref[...] = dx1_ref[...] + dx
        dg_ref[...] += _row_sum(dg)

    return pl.pallas_call(
        body, name="inproj_bwd", grid=(s // ts,),
        in_specs=[_tile_spec(ts, 2 * D_RNN), _tile_spec(ts, 2 * D_SGU), _tile_spec(ts, 2 * D_MODEL),
                  _tile_spec(ts, D_MODEL), _tile_spec(ts, D_MODEL), _full_spec((1, D_MODEL)),
                  pl.BlockSpec((None, N_QUARTERS, D_MODEL, Q_IN), lambda i: (layer, 0, 0, 0))],
        out_specs=[_tile_spec(ts, D_MODEL), _tile_spec(ts, D_IN), _full_spec((1, D_MODEL))],
        out_shape=[jax.ShapeDtypeStruct((s, D_MODEL), F32), jax.ShapeDtypeStruct((s, D_IN), BF),
                   jax.ShapeDtypeStruct((1, D_MODEL), F32)],
        compiler_params=_params(("arbitrary",)),
    )(dxg, duv, dgate, dx1, x, g1, w_in)


def _relu_sq(p):
    return jnp.square(jnp.maximum(p, 0))


def _wgrad_call(a, b, core, tm, tn, tk, col_blocked, name, a_fn=None):
    s, m = a.shape
    n = b.shape[1]
    r, cols = (m, n // N_QUARTERS) if col_blocked else (m // N_QUARTERS, n)
    r2 = r // 2
    per_tile = tm // r
    steps = s // tk

    def body(core_ref, a_ref, b_ref, keep_ref, send_ref, *acc):
        av = a_ref[...]
        if a_fn is not None:
            av = a_fn(av)
        prod = _dot_tn(av.astype(BF), b_ref[...].astype(BF))

        def emit(total):
            for h in range(2):
                @pl.when(core_ref[0] == h)
                def _():
                    for q in range(per_tile):
                        keep_ref[q] = total[q * r + h * r2:q * r + (h + 1) * r2]
                        send_ref[q] = total[q * r + (1 - h) * r2:q * r + (2 - h) * r2].astype(BF)

        if steps == 1:
            emit(prod)
        else:
            acc_ref, = acc
            step = pl.program_id(2)

            @pl.when(step == 0)
            def _():
                acc_ref[...] = prod

            @pl.when(jnp.logical_and(step > 0, step < steps - 1))
            def _():
                acc_ref[...] += prod

            @pl.when(step == steps - 1)
            def _():
                emit(acc_ref[...] + prod)

    if col_blocked:
        per_q = cols // tn
        out_spec = pl.BlockSpec((1, r2, tn), lambda i, j, k, c: (j // per_q, 0, j % per_q))
    else:
        out_spec = pl.BlockSpec((per_tile, r2, tn), lambda i, j, k, c: (i, 0, j))
    return pl.pallas_call(
        body, name=name,
        out_shape=[jax.ShapeDtypeStruct((N_QUARTERS, r2, cols), F32),
                   jax.ShapeDtypeStruct((N_QUARTERS, r2, cols), BF)],
        grid_spec=pltpu.PrefetchScalarGridSpec(
            num_scalar_prefetch=1, grid=(m // tm, n // tn, steps),
            in_specs=[pl.BlockSpec((tk, tm), lambda i, j, k, c: (k, i)),
                      pl.BlockSpec((tk, tn), lambda i, j, k, c: (k, j))],
            out_specs=[out_spec, out_spec],
            scratch_shapes=[] if steps == 1 else [pltpu.VMEM((tm, tn), F32)]),
        compiler_params=_params(("parallel", "parallel", "arbitrary")),
    )(core, a, b)


BIG = ("w_in", "w_up", "w_down", "w_branch_a", "w_branch_b", "w_out")


def _block_diag(w):
    w4 = w.reshape(N_LRU_GROUPS, HEADS_PER_GROUP, RNN_HEAD_DIM, RNN_HEAD_DIM)
    eye = jnp.eye(HEADS_PER_GROUP, dtype=w.dtype)
    return jnp.einsum("gjio,jk->gjiko", w4, eye).reshape(N_LRU_GROUPS, LRU_GROUP, LRU_GROUP)


def _block_diag_extract(d):
    d5 = d.reshape(N_LRU_GROUPS, HEADS_PER_GROUP, RNN_HEAD_DIM, HEADS_PER_GROUP, RNN_HEAD_DIM)
    blocks = [d5[:, j, :, j, :] for j in range(HEADS_PER_GROUP)]
    return jnp.stack(blocks, axis=1).reshape(RNN_HEADS, RNN_HEAD_DIM, RNN_HEAD_DIM)


def _sgu_mask():
    chunk = jnp.arange(SGU_BLOCK) // CHUNK
    return (chunk[:, None] >= chunk[None, :]).astype(F32)


def _layer_small(sm, l, core):
    row = lambda v: v.reshape(1, -1)
    return dict(
        core=core,
        g1=row(sm["norm_mix_g"][l]), g2=row(sm["norm_ffn_g"][l]),
        wa=_block_diag(sm["lru_w_a"][l]).astype(BF), wx=_block_diag(sm["lru_w_x"][l]).astype(BF),
        ba=row(sm["lru_b_a"][l]), bx=row(sm["lru_b_x"][l]),
        sp=row(jax.nn.softplus(-sm["lru_lambda"][l])),
        cw=sm["conv_w"][l], cb=row(sm["conv_b"][l]),
        wm=(sm["sgu_w_s"][l] * _sgu_mask()).astype(BF),
        bsb=jnp.broadcast_to(sm["sgu_b_s"][l][:, :, None], (SGU_GROUPS, SGU_BLOCK, SGU_BLOCK)),
        lg=row(sm["sgu_ln_g"][l]), lb=row(sm["sgu_ln_b"][l]),
    )


def _layer_fwd_mix(x, big, p, ts, h=None, before_sgu=None, proj=None):
    if h is None:
        h = _norm_call(x, p["g1"], ts)
    if proj is None:
        proj = _inproj_call(h, big["w_in"], 0, 2 * ts)
    xr, hr, ya_pre = _rnn_fwd_call(proj, p["wa"], p["wx"], p["ba"], p["bx"], p["sp"], p["cw"], p["cb"], ts)
    lg = p["lg"] if before_sgu is None else p["lg"] + before_sgu(ya_pre)
    yb_pre = _sgu_fwd_call(proj, p["wm"], p["bsb"], lg, p["lb"], ts)
    return dict(p=p, x=x, h=h, proj=proj, xr=xr, hr=hr, ya_pre=ya_pre, yb_pre=yb_pre)


def _layer_fwd_out(sv, big, ts):
    x1, ya, yb, merged, h2 = _merge_call(sv["x"], sv["proj"], sv["ya_pre"], sv["yb_pre"], big["w_branch_a"],
                                         big["w_branch_b"], big["w_out"], sv["p"]["g2"], 0, ts)
    x2, pre = _ffn_call(x1, h2, big["w_up"], big["w_down"], 0, ts)
    sv.update(x1=x1, ya=ya, yb=yb, merged=merged, h2=h2, pre=pre)
    return x2


def _layer_bwd_ffn(dx, sv, big, ts):
    p = sv["p"]
    dx1, dpre, dg2 = _ffn_bwd_call(dx, sv["pre"], sv["x1"], p["g2"], big["w_up"], big["w_down"], 0, ts)
    tk = dx.shape[0]
    gb = dict(
        w_down=_wgrad_call(sv["pre"], dx, p["core"], Q_FF, D_MODEL // 2, tk, False, "wgrad_down", a_fn=_relu_sq),
        w_up=_wgrad_call(sv["h2"], dpre, p["core"], D_MODEL, Q_FF, tk, True, "wgrad_up"))
    return dx1, gb, dict(norm_ffn_g=dg2[0])


def _layer_bwd_merge(dx1, sv, big, ts, after=None):
    tk = dx1.shape[0]
    core = sv["p"]["core"]
    dya, dyb, dgate, dya_pre, dyb_pre = _merge_bwd_call(
        dx1, sv["proj"], sv["ya"], sv["yb"], big["w_branch_a"], big["w_branch_b"], big["w_out"], 0, ts, after)
    gb = dict(
        w_out=_wgrad_call(sv["merged"], dx1, core, D_MODEL, D_MODEL // 2, tk, False, "wgrad_out"),
        w_branch_a=_wgrad_call(sv["ya_pre"], dya, core, D_RNN, D_MODEL // 2, tk, False, "wgrad_branch_a"),
        w_branch_b=_wgrad_call(sv["yb_pre"], dyb, core, D_SGU, D_MODEL // 2, tk, False, "wgrad_branch_b"))
    return (dgate, dya_pre, dyb_pre), gb


def _layer_bwd_branches(dx1, merge_out, sv, big, lam, ts, after_sgu=None):
    p = sv["p"]
    tk = dx1.shape[0]
    dgate, dya_pre, dyb_pre = merge_out
    gb = {}
    duv, dws, dbs, dlg, dlb = _sgu_bwd_call(dyb_pre, sv["proj"], p["wm"], p["bsb"], _sgu_mask(), p["lg"], p["lb"],
                                            ts)
    ba = p["ba"] if after_sgu is None else p["ba"] + after_sgu(duv)
    dxg, dwa, dwx, vec = _rnn_bwd_call(dya_pre, sv["proj"], sv["xr"], sv["hr"], p["wa"], p["wx"], ba, p["bx"],
                                       p["sp"], p["cw"], ts // 2)
    dx, dproj, dg1 = _inproj_bwd_call(dxg, duv, dgate, dx1, sv["x"], p["g1"], big["w_in"], 0, ts)
    gb["w_in"] = _wgrad_call(sv["h"], dproj, p["core"], D_MODEL, Q_IN, tk // 2, True, "wgrad_in")
    gs = dict(
        norm_mix_g=dg1[0], conv_w=vec[_ROW_DCW:_ROW_DCW + CONV_WIDTH], conv_b=vec[_ROW_DCB],
        lru_w_a=_block_diag_extract(dwa), lru_w_x=_block_diag_extract(dwx),
        lru_b_a=vec[_ROW_DBA].reshape(RNN_HEADS, RNN_HEAD_DIM), lru_b_x=vec[_ROW_DBX].reshape(RNN_HEADS, RNN_HEAD_DIM),
        lru_lambda=-vec[_ROW_DSP] * jax.nn.sigmoid(-lam),
        sgu_ln_g=dlg[0], sgu_ln_b=dlb[0], sgu_w_s=dws, sgu_b_s=dbs.T)
    return dx, gb, gs


def _local_step(x, target, big, sm, ts):
    saved = []
    core = jnp.zeros((1,), jnp.int32)
    for l in range(DEPTH):
        sv = _layer_fwd_mix(x, big[l], _layer_small(sm, l, core), ts)
        x = _layer_fwd_out(sv, big[l], ts)
        saved.append(sv)
    dx, loss, dgf = _loss_call(x, target, sm["final_norm_g"].reshape(1, -1), ts)
    gb, gs = [None] * DEPTH, [None] * DEPTH
    for l in reversed(range(DEPTH)):
        dx1, gb_ffn, gs_ffn = _layer_bwd_ffn(dx, saved[l], big[l], ts)
        merge_out, gb_merge = _layer_bwd_merge(dx1, saved[l], big[l], ts)
        dx, gb_mix, gs_mix = _layer_bwd_branches(dx1, merge_out, saved[l], big[l], sm["lru_lambda"][l], ts)
        gb[l] = {**gb_ffn, **gb_merge, **gb_mix}
        gs[l] = {**gs_ffn, **gs_mix}
    gs = {k: jnp.stack([g[k] for g in gs]) for k in gs[0]}
    gs["final_norm_g"] = dgf[0]
    return loss, dx, gb, gs


EW_VMEM_BYTES = 24 * 1024 * 1024


def _row_block(rows, cols, bytes_per_elem):
    for br in range(min(rows, EW_VMEM_BYTES // (2 * bytes_per_elem * cols)), 0, -1):
        if rows % br == 0 and br % 16 == 0:
            return br
    return rows


def _ew_call(fn, name, operands, outputs, slabs=1, sel=None, into=None, after=None):
    if into is not None and not isinstance(into, (list, tuple)):
        into = [into]
    rows, cols = outputs[0][0].shape[2:]
    br = _row_block(rows, cols, sum(jnp.dtype(a.dtype).itemsize for a, _ in operands + outputs))
    n_in = len(operands)

    def pick(tok, g, s):
        if callable(tok):
            return tok(g, s)
        if tok == "g":
            return g
        if isinstance(tok, tuple):
            return s[tok[1]]
        return tok

    def spec(idx):
        return pl.BlockSpec((None, None, br, cols),
                            lambda g, i, s, idx=idx: (pick(idx[0], g, s), pick(idx[1], g, s), i, 0))

    if sel is None:
        sel = jnp.zeros((1,), jnp.int32)
    in_specs = [spec(idx) for _, idx in operands]
    arrays = [a for a, _ in operands]
    aliases = {}
    for j, buf in enumerate(into or ()):
        in_specs.append(pl.BlockSpec(memory_space=pl.ANY))
        arrays.append(buf)
        aliases[1 + n_in + j] = j
    if after is not None:
        in_specs.append(pl.BlockSpec(memory_space=pl.ANY))
        arrays.append(after)

    def body(sel_ref, *refs):
        outs = fn(*[r[...] for r in refs[:n_in]])
        for o_ref, o in zip(refs[len(arrays):], outs):
            o_ref[...] = o.astype(o_ref.dtype)

    return pl.pallas_call(
        body, name=name, out_shape=[s for s, _ in outputs],
        grid_spec=pltpu.PrefetchScalarGridSpec(
            num_scalar_prefetch=1, grid=(slabs, rows // br),
            in_specs=in_specs,
            out_specs=[spec(idx) for _, idx in outputs]),
        input_output_aliases=aliases,
        compiler_params=_params(("parallel", "parallel")),
    )(sel, *arrays)


def _as4(a):
    return a.reshape((1,) * (4 - a.ndim) + a.shape)


def _adamw(w, g, m, v):
    m = ADAM_B1 * m + (1.0 - ADAM_B1) * g
    v = ADAM_B2 * v + (1.0 - ADAM_B2) * jnp.square(g)
    m_hat = m / (1.0 - ADAM_B1 ** ADAM_STEP)
    v_hat = v / (1.0 - ADAM_B2 ** ADAM_STEP)
    delta = -ADAM_LR * (m_hat / (jnp.sqrt(v_hat) + ADAM_EPS) + ADAM_WD * w)
    return delta, m, v


def _small_adamw_call(ws, gs, ms, vs):
    n = len(ws)

    def body(*refs):
        for k in range(n):
            w, g, m, v = (refs[j * n + k][...] for j in range(4))
            outs = _adamw(w, g, m, v)
            for j in range(3):
                refs[(4 + j) * n + k][...] = outs[j]

    shapes = [jax.ShapeDtypeStruct(w.shape, F32) for w in ws]
    outs = pl.pallas_call(
        body, name="adamw_small", out_shape=shapes * 3,
        in_specs=[pl.BlockSpec(memory_space=pltpu.VMEM)] * (4 * n),
        out_specs=[pl.BlockSpec(memory_space=pltpu.VMEM)] * (3 * n),
        compiler_params=_params(),
    )(*ws, *gs, *ms, *vs)
    return outs[:n], outs[n:2 * n], outs[2 * n:]


ANY = pl.BlockSpec(memory_space=pl.ANY)


def _place():
    x, y, c = lax.axis_index("x"), lax.axis_index("y"), lax.axis_index("c")
    chips = [(1 - x, y), (x, 1 - y), (1 - x, 1 - y)]
    return x, y, c, chips


def _remote(src, dst, send_sem, recv_sem, to):
    return pltpu.make_async_remote_copy(src_ref=src, dst_ref=dst, send_sem=send_sem, recv_sem=recv_sem,
                                        device_id=to, device_id_type=MESH)


def _gather_call(bufs):
    n = len(bufs)

    def body(*refs):
        out = refs[n:2 * n]
        send_sems, recv_sems = refs[2 * n:]
        x, y, c, chips = _place()
        me_q = 2 * x + y
        sibling = (x, y, 1 - c)
        first = []
        for w in range(n):
            for j, chip in enumerate(chips):
                mine = out[w].at[c, me_q]
                first.append(_remote(mine, mine, send_sems.at[w * 3 + j], recv_sems.at[w * 3 + j], (*chip, c)))
        for cp in first:
            cp.start()
        passed = []
        for w in range(n):
            for j, (qx, qy) in enumerate(chips):
                landed = out[w].at[c, 2 * qx + qy]
                k = w * 3 + j
                _remote(landed, landed, send_sems.at[k], recv_sems.at[k], (qx, qy, c)).wait_recv()
                cp = _remote(landed, landed, send_sems.at[3 * n + k], recv_sems.at[3 * n + k], sibling)
                cp.start()
                passed.append(cp)
        for w in range(n):
            for j, (qx, qy) in enumerate(chips):
                landed = out[w].at[1 - c, 2 * qx + qy]
                k = 3 * n + w * 3 + j
                _remote(landed, landed, send_sems.at[k], recv_sems.at[k], sibling).wait_recv()
        for cp in first + passed:
            cp.wait_send()

    return pl.pallas_call(
        body, name="gather_weights",
        out_shape=[jax.ShapeDtypeStruct(a.shape, a.dtype) for a in bufs],
        in_specs=[ANY] * n, out_specs=[ANY] * n,
        input_output_aliases={w: w for w in range(n)},
        scratch_shapes=[pltpu.SemaphoreType.DMA((6 * n,)), pltpu.SemaphoreType.DMA((6 * n,))],
        compiler_params=_params(vmem=False, has_side_effects=True),
    )(*bufs)


def _sibling_send_call(items):
    n = len(items)

    def body(*refs):
        src, out = refs[:n], refs[n:2 * n]
        send_sems, recv_sems = refs[2 * n:]
        x, y, c, _ = _place()
        copies = [_remote(src[w], out[w], send_sems.at[w], recv_sems.at[w], (x, y, 1 - c)) for w in range(n)]
        for cp in copies:
            cp.start()
        for cp in copies:
            cp.wait()

    return pl.pallas_call(
        body, name="grads_to_sibling",
        out_shape=[jax.ShapeDtypeStruct(a.shape, a.dtype) for a in items],
        in_specs=[ANY] * n, out_specs=[ANY] * n,
        scratch_shapes=[pltpu.SemaphoreType.DMA((n,)), pltpu.SemaphoreType.DMA((n,))],
        compiler_params=_params(vmem=False, has_side_effects=True),
    )(*items)


def _sibling_inplace_call(name, bufs, slabs, n_pairs):
    n = len(bufs)

    def body(*refs):
        out = refs[n:2 * n]
        send_sems, recv_sems = refs[2 * n:]
        x, y, c, _ = _place()
        sibling = (x, y, 1 - c)
        pairs = [pair for w, ref in enumerate(out) for pair in slabs(ref, c, w)]
        sends = [_remote(s, s, send_sems.at[k], recv_sems.at[k], sibling) for k, (s, _) in enumerate(pairs)]
        for cp in sends:
            cp.start()
        for k, (_, r) in enumerate(pairs):
            _remote(r, r, send_sems.at[k], recv_sems.at[k], sibling).wait_recv()
        for cp in sends:
            cp.wait_send()

    return pl.pallas_call(
        body, name=name,
        out_shape=[jax.ShapeDtypeStruct(a.shape, a.dtype) for a in bufs],
        in_specs=[ANY] * n, out_specs=[ANY] * n,
        input_output_aliases={w: w for w in range(n)},
        scratch_shapes=[pltpu.SemaphoreType.DMA((n_pairs,)), pltpu.SemaphoreType.DMA((n_pairs,))],
        compiler_params=_params(vmem=False, has_side_effects=True),
    )(*bufs)


HBM_SPEC = pl.BlockSpec(memory_space=pltpu.HBM)
SEM_SPEC = pl.BlockSpec(memory_space=pltpu.SEMAPHORE)
DATAFLOW_EFFECT = pltpu.SideEffectType.DATAFLOW_SIDE_EFFECTING


def _exchange_start(name, bufs, copies, n_copies, after):
    n = len(bufs)

    def body(*refs):
        ins, send_sems, recv_sems, token = refs[:n], refs[n + 1], refs[n + 2], refs[-1]
        for k, (src, dst, to) in enumerate(copies(ins)):
            _remote(src, dst, send_sems.at[k], recv_sems.at[k], to).start()
        token[...] = jnp.zeros_like(token)

    outs = pl.pallas_call(
        body, name=name,
        out_shape=(pltpu.SemaphoreType.DMA((n_copies,)), pltpu.SemaphoreType.DMA((n_copies,)),
                   *[pltpu.HBM(b.shape, b.dtype) for b in bufs], jax.ShapeDtypeStruct((SUBLANES, 128), F32)),
        in_specs=[HBM_SPEC] * n + [ANY],
        out_specs=(SEM_SPEC, SEM_SPEC, *[HBM_SPEC] * n, pl.BlockSpec(memory_space=pltpu.VMEM)),
        input_output_aliases={w: w + 2 for w in range(n)},
        compiler_params=pltpu.CompilerParams(has_side_effects=DATAFLOW_EFFECT),
    )(*[pltpu.with_memory_space_constraint(b, pltpu.HBM) for b in bufs], after)
    return outs[0], outs[1], list(outs[2:2 + n]), outs[-1]


def _exchange_wait(name, send_sems, recv_sems, bufs, copies, after):
    n = len(bufs)

    def body(*refs):
        ins, send_sems, recv_sems = refs[:n], refs[n], refs[n + 1]
        for k, (src, dst, to) in enumerate(copies(ins)):
            cp = _remote(src, dst, send_sems.at[k], recv_sems.at[k], to)
            cp.wait_send()
            cp.wait_recv()

    return pl.pallas_call(
        body, name=name,
        out_shape=[pltpu.HBM(b.shape, b.dtype) for b in bufs],
        in_specs=[HBM_SPEC] * n + [SEM_SPEC, SEM_SPEC, ANY],
        out_specs=[HBM_SPEC] * n,
        input_output_aliases={w: w for w in range(n)},
        compiler_params=pltpu.CompilerParams(has_side_effects=DATAFLOW_EFFECT),
    )(*bufs, send_sems, recv_sems, after)


def _gather_copies(refs):
    x, y, c, chips = _place()
    mine = 2 * (2 * x + y) + c
    return [(ref.at[mine], ref.at[mine], (qx, qy, c)) for ref in refs for qx, qy in chips]


def _forward_copies(refs):
    x, y, c, chips = _place()
    return [(ref.at[2 * (2 * qx + qy) + c], ref.at[2 * (2 * qx + qy) + c], (x, y, 1 - c))
            for ref in refs for qx, qy in chips]


def _gather_forward_slabs(ref, c, w):
    x, y, _, chips = _place()
    return [(ref.at[2 * (2 * qx + qy) + c], ref.at[2 * (2 * qx + qy) + 1 - c]) for qx, qy in chips]


def _device_peers():
    x, y, c, _ = _place()
    return 4 * x + 2 * y + c, [(k, (x ^ ((k >> 2) & 1), y ^ ((k >> 1) & 1), c ^ (k & 1))) for k in range(1, 8)]


def _small_scatter_copies(refs):
    me, peers = _device_peers()
    return [(refs[0].at[me ^ k], refs[1].at[me], to) for k, to in peers]


def _small_spread_copies(refs):
    me, peers = _device_peers()
    return [(refs[0].at[me], refs[0].at[me], to) for _, to in peers]


def _sibling_copies(refs):
    n = len(refs) // 2
    x, y, c, _ = _place()
    return [(refs[w], refs[n + w], (x, y, 1 - c)) for w in range(n)]


def _owner_copies(refs):
    n = len(refs) // 2
    x, y, c, chips = _place()
    return [(refs[w].at[2 * qx + qy], refs[n + w].at[j], (qx, qy, c))
            for w in range(n) for j, (qx, qy) in enumerate(chips)]


N_DEVICES = 8
SMALL_ROWS = 616


SMALL = ("norm_mix_g", "conv_w", "conv_b", "lru_w_a", "lru_b_a", "lru_w_x", "lru_b_x", "lru_lambda",
         "sgu_ln_g", "sgu_ln_b", "sgu_w_s", "sgu_b_s", "norm_ffn_g", "final_norm_g")
WEIGHTS = ("norm_mix_g", "w_in", "conv_w", "conv_b", "lru_w_a", "lru_b_a", "lru_w_x", "lru_b_x", "lru_lambda",
           "sgu_ln_g", "sgu_ln_b", "sgu_w_s", "sgu_b_s", "w_branch_a", "w_branch_b", "w_out", "norm_ffn_g",
           "w_up", "w_down", "final_norm_g")
PACK_ALIGN = SUBLANES * 128


PACKED = SMALL + ("loss",)


def _pack_small(gs):
    parts = []
    for k in PACKED:
        flat = gs[k].reshape(-1)
        parts.append(jnp.pad(flat, (0, -flat.size % PACK_ALIGN)))
    flat = jnp.concatenate(parts)
    flat = jnp.pad(flat, (0, N_DEVICES * SMALL_ROWS * 128 - flat.size))
    return flat.reshape(N_DEVICES, SMALL_ROWS, 128)


def _unpack_small(buf, like):
    flat = buf.reshape(-1)
    out, off = {}, 0
    for k in PACKED:
        size = like[k].size
        out[k] = flat[off:off + size].reshape(like[k].shape)
        off += size + (-size % PACK_ALIGN)
    return out


def _as_rows(a):
    return a.reshape(-1, a.shape[-1])


def kernel(x, norm_mix_g, w_in, conv_w, conv_b, lru_w_a, lru_b_a, lru_w_x, lru_b_x, lru_lambda, sgu_ln_g, sgu_ln_b, sgu_w_s, sgu_b_s, w_branch_a, w_branch_b, w_out, norm_ffn_g, w_up, w_down, final_norm_g, loss_target, m_norm_mix_g, m_w_in, m_conv_w, m_conv_b, m_lru_w_a, m_lru_b_a, m_lru_w_x, m_lru_b_x, m_lru_lambda, m_sgu_ln_g, m_sgu_ln_b, m_sgu_w_s, m_sgu_b_s, m_w_branch_a, m_w_branch_b, m_w_out, m_norm_ffn_g, m_w_up, m_w_down, m_final_norm_g, v_norm_mix_g, v_w_in, v_conv_w, v_conv_b, v_lru_w_a, v_lru_b_a, v_lru_w_x, v_lru_b_x, v_lru_lambda, v_sgu_ln_g, v_sgu_ln_b, v_sgu_w_s, v_sgu_b_s, v_w_branch_a, v_w_branch_b, v_w_out, v_norm_ffn_g, v_w_up, v_w_down, v_final_norm_g):
    w = dict(norm_mix_g=norm_mix_g, w_in=w_in, conv_w=conv_w, conv_b=conv_b, lru_w_a=lru_w_a, lru_b_a=lru_b_a,
             lru_w_x=lru_w_x, lru_b_x=lru_b_x, lru_lambda=lru_lambda, sgu_ln_g=sgu_ln_g, sgu_ln_b=sgu_ln_b,
             sgu_w_s=sgu_w_s, sgu_b_s=sgu_b_s, w_branch_a=w_branch_a, w_branch_b=w_branch_b, w_out=w_out,
             norm_ffn_g=norm_ffn_g, w_up=w_up, w_down=w_down, final_norm_g=final_norm_g)
    m = dict(norm_mix_g=m_norm_mix_g, w_in=m_w_in, conv_w=m_conv_w, conv_b=m_conv_b, lru_w_a=m_lru_w_a,
             lru_b_a=m_lru_b_a, lru_w_x=m_lru_w_x, lru_b_x=m_lru_b_x, lru_lambda=m_lru_lambda,
             sgu_ln_g=m_sgu_ln_g, sgu_ln_b=m_sgu_ln_b, sgu_w_s=m_sgu_w_s, sgu_b_s=m_sgu_b_s,
             w_branch_a=m_w_branch_a, w_branch_b=m_w_branch_b, w_out=m_w_out, norm_ffn_g=m_norm_ffn_g,
             w_up=m_w_up, w_down=m_w_down, final_norm_g=m_final_norm_g)
    v = dict(norm_mix_g=v_norm_mix_g, w_in=v_w_in, conv_w=v_conv_w, conv_b=v_conv_b, lru_w_a=v_lru_w_a,
             lru_b_a=v_lru_b_a, lru_w_x=v_lru_w_x, lru_b_x=v_lru_b_x, lru_lambda=v_lru_lambda,
             sgu_ln_g=v_sgu_ln_g, sgu_ln_b=v_sgu_ln_b, sgu_w_s=v_sgu_w_s, sgu_b_s=v_sgu_b_s,
             w_branch_a=v_w_branch_a, w_branch_b=v_w_branch_b, w_out=v_w_out, norm_ffn_g=v_norm_ffn_g,
             w_up=v_w_up, w_down=v_w_down, final_norm_g=v_final_norm_g)
    core = lax.axis_index("c")
    chip = 2 * lax.axis_index("x") + lax.axis_index("y")
    sel = jnp.stack([core, 1 - core, chip, 2 * chip + core]).astype(jnp.int32)
    this_core, other_core, this_chip = ("sel", 0), ("sel", 1), ("sel", 2)
    sds = jax.ShapeDtypeStruct

    ts = TOKEN_TILE

    def after_all(arrays):
        return jnp.stack([a[(0,) * a.ndim].astype(F32) for a in arrays])

    halves ={k: (w[k].shape[1] // 2, w[k].shape[2]) for k in BIG}

    def half_view(k, a):
        return a.reshape((2 * N_QUARTERS,) + halves[k])

    def full_view(k, a):
        r2, cols = halves[k]
        if k in ("w_in", "w_up"):
            return a.reshape(1, N_QUARTERS, 2 * r2, cols)
        return a.reshape(1, 2 * N_QUARTERS * r2, cols)

    layer_bufs = [{}, {}]

    def cast_weights(k, after):
        _, r, cols = w[k].shape
        w4 = w[k].reshape(DEPTH, 1, r, cols)
        outs = _ew_call(lambda a, b: (a, b), "cast_weights", [(w4, (0, 0)), (w4, (1, 0))],
                        [(sds((1, N_QUARTERS, r, cols), BF), (0, this_chip))] * DEPTH, 1, sel, after=after)
        for l in range(DEPTH):
            layer_bufs[l][k] = half_view(k, outs[l])

    conv_buf = lax.dynamic_update_slice_in_dim(
        jnp.zeros((DEPTH, N_QUARTERS) + conv_w.shape[1:], F32), conv_w[:, None], chip, axis=1)
    sm = {k: w[k] for k in SMALL}
    sm["conv_w"] = _gather_call([conv_buf])[0].transpose(0, 2, 1, 3).reshape(DEPTH, CONV_WIDTH, D_RNN)

    def gather_start(tag, l, keys, after):
        bufs = [layer_bufs[l][k] for k in keys]
        return _exchange_start(f"gather_start_{tag}", bufs, _gather_copies, 3 * len(keys), after)

    def gather_finish(tag, keys, started, after):
        send_sems, recv_sems, thru, _ = started
        landed = _exchange_wait(f"gather_wait_{tag}", send_sems, recv_sems, thru, _gather_copies, after)
        landed = _sibling_inplace_call("gather_forward", landed, _gather_forward_slabs, 3 * len(keys))
        return {k: full_view(k, a) for k, a in zip(keys, landed)}

    first, rest = ("w_in",), tuple(k for k in BIG if k != "w_in")
    cast_weights("w_in", None)
    started_a = gather_start("0a", 0, first, sm["conv_w"])
    for k in rest:
        cast_weights(k, started_a[3])
    started_b = gather_start("0b", 0, rest, started_a[3])
    started_c = gather_start("1a", 1, first, started_b[3])
    started_d = gather_start("1b", 1, rest, started_c[3])

    def arrives(tag, keys, started):
        state = {}

        def hook(after):
            landed = _exchange_wait(f"gather_wait_{tag}", started[0], started[1], started[2], _gather_copies, after)
            state["forward"] = _exchange_start(f"forward_start_{tag}", landed, _forward_copies, 3 * len(keys), after)
            return state["forward"][3][0, 0]

        def finish(after):
            send_sems, recv_sems, thru, _ = state["forward"]
            done = _exchange_wait(f"forward_wait_{tag}", send_sems, recv_sems, thru, _forward_copies, after)
            return {k: full_view(k, a) for k, a in zip(keys, done)}

        return hook, finish

    p0, p1 = _layer_small(sm, 0, sel[0:1]), _layer_small(sm, 1, sel[0:1])
    h0 = _norm_call(x[0], p0["g1"], ts)
    proj_own = _inproj_part_call(h0, full_view("w_in", started_a[2][0]), 2 * ts, sel[2:3], 0, 1)
    ready = after_all([started_d[3], proj_own] + [p[k] for p in (p0, p1) for k in ("wa", "wx", "wm")])
    big0 = gather_finish("0a", first, started_a, ready)
    proj0 = _inproj_part_call(h0, big0["w_in"], 2 * ts, sel[2:3], 1, N_QUARTERS - 1, proj_own)
    hook, finish = arrives("0b", rest, started_b)
    sv0 = _layer_fwd_mix(x[0], big0, p0, ts, h0, hook, proj0)
    big0.update(finish(sv0["yb_pre"]))
    x_mid = _layer_fwd_out(sv0, big0, ts)
    hook, finish = arrives("1a", first, started_c)
    h1 = _norm_call(x_mid, p1["g1"] + hook(x_mid), ts)
    big1 = finish(h1)
    hook, finish = arrives("1b", rest, started_d)
    sv1 = _layer_fwd_mix(x_mid, big1, p1, ts, h1, hook)
    big1.update(finish(sv1["yb_pre"]))
    x_out = _layer_fwd_out(sv1, big1, ts)
    dx, loss, dgf = _loss_call(x_out, loss_target[0], final_norm_g.reshape(1, -1), ts)

    def pair_start(tag, gb, after):
        sends = [gb[k][1] for k in gb]
        zones = [lax.empty(a.shape, BF) for a in sends]
        return _exchange_start(f"pair_start_{tag}", sends + zones, _sibling_copies, len(sends), after)

    def reduce_start(tag, gb, after, pair=None):
        keys = tuple(gb)
        if pair is None:
            from_sibling = _sibling_send_call([gb[k][1] for k in keys])
        else:
            done = _exchange_wait(f"pair_wait_{tag}", pair[0], pair[1], pair[2], _sibling_copies, after)
            from_sibling = done[len(keys):]
        sums = [
            _ew_call(lambda a, b: (a + b.astype(F32),), "pair_sum", [(gb[k][0][None], (0, "g")), (r[None], (0, "g"))],
                     [(sds((1,) + r.shape, BF), (0, "g"))], N_QUARTERS)[0][0]
            for k, r in zip(keys, from_sibling)]
        zones = [lax.empty((3,) + a.shape[1:], BF) for a in sums]
        started = _exchange_start(f"reduce_start_{tag}", sums + zones, _owner_copies, 3 * len(keys), after)
        return keys, started

    def reduce_finish(tag, l, keys_started, after, reduced):
        keys, (send_sems, recv_sems, thru, _) = keys_started
        done = _exchange_wait(f"reduce_wait_{tag}", send_sems, recv_sems, thru, _owner_copies, after)
        sums, zones = done[:len(keys)], done[len(keys):]
        for i, k in enumerate(keys):
            r2, cols = halves[k]
            reduced[k] = _ew_call(
                lambda a, b, c, d: (((a.astype(F32) + b.astype(F32)) + c.astype(F32)) + d.astype(F32),),
                "quarter_sum", [(sums[i][None], (0, this_chip))] + [(zones[i][None], (0, j)) for j in range(3)],
                [(sds((DEPTH, 2, r2, cols), F32), (l, this_core))], 1, sel, into=reduced.get(k))[0]

    def behind(params, key, started):
        return dict(params, **{key: params[key] + started[1][3][0, 0]})

    dx1, gb_ffn, gs1 = _layer_bwd_ffn(dx, sv1, big1, ts)
    merge_out, gb_merge = _layer_bwd_merge(dx1, sv1, big1, ts)
    dx_mid, gb_in, gs1_mix = _layer_bwd_branches(dx1, merge_out, sv1, big1, lru_lambda[1], ts)
    gb_1 = {**gb_ffn, **gb_merge, **gb_in}
    pair_1 = pair_start("1", gb_1, dx_mid)
    sv0["p"] = behind(sv0["p"], "g2", (None, pair_1))
    dx1, gb_ffn, gs0 = _layer_bwd_ffn(dx_mid, sv0, big0, ts)
    exchange_1 = reduce_start("1", gb_1, dx1, pair_1)
    pair_0a = pair_start("0a", gb_ffn, exchange_1[1][3])
    merge_out, gb_merge = _layer_bwd_merge(dx1, sv0, big0, ts, pair_0a[3])
    exchange_0a = reduce_start("0a", gb_ffn, merge_out[0], pair_0a)
    pair_0b = pair_start("0b", gb_merge, exchange_0a[1][3])
    sv0["p"] = behind(sv0["p"], "lg", (None, pair_0b))
    started_0b = {}

    def after_sgu(duv):
        started_0b["exchange"] = reduce_start("0b", gb_merge, duv, pair_0b)
        return started_0b["exchange"][1][3][0, 0]

    grad_x, gb_in, gs0_mix = _layer_bwd_branches(dx1, merge_out, sv0, big0, lru_lambda[0], ts, after_sgu)
    exchange_0b = started_0b["exchange"]
    exchange_0c = reduce_start("0c", gb_in, exchange_0b[1][3])
    layer_gs = [{**gs0, **gs0_mix}, {**gs1, **gs1_mix}]
    gs = {k: jnp.stack([g[k] for g in layer_gs]) for k in layer_gs[0]}
    gs["final_norm_g"] = dgf[0]
    gs["loss"] = loss[0, 0:1]

    me = ("sel", 3)
    piece = (1, N_DEVICES, SMALL_ROWS, 128)
    packed = _pack_small(gs).reshape(piece)
    scatter = _exchange_start("small_scatter_start", [packed[0], lax.empty(piece[1:], F32)], _small_scatter_copies,
                              N_DEVICES - 1, exchange_0c[1][3])
    reduced = {}
    reduce_finish("1", 1, exchange_1, scatter[3], reduced)
    reduce_finish("0a", 0, exchange_0a, reduced["w_in"], reduced)
    reduce_finish("0b", 0, exchange_0b, reduced["w_down"], reduced)

    def swap_slabs(ref, c, i):
        layers = (1,) if BIG[i] == "w_in" else range(DEPTH)
        return [(ref.at[l, c], ref.at[l, 1 - c]) for l in layers]

    swapped = dict(zip(BIG, _sibling_inplace_call("grads_swap_halves", [reduced[k] for k in BIG], swap_slabs,
                                                  DEPTH * len(BIG) - 1)))

    def adamw_layers(k, grad, layer, into, after=None):
        if layer is None:
            views = [_as4(_as_rows(a)) for a in (w[k], grad, m[k], v[k])]
            idx = (0, 0)
        else:
            views = [a.reshape((1,) + w[k].shape) for a in (w[k], grad, m[k], v[k])]
            idx = (0, layer)
        return _ew_call(_adamw, "adamw_big", [(a, idx) for a in views], [(sds(views[0].shape, F32), idx)] * 3,
                        into=into, after=after)

    updated, last_update = {}, None
    for k in BIG:
        updated[k] = adamw_layers(k, swapped[k], 1 if k == "w_in" else None, None, last_update)
        last_update = updated[k][0]
    scattered = _exchange_wait("small_scatter_wait", scatter[0], scatter[1], scatter[2], _small_scatter_copies,
                               last_update)
    summed = _ew_call(
        lambda *parts: (functools.reduce(lambda a, b: a + b, parts),), "small_sum",
        [(scattered[0][None], (0, me))]
        + [(scattered[1][None], (0, lambda g, s, k=k: s[3] ^ k)) for k in range(1, N_DEVICES)],
        [(sds(piece, F32), (0, me))], 1, sel)[0]
    spread = _exchange_start("small_spread_start", [summed[0]], _small_spread_copies, N_DEVICES - 1, summed)
    reduced["w_in"] = swapped["w_in"]
    reduce_finish("0c", 0, exchange_0c, spread[3], reduced)
    last = _sibling_inplace_call("grads_swap_last", [reduced["w_in"]],
                                 lambda ref, c, i: [(ref.at[0, c], ref.at[0, 1 - c])], 1)[0]
    swapped["w_in"] = last
    updated["w_in"] = adamw_layers("w_in", last, 0, updated["w_in"])
    grads_big = {k: swapped[k].reshape(w[k].shape) for k in BIG}
    delta, new_m, new_v = ({k: updated[k][j].reshape(w[k].shape) for k in BIG} for j in range(3))
    gathered_small = _exchange_wait("small_spread_wait", spread[0], spread[1], spread[2], _small_spread_copies,
                                    updated["w_in"][0])[0]

    like = {k: jax.ShapeDtypeStruct(sm[k].shape, F32) for k in SMALL}
    like["loss"] = jax.ShapeDtypeStruct((1,), F32)
    grads_small = _unpack_small(gathered_small, like)
    total = grads_small.pop("loss")[0]
    conv_q = grads_small["conv_w"].reshape(DEPTH, CONV_WIDTH, N_QUARTERS, D_RNN // N_QUARTERS)
    grads_small["conv_w"] = lax.dynamic_index_in_dim(conv_q, chip, axis=2, keepdims=False)
    outs = _small_adamw_call(*[[_as_rows(d[k]) for k in SMALL] for d in (w, grads_small, m, v)])
    for d, o in zip((delta, new_m, new_v), outs):
        for k, a in zip(SMALL, o):
            d[k] = a.reshape(w[k].shape)

    grads = {**grads_big, **grads_small}
    return (total, grad_x[None], *[grads[k] for k in WEIGHTS], *[delta[k] for k in WEIGHTS],
            *[new_m[k] for k in WEIGHTS], *[new_v[k] for k in WEIGHTS])
```

```python
import functools
import math

import jax
import jax.numpy as jnp
from jax import lax
from jax.experimental import pallas as pl
from jax.experimental.pallas import tpu as pltpu

F32 = jnp.float32
BF = jnp.bfloat16

DEPTH = 2
D_MODEL = 1024
D_RNN = 1280
D_SGU = 1024
D_FF = 4096
D_IN = 2 * D_RNN + 2 * D_SGU + 2 * D_MODEL
N_QUARTERS = 4
Q_IN = D_IN // N_QUARTERS
Q_FF = D_FF // N_QUARTERS
RNN_HEADS = 20
RNN_HEAD_DIM = 64
LRU_GROUP = 256
N_LRU_GROUPS = D_RNN // LRU_GROUP
HEADS_PER_GROUP = LRU_GROUP // RNN_HEAD_DIM
CONV_WIDTH = 4
LRU_C = 8.0
SGU_GROUPS = 8
SGU_BLOCK = 128
CHUNK = 64
EPS = 1e-6

ADAM_LR = 0.001
ADAM_B1 = 0.9
ADAM_B2 = 0.999
ADAM_EPS = 1e-08
ADAM_WD = 0.01
ADAM_STEP = 10

SUBLANES = 8
TOKEN_TILE = 512
VMEM_LIMIT_BYTES = 56 * 1024 * 1024

MESH = pl.DeviceIdType.MESH


def _params(semantics=None, vmem=True, **kw):
    return pltpu.CompilerParams(
        dimension_semantics=semantics,
        vmem_limit_bytes=VMEM_LIMIT_BYTES if vmem else None,
        **kw,
    )


def _dot(a, b):
    return jnp.dot(a, b, preferred_element_type=F32)


def _dot_nt(a, b):
    return lax.dot_general(a, b, (((1,), (1,)), ((), ())), preferred_element_type=F32)


def _dot_tn(a, b):
    return lax.dot_general(a, b, (((0,), (0,)), ((), ())), preferred_element_type=F32)


_GELU_C = math.sqrt(2.0 / math.pi)
_GELU_A = 0.044715


def _gelu(x):
    return 0.5 * x * (1.0 + jnp.tanh(_GELU_C * (x + _GELU_A * x * x * x)))


def _gelu_and_grad(x):
    x2 = x * x
    t = jnp.tanh(_GELU_C * (x + _GELU_A * x2 * x))
    du = _GELU_C * (1.0 + 3.0 * _GELU_A * x2)
    return 0.5 * x * (1.0 + t), 0.5 * (1.0 + t) + 0.5 * x * (1.0 - t * t) * du


def _rms_stats(x):
    return lax.rsqrt(jnp.mean(x * x, axis=-1, keepdims=True) + EPS)


def _rms_bwd(dy, x, g):
    rs = _rms_stats(x)
    n = x * rs
    dn = dy * g
    dx = rs * (dn - n * jnp.mean(dn * n, axis=-1, keepdims=True))
    return dx, dy * n


def _row_sum(x):
    return jnp.sum(x, axis=0, keepdims=True)


def _tile_spec(ts, width, col=0):
    return pl.BlockSpec((ts, width), lambda i, col=col: (i, col))


def _full_spec(shape):
    zeros = (0,) * len(shape)
    return pl.BlockSpec(shape, lambda *_: zeros)


def _layer_spec(w, layer):
    zeros = (0,) * (w.ndim - 1)
    return pl.BlockSpec((None,) + tuple(w.shape[1:]), lambda *_: (layer,) + zeros)


def _norm_call(x, g, ts):
    s = x.shape[0]

    def body(x_ref, g_ref, h_ref):
        xv = x_ref[...]
        h_ref[...] = (xv * _rms_stats(xv) * g_ref[...]).astype(BF)

    return pl.pallas_call(
        body, name="norm_fwd", grid=(s // ts,),
        in_specs=[_tile_spec(ts, D_MODEL), _full_spec((1, D_MODEL))],
        out_specs=_tile_spec(ts, D_MODEL),
        out_shape=jax.ShapeDtypeStruct((s, D_MODEL), BF),
        compiler_params=_params(("parallel",)),
    )(x, g)


def _inproj_call(h, w_in, layer, ts):
    s = h.shape[0]

    def body(h_ref, w_ref, o_ref):
        o_ref[...] = _dot(h_ref[...], w_ref[...]).astype(BF)

    return pl.pallas_call(
        body, name="inproj_fwd", grid=(N_QUARTERS, s // ts),
        in_specs=[
            pl.BlockSpec((ts, D_MODEL), lambda q, i: (i, 0)),
            pl.BlockSpec((None, None, D_MODEL, Q_IN), lambda q, i: (layer, q, 0, 0)),
        ],
        out_specs=pl.BlockSpec((ts, Q_IN), lambda q, i: (i, q)),
        out_shape=jax.ShapeDtypeStruct((s, D_IN), BF),
        compiler_params=_params(("parallel", "parallel")),
    )(h, w_in)


def _inproj_part_call(h, w_in, ts, own, first, count, into=None):
    s = h.shape[0]

    def quarter(j, sel):
        return (sel[0] + first + j) % N_QUARTERS

    def body(sel_ref, h_ref, w_ref, *rest):
        rest[-1][...] = _dot(h_ref[...], w_ref[...]).astype(BF)

    in_specs = [pl.BlockSpec((ts, D_MODEL), lambda j, i, sel: (i, 0)),
                pl.BlockSpec((None, None, D_MODEL, Q_IN), lambda j, i, sel: (0, quarter(j, sel), 0, 0))]
    operands = [h, w_in]
    aliases = {}
    if into is not None:
        in_specs.append(pl.BlockSpec(memory_space=pl.ANY))
        operands.append(into)
        aliases = {3: 0}
    return pl.pallas_call(
        body, name="inproj_fwd_part", out_shape=jax.ShapeDtypeStruct((s, D_IN), BF),
        grid_spec=pltpu.PrefetchScalarGridSpec(
            num_scalar_prefetch=1, grid=(count, s // ts), in_specs=in_specs,
            out_specs=pl.BlockSpec((ts, Q_IN), lambda j, i, sel: (i, quarter(j, sel)))),
        input_output_aliases=aliases,
        compiler_params=_params(("parallel", "parallel")),
    )(own, *operands)


def _shift_down(x, tail, s):
    xr = pltpu.roll(x, s, 0)
    tr = pltpu.roll(tail, s, 0)
    row = lax.broadcasted_iota(jnp.int32, tail.shape, 0)
    top = jnp.where(row < s, tr, xr[0:SUBLANES])
    return jnp.concatenate([top, xr[SUBLANES:]], axis=0)


def _shift_up(x, head, s):
    t = x.shape[0]
    xr = pltpu.roll(x, t - s, 0)
    hr = pltpu.roll(head, SUBLANES - s, 0)
    row = lax.broadcasted_iota(jnp.int32, head.shape, 0)
    bottom = jnp.where(row >= SUBLANES - s, hr, xr[t - SUBLANES:])
    return jnp.concatenate([xr[: t - SUBLANES], bottom], axis=0)


def _conv_fwd(x, tail, cw_ref, cb_ref):
    out = cb_ref[...] + cw_ref[CONV_WIDTH - 1:CONV_WIDTH, :] * x
    for s in range(1, CONV_WIDTH):
        k = CONV_WIDTH - 1 - s
        out = out + cw_ref[k:k + 1, :] * _shift_down(x, tail, s)
    return out


def _group_dot(x_bf, w_ref, dot):
    cols = [dot(x_bf[:, g * LRU_GROUP:(g + 1) * LRU_GROUP], w_ref[g]) for g in range(N_LRU_GROUPS)]
    return jnp.concatenate(cols, axis=1)


def _lru_gates(xr, wa_ref, wx_ref, ba_ref, bx_ref, sp_ref):
    xb = xr.astype(BF)
    r = jax.nn.sigmoid(_group_dot(xb, wa_ref, _dot) + ba_ref[...])
    i = jax.nn.sigmoid(_group_dot(xb, wx_ref, _dot) + bx_ref[...])
    log_a = (-LRU_C * r) * sp_ref[...]
    a = jnp.exp(log_a)
    nrm2 = -jnp.tanh(log_a) * (a * a + 1.0)
    inv_nrm = lax.rsqrt(jnp.maximum(nrm2, 1e-36))
    return r, i, a, nrm2 * inv_nrm, inv_nrm


def _linear_scan(a, b, carry, al_ref, bl_ref, h_ref, reverse):
    t, c = a.shape
    rowm = lax.broadcasted_iota(jnp.int32, (t, c), 0) & (SUBLANES - 1)
    for d in (1, 2, 4):
        if reverse:
            keep, sh = rowm < SUBLANES - d, t - d
        else:
            keep, sh = rowm >= d, d
        a_sh = jnp.where(keep, pltpu.roll(a, sh, 0), 1.0)
        b_sh = jnp.where(keep, pltpu.roll(b, sh, 0), 0.0)
        b = a * b_sh + b
        a = a * a_sh
    al_ref[...] = a
    bl_ref[...] = b
    groups = t // SUBLANES

    def step(j, state):
        jj = groups - 1 - j if reverse else j
        off = pl.multiple_of(jj * SUBLANES, SUBLANES)
        rows = bl_ref[pl.ds(off, SUBLANES), :] + al_ref[pl.ds(off, SUBLANES), :] * state
        h_ref[pl.ds(off, SUBLANES), :] = rows
        last = rows[0:1, :] if reverse else rows[SUBLANES - 1:SUBLANES, :]
        return jnp.broadcast_to(last, (SUBLANES, c))

    out = lax.fori_loop(0, groups, step, jnp.broadcast_to(carry, (SUBLANES, c)))
    return out[0:1, :]


def _rnn_fwd_call(proj, wa, wx, ba, bx, sp, cw, cb, ts):
    s = proj.shape[0]

    def body(xg_ref, wa_ref, wx_ref, ba_ref, bx_ref, sp_ref, cw_ref, cb_ref, xr_ref, hr_ref, ya_ref,
             tail_sc, carry_sc, al_sc, bl_sc, h_sc):
        @pl.when(pl.program_id(0) == 0)
        def _():
            tail_sc[...] = jnp.zeros_like(tail_sc)
            carry_sc[...] = jnp.zeros_like(carry_sc)

        x = xg_ref[:, :D_RNN].astype(F32)
        g = xg_ref[:, D_RNN:]
        xr = _conv_fwd(x, tail_sc[...], cw_ref, cb_ref)
        tail_sc[...] = x[ts - SUBLANES:, :]
        xr_ref[...] = xr.astype(BF)
        _, i, a, nrm, _ = _lru_gates(xr, wa_ref, wx_ref, ba_ref, bx_ref, sp_ref)
        carry_sc[...] = _linear_scan(a, nrm * (i * xr), carry_sc[...], al_sc, bl_sc, h_sc, False)
        h = h_sc[...]
        hr_ref[...] = h.astype(BF)
        ya_ref[...] = (h * _gelu(g)).astype(BF)

    gw = (N_LRU_GROUPS, LRU_GROUP, LRU_GROUP)
    return pl.pallas_call(
        body, name="rnn_fwd", grid=(s // ts,),
        in_specs=[_tile_spec(ts, 2 * D_RNN), _full_spec(gw), _full_spec(gw),
                  _full_spec((1, D_RNN)), _full_spec((1, D_RNN)), _full_spec((1, D_RNN)),
                  _full_spec((CONV_WIDTH, D_RNN)), _full_spec((1, D_RNN))],
        out_specs=[_tile_spec(ts, D_RNN)] * 3,
        out_shape=[jax.ShapeDtypeStruct((s, D_RNN), BF)] * 3,
        scratch_shapes=[pltpu.VMEM((SUBLANES, D_RNN), F32), pltpu.VMEM((1, D_RNN), F32),
                        pltpu.VMEM((ts, D_RNN), F32), pltpu.VMEM((ts, D_RNN), F32),
                        pltpu.VMEM((ts, D_RNN), F32)],
        compiler_params=_params(("arbitrary",)),
    )(proj, wa, wx, ba, bx, sp, cw, cb)


def _layernorm_fwd(x):
    mu = jnp.mean(x, axis=-1, keepdims=True)
    xc = x - mu
    rstd = lax.rsqrt(jnp.mean(xc * xc, axis=-1, keepdims=True) + EPS)
    return xc * rstd, rstd


def _sgu_mix(vn_bf, wm_ref, bsb_ref, ts):
    rows = []
    for blk in range(ts // SGU_BLOCK):
        r0 = blk * SGU_BLOCK
        cols = [
            _dot(wm_ref[g], vn_bf[r0:r0 + SGU_BLOCK, g * SGU_BLOCK:(g + 1) * SGU_BLOCK]) + bsb_ref[g]
            for g in range(SGU_GROUPS)
        ]
        rows.append(jnp.concatenate(cols, axis=1))
    return jnp.concatenate(rows, axis=0)


def _sgu_fwd_call(proj, wm, bsb, lg, lb, ts):
    s = proj.shape[0]

    def body(uv_ref, wm_ref, bsb_ref, lg_ref, lb_ref, yb_ref):
        gu = _gelu(uv_ref[:, :D_SGU])
        gv = _gelu(uv_ref[:, D_SGU:2 * D_SGU]).astype(F32)
        nh, _ = _layernorm_fwd(gv)
        vn = (nh * lg_ref[...] + lb_ref[...]).astype(BF)
        yb_ref[...] = (gu * _sgu_mix(vn, wm_ref, bsb_ref, ts)).astype(BF)

    sw = (SGU_GROUPS, SGU_BLOCK, SGU_BLOCK)
    return pl.pallas_call(
        body, name="sgu_fwd", grid=(s // ts,),
        in_specs=[_tile_spec(ts, 2 * D_RNN, 1), _full_spec(sw), _full_spec(sw),
                  _full_spec((1, D_SGU)), _full_spec((1, D_SGU))],
        out_specs=_tile_spec(ts, D_SGU),
        out_shape=jax.ShapeDtypeStruct((s, D_SGU), BF),
        compiler_params=_params(("parallel",)),
    )(proj, wm, bsb, lg, lb)


_GATE_COL0 = (2 * D_RNN + 2 * D_SGU) // 512


def _gate_specs(ts):
    return [_tile_spec(ts, 512, _GATE_COL0 + j) for j in range(4)]


def _merge_call(x, proj, ya_pre, yb_pre, w_ba, w_bb, w_out, g2, layer, ts):
    s = x.shape[0]

    def body(x_ref, ga0, ga1, gb0, gb1, ya_ref, yb_ref, wa_ref, wb_ref, wo_ref, g2_ref,
             x1_ref, yao_ref, ybo_ref, mg_ref, h2_ref):
        ya = _dot(ya_ref[...], wa_ref[...])
        yb = _dot(yb_ref[...], wb_ref[...])
        sa = jax.nn.sigmoid(jnp.concatenate([ga0[...], ga1[...]], axis=1).astype(F32))
        sb = jax.nn.sigmoid(jnp.concatenate([gb0[...], gb1[...]], axis=1).astype(F32))
        merged = (sa * ya + sb * yb).astype(BF)
        x1 = x_ref[...] + _dot(merged, wo_ref[...])
        x1_ref[...] = x1
        yao_ref[...] = ya.astype(BF)
        ybo_ref[...] = yb.astype(BF)
        mg_ref[...] = merged
        h2_ref[...] = (x1 * _rms_stats(x1) * g2_ref[...]).astype(BF)

    act = jax.ShapeDtypeStruct((s, D_MODEL), BF)
    return pl.pallas_call(
        body, name="merge_fwd", grid=(s // ts,),
        in_specs=[_tile_spec(ts, D_MODEL)] + _gate_specs(ts) + [
            _tile_spec(ts, D_RNN), _tile_spec(ts, D_SGU),
            _layer_spec(w_ba, layer), _layer_spec(w_bb, layer), _layer_spec(w_out, layer),
            _full_spec((1, D_MODEL))],
        out_specs=[_tile_spec(ts, D_MODEL)] * 5,
        out_shape=[jax.ShapeDtypeStruct((s, D_MODEL), F32), act, act, act, act],
        compiler_params=_params(("parallel",)),
    )(x, proj, proj, proj, proj, ya_pre, yb_pre, w_ba, w_bb, w_out, g2)


def _ffn_call(x1, h2, w_up, w_down, layer, ts):
    s = x1.shape[0]

    def body(x1_ref, h2_ref, wu_ref, wd_ref, x2_ref, p_ref):
        h2v = h2_ref[...]
        acc = x1_ref[...]
        for q in range(N_QUARTERS):
            p = _dot(h2v, wu_ref[q])
            p_ref[:, q * Q_FF:(q + 1) * Q_FF] = p.astype(BF)
            f = jnp.square(jnp.maximum(p, 0.0)).astype(BF)
            acc = acc + _dot(f, wd_ref[q * Q_FF:(q + 1) * Q_FF, :])
        x2_ref[...] = acc

    return pl.pallas_call(
        body, name="ffn_fwd", grid=(s // ts,),
        in_specs=[_tile_spec(ts, D_MODEL), _tile_spec(ts, D_MODEL),
                  pl.BlockSpec((None, N_QUARTERS, D_MODEL, Q_FF), lambda i: (layer, 0, 0, 0)),
                  pl.BlockSpec((None, D_FF, D_MODEL), lambda i: (layer, 0, 0))],
        out_specs=[_tile_spec(ts, D_MODEL), _tile_spec(ts, D_FF)],
        out_shape=[jax.ShapeDtypeStruct((s, D_MODEL), F32), jax.ShapeDtypeStruct((s, D_FF), BF)],
        compiler_params=_params(("parallel",)),
    )(x1, h2, w_up, w_down)


def _loss_call(x, target, gf, ts):
    s = x.shape[0]

    def body(x_ref, t_ref, g_ref, dx_ref, loss_ref, dg_ref):
        @pl.when(pl.program_id(0) == 0)
        def _():
            loss_ref[...] = jnp.zeros_like(loss_ref)
            dg_ref[...] = jnp.zeros_like(dg_ref)

        xv = x_ref[...]
        gv = g_ref[...]
        err = xv * _rms_stats(xv) * gv - t_ref[...]
        part = 0.5 * jnp.sum(jnp.mean(err * err, axis=-1, keepdims=True), axis=0, keepdims=True)
        loss_ref[...] += jnp.broadcast_to(part, loss_ref.shape)
        dx, dg = _rms_bwd(err * (1.0 / D_MODEL), xv, gv)
        dx_ref[...] = dx
        dg_ref[...] += _row_sum(dg)

    return pl.pallas_call(
        body, name="loss_head", grid=(s // ts,),
        in_specs=[_tile_spec(ts, D_MODEL), _tile_spec(ts, D_MODEL), _full_spec((1, D_MODEL))],
        out_specs=[_tile_spec(ts, D_MODEL), _full_spec((1, 128)), _full_spec((1, D_MODEL))],
        out_shape=[jax.ShapeDtypeStruct((s, D_MODEL), F32), jax.ShapeDtypeStruct((1, 128), F32),
                   jax.ShapeDtypeStruct((1, D_MODEL), F32)],
        compiler_params=_params(("arbitrary",)),
    )(x, target, gf)


def _ffn_bwd_call(dx2, p, x1, g2, w_up, w_down, layer, ts):
    s = dx2.shape[0]

    def body(dx2_ref, p_ref, x1_ref, g2_ref, wu_ref, wd_ref, dx1_ref, dp_ref, dg_ref):
        @pl.when(pl.program_id(0) == 0)
        def _():
            dg_ref[...] = jnp.zeros_like(dg_ref)

        dx2v = dx2_ref[...]
        dyb = dx2v.astype(BF)
        dh2 = jnp.zeros((ts, D_MODEL), F32)
        for q in range(N_QUARTERS):
            cols = slice(q * Q_FF, (q + 1) * Q_FF)
            df = _dot_nt(dyb, wd_ref[cols, :])
            dp = (df * (2.0 * jnp.maximum(p_ref[:, cols].astype(F32), 0.0))).astype(BF)
            dp_ref[:, cols] = dp
            dh2 = dh2 + _dot_nt(dp, wu_ref[q])
        dx, dg = _rms_bwd(dh2, x1_ref[...], g2_ref[...])
        dx1_ref[...] = dx2v + dx
        dg_ref[...] += _row_sum(dg)

    return pl.pallas_call(
        body, name="ffn_bwd", grid=(s // ts,),
        in_specs=[_tile_spec(ts, D_MODEL), _tile_spec(ts, D_FF), _tile_spec(ts, D_MODEL),
                  _full_spec((1, D_MODEL)),
                  pl.BlockSpec((None, N_QUARTERS, D_MODEL, Q_FF), lambda i: (layer, 0, 0, 0)),
                  pl.BlockSpec((None, D_FF, D_MODEL), lambda i: (layer, 0, 0))],
        out_specs=[_tile_spec(ts, D_MODEL), _tile_spec(ts, D_FF), _full_spec((1, D_MODEL))],
        out_shape=[jax.ShapeDtypeStruct((s, D_MODEL), F32), jax.ShapeDtypeStruct((s, D_FF), BF),
                   jax.ShapeDtypeStruct((1, D_MODEL), F32)],
        compiler_params=_params(("arbitrary",)),
    )(dx2, p, x1, g2, w_up, w_down)


def _merge_bwd_call(dx1, proj, ya, yb, w_ba, w_bb, w_out, layer, ts, after=None):
    s = dx1.shape[0]

    def body(dx1_ref, ga0, ga1, gb0, gb1, ya_ref, yb_ref, wa_ref, wb_ref, wo_ref, *rest):
        dya_ref, dyb_ref, dgate_ref, dyap_ref, dybp_ref = rest[-5:]
        dm = _dot_nt(dx1_ref[...].astype(BF), wo_ref[...])
        sa = jax.nn.sigmoid(jnp.concatenate([ga0[...], ga1[...]], axis=1).astype(F32))
        sb = jax.nn.sigmoid(jnp.concatenate([gb0[...], gb1[...]], axis=1).astype(F32))
        dya = (dm * sa).astype(BF)
        dyb = (dm * sb).astype(BF)
        dya_ref[...] = dya
        dyb_ref[...] = dyb
        dgate_ref[:, :D_MODEL] = (dm * ya_ref[...].astype(F32) * sa * (1.0 - sa)).astype(BF)
        dgate_ref[:, D_MODEL:] = (dm * yb_ref[...].astype(F32) * sb * (1.0 - sb)).astype(BF)
        dyap_ref[...] = _dot_nt(dya, wa_ref[...]).astype(BF)
        dybp_ref[...] = _dot_nt(dyb, wb_ref[...]).astype(BF)

    act = jax.ShapeDtypeStruct((s, D_MODEL), BF)
    return pl.pallas_call(
        body, name="merge_bwd", grid=(s // ts,),
        in_specs=[_tile_spec(ts, D_MODEL)] + _gate_specs(ts) + [
            _tile_spec(ts, D_MODEL), _tile_spec(ts, D_MODEL),
            _layer_spec(w_ba, layer), _layer_spec(w_bb, layer), _layer_spec(w_out, layer)]
        + ([] if after is None else [pl.BlockSpec(memory_space=pl.ANY)]),
        out_specs=[_tile_spec(ts, D_MODEL), _tile_spec(ts, D_MODEL), _tile_spec(ts, 2 * D_MODEL),
                   _tile_spec(ts, D_RNN), _tile_spec(ts, D_SGU)],
        out_shape=[act, act, jax.ShapeDtypeStruct((s, 2 * D_MODEL), BF),
                   jax.ShapeDtypeStruct((s, D_RNN), BF), jax.ShapeDtypeStruct((s, D_SGU), BF)],
        compiler_params=_params(("parallel",)),
    )(dx1, proj, proj, proj, proj, ya, yb, w_ba, w_bb, w_out, *([] if after is None else [after]))


def _sgu_bwd_call(dyb_pre, proj, wm, bsb, mask, lg, lb, ts):
    s = proj.shape[0]

    def body(dy_ref, uv_ref, wm_ref, bsb_ref, mask_ref, lg_ref, lb_ref,
             duv_ref, dws_ref, dbs_ref, dlg_ref, dlb_ref, dm_sc):
        step = pl.program_id(0)

        @pl.when(step == 0)
        def _():
            dws_ref[...] = jnp.zeros_like(dws_ref)
            dlg_ref[...] = jnp.zeros_like(dlg_ref)
            dlb_ref[...] = jnp.zeros_like(dlb_ref)
            dm_sc[...] = jnp.zeros_like(dm_sc)

        gu, dgu_du = _gelu_and_grad(uv_ref[:, :D_SGU])
        gv, dgv_dv = _gelu_and_grad(uv_ref[:, D_SGU:2 * D_SGU])
        nh, rstd = _layernorm_fwd(gv.astype(F32))
        lgv = lg_ref[...]
        vn = (nh * lgv + lb_ref[...]).astype(BF)
        dy = dy_ref[...].astype(F32)
        du = dy * _sgu_mix(vn, wm_ref, bsb_ref, ts) * dgu_du
        dmix = dy * gu
        dmix_bf = dmix.astype(BF)
        dm_acc = dm_sc[...]
        rows = []
        for blk in range(ts // SGU_BLOCK):
            r0 = blk * SGU_BLOCK
            dm_acc = dm_acc + dmix[r0:r0 + SGU_BLOCK, :]
            cols = []
            for g in range(SGU_GROUPS):
                c0 = g * SGU_BLOCK
                dmg = dmix_bf[r0:r0 + SGU_BLOCK, c0:c0 + SGU_BLOCK]
                cols.append(_dot_tn(wm_ref[g], dmg))
                dws_ref[g] += mask_ref[...] * _dot_nt(dmg, vn[r0:r0 + SGU_BLOCK, c0:c0 + SGU_BLOCK])
            rows.append(jnp.concatenate(cols, axis=1))
        dm_sc[...] = dm_acc
        dvn = jnp.concatenate(rows, axis=0)
        dlg_ref[...] += _row_sum(dvn * nh)
        dlb_ref[...] += _row_sum(dvn)
        dnh = dvn * lgv
        dgv = rstd * (dnh - jnp.mean(dnh, axis=-1, keepdims=True)
                      - nh * jnp.mean(dnh * nh, axis=-1, keepdims=True))
        duv_ref[:, :D_SGU] = du.astype(BF)
        duv_ref[:, D_SGU:] = (dgv * dgv_dv).astype(BF)

        @pl.when(step == pl.num_programs(0) - 1)
        def _():
            for g in range(SGU_GROUPS):
                dbs_ref[:, g:g + 1] = jnp.sum(
                    dm_acc[:, g * SGU_BLOCK:(g + 1) * SGU_BLOCK], axis=1, keepdims=True)

    sw = (SGU_GROUPS, SGU_BLOCK, SGU_BLOCK)
    return pl.pallas_call(
        body, name="sgu_bwd", grid=(s // ts,),
        in_specs=[_tile_spec(ts, D_SGU), _tile_spec(ts, 2 * D_RNN, 1), _full_spec(sw), _full_spec(sw),
                  _full_spec((SGU_BLOCK, SGU_BLOCK)), _full_spec((1, D_SGU)), _full_spec((1, D_SGU))],
        out_specs=[_tile_spec(ts, 2 * D_SGU), _full_spec(sw), _full_spec((SGU_BLOCK, SGU_GROUPS)),
                   _full_spec((1, D_SGU)), _full_spec((1, D_SGU))],
        out_shape=[jax.ShapeDtypeStruct((s, 2 * D_SGU), BF), jax.ShapeDtypeStruct(sw, F32),
                   jax.ShapeDtypeStruct((SGU_BLOCK, SGU_GROUPS), F32),
                   jax.ShapeDtypeStruct((1, D_SGU), F32), jax.ShapeDtypeStruct((1, D_SGU), F32)],
        scratch_shapes=[pltpu.VMEM((SGU_BLOCK, D_SGU), F32)],
        compiler_params=_params(("arbitrary",)),
    )(dyb_pre, proj, wm, bsb, mask, lg, lb)


_ROW_DBA, _ROW_DBX, _ROW_DSP, _ROW_DCB, _ROW_DCW = 0, 1, 2, 3, 4
_PREV_ROWS = 16


def _rnn_bwd_call(dya_pre, proj, xr_saved, hr, wa, wx, ba, bx, sp, cw, ts):
    s = proj.shape[0]
    nt = s // ts
    per = ts // _PREV_ROWS

    def tile(i):
        return nt - 1 - i

    def prev(i):
        return jnp.maximum(tile(i) * per - 1, 0)

    def body(dy_ref, xg_ref, xr_ref, hr_ref, hrp_ref, wa_ref, wx_ref, ba_ref, bx_ref, sp_ref,
             cw_ref, dxg_ref, dwa_ref, dwx_ref, vec_ref,
             lam_carry, a_first, dxr_head, al_sc, bl_sc, lam_sc):
        step = pl.program_id(0)

        @pl.when(step == 0)
        def _():
            dwa_ref[...] = jnp.zeros_like(dwa_ref)
            dwx_ref[...] = jnp.zeros_like(dwx_ref)
            vec_ref[...] = jnp.zeros_like(vec_ref)
            lam_carry[...] = jnp.zeros_like(lam_carry)
            a_first[...] = jnp.zeros_like(a_first)
            dxr_head[...] = jnp.zeros_like(dxr_head)

        has_prev = (step < nt - 1).astype(F32)
        x = xg_ref[:, :D_RNN].astype(F32)
        g = xg_ref[:, D_RNN:]
        h_tail =hrp_ref[_PREV_ROWS - SUBLANES:, :].astype(F32) * has_prev
        xr = xr_ref[...].astype(F32)
        r, i, a, nrm, inv_nrm = _lru_gates(xr, wa_ref, wx_ref, ba_ref, bx_ref, sp_ref)
        h = hr_ref[...].astype(F32)
        dy = dy_ref[...].astype(F32)
        gg, dgg = _gelu_and_grad(g)

        coef = _shift_up(a, jnp.broadcast_to(a_first[...], (SUBLANES, D_RNN)), 1)
        lam_carry[...] = _linear_scan(coef, dy * gg, lam_carry[...], al_sc, bl_sc, lam_sc, True)
        a_first[...] = a[0:1, :]
        lam = lam_sc[...]

        da = lam * _shift_down(h, h_tail, 1)
        dnrm = lam * (i * xr)
        di = lam * nrm * xr
        dlog_a = da * a - dnrm * (a * a) * inv_nrm
        spv = sp_ref[...]
        dza = (dlog_a * (-LRU_C * spv)) * (r * (1.0 - r))
        dzx = di * (i * (1.0 - i))
        vec_ref[_ROW_DSP:_ROW_DSP + 1, :] += _row_sum(dlog_a * (-LRU_C * r))
        vec_ref[_ROW_DBA:_ROW_DBA + 1, :] += _row_sum(dza)
        vec_ref[_ROW_DBX:_ROW_DBX + 1, :] += _row_sum(dzx)
        xb = xr.astype(BF)
        dza_bf = dza.astype(BF)
        dzx_bf = dzx.astype(BF)
        for grp in range(N_LRU_GROUPS):
            cols = slice(grp * LRU_GROUP, (grp + 1) * LRU_GROUP)
            dwa_ref[grp] += _dot_tn(xb[:, cols], dza_bf[:, cols])
            dwx_ref[grp] += _dot_tn(xb[:, cols], dzx_bf[:, cols])
        dxr = (lam * nrm * i + _group_dot(dza_bf, wa_ref, _dot_nt) + _group_dot(dzx_bf, wx_ref, _dot_nt))

        vec_ref[_ROW_DCB:_ROW_DCB + 1, :] += _row_sum(dxr)
        head = dxr_head[...]
        dx = cw_ref[CONV_WIDTH - 1:CONV_WIDTH, :] * dxr
        vec_ref[_ROW_DCW + 3:_ROW_DCW + 4, :] += _row_sum(dxr * x)
        for sft in range(1, CONV_WIDTH):
            k = CONV_WIDTH - 1 - sft
            ahead = _shift_up(dxr, head, sft)
            dx = dx + cw_ref[k:k + 1, :] * ahead
            vec_ref[_ROW_DCW + k:_ROW_DCW + k + 1, :] += _row_sum(ahead * x)
        dxr_head[...] = dxr[0:SUBLANES, :]
        dxg_ref[:, :D_RNN] = dx.astype(BF)
        dxg_ref[:, D_RNN:] = (dy * h * dgg).astype(BF)

    gw = (N_LRU_GROUPS, LRU_GROUP, LRU_GROUP)
    rev = lambda width: pl.BlockSpec((ts, width), lambda i: (tile(i), 0))
    return pl.pallas_call(
        body, name="rnn_bwd", grid=(nt,),
        in_specs=[rev(D_RNN), rev(2 * D_RNN), rev(D_RNN), rev(D_RNN),
                  pl.BlockSpec((_PREV_ROWS, D_RNN), lambda i: (prev(i), 0)),
                  _full_spec(gw), _full_spec(gw),
                  _full_spec((1, D_RNN)), _full_spec((1, D_RNN)), _full_spec((1, D_RNN)),
                  _full_spec((CONV_WIDTH, D_RNN))],
        out_specs=[rev(2 * D_RNN), _full_spec(gw), _full_spec(gw), _full_spec((SUBLANES, D_RNN))],
        out_shape=[jax.ShapeDtypeStruct((s, 2 * D_RNN), BF), jax.ShapeDtypeStruct(gw, F32),
                   jax.ShapeDtypeStruct(gw, F32), jax.ShapeDtypeStruct((SUBLANES, D_RNN), F32)],
        scratch_shapes=[pltpu.VMEM((1, D_RNN), F32), pltpu.VMEM((1, D_RNN), F32),
                        pltpu.VMEM((SUBLANES, D_RNN), F32),
                        pltpu.VMEM((ts, D_RNN), F32), pltpu.VMEM((ts, D_RNN), F32),
                        pltpu.VMEM((ts, D_RNN), F32)],
        compiler_params=_params(("arbitrary",)),
    )(dya_pre, proj, xr_saved, hr, hr, wa, wx, ba, bx, sp, cw)


def _inproj_bwd_call(dxg, duv, dgate, dx1, x, g1, w_in, layer, ts):
    s = x.shape[0]

    def body(dxg_ref, duv_ref, dgt_ref, dx1_ref, x_ref, g_ref, w_ref, dx_ref, dproj_ref, dg_ref):
        @pl.when(pl.program_id(0) == 0)
        def _():
            dg_ref[...] = jnp.zeros_like(dg_ref)

        dproj = jnp.concatenate([dxg_ref[...], duv_ref[...], dgt_ref[...]], axis=1)
        dproj_ref[...] = dproj
        dh = jnp.zeros((ts, D_MODEL), F32)
        for q in range(N_QUARTERS):
            dh = dh + _dot_nt(dproj[:, q * Q_IN:(q + 1) * Q_IN], w_ref[q])
        dx, dg = _rms_bwd(dh, x_ref[...], g_ref[...])
        dx_ref[...] = dx1_ref[...] + dx
        dg_ref[...] += _row_sum(dg)

    return pl.pallas_call(
        body, name="inproj_bwd", grid=(s // ts,),
        in_specs=[_tile_spec(ts, 2 * D_RNN), _tile_spec(ts, 2 * D_SGU), _tile_spec(ts, 2 * D_MODEL),
                  _tile_spec(ts, D_MODEL), _tile_spec(ts, D_MODEL), _full_spec((1, D_MODEL)),
                  pl.BlockSpec((None, N_QUARTERS, D_MODEL, Q_IN), lambda i: (layer, 0, 0, 0))],
        out_specs=[_tile_spec(ts, D_MODEL), _tile_spec(ts, D_IN), _full_spec((1, D_MODEL))],
        out_shape=[jax.ShapeDtypeStruct((s, D_MODEL), F32), jax.ShapeDtypeStruct((s, D_IN), BF),
                   jax.ShapeDtypeStruct((1, D_MODEL), F32)],
        compiler_params=_params(("arbitrary",)),
    )(dxg, duv, dgate, dx1, x, g1, w_in)


def _relu_sq(p):
    return jnp.square(jnp.maximum(p, 0))


def _wgrad_call(a, b, core, tm, tn, tk, col_blocked, name, a_fn=None):
    s, m = a.shape
    n = b.shape[1]
    r, cols = (m, n // N_QUARTERS) if col_blocked else (m // N_QUARTERS, n)
    r2 = r // 2
    per_tile = tm // r
    steps = s // tk

    def body(core_ref, a_ref, b_ref, keep_ref, send_ref, *acc):
        av = a_ref[...]
        if a_fn is not None:
            av = a_fn(av)
        prod = _dot_tn(av.astype(BF), b_ref[...].astype(BF))

        def emit(total):
            for h in range(2):
                @pl.when(core_ref[0] == h)
                def _():
                    for q in range(per_tile):
                        keep_ref[q] = total[q * r + h * r2:q * r + (h + 1) * r2]
                        send_ref[q] = total[q * r + (1 - h) * r2:q * r + (2 - h) * r2].astype(BF)

        if steps == 1:
            emit(prod)
        else:
            acc_ref, = acc
            step = pl.program_id(2)

            @pl.when(step == 0)
            def _():
                acc_ref[...] = prod

            @pl.when(jnp.logical_and(step > 0, step < steps - 1))
            def _():
                acc_ref[...] += prod

            @pl.when(step == steps - 1)
            def _():
                emit(acc_ref[...] + prod)

    if col_blocked:
        per_q = cols // tn
        out_spec = pl.BlockSpec((1, r2, tn), lambda i, j, k, c: (j // per_q, 0, j % per_q))
    else:
        out_spec = pl.BlockSpec((per_tile, r2, tn), lambda i, j, k, c: (i, 0, j))
    return pl.pallas_call(
        body, name=name,
        out_shape=[jax.ShapeDtypeStruct((N_QUARTERS, r2, cols), F32),
                   jax.ShapeDtypeStruct((N_QUARTERS, r2, cols), BF)],
        grid_spec=pltpu.PrefetchScalarGridSpec(
            num_scalar_prefetch=1, grid=(m // tm, n // tn, steps),
            in_specs=[pl.BlockSpec((tk, tm), lambda i, j, k, c: (k, i)),
                      pl.BlockSpec((tk, tn), lambda i, j, k, c: (k, j))],
            out_specs=[out_spec, out_spec],
            scratch_shapes=[] if steps == 1 else [pltpu.VMEM((tm, tn), F32)]),
        compiler_params=_params(("parallel", "parallel", "arbitrary")),
    )(core, a, b)


BIG = ("w_in", "w_up", "w_down", "w_branch_a", "w_branch_b", "w_out")


def _block_diag(w):
    w4 = w.reshape(N_LRU_GROUPS, HEADS_PER_GROUP, RNN_HEAD_DIM, RNN_HEAD_DIM)
    eye = jnp.eye(HEADS_PER_GROUP, dtype=w.dtype)
    return jnp.einsum("gjio,jk->gjiko", w4, eye).reshape(N_LRU_GROUPS, LRU_GROUP, LRU_GROUP)


def _block_diag_extract(d):
    d5 = d.reshape(N_LRU_GROUPS, HEADS_PER_GROUP, RNN_HEAD_DIM, HEADS_PER_GROUP, RNN_HEAD_DIM)
    blocks = [d5[:, j, :, j, :] for j in range(HEADS_PER_GROUP)]
    return jnp.stack(blocks, axis=1).reshape(RNN_HEADS, RNN_HEAD_DIM, RNN_HEAD_DIM)


def _sgu_mask():
    chunk = jnp.arange(SGU_BLOCK) // CHUNK
    return (chunk[:, None] >= chunk[None, :]).astype(F32)


def _layer_small(sm, l, core):
    row = lambda v: v.reshape(1, -1)
    return dict(
        core=core,
        g1=row(sm["norm_mix_g"][l]), g2=row(sm["norm_ffn_g"][l]),
        wa=_block_diag(sm["lru_w_a"][l]).astype(BF), wx=_block_diag(sm["lru_w_x"][l]).astype(BF),
        ba=row(sm["lru_b_a"][l]), bx=row(sm["lru_b_x"][l]),
        sp=row(jax.nn.softplus(-sm["lru_lambda"][l])),
        cw=sm["conv_w"][l] if "conv_w" in sm else None, cb=row(sm["conv_b"][l]),
        wm=(sm["sgu_w_s"][l] * _sgu_mask()).astype(BF),
        bsb=jnp.broadcast_to(sm["sgu_b_s"][l][:, :, None], (SGU_GROUPS, SGU_BLOCK, SGU_BLOCK)),
        lg=row(sm["sgu_ln_g"][l]), lb=row(sm["sgu_ln_b"][l]),
    )


def _layer_fwd_mix(x, big, p, ts, h=None, before_sgu=None, proj=None):
    if h is None:
        h = _norm_call(x, p["g1"], ts)
    if proj is None:
        proj = _inproj_call(h, big["w_in"], 0, 2 * ts)
    xr, hr, ya_pre = _rnn_fwd_call(proj, p["wa"], p["wx"], p["ba"], p["bx"], p["sp"], p["cw"], p["cb"], ts)
    lg = p["lg"] if before_sgu is None else p["lg"] + before_sgu(ya_pre)
    yb_pre = _sgu_fwd_call(proj, p["wm"], p["bsb"], lg, p["lb"], ts)
    return dict(p=p, x=x, h=h, proj=proj, xr=xr, hr=hr, ya_pre=ya_pre, yb_pre=yb_pre)


def _layer_fwd_out(sv, big, ts):
    x1, ya, yb, merged, h2 = _merge_call(sv["x"], sv["proj"], sv["ya_pre"], sv["yb_pre"], big["w_branch_a"],
                                         big["w_branch_b"], big["w_out"], sv["p"]["g2"], 0, ts)
    x2, pre = _ffn_call(x1, h2, big["w_up"], big["w_down"], 0, ts)
    sv.update(x1=x1, ya=ya, yb=yb, merged=merged, h2=h2, pre=pre)
    return x2


def _layer_bwd_ffn(dx, sv, big, ts):
    p = sv["p"]
    dx1, dpre, dg2 = _ffn_bwd_call(dx, sv["pre"], sv["x1"], p["g2"], big["w_up"], big["w_down"], 0, ts)
    tk = dx.shape[0]
    gb = dict(
        w_down=_wgrad_call(sv["pre"], dx, p["core"], Q_FF, D_MODEL // 2, tk, False, "wgrad_down", a_fn=_relu_sq),
        w_up=_wgrad_call(sv["h2"], dpre, p["core"], D_MODEL, Q_FF, tk, True, "wgrad_up"))
    return dx1, gb, dict(norm_ffn_g=dg2[0])


def _layer_bwd_merge(dx1, sv, big, ts, after=None):
    tk = dx1.shape[0]
    core = sv["p"]["core"]
    dya, dyb, dgate, dya_pre, dyb_pre = _merge_bwd_call(
        dx1, sv["proj"], sv["ya"], sv["yb"], big["w_branch_a"], big["w_branch_b"], big["w_out"], 0, ts, after)
    gb = dict(
        w_out=_wgrad_call(sv["merged"], dx1, core, D_MODEL, D_MODEL // 2, tk, False, "wgrad_out"),
        w_branch_a=_wgrad_call(sv["ya_pre"], dya, core, D_RNN, D_MODEL // 2, tk, False, "wgrad_branch_a"),
        w_branch_b=_wgrad_call(sv["yb_pre"], dyb, core, D_SGU, D_MODEL // 2, tk, False, "wgrad_branch_b"))
    return (dgate, dya_pre, dyb_pre), gb


def _layer_bwd_branches(dx1, merge_out, sv, big, lam, ts, after_sgu=None):
    p = sv["p"]
    tk = dx1.shape[0]
    dgate, dya_pre, dyb_pre = merge_out
    gb = {}
    duv, dws, dbs, dlg, dlb = _sgu_bwd_call(dyb_pre, sv["proj"], p["wm"], p["bsb"], _sgu_mask(), p["lg"], p["lb"],
                                            ts)
    ba = p["ba"] if after_sgu is None else p["ba"] + after_sgu(duv)
    dxg, dwa, dwx, vec = _rnn_bwd_call(dya_pre, sv["proj"], sv["xr"], sv["hr"], p["wa"], p["wx"], ba, p["bx"],
                                       p["sp"], p["cw"], ts // 2)
    dx, dproj, dg1 = _inproj_bwd_call(dxg, duv, dgate, dx1, sv["x"], p["g1"], big["w_in"], 0, ts)
    gb["w_in"] = _wgrad_call(sv["h"], dproj, p["core"], D_MODEL, Q_IN, tk // 2, True, "wgrad_in")
    gs = dict(
        norm_mix_g=dg1[0], conv_w=vec[_ROW_DCW:_ROW_DCW + CONV_WIDTH], conv_b=vec[_ROW_DCB],
        lru_w_a=_block_diag_extract(dwa), lru_w_x=_block_diag_extract(dwx),
        lru_b_a=vec[_ROW_DBA].reshape(RNN_HEADS, RNN_HEAD_DIM), lru_b_x=vec[_ROW_DBX].reshape(RNN_HEADS, RNN_HEAD_DIM),
        lru_lambda=-vec[_ROW_DSP] * jax.nn.sigmoid(-lam),
        sgu_ln_g=dlg[0], sgu_ln_b=dlb[0], sgu_w_s=dws, sgu_b_s=dbs.T)
    return dx, gb, gs


def _local_step(x, target, big, sm, ts):
    saved = []
    core = jnp.zeros((1,), jnp.int32)
    for l in range(DEPTH):
        sv = _layer_fwd_mix(x, big[l], _layer_small(sm, l, core), ts)
        x = _layer_fwd_out(sv, big[l], ts)
        saved.append(sv)
    dx, loss, dgf = _loss_call(x, target, sm["final_norm_g"].reshape(1, -1), ts)
    gb, gs = [None] * DEPTH, [None] * DEPTH
    for l in reversed(range(DEPTH)):
        dx1, gb_ffn, gs_ffn = _layer_bwd_ffn(dx, saved[l], big[l], ts)
        merge_out, gb_merge = _layer_bwd_merge(dx1, saved[l], big[l], ts)
        dx, gb_mix, gs_mix = _layer_bwd_branches(dx1, merge_out, saved[l], big[l], sm["lru_lambda"][l], ts)
        gb[l] = {**gb_ffn, **gb_merge, **gb_mix}
        gs[l] = {**gs_ffn, **gs_mix}
    gs = {k: jnp.stack([g[k] for g in gs]) for k in gs[0]}
    gs["final_norm_g"] = dgf[0]
    return loss, dx, gb, gs


EW_VMEM_BYTES = 24 * 1024 * 1024


def _row_block(rows, cols, bytes_per_elem):
    for br in range(min(rows, EW_VMEM_BYTES // (2 * bytes_per_elem * cols)), 0, -1):
        if rows % br == 0 and br % 16 == 0:
            return br
    return rows


def _ew_call(fn, name, operands, outputs, slabs=1, sel=None, into=None, after=None):
    if into is not None and not isinstance(into, (list, tuple)):
        into = [into]
    rows, cols = outputs[0][0].shape[2:]
    br = _row_block(rows, cols, sum(jnp.dtype(a.dtype).itemsize for a, _ in operands + outputs))
    n_in = len(operands)

    def pick(tok, g, s):
        if callable(tok):
            return tok(g, s)
        if tok == "g":
            return g
        if isinstance(tok, tuple):
            return s[tok[1]]
        return tok

    def spec(idx):
        return pl.BlockSpec((None, None, br, cols),
                            lambda g, i, s, idx=idx: (pick(idx[0], g, s), pick(idx[1], g, s), i, 0))

    if sel is None:
        sel = jnp.zeros((1,), jnp.int32)
    in_specs = [spec(idx) for _, idx in operands]
    arrays = [a for a, _ in operands]
    aliases = {}
    for j, buf in enumerate(into or ()):
        in_specs.append(pl.BlockSpec(memory_space=pl.ANY))
        arrays.append(buf)
        aliases[1 + n_in + j] = j
    if after is not None:
        in_specs.append(pl.BlockSpec(memory_space=pl.ANY))
        arrays.append(after)

    def body(sel_ref, *refs):
        outs = fn(*[r[...] for r in refs[:n_in]])
        for o_ref, o in zip(refs[len(arrays):], outs):
            o_ref[...] = o.astype(o_ref.dtype)

    return pl.pallas_call(
        body, name=name, out_shape=[s for s, _ in outputs],
        grid_spec=pltpu.PrefetchScalarGridSpec(
            num_scalar_prefetch=1, grid=(slabs, rows // br),
            in_specs=in_specs,
            out_specs=[spec(idx) for _, idx in outputs]),
        input_output_aliases=aliases,
        compiler_params=_params(("parallel", "parallel")),
    )(sel, *arrays)


def _as4(a):
    return a.reshape((1,) * (4 - a.ndim) + a.shape)


def _adamw(w, g, m, v):
    m = ADAM_B1 * m + (1.0 - ADAM_B1) * g
    v = ADAM_B2 * v + (1.0 - ADAM_B2) * jnp.square(g)
    m_hat = m / (1.0 - ADAM_B1 ** ADAM_STEP)
    v_hat = v / (1.0 - ADAM_B2 ** ADAM_STEP)
    delta = -ADAM_LR * (m_hat / (jnp.sqrt(v_hat) + ADAM_EPS) + ADAM_WD * w)
    return delta, m, v


def _small_adamw_call(ws, gs, ms, vs):
    n = len(ws)

    def body(*refs):
        for k in range(n):
            w, g, m, v = (refs[j * n + k][...] for j in range(4))
            outs = _adamw(w, g, m, v)
            for j in range(3):
                refs[(4 + j) * n + k][...] = outs[j]

    shapes = [jax.ShapeDtypeStruct(w.shape, F32) for w in ws]
    outs = pl.pallas_call(
        body, name="adamw_small", out_shape=shapes * 3,
        in_specs=[pl.BlockSpec(memory_space=pltpu.VMEM)] * (4 * n),
        out_specs=[pl.BlockSpec(memory_space=pltpu.VMEM)] * (3 * n),
        compiler_params=_params(),
    )(*ws, *gs, *ms, *vs)
    return outs[:n], outs[n:2 * n], outs[2 * n:]


ANY = pl.BlockSpec(memory_space=pl.ANY)


def _place():
    x, y, c = lax.axis_index("x"), lax.axis_index("y"), lax.axis_index("c")
    chips = [(1 - x, y), (x, 1 - y), (1 - x, 1 - y)]
    return x, y, c, chips


def _remote(src, dst, send_sem, recv_sem, to):
    return pltpu.make_async_remote_copy(src_ref=src, dst_ref=dst, send_sem=send_sem, recv_sem=recv_sem,
                                        device_id=to, device_id_type=MESH)


def _sibling_send_call(items):
    n = len(items)

    def body(*refs):
        src, out = refs[:n], refs[n:2 * n]
        send_sems, recv_sems = refs[2 * n:]
        x, y, c, _ = _place()
        copies = [_remote(src[w], out[w], send_sems.at[w], recv_sems.at[w], (x, y, 1 - c)) for w in range(n)]
        for cp in copies:
            cp.start()
        for cp in copies:
            cp.wait()

    return pl.pallas_call(
        body, name="grads_to_sibling",
        out_shape=[jax.ShapeDtypeStruct(a.shape, a.dtype) for a in items],
        in_specs=[ANY] * n, out_specs=[ANY] * n,
        scratch_shapes=[pltpu.SemaphoreType.DMA((n,)), pltpu.SemaphoreType.DMA((n,))],
        compiler_params=_params(vmem=False, has_side_effects=True),
    )(*items)


def _sibling_inplace_call(name, bufs, slabs, n_pairs):
    n = len(bufs)

    def body(*refs):
        out = refs[n:2 * n]
        send_sems, recv_sems = refs[2 * n:]
        x, y, c, _ = _place()
        sibling = (x, y, 1 - c)
        pairs = [pair for w, ref in enumerate(out) for pair in slabs(ref, c, w)]
        sends = [_remote(s, s, send_sems.at[k], recv_sems.at[k], sibling) for k, (s, _) in enumerate(pairs)]
        for cp in sends:
            cp.start()
        for k, (_, r) in enumerate(pairs):
            _remote(r, r, send_sems.at[k], recv_sems.at[k], sibling).wait_recv()
        for cp in sends:
            cp.wait_send()

    return pl.pallas_call(
        body, name=name,
        out_shape=[jax.ShapeDtypeStruct(a.shape, a.dtype) for a in bufs],
        in_specs=[ANY] * n, out_specs=[ANY] * n,
        input_output_aliases={w: w for w in range(n)},
        scratch_shapes=[pltpu.SemaphoreType.DMA((n_pairs,)), pltpu.SemaphoreType.DMA((n_pairs,))],
        compiler_params=_params(vmem=False, has_side_effects=True),
    )(*bufs)


HBM_SPEC = pl.BlockSpec(memory_space=pltpu.HBM)
SEM_SPEC = pl.BlockSpec(memory_space=pltpu.SEMAPHORE)
DATAFLOW_EFFECT = pltpu.SideEffectType.DATAFLOW_SIDE_EFFECTING


def _exchange_start(name, bufs, copies, n_copies, after):
    n = len(bufs)

    def body(*refs):
        ins, send_sems, recv_sems, token = refs[:n], refs[n + 1], refs[n + 2], refs[-1]
        for k, (src, dst, to) in enumerate(copies(ins)):
            _remote(src, dst, send_sems.at[k], recv_sems.at[k], to).start()
        token[...] = jnp.zeros_like(token)

    outs = pl.pallas_call(
        body, name=name,
        out_shape=(pltpu.SemaphoreType.DMA((n_copies,)), pltpu.SemaphoreType.DMA((n_copies,)),
                   *[pltpu.HBM(b.shape, b.dtype) for b in bufs], jax.ShapeDtypeStruct((SUBLANES, 128), F32)),
        in_specs=[HBM_SPEC] * n + [ANY],
        out_specs=(SEM_SPEC, SEM_SPEC, *[HBM_SPEC] * n, pl.BlockSpec(memory_space=pltpu.VMEM)),
        input_output_aliases={w: w + 2 for w in range(n)},
        compiler_params=pltpu.CompilerParams(has_side_effects=DATAFLOW_EFFECT),
    )(*[pltpu.with_memory_space_constraint(b, pltpu.HBM) for b in bufs], after)
    return outs[0], outs[1], list(outs[2:2 + n]), outs[-1]


def _exchange_wait(name, send_sems, recv_sems, bufs, copies, after):
    n = len(bufs)

    def body(*refs):
        ins, send_sems, recv_sems = refs[:n], refs[n], refs[n + 1]
        for k, (src, dst, to) in enumerate(copies(ins)):
            cp = _remote(src, dst, send_sems.at[k], recv_sems.at[k], to)
            cp.wait_send()
            cp.wait_recv()

    return pl.pallas_call(
        body, name=name,
        out_shape=[pltpu.HBM(b.shape, b.dtype) for b in bufs],
        in_specs=[HBM_SPEC] * n + [SEM_SPEC, SEM_SPEC, ANY],
        out_specs=[HBM_SPEC] * n,
        input_output_aliases={w: w for w in range(n)},
        compiler_params=pltpu.CompilerParams(has_side_effects=DATAFLOW_EFFECT),
    )(*bufs, send_sems, recv_sems, after)


def _gather_copies(refs):
    x, y, c, chips = _place()
    mine = 2 * (2 * x + y) + c
    return [(ref.at[mine], ref.at[mine], (qx, qy, c)) for ref in refs for qx, qy in chips]


def _forward_copies(refs):
    x, y, c, chips = _place()
    return [(ref.at[2 * (2 * qx + qy) + c], ref.at[2 * (2 * qx + qy) + c], (x, y, 1 - c))
            for ref in refs for qx, qy in chips]


def _gather_forward_slabs(ref, c, w):
    x, y, _, chips = _place()
    return [(ref.at[2 * (2 * qx + qy) + c], ref.at[2 * (2 * qx + qy) + 1 - c]) for qx, qy in chips]


def _device_peers():
    x, y, c, _ = _place()
    return 4 * x + 2 * y + c, [(k, (x ^ ((k >> 2) & 1), y ^ ((k >> 1) & 1), c ^ (k & 1))) for k in range(1, 8)]


def _small_scatter_copies(refs):
    me, peers = _device_peers()
    return [(refs[0].at[me ^ k], refs[1].at[me], to) for k, to in peers]


def _small_spread_copies(refs):
    me, peers = _device_peers()
    return [(refs[0].at[me], refs[0].at[me], to) for _, to in peers]


def _sibling_copies(refs):
    n = len(refs) // 2
    x, y, c, _ = _place()
    return [(refs[w], refs[n + w], (x, y, 1 - c)) for w in range(n)]


def _owner_copies(refs):
    n = len(refs) // 2
    x, y, c, chips = _place()
    return [(refs[w].at[2 * qx + qy], refs[n + w].at[j], (qx, qy, c))
            for w in range(n) for j, (qx, qy) in enumerate(chips)]


N_DEVICES = 8
SMALL_ROWS = 616


SMALL = ("norm_mix_g", "conv_w", "conv_b", "lru_w_a", "lru_b_a", "lru_w_x", "lru_b_x", "lru_lambda",
         "sgu_ln_g", "sgu_ln_b", "sgu_w_s", "sgu_b_s", "norm_ffn_g", "final_norm_g")
WEIGHTS = ("norm_mix_g", "w_in", "conv_w", "conv_b", "lru_w_a", "lru_b_a", "lru_w_x", "lru_b_x", "lru_lambda",
           "sgu_ln_g", "sgu_ln_b", "sgu_w_s", "sgu_b_s", "w_branch_a", "w_branch_b", "w_out", "norm_ffn_g",
           "w_up", "w_down", "final_norm_g")
PACK_ALIGN = SUBLANES * 128


PACKED = SMALL + ("loss",)


def _pack_small(gs):
    parts = []
    for k in PACKED:
        flat = gs[k].reshape(-1)
        parts.append(jnp.pad(flat, (0, -flat.size % PACK_ALIGN)))
    flat = jnp.concatenate(parts)
    flat = jnp.pad(flat, (0, N_DEVICES * SMALL_ROWS * 128 - flat.size))
    return flat.reshape(N_DEVICES, SMALL_ROWS, 128)


def _unpack_small(buf, like):
    flat = buf.reshape(-1)
    out, off = {}, 0
    for k in PACKED:
        size = like[k].size
        out[k] = flat[off:off + size].reshape(like[k].shape)
        off += size + (-size % PACK_ALIGN)
    return out


def _as_rows(a):
    return a.reshape(-1, a.shape[-1])


def kernel(x, norm_mix_g, w_in, conv_w, conv_b, lru_w_a, lru_b_a, lru_w_x, lru_b_x, lru_lambda, sgu_ln_g, sgu_ln_b, sgu_w_s, sgu_b_s, w_branch_a, w_branch_b, w_out, norm_ffn_g, w_up, w_down, final_norm_g, loss_target, m_norm_mix_g, m_w_in, m_conv_w, m_conv_b, m_lru_w_a, m_lru_b_a, m_lru_w_x, m_lru_b_x, m_lru_lambda, m_sgu_ln_g, m_sgu_ln_b, m_sgu_w_s, m_sgu_b_s, m_w_branch_a, m_w_branch_b, m_w_out, m_norm_ffn_g, m_w_up, m_w_down, m_final_norm_g, v_norm_mix_g, v_w_in, v_conv_w, v_conv_b, v_lru_w_a, v_lru_b_a, v_lru_w_x, v_lru_b_x, v_lru_lambda, v_sgu_ln_g, v_sgu_ln_b, v_sgu_w_s, v_sgu_b_s, v_w_branch_a, v_w_branch_b, v_w_out, v_norm_ffn_g, v_w_up, v_w_down, v_final_norm_g):
    w = dict(norm_mix_g=norm_mix_g, w_in=w_in, conv_w=conv_w, conv_b=conv_b, lru_w_a=lru_w_a, lru_b_a=lru_b_a,
             lru_w_x=lru_w_x, lru_b_x=lru_b_x, lru_lambda=lru_lambda, sgu_ln_g=sgu_ln_g, sgu_ln_b=sgu_ln_b,
             sgu_w_s=sgu_w_s, sgu_b_s=sgu_b_s, w_branch_a=w_branch_a, w_branch_b=w_branch_b, w_out=w_out,
             norm_ffn_g=norm_ffn_g, w_up=w_up, w_down=w_down, final_norm_g=final_norm_g)
    m = dict(norm_mix_g=m_norm_mix_g, w_in=m_w_in, conv_w=m_conv_w, conv_b=m_conv_b, lru_w_a=m_lru_w_a,
             lru_b_a=m_lru_b_a, lru_w_x=m_lru_w_x, lru_b_x=m_lru_b_x, lru_lambda=m_lru_lambda,
             sgu_ln_g=m_sgu_ln_g, sgu_ln_b=m_sgu_ln_b, sgu_w_s=m_sgu_w_s, sgu_b_s=m_sgu_b_s,
             w_branch_a=m_w_branch_a, w_branch_b=m_w_branch_b, w_out=m_w_out, norm_ffn_g=m_norm_ffn_g,
             w_up=m_w_up, w_down=m_w_down, final_norm_g=m_final_norm_g)
    v = dict(norm_mix_g=v_norm_mix_g, w_in=v_w_in, conv_w=v_conv_w, conv_b=v_conv_b, lru_w_a=v_lru_w_a,
             lru_b_a=v_lru_b_a, lru_w_x=v_lru_w_x, lru_b_x=v_lru_b_x, lru_lambda=v_lru_lambda,
             sgu_ln_g=v_sgu_ln_g, sgu_ln_b=v_sgu_ln_b, sgu_w_s=v_sgu_w_s, sgu_b_s=v_sgu_b_s,
             w_branch_a=v_w_branch_a, w_branch_b=v_w_branch_b, w_out=v_w_out, norm_ffn_g=v_norm_ffn_g,
             w_up=v_w_up, w_down=v_w_down, final_norm_g=v_final_norm_g)
    core = lax.axis_index("c")
    chip = 2 * lax.axis_index("x") + lax.axis_index("y")
    sel = jnp.stack([core, 1 - core, chip, 2 * chip + core]).astype(jnp.int32)
    this_core, other_core, this_chip = ("sel", 0), ("sel", 1), ("sel", 2)
    sds = jax.ShapeDtypeStruct

    ts = TOKEN_TILE

    def after_all(arrays):
        return jnp.stack([a[(0,) * a.ndim].astype(F32) for a in arrays])

    halves ={k: (w[k].shape[1] // 2, w[k].shape[2]) for k in BIG}

    def half_view(k, a):
        return a.reshape((2 * N_QUARTERS,) + halves[k])

    def full_view(k, a):
        if k == "conv_w":
            return a.reshape(N_QUARTERS, DEPTH, CONV_WIDTH, -1).transpose(1, 2, 0, 3).reshape(DEPTH, CONV_WIDTH, D_RNN)
        r2, cols = halves[k]
        if k in ("w_in", "w_up"):
            return a.reshape(1, N_QUARTERS, 2 * r2, cols)
        return a.reshape(1, 2 * N_QUARTERS * r2, cols)

    layer_bufs = [{}, {}]

    def cast_weights(k, after):
        _, r, cols = w[k].shape
        w4 = w[k].reshape(DEPTH, 1, r, cols)
        outs = _ew_call(lambda a, b: (a, b), "cast_weights", [(w4, (0, 0)), (w4, (1, 0))],
                        [(sds((1, N_QUARTERS, r, cols), BF), (0, this_chip))] * DEPTH, 1, sel, after=after)
        for l in range(DEPTH):
            layer_bufs[l][k] = half_view(k, outs[l])

    conv_buf = lax.dynamic_update_slice_in_dim(
        jnp.zeros((N_QUARTERS, DEPTH) + conv_w.shape[1:], F32), conv_w[None], chip, axis=0)
    layer_bufs[0]["conv_w"] = conv_buf.reshape((2 * N_QUARTERS,) + conv_w.shape[1:])
    sm = {k: w[k] for k in SMALL if k != "conv_w"}

    def gather_start(tag, l, keys, after):
        bufs = [layer_bufs[l][k] for k in keys]
        return _exchange_start(f"gather_start_{tag}", bufs, _gather_copies, 3 * len(keys), after)

    def gather_finish(tag, keys, started, after):
        send_sems, recv_sems, thru, _ = started
        landed = _exchange_wait(f"gather_wait_{tag}", send_sems, recv_sems, thru, _gather_copies, after)
        landed = _sibling_inplace_call("gather_forward", landed, _gather_forward_slabs, 3 * len(keys))
        return {k: full_view(k, a) for k, a in zip(keys, landed)}

    first, rest = ("w_in",), tuple(k for k in BIG if k != "w_in")
    cast_weights("w_in", None)
    started_a = gather_start("0a", 0, first + ("conv_w",), sel)
    for k in rest:
        cast_weights(k, started_a[3])
    started_b = gather_start("0b", 0, rest, started_a[3])
    started_c = gather_start("1a", 1, first, started_b[3])
    started_d = gather_start("1b", 1, rest, started_c[3])

    def arrives(tag, keys, started):
        state = {}

        def hook(after):
            landed = _exchange_wait(f"gather_wait_{tag}", started[0], started[1], started[2], _gather_copies, after)
            state["forward"] = _exchange_start(f"forward_start_{tag}", landed, _forward_copies, 3 * len(keys), after)
            return state["forward"][3][0, 0]

        def finish(after):
            send_sems, recv_sems, thru, _ = state["forward"]
            done = _exchange_wait(f"forward_wait_{tag}", send_sems, recv_sems, thru, _forward_copies, after)
            return {k: full_view(k, a) for k, a in zip(keys, done)}

        return hook, finish

    p0, p1 = _layer_small(sm, 0, sel[0:1]), _layer_small(sm, 1, sel[0:1])
    h0 = _norm_call(x[0], p0["g1"], ts)
    proj_own = _inproj_part_call(h0, full_view("w_in", started_a[2][0]), 2 * ts, sel[2:3], 0, 1)
    ready = after_all([started_d[3], proj_own] + [p[k] for p in (p0, p1) for k in ("wa", "wx", "wm")])
    big0 = gather_finish("0a", first + ("conv_w",), started_a, ready)
    for l, p in enumerate((p0, p1)):
        p["cw"] = big0["conv_w"][l]
    proj0 = _inproj_part_call(h0, big0["w_in"], 2 * ts, sel[2:3], 1, N_QUARTERS - 1, proj_own)
    hook, finish = arrives("0b", rest, started_b)
    sv0 = _layer_fwd_mix(x[0], big0, p0, ts, h0, hook, proj0)
    big0.update(finish(sv0["yb_pre"]))
    x_mid = _layer_fwd_out(sv0, big0, ts)
    hook, finish = arrives("1a", first, started_c)
    h1 = _norm_call(x_mid, p1["g1"] + hook(x_mid), ts)
    big1 = finish(h1)
    hook, finish = arrives("1b", rest, started_d)
    sv1 = _layer_fwd_mix(x_mid, big1, p1, ts, h1, hook)
    big1.update(finish(sv1["yb_pre"]))
    x_out = _layer_fwd_out(sv1, big1, ts)
    dx, loss, dgf = _loss_call(x_out, loss_target[0], final_norm_g.reshape(1, -1), ts)

    def pair_start(tag, gb, after):
        sends = [gb[k][1] for k in gb]
        zones = [lax.empty(a.shape, BF) for a in sends]
        return _exchange_start(f"pair_start_{tag}", sends + zones, _sibling_copies, len(sends), after)

    def reduce_start(tag, gb, after, pair=None):
        keys = tuple(gb)
        if pair is None:
            from_sibling = _sibling_send_call([gb[k][1] for k in keys])
        else:
            done = _exchange_wait(f"pair_wait_{tag}", pair[0], pair[1], pair[2], _sibling_copies, after)
            from_sibling = done[len(keys):]
        sums = [
            _ew_call(lambda a, b: (a + b.astype(F32),), "pair_sum", [(gb[k][0][None], (0, "g")), (r[None], (0, "g"))],
                     [(sds((1,) + r.shape, BF), (0, "g"))], N_QUARTERS)[0][0]
            for k, r in zip(keys, from_sibling)]
        zones = [lax.empty((3,) + a.shape[1:], BF) for a in sums]
        started = _exchange_start(f"reduce_start_{tag}", sums + zones, _owner_copies, 3 * len(keys), after)
        return keys, started

    def reduce_finish(tag, l, keys_started, after, reduced):
        keys, (send_sems, recv_sems, thru, _) = keys_started
        done = _exchange_wait(f"reduce_wait_{tag}", send_sems, recv_sems, thru, _owner_copies, after)
        sums, zones = done[:len(keys)], done[len(keys):]
        for i, k in enumerate(keys):
            r2, cols = halves[k]
            reduced[k] = _ew_call(
                lambda a, b, c, d: (((a.astype(F32) + b.astype(F32)) + c.astype(F32)) + d.astype(F32),),
                "quarter_sum", [(sums[i][None], (0, this_chip))] + [(zones[i][None], (0, j)) for j in range(3)],
                [(sds((DEPTH, 2, r2, cols), F32), (l, this_core))], 1, sel, into=reduced.get(k))[0]

    def behind(params, key, started):
        return dict(params, **{key: params[key] + started[1][3][0, 0]})

    dx1, gb_ffn, gs1 = _layer_bwd_ffn(dx, sv1, big1, ts)
    merge_out, gb_merge = _layer_bwd_merge(dx1, sv1, big1, ts)
    dx_mid, gb_in, gs1_mix = _layer_bwd_branches(dx1, merge_out, sv1, big1, lru_lambda[1], ts)
    gb_1 = {**gb_ffn, **gb_merge, **gb_in}
    pair_1 = pair_start("1", gb_1, dx_mid)
    sv0["p"] = behind(sv0["p"], "g2", (None, pair_1))
    dx1, gb_ffn, gs0 = _layer_bwd_ffn(dx_mid, sv0, big0, ts)
    exchange_1 = reduce_start("1", gb_1, dx1, pair_1)
    pair_0a = pair_start("0a", gb_ffn, exchange_1[1][3])
    merge_out, gb_merge = _layer_bwd_merge(dx1, sv0, big0, ts, pair_0a[3])
    exchange_0a = reduce_start("0a", gb_ffn, merge_out[0], pair_0a)
    pair_0b = pair_start("0b", gb_merge, exchange_0a[1][3])
    sv0["p"] = behind(sv0["p"], "lg", (None, pair_0b))
    started_0b = {}

    def after_sgu(duv):
        started_0b["exchange"] = reduce_start("0b", gb_merge, duv, pair_0b)
        return started_0b["exchange"][1][3][0, 0]

    grad_x, gb_in, gs0_mix = _layer_bwd_branches(dx1, merge_out, sv0, big0, lru_lambda[0], ts, after_sgu)
    exchange_0b = started_0b["exchange"]
    exchange_0c = reduce_start("0c", gb_in, exchange_0b[1][3])
    layer_gs = [{**gs0, **gs0_mix}, {**gs1, **gs1_mix}]
    gs = {k: jnp.stack([g[k] for g in layer_gs]) for k in layer_gs[0]}
    gs["final_norm_g"] = dgf[0]
    gs["loss"] = loss[0, 0:1]

    me = ("sel", 3)
    piece = (1, N_DEVICES, SMALL_ROWS, 128)
    packed = _pack_small(gs).reshape(piece)
    scatter = _exchange_start("small_scatter_start", [packed[0], lax.empty(piece[1:], F32)], _small_scatter_copies,
                              N_DEVICES - 1, exchange_0c[1][3])
    reduced = {}
    reduce_finish("1", 1, exchange_1, scatter[3], reduced)
    reduce_finish("0a", 0, exchange_0a, reduced["w_in"], reduced)
    reduce_finish("0b", 0, exchange_0b, reduced["w_down"], reduced)

    def swap_slabs(ref, c, i):
        layers = (1,) if BIG[i] == "w_in" else range(DEPTH)
        return [(ref.at[l, c], ref.at[l, 1 - c]) for l in layers]

    swapped = dict(zip(BIG, _sibling_inplace_call("grads_swap_halves", [reduced[k] for k in BIG], swap_slabs,
                                                  DEPTH * len(BIG) - 1)))

    def adamw_layers(k, grad, layer, into, after=None):
        if layer is None:
            views = [_as4(_as_rows(a)) for a in (w[k], grad, m[k], v[k])]
            idx = (0, 0)
        else:
            views = [a.reshape((1,) + w[k].shape) for a in (w[k], grad, m[k], v[k])]
            idx = (0, layer)
        return _ew_call(_adamw, "adamw_big", [(a, idx) for a in views], [(sds(views[0].shape, F32), idx)] * 3,
                        into=into, after=after)

    updated, last_update = {}, None
    for k in BIG:
        updated[k] = adamw_layers(k, swapped[k], 1 if k == "w_in" else None, None, last_update)
        last_update = updated[k][0]
    scattered = _exchange_wait("small_scatter_wait", scatter[0], scatter[1], scatter[2], _small_scatter_copies,
                               last_update)
    summed = _ew_call(
        lambda *parts: (functools.reduce(lambda a, b: a + b, parts),), "small_sum",
        [(scattered[0][None], (0, me))]
        + [(scattered[1][None], (0, lambda g, s, k=k: s[3] ^ k)) for k in range(1, N_DEVICES)],
        [(sds(piece, F32), (0, me))], 1, sel)[0]
    spread = _exchange_start("small_spread_start", [summed[0]], _small_spread_copies, N_DEVICES - 1, summed)
    reduced["w_in"] = swapped["w_in"]
    reduce_finish("0c", 0, exchange_0c, spread[3], reduced)
    last = _sibling_inplace_call("grads_swap_last", [reduced["w_in"]],
                                 lambda ref, c, i: [(ref.at[0, c], ref.at[0, 1 - c])], 1)[0]
    swapped["w_in"] = last
    updated["w_in"] = adamw_layers("w_in", last, 0, updated["w_in"])
    grads_big = {k: swapped[k].reshape(w[k].shape) for k in BIG}
    delta, new_m, new_v = ({k: updated[k][j].reshape(w[k].shape) for k in BIG} for j in range(3))
    gathered_small = _exchange_wait("small_spread_wait", spread[0], spread[1], spread[2], _small_spread_copies,
                                    updated["w_in"][0])[0]

    like = {k: jax.ShapeDtypeStruct(gs[k].shape, F32) for k in SMALL}
    like["loss"] = jax.ShapeDtypeStruct((1,), F32)
    grads_small = _unpack_small(gathered_small, like)
    total = grads_small.pop("loss")[0]
    conv_q = grads_small["conv_w"].reshape(DEPTH, CONV_WIDTH, N_QUARTERS, D_RNN // N_QUARTERS)
    grads_small["conv_w"] = lax.dynamic_index_in_dim(conv_q, chip, axis=2, keepdims=False)
    outs = _small_adamw_call(*[[_as_rows(d[k]) for k in SMALL] for d in (w, grads_small, m, v)])
    for d, o in zip((delta, new_m, new_v), outs):
        for k, a in zip(SMALL, o):
            d[k] = a.reshape(w[k].shape)

    grads = {**grads_big, **grads_small}
    return (total, grad_x[None], *[grads[k] for k in WEIGHTS], *[delta[k] for k in WEIGHTS],
            *[new_m[k] for k in WEIGHTS], *[new_v[k] for k in WEIGHTS])
```

```python
import functools
import math

import jax
import jax.numpy as jnp
from jax import lax
from jax.experimental import pallas as pl
from jax.experimental.pallas import tpu as pltpu

F32 = jnp.float32
BF = jnp.bfloat16

DEPTH = 2
D_MODEL = 1024
D_RNN = 1280
D_SGU = 1024
D_FF = 4096
D_IN = 2 * D_RNN + 2 * D_SGU + 2 * D_MODEL
N_QUARTERS = 4
Q_IN = D_IN // N_QUARTERS
Q_FF = D_FF // N_QUARTERS
RNN_HEADS = 20
RNN_HEAD_DIM = 64
LRU_GROUP = 256
N_LRU_GROUPS = D_RNN // LRU_GROUP
HEADS_PER_GROUP = LRU_GROUP // RNN_HEAD_DIM
CONV_WIDTH = 4
LRU_C = 8.0
SGU_GROUPS = 8
SGU_BLOCK = 128
CHUNK = 64
EPS = 1e-6

ADAM_LR = 0.001
ADAM_B1 = 0.9
ADAM_B2 = 0.999
ADAM_EPS = 1e-08
ADAM_WD = 0.01
ADAM_STEP = 10

SUBLANES = 8
TOKEN_TILE = 512
VMEM_LIMIT_BYTES = 56 * 1024 * 1024

MESH = pl.DeviceIdType.MESH


def _params(semantics=None, vmem=True, **kw):
    return pltpu.CompilerParams(
        dimension_semantics=semantics,
        vmem_limit_bytes=VMEM_LIMIT_BYTES if vmem else None,
        **kw,
    )


def _dot(a, b):
    return jnp.dot(a, b, preferred_element_type=F32)


def _dot_nt(a, b):
    return lax.dot_general(a, b, (((1,), (1,)), ((), ())), preferred_element_type=F32)


def _dot_tn(a, b):
    return lax.dot_general(a, b, (((0,), (0,)), ((), ())), preferred_element_type=F32)


_GELU_C = math.sqrt(2.0 / math.pi)
_GELU_A = 0.044715


def _gelu(x):
    return 0.5 * x * (1.0 + jnp.tanh(_GELU_C * (x + _GELU_A * x * x * x)))


def _gelu_and_grad(x):
    x2 = x * x
    t = jnp.tanh(_GELU_C * (x + _GELU_A * x2 * x))
    du = _GELU_C * (1.0 + 3.0 * _GELU_A * x2)
    return 0.5 * x * (1.0 + t), 0.5 * (1.0 + t) + 0.5 * x * (1.0 - t * t) * du


def _rms_stats(x):
    return lax.rsqrt(jnp.mean(x * x, axis=-1, keepdims=True) + EPS)


def _rms_bwd(dy, x, g):
    rs = _rms_stats(x)
    n = x * rs
    dn = dy * g
    dx = rs * (dn - n * jnp.mean(dn * n, axis=-1, keepdims=True))
    return dx, dy * n


def _row_sum(x):
    return jnp.sum(x, axis=0, keepdims=True)


def _tile_spec(ts, width, col=0):
    return pl.BlockSpec((ts, width), lambda i, col=col: (i, col))


def _full_spec(shape):
    zeros = (0,) * len(shape)
    return pl.BlockSpec(shape, lambda *_: zeros)


def _layer_spec(w, layer):
    zeros = (0,) * (w.ndim - 1)
    return pl.BlockSpec((None,) + tuple(w.shape[1:]), lambda *_: (layer,) + zeros)


def _norm_call(x, g, ts):
    s = x.shape[0]

    def body(x_ref, g_ref, h_ref):
        xv = x_ref[...]
        h_ref[...] = (xv * _rms_stats(xv) * g_ref[...]).astype(BF)

    return pl.pallas_call(
        body, name="norm_fwd", grid=(s // ts,),
        in_specs=[_tile_spec(ts, D_MODEL), _full_spec((1, D_MODEL))],
        out_specs=_tile_spec(ts, D_MODEL),
        out_shape=jax.ShapeDtypeStruct((s, D_MODEL), BF),
        compiler_params=_params(("parallel",)),
    )(x, g)


def _inproj_call(h, w_in, layer, ts):
    s = h.shape[0]

    def body(h_ref, w_ref, o_ref):
        o_ref[...] = _dot(h_ref[...], w_ref[...]).astype(BF)

    return pl.pallas_call(
        body, name="inproj_fwd", grid=(N_QUARTERS, s // ts),
        in_specs=[
            pl.BlockSpec((ts, D_MODEL), lambda q, i: (i, 0)),
            pl.BlockSpec((None, None, D_MODEL, Q_IN), lambda q, i: (layer, q, 0, 0)),
        ],
        out_specs=pl.BlockSpec((ts, Q_IN), lambda q, i: (i, q)),
        out_shape=jax.ShapeDtypeStruct((s, D_IN), BF),
        compiler_params=_params(("parallel", "parallel")),
    )(h, w_in)


def _inproj_part_call(h, w_in, ts, own, first, count, into=None):
    s = h.shape[0]

    def quarter(j, sel):
        return (sel[0] + first + j) % N_QUARTERS

    def body(sel_ref, h_ref, w_ref, *rest):
        rest[-1][...] = _dot(h_ref[...], w_ref[...]).astype(BF)

    in_specs = [pl.BlockSpec((ts, D_MODEL), lambda j, i, sel: (i, 0)),
                pl.BlockSpec((None, None, D_MODEL, Q_IN), lambda j, i, sel: (0, quarter(j, sel), 0, 0))]
    operands = [h, w_in]
    aliases = {}
    if into is not None:
        in_specs.append(pl.BlockSpec(memory_space=pl.ANY))
        operands.append(into)
        aliases = {3: 0}
    return pl.pallas_call(
        body, name="inproj_fwd_part", out_shape=jax.ShapeDtypeStruct((s, D_IN), BF),
        grid_spec=pltpu.PrefetchScalarGridSpec(
            num_scalar_prefetch=1, grid=(count, s // ts), in_specs=in_specs,
            out_specs=pl.BlockSpec((ts, Q_IN), lambda j, i, sel: (i, quarter(j, sel)))),
        input_output_aliases=aliases,
        compiler_params=_params(("parallel", "parallel")),
    )(own, *operands)


def _shift_down(x, tail, s):
    xr = pltpu.roll(x, s, 0)
    tr = pltpu.roll(tail, s, 0)
    row = lax.broadcasted_iota(jnp.int32, tail.shape, 0)
    top = jnp.where(row < s, tr, xr[0:SUBLANES])
    return jnp.concatenate([top, xr[SUBLANES:]], axis=0)


def _shift_up(x, head, s):
    t = x.shape[0]
    xr = pltpu.roll(x, t - s, 0)
    hr = pltpu.roll(head, SUBLANES - s, 0)
    row = lax.broadcasted_iota(jnp.int32, head.shape, 0)
    bottom = jnp.where(row >= SUBLANES - s, hr, xr[t - SUBLANES:])
    return jnp.concatenate([xr[: t - SUBLANES], bottom], axis=0)


def _conv_fwd(x, tail, cw_ref, cb_ref):
    out = cb_ref[...] + cw_ref[CONV_WIDTH - 1:CONV_WIDTH, :] * x
    for s in range(1, CONV_WIDTH):
        k = CONV_WIDTH - 1 - s
        out = out + cw_ref[k:k + 1, :] * _shift_down(x, tail, s)
    return out


def _group_dot(x_bf, w_ref, dot):
    cols = [dot(x_bf[:, g * LRU_GROUP:(g + 1) * LRU_GROUP], w_ref[g]) for g in range(N_LRU_GROUPS)]
    return jnp.concatenate(cols, axis=1)


def _lru_gates(xr, wa_ref, wx_ref, ba_ref, bx_ref, sp_ref):
    xb = xr.astype(BF)
    r = jax.nn.sigmoid(_group_dot(xb, wa_ref, _dot) + ba_ref[...])
    i = jax.nn.sigmoid(_group_dot(xb, wx_ref, _dot) + bx_ref[...])
    log_a = (-LRU_C * r) * sp_ref[...]
    a = jnp.exp(log_a)
    nrm2 = -jnp.tanh(log_a) * (a * a + 1.0)
    inv_nrm = lax.rsqrt(jnp.maximum(nrm2, 1e-36))
    return r, i, a, nrm2 * inv_nrm, inv_nrm


def _linear_scan(a, b, carry, al_ref, bl_ref, h_ref, reverse):
    t, c = a.shape
    rowm = lax.broadcasted_iota(jnp.int32, (t, c), 0) & (SUBLANES - 1)
    for d in (1, 2, 4):
        if reverse:
            keep, sh = rowm < SUBLANES - d, t - d
        else:
            keep, sh = rowm >= d, d
        a_sh = jnp.where(keep, pltpu.roll(a, sh, 0), 1.0)
        b_sh = jnp.where(keep, pltpu.roll(b, sh, 0), 0.0)
        b = a * b_sh + b
        a = a * a_sh
    al_ref[...] = a
    bl_ref[...] = b
    groups = t // SUBLANES

    def step(j, state):
        jj = groups - 1 - j if reverse else j
        off = pl.multiple_of(jj * SUBLANES, SUBLANES)
        rows = bl_ref[pl.ds(off, SUBLANES), :] + al_ref[pl.ds(off, SUBLANES), :] * state
        h_ref[pl.ds(off, SUBLANES), :] = rows
        last = rows[0:1, :] if reverse else rows[SUBLANES - 1:SUBLANES, :]
        return jnp.broadcast_to(last, (SUBLANES, c))

    out = lax.fori_loop(0, groups, step, jnp.broadcast_to(carry, (SUBLANES, c)))
    return out[0:1, :]


def _rnn_fwd_call(proj, wa, wx, ba, bx, sp, cw, cb, ts):
    s = proj.shape[0]

    def body(xg_ref, wa_ref, wx_ref, ba_ref, bx_ref, sp_ref, cw_ref, cb_ref, xr_ref, hr_ref, ya_ref,
             tail_sc, carry_sc, al_sc, bl_sc, h_sc):
        @pl.when(pl.program_id(0) == 0)
        def _():
            tail_sc[...] = jnp.zeros_like(tail_sc)
            carry_sc[...] = jnp.zeros_like(carry_sc)

        x = xg_ref[:, :D_RNN].astype(F32)
        g = xg_ref[:, D_RNN:]
        xr = _conv_fwd(x, tail_sc[...], cw_ref, cb_ref)
        tail_sc[...] = x[ts - SUBLANES:, :]
        xr_ref[...] = xr.astype(BF)
        _, i, a, nrm, _ = _lru_gates(xr, wa_ref, wx_ref, ba_ref, bx_ref, sp_ref)
        carry_sc[...] = _linear_scan(a, nrm * (i * xr), carry_sc[...], al_sc, bl_sc, h_sc, False)
        h = h_sc[...]
        hr_ref[...] = h.astype(BF)
        ya_ref[...] = (h * _gelu(g)).astype(BF)

    gw = (N_LRU_GROUPS, LRU_GROUP, LRU_GROUP)
    return pl.pallas_call(
        body, name="rnn_fwd", grid=(s // ts,),
        in_specs=[_tile_spec(ts, 2 * D_RNN), _full_spec(gw), _full_spec(gw),
                  _full_spec((1, D_RNN)), _full_spec((1, D_RNN)), _full_spec((1, D_RNN)),
                  _full_spec((CONV_WIDTH, D_RNN)), _full_spec((1, D_RNN))],
        out_specs=[_tile_spec(ts, D_RNN)] * 3,
        out_shape=[jax.ShapeDtypeStruct((s, D_RNN), BF)] * 3,
        scratch_shapes=[pltpu.VMEM((SUBLANES, D_RNN), F32), pltpu.VMEM((1, D_RNN), F32),
                        pltpu.VMEM((ts, D_RNN), F32), pltpu.VMEM((ts, D_RNN), F32),
                        pltpu.VMEM((ts, D_RNN), F32)],
        compiler_params=_params(("arbitrary",)),
    )(proj, wa, wx, ba, bx, sp, cw, cb)


def _layernorm_fwd(x):
    mu = jnp.mean(x, axis=-1, keepdims=True)
    xc = x - mu
    rstd = lax.rsqrt(jnp.mean(xc * xc, axis=-1, keepdims=True) + EPS)
    return xc * rstd, rstd


def _sgu_mix(vn_bf, wm_ref, bsb_ref, ts):
    rows = []
    for blk in range(ts // SGU_BLOCK):
        r0 = blk * SGU_BLOCK
        cols = [
            _dot(wm_ref[g], vn_bf[r0:r0 + SGU_BLOCK, g * SGU_BLOCK:(g + 1) * SGU_BLOCK]) + bsb_ref[g]
            for g in range(SGU_GROUPS)
        ]
        rows.append(jnp.concatenate(cols, axis=1))
    return jnp.concatenate(rows, axis=0)


def _sgu_fwd_call(proj, wm, bsb, lg, lb, ts):
    s = proj.shape[0]

    def body(uv_ref, wm_ref, bsb_ref, lg_ref, lb_ref, yb_ref):
        gu = _gelu(uv_ref[:, :D_SGU])
        gv = _gelu(uv_ref[:, D_SGU:2 * D_SGU]).astype(F32)
        nh, _ = _layernorm_fwd(gv)
        vn = (nh * lg_ref[...] + lb_ref[...]).astype(BF)
        yb_ref[...] = (gu * _sgu_mix(vn, wm_ref, bsb_ref, ts)).astype(BF)

    sw = (SGU_GROUPS, SGU_BLOCK, SGU_BLOCK)
    return pl.pallas_call(
        body, name="sgu_fwd", grid=(s // ts,),
        in_specs=[_tile_spec(ts, 2 * D_RNN, 1), _full_spec(sw), _full_spec(sw),
                  _full_spec((1, D_SGU)), _full_spec((1, D_SGU))],
        out_specs=_tile_spec(ts, D_SGU),
        out_shape=jax.ShapeDtypeStruct((s, D_SGU), BF),
        compiler_params=_params(("parallel",)),
    )(proj, wm, bsb, lg, lb)


_GATE_COL0 = (2 * D_RNN + 2 * D_SGU) // 512


def _gate_specs(ts):
    return [_tile_spec(ts, 512, _GATE_COL0 + j) for j in range(4)]


def _merge_call(x, proj, ya_pre, yb_pre, w_ba, w_bb, w_out, g2, layer, ts):
    s = x.shape[0]

    def body(x_ref, ga0, ga1, gb0, gb1, ya_ref, yb_ref, wa_ref, wb_ref, wo_ref, g2_ref,
             x1_ref, yao_ref, ybo_ref, mg_ref, h2_ref):
        ya = _dot(ya_ref[...], wa_ref[...])
        yb = _dot(yb_ref[...], wb_ref[...])
        sa = jax.nn.sigmoid(jnp.concatenate([ga0[...], ga1[...]], axis=1).astype(F32))
        sb = jax.nn.sigmoid(jnp.concatenate([gb0[...], gb1[...]], axis=1).astype(F32))
        merged = (sa * ya + sb * yb).astype(BF)
        x1 = x_ref[...] + _dot(merged, wo_ref[...])
        x1_ref[...] = x1
        yao_ref[...] = ya.astype(BF)
        ybo_ref[...] = yb.astype(BF)
        mg_ref[...] = merged
        h2_ref[...] = (x1 * _rms_stats(x1) * g2_ref[...]).astype(BF)

    act = jax.ShapeDtypeStruct((s, D_MODEL), BF)
    return pl.pallas_call(
        body, name="merge_fwd", grid=(s // ts,),
        in_specs=[_tile_spec(ts, D_MODEL)] + _gate_specs(ts) + [
            _tile_spec(ts, D_RNN), _tile_spec(ts, D_SGU),
            _layer_spec(w_ba, layer), _layer_spec(w_bb, layer), _layer_spec(w_out, layer),
            _full_spec((1, D_MODEL))],
        out_specs=[_tile_spec(ts, D_MODEL)] * 5,
        out_shape=[jax.ShapeDtypeStruct((s, D_MODEL), F32), act, act, act, act],
        compiler_params=_params(("parallel",)),
    )(x, proj, proj, proj, proj, ya_pre, yb_pre, w_ba, w_bb, w_out, g2)


def _ffn_call(x1, h2, w_up, w_down, layer, ts):
    s = x1.shape[0]

    def body(x1_ref, h2_ref, wu_ref, wd_ref, x2_ref, p_ref):
        h2v = h2_ref[...]
        acc = x1_ref[...]
        for q in range(N_QUARTERS):
            p = _dot(h2v, wu_ref[q])
            p_ref[:, q * Q_FF:(q + 1) * Q_FF] = p.astype(BF)
            f = jnp.square(jnp.maximum(p, 0.0)).astype(BF)
            acc = acc + _dot(f, wd_ref[q * Q_FF:(q + 1) * Q_FF, :])
        x2_ref[...] = acc

    return pl.pallas_call(
        body, name="ffn_fwd", grid=(s // ts,),
        in_specs=[_tile_spec(ts, D_MODEL), _tile_spec(ts, D_MODEL),
                  pl.BlockSpec((None, N_QUARTERS, D_MODEL, Q_FF), lambda i: (layer, 0, 0, 0)),
                  pl.BlockSpec((None, D_FF, D_MODEL), lambda i: (layer, 0, 0))],
        out_specs=[_tile_spec(ts, D_MODEL), _tile_spec(ts, D_FF)],
        out_shape=[jax.ShapeDtypeStruct((s, D_MODEL), F32), jax.ShapeDtypeStruct((s, D_FF), BF)],
        compiler_params=_params(("parallel",)),
    )(x1, h2, w_up, w_down)


def _loss_call(x, target, gf, ts):
    s = x.shape[0]

    def body(x_ref, t_ref, g_ref, dx_ref, loss_ref, dg_ref):
        @pl.when(pl.program_id(0) == 0)
        def _():
            loss_ref[...] = jnp.zeros_like(loss_ref)
            dg_ref[...] = jnp.zeros_like(dg_ref)

        xv = x_ref[...]
        gv = g_ref[...]
        err = xv * _rms_stats(xv) * gv - t_ref[...]
        part = 0.5 * jnp.sum(jnp.mean(err * err, axis=-1, keepdims=True), axis=0, keepdims=True)
        loss_ref[...] += jnp.broadcast_to(part, loss_ref.shape)
        dx, dg = _rms_bwd(err * (1.0 / D_MODEL), xv, gv)
        dx_ref[...] = dx
        dg_ref[...] += _row_sum(dg)

    return pl.pallas_call(
        body, name="loss_head", grid=(s // ts,),
        in_specs=[_tile_spec(ts, D_MODEL), _tile_spec(ts, D_MODEL), _full_spec((1, D_MODEL))],
        out_specs=[_tile_spec(ts, D_MODEL), _full_spec((1, 128)), _full_spec((1, D_MODEL))],
        out_shape=[jax.ShapeDtypeStruct((s, D_MODEL), F32), jax.ShapeDtypeStruct((1, 128), F32),
                   jax.ShapeDtypeStruct((1, D_MODEL), F32)],
        compiler_params=_params(("arbitrary",)),
    )(x, target, gf)


def _ffn_bwd_call(dx2, p, x1, g2, w_up, w_down, layer, ts):
    s = dx2.shape[0]

    def body(dx2_ref, p_ref, x1_ref, g2_ref, wu_ref, wd_ref, dx1_ref, dp_ref, dg_ref, dx2b_ref, dx1b_ref):
        @pl.when(pl.program_id(0) == 0)
        def _():
            dg_ref[...] = jnp.zeros_like(dg_ref)

        dx2v = dx2_ref[...]
        dyb = dx2v.astype(BF)
        dx2b_ref[...] = dyb
        dh2 = jnp.zeros((ts, D_MODEL), F32)
        for q in range(N_QUARTERS):
            cols = slice(q * Q_FF, (q + 1) * Q_FF)
            df = _dot_nt(dyb, wd_ref[cols, :])
            dp = (df * (2.0 * jnp.maximum(p_ref[:, cols].astype(F32), 0.0))).astype(BF)
            dp_ref[:, cols] = dp
            dh2 = dh2 + _dot_nt(dp, wu_ref[q])
        dx, dg = _rms_bwd(dh2, x1_ref[...], g2_ref[...])
        dx1 = dx2v + dx
        dx1_ref[...] = dx1
        dx1b_ref[...] = dx1.astype(BF)
        dg_ref[...] += _row_sum(dg)

    return pl.pallas_call(
        body, name="ffn_bwd", grid=(s // ts,),
        in_specs=[_tile_spec(ts, D_MODEL), _tile_spec(ts, D_FF), _tile_spec(ts, D_MODEL),
                  _full_spec((1, D_MODEL)),
                  pl.BlockSpec((None, N_QUARTERS, D_MODEL, Q_FF), lambda i: (layer, 0, 0, 0)),
                  pl.BlockSpec((None, D_FF, D_MODEL), lambda i: (layer, 0, 0))],
        out_specs=[_tile_spec(ts, D_MODEL), _tile_spec(ts, D_FF), _full_spec((1, D_MODEL)),
                   _tile_spec(ts, D_MODEL), _tile_spec(ts, D_MODEL)],
        out_shape=[jax.ShapeDtypeStruct((s, D_MODEL), F32), jax.ShapeDtypeStruct((s, D_FF), BF),
                   jax.ShapeDtypeStruct((1, D_MODEL), F32),
                   jax.ShapeDtypeStruct((s, D_MODEL), BF), jax.ShapeDtypeStruct((s, D_MODEL), BF)],
        compiler_params=_params(("arbitrary",)),
    )(dx2, p, x1, g2, w_up, w_down)


def _merge_bwd_call(dx1, proj, ya, yb, w_ba, w_bb, w_out, layer, ts, after=None):
    s = dx1.shape[0]

    def body(dx1_ref, ga0, ga1, gb0, gb1, ya_ref, yb_ref, wa_ref, wb_ref, wo_ref, *rest):
        dya_ref, dyb_ref, dgate_ref, dyap_ref, dybp_ref = rest[-5:]
        dm = _dot_nt(dx1_ref[...].astype(BF), wo_ref[...])
        sa = jax.nn.sigmoid(jnp.concatenate([ga0[...], ga1[...]], axis=1).astype(F32))
        sb = jax.nn.sigmoid(jnp.concatenate([gb0[...], gb1[...]], axis=1).astype(F32))
        dya = (dm * sa).astype(BF)
        dyb = (dm * sb).astype(BF)
        dya_ref[...] = dya
        dyb_ref[...] = dyb
        dgate_ref[:, :D_MODEL] = (dm * ya_ref[...].astype(F32) * sa * (1.0 - sa)).astype(BF)
        dgate_ref[:, D_MODEL:] = (dm * yb_ref[...].astype(F32) * sb * (1.0 - sb)).astype(BF)
        dyap_ref[...] = _dot_nt(dya, wa_ref[...]).astype(BF)
        dybp_ref[...] = _dot_nt(dyb, wb_ref[...]).astype(BF)

    act = jax.ShapeDtypeStruct((s, D_MODEL), BF)
    return pl.pallas_call(
        body, name="merge_bwd", grid=(s // ts,),
        in_specs=[_tile_spec(ts, D_MODEL)] + _gate_specs(ts) + [
            _tile_spec(ts, D_MODEL), _tile_spec(ts, D_MODEL),
            _layer_spec(w_ba, layer), _layer_spec(w_bb, layer), _layer_spec(w_out, layer)]
        + ([] if after is None else [pl.BlockSpec(memory_space=pl.ANY)]),
        out_specs=[_tile_spec(ts, D_MODEL), _tile_spec(ts, D_MODEL), _tile_spec(ts, 2 * D_MODEL),
                   _tile_spec(ts, D_RNN), _tile_spec(ts, D_SGU)],
        out_shape=[act, act, jax.ShapeDtypeStruct((s, 2 * D_MODEL), BF),
                   jax.ShapeDtypeStruct((s, D_RNN), BF), jax.ShapeDtypeStruct((s, D_SGU), BF)],
        compiler_params=_params(("parallel",)),
    )(dx1, proj, proj, proj, proj, ya, yb, w_ba, w_bb, w_out, *([] if after is None else [after]))


def _sgu_bwd_call(dyb_pre, proj, wm, bsb, mask, lg, lb, ts):
    s = proj.shape[0]

    def body(dy_ref, uv_ref, wm_ref, bsb_ref, mask_ref, lg_ref, lb_ref,
             duv_ref, dws_ref, dbs_ref, dlg_ref, dlb_ref, dm_sc):
        step = pl.program_id(0)

        @pl.when(step == 0)
        def _():
            dws_ref[...] = jnp.zeros_like(dws_ref)
            dlg_ref[...] = jnp.zeros_like(dlg_ref)
            dlb_ref[...] = jnp.zeros_like(dlb_ref)
            dm_sc[...] = jnp.zeros_like(dm_sc)

        gu, dgu_du = _gelu_and_grad(uv_ref[:, :D_SGU])
        gv, dgv_dv = _gelu_and_grad(uv_ref[:, D_SGU:2 * D_SGU])
        nh, rstd = _layernorm_fwd(gv.astype(F32))
        lgv = lg_ref[...]
        vn = (nh * lgv + lb_ref[...]).astype(BF)
        dy = dy_ref[...].astype(F32)
        du = dy * _sgu_mix(vn, wm_ref, bsb_ref, ts) * dgu_du
        dmix = dy * gu
        dmix_bf = dmix.astype(BF)
        dm_acc = dm_sc[...]
        rows = []
        for blk in range(ts // SGU_BLOCK):
            r0 = blk * SGU_BLOCK
            dm_acc = dm_acc + dmix[r0:r0 + SGU_BLOCK, :]
            cols = []
            for g in range(SGU_GROUPS):
                c0 = g * SGU_BLOCK
                dmg = dmix_bf[r0:r0 + SGU_BLOCK, c0:c0 + SGU_BLOCK]
                cols.append(_dot_tn(wm_ref[g], dmg))
                dws_ref[g] += mask_ref[...] * _dot_nt(dmg, vn[r0:r0 + SGU_BLOCK, c0:c0 + SGU_BLOCK])
            rows.append(jnp.concatenate(cols, axis=1))
        dm_sc[...] = dm_acc
        dvn = jnp.concatenate(rows, axis=0)
        dlg_ref[...] += _row_sum(dvn * nh)
        dlb_ref[...] += _row_sum(dvn)
        dnh = dvn * lgv
        dgv = rstd * (dnh - jnp.mean(dnh, axis=-1, keepdims=True)
                      - nh * jnp.mean(dnh * nh, axis=-1, keepdims=True))
        duv_ref[:, :D_SGU] = du.astype(BF)
        duv_ref[:, D_SGU:] = (dgv * dgv_dv).astype(BF)

        @pl.when(step == pl.num_programs(0) - 1)
        def _():
            for g in range(SGU_GROUPS):
                dbs_ref[:, g:g + 1] = jnp.sum(
                    dm_acc[:, g * SGU_BLOCK:(g + 1) * SGU_BLOCK], axis=1, keepdims=True)

    sw = (SGU_GROUPS, SGU_BLOCK, SGU_BLOCK)
    return pl.pallas_call(
        body, name="sgu_bwd", grid=(s // ts,),
        in_specs=[_tile_spec(ts, D_SGU), _tile_spec(ts, 2 * D_RNN, 1), _full_spec(sw), _full_spec(sw),
                  _full_spec((SGU_BLOCK, SGU_BLOCK)), _full_spec((1, D_SGU)), _full_spec((1, D_SGU))],
        out_specs=[_tile_spec(ts, 2 * D_SGU), _full_spec(sw), _full_spec((SGU_BLOCK, SGU_GROUPS)),
                   _full_spec((1, D_SGU)), _full_spec((1, D_SGU))],
        out_shape=[jax.ShapeDtypeStruct((s, 2 * D_SGU), BF), jax.ShapeDtypeStruct(sw, F32),
                   jax.ShapeDtypeStruct((SGU_BLOCK, SGU_GROUPS), F32),
                   jax.ShapeDtypeStruct((1, D_SGU), F32), jax.ShapeDtypeStruct((1, D_SGU), F32)],
        scratch_shapes=[pltpu.VMEM((SGU_BLOCK, D_SGU), F32)],
        compiler_params=_params(("arbitrary",)),
    )(dyb_pre, proj, wm, bsb, mask, lg, lb)


_ROW_DBA, _ROW_DBX, _ROW_DSP, _ROW_DCB, _ROW_DCW = 0, 1, 2, 3, 4
_PREV_ROWS = 16


def _rnn_bwd_call(dya_pre, proj, xr_saved, hr, wa, wx, ba, bx, sp, cw, ts):
    s = proj.shape[0]
    nt = s // ts
    per = ts // _PREV_ROWS

    def tile(i):
        return nt - 1 - i

    def prev(i):
        return jnp.maximum(tile(i) * per - 1, 0)

    def body(dy_ref, xg_ref, xr_ref, hr_ref, hrp_ref, wa_ref, wx_ref, ba_ref, bx_ref, sp_ref,
             cw_ref, dxg_ref, dwa_ref, dwx_ref, vec_ref,
             lam_carry, a_first, dxr_head, al_sc, bl_sc, lam_sc):
        step = pl.program_id(0)

        @pl.when(step == 0)
        def _():
            dwa_ref[...] = jnp.zeros_like(dwa_ref)
            dwx_ref[...] = jnp.zeros_like(dwx_ref)
            vec_ref[...] = jnp.zeros_like(vec_ref)
            lam_carry[...] = jnp.zeros_like(lam_carry)
            a_first[...] = jnp.zeros_like(a_first)
            dxr_head[...] = jnp.zeros_like(dxr_head)

        has_prev = (step < nt - 1).astype(F32)
        x = xg_ref[:, :D_RNN].astype(F32)
        g = xg_ref[:, D_RNN:]
        h_tail =hrp_ref[_PREV_ROWS - SUBLANES:, :].astype(F32) * has_prev
        xr = xr_ref[...].astype(F32)
        r, i, a, nrm, inv_nrm = _lru_gates(xr, wa_ref, wx_ref, ba_ref, bx_ref, sp_ref)
        h = hr_ref[...].astype(F32)
        dy = dy_ref[...].astype(F32)
        gg, dgg = _gelu_and_grad(g)

        coef = _shift_up(a, jnp.broadcast_to(a_first[...], (SUBLANES, D_RNN)), 1)
        lam_carry[...] = _linear_scan(coef, dy * gg, lam_carry[...], al_sc, bl_sc, lam_sc, True)
        a_first[...] = a[0:1, :]
        lam = lam_sc[...]

        da = lam * _shift_down(h, h_tail, 1)
        dnrm = lam * (i * xr)
        di = lam * nrm * xr
        dlog_a = da * a - dnrm * (a * a) * inv_nrm
        spv = sp_ref[...]
        dza = (dlog_a * (-LRU_C * spv)) * (r * (1.0 - r))
        dzx = di * (i * (1.0 - i))
        vec_ref[_ROW_DSP:_ROW_DSP + 1, :] += _row_sum(dlog_a * (-LRU_C * r))
        vec_ref[_ROW_DBA:_ROW_DBA + 1, :] += _row_sum(dza)
        vec_ref[_ROW_DBX:_ROW_DBX + 1, :] += _row_sum(dzx)
        xb = xr.astype(BF)
        dza_bf = dza.astype(BF)
        dzx_bf = dzx.astype(BF)
        for grp in range(N_LRU_GROUPS):
            cols = slice(grp * LRU_GROUP, (grp + 1) * LRU_GROUP)
            dwa_ref[grp] += _dot_tn(xb[:, cols], dza_bf[:, cols])
            dwx_ref[grp] += _dot_tn(xb[:, cols], dzx_bf[:, cols])
        dxr = (lam * nrm * i + _group_dot(dza_bf, wa_ref, _dot_nt) + _group_dot(dzx_bf, wx_ref, _dot_nt))

        vec_ref[_ROW_DCB:_ROW_DCB + 1, :] += _row_sum(dxr)
        head = dxr_head[...]
        dx = cw_ref[CONV_WIDTH - 1:CONV_WIDTH, :] * dxr
        vec_ref[_ROW_DCW + 3:_ROW_DCW + 4, :] += _row_sum(dxr * x)
        for sft in range(1, CONV_WIDTH):
            k = CONV_WIDTH - 1 - sft
            ahead = _shift_up(dxr, head, sft)
            dx = dx + cw_ref[k:k + 1, :] * ahead
            vec_ref[_ROW_DCW + k:_ROW_DCW + k + 1, :] += _row_sum(ahead * x)
        dxr_head[...] = dxr[0:SUBLANES, :]
        dxg_ref[:, :D_RNN] = dx.astype(BF)
        dxg_ref[:, D_RNN:] = (dy * h * dgg).astype(BF)

    gw = (N_LRU_GROUPS, LRU_GROUP, LRU_GROUP)
    rev = lambda width: pl.BlockSpec((ts, width), lambda i: (tile(i), 0))
    return pl.pallas_call(
        body, name="rnn_bwd", grid=(nt,),
        in_specs=[rev(D_RNN), rev(2 * D_RNN), rev(D_RNN), rev(D_RNN),
                  pl.BlockSpec((_PREV_ROWS, D_RNN), lambda i: (prev(i), 0)),
                  _full_spec(gw), _full_spec(gw),
                  _full_spec((1, D_RNN)), _full_spec((1, D_RNN)), _full_spec((1, D_RNN)),
                  _full_spec((CONV_WIDTH, D_RNN))],
        out_specs=[rev(2 * D_RNN), _full_spec(gw), _full_spec(gw), _full_spec((SUBLANES, D_RNN))],
        out_shape=[jax.ShapeDtypeStruct((s, 2 * D_RNN), BF), jax.ShapeDtypeStruct(gw, F32),
                   jax.ShapeDtypeStruct(gw, F32), jax.ShapeDtypeStruct((SUBLANES, D_RNN), F32)],
        scratch_shapes=[pltpu.VMEM((1, D_RNN), F32), pltpu.VMEM((1, D_RNN), F32),
                        pltpu.VMEM((SUBLANES, D_RNN), F32),
                        pltpu.VMEM((ts, D_RNN), F32), pltpu.VMEM((ts, D_RNN), F32),
                        pltpu.VMEM((ts, D_RNN), F32)],
        compiler_params=_params(("arbitrary",)),
    )(dya_pre, proj, xr_saved, hr, hr, wa, wx, ba, bx, sp, cw)


def _inproj_bwd_call(dxg, duv, dgate, dx1, x, g1, w_in, layer, ts):
    s = x.shape[0]

    def body(dxg_ref, duv_ref, dgt_ref, dx1_ref, x_ref, g_ref, w_ref, dx_ref, dproj_ref, dg_ref):
        @pl.when(pl.program_id(0) == 0)
        def _():
            dg_ref[...] = jnp.zeros_like(dg_ref)

        dproj = jnp.concatenate([dxg_ref[...], duv_ref[...], dgt_ref[...]], axis=1)
        dproj_ref[...] = dproj
        dh = jnp.zeros((ts, D_MODEL), F32)
        for q in range(N_QUARTERS):
            dh = dh + _dot_nt(dproj[:, q * Q_IN:(q + 1) * Q_IN], w_ref[q])
        dx, dg = _rms_bwd(dh, x_ref[...], g_ref[...])
        dx_ref[...] = dx1_ref[...] + dx
        dg_ref[...] += _row_sum(dg)

    return pl.pallas_call(
        body, name="inproj_bwd", grid=(s // ts,),
        in_specs=[_tile_spec(ts, 2 * D_RNN), _tile_spec(ts, 2 * D_SGU), _tile_spec(ts, 2 * D_MODEL),
                  _tile_spec(ts, D_MODEL), _tile_spec(ts, D_MODEL), _full_spec((1, D_MODEL)),
                  pl.BlockSpec((None, N_QUARTERS, D_MODEL, Q_IN), lambda i: (layer, 0, 0, 0))],
        out_specs=[_tile_spec(ts, D_MODEL), _tile_spec(ts, D_IN), _full_spec((1, D_MODEL))],
        out_shape=[jax.ShapeDtypeStruct((s, D_MODEL), F32), jax.ShapeDtypeStruct((s, D_IN), BF),
                   jax.ShapeDtypeStruct((1, D_MODEL), F32)],
        compiler_params=_params(("arbitrary",)),
    )(dxg, duv, dgate, dx1, x, g1, w_in)


def _relu_sq(p):
    return jnp.square(jnp.maximum(p, 0))


def _wgrad_call(a, b, core, tm, tn, tk, col_blocked, name, a_fn=None):
    s, m = a.shape
    n = b.shape[1]
    r, cols = (m, n // N_QUARTERS) if col_blocked else (m // N_QUARTERS, n)
    r2 = r // 2
    per_tile = tm // r
    steps = s // tk

    def body(core_ref, a_ref, b_ref, keep_ref, send_ref, *acc):
        av = a_ref[...]
        if a_fn is not None:
            av = a_fn(av)
        prod = _dot_tn(av.astype(BF), b_ref[...].astype(BF))

        def emit(total):
            for h in range(2):
                @pl.when(core_ref[0] == h)
                def _():
                    for q in range(per_tile):
                        keep_ref[q] = total[q * r + h * r2:q * r + (h + 1) * r2]
                        send_ref[q] = total[q * r + (1 - h) * r2:q * r + (2 - h) * r2].astype(BF)

        if steps == 1:
            emit(prod)
        else:
            acc_ref, = acc
            step = pl.program_id(2)

            @pl.when(step == 0)
            def _():
                acc_ref[...] = prod

            @pl.when(jnp.logical_and(step > 0, step < steps - 1))
            def _():
                acc_ref[...] += prod

            @pl.when(step == steps - 1)
            def _():
                emit(acc_ref[...] + prod)

    if col_blocked:
        per_q = cols // tn
        out_spec = pl.BlockSpec((1, r2, tn), lambda i, j, k, c: (j // per_q, 0, j % per_q))
    else:
        out_spec = pl.BlockSpec((per_tile, r2, tn), lambda i, j, k, c: (i, 0, j))
    return pl.pallas_call(
        body, name=name,
        out_shape=[jax.ShapeDtypeStruct((N_QUARTERS, r2, cols), F32),
                   jax.ShapeDtypeStruct((N_QUARTERS, r2, cols), BF)],
        grid_spec=pltpu.PrefetchScalarGridSpec(
            num_scalar_prefetch=1, grid=(m // tm, n // tn, steps),
            in_specs=[pl.BlockSpec((tk, tm), lambda i, j, k, c: (k, i)),
                      pl.BlockSpec((tk, tn), lambda i, j, k, c: (k, j))],
            out_specs=[out_spec, out_spec],
            scratch_shapes=[] if steps == 1 else [pltpu.VMEM((tm, tn), F32)]),
        compiler_params=_params(("parallel", "parallel", "arbitrary")),
    )(core, a, b)


BIG = ("w_in", "w_up", "w_down", "w_branch_a", "w_branch_b", "w_out")


def _block_diag(w):
    w4 = w.reshape(N_LRU_GROUPS, HEADS_PER_GROUP, RNN_HEAD_DIM, RNN_HEAD_DIM)
    eye = jnp.eye(HEADS_PER_GROUP, dtype=w.dtype)
    return jnp.einsum("gjio,jk->gjiko", w4, eye).reshape(N_LRU_GROUPS, LRU_GROUP, LRU_GROUP)


def _block_diag_extract(d):
    d5 = d.reshape(N_LRU_GROUPS, HEADS_PER_GROUP, RNN_HEAD_DIM, HEADS_PER_GROUP, RNN_HEAD_DIM)
    blocks = [d5[:, j, :, j, :] for j in range(HEADS_PER_GROUP)]
    return jnp.stack(blocks, axis=1).reshape(RNN_HEADS, RNN_HEAD_DIM, RNN_HEAD_DIM)


def _sgu_mask():
    chunk = jnp.arange(SGU_BLOCK) // CHUNK
    return (chunk[:, None] >= chunk[None, :]).astype(F32)


def _layer_small(sm, l, core):
    row = lambda v: v.reshape(1, -1)
    return dict(
        core=core,
        g1=row(sm["norm_mix_g"][l]), g2=row(sm["norm_ffn_g"][l]),
        wa=_block_diag(sm["lru_w_a"][l]).astype(BF), wx=_block_diag(sm["lru_w_x"][l]).astype(BF),
        ba=row(sm["lru_b_a"][l]), bx=row(sm["lru_b_x"][l]),
        sp=row(jax.nn.softplus(-sm["lru_lambda"][l])),
        cw=sm["conv_w"][l] if "conv_w" in sm else None, cb=row(sm["conv_b"][l]),
        wm=(sm["sgu_w_s"][l] * _sgu_mask()).astype(BF),
        bsb=jnp.broadcast_to(sm["sgu_b_s"][l][:, :, None], (SGU_GROUPS, SGU_BLOCK, SGU_BLOCK)),
        lg=row(sm["sgu_ln_g"][l]), lb=row(sm["sgu_ln_b"][l]),
    )


def _layer_fwd_mix(x, big, p, ts, h=None, before_sgu=None, proj=None):
    if h is None:
        h = _norm_call(x, p["g1"], ts)
    if proj is None:
        proj = _inproj_call(h, big["w_in"], 0, 2 * ts)
    xr, hr, ya_pre = _rnn_fwd_call(proj, p["wa"], p["wx"], p["ba"], p["bx"], p["sp"], p["cw"], p["cb"], ts)
    lg = p["lg"] if before_sgu is None else p["lg"] + before_sgu(ya_pre)
    yb_pre = _sgu_fwd_call(proj, p["wm"], p["bsb"], lg, p["lb"], ts)
    return dict(p=p, x=x, h=h, proj=proj, xr=xr, hr=hr, ya_pre=ya_pre, yb_pre=yb_pre)


def _layer_fwd_out(sv, big, ts):
    x1, ya, yb, merged, h2 = _merge_call(sv["x"], sv["proj"], sv["ya_pre"], sv["yb_pre"], big["w_branch_a"],
                                         big["w_branch_b"], big["w_out"], sv["p"]["g2"], 0, ts)
    x2, pre = _ffn_call(x1, h2, big["w_up"], big["w_down"], 0, ts)
    sv.update(x1=x1, ya=ya, yb=yb, merged=merged, h2=h2, pre=pre)
    return x2


def _layer_bwd_ffn(dx, sv, big, ts):
    p = sv["p"]
    dx1, dpre, dg2, dx_bf, sv["dx1_bf"] = _ffn_bwd_call(dx, sv["pre"], sv["x1"], p["g2"], big["w_up"],
                                                       big["w_down"], 0, ts)
    tk = dx.shape[0]
    gb = dict(
        w_down=_wgrad_call(sv["pre"], dx_bf, p["core"], Q_FF, D_MODEL, tk, False, "wgrad_down", a_fn=_relu_sq),
        w_up=_wgrad_call(sv["h2"], dpre, p["core"], D_MODEL, Q_FF, tk, True, "wgrad_up"))
    return dx1, gb, dict(norm_ffn_g=dg2[0])


def _layer_bwd_merge(dx1, sv, big, ts, after=None):
    tk = dx1.shape[0]
    core = sv["p"]["core"]
    dya, dyb, dgate, dya_pre, dyb_pre = _merge_bwd_call(
        dx1, sv["proj"], sv["ya"], sv["yb"], big["w_branch_a"], big["w_branch_b"], big["w_out"], 0, ts, after)
    gb = dict(
        w_out=_wgrad_call(sv["merged"], sv["dx1_bf"], core, D_MODEL, D_MODEL, tk, False, "wgrad_out"),
        w_branch_a=_wgrad_call(sv["ya_pre"], dya, core, D_RNN, D_MODEL // 2, tk, False, "wgrad_branch_a"),
        w_branch_b=_wgrad_call(sv["yb_pre"], dyb, core, D_SGU, D_MODEL, tk, False, "wgrad_branch_b"))
    return (dgate, dya_pre, dyb_pre), gb


def _layer_bwd_branches(dx1, merge_out, sv, big, lam, ts, after_sgu=None):
    p = sv["p"]
    tk = dx1.shape[0]
    dgate, dya_pre, dyb_pre = merge_out
    gb = {}
    duv, dws, dbs, dlg, dlb = _sgu_bwd_call(dyb_pre, sv["proj"], p["wm"], p["bsb"], _sgu_mask(), p["lg"], p["lb"],
                                            ts)
    ba = p["ba"] if after_sgu is None else p["ba"] + after_sgu(duv)
    dxg, dwa, dwx, vec = _rnn_bwd_call(dya_pre, sv["proj"], sv["xr"], sv["hr"], p["wa"], p["wx"], ba, p["bx"],
                                       p["sp"], p["cw"], ts // 2)
    dx, dproj, dg1 = _inproj_bwd_call(dxg, duv, dgate, dx1, sv["x"], p["g1"], big["w_in"], 0, ts)
    gb["w_in"] = _wgrad_call(sv["h"], dproj, p["core"], D_MODEL, Q_IN, tk // 2, True, "wgrad_in")
    gs = dict(
        norm_mix_g=dg1[0], conv_w=vec[_ROW_DCW:_ROW_DCW + CONV_WIDTH], conv_b=vec[_ROW_DCB],
        lru_w_a=_block_diag_extract(dwa), lru_w_x=_block_diag_extract(dwx),
        lru_b_a=vec[_ROW_DBA].reshape(RNN_HEADS, RNN_HEAD_DIM), lru_b_x=vec[_ROW_DBX].reshape(RNN_HEADS, RNN_HEAD_DIM),
        lru_lambda=-vec[_ROW_DSP] * jax.nn.sigmoid(-lam),
        sgu_ln_g=dlg[0], sgu_ln_b=dlb[0], sgu_w_s=dws, sgu_b_s=dbs.T)
    return dx, gb, gs


def _local_step(x, target, big, sm, ts):
    saved = []
    core = jnp.zeros((1,), jnp.int32)
    for l in range(DEPTH):
        sv = _layer_fwd_mix(x, big[l], _layer_small(sm, l, core), ts)
        x = _layer_fwd_out(sv, big[l], ts)
        saved.append(sv)
    dx, loss, dgf = _loss_call(x, target, sm["final_norm_g"].reshape(1, -1), ts)
    gb, gs = [None] * DEPTH, [None] * DEPTH
    for l in reversed(range(DEPTH)):
        dx1, gb_ffn, gs_ffn = _layer_bwd_ffn(dx, saved[l], big[l], ts)
        merge_out, gb_merge = _layer_bwd_merge(dx1, saved[l], big[l], ts)
        dx, gb_mix, gs_mix = _layer_bwd_branches(dx1, merge_out, saved[l], big[l], sm["lru_lambda"][l], ts)
        gb[l] = {**gb_ffn, **gb_merge, **gb_mix}
        gs[l] = {**gs_ffn, **gs_mix}
    gs = {k: jnp.stack([g[k] for g in gs]) for k in gs[0]}
    gs["final_norm_g"] = dgf[0]
    return loss, dx, gb, gs


EW_VMEM_BYTES = 24 * 1024 * 1024


def _row_block(rows, cols, bytes_per_elem):
    for br in range(min(rows, EW_VMEM_BYTES // (2 * bytes_per_elem * cols)), 0, -1):
        if rows % br == 0 and br % 16 == 0:
            return br
    return rows


def _ew_call(fn, name, operands, outputs, slabs=1, sel=None, into=None, after=None):
    if into is not None and not isinstance(into, (list, tuple)):
        into = [into]
    rows, cols = outputs[0][0].shape[2:]
    br = _row_block(rows, cols, sum(jnp.dtype(a.dtype).itemsize for a, _ in operands + outputs))
    n_in = len(operands)

    def pick(tok, g, s):
        if callable(tok):
            return tok(g, s)
        if tok == "g":
            return g
        if isinstance(tok, tuple):
            return s[tok[1]]
        return tok

    def spec(idx):
        return pl.BlockSpec((None, None, br, cols),
                            lambda g, i, s, idx=idx: (pick(idx[0], g, s), pick(idx[1], g, s), i, 0))

    if sel is None:
        sel = jnp.zeros((1,), jnp.int32)
    in_specs = [spec(idx) for _, idx in operands]
    arrays = [a for a, _ in operands]
    aliases = {}
    for j, buf in enumerate(into or ()):
        in_specs.append(pl.BlockSpec(memory_space=pl.ANY))
        arrays.append(buf)
        aliases[1 + n_in + j] = j
    if after is not None:
        in_specs.append(pl.BlockSpec(memory_space=pl.ANY))
        arrays.append(after)

    def body(sel_ref, *refs):
        outs = fn(*[r[...] for r in refs[:n_in]])
        for o_ref, o in zip(refs[len(arrays):], outs):
            o_ref[...] = o.astype(o_ref.dtype)

    return pl.pallas_call(
        body, name=name, out_shape=[s for s, _ in outputs],
        grid_spec=pltpu.PrefetchScalarGridSpec(
            num_scalar_prefetch=1, grid=(slabs, rows // br),
            in_specs=in_specs,
            out_specs=[spec(idx) for _, idx in outputs]),
        input_output_aliases=aliases,
        compiler_params=_params(("parallel", "parallel")),
    )(sel, *arrays)


def _as4(a):
    return a.reshape((1,) * (4 - a.ndim) + a.shape)


def _adamw(w, g, m, v):
    m = ADAM_B1 * m + (1.0 - ADAM_B1) * g
    v = ADAM_B2 * v + (1.0 - ADAM_B2) * jnp.square(g)
    m_hat = m / (1.0 - ADAM_B1 ** ADAM_STEP)
    v_hat = v / (1.0 - ADAM_B2 ** ADAM_STEP)
    delta = -ADAM_LR * (m_hat / (jnp.sqrt(v_hat) + ADAM_EPS) + ADAM_WD * w)
    return delta, m, v


def _small_adamw_call(ws, gs, ms, vs):
    n = len(ws)

    def body(*refs):
        for k in range(n):
            w, g, m, v = (refs[j * n + k][...] for j in range(4))
            outs = _adamw(w, g, m, v)
            for j in range(3):
                refs[(4 + j) * n + k][...] = outs[j]

    shapes = [jax.ShapeDtypeStruct(w.shape, F32) for w in ws]
    outs = pl.pallas_call(
        body, name="adamw_small", out_shape=shapes * 3,
        in_specs=[pl.BlockSpec(memory_space=pltpu.VMEM)] * (4 * n),
        out_specs=[pl.BlockSpec(memory_space=pltpu.VMEM)] * (3 * n),
        compiler_params=_params(),
    )(*ws, *gs, *ms, *vs)
    return outs[:n], outs[n:2 * n], outs[2 * n:]


ANY = pl.BlockSpec(memory_space=pl.ANY)


def _place():
    x, y, c = lax.axis_index("x"), lax.axis_index("y"), lax.axis_index("c")
    chips = [(1 - x, y), (x, 1 - y), (1 - x, 1 - y)]
    return x, y, c, chips


def _remote(src, dst, send_sem, recv_sem, to):
    return pltpu.make_async_remote_copy(src_ref=src, dst_ref=dst, send_sem=send_sem, recv_sem=recv_sem,
                                        device_id=to, device_id_type=MESH)


def _sibling_send_call(items):
    n = len(items)

    def body(*refs):
        src, out = refs[:n], refs[n:2 * n]
        send_sems, recv_sems = refs[2 * n:]
        x, y, c, _ = _place()
        copies = [_remote(src[w], out[w], send_sems.at[w], recv_sems.at[w], (x, y, 1 - c)) for w in range(n)]
        for cp in copies:
            cp.start()
        for cp in copies:
            cp.wait()

    return pl.pallas_call(
        body, name="grads_to_sibling",
        out_shape=[jax.ShapeDtypeStruct(a.shape, a.dtype) for a in items],
        in_specs=[ANY] * n, out_specs=[ANY] * n,
        scratch_shapes=[pltpu.SemaphoreType.DMA((n,)), pltpu.SemaphoreType.DMA((n,))],
        compiler_params=_params(vmem=False, has_side_effects=True),
    )(*items)


def _sibling_inplace_call(name, bufs, slabs, n_pairs):
    n = len(bufs)

    def body(*refs):
        out = refs[n:2 * n]
        send_sems, recv_sems = refs[2 * n:]
        x, y, c, _ = _place()
        sibling = (x, y, 1 - c)
        pairs = [pair for w, ref in enumerate(out) for pair in slabs(ref, c, w)]
        sends = [_remote(s, s, send_sems.at[k], recv_sems.at[k], sibling) for k, (s, _) in enumerate(pairs)]
        for cp in sends:
            cp.start()
        for k, (_, r) in enumerate(pairs):
            _remote(r, r, send_sems.at[k], recv_sems.at[k], sibling).wait_recv()
        for cp in sends:
            cp.wait_send()

    return pl.pallas_call(
        body, name=name,
        out_shape=[jax.ShapeDtypeStruct(a.shape, a.dtype) for a in bufs],
        in_specs=[ANY] * n, out_specs=[ANY] * n,
        input_output_aliases={w: w for w in range(n)},
        scratch_shapes=[pltpu.SemaphoreType.DMA((n_pairs,)), pltpu.SemaphoreType.DMA((n_pairs,))],
        compiler_params=_params(vmem=False, has_side_effects=True),
    )(*bufs)


HBM_SPEC = pl.BlockSpec(memory_space=pltpu.HBM)
SEM_SPEC = pl.BlockSpec(memory_space=pltpu.SEMAPHORE)
DATAFLOW_EFFECT = pltpu.SideEffectType.DATAFLOW_SIDE_EFFECTING


def _exchange_start(name, bufs, copies, n_copies, after):
    n = len(bufs)

    def body(*refs):
        ins, send_sems, recv_sems, token = refs[:n], refs[n + 1], refs[n + 2], refs[-1]
        for k, (src, dst, to) in enumerate(copies(ins)):
            _remote(src, dst, send_sems.at[k], recv_sems.at[k], to).start()
        token[...] = jnp.zeros_like(token)

    outs = pl.pallas_call(
        body, name=name,
        out_shape=(pltpu.SemaphoreType.DMA((n_copies,)), pltpu.SemaphoreType.DMA((n_copies,)),
                   *[pltpu.HBM(b.shape, b.dtype) for b in bufs], jax.ShapeDtypeStruct((SUBLANES, 128), F32)),
        in_specs=[HBM_SPEC] * n + [ANY],
        out_specs=(SEM_SPEC, SEM_SPEC, *[HBM_SPEC] * n, pl.BlockSpec(memory_space=pltpu.VMEM)),
        input_output_aliases={w: w + 2 for w in range(n)},
        compiler_params=pltpu.CompilerParams(has_side_effects=DATAFLOW_EFFECT),
    )(*[pltpu.with_memory_space_constraint(b, pltpu.HBM) for b in bufs], after)
    return outs[0], outs[1], list(outs[2:2 + n]), outs[-1]


def _exchange_wait(name, send_sems, recv_sems, bufs, copies, after):
    n = len(bufs)

    def body(*refs):
        ins, send_sems, recv_sems = refs[:n], refs[n], refs[n + 1]
        for k, (src, dst, to) in enumerate(copies(ins)):
            cp = _remote(src, dst, send_sems.at[k], recv_sems.at[k], to)
            cp.wait_send()
            cp.wait_recv()

    return pl.pallas_call(
        body, name=name,
        out_shape=[pltpu.HBM(b.shape, b.dtype) for b in bufs],
        in_specs=[HBM_SPEC] * n + [SEM_SPEC, SEM_SPEC, ANY],
        out_specs=[HBM_SPEC] * n,
        input_output_aliases={w: w for w in range(n)},
        compiler_params=pltpu.CompilerParams(has_side_effects=DATAFLOW_EFFECT),
    )(*bufs, send_sems, recv_sems, after)


def _gather_copies(refs):
    x, y, c, chips = _place()
    mine = 2 * (2 * x + y) + c
    return [(ref.at[mine], ref.at[mine], (qx, qy, c)) for ref in refs for qx, qy in chips]


def _forward_copies(refs):
    x, y, c, chips = _place()
    return [(ref.at[2 * (2 * qx + qy) + c], ref.at[2 * (2 * qx + qy) + c], (x, y, 1 - c))
            for ref in refs for qx, qy in chips]


def _gather_forward_slabs(ref, c, w):
    x, y, _, chips = _place()
    return [(ref.at[2 * (2 * qx + qy) + c], ref.at[2 * (2 * qx + qy) + 1 - c]) for qx, qy in chips]


def _device_peers():
    x, y, c, _ = _place()
    return 4 * x + 2 * y + c, [(k, (x ^ ((k >> 2) & 1), y ^ ((k >> 1) & 1), c ^ (k & 1))) for k in range(1, 8)]


def _small_scatter_copies(refs):
    me, peers = _device_peers()
    return [(refs[0].at[me ^ k], refs[1].at[me], to) for k, to in peers]


def _small_spread_copies(refs):
    me, peers = _device_peers()
    return [(refs[0].at[me], refs[0].at[me], to) for _, to in peers]


def _sibling_copies(refs):
    n = len(refs) // 2
    x, y, c, _ = _place()
    return [(refs[w], refs[n + w], (x, y, 1 - c)) for w in range(n)]


def _owner_copies(refs):
    n = len(refs) // 2
    x, y, c, chips = _place()
    return [(refs[w].at[2 * qx + qy], refs[n + w].at[j], (qx, qy, c))
            for w in range(n) for j, (qx, qy) in enumerate(chips)]


N_DEVICES = 8
SMALL_ROWS = 616


SMALL = ("norm_mix_g", "conv_w", "conv_b", "lru_w_a", "lru_b_a", "lru_w_x", "lru_b_x", "lru_lambda",
         "sgu_ln_g", "sgu_ln_b", "sgu_w_s", "sgu_b_s", "norm_ffn_g", "final_norm_g")
WEIGHTS = ("norm_mix_g", "w_in", "conv_w", "conv_b", "lru_w_a", "lru_b_a", "lru_w_x", "lru_b_x", "lru_lambda",
           "sgu_ln_g", "sgu_ln_b", "sgu_w_s", "sgu_b_s", "w_branch_a", "w_branch_b", "w_out", "norm_ffn_g",
           "w_up", "w_down", "final_norm_g")
PACK_ALIGN = SUBLANES * 128


PACKED = SMALL + ("loss",)


def _pack_small(gs):
    parts = []
    for k in PACKED:
        flat = gs[k].reshape(-1)
        parts.append(jnp.pad(flat, (0, -flat.size % PACK_ALIGN)))
    flat = jnp.concatenate(parts)
    flat = jnp.pad(flat, (0, N_DEVICES * SMALL_ROWS * 128 - flat.size))
    return flat.reshape(N_DEVICES, SMALL_ROWS, 128)


def _unpack_small(buf, like):
    flat = buf.reshape(-1)
    out, off = {}, 0
    for k in PACKED:
        size = like[k].size
        out[k] = flat[off:off + size].reshape(like[k].shape)
        off += size + (-size % PACK_ALIGN)
    return out


def _as_rows(a):
    return a.reshape(-1, a.shape[-1])


def kernel(x, norm_mix_g, w_in, conv_w, conv_b, lru_w_a, lru_b_a, lru_w_x, lru_b_x, lru_lambda, sgu_ln_g, sgu_ln_b, sgu_w_s, sgu_b_s, w_branch_a, w_branch_b, w_out, norm_ffn_g, w_up, w_down, final_norm_g, loss_target, m_norm_mix_g, m_w_in, m_conv_w, m_conv_b, m_lru_w_a, m_lru_b_a, m_lru_w_x, m_lru_b_x, m_lru_lambda, m_sgu_ln_g, m_sgu_ln_b, m_sgu_w_s, m_sgu_b_s, m_w_branch_a, m_w_branch_b, m_w_out, m_norm_ffn_g, m_w_up, m_w_down, m_final_norm_g, v_norm_mix_g, v_w_in, v_conv_w, v_conv_b, v_lru_w_a, v_lru_b_a, v_lru_w_x, v_lru_b_x, v_lru_lambda, v_sgu_ln_g, v_sgu_ln_b, v_sgu_w_s, v_sgu_b_s, v_w_branch_a, v_w_branch_b, v_w_out, v_norm_ffn_g, v_w_up, v_w_down, v_final_norm_g):
    w = dict(norm_mix_g=norm_mix_g, w_in=w_in, conv_w=conv_w, conv_b=conv_b, lru_w_a=lru_w_a, lru_b_a=lru_b_a,
             lru_w_x=lru_w_x, lru_b_x=lru_b_x, lru_lambda=lru_lambda, sgu_ln_g=sgu_ln_g, sgu_ln_b=sgu_ln_b,
             sgu_w_s=sgu_w_s, sgu_b_s=sgu_b_s, w_branch_a=w_branch_a, w_branch_b=w_branch_b, w_out=w_out,
             norm_ffn_g=norm_ffn_g, w_up=w_up, w_down=w_down, final_norm_g=final_norm_g)
    m = dict(norm_mix_g=m_norm_mix_g, w_in=m_w_in, conv_w=m_conv_w, conv_b=m_conv_b, lru_w_a=m_lru_w_a,
             lru_b_a=m_lru_b_a, lru_w_x=m_lru_w_x, lru_b_x=m_lru_b_x, lru_lambda=m_lru_lambda,
             sgu_ln_g=m_sgu_ln_g, sgu_ln_b=m_sgu_ln_b, sgu_w_s=m_sgu_w_s, sgu_b_s=m_sgu_b_s,
             w_branch_a=m_w_branch_a, w_branch_b=m_w_branch_b, w_out=m_w_out, norm_ffn_g=m_norm_ffn_g,
             w_up=m_w_up, w_down=m_w_down, final_norm_g=m_final_norm_g)
    v = dict(norm_mix_g=v_norm_mix_g, w_in=v_w_in, conv_w=v_conv_w, conv_b=v_conv_b, lru_w_a=v_lru_w_a,
             lru_b_a=v_lru_b_a, lru_w_x=v_lru_w_x, lru_b_x=v_lru_b_x, lru_lambda=v_lru_lambda,
             sgu_ln_g=v_sgu_ln_g, sgu_ln_b=v_sgu_ln_b, sgu_w_s=v_sgu_w_s, sgu_b_s=v_sgu_b_s,
             w_branch_a=v_w_branch_a, w_branch_b=v_w_branch_b, w_out=v_w_out, norm_ffn_g=v_norm_ffn_g,
             w_up=v_w_up, w_down=v_w_down, final_norm_g=v_final_norm_g)
    core = lax.axis_index("c")
    chip = 2 * lax.axis_index("x") + lax.axis_index("y")
    sel = jnp.stack([core, 1 - core, chip, 2 * chip + core]).astype(jnp.int32)
    this_core, this_chip = ("sel", 0), ("sel", 2)
    sds = jax.ShapeDtypeStruct

    ts = TOKEN_TILE

    def after_all(arrays):
        return jnp.stack([a[(0,) * a.ndim].astype(F32) for a in arrays])

    halves = {k:(w[k].shape[1] // 2, w[k].shape[2]) for k in BIG}

    def half_view(k, a):
        return a.reshape((2 * N_QUARTERS,) + halves[k])

    def full_view(k, a):
        if k == "conv_w":
            return a.reshape(N_QUARTERS, DEPTH, CONV_WIDTH, -1).transpose(1, 2, 0, 3).reshape(DEPTH, CONV_WIDTH, D_RNN)
        r2, cols = halves[k]
        if k in ("w_in", "w_up"):
            return a.reshape(1, N_QUARTERS, 2 * r2, cols)
        return a.reshape(1, 2 * N_QUARTERS * r2, cols)

    layer_bufs = [{}, {}]

    def cast_weights(k, after):
        _, r, cols = w[k].shape
        w4 = w[k].reshape(DEPTH, 1, r, cols)
        outs = _ew_call(lambda a, b: (a, b), "cast_weights", [(w4, (0, 0)), (w4, (1, 0))],
                        [(sds((1, N_QUARTERS, r, cols), BF), (0, this_chip))] * DEPTH, 1, sel, after=after)
        for l in range(DEPTH):
            layer_bufs[l][k] = half_view(k, outs[l])

    conv_buf = lax.dynamic_update_slice_in_dim(
        jnp.zeros((N_QUARTERS, DEPTH) + conv_w.shape[1:], F32), conv_w[None], chip, axis=0)
    layer_bufs[0]["conv_w"] = conv_buf.reshape((2 * N_QUARTERS,) + conv_w.shape[1:])
    sm = {k: w[k] for k in SMALL if k != "conv_w"}

    def gather_start(tag, l, keys, after):
        bufs = [layer_bufs[l][k] for k in keys]
        return _exchange_start(f"gather_start_{tag}", bufs, _gather_copies, 3 * len(keys), after)

    def gather_finish(tag, keys, started, after):
        send_sems, recv_sems, thru, _ = started
        landed = _exchange_wait(f"gather_wait_{tag}", send_sems, recv_sems, thru, _gather_copies, after)
        landed = _sibling_inplace_call("gather_forward", landed, _gather_forward_slabs, 3 * len(keys))
        return {k: full_view(k, a) for k, a in zip(keys, landed)}

    first, rest = ("w_in",), tuple(k for k in BIG if k != "w_in")
    cast_weights("w_in", None)
    started_a = gather_start("0a", 0, first + ("conv_w",), sel)
    for k in rest:
        cast_weights(k, started_a[3])
    started_b = gather_start("0b", 0, rest, started_a[3])
    started_c = gather_start("1a", 1, first, started_b[3])
    started_d = gather_start("1b", 1, rest, started_c[3])

    def arrives(tag, keys, started):
        state = {}

        def hook(after):
            landed = _exchange_wait(f"gather_wait_{tag}", started[0], started[1], started[2], _gather_copies, after)
            state["forward"] = _exchange_start(f"forward_start_{tag}", landed, _forward_copies, 3 * len(keys), after)
            return state["forward"][3][0, 0]

        def finish(after):
            send_sems, recv_sems, thru, _ = state["forward"]
            done = _exchange_wait(f"forward_wait_{tag}", send_sems, recv_sems, thru, _forward_copies, after)
            return {k: full_view(k, a) for k, a in zip(keys, done)}

        return hook, finish

    p0, p1 = _layer_small(sm, 0, sel[0:1]), _layer_small(sm, 1, sel[0:1])
    h0 = _norm_call(x[0], p0["g1"], ts)
    proj_own = _inproj_part_call(h0, full_view("w_in", started_a[2][0]), 2 * ts, sel[2:3], 0, 1)
    ready = after_all([started_d[3], proj_own] + [p[k] for p in (p0, p1) for k in ("wa", "wx", "wm")])
    big0 = gather_finish("0a", first + ("conv_w",), started_a, ready)
    for l, p in enumerate((p0, p1)):
        p["cw"] = big0["conv_w"][l]
    proj0 = _inproj_part_call(h0, big0["w_in"], 2 * ts, sel[2:3], 1, N_QUARTERS - 1, proj_own)
    hook, finish = arrives("0b", rest, started_b)
    sv0 = _layer_fwd_mix(x[0], big0, p0, ts, h0, hook, proj0)
    big0.update(finish(sv0["yb_pre"]))
    x_mid = _layer_fwd_out(sv0, big0, ts)
    hook, finish = arrives("1a", first, started_c)
    h1 = _norm_call(x_mid, p1["g1"] + hook(x_mid), ts)
    big1 = finish(h1)
    hook, finish = arrives("1b", rest, started_d)
    sv1 = _layer_fwd_mix(x_mid, big1, p1, ts, h1, hook)
    big1.update(finish(sv1["yb_pre"]))
    x_out = _layer_fwd_out(sv1, big1, ts)
    dx, loss, dgf = _loss_call(x_out, loss_target[0], final_norm_g.reshape(1, -1), ts)

    def pair_start(tag, gb, after):
        sends = [gb[k][1] for k in gb]
        zones = [lax.empty(a.shape, BF) for a in sends]
        return _exchange_start(f"pair_start_{tag}", sends + zones, _sibling_copies, len(sends), after)

    def reduce_start(tag, gb, after, pair=None):
        keys = tuple(gb)
        if pair is None:
            from_sibling = _sibling_send_call([gb[k][1] for k in keys])
        else:
            done = _exchange_wait(f"pair_wait_{tag}", pair[0], pair[1], pair[2], _sibling_copies, after)
            from_sibling = done[len(keys):]
        sums = [
            _ew_call(lambda a, b: (a + b.astype(F32),), "pair_sum", [(gb[k][0][None], (0, "g")), (r[None], (0, "g"))],
                     [(sds((1,) + r.shape, BF), (0, "g"))], N_QUARTERS)[0][0]
            for k, r in zip(keys, from_sibling)]
        zones = [lax.empty((3,) + a.shape[1:], BF) for a in sums]
        started = _exchange_start(f"reduce_start_{tag}", sums + zones, _owner_copies, 3 * len(keys), after)
        return keys, started

    def reduce_finish(tag, l, keys_started, after, reduced):
        keys, (send_sems, recv_sems, thru, _) = keys_started
        done = _exchange_wait(f"reduce_wait_{tag}", send_sems, recv_sems, thru, _owner_copies, after)
        sums, zones = done[:len(keys)], done[len(keys):]
        for i, k in enumerate(keys):
            r2, cols = halves[k]
            reduced[k] = _ew_call(
                lambda a, b, c, d: (((a.astype(F32) + b.astype(F32)) + c.astype(F32)) + d.astype(F32),),
                "quarter_sum", [(sums[i][None], (0, this_chip))] + [(zones[i][None], (0, j)) for j in range(3)],
                [(sds((DEPTH, 2, r2, cols), F32), (l, this_core))], 1, sel, into=reduced.get(k))[0]

    def behind(params, key, started):
        return dict(params, **{key: params[key] + started[1][3][0, 0]})

    dx1, gb_ffn, gs1 = _layer_bwd_ffn(dx, sv1, big1, ts)
    merge_out, gb_merge = _layer_bwd_merge(dx1, sv1, big1, ts)
    dx_mid, gb_in, gs1_mix = _layer_bwd_branches(dx1, merge_out, sv1, big1, lru_lambda[1], ts)
    gb_1 = {**gb_ffn, **gb_merge, **gb_in}
    pair_1 = pair_start("1", gb_1, dx_mid)
    sv0["p"] = behind(sv0["p"], "g2", (None, pair_1))
    dx1, gb_ffn, gs0 = _layer_bwd_ffn(dx_mid, sv0, big0, ts)
    exchange_1 = reduce_start("1", gb_1, dx1, pair_1)
    pair_0a = pair_start("0a", gb_ffn, exchange_1[1][3])
    merge_out, gb_merge = _layer_bwd_merge(dx1, sv0, big0, ts, pair_0a[3])
    exchange_0a = reduce_start("0a", gb_ffn, merge_out[0], pair_0a)
    pair_0b = pair_start("0b", gb_merge, exchange_0a[1][3])
    sv0["p"] = behind(sv0["p"], "lg", (None, pair_0b))
    started_0b = {}

    def after_sgu(duv):
        started_0b["exchange"] = reduce_start("0b", gb_merge, duv, pair_0b)
        return started_0b["exchange"][1][3][0, 0]

    grad_x, gb_in, gs0_mix = _layer_bwd_branches(dx1, merge_out, sv0, big0, lru_lambda[0], ts, after_sgu)
    exchange_0b = started_0b["exchange"]
    exchange_0c = reduce_start("0c", gb_in, exchange_0b[1][3])
    layer_gs = [{**gs0, **gs0_mix}, {**gs1, **gs1_mix}]
    gs = {k: jnp.stack([g[k] for g in layer_gs]) for k in layer_gs[0]}
    gs["final_norm_g"] = dgf[0]
    gs["loss"] = loss[0, 0:1]

    me = ("sel", 3)
    piece = (1, N_DEVICES, SMALL_ROWS, 128)
    packed = _pack_small(gs).reshape(piece)
    scatter = _exchange_start("small_scatter_start", [packed[0], lax.empty(piece[1:], F32)], _small_scatter_copies,
                              N_DEVICES - 1, exchange_0c[1][3])
    reduced = {}
    reduce_finish("1", 1, exchange_1, scatter[3], reduced)
    reduce_finish("0a", 0, exchange_0a, reduced["w_in"], reduced)
    reduce_finish("0b", 0, exchange_0b, reduced["w_down"], reduced)

    def swap_slabs(ref, c, i):
        layers = (1,) if BIG[i] == "w_in" else range(DEPTH)
        return [(ref.at[l, c], ref.at[l, 1 - c]) for l in layers]

    swapped = dict(zip(BIG, _sibling_inplace_call("grads_swap_halves", [reduced[k] for k in BIG], swap_slabs,
                                                  DEPTH * len(BIG) - 1)))

    def adamw_layers(k, grad, layer, into, after=None):
        if layer is None:
            views = [_as4(_as_rows(a)) for a in (w[k], grad, m[k], v[k])]
            idx = (0, 0)
        else:
            views = [a.reshape((1,) + w[k].shape) for a in (w[k], grad, m[k], v[k])]
            idx = (0, layer)
        return _ew_call(_adamw, "adamw_big", [(a, idx) for a in views], [(sds(views[0].shape, F32), idx)] * 3,
                        into=into, after=after)

    updated, last_update = {}, None
    for k in BIG:
        updated[k] = adamw_layers(k, swapped[k], 1 if k == "w_in" else None, None, last_update)
        last_update = updated[k][0]
    scattered = _exchange_wait("small_scatter_wait", scatter[0], scatter[1], scatter[2], _small_scatter_copies,
                               last_update)
    summed = _ew_call(
        lambda *parts: (functools.reduce(lambda a, b: a + b, parts),), "small_sum",
        [(scattered[0][None], (0, me))]
        + [(scattered[1][None], (0, lambda g, s, k=k: s[3] ^ k)) for k in range(1, N_DEVICES)],
        [(sds(piece, F32), (0, me))], 1, sel)[0]
    spread = _exchange_start("small_spread_start", [summed[0]], _small_spread_copies, N_DEVICES - 1, summed)
    reduced["w_in"] = swapped["w_in"]
    reduce_finish("0c", 0, exchange_0c, spread[3], reduced)
    last = _sibling_inplace_call("grads_swap_last", [reduced["w_in"]],
                                 lambda ref, c, i: [(ref.at[0, c], ref.at[0, 1 - c])], 1)[0]
    swapped["w_in"] = last
    updated["w_in"] = adamw_layers("w_in", last, 0, updated["w_in"])
    grads_big = {k: swapped[k].reshape(w[k].shape) for k in BIG}
    delta, new_m, new_v = ({k: updated[k][j].reshape(w[k].shape) for k in BIG} for j in range(3))
    gathered_small = _exchange_wait("small_spread_wait", spread[0], spread[1], spread[2], _small_spread_copies,
                                    updated["w_in"][0])[0]

    like = {k: jax.ShapeDtypeStruct(gs[k].shape, F32) for k in SMALL}
    like["loss"] = jax.ShapeDtypeStruct((1,), F32)
    grads_small = _unpack_small(gathered_small, like)
    total = grads_small.pop("loss")[0]
    conv_q = grads_small["conv_w"].reshape(DEPTH, CONV_WIDTH, N_QUARTERS, D_RNN // N_QUARTERS)
    grads_small["conv_w"] = lax.dynamic_index_in_dim(conv_q, chip, axis=2, keepdims=False)
    outs = _small_adamw_call(*[[_as_rows(d[k]) for k in SMALL] for d in (w, grads_small, m, v)])
    for d, o in zip((delta, new_m, new_v), outs):
        for k, a in zip(SMALL, o):
            d[k] = a.reshape(w[k].shape)

    grads = {**grads_big, **grads_small}
    return (total, grad_x[None], *[grads[k] for k in WEIGHTS], *[delta[k] for k in WEIGHTS],
            *[new_m[k] for k in WEIGHTS], *[new_v[k] for k in WEIGHTS])
```

```python
import functools
import math

import jax
import jax.numpy as jnp
from jax import lax
from jax.experimental import pallas as pl
from jax.experimental.pallas import tpu as pltpu

F32 = jnp.float32
BF = jnp.bfloat16

DEPTH = 2
D_MODEL = 1024
D_RNN = 1280
D_SGU = 1024
D_FF = 4096
D_IN = 2 * D_RNN + 2 * D_SGU + 2 * D_MODEL
N_QUARTERS = 4
Q_IN = D_IN // N_QUARTERS
Q_FF = D_FF // N_QUARTERS
RNN_HEADS = 20
RNN_HEAD_DIM = 64
LRU_GROUP = 256
N_LRU_GROUPS = D_RNN // LRU_GROUP
HEADS_PER_GROUP = LRU_GROUP // RNN_HEAD_DIM
CONV_WIDTH = 4
LRU_C = 8.0
SGU_GROUPS = 8
SGU_BLOCK = 128
CHUNK = 64
EPS = 1e-6

ADAM_LR = 0.001
ADAM_B1 = 0.9
ADAM_B2 = 0.999
ADAM_EPS = 1e-08
ADAM_WD = 0.01
ADAM_STEP = 10

SUBLANES = 8
TOKEN_TILE = 512
VMEM_LIMIT_BYTES = 56 * 1024 * 1024

MESH = pl.DeviceIdType.MESH


def _params(semantics=None, vmem=True, **kw):
    return pltpu.CompilerParams(
        dimension_semantics=semantics,
        vmem_limit_bytes=VMEM_LIMIT_BYTES if vmem else None,
        **kw,
    )


def _dot(a, b):
    return jnp.dot(a, b, preferred_element_type=F32)


def _dot_nt(a, b):
    return lax.dot_general(a, b, (((1,), (1,)), ((), ())), preferred_element_type=F32)


def _dot_tn(a, b):
    return lax.dot_general(a, b, (((0,), (0,)), ((), ())), preferred_element_type=F32)


_GELU_C = math.sqrt(2.0 / math.pi)
_GELU_A = 0.044715


def _gelu(x):
    return 0.5 * x * (1.0 + jnp.tanh(_GELU_C * (x + _GELU_A * x * x * x)))


def _gelu_and_grad(x):
    x2 = x * x
    t = jnp.tanh(_GELU_C * (x + _GELU_A * x2 * x))
    du = _GELU_C * (1.0 + 3.0 * _GELU_A * x2)
    return 0.5 * x * (1.0 + t), 0.5 * (1.0 + t) + 0.5 * x * (1.0 - t * t) * du


def _rms_stats(x):
    return lax.rsqrt(jnp.mean(x * x, axis=-1, keepdims=True) + EPS)


def _rms_bwd(dy, x, g):
    rs = _rms_stats(x)
    n = x * rs
    dn = dy * g
    dx = rs * (dn - n * jnp.mean(dn * n, axis=-1, keepdims=True))
    return dx, dy * n


def _row_sum(x):
    return jnp.sum(x, axis=0, keepdims=True)


def _tile_spec(ts, width, col=0):
    return pl.BlockSpec((ts, width), lambda i, col=col: (i, col))


def _full_spec(shape):
    zeros = (0,) * len(shape)
    return pl.BlockSpec(shape, lambda *_: zeros)


def _layer_spec(w, layer):
    zeros = (0,) * (w.ndim - 1)
    return pl.BlockSpec((None,) + tuple(w.shape[1:]), lambda *_: (layer,) + zeros)


def _norm_call(x, g, ts):
    s = x.shape[0]

    def body(x_ref, g_ref, h_ref):
        xv = x_ref[...]
        h_ref[...] = (xv * _rms_stats(xv) * g_ref[...]).astype(BF)

    return pl.pallas_call(
        body, name="norm_fwd", grid=(s // ts,),
        in_specs=[_tile_spec(ts, D_MODEL), _full_spec((1, D_MODEL))],
        out_specs=_tile_spec(ts, D_MODEL),
        out_shape=jax.ShapeDtypeStruct((s, D_MODEL), BF),
        compiler_params=_params(("parallel",)),
    )(x, g)


def _inproj_call(h, w_in, layer, ts):
    s = h.shape[0]

    def body(h_ref, w_ref, o_ref):
        o_ref[...] = _dot(h_ref[...], w_ref[...]).astype(BF)

    return pl.pallas_call(
        body, name="inproj_fwd", grid=(N_QUARTERS, s // ts),
        in_specs=[
            pl.BlockSpec((ts, D_MODEL), lambda q, i: (i, 0)),
            pl.BlockSpec((None, None, D_MODEL, Q_IN), lambda q, i: (layer, q, 0, 0)),
        ],
        out_specs=pl.BlockSpec((ts, Q_IN), lambda q, i: (i, q)),
        out_shape=jax.ShapeDtypeStruct((s, D_IN), BF),
        compiler_params=_params(("parallel", "parallel")),
    )(h, w_in)


def _inproj_part_call(h, w_in, ts, own, first, count, into=None):
    s = h.shape[0]

    def quarter(j, sel):
        return (sel[0] + first + j) % N_QUARTERS

    def body(sel_ref, h_ref, w_ref, *rest):
        rest[-1][...] = _dot(h_ref[...], w_ref[...]).astype(BF)

    in_specs = [pl.BlockSpec((ts, D_MODEL), lambda j, i, sel: (i, 0)),
                pl.BlockSpec((None, None, D_MODEL, Q_IN), lambda j, i, sel: (0, quarter(j, sel), 0, 0))]
    operands = [h, w_in]
    aliases = {}
    if into is not None:
        in_specs.append(pl.BlockSpec(memory_space=pl.ANY))
        operands.append(into)
        aliases = {3: 0}
    return pl.pallas_call(
        body, name="inproj_fwd_part", out_shape=jax.ShapeDtypeStruct((s, D_IN), BF),
        grid_spec=pltpu.PrefetchScalarGridSpec(
            num_scalar_prefetch=1, grid=(count, s // ts), in_specs=in_specs,
            out_specs=pl.BlockSpec((ts, Q_IN), lambda j, i, sel: (i, quarter(j, sel)))),
        input_output_aliases=aliases,
        compiler_params=_params(("parallel", "parallel")),
    )(own, *operands)


def _shift_down(x, tail, s):
    xr = pltpu.roll(x, s, 0)
    tr = pltpu.roll(tail, s, 0)
    row = lax.broadcasted_iota(jnp.int32, tail.shape, 0)
    top = jnp.where(row < s, tr, xr[0:SUBLANES])
    return jnp.concatenate([top, xr[SUBLANES:]], axis=0)


def _shift_up(x, head, s):
    t = x.shape[0]
    xr = pltpu.roll(x, t - s, 0)
    hr = pltpu.roll(head, SUBLANES - s, 0)
    row = lax.broadcasted_iota(jnp.int32, head.shape, 0)
    bottom = jnp.where(row >= SUBLANES - s, hr, xr[t - SUBLANES:])
    return jnp.concatenate([xr[: t - SUBLANES], bottom], axis=0)


def _conv_fwd(x, tail, cw_ref, cb_ref):
    out = cb_ref[...] + cw_ref[CONV_WIDTH - 1:CONV_WIDTH, :] * x
    for s in range(1, CONV_WIDTH):
        k = CONV_WIDTH - 1 - s
        out = out + cw_ref[k:k + 1, :] * _shift_down(x, tail, s)
    return out


def _group_dot(x_bf, w_ref, dot):
    cols = [dot(x_bf[:, g * LRU_GROUP:(g + 1) * LRU_GROUP], w_ref[g]) for g in range(N_LRU_GROUPS)]
    return jnp.concatenate(cols, axis=1)


def _lru_gates(xr, wa_ref, wx_ref, ba_ref, bx_ref, sp_ref):
    xb = xr.astype(BF)
    r = jax.nn.sigmoid(_group_dot(xb, wa_ref, _dot) + ba_ref[...])
    i = jax.nn.sigmoid(_group_dot(xb, wx_ref, _dot) + bx_ref[...])
    log_a = (-LRU_C * r) * sp_ref[...]
    a = jnp.exp(log_a)
    nrm2 = -jnp.tanh(log_a) * (a * a + 1.0)
    inv_nrm = lax.rsqrt(jnp.maximum(nrm2, 1e-36))
    return r, i, a, nrm2 * inv_nrm, inv_nrm


def _linear_scan(a, b, carry, al_ref, bl_ref, h_ref, reverse):
    t, c = a.shape
    rowm = lax.broadcasted_iota(jnp.int32, (t, c), 0) & (SUBLANES - 1)
    for d in (1, 2, 4):
        if reverse:
            keep, sh = rowm < SUBLANES - d, t - d
        else:
            keep, sh = rowm >= d, d
        a_sh = jnp.where(keep, pltpu.roll(a, sh, 0), 1.0)
        b_sh = jnp.where(keep, pltpu.roll(b, sh, 0), 0.0)
        b = a * b_sh + b
        a = a * a_sh
    al_ref[...] = a
    bl_ref[...] = b
    groups = t // SUBLANES

    def step(j, state):
        jj = groups - 1 - j if reverse else j
        off = pl.multiple_of(jj * SUBLANES, SUBLANES)
        rows = bl_ref[pl.ds(off, SUBLANES), :] + al_ref[pl.ds(off, SUBLANES), :] * state
        h_ref[pl.ds(off, SUBLANES), :] = rows
        last = rows[0:1, :] if reverse else rows[SUBLANES - 1:SUBLANES, :]
        return jnp.broadcast_to(last, (SUBLANES, c))

    out = lax.fori_loop(0, groups, step, jnp.broadcast_to(carry, (SUBLANES, c)))
    return out[0:1, :]


def _rnn_fwd_call(proj, wa, wx, ba, bx, sp, cw, cb, ts):
    s = proj.shape[0]

    def body(xg_ref, wa_ref, wx_ref, ba_ref, bx_ref, sp_ref, cw_ref, cb_ref, xr_ref, hr_ref, ya_ref,
             tail_sc, carry_sc, al_sc, bl_sc, h_sc):
        @pl.when(pl.program_id(0) == 0)
        def _():
            tail_sc[...] = jnp.zeros_like(tail_sc)
            carry_sc[...] = jnp.zeros_like(carry_sc)

        x = xg_ref[:, :D_RNN].astype(F32)
        g = xg_ref[:, D_RNN:]
        xr = _conv_fwd(x, tail_sc[...], cw_ref, cb_ref)
        tail_sc[...] = x[ts - SUBLANES:, :]
        xr_ref[...] = xr.astype(BF)
        _, i, a, nrm, _ = _lru_gates(xr, wa_ref, wx_ref, ba_ref, bx_ref, sp_ref)
        carry_sc[...] = _linear_scan(a, nrm * (i * xr), carry_sc[...], al_sc, bl_sc, h_sc, False)
        h = h_sc[...]
        hr_ref[...] = h.astype(BF)
        ya_ref[...] = (h * _gelu(g)).astype(BF)

    gw = (N_LRU_GROUPS, LRU_GROUP, LRU_GROUP)
    return pl.pallas_call(
        body, name="rnn_fwd", grid=(s // ts,),
        in_specs=[_tile_spec(ts, 2 * D_RNN), _full_spec(gw), _full_spec(gw),
                  _full_spec((1, D_RNN)), _full_spec((1, D_RNN)), _full_spec((1, D_RNN)),
                  _full_spec((CONV_WIDTH, D_RNN)), _full_spec((1, D_RNN))],
        out_specs=[_tile_spec(ts, D_RNN)] * 3,
        out_shape=[jax.ShapeDtypeStruct((s, D_RNN), BF)] * 3,
        scratch_shapes=[pltpu.VMEM((SUBLANES, D_RNN), F32), pltpu.VMEM((1, D_RNN), F32),
                        pltpu.VMEM((ts, D_RNN), F32), pltpu.VMEM((ts, D_RNN), F32),
                        pltpu.VMEM((ts, D_RNN), F32)],
        compiler_params=_params(("arbitrary",)),
    )(proj, wa, wx, ba, bx, sp, cw, cb)


def _layernorm_fwd(x):
    mu = jnp.mean(x, axis=-1, keepdims=True)
    xc = x - mu
    rstd = lax.rsqrt(jnp.mean(xc * xc, axis=-1, keepdims=True) + EPS)
    return xc * rstd, rstd


def _sgu_mix(vn_bf, wm_ref, bsb_ref, ts):
    rows = []
    for blk in range(ts // SGU_BLOCK):
        r0 = blk * SGU_BLOCK
        cols = [
            _dot(wm_ref[g], vn_bf[r0:r0 + SGU_BLOCK, g * SGU_BLOCK:(g + 1) * SGU_BLOCK]) + bsb_ref[g]
            for g in range(SGU_GROUPS)
        ]
        rows.append(jnp.concatenate(cols, axis=1))
    return jnp.concatenate(rows, axis=0)


def _sgu_fwd_call(proj, wm, bsb, lg, lb, ts):
    s = proj.shape[0]

    def body(uv_ref, wm_ref, bsb_ref, lg_ref, lb_ref, yb_ref):
        gu = _gelu(uv_ref[:, :D_SGU])
        gv = _gelu(uv_ref[:, D_SGU:2 * D_SGU]).astype(F32)
        nh, _ = _layernorm_fwd(gv)
        vn = (nh * lg_ref[...] + lb_ref[...]).astype(BF)
        yb_ref[...] = (gu * _sgu_mix(vn, wm_ref, bsb_ref, ts)).astype(BF)

    sw = (SGU_GROUPS, SGU_BLOCK, SGU_BLOCK)
    return pl.pallas_call(
        body, name="sgu_fwd", grid=(s // ts,),
        in_specs=[_tile_spec(ts, 2 * D_RNN, 1), _full_spec(sw), _full_spec(sw),
                  _full_spec((1, D_SGU)), _full_spec((1, D_SGU))],
        out_specs=_tile_spec(ts, D_SGU),
        out_shape=jax.ShapeDtypeStruct((s, D_SGU), BF),
        compiler_params=_params(("parallel",)),
    )(proj, wm, bsb, lg, lb)


_GATE_COL0 = (2 * D_RNN + 2 * D_SGU) // 512


def _gate_specs(ts):
    return [_tile_spec(ts, 512, _GATE_COL0 + j) for j in range(4)]


def _merge_call(x, proj, ya_pre, yb_pre, w_ba, w_bb, w_out, g2, layer, ts):
    s = x.shape[0]

    def body(x_ref, ga0, ga1, gb0, gb1, ya_ref, yb_ref, wa_ref, wb_ref, wo_ref, g2_ref,
             x1_ref, yao_ref, ybo_ref, mg_ref, h2_ref):
        ya = _dot(ya_ref[...], wa_ref[...])
        yb = _dot(yb_ref[...], wb_ref[...])
        sa = jax.nn.sigmoid(jnp.concatenate([ga0[...], ga1[...]], axis=1).astype(F32))
        sb = jax.nn.sigmoid(jnp.concatenate([gb0[...], gb1[...]], axis=1).astype(F32))
        merged = (sa * ya + sb * yb).astype(BF)
        x1 = x_ref[...] + _dot(merged, wo_ref[...])
        x1_ref[...] = x1
        yao_ref[...] = ya.astype(BF)
        ybo_ref[...] = yb.astype(BF)
        mg_ref[...] = merged
        h2_ref[...] = (x1 * _rms_stats(x1) * g2_ref[...]).astype(BF)

    act = jax.ShapeDtypeStruct((s, D_MODEL), BF)
    return pl.pallas_call(
        body, name="merge_fwd", grid=(s // ts,),
        in_specs=[_tile_spec(ts, D_MODEL)] + _gate_specs(ts) + [
            _tile_spec(ts, D_RNN), _tile_spec(ts, D_SGU),
            _layer_spec(w_ba, layer), _layer_spec(w_bb, layer), _layer_spec(w_out, layer),
            _full_spec((1, D_MODEL))],
        out_specs=[_tile_spec(ts, D_MODEL)] * 5,
        out_shape=[jax.ShapeDtypeStruct((s, D_MODEL), F32), act, act, act, act],
        compiler_params=_params(("parallel",)),
    )(x, proj, proj, proj, proj, ya_pre, yb_pre, w_ba, w_bb, w_out, g2)


def _ffn_call(x1, h2, w_up, w_down, layer, ts):
    s = x1.shape[0]

    def body(x1_ref, h2_ref, wu_ref, wd_ref, x2_ref, p_ref):
        h2v = h2_ref[...]
        acc = x1_ref[...]
        for q in range(N_QUARTERS):
            p = _dot(h2v, wu_ref[q])
            p_ref[:, q * Q_FF:(q + 1) * Q_FF] = p.astype(BF)
            f = jnp.square(jnp.maximum(p, 0.0)).astype(BF)
            acc = acc + _dot(f, wd_ref[q * Q_FF:(q + 1) * Q_FF, :])
        x2_ref[...] = acc

    return pl.pallas_call(
        body, name="ffn_fwd", grid=(s // ts,),
        in_specs=[_tile_spec(ts, D_MODEL), _tile_spec(ts, D_MODEL),
                  pl.BlockSpec((None, N_QUARTERS, D_MODEL, Q_FF), lambda i: (layer, 0, 0, 0)),
                  pl.BlockSpec((None, D_FF, D_MODEL), lambda i: (layer, 0, 0))],
        out_specs=[_tile_spec(ts, D_MODEL), _tile_spec(ts, D_FF)],
        out_shape=[jax.ShapeDtypeStruct((s, D_MODEL), F32), jax.ShapeDtypeStruct((s, D_FF), BF)],
        compiler_params=_params(("parallel",)),
    )(x1, h2, w_up, w_down)


def _loss_call(x, target, gf, ts):
    s = x.shape[0]

    def body(x_ref, t_ref, g_ref, dx_ref, loss_ref, dg_ref):
        @pl.when(pl.program_id(0) == 0)
        def _():
            loss_ref[...] = jnp.zeros_like(loss_ref)
            dg_ref[...] = jnp.zeros_like(dg_ref)

        xv = x_ref[...]
        gv = g_ref[...]
        err = xv * _rms_stats(xv) * gv - t_ref[...]
        part = 0.5 * jnp.sum(jnp.mean(err * err, axis=-1, keepdims=True), axis=0, keepdims=True)
        loss_ref[...] += jnp.broadcast_to(part, loss_ref.shape)
        dx, dg = _rms_bwd(err * (1.0 / D_MODEL), xv, gv)
        dx_ref[...] = dx
        dg_ref[...] += _row_sum(dg)

    return pl.pallas_call(
        body, name="loss_head", grid=(s // ts,),
        in_specs=[_tile_spec(ts, D_MODEL), _tile_spec(ts, D_MODEL), _full_spec((1, D_MODEL))],
        out_specs=[_tile_spec(ts, D_MODEL), _full_spec((1, 128)), _full_spec((1, D_MODEL))],
        out_shape=[jax.ShapeDtypeStruct((s, D_MODEL), F32), jax.ShapeDtypeStruct((1, 128), F32),
                   jax.ShapeDtypeStruct((1, D_MODEL), F32)],
        compiler_params=_params(("arbitrary",)),
    )(x, target, gf)


def _ffn_bwd_call(dx2, p, x1, g2, w_up, w_down, layer, ts):
    s = dx2.shape[0]

    def body(dx2_ref, p_ref, x1_ref, g2_ref, wu_ref, wd_ref, dx1_ref, dp_ref, dg_ref, dx2b_ref, dx1b_ref):
        @pl.when(pl.program_id(0) == 0)
        def _():
            dg_ref[...] = jnp.zeros_like(dg_ref)

        dx2v = dx2_ref[...]
        dyb = dx2v.astype(BF)
        dx2b_ref[...] = dyb
        dh2 = jnp.zeros((ts, D_MODEL), F32)
        for q in range(N_QUARTERS):
            cols = slice(q * Q_FF, (q + 1) * Q_FF)
            df = _dot_nt(dyb, wd_ref[cols, :])
            dp = (df * (2.0 * jnp.maximum(p_ref[:, cols].astype(F32), 0.0))).astype(BF)
            dp_ref[:, cols] = dp
            dh2 = dh2 + _dot_nt(dp, wu_ref[q])
        dx, dg = _rms_bwd(dh2, x1_ref[...], g2_ref[...])
        dx1 = dx2v + dx
        dx1_ref[...] = dx1
        dx1b_ref[...] = dx1.astype(BF)
        dg_ref[...] += _row_sum(dg)

    return pl.pallas_call(
        body, name="ffn_bwd", grid=(s // ts,),
        in_specs=[_tile_spec(ts, D_MODEL), _tile_spec(ts, D_FF), _tile_spec(ts, D_MODEL),
                  _full_spec((1, D_MODEL)),
                  pl.BlockSpec((None, N_QUARTERS, D_MODEL, Q_FF), lambda i: (layer, 0, 0, 0)),
                  pl.BlockSpec((None, D_FF, D_MODEL), lambda i: (layer, 0, 0))],
        out_specs=[_tile_spec(ts, D_MODEL), _tile_spec(ts, D_FF), _full_spec((1, D_MODEL)),
                   _tile_spec(ts, D_MODEL), _tile_spec(ts, D_MODEL)],
        out_shape=[jax.ShapeDtypeStruct((s, D_MODEL), F32), jax.ShapeDtypeStruct((s, D_FF), BF),
                   jax.ShapeDtypeStruct((1, D_MODEL), F32),
                   jax.ShapeDtypeStruct((s, D_MODEL), BF), jax.ShapeDtypeStruct((s, D_MODEL), BF)],
        compiler_params=_params(("arbitrary",)),
    )(dx2, p, x1, g2, w_up, w_down)


def _merge_bwd_call(dx1, proj, ya, yb, w_ba, w_bb, w_out, layer, ts, after=None):
    s = dx1.shape[0]

    def body(dx1_ref, ga0, ga1, gb0, gb1, ya_ref, yb_ref, wa_ref, wb_ref, wo_ref, *rest):
        dya_ref, dyb_ref, dgate_ref, dyap_ref, dybp_ref = rest[-5:]
        dm = _dot_nt(dx1_ref[...].astype(BF), wo_ref[...])
        sa = jax.nn.sigmoid(jnp.concatenate([ga0[...], ga1[...]], axis=1).astype(F32))
        sb = jax.nn.sigmoid(jnp.concatenate([gb0[...], gb1[...]], axis=1).astype(F32))
        dya = (dm * sa).astype(BF)
        dyb = (dm * sb).astype(BF)
        dya_ref[...] = dya
        dyb_ref[...] = dyb
        dgate_ref[:, :D_MODEL] = (dm * ya_ref[...].astype(F32) * sa * (1.0 - sa)).astype(BF)
        dgate_ref[:, D_MODEL:] = (dm * yb_ref[...].astype(F32) * sb * (1.0 - sb)).astype(BF)
        dyap_ref[...] = _dot_nt(dya, wa_ref[...]).astype(BF)
        dybp_ref[...] = _dot_nt(dyb, wb_ref[...]).astype(BF)

    act = jax.ShapeDtypeStruct((s, D_MODEL), BF)
    return pl.pallas_call(
        body, name="merge_bwd", grid=(s // ts,),
        in_specs=[_tile_spec(ts, D_MODEL)] + _gate_specs(ts) + [
            _tile_spec(ts, D_MODEL), _tile_spec(ts, D_MODEL),
            _layer_spec(w_ba, layer), _layer_spec(w_bb, layer), _layer_spec(w_out, layer)]
        + ([] if after is None else [pl.BlockSpec(memory_space=pl.ANY)]),
        out_specs=[_tile_spec(ts, D_MODEL), _tile_spec(ts, D_MODEL), _tile_spec(ts, 2 * D_MODEL),
                   _tile_spec(ts, D_RNN), _tile_spec(ts, D_SGU)],
        out_shape=[act, act, jax.ShapeDtypeStruct((s, 2 * D_MODEL), BF),
                   jax.ShapeDtypeStruct((s, D_RNN), BF), jax.ShapeDtypeStruct((s, D_SGU), BF)],
        compiler_params=_params(("parallel",)),
    )(dx1, proj, proj, proj, proj, ya, yb, w_ba, w_bb, w_out, *([] if after is None else [after]))


def _sgu_bwd_call(dyb_pre, proj, wm, bsb, mask, lg, lb, ts):
    s = proj.shape[0]

    def body(dy_ref, uv_ref, wm_ref, bsb_ref, mask_ref, lg_ref, lb_ref,
             duv_ref, dws_ref, dbs_ref, dlg_ref, dlb_ref, dm_sc):
        step = pl.program_id(0)

        @pl.when(step == 0)
        def _():
            dws_ref[...] = jnp.zeros_like(dws_ref)
            dlg_ref[...] = jnp.zeros_like(dlg_ref)
            dlb_ref[...] = jnp.zeros_like(dlb_ref)
            dm_sc[...] = jnp.zeros_like(dm_sc)

        gu, dgu_du = _gelu_and_grad(uv_ref[:, :D_SGU])
        gv, dgv_dv = _gelu_and_grad(uv_ref[:, D_SGU:2 * D_SGU])
        nh, rstd = _layernorm_fwd(gv.astype(F32))
        lgv = lg_ref[...]
        vn = (nh * lgv + lb_ref[...]).astype(BF)
        dy = dy_ref[...].astype(F32)
        du = dy * _sgu_mix(vn, wm_ref, bsb_ref, ts) * dgu_du
        dmix = dy * gu
        dmix_bf = dmix.astype(BF)
        dm_acc = dm_sc[...]
        rows = []
        for blk in range(ts // SGU_BLOCK):
            r0 = blk * SGU_BLOCK
            dm_acc = dm_acc + dmix[r0:r0 + SGU_BLOCK, :]
            cols = []
            for g in range(SGU_GROUPS):
                c0 = g * SGU_BLOCK
                dmg = dmix_bf[r0:r0 + SGU_BLOCK, c0:c0 + SGU_BLOCK]
                cols.append(_dot_tn(wm_ref[g], dmg))
                dws_ref[g] += mask_ref[...] * _dot_nt(dmg, vn[r0:r0 + SGU_BLOCK, c0:c0 + SGU_BLOCK])
            rows.append(jnp.concatenate(cols, axis=1))
        dm_sc[...] = dm_acc
        dvn = jnp.concatenate(rows, axis=0)
        dlg_ref[...] += _row_sum(dvn * nh)
        dlb_ref[...] += _row_sum(dvn)
        dnh = dvn * lgv
        dgv = rstd * (dnh - jnp.mean(dnh, axis=-1, keepdims=True)
                      - nh * jnp.mean(dnh * nh, axis=-1, keepdims=True))
        duv_ref[:, :D_SGU] = du.astype(BF)
        duv_ref[:, D_SGU:] = (dgv * dgv_dv).astype(BF)

        @pl.when(step == pl.num_programs(0) - 1)
        def _():
            for g in range(SGU_GROUPS):
                dbs_ref[:, g:g + 1] = jnp.sum(
                    dm_acc[:, g * SGU_BLOCK:(g + 1) * SGU_BLOCK], axis=1, keepdims=True)

    sw = (SGU_GROUPS, SGU_BLOCK, SGU_BLOCK)
    return pl.pallas_call(
        body, name="sgu_bwd", grid=(s // ts,),
        in_specs=[_tile_spec(ts, D_SGU), _tile_spec(ts, 2 * D_RNN, 1), _full_spec(sw), _full_spec(sw),
                  _full_spec((SGU_BLOCK, SGU_BLOCK)), _full_spec((1, D_SGU)), _full_spec((1, D_SGU))],
        out_specs=[_tile_spec(ts, 2 * D_SGU), _full_spec(sw), _full_spec((SGU_BLOCK, SGU_GROUPS)),
                   _full_spec((1, D_SGU)), _full_spec((1, D_SGU))],
        out_shape=[jax.ShapeDtypeStruct((s, 2 * D_SGU), BF), jax.ShapeDtypeStruct(sw, F32),
                   jax.ShapeDtypeStruct((SGU_BLOCK, SGU_GROUPS), F32),
                   jax.ShapeDtypeStruct((1, D_SGU), F32), jax.ShapeDtypeStruct((1, D_SGU), F32)],
        scratch_shapes=[pltpu.VMEM((SGU_BLOCK, D_SGU), F32)],
        compiler_params=_params(("arbitrary",)),
    )(dyb_pre, proj, wm, bsb, mask, lg, lb)


_ROW_DBA, _ROW_DBX, _ROW_DSP, _ROW_DCB, _ROW_DCW = 0, 1, 2, 3, 4
_PREV_ROWS = 16


def _rnn_bwd_call(dya_pre, proj, xr_saved, hr, wa, wx, ba, bx, sp, cw, ts):
    s = proj.shape[0]
    nt = s // ts
    per = ts // _PREV_ROWS

    def tile(i):
        return nt - 1 - i

    def prev(i):
        return jnp.maximum(tile(i) * per - 1, 0)

    def body(dy_ref, xg_ref, xr_ref, hr_ref, hrp_ref, wa_ref, wx_ref, ba_ref, bx_ref, sp_ref,
             cw_ref, dxg_ref, dwa_ref, dwx_ref, vec_ref,
             lam_carry, a_first, dxr_head, al_sc, bl_sc, lam_sc):
        step = pl.program_id(0)

        @pl.when(step == 0)
        def _():
            dwa_ref[...] = jnp.zeros_like(dwa_ref)
            dwx_ref[...] = jnp.zeros_like(dwx_ref)
            vec_ref[...] = jnp.zeros_like(vec_ref)
            lam_carry[...] = jnp.zeros_like(lam_carry)
            a_first[...] = jnp.zeros_like(a_first)
            dxr_head[...] = jnp.zeros_like(dxr_head)

        has_prev = (step < nt - 1).astype(F32)
        x = xg_ref[:, :D_RNN].astype(F32)
        g = xg_ref[:, D_RNN:]
        h_tail =hrp_ref[_PREV_ROWS - SUBLANES:, :].astype(F32) * has_prev
        xr = xr_ref[...].astype(F32)
        r, i, a, nrm, inv_nrm = _lru_gates(xr, wa_ref, wx_ref, ba_ref, bx_ref, sp_ref)
        h = hr_ref[...].astype(F32)
        dy = dy_ref[...].astype(F32)
        gg, dgg = _gelu_and_grad(g)

        coef = _shift_up(a, jnp.broadcast_to(a_first[...], (SUBLANES, D_RNN)), 1)
        lam_carry[...] = _linear_scan(coef, dy * gg, lam_carry[...], al_sc, bl_sc, lam_sc, True)
        a_first[...] = a[0:1, :]
        lam = lam_sc[...]

        da = lam * _shift_down(h, h_tail, 1)
        dnrm = lam * (i * xr)
        di = lam * nrm * xr
        dlog_a = da * a - dnrm * (a * a) * inv_nrm
        spv = sp_ref[...]
        dza = (dlog_a * (-LRU_C * spv)) * (r * (1.0 - r))
        dzx = di * (i * (1.0 - i))
        vec_ref[_ROW_DSP:_ROW_DSP + 1, :] += _row_sum(dlog_a * (-LRU_C * r))
        vec_ref[_ROW_DBA:_ROW_DBA + 1, :] += _row_sum(dza)
        vec_ref[_ROW_DBX:_ROW_DBX + 1, :] += _row_sum(dzx)
        xb = xr.astype(BF)
        dza_bf = dza.astype(BF)
        dzx_bf = dzx.astype(BF)
        for grp in range(N_LRU_GROUPS):
            cols = slice(grp * LRU_GROUP, (grp + 1) * LRU_GROUP)
            dwa_ref[grp] += _dot_tn(xb[:, cols], dza_bf[:, cols])
            dwx_ref[grp] += _dot_tn(xb[:, cols], dzx_bf[:, cols])
        dxr = (lam * nrm * i + _group_dot(dza_bf, wa_ref, _dot_nt) + _group_dot(dzx_bf, wx_ref, _dot_nt))

        vec_ref[_ROW_DCB:_ROW_DCB + 1, :] += _row_sum(dxr)
        head = dxr_head[...]
        dx = cw_ref[CONV_WIDTH - 1:CONV_WIDTH, :] * dxr
        vec_ref[_ROW_DCW + 3:_ROW_DCW + 4, :] += _row_sum(dxr * x)
        for sft in range(1, CONV_WIDTH):
            k = CONV_WIDTH - 1 - sft
            ahead = _shift_up(dxr, head, sft)
            dx = dx + cw_ref[k:k + 1, :] * ahead
            vec_ref[_ROW_DCW + k:_ROW_DCW + k + 1, :] += _row_sum(ahead * x)
        dxr_head[...] = dxr[0:SUBLANES, :]
        dxg_ref[:, :D_RNN] = dx.astype(BF)
        dxg_ref[:, D_RNN:] = (dy * h * dgg).astype(BF)

    gw = (N_LRU_GROUPS, LRU_GROUP, LRU_GROUP)
    rev = lambda width: pl.BlockSpec((ts, width), lambda i: (tile(i), 0))
    return pl.pallas_call(
        body, name="rnn_bwd", grid=(nt,),
        in_specs=[rev(D_RNN), rev(2 * D_RNN), rev(D_RNN), rev(D_RNN),
                  pl.BlockSpec((_PREV_ROWS, D_RNN), lambda i: (prev(i), 0)),
                  _full_spec(gw), _full_spec(gw),
                  _full_spec((1, D_RNN)), _full_spec((1, D_RNN)), _full_spec((1, D_RNN)),
                  _full_spec((CONV_WIDTH, D_RNN))],
        out_specs=[rev(2 * D_RNN), _full_spec(gw), _full_spec(gw), _full_spec((SUBLANES, D_RNN))],
        out_shape=[jax.ShapeDtypeStruct((s, 2 * D_RNN), BF), jax.ShapeDtypeStruct(gw, F32),
                   jax.ShapeDtypeStruct(gw, F32), jax.ShapeDtypeStruct((SUBLANES, D_RNN), F32)],
        scratch_shapes=[pltpu.VMEM((1, D_RNN), F32), pltpu.VMEM((1, D_RNN), F32),
                        pltpu.VMEM((SUBLANES, D_RNN), F32),
                        pltpu.VMEM((ts, D_RNN), F32), pltpu.VMEM((ts, D_RNN), F32),
                        pltpu.VMEM((ts, D_RNN), F32)],
        compiler_params=_params(("arbitrary",)),
    )(dya_pre, proj, xr_saved, hr, hr, wa, wx, ba, bx, sp, cw)


def _inproj_bwd_call(dxg, duv, dgate, dx1, x, g1, w_in, layer, ts):
    s = x.shape[0]

    def body(dxg_ref, duv_ref, dgt_ref, dx1_ref, x_ref, g_ref, w_ref, dx_ref, dproj_ref, dg_ref):
        @pl.when(pl.program_id(0) == 0)
        def _():
            dg_ref[...] = jnp.zeros_like(dg_ref)

        dproj = jnp.concatenate([dxg_ref[...], duv_ref[...], dgt_ref[...]], axis=1)
        dproj_ref[...] = dproj
        dh = jnp.zeros((ts, D_MODEL), F32)
        for q in range(N_QUARTERS):
            dh = dh + _dot_nt(dproj[:, q * Q_IN:(q + 1) * Q_IN], w_ref[q])
        dx, dg = _rms_bwd(dh, x_ref[...], g_ref[...])
        dx_ref[...] = dx1_ref[...] + dx
        dg_ref[...] += _row_sum(dg)

    return pl.pallas_call(
        body, name="inproj_bwd", grid=(s // ts,),
        in_specs=[_tile_spec(ts, 2 * D_RNN), _tile_spec(ts, 2 * D_SGU), _tile_spec(ts, 2 * D_MODEL),
                  _tile_spec(ts, D_MODEL), _tile_spec(ts, D_MODEL), _full_spec((1, D_MODEL)),
                  pl.BlockSpec((None, N_QUARTERS, D_MODEL, Q_IN), lambda i: (layer, 0, 0, 0))],
        out_specs=[_tile_spec(ts, D_MODEL), _tile_spec(ts, D_IN), _full_spec((1, D_MODEL))],
        out_shape=[jax.ShapeDtypeStruct((s, D_MODEL), F32), jax.ShapeDtypeStruct((s, D_IN), BF),
                   jax.ShapeDtypeStruct((1, D_MODEL), F32)],
        compiler_params=_params(("arbitrary",)),
    )(dxg, duv, dgate, dx1, x, g1, w_in)


def _relu_sq(p):
    return jnp.square(jnp.maximum(p, 0))


def _wgrad_call(a, b, core, tm, tn, tk, col_blocked, name, a_fn=None):
    s, m = a.shape
    n = b.shape[1]
    r, cols = (m, n // N_QUARTERS) if col_blocked else (m // N_QUARTERS, n)
    r2 = r // 2
    per_tile = tm // r
    steps = s // tk
    assert per_tile > 0 or steps == 1

    def body(core_ref, a_ref, b_ref, keep_ref, send_ref, *acc):
        av = a_ref[...]
        if a_fn is not None:
            av = a_fn(av)
        prod = _dot_tn(av.astype(BF), b_ref[...].astype(BF))

        def emit(total):
            for h in range(2):
                @pl.when(core_ref[0] == h)
                def _():
                    for q in range(per_tile):
                        keep_ref[q] = total[q * r + h * r2:q * r + (h + 1) * r2]
                        send_ref[q] = total[q * r + (1 - h) * r2:q * r + (2 - h) * r2].astype(BF)

        if per_tile == 0:
            mine = pl.program_id(1) == core_ref[0]

            @pl.when(mine)
            def _():
                keep_ref[0] = prod

            @pl.when(jnp.logical_not(mine))
            def _():
                send_ref[0] = prod.astype(BF)
        elif steps == 1:
            emit(prod)
        else:
            acc_ref, = acc
            step = pl.program_id(2)

            @pl.when(step == 0)
            def _():
                acc_ref[...] = prod

            @pl.when(jnp.logical_and(step > 0, step < steps - 1))
            def _():
                acc_ref[...] += prod

            @pl.when(step == steps - 1)
            def _():
                emit(acc_ref[...] + prod)

    if col_blocked:
        per_q = cols // tn
        out_spec = pl.BlockSpec((1, r2, tn), lambda j, i, k, c: (j // per_q, 0, j % per_q))
    else:
        out_spec = pl.BlockSpec((per_tile, r2, tn), lambda j, i, k, c: (i, 0, j))
    return pl.pallas_call(
        body, name=name,
        out_shape=[jax.ShapeDtypeStruct((N_QUARTERS, r2, cols), F32),
                   jax.ShapeDtypeStruct((N_QUARTERS, r2, cols), BF)],
        grid_spec=pltpu.PrefetchScalarGridSpec(
            num_scalar_prefetch=1, grid=(n // tn, m // tm, steps),
            in_specs=[pl.BlockSpec((tk, tm), lambda j, i, k, c: (k, i)),
                      pl.BlockSpec((tk, tn), lambda j, i, k, c: (k, j))],
            out_specs=[out_spec, out_spec],
            scratch_shapes=[] if steps == 1 else [pltpu.VMEM((tm, tn), F32)]),
        compiler_params=_params(("parallel", "parallel", "arbitrary")),
    )(core, a, b)


BIG = ("w_in", "w_up", "w_down", "w_branch_a", "w_branch_b", "w_out")


def _block_diag(w):
    w4 = w.reshape(N_LRU_GROUPS, HEADS_PER_GROUP, RNN_HEAD_DIM, RNN_HEAD_DIM)
    eye = jnp.eye(HEADS_PER_GROUP, dtype=w.dtype)
    return jnp.einsum("gjio,jk->gjiko", w4, eye).reshape(N_LRU_GROUPS, LRU_GROUP, LRU_GROUP)


def _block_diag_extract(d):
    d5 = d.reshape(N_LRU_GROUPS, HEADS_PER_GROUP, RNN_HEAD_DIM, HEADS_PER_GROUP, RNN_HEAD_DIM)
    blocks = [d5[:, j, :, j, :] for j in range(HEADS_PER_GROUP)]
    return jnp.stack(blocks, axis=1).reshape(RNN_HEADS, RNN_HEAD_DIM, RNN_HEAD_DIM)


def _sgu_mask():
    chunk = jnp.arange(SGU_BLOCK) // CHUNK
    return (chunk[:, None] >= chunk[None, :]).astype(F32)


def _layer_small(sm, l, core):
    row = lambda v: v.reshape(1, -1)
    return dict(
        core=core,
        g1=row(sm["norm_mix_g"][l]), g2=row(sm["norm_ffn_g"][l]),
        wa=_block_diag(sm["lru_w_a"][l]).astype(BF), wx=_block_diag(sm["lru_w_x"][l]).astype(BF),
        ba=row(sm["lru_b_a"][l]), bx=row(sm["lru_b_x"][l]),
        sp=row(jax.nn.softplus(-sm["lru_lambda"][l])),
        cw=sm["conv_w"][l] if "conv_w" in sm else None, cb=row(sm["conv_b"][l]),
        wm=(sm["sgu_w_s"][l] * _sgu_mask()).astype(BF),
        bsb=jnp.broadcast_to(sm["sgu_b_s"][l][:, :, None], (SGU_GROUPS, SGU_BLOCK, SGU_BLOCK)),
        lg=row(sm["sgu_ln_g"][l]), lb=row(sm["sgu_ln_b"][l]),
    )


def _layer_fwd_mix(x, big, p, ts, h=None, before_sgu=None, proj=None):
    if h is None:
        h = _norm_call(x, p["g1"], ts)
    if proj is None:
        proj = _inproj_call(h, big["w_in"], 0, 2 * ts)
    xr, hr, ya_pre = _rnn_fwd_call(proj, p["wa"], p["wx"], p["ba"], p["bx"], p["sp"], p["cw"], p["cb"], ts)
    lg = p["lg"] if before_sgu is None else p["lg"] + before_sgu(ya_pre)
    yb_pre = _sgu_fwd_call(proj, p["wm"], p["bsb"], lg, p["lb"], ts)
    return dict(p=p, x=x, h=h, proj=proj, xr=xr, hr=hr, ya_pre=ya_pre, yb_pre=yb_pre)


def _layer_fwd_out(sv, big, ts):
    x1, ya, yb, merged, h2 = _merge_call(sv["x"], sv["proj"], sv["ya_pre"], sv["yb_pre"], big["w_branch_a"],
                                         big["w_branch_b"], big["w_out"], sv["p"]["g2"], 0, ts)
    x2, pre = _ffn_call(x1, h2, big["w_up"], big["w_down"], 0, ts)
    sv.update(x1=x1, ya=ya, yb=yb, merged=merged, h2=h2, pre=pre)
    return x2


def _layer_bwd_ffn(dx, sv, big, ts):
    p = sv["p"]
    dx1, dpre, dg2, dx_bf, sv["dx1_bf"] = _ffn_bwd_call(dx, sv["pre"], sv["x1"], p["g2"], big["w_up"],
                                                       big["w_down"], 0, ts)
    tk = dx.shape[0]
    gb = dict(
        w_down=_wgrad_call(sv["pre"], dx_bf, p["core"], Q_FF, D_MODEL, tk, False, "wgrad_down", a_fn=_relu_sq),
        w_up=_wgrad_call(sv["h2"], dpre, p["core"], D_MODEL, Q_FF, tk, True, "wgrad_up"))
    return dx1, gb, dict(norm_ffn_g=dg2[0])


def _layer_bwd_merge(dx1, sv, big, ts, after=None):
    tk = dx1.shape[0]
    core = sv["p"]["core"]
    dya, dyb, dgate, dya_pre, dyb_pre = _merge_bwd_call(
        dx1, sv["proj"], sv["ya"], sv["yb"], big["w_branch_a"], big["w_branch_b"], big["w_out"], 0, ts, after)
    gb = dict(
        w_out=_wgrad_call(sv["merged"], sv["dx1_bf"], core, D_MODEL, D_MODEL, tk, False, "wgrad_out"),
        w_branch_a=_wgrad_call(sv["ya_pre"], dya, core, D_RNN, D_MODEL // 2, tk, False, "wgrad_branch_a"),
        w_branch_b=_wgrad_call(sv["yb_pre"], dyb, core, D_SGU, D_MODEL, tk, False, "wgrad_branch_b"))
    return (dgate, dya_pre, dyb_pre), gb


def _layer_bwd_branches(dx1, merge_out, sv, big, lam, ts, after_sgu=None):
    p = sv["p"]
    tk = dx1.shape[0]
    dgate, dya_pre, dyb_pre = merge_out
    gb = {}
    duv, dws, dbs, dlg, dlb = _sgu_bwd_call(dyb_pre, sv["proj"], p["wm"], p["bsb"], _sgu_mask(), p["lg"], p["lb"],
                                            ts)
    ba = p["ba"] if after_sgu is None else p["ba"] + after_sgu(duv)
    dxg, dwa, dwx, vec = _rnn_bwd_call(dya_pre, sv["proj"], sv["xr"], sv["hr"], p["wa"], p["wx"], ba, p["bx"],
                                       p["sp"], p["cw"], ts // 2)
    dx, dproj, dg1 = _inproj_bwd_call(dxg, duv, dgate, dx1, sv["x"], p["g1"], big["w_in"], 0, ts)
    gb["w_in"] = _wgrad_call(sv["h"], dproj, p["core"], D_MODEL // 2, Q_IN, tk, True, "wgrad_in")
    gs = dict(
        norm_mix_g=dg1[0], conv_w=vec[_ROW_DCW:_ROW_DCW + CONV_WIDTH], conv_b=vec[_ROW_DCB],
        lru_w_a=_block_diag_extract(dwa), lru_w_x=_block_diag_extract(dwx),
        lru_b_a=vec[_ROW_DBA].reshape(RNN_HEADS, RNN_HEAD_DIM), lru_b_x=vec[_ROW_DBX].reshape(RNN_HEADS, RNN_HEAD_DIM),
        lru_lambda=-vec[_ROW_DSP] * jax.nn.sigmoid(-lam),
        sgu_ln_g=dlg[0], sgu_ln_b=dlb[0], sgu_w_s=dws, sgu_b_s=dbs.T)
    return dx, gb, gs


def _local_step(x, target, big, sm, ts):
    saved = []
    core = jnp.zeros((1,), jnp.int32)
    for l in range(DEPTH):
        sv = _layer_fwd_mix(x, big[l], _layer_small(sm, l, core), ts)
        x = _layer_fwd_out(sv, big[l], ts)
        saved.append(sv)
    dx, loss, dgf = _loss_call(x, target, sm["final_norm_g"].reshape(1, -1), ts)
    gb, gs = [None] * DEPTH, [None] * DEPTH
    for l in reversed(range(DEPTH)):
        dx1, gb_ffn, gs_ffn = _layer_bwd_ffn(dx, saved[l], big[l], ts)
        merge_out, gb_merge = _layer_bwd_merge(dx1, saved[l], big[l], ts)
        dx, gb_mix, gs_mix = _layer_bwd_branches(dx1, merge_out, saved[l], big[l], sm["lru_lambda"][l], ts)
        gb[l] = {**gb_ffn, **gb_merge, **gb_mix}
        gs[l] = {**gs_ffn, **gs_mix}
    gs = {k: jnp.stack([g[k] for g in gs]) for k in gs[0]}
    gs["final_norm_g"] = dgf[0]
    return loss, dx, gb, gs


EW_VMEM_BYTES = 24 * 1024 * 1024


def _row_block(rows, cols, bytes_per_elem):
    for br in range(min(rows, EW_VMEM_BYTES // (2 * bytes_per_elem * cols)), 0, -1):
        if rows % br == 0 and br % 16 == 0:
            return br
    return rows


def _ew_call(fn, name, operands, outputs, slabs=1, sel=None, into=None, after=None):
    if into is not None and not isinstance(into, (list, tuple)):
        into = [into]
    rows, cols = outputs[0][0].shape[2:]
    br = _row_block(rows, cols, sum(jnp.dtype(a.dtype).itemsize for a, _ in operands + outputs))
    n_in = len(operands)

    def pick(tok, g, s):
        if callable(tok):
            return tok(g, s)
        if tok == "g":
            return g
        if isinstance(tok, tuple):
            return s[tok[1]]
        return tok

    def spec(idx):
        return pl.BlockSpec((None, None, br, cols),
                            lambda g, i, s, idx=idx: (pick(idx[0], g, s), pick(idx[1], g, s), i, 0))

    if sel is None:
        sel = jnp.zeros((1,), jnp.int32)
    in_specs = [spec(idx) for _, idx in operands]
    arrays = [a for a, _ in operands]
    aliases = {}
    for j, buf in enumerate(into or ()):
        in_specs.append(pl.BlockSpec(memory_space=pl.ANY))
        arrays.append(buf)
        aliases[1 + n_in + j] = j
    if after is not None:
        in_specs.append(pl.BlockSpec(memory_space=pl.ANY))
        arrays.append(after)

    def body(sel_ref, *refs):
        outs = fn(*[r[...] for r in refs[:n_in]])
        for o_ref, o in zip(refs[len(arrays):], outs):
            o_ref[...] = o.astype(o_ref.dtype)

    return pl.pallas_call(
        body, name=name, out_shape=[s for s, _ in outputs],
        grid_spec=pltpu.PrefetchScalarGridSpec(
            num_scalar_prefetch=1, grid=(slabs, rows // br),
            in_specs=in_specs,
            out_specs=[spec(idx) for _, idx in outputs]),
        input_output_aliases=aliases,
        compiler_params=_params(("parallel", "parallel")),
    )(sel, *arrays)


def _as4(a):
    return a.reshape((1,) * (4 - a.ndim) + a.shape)


def _adamw(w, g, m, v):
    m = ADAM_B1 * m + (1.0 - ADAM_B1) * g
    v = ADAM_B2 * v + (1.0 - ADAM_B2) * jnp.square(g)
    m_hat = m / (1.0 - ADAM_B1 ** ADAM_STEP)
    v_hat = v / (1.0 - ADAM_B2 ** ADAM_STEP)
    delta = -ADAM_LR * (m_hat / (jnp.sqrt(v_hat) + ADAM_EPS) + ADAM_WD * w)
    return delta, m, v


def _small_adamw_call(ws, gs, ms, vs):
    n = len(ws)

    def body(*refs):
        for k in range(n):
            w, g, m, v = (refs[j * n + k][...] for j in range(4))
            outs = _adamw(w, g, m, v)
            for j in range(3):
                refs[(4 + j) * n + k][...] = outs[j]

    shapes = [jax.ShapeDtypeStruct(w.shape, F32) for w in ws]
    outs = pl.pallas_call(
        body, name="adamw_small", out_shape=shapes * 3,
        in_specs=[pl.BlockSpec(memory_space=pltpu.VMEM)] * (4 * n),
        out_specs=[pl.BlockSpec(memory_space=pltpu.VMEM)] * (3 * n),
        compiler_params=_params(),
    )(*ws, *gs, *ms, *vs)
    return outs[:n], outs[n:2 * n], outs[2 * n:]


ANY = pl.BlockSpec(memory_space=pl.ANY)


def _place():
    x, y, c = lax.axis_index("x"), lax.axis_index("y"), lax.axis_index("c")
    chips = [(1 - x, y), (x, 1 - y), (1 - x, 1 - y)]
    return x, y, c, chips


def _remote(src, dst, send_sem, recv_sem, to):
    return pltpu.make_async_remote_copy(src_ref=src, dst_ref=dst, send_sem=send_sem, recv_sem=recv_sem,
                                        device_id=to, device_id_type=MESH)


def _sibling_send_call(items):
    n = len(items)

    def body(*refs):
        src, out = refs[:n], refs[n:2 * n]
        send_sems, recv_sems = refs[2 * n:]
        x, y, c, _ = _place()
        copies = [_remote(src[w], out[w], send_sems.at[w], recv_sems.at[w], (x, y, 1 - c)) for w in range(n)]
        for cp in copies:
            cp.start()
        for cp in copies:
            cp.wait()

    return pl.pallas_call(
        body, name="grads_to_sibling",
        out_shape=[jax.ShapeDtypeStruct(a.shape, a.dtype) for a in items],
        in_specs=[ANY] * n, out_specs=[ANY] * n,
        scratch_shapes=[pltpu.SemaphoreType.DMA((n,)), pltpu.SemaphoreType.DMA((n,))],
        compiler_params=_params(vmem=False, has_side_effects=True),
    )(*items)


def _sibling_inplace_call(name, bufs, slabs, n_pairs):
    n = len(bufs)

    def body(*refs):
        out = refs[n:2 * n]
        send_sems, recv_sems = refs[2 * n:]
        x, y, c, _ = _place()
        sibling = (x, y, 1 - c)
        pairs = [pair for w, ref in enumerate(out) for pair in slabs(ref, c, w)]
        sends = [_remote(s, s, send_sems.at[k], recv_sems.at[k], sibling) for k, (s, _) in enumerate(pairs)]
        for cp in sends:
            cp.start()
        for k, (_, r) in enumerate(pairs):
            _remote(r, r, send_sems.at[k], recv_sems.at[k], sibling).wait_recv()
        for cp in sends:
            cp.wait_send()

    return pl.pallas_call(
        body, name=name,
        out_shape=[jax.ShapeDtypeStruct(a.shape, a.dtype) for a in bufs],
        in_specs=[ANY] * n, out_specs=[ANY] * n,
        input_output_aliases={w: w for w in range(n)},
        scratch_shapes=[pltpu.SemaphoreType.DMA((n_pairs,)), pltpu.SemaphoreType.DMA((n_pairs,))],
        compiler_params=_params(vmem=False, has_side_effects=True),
    )(*bufs)


HBM_SPEC = pl.BlockSpec(memory_space=pltpu.HBM)
SEM_SPEC = pl.BlockSpec(memory_space=pltpu.SEMAPHORE)
DATAFLOW_EFFECT = pltpu.SideEffectType.DATAFLOW_SIDE_EFFECTING


def _exchange_start(name, bufs, copies, n_copies, after):
    n = len(bufs)

    def body(*refs):
        ins, send_sems, recv_sems, token = refs[:n], refs[n + 1], refs[n + 2], refs[-1]
        for k, (src, dst, to) in enumerate(copies(ins)):
            _remote(src, dst, send_sems.at[k], recv_sems.at[k], to).start()
        token[...] = jnp.zeros_like(token)

    outs = pl.pallas_call(
        body, name=name,
        out_shape=(pltpu.SemaphoreType.DMA((n_copies,)), pltpu.SemaphoreType.DMA((n_copies,)),
                   *[pltpu.HBM(b.shape, b.dtype) for b in bufs], jax.ShapeDtypeStruct((SUBLANES, 128), F32)),
        in_specs=[HBM_SPEC] * n + [ANY],
        out_specs=(SEM_SPEC, SEM_SPEC, *[HBM_SPEC] * n, pl.BlockSpec(memory_space=pltpu.VMEM)),
        input_output_aliases={w: w + 2 for w in range(n)},
        compiler_params=pltpu.CompilerParams(has_side_effects=DATAFLOW_EFFECT),
    )(*[pltpu.with_memory_space_constraint(b, pltpu.HBM) for b in bufs], after)
    return outs[0], outs[1], list(outs[2:2 + n]), outs[-1]


def _exchange_wait(name, send_sems, recv_sems, bufs, copies, after):
    n = len(bufs)

    def body(*refs):
        ins, send_sems, recv_sems = refs[:n], refs[n], refs[n + 1]
        for k, (src, dst, to) in enumerate(copies(ins)):
            cp = _remote(src, dst, send_sems.at[k], recv_sems.at[k], to)
            cp.wait_send()
            cp.wait_recv()

    return pl.pallas_call(
        body, name=name,
        out_shape=[pltpu.HBM(b.shape, b.dtype) for b in bufs],
        in_specs=[HBM_SPEC] * n + [SEM_SPEC, SEM_SPEC, ANY],
        out_specs=[HBM_SPEC] * n,
        input_output_aliases={w: w for w in range(n)},
        compiler_params=pltpu.CompilerParams(has_side_effects=DATAFLOW_EFFECT),
    )(*bufs, send_sems, recv_sems, after)


def _gather_copies(refs):
    x, y, c, chips = _place()
    mine = 2 * (2 * x + y) + c
    return [(ref.at[mine], ref.at[mine], (qx, qy, c)) for ref in refs for qx, qy in chips]


def _forward_copies(refs):
    x, y, c, chips = _place()
    return [(ref.at[2 * (2 * qx + qy) + c], ref.at[2 * (2 * qx + qy) + c], (x, y, 1 - c))
            for ref in refs for qx, qy in chips]


def _gather_forward_slabs(ref, c, w):
    x, y, _, chips = _place()
    return [(ref.at[2 * (2 * qx + qy) + c], ref.at[2 * (2 * qx + qy) + 1 - c]) for qx, qy in chips]


def _device_peers():
    x, y, c, _ = _place()
    return 4 * x + 2 * y + c, [(k, (x ^ ((k >> 2) & 1), y ^ ((k >> 1) & 1), c ^ (k & 1))) for k in range(1, 8)]


def _small_scatter_copies(refs):
    me, peers = _device_peers()
    return [(refs[0].at[me ^ k], refs[1].at[me], to) for k, to in peers]


def _small_spread_copies(refs):
    me, peers = _device_peers()
    return [(refs[0].at[me], refs[0].at[me], to) for _, to in peers]


def _sibling_copies(refs):
    n = len(refs) // 2
    x, y, c, _ = _place()
    return [(refs[w], refs[n + w], (x, y, 1 - c)) for w in range(n)]


def _owner_copies(refs):
    n = len(refs) // 2
    x, y, c, chips = _place()
    return [(refs[w].at[2 * qx + qy], refs[n + w].at[j], (qx, qy, c))
            for w in range(n) for j, (qx, qy) in enumerate(chips)]


N_DEVICES = 8
SMALL_ROWS = 616


SMALL = ("norm_mix_g", "conv_w", "conv_b", "lru_w_a", "lru_b_a", "lru_w_x", "lru_b_x", "lru_lambda",
         "sgu_ln_g", "sgu_ln_b", "sgu_w_s", "sgu_b_s", "norm_ffn_g", "final_norm_g")
WEIGHTS = ("norm_mix_g", "w_in", "conv_w", "conv_b", "lru_w_a", "lru_b_a", "lru_w_x", "lru_b_x", "lru_lambda",
           "sgu_ln_g", "sgu_ln_b", "sgu_w_s", "sgu_b_s", "w_branch_a", "w_branch_b", "w_out", "norm_ffn_g",
           "w_up", "w_down", "final_norm_g")
PACK_ALIGN = SUBLANES * 128


PACKED = SMALL + ("loss",)


def _pack_small(gs):
    parts = []
    for k in PACKED:
        flat = gs[k].reshape(-1)
        parts.append(jnp.pad(flat, (0, -flat.size % PACK_ALIGN)))
    flat = jnp.concatenate(parts)
    flat = jnp.pad(flat, (0, N_DEVICES * SMALL_ROWS * 128 - flat.size))
    return flat.reshape(N_DEVICES, SMALL_ROWS, 128)


def _unpack_small(buf, like):
    flat = buf.reshape(-1)
    out, off = {}, 0
    for k in PACKED:
        size = like[k].size
        out[k] = flat[off:off + size].reshape(like[k].shape)
        off += size + (-size % PACK_ALIGN)
    return out


def _as_rows(a):
    return a.reshape(-1, a.shape[-1])


def kernel(x, norm_mix_g, w_in, conv_w, conv_b, lru_w_a, lru_b_a, lru_w_x, lru_b_x, lru_lambda, sgu_ln_g, sgu_ln_b, sgu_w_s, sgu_b_s, w_branch_a, w_branch_b, w_out, norm_ffn_g, w_up, w_down, final_norm_g, loss_target, m_norm_mix_g, m_w_in, m_conv_w, m_conv_b, m_lru_w_a, m_lru_b_a, m_lru_w_x, m_lru_b_x, m_lru_lambda, m_sgu_ln_g, m_sgu_ln_b, m_sgu_w_s, m_sgu_b_s, m_w_branch_a, m_w_branch_b, m_w_out, m_norm_ffn_g, m_w_up, m_w_down, m_final_norm_g, v_norm_mix_g, v_w_in, v_conv_w, v_conv_b, v_lru_w_a, v_lru_b_a, v_lru_w_x, v_lru_b_x, v_lru_lambda, v_sgu_ln_g, v_sgu_ln_b, v_sgu_w_s, v_sgu_b_s, v_w_branch_a, v_w_branch_b, v_w_out, v_norm_ffn_g, v_w_up, v_w_down, v_final_norm_g):
    w = dict(norm_mix_g=norm_mix_g, w_in=w_in, conv_w=conv_w, conv_b=conv_b, lru_w_a=lru_w_a, lru_b_a=lru_b_a,
             lru_w_x=lru_w_x, lru_b_x=lru_b_x, lru_lambda=lru_lambda, sgu_ln_g=sgu_ln_g, sgu_ln_b=sgu_ln_b,
             sgu_w_s=sgu_w_s, sgu_b_s=sgu_b_s, w_branch_a=w_branch_a, w_branch_b=w_branch_b, w_out=w_out,
             norm_ffn_g=norm_ffn_g, w_up=w_up, w_down=w_down, final_norm_g=final_norm_g)
    m = dict(norm_mix_g=m_norm_mix_g, w_in=m_w_in, conv_w=m_conv_w, conv_b=m_conv_b, lru_w_a=m_lru_w_a,
             lru_b_a=m_lru_b_a, lru_w_x=m_lru_w_x, lru_b_x=m_lru_b_x, lru_lambda=m_lru_lambda,
             sgu_ln_g=m_sgu_ln_g, sgu_ln_b=m_sgu_ln_b, sgu_w_s=m_sgu_w_s, sgu_b_s=m_sgu_b_s,
             w_branch_a=m_w_branch_a, w_branch_b=m_w_branch_b, w_out=m_w_out, norm_ffn_g=m_norm_ffn_g,
             w_up=m_w_up, w_down=m_w_down, final_norm_g=m_final_norm_g)
    v = dict(norm_mix_g=v_norm_mix_g, w_in=v_w_in, conv_w=v_conv_w, conv_b=v_conv_b, lru_w_a=v_lru_w_a,
             lru_b_a=v_lru_b_a, lru_w_x=v_lru_w_x, lru_b_x=v_lru_b_x, lru_lambda=v_lru_lambda,
             sgu_ln_g=v_sgu_ln_g, sgu_ln_b=v_sgu_ln_b, sgu_w_s=v_sgu_w_s, sgu_b_s=v_sgu_b_s,
             w_branch_a=v_w_branch_a, w_branch_b=v_w_branch_b, w_out=v_w_out, norm_ffn_g=v_norm_ffn_g,
             w_up=v_w_up, w_down=v_w_down, final_norm_g=v_final_norm_g)
    core = lax.axis_index("c")
    chip = 2 * lax.axis_index("x") + lax.axis_index("y")
    sel = jnp.stack([core, 1 - core, chip, 2 * chip + core]).astype(jnp.int32)
    this_core, this_chip = ("sel", 0), ("sel", 2)
    sds = jax.ShapeDtypeStruct

    ts = TOKEN_TILE

    def after_all(arrays):
        return jnp.stack([a[(0,) * a.ndim].astype(F32) for a in arrays])

    halves = {k:(w[k].shape[1] // 2, w[k].shape[2]) for k in BIG}

    def half_view(k, a):
        return a.reshape((2 * N_QUARTERS,) + halves[k])

    def full_view(k, a):
        if k == "conv_w":
            return a.reshape(N_QUARTERS, DEPTH, CONV_WIDTH, -1).transpose(1, 2, 0, 3).reshape(DEPTH, CONV_WIDTH, D_RNN)
        r2, cols = halves[k]
        if k in ("w_in", "w_up"):
            return a.reshape(1, N_QUARTERS, 2 * r2, cols)
        return a.reshape(1, 2 * N_QUARTERS * r2, cols)

    layer_bufs = [{}, {}]

    def cast_weights(k, after):
        _, r, cols = w[k].shape
        w4 = w[k].reshape(DEPTH, 1, r, cols)
        outs = _ew_call(lambda a, b: (a, b), "cast_weights", [(w4, (0, 0)), (w4, (1, 0))],
                        [(sds((1, N_QUARTERS, r, cols), BF), (0, this_chip))] * DEPTH, 1, sel, after=after)
        for l in range(DEPTH):
            layer_bufs[l][k] = half_view(k, outs[l])

    conv_buf = lax.dynamic_update_slice_in_dim(
        jnp.zeros((N_QUARTERS, DEPTH) + conv_w.shape[1:], F32), conv_w[None], chip, axis=0)
    layer_bufs[0]["conv_w"] = conv_buf.reshape((2 * N_QUARTERS,) + conv_w.shape[1:])
    sm = {k: w[k] for k in SMALL if k != "conv_w"}

    def gather_start(tag, l, keys, after):
        bufs = [layer_bufs[l][k] for k in keys]
        return _exchange_start(f"gather_start_{tag}", bufs, _gather_copies, 3 * len(keys), after)

    def gather_finish(tag, keys, started, after):
        send_sems, recv_sems, thru, _ = started
        landed = _exchange_wait(f"gather_wait_{tag}", send_sems, recv_sems, thru, _gather_copies, after)
        landed = _sibling_inplace_call("gather_forward", landed, _gather_forward_slabs, 3 * len(keys))
        return {k: full_view(k, a) for k, a in zip(keys, landed)}

    first, rest = ("w_in",), tuple(k for k in BIG if k != "w_in")
    cast_weights("w_in", None)
    started_a = gather_start("0a", 0, first + ("conv_w",), sel)
    for k in rest:
        cast_weights(k, started_a[3])
    started_b = gather_start("0b", 0, rest, started_a[3])
    started_c = gather_start("1a", 1, first, started_b[3])
    started_d = gather_start("1b", 1, rest, started_c[3])

    def arrives(tag, keys, started):
        state = {}

        def hook(after):
            landed = _exchange_wait(f"gather_wait_{tag}", started[0], started[1], started[2], _gather_copies, after)
            state["forward"] = _exchange_start(f"forward_start_{tag}", landed, _forward_copies, 3 * len(keys), after)
            return state["forward"][3][0, 0]

        def finish(after):
            send_sems, recv_sems, thru, _ = state["forward"]
            done = _exchange_wait(f"forward_wait_{tag}", send_sems, recv_sems, thru, _forward_copies, after)
            return {k: full_view(k, a) for k, a in zip(keys, done)}

        return hook, finish

    p0, p1 = _layer_small(sm, 0, sel[0:1]), _layer_small(sm, 1, sel[0:1])
    h0 = _norm_call(x[0], p0["g1"], ts)
    proj_own = _inproj_part_call(h0, full_view("w_in", started_a[2][0]), 2 * ts, sel[2:3], 0, 1)
    ready = after_all([started_d[3], proj_own] + [p[k] for p in (p0, p1) for k in ("wa", "wx", "wm")])
    big0 = gather_finish("0a", first + ("conv_w",), started_a, ready)
    for l, p in enumerate((p0, p1)):
        p["cw"] = big0["conv_w"][l]
    proj0 = _inproj_part_call(h0, big0["w_in"], 2 * ts, sel[2:3], 1, N_QUARTERS - 1, proj_own)
    hook, finish = arrives("0b", rest, started_b)
    sv0 = _layer_fwd_mix(x[0], big0, p0, ts, h0, hook, proj0)
    big0.update(finish(sv0["yb_pre"]))
    x_mid = _layer_fwd_out(sv0, big0, ts)
    hook, finish = arrives("1a", first, started_c)
    h1 = _norm_call(x_mid, p1["g1"] + hook(x_mid), ts)
    big1 = finish(h1)
    hook, finish = arrives("1b", rest, started_d)
    sv1 = _layer_fwd_mix(x_mid, big1, p1, ts, h1, hook)
    big1.update(finish(sv1["yb_pre"]))
    x_out = _layer_fwd_out(sv1, big1, ts)
    dx, loss, dgf = _loss_call(x_out, loss_target[0], final_norm_g.reshape(1, -1), ts)

    def pair_start(tag, gb, after):
        sends = [gb[k][1] for k in gb]
        zones = [lax.empty(a.shape, BF) for a in sends]
        return _exchange_start(f"pair_start_{tag}", sends + zones, _sibling_copies, len(sends), after)

    def reduce_start(tag, gb, after, pair=None):
        keys = tuple(gb)
        if pair is None:
            from_sibling = _sibling_send_call([gb[k][1] for k in keys])
        else:
            done = _exchange_wait(f"pair_wait_{tag}", pair[0], pair[1], pair[2], _sibling_copies, after)
            from_sibling = done[len(keys):]
        sums = [
            _ew_call(lambda a, b: (a + b.astype(F32),), "pair_sum", [(gb[k][0][None], (0, "g")), (r[None], (0, "g"))],
                     [(sds((1,) + r.shape, BF), (0, "g"))], N_QUARTERS)[0][0]
            for k, r in zip(keys, from_sibling)]
        zones = [lax.empty((3,) + a.shape[1:], BF) for a in sums]
        started = _exchange_start(f"reduce_start_{tag}", sums + zones, _owner_copies, 3 * len(keys), after)
        return keys, started

    def reduce_finish(tag, l, keys_started, after, reduced):
        keys, (send_sems, recv_sems, thru, _) = keys_started
        done = _exchange_wait(f"reduce_wait_{tag}", send_sems, recv_sems, thru, _owner_copies, after)
        sums, zones = done[:len(keys)], done[len(keys):]
        for i, k in enumerate(keys):
            r2, cols = halves[k]
            reduced[k] = _ew_call(
                lambda a, b, c, d: (((a.astype(F32) + b.astype(F32)) + c.astype(F32)) + d.astype(F32),),
                "quarter_sum", [(sums[i][None], (0, this_chip))] + [(zones[i][None], (0, j)) for j in range(3)],
                [(sds((DEPTH, 2, r2, cols), F32), (l, this_core))], 1, sel, into=reduced.get(k))[0]

    def behind(params, key, started):
        return dict(params, **{key: params[key] + started[1][3][0, 0]})

    dx1, gb_ffn, gs1 = _layer_bwd_ffn(dx, sv1, big1, ts)
    merge_out, gb_merge = _layer_bwd_merge(dx1, sv1, big1, ts)
    dx_mid, gb_in, gs1_mix = _layer_bwd_branches(dx1, merge_out, sv1, big1, lru_lambda[1], ts)
    gb_1 = {**gb_ffn, **gb_merge, **gb_in}
    pair_1 = pair_start("1", gb_1, dx_mid)
    sv0["p"] = behind(sv0["p"], "g2", (None, pair_1))
    dx1, gb_ffn, gs0 = _layer_bwd_ffn(dx_mid, sv0, big0, ts)
    exchange_1 = reduce_start("1", gb_1, dx1, pair_1)
    pair_0a = pair_start("0a", gb_ffn, exchange_1[1][3])
    merge_out, gb_merge = _layer_bwd_merge(dx1, sv0, big0, ts, pair_0a[3])
    exchange_0a = reduce_start("0a", gb_ffn, merge_out[0], pair_0a)
    pair_0b = pair_start("0b", gb_merge, exchange_0a[1][3])
    sv0["p"] = behind(sv0["p"], "lg", (None, pair_0b))
    started_0b = {}

    def after_sgu(duv):
        started_0b["exchange"] = reduce_start("0b", gb_merge, duv, pair_0b)
        return started_0b["exchange"][1][3][0, 0]

    grad_x, gb_in, gs0_mix = _layer_bwd_branches(dx1, merge_out, sv0, big0, lru_lambda[0], ts, after_sgu)
    exchange_0b = started_0b["exchange"]
    exchange_0c = reduce_start("0c", gb_in, exchange_0b[1][3])
    layer_gs = [{**gs0, **gs0_mix}, {**gs1, **gs1_mix}]
    gs = {k: jnp.stack([g[k] for g in layer_gs]) for k in layer_gs[0]}
    gs["final_norm_g"] = dgf[0]
    gs["loss"] = loss[0, 0:1]

    me = ("sel", 3)
    piece = (1, N_DEVICES, SMALL_ROWS, 128)
    packed = _pack_small(gs).reshape(piece)
    scatter = _exchange_start("small_scatter_start", [packed[0], lax.empty(piece[1:], F32)], _small_scatter_copies,
                              N_DEVICES - 1, exchange_0c[1][3])
    reduced = {}
    reduce_finish("1", 1, exchange_1, scatter[3], reduced)
    reduce_finish("0a", 0, exchange_0a, reduced["w_in"], reduced)
    reduce_finish("0b", 0, exchange_0b, reduced["w_down"], reduced)

    def swap_slabs(ref, c, i):
        layers = (1,) if BIG[i] == "w_in" else range(DEPTH)
        return [(ref.at[l, c], ref.at[l, 1 - c]) for l in layers]

    swapped = dict(zip(BIG, _sibling_inplace_call("grads_swap_halves", [reduced[k] for k in BIG], swap_slabs,
                                                  DEPTH * len(BIG) - 1)))

    def adamw_layers(k, grad, layer, into, after=None):
        if layer is None:
            views = [_as4(_as_rows(a)) for a in (w[k], grad, m[k], v[k])]
            idx = (0, 0)
        else:
            views = [a.reshape((1,) + w[k].shape) for a in (w[k], grad, m[k], v[k])]
            idx = (0, layer)
        return _ew_call(_adamw, "adamw_big", [(a, idx) for a in views], [(sds(views[0].shape, F32), idx)] * 3,
                        into=into, after=after)

    updated, last_update = {}, None
    for k in BIG:
        updated[k] = adamw_layers(k, swapped[k], 1 if k == "w_in" else None, None, last_update)
        last_update = updated[k][0]
    scattered = _exchange_wait("small_scatter_wait", scatter[0], scatter[1], scatter[2], _small_scatter_copies,
                               last_update)
    summed = _ew_call(
        lambda *parts: (functools.reduce(lambda a, b: a + b, parts),), "small_sum",
        [(scattered[0][None], (0, me))]
        + [(scattered[1][None], (0, lambda g, s, k=k: s[3] ^ k)) for k in range(1, N_DEVICES)],
        [(sds(piece, F32), (0, me))], 1, sel)[0]
    spread = _exchange_start("small_spread_start", [summed[0]], _small_spread_copies, N_DEVICES - 1, summed)
    reduced["w_in"] = swapped["w_in"]
    reduce_finish("0c", 0, exchange_0c, spread[3], reduced)
    last = _sibling_inplace_call("grads_swap_last", [reduced["w_in"]],
                                 lambda ref, c, i: [(ref.at[0, c], ref.at[0, 1 - c])], 1)[0]
    swapped["w_in"] = last
    updated["w_in"] = adamw_layers("w_in", last, 0, updated["w_in"])
    grads_big = {k: swapped[k].reshape(w[k].shape) for k in BIG}
    delta, new_m, new_v = ({k: updated[k][j].reshape(w[k].shape) for k in BIG} for j in range(3))
    gathered_small = _exchange_wait("small_spread_wait", spread[0], spread[1], spread[2], _small_spread_copies,
                                    updated["w_in"][0])[0]

    like = {k: jax.ShapeDtypeStruct(gs[k].shape, F32) for k in SMALL}
    like["loss"] = jax.ShapeDtypeStruct((1,), F32)
    grads_small = _unpack_small(gathered_small, like)
    total = grads_small.pop("loss")[0]
    conv_q = grads_small["conv_w"].reshape(DEPTH, CONV_WIDTH, N_QUARTERS, D_RNN // N_QUARTERS)
    grads_small["conv_w"] = lax.dynamic_index_in_dim(conv_q, chip, axis=2, keepdims=False)
    outs = _small_adamw_call(*[[_as_rows(d[k]) for k in SMALL] for d in (w, grads_small, m, v)])
    for d, o in zip((delta, new_m, new_v), outs):
        for k, a in zip(SMALL, o):
            d[k] = a.reshape(w[k].shape)

    grads = {**grads_big, **grads_small}
    return (total, grad_x[None], *[grads[k] for k in WEIGHTS], *[delta[k] for k in WEIGHTS],
            *[new_m[k] for k in WEIGHTS], *[new_v[k] for k in WEIGHTS])
```

```python
import functools
import math

import jax
import jax.numpy as jnp
from jax import lax
from jax.experimental import pallas as pl
from jax.experimental.pallas import tpu as pltpu

F32 = jnp.float32
BF = jnp.bfloat16

DEPTH = 2
D_MODEL = 1024
D_RNN = 1280
D_SGU = 1024
D_FF = 4096
D_IN = 2 * D_RNN + 2 * D_SGU + 2 * D_MODEL
N_QUARTERS = 4
Q_IN = D_IN // N_QUARTERS
Q_FF = D_FF // N_QUARTERS
RNN_HEADS = 20
RNN_HEAD_DIM = 64
LRU_GROUP = 256
N_LRU_GROUPS = D_RNN // LRU_GROUP
HEADS_PER_GROUP = LRU_GROUP // RNN_HEAD_DIM
CONV_WIDTH = 4
LRU_C = 8.0
SGU_GROUPS = 8
SGU_BLOCK = 128
CHUNK = 64
EPS = 1e-6

ADAM_LR = 0.001
ADAM_B1 = 0.9
ADAM_B2 = 0.999
ADAM_EPS = 1e-08
ADAM_WD = 0.01
ADAM_STEP = 10

SUBLANES = 8
TOKEN_TILE = 512
VMEM_LIMIT_BYTES = 56 * 1024 * 1024

MESH = pl.DeviceIdType.MESH


def _params(semantics=None, vmem=True, **kw):
    return pltpu.CompilerParams(
        dimension_semantics=semantics,
        vmem_limit_bytes=VMEM_LIMIT_BYTES if vmem else None,
        **kw,
    )


def _dot(a, b):
    return jnp.dot(a, b, preferred_element_type=F32)


def _dot_nt(a, b):
    return lax.dot_general(a, b, (((1,), (1,)), ((), ())), preferred_element_type=F32)


def _dot_tn(a, b):
    return lax.dot_general(a, b, (((0,), (0,)), ((), ())), preferred_element_type=F32)


_GELU_C = math.sqrt(2.0 / math.pi)
_GELU_A = 0.044715


def _gelu(x):
    return 0.5 * x * (1.0 + jnp.tanh(_GELU_C * (x + _GELU_A * x * x * x)))


def _gelu_and_grad(x):
    x2 = x * x
    t = jnp.tanh(_GELU_C * (x + _GELU_A * x2 * x))
    du = _GELU_C * (1.0 + 3.0 * _GELU_A * x2)
    return 0.5 * x * (1.0 + t), 0.5 * (1.0 + t) + 0.5 * x * (1.0 - t * t) * du


def _rms_stats(x):
    return lax.rsqrt(jnp.mean(x * x, axis=-1, keepdims=True) + EPS)


def _rms_bwd(dy, x, g):
    rs = _rms_stats(x)
    n = x * rs
    dn = dy * g
    dx = rs * (dn - n * jnp.mean(dn * n, axis=-1, keepdims=True))
    return dx, dy * n


def _row_sum(x):
    return jnp.sum(x, axis=0, keepdims=True)


def _tile_spec(ts, width, col=0):
    return pl.BlockSpec((ts, width), lambda i, col=col: (i, col))


def _full_spec(shape):
    zeros = (0,) * len(shape)
    return pl.BlockSpec(shape, lambda *_: zeros)


def _layer_spec(w, layer):
    zeros = (0,) * (w.ndim - 1)
    return pl.BlockSpec((None,) + tuple(w.shape[1:]), lambda *_: (layer,) + zeros)


def _norm_call(x, g, ts):
    s = x.shape[0]

    def body(x_ref, g_ref, h_ref):
        xv = x_ref[...]
        h_ref[...] = (xv * _rms_stats(xv) * g_ref[...]).astype(BF)

    return pl.pallas_call(
        body, name="norm_fwd", grid=(s // ts,),
        in_specs=[_tile_spec(ts, D_MODEL), _full_spec((1, D_MODEL))],
        out_specs=_tile_spec(ts, D_MODEL),
        out_shape=jax.ShapeDtypeStruct((s, D_MODEL), BF),
        compiler_params=_params(("parallel",)),
    )(x, g)


def _inproj_call(h, w_in, layer, ts):
    s = h.shape[0]

    def body(h_ref, w_ref, o_ref):
        o_ref[...] = _dot(h_ref[...], w_ref[...]).astype(BF)

    return pl.pallas_call(
        body, name="inproj_fwd", grid=(N_QUARTERS, s // ts),
        in_specs=[
            pl.BlockSpec((ts, D_MODEL), lambda q, i: (i, 0)),
            pl.BlockSpec((None, None, D_MODEL, Q_IN), lambda q, i: (layer, q, 0, 0)),
        ],
        out_specs=pl.BlockSpec((ts, Q_IN), lambda q, i: (i, q)),
        out_shape=jax.ShapeDtypeStruct((s, D_IN), BF),
        compiler_params=_params(("parallel", "parallel")),
    )(h, w_in)


def _inproj_part_call(h, w_in, ts, own, first, count, into=None):
    s = h.shape[0]

    def quarter(j, sel):
        return (sel[0] + first + j) % N_QUARTERS

    def body(sel_ref, h_ref, w_ref, *rest):
        rest[-1][...] = _dot(h_ref[...], w_ref[...]).astype(BF)

    in_specs = [pl.BlockSpec((ts, D_MODEL), lambda j, i, sel: (i, 0)),
                pl.BlockSpec((None, None, D_MODEL, Q_IN), lambda j, i, sel: (0, quarter(j, sel), 0, 0))]
    operands = [h, w_in]
    aliases = {}
    if into is not None:
        in_specs.append(pl.BlockSpec(memory_space=pl.ANY))
        operands.append(into)
        aliases = {3: 0}
    return pl.pallas_call(
        body, name="inproj_fwd_part", out_shape=jax.ShapeDtypeStruct((s, D_IN), BF),
        grid_spec=pltpu.PrefetchScalarGridSpec(
            num_scalar_prefetch=1, grid=(count, s // ts), in_specs=in_specs,
            out_specs=pl.BlockSpec((ts, Q_IN), lambda j, i, sel: (i, quarter(j, sel)))),
        input_output_aliases=aliases,
        compiler_params=_params(("parallel", "parallel")),
    )(own, *operands)


def _shift_down(x, tail, s):
    xr = pltpu.roll(x, s, 0)
    tr = pltpu.roll(tail, s, 0)
    row = lax.broadcasted_iota(jnp.int32, tail.shape, 0)
    top = jnp.where(row < s, tr, xr[0:SUBLANES])
    return jnp.concatenate([top, xr[SUBLANES:]], axis=0)


def _shift_up(x, head, s):
    t = x.shape[0]
    xr = pltpu.roll(x, t - s, 0)
    hr = pltpu.roll(head, SUBLANES - s, 0)
    row = lax.broadcasted_iota(jnp.int32, head.shape, 0)
    bottom = jnp.where(row >= SUBLANES - s, hr, xr[t - SUBLANES:])
    return jnp.concatenate([xr[: t - SUBLANES], bottom], axis=0)


def _conv_fwd(x, tail, cw_ref, cb_ref):
    out = cb_ref[...] + cw_ref[CONV_WIDTH - 1:CONV_WIDTH, :] * x
    for s in range(1, CONV_WIDTH):
        k = CONV_WIDTH - 1 - s
        out = out + cw_ref[k:k + 1, :] * _shift_down(x, tail, s)
    return out


def _group_dot(x_bf, w_ref, dot):
    cols = [dot(x_bf[:, g * LRU_GROUP:(g + 1) * LRU_GROUP], w_ref[g]) for g in range(N_LRU_GROUPS)]
    return jnp.concatenate(cols, axis=1)


def _lru_gates(xr, wa_ref, wx_ref, ba_ref, bx_ref, sp_ref):
    xb = xr.astype(BF)
    r = jax.nn.sigmoid(_group_dot(xb, wa_ref, _dot) + ba_ref[...])
    i = jax.nn.sigmoid(_group_dot(xb, wx_ref, _dot) + bx_ref[...])
    log_a = (-LRU_C * r) * sp_ref[...]
    a = jnp.exp(log_a)
    nrm2 = -jnp.tanh(log_a) * (a * a + 1.0)
    inv_nrm = lax.rsqrt(jnp.maximum(nrm2, 1e-36))
    return r, i, a, nrm2 * inv_nrm, inv_nrm


def _linear_scan(a, b, carry, al_ref, bl_ref, h_ref, reverse):
    t, c = a.shape
    rowm = lax.broadcasted_iota(jnp.int32, (t, c), 0) & (SUBLANES - 1)
    for d in (1, 2, 4):
        if reverse:
            keep, sh = rowm < SUBLANES - d, t - d
        else:
            keep, sh = rowm >= d, d
        a_sh = jnp.where(keep, pltpu.roll(a, sh, 0), 1.0)
        b_sh = jnp.where(keep, pltpu.roll(b, sh, 0), 0.0)
        b = a * b_sh + b
        a = a * a_sh
    al_ref[...] = a
    bl_ref[...] = b
    groups = t // SUBLANES

    def step(j, state):
        jj = groups - 1 - j if reverse else j
        off = pl.multiple_of(jj * SUBLANES, SUBLANES)
        rows = bl_ref[pl.ds(off, SUBLANES), :] + al_ref[pl.ds(off, SUBLANES), :] * state
        h_ref[pl.ds(off, SUBLANES), :] = rows
        last = rows[0:1, :] if reverse else rows[SUBLANES - 1:SUBLANES, :]
        return jnp.broadcast_to(last, (SUBLANES, c))

    out = lax.fori_loop(0, groups, step, jnp.broadcast_to(carry, (SUBLANES, c)))
    return out[0:1, :]


def _rnn_fwd_call(proj, wa, wx, ba, bx, sp, cw, cb, ts):
    s = proj.shape[0]

    def body(xg_ref, wa_ref, wx_ref, ba_ref, bx_ref, sp_ref, cw_ref, cb_ref, xr_ref, hr_ref, ya_ref,
             tail_sc, carry_sc, al_sc, bl_sc, h_sc):
        @pl.when(pl.program_id(0) == 0)
        def _():
            tail_sc[...] = jnp.zeros_like(tail_sc)
            carry_sc[...] = jnp.zeros_like(carry_sc)

        x = xg_ref[:, :D_RNN].astype(F32)
        g = xg_ref[:, D_RNN:]
        xr = _conv_fwd(x, tail_sc[...], cw_ref, cb_ref)
        tail_sc[...] = x[ts - SUBLANES:, :]
        xr_ref[...] = xr.astype(BF)
        _, i, a, nrm, _ = _lru_gates(xr, wa_ref, wx_ref, ba_ref, bx_ref, sp_ref)
        carry_sc[...] = _linear_scan(a, nrm * (i * xr), carry_sc[...], al_sc, bl_sc, h_sc, False)
        h = h_sc[...]
        hr_ref[...] = h.astype(BF)
        ya_ref[...] = (h * _gelu(g)).astype(BF)

    gw = (N_LRU_GROUPS, LRU_GROUP, LRU_GROUP)
    return pl.pallas_call(
        body, name="rnn_fwd", grid=(s // ts,),
        in_specs=[_tile_spec(ts, 2 * D_RNN), _full_spec(gw), _full_spec(gw),
                  _full_spec((1, D_RNN)), _full_spec((1, D_RNN)), _full_spec((1, D_RNN)),
                  _full_spec((CONV_WIDTH, D_RNN)), _full_spec((1, D_RNN))],
        out_specs=[_tile_spec(ts, D_RNN)] * 3,
        out_shape=[jax.ShapeDtypeStruct((s, D_RNN), BF)] * 3,
        scratch_shapes=[pltpu.VMEM((SUBLANES, D_RNN), F32), pltpu.VMEM((1, D_RNN), F32),
                        pltpu.VMEM((ts, D_RNN), F32), pltpu.VMEM((ts, D_RNN), F32),
                        pltpu.VMEM((ts, D_RNN), F32)],
        compiler_params=_params(("arbitrary",)),
    )(proj, wa, wx, ba, bx, sp, cw, cb)


def _layernorm_fwd(x):
    mu = jnp.mean(x, axis=-1, keepdims=True)
    xc = x - mu
    rstd = lax.rsqrt(jnp.mean(xc * xc, axis=-1, keepdims=True) + EPS)
    return xc * rstd, rstd


def _sgu_mix(vn_bf, wm_ref, bsb_ref, ts):
    rows = []
    for blk in range(ts // SGU_BLOCK):
        r0 = blk * SGU_BLOCK
        cols = [
            _dot(wm_ref[g], vn_bf[r0:r0 + SGU_BLOCK, g * SGU_BLOCK:(g + 1) * SGU_BLOCK]) + bsb_ref[g]
            for g in range(SGU_GROUPS)
        ]
        rows.append(jnp.concatenate(cols, axis=1))
    return jnp.concatenate(rows, axis=0)


def _sgu_fwd_call(proj, wm, bsb, lg, lb, ts):
    s = proj.shape[0]

    def body(uv_ref, wm_ref, bsb_ref, lg_ref, lb_ref, yb_ref):
        gu = _gelu(uv_ref[:, :D_SGU])
        gv = _gelu(uv_ref[:, D_SGU:2 * D_SGU]).astype(F32)
        nh, _ = _layernorm_fwd(gv)
        vn = (nh * lg_ref[...] + lb_ref[...]).astype(BF)
        yb_ref[...] = (gu * _sgu_mix(vn, wm_ref, bsb_ref, ts)).astype(BF)

    sw = (SGU_GROUPS, SGU_BLOCK, SGU_BLOCK)
    return pl.pallas_call(
        body, name="sgu_fwd", grid=(s // ts,),
        in_specs=[_tile_spec(ts, 2 * D_RNN, 1), _full_spec(sw), _full_spec(sw),
                  _full_spec((1, D_SGU)), _full_spec((1, D_SGU))],
        out_specs=_tile_spec(ts, D_SGU),
        out_shape=jax.ShapeDtypeStruct((s, D_SGU), BF),
        compiler_params=_params(("parallel",)),
    )(proj, wm, bsb, lg, lb)


_GATE_COL0 = (2 * D_RNN + 2 * D_SGU) // 512


def _gate_specs(ts):
    return [_tile_spec(ts, 512, _GATE_COL0 + j) for j in range(4)]


def _merge_call(x, proj, ya_pre, yb_pre, w_ba, w_bb, w_out, g2, layer, ts):
    s = x.shape[0]

    def body(x_ref, ga0, ga1, gb0, gb1, ya_ref, yb_ref, wa_ref, wb_ref, wo_ref, g2_ref,
             x1_ref, yao_ref, ybo_ref, mg_ref, h2_ref):
        ya = _dot(ya_ref[...], wa_ref[...])
        yb = _dot(yb_ref[...], wb_ref[...])
        sa = jax.nn.sigmoid(jnp.concatenate([ga0[...], ga1[...]], axis=1).astype(F32))
        sb = jax.nn.sigmoid(jnp.concatenate([gb0[...], gb1[...]], axis=1).astype(F32))
        merged = (sa * ya + sb * yb).astype(BF)
        x1 = x_ref[...] + _dot(merged, wo_ref[...])
        x1_ref[...] = x1
        yao_ref[...] = ya.astype(BF)
        ybo_ref[...] = yb.astype(BF)
        mg_ref[...] = merged
        h2_ref[...] = (x1 * _rms_stats(x1) * g2_ref[...]).astype(BF)

    act = jax.ShapeDtypeStruct((s, D_MODEL), BF)
    return pl.pallas_call(
        body, name="merge_fwd", grid=(s // ts,),
        in_specs=[_tile_spec(ts, D_MODEL)] + _gate_specs(ts) + [
            _tile_spec(ts, D_RNN), _tile_spec(ts, D_SGU),
            _layer_spec(w_ba, layer), _layer_spec(w_bb, layer), _layer_spec(w_out, layer),
            _full_spec((1, D_MODEL))],
        out_specs=[_tile_spec(ts, D_MODEL)] * 5,
        out_shape=[jax.ShapeDtypeStruct((s, D_MODEL), F32), act, act, act, act],
        compiler_params=_params(("parallel",)),
    )(x, proj, proj, proj, proj, ya_pre, yb_pre, w_ba, w_bb, w_out, g2)


def _ffn_call(x1, h2, w_up, w_down, layer, ts):
    s = x1.shape[0]

    def body(x1_ref, h2_ref, wu_ref, wd_ref, x2_ref, p_ref):
        h2v = h2_ref[...]
        acc = x1_ref[...]
        for q in range(N_QUARTERS):
            p = _dot(h2v, wu_ref[q])
            p_ref[:, q * Q_FF:(q + 1) * Q_FF] = p.astype(BF)
            f = jnp.square(jnp.maximum(p, 0.0)).astype(BF)
            acc = acc + _dot(f, wd_ref[q * Q_FF:(q + 1) * Q_FF, :])
        x2_ref[...] = acc

    return pl.pallas_call(
        body, name="ffn_fwd", grid=(s // ts,),
        in_specs=[_tile_spec(ts, D_MODEL), _tile_spec(ts, D_MODEL),
                  pl.BlockSpec((None, N_QUARTERS, D_MODEL, Q_FF), lambda i: (layer, 0, 0, 0)),
                  pl.BlockSpec((None, D_FF, D_MODEL), lambda i: (layer, 0, 0))],
        out_specs=[_tile_spec(ts, D_MODEL), _tile_spec(ts, D_FF)],
        out_shape=[jax.ShapeDtypeStruct((s, D_MODEL), F32), jax.ShapeDtypeStruct((s, D_FF), BF)],
        compiler_params=_params(("parallel",)),
    )(x1, h2, w_up, w_down)


def _loss_call(x, target, gf, ts):
    s = x.shape[0]

    def body(x_ref, t_ref, g_ref, dx_ref, loss_ref, dg_ref):
        @pl.when(pl.program_id(0) == 0)
        def _():
            loss_ref[...] = jnp.zeros_like(loss_ref)
            dg_ref[...] = jnp.zeros_like(dg_ref)

        xv = x_ref[...]
        gv = g_ref[...]
        err = xv * _rms_stats(xv) * gv - t_ref[...]
        part = 0.5 * jnp.sum(jnp.mean(err * err, axis=-1, keepdims=True), axis=0, keepdims=True)
        loss_ref[...] += jnp.broadcast_to(part, loss_ref.shape)
        dx, dg = _rms_bwd(err * (1.0 / D_MODEL), xv, gv)
        dx_ref[...] = dx
        dg_ref[...] += _row_sum(dg)

    return pl.pallas_call(
        body, name="loss_head", grid=(s // ts,),
        in_specs=[_tile_spec(ts, D_MODEL), _tile_spec(ts, D_MODEL), _full_spec((1, D_MODEL))],
        out_specs=[_tile_spec(ts, D_MODEL), _full_spec((1, 128)), _full_spec((1, D_MODEL))],
        out_shape=[jax.ShapeDtypeStruct((s, D_MODEL), F32), jax.ShapeDtypeStruct((1, 128), F32),
                   jax.ShapeDtypeStruct((1, D_MODEL), F32)],
        compiler_params=_params(("arbitrary",)),
    )(x, target, gf)


def _ffn_bwd_call(dx2, p, x1, g2, w_up, w_down, layer, ts):
    s = dx2.shape[0]

    def body(dx2_ref, p_ref, x1_ref, g2_ref, wu_ref, wd_ref, dx1_ref, dp_ref, dg_ref, dx2b_ref, dx1b_ref):
        @pl.when(pl.program_id(0) == 0)
        def _():
            dg_ref[...] = jnp.zeros_like(dg_ref)

        dx2v = dx2_ref[...]
        dyb = dx2v.astype(BF)
        dx2b_ref[...] = dyb
        dh2 = jnp.zeros((ts, D_MODEL), F32)
        for q in range(N_QUARTERS):
            cols = slice(q * Q_FF, (q + 1) * Q_FF)
            df = _dot_nt(dyb, wd_ref[cols, :])
            dp = (df * (2.0 * jnp.maximum(p_ref[:, cols].astype(F32), 0.0))).astype(BF)
            dp_ref[:, cols] = dp
            dh2 = dh2 + _dot_nt(dp, wu_ref[q])
        dx, dg = _rms_bwd(dh2, x1_ref[...], g2_ref[...])
        dx1 = dx2v + dx
        dx1_ref[...] = dx1
        dx1b_ref[...] = dx1.astype(BF)
        dg_ref[...] += _row_sum(dg)

    return pl.pallas_call(
        body, name="ffn_bwd", grid=(s // ts,),
        in_specs=[_tile_spec(ts, D_MODEL), _tile_spec(ts, D_FF), _tile_spec(ts, D_MODEL),
                  _full_spec((1, D_MODEL)),
                  pl.BlockSpec((None, N_QUARTERS, D_MODEL, Q_FF), lambda i: (layer, 0, 0, 0)),
                  pl.BlockSpec((None, D_FF, D_MODEL), lambda i: (layer, 0, 0))],
        out_specs=[_tile_spec(ts, D_MODEL), _tile_spec(ts, D_FF), _full_spec((1, D_MODEL)),
                   _tile_spec(ts, D_MODEL), _tile_spec(ts, D_MODEL)],
        out_shape=[jax.ShapeDtypeStruct((s, D_MODEL), F32), jax.ShapeDtypeStruct((s, D_FF), BF),
                   jax.ShapeDtypeStruct((1, D_MODEL), F32),
                   jax.ShapeDtypeStruct((s, D_MODEL), BF), jax.ShapeDtypeStruct((s, D_MODEL), BF)],
        compiler_params=_params(("arbitrary",)),
    )(dx2, p, x1, g2, w_up, w_down)


def _merge_bwd_call(dx1, proj, ya, yb, w_ba, w_bb, w_out, layer, ts, after=None):
    s = dx1.shape[0]

    def body(dx1_ref, ga0, ga1, gb0, gb1, ya_ref, yb_ref, wa_ref, wb_ref, wo_ref, *rest):
        dya_ref, dyb_ref, dgate_ref, dyap_ref, dybp_ref = rest[-5:]
        dm = _dot_nt(dx1_ref[...].astype(BF), wo_ref[...])
        sa = jax.nn.sigmoid(jnp.concatenate([ga0[...], ga1[...]], axis=1).astype(F32))
        sb = jax.nn.sigmoid(jnp.concatenate([gb0[...], gb1[...]], axis=1).astype(F32))
        dya = (dm * sa).astype(BF)
        dyb = (dm * sb).astype(BF)
        dya_ref[...] = dya
        dyb_ref[...] = dyb
        dgate_ref[:, :D_MODEL] = (dm * ya_ref[...].astype(F32) * sa * (1.0 - sa)).astype(BF)
        dgate_ref[:, D_MODEL:] = (dm * yb_ref[...].astype(F32) * sb * (1.0 - sb)).astype(BF)
        dyap_ref[...] = _dot_nt(dya, wa_ref[...]).astype(BF)
        dybp_ref[...] = _dot_nt(dyb, wb_ref[...]).astype(BF)

    act = jax.ShapeDtypeStruct((s, D_MODEL), BF)
    return pl.pallas_call(
        body, name="merge_bwd", grid=(s // ts,),
        in_specs=[_tile_spec(ts, D_MODEL)] + _gate_specs(ts) + [
            _tile_spec(ts, D_MODEL), _tile_spec(ts, D_MODEL),
            _layer_spec(w_ba, layer), _layer_spec(w_bb, layer), _layer_spec(w_out, layer)]
        + ([] if after is None else [pl.BlockSpec(memory_space=pl.ANY)]),
        out_specs=[_tile_spec(ts, D_MODEL), _tile_spec(ts, D_MODEL), _tile_spec(ts, 2 * D_MODEL),
                   _tile_spec(ts, D_RNN), _tile_spec(ts, D_SGU)],
        out_shape=[act, act, jax.ShapeDtypeStruct((s, 2 * D_MODEL), BF),
                   jax.ShapeDtypeStruct((s, D_RNN), BF), jax.ShapeDtypeStruct((s, D_SGU), BF)],
        compiler_params=_params(("parallel",)),
    )(dx1, proj, proj, proj, proj, ya, yb, w_ba, w_bb, w_out, *([] if after is None else [after]))


def _sgu_bwd_call(dyb_pre, proj, wm, bsb, mask, lg, lb, ts):
    s = proj.shape[0]

    def body(dy_ref, uv_ref, wm_ref, bsb_ref, mask_ref, lg_ref, lb_ref,
             duv_ref, dws_ref, dbs_ref, dlg_ref, dlb_ref, dm_sc):
        step = pl.program_id(0)

        @pl.when(step == 0)
        def _():
            dws_ref[...] = jnp.zeros_like(dws_ref)
            dlg_ref[...] = jnp.zeros_like(dlg_ref)
            dlb_ref[...] = jnp.zeros_like(dlb_ref)
            dm_sc[...] = jnp.zeros_like(dm_sc)

        gu, dgu_du = _gelu_and_grad(uv_ref[:, :D_SGU])
        gv, dgv_dv = _gelu_and_grad(uv_ref[:, D_SGU:2 * D_SGU])
        nh, rstd = _layernorm_fwd(gv.astype(F32))
        lgv = lg_ref[...]
        vn = (nh * lgv + lb_ref[...]).astype(BF)
        dy = dy_ref[...].astype(F32)
        du = dy * _sgu_mix(vn, wm_ref, bsb_ref, ts) * dgu_du
        dmix = dy * gu
        dmix_bf = dmix.astype(BF)
        dm_acc = dm_sc[...]
        rows = []
        for blk in range(ts // SGU_BLOCK):
            r0 = blk * SGU_BLOCK
            dm_acc = dm_acc + dmix[r0:r0 + SGU_BLOCK, :]
            cols = []
            for g in range(SGU_GROUPS):
                c0 = g * SGU_BLOCK
                dmg = dmix_bf[r0:r0 + SGU_BLOCK, c0:c0 + SGU_BLOCK]
                cols.append(_dot_tn(wm_ref[g], dmg))
                dws_ref[g] += mask_ref[...] * _dot_nt(dmg, vn[r0:r0 + SGU_BLOCK, c0:c0 + SGU_BLOCK])
            rows.append(jnp.concatenate(cols, axis=1))
        dm_sc[...] = dm_acc
        dvn = jnp.concatenate(rows, axis=0)
        dlg_ref[...] += _row_sum(dvn * nh)
        dlb_ref[...] += _row_sum(dvn)
        dnh = dvn * lgv
        dgv = rstd * (dnh - jnp.mean(dnh, axis=-1, keepdims=True)
                      - nh * jnp.mean(dnh * nh, axis=-1, keepdims=True))
        duv_ref[:, :D_SGU] = du.astype(BF)
        duv_ref[:, D_SGU:] = (dgv * dgv_dv).astype(BF)

        @pl.when(step == pl.num_programs(0) - 1)
        def _():
            for g in range(SGU_GROUPS):
                dbs_ref[:, g:g + 1] = jnp.sum(
                    dm_acc[:, g * SGU_BLOCK:(g + 1) * SGU_BLOCK], axis=1, keepdims=True)

    sw = (SGU_GROUPS, SGU_BLOCK, SGU_BLOCK)
    return pl.pallas_call(
        body, name="sgu_bwd", grid=(s // ts,),
        in_specs=[_tile_spec(ts, D_SGU), _tile_spec(ts, 2 * D_RNN, 1), _full_spec(sw), _full_spec(sw),
                  _full_spec((SGU_BLOCK, SGU_BLOCK)), _full_spec((1, D_SGU)), _full_spec((1, D_SGU))],
        out_specs=[_tile_spec(ts, 2 * D_SGU), _full_spec(sw), _full_spec((SGU_BLOCK, SGU_GROUPS)),
                   _full_spec((1, D_SGU)), _full_spec((1, D_SGU))],
        out_shape=[jax.ShapeDtypeStruct((s, 2 * D_SGU), BF), jax.ShapeDtypeStruct(sw, F32),
                   jax.ShapeDtypeStruct((SGU_BLOCK, SGU_GROUPS), F32),
                   jax.ShapeDtypeStruct((1, D_SGU), F32), jax.ShapeDtypeStruct((1, D_SGU), F32)],
        scratch_shapes=[pltpu.VMEM((SGU_BLOCK, D_SGU), F32)],
        compiler_params=_params(("arbitrary",)),
    )(dyb_pre, proj, wm, bsb, mask, lg, lb)


_ROW_DBA, _ROW_DBX, _ROW_DSP, _ROW_DCB, _ROW_DCW = 0, 1, 2, 3, 4
_PREV_ROWS = 16


def _rnn_bwd_call(dya_pre, proj, xr_saved, hr, wa, wx, ba, bx, sp, cw, ts):
    s = proj.shape[0]
    nt = s // ts
    per = ts // _PREV_ROWS

    def tile(i):
        return nt - 1 - i

    def prev(i):
        return jnp.maximum(tile(i) * per - 1, 0)

    def body(dy_ref, xg_ref, xr_ref, hr_ref, hrp_ref, wa_ref, wx_ref, ba_ref, bx_ref, sp_ref,
             cw_ref, dxg_ref, dwa_ref, dwx_ref, vec_ref,
             lam_carry, a_first, dxr_head, al_sc, bl_sc, lam_sc):
        step = pl.program_id(0)

        @pl.when(step == 0)
        def _():
            dwa_ref[...] = jnp.zeros_like(dwa_ref)
            dwx_ref[...] = jnp.zeros_like(dwx_ref)
            vec_ref[...] = jnp.zeros_like(vec_ref)
            lam_carry[...] = jnp.zeros_like(lam_carry)
            a_first[...] = jnp.zeros_like(a_first)
            dxr_head[...] = jnp.zeros_like(dxr_head)

        has_prev = (step < nt - 1).astype(F32)
        x = xg_ref[:, :D_RNN].astype(F32)
        g = xg_ref[:, D_RNN:]
        h_tail =hrp_ref[_PREV_ROWS - SUBLANES:, :].astype(F32) * has_prev
        xr = xr_ref[...].astype(F32)
        r, i, a, nrm, inv_nrm = _lru_gates(xr, wa_ref, wx_ref, ba_ref, bx_ref, sp_ref)
        h = hr_ref[...].astype(F32)
        dy = dy_ref[...].astype(F32)
        gg, dgg = _gelu_and_grad(g)

        coef = _shift_up(a, jnp.broadcast_to(a_first[...], (SUBLANES, D_RNN)), 1)
        lam_carry[...] = _linear_scan(coef, dy * gg, lam_carry[...], al_sc, bl_sc, lam_sc, True)
        a_first[...] = a[0:1, :]
        lam = lam_sc[...]

        da = lam * _shift_down(h, h_tail, 1)
        dnrm = lam * (i * xr)
        di = lam * nrm * xr
        dlog_a = da * a - dnrm * (a * a) * inv_nrm
        spv = sp_ref[...]
        dza = (dlog_a * (-LRU_C * spv)) * (r * (1.0 - r))
        dzx = di * (i * (1.0 - i))
        vec_ref[_ROW_DSP:_ROW_DSP + 1, :] += _row_sum(dlog_a * (-LRU_C * r))
        vec_ref[_ROW_DBA:_ROW_DBA + 1, :] += _row_sum(dza)
        vec_ref[_ROW_DBX:_ROW_DBX + 1, :] += _row_sum(dzx)
        xb = xr.astype(BF)
        dza_bf = dza.astype(BF)
        dzx_bf = dzx.astype(BF)
        for grp in range(N_LRU_GROUPS):
            cols = slice(grp * LRU_GROUP, (grp + 1) * LRU_GROUP)
            dwa_ref[grp] += _dot_tn(xb[:, cols], dza_bf[:, cols])
            dwx_ref[grp] += _dot_tn(xb[:, cols], dzx_bf[:, cols])
        dxr = (lam * nrm * i + _group_dot(dza_bf, wa_ref, _dot_nt) + _group_dot(dzx_bf, wx_ref, _dot_nt))

        vec_ref[_ROW_DCB:_ROW_DCB + 1, :] += _row_sum(dxr)
        head = dxr_head[...]
        dx = cw_ref[CONV_WIDTH - 1:CONV_WIDTH, :] * dxr
        vec_ref[_ROW_DCW + 3:_ROW_DCW + 4, :] += _row_sum(dxr * x)
        for sft in range(1, CONV_WIDTH):
            k = CONV_WIDTH - 1 - sft
            ahead = _shift_up(dxr, head, sft)
            dx = dx + cw_ref[k:k + 1, :] * ahead
            vec_ref[_ROW_DCW + k:_ROW_DCW + k + 1, :] += _row_sum(ahead * x)
        dxr_head[...] = dxr[0:SUBLANES, :]
        dxg_ref[:, :D_RNN] = dx.astype(BF)
        dxg_ref[:, D_RNN:] = (dy * h * dgg).astype(BF)

    gw = (N_LRU_GROUPS, LRU_GROUP, LRU_GROUP)
    rev = lambda width: pl.BlockSpec((ts, width), lambda i: (tile(i), 0))
    return pl.pallas_call(
        body, name="rnn_bwd", grid=(nt,),
        in_specs=[rev(D_RNN), rev(2 * D_RNN), rev(D_RNN), rev(D_RNN),
                  pl.BlockSpec((_PREV_ROWS, D_RNN), lambda i: (prev(i), 0)),
                  _full_spec(gw), _full_spec(gw),
                  _full_spec((1, D_RNN)), _full_spec((1, D_RNN)), _full_spec((1, D_RNN)),
                  _full_spec((CONV_WIDTH, D_RNN))],
        out_specs=[rev(2 * D_RNN), _full_spec(gw), _full_spec(gw), _full_spec((SUBLANES, D_RNN))],
        out_shape=[jax.ShapeDtypeStruct((s, 2 * D_RNN), BF), jax.ShapeDtypeStruct(gw, F32),
                   jax.ShapeDtypeStruct(gw, F32), jax.ShapeDtypeStruct((SUBLANES, D_RNN), F32)],
        scratch_shapes=[pltpu.VMEM((1, D_RNN), F32), pltpu.VMEM((1, D_RNN), F32),
                        pltpu.VMEM((SUBLANES, D_RNN), F32),
                        pltpu.VMEM((ts, D_RNN), F32), pltpu.VMEM((ts, D_RNN), F32),
                        pltpu.VMEM((ts, D_RNN), F32)],
        compiler_params=_params(("arbitrary",)),
    )(dya_pre, proj, xr_saved, hr, hr, wa, wx, ba, bx, sp, cw)


def _inproj_bwd_call(dxg, duv, dgate, dx1, x, g1, w_in, layer, ts):
    s = x.shape[0]

    def body(dxg_ref, duv_ref, dgt_ref, dx1_ref, x_ref, g_ref, w_ref, dx_ref, dproj_ref, dg_ref):
        @pl.when(pl.program_id(0) == 0)
        def _():
            dg_ref[...] = jnp.zeros_like(dg_ref)

        dproj = jnp.concatenate([dxg_ref[...], duv_ref[...], dgt_ref[...]], axis=1)
        dproj_ref[...] = dproj
        dh = jnp.zeros((ts, D_MODEL), F32)
        for q in range(N_QUARTERS):
            dh = dh + _dot_nt(dproj[:, q * Q_IN:(q + 1) * Q_IN], w_ref[q])
        dx, dg = _rms_bwd(dh, x_ref[...], g_ref[...])
        dx_ref[...] = dx1_ref[...] + dx
        dg_ref[...] += _row_sum(dg)

    return pl.pallas_call(
        body, name="inproj_bwd", grid=(s // ts,),
        in_specs=[_tile_spec(ts, 2 * D_RNN), _tile_spec(ts, 2 * D_SGU), _tile_spec(ts, 2 * D_MODEL),
                  _tile_spec(ts, D_MODEL), _tile_spec(ts, D_MODEL), _full_spec((1, D_MODEL)),
                  pl.BlockSpec((None, N_QUARTERS, D_MODEL, Q_IN), lambda i: (layer, 0, 0, 0))],
        out_specs=[_tile_spec(ts, D_MODEL), _tile_spec(ts, D_IN), _full_spec((1, D_MODEL))],
        out_shape=[jax.ShapeDtypeStruct((s, D_MODEL), F32), jax.ShapeDtypeStruct((s, D_IN), BF),
                   jax.ShapeDtypeStruct((1, D_MODEL), F32)],
        compiler_params=_params(("arbitrary",)),
    )(dxg, duv, dgate, dx1, x, g1, w_in)


def _relu_sq(p):
    return jnp.square(jnp.maximum(p, 0))


def _wgrad_call(a, b, core, tm, tn, tk, col_blocked, name, a_fn=None):
    s, m = a.shape
    n = b.shape[1]
    r, cols = (m, n // N_QUARTERS) if col_blocked else (m // N_QUARTERS, n)
    r2 = r // 2
    per_tile = tm // r
    steps = s // tk
    assert per_tile > 0 or steps == 1

    def body(core_ref, a_ref, b_ref, keep_ref, send_ref, *acc):
        av = a_ref[...]
        if a_fn is not None:
            av = a_fn(av)
        prod = _dot_tn(av.astype(BF), b_ref[...].astype(BF))

        def emit(total):
            for h in range(2):
                @pl.when(core_ref[0] == h)
                def _():
                    for q in range(per_tile):
                        keep_ref[q] = total[q * r + h * r2:q * r + (h + 1) * r2]
                        send_ref[q] = total[q * r + (1 - h) * r2:q * r + (2 - h) * r2].astype(BF)

        if per_tile == 0:
            mine = pl.program_id(1) == core_ref[0]

            @pl.when(mine)
            def _():
                keep_ref[0] = prod

            @pl.when(jnp.logical_not(mine))
            def _():
                send_ref[0] = prod.astype(BF)
        elif steps == 1:
            emit(prod)
        else:
            acc_ref, = acc
            step = pl.program_id(2)

            @pl.when(step == 0)
            def _():
                acc_ref[...] = prod

            @pl.when(jnp.logical_and(step > 0, step < steps - 1))
            def _():
                acc_ref[...] += prod

            @pl.when(step == steps - 1)
            def _():
                emit(acc_ref[...] + prod)

    if col_blocked:
        per_q = cols // tn
        out_spec = pl.BlockSpec((1, r2, tn), lambda j, i, k, c: (j // per_q, 0, j % per_q))
    else:
        out_spec = pl.BlockSpec((per_tile, r2, tn), lambda j, i, k, c: (i, 0, j))
    return pl.pallas_call(
        body, name=name,
        out_shape=[jax.ShapeDtypeStruct((N_QUARTERS, r2, cols), F32),
                   jax.ShapeDtypeStruct((N_QUARTERS, r2, cols), BF)],
        grid_spec=pltpu.PrefetchScalarGridSpec(
            num_scalar_prefetch=1, grid=(n // tn, m // tm, steps),
            in_specs=[pl.BlockSpec((tk, tm), lambda j, i, k, c: (k, i)),
                      pl.BlockSpec((tk, tn), lambda j, i, k, c: (k, j))],
            out_specs=[out_spec, out_spec],
            scratch_shapes=[] if steps == 1 else [pltpu.VMEM((tm, tn), F32)]),
        compiler_params=_params(("parallel", "parallel", "arbitrary")),
    )(core, a, b)


BIG = ("w_in", "w_up", "w_down", "w_branch_a", "w_branch_b", "w_out")


def _block_diag(w):
    w4 = w.reshape(N_LRU_GROUPS, HEADS_PER_GROUP, RNN_HEAD_DIM, RNN_HEAD_DIM)
    eye = jnp.eye(HEADS_PER_GROUP, dtype=w.dtype)
    return jnp.einsum("gjio,jk->gjiko", w4, eye).reshape(N_LRU_GROUPS, LRU_GROUP, LRU_GROUP)


def _block_diag_extract(d):
    d5 = d.reshape(N_LRU_GROUPS, HEADS_PER_GROUP, RNN_HEAD_DIM, HEADS_PER_GROUP, RNN_HEAD_DIM)
    blocks = [d5[:, j, :, j, :] for j in range(HEADS_PER_GROUP)]
    return jnp.stack(blocks, axis=1).reshape(RNN_HEADS, RNN_HEAD_DIM, RNN_HEAD_DIM)


def _sgu_mask():
    chunk = jnp.arange(SGU_BLOCK) // CHUNK
    return (chunk[:, None] >= chunk[None, :]).astype(F32)


def _layer_small(sm, l, core):
    row = lambda v: v.reshape(1, -1)
    return dict(
        core=core,
        g1=row(sm["norm_mix_g"][l]), g2=row(sm["norm_ffn_g"][l]),
        wa=_block_diag(sm["lru_w_a"][l]).astype(BF), wx=_block_diag(sm["lru_w_x"][l]).astype(BF),
        ba=row(sm["lru_b_a"][l]), bx=row(sm["lru_b_x"][l]),
        sp=row(jax.nn.softplus(-sm["lru_lambda"][l])),
        cw=sm["conv_w"][l] if "conv_w" in sm else None, cb=row(sm["conv_b"][l]),
        wm=(sm["sgu_w_s"][l] * _sgu_mask()).astype(BF),
        bsb=jnp.broadcast_to(sm["sgu_b_s"][l][:, :, None], (SGU_GROUPS, SGU_BLOCK, SGU_BLOCK)),
        lg=row(sm["sgu_ln_g"][l]), lb=row(sm["sgu_ln_b"][l]),
    )


def _layer_fwd_mix(x, big, p, ts, h=None, before_sgu=None, proj=None):
    if h is None:
        h = _norm_call(x, p["g1"], ts)
    if proj is None:
        proj = _inproj_call(h, big["w_in"], 0, 2 * ts)
    xr, hr, ya_pre = _rnn_fwd_call(proj, p["wa"], p["wx"], p["ba"], p["bx"], p["sp"], p["cw"], p["cb"], ts)
    lg = p["lg"] if before_sgu is None else p["lg"] + before_sgu(ya_pre)
    yb_pre = _sgu_fwd_call(proj, p["wm"], p["bsb"], lg, p["lb"], ts)
    return dict(p=p, x=x, h=h, proj=proj, xr=xr, hr=hr, ya_pre=ya_pre, yb_pre=yb_pre)


def _layer_fwd_out(sv, big, ts):
    x1, ya, yb, merged, h2 = _merge_call(sv["x"], sv["proj"], sv["ya_pre"], sv["yb_pre"], big["w_branch_a"],
                                         big["w_branch_b"], big["w_out"], sv["p"]["g2"], 0, ts)
    x2, pre = _ffn_call(x1, h2, big["w_up"], big["w_down"], 0, ts)
    sv.update(x1=x1, ya=ya, yb=yb, merged=merged, h2=h2, pre=pre)
    return x2


def _layer_bwd_ffn(dx, sv, big, ts):
    p = sv["p"]
    dx1, dpre, dg2, dx_bf, sv["dx1_bf"] = _ffn_bwd_call(dx, sv["pre"], sv["x1"], p["g2"], big["w_up"],
                                                       big["w_down"], 0, ts)
    tk = dx.shape[0]
    gb = dict(
        w_down=_wgrad_call(sv["pre"], dx_bf, p["core"], Q_FF, D_MODEL, tk, False, "wgrad_down", a_fn=_relu_sq),
        w_up=_wgrad_call(sv["h2"], dpre, p["core"], D_MODEL, Q_FF, tk, True, "wgrad_up"))
    return dx1, gb, dict(norm_ffn_g=dg2[0])


def _layer_bwd_merge(dx1, sv, big, ts, after=None):
    tk = dx1.shape[0]
    core = sv["p"]["core"]
    dya, dyb, dgate, dya_pre, dyb_pre = _merge_bwd_call(
        dx1, sv["proj"], sv["ya"], sv["yb"], big["w_branch_a"], big["w_branch_b"], big["w_out"], 0, ts, after)
    gb = dict(
        w_out=_wgrad_call(sv["merged"], sv["dx1_bf"], core, D_MODEL, D_MODEL, tk, False, "wgrad_out"),
        w_branch_a=_wgrad_call(sv["ya_pre"], dya, core, D_RNN, D_MODEL // 2, tk, False, "wgrad_branch_a"),
        w_branch_b=_wgrad_call(sv["yb_pre"], dyb, core, D_SGU, D_MODEL, tk, False, "wgrad_branch_b"))
    return (dgate, dya_pre, dyb_pre), gb


def _layer_bwd_branches(dx1, merge_out, sv, big, lam, ts, after_sgu=None):
    p = sv["p"]
    tk = dx1.shape[0]
    dgate, dya_pre, dyb_pre = merge_out
    gb = {}
    duv, dws, dbs, dlg, dlb = _sgu_bwd_call(dyb_pre, sv["proj"], p["wm"], p["bsb"], _sgu_mask(), p["lg"], p["lb"],
                                            ts)
    ba = p["ba"] if after_sgu is None else p["ba"] + after_sgu(duv)
    dxg, dwa, dwx, vec = _rnn_bwd_call(dya_pre, sv["proj"], sv["xr"], sv["hr"], p["wa"], p["wx"], ba, p["bx"],
                                       p["sp"], p["cw"], ts)
    dx, dproj, dg1 = _inproj_bwd_call(dxg, duv, dgate, dx1, sv["x"], p["g1"], big["w_in"], 0, ts)
    gb["w_in"] = _wgrad_call(sv["h"], dproj, p["core"], D_MODEL // 2, Q_IN, tk, True, "wgrad_in")
    gs = dict(
        norm_mix_g=dg1[0], conv_w=vec[_ROW_DCW:_ROW_DCW + CONV_WIDTH], conv_b=vec[_ROW_DCB],
        lru_w_a=_block_diag_extract(dwa), lru_w_x=_block_diag_extract(dwx),
        lru_b_a=vec[_ROW_DBA].reshape(RNN_HEADS, RNN_HEAD_DIM), lru_b_x=vec[_ROW_DBX].reshape(RNN_HEADS, RNN_HEAD_DIM),
        lru_lambda=-vec[_ROW_DSP] * jax.nn.sigmoid(-lam),
        sgu_ln_g=dlg[0], sgu_ln_b=dlb[0], sgu_w_s=dws, sgu_b_s=dbs.T)
    return dx, gb, gs


def _local_step(x, target, big, sm, ts):
    saved = []
    core = jnp.zeros((1,), jnp.int32)
    for l in range(DEPTH):
        sv = _layer_fwd_mix(x, big[l], _layer_small(sm, l, core), ts)
        x = _layer_fwd_out(sv, big[l], ts)
        saved.append(sv)
    dx, loss, dgf = _loss_call(x, target, sm["final_norm_g"].reshape(1, -1), ts)
    gb, gs = [None] * DEPTH, [None] * DEPTH
    for l in reversed(range(DEPTH)):
        dx1, gb_ffn, gs_ffn = _layer_bwd_ffn(dx, saved[l], big[l], ts)
        merge_out, gb_merge = _layer_bwd_merge(dx1, saved[l], big[l], ts)
        dx, gb_mix, gs_mix = _layer_bwd_branches(dx1, merge_out, saved[l], big[l], sm["lru_lambda"][l], ts)
        gb[l] = {**gb_ffn, **gb_merge, **gb_mix}
        gs[l] = {**gs_ffn, **gs_mix}
    gs = {k: jnp.stack([g[k] for g in gs]) for k in gs[0]}
    gs["final_norm_g"] = dgf[0]
    return loss, dx, gb, gs


EW_VMEM_BYTES = 24 * 1024 * 1024


def _row_block(rows, cols, bytes_per_elem):
    for br in range(min(rows, EW_VMEM_BYTES // (2 * bytes_per_elem * cols)), 0, -1):
        if rows % br == 0 and br % 16 == 0:
            return br
    return rows


def _ew_call(fn, name, operands, outputs, slabs=1, sel=None, into=None, after=None):
    if into is not None and not isinstance(into, (list, tuple)):
        into = [into]
    rows, cols = outputs[0][0].shape[2:]
    br = _row_block(rows, cols, sum(jnp.dtype(a.dtype).itemsize for a, _ in operands + outputs))
    n_in = len(operands)

    def pick(tok, g, s):
        if callable(tok):
            return tok(g, s)
        if tok == "g":
            return g
        if isinstance(tok, tuple):
            return s[tok[1]]
        return tok

    def spec(idx):
        return pl.BlockSpec((None, None, br, cols),
                            lambda g, i, s, idx=idx: (pick(idx[0], g, s), pick(idx[1], g, s), i, 0))

    if sel is None:
        sel = jnp.zeros((1,), jnp.int32)
    in_specs = [spec(idx) for _, idx in operands]
    arrays = [a for a, _ in operands]
    aliases = {}
    for j, buf in enumerate(into or ()):
        in_specs.append(pl.BlockSpec(memory_space=pl.ANY))
        arrays.append(buf)
        aliases[1 + n_in + j] = j
    if after is not None:
        in_specs.append(pl.BlockSpec(memory_space=pl.ANY))
        arrays.append(after)

    def body(sel_ref, *refs):
        outs = fn(*[r[...] for r in refs[:n_in]])
        for o_ref, o in zip(refs[len(arrays):], outs):
            o_ref[...] = o.astype(o_ref.dtype)

    return pl.pallas_call(
        body, name=name, out_shape=[s for s, _ in outputs],
        grid_spec=pltpu.PrefetchScalarGridSpec(
            num_scalar_prefetch=1, grid=(slabs, rows // br),
            in_specs=in_specs,
            out_specs=[spec(idx) for _, idx in outputs]),
        input_output_aliases=aliases,
        compiler_params=_params(("parallel", "parallel")),
    )(sel, *arrays)


def _as4(a):
    return a.reshape((1,) * (4 - a.ndim) + a.shape)


def _adamw(w, g, m, v):
    m = ADAM_B1 * m + (1.0 - ADAM_B1) * g
    v = ADAM_B2 * v + (1.0 - ADAM_B2) * jnp.square(g)
    m_hat = m / (1.0 - ADAM_B1 ** ADAM_STEP)
    v_hat = v / (1.0 - ADAM_B2 ** ADAM_STEP)
    delta = -ADAM_LR * (m_hat / (jnp.sqrt(v_hat) + ADAM_EPS) + ADAM_WD * w)
    return delta, m, v


def _small_adamw_call(ws, gs, ms, vs):
    n = len(ws)

    def body(*refs):
        for k in range(n):
            w, g, m, v = (refs[j * n + k][...] for j in range(4))
            outs = _adamw(w, g, m, v)
            for j in range(3):
                refs[(4 + j) * n + k][...] = outs[j]

    shapes = [jax.ShapeDtypeStruct(w.shape, F32) for w in ws]
    outs = pl.pallas_call(
        body, name="adamw_small", out_shape=shapes * 3,
        in_specs=[pl.BlockSpec(memory_space=pltpu.VMEM)] * (4 * n),
        out_specs=[pl.BlockSpec(memory_space=pltpu.VMEM)] * (3 * n),
        compiler_params=_params(),
    )(*ws, *gs, *ms, *vs)
    return outs[:n], outs[n:2 * n], outs[2 * n:]


ANY = pl.BlockSpec(memory_space=pl.ANY)


def _place():
    x, y, c = lax.axis_index("x"), lax.axis_index("y"), lax.axis_index("c")
    chips = [(1 - x, y), (x, 1 - y), (1 - x, 1 - y)]
    return x, y, c, chips


def _remote(src, dst, send_sem, recv_sem, to):
    return pltpu.make_async_remote_copy(src_ref=src, dst_ref=dst, send_sem=send_sem, recv_sem=recv_sem,
                                        device_id=to, device_id_type=MESH)


def _sibling_send_call(items):
    n = len(items)

    def body(*refs):
        src, out = refs[:n], refs[n:2 * n]
        send_sems, recv_sems = refs[2 * n:]
        x, y, c, _ = _place()
        copies = [_remote(src[w], out[w], send_sems.at[w], recv_sems.at[w], (x, y, 1 - c)) for w in range(n)]
        for cp in copies:
            cp.start()
        for cp in copies:
            cp.wait()

    return pl.pallas_call(
        body, name="grads_to_sibling",
        out_shape=[jax.ShapeDtypeStruct(a.shape, a.dtype) for a in items],
        in_specs=[ANY] * n, out_specs=[ANY] * n,
        scratch_shapes=[pltpu.SemaphoreType.DMA((n,)), pltpu.SemaphoreType.DMA((n,))],
        compiler_params=_params(vmem=False, has_side_effects=True),
    )(*items)


def _sibling_inplace_call(name, bufs, slabs, n_pairs):
    n = len(bufs)

    def body(*refs):
        out = refs[n:2 * n]
        send_sems, recv_sems = refs[2 * n:]
        x, y, c, _ = _place()
        sibling = (x, y, 1 - c)
        pairs = [pair for w, ref in enumerate(out) for pair in slabs(ref, c, w)]
        sends = [_remote(s, s, send_sems.at[k], recv_sems.at[k], sibling) for k, (s, _) in enumerate(pairs)]
        for cp in sends:
            cp.start()
        for k, (_, r) in enumerate(pairs):
            _remote(r, r, send_sems.at[k], recv_sems.at[k], sibling).wait_recv()
        for cp in sends:
            cp.wait_send()

    return pl.pallas_call(
        body, name=name,
        out_shape=[jax.ShapeDtypeStruct(a.shape, a.dtype) for a in bufs],
        in_specs=[ANY] * n, out_specs=[ANY] * n,
        input_output_aliases={w: w for w in range(n)},
        scratch_shapes=[pltpu.SemaphoreType.DMA((n_pairs,)), pltpu.SemaphoreType.DMA((n_pairs,))],
        compiler_params=_params(vmem=False, has_side_effects=True),
    )(*bufs)


HBM_SPEC = pl.BlockSpec(memory_space=pltpu.HBM)
SEM_SPEC = pl.BlockSpec(memory_space=pltpu.SEMAPHORE)
DATAFLOW_EFFECT = pltpu.SideEffectType.DATAFLOW_SIDE_EFFECTING


def _exchange_start(name, bufs, copies, n_copies, after):
    n = len(bufs)

    def body(*refs):
        ins, send_sems, recv_sems, token = refs[:n], refs[n + 1], refs[n + 2], refs[-1]
        for k, (src, dst, to) in enumerate(copies(ins)):
            _remote(src, dst, send_sems.at[k], recv_sems.at[k], to).start()
        token[...] = jnp.zeros_like(token)

    outs = pl.pallas_call(
        body, name=name,
        out_shape=(pltpu.SemaphoreType.DMA((n_copies,)), pltpu.SemaphoreType.DMA((n_copies,)),
                   *[pltpu.HBM(b.shape, b.dtype) for b in bufs], jax.ShapeDtypeStruct((SUBLANES, 128), F32)),
        in_specs=[HBM_SPEC] * n + [ANY],
        out_specs=(SEM_SPEC, SEM_SPEC, *[HBM_SPEC] * n, pl.BlockSpec(memory_space=pltpu.VMEM)),
        input_output_aliases={w: w + 2 for w in range(n)},
        compiler_params=pltpu.CompilerParams(has_side_effects=DATAFLOW_EFFECT),
    )(*[pltpu.with_memory_space_constraint(b, pltpu.HBM) for b in bufs], after)
    return outs[0], outs[1], list(outs[2:2 + n]), outs[-1]


def _exchange_wait(name, send_sems, recv_sems, bufs, copies, after):
    n = len(bufs)

    def body(*refs):
        ins, send_sems, recv_sems = refs[:n], refs[n], refs[n + 1]
        for k, (src, dst, to) in enumerate(copies(ins)):
            cp = _remote(src, dst, send_sems.at[k], recv_sems.at[k], to)
            cp.wait_send()
            cp.wait_recv()

    return pl.pallas_call(
        body, name=name,
        out_shape=[pltpu.HBM(b.shape, b.dtype) for b in bufs],
        in_specs=[HBM_SPEC] * n + [SEM_SPEC, SEM_SPEC, ANY],
        out_specs=[HBM_SPEC] * n,
        input_output_aliases={w: w for w in range(n)},
        compiler_params=pltpu.CompilerParams(has_side_effects=DATAFLOW_EFFECT),
    )(*bufs, send_sems, recv_sems, after)


def _gather_copies(refs):
    x, y, c, chips = _place()
    mine = 2 * (2 * x + y) + c
    return [(ref.at[mine], ref.at[mine], (qx, qy, c)) for ref in refs for qx, qy in chips]


def _forward_copies(refs):
    x, y, c, chips = _place()
    return [(ref.at[2 * (2 * qx + qy) + c], ref.at[2 * (2 * qx + qy) + c], (x, y, 1 - c))
            for ref in refs for qx, qy in chips]


def _gather_forward_slabs(ref, c, w):
    x, y, _, chips = _place()
    return [(ref.at[2 * (2 * qx + qy) + c], ref.at[2 * (2 * qx + qy) + 1 - c]) for qx, qy in chips]


def _device_peers():
    x, y, c, _ = _place()
    return 4 * x + 2 * y + c, [(k, (x ^ ((k >> 2) & 1), y ^ ((k >> 1) & 1), c ^ (k & 1))) for k in range(1, 8)]


def _small_scatter_copies(refs):
    me, peers = _device_peers()
    return [(refs[0].at[me ^ k], refs[1].at[me], to) for k, to in peers]


def _small_spread_copies(refs):
    me, peers = _device_peers()
    return [(refs[0].at[me], refs[0].at[me], to) for _, to in peers]


def _sibling_copies(refs):
    n = len(refs) // 2
    x, y, c, _ = _place()
    return [(refs[w], refs[n + w], (x, y, 1 - c)) for w in range(n)]


def _owner_copies(refs):
    n = len(refs) // 2
    x, y, c, chips = _place()
    return [(refs[w].at[2 * qx + qy], refs[n + w].at[j], (qx, qy, c))
            for w in range(n) for j, (qx, qy) in enumerate(chips)]


N_DEVICES = 8
SMALL_ROWS = 616


SMALL = ("norm_mix_g", "conv_w", "conv_b", "lru_w_a", "lru_b_a", "lru_w_x", "lru_b_x", "lru_lambda",
         "sgu_ln_g", "sgu_ln_b", "sgu_w_s", "sgu_b_s", "norm_ffn_g", "final_norm_g")
WEIGHTS = ("norm_mix_g", "w_in", "conv_w", "conv_b", "lru_w_a", "lru_b_a", "lru_w_x", "lru_b_x", "lru_lambda",
           "sgu_ln_g", "sgu_ln_b", "sgu_w_s", "sgu_b_s", "w_branch_a", "w_branch_b", "w_out", "norm_ffn_g",
           "w_up", "w_down", "final_norm_g")
PACK_ALIGN = SUBLANES * 128


PACKED = SMALL + ("loss",)


def _pack_small(gs):
    parts = []
    for k in PACKED:
        flat = gs[k].reshape(-1)
        parts.append(jnp.pad(flat, (0, -flat.size % PACK_ALIGN)))
    flat = jnp.concatenate(parts)
    flat = jnp.pad(flat, (0, N_DEVICES * SMALL_ROWS * 128 - flat.size))
    return flat.reshape(N_DEVICES, SMALL_ROWS, 128)


def _unpack_small(buf, like):
    flat = buf.reshape(-1)
    out, off = {}, 0
    for k in PACKED:
        size = like[k].size
        out[k] = flat[off:off + size].reshape(like[k].shape)
        off += size + (-size % PACK_ALIGN)
    return out


def _as_rows(a):
    return a.reshape(-1, a.shape[-1])


def kernel(x, norm_mix_g, w_in, conv_w, conv_b, lru_w_a, lru_b_a, lru_w_x, lru_b_x, lru_lambda, sgu_ln_g, sgu_ln_b, sgu_w_s, sgu_b_s, w_branch_a, w_branch_b, w_out, norm_ffn_g, w_up, w_down, final_norm_g, loss_target, m_norm_mix_g, m_w_in, m_conv_w, m_conv_b, m_lru_w_a, m_lru_b_a, m_lru_w_x, m_lru_b_x, m_lru_lambda, m_sgu_ln_g, m_sgu_ln_b, m_sgu_w_s, m_sgu_b_s, m_w_branch_a, m_w_branch_b, m_w_out, m_norm_ffn_g, m_w_up, m_w_down, m_final_norm_g, v_norm_mix_g, v_w_in, v_conv_w, v_conv_b, v_lru_w_a, v_lru_b_a, v_lru_w_x, v_lru_b_x, v_lru_lambda, v_sgu_ln_g, v_sgu_ln_b, v_sgu_w_s, v_sgu_b_s, v_w_branch_a, v_w_branch_b, v_w_out, v_norm_ffn_g, v_w_up, v_w_down, v_final_norm_g):
    w = dict(norm_mix_g=norm_mix_g, w_in=w_in, conv_w=conv_w, conv_b=conv_b, lru_w_a=lru_w_a, lru_b_a=lru_b_a,
             lru_w_x=lru_w_x, lru_b_x=lru_b_x, lru_lambda=lru_lambda, sgu_ln_g=sgu_ln_g, sgu_ln_b=sgu_ln_b,
             sgu_w_s=sgu_w_s, sgu_b_s=sgu_b_s, w_branch_a=w_branch_a, w_branch_b=w_branch_b, w_out=w_out,
             norm_ffn_g=norm_ffn_g, w_up=w_up, w_down=w_down, final_norm_g=final_norm_g)
    m = dict(norm_mix_g=m_norm_mix_g, w_in=m_w_in, conv_w=m_conv_w, conv_b=m_conv_b, lru_w_a=m_lru_w_a,
             lru_b_a=m_lru_b_a, lru_w_x=m_lru_w_x, lru_b_x=m_lru_b_x, lru_lambda=m_lru_lambda,
             sgu_ln_g=m_sgu_ln_g, sgu_ln_b=m_sgu_ln_b, sgu_w_s=m_sgu_w_s, sgu_b_s=m_sgu_b_s,
             w_branch_a=m_w_branch_a, w_branch_b=m_w_branch_b, w_out=m_w_out, norm_ffn_g=m_norm_ffn_g,
             w_up=m_w_up, w_down=m_w_down, final_norm_g=m_final_norm_g)
    v = dict(norm_mix_g=v_norm_mix_g, w_in=v_w_in, conv_w=v_conv_w, conv_b=v_conv_b, lru_w_a=v_lru_w_a,
             lru_b_a=v_lru_b_a, lru_w_x=v_lru_w_x, lru_b_x=v_lru_b_x, lru_lambda=v_lru_lambda,
             sgu_ln_g=v_sgu_ln_g, sgu_ln_b=v_sgu_ln_b, sgu_w_s=v_sgu_w_s, sgu_b_s=v_sgu_b_s,
             w_branch_a=v_w_branch_a, w_branch_b=v_w_branch_b, w_out=v_w_out, norm_ffn_g=v_norm_ffn_g,
             w_up=v_w_up, w_down=v_w_down, final_norm_g=v_final_norm_g)
    core = lax.axis_index("c")
    chip = 2 * lax.axis_index("x") + lax.axis_index("y")
    sel = jnp.stack([core, 1 - core, chip, 2 * chip + core]).astype(jnp.int32)
    this_core, this_chip = ("sel", 0), ("sel", 2)
    sds = jax.ShapeDtypeStruct

    ts = TOKEN_TILE

    def after_all(arrays):
        return jnp.stack([a[(0,) * a.ndim].astype(F32) for a in arrays])

    halves = {k:(w[k].shape[1] // 2, w[k].shape[2]) for k in BIG}

    def half_view(k, a):
        return a.reshape((2 * N_QUARTERS,) + halves[k])

    def full_view(k, a):
        if k == "conv_w":
            return a.reshape(N_QUARTERS, DEPTH, CONV_WIDTH, -1).transpose(1, 2, 0, 3).reshape(DEPTH, CONV_WIDTH, D_RNN)
        r2, cols = halves[k]
        if k in ("w_in", "w_up"):
            return a.reshape(1, N_QUARTERS, 2 * r2, cols)
        return a.reshape(1, 2 * N_QUARTERS * r2, cols)

    layer_bufs = [{}, {}]

    def cast_weights(k, after):
        _, r, cols = w[k].shape
        w4 = w[k].reshape(DEPTH, 1, r, cols)
        outs = _ew_call(lambda a, b: (a, b), "cast_weights", [(w4, (0, 0)), (w4, (1, 0))],
                        [(sds((1, N_QUARTERS, r, cols), BF), (0, this_chip))] * DEPTH, 1, sel, after=after)
        for l in range(DEPTH):
            layer_bufs[l][k] = half_view(k, outs[l])

    conv_buf = lax.dynamic_update_slice_in_dim(
        jnp.zeros((N_QUARTERS, DEPTH) + conv_w.shape[1:], F32), conv_w[None], chip, axis=0)
    layer_bufs[0]["conv_w"] = conv_buf.reshape((2 * N_QUARTERS,) + conv_w.shape[1:])
    sm = {k: w[k] for k in SMALL if k != "conv_w"}

    def gather_start(tag, l, keys, after):
        bufs = [layer_bufs[l][k] for k in keys]
        return _exchange_start(f"gather_start_{tag}", bufs, _gather_copies, 3 * len(keys), after)

    def gather_finish(tag, keys, started, after):
        send_sems, recv_sems, thru, _ = started
        landed = _exchange_wait(f"gather_wait_{tag}", send_sems, recv_sems, thru, _gather_copies, after)
        landed = _sibling_inplace_call("gather_forward", landed, _gather_forward_slabs, 3 * len(keys))
        return {k: full_view(k, a) for k, a in zip(keys, landed)}

    first, rest = ("w_in",), tuple(k for k in BIG if k != "w_in")
    cast_weights("w_in", None)
    started_a = gather_start("0a", 0, first + ("conv_w",), sel)
    for k in rest:
        cast_weights(k, started_a[3])
    started_b = gather_start("0b", 0, rest, started_a[3])
    started_c = gather_start("1a", 1, first, started_b[3])
    started_d = gather_start("1b", 1, rest, started_c[3])

    def arrives(tag, keys, started):
        state = {}

        def hook(after):
            landed = _exchange_wait(f"gather_wait_{tag}", started[0], started[1], started[2], _gather_copies, after)
            state["forward"] = _exchange_start(f"forward_start_{tag}", landed, _forward_copies, 3 * len(keys), after)
            return state["forward"][3][0, 0]

        def finish(after):
            send_sems, recv_sems, thru, _ = state["forward"]
            done = _exchange_wait(f"forward_wait_{tag}", send_sems, recv_sems, thru, _forward_copies, after)
            return {k: full_view(k, a) for k, a in zip(keys, done)}

        return hook, finish

    p0, p1 = _layer_small(sm, 0, sel[0:1]), _layer_small(sm, 1, sel[0:1])
    h0 = _norm_call(x[0], p0["g1"], ts)
    proj_own = _inproj_part_call(h0, full_view("w_in", started_a[2][0]), 2 * ts, sel[2:3], 0, 1)
    ready = after_all([started_d[3], proj_own] + [p[k] for p in (p0, p1) for k in ("wa", "wx", "wm")])
    big0 = gather_finish("0a", first + ("conv_w",), started_a, ready)
    for l, p in enumerate((p0, p1)):
        p["cw"] = big0["conv_w"][l]
    proj0 = _inproj_part_call(h0, big0["w_in"], 2 * ts, sel[2:3], 1, N_QUARTERS - 1, proj_own)
    hook, finish = arrives("0b", rest, started_b)
    sv0 = _layer_fwd_mix(x[0], big0, p0, ts, h0, hook, proj0)
    big0.update(finish(sv0["yb_pre"]))
    x_mid = _layer_fwd_out(sv0, big0, ts)
    hook, finish = arrives("1a", first, started_c)
    h1 = _norm_call(x_mid, p1["g1"] + hook(x_mid), ts)
    big1 = finish(h1)
    hook, finish = arrives("1b", rest, started_d)
    sv1 = _layer_fwd_mix(x_mid, big1, p1, ts, h1, hook)
    big1.update(finish(sv1["yb_pre"]))
    x_out = _layer_fwd_out(sv1, big1, ts)
    dx, loss, dgf = _loss_call(x_out, loss_target[0], final_norm_g.reshape(1, -1), ts)

    def pair_start(tag, gb, after):
        sends = [gb[k][1] for k in gb]
        zones = [lax.empty(a.shape, BF) for a in sends]
        return _exchange_start(f"pair_start_{tag}", sends + zones, _sibling_copies, len(sends), after)

    def reduce_start(tag, gb, after, pair=None):
        keys = tuple(gb)
        if pair is None:
            from_sibling = _sibling_send_call([gb[k][1] for k in keys])
        else:
            done = _exchange_wait(f"pair_wait_{tag}", pair[0], pair[1], pair[2], _sibling_copies, after)
            from_sibling = done[len(keys):]
        sums = [
            _ew_call(lambda a, b: (a + b.astype(F32),), "pair_sum", [(gb[k][0][None], (0, "g")), (r[None], (0, "g"))],
                     [(sds((1,) + r.shape, BF), (0, "g"))], N_QUARTERS)[0][0]
            for k, r in zip(keys, from_sibling)]
        zones = [lax.empty((3,) + a.shape[1:], BF) for a in sums]
        started = _exchange_start(f"reduce_start_{tag}", sums + zones, _owner_copies, 3 * len(keys), after)
        return keys, started

    def reduce_finish(tag, l, keys_started, after, reduced):
        keys, (send_sems, recv_sems, thru, _) = keys_started
        done = _exchange_wait(f"reduce_wait_{tag}", send_sems, recv_sems, thru, _owner_copies, after)
        sums, zones = done[:len(keys)], done[len(keys):]
        for i, k in enumerate(keys):
            r2, cols = halves[k]
            reduced[k] = _ew_call(
                lambda a, b, c, d: (((a.astype(F32) + b.astype(F32)) + c.astype(F32)) + d.astype(F32),),
                "quarter_sum", [(sums[i][None], (0, this_chip))] + [(zones[i][None], (0, j)) for j in range(3)],
                [(sds((DEPTH, 2, r2, cols), F32), (l, this_core))], 1, sel, into=reduced.get(k))[0]

    def behind(params, key, started):
        return dict(params, **{key: params[key] + started[1][3][0, 0]})

    dx1, gb_ffn, gs1 = _layer_bwd_ffn(dx, sv1, big1, ts)
    merge_out, gb_merge = _layer_bwd_merge(dx1, sv1, big1, ts)
    dx_mid, gb_in, gs1_mix = _layer_bwd_branches(dx1, merge_out, sv1, big1, lru_lambda[1], ts)
    gb_1 = {**gb_ffn, **gb_merge, **gb_in}
    pair_1 = pair_start("1", gb_1, dx_mid)
    sv0["p"] = behind(sv0["p"], "g2", (None, pair_1))
    dx1, gb_ffn, gs0 = _layer_bwd_ffn(dx_mid, sv0, big0, ts)
    exchange_1 = reduce_start("1", gb_1, dx1, pair_1)
    pair_0a = pair_start("0a", gb_ffn, exchange_1[1][3])
    merge_out, gb_merge = _layer_bwd_merge(dx1, sv0, big0, ts, pair_0a[3])
    exchange_0a = reduce_start("0a", gb_ffn, merge_out[0], pair_0a)
    pair_0b = pair_start("0b", gb_merge, exchange_0a[1][3])
    sv0["p"] = behind(sv0["p"], "lg", (None, pair_0b))
    started_0b = {}

    def after_sgu(duv):
        started_0b["exchange"] = reduce_start("0b", gb_merge, duv, pair_0b)
        return started_0b["exchange"][1][3][0, 0]

    grad_x, gb_in, gs0_mix = _layer_bwd_branches(dx1, merge_out, sv0, big0, lru_lambda[0], ts, after_sgu)
    exchange_0b = started_0b["exchange"]
    exchange_0c = reduce_start("0c", gb_in, exchange_0b[1][3])
    layer_gs = [{**gs0, **gs0_mix}, {**gs1, **gs1_mix}]
    gs = {k: jnp.stack([g[k] for g in layer_gs]) for k in layer_gs[0]}
    gs["final_norm_g"] = dgf[0]
    gs["loss"] = loss[0, 0:1]

    me = ("sel", 3)
    piece = (1, N_DEVICES, SMALL_ROWS, 128)
    packed = _pack_small(gs).reshape(piece)
    scatter = _exchange_start("small_scatter_start", [packed[0], lax.empty(piece[1:], F32)], _small_scatter_copies,
                              N_DEVICES - 1, exchange_0c[1][3])
    reduced = {}
    reduce_finish("1", 1, exchange_1, scatter[3], reduced)
    reduce_finish("0a", 0, exchange_0a, reduced["w_in"], reduced)
    reduce_finish("0b", 0, exchange_0b, reduced["w_down"], reduced)

    def swap_slabs(ref, c, i):
        layers = (1,) if BIG[i] == "w_in" else range(DEPTH)
        return [(ref.at[l, c], ref.at[l, 1 - c]) for l in layers]

    swapped = dict(zip(BIG, _sibling_inplace_call("grads_swap_halves", [reduced[k] for k in BIG], swap_slabs,
                                                  DEPTH * len(BIG) - 1)))

    def adamw_layers(k, grad, layer, into, after=None):
        if layer is None:
            views = [_as4(_as_rows(a)) for a in (w[k], grad, m[k], v[k])]
            idx = (0, 0)
        else:
            views = [a.reshape((1,) + w[k].shape) for a in (w[k], grad, m[k], v[k])]
            idx = (0, layer)
        return _ew_call(_adamw, "adamw_big", [(a, idx) for a in views], [(sds(views[0].shape, F32), idx)] * 3,
                        into=into, after=after)

    updated, last_update = {}, None
    for k in BIG:
        updated[k] = adamw_layers(k, swapped[k], 1 if k == "w_in" else None, None, last_update)
        last_update = updated[k][0]
    scattered = _exchange_wait("small_scatter_wait", scatter[0], scatter[1], scatter[2], _small_scatter_copies,
                               last_update)
    summed = _ew_call(
        lambda *parts: (functools.reduce(lambda a, b: a + b, parts),), "small_sum",
        [(scattered[0][None], (0, me))]
        + [(scattered[1][None], (0, lambda g, s, k=k: s[3] ^ k)) for k in range(1, N_DEVICES)],
        [(sds(piece, F32), (0, me))], 1, sel)[0]
    spread = _exchange_start("small_spread_start", [summed[0]], _small_spread_copies, N_DEVICES - 1, summed)
    reduced["w_in"] = swapped["w_in"]
    reduce_finish("0c", 0, exchange_0c, spread[3], reduced)
    last = _sibling_inplace_call("grads_swap_last", [reduced["w_in"]],
                                 lambda ref, c, i: [(ref.at[0, c], ref.at[0, 1 - c])], 1)[0]
    swapped["w_in"] = last
    updated["w_in"] = adamw_layers("w_in", last, 0, updated["w_in"])
    grads_big = {k: swapped[k].reshape(w[k].shape) for k in BIG}
    delta, new_m, new_v = ({k: updated[k][j].reshape(w[k].shape) for k in BIG} for j in range(3))
    gathered_small = _exchange_wait("small_spread_wait", spread[0], spread[1], spread[2], _small_spread_copies,
                                    updated["w_in"][0])[0]

    like = {k: jax.ShapeDtypeStruct(gs[k].shape, F32) for k in SMALL}
    like["loss"] = jax.ShapeDtypeStruct((1,), F32)
    grads_small = _unpack_small(gathered_small, like)
    total = grads_small.pop("loss")[0]
    conv_q = grads_small["conv_w"].reshape(DEPTH, CONV_WIDTH, N_QUARTERS, D_RNN // N_QUARTERS)
    grads_small["conv_w"] = lax.dynamic_index_in_dim(conv_q, chip, axis=2, keepdims=False)
    outs = _small_adamw_call(*[[_as_rows(d[k]) for k in SMALL] for d in (w, grads_small, m, v)])
    for d, o in zip((delta, new_m, new_v), outs):
        for k, a in zip(SMALL, o):
            d[k] = a.reshape(w[k].shape)

    grads = {**grads_big, **grads_small}
    return (total, grad_x[None], *[grads[k] for k in WEIGHTS], *[delta[k] for k in WEIGHTS],
            *[new_m[k] for k in WEIGHTS], *[new_v[k] for k in WEIGHTS])
```

```python
import functools
import math

import jax
import jax.numpy as jnp
from jax import lax
from jax.experimental import pallas as pl
from jax.experimental.pallas import tpu as pltpu

F32 = jnp.float32
BF = jnp.bfloat16

DEPTH = 2
D_MODEL = 1024
D_RNN = 1280
D_SGU = 1024
D_FF = 4096
D_IN = 2 * D_RNN + 2 * D_SGU + 2 * D_MODEL
N_QUARTERS = 4
Q_IN = D_IN // N_QUARTERS
Q_FF = D_FF // N_QUARTERS
RNN_HEADS = 20
RNN_HEAD_DIM = 64
LRU_GROUP = 256
N_LRU_GROUPS = D_RNN // LRU_GROUP
HEADS_PER_GROUP = LRU_GROUP // RNN_HEAD_DIM
CONV_WIDTH = 4
LRU_C = 8.0
SGU_GROUPS = 8
SGU_BLOCK = 128
CHUNK = 64
EPS = 1e-6

ADAM_LR = 0.001
ADAM_B1 = 0.9
ADAM_B2 = 0.999
ADAM_EPS = 1e-08
ADAM_WD = 0.01
ADAM_STEP = 10

SUBLANES = 8
TOKEN_TILE = 512
VMEM_LIMIT_BYTES = 56 * 1024 * 1024

MESH = pl.DeviceIdType.MESH


def _params(semantics=None, vmem=True, **kw):
    return pltpu.CompilerParams(
        dimension_semantics=semantics,
        vmem_limit_bytes=VMEM_LIMIT_BYTES if vmem else None,
        **kw,
    )


def _dot(a, b):
    return jnp.dot(a, b, preferred_element_type=F32)


def _dot_nt(a, b):
    return lax.dot_general(a, b, (((1,), (1,)), ((), ())), preferred_element_type=F32)


def _dot_tn(a, b):
    return lax.dot_general(a, b, (((0,), (0,)), ((), ())), preferred_element_type=F32)


_GELU_C = math.sqrt(2.0 / math.pi)
_GELU_A = 0.044715


def _gelu(x):
    return 0.5 * x * (1.0 + jnp.tanh(_GELU_C * (x + _GELU_A * x * x * x)))


def _gelu_and_grad(x):
    x2 = x * x
    t = jnp.tanh(_GELU_C * (x + _GELU_A * x2 * x))
    du = _GELU_C * (1.0 + 3.0 * _GELU_A * x2)
    return 0.5 * x * (1.0 + t), 0.5 * (1.0 + t) + 0.5 * x * (1.0 - t * t) * du


def _rms_stats(x):
    return lax.rsqrt(jnp.mean(x * x, axis=-1, keepdims=True) + EPS)


def _rms_bwd(dy, x, g):
    rs = _rms_stats(x)
    n = x * rs
    dn = dy * g
    dx = rs * (dn - n * jnp.mean(dn * n, axis=-1, keepdims=True))
    return dx, dy * n


def _row_sum(x):
    return jnp.sum(x, axis=0, keepdims=True)


def _tile_spec(ts, width, col=0):
    return pl.BlockSpec((ts, width), lambda i, col=col: (i, col))


def _full_spec(shape):
    zeros = (0,) * len(shape)
    return pl.BlockSpec(shape, lambda *_: zeros)


def _layer_spec(w, layer):
    zeros = (0,) * (w.ndim - 1)
    return pl.BlockSpec((None,) + tuple(w.shape[1:]), lambda *_: (layer,) + zeros)


def _norm_call(x, g, ts):
    s = x.shape[0]

    def body(x_ref, g_ref, h_ref):
        xv = x_ref[...]
        h_ref[...] = (xv * _rms_stats(xv) * g_ref[...]).astype(BF)

    return pl.pallas_call(
        body, name="norm_fwd", grid=(s // ts,),
        in_specs=[_tile_spec(ts, D_MODEL), _full_spec((1, D_MODEL))],
        out_specs=_tile_spec(ts, D_MODEL),
        out_shape=jax.ShapeDtypeStruct((s, D_MODEL), BF),
        compiler_params=_params(("parallel",)),
    )(x, g)


def _inproj_call(h, w_in, layer, ts):
    s = h.shape[0]

    def body(h_ref, w_ref, o_ref):
        o_ref[...] = _dot(h_ref[...], w_ref[...]).astype(BF)

    return pl.pallas_call(
        body, name="inproj_fwd", grid=(N_QUARTERS, s // ts),
        in_specs=[
            pl.BlockSpec((ts, D_MODEL), lambda q, i: (i, 0)),
            pl.BlockSpec((None, None, D_MODEL, Q_IN), lambda q, i: (layer, q, 0, 0)),
        ],
        out_specs=pl.BlockSpec((ts, Q_IN), lambda q, i: (i, q)),
        out_shape=jax.ShapeDtypeStruct((s, D_IN), BF),
        compiler_params=_params(("parallel", "parallel")),
    )(h, w_in)


def _inproj_part_call(h, w_in, ts, own, first, count, into=None):
    s = h.shape[0]

    def quarter(j, sel):
        return (sel[0] + first + j) % N_QUARTERS

    def body(sel_ref, h_ref, w_ref, *rest):
        rest[-1][...] = _dot(h_ref[...], w_ref[...]).astype(BF)

    in_specs = [pl.BlockSpec((ts, D_MODEL), lambda j, i, sel: (i, 0)),
                pl.BlockSpec((None, None, D_MODEL, Q_IN), lambda j, i, sel: (0, quarter(j, sel), 0, 0))]
    operands = [h, w_in]
    aliases = {}
    if into is not None:
        in_specs.append(pl.BlockSpec(memory_space=pl.ANY))
        operands.append(into)
        aliases = {3: 0}
    return pl.pallas_call(
        body, name="inproj_fwd_part", out_shape=jax.ShapeDtypeStruct((s, D_IN), BF),
        grid_spec=pltpu.PrefetchScalarGridSpec(
            num_scalar_prefetch=1, grid=(count, s // ts), in_specs=in_specs,
            out_specs=pl.BlockSpec((ts, Q_IN), lambda j, i, sel: (i, quarter(j, sel)))),
        input_output_aliases=aliases,
        compiler_params=_params(("parallel", "parallel")),
    )(own, *operands)


def _shift_down(x, tail, s):
    xr = pltpu.roll(x, s, 0)
    tr = pltpu.roll(tail, s, 0)
    row = lax.broadcasted_iota(jnp.int32, tail.shape, 0)
    top = jnp.where(row < s, tr, xr[0:SUBLANES])
    return jnp.concatenate([top, xr[SUBLANES:]], axis=0)


def _shift_up(x, head, s):
    t = x.shape[0]
    xr = pltpu.roll(x, t - s, 0)
    hr = pltpu.roll(head, SUBLANES - s, 0)
    row = lax.broadcasted_iota(jnp.int32, head.shape, 0)
    bottom = jnp.where(row >= SUBLANES - s, hr, xr[t - SUBLANES:])
    return jnp.concatenate([xr[: t - SUBLANES], bottom], axis=0)


def _conv_fwd(x, tail, cw_ref, cb_ref):
    out = cb_ref[...] + cw_ref[CONV_WIDTH - 1:CONV_WIDTH, :] * x
    for s in range(1, CONV_WIDTH):
        k = CONV_WIDTH - 1 - s
        out = out + cw_ref[k:k + 1, :] * _shift_down(x, tail, s)
    return out


def _group_dot(x_bf, w_ref, dot):
    cols = [dot(x_bf[:, g * LRU_GROUP:(g + 1) * LRU_GROUP], w_ref[g]) for g in range(N_LRU_GROUPS)]
    return jnp.concatenate(cols, axis=1)


def _lru_gates(xr, wa_ref, wx_ref, ba_ref, bx_ref, sp_ref):
    xb = xr.astype(BF)
    r = jax.nn.sigmoid(_group_dot(xb, wa_ref, _dot) + ba_ref[...])
    i = jax.nn.sigmoid(_group_dot(xb, wx_ref, _dot) + bx_ref[...])
    log_a = (-LRU_C * r) * sp_ref[...]
    a = jnp.exp(log_a)
    nrm2 = -jnp.tanh(log_a) * (a * a + 1.0)
    inv_nrm = lax.rsqrt(jnp.maximum(nrm2, 1e-36))
    return r, i, a, nrm2 * inv_nrm, inv_nrm


def _linear_scan(a, b, carry, al_ref, bl_ref, h_ref, reverse):
    t, c = a.shape
    rowm = lax.broadcasted_iota(jnp.int32, (t, c), 0) & (SUBLANES - 1)
    for d in (1, 2, 4):
        if reverse:
            keep, sh = rowm < SUBLANES - d, t - d
        else:
            keep, sh = rowm >= d, d
        a_sh = jnp.where(keep, pltpu.roll(a, sh, 0), 1.0)
        b_sh = jnp.where(keep, pltpu.roll(b, sh, 0), 0.0)
        b = a * b_sh + b
        a = a * a_sh
    al_ref[...] = a
    bl_ref[...] = b
    groups = t // SUBLANES

    def step(j, state):
        jj = groups - 1 - j if reverse else j
        off = pl.multiple_of(jj * SUBLANES, SUBLANES)
        rows = bl_ref[pl.ds(off, SUBLANES), :] + al_ref[pl.ds(off, SUBLANES), :] * state
        h_ref[pl.ds(off, SUBLANES), :] = rows
        last = rows[0:1, :] if reverse else rows[SUBLANES - 1:SUBLANES, :]
        return jnp.broadcast_to(last, (SUBLANES, c))

    out = lax.fori_loop(0, groups, step, jnp.broadcast_to(carry, (SUBLANES, c)))
    return out[0:1, :]


def _rnn_fwd_call(proj, wa, wx, ba, bx, sp, cw, cb, ts):
    s = proj.shape[0]

    def body(xg_ref, wa_ref, wx_ref, ba_ref, bx_ref, sp_ref, cw_ref, cb_ref, xr_ref, hr_ref, ya_ref,
             tail_sc, carry_sc, al_sc, bl_sc, h_sc):
        @pl.when(pl.program_id(0) == 0)
        def _():
            tail_sc[...] = jnp.zeros_like(tail_sc)
            carry_sc[...] = jnp.zeros_like(carry_sc)

        x = xg_ref[:, :D_RNN].astype(F32)
        g = xg_ref[:, D_RNN:]
        xr = _conv_fwd(x, tail_sc[...], cw_ref, cb_ref)
        tail_sc[...] = x[ts - SUBLANES:, :]
        xr_ref[...] = xr.astype(BF)
        _, i, a, nrm, _ = _lru_gates(xr, wa_ref, wx_ref, ba_ref, bx_ref, sp_ref)
        carry_sc[...] = _linear_scan(a, nrm * (i * xr), carry_sc[...], al_sc, bl_sc, h_sc, False)
        h = h_sc[...]
        hr_ref[...] = h.astype(BF)
        ya_ref[...] = (h * _gelu(g)).astype(BF)

    gw = (N_LRU_GROUPS, LRU_GROUP, LRU_GROUP)
    return pl.pallas_call(
        body, name="rnn_fwd", grid=(s // ts,),
        in_specs=[_tile_spec(ts, 2 * D_RNN), _full_spec(gw), _full_spec(gw),
                  _full_spec((1, D_RNN)), _full_spec((1, D_RNN)), _full_spec((1, D_RNN)),
                  _full_spec((CONV_WIDTH, D_RNN)), _full_spec((1, D_RNN))],
        out_specs=[_tile_spec(ts, D_RNN)] * 3,
        out_shape=[jax.ShapeDtypeStruct((s, D_RNN), BF)] * 3,
        scratch_shapes=[pltpu.VMEM((SUBLANES, D_RNN), F32), pltpu.VMEM((1, D_RNN), F32),
                        pltpu.VMEM((ts, D_RNN), F32), pltpu.VMEM((ts, D_RNN), F32),
                        pltpu.VMEM((ts, D_RNN), F32)],
        compiler_params=_params(("arbitrary",)),
    )(proj, wa, wx, ba, bx, sp, cw, cb)


def _layernorm_fwd(x):
    mu = jnp.mean(x, axis=-1, keepdims=True)
    xc = x - mu
    rstd = lax.rsqrt(jnp.mean(xc * xc, axis=-1, keepdims=True) + EPS)
    return xc * rstd, rstd


def _sgu_mix(vn_bf, wm_ref, bsb_ref, ts):
    rows = []
    for blk in range(ts // SGU_BLOCK):
        r0 = blk * SGU_BLOCK
        cols = [
            _dot(wm_ref[g], vn_bf[r0:r0 + SGU_BLOCK, g * SGU_BLOCK:(g + 1) * SGU_BLOCK]) + bsb_ref[g]
            for g in range(SGU_GROUPS)
        ]
        rows.append(jnp.concatenate(cols, axis=1))
    return jnp.concatenate(rows, axis=0)


def _sgu_fwd_call(proj, wm, bsb, lg, lb, ts):
    s = proj.shape[0]

    def body(uv_ref, wm_ref, bsb_ref, lg_ref, lb_ref, yb_ref):
        gu = _gelu(uv_ref[:, :D_SGU])
        gv = _gelu(uv_ref[:, D_SGU:2 * D_SGU]).astype(F32)
        nh, _ = _layernorm_fwd(gv)
        vn = (nh * lg_ref[...] + lb_ref[...]).astype(BF)
        yb_ref[...] = (gu * _sgu_mix(vn, wm_ref, bsb_ref, ts)).astype(BF)

    sw = (SGU_GROUPS, SGU_BLOCK, SGU_BLOCK)
    return pl.pallas_call(
        body, name="sgu_fwd", grid=(s // ts,),
        in_specs=[_tile_spec(ts, 2 * D_RNN, 1), _full_spec(sw), _full_spec(sw),
                  _full_spec((1, D_SGU)), _full_spec((1, D_SGU))],
        out_specs=_tile_spec(ts, D_SGU),
        out_shape=jax.ShapeDtypeStruct((s, D_SGU), BF),
        compiler_params=_params(("parallel",)),
    )(proj, wm, bsb, lg, lb)


_GATE_COL0 = (2 * D_RNN + 2 * D_SGU) // 512


def _gate_specs(ts):
    return [_tile_spec(ts, 512, _GATE_COL0 + j) for j in range(4)]


def _merge_call(x, proj, ya_pre, yb_pre, w_ba, w_bb, w_out, g2, layer, ts):
    s = x.shape[0]

    def body(x_ref, ga0, ga1, gb0, gb1, ya_ref, yb_ref, wa_ref, wb_ref, wo_ref, g2_ref,
             x1_ref, yao_ref, ybo_ref, mg_ref, h2_ref):
        ya = _dot(ya_ref[...], wa_ref[...])
        yb = _dot(yb_ref[...], wb_ref[...])
        sa = jax.nn.sigmoid(jnp.concatenate([ga0[...], ga1[...]], axis=1).astype(F32))
        sb = jax.nn.sigmoid(jnp.concatenate([gb0[...], gb1[...]], axis=1).astype(F32))
        merged = (sa * ya + sb * yb).astype(BF)
        x1 = x_ref[...] + _dot(merged, wo_ref[...])
        x1_ref[...] = x1
        yao_ref[...] = ya.astype(BF)
        ybo_ref[...] = yb.astype(BF)
        mg_ref[...] = merged
        h2_ref[...] = (x1 * _rms_stats(x1) * g2_ref[...]).astype(BF)

    act = jax.ShapeDtypeStruct((s, D_MODEL), BF)
    return pl.pallas_call(
        body, name="merge_fwd", grid=(s // ts,),
        in_specs=[_tile_spec(ts, D_MODEL)] + _gate_specs(ts) + [
            _tile_spec(ts, D_RNN), _tile_spec(ts, D_SGU),
            _layer_spec(w_ba, layer), _layer_spec(w_bb, layer), _layer_spec(w_out, layer),
            _full_spec((1, D_MODEL))],
        out_specs=[_tile_spec(ts, D_MODEL)] * 5,
        out_shape=[jax.ShapeDtypeStruct((s, D_MODEL), F32), act, act, act, act],
        compiler_params=_params(("parallel",)),
    )(x, proj, proj, proj, proj, ya_pre, yb_pre, w_ba, w_bb, w_out, g2)


def _ffn_call(x1, h2, w_up, w_down, layer, ts):
    s = x1.shape[0]

    def body(x1_ref, h2_ref, wu_ref, wd_ref, x2_ref, p_ref):
        h2v = h2_ref[...]
        acc = x1_ref[...]
        for q in range(N_QUARTERS):
            p = _dot(h2v, wu_ref[q])
            p_ref[:, q * Q_FF:(q + 1) * Q_FF] = p.astype(BF)
            f = jnp.square(jnp.maximum(p, 0.0)).astype(BF)
            acc = acc + _dot(f, wd_ref[q * Q_FF:(q + 1) * Q_FF, :])
        x2_ref[...] = acc

    return pl.pallas_call(
        body, name="ffn_fwd", grid=(s // ts,),
        in_specs=[_tile_spec(ts, D_MODEL), _tile_spec(ts, D_MODEL),
                  pl.BlockSpec((None, N_QUARTERS, D_MODEL, Q_FF), lambda i: (layer, 0, 0, 0)),
                  pl.BlockSpec((None, D_FF, D_MODEL), lambda i: (layer, 0, 0))],
        out_specs=[_tile_spec(ts, D_MODEL), _tile_spec(ts, D_FF)],
        out_shape=[jax.ShapeDtypeStruct((s, D_MODEL), F32), jax.ShapeDtypeStruct((s, D_FF), BF)],
        compiler_params=_params(("parallel",)),
    )(x1, h2, w_up, w_down)


def _loss_call(x, target, gf, ts):
    s = x.shape[0]

    def body(x_ref, t_ref, g_ref, dx_ref, loss_ref, dg_ref):
        @pl.when(pl.program_id(0) == 0)
        def _():
            loss_ref[...] = jnp.zeros_like(loss_ref)
            dg_ref[...] = jnp.zeros_like(dg_ref)

        xv = x_ref[...]
        gv = g_ref[...]
        err = xv * _rms_stats(xv) * gv - t_ref[...]
        part = 0.5 * jnp.sum(jnp.mean(err * err, axis=-1, keepdims=True), axis=0, keepdims=True)
        loss_ref[...] += jnp.broadcast_to(part, loss_ref.shape)
        dx, dg = _rms_bwd(err * (1.0 / D_MODEL), xv, gv)
        dx_ref[...] = dx
        dg_ref[...] += _row_sum(dg)

    return pl.pallas_call(
        body, name="loss_head", grid=(s // ts,),
        in_specs=[_tile_spec(ts, D_MODEL), _tile_spec(ts, D_MODEL), _full_spec((1, D_MODEL))],
        out_specs=[_tile_spec(ts, D_MODEL), _full_spec((1, 128)), _full_spec((1, D_MODEL))],
        out_shape=[jax.ShapeDtypeStruct((s, D_MODEL), F32), jax.ShapeDtypeStruct((1, 128), F32),
                   jax.ShapeDtypeStruct((1, D_MODEL), F32)],
        compiler_params=_params(("arbitrary",)),
    )(x, target, gf)


def _ffn_bwd_call(dx2, p, x1, g2, w_up, w_down, layer, ts):
    s = dx2.shape[0]

    def body(dx2_ref, p_ref, x1_ref, g2_ref, wu_ref, wd_ref, dx1_ref, dp_ref, dg_ref, dx2b_ref, dx1b_ref):
        @pl.when(pl.program_id(0) == 0)
        def _():
            dg_ref[...] = jnp.zeros_like(dg_ref)

        dx2v = dx2_ref[...]
        dyb = dx2v.astype(BF)
        dx2b_ref[...] = dyb
        dh2 = jnp.zeros((ts, D_MODEL), F32)
        for q in range(N_QUARTERS):
            cols = slice(q * Q_FF, (q + 1) * Q_FF)
            df = _dot_nt(dyb, wd_ref[cols, :])
            dp = (df * (2.0 * jnp.maximum(p_ref[:, cols].astype(F32), 0.0))).astype(BF)
            dp_ref[:, cols] = dp
            dh2 = dh2 + _dot_nt(dp, wu_ref[q])
        dx, dg = _rms_bwd(dh2, x1_ref[...], g2_ref[...])
        dx1 = dx2v + dx
        dx1_ref[...] = dx1
        dx1b_ref[...] = dx1.astype(BF)
        dg_ref[...] += _row_sum(dg)

    return pl.pallas_call(
        body, name="ffn_bwd", grid=(s // ts,),
        in_specs=[_tile_spec(ts, D_MODEL), _tile_spec(ts, D_FF), _tile_spec(ts, D_MODEL),
                  _full_spec((1, D_MODEL)),
                  pl.BlockSpec((None, N_QUARTERS, D_MODEL, Q_FF), lambda i: (layer, 0, 0, 0)),
                  pl.BlockSpec((None, D_FF, D_MODEL), lambda i: (layer, 0, 0))],
        out_specs=[_tile_spec(ts, D_MODEL), _tile_spec(ts, D_FF), _full_spec((1, D_MODEL)),
                   _tile_spec(ts, D_MODEL), _tile_spec(ts, D_MODEL)],
        out_shape=[jax.ShapeDtypeStruct((s, D_MODEL), F32), jax.ShapeDtypeStruct((s, D_FF), BF),
                   jax.ShapeDtypeStruct((1, D_MODEL), F32),
                   jax.ShapeDtypeStruct((s, D_MODEL), BF), jax.ShapeDtypeStruct((s, D_MODEL), BF)],
        compiler_params=_params(("arbitrary",)),
    )(dx2, p, x1, g2, w_up, w_down)


def _merge_bwd_call(dx1, proj, ya, yb, w_ba, w_bb, w_out, layer, ts, after=None):
    s = dx1.shape[0]

    def body(dx1_ref, ga0, ga1, gb0, gb1, ya_ref, yb_ref, wa_ref, wb_ref, wo_ref, *rest):
        dya_ref, dyb_ref, dgate_ref, dyap_ref, dybp_ref = rest[-5:]
        dm = _dot_nt(dx1_ref[...].astype(BF), wo_ref[...])
        sa = jax.nn.sigmoid(jnp.concatenate([ga0[...], ga1[...]], axis=1).astype(F32))
        sb = jax.nn.sigmoid(jnp.concatenate([gb0[...], gb1[...]], axis=1).astype(F32))
        dya = (dm * sa).astype(BF)
        dyb = (dm * sb).astype(BF)
        dya_ref[...] = dya
        dyb_ref[...] = dyb
        dgate_ref[:, :D_MODEL] = (dm * ya_ref[...].astype(F32) * sa * (1.0 - sa)).astype(BF)
        dgate_ref[:, D_MODEL:] = (dm * yb_ref[...].astype(F32) * sb * (1.0 - sb)).astype(BF)
        dyap_ref[...] = _dot_nt(dya, wa_ref[...]).astype(BF)
        dybp_ref[...] = _dot_nt(dyb, wb_ref[...]).astype(BF)

    act = jax.ShapeDtypeStruct((s, D_MODEL), BF)
    return pl.pallas_call(
        body, name="merge_bwd", grid=(s // ts,),
        in_specs=[_tile_spec(ts, D_MODEL)] + _gate_specs(ts) + [
            _tile_spec(ts, D_MODEL), _tile_spec(ts, D_MODEL),
            _layer_spec(w_ba, layer), _layer_spec(w_bb, layer), _layer_spec(w_out, layer)]
        + ([] if after is None else [pl.BlockSpec(memory_space=pl.ANY)]),
        out_specs=[_tile_spec(ts, D_MODEL), _tile_spec(ts, D_MODEL), _tile_spec(ts, 2 * D_MODEL),
                   _tile_spec(ts, D_RNN), _tile_spec(ts, D_SGU)],
        out_shape=[act, act, jax.ShapeDtypeStruct((s, 2 * D_MODEL), BF),
                   jax.ShapeDtypeStruct((s, D_RNN), BF), jax.ShapeDtypeStruct((s, D_SGU), BF)],
        compiler_params=_params(("parallel",)),
    )(dx1, proj, proj, proj, proj, ya, yb, w_ba, w_bb, w_out, *([] if after is None else [after]))


def _sgu_bwd_call(dyb_pre, proj, wm, bsb, mask, lg, lb, ts):
    s = proj.shape[0]

    def body(dy_ref, uv_ref, wm_ref, bsb_ref, mask_ref, lg_ref, lb_ref,
             duv_ref, dws_ref, dbs_ref, dlg_ref, dlb_ref, dm_sc):
        step = pl.program_id(0)

        @pl.when(step == 0)
        def _():
            dws_ref[...] = jnp.zeros_like(dws_ref)
            dlg_ref[...] = jnp.zeros_like(dlg_ref)
            dlb_ref[...] = jnp.zeros_like(dlb_ref)
            dm_sc[...] = jnp.zeros_like(dm_sc)

        gu, dgu_du = _gelu_and_grad(uv_ref[:, :D_SGU])
        gv, dgv_dv = _gelu_and_grad(uv_ref[:, D_SGU:2 * D_SGU])
        nh, rstd = _layernorm_fwd(gv.astype(F32))
        lgv = lg_ref[...]
        vn = (nh * lgv + lb_ref[...]).astype(BF)
        dy = dy_ref[...].astype(F32)
        du = dy * _sgu_mix(vn, wm_ref, bsb_ref, ts) * dgu_du
        dmix = dy * gu
        dmix_bf = dmix.astype(BF)
        dm_acc = dm_sc[...]
        rows = []
        for blk in range(ts // SGU_BLOCK):
            r0 = blk * SGU_BLOCK
            dm_acc = dm_acc + dmix[r0:r0 + SGU_BLOCK, :]
            cols = []
            for g in range(SGU_GROUPS):
                c0 = g * SGU_BLOCK
                dmg = dmix_bf[r0:r0 + SGU_BLOCK, c0:c0 + SGU_BLOCK]
                cols.append(_dot_tn(wm_ref[g], dmg))
                dws_ref[g] += mask_ref[...] * _dot_nt(dmg, vn[r0:r0 + SGU_BLOCK, c0:c0 + SGU_BLOCK])
            rows.append(jnp.concatenate(cols, axis=1))
        dm_sc[...] = dm_acc
        dvn = jnp.concatenate(rows, axis=0)
        dlg_ref[...] += _row_sum(dvn * nh)
        dlb_ref[...] += _row_sum(dvn)
        dnh = dvn * lgv
        dgv = rstd * (dnh - jnp.mean(dnh, axis=-1, keepdims=True)
                      - nh * jnp.mean(dnh * nh, axis=-1, keepdims=True))
        duv_ref[:, :D_SGU] = du.astype(BF)
        duv_ref[:, D_SGU:] = (dgv * dgv_dv).astype(BF)

        @pl.when(step == pl.num_programs(0) - 1)
        def _():
            for g in range(SGU_GROUPS):
                dbs_ref[:, g:g + 1] = jnp.sum(
                    dm_acc[:, g * SGU_BLOCK:(g + 1) * SGU_BLOCK], axis=1, keepdims=True)

    sw = (SGU_GROUPS, SGU_BLOCK, SGU_BLOCK)
    return pl.pallas_call(
        body, name="sgu_bwd", grid=(s // ts,),
        in_specs=[_tile_spec(ts, D_SGU), _tile_spec(ts, 2 * D_RNN, 1), _full_spec(sw), _full_spec(sw),
                  _full_spec((SGU_BLOCK, SGU_BLOCK)), _full_spec((1, D_SGU)), _full_spec((1, D_SGU))],
        out_specs=[_tile_spec(ts, 2 * D_SGU), _full_spec(sw), _full_spec((SGU_BLOCK, SGU_GROUPS)),
                   _full_spec((1, D_SGU)), _full_spec((1, D_SGU))],
        out_shape=[jax.ShapeDtypeStruct((s, 2 * D_SGU), BF), jax.ShapeDtypeStruct(sw, F32),
                   jax.ShapeDtypeStruct((SGU_BLOCK, SGU_GROUPS), F32),
                   jax.ShapeDtypeStruct((1, D_SGU), F32), jax.ShapeDtypeStruct((1, D_SGU), F32)],
        scratch_shapes=[pltpu.VMEM((SGU_BLOCK, D_SGU), F32)],
        compiler_params=_params(("arbitrary",)),
    )(dyb_pre, proj, wm, bsb, mask, lg, lb)


_ROW_DBA, _ROW_DBX, _ROW_DSP, _ROW_DCB, _ROW_DCW = 0, 1, 2, 3, 4
_PREV_ROWS = 16


def _rnn_bwd_call(dya_pre, proj, xr_saved, hr, wa, wx, ba, bx, sp, cw, ts):
    s = proj.shape[0]
    nt = s // ts
    per = ts // _PREV_ROWS

    def tile(i):
        return nt - 1 - i

    def prev(i):
        return jnp.maximum(tile(i) * per - 1, 0)

    def body(dy_ref, xg_ref, xr_ref, hr_ref, hrp_ref, wa_ref, wx_ref, ba_ref, bx_ref, sp_ref,
             cw_ref, dxg_ref, dwa_ref, dwx_ref, vec_ref,
             lam_carry, a_first, dxr_head, al_sc, bl_sc, lam_sc):
        step = pl.program_id(0)

        @pl.when(step == 0)
        def _():
            dwa_ref[...] = jnp.zeros_like(dwa_ref)
            dwx_ref[...] = jnp.zeros_like(dwx_ref)
            vec_ref[...] = jnp.zeros_like(vec_ref)
            lam_carry[...] = jnp.zeros_like(lam_carry)
            a_first[...] = jnp.zeros_like(a_first)
            dxr_head[...] = jnp.zeros_like(dxr_head)

        has_prev = (step < nt - 1).astype(F32)
        x = xg_ref[:, :D_RNN].astype(F32)
        g = xg_ref[:, D_RNN:]
        h_tail =hrp_ref[_PREV_ROWS - SUBLANES:, :].astype(F32) * has_prev
        xr = xr_ref[...].astype(F32)
        r, i, a, nrm, inv_nrm = _lru_gates(xr, wa_ref, wx_ref, ba_ref, bx_ref, sp_ref)
        h = hr_ref[...].astype(F32)
        dy = dy_ref[...].astype(F32)
        gg, dgg = _gelu_and_grad(g)

        coef = _shift_up(a, jnp.broadcast_to(a_first[...], (SUBLANES, D_RNN)), 1)
        lam_carry[...] = _linear_scan(coef, dy * gg, lam_carry[...], al_sc, bl_sc, lam_sc, True)
        a_first[...] = a[0:1, :]
        lam = lam_sc[...]

        da = lam * _shift_down(h, h_tail, 1)
        dnrm = lam * (i * xr)
        di = lam * nrm * xr
        dlog_a = da * a - dnrm * (a * a) * inv_nrm
        spv = sp_ref[...]
        dza = (dlog_a * (-LRU_C * spv)) * (r * (1.0 - r))
        dzx = di * (i * (1.0 - i))
        vec_ref[_ROW_DSP:_ROW_DSP + 1, :] += _row_sum(dlog_a * (-LRU_C * r))
        vec_ref[_ROW_DBA:_ROW_DBA + 1, :] += _row_sum(dza)
        vec_ref[_ROW_DBX:_ROW_DBX + 1, :] += _row_sum(dzx)
        xb = xr.astype(BF)
        dza_bf = dza.astype(BF)
        dzx_bf = dzx.astype(BF)
        for grp in range(N_LRU_GROUPS):
            cols = slice(grp * LRU_GROUP, (grp + 1) * LRU_GROUP)
            dwa_ref[grp] += _dot_tn(xb[:, cols], dza_bf[:, cols])
            dwx_ref[grp] += _dot_tn(xb[:, cols], dzx_bf[:, cols])
        dxr = (lam * nrm * i + _group_dot(dza_bf, wa_ref, _dot_nt) + _group_dot(dzx_bf, wx_ref, _dot_nt))

        vec_ref[_ROW_DCB:_ROW_DCB + 1, :] += _row_sum(dxr)
        head = dxr_head[...]
        dx = cw_ref[CONV_WIDTH - 1:CONV_WIDTH, :] * dxr
        vec_ref[_ROW_DCW + 3:_ROW_DCW + 4, :] += _row_sum(dxr * x)
        for sft in range(1, CONV_WIDTH):
            k = CONV_WIDTH - 1 - sft
            ahead = _shift_up(dxr, head, sft)
            dx = dx + cw_ref[k:k + 1, :] * ahead
            vec_ref[_ROW_DCW + k:_ROW_DCW + k + 1, :] += _row_sum(ahead * x)
        dxr_head[...] = dxr[0:SUBLANES, :]
        dxg_ref[:, :D_RNN] = dx.astype(BF)
        dxg_ref[:, D_RNN:] = (dy * h * dgg).astype(BF)

    gw = (N_LRU_GROUPS, LRU_GROUP, LRU_GROUP)
    rev = lambda width: pl.BlockSpec((ts, width), lambda i: (tile(i), 0))
    return pl.pallas_call(
        body, name="rnn_bwd", grid=(nt,),
        in_specs=[rev(D_RNN), rev(2 * D_RNN), rev(D_RNN), rev(D_RNN),
                  pl.BlockSpec((_PREV_ROWS, D_RNN), lambda i: (prev(i), 0)),
                  _full_spec(gw), _full_spec(gw),
                  _full_spec((1, D_RNN)), _full_spec((1, D_RNN)), _full_spec((1, D_RNN)),
                  _full_spec((CONV_WIDTH, D_RNN))],
        out_specs=[rev(2 * D_RNN), _full_spec(gw), _full_spec(gw), _full_spec((SUBLANES, D_RNN))],
        out_shape=[jax.ShapeDtypeStruct((s, 2 * D_RNN), BF), jax.ShapeDtypeStruct(gw, F32),
                   jax.ShapeDtypeStruct(gw, F32), jax.ShapeDtypeStruct((SUBLANES, D_RNN), F32)],
        scratch_shapes=[pltpu.VMEM((1, D_RNN), F32), pltpu.VMEM((1, D_RNN), F32),
                        pltpu.VMEM((SUBLANES, D_RNN), F32),
                        pltpu.VMEM((ts, D_RNN), F32), pltpu.VMEM((ts, D_RNN), F32),
                        pltpu.VMEM((ts, D_RNN), F32)],
        compiler_params=_params(("arbitrary",)),
    )(dya_pre, proj, xr_saved, hr, hr, wa, wx, ba, bx, sp, cw)


def _inproj_bwd_call(dxg, duv, dgate, dx1, x, g1, w_in, layer, ts):
    s = x.shape[0]

    def body(dxg_ref, duv_ref, dgt_ref, dx1_ref, x_ref, g_ref, w_ref, dx_ref, dproj_ref, dg_ref):
        @pl.when(pl.program_id(0) == 0)
        def _():
            dg_ref[...] = jnp.zeros_like(dg_ref)

        dproj = jnp.concatenate([dxg_ref[...], duv_ref[...], dgt_ref[...]], axis=1)
        dproj_ref[...] = dproj
        dh = jnp.zeros((ts, D_MODEL), F32)
        for q in range(N_QUARTERS):
            dh = dh + _dot_nt(dproj[:, q * Q_IN:(q + 1) * Q_IN], w_ref[q])
        dx, dg = _rms_bwd(dh, x_ref[...], g_ref[...])
        dx_ref[...] = dx1_ref[...] + dx
        dg_ref[...] += _row_sum(dg)

    return pl.pallas_call(
        body, name="inproj_bwd", grid=(s // ts,),
        in_specs=[_tile_spec(ts, 2 * D_RNN), _tile_spec(ts, 2 * D_SGU), _tile_spec(ts, 2 * D_MODEL),
                  _tile_spec(ts, D_MODEL), _tile_spec(ts, D_MODEL), _full_spec((1, D_MODEL)),
                  pl.BlockSpec((None, N_QUARTERS, D_MODEL, Q_IN), lambda i: (layer, 0, 0, 0))],
        out_specs=[_tile_spec(ts, D_MODEL), _tile_spec(ts, D_IN), _full_spec((1, D_MODEL))],
        out_shape=[jax.ShapeDtypeStruct((s, D_MODEL), F32), jax.ShapeDtypeStruct((s, D_IN), BF),
                   jax.ShapeDtypeStruct((1, D_MODEL), F32)],
        compiler_params=_params(("arbitrary",)),
    )(dxg, duv, dgate, dx1, x, g1, w_in)


def _relu_sq(p):
    return jnp.square(jnp.maximum(p, 0))


def _wgrad_call(a, b, core, tm, tn, tk, col_blocked, name, a_fn=None):
    s, m = a.shape
    n = b.shape[1]
    r, cols = (m, n // N_QUARTERS) if col_blocked else (m // N_QUARTERS, n)
    r2 = r // 2
    per_tile = tm // r
    steps = s // tk
    assert per_tile > 0 or steps == 1

    def body(core_ref, a_ref, b_ref, keep_ref, send_ref, *acc):
        av = a_ref[...]
        if a_fn is not None:
            av = a_fn(av)
        prod = _dot_tn(av.astype(BF), b_ref[...].astype(BF))

        def emit(total):
            for h in range(2):
                @pl.when(core_ref[0] == h)
                def _():
                    for q in range(per_tile):
                        keep_ref[q] = total[q * r + h * r2:q * r + (h + 1) * r2].astype(BF)
                        send_ref[q] = total[q * r + (1 - h) * r2:q * r + (2 - h) * r2].astype(BF)

        if per_tile == 0:
            mine = pl.program_id(1) == core_ref[0]

            @pl.when(mine)
            def _():
                keep_ref[0] = prod.astype(BF)

            @pl.when(jnp.logical_not(mine))
            def _():
                send_ref[0] = prod.astype(BF)
        elif steps == 1:
            emit(prod)
        else:
            acc_ref, = acc
            step = pl.program_id(2)

            @pl.when(step == 0)
            def _():
                acc_ref[...] = prod

            @pl.when(jnp.logical_and(step > 0, step < steps - 1))
            def _():
                acc_ref[...] += prod

            @pl.when(step == steps - 1)
            def _():
                emit(acc_ref[...] + prod)

    if col_blocked:
        per_q = cols // tn
        out_spec = pl.BlockSpec((1, r2, tn), lambda j, i, k, c: (j // per_q, 0, j % per_q))
    else:
        out_spec = pl.BlockSpec((per_tile, r2, tn), lambda j, i, k, c: (i, 0, j))
    return pl.pallas_call(
        body, name=name,
        out_shape=[jax.ShapeDtypeStruct((N_QUARTERS, r2, cols), BF)] * 2,
        grid_spec=pltpu.PrefetchScalarGridSpec(
            num_scalar_prefetch=1, grid=(n // tn, m // tm, steps),
            in_specs=[pl.BlockSpec((tk, tm), lambda j, i, k, c: (k, i)),
                      pl.BlockSpec((tk, tn), lambda j, i, k, c: (k, j))],
            out_specs=[out_spec, out_spec],
            scratch_shapes=[] if steps == 1 else [pltpu.VMEM((tm, tn), F32)]),
        compiler_params=_params(("parallel", "parallel", "arbitrary")),
    )(core, a, b)


BIG = ("w_in", "w_up", "w_down", "w_branch_a", "w_branch_b", "w_out")


def _block_diag(w):
    w4 = w.reshape(N_LRU_GROUPS, HEADS_PER_GROUP, RNN_HEAD_DIM, RNN_HEAD_DIM)
    eye = jnp.eye(HEADS_PER_GROUP, dtype=w.dtype)
    return jnp.einsum("gjio,jk->gjiko", w4, eye).reshape(N_LRU_GROUPS, LRU_GROUP, LRU_GROUP)


def _block_diag_extract(d):
    d5 = d.reshape(N_LRU_GROUPS, HEADS_PER_GROUP, RNN_HEAD_DIM, HEADS_PER_GROUP, RNN_HEAD_DIM)
    blocks = [d5[:, j, :, j, :] for j in range(HEADS_PER_GROUP)]
    return jnp.stack(blocks, axis=1).reshape(RNN_HEADS, RNN_HEAD_DIM, RNN_HEAD_DIM)


def _sgu_mask():
    chunk = jnp.arange(SGU_BLOCK) // CHUNK
    return (chunk[:, None] >= chunk[None, :]).astype(F32)


def _layer_small(sm, l, core):
    row = lambda v: v.reshape(1, -1)
    return dict(
        core=core,
        g1=row(sm["norm_mix_g"][l]), g2=row(sm["norm_ffn_g"][l]),
        wa=_block_diag(sm["lru_w_a"][l]).astype(BF), wx=_block_diag(sm["lru_w_x"][l]).astype(BF),
        ba=row(sm["lru_b_a"][l]), bx=row(sm["lru_b_x"][l]),
        sp=row(jax.nn.softplus(-sm["lru_lambda"][l])),
        cw=sm["conv_w"][l] if "conv_w" in sm else None, cb=row(sm["conv_b"][l]),
        wm=(sm["sgu_w_s"][l] * _sgu_mask()).astype(BF),
        bsb=jnp.broadcast_to(sm["sgu_b_s"][l][:, :, None], (SGU_GROUPS, SGU_BLOCK, SGU_BLOCK)),
        lg=row(sm["sgu_ln_g"][l]), lb=row(sm["sgu_ln_b"][l]),
    )


def _layer_fwd_mix(x, big, p, ts, h=None, before_sgu=None, proj=None):
    if h is None:
        h = _norm_call(x, p["g1"], ts)
    if proj is None:
        proj = _inproj_call(h, big["w_in"], 0, 2 * ts)
    xr, hr, ya_pre = _rnn_fwd_call(proj, p["wa"], p["wx"], p["ba"], p["bx"], p["sp"], p["cw"], p["cb"], ts)
    lg = p["lg"] if before_sgu is None else p["lg"] + before_sgu(ya_pre)
    yb_pre = _sgu_fwd_call(proj, p["wm"], p["bsb"], lg, p["lb"], ts)
    return dict(p=p, x=x, h=h, proj=proj, xr=xr, hr=hr, ya_pre=ya_pre, yb_pre=yb_pre)


def _layer_fwd_out(sv, big, ts):
    x1, ya, yb, merged, h2 = _merge_call(sv["x"], sv["proj"], sv["ya_pre"], sv["yb_pre"], big["w_branch_a"],
                                         big["w_branch_b"], big["w_out"], sv["p"]["g2"], 0, ts)
    x2, pre = _ffn_call(x1, h2, big["w_up"], big["w_down"], 0, ts)
    sv.update(x1=x1, ya=ya, yb=yb, merged=merged, h2=h2, pre=pre)
    return x2


def _layer_bwd_ffn(dx, sv, big, ts):
    p = sv["p"]
    dx1, dpre, dg2, dx_bf, sv["dx1_bf"] = _ffn_bwd_call(dx, sv["pre"], sv["x1"], p["g2"], big["w_up"],
                                                       big["w_down"], 0, ts)
    tk = dx.shape[0]
    gb = dict(
        w_down=_wgrad_call(sv["pre"], dx_bf, p["core"], Q_FF, D_MODEL, tk, False, "wgrad_down", a_fn=_relu_sq),
        w_up=_wgrad_call(sv["h2"], dpre, p["core"], D_MODEL, Q_FF, tk, True, "wgrad_up"))
    return dx1, gb, dict(norm_ffn_g=dg2[0])


def _layer_bwd_merge(dx1, sv, big, ts, after=None):
    tk = dx1.shape[0]
    core = sv["p"]["core"]
    dya, dyb, dgate, dya_pre, dyb_pre = _merge_bwd_call(
        dx1, sv["proj"], sv["ya"], sv["yb"], big["w_branch_a"], big["w_branch_b"], big["w_out"], 0, ts, after)
    gb = dict(
        w_out=_wgrad_call(sv["merged"], sv["dx1_bf"], core, D_MODEL, D_MODEL, tk, False, "wgrad_out"),
        w_branch_a=_wgrad_call(sv["ya_pre"], dya, core, D_RNN, D_MODEL // 2, tk, False, "wgrad_branch_a"),
        w_branch_b=_wgrad_call(sv["yb_pre"], dyb, core, D_SGU, D_MODEL, tk, False, "wgrad_branch_b"))
    return (dgate, dya_pre, dyb_pre), gb


def _layer_bwd_branches(dx1, merge_out, sv, big, lam, ts, after_sgu=None):
    p = sv["p"]
    tk = dx1.shape[0]
    dgate, dya_pre, dyb_pre = merge_out
    gb = {}
    duv, dws, dbs, dlg, dlb = _sgu_bwd_call(dyb_pre, sv["proj"], p["wm"], p["bsb"], _sgu_mask(), p["lg"], p["lb"],
                                            ts)
    ba = p["ba"] if after_sgu is None else p["ba"] + after_sgu(duv)
    dxg, dwa, dwx, vec = _rnn_bwd_call(dya_pre, sv["proj"], sv["xr"], sv["hr"], p["wa"], p["wx"], ba, p["bx"],
                                       p["sp"], p["cw"], ts)
    dx, dproj, dg1 = _inproj_bwd_call(dxg, duv, dgate, dx1, sv["x"], p["g1"], big["w_in"], 0, ts)
    gb["w_in"] = _wgrad_call(sv["h"], dproj, p["core"], D_MODEL // 2, Q_IN, tk, True, "wgrad_in")
    gs = dict(
        norm_mix_g=dg1[0], conv_w=vec[_ROW_DCW:_ROW_DCW + CONV_WIDTH], conv_b=vec[_ROW_DCB],
        lru_w_a=_block_diag_extract(dwa), lru_w_x=_block_diag_extract(dwx),
        lru_b_a=vec[_ROW_DBA].reshape(RNN_HEADS, RNN_HEAD_DIM), lru_b_x=vec[_ROW_DBX].reshape(RNN_HEADS, RNN_HEAD_DIM),
        lru_lambda=-vec[_ROW_DSP] * jax.nn.sigmoid(-lam),
        sgu_ln_g=dlg[0], sgu_ln_b=dlb[0], sgu_w_s=dws, sgu_b_s=dbs.T)
    return dx, gb, gs


def _local_step(x, target, big, sm, ts):
    saved = []
    core = jnp.zeros((1,), jnp.int32)
    for l in range(DEPTH):
        sv = _layer_fwd_mix(x, big[l], _layer_small(sm, l, core), ts)
        x = _layer_fwd_out(sv, big[l], ts)
        saved.append(sv)
    dx, loss, dgf = _loss_call(x, target, sm["final_norm_g"].reshape(1, -1), ts)
    gb, gs = [None] * DEPTH, [None] * DEPTH
    for l in reversed(range(DEPTH)):
        dx1, gb_ffn, gs_ffn = _layer_bwd_ffn(dx, saved[l], big[l], ts)
        merge_out, gb_merge = _layer_bwd_merge(dx1, saved[l], big[l], ts)
        dx, gb_mix, gs_mix = _layer_bwd_branches(dx1, merge_out, saved[l], big[l], sm["lru_lambda"][l], ts)
        gb[l] = {**gb_ffn, **gb_merge, **gb_mix}
        gs[l] = {**gs_ffn, **gs_mix}
    gs = {k: jnp.stack([g[k] for g in gs]) for k in gs[0]}
    gs["final_norm_g"] = dgf[0]
    return loss, dx, gb, gs


EW_VMEM_BYTES = 24 * 1024 * 1024


def _row_block(rows, cols, bytes_per_elem):
    for br in range(min(rows, EW_VMEM_BYTES // (2 * bytes_per_elem * cols)), 0, -1):
        if rows % br == 0 and br % 16 == 0:
            return br
    return rows


def _ew_call(fn, name, operands, outputs, slabs=1, sel=None, into=None, after=None):
    if into is not None and not isinstance(into, (list, tuple)):
        into = [into]
    rows, cols = outputs[0][0].shape[2:]
    br = _row_block(rows, cols, sum(jnp.dtype(a.dtype).itemsize for a, _ in operands + outputs))
    n_in = len(operands)

    def pick(tok, g, s):
        if callable(tok):
            return tok(g, s)
        if tok == "g":
            return g
        if isinstance(tok, tuple):
            return s[tok[1]]
        return tok

    def spec(idx):
        return pl.BlockSpec((None, None, br, cols),
                            lambda g, i, s, idx=idx: (pick(idx[0], g, s), pick(idx[1], g, s), i, 0))

    if sel is None:
        sel = jnp.zeros((1,), jnp.int32)
    in_specs = [spec(idx) for _, idx in operands]
    arrays = [a for a, _ in operands]
    aliases = {}
    for j, buf in enumerate(into or ()):
        in_specs.append(pl.BlockSpec(memory_space=pl.ANY))
        arrays.append(buf)
        aliases[1 + n_in + j] = j
    if after is not None:
        in_specs.append(pl.BlockSpec(memory_space=pl.ANY))
        arrays.append(after)

    def body(sel_ref, *refs):
        outs = fn(*[r[...] for r in refs[:n_in]])
        for o_ref, o in zip(refs[len(arrays):], outs):
            o_ref[...] = o.astype(o_ref.dtype)

    return pl.pallas_call(
        body, name=name, out_shape=[s for s, _ in outputs],
        grid_spec=pltpu.PrefetchScalarGridSpec(
            num_scalar_prefetch=1, grid=(slabs, rows // br),
            in_specs=in_specs,
            out_specs=[spec(idx) for _, idx in outputs]),
        input_output_aliases=aliases,
        compiler_params=_params(("parallel", "parallel")),
    )(sel, *arrays)


def _as4(a):
    return a.reshape((1,) * (4 - a.ndim) + a.shape)


def _adamw(w, g, m, v):
    m = ADAM_B1 * m + (1.0 - ADAM_B1) * g
    v = ADAM_B2 * v + (1.0 - ADAM_B2) * jnp.square(g)
    m_hat = m / (1.0 - ADAM_B1 ** ADAM_STEP)
    v_hat = v / (1.0 - ADAM_B2 ** ADAM_STEP)
    delta = -ADAM_LR * (m_hat / (jnp.sqrt(v_hat) + ADAM_EPS) + ADAM_WD * w)
    return delta, m, v


def _small_adamw_call(ws, gs, ms, vs):
    n = len(ws)

    def body(*refs):
        for k in range(n):
            w, g, m, v = (refs[j * n + k][...] for j in range(4))
            outs = _adamw(w, g, m, v)
            for j in range(3):
                refs[(4 + j) * n + k][...] = outs[j]

    shapes = [jax.ShapeDtypeStruct(w.shape, F32) for w in ws]
    outs = pl.pallas_call(
        body, name="adamw_small", out_shape=shapes * 3,
        in_specs=[pl.BlockSpec(memory_space=pltpu.VMEM)] * (4 * n),
        out_specs=[pl.BlockSpec(memory_space=pltpu.VMEM)] * (3 * n),
        compiler_params=_params(),
    )(*ws, *gs, *ms, *vs)
    return outs[:n], outs[n:2 * n], outs[2 * n:]


ANY = pl.BlockSpec(memory_space=pl.ANY)


def _place():
    x, y, c = lax.axis_index("x"), lax.axis_index("y"), lax.axis_index("c")
    chips = [(1 - x, y), (x, 1 - y), (1 - x, 1 - y)]
    return x, y, c, chips


def _remote(src, dst, send_sem, recv_sem, to):
    return pltpu.make_async_remote_copy(src_ref=src, dst_ref=dst, send_sem=send_sem, recv_sem=recv_sem,
                                        device_id=to, device_id_type=MESH)


def _sibling_send_call(items):
    n = len(items)

    def body(*refs):
        src, out = refs[:n], refs[n:2 * n]
        send_sems, recv_sems = refs[2 * n:]
        x, y, c, _ = _place()
        copies = [_remote(src[w], out[w], send_sems.at[w], recv_sems.at[w], (x, y, 1 - c)) for w in range(n)]
        for cp in copies:
            cp.start()
        for cp in copies:
            cp.wait()

    return pl.pallas_call(
        body, name="grads_to_sibling",
        out_shape=[jax.ShapeDtypeStruct(a.shape, a.dtype) for a in items],
        in_specs=[ANY] * n, out_specs=[ANY] * n,
        scratch_shapes=[pltpu.SemaphoreType.DMA((n,)), pltpu.SemaphoreType.DMA((n,))],
        compiler_params=_params(vmem=False, has_side_effects=True),
    )(*items)


def _sibling_inplace_call(name, bufs, slabs, n_pairs):
    n = len(bufs)

    def body(*refs):
        out = refs[n:2 * n]
        send_sems, recv_sems = refs[2 * n:]
        x, y, c, _ = _place()
        sibling = (x, y, 1 - c)
        pairs = [pair for w, ref in enumerate(out) for pair in slabs(ref, c, w)]
        sends = [_remote(s, s, send_sems.at[k], recv_sems.at[k], sibling) for k, (s, _) in enumerate(pairs)]
        for cp in sends:
            cp.start()
        for k, (_, r) in enumerate(pairs):
            _remote(r, r, send_sems.at[k], recv_sems.at[k], sibling).wait_recv()
        for cp in sends:
            cp.wait_send()

    return pl.pallas_call(
        body, name=name,
        out_shape=[jax.ShapeDtypeStruct(a.shape, a.dtype) for a in bufs],
        in_specs=[ANY] * n, out_specs=[ANY] * n,
        input_output_aliases={w: w for w in range(n)},
        scratch_shapes=[pltpu.SemaphoreType.DMA((n_pairs,)), pltpu.SemaphoreType.DMA((n_pairs,))],
        compiler_params=_params(vmem=False, has_side_effects=True),
    )(*bufs)


HBM_SPEC = pl.BlockSpec(memory_space=pltpu.HBM)
SEM_SPEC = pl.BlockSpec(memory_space=pltpu.SEMAPHORE)
DATAFLOW_EFFECT = pltpu.SideEffectType.DATAFLOW_SIDE_EFFECTING


def _exchange_start(name, bufs, copies, n_copies, after):
    n = len(bufs)

    def body(*refs):
        ins, send_sems, recv_sems, token = refs[:n], refs[n + 1], refs[n + 2], refs[-1]
        for k, (src, dst, to) in enumerate(copies(ins)):
            _remote(src, dst, send_sems.at[k], recv_sems.at[k], to).start()
        token[...] = jnp.zeros_like(token)

    outs = pl.pallas_call(
        body, name=name,
        out_shape=(pltpu.SemaphoreType.DMA((n_copies,)), pltpu.SemaphoreType.DMA((n_copies,)),
                   *[pltpu.HBM(b.shape, b.dtype) for b in bufs], jax.ShapeDtypeStruct((SUBLANES, 128), F32)),
        in_specs=[HBM_SPEC] * n + [ANY],
        out_specs=(SEM_SPEC, SEM_SPEC, *[HBM_SPEC] * n, pl.BlockSpec(memory_space=pltpu.VMEM)),
        input_output_aliases={w: w + 2 for w in range(n)},
        compiler_params=pltpu.CompilerParams(has_side_effects=DATAFLOW_EFFECT),
    )(*[pltpu.with_memory_space_constraint(b, pltpu.HBM) for b in bufs], after)
    return outs[0], outs[1], list(outs[2:2 + n]), outs[-1]


def _exchange_wait(name, send_sems, recv_sems, bufs, copies, after):
    n = len(bufs)

    def body(*refs):
        ins, send_sems, recv_sems = refs[:n], refs[n], refs[n + 1]
        for k, (src, dst, to) in enumerate(copies(ins)):
            cp = _remote(src, dst, send_sems.at[k], recv_sems.at[k], to)
            cp.wait_send()
            cp.wait_recv()

    return pl.pallas_call(
        body, name=name,
        out_shape=[pltpu.HBM(b.shape, b.dtype) for b in bufs],
        in_specs=[HBM_SPEC] * n + [SEM_SPEC, SEM_SPEC, ANY],
        out_specs=[HBM_SPEC] * n,
        input_output_aliases={w: w for w in range(n)},
        compiler_params=pltpu.CompilerParams(has_side_effects=DATAFLOW_EFFECT),
    )(*bufs, send_sems, recv_sems, after)


def _gather_copies(refs):
    x, y, c, chips = _place()
    mine = 2 * (2 * x + y) + c
    return [(ref.at[mine], ref.at[mine], (qx, qy, c)) for ref in refs for qx, qy in chips]


def _forward_copies(refs):
    x, y, c, chips = _place()
    return [(ref.at[2 * (2 * qx + qy) + c], ref.at[2 * (2 * qx + qy) + c], (x, y, 1 - c))
            for ref in refs for qx, qy in chips]


def _gather_forward_slabs(ref, c, w):
    x, y, _, chips = _place()
    return [(ref.at[2 * (2 * qx + qy) + c], ref.at[2 * (2 * qx + qy) + 1 - c]) for qx, qy in chips]


def _device_peers():
    x, y, c, _ = _place()
    return 4 * x + 2 * y + c, [(k, (x ^ ((k >> 2) & 1), y ^ ((k >> 1) & 1), c ^ (k & 1))) for k in range(1, 8)]


def _small_scatter_copies(refs):
    me, peers = _device_peers()
    return [(refs[0].at[me ^ k], refs[1].at[me], to) for k, to in peers]


def _small_spread_copies(refs):
    me, peers = _device_peers()
    return [(refs[0].at[me], refs[0].at[me], to) for _, to in peers]


def _sibling_copies(refs):
    n = len(refs) // 2
    x, y, c, _ = _place()
    return [(refs[w], refs[n + w], (x, y, 1 - c)) for w in range(n)]


def _owner_copies(refs):
    n = len(refs) // 2
    x, y, c, chips = _place()
    return [(refs[w].at[2 * qx + qy], refs[n + w].at[j], (qx, qy, c))
            for w in range(n) for j, (qx, qy) in enumerate(chips)]


N_DEVICES = 8
SMALL_ROWS = 616


SMALL = ("norm_mix_g", "conv_w", "conv_b", "lru_w_a", "lru_b_a", "lru_w_x", "lru_b_x", "lru_lambda",
         "sgu_ln_g", "sgu_ln_b", "sgu_w_s", "sgu_b_s", "norm_ffn_g", "final_norm_g")
WEIGHTS = ("norm_mix_g", "w_in", "conv_w", "conv_b", "lru_w_a", "lru_b_a", "lru_w_x", "lru_b_x", "lru_lambda",
           "sgu_ln_g", "sgu_ln_b", "sgu_w_s", "sgu_b_s", "w_branch_a", "w_branch_b", "w_out", "norm_ffn_g",
           "w_up", "w_down", "final_norm_g")
PACK_ALIGN = SUBLANES * 128


PACKED = SMALL + ("loss",)


def _pack_small(gs):
    parts = []
    for k in PACKED:
        flat = gs[k].reshape(-1)
        parts.append(jnp.pad(flat, (0, -flat.size % PACK_ALIGN)))
    flat = jnp.concatenate(parts)
    flat = jnp.pad(flat, (0, N_DEVICES * SMALL_ROWS * 128 - flat.size))
    return flat.reshape(N_DEVICES, SMALL_ROWS, 128)


def _unpack_small(buf, like):
    flat = buf.reshape(-1)
    out, off = {}, 0
    for k in PACKED:
        size = like[k].size
        out[k] = flat[off:off + size].reshape(like[k].shape)
        off += size + (-size % PACK_ALIGN)
    return out


def _as_rows(a):
    return a.reshape(-1, a.shape[-1])


def kernel(x, norm_mix_g, w_in, conv_w, conv_b, lru_w_a, lru_b_a, lru_w_x, lru_b_x, lru_lambda, sgu_ln_g, sgu_ln_b, sgu_w_s, sgu_b_s, w_branch_a, w_branch_b, w_out, norm_ffn_g, w_up, w_down, final_norm_g, loss_target, m_norm_mix_g, m_w_in, m_conv_w, m_conv_b, m_lru_w_a, m_lru_b_a, m_lru_w_x, m_lru_b_x, m_lru_lambda, m_sgu_ln_g, m_sgu_ln_b, m_sgu_w_s, m_sgu_b_s, m_w_branch_a, m_w_branch_b, m_w_out, m_norm_ffn_g, m_w_up, m_w_down, m_final_norm_g, v_norm_mix_g, v_w_in, v_conv_w, v_conv_b, v_lru_w_a, v_lru_b_a, v_lru_w_x, v_lru_b_x, v_lru_lambda, v_sgu_ln_g, v_sgu_ln_b, v_sgu_w_s, v_sgu_b_s, v_w_branch_a, v_w_branch_b, v_w_out, v_norm_ffn_g, v_w_up, v_w_down, v_final_norm_g):
    w = dict(norm_mix_g=norm_mix_g, w_in=w_in, conv_w=conv_w, conv_b=conv_b, lru_w_a=lru_w_a, lru_b_a=lru_b_a,
             lru_w_x=lru_w_x, lru_b_x=lru_b_x, lru_lambda=lru_lambda, sgu_ln_g=sgu_ln_g, sgu_ln_b=sgu_ln_b,
             sgu_w_s=sgu_w_s, sgu_b_s=sgu_b_s, w_branch_a=w_branch_a, w_branch_b=w_branch_b, w_out=w_out,
             norm_ffn_g=norm_ffn_g, w_up=w_up, w_down=w_down, final_norm_g=final_norm_g)
    m = dict(norm_mix_g=m_norm_mix_g, w_in=m_w_in, conv_w=m_conv_w, conv_b=m_conv_b, lru_w_a=m_lru_w_a,
             lru_b_a=m_lru_b_a, lru_w_x=m_lru_w_x, lru_b_x=m_lru_b_x, lru_lambda=m_lru_lambda,
             sgu_ln_g=m_sgu_ln_g, sgu_ln_b=m_sgu_ln_b, sgu_w_s=m_sgu_w_s, sgu_b_s=m_sgu_b_s,
             w_branch_a=m_w_branch_a, w_branch_b=m_w_branch_b, w_out=m_w_out, norm_ffn_g=m_norm_ffn_g,
             w_up=m_w_up, w_down=m_w_down, final_norm_g=m_final_norm_g)
    v = dict(norm_mix_g=v_norm_mix_g, w_in=v_w_in, conv_w=v_conv_w, conv_b=v_conv_b, lru_w_a=v_lru_w_a,
             lru_b_a=v_lru_b_a, lru_w_x=v_lru_w_x, lru_b_x=v_lru_b_x, lru_lambda=v_lru_lambda,
             sgu_ln_g=v_sgu_ln_g, sgu_ln_b=v_sgu_ln_b, sgu_w_s=v_sgu_w_s, sgu_b_s=v_sgu_b_s,
             w_branch_a=v_w_branch_a, w_branch_b=v_w_branch_b, w_out=v_w_out, norm_ffn_g=v_norm_ffn_g,
             w_up=v_w_up, w_down=v_w_down, final_norm_g=v_final_norm_g)
    core = lax.axis_index("c")
    chip = 2 * lax.axis_index("x") + lax.axis_index("y")
    sel = jnp.stack([core, 1 - core, chip, 2 * chip + core]).astype(jnp.int32)
    this_core, this_chip = ("sel", 0), ("sel", 2)
    sds = jax.ShapeDtypeStruct

    ts = TOKEN_TILE

    def after_all(arrays):
        return jnp.stack([a[(0,) * a.ndim].astype(F32) for a in arrays])

    halves = {k:(w[k].shape[1] // 2, w[k].shape[2]) for k in BIG}

    def half_view(k, a):
        return a.reshape((2 * N_QUARTERS,) + halves[k])

    def full_view(k, a):
        if k == "conv_w":
            return a.reshape(N_QUARTERS, DEPTH, CONV_WIDTH, -1).transpose(1, 2, 0, 3).reshape(DEPTH, CONV_WIDTH, D_RNN)
        r2, cols = halves[k]
        if k in ("w_in", "w_up"):
            return a.reshape(1, N_QUARTERS, 2 * r2, cols)
        return a.reshape(1, 2 * N_QUARTERS * r2, cols)

    layer_bufs = [{}, {}]

    def cast_weights(k, after):
        _, r, cols = w[k].shape
        w4 = w[k].reshape(DEPTH, 1, r, cols)
        outs = _ew_call(lambda a, b: (a, b), "cast_weights", [(w4, (0, 0)), (w4, (1, 0))],
                        [(sds((1, N_QUARTERS, r, cols), BF), (0, this_chip))] * DEPTH, 1, sel, after=after)
        for l in range(DEPTH):
            layer_bufs[l][k] = half_view(k, outs[l])

    conv_buf = lax.dynamic_update_slice_in_dim(
        jnp.zeros((N_QUARTERS, DEPTH) + conv_w.shape[1:], F32), conv_w[None], chip, axis=0)
    layer_bufs[0]["conv_w"] = conv_buf.reshape((2 * N_QUARTERS,) + conv_w.shape[1:])
    sm = {k: w[k] for k in SMALL if k != "conv_w"}

    def gather_start(tag, l, keys, after):
        bufs = [layer_bufs[l][k] for k in keys]
        return _exchange_start(f"gather_start_{tag}", bufs, _gather_copies, 3 * len(keys), after)

    def gather_finish(tag, keys, started, after):
        send_sems, recv_sems, thru, _ = started
        landed = _exchange_wait(f"gather_wait_{tag}", send_sems, recv_sems, thru, _gather_copies, after)
        landed = _sibling_inplace_call("gather_forward", landed, _gather_forward_slabs, 3 * len(keys))
        return {k: full_view(k, a) for k, a in zip(keys, landed)}

    first, rest = ("w_in",), tuple(k for k in BIG if k != "w_in")
    cast_weights("w_in", None)
    started_a = gather_start("0a", 0, first + ("conv_w",), sel)
    for k in rest:
        cast_weights(k, started_a[3])
    started_b = gather_start("0b", 0, rest, started_a[3])
    started_c = gather_start("1a", 1, first, started_b[3])
    started_d = gather_start("1b", 1, rest, started_c[3])

    def arrives(tag, keys, started):
        state = {}

        def hook(after):
            landed = _exchange_wait(f"gather_wait_{tag}", started[0], started[1], started[2], _gather_copies, after)
            state["forward"] = _exchange_start(f"forward_start_{tag}", landed, _forward_copies, 3 * len(keys), after)
            return state["forward"][3][0, 0]

        def finish(after):
            send_sems, recv_sems, thru, _ = state["forward"]
            done = _exchange_wait(f"forward_wait_{tag}", send_sems, recv_sems, thru, _forward_copies, after)
            return {k: full_view(k, a) for k, a in zip(keys, done)}

        return hook, finish

    p0, p1 = _layer_small(sm, 0, sel[0:1]), _layer_small(sm, 1, sel[0:1])
    h0 = _norm_call(x[0], p0["g1"], ts)
    proj_own = _inproj_part_call(h0, full_view("w_in", started_a[2][0]), 2 * ts, sel[2:3], 0, 1)
    ready = after_all([started_d[3], proj_own] + [p[k] for p in (p0, p1) for k in ("wa", "wx", "wm")])
    big0 = gather_finish("0a", first + ("conv_w",), started_a, ready)
    for l, p in enumerate((p0, p1)):
        p["cw"] = big0["conv_w"][l]
    proj0 = _inproj_part_call(h0, big0["w_in"], 2 * ts, sel[2:3], 1, N_QUARTERS - 1, proj_own)
    hook, finish = arrives("0b", rest, started_b)
    sv0 = _layer_fwd_mix(x[0], big0, p0, ts, h0, hook, proj0)
    big0.update(finish(sv0["yb_pre"]))
    x_mid = _layer_fwd_out(sv0, big0, ts)
    hook, finish = arrives("1a", first, started_c)
    h1 = _norm_call(x_mid, p1["g1"] + hook(x_mid), ts)
    big1 = finish(h1)
    hook, finish = arrives("1b", rest, started_d)
    sv1 = _layer_fwd_mix(x_mid, big1, p1, ts, h1, hook)
    big1.update(finish(sv1["yb_pre"]))
    x_out = _layer_fwd_out(sv1, big1, ts)
    dx, loss, dgf = _loss_call(x_out, loss_target[0], final_norm_g.reshape(1, -1), ts)

    def pair_start(tag, gb, after):
        sends = [gb[k][1] for k in gb]
        zones = [lax.empty(a.shape, BF) for a in sends]
        return _exchange_start(f"pair_start_{tag}", sends + zones, _sibling_copies, len(sends), after)

    def reduce_start(tag, gb, after, pair=None):
        keys = tuple(gb)
        if pair is None:
            from_sibling = _sibling_send_call([gb[k][1] for k in keys])
        else:
            done = _exchange_wait(f"pair_wait_{tag}", pair[0], pair[1], pair[2], _sibling_copies, after)
            from_sibling = done[len(keys):]
        sums = [
            _ew_call(lambda a, b: (a.astype(F32) + b.astype(F32),), "pair_sum", [(gb[k][0][None], (0, "g")), (r[None], (0, "g"))],
                     [(sds((1,) + r.shape, BF), (0, "g"))], N_QUARTERS)[0][0]
            for k, r in zip(keys, from_sibling)]
        zones = [lax.empty((3,) + a.shape[1:], BF) for a in sums]
        started = _exchange_start(f"reduce_start_{tag}", sums + zones, _owner_copies, 3 * len(keys), after)
        return keys, started

    def reduce_finish(tag, l, keys_started, after, reduced):
        keys, (send_sems, recv_sems, thru, _) = keys_started
        done = _exchange_wait(f"reduce_wait_{tag}", send_sems, recv_sems, thru, _owner_copies, after)
        sums, zones = done[:len(keys)], done[len(keys):]
        for i, k in enumerate(keys):
            r2, cols = halves[k]
            reduced[k] = _ew_call(
                lambda a, b, c, d: (((a.astype(F32) + b.astype(F32)) + c.astype(F32)) + d.astype(F32),),
                "quarter_sum", [(sums[i][None], (0, this_chip))] + [(zones[i][None], (0, j)) for j in range(3)],
                [(sds((DEPTH, 2, r2, cols), F32), (l, this_core))], 1, sel, into=reduced.get(k))[0]

    def behind(params, key, started):
        return dict(params, **{key: params[key] + started[1][3][0, 0]})

    dx1, gb_ffn, gs1 = _layer_bwd_ffn(dx, sv1, big1, ts)
    merge_out, gb_merge = _layer_bwd_merge(dx1, sv1, big1, ts)
    dx_mid, gb_in, gs1_mix = _layer_bwd_branches(dx1, merge_out, sv1, big1, lru_lambda[1], ts)
    gb_1 = {**gb_ffn, **gb_merge, **gb_in}
    pair_1 = pair_start("1", gb_1, dx_mid)
    sv0["p"] = behind(sv0["p"], "g2", (None, pair_1))
    dx1, gb_ffn, gs0 = _layer_bwd_ffn(dx_mid, sv0, big0, ts)
    exchange_1 = reduce_start("1", gb_1, dx1, pair_1)
    pair_0a = pair_start("0a", gb_ffn, exchange_1[1][3])
    merge_out, gb_merge = _layer_bwd_merge(dx1, sv0, big0, ts, pair_0a[3])
    exchange_0a = reduce_start("0a", gb_ffn, merge_out[0], pair_0a)
    pair_0b = pair_start("0b", gb_merge, exchange_0a[1][3])
    sv0["p"] = behind(sv0["p"], "lg", (None, pair_0b))
    started_0b = {}

    def after_sgu(duv):
        started_0b["exchange"] = reduce_start("0b", gb_merge, duv, pair_0b)
        return started_0b["exchange"][1][3][0, 0]

    grad_x, gb_in, gs0_mix = _layer_bwd_branches(dx1, merge_out, sv0, big0, lru_lambda[0], ts, after_sgu)
    exchange_0b = started_0b["exchange"]
    exchange_0c = reduce_start("0c", gb_in, exchange_0b[1][3])
    layer_gs = [{**gs0, **gs0_mix}, {**gs1, **gs1_mix}]
    gs = {k: jnp.stack([g[k] for g in layer_gs]) for k in layer_gs[0]}
    gs["final_norm_g"] = dgf[0]
    gs["loss"] = loss[0, 0:1]

    me = ("sel", 3)
    piece = (1, N_DEVICES, SMALL_ROWS, 128)
    packed = _pack_small(gs).reshape(piece)
    scatter = _exchange_start("small_scatter_start", [packed[0], lax.empty(piece[1:], F32)], _small_scatter_copies,
                              N_DEVICES - 1, exchange_0c[1][3])
    reduced = {}
    reduce_finish("1", 1, exchange_1, scatter[3], reduced)
    reduce_finish("0a", 0, exchange_0a, reduced["w_in"], reduced)
    reduce_finish("0b", 0, exchange_0b, reduced["w_down"], reduced)

    def swap_slabs(ref, c, i):
        layers = (1,) if BIG[i] == "w_in" else range(DEPTH)
        return [(ref.at[l, c], ref.at[l, 1 - c]) for l in layers]

    swapped = dict(zip(BIG, _sibling_inplace_call("grads_swap_halves", [reduced[k] for k in BIG], swap_slabs,
                                                  DEPTH * len(BIG) - 1)))

    def adamw_layers(k, grad, layer, into, after=None):
        if layer is None:
            views = [_as4(_as_rows(a)) for a in (w[k], grad, m[k], v[k])]
            idx = (0, 0)
        else:
            views = [a.reshape((1,) + w[k].shape) for a in (w[k], grad, m[k], v[k])]
            idx = (0, layer)
        return _ew_call(_adamw, "adamw_big", [(a, idx) for a in views], [(sds(views[0].shape, F32), idx)] * 3,
                        into=into, after=after)

    updated, last_update = {}, None
    for k in BIG:
        updated[k] = adamw_layers(k, swapped[k], 1 if k == "w_in" else None, None, last_update)
        last_update = updated[k][0]
    scattered = _exchange_wait("small_scatter_wait", scatter[0], scatter[1], scatter[2], _small_scatter_copies,
                               last_update)
    summed = _ew_call(
        lambda *parts: (functools.reduce(lambda a, b: a + b, parts),), "small_sum",
        [(scattered[0][None], (0, me))]
        + [(scattered[1][None], (0, lambda g, s, k=k: s[3] ^ k)) for k in range(1, N_DEVICES)],
        [(sds(piece, F32), (0, me))], 1, sel)[0]
    spread = _exchange_start("small_spread_start", [summed[0]], _small_spread_copies, N_DEVICES - 1, summed)
    reduced["w_in"] = swapped["w_in"]
    reduce_finish("0c", 0, exchange_0c, spread[3], reduced)
    last = _sibling_inplace_call("grads_swap_last", [reduced["w_in"]],
                                 lambda ref, c, i: [(ref.at[0, c], ref.at[0, 1 - c])], 1)[0]
    swapped["w_in"] = last
    updated["w_in"] = adamw_layers("w_in", last, 0, updated["w_in"])
    grads_big = {k: swapped[k].reshape(w[k].shape) for k in BIG}
    delta, new_m, new_v = ({k: updated[k][j].reshape(w[k].shape) for k in BIG} for j in range(3))
    gathered_small = _exchange_wait("small_spread_wait", spread[0], spread[1], spread[2], _small_spread_copies,
                                    updated["w_in"][0])[0]

    like = {k: jax.ShapeDtypeStruct(gs[k].shape, F32) for k in SMALL}
    like["loss"] = jax.ShapeDtypeStruct((1,), F32)
    grads_small = _unpack_small(gathered_small, like)
    total = grads_small.pop("loss")[0]
    conv_q = grads_small["conv_w"].reshape(DEPTH, CONV_WIDTH, N_QUARTERS, D_RNN // N_QUARTERS)
    grads_small["conv_w"] = lax.dynamic_index_in_dim(conv_q, chip, axis=2, keepdims=False)
    outs = _small_adamw_call(*[[_as_rows(d[k]) for k in SMALL] for d in (w, grads_small, m, v)])
    for d, o in zip((delta, new_m, new_v), outs):
        for k, a in zip(SMALL, o):
            d[k] = a.reshape(w[k].shape)

    grads = {**grads_big, **grads_small}
    return (total, grad_x[None], *[grads[k] for k in WEIGHTS], *[delta[k] for k in WEIGHTS],
            *[new_m[k] for k in WEIGHTS], *[new_v[k] for k in WEIGHTS])
```

```python
import functools
import math

import jax
import jax.numpy as jnp
from jax import lax
from jax.experimental import pallas as pl
from jax.experimental.pallas import tpu as pltpu

F32 = jnp.float32
BF = jnp.bfloat16

DEPTH = 2
D_MODEL = 1024
D_RNN = 1280
D_SGU = 1024
D_FF = 4096
D_IN = 2 * D_RNN + 2 * D_SGU + 2 * D_MODEL
N_QUARTERS = 4
Q_IN = D_IN // N_QUARTERS
Q_FF = D_FF // N_QUARTERS
RNN_HEADS = 20
RNN_HEAD_DIM = 64
LRU_GROUP = 256
N_LRU_GROUPS = D_RNN // LRU_GROUP
HEADS_PER_GROUP = LRU_GROUP // RNN_HEAD_DIM
CONV_WIDTH = 4
LRU_C = 8.0
SGU_GROUPS = 8
SGU_BLOCK = 128
CHUNK = 64
EPS = 1e-6

ADAM_LR = 0.001
ADAM_B1 = 0.9
ADAM_B2 = 0.999
ADAM_EPS = 1e-08
ADAM_WD = 0.01
ADAM_STEP = 10

SUBLANES = 8
TOKEN_TILE = 512
VMEM_LIMIT_BYTES = 56 * 1024 * 1024

MESH = pl.DeviceIdType.MESH


def _params(semantics=None, vmem=True, **kw):
    return pltpu.CompilerParams(
        dimension_semantics=semantics,
        vmem_limit_bytes=VMEM_LIMIT_BYTES if vmem else None,
        **kw,
    )


def _dot(a, b):
    return jnp.dot(a, b, preferred_element_type=F32)


def _dot_nt(a, b):
    return lax.dot_general(a, b, (((1,), (1,)), ((), ())), preferred_element_type=F32)


def _dot_tn(a, b):
    return lax.dot_general(a, b, (((0,), (0,)), ((), ())), preferred_element_type=F32)


_GELU_C = math.sqrt(2.0 / math.pi)
_GELU_A = 0.044715


def _gelu(x):
    return 0.5 * x * (1.0 + jnp.tanh(_GELU_C * (x + _GELU_A * x * x * x)))


def _gelu_and_grad(x):
    x2 = x * x
    t = jnp.tanh(_GELU_C * (x + _GELU_A * x2 * x))
    du = _GELU_C * (1.0 + 3.0 * _GELU_A * x2)
    return 0.5 * x * (1.0 + t), 0.5 * (1.0 + t) + 0.5 * x * (1.0 - t * t) * du


def _rms_stats(x):
    return lax.rsqrt(jnp.mean(x * x, axis=-1, keepdims=True) + EPS)


def _rms_bwd(dy, x, g):
    rs = _rms_stats(x)
    n = x * rs
    dn = dy * g
    dx = rs * (dn - n * jnp.mean(dn * n, axis=-1, keepdims=True))
    return dx, dy * n


def _row_sum(x):
    return jnp.sum(x, axis=0, keepdims=True)


def _tile_spec(ts, width, col=0):
    return pl.BlockSpec((ts, width), lambda i, col=col: (i, col))


def _full_spec(shape):
    zeros = (0,) * len(shape)
    return pl.BlockSpec(shape, lambda *_: zeros)


def _layer_spec(w, layer):
    zeros = (0,) * (w.ndim - 1)
    return pl.BlockSpec((None,) + tuple(w.shape[1:]), lambda *_: (layer,) + zeros)


def _with_after(body, n_in, after):
    if after is None:
        return body, [], []

    def wrapped(*refs):
        return body(*refs[:n_in], *refs[n_in + 1:])

    return wrapped, [pl.BlockSpec(memory_space=pl.ANY)], [after]


def _norm_call(x, g, ts, after=None):
    s = x.shape[0]

    def body(x_ref, g_ref, h_ref):
        xv = x_ref[...]
        h_ref[...] = (xv * _rms_stats(xv) * g_ref[...]).astype(BF)

    body, more_specs, more = _with_after(body, 2, after)
    return pl.pallas_call(
        body, name="norm_fwd", grid=(s // ts,),
        in_specs=[_tile_spec(ts, D_MODEL), _full_spec((1, D_MODEL))] + more_specs,
        out_specs=_tile_spec(ts, D_MODEL),
        out_shape=jax.ShapeDtypeStruct((s, D_MODEL), BF),
        compiler_params=_params(("parallel",)),
    )(x, g, *more)


def _inproj_call(h, w_in, layer, ts):
    s = h.shape[0]

    def body(h_ref, w_ref, o_ref):
        o_ref[...] = _dot(h_ref[...], w_ref[...]).astype(BF)

    return pl.pallas_call(
        body, name="inproj_fwd", grid=(N_QUARTERS, s // ts),
        in_specs=[
            pl.BlockSpec((ts, D_MODEL), lambda q, i: (i, 0)),
            pl.BlockSpec((None, None, D_MODEL, Q_IN), lambda q, i: (layer, q, 0, 0)),
        ],
        out_specs=pl.BlockSpec((ts, Q_IN), lambda q, i: (i, q)),
        out_shape=jax.ShapeDtypeStruct((s, D_IN), BF),
        compiler_params=_params(("parallel", "parallel")),
    )(h, w_in)


def _inproj_part_call(h, w_in, ts, own, first, count, into=None):
    s = h.shape[0]

    def quarter(j, sel):
        return (sel[0] + first + j) % N_QUARTERS

    def body(sel_ref, h_ref, w_ref, *rest):
        rest[-1][...] = _dot(h_ref[...], w_ref[...]).astype(BF)

    in_specs = [pl.BlockSpec((ts, D_MODEL), lambda j, i, sel: (i, 0)),
                pl.BlockSpec((None, None, D_MODEL, Q_IN), lambda j, i, sel: (0, quarter(j, sel), 0, 0))]
    operands = [h, w_in]
    aliases = {}
    if into is not None:
        in_specs.append(pl.BlockSpec(memory_space=pl.ANY))
        operands.append(into)
        aliases = {3: 0}
    return pl.pallas_call(
        body, name="inproj_fwd_part", out_shape=jax.ShapeDtypeStruct((s, D_IN), BF),
        grid_spec=pltpu.PrefetchScalarGridSpec(
            num_scalar_prefetch=1, grid=(count, s // ts), in_specs=in_specs,
            out_specs=pl.BlockSpec((ts, Q_IN), lambda j, i, sel: (i, quarter(j, sel)))),
        input_output_aliases=aliases,
        compiler_params=_params(("parallel", "parallel")),
    )(own, *operands)


def _shift_down(x, tail, s):
    xr = pltpu.roll(x, s, 0)
    tr = pltpu.roll(tail, s, 0)
    row = lax.broadcasted_iota(jnp.int32, tail.shape, 0)
    top = jnp.where(row < s, tr, xr[0:SUBLANES])
    return jnp.concatenate([top, xr[SUBLANES:]], axis=0)


def _shift_up(x, head, s):
    t = x.shape[0]
    xr = pltpu.roll(x, t - s, 0)
    hr = pltpu.roll(head, SUBLANES - s, 0)
    row = lax.broadcasted_iota(jnp.int32, head.shape, 0)
    bottom = jnp.where(row >= SUBLANES - s, hr, xr[t - SUBLANES:])
    return jnp.concatenate([xr[: t - SUBLANES], bottom], axis=0)


def _conv_fwd(x, tail, cw_ref, cb_ref):
    out = cb_ref[...] + cw_ref[CONV_WIDTH - 1:CONV_WIDTH, :] * x
    for s in range(1, CONV_WIDTH):
        k = CONV_WIDTH - 1 - s
        out = out + cw_ref[k:k + 1, :] * _shift_down(x, tail, s)
    return out


def _group_dot(x_bf, w_ref, dot):
    cols = [dot(x_bf[:, g * LRU_GROUP:(g + 1) * LRU_GROUP], w_ref[g]) for g in range(N_LRU_GROUPS)]
    return jnp.concatenate(cols, axis=1)


def _lru_gates(xr, wa_ref, wx_ref, ba_ref, bx_ref, sp_ref):
    xb = xr.astype(BF)
    r = jax.nn.sigmoid(_group_dot(xb, wa_ref, _dot) + ba_ref[...])
    i = jax.nn.sigmoid(_group_dot(xb, wx_ref, _dot) + bx_ref[...])
    log_a = (-LRU_C * r) * sp_ref[...]
    a = jnp.exp(log_a)
    nrm2 = -jnp.tanh(log_a) * (a * a + 1.0)
    inv_nrm = lax.rsqrt(jnp.maximum(nrm2, 1e-36))
    return r, i, a, nrm2 * inv_nrm, inv_nrm


def _linear_scan(a, b, carry, al_ref, bl_ref, h_ref, reverse):
    t, c = a.shape
    rowm = lax.broadcasted_iota(jnp.int32, (t, c), 0) & (SUBLANES - 1)
    for d in (1, 2, 4):
        if reverse:
            keep, sh = rowm < SUBLANES - d, t - d
        else:
            keep, sh = rowm >= d, d
        a_sh = jnp.where(keep, pltpu.roll(a, sh, 0), 1.0)
        b_sh = jnp.where(keep, pltpu.roll(b, sh, 0), 0.0)
        b = a * b_sh + b
        a = a * a_sh
    al_ref[...] = a
    bl_ref[...] = b
    groups = t // SUBLANES

    def step(j, state):
        jj = groups - 1 - j if reverse else j
        off = pl.multiple_of(jj * SUBLANES, SUBLANES)
        rows = bl_ref[pl.ds(off, SUBLANES), :] + al_ref[pl.ds(off, SUBLANES), :] * state
        h_ref[pl.ds(off, SUBLANES), :] = rows
        last = rows[0:1, :] if reverse else rows[SUBLANES - 1:SUBLANES, :]
        return jnp.broadcast_to(last, (SUBLANES, c))

    out = lax.fori_loop(0, groups, step, jnp.broadcast_to(carry, (SUBLANES, c)))
    return out[0:1, :]


def _rnn_fwd_call(proj, wa, wx, ba, bx, sp, cw, cb, ts):
    s = proj.shape[0]

    def body(xg_ref, wa_ref, wx_ref, ba_ref, bx_ref, sp_ref, cw_ref, cb_ref, xr_ref, hr_ref, ya_ref,
             tail_sc, carry_sc, al_sc, bl_sc, h_sc):
        @pl.when(pl.program_id(0) == 0)
        def _():
            tail_sc[...] = jnp.zeros_like(tail_sc)
            carry_sc[...] = jnp.zeros_like(carry_sc)

        x = xg_ref[:, :D_RNN].astype(F32)
        g = xg_ref[:, D_RNN:]
        xr = _conv_fwd(x, tail_sc[...], cw_ref, cb_ref)
        tail_sc[...] = x[ts - SUBLANES:, :]
        xr_ref[...] = xr.astype(BF)
        _, i, a, nrm, _ = _lru_gates(xr, wa_ref, wx_ref, ba_ref, bx_ref, sp_ref)
        carry_sc[...] = _linear_scan(a, nrm * (i * xr), carry_sc[...], al_sc, bl_sc, h_sc, False)
        h = h_sc[...]
        hr_ref[...] = h.astype(BF)
        ya_ref[...] = (h * _gelu(g)).astype(BF)

    gw = (N_LRU_GROUPS, LRU_GROUP, LRU_GROUP)
    return pl.pallas_call(
        body, name="rnn_fwd", grid=(s // ts,),
        in_specs=[_tile_spec(ts, 2 * D_RNN), _full_spec(gw), _full_spec(gw),
                  _full_spec((1, D_RNN)), _full_spec((1, D_RNN)), _full_spec((1, D_RNN)),
                  _full_spec((CONV_WIDTH, D_RNN)), _full_spec((1, D_RNN))],
        out_specs=[_tile_spec(ts, D_RNN)] * 3,
        out_shape=[jax.ShapeDtypeStruct((s, D_RNN), BF)] * 3,
        scratch_shapes=[pltpu.VMEM((SUBLANES, D_RNN), F32), pltpu.VMEM((1, D_RNN), F32),
                        pltpu.VMEM((ts, D_RNN), F32), pltpu.VMEM((ts, D_RNN), F32),
                        pltpu.VMEM((ts, D_RNN), F32)],
        compiler_params=_params(("arbitrary",)),
    )(proj, wa, wx, ba, bx, sp, cw, cb)


def _layernorm_fwd(x):
    mu = jnp.mean(x, axis=-1, keepdims=True)
    xc = x - mu
    rstd = lax.rsqrt(jnp.mean(xc * xc, axis=-1, keepdims=True) + EPS)
    return xc * rstd, rstd


def _sgu_mix(vn_bf, wm_ref, bsb_ref, ts):
    rows = []
    for blk in range(ts // SGU_BLOCK):
        r0 = blk * SGU_BLOCK
        cols = [
            _dot(wm_ref[g], vn_bf[r0:r0 + SGU_BLOCK, g * SGU_BLOCK:(g + 1) * SGU_BLOCK]) + bsb_ref[g]
            for g in range(SGU_GROUPS)
        ]
        rows.append(jnp.concatenate(cols, axis=1))
    return jnp.concatenate(rows, axis=0)


def _sgu_fwd_call(proj, wm, bsb, lg, lb, ts, after=None):
    s = proj.shape[0]

    def body(uv_ref, wm_ref, bsb_ref, lg_ref, lb_ref, yb_ref):
        gu = _gelu(uv_ref[:, :D_SGU])
        gv = _gelu(uv_ref[:, D_SGU:2 * D_SGU]).astype(F32)
        nh, _ = _layernorm_fwd(gv)
        vn = (nh * lg_ref[...] + lb_ref[...]).astype(BF)
        yb_ref[...] = (gu * _sgu_mix(vn, wm_ref, bsb_ref, ts)).astype(BF)

    sw = (SGU_GROUPS, SGU_BLOCK, SGU_BLOCK)
    body, more_specs, more = _with_after(body, 5, after)
    return pl.pallas_call(
        body, name="sgu_fwd", grid=(s // ts,),
        in_specs=[_tile_spec(ts, 2 * D_RNN, 1), _full_spec(sw), _full_spec(sw),
                  _full_spec((1, D_SGU)), _full_spec((1, D_SGU))] + more_specs,
        out_specs=_tile_spec(ts, D_SGU),
        out_shape=jax.ShapeDtypeStruct((s, D_SGU), BF),
        compiler_params=_params(("parallel",)),
    )(proj, wm, bsb, lg, lb, *more)


_GATE_COL0 = (2 * D_RNN + 2 * D_SGU) // 512


def _gate_specs(ts):
    return [_tile_spec(ts, 512, _GATE_COL0 + j) for j in range(4)]


def _merge_call(x, proj, ya_pre, yb_pre, w_ba, w_bb, w_out, g2, layer, ts):
    s = x.shape[0]

    def body(x_ref, ga0, ga1, gb0, gb1, ya_ref, yb_ref, wa_ref, wb_ref, wo_ref, g2_ref,
             x1_ref, yao_ref, ybo_ref, mg_ref, h2_ref):
        ya = _dot(ya_ref[...], wa_ref[...])
        yb = _dot(yb_ref[...], wb_ref[...])
        sa = jax.nn.sigmoid(jnp.concatenate([ga0[...], ga1[...]], axis=1).astype(F32))
        sb = jax.nn.sigmoid(jnp.concatenate([gb0[...], gb1[...]], axis=1).astype(F32))
        merged = (sa * ya + sb * yb).astype(BF)
        x1 = x_ref[...] + _dot(merged, wo_ref[...])
        x1_ref[...] = x1
        yao_ref[...] = ya.astype(BF)
        ybo_ref[...] = yb.astype(BF)
        mg_ref[...] = merged
        h2_ref[...] = (x1 * _rms_stats(x1) * g2_ref[...]).astype(BF)

    act = jax.ShapeDtypeStruct((s, D_MODEL), BF)
    return pl.pallas_call(
        body, name="merge_fwd", grid=(s // ts,),
        in_specs=[_tile_spec(ts, D_MODEL)] + _gate_specs(ts) + [
            _tile_spec(ts, D_RNN), _tile_spec(ts, D_SGU),
            _layer_spec(w_ba, layer), _layer_spec(w_bb, layer), _layer_spec(w_out, layer),
            _full_spec((1, D_MODEL))],
        out_specs=[_tile_spec(ts, D_MODEL)] * 5,
        out_shape=[jax.ShapeDtypeStruct((s, D_MODEL), F32), act, act, act, act],
        compiler_params=_params(("parallel",)),
    )(x, proj, proj, proj, proj, ya_pre, yb_pre, w_ba, w_bb, w_out, g2)


def _ffn_call(x1, h2, w_up, w_down, layer, ts):
    s = x1.shape[0]

    def body(x1_ref, h2_ref, wu_ref, wd_ref, x2_ref, p_ref):
        h2v = h2_ref[...]
        acc = x1_ref[...]
        for q in range(N_QUARTERS):
            p = _dot(h2v, wu_ref[q])
            p_ref[:, q * Q_FF:(q + 1) * Q_FF] = p.astype(BF)
            f = jnp.square(jnp.maximum(p, 0.0)).astype(BF)
            acc = acc + _dot(f, wd_ref[q * Q_FF:(q + 1) * Q_FF, :])
        x2_ref[...] = acc

    return pl.pallas_call(
        body, name="ffn_fwd", grid=(s // ts,),
        in_specs=[_tile_spec(ts, D_MODEL), _tile_spec(ts, D_MODEL),
                  pl.BlockSpec((None, N_QUARTERS, D_MODEL, Q_FF), lambda i: (layer, 0, 0, 0)),
                  pl.BlockSpec((None, D_FF, D_MODEL), lambda i: (layer, 0, 0))],
        out_specs=[_tile_spec(ts, D_MODEL), _tile_spec(ts, D_FF)],
        out_shape=[jax.ShapeDtypeStruct((s, D_MODEL), F32), jax.ShapeDtypeStruct((s, D_FF), BF)],
        compiler_params=_params(("parallel",)),
    )(x1, h2, w_up, w_down)


def _loss_call(x, target, gf, ts):
    s = x.shape[0]

    def body(x_ref, t_ref, g_ref, dx_ref, loss_ref, dg_ref):
        @pl.when(pl.program_id(0) == 0)
        def _():
            loss_ref[...] = jnp.zeros_like(loss_ref)
            dg_ref[...] = jnp.zeros_like(dg_ref)

        xv = x_ref[...]
        gv = g_ref[...]
        err = xv * _rms_stats(xv) * gv - t_ref[...]
        part = 0.5 * jnp.sum(jnp.mean(err * err, axis=-1, keepdims=True), axis=0, keepdims=True)
        loss_ref[...] += jnp.broadcast_to(part, loss_ref.shape)
        dx, dg = _rms_bwd(err * (1.0 / D_MODEL), xv, gv)
        dx_ref[...] = dx
        dg_ref[...] += _row_sum(dg)

    return pl.pallas_call(
        body, name="loss_head", grid=(s // ts,),
        in_specs=[_tile_spec(ts, D_MODEL), _tile_spec(ts, D_MODEL), _full_spec((1, D_MODEL))],
        out_specs=[_tile_spec(ts, D_MODEL), _full_spec((1, 128)), _full_spec((1, D_MODEL))],
        out_shape=[jax.ShapeDtypeStruct((s, D_MODEL), F32), jax.ShapeDtypeStruct((1, 128), F32),
                   jax.ShapeDtypeStruct((1, D_MODEL), F32)],
        compiler_params=_params(("arbitrary",)),
    )(x, target, gf)


def _ffn_bwd_call(dx2, p, x1, g2, w_up, w_down, layer, ts, after=None):
    s = dx2.shape[0]

    def body(dx2_ref, p_ref, x1_ref, g2_ref, wu_ref, wd_ref, dx1_ref, dp_ref, dg_ref, dx2b_ref, dx1b_ref):
        @pl.when(pl.program_id(0) == 0)
        def _():
            dg_ref[...] = jnp.zeros_like(dg_ref)

        dx2v = dx2_ref[...]
        dyb = dx2v.astype(BF)
        dx2b_ref[...] = dyb
        dh2 = jnp.zeros((ts, D_MODEL), F32)
        for q in range(N_QUARTERS):
            cols = slice(q * Q_FF, (q + 1) * Q_FF)
            df = _dot_nt(dyb, wd_ref[cols, :])
            dp = (df * (2.0 * jnp.maximum(p_ref[:, cols].astype(F32), 0.0))).astype(BF)
            dp_ref[:, cols] = dp
            dh2 = dh2 + _dot_nt(dp, wu_ref[q])
        dx, dg = _rms_bwd(dh2, x1_ref[...], g2_ref[...])
        dx1 = dx2v + dx
        dx1_ref[...] = dx1
        dx1b_ref[...] = dx1.astype(BF)
        dg_ref[...] += _row_sum(dg)

    body, more_specs, more = _with_after(body, 6, after)
    return pl.pallas_call(
        body, name="ffn_bwd", grid=(s // ts,),
        in_specs=[_tile_spec(ts, D_MODEL), _tile_spec(ts, D_FF), _tile_spec(ts, D_MODEL),
                  _full_spec((1, D_MODEL)),
                  pl.BlockSpec((None, N_QUARTERS, D_MODEL, Q_FF), lambda i: (layer, 0, 0, 0)),
                  pl.BlockSpec((None, D_FF, D_MODEL), lambda i: (layer, 0, 0))] + more_specs,
        out_specs=[_tile_spec(ts, D_MODEL), _tile_spec(ts, D_FF), _full_spec((1, D_MODEL)),
                   _tile_spec(ts, D_MODEL), _tile_spec(ts, D_MODEL)],
        out_shape=[jax.ShapeDtypeStruct((s, D_MODEL), F32), jax.ShapeDtypeStruct((s, D_FF), BF),
                   jax.ShapeDtypeStruct((1, D_MODEL), F32),
                   jax.ShapeDtypeStruct((s, D_MODEL), BF), jax.ShapeDtypeStruct((s, D_MODEL), BF)],
        compiler_params=_params(("arbitrary",)),
    )(dx2, p, x1, g2, w_up, w_down, *more)


def _merge_bwd_call(dx1, proj, ya, yb, w_ba, w_bb, w_out, layer, ts, after=None):
    s = dx1.shape[0]

    def body(dx1_ref, ga0, ga1, gb0, gb1, ya_ref, yb_ref, wa_ref, wb_ref, wo_ref, *rest):
        dya_ref, dyb_ref, dgate_ref, dyap_ref, dybp_ref = rest[-5:]
        dm = _dot_nt(dx1_ref[...].astype(BF), wo_ref[...])
        sa = jax.nn.sigmoid(jnp.concatenate([ga0[...], ga1[...]], axis=1).astype(F32))
        sb = jax.nn.sigmoid(jnp.concatenate([gb0[...], gb1[...]], axis=1).astype(F32))
        dya = (dm * sa).astype(BF)
        dyb = (dm * sb).astype(BF)
        dya_ref[...] = dya
        dyb_ref[...] = dyb
        dgate_ref[:, :D_MODEL] = (dm * ya_ref[...].astype(F32) * sa * (1.0 - sa)).astype(BF)
        dgate_ref[:, D_MODEL:] = (dm * yb_ref[...].astype(F32) * sb * (1.0 - sb)).astype(BF)
        dyap_ref[...] = _dot_nt(dya, wa_ref[...]).astype(BF)
        dybp_ref[...] = _dot_nt(dyb, wb_ref[...]).astype(BF)

    act = jax.ShapeDtypeStruct((s, D_MODEL), BF)
    return pl.pallas_call(
        body, name="merge_bwd", grid=(s // ts,),
        in_specs=[_tile_spec(ts, D_MODEL)] + _gate_specs(ts) + [
            _tile_spec(ts, D_MODEL), _tile_spec(ts, D_MODEL),
            _layer_spec(w_ba, layer), _layer_spec(w_bb, layer), _layer_spec(w_out, layer)]
        + ([] if after is None else [pl.BlockSpec(memory_space=pl.ANY)]),
        out_specs=[_tile_spec(ts, D_MODEL), _tile_spec(ts, D_MODEL), _tile_spec(ts, 2 * D_MODEL),
                   _tile_spec(ts, D_RNN), _tile_spec(ts, D_SGU)],
        out_shape=[act, act, jax.ShapeDtypeStruct((s, 2 * D_MODEL), BF),
                   jax.ShapeDtypeStruct((s, D_RNN), BF), jax.ShapeDtypeStruct((s, D_SGU), BF)],
        compiler_params=_params(("parallel",)),
    )(dx1, proj, proj, proj, proj, ya, yb, w_ba, w_bb, w_out, *([] if after is None else [after]))


def _sgu_bwd_call(dyb_pre, proj, wm, bsb, mask, lg, lb, ts, after=None):
    s = proj.shape[0]

    def body(dy_ref, uv_ref, wm_ref, bsb_ref, mask_ref, lg_ref, lb_ref,
             duv_ref, dws_ref, dbs_ref, dlg_ref, dlb_ref, dm_sc):
        step = pl.program_id(0)

        @pl.when(step == 0)
        def _():
            dws_ref[...] = jnp.zeros_like(dws_ref)
            dlg_ref[...] = jnp.zeros_like(dlg_ref)
            dlb_ref[...] = jnp.zeros_like(dlb_ref)
            dm_sc[...] = jnp.zeros_like(dm_sc)

        gu, dgu_du = _gelu_and_grad(uv_ref[:, :D_SGU])
        gv, dgv_dv = _gelu_and_grad(uv_ref[:, D_SGU:2 * D_SGU])
        nh, rstd = _layernorm_fwd(gv.astype(F32))
        lgv = lg_ref[...]
        vn = (nh * lgv + lb_ref[...]).astype(BF)
        dy = dy_ref[...].astype(F32)
        du = dy * _sgu_mix(vn, wm_ref, bsb_ref, ts) * dgu_du
        dmix = dy * gu
        dmix_bf = dmix.astype(BF)
        dm_acc = dm_sc[...]
        rows = []
        for blk in range(ts // SGU_BLOCK):
            r0 = blk * SGU_BLOCK
            dm_acc = dm_acc + dmix[r0:r0 + SGU_BLOCK, :]
            cols = []
            for g in range(SGU_GROUPS):
                c0 = g * SGU_BLOCK
                dmg = dmix_bf[r0:r0 + SGU_BLOCK, c0:c0 + SGU_BLOCK]
                cols.append(_dot_tn(wm_ref[g], dmg))
                dws_ref[g] += mask_ref[...] * _dot_nt(dmg, vn[r0:r0 + SGU_BLOCK, c0:c0 + SGU_BLOCK])
            rows.append(jnp.concatenate(cols, axis=1))
        dm_sc[...] = dm_acc
        dvn = jnp.concatenate(rows, axis=0)
        dlg_ref[...] += _row_sum(dvn * nh)
        dlb_ref[...] += _row_sum(dvn)
        dnh = dvn * lgv
        dgv = rstd * (dnh - jnp.mean(dnh, axis=-1, keepdims=True)
                      - nh * jnp.mean(dnh * nh, axis=-1, keepdims=True))
        duv_ref[:, :D_SGU] = du.astype(BF)
        duv_ref[:, D_SGU:] = (dgv * dgv_dv).astype(BF)

        @pl.when(step == pl.num_programs(0) - 1)
        def _():
            for g in range(SGU_GROUPS):
                dbs_ref[:, g:g + 1] = jnp.sum(
                    dm_acc[:, g * SGU_BLOCK:(g + 1) * SGU_BLOCK], axis=1, keepdims=True)

    sw = (SGU_GROUPS, SGU_BLOCK, SGU_BLOCK)
    body, more_specs, more = _with_after(body, 7, after)
    return pl.pallas_call(
        body, name="sgu_bwd", grid=(s // ts,),
        in_specs=[_tile_spec(ts, D_SGU), _tile_spec(ts, 2 * D_RNN, 1), _full_spec(sw), _full_spec(sw),
                  _full_spec((SGU_BLOCK, SGU_BLOCK)), _full_spec((1, D_SGU)), _full_spec((1, D_SGU))] + more_specs,
        out_specs=[_tile_spec(ts, 2 * D_SGU), _full_spec(sw), _full_spec((SGU_BLOCK, SGU_GROUPS)),
                   _full_spec((1, D_SGU)), _full_spec((1, D_SGU))],
        out_shape=[jax.ShapeDtypeStruct((s, 2 * D_SGU), BF), jax.ShapeDtypeStruct(sw, F32),
                   jax.ShapeDtypeStruct((SGU_BLOCK, SGU_GROUPS), F32),
                   jax.ShapeDtypeStruct((1, D_SGU), F32), jax.ShapeDtypeStruct((1, D_SGU), F32)],
        scratch_shapes=[pltpu.VMEM((SGU_BLOCK, D_SGU), F32)],
        compiler_params=_params(("arbitrary",)),
    )(dyb_pre, proj, wm, bsb, mask, lg, lb, *more)


_ROW_DBA, _ROW_DBX, _ROW_DSP, _ROW_DCB, _ROW_DCW = 0, 1, 2, 3, 4
_PREV_ROWS = 16


def _rnn_bwd_call(dya_pre, proj, xr_saved, hr, wa, wx, ba, bx, sp, cw, ts, after=None):
    s = proj.shape[0]
    nt = s // ts
    per = ts // _PREV_ROWS

    def tile(i):
        return nt - 1 - i

    def prev(i):
        return jnp.maximum(tile(i) * per - 1, 0)

    def body(dy_ref, xg_ref, xr_ref, hr_ref, hrp_ref, wa_ref, wx_ref, ba_ref, bx_ref, sp_ref,
             cw_ref, dxg_ref, dwa_ref, dwx_ref, vec_ref,
             lam_carry, a_first, dxr_head, al_sc, bl_sc, lam_sc):
        step = pl.program_id(0)

        @pl.when(step == 0)
        def _():
            dwa_ref[...] = jnp.zeros_like(dwa_ref)
            dwx_ref[...] = jnp.zeros_like(dwx_ref)
            vec_ref[...] = jnp.zeros_like(vec_ref)
            lam_carry[...] = jnp.zeros_like(lam_carry)
            a_first[...] = jnp.zeros_like(a_first)
            dxr_head[...] = jnp.zeros_like(dxr_head)

        has_prev = (step < nt - 1).astype(F32)
        x = xg_ref[:, :D_RNN].astype(F32)
        g = xg_ref[:, D_RNN:]
        h_tail =hrp_ref[_PREV_ROWS - SUBLANES:, :].astype(F32) * has_prev
        xr = xr_ref[...].astype(F32)
        r, i, a, nrm, inv_nrm = _lru_gates(xr, wa_ref, wx_ref, ba_ref, bx_ref, sp_ref)
        h = hr_ref[...].astype(F32)
        dy = dy_ref[...].astype(F32)
        gg, dgg = _gelu_and_grad(g)

        coef = _shift_up(a, jnp.broadcast_to(a_first[...], (SUBLANES, D_RNN)), 1)
        lam_carry[...] = _linear_scan(coef, dy * gg, lam_carry[...], al_sc, bl_sc, lam_sc, True)
        a_first[...] = a[0:1, :]
        lam = lam_sc[...]

        da = lam * _shift_down(h, h_tail, 1)
        dnrm = lam * (i * xr)
        di = lam * nrm * xr
        dlog_a = da * a - dnrm * (a * a) * inv_nrm
        spv = sp_ref[...]
        dza = (dlog_a * (-LRU_C * spv)) * (r * (1.0 - r))
        dzx = di * (i * (1.0 - i))
        vec_ref[_ROW_DSP:_ROW_DSP + 1, :] += _row_sum(dlog_a * (-LRU_C * r))
        vec_ref[_ROW_DBA:_ROW_DBA + 1, :] += _row_sum(dza)
        vec_ref[_ROW_DBX:_ROW_DBX + 1, :] += _row_sum(dzx)
        xb = xr.astype(BF)
        dza_bf = dza.astype(BF)
        dzx_bf = dzx.astype(BF)
        for grp in range(N_LRU_GROUPS):
            cols = slice(grp * LRU_GROUP, (grp + 1) * LRU_GROUP)
            dwa_ref[grp] += _dot_tn(xb[:, cols], dza_bf[:, cols])
            dwx_ref[grp] += _dot_tn(xb[:, cols], dzx_bf[:, cols])
        dxr = (lam * nrm * i + _group_dot(dza_bf, wa_ref, _dot_nt) + _group_dot(dzx_bf, wx_ref, _dot_nt))

        vec_ref[_ROW_DCB:_ROW_DCB + 1, :] += _row_sum(dxr)
        head = dxr_head[...]
        dx = cw_ref[CONV_WIDTH - 1:CONV_WIDTH, :] * dxr
        vec_ref[_ROW_DCW + 3:_ROW_DCW + 4, :] += _row_sum(dxr * x)
        for sft in range(1, CONV_WIDTH):
            k = CONV_WIDTH - 1 - sft
            ahead = _shift_up(dxr, head, sft)
            dx = dx + cw_ref[k:k + 1, :] * ahead
            vec_ref[_ROW_DCW + k:_ROW_DCW + k + 1, :] += _row_sum(ahead * x)
        dxr_head[...] = dxr[0:SUBLANES, :]
        dxg_ref[:, :D_RNN] = dx.astype(BF)
        dxg_ref[:, D_RNN:] = (dy * h * dgg).astype(BF)

    gw = (N_LRU_GROUPS, LRU_GROUP, LRU_GROUP)
    rev = lambda width: pl.BlockSpec((ts, width), lambda i: (tile(i), 0))
    body, more_specs, more = _with_after(body, 11, after)
    return pl.pallas_call(
        body, name="rnn_bwd", grid=(nt,),
        in_specs=[rev(D_RNN), rev(2 * D_RNN), rev(D_RNN), rev(D_RNN),
                  pl.BlockSpec((_PREV_ROWS, D_RNN), lambda i: (prev(i), 0)),
                  _full_spec(gw), _full_spec(gw),
                  _full_spec((1, D_RNN)), _full_spec((1, D_RNN)), _full_spec((1, D_RNN)),
                  _full_spec((CONV_WIDTH, D_RNN))] + more_specs,
        out_specs=[rev(2 * D_RNN), _full_spec(gw), _full_spec(gw), _full_spec((SUBLANES, D_RNN))],
        out_shape=[jax.ShapeDtypeStruct((s, 2 * D_RNN), BF), jax.ShapeDtypeStruct(gw, F32),
                   jax.ShapeDtypeStruct(gw, F32), jax.ShapeDtypeStruct((SUBLANES, D_RNN), F32)],
        scratch_shapes=[pltpu.VMEM((1, D_RNN), F32), pltpu.VMEM((1, D_RNN), F32),
                        pltpu.VMEM((SUBLANES, D_RNN), F32),
                        pltpu.VMEM((ts, D_RNN), F32), pltpu.VMEM((ts, D_RNN), F32),
                        pltpu.VMEM((ts, D_RNN), F32)],
        compiler_params=_params(("arbitrary",)),
    )(dya_pre, proj, xr_saved, hr, hr, wa, wx, ba, bx, sp, cw, *more)


def _inproj_bwd_call(dxg, duv, dgate, dx1, x, g1, w_in, layer, ts):
    s = x.shape[0]

    def body(dxg_ref, duv_ref, dgt_ref, dx1_ref, x_ref, g_ref, w_ref, dx_ref, dproj_ref, dg_ref):
        @pl.when(pl.program_id(0) == 0)
        def _():
            dg_ref[...] = jnp.zeros_like(dg_ref)

        dproj = jnp.concatenate([dxg_ref[...], duv_ref[...], dgt_ref[...]], axis=1)
        dproj_ref[...] = dproj
        dh = jnp.zeros((ts, D_MODEL), F32)
        for q in range(N_QUARTERS):
            dh = dh + _dot_nt(dproj[:, q * Q_IN:(q + 1) * Q_IN], w_ref[q])
        dx, dg = _rms_bwd(dh, x_ref[...], g_ref[...])
        dx_ref[...] = dx1_ref[...] + dx
        dg_ref[...] += _row_sum(dg)

    return pl.pallas_call(
        body, name="inproj_bwd", grid=(s // ts,),
        in_specs=[_tile_spec(ts, 2 * D_RNN), _tile_spec(ts, 2 * D_SGU), _tile_spec(ts, 2 * D_MODEL),
                  _tile_spec(ts, D_MODEL), _tile_spec(ts, D_MODEL), _full_spec((1, D_MODEL)),
                  pl.BlockSpec((None, N_QUARTERS, D_MODEL, Q_IN), lambda i: (layer, 0, 0, 0))],
        out_specs=[_tile_spec(ts, D_MODEL), _tile_spec(ts, D_IN), _full_spec((1, D_MODEL))],
        out_shape=[jax.ShapeDtypeStruct((s, D_MODEL), F32), jax.ShapeDtypeStruct((s, D_IN), BF),
                   jax.ShapeDtypeStruct((1, D_MODEL), F32)],
        compiler_params=_params(("arbitrary",)),
    )(dxg, duv, dgate, dx1, x, g1, w_in)


def _relu_sq(p):
    return jnp.square(jnp.maximum(p, 0))


def _wgrad_call(a, b, core, tm, tn, tk, col_blocked, name, a_fn=None):
    s, m = a.shape
    n = b.shape[1]
    r, cols = (m, n // N_QUARTERS) if col_blocked else (m // N_QUARTERS, n)
    r2 = r // 2
    per_tile = tm // r
    steps = s // tk
    assert per_tile > 0 or steps == 1

    def body(core_ref, a_ref, b_ref, keep_ref, send_ref, *acc):
        av = a_ref[...]
        if a_fn is not None:
            av = a_fn(av)
        prod = _dot_tn(av.astype(BF), b_ref[...].astype(BF))

        def emit(total):
            for h in range(2):
                @pl.when(core_ref[0] == h)
                def _():
                    for q in range(per_tile):
                        keep_ref[q] = total[q * r + h * r2:q * r + (h + 1) * r2].astype(BF)
                        send_ref[q] = total[q * r + (1 - h) * r2:q * r + (2 - h) * r2].astype(BF)

        if per_tile == 0:
            mine = pl.program_id(1) == core_ref[0]

            @pl.when(mine)
            def _():
                keep_ref[0] = prod.astype(BF)

            @pl.when(jnp.logical_not(mine))
            def _():
                send_ref[0] = prod.astype(BF)
        elif steps == 1:
            emit(prod)
        else:
            acc_ref, = acc
            step = pl.program_id(2)

            @pl.when(step == 0)
            def _():
                acc_ref[...] = prod

            @pl.when(jnp.logical_and(step > 0, step < steps - 1))
            def _():
                acc_ref[...] += prod

            @pl.when(step == steps - 1)
            def _():
                emit(acc_ref[...] + prod)

    if col_blocked:
        per_q = cols // tn
        out_spec = pl.BlockSpec((1, r2, tn), lambda j, i, k, c: (j // per_q, 0, j % per_q))
    else:
        out_spec = pl.BlockSpec((per_tile, r2, tn), lambda j, i, k, c: (i, 0, j))
    return pl.pallas_call(
        body, name=name,
        out_shape=[jax.ShapeDtypeStruct((N_QUARTERS, r2, cols), BF)] * 2,
        grid_spec=pltpu.PrefetchScalarGridSpec(
            num_scalar_prefetch=1, grid=(n // tn, m // tm, steps),
            in_specs=[pl.BlockSpec((tk, tm), lambda j, i, k, c: (k, i)),
                      pl.BlockSpec((tk, tn), lambda j, i, k, c: (k, j))],
            out_specs=[out_spec, out_spec],
            scratch_shapes=[] if steps == 1 else [pltpu.VMEM((tm, tn), F32)]),
        compiler_params=_params(("parallel", "parallel", "arbitrary")),
    )(core, a, b)


BIG = ("w_in", "w_up", "w_down", "w_branch_a", "w_branch_b", "w_out")


def _block_diag(w):
    w4 = w.reshape(N_LRU_GROUPS, HEADS_PER_GROUP, RNN_HEAD_DIM, RNN_HEAD_DIM)
    eye = jnp.eye(HEADS_PER_GROUP, dtype=w.dtype)
    return jnp.einsum("gjio,jk->gjiko", w4, eye).reshape(N_LRU_GROUPS, LRU_GROUP, LRU_GROUP)


def _block_diag_extract(d):
    d5 = d.reshape(N_LRU_GROUPS, HEADS_PER_GROUP, RNN_HEAD_DIM, HEADS_PER_GROUP, RNN_HEAD_DIM)
    blocks = [d5[:, j, :, j, :] for j in range(HEADS_PER_GROUP)]
    return jnp.stack(blocks, axis=1).reshape(RNN_HEADS, RNN_HEAD_DIM, RNN_HEAD_DIM)


def _sgu_mask():
    chunk = jnp.arange(SGU_BLOCK) // CHUNK
    return (chunk[:, None] >= chunk[None, :]).astype(F32)


def _layer_small(sm, l, core):
    row = lambda v: v.reshape(1, -1)
    return dict(
        core=core,
        g1=row(sm["norm_mix_g"][l]), g2=row(sm["norm_ffn_g"][l]),
        wa=_block_diag(sm["lru_w_a"][l]).astype(BF), wx=_block_diag(sm["lru_w_x"][l]).astype(BF),
        ba=row(sm["lru_b_a"][l]), bx=row(sm["lru_b_x"][l]),
        sp=row(jax.nn.softplus(-sm["lru_lambda"][l])),
        cw=sm["conv_w"][l] if "conv_w" in sm else None, cb=row(sm["conv_b"][l]),
        wm=(sm["sgu_w_s"][l] * _sgu_mask()).astype(BF),
        bsb=jnp.broadcast_to(sm["sgu_b_s"][l][:, :, None], (SGU_GROUPS, SGU_BLOCK, SGU_BLOCK)),
        lg=row(sm["sgu_ln_g"][l]), lb=row(sm["sgu_ln_b"][l]),
    )


def _layer_fwd_mix(x, big, p, ts, h=None, before_sgu=None, proj=None):
    if h is None:
        h = _norm_call(x, p["g1"], ts)
    if proj is None:
        proj = _inproj_call(h, big["w_in"], 0, 2 * ts)
    xr, hr, ya_pre = _rnn_fwd_call(proj, p["wa"], p["wx"], p["ba"], p["bx"], p["sp"], p["cw"], p["cb"], ts)
    yb_pre = _sgu_fwd_call(proj, p["wm"], p["bsb"], p["lg"], p["lb"], ts,
                           None if before_sgu is None else before_sgu(ya_pre))
    return dict(p=p, x=x, h=h, proj=proj, xr=xr, hr=hr, ya_pre=ya_pre, yb_pre=yb_pre)


def _layer_fwd_out(sv, big, ts):
    x1, ya, yb, merged, h2 = _merge_call(sv["x"], sv["proj"], sv["ya_pre"], sv["yb_pre"], big["w_branch_a"],
                                         big["w_branch_b"], big["w_out"], sv["p"]["g2"], 0, ts)
    x2, pre = _ffn_call(x1, h2, big["w_up"], big["w_down"], 0, ts)
    sv.update(x1=x1, ya=ya, yb=yb, merged=merged, h2=h2, pre=pre)
    return x2


def _layer_bwd_ffn(dx, sv, big, ts, after=None):
    p = sv["p"]
    dx1, dpre, dg2, dx_bf, sv["dx1_bf"] = _ffn_bwd_call(dx, sv["pre"], sv["x1"], p["g2"], big["w_up"],
                                                       big["w_down"], 0, ts, after)
    tk = dx.shape[0]
    gb = dict(
        w_down=_wgrad_call(sv["pre"], dx_bf, p["core"], Q_FF, D_MODEL, tk, False, "wgrad_down", a_fn=_relu_sq),
        w_up=_wgrad_call(sv["h2"], dpre, p["core"], D_MODEL, Q_FF, tk, True, "wgrad_up"))
    return dx1, gb, dict(norm_ffn_g=dg2[0])


def _layer_bwd_merge(dx1, sv, big, ts, after=None):
    tk = dx1.shape[0]
    core = sv["p"]["core"]
    dya, dyb, dgate, dya_pre, dyb_pre = _merge_bwd_call(
        dx1, sv["proj"], sv["ya"], sv["yb"], big["w_branch_a"], big["w_branch_b"], big["w_out"], 0, ts, after)
    gb = dict(
        w_out=_wgrad_call(sv["merged"], sv["dx1_bf"], core, D_MODEL, D_MODEL, tk, False, "wgrad_out"),
        w_branch_a=_wgrad_call(sv["ya_pre"], dya, core, D_RNN, D_MODEL // 2, tk, False, "wgrad_branch_a"),
        w_branch_b=_wgrad_call(sv["yb_pre"], dyb, core, D_SGU, D_MODEL, tk, False, "wgrad_branch_b"))
    return (dgate, dya_pre, dyb_pre), gb


def _layer_bwd_branches(dx1, merge_out, sv, big, lam, ts, after=None, after_sgu=None):
    p = sv["p"]
    tk = dx1.shape[0]
    dgate, dya_pre, dyb_pre = merge_out
    gb = {}
    duv, dws, dbs, dlg, dlb = _sgu_bwd_call(dyb_pre, sv["proj"], p["wm"], p["bsb"], _sgu_mask(), p["lg"], p["lb"],
                                            ts, after)
    dxg, dwa, dwx, vec = _rnn_bwd_call(dya_pre, sv["proj"], sv["xr"], sv["hr"], p["wa"], p["wx"], p["ba"], p["bx"],
                                       p["sp"], p["cw"], ts, None if after_sgu is None else after_sgu(duv))
    dx, dproj, dg1 = _inproj_bwd_call(dxg, duv, dgate, dx1, sv["x"], p["g1"], big["w_in"], 0, ts)
    gb["w_in"] = _wgrad_call(sv["h"], dproj, p["core"], D_MODEL // 2, Q_IN, tk, True, "wgrad_in")
    gs = dict(
        norm_mix_g=dg1[0], conv_w=vec[_ROW_DCW:_ROW_DCW + CONV_WIDTH], conv_b=vec[_ROW_DCB],
        lru_w_a=_block_diag_extract(dwa), lru_w_x=_block_diag_extract(dwx),
        lru_b_a=vec[_ROW_DBA].reshape(RNN_HEADS, RNN_HEAD_DIM), lru_b_x=vec[_ROW_DBX].reshape(RNN_HEADS, RNN_HEAD_DIM),
        lru_lambda=-vec[_ROW_DSP] * jax.nn.sigmoid(-lam),
        sgu_ln_g=dlg[0], sgu_ln_b=dlb[0], sgu_w_s=dws, sgu_b_s=dbs.T)
    return dx, gb, gs


def _local_step(x, target, big, sm, ts):
    saved = []
    core = jnp.zeros((1,), jnp.int32)
    for l in range(DEPTH):
        sv = _layer_fwd_mix(x, big[l], _layer_small(sm, l, core), ts)
        x = _layer_fwd_out(sv, big[l], ts)
        saved.append(sv)
    dx, loss, dgf = _loss_call(x, target, sm["final_norm_g"].reshape(1, -1), ts)
    gb, gs = [None] * DEPTH, [None] * DEPTH
    for l in reversed(range(DEPTH)):
        dx1, gb_ffn, gs_ffn = _layer_bwd_ffn(dx, saved[l], big[l], ts)
        merge_out, gb_merge = _layer_bwd_merge(dx1, saved[l], big[l], ts)
        dx, gb_mix, gs_mix = _layer_bwd_branches(dx1, merge_out, saved[l], big[l], sm["lru_lambda"][l], ts)
        gb[l] = {**gb_ffn, **gb_merge, **gb_mix}
        gs[l] = {**gs_ffn, **gs_mix}
    gs = {k: jnp.stack([g[k] for g in gs]) for k in gs[0]}
    gs["final_norm_g"] = dgf[0]
    return loss, dx, gb, gs


EW_VMEM_BYTES = 24 * 1024 * 1024


def _row_block(rows, cols, bytes_per_elem):
    for br in range(min(rows, EW_VMEM_BYTES // (2 * bytes_per_elem * cols)), 0, -1):
        if rows % br == 0 and br % 16 == 0:
            return br
    return rows


def _ew_call(fn, name, operands, outputs, slabs=1, sel=None, into=None, after=None):
    if into is not None and not isinstance(into, (list, tuple)):
        into = [into]
    rows, cols = outputs[0][0].shape[2:]
    br = _row_block(rows, cols, sum(jnp.dtype(a.dtype).itemsize for a, _ in operands + outputs))
    n_in = len(operands)

    def pick(tok, g, s):
        if callable(tok):
            return tok(g, s)
        if tok == "g":
            return g
        if isinstance(tok, tuple):
            return s[tok[1]]
        return tok

    def spec(idx):
        return pl.BlockSpec((None, None, br, cols),
                            lambda g, i, s, idx=idx: (pick(idx[0], g, s), pick(idx[1], g, s), i, 0))

    if sel is None:
        sel = jnp.zeros((1,), jnp.int32)
    in_specs = [spec(idx) for _, idx in operands]
    arrays = [a for a, _ in operands]
    aliases = {}
    for j, buf in enumerate(into or ()):
        in_specs.append(pl.BlockSpec(memory_space=pl.ANY))
        arrays.append(buf)
        aliases[1 + n_in + j] = j
    if after is not None:
        in_specs.append(pl.BlockSpec(memory_space=pl.ANY))
        arrays.append(after)

    def body(sel_ref, *refs):
        outs = fn(*[r[...] for r in refs[:n_in]])
        for o_ref, o in zip(refs[len(arrays):], outs):
            o_ref[...] = o.astype(o_ref.dtype)

    return pl.pallas_call(
        body, name=name, out_shape=[s for s, _ in outputs],
        grid_spec=pltpu.PrefetchScalarGridSpec(
            num_scalar_prefetch=1, grid=(slabs, rows // br),
            in_specs=in_specs,
            out_specs=[spec(idx) for _, idx in outputs]),
        input_output_aliases=aliases,
        compiler_params=_params(("parallel", "parallel")),
    )(sel, *arrays)


def _as4(a):
    return a.reshape((1,) * (4 - a.ndim) + a.shape)


def _adamw(w, g, m, v):
    m = ADAM_B1 * m + (1.0 - ADAM_B1) * g
    v = ADAM_B2 * v + (1.0 - ADAM_B2) * jnp.square(g)
    m_hat = m / (1.0 - ADAM_B1 ** ADAM_STEP)
    v_hat = v / (1.0 - ADAM_B2 ** ADAM_STEP)
    delta = -ADAM_LR * (m_hat / (jnp.sqrt(v_hat) + ADAM_EPS) + ADAM_WD * w)
    return delta, m, v


def _small_adamw_call(ws, gs, ms, vs):
    n = len(ws)

    def body(*refs):
        for k in range(n):
            w, g, m, v = (refs[j * n + k][...] for j in range(4))
            outs = _adamw(w, g, m, v)
            for j in range(3):
                refs[(4 + j) * n + k][...] = outs[j]

    shapes = [jax.ShapeDtypeStruct(w.shape, F32) for w in ws]
    outs = pl.pallas_call(
        body, name="adamw_small", out_shape=shapes * 3,
        in_specs=[pl.BlockSpec(memory_space=pltpu.VMEM)] * (4 * n),
        out_specs=[pl.BlockSpec(memory_space=pltpu.VMEM)] * (3 * n),
        compiler_params=_params(),
    )(*ws, *gs, *ms, *vs)
    return outs[:n], outs[n:2 * n], outs[2 * n:]


ANY = pl.BlockSpec(memory_space=pl.ANY)


def _place():
    x, y, c = lax.axis_index("x"), lax.axis_index("y"), lax.axis_index("c")
    chips = [(1 - x, y), (x, 1 - y), (1 - x, 1 - y)]
    return x, y, c, chips


def _remote(src, dst, send_sem, recv_sem, to):
    return pltpu.make_async_remote_copy(src_ref=src, dst_ref=dst, send_sem=send_sem, recv_sem=recv_sem,
                                        device_id=to, device_id_type=MESH)


def _sibling_send_call(items):
    n = len(items)

    def body(*refs):
        src, out = refs[:n], refs[n:2 * n]
        send_sems, recv_sems = refs[2 * n:]
        x, y, c, _ = _place()
        copies = [_remote(src[w], out[w], send_sems.at[w], recv_sems.at[w], (x, y, 1 - c)) for w in range(n)]
        for cp in copies:
            cp.start()
        for cp in copies:
            cp.wait()

    return pl.pallas_call(
        body, name="grads_to_sibling",
        out_shape=[jax.ShapeDtypeStruct(a.shape, a.dtype) for a in items],
        in_specs=[ANY] * n, out_specs=[ANY] * n,
        scratch_shapes=[pltpu.SemaphoreType.DMA((n,)), pltpu.SemaphoreType.DMA((n,))],
        compiler_params=_params(vmem=False, has_side_effects=True),
    )(*items)


def _sibling_inplace_call(name, bufs, slabs, n_pairs):
    n = len(bufs)

    def body(*refs):
        out = refs[n:2 * n]
        send_sems, recv_sems = refs[2 * n:]
        x, y, c, _ = _place()
        sibling = (x, y, 1 - c)
        pairs = [pair for w, ref in enumerate(out) for pair in slabs(ref, c, w)]
        sends = [_remote(s, s, send_sems.at[k], recv_sems.at[k], sibling) for k, (s, _) in enumerate(pairs)]
        for cp in sends:
            cp.start()
        for k, (_, r) in enumerate(pairs):
            _remote(r, r, send_sems.at[k], recv_sems.at[k], sibling).wait_recv()
        for cp in sends:
            cp.wait_send()

    return pl.pallas_call(
        body, name=name,
        out_shape=[jax.ShapeDtypeStruct(a.shape, a.dtype) for a in bufs],
        in_specs=[ANY] * n, out_specs=[ANY] * n,
        input_output_aliases={w: w for w in range(n)},
        scratch_shapes=[pltpu.SemaphoreType.DMA((n_pairs,)), pltpu.SemaphoreType.DMA((n_pairs,))],
        compiler_params=_params(vmem=False, has_side_effects=True),
    )(*bufs)


HBM_SPEC = pl.BlockSpec(memory_space=pltpu.HBM)
SEM_SPEC = pl.BlockSpec(memory_space=pltpu.SEMAPHORE)
DATAFLOW_EFFECT = pltpu.SideEffectType.DATAFLOW_SIDE_EFFECTING


def _exchange_start(name, bufs, copies, n_copies, after):
    n = len(bufs)

    def body(*refs):
        ins, send_sems, recv_sems, token = refs[:n], refs[n + 1], refs[n + 2], refs[-1]
        for k, (src, dst, to) in enumerate(copies(ins)):
            _remote(src, dst, send_sems.at[k], recv_sems.at[k], to).start()
        token[...] = jnp.zeros_like(token)

    outs = pl.pallas_call(
        body, name=name,
        out_shape=(pltpu.SemaphoreType.DMA((n_copies,)), pltpu.SemaphoreType.DMA((n_copies,)),
                   *[pltpu.HBM(b.shape, b.dtype) for b in bufs], jax.ShapeDtypeStruct((SUBLANES, 128), F32)),
        in_specs=[HBM_SPEC] * n + [ANY],
        out_specs=(SEM_SPEC, SEM_SPEC, *[HBM_SPEC] * n, pl.BlockSpec(memory_space=pltpu.VMEM)),
        input_output_aliases={w: w + 2 for w in range(n)},
        compiler_params=pltpu.CompilerParams(has_side_effects=DATAFLOW_EFFECT),
    )(*[pltpu.with_memory_space_constraint(b, pltpu.HBM) for b in bufs], after)
    return outs[0], outs[1], list(outs[2:2 + n]), outs[-1]


def _exchange_wait(name, send_sems, recv_sems, bufs, copies, after):
    n = len(bufs)

    def body(*refs):
        ins, send_sems, recv_sems = refs[:n], refs[n], refs[n + 1]
        for k, (src, dst, to) in enumerate(copies(ins)):
            cp = _remote(src, dst, send_sems.at[k], recv_sems.at[k], to)
            cp.wait_send()
            cp.wait_recv()

    return pl.pallas_call(
        body, name=name,
        out_shape=[pltpu.HBM(b.shape, b.dtype) for b in bufs],
        in_specs=[HBM_SPEC] * n + [SEM_SPEC, SEM_SPEC, ANY],
        out_specs=[HBM_SPEC] * n,
        input_output_aliases={w: w for w in range(n)},
        compiler_params=pltpu.CompilerParams(has_side_effects=DATAFLOW_EFFECT),
    )(*bufs, send_sems, recv_sems, after)


def _gather_copies(refs):
    x, y, c, chips = _place()
    mine = 2 * (2 * x + y) + c
    return [(ref.at[mine], ref.at[mine], (qx, qy, c)) for ref in refs for qx, qy in chips]


def _forward_copies(refs):
    x, y, c, chips = _place()
    return [(ref.at[2 * (2 * qx + qy) + c], ref.at[2 * (2 * qx + qy) + c], (x, y, 1 - c))
            for ref in refs for qx, qy in chips]


def _gather_forward_slabs(ref, c, w):
    x, y, _, chips = _place()
    return [(ref.at[2 * (2 * qx + qy) + c], ref.at[2 * (2 * qx + qy) + 1 - c]) for qx, qy in chips]


def _device_peers():
    x, y, c, _ = _place()
    return 4 * x + 2 * y + c, [(k, (x ^ ((k >> 2) & 1), y ^ ((k >> 1) & 1), c ^ (k & 1))) for k in range(1, 8)]


def _small_scatter_copies(refs):
    me, peers = _device_peers()
    return [(refs[0].at[me ^ k], refs[1].at[me], to) for k, to in peers]


def _small_spread_copies(refs):
    me, peers = _device_peers()
    return [(refs[0].at[me], refs[0].at[me], to) for _, to in peers]


def _sibling_copies(refs):
    n = len(refs) // 2
    x, y, c, _ = _place()
    return [(refs[w], refs[n + w], (x, y, 1 - c)) for w in range(n)]


def _owner_copies(refs):
    n = len(refs) // 2
    x, y, c, chips = _place()
    return [(refs[w].at[2 * qx + qy], refs[n + w].at[j], (qx, qy, c))
            for w in range(n) for j, (qx, qy) in enumerate(chips)]


N_DEVICES = 8
SMALL_ROWS = 616


SMALL = ("norm_mix_g", "conv_w", "conv_b", "lru_w_a", "lru_b_a", "lru_w_x", "lru_b_x", "lru_lambda",
         "sgu_ln_g", "sgu_ln_b", "sgu_w_s", "sgu_b_s", "norm_ffn_g", "final_norm_g")
WEIGHTS = ("norm_mix_g", "w_in", "conv_w", "conv_b", "lru_w_a", "lru_b_a", "lru_w_x", "lru_b_x", "lru_lambda",
           "sgu_ln_g", "sgu_ln_b", "sgu_w_s", "sgu_b_s", "w_branch_a", "w_branch_b", "w_out", "norm_ffn_g",
           "w_up", "w_down", "final_norm_g")
PACK_ALIGN = SUBLANES * 128


PACKED = SMALL + ("loss",)


def _pack_small(gs):
    parts = []
    for k in PACKED:
        flat = gs[k].reshape(-1)
        parts.append(jnp.pad(flat, (0, -flat.size % PACK_ALIGN)))
    flat = jnp.concatenate(parts)
    flat = jnp.pad(flat, (0, N_DEVICES * SMALL_ROWS * 128 - flat.size))
    return flat.reshape(N_DEVICES, SMALL_ROWS, 128)


def _unpack_small(buf, like):
    flat = buf.reshape(-1)
    out, off = {}, 0
    for k in PACKED:
        size = like[k].size
        out[k] = flat[off:off + size].reshape(like[k].shape)
        off += size + (-size % PACK_ALIGN)
    return out


def _as_rows(a):
    return a.reshape(-1, a.shape[-1])


def kernel(x, norm_mix_g, w_in, conv_w, conv_b, lru_w_a, lru_b_a, lru_w_x, lru_b_x, lru_lambda, sgu_ln_g, sgu_ln_b, sgu_w_s, sgu_b_s, w_branch_a, w_branch_b, w_out, norm_ffn_g, w_up, w_down, final_norm_g, loss_target, m_norm_mix_g, m_w_in, m_conv_w, m_conv_b, m_lru_w_a, m_lru_b_a, m_lru_w_x, m_lru_b_x, m_lru_lambda, m_sgu_ln_g, m_sgu_ln_b, m_sgu_w_s, m_sgu_b_s, m_w_branch_a, m_w_branch_b, m_w_out, m_norm_ffn_g, m_w_up, m_w_down, m_final_norm_g, v_norm_mix_g, v_w_in, v_conv_w, v_conv_b, v_lru_w_a, v_lru_b_a, v_lru_w_x, v_lru_b_x, v_lru_lambda, v_sgu_ln_g, v_sgu_ln_b, v_sgu_w_s, v_sgu_b_s, v_w_branch_a, v_w_branch_b, v_w_out, v_norm_ffn_g, v_w_up, v_w_down, v_final_norm_g):
    w = dict(norm_mix_g=norm_mix_g, w_in=w_in, conv_w=conv_w, conv_b=conv_b, lru_w_a=lru_w_a, lru_b_a=lru_b_a,
             lru_w_x=lru_w_x, lru_b_x=lru_b_x, lru_lambda=lru_lambda, sgu_ln_g=sgu_ln_g, sgu_ln_b=sgu_ln_b,
             sgu_w_s=sgu_w_s, sgu_b_s=sgu_b_s, w_branch_a=w_branch_a, w_branch_b=w_branch_b, w_out=w_out,
             norm_ffn_g=norm_ffn_g, w_up=w_up, w_down=w_down, final_norm_g=final_norm_g)
    m = dict(norm_mix_g=m_norm_mix_g, w_in=m_w_in, conv_w=m_conv_w, conv_b=m_conv_b, lru_w_a=m_lru_w_a,
             lru_b_a=m_lru_b_a, lru_w_x=m_lru_w_x, lru_b_x=m_lru_b_x, lru_lambda=m_lru_lambda,
             sgu_ln_g=m_sgu_ln_g, sgu_ln_b=m_sgu_ln_b, sgu_w_s=m_sgu_w_s, sgu_b_s=m_sgu_b_s,
             w_branch_a=m_w_branch_a, w_branch_b=m_w_branch_b, w_out=m_w_out, norm_ffn_g=m_norm_ffn_g,
             w_up=m_w_up, w_down=m_w_down, final_norm_g=m_final_norm_g)
    v = dict(norm_mix_g=v_norm_mix_g, w_in=v_w_in, conv_w=v_conv_w, conv_b=v_conv_b, lru_w_a=v_lru_w_a,
             lru_b_a=v_lru_b_a, lru_w_x=v_lru_w_x, lru_b_x=v_lru_b_x, lru_lambda=v_lru_lambda,
             sgu_ln_g=v_sgu_ln_g, sgu_ln_b=v_sgu_ln_b, sgu_w_s=v_sgu_w_s, sgu_b_s=v_sgu_b_s,
             w_branch_a=v_w_branch_a, w_branch_b=v_w_branch_b, w_out=v_w_out, norm_ffn_g=v_norm_ffn_g,
             w_up=v_w_up, w_down=v_w_down, final_norm_g=v_final_norm_g)
    core = lax.axis_index("c")
    chip = 2 * lax.axis_index("x") + lax.axis_index("y")
    sel = jnp.stack([core, 1 - core, chip, 2 * chip + core]).astype(jnp.int32)
    this_core, this_chip = ("sel", 0), ("sel", 2)
    sds = jax.ShapeDtypeStruct

    ts = TOKEN_TILE

    def after_all(arrays):
        return jnp.stack([a[(0,) * a.ndim].astype(F32) for a in arrays])

    halves = {k:(w[k].shape[1] // 2, w[k].shape[2]) for k in BIG}

    def half_view(k, a):
        return a.reshape((2 * N_QUARTERS,) + halves[k])

    def full_view(k, a):
        if k == "conv_w":
            return a.reshape(N_QUARTERS, DEPTH, CONV_WIDTH, -1).transpose(1, 2, 0, 3).reshape(DEPTH, CONV_WIDTH, D_RNN)
        r2, cols = halves[k]
        if k in ("w_in", "w_up"):
            return a.reshape(1, N_QUARTERS, 2 * r2, cols)
        return a.reshape(1, 2 * N_QUARTERS * r2, cols)

    layer_bufs = [{}, {}]

    def cast_weights(k, after):
        _, r, cols = w[k].shape
        w4 = w[k].reshape(DEPTH, 1, r, cols)
        outs = _ew_call(lambda a, b: (a, b), "cast_weights", [(w4, (0, 0)), (w4, (1, 0))],
                        [(sds((1, N_QUARTERS, r, cols), BF), (0, this_chip))] * DEPTH, 1, sel, after=after)
        for l in range(DEPTH):
            layer_bufs[l][k] = half_view(k, outs[l])

    conv_buf = lax.dynamic_update_slice_in_dim(
        jnp.zeros((N_QUARTERS, DEPTH) + conv_w.shape[1:], F32), conv_w[None], chip, axis=0)
    layer_bufs[0]["conv_w"] = conv_buf.reshape((2 * N_QUARTERS,) + conv_w.shape[1:])
    sm = {k: w[k] for k in SMALL if k != "conv_w"}

    def gather_start(tag, l, keys, after):
        bufs = [layer_bufs[l][k] for k in keys]
        return _exchange_start(f"gather_start_{tag}", bufs, _gather_copies, 3 * len(keys), after)

    def gather_finish(tag, keys, started, after):
        send_sems, recv_sems, thru, _ = started
        landed = _exchange_wait(f"gather_wait_{tag}", send_sems, recv_sems, thru, _gather_copies, after)
        landed = _sibling_inplace_call("gather_forward", landed, _gather_forward_slabs, 3 * len(keys))
        return {k: full_view(k, a) for k, a in zip(keys, landed)}

    first, rest = ("w_in",), tuple(k for k in BIG if k != "w_in")
    cast_weights("w_in", None)
    started_a = gather_start("0a", 0, first + ("conv_w",), sel)
    for k in rest:
        cast_weights(k, started_a[3])
    started_b = gather_start("0b", 0, rest, started_a[3])
    started_c = gather_start("1a", 1, first, started_b[3])
    started_d = gather_start("1b", 1, rest, started_c[3])

    def arrives(tag, keys, started):
        state = {}

        def hook(after):
            landed = _exchange_wait(f"gather_wait_{tag}", started[0], started[1], started[2], _gather_copies, after)
            state["forward"] = _exchange_start(f"forward_start_{tag}", landed, _forward_copies, 3 * len(keys), after)
            return state["forward"][3]

        def finish(after):
            send_sems, recv_sems, thru, _ = state["forward"]
            done = _exchange_wait(f"forward_wait_{tag}", send_sems, recv_sems, thru, _forward_copies, after)
            return {k: full_view(k, a) for k, a in zip(keys, done)}

        return hook, finish

    p0, p1 = _layer_small(sm, 0, sel[0:1]), _layer_small(sm, 1, sel[0:1])
    h0 = _norm_call(x[0], p0["g1"], ts)
    proj_own = _inproj_part_call(h0, full_view("w_in", started_a[2][0]), 2 * ts, sel[2:3], 0, 1)
    ready = after_all([started_d[3], proj_own] + [p[k] for p in (p0, p1) for k in ("wa", "wx", "wm")])
    big0 = gather_finish("0a", first + ("conv_w",), started_a, ready)
    for l, p in enumerate((p0, p1)):
        p["cw"] = big0["conv_w"][l]
    proj0 = _inproj_part_call(h0, big0["w_in"], 2 * ts, sel[2:3], 1, N_QUARTERS - 1, proj_own)
    hook, finish = arrives("0b", rest, started_b)
    sv0 = _layer_fwd_mix(x[0], big0, p0, ts, h0, hook, proj0)
    big0.update(finish(sv0["yb_pre"]))
    x_mid = _layer_fwd_out(sv0, big0, ts)
    hook, finish = arrives("1a", first, started_c)
    h1 = _norm_call(x_mid, p1["g1"], ts, hook(x_mid))
    big1 = finish(h1)
    hook, finish = arrives("1b", rest, started_d)
    sv1 = _layer_fwd_mix(x_mid, big1, p1, ts, h1, hook)
    big1.update(finish(sv1["yb_pre"]))
    x_out = _layer_fwd_out(sv1, big1, ts)
    dx, loss, dgf = _loss_call(x_out, loss_target[0], final_norm_g.reshape(1, -1), ts)

    def pair_start(tag, gb, after):
        sends = [gb[k][1] for k in gb]
        zones = [lax.empty(a.shape, BF) for a in sends]
        return _exchange_start(f"pair_start_{tag}", sends + zones, _sibling_copies, len(sends), after)

    def reduce_start(tag, gb, after, pair=None):
        keys = tuple(gb)
        if pair is None:
            from_sibling = _sibling_send_call([gb[k][1] for k in keys])
        else:
            done = _exchange_wait(f"pair_wait_{tag}", pair[0], pair[1], pair[2], _sibling_copies, after)
            from_sibling = done[len(keys):]
        sums = [
            _ew_call(lambda a, b: (a.astype(F32) + b.astype(F32),), "pair_sum", [(gb[k][0][None], (0, "g")), (r[None], (0, "g"))],
                     [(sds((1,) + r.shape, BF), (0, "g"))], N_QUARTERS)[0][0]
            for k, r in zip(keys, from_sibling)]
        zones = [lax.empty((3,) + a.shape[1:], BF) for a in sums]
        started = _exchange_start(f"reduce_start_{tag}", sums + zones, _owner_copies, 3 * len(keys), after)
        return keys, started

    def reduce_finish(tag, l, keys_started, after, reduced):
        keys, (send_sems, recv_sems, thru, _) = keys_started
        done = _exchange_wait(f"reduce_wait_{tag}", send_sems, recv_sems, thru, _owner_copies, after)
        sums, zones = done[:len(keys)], done[len(keys):]
        for i, k in enumerate(keys):
            r2, cols = halves[k]
            reduced[k] = _ew_call(
                lambda a, b, c, d: (((a.astype(F32) + b.astype(F32)) + c.astype(F32)) + d.astype(F32),),
                "quarter_sum", [(sums[i][None], (0, this_chip))] + [(zones[i][None], (0, j)) for j in range(3)],
                [(sds((DEPTH, 2, r2, cols), F32), (l, this_core))], 1, sel, into=reduced.get(k))[0]

    dx1, gb_ffn, gs1 = _layer_bwd_ffn(dx, sv1, big1, ts)
    merge_out, gb_merge = _layer_bwd_merge(dx1, sv1, big1, ts)
    dx_mid, gb_in, gs1_mix = _layer_bwd_branches(dx1, merge_out, sv1, big1, lru_lambda[1], ts)
    gb_1 = {**gb_ffn, **gb_merge, **gb_in}
    pair_1 = pair_start("1", gb_1, dx_mid)
    dx1, gb_ffn, gs0 = _layer_bwd_ffn(dx_mid, sv0, big0, ts, pair_1[3])
    exchange_1 = reduce_start("1", gb_1, dx1, pair_1)
    pair_0a = pair_start("0a", gb_ffn, exchange_1[1][3])
    merge_out, gb_merge = _layer_bwd_merge(dx1, sv0, big0, ts, pair_0a[3])
    exchange_0a = reduce_start("0a", gb_ffn, merge_out[0], pair_0a)
    pair_0b = pair_start("0b", gb_merge, exchange_0a[1][3])
    started_0b = {}

    def after_sgu(duv):
        started_0b["exchange"] = reduce_start("0b", gb_merge, duv, pair_0b)
        return started_0b["exchange"][1][3]

    grad_x, gb_in, gs0_mix = _layer_bwd_branches(dx1, merge_out, sv0, big0, lru_lambda[0], ts, pair_0b[3],
                                                 after_sgu)
    exchange_0b = started_0b["exchange"]
    exchange_0c = reduce_start("0c", gb_in, exchange_0b[1][3])
    layer_gs = [{**gs0, **gs0_mix}, {**gs1, **gs1_mix}]
    gs = {k: jnp.stack([g[k] for g in layer_gs]) for k in layer_gs[0]}
    gs["final_norm_g"] = dgf[0]
    gs["loss"] = loss[0, 0:1]

    me = ("sel", 3)
    piece = (1, N_DEVICES, SMALL_ROWS, 128)
    packed = _pack_small(gs).reshape(piece)
    scatter = _exchange_start("small_scatter_start", [packed[0], lax.empty(piece[1:], F32)], _small_scatter_copies,
                              N_DEVICES - 1, exchange_0c[1][3])
    reduced = {}
    reduce_finish("1", 1, exchange_1, scatter[3], reduced)
    reduce_finish("0a", 0, exchange_0a, reduced["w_in"], reduced)
    reduce_finish("0b", 0, exchange_0b, reduced["w_down"], reduced)

    def swap_slabs(ref, c, i):
        layers = (1,) if BIG[i] == "w_in" else range(DEPTH)
        return [(ref.at[l, c], ref.at[l, 1 - c]) for l in layers]

    swapped = dict(zip(BIG, _sibling_inplace_call("grads_swap_halves", [reduced[k] for k in BIG], swap_slabs,
                                                  DEPTH * len(BIG) - 1)))

    def adamw_layers(k, grad, layer, into, after=None):
        if layer is None:
            views = [_as4(_as_rows(a)) for a in (w[k], grad, m[k], v[k])]
            idx = (0, 0)
        else:
            views = [a.reshape((1,) + w[k].shape) for a in (w[k], grad, m[k], v[k])]
            idx = (0, layer)
        return _ew_call(_adamw, "adamw_big", [(a, idx) for a in views], [(sds(views[0].shape, F32), idx)] * 3,
                        into=into, after=after)

    updated, last_update = {}, None
    for k in BIG:
        updated[k] = adamw_layers(k, swapped[k], 1 if k == "w_in" else None, None, last_update)
        last_update = updated[k][0]
    scattered = _exchange_wait("small_scatter_wait", scatter[0], scatter[1], scatter[2], _small_scatter_copies,
                               last_update)
    summed = _ew_call(
        lambda *parts: (functools.reduce(lambda a, b: a + b, parts),), "small_sum",
        [(scattered[0][None], (0, me))]
        + [(scattered[1][None], (0, lambda g, s, k=k: s[3] ^ k)) for k in range(1, N_DEVICES)],
        [(sds(piece, F32), (0, me))], 1, sel)[0]
    spread = _exchange_start("small_spread_start", [summed[0]], _small_spread_copies, N_DEVICES - 1, summed)
    reduced["w_in"] = swapped["w_in"]
    reduce_finish("0c", 0, exchange_0c, spread[3], reduced)
    last = _sibling_inplace_call("grads_swap_last", [reduced["w_in"]],
                                 lambda ref, c, i: [(ref.at[0, c], ref.at[0, 1 - c])], 1)[0]
    swapped["w_in"] = last
    updated["w_in"] = adamw_layers("w_in", last, 0, updated["w_in"])
    grads_big = {k: swapped[k].reshape(w[k].shape) for k in BIG}
    delta, new_m, new_v = ({k: updated[k][j].reshape(w[k].shape) for k in BIG} for j in range(3))
    gathered_small = _exchange_wait("small_spread_wait", spread[0], spread[1], spread[2], _small_spread_copies,
                                    updated["w_in"][0])[0]

    like = {k: jax.ShapeDtypeStruct(gs[k].shape, F32) for k in SMALL}
    like["loss"] = jax.ShapeDtypeStruct((1,), F32)
    grads_small = _unpack_small(gathered_small, like)
    total = grads_small.pop("loss")[0]
    conv_q = grads_small["conv_w"].reshape(DEPTH, CONV_WIDTH, N_QUARTERS, D_RNN // N_QUARTERS)
    grads_small["conv_w"] = lax.dynamic_index_in_dim(conv_q, chip, axis=2, keepdims=False)
    outs = _small_adamw_call(*[[_as_rows(d[k]) for k in SMALL] for d in (w, grads_small, m, v)])
    for d, o in zip((delta, new_m, new_v), outs):
        for k, a in zip(SMALL, o):
            d[k] = a.reshape(w[k].shape)

    grads = {**grads_big, **grads_small}
    return (total, grad_x[None], *[grads[k] for k in WEIGHTS], *[delta[k] for k in WEIGHTS],
            *[new_m[k] for k in WEIGHTS], *[new_v[k] for k in WEIGHTS])
```

```python
import functools
import math

import jax
import jax.numpy as jnp
from jax import lax
from jax.experimental import pallas as pl
from jax.experimental.pallas import tpu as pltpu

F32 = jnp.float32
BF = jnp.bfloat16

DEPTH = 2
D_MODEL = 1024
D_RNN = 1280
D_SGU = 1024
D_FF = 4096
D_IN = 2 * D_RNN + 2 * D_SGU + 2 * D_MODEL
N_QUARTERS = 4
Q_IN = D_IN // N_QUARTERS
Q_FF = D_FF // N_QUARTERS
RNN_HEADS = 20
RNN_HEAD_DIM = 64
LRU_GROUP = 256
N_LRU_GROUPS = D_RNN // LRU_GROUP
HEADS_PER_GROUP = LRU_GROUP // RNN_HEAD_DIM
CONV_WIDTH = 4
LRU_C = 8.0
SGU_GROUPS = 8
SGU_BLOCK = 128
CHUNK = 64
EPS = 1e-6

ADAM_LR = 0.001
ADAM_B1 = 0.9
ADAM_B2 = 0.999
ADAM_EPS = 1e-08
ADAM_WD = 0.01
ADAM_STEP = 10

SUBLANES = 8
TOKEN_TILE = 512
VMEM_LIMIT_BYTES = 56 * 1024 * 1024

MESH = pl.DeviceIdType.MESH


def _params(semantics=None, vmem=True, **kw):
    return pltpu.CompilerParams(
        dimension_semantics=semantics,
        vmem_limit_bytes=VMEM_LIMIT_BYTES if vmem else None,
        **kw,
    )


def _dot(a, b):
    return jnp.dot(a, b, preferred_element_type=F32)


def _dot_nt(a, b):
    return lax.dot_general(a, b, (((1,), (1,)), ((), ())), preferred_element_type=F32)


def _dot_tn(a, b):
    return lax.dot_general(a, b, (((0,), (0,)), ((), ())), preferred_element_type=F32)


_GELU_C = math.sqrt(2.0 / math.pi)
_GELU_A = 0.044715


def _gelu(x):
    return 0.5 * x * (1.0 + jnp.tanh(_GELU_C * (x + _GELU_A * x * x * x)))


def _gelu_and_grad(x):
    x2 = x * x
    t = jnp.tanh(_GELU_C * (x + _GELU_A * x2 * x))
    du = _GELU_C * (1.0 + 3.0 * _GELU_A * x2)
    return 0.5 * x * (1.0 + t), 0.5 * (1.0 + t) + 0.5 * x * (1.0 - t * t) * du


def _rms_stats(x):
    return lax.rsqrt(jnp.mean(x * x, axis=-1, keepdims=True) + EPS)


def _rms_bwd(dy, x, g):
    rs = _rms_stats(x)
    n = x * rs
    dn = dy * g
    dx = rs * (dn - n * jnp.mean(dn * n, axis=-1, keepdims=True))
    return dx, dy * n


def _row_sum(x):
    return jnp.sum(x, axis=0, keepdims=True)


def _tile_spec(ts, width, col=0):
    return pl.BlockSpec((ts, width), lambda i, col=col: (i, col))


def _full_spec(shape):
    zeros = (0,) * len(shape)
    return pl.BlockSpec(shape, lambda *_: zeros)


def _layer_spec(w, layer):
    zeros = (0,) * (w.ndim - 1)
    return pl.BlockSpec((None,) + tuple(w.shape[1:]), lambda *_: (layer,) + zeros)


def _with_after(body, n_in, after):
    if after is None:
        return body, [], []

    def wrapped(*refs):
        return body(*refs[:n_in], *refs[n_in + 1:])

    return wrapped, [pl.BlockSpec(memory_space=pl.ANY)], [after]


def _norm_call(x, g, ts, after=None):
    s = x.shape[0]

    def body(x_ref, g_ref, h_ref):
        xv = x_ref[...]
        h_ref[...] = (xv * _rms_stats(xv) * g_ref[...]).astype(BF)

    body, more_specs, more = _with_after(body, 2, after)
    return pl.pallas_call(
        body, name="norm_fwd", grid=(s // ts,),
        in_specs=[_tile_spec(ts, D_MODEL), _full_spec((1, D_MODEL))] + more_specs,
        out_specs=_tile_spec(ts, D_MODEL),
        out_shape=jax.ShapeDtypeStruct((s, D_MODEL), BF),
        compiler_params=_params(("parallel",)),
    )(x, g, *more)


def _inproj_call(h, w_in, layer, ts):
    s = h.shape[0]

    def body(h_ref, w_ref, o_ref):
        o_ref[...] = _dot(h_ref[...], w_ref[...]).astype(BF)

    return pl.pallas_call(
        body, name="inproj_fwd", grid=(N_QUARTERS, s // ts),
        in_specs=[
            pl.BlockSpec((ts, D_MODEL), lambda q, i: (i, 0)),
            pl.BlockSpec((None, None, D_MODEL, Q_IN), lambda q, i: (layer, q, 0, 0)),
        ],
        out_specs=pl.BlockSpec((ts, Q_IN), lambda q, i: (i, q)),
        out_shape=jax.ShapeDtypeStruct((s, D_IN), BF),
        compiler_params=_params(("parallel", "parallel")),
    )(h, w_in)


def _inproj_part_call(h, w_in, ts, own, first, count, into=None):
    s = h.shape[0]

    def quarter(j, sel):
        return (sel[0] + first + j) % N_QUARTERS

    def body(sel_ref, h_ref, w_ref, *rest):
        rest[-1][...] = _dot(h_ref[...], w_ref[...]).astype(BF)

    in_specs = [pl.BlockSpec((ts, D_MODEL), lambda j, i, sel: (i, 0)),
                pl.BlockSpec((None, None, D_MODEL, Q_IN), lambda j, i, sel: (0, quarter(j, sel), 0, 0))]
    operands = [h, w_in]
    aliases = {}
    if into is not None:
        in_specs.append(pl.BlockSpec(memory_space=pl.ANY))
        operands.append(into)
        aliases = {3: 0}
    return pl.pallas_call(
        body, name="inproj_fwd_part", out_shape=jax.ShapeDtypeStruct((s, D_IN), BF),
        grid_spec=pltpu.PrefetchScalarGridSpec(
            num_scalar_prefetch=1, grid=(count, s // ts), in_specs=in_specs,
            out_specs=pl.BlockSpec((ts, Q_IN), lambda j, i, sel: (i, quarter(j, sel)))),
        input_output_aliases=aliases,
        compiler_params=_params(("parallel", "parallel")),
    )(own, *operands)


def _shift_down(x, tail, s):
    xr = pltpu.roll(x, s, 0)
    tr = pltpu.roll(tail, s, 0)
    row = lax.broadcasted_iota(jnp.int32, tail.shape, 0)
    top = jnp.where(row < s, tr, xr[0:SUBLANES])
    return jnp.concatenate([top, xr[SUBLANES:]], axis=0)


def _shift_up(x, head, s):
    t = x.shape[0]
    xr = pltpu.roll(x, t - s, 0)
    hr = pltpu.roll(head, SUBLANES - s, 0)
    row = lax.broadcasted_iota(jnp.int32, head.shape, 0)
    bottom = jnp.where(row >= SUBLANES - s, hr, xr[t - SUBLANES:])
    return jnp.concatenate([xr[: t - SUBLANES], bottom], axis=0)


def _conv_fwd(x, tail, cw_ref, cb_ref):
    out = cb_ref[...] + cw_ref[CONV_WIDTH - 1:CONV_WIDTH, :] * x
    for s in range(1, CONV_WIDTH):
        k = CONV_WIDTH - 1 - s
        out = out + cw_ref[k:k + 1, :] * _shift_down(x, tail, s)
    return out


def _group_dot(x_bf, w_ref, dot):
    cols = [dot(x_bf[:, g * LRU_GROUP:(g + 1) * LRU_GROUP], w_ref[g]) for g in range(N_LRU_GROUPS)]
    return jnp.concatenate(cols, axis=1)


def _lru_gates(xr, wa_ref, wx_ref, ba_ref, bx_ref, sp_ref):
    xb = xr.astype(BF)
    r = jax.nn.sigmoid(_group_dot(xb, wa_ref, _dot) + ba_ref[...])
    i = jax.nn.sigmoid(_group_dot(xb, wx_ref, _dot) + bx_ref[...])
    log_a = (-LRU_C * r) * sp_ref[...]
    a = jnp.exp(log_a)
    nrm2 = -jnp.tanh(log_a) * (a * a + 1.0)
    inv_nrm = lax.rsqrt(jnp.maximum(nrm2, 1e-36))
    return r, i, a, nrm2 * inv_nrm, inv_nrm


def _linear_scan(a, b, carry, al_ref, bl_ref, h_ref, reverse):
    t, c = a.shape
    rowm = lax.broadcasted_iota(jnp.int32, (t, c), 0) & (SUBLANES - 1)
    for d in (1, 2, 4):
        if reverse:
            keep, sh = rowm < SUBLANES - d, t - d
        else:
            keep, sh = rowm >= d, d
        a_sh = jnp.where(keep, pltpu.roll(a, sh, 0), 1.0)
        b_sh = jnp.where(keep, pltpu.roll(b, sh, 0), 0.0)
        b = a * b_sh + b
        a = a * a_sh
    al_ref[...] = a
    bl_ref[...] = b
    groups = t // SUBLANES

    def step(j, state):
        jj = groups - 1 - j if reverse else j
        off = pl.multiple_of(jj * SUBLANES, SUBLANES)
        rows = bl_ref[pl.ds(off, SUBLANES), :] + al_ref[pl.ds(off, SUBLANES), :] * state
        h_ref[pl.ds(off, SUBLANES), :] = rows
        last = rows[0:1, :] if reverse else rows[SUBLANES - 1:SUBLANES, :]
        return jnp.broadcast_to(last, (SUBLANES, c))

    out = lax.fori_loop(0, groups, step, jnp.broadcast_to(carry, (SUBLANES, c)))
    return out[0:1, :]


def _rnn_fwd_call(proj, wa, wx, ba, bx, sp, cw, cb, ts):
    s = proj.shape[0]

    def body(xg_ref, wa_ref, wx_ref, ba_ref, bx_ref, sp_ref, cw_ref, cb_ref, xr_ref, hr_ref, ya_ref,
             tail_sc, carry_sc, al_sc, bl_sc, h_sc):
        @pl.when(pl.program_id(0) == 0)
        def _():
            tail_sc[...] = jnp.zeros_like(tail_sc)
            carry_sc[...] = jnp.zeros_like(carry_sc)

        x = xg_ref[:, :D_RNN].astype(F32)
        g = xg_ref[:, D_RNN:]
        xr = _conv_fwd(x, tail_sc[...], cw_ref, cb_ref)
        tail_sc[...] = x[ts - SUBLANES:, :]
        xr_ref[...] = xr.astype(BF)
        _, i, a, nrm, _ = _lru_gates(xr, wa_ref, wx_ref, ba_ref, bx_ref, sp_ref)
        carry_sc[...] = _linear_scan(a, nrm * (i * xr), carry_sc[...], al_sc, bl_sc, h_sc, False)
        h = h_sc[...]
        hr_ref[...] = h.astype(BF)
        ya_ref[...] = (h * _gelu(g)).astype(BF)

    gw = (N_LRU_GROUPS, LRU_GROUP, LRU_GROUP)
    return pl.pallas_call(
        body, name="rnn_fwd", grid=(s // ts,),
        in_specs=[_tile_spec(ts, 2 * D_RNN), _full_spec(gw), _full_spec(gw),
                  _full_spec((1, D_RNN)), _full_spec((1, D_RNN)), _full_spec((1, D_RNN)),
                  _full_spec((CONV_WIDTH, D_RNN)), _full_spec((1, D_RNN))],
        out_specs=[_tile_spec(ts, D_RNN)] * 3,
        out_shape=[jax.ShapeDtypeStruct((s, D_RNN), BF)] * 3,
        scratch_shapes=[pltpu.VMEM((SUBLANES, D_RNN), F32), pltpu.VMEM((1, D_RNN), F32),
                        pltpu.VMEM((ts, D_RNN), F32), pltpu.VMEM((ts, D_RNN), F32),
                        pltpu.VMEM((ts, D_RNN), F32)],
        compiler_params=_params(("arbitrary",)),
    )(proj, wa, wx, ba, bx, sp, cw, cb)


def _layernorm_fwd(x):
    mu = jnp.mean(x, axis=-1, keepdims=True)
    xc = x - mu
    rstd = lax.rsqrt(jnp.mean(xc * xc, axis=-1, keepdims=True) + EPS)
    return xc * rstd, rstd


def _sgu_mix(vn_bf, wm_ref, bsb_ref, ts):
    rows = []
    for blk in range(ts // SGU_BLOCK):
        r0 = blk * SGU_BLOCK
        cols = [
            _dot(wm_ref[g], vn_bf[r0:r0 + SGU_BLOCK, g * SGU_BLOCK:(g + 1) * SGU_BLOCK]) + bsb_ref[g]
            for g in range(SGU_GROUPS)
        ]
        rows.append(jnp.concatenate(cols, axis=1))
    return jnp.concatenate(rows, axis=0)


def _sgu_fwd_call(proj, wm, bsb, lg, lb, ts, after=None):
    s = proj.shape[0]

    def body(uv_ref, wm_ref, bsb_ref, lg_ref, lb_ref, yb_ref):
        gu = _gelu(uv_ref[:, :D_SGU])
        gv = _gelu(uv_ref[:, D_SGU:2 * D_SGU]).astype(F32)
        nh, _ = _layernorm_fwd(gv)
        vn = (nh * lg_ref[...] + lb_ref[...]).astype(BF)
        yb_ref[...] = (gu * _sgu_mix(vn, wm_ref, bsb_ref, ts)).astype(BF)

    sw = (SGU_GROUPS, SGU_BLOCK, SGU_BLOCK)
    body, more_specs, more = _with_after(body, 5, after)
    return pl.pallas_call(
        body, name="sgu_fwd", grid=(s // ts,),
        in_specs=[_tile_spec(ts, 2 * D_RNN, 1), _full_spec(sw), _full_spec(sw),
                  _full_spec((1, D_SGU)), _full_spec((1, D_SGU))] + more_specs,
        out_specs=_tile_spec(ts, D_SGU),
        out_shape=jax.ShapeDtypeStruct((s, D_SGU), BF),
        compiler_params=_params(("parallel",)),
    )(proj, wm, bsb, lg, lb, *more)


_GATE_COL0 = (2 * D_RNN + 2 * D_SGU) // 512


def _gate_specs(ts):
    return [_tile_spec(ts, 512, _GATE_COL0 + j) for j in range(4)]


def _merge_call(x, proj, ya_pre, yb_pre, w_ba, w_bb, w_out, g2, layer, ts):
    s = x.shape[0]

    def body(x_ref, ga0, ga1, gb0, gb1, ya_ref, yb_ref, wa_ref, wb_ref, wo_ref, g2_ref,
             x1_ref, yao_ref, ybo_ref, mg_ref, h2_ref):
        ya = _dot(ya_ref[...], wa_ref[...])
        yb = _dot(yb_ref[...], wb_ref[...])
        sa = jax.nn.sigmoid(jnp.concatenate([ga0[...], ga1[...]], axis=1).astype(F32))
        sb = jax.nn.sigmoid(jnp.concatenate([gb0[...], gb1[...]], axis=1).astype(F32))
        merged = (sa * ya + sb * yb).astype(BF)
        x1 = x_ref[...] + _dot(merged, wo_ref[...])
        x1_ref[...] = x1
        yao_ref[...] = ya.astype(BF)
        ybo_ref[...] = yb.astype(BF)
        mg_ref[...] = merged
        h2_ref[...] = (x1 * _rms_stats(x1) * g2_ref[...]).astype(BF)

    act = jax.ShapeDtypeStruct((s, D_MODEL), BF)
    return pl.pallas_call(
        body, name="merge_fwd", grid=(s // ts,),
        in_specs=[_tile_spec(ts, D_MODEL)] + _gate_specs(ts) + [
            _tile_spec(ts, D_RNN), _tile_spec(ts, D_SGU),
            _layer_spec(w_ba, layer), _layer_spec(w_bb, layer), _layer_spec(w_out, layer),
            _full_spec((1, D_MODEL))],
        out_specs=[_tile_spec(ts, D_MODEL)] * 5,
        out_shape=[jax.ShapeDtypeStruct((s, D_MODEL), F32), act, act, act, act],
        compiler_params=_params(("parallel",)),
    )(x, proj, proj, proj, proj, ya_pre, yb_pre, w_ba, w_bb, w_out, g2)


def _ffn_call(x1, h2, w_up, w_down, layer, ts):
    s = x1.shape[0]

    def body(x1_ref, h2_ref, wu_ref, wd_ref, x2_ref, p_ref):
        h2v = h2_ref[...]
        acc = x1_ref[...]
        for q in range(N_QUARTERS):
            p = _dot(h2v, wu_ref[q])
            p_ref[:, q * Q_FF:(q + 1) * Q_FF] = p.astype(BF)
            f = jnp.square(jnp.maximum(p, 0.0)).astype(BF)
            acc = acc + _dot(f, wd_ref[q * Q_FF:(q + 1) * Q_FF, :])
        x2_ref[...] = acc

    return pl.pallas_call(
        body, name="ffn_fwd", grid=(s // ts,),
        in_specs=[_tile_spec(ts, D_MODEL), _tile_spec(ts, D_MODEL),
                  pl.BlockSpec((None, N_QUARTERS, D_MODEL, Q_FF), lambda i: (layer, 0, 0, 0)),
                  pl.BlockSpec((None, D_FF, D_MODEL), lambda i: (layer, 0, 0))],
        out_specs=[_tile_spec(ts, D_MODEL), _tile_spec(ts, D_FF)],
        out_shape=[jax.ShapeDtypeStruct((s, D_MODEL), F32), jax.ShapeDtypeStruct((s, D_FF), BF)],
        compiler_params=_params(("parallel",)),
    )(x1, h2, w_up, w_down)


def _loss_call(x, target, gf, ts):
    s = x.shape[0]

    def body(x_ref, t_ref, g_ref, dx_ref, loss_ref, dg_ref):
        @pl.when(pl.program_id(0) == 0)
        def _():
            loss_ref[...] = jnp.zeros_like(loss_ref)
            dg_ref[...] = jnp.zeros_like(dg_ref)

        xv = x_ref[...]
        gv = g_ref[...]
        err = xv * _rms_stats(xv) * gv - t_ref[...]
        part = 0.5 * jnp.sum(jnp.mean(err * err, axis=-1, keepdims=True), axis=0, keepdims=True)
        loss_ref[...] += jnp.broadcast_to(part, loss_ref.shape)
        dx, dg = _rms_bwd(err * (1.0 / D_MODEL), xv, gv)
        dx_ref[...] = dx
        dg_ref[...] += _row_sum(dg)

    return pl.pallas_call(
        body, name="loss_head", grid=(s // ts,),
        in_specs=[_tile_spec(ts, D_MODEL), _tile_spec(ts, D_MODEL), _full_spec((1, D_MODEL))],
        out_specs=[_tile_spec(ts, D_MODEL), _full_spec((1, 128)), _full_spec((1, D_MODEL))],
        out_shape=[jax.ShapeDtypeStruct((s, D_MODEL), F32), jax.ShapeDtypeStruct((1, 128), F32),
                   jax.ShapeDtypeStruct((1, D_MODEL), F32)],
        compiler_params=_params(("arbitrary",)),
    )(x, target, gf)


def _ffn_bwd_call(dx2, p, x1, g2, w_up, w_down, layer, ts, after=None):
    s = dx2.shape[0]

    def body(dx2_ref, p_ref, x1_ref, g2_ref, wu_ref, wd_ref, dx1_ref, dp_ref, dg_ref, dx2b_ref, dx1b_ref):
        @pl.when(pl.program_id(0) == 0)
        def _():
            dg_ref[...] = jnp.zeros_like(dg_ref)

        dx2v = dx2_ref[...]
        dyb = dx2v.astype(BF)
        dx2b_ref[...] = dyb
        dh2 = jnp.zeros((ts, D_MODEL), F32)
        for q in range(N_QUARTERS):
            cols = slice(q * Q_FF, (q + 1) * Q_FF)
            df = _dot_nt(dyb, wd_ref[cols, :])
            dp = (df * (2.0 * jnp.maximum(p_ref[:, cols].astype(F32), 0.0))).astype(BF)
            dp_ref[:, cols] = dp
            dh2 = dh2 + _dot_nt(dp, wu_ref[q])
        dx, dg = _rms_bwd(dh2, x1_ref[...], g2_ref[...])
        dx1 = dx2v + dx
        dx1_ref[...] = dx1
        dx1b_ref[...] = dx1.astype(BF)
        dg_ref[...] += _row_sum(dg)

    body, more_specs, more = _with_after(body, 6, after)
    return pl.pallas_call(
        body, name="ffn_bwd", grid=(s // ts,),
        in_specs=[_tile_spec(ts, D_MODEL), _tile_spec(ts, D_FF), _tile_spec(ts, D_MODEL),
                  _full_spec((1, D_MODEL)),
                  pl.BlockSpec((None, N_QUARTERS, D_MODEL, Q_FF), lambda i: (layer, 0, 0, 0)),
                  pl.BlockSpec((None, D_FF, D_MODEL), lambda i: (layer, 0, 0))] + more_specs,
        out_specs=[_tile_spec(ts, D_MODEL), _tile_spec(ts, D_FF), _full_spec((1, D_MODEL)),
                   _tile_spec(ts, D_MODEL), _tile_spec(ts, D_MODEL)],
        out_shape=[jax.ShapeDtypeStruct((s, D_MODEL), F32), jax.ShapeDtypeStruct((s, D_FF), BF),
                   jax.ShapeDtypeStruct((1, D_MODEL), F32),
                   jax.ShapeDtypeStruct((s, D_MODEL), BF), jax.ShapeDtypeStruct((s, D_MODEL), BF)],
        compiler_params=_params(("arbitrary",)),
    )(dx2, p, x1, g2, w_up, w_down, *more)


def _merge_bwd_call(dx1, proj, ya, yb, w_ba, w_bb, w_out, layer, ts, after=None):
    s = dx1.shape[0]

    def body(dx1_ref, ga0, ga1, gb0, gb1, ya_ref, yb_ref, wa_ref, wb_ref, wo_ref, *rest):
        dya_ref, dyb_ref, dgate_ref, dyap_ref, dybp_ref = rest[-5:]
        dm = _dot_nt(dx1_ref[...].astype(BF), wo_ref[...])
        sa = jax.nn.sigmoid(jnp.concatenate([ga0[...], ga1[...]], axis=1).astype(F32))
        sb = jax.nn.sigmoid(jnp.concatenate([gb0[...], gb1[...]], axis=1).astype(F32))
        dya = (dm * sa).astype(BF)
        dyb = (dm * sb).astype(BF)
        dya_ref[...] = dya
        dyb_ref[...] = dyb
        dgate_ref[:, :D_MODEL] = (dm * ya_ref[...].astype(F32) * sa * (1.0 - sa)).astype(BF)
        dgate_ref[:, D_MODEL:] = (dm * yb_ref[...].astype(F32) * sb * (1.0 - sb)).astype(BF)
        dyap_ref[...] = _dot_nt(dya, wa_ref[...]).astype(BF)
        dybp_ref[...] = _dot_nt(dyb, wb_ref[...]).astype(BF)

    act = jax.ShapeDtypeStruct((s, D_MODEL), BF)
    return pl.pallas_call(
        body, name="merge_bwd", grid=(s // ts,),
        in_specs=[_tile_spec(ts, D_MODEL)] + _gate_specs(ts) + [
            _tile_spec(ts, D_MODEL), _tile_spec(ts, D_MODEL),
            _layer_spec(w_ba, layer), _layer_spec(w_bb, layer), _layer_spec(w_out, layer)]
        + ([] if after is None else [pl.BlockSpec(memory_space=pl.ANY)]),
        out_specs=[_tile_spec(ts, D_MODEL), _tile_spec(ts, D_MODEL), _tile_spec(ts, 2 * D_MODEL),
                   _tile_spec(ts, D_RNN), _tile_spec(ts, D_SGU)],
        out_shape=[act, act, jax.ShapeDtypeStruct((s, 2 * D_MODEL), BF),
                   jax.ShapeDtypeStruct((s, D_RNN), BF), jax.ShapeDtypeStruct((s, D_SGU), BF)],
        compiler_params=_params(("parallel",)),
    )(dx1, proj, proj, proj, proj, ya, yb, w_ba, w_bb, w_out, *([] if after is None else [after]))


def _sgu_bwd_call(dyb_pre, proj, wm, bsb, mask, lg, lb, ts, after=None):
    s = proj.shape[0]

    def body(dy_ref, uv_ref, wm_ref, bsb_ref, mask_ref, lg_ref, lb_ref,
             duv_ref, dws_ref, dbs_ref, dlg_ref, dlb_ref, dm_sc):
        step = pl.program_id(0)

        @pl.when(step == 0)
        def _():
            dws_ref[...] = jnp.zeros_like(dws_ref)
            dlg_ref[...] = jnp.zeros_like(dlg_ref)
            dlb_ref[...] = jnp.zeros_like(dlb_ref)
            dm_sc[...] = jnp.zeros_like(dm_sc)

        gu, dgu_du = _gelu_and_grad(uv_ref[:, :D_SGU])
        gv, dgv_dv = _gelu_and_grad(uv_ref[:, D_SGU:2 * D_SGU])
        nh, rstd = _layernorm_fwd(gv.astype(F32))
        lgv = lg_ref[...]
        vn = (nh * lgv + lb_ref[...]).astype(BF)
        dy = dy_ref[...].astype(F32)
        du = dy * _sgu_mix(vn, wm_ref, bsb_ref, ts) * dgu_du
        dmix = dy * gu
        dmix_bf = dmix.astype(BF)
        dm_acc = dm_sc[...]
        rows = []
        for blk in range(ts // SGU_BLOCK):
            r0 = blk * SGU_BLOCK
            dm_acc = dm_acc + dmix[r0:r0 + SGU_BLOCK, :]
            cols = []
            for g in range(SGU_GROUPS):
                c0 = g * SGU_BLOCK
                dmg = dmix_bf[r0:r0 + SGU_BLOCK, c0:c0 + SGU_BLOCK]
                cols.append(_dot_tn(wm_ref[g], dmg))
                dws_ref[g] += mask_ref[...] * _dot_nt(dmg, vn[r0:r0 + SGU_BLOCK, c0:c0 + SGU_BLOCK])
            rows.append(jnp.concatenate(cols, axis=1))
        dm_sc[...] = dm_acc
        dvn = jnp.concatenate(rows, axis=0)
        dlg_ref[...] += _row_sum(dvn * nh)
        dlb_ref[...] += _row_sum(dvn)
        dnh = dvn * lgv
        dgv = rstd * (dnh - jnp.mean(dnh, axis=-1, keepdims=True)
                      - nh * jnp.mean(dnh * nh, axis=-1, keepdims=True))
        duv_ref[:, :D_SGU] = du.astype(BF)
        duv_ref[:, D_SGU:] = (dgv * dgv_dv).astype(BF)

        @pl.when(step == pl.num_programs(0) - 1)
        def _():
            for g in range(SGU_GROUPS):
                dbs_ref[:, g:g + 1] = jnp.sum(
                    dm_acc[:, g * SGU_BLOCK:(g + 1) * SGU_BLOCK], axis=1, keepdims=True)

    sw = (SGU_GROUPS, SGU_BLOCK, SGU_BLOCK)
    body, more_specs, more = _with_after(body, 7, after)
    return pl.pallas_call(
        body, name="sgu_bwd", grid=(s // ts,),
        in_specs=[_tile_spec(ts, D_SGU), _tile_spec(ts, 2 * D_RNN, 1), _full_spec(sw), _full_spec(sw),
                  _full_spec((SGU_BLOCK, SGU_BLOCK)), _full_spec((1, D_SGU)), _full_spec((1, D_SGU))] + more_specs,
        out_specs=[_tile_spec(ts, 2 * D_SGU), _full_spec(sw), _full_spec((SGU_BLOCK, SGU_GROUPS)),
                   _full_spec((1, D_SGU)), _full_spec((1, D_SGU))],
        out_shape=[jax.ShapeDtypeStruct((s, 2 * D_SGU), BF), jax.ShapeDtypeStruct(sw, F32),
                   jax.ShapeDtypeStruct((SGU_BLOCK, SGU_GROUPS), F32),
                   jax.ShapeDtypeStruct((1, D_SGU), F32), jax.ShapeDtypeStruct((1, D_SGU), F32)],
        scratch_shapes=[pltpu.VMEM((SGU_BLOCK, D_SGU), F32)],
        compiler_params=_params(("arbitrary",)),
    )(dyb_pre, proj, wm, bsb, mask, lg, lb, *more)


_ROW_DBA, _ROW_DBX, _ROW_DSP, _ROW_DCB, _ROW_DCW = 0, 1, 2, 3, 4
_PREV_ROWS = 16


def _rnn_bwd_call(dya_pre, proj, xr_saved, hr, wa, wx, ba, bx, sp, cw, ts, after=None):
    s = proj.shape[0]
    nt = s // ts
    per = ts // _PREV_ROWS

    def tile(i):
        return nt - 1 - i

    def prev(i):
        return jnp.maximum(tile(i) * per - 1, 0)

    def body(dy_ref, xg_ref, xr_ref, hr_ref, hrp_ref, wa_ref, wx_ref, ba_ref, bx_ref, sp_ref,
             cw_ref, dxg_ref, dwa_ref, dwx_ref, vec_ref,
             lam_carry, a_first, dxr_head, al_sc, bl_sc, lam_sc):
        step = pl.program_id(0)

        @pl.when(step == 0)
        def _():
            dwa_ref[...] = jnp.zeros_like(dwa_ref)
            dwx_ref[...] = jnp.zeros_like(dwx_ref)
            vec_ref[...] = jnp.zeros_like(vec_ref)
            lam_carry[...] = jnp.zeros_like(lam_carry)
            a_first[...] = jnp.zeros_like(a_first)
            dxr_head[...] = jnp.zeros_like(dxr_head)

        has_prev = (step < nt - 1).astype(F32)
        x = xg_ref[:, :D_RNN].astype(F32)
        g = xg_ref[:, D_RNN:]
        h_tail =hrp_ref[_PREV_ROWS - SUBLANES:, :].astype(F32) * has_prev
        xr = xr_ref[...].astype(F32)
        r, i, a, nrm, inv_nrm = _lru_gates(xr, wa_ref, wx_ref, ba_ref, bx_ref, sp_ref)
        h = hr_ref[...].astype(F32)
        dy = dy_ref[...].astype(F32)
        gg, dgg = _gelu_and_grad(g)

        coef = _shift_up(a, jnp.broadcast_to(a_first[...], (SUBLANES, D_RNN)), 1)
        lam_carry[...] = _linear_scan(coef, dy * gg, lam_carry[...], al_sc, bl_sc, lam_sc, True)
        a_first[...] = a[0:1, :]
        lam = lam_sc[...]

        da = lam * _shift_down(h, h_tail, 1)
        dnrm = lam * (i * xr)
        di = lam * nrm * xr
        dlog_a = da * a - dnrm * (a * a) * inv_nrm
        spv = sp_ref[...]
        dza = (dlog_a * (-LRU_C * spv)) * (r * (1.0 - r))
        dzx = di * (i * (1.0 - i))
        vec_ref[_ROW_DSP:_ROW_DSP + 1, :] += _row_sum(dlog_a * (-LRU_C * r))
        vec_ref[_ROW_DBA:_ROW_DBA + 1, :] += _row_sum(dza)
        vec_ref[_ROW_DBX:_ROW_DBX + 1, :] += _row_sum(dzx)
        xb = xr.astype(BF)
        dza_bf = dza.astype(BF)
        dzx_bf = dzx.astype(BF)
        for grp in range(N_LRU_GROUPS):
            cols = slice(grp * LRU_GROUP, (grp + 1) * LRU_GROUP)
            dwa_ref[grp] += _dot_tn(xb[:, cols], dza_bf[:, cols])
            dwx_ref[grp] += _dot_tn(xb[:, cols], dzx_bf[:, cols])
        dxr = (lam * nrm * i + _group_dot(dza_bf, wa_ref, _dot_nt) + _group_dot(dzx_bf, wx_ref, _dot_nt))

        vec_ref[_ROW_DCB:_ROW_DCB + 1, :] += _row_sum(dxr)
        head = dxr_head[...]
        dx = cw_ref[CONV_WIDTH - 1:CONV_WIDTH, :] * dxr
        vec_ref[_ROW_DCW + 3:_ROW_DCW + 4, :] += _row_sum(dxr * x)
        for sft in range(1, CONV_WIDTH):
            k = CONV_WIDTH - 1 - sft
            ahead = _shift_up(dxr, head, sft)
            dx = dx + cw_ref[k:k + 1, :] * ahead
            vec_ref[_ROW_DCW + k:_ROW_DCW + k + 1, :] += _row_sum(ahead * x)
        dxr_head[...] = dxr[0:SUBLANES, :]
        dxg_ref[:, :D_RNN] = dx.astype(BF)
        dxg_ref[:, D_RNN:] = (dy * h * dgg).astype(BF)

    gw = (N_LRU_GROUPS, LRU_GROUP, LRU_GROUP)
    rev = lambda width: pl.BlockSpec((ts, width), lambda i: (tile(i), 0))
    body, more_specs, more = _with_after(body, 11, after)
    return pl.pallas_call(
        body, name="rnn_bwd", grid=(nt,),
        in_specs=[rev(D_RNN), rev(2 * D_RNN), rev(D_RNN), rev(D_RNN),
                  pl.BlockSpec((_PREV_ROWS, D_RNN), lambda i: (prev(i), 0)),
                  _full_spec(gw), _full_spec(gw),
                  _full_spec((1, D_RNN)), _full_spec((1, D_RNN)), _full_spec((1, D_RNN)),
                  _full_spec((CONV_WIDTH, D_RNN))] + more_specs,
        out_specs=[rev(2 * D_RNN), _full_spec(gw), _full_spec(gw), _full_spec((SUBLANES, D_RNN))],
        out_shape=[jax.ShapeDtypeStruct((s, 2 * D_RNN), BF), jax.ShapeDtypeStruct(gw, F32),
                   jax.ShapeDtypeStruct(gw, F32), jax.ShapeDtypeStruct((SUBLANES, D_RNN), F32)],
        scratch_shapes=[pltpu.VMEM((1, D_RNN), F32), pltpu.VMEM((1, D_RNN), F32),
                        pltpu.VMEM((SUBLANES, D_RNN), F32),
                        pltpu.VMEM((ts, D_RNN), F32), pltpu.VMEM((ts, D_RNN), F32),
                        pltpu.VMEM((ts, D_RNN), F32)],
        compiler_params=_params(("arbitrary",)),
    )(dya_pre, proj, xr_saved, hr, hr, wa, wx, ba, bx, sp, cw, *more)


def _inproj_bwd_call(dxg, duv, dgate, dx1, x, g1, w_in, layer, ts):
    s = x.shape[0]

    def body(dxg_ref, duv_ref, dgt_ref, dx1_ref, x_ref, g_ref, w_ref, dx_ref, dproj_ref, dg_ref):
        @pl.when(pl.program_id(0) == 0)
        def _():
            dg_ref[...] = jnp.zeros_like(dg_ref)

        dproj = jnp.concatenate([dxg_ref[...], duv_ref[...], dgt_ref[...]], axis=1)
        dproj_ref[...] = dproj
        dh = jnp.zeros((ts, D_MODEL), F32)
        for q in range(N_QUARTERS):
            dh = dh + _dot_nt(dproj[:, q * Q_IN:(q + 1) * Q_IN], w_ref[q])
        dx, dg = _rms_bwd(dh, x_ref[...], g_ref[...])
        dx_ref[...] = dx1_ref[...] + dx
        dg_ref[...] += _row_sum(dg)

    return pl.pallas_call(
        body, name="inproj_bwd", grid=(s // ts,),
        in_specs=[_tile_spec(ts, 2 * D_RNN), _tile_spec(ts, 2 * D_SGU), _tile_spec(ts, 2 * D_MODEL),
                  _tile_spec(ts, D_MODEL), _tile_spec(ts, D_MODEL), _full_spec((1, D_MODEL)),
                  pl.BlockSpec((None, N_QUARTERS, D_MODEL, Q_IN), lambda i: (layer, 0, 0, 0))],
        out_specs=[_tile_spec(ts, D_MODEL), _tile_spec(ts, D_IN), _full_spec((1, D_MODEL))],
        out_shape=[jax.ShapeDtypeStruct((s, D_MODEL), F32), jax.ShapeDtypeStruct((s, D_IN), BF),
                   jax.ShapeDtypeStruct((1, D_MODEL), F32)],
        compiler_params=_params(("arbitrary",)),
    )(dxg, duv, dgate, dx1, x, g1, w_in)


def _relu_sq(p):
    return jnp.square(jnp.maximum(p, 0))


def _wgrad_call(a, b, core, tm, tn, tk, col_blocked, name, a_fn=None):
    s, m = a.shape
    n = b.shape[1]
    r, cols = (m, n // N_QUARTERS) if col_blocked else (m // N_QUARTERS, n)
    r2 = r // 2
    per_tile = tm // r
    steps = s // tk
    assert per_tile > 0 or steps == 1

    def body(core_ref, a_ref, b_ref, keep_ref, send_ref, *acc):
        av = a_ref[...]
        if a_fn is not None:
            av = a_fn(av)
        prod = _dot_tn(av.astype(BF), b_ref[...].astype(BF))

        def emit(total):
            for h in range(2):
                @pl.when(core_ref[0] == h)
                def _():
                    for q in range(per_tile):
                        keep_ref[q] = total[q * r + h * r2:q * r + (h + 1) * r2].astype(BF)
                        send_ref[q] = total[q * r + (1 - h) * r2:q * r + (2 - h) * r2].astype(BF)

        if per_tile == 0:
            mine = pl.program_id(1) == core_ref[0]

            @pl.when(mine)
            def _():
                keep_ref[0] = prod.astype(BF)

            @pl.when(jnp.logical_not(mine))
            def _():
                send_ref[0] = prod.astype(BF)
        elif steps == 1:
            emit(prod)
        else:
            acc_ref, = acc
            step = pl.program_id(2)

            @pl.when(step == 0)
            def _():
                acc_ref[...] = prod

            @pl.when(jnp.logical_and(step > 0, step < steps - 1))
            def _():
                acc_ref[...] += prod

            @pl.when(step == steps - 1)
            def _():
                emit(acc_ref[...] + prod)

    if col_blocked:
        per_q = cols // tn
        out_spec = pl.BlockSpec((1, r2, tn), lambda j, i, k, c: (j // per_q, 0, j % per_q))
    else:
        out_spec = pl.BlockSpec((per_tile, r2, tn), lambda j, i, k, c: (i, 0, j))
    return pl.pallas_call(
        body, name=name,
        out_shape=[jax.ShapeDtypeStruct((N_QUARTERS, r2, cols), BF)] * 2,
        grid_spec=pltpu.PrefetchScalarGridSpec(
            num_scalar_prefetch=1, grid=(n // tn, m // tm, steps),
            in_specs=[pl.BlockSpec((tk, tm), lambda j, i, k, c: (k, i)),
                      pl.BlockSpec((tk, tn), lambda j, i, k, c: (k, j))],
            out_specs=[out_spec, out_spec],
            scratch_shapes=[] if steps == 1 else [pltpu.VMEM((tm, tn), F32)]),
        compiler_params=_params(("parallel", "parallel", "arbitrary")),
    )(core, a, b)


BIG = ("w_in", "w_up", "w_down", "w_branch_a", "w_branch_b", "w_out")


def _block_diag(w):
    w4 = w.reshape(N_LRU_GROUPS, HEADS_PER_GROUP, RNN_HEAD_DIM, RNN_HEAD_DIM)
    eye = jnp.eye(HEADS_PER_GROUP, dtype=w.dtype)
    return jnp.einsum("gjio,jk->gjiko", w4, eye).reshape(N_LRU_GROUPS, LRU_GROUP, LRU_GROUP)


def _block_diag_extract(d):
    d5 = d.reshape(N_LRU_GROUPS, HEADS_PER_GROUP, RNN_HEAD_DIM, HEADS_PER_GROUP, RNN_HEAD_DIM)
    blocks = [d5[:, j, :, j, :] for j in range(HEADS_PER_GROUP)]
    return jnp.stack(blocks, axis=1).reshape(RNN_HEADS, RNN_HEAD_DIM, RNN_HEAD_DIM)


def _sgu_mask():
    chunk = jnp.arange(SGU_BLOCK) // CHUNK
    return (chunk[:, None] >= chunk[None, :]).astype(F32)


def _layer_small(sm, l, core):
    row = lambda v: v.reshape(1, -1)
    return dict(
        core=core,
        g1=row(sm["norm_mix_g"][l]), g2=row(sm["norm_ffn_g"][l]),
        wa=_block_diag(sm["lru_w_a"][l]).astype(BF), wx=_block_diag(sm["lru_w_x"][l]).astype(BF),
        ba=row(sm["lru_b_a"][l]), bx=row(sm["lru_b_x"][l]),
        sp=row(jax.nn.softplus(-sm["lru_lambda"][l])),
        cw=sm["conv_w"][l] if "conv_w" in sm else None, cb=row(sm["conv_b"][l]),
        wm=(sm["sgu_w_s"][l] * _sgu_mask()).astype(BF),
        bsb=jnp.broadcast_to(sm["sgu_b_s"][l][:, :, None], (SGU_GROUPS, SGU_BLOCK, SGU_BLOCK)),
        lg=row(sm["sgu_ln_g"][l]), lb=row(sm["sgu_ln_b"][l]),
    )


def _layer_fwd_mix(x, big, p, ts, h=None, before_sgu=None, proj=None):
    if h is None:
        h = _norm_call(x, p["g1"], ts)
    if proj is None:
        proj = _inproj_call(h, big["w_in"], 0, 2 * ts)
    xr, hr, ya_pre = _rnn_fwd_call(proj, p["wa"], p["wx"], p["ba"], p["bx"], p["sp"], p["cw"], p["cb"], ts)
    yb_pre = _sgu_fwd_call(proj, p["wm"], p["bsb"], p["lg"], p["lb"], ts,
                           None if before_sgu is None else before_sgu(ya_pre))
    return dict(p=p, x=x, h=h, proj=proj, xr=xr, hr=hr, ya_pre=ya_pre, yb_pre=yb_pre)


def _layer_fwd_out(sv, big, ts):
    x1, ya, yb, merged, h2 = _merge_call(sv["x"], sv["proj"], sv["ya_pre"], sv["yb_pre"], big["w_branch_a"],
                                         big["w_branch_b"], big["w_out"], sv["p"]["g2"], 0, ts)
    x2, pre = _ffn_call(x1, h2, big["w_up"], big["w_down"], 0, ts)
    sv.update(x1=x1, ya=ya, yb=yb, merged=merged, h2=h2, pre=pre)
    return x2


def _layer_bwd_ffn(dx, sv, big, ts, after=None):
    p = sv["p"]
    dx1, dpre, dg2, dx_bf, sv["dx1_bf"] = _ffn_bwd_call(dx, sv["pre"], sv["x1"], p["g2"], big["w_up"],
                                                       big["w_down"], 0, ts, after)
    tk = dx.shape[0]
    gb = dict(
        w_down=_wgrad_call(sv["pre"], dx_bf, p["core"], Q_FF, D_MODEL, tk, False, "wgrad_down", a_fn=_relu_sq),
        w_up=_wgrad_call(sv["h2"], dpre, p["core"], D_MODEL, Q_FF, tk, True, "wgrad_up"))
    return dx1, gb, dict(norm_ffn_g=dg2[0])


def _layer_bwd_merge(dx1, sv, big, ts, after=None):
    tk = dx1.shape[0]
    core = sv["p"]["core"]
    dya, dyb, dgate, dya_pre, dyb_pre = _merge_bwd_call(
        dx1, sv["proj"], sv["ya"], sv["yb"], big["w_branch_a"], big["w_branch_b"], big["w_out"], 0, ts, after)
    gb = dict(
        w_out=_wgrad_call(sv["merged"], sv["dx1_bf"], core, D_MODEL, D_MODEL, tk, False, "wgrad_out"),
        w_branch_a=_wgrad_call(sv["ya_pre"], dya, core, D_RNN, D_MODEL // 2, tk, False, "wgrad_branch_a"),
        w_branch_b=_wgrad_call(sv["yb_pre"], dyb, core, D_SGU, D_MODEL, tk, False, "wgrad_branch_b"))
    return (dgate, dya_pre, dyb_pre), gb


def _layer_bwd_branches(dx1, merge_out, sv, big, lam, ts, after=None, after_sgu=None):
    p = sv["p"]
    tk = dx1.shape[0]
    dgate, dya_pre, dyb_pre = merge_out
    gb = {}
    duv, dws, dbs, dlg, dlb = _sgu_bwd_call(dyb_pre, sv["proj"], p["wm"], p["bsb"], _sgu_mask(), p["lg"], p["lb"],
                                            ts, after)
    dxg, dwa, dwx, vec = _rnn_bwd_call(dya_pre, sv["proj"], sv["xr"], sv["hr"], p["wa"], p["wx"], p["ba"], p["bx"],
                                       p["sp"], p["cw"], ts, None if after_sgu is None else after_sgu(duv))
    dx, dproj, dg1 = _inproj_bwd_call(dxg, duv, dgate, dx1, sv["x"], p["g1"], big["w_in"], 0, ts)
    gb["w_in"] = _wgrad_call(sv["h"], dproj, p["core"], D_MODEL // 2, Q_IN, tk, True, "wgrad_in")
    gs = dict(
        norm_mix_g=dg1[0], conv_w=vec[_ROW_DCW:_ROW_DCW + CONV_WIDTH], conv_b=vec[_ROW_DCB],
        lru_w_a=_block_diag_extract(dwa), lru_w_x=_block_diag_extract(dwx),
        lru_b_a=vec[_ROW_DBA].reshape(RNN_HEADS, RNN_HEAD_DIM), lru_b_x=vec[_ROW_DBX].reshape(RNN_HEADS, RNN_HEAD_DIM),
        lru_lambda=-vec[_ROW_DSP] * jax.nn.sigmoid(-lam),
        sgu_ln_g=dlg[0], sgu_ln_b=dlb[0], sgu_w_s=dws, sgu_b_s=dbs.T)
    return dx, gb, gs


def _local_step(x, target, big, sm, ts):
    saved = []
    core = jnp.zeros((1,), jnp.int32)
    for l in range(DEPTH):
        sv = _layer_fwd_mix(x, big[l], _layer_small(sm, l, core), ts)
        x = _layer_fwd_out(sv, big[l], ts)
        saved.append(sv)
    dx, loss, dgf = _loss_call(x, target, sm["final_norm_g"].reshape(1, -1), ts)
    gb, gs = [None] * DEPTH, [None] * DEPTH
    for l in reversed(range(DEPTH)):
        dx1, gb_ffn, gs_ffn = _layer_bwd_ffn(dx, saved[l], big[l], ts)
        merge_out, gb_merge = _layer_bwd_merge(dx1, saved[l], big[l], ts)
        dx, gb_mix, gs_mix = _layer_bwd_branches(dx1, merge_out, saved[l], big[l], sm["lru_lambda"][l], ts)
        gb[l] = {**gb_ffn, **gb_merge, **gb_mix}
        gs[l] = {**gs_ffn, **gs_mix}
    gs = {k: jnp.stack([g[k] for g in gs]) for k in gs[0]}
    gs["final_norm_g"] = dgf[0]
    return loss, dx, gb, gs


EW_VMEM_BYTES = 24 * 1024 * 1024


def _row_block(rows, cols, bytes_per_elem):
    for br in range(min(rows, EW_VMEM_BYTES // (2 * bytes_per_elem * cols)), 0, -1):
        if rows % br == 0 and br % 16 == 0:
            return br
    return rows


def _ew_call(fn, name, operands, outputs, slabs=1, sel=None, into=None, after=None):
    if into is not None and not isinstance(into, (list, tuple)):
        into = [into]
    rows, cols = outputs[0][0].shape[2:]
    br = _row_block(rows, cols, sum(jnp.dtype(a.dtype).itemsize for a, _ in operands + outputs))
    n_in = len(operands)

    def pick(tok, g, s):
        if callable(tok):
            return tok(g, s)
        if tok == "g":
            return g
        if isinstance(tok, tuple):
            return s[tok[1]]
        return tok

    def spec(idx):
        return pl.BlockSpec((None, None, br, cols),
                            lambda g, i, s, idx=idx: (pick(idx[0], g, s), pick(idx[1], g, s), i, 0))

    if sel is None:
        sel = jnp.zeros((1,), jnp.int32)
    in_specs = [spec(idx) for _, idx in operands]
    arrays = [a for a, _ in operands]
    aliases = {}
    for j, buf in enumerate(into or ()):
        in_specs.append(pl.BlockSpec(memory_space=pl.ANY))
        arrays.append(buf)
        aliases[1 + n_in + j] = j
    if after is not None:
        in_specs.append(pl.BlockSpec(memory_space=pl.ANY))
        arrays.append(after)

    def body(sel_ref, *refs):
        outs = fn(*[r[...] for r in refs[:n_in]])
        for o_ref, o in zip(refs[len(arrays):], outs):
            o_ref[...] = o.astype(o_ref.dtype)

    return pl.pallas_call(
        body, name=name, out_shape=[s for s, _ in outputs],
        grid_spec=pltpu.PrefetchScalarGridSpec(
            num_scalar_prefetch=1, grid=(slabs, rows // br),
            in_specs=in_specs,
            out_specs=[spec(idx) for _, idx in outputs]),
        input_output_aliases=aliases,
        compiler_params=_params(("parallel", "parallel")),
    )(sel, *arrays)


def _as4(a):
    return a.reshape((1,) * (4 - a.ndim) + a.shape)


def _adamw(w, g, m, v):
    m = ADAM_B1 * m + (1.0 - ADAM_B1) * g
    v = ADAM_B2 * v + (1.0 - ADAM_B2) * jnp.square(g)
    m_hat = m / (1.0 - ADAM_B1 ** ADAM_STEP)
    v_hat = v / (1.0 - ADAM_B2 ** ADAM_STEP)
    delta = -ADAM_LR * (m_hat / (jnp.sqrt(v_hat) + ADAM_EPS) + ADAM_WD * w)
    return delta, m, v


def _small_adamw_call(ws, gs, ms, vs):
    n = len(ws)

    def body(*refs):
        for k in range(n):
            w, g, m, v = (refs[j * n + k][...] for j in range(4))
            outs = _adamw(w, g, m, v)
            for j in range(3):
                refs[(4 + j) * n + k][...] = outs[j]

    shapes = [jax.ShapeDtypeStruct(w.shape, F32) for w in ws]
    outs = pl.pallas_call(
        body, name="adamw_small", out_shape=shapes * 3,
        in_specs=[pl.BlockSpec(memory_space=pltpu.VMEM)] * (4 * n),
        out_specs=[pl.BlockSpec(memory_space=pltpu.VMEM)] * (3 * n),
        compiler_params=_params(),
    )(*ws, *gs, *ms, *vs)
    return outs[:n], outs[n:2 * n], outs[2 * n:]


ANY = pl.BlockSpec(memory_space=pl.ANY)


def _place():
    x, y, c = lax.axis_index("x"), lax.axis_index("y"), lax.axis_index("c")
    chips = [(1 - x, y), (x, 1 - y), (1 - x, 1 - y)]
    return x, y, c, chips


def _remote(src, dst, send_sem, recv_sem, to):
    return pltpu.make_async_remote_copy(src_ref=src, dst_ref=dst, send_sem=send_sem, recv_sem=recv_sem,
                                        device_id=to, device_id_type=MESH)


def _sibling_send_call(items):
    n = len(items)

    def body(*refs):
        src, out = refs[:n], refs[n:2 * n]
        send_sems, recv_sems = refs[2 * n:]
        x, y, c, _ = _place()
        copies = [_remote(src[w], out[w], send_sems.at[w], recv_sems.at[w], (x, y, 1 - c)) for w in range(n)]
        for cp in copies:
            cp.start()
        for cp in copies:
            cp.wait()

    return pl.pallas_call(
        body, name="grads_to_sibling",
        out_shape=[jax.ShapeDtypeStruct(a.shape, a.dtype) for a in items],
        in_specs=[ANY] * n, out_specs=[ANY] * n,
        scratch_shapes=[pltpu.SemaphoreType.DMA((n,)), pltpu.SemaphoreType.DMA((n,))],
        compiler_params=_params(vmem=False, has_side_effects=True),
    )(*items)


def _sibling_inplace_call(name, bufs, slabs, n_pairs):
    n = len(bufs)

    def body(*refs):
        out = refs[n:2 * n]
        send_sems, recv_sems = refs[2 * n:]
        x, y, c, _ = _place()
        sibling = (x, y, 1 - c)
        pairs = [pair for w, ref in enumerate(out) for pair in slabs(ref, c, w)]
        sends = [_remote(s, s, send_sems.at[k], recv_sems.at[k], sibling) for k, (s, _) in enumerate(pairs)]
        for cp in sends:
            cp.start()
        for k, (_, r) in enumerate(pairs):
            _remote(r, r, send_sems.at[k], recv_sems.at[k], sibling).wait_recv()
        for cp in sends:
            cp.wait_send()

    return pl.pallas_call(
        body, name=name,
        out_shape=[jax.ShapeDtypeStruct(a.shape, a.dtype) for a in bufs],
        in_specs=[ANY] * n, out_specs=[ANY] * n,
        input_output_aliases={w: w for w in range(n)},
        scratch_shapes=[pltpu.SemaphoreType.DMA((n_pairs,)), pltpu.SemaphoreType.DMA((n_pairs,))],
        compiler_params=_params(vmem=False, has_side_effects=True),
    )(*bufs)


HBM_SPEC = pl.BlockSpec(memory_space=pltpu.HBM)
SEM_SPEC = pl.BlockSpec(memory_space=pltpu.SEMAPHORE)
DATAFLOW_EFFECT = pltpu.SideEffectType.DATAFLOW_SIDE_EFFECTING


def _exchange_start(name, bufs, copies, n_copies, after):
    n = len(bufs)

    def body(*refs):
        ins, send_sems, recv_sems, token = refs[:n], refs[n + 1], refs[n + 2], refs[-1]
        for k, (src, dst, to) in enumerate(copies(ins)):
            _remote(src, dst, send_sems.at[k], recv_sems.at[k], to).start()
        token[...] = jnp.zeros_like(token)

    outs = pl.pallas_call(
        body, name=name,
        out_shape=(pltpu.SemaphoreType.DMA((n_copies,)), pltpu.SemaphoreType.DMA((n_copies,)),
                   *[pltpu.HBM(b.shape, b.dtype) for b in bufs], jax.ShapeDtypeStruct((SUBLANES, 128), F32)),
        in_specs=[HBM_SPEC] * n + [ANY],
        out_specs=(SEM_SPEC, SEM_SPEC, *[HBM_SPEC] * n, pl.BlockSpec(memory_space=pltpu.VMEM)),
        input_output_aliases={w: w + 2 for w in range(n)},
        compiler_params=pltpu.CompilerParams(has_side_effects=DATAFLOW_EFFECT),
    )(*[pltpu.with_memory_space_constraint(b, pltpu.HBM) for b in bufs], after)
    return outs[0], outs[1], list(outs[2:2 + n]), outs[-1]


def _exchange_wait(name, send_sems, recv_sems, bufs, copies, after):
    n = len(bufs)

    def body(*refs):
        ins, send_sems, recv_sems = refs[:n], refs[n], refs[n + 1]
        for k, (src, dst, to) in enumerate(copies(ins)):
            cp = _remote(src, dst, send_sems.at[k], recv_sems.at[k], to)
            cp.wait_send()
            cp.wait_recv()

    return pl.pallas_call(
        body, name=name,
        out_shape=[pltpu.HBM(b.shape, b.dtype) for b in bufs],
        in_specs=[HBM_SPEC] * n + [SEM_SPEC, SEM_SPEC, ANY],
        out_specs=[HBM_SPEC] * n,
        input_output_aliases={w: w for w in range(n)},
        compiler_params=pltpu.CompilerParams(has_side_effects=DATAFLOW_EFFECT),
    )(*bufs, send_sems, recv_sems, after)


def _gather_copies(refs):
    x, y, c, chips = _place()
    mine = 2 * (2 * x + y) + c
    return [(ref.at[mine], ref.at[mine], (qx, qy, c)) for ref in refs for qx, qy in chips]


def _forward_copies(refs):
    x, y, c, chips = _place()
    return [(ref.at[2 * (2 * qx + qy) + c], ref.at[2 * (2 * qx + qy) + c], (x, y, 1 - c))
            for ref in refs for qx, qy in chips]


def _gather_forward_slabs(ref, c, w):
    x, y, _, chips = _place()
    return [(ref.at[2 * (2 * qx + qy) + c], ref.at[2 * (2 * qx + qy) + 1 - c]) for qx, qy in chips]


def _device_peers():
    x, y, c, _ = _place()
    return 4 * x + 2 * y + c, [(k, (x ^ ((k >> 2) & 1), y ^ ((k >> 1) & 1), c ^ (k & 1))) for k in range(1, 8)]


def _small_scatter_copies(refs):
    me, peers = _device_peers()
    return [(refs[0].at[me ^ k], refs[1].at[me], to) for k, to in peers]


def _small_spread_copies(refs):
    me, peers = _device_peers()
    return [(refs[0].at[me], refs[0].at[me], to) for _, to in peers]


def _sibling_copies(refs):
    n = len(refs) // 2
    x, y, c, _ = _place()
    return [(refs[w], refs[n + w], (x, y, 1 - c)) for w in range(n)]


def _owner_copies(refs):
    n = len(refs) // 2
    x, y, c, chips = _place()
    return [(refs[w].at[2 * qx + qy], refs[n + w].at[j], (qx, qy, c))
            for w in range(n) for j, (qx, qy) in enumerate(chips)]


N_DEVICES = 8
SMALL_ROWS = 616


SMALL = ("norm_mix_g", "conv_w", "conv_b", "lru_w_a", "lru_b_a", "lru_w_x", "lru_b_x", "lru_lambda",
         "sgu_ln_g", "sgu_ln_b", "sgu_w_s", "sgu_b_s", "norm_ffn_g", "final_norm_g")
WEIGHTS = ("norm_mix_g", "w_in", "conv_w", "conv_b", "lru_w_a", "lru_b_a", "lru_w_x", "lru_b_x", "lru_lambda",
           "sgu_ln_g", "sgu_ln_b", "sgu_w_s", "sgu_b_s", "w_branch_a", "w_branch_b", "w_out", "norm_ffn_g",
           "w_up", "w_down", "final_norm_g")
PACK_ALIGN = SUBLANES * 128


PACKED = SMALL + ("loss",)


def _pack_small(gs):
    parts = []
    for k in PACKED:
        flat = gs[k].reshape(-1)
        parts.append(jnp.pad(flat, (0, -flat.size % PACK_ALIGN)))
    flat = jnp.concatenate(parts)
    flat = jnp.pad(flat, (0, N_DEVICES * SMALL_ROWS * 128 - flat.size))
    return flat.reshape(N_DEVICES, SMALL_ROWS, 128)


def _unpack_small(buf, like):
    flat = buf.reshape(-1)
    out, off = {}, 0
    for k in PACKED:
        size = like[k].size
        out[k] = flat[off:off + size].reshape(like[k].shape)
        off += size + (-size % PACK_ALIGN)
    return out


def _as_rows(a):
    return a.reshape(-1, a.shape[-1])


def kernel(x, norm_mix_g, w_in, conv_w, conv_b, lru_w_a, lru_b_a, lru_w_x, lru_b_x, lru_lambda, sgu_ln_g, sgu_ln_b, sgu_w_s, sgu_b_s, w_branch_a, w_branch_b, w_out, norm_ffn_g, w_up, w_down, final_norm_g, loss_target, m_norm_mix_g, m_w_in, m_conv_w, m_conv_b, m_lru_w_a, m_lru_b_a, m_lru_w_x, m_lru_b_x, m_lru_lambda, m_sgu_ln_g, m_sgu_ln_b, m_sgu_w_s, m_sgu_b_s, m_w_branch_a, m_w_branch_b, m_w_out, m_norm_ffn_g, m_w_up, m_w_down, m_final_norm_g, v_norm_mix_g, v_w_in, v_conv_w, v_conv_b, v_lru_w_a, v_lru_b_a, v_lru_w_x, v_lru_b_x, v_lru_lambda, v_sgu_ln_g, v_sgu_ln_b, v_sgu_w_s, v_sgu_b_s, v_w_branch_a, v_w_branch_b, v_w_out, v_norm_ffn_g, v_w_up, v_w_down, v_final_norm_g):
    w = dict(norm_mix_g=norm_mix_g, w_in=w_in, conv_w=conv_w, conv_b=conv_b, lru_w_a=lru_w_a, lru_b_a=lru_b_a,
             lru_w_x=lru_w_x, lru_b_x=lru_b_x, lru_lambda=lru_lambda, sgu_ln_g=sgu_ln_g, sgu_ln_b=sgu_ln_b,
             sgu_w_s=sgu_w_s, sgu_b_s=sgu_b_s, w_branch_a=w_branch_a, w_branch_b=w_branch_b, w_out=w_out,
             norm_ffn_g=norm_ffn_g, w_up=w_up, w_down=w_down, final_norm_g=final_norm_g)
    m = dict(norm_mix_g=m_norm_mix_g, w_in=m_w_in, conv_w=m_conv_w, conv_b=m_conv_b, lru_w_a=m_lru_w_a,
             lru_b_a=m_lru_b_a, lru_w_x=m_lru_w_x, lru_b_x=m_lru_b_x, lru_lambda=m_lru_lambda,
             sgu_ln_g=m_sgu_ln_g, sgu_ln_b=m_sgu_ln_b, sgu_w_s=m_sgu_w_s, sgu_b_s=m_sgu_b_s,
             w_branch_a=m_w_branch_a, w_branch_b=m_w_branch_b, w_out=m_w_out, norm_ffn_g=m_norm_ffn_g,
             w_up=m_w_up, w_down=m_w_down, final_norm_g=m_final_norm_g)
    v = dict(norm_mix_g=v_norm_mix_g, w_in=v_w_in, conv_w=v_conv_w, conv_b=v_conv_b, lru_w_a=v_lru_w_a,
             lru_b_a=v_lru_b_a, lru_w_x=v_lru_w_x, lru_b_x=v_lru_b_x, lru_lambda=v_lru_lambda,
             sgu_ln_g=v_sgu_ln_g, sgu_ln_b=v_sgu_ln_b, sgu_w_s=v_sgu_w_s, sgu_b_s=v_sgu_b_s,
             w_branch_a=v_w_branch_a, w_branch_b=v_w_branch_b, w_out=v_w_out, norm_ffn_g=v_norm_ffn_g,
             w_up=v_w_up, w_down=v_w_down, final_norm_g=v_final_norm_g)
    core = lax.axis_index("c")
    chip = 2 * lax.axis_index("x") + lax.axis_index("y")
    sel = jnp.stack([core, 1 - core, chip, 2 * chip + core]).astype(jnp.int32)
    this_core, this_chip = ("sel", 0), ("sel", 2)
    sds = jax.ShapeDtypeStruct

    ts = TOKEN_TILE

    def after_all(arrays):
        return jnp.stack([a[(0,) * a.ndim].astype(F32) for a in arrays])

    halves = {k: (w[k].shape[1] // 2, w[k].shape[2]) for k in BIG}

    def half_view(k, a):
        return a.reshape((2 * N_QUARTERS,) + halves[k])

    def full_view(k, a):
        if k == "conv_w":
            return a.reshape(N_QUARTERS, DEPTH, CONV_WIDTH, -1).transpose(1, 2, 0, 3).reshape(DEPTH, CONV_WIDTH, D_RNN)
        r2, cols = halves[k]
        if k in ("w_in", "w_up"):
            return a.reshape(1, N_QUARTERS, 2 * r2, cols)
        return a.reshape(1, 2 * N_QUARTERS * r2, cols)

    layer_bufs = [{}, {}]

    def cast_weights(k, after):
        _, r, cols = w[k].shape
        w4 = w[k].reshape(DEPTH, 1, r, cols)
        outs = _ew_call(lambda a, b: (a, b), "cast_weights", [(w4, (0, 0)), (w4, (1, 0))],
                        [(sds((1, N_QUARTERS, r, cols), BF), (0, this_chip))] * DEPTH, 1, sel, after=after)
        for l in range(DEPTH):
            layer_bufs[l][k] = half_view(k, outs[l])

    conv_buf = lax.dynamic_update_slice_in_dim(
        jnp.zeros((N_QUARTERS, DEPTH) + conv_w.shape[1:], F32), conv_w[None], chip, axis=0)
    layer_bufs[0]["conv_w"] = conv_buf.reshape((2 * N_QUARTERS,) + conv_w.shape[1:])
    sm = {k: w[k] for k in SMALL if k != "conv_w"}

    def gather_start(tag, l, keys, after):
        bufs = [layer_bufs[l][k] for k in keys]
        return _exchange_start(f"gather_start_{tag}", bufs, _gather_copies, 3 * len(keys), after)

    def gather_finish(tag, keys, started, after):
        send_sems, recv_sems, thru, _ = started
        landed = _exchange_wait(f"gather_wait_{tag}", send_sems, recv_sems, thru, _gather_copies, after)
        landed = _sibling_inplace_call("gather_forward", landed, _gather_forward_slabs, 3 * len(keys))
        return {k: full_view(k, a) for k, a in zip(keys, landed)}

    first, rest = ("w_in",), tuple(k for k in BIG if k != "w_in")
    cast_weights("w_in", None)
    started_a = gather_start("0a", 0, first + ("conv_w",), sel)
    for k in rest:
        cast_weights(k, started_a[3])
    started_b = gather_start("0b", 0, rest, started_a[3])
    started_c = gather_start("1a", 1, first, started_b[3])
    started_d = gather_start("1b", 1, rest, started_c[3])

    def arrives(tag, keys, started):
        state = {}

        def hook(after):
            landed = _exchange_wait(f"gather_wait_{tag}", started[0], started[1], started[2], _gather_copies, after)
            state["forward"] = _exchange_start(f"forward_start_{tag}", landed, _forward_copies, 3 * len(keys), after)
            return state["forward"][3]

        def finish(after):
            send_sems, recv_sems, thru, _ = state["forward"]
            done = _exchange_wait(f"forward_wait_{tag}", send_sems, recv_sems, thru, _forward_copies, after)
            return {k: full_view(k, a) for k, a in zip(keys, done)}

        return hook, finish

    p0, p1 = _layer_small(sm, 0, sel[0:1]), _layer_small(sm, 1, sel[0:1])
    h0 = _norm_call(x[0], p0["g1"], ts)
    proj_own = _inproj_part_call(h0, full_view("w_in", started_a[2][0]), 2 * ts, sel[2:3], 0, 1)
    ready = after_all([started_d[3], proj_own] + [p[k] for p in (p0, p1) for k in ("wa", "wx", "wm")])
    big0 = gather_finish("0a", first + ("conv_w",), started_a, ready)
    for l, p in enumerate((p0, p1)):
        p["cw"] = big0["conv_w"][l]
    proj0 = _inproj_part_call(h0, big0["w_in"], 2 * ts, sel[2:3], 1, N_QUARTERS - 1, proj_own)
    hook, finish = arrives("0b", rest, started_b)
    sv0 = _layer_fwd_mix(x[0], big0, p0, ts, h0, hook, proj0)
    big0.update(finish(sv0["yb_pre"]))
    x_mid = _layer_fwd_out(sv0, big0, ts)
    hook, finish = arrives("1a", first, started_c)
    h1 = _norm_call(x_mid, p1["g1"], ts, hook(x_mid))
    big1 = finish(h1)
    hook, finish = arrives("1b", rest, started_d)
    sv1 = _layer_fwd_mix(x_mid, big1, p1, ts, h1, hook)
    big1.update(finish(sv1["yb_pre"]))
    x_out = _layer_fwd_out(sv1, big1, ts)
    dx, loss, dgf = _loss_call(x_out, loss_target[0], final_norm_g.reshape(1, -1), ts)

    def pair_start(tag, gb, after):
        sends = [gb[k][1] for k in gb]
        zones = [lax.empty(a.shape, BF) for a in sends]
        return _exchange_start(f"pair_start_{tag}", sends + zones, _sibling_copies, len(sends), after)

    def reduce_start(tag, gb, after, pair=None):
        keys = tuple(gb)
        if pair is None:
            from_sibling = _sibling_send_call([gb[k][1] for k in keys])
        else:
            done = _exchange_wait(f"pair_wait_{tag}", pair[0], pair[1], pair[2], _sibling_copies, after)
            from_sibling = done[len(keys):]
        sums = [
            _ew_call(lambda a, b: (a.astype(F32) + b.astype(F32),), "pair_sum", [(gb[k][0][None], (0, "g")), (r[None], (0, "g"))],
                     [(sds((1,) + r.shape, BF), (0, "g"))], N_QUARTERS)[0][0]
            for k, r in zip(keys, from_sibling)]
        zones = [lax.empty((3,) + a.shape[1:], BF) for a in sums]
        started = _exchange_start(f"reduce_start_{tag}", sums + zones, _owner_copies, 3 * len(keys), after)
        return keys, started

    def reduce_finish(tag, l, keys_started, after, reduced):
        keys, (send_sems, recv_sems, thru, _) = keys_started
        done = _exchange_wait(f"reduce_wait_{tag}", send_sems, recv_sems, thru, _owner_copies, after)
        sums, zones = done[:len(keys)], done[len(keys):]
        for i, k in enumerate(keys):
            r2, cols = halves[k]
            reduced[k] = _ew_call(
                lambda a, b, c, d: (((a.astype(F32) + b.astype(F32)) + c.astype(F32)) + d.astype(F32),),
                "quarter_sum", [(sums[i][None], (0, this_chip))] + [(zones[i][None], (0, j)) for j in range(3)],
                [(sds((DEPTH, 2, r2, cols), F32), (l, this_core))], 1, sel, into=reduced.get(k))[0]

    dx1, gb_ffn, gs1 = _layer_bwd_ffn(dx, sv1, big1, ts)
    merge_out, gb_merge = _layer_bwd_merge(dx1, sv1, big1, ts)
    dx_mid, gb_in, gs1_mix = _layer_bwd_branches(dx1, merge_out, sv1, big1, lru_lambda[1], ts)
    gb_1 = {**gb_ffn, **gb_merge, **gb_in}
    pair_1 = pair_start("1", gb_1, dx_mid)
    dx1, gb_ffn, gs0 = _layer_bwd_ffn(dx_mid, sv0, big0, ts, pair_1[3])
    exchange_1 = reduce_start("1", gb_1, dx1, pair_1)
    pair_0a = pair_start("0a", gb_ffn, exchange_1[1][3])
    merge_out, gb_merge = _layer_bwd_merge(dx1, sv0, big0, ts, pair_0a[3])
    exchange_0a = reduce_start("0a", gb_ffn, merge_out[0], pair_0a)
    pair_0b = pair_start("0b", gb_merge, exchange_0a[1][3])
    started_0b = {}

    def after_sgu(duv):
        started_0b["exchange"] = reduce_start("0b", gb_merge, duv, pair_0b)
        return started_0b["exchange"][1][3]

    grad_x, gb_in, gs0_mix = _layer_bwd_branches(dx1, merge_out, sv0, big0, lru_lambda[0], ts, pair_0b[3],
                                                 after_sgu)
    exchange_0b = started_0b["exchange"]
    exchange_0c = reduce_start("0c", gb_in, exchange_0b[1][3])
    layer_gs = [{**gs0, **gs0_mix}, {**gs1, **gs1_mix}]
    gs = {k: jnp.stack([g[k] for g in layer_gs]) for k in layer_gs[0]}
    gs["final_norm_g"] = dgf[0]
    gs["loss"] = loss[0, 0:1]

    me = ("sel", 3)
    piece = (1, N_DEVICES, SMALL_ROWS, 128)
    packed = _pack_small(gs).reshape(piece)
    scatter = _exchange_start("small_scatter_start", [packed[0], lax.empty(piece[1:], F32)], _small_scatter_copies,
                              N_DEVICES - 1, exchange_0c[1][3])
    reduced = {}
    reduce_finish("1", 1, exchange_1, scatter[3], reduced)
    reduce_finish("0a", 0, exchange_0a, reduced["w_in"], reduced)
    reduce_finish("0b", 0, exchange_0b, reduced["w_down"], reduced)

    def swap_slabs(ref, c, i):
        layers = (1,) if BIG[i] == "w_in" else range(DEPTH)
        return [(ref.at[l, c], ref.at[l, 1 - c]) for l in layers]

    swapped = dict(zip(BIG, _sibling_inplace_call("grads_swap_halves", [reduced[k] for k in BIG], swap_slabs,
                                                  DEPTH * len(BIG) - 1)))

    def adamw_layers(k, grad, layer, into, after=None):
        if layer is None:
            views = [_as4(_as_rows(a)) for a in (w[k], grad, m[k], v[k])]
            idx = (0, 0)
        else:
            views = [a.reshape((1,) + w[k].shape) for a in (w[k], grad, m[k], v[k])]
            idx = (0, layer)
        return _ew_call(_adamw, "adamw_big", [(a, idx) for a in views], [(sds(views[0].shape, F32), idx)] * 3,
                        into=into, after=after)

    updated, last_update = {}, None
    for k in BIG:
        updated[k] = adamw_layers(k, swapped[k], 1 if k == "w_in" else None, None, last_update)
        last_update = updated[k][0]
    scattered = _exchange_wait("small_scatter_wait", scatter[0], scatter[1], scatter[2], _small_scatter_copies,
                               last_update)
    summed = _ew_call(
        lambda *parts: (functools.reduce(lambda a, b: a + b, parts),), "small_sum",
        [(scattered[0][None], (0, me))]
        + [(scattered[1][None], (0, lambda g, s, k=k: s[3] ^ k)) for k in range(1, N_DEVICES)],
        [(sds(piece, F32), (0, me))], 1, sel)[0]
    spread = _exchange_start("small_spread_start", [summed[0]], _small_spread_copies, N_DEVICES - 1, summed)
    reduced["w_in"] = swapped["w_in"]
    reduce_finish("0c", 0, exchange_0c, spread[3], reduced)
    last = _sibling_inplace_call("grads_swap_last", [reduced["w_in"]],
                                 lambda ref, c, i: [(ref.at[0, c], ref.at[0, 1 - c])], 1)[0]
    swapped["w_in"] = last
    updated["w_in"] = adamw_layers("w_in", last, 0, updated["w_in"])
    grads_big = {k: swapped[k].reshape(w[k].shape) for k in BIG}
    delta, new_m, new_v = ({k: updated[k][j].reshape(w[k].shape) for k in BIG} for j in range(3))
    gathered_small = _exchange_wait("small_spread_wait", spread[0], spread[1], spread[2], _small_spread_copies,
                                    updated["w_in"][0])[0]

    like = {k: jax.ShapeDtypeStruct(gs[k].shape, F32) for k in SMALL}
    like["loss"] = jax.ShapeDtypeStruct((1,), F32)
    grads_small = _unpack_small(gathered_small, like)
    total = grads_small.pop("loss")[0]
    conv_q = grads_small["conv_w"].reshape(DEPTH, CONV_WIDTH, N_QUARTERS, D_RNN // N_QUARTERS)
    grads_small["conv_w"] = lax.dynamic_index_in_dim(conv_q, chip, axis=2, keepdims=False)
    at_least_2d = lambda a: a.reshape(1, -1) if a.ndim == 1 else a
    outs = _small_adamw_call(*[[at_least_2d(d[k]) for k in SMALL] for d in (w, grads_small, m, v)])
    for d, o in zip((delta, new_m, new_v), outs):
        for k, a in zip(SMALL, o):
            d[k] = a.reshape(w[k].shape)

    grads = {**grads_big, **grads_small}
    return (total, grad_x[None], *[grads[k] for k in WEIGHTS], *[delta[k] for k in WEIGHTS],
            *[new_m[k] for k in WEIGHTS], *[new_v[k] for k in WEIGHTS])
```

```python
import functools
import math

import jax
import jax.numpy as jnp
from jax import lax
from jax.experimental import pallas as pl
from jax.experimental.pallas import tpu as pltpu

F32 = jnp.float32
BF = jnp.bfloat16

DEPTH = 2
D_MODEL = 1024
D_RNN = 1280
D_SGU = 1024
D_FF = 4096
D_IN = 2 * D_RNN + 2 * D_SGU + 2 * D_MODEL
N_QUARTERS = 4
Q_IN = D_IN // N_QUARTERS
Q_FF = D_FF // N_QUARTERS
RNN_HEADS = 20
RNN_HEAD_DIM = 64
LRU_GROUP = 256
N_LRU_GROUPS = D_RNN // LRU_GROUP
HEADS_PER_GROUP = LRU_GROUP // RNN_HEAD_DIM
CONV_WIDTH = 4
LRU_C = 8.0
SGU_GROUPS = 8
SGU_BLOCK = 128
CHUNK = 64
EPS = 1e-6

ADAM_LR = 0.001
ADAM_B1 = 0.9
ADAM_B2 = 0.999
ADAM_EPS = 1e-08
ADAM_WD = 0.01
ADAM_STEP = 10

SUBLANES = 8
TOKEN_TILE = 512
VMEM_LIMIT_BYTES = 56 * 1024 * 1024

MESH = pl.DeviceIdType.MESH


def _params(semantics=None, vmem=True, **kw):
    return pltpu.CompilerParams(
        dimension_semantics=semantics,
        vmem_limit_bytes=VMEM_LIMIT_BYTES if vmem else None,
        **kw,
    )


def _dot(a, b):
    return jnp.dot(a, b, preferred_element_type=F32)


def _dot_nt(a, b):
    return lax.dot_general(a, b, (((1,), (1,)), ((), ())), preferred_element_type=F32)


def _dot_tn(a, b):
    return lax.dot_general(a, b, (((0,), (0,)), ((), ())), preferred_element_type=F32)


_GELU_C = math.sqrt(2.0 / math.pi)
_GELU_A = 0.044715


def _gelu(x):
    return 0.5 * x * (1.0 + jnp.tanh(_GELU_C * (x + _GELU_A * x * x * x)))


def _gelu_and_grad(x):
    x2 = x * x
    t = jnp.tanh(_GELU_C * (x + _GELU_A * x2 * x))
    du = _GELU_C * (1.0 + 3.0 * _GELU_A * x2)
    return 0.5 * x * (1.0 + t), 0.5 * (1.0 + t) + 0.5 * x * (1.0 - t * t) * du


def _rms_stats(x):
    return lax.rsqrt(jnp.mean(x * x, axis=-1, keepdims=True) + EPS)


def _rms_bwd(dy, x, g):
    rs = _rms_stats(x)
    n = x * rs
    dn = dy * g
    dx = rs * (dn - n * jnp.mean(dn * n, axis=-1, keepdims=True))
    return dx, dy * n


def _row_sum(x):
    return jnp.sum(x, axis=0, keepdims=True)


def _tile_spec(ts, width, col=0):
    return pl.BlockSpec((ts, width), lambda i, col=col: (i, col))


def _full_spec(shape):
    zeros = (0,) * len(shape)
    return pl.BlockSpec(shape, lambda *_: zeros)


def _layer_spec(w, layer):
    zeros = (0,) * (w.ndim - 1)
    return pl.BlockSpec((None,) + tuple(w.shape[1:]), lambda *_: (layer,) + zeros)


def _with_after(body, n_in, after):
    if after is None:
        return body, [], []

    def wrapped(*refs):
        return body(*refs[:n_in], *refs[n_in + 1:])

    return wrapped, [pl.BlockSpec(memory_space=pl.ANY)], [after]


def _norm_call(x, g, ts, after=None):
    s = x.shape[0]

    def body(x_ref, g_ref, h_ref):
        xv = x_ref[...]
        h_ref[...] = (xv * _rms_stats(xv) * g_ref[...]).astype(BF)

    body, more_specs, more = _with_after(body, 2, after)
    return pl.pallas_call(
        body, name="norm_fwd", grid=(s // ts,),
        in_specs=[_tile_spec(ts, D_MODEL), _full_spec((1, D_MODEL))] + more_specs,
        out_specs=_tile_spec(ts, D_MODEL),
        out_shape=jax.ShapeDtypeStruct((s, D_MODEL), BF),
        compiler_params=_params(("parallel",)),
    )(x, g, *more)


def _inproj_call(h, w_in, layer, ts):
    s = h.shape[0]

    def body(h_ref, w_ref, o_ref):
        o_ref[...] = _dot(h_ref[...], w_ref[...]).astype(BF)

    return pl.pallas_call(
        body, name="inproj_fwd", grid=(N_QUARTERS, s // ts),
        in_specs=[
            pl.BlockSpec((ts, D_MODEL), lambda q, i: (i, 0)),
            pl.BlockSpec((None, None, D_MODEL, Q_IN), lambda q, i: (layer, q, 0, 0)),
        ],
        out_specs=pl.BlockSpec((ts, Q_IN), lambda q, i: (i, q)),
        out_shape=jax.ShapeDtypeStruct((s, D_IN), BF),
        compiler_params=_params(("parallel", "parallel")),
    )(h, w_in)


def _inproj_part_call(h, w_in, ts, own, first, count, into=None):
    s = h.shape[0]

    def quarter(j, sel):
        return (sel[0] + first + j) % N_QUARTERS

    def body(sel_ref, h_ref, w_ref, *rest):
        rest[-1][...] = _dot(h_ref[...], w_ref[...]).astype(BF)

    in_specs = [pl.BlockSpec((ts, D_MODEL), lambda j, i, sel: (i, 0)),
                pl.BlockSpec((None, None, D_MODEL, Q_IN), lambda j, i, sel: (0, quarter(j, sel), 0, 0))]
    operands = [h, w_in]
    aliases = {}
    if into is not None:
        in_specs.append(pl.BlockSpec(memory_space=pl.ANY))
        operands.append(into)
        aliases = {3: 0}
    return pl.pallas_call(
        body, name="inproj_fwd_part", out_shape=jax.ShapeDtypeStruct((s, D_IN), BF),
        grid_spec=pltpu.PrefetchScalarGridSpec(
            num_scalar_prefetch=1, grid=(count, s // ts), in_specs=in_specs,
            out_specs=pl.BlockSpec((ts, Q_IN), lambda j, i, sel: (i, quarter(j, sel)))),
        input_output_aliases=aliases,
        compiler_params=_params(("parallel", "parallel")),
    )(own, *operands)


def _shift_down(x, tail, s):
    xr = pltpu.roll(x, s, 0)
    tr = pltpu.roll(tail, s, 0)
    row = lax.broadcasted_iota(jnp.int32, tail.shape, 0)
    top = jnp.where(row < s, tr, xr[0:SUBLANES])
    return jnp.concatenate([top, xr[SUBLANES:]], axis=0)


def _shift_up(x, head, s):
    t = x.shape[0]
    xr = pltpu.roll(x, t - s, 0)
    hr = pltpu.roll(head, SUBLANES - s, 0)
    row = lax.broadcasted_iota(jnp.int32, head.shape, 0)
    bottom = jnp.where(row >= SUBLANES - s, hr, xr[t - SUBLANES:])
    return jnp.concatenate([xr[: t - SUBLANES], bottom], axis=0)


def _conv_fwd(x, tail, cw_ref, cb_ref):
    out = cb_ref[...] + cw_ref[CONV_WIDTH - 1:CONV_WIDTH, :] * x
    for s in range(1, CONV_WIDTH):
        k = CONV_WIDTH - 1 - s
        out = out + cw_ref[k:k + 1, :] * _shift_down(x, tail, s)
    return out


def _group_dot(x_bf, w_ref, dot):
    cols = [dot(x_bf[:, g * LRU_GROUP:(g + 1) * LRU_GROUP], w_ref[g]) for g in range(N_LRU_GROUPS)]
    return jnp.concatenate(cols, axis=1)


def _lru_gates(xr, wa_ref, wx_ref, ba_ref, bx_ref, sp_ref):
    xb = xr.astype(BF)
    r = jax.nn.sigmoid(_group_dot(xb, wa_ref, _dot) + ba_ref[...])
    i = jax.nn.sigmoid(_group_dot(xb, wx_ref, _dot) + bx_ref[...])
    log_a = (-LRU_C * r) * sp_ref[...]
    a = jnp.exp(log_a)
    nrm2 = -jnp.tanh(log_a) * (a * a + 1.0)
    inv_nrm = lax.rsqrt(jnp.maximum(nrm2, 1e-36))
    return r, i, a, nrm2 * inv_nrm, inv_nrm


def _linear_scan(a, b, carry, al_ref, bl_ref, h_ref, reverse):
    t, c = a.shape
    rowm = lax.broadcasted_iota(jnp.int32, (t, c), 0) & (SUBLANES - 1)
    for d in (1, 2, 4):
        if reverse:
            keep, sh = rowm < SUBLANES - d, t - d
        else:
            keep, sh = rowm >= d, d
        a_sh = jnp.where(keep, pltpu.roll(a, sh, 0), 1.0)
        b_sh = jnp.where(keep, pltpu.roll(b, sh, 0), 0.0)
        b = a * b_sh + b
        a = a * a_sh
    al_ref[...] = a
    bl_ref[...] = b
    groups = t // SUBLANES

    def step(j, state):
        jj = groups - 1 - j if reverse else j
        off = pl.multiple_of(jj * SUBLANES, SUBLANES)
        rows = bl_ref[pl.ds(off, SUBLANES), :] + al_ref[pl.ds(off, SUBLANES), :] * state
        h_ref[pl.ds(off, SUBLANES), :] = rows
        last = rows[0:1, :] if reverse else rows[SUBLANES - 1:SUBLANES, :]
        return jnp.broadcast_to(last, (SUBLANES, c))

    out = lax.fori_loop(0, groups, step, jnp.broadcast_to(carry, (SUBLANES, c)))
    return out[0:1, :]


def _rnn_fwd_call(proj, wa, wx, ba, bx, sp, cw, cb, ts):
    s = proj.shape[0]

    def body(xg_ref, wa_ref, wx_ref, ba_ref, bx_ref, sp_ref, cw_ref, cb_ref, xr_ref, hr_ref, ya_ref,
             tail_sc, carry_sc, al_sc, bl_sc, h_sc):
        @pl.when(pl.program_id(0) == 0)
        def _():
            tail_sc[...] = jnp.zeros_like(tail_sc)
            carry_sc[...] = jnp.zeros_like(carry_sc)

        x = xg_ref[:, :D_RNN].astype(F32)
        g = xg_ref[:, D_RNN:]
        xr = _conv_fwd(x, tail_sc[...], cw_ref, cb_ref)
        tail_sc[...] = x[ts - SUBLANES:, :]
        xr_ref[...] = xr.astype(BF)
        _, i, a, nrm, _ = _lru_gates(xr, wa_ref, wx_ref, ba_ref, bx_ref, sp_ref)
        carry_sc[...] = _linear_scan(a, nrm * (i * xr), carry_sc[...], al_sc, bl_sc, h_sc, False)
        h = h_sc[...]
        hr_ref[...] = h.astype(BF)
        ya_ref[...] = (h * _gelu(g)).astype(BF)

    gw = (N_LRU_GROUPS, LRU_GROUP, LRU_GROUP)
    return pl.pallas_call(
        body, name="rnn_fwd", grid=(s // ts,),
        in_specs=[_tile_spec(ts, 2 * D_RNN), _full_spec(gw), _full_spec(gw),
                  _full_spec((1, D_RNN)), _full_spec((1, D_RNN)), _full_spec((1, D_RNN)),
                  _full_spec((CONV_WIDTH, D_RNN)), _full_spec((1, D_RNN))],
        out_specs=[_tile_spec(ts, D_RNN)] * 3,
        out_shape=[jax.ShapeDtypeStruct((s, D_RNN), BF)] * 3,
        scratch_shapes=[pltpu.VMEM((SUBLANES, D_RNN), F32), pltpu.VMEM((1, D_RNN), F32),
                        pltpu.VMEM((ts, D_RNN), F32), pltpu.VMEM((ts, D_RNN), F32),
                        pltpu.VMEM((ts, D_RNN), F32)],
        compiler_params=_params(("arbitrary",)),
    )(proj, wa, wx, ba, bx, sp, cw, cb)


def _layernorm_fwd(x):
    mu = jnp.mean(x, axis=-1, keepdims=True)
    xc = x - mu
    rstd = lax.rsqrt(jnp.mean(xc * xc, axis=-1, keepdims=True) + EPS)
    return xc * rstd, rstd


def _sgu_mix(vn_bf, wm_ref, bsb_ref, ts):
    rows = []
    for blk in range(ts // SGU_BLOCK):
        r0 = blk * SGU_BLOCK
        cols = [
            _dot(wm_ref[g], vn_bf[r0:r0 + SGU_BLOCK, g * SGU_BLOCK:(g + 1) * SGU_BLOCK]) + bsb_ref[g]
            for g in range(SGU_GROUPS)
        ]
        rows.append(jnp.concatenate(cols, axis=1))
    return jnp.concatenate(rows, axis=0)


def _sgu_fwd_call(proj, wm, bsb, lg, lb, ts, after=None):
    s = proj.shape[0]

    def body(uv_ref, wm_ref, bsb_ref, lg_ref, lb_ref, yb_ref):
        gu = _gelu(uv_ref[:, :D_SGU])
        gv = _gelu(uv_ref[:, D_SGU:2 * D_SGU]).astype(F32)
        nh, _ = _layernorm_fwd(gv)
        vn = (nh * lg_ref[...] + lb_ref[...]).astype(BF)
        yb_ref[...] = (gu * _sgu_mix(vn, wm_ref, bsb_ref, ts)).astype(BF)

    sw = (SGU_GROUPS, SGU_BLOCK, SGU_BLOCK)
    body, more_specs, more = _with_after(body, 5, after)
    return pl.pallas_call(
        body, name="sgu_fwd", grid=(s // ts,),
        in_specs=[_tile_spec(ts, 2 * D_RNN, 1), _full_spec(sw), _full_spec(sw),
                  _full_spec((1, D_SGU)), _full_spec((1, D_SGU))] + more_specs,
        out_specs=_tile_spec(ts, D_SGU),
        out_shape=jax.ShapeDtypeStruct((s, D_SGU), BF),
        compiler_params=_params(("parallel",)),
    )(proj, wm, bsb, lg, lb, *more)


_GATE_COL0 = (2 * D_RNN + 2 * D_SGU) // 512


def _gate_specs(ts):
    return [_tile_spec(ts, 512, _GATE_COL0 + j) for j in range(4)]


def _merge_call(x, proj, ya_pre, yb_pre, w_ba, w_bb, w_out, g2, layer, ts):
    s = x.shape[0]

    def body(x_ref, ga0, ga1, gb0, gb1, ya_ref, yb_ref, wa_ref, wb_ref, wo_ref, g2_ref,
             x1_ref, yao_ref, ybo_ref, mg_ref, h2_ref):
        ya = _dot(ya_ref[...], wa_ref[...])
        yb = _dot(yb_ref[...], wb_ref[...])
        sa = jax.nn.sigmoid(jnp.concatenate([ga0[...], ga1[...]], axis=1).astype(F32))
        sb = jax.nn.sigmoid(jnp.concatenate([gb0[...], gb1[...]], axis=1).astype(F32))
        merged = (sa * ya + sb * yb).astype(BF)
        x1 = x_ref[...] + _dot(merged, wo_ref[...])
        x1_ref[...] = x1
        yao_ref[...] = ya.astype(BF)
        ybo_ref[...] = yb.astype(BF)
        mg_ref[...] = merged
        h2_ref[...] = (x1 * _rms_stats(x1) * g2_ref[...]).astype(BF)

    act = jax.ShapeDtypeStruct((s, D_MODEL), BF)
    return pl.pallas_call(
        body, name="merge_fwd", grid=(s // ts,),
        in_specs=[_tile_spec(ts, D_MODEL)] + _gate_specs(ts) + [
            _tile_spec(ts, D_RNN), _tile_spec(ts, D_SGU),
            _layer_spec(w_ba, layer), _layer_spec(w_bb, layer), _layer_spec(w_out, layer),
            _full_spec((1, D_MODEL))],
        out_specs=[_tile_spec(ts, D_MODEL)] * 5,
        out_shape=[jax.ShapeDtypeStruct((s, D_MODEL), F32), act, act, act, act],
        compiler_params=_params(("parallel",)),
    )(x, proj, proj, proj, proj, ya_pre, yb_pre, w_ba, w_bb, w_out, g2)


def _ffn_call(x1, h2, w_up, w_down, layer, ts):
    s = x1.shape[0]

    def body(x1_ref, h2_ref, wu_ref, wd_ref, x2_ref, p_ref):
        h2v = h2_ref[...]
        acc = x1_ref[...]
        for q in range(N_QUARTERS):
            p = _dot(h2v, wu_ref[q])
            p_ref[:, q * Q_FF:(q + 1) * Q_FF] = p.astype(BF)
            f = jnp.square(jnp.maximum(p, 0.0)).astype(BF)
            acc = acc + _dot(f, wd_ref[q * Q_FF:(q + 1) * Q_FF, :])
        x2_ref[...] = acc

    return pl.pallas_call(
        body, name="ffn_fwd", grid=(s // ts,),
        in_specs=[_tile_spec(ts, D_MODEL), _tile_spec(ts, D_MODEL),
                  pl.BlockSpec((None, N_QUARTERS, D_MODEL, Q_FF), lambda i: (layer, 0, 0, 0)),
                  pl.BlockSpec((None, D_FF, D_MODEL), lambda i: (layer, 0, 0))],
        out_specs=[_tile_spec(ts, D_MODEL), _tile_spec(ts, D_FF)],
        out_shape=[jax.ShapeDtypeStruct((s, D_MODEL), F32), jax.ShapeDtypeStruct((s, D_FF), BF)],
        compiler_params=_params(("parallel",)),
    )(x1, h2, w_up, w_down)


def _loss_call(x, target, gf, ts):
    s = x.shape[0]

    def body(x_ref, t_ref, g_ref, dx_ref, loss_ref, dg_ref):
        @pl.when(pl.program_id(0) == 0)
        def _():
            loss_ref[...] = jnp.zeros_like(loss_ref)
            dg_ref[...] = jnp.zeros_like(dg_ref)

        xv = x_ref[...]
        gv = g_ref[...]
        err = xv * _rms_stats(xv) * gv - t_ref[...]
        part = 0.5 * jnp.sum(jnp.mean(err * err, axis=-1, keepdims=True), axis=0, keepdims=True)
        loss_ref[...] += jnp.broadcast_to(part, loss_ref.shape)
        dx, dg = _rms_bwd(err * (1.0 / D_MODEL), xv, gv)
        dx_ref[...] = dx
        dg_ref[...] += _row_sum(dg)

    return pl.pallas_call(
        body, name="loss_head", grid=(s // ts,),
        in_specs=[_tile_spec(ts, D_MODEL), _tile_spec(ts, D_MODEL), _full_spec((1, D_MODEL))],
        out_specs=[_tile_spec(ts, D_MODEL), _full_spec((1, 128)), _full_spec((1, D_MODEL))],
        out_shape=[jax.ShapeDtypeStruct((s, D_MODEL), F32), jax.ShapeDtypeStruct((1, 128), F32),
                   jax.ShapeDtypeStruct((1, D_MODEL), F32)],
        compiler_params=_params(("arbitrary",)),
    )(x, target, gf)


def _ffn_bwd_call(dx2, p, x1, g2, w_up, w_down, layer, ts, after=None):
    s = dx2.shape[0]

    def body(dx2_ref, p_ref, x1_ref, g2_ref, wu_ref, wd_ref, dx1_ref, dp_ref, dg_ref, dx2b_ref, dx1b_ref):
        @pl.when(pl.program_id(0) == 0)
        def _():
            dg_ref[...] = jnp.zeros_like(dg_ref)

        dx2v = dx2_ref[...]
        dyb = dx2v.astype(BF)
        dx2b_ref[...] = dyb
        dh2 = jnp.zeros((ts, D_MODEL), F32)
        for q in range(N_QUARTERS):
            cols = slice(q * Q_FF, (q + 1) * Q_FF)
            df = _dot_nt(dyb, wd_ref[cols, :])
            dp = (df * (2.0 * jnp.maximum(p_ref[:, cols].astype(F32), 0.0))).astype(BF)
            dp_ref[:, cols] = dp
            dh2 = dh2 + _dot_nt(dp, wu_ref[q])
        dx, dg = _rms_bwd(dh2, x1_ref[...], g2_ref[...])
        dx1 = dx2v + dx
        dx1_ref[...] = dx1
        dx1b_ref[...] = dx1.astype(BF)
        dg_ref[...] += _row_sum(dg)

    body, more_specs, more = _with_after(body, 6, after)
    return pl.pallas_call(
        body, name="ffn_bwd", grid=(s // ts,),
        in_specs=[_tile_spec(ts, D_MODEL), _tile_spec(ts, D_FF), _tile_spec(ts, D_MODEL),
                  _full_spec((1, D_MODEL)),
                  pl.BlockSpec((None, N_QUARTERS, D_MODEL, Q_FF), lambda i: (layer, 0, 0, 0)),
                  pl.BlockSpec((None, D_FF, D_MODEL), lambda i: (layer, 0, 0))] + more_specs,
        out_specs=[_tile_spec(ts, D_MODEL), _tile_spec(ts, D_FF), _full_spec((1, D_MODEL)),
                   _tile_spec(ts, D_MODEL), _tile_spec(ts, D_MODEL)],
        out_shape=[jax.ShapeDtypeStruct((s, D_MODEL), F32), jax.ShapeDtypeStruct((s, D_FF), BF),
                   jax.ShapeDtypeStruct((1, D_MODEL), F32),
                   jax.ShapeDtypeStruct((s, D_MODEL), BF), jax.ShapeDtypeStruct((s, D_MODEL), BF)],
        compiler_params=_params(("arbitrary",)),
    )(dx2, p, x1, g2, w_up, w_down, *more)


def _merge_bwd_call(dx1, proj, ya, yb, w_ba, w_bb, w_out, layer, ts, after=None):
    s = dx1.shape[0]

    def body(dx1_ref, ga0, ga1, gb0, gb1, ya_ref, yb_ref, wa_ref, wb_ref, wo_ref, *rest):
        dya_ref, dyb_ref, dgate_ref, dyap_ref, dybp_ref = rest[-5:]
        dm = _dot_nt(dx1_ref[...].astype(BF), wo_ref[...])
        sa = jax.nn.sigmoid(jnp.concatenate([ga0[...], ga1[...]], axis=1).astype(F32))
        sb = jax.nn.sigmoid(jnp.concatenate([gb0[...], gb1[...]], axis=1).astype(F32))
        dya = (dm * sa).astype(BF)
        dyb = (dm * sb).astype(BF)
        dya_ref[...] = dya
        dyb_ref[...] = dyb
        dgate_ref[:, :D_MODEL] = (dm * ya_ref[...].astype(F32) * sa * (1.0 - sa)).astype(BF)
        dgate_ref[:, D_MODEL:] = (dm * yb_ref[...].astype(F32) * sb * (1.0 - sb)).astype(BF)
        dyap_ref[...] = _dot_nt(dya, wa_ref[...]).astype(BF)
        dybp_ref[...] = _dot_nt(dyb, wb_ref[...]).astype(BF)

    act = jax.ShapeDtypeStruct((s, D_MODEL), BF)
    return pl.pallas_call(
        body, name="merge_bwd", grid=(s // ts,),
        in_specs=[_tile_spec(ts, D_MODEL)] + _gate_specs(ts) + [
            _tile_spec(ts, D_MODEL), _tile_spec(ts, D_MODEL),
            _layer_spec(w_ba, layer), _layer_spec(w_bb, layer), _layer_spec(w_out, layer)]
        + ([] if after is None else [pl.BlockSpec(memory_space=pl.ANY)]),
        out_specs=[_tile_spec(ts, D_MODEL), _tile_spec(ts, D_MODEL), _tile_spec(ts, 2 * D_MODEL),
                   _tile_spec(ts, D_RNN), _tile_spec(ts, D_SGU)],
        out_shape=[act, act, jax.ShapeDtypeStruct((s, 2 * D_MODEL), BF),
                   jax.ShapeDtypeStruct((s, D_RNN), BF), jax.ShapeDtypeStruct((s, D_SGU), BF)],
        compiler_params=_params(("parallel",)),
    )(dx1, proj, proj, proj, proj, ya, yb, w_ba, w_bb, w_out, *([] if after is None else [after]))


def _sgu_bwd_call(dyb_pre, proj, wm, bsb, mask, lg, lb, ts, after=None):
    s = proj.shape[0]

    def body(dy_ref, uv_ref, wm_ref, bsb_ref, mask_ref, lg_ref, lb_ref,
             duv_ref, dws_ref, dbs_ref, dlg_ref, dlb_ref, dm_sc):
        step = pl.program_id(0)

        @pl.when(step == 0)
        def _():
            dws_ref[...] = jnp.zeros_like(dws_ref)
            dlg_ref[...] = jnp.zeros_like(dlg_ref)
            dlb_ref[...] = jnp.zeros_like(dlb_ref)
            dm_sc[...] = jnp.zeros_like(dm_sc)

        gu, dgu_du = _gelu_and_grad(uv_ref[:, :D_SGU])
        gv, dgv_dv = _gelu_and_grad(uv_ref[:, D_SGU:2 * D_SGU])
        nh, rstd = _layernorm_fwd(gv.astype(F32))
        lgv = lg_ref[...]
        vn = (nh * lgv + lb_ref[...]).astype(BF)
        dy = dy_ref[...].astype(F32)
        du = dy * _sgu_mix(vn, wm_ref, bsb_ref, ts) * dgu_du
        dmix = dy * gu
        dmix_bf = dmix.astype(BF)
        dm_acc = dm_sc[...]
        rows = []
        for blk in range(ts // SGU_BLOCK):
            r0 = blk * SGU_BLOCK
            dm_acc = dm_acc + dmix[r0:r0 + SGU_BLOCK, :]
            cols = []
            for g in range(SGU_GROUPS):
                c0 = g * SGU_BLOCK
                dmg = dmix_bf[r0:r0 + SGU_BLOCK, c0:c0 + SGU_BLOCK]
                cols.append(_dot_tn(wm_ref[g], dmg))
                dws_ref[g] += mask_ref[...] * _dot_nt(dmg, vn[r0:r0 + SGU_BLOCK, c0:c0 + SGU_BLOCK])
            rows.append(jnp.concatenate(cols, axis=1))
        dm_sc[...] = dm_acc
        dvn = jnp.concatenate(rows, axis=0)
        dlg_ref[...] += _row_sum(dvn * nh)
        dlb_ref[...] += _row_sum(dvn)
        dnh = dvn * lgv
        dgv = rstd * (dnh - jnp.mean(dnh, axis=-1, keepdims=True)
                      - nh * jnp.mean(dnh * nh, axis=-1, keepdims=True))
        duv_ref[:, :D_SGU] = du.astype(BF)
        duv_ref[:, D_SGU:] = (dgv * dgv_dv).astype(BF)

        @pl.when(step == pl.num_programs(0) - 1)
        def _():
            for g in range(SGU_GROUPS):
                dbs_ref[:, g:g + 1] = jnp.sum(
                    dm_acc[:, g * SGU_BLOCK:(g + 1) * SGU_BLOCK], axis=1, keepdims=True)

    sw = (SGU_GROUPS, SGU_BLOCK, SGU_BLOCK)
    body, more_specs, more = _with_after(body, 7, after)
    return pl.pallas_call(
        body, name="sgu_bwd", grid=(s // ts,),
        in_specs=[_tile_spec(ts, D_SGU), _tile_spec(ts, 2 * D_RNN, 1), _full_spec(sw), _full_spec(sw),
                  _full_spec((SGU_BLOCK, SGU_BLOCK)), _full_spec((1, D_SGU)), _full_spec((1, D_SGU))] + more_specs,
        out_specs=[_tile_spec(ts, 2 * D_SGU), _full_spec(sw), _full_spec((SGU_BLOCK, SGU_GROUPS)),
                   _full_spec((1, D_SGU)), _full_spec((1, D_SGU))],
        out_shape=[jax.ShapeDtypeStruct((s, 2 * D_SGU), BF), jax.ShapeDtypeStruct(sw, F32),
                   jax.ShapeDtypeStruct((SGU_BLOCK, SGU_GROUPS), F32),
                   jax.ShapeDtypeStruct((1, D_SGU), F32), jax.ShapeDtypeStruct((1, D_SGU), F32)],
        scratch_shapes=[pltpu.VMEM((SGU_BLOCK, D_SGU), F32)],
        compiler_params=_params(("arbitrary",)),
    )(dyb_pre, proj, wm, bsb, mask, lg, lb, *more)


_ROW_DBA, _ROW_DBX, _ROW_DSP, _ROW_DCB, _ROW_DCW = 0, 1, 2, 3, 4
_PREV_ROWS = 16


def _rnn_bwd_call(dya_pre, proj, xr_saved, hr, wa, wx, ba, bx, sp, cw, ts, after=None):
    s = proj.shape[0]
    nt = s // ts
    per = ts // _PREV_ROWS

    def tile(i):
        return nt - 1 - i

    def prev(i):
        return jnp.maximum(tile(i) * per - 1, 0)

    def body(dy_ref, xg_ref, xr_ref, hr_ref, hrp_ref, wa_ref, wx_ref, ba_ref, bx_ref, sp_ref,
             cw_ref, dxg_ref, dwa_ref, dwx_ref, vec_ref,
             lam_carry, a_first, dxr_head, al_sc, bl_sc, lam_sc):
        step = pl.program_id(0)

        @pl.when(step == 0)
        def _():
            dwa_ref[...] = jnp.zeros_like(dwa_ref)
            dwx_ref[...] = jnp.zeros_like(dwx_ref)
            vec_ref[...] = jnp.zeros_like(vec_ref)
            lam_carry[...] = jnp.zeros_like(lam_carry)
            a_first[...] = jnp.zeros_like(a_first)
            dxr_head[...] = jnp.zeros_like(dxr_head)

        has_prev = (step < nt - 1).astype(F32)
        x = xg_ref[:, :D_RNN].astype(F32)
        g = xg_ref[:, D_RNN:]
        h_tail =hrp_ref[_PREV_ROWS - SUBLANES:, :].astype(F32) * has_prev
        xr = xr_ref[...].astype(F32)
        r, i, a, nrm, inv_nrm = _lru_gates(xr, wa_ref, wx_ref, ba_ref, bx_ref, sp_ref)
        h = hr_ref[...].astype(F32)
        dy = dy_ref[...].astype(F32)
        gg, dgg = _gelu_and_grad(g)

        coef = _shift_up(a, jnp.broadcast_to(a_first[...], (SUBLANES, D_RNN)), 1)
        lam_carry[...] = _linear_scan(coef, dy * gg, lam_carry[...], al_sc, bl_sc, lam_sc, True)
        a_first[...] = a[0:1, :]
        lam = lam_sc[...]

        da = lam * _shift_down(h, h_tail, 1)
        dnrm = lam * (i * xr)
        di = lam * nrm * xr
        dlog_a = da * a - dnrm * (a * a) * inv_nrm
        spv = sp_ref[...]
        dza = (dlog_a * (-LRU_C * spv)) * (r * (1.0 - r))
        dzx = di * (i * (1.0 - i))
        vec_ref[_ROW_DSP:_ROW_DSP + 1, :] += _row_sum(dlog_a * (-LRU_C * r))
        vec_ref[_ROW_DBA:_ROW_DBA + 1, :] += _row_sum(dza)
        vec_ref[_ROW_DBX:_ROW_DBX + 1, :] += _row_sum(dzx)
        xb = xr.astype(BF)
        dza_bf = dza.astype(BF)
        dzx_bf = dzx.astype(BF)
        for grp in range(N_LRU_GROUPS):
            cols = slice(grp * LRU_GROUP, (grp + 1) * LRU_GROUP)
            dwa_ref[grp] += _dot_tn(xb[:, cols], dza_bf[:, cols])
            dwx_ref[grp] += _dot_tn(xb[:, cols], dzx_bf[:, cols])
        dxr = (lam * nrm * i + _group_dot(dza_bf, wa_ref, _dot_nt) + _group_dot(dzx_bf, wx_ref, _dot_nt))

        vec_ref[_ROW_DCB:_ROW_DCB + 1, :] += _row_sum(dxr)
        head = dxr_head[...]
        dx = cw_ref[CONV_WIDTH - 1:CONV_WIDTH, :] * dxr
        vec_ref[_ROW_DCW + 3:_ROW_DCW + 4, :] += _row_sum(dxr * x)
        for sft in range(1, CONV_WIDTH):
            k = CONV_WIDTH - 1 - sft
            ahead = _shift_up(dxr, head, sft)
            dx = dx + cw_ref[k:k + 1, :] * ahead
            vec_ref[_ROW_DCW + k:_ROW_DCW + k + 1, :] += _row_sum(ahead * x)
        dxr_head[...] = dxr[0:SUBLANES, :]
        dxg_ref[:, :D_RNN] = dx.astype(BF)
        dxg_ref[:, D_RNN:] = (dy * h * dgg).astype(BF)

    gw = (N_LRU_GROUPS, LRU_GROUP, LRU_GROUP)
    rev = lambda width: pl.BlockSpec((ts, width), lambda i: (tile(i), 0))
    body, more_specs, more = _with_after(body, 11, after)
    return pl.pallas_call(
        body, name="rnn_bwd", grid=(nt,),
        in_specs=[rev(D_RNN), rev(2 * D_RNN), rev(D_RNN), rev(D_RNN),
                  pl.BlockSpec((_PREV_ROWS, D_RNN), lambda i: (prev(i), 0)),
                  _full_spec(gw), _full_spec(gw),
                  _full_spec((1, D_RNN)), _full_spec((1, D_RNN)), _full_spec((1, D_RNN)),
                  _full_spec((CONV_WIDTH, D_RNN))] + more_specs,
        out_specs=[rev(2 * D_RNN), _full_spec(gw), _full_spec(gw), _full_spec((SUBLANES, D_RNN))],
        out_shape=[jax.ShapeDtypeStruct((s, 2 * D_RNN), BF), jax.ShapeDtypeStruct(gw, F32),
                   jax.ShapeDtypeStruct(gw, F32), jax.ShapeDtypeStruct((SUBLANES, D_RNN), F32)],
        scratch_shapes=[pltpu.VMEM((1, D_RNN), F32), pltpu.VMEM((1, D_RNN), F32),
                        pltpu.VMEM((SUBLANES, D_RNN), F32),
                        pltpu.VMEM((ts, D_RNN), F32), pltpu.VMEM((ts, D_RNN), F32),
                        pltpu.VMEM((ts, D_RNN), F32)],
        compiler_params=_params(("arbitrary",)),
    )(dya_pre, proj, xr_saved, hr, hr, wa, wx, ba, bx, sp, cw, *more)


def _inproj_bwd_call(dxg, duv, dgate, dx1, x, g1, w_in, layer, ts):
    s = x.shape[0]

    def body(dxg_ref, duv_ref, dgt_ref, dx1_ref, x_ref, g_ref, w_ref, dx_ref, dproj_ref, dg_ref):
        @pl.when(pl.program_id(0) == 0)
        def _():
            dg_ref[...] = jnp.zeros_like(dg_ref)

        dproj = jnp.concatenate([dxg_ref[...], duv_ref[...], dgt_ref[...]], axis=1)
        dproj_ref[...] = dproj
        dh = jnp.zeros((ts, D_MODEL), F32)
        for q in range(N_QUARTERS):
            dh = dh + _dot_nt(dproj[:, q * Q_IN:(q + 1) * Q_IN], w_ref[q])
        dx, dg = _rms_bwd(dh, x_ref[...], g_ref[...])
        dx_ref[...] = dx1_ref[...] + dx
        dg_ref[...] += _row_sum(dg)

    return pl.pallas_call(
        body, name="inproj_bwd", grid=(s // ts,),
        in_specs=[_tile_spec(ts, 2 * D_RNN), _tile_spec(ts, 2 * D_SGU), _tile_spec(ts, 2 * D_MODEL),
                  _tile_spec(ts, D_MODEL), _tile_spec(ts, D_MODEL), _full_spec((1, D_MODEL)),
                  pl.BlockSpec((None, N_QUARTERS, D_MODEL, Q_IN), lambda i: (layer, 0, 0, 0))],
        out_specs=[_tile_spec(ts, D_MODEL), _tile_spec(ts, D_IN), _full_spec((1, D_MODEL))],
        out_shape=[jax.ShapeDtypeStruct((s, D_MODEL), F32), jax.ShapeDtypeStruct((s, D_IN), BF),
                   jax.ShapeDtypeStruct((1, D_MODEL), F32)],
        compiler_params=_params(("arbitrary",)),
    )(dxg, duv, dgate, dx1, x, g1, w_in)


def _relu_sq(p):
    return jnp.square(jnp.maximum(p, 0))


def _wgrad_call(a, b, core, tm, tn, tk, col_blocked, name, a_fn=None):
    s, m = a.shape
    n = b.shape[1]
    r, cols = (m, n // N_QUARTERS) if col_blocked else (m // N_QUARTERS, n)
    r2 = r // 2
    per_tile = tm // r
    steps = s // tk
    assert per_tile > 0 or steps == 1

    def body(core_ref, a_ref, b_ref, keep_ref, send_ref, *acc):
        av = a_ref[...]
        if a_fn is not None:
            av = a_fn(av)
        prod = _dot_tn(av.astype(BF), b_ref[...].astype(BF))

        def emit(total):
            for h in range(2):
                @pl.when(core_ref[0] == h)
                def _():
                    for q in range(per_tile):
                        keep_ref[q] = total[q * r + h * r2:q * r + (h + 1) * r2].astype(BF)
                        send_ref[q] = total[q * r + (1 - h) * r2:q * r + (2 - h) * r2].astype(BF)

        if per_tile == 0:
            mine = pl.program_id(1) == core_ref[0]

            @pl.when(mine)
            def _():
                keep_ref[0] = prod.astype(BF)

            @pl.when(jnp.logical_not(mine))
            def _():
                send_ref[0] = prod.astype(BF)
        elif steps == 1:
            emit(prod)
        else:
            acc_ref, = acc
            step = pl.program_id(2)

            @pl.when(step == 0)
            def _():
                acc_ref[...] = prod

            @pl.when(jnp.logical_and(step > 0, step < steps - 1))
            def _():
                acc_ref[...] += prod

            @pl.when(step == steps - 1)
            def _():
                emit(acc_ref[...] + prod)

    if col_blocked:
        per_q = cols // tn
        out_spec = pl.BlockSpec((1, r2, tn), lambda j, i, k, c: (j // per_q, 0, j % per_q))
    else:
        out_spec = pl.BlockSpec((per_tile, r2, tn), lambda j, i, k, c: (i, 0, j))
    return pl.pallas_call(
        body, name=name,
        out_shape=[jax.ShapeDtypeStruct((N_QUARTERS, r2, cols), BF)] * 2,
        grid_spec=pltpu.PrefetchScalarGridSpec(
            num_scalar_prefetch=1, grid=(n // tn, m // tm, steps),
            in_specs=[pl.BlockSpec((tk, tm), lambda j, i, k, c: (k, i)),
                      pl.BlockSpec((tk, tn), lambda j, i, k, c: (k, j))],
            out_specs=[out_spec, out_spec],
            scratch_shapes=[] if steps == 1 else [pltpu.VMEM((tm, tn), F32)]),
        compiler_params=_params(("parallel", "parallel", "arbitrary")),
    )(core, a, b)


BIG = ("w_in", "w_up", "w_down", "w_branch_a", "w_branch_b", "w_out")


def _block_diag(w):
    w4 = w.reshape(N_LRU_GROUPS, HEADS_PER_GROUP, RNN_HEAD_DIM, RNN_HEAD_DIM)
    eye = jnp.eye(HEADS_PER_GROUP, dtype=w.dtype)
    return jnp.einsum("gjio,jk->gjiko", w4, eye).reshape(N_LRU_GROUPS, LRU_GROUP, LRU_GROUP)


def _block_diag_extract(d):
    d5 = d.reshape(N_LRU_GROUPS, HEADS_PER_GROUP, RNN_HEAD_DIM, HEADS_PER_GROUP, RNN_HEAD_DIM)
    blocks = [d5[:, j, :, j, :] for j in range(HEADS_PER_GROUP)]
    return jnp.stack(blocks, axis=1).reshape(RNN_HEADS, RNN_HEAD_DIM, RNN_HEAD_DIM)


def _sgu_mask():
    chunk = jnp.arange(SGU_BLOCK) // CHUNK
    return (chunk[:, None] >= chunk[None, :]).astype(F32)


def _layer_small(sm, l, core):
    row = lambda v: v.reshape(1, -1)
    return dict(
        core=core,
        g1=row(sm["norm_mix_g"][l]), g2=row(sm["norm_ffn_g"][l]),
        wa=_block_diag(sm["lru_w_a"][l]).astype(BF), wx=_block_diag(sm["lru_w_x"][l]).astype(BF),
        ba=row(sm["lru_b_a"][l]), bx=row(sm["lru_b_x"][l]),
        sp=row(jax.nn.softplus(-sm["lru_lambda"][l])),
        cw=sm["conv_w"][l] if "conv_w" in sm else None, cb=row(sm["conv_b"][l]),
        wm=(sm["sgu_w_s"][l] * _sgu_mask()).astype(BF),
        bsb=jnp.broadcast_to(sm["sgu_b_s"][l][:, :, None], (SGU_GROUPS, SGU_BLOCK, SGU_BLOCK)),
        lg=row(sm["sgu_ln_g"][l]), lb=row(sm["sgu_ln_b"][l]),
    )


def _layer_fwd_mix(x, big, p, ts, h=None, before_sgu=None, proj=None):
    if h is None:
        h = _norm_call(x, p["g1"], ts)
    if proj is None:
        proj = _inproj_call(h, big["w_in"], 0, 2 * ts)
    xr, hr, ya_pre = _rnn_fwd_call(proj, p["wa"], p["wx"], p["ba"], p["bx"], p["sp"], p["cw"], p["cb"], ts)
    yb_pre = _sgu_fwd_call(proj, p["wm"], p["bsb"], p["lg"], p["lb"], ts,
                           None if before_sgu is None else before_sgu(ya_pre))
    return dict(p=p, x=x, h=h, proj=proj, xr=xr, hr=hr, ya_pre=ya_pre, yb_pre=yb_pre)


def _layer_fwd_out(sv, big, ts):
    x1, ya, yb, merged, h2 = _merge_call(sv["x"], sv["proj"], sv["ya_pre"], sv["yb_pre"], big["w_branch_a"],
                                         big["w_branch_b"], big["w_out"], sv["p"]["g2"], 0, ts)
    x2, pre = _ffn_call(x1, h2, big["w_up"], big["w_down"], 0, ts)
    sv.update(x1=x1, ya=ya, yb=yb, merged=merged, h2=h2, pre=pre)
    return x2


def _layer_bwd_ffn(dx, sv, big, ts, after=None):
    p = sv["p"]
    dx1, dpre, dg2, dx_bf, sv["dx1_bf"] = _ffn_bwd_call(dx, sv["pre"], sv["x1"], p["g2"], big["w_up"],
                                                       big["w_down"], 0, ts, after)
    tk = dx.shape[0]
    gb = dict(
        w_down=_wgrad_call(sv["pre"], dx_bf, p["core"], Q_FF, D_MODEL, tk, False, "wgrad_down", a_fn=_relu_sq),
        w_up=_wgrad_call(sv["h2"], dpre, p["core"], D_MODEL, Q_FF, tk, True, "wgrad_up"))
    return dx1, gb, dict(norm_ffn_g=dg2[0])


def _layer_bwd_merge(dx1, sv, big, ts, after=None):
    tk = dx1.shape[0]
    core = sv["p"]["core"]
    dya, dyb, dgate, dya_pre, dyb_pre = _merge_bwd_call(
        dx1, sv["proj"], sv["ya"], sv["yb"], big["w_branch_a"], big["w_branch_b"], big["w_out"], 0, ts, after)
    gb = dict(
        w_out=_wgrad_call(sv["merged"], sv["dx1_bf"], core, D_MODEL, D_MODEL, tk, False, "wgrad_out"),
        w_branch_a=_wgrad_call(sv["ya_pre"], dya, core, D_RNN, D_MODEL // 2, tk, False, "wgrad_branch_a"),
        w_branch_b=_wgrad_call(sv["yb_pre"], dyb, core, D_SGU, D_MODEL, tk, False, "wgrad_branch_b"))
    return (dgate, dya_pre, dyb_pre), gb


def _layer_bwd_branches(dx1, merge_out, sv, big, lam, ts, after=None, after_sgu=None):
    p = sv["p"]
    tk = dx1.shape[0]
    dgate, dya_pre, dyb_pre = merge_out
    gb = {}
    duv, dws, dbs, dlg, dlb = _sgu_bwd_call(dyb_pre, sv["proj"], p["wm"], p["bsb"], _sgu_mask(), p["lg"], p["lb"],
                                            ts, after)
    dxg, dwa, dwx, vec = _rnn_bwd_call(dya_pre, sv["proj"], sv["xr"], sv["hr"], p["wa"], p["wx"], p["ba"], p["bx"],
                                       p["sp"], p["cw"], ts, None if after_sgu is None else after_sgu(duv))
    dx, dproj, dg1 = _inproj_bwd_call(dxg, duv, dgate, dx1, sv["x"], p["g1"], big["w_in"], 0, ts)
    gb["w_in"] = _wgrad_call(sv["h"], dproj, p["core"], D_MODEL // 2, Q_IN, tk, True, "wgrad_in")
    gs = dict(
        norm_mix_g=dg1[0], conv_w=vec[_ROW_DCW:_ROW_DCW + CONV_WIDTH], conv_b=vec[_ROW_DCB],
        lru_w_a=_block_diag_extract(dwa), lru_w_x=_block_diag_extract(dwx),
        lru_b_a=vec[_ROW_DBA].reshape(RNN_HEADS, RNN_HEAD_DIM), lru_b_x=vec[_ROW_DBX].reshape(RNN_HEADS, RNN_HEAD_DIM),
        lru_lambda=-vec[_ROW_DSP] * jax.nn.sigmoid(-lam),
        sgu_ln_g=dlg[0], sgu_ln_b=dlb[0], sgu_w_s=dws, sgu_b_s=dbs.T)
    return dx, gb, gs


def _local_step(x, target, big, sm, ts):
    saved = []
    core = jnp.zeros((1,), jnp.int32)
    for l in range(DEPTH):
        sv = _layer_fwd_mix(x, big[l], _layer_small(sm, l, core), ts)
        x = _layer_fwd_out(sv, big[l], ts)
        saved.append(sv)
    dx, loss, dgf = _loss_call(x, target, sm["final_norm_g"].reshape(1, -1), ts)
    gb, gs = [None] * DEPTH, [None] * DEPTH
    for l in reversed(range(DEPTH)):
        dx1, gb_ffn, gs_ffn = _layer_bwd_ffn(dx, saved[l], big[l], ts)
        merge_out, gb_merge = _layer_bwd_merge(dx1, saved[l], big[l], ts)
        dx, gb_mix, gs_mix = _layer_bwd_branches(dx1, merge_out, saved[l], big[l], sm["lru_lambda"][l], ts)
        gb[l] = {**gb_ffn, **gb_merge, **gb_mix}
        gs[l] = {**gs_ffn, **gs_mix}
    gs = {k: jnp.stack([g[k] for g in gs]) for k in gs[0]}
    gs["final_norm_g"] = dgf[0]
    return loss, dx, gb, gs


EW_VMEM_BYTES = 24 * 1024 * 1024


def _row_block(rows, cols, bytes_per_elem):
    for br in range(min(rows, EW_VMEM_BYTES // (2 * bytes_per_elem * cols)), 0, -1):
        if rows % br == 0 and br % 16 == 0:
            return br
    return rows


def _ew_call(fn, name, operands, outputs, slabs=1, sel=None, into=None, after=None):
    if into is not None and not isinstance(into, (list, tuple)):
        into = [into]
    rows, cols = outputs[0][0].shape[2:]
    br = _row_block(rows, cols, sum(jnp.dtype(a.dtype).itemsize for a, _ in operands + outputs))
    n_in = len(operands)

    def pick(tok, g, s):
        if callable(tok):
            return tok(g, s)
        if tok == "g":
            return g
        if isinstance(tok, tuple):
            return s[tok[1]]
        return tok

    def spec(idx):
        return pl.BlockSpec((None, None, br, cols),
                            lambda g, i, s, idx=idx: (pick(idx[0], g, s), pick(idx[1], g, s), i, 0))

    if sel is None:
        sel = jnp.zeros((1,), jnp.int32)
    in_specs = [spec(idx) for _, idx in operands]
    arrays = [a for a, _ in operands]
    aliases = {}
    for j, buf in enumerate(into or ()):
        in_specs.append(pl.BlockSpec(memory_space=pl.ANY))
        arrays.append(buf)
        aliases[1 + n_in + j] = j
    if after is not None:
        in_specs.append(pl.BlockSpec(memory_space=pl.ANY))
        arrays.append(after)

    def body(sel_ref, *refs):
        outs = fn(*[r[...] for r in refs[:n_in]])
        for o_ref, o in zip(refs[len(arrays):], outs):
            o_ref[...] = o.astype(o_ref.dtype)

    return pl.pallas_call(
        body, name=name, out_shape=[s for s, _ in outputs],
        grid_spec=pltpu.PrefetchScalarGridSpec(
            num_scalar_prefetch=1, grid=(slabs, rows // br),
            in_specs=in_specs,
            out_specs=[spec(idx) for _, idx in outputs]),
        input_output_aliases=aliases,
        compiler_params=_params(("parallel", "parallel")),
    )(sel, *arrays)


def _as4(a):
    return a.reshape((1,) * (4 - a.ndim) + a.shape)


def _adamw(w, g, m, v):
    m = ADAM_B1 * m + (1.0 - ADAM_B1) * g
    v = ADAM_B2 * v + (1.0 - ADAM_B2) * jnp.square(g)
    m_hat = m / (1.0 - ADAM_B1 ** ADAM_STEP)
    v_hat = v / (1.0 - ADAM_B2 ** ADAM_STEP)
    delta = -ADAM_LR * (m_hat / (jnp.sqrt(v_hat) + ADAM_EPS) + ADAM_WD * w)
    return delta, m, v


def _small_adamw_call(ws, gs, ms, vs):
    n = len(ws)

    def body(*refs):
        for k in range(n):
            w, g, m, v = (refs[j * n + k][...] for j in range(4))
            outs = _adamw(w, g, m, v)
            for j in range(3):
                refs[(4 + j) * n + k][...] = outs[j]

    shapes = [jax.ShapeDtypeStruct(w.shape, F32) for w in ws]
    outs = pl.pallas_call(
        body, name="adamw_small", out_shape=shapes * 3,
        in_specs=[pl.BlockSpec(memory_space=pltpu.VMEM)] * (4 * n),
        out_specs=[pl.BlockSpec(memory_space=pltpu.VMEM)] * (3 * n),
        compiler_params=_params(),
    )(*ws, *gs, *ms, *vs)
    return outs[:n], outs[n:2 * n], outs[2 * n:]


ANY = pl.BlockSpec(memory_space=pl.ANY)


def _place():
    x, y, c = lax.axis_index("x"), lax.axis_index("y"), lax.axis_index("c")
    chips = [(1 - x, y), (x, 1 - y), (1 - x, 1 - y)]
    return x, y, c, chips


def _remote(src, dst, send_sem, recv_sem, to):
    return pltpu.make_async_remote_copy(src_ref=src, dst_ref=dst, send_sem=send_sem, recv_sem=recv_sem,
                                        device_id=to, device_id_type=MESH)


def _sibling_send_call(items):
    n = len(items)

    def body(*refs):
        src, out = refs[:n], refs[n:2 * n]
        send_sems, recv_sems = refs[2 * n:]
        x, y, c, _ = _place()
        copies = [_remote(src[w], out[w], send_sems.at[w], recv_sems.at[w], (x, y, 1 - c)) for w in range(n)]
        for cp in copies:
            cp.start()
        for cp in copies:
            cp.wait()

    return pl.pallas_call(
        body, name="grads_to_sibling",
        out_shape=[jax.ShapeDtypeStruct(a.shape, a.dtype) for a in items],
        in_specs=[ANY] * n, out_specs=[ANY] * n,
        scratch_shapes=[pltpu.SemaphoreType.DMA((n,)), pltpu.SemaphoreType.DMA((n,))],
        compiler_params=_params(vmem=False, has_side_effects=True),
    )(*items)


def _sibling_inplace_call(name, bufs, slabs, n_pairs):
    n = len(bufs)

    def body(*refs):
        out = refs[n:2 * n]
        send_sems, recv_sems = refs[2 * n:]
        x, y, c, _ = _place()
        sibling = (x, y, 1 - c)
        pairs = [pair for w, ref in enumerate(out) for pair in slabs(ref, c, w)]
        sends = [_remote(s, s, send_sems.at[k], recv_sems.at[k], sibling) for k, (s, _) in enumerate(pairs)]
        for cp in sends:
            cp.start()
        for k, (_, r) in enumerate(pairs):
            _remote(r, r, send_sems.at[k], recv_sems.at[k], sibling).wait_recv()
        for cp in sends:
            cp.wait_send()

    return pl.pallas_call(
        body, name=name,
        out_shape=[jax.ShapeDtypeStruct(a.shape, a.dtype) for a in bufs],
        in_specs=[ANY] * n, out_specs=[ANY] * n,
        input_output_aliases={w: w for w in range(n)},
        scratch_shapes=[pltpu.SemaphoreType.DMA((n_pairs,)), pltpu.SemaphoreType.DMA((n_pairs,))],
        compiler_params=_params(vmem=False, has_side_effects=True),
    )(*bufs)


HBM_SPEC = pl.BlockSpec(memory_space=pltpu.HBM)
SEM_SPEC = pl.BlockSpec(memory_space=pltpu.SEMAPHORE)
DATAFLOW_EFFECT = pltpu.SideEffectType.DATAFLOW_SIDE_EFFECTING


def _exchange_start(name, bufs, copies, n_copies, after):
    return _exchange_start_many(name, [(bufs, copies, n_copies)], after)[0]


def _exchange_start_many(name, groups, after):
    sizes = [len(bufs) for bufs, _, _ in groups]
    starts = [sum(sizes[:g]) for g in range(len(groups))]
    n, n_sems = sum(sizes), 2 * len(groups)

    def body(*refs):
        ins, sems, token = refs[:n], refs[n + 1:n + 1 + n_sems], refs[-1]
        for g, (_, copies, _) in enumerate(groups):
            send_sems, recv_sems = sems[2 * g], sems[2 * g + 1]
            for k, (src, dst, to) in enumerate(copies(ins[starts[g]:starts[g] + sizes[g]])):
                _remote(src, dst, send_sems.at[k], recv_sems.at[k], to).start()
        token[...] = jnp.zeros_like(token)

    every = [b for bufs, _, _ in groups for b in bufs]
    outs = pl.pallas_call(
        body, name=name,
        out_shape=(*[pltpu.SemaphoreType.DMA((c,)) for _, _, c in groups for _ in range(2)],
                   *[pltpu.HBM(b.shape, b.dtype) for b in every], jax.ShapeDtypeStruct((SUBLANES, 128), F32)),
        in_specs=[HBM_SPEC] * n + [ANY],
        out_specs=(*[SEM_SPEC] * n_sems, *[HBM_SPEC] * n, pl.BlockSpec(memory_space=pltpu.VMEM)),
        input_output_aliases={w: w + n_sems for w in range(n)},
        compiler_params=pltpu.CompilerParams(has_side_effects=DATAFLOW_EFFECT),
    )(*[pltpu.with_memory_space_constraint(b, pltpu.HBM) for b in every], after)
    thru = outs[n_sems:n_sems + n]
    return [(outs[2 * g], outs[2 * g + 1], list(thru[starts[g]:starts[g] + sizes[g]]), outs[-1])
            for g in range(len(groups))]


def _exchange_wait(name, send_sems, recv_sems, bufs, copies, after):
    n = len(bufs)

    def body(*refs):
        ins, send_sems, recv_sems = refs[:n], refs[n], refs[n + 1]
        for k, (src, dst, to) in enumerate(copies(ins)):
            cp = _remote(src, dst, send_sems.at[k], recv_sems.at[k], to)
            cp.wait_send()
            cp.wait_recv()

    return pl.pallas_call(
        body, name=name,
        out_shape=[pltpu.HBM(b.shape, b.dtype) for b in bufs],
        in_specs=[HBM_SPEC] * n + [SEM_SPEC, SEM_SPEC, ANY],
        out_specs=[HBM_SPEC] * n,
        input_output_aliases={w: w for w in range(n)},
        compiler_params=pltpu.CompilerParams(has_side_effects=DATAFLOW_EFFECT),
    )(*bufs, send_sems, recv_sems, after)


def _gather_copies(refs):
    x, y, c, chips = _place()
    mine = 2 * (2 * x + y) + c
    return [(ref.at[mine], ref.at[mine], (qx, qy, c)) for ref in refs for qx, qy in chips]


def _forward_copies(refs):
    x, y, c, chips = _place()
    return [(ref.at[2 * (2 * qx + qy) + c], ref.at[2 * (2 * qx + qy) + c], (x, y, 1 - c))
            for ref in refs for qx, qy in chips]


def _gather_forward_slabs(ref, c, w):
    x, y, _, chips = _place()
    return [(ref.at[2 * (2 * qx + qy) + c], ref.at[2 * (2 * qx + qy) + 1 - c]) for qx, qy in chips]


def _device_peers():
    x, y, c, _ = _place()
    return 4 * x + 2 * y + c, [(k, (x ^ ((k >> 2) & 1), y ^ ((k >> 1) & 1), c ^ (k & 1))) for k in range(1, 8)]


def _small_scatter_copies(refs):
    me, peers = _device_peers()
    return [(refs[0].at[me ^ k], refs[1].at[me], to) for k, to in peers]


def _small_spread_copies(refs):
    me, peers = _device_peers()
    return [(refs[0].at[me], refs[0].at[me], to) for _, to in peers]


def _sibling_copies(refs):
    n = len(refs) // 2
    x, y, c, _ = _place()
    return [(refs[w], refs[n + w], (x, y, 1 - c)) for w in range(n)]


def _owner_copies(refs):
    n = len(refs) // 2
    x, y, c, chips = _place()
    return [(refs[w].at[2 * qx + qy], refs[n + w].at[j], (qx, qy, c))
            for w in range(n) for j, (qx, qy) in enumerate(chips)]


N_DEVICES = 8
SMALL_ROWS = 616


SMALL = ("norm_mix_g", "conv_w", "conv_b", "lru_w_a", "lru_b_a", "lru_w_x", "lru_b_x", "lru_lambda",
         "sgu_ln_g", "sgu_ln_b", "sgu_w_s", "sgu_b_s", "norm_ffn_g", "final_norm_g")
WEIGHTS = ("norm_mix_g", "w_in", "conv_w", "conv_b", "lru_w_a", "lru_b_a", "lru_w_x", "lru_b_x", "lru_lambda",
           "sgu_ln_g", "sgu_ln_b", "sgu_w_s", "sgu_b_s", "w_branch_a", "w_branch_b", "w_out", "norm_ffn_g",
           "w_up", "w_down", "final_norm_g")
PACK_ALIGN = SUBLANES * 128


PACKED = SMALL + ("loss",)


def _pack_small(gs):
    parts = []
    for k in PACKED:
        flat = gs[k].reshape(-1)
        parts.append(jnp.pad(flat, (0, -flat.size % PACK_ALIGN)))
    flat = jnp.concatenate(parts)
    flat = jnp.pad(flat, (0, N_DEVICES * SMALL_ROWS * 128 - flat.size))
    return flat.reshape(N_DEVICES, SMALL_ROWS, 128)


def _unpack_small(buf, like):
    flat = buf.reshape(-1)
    out, off = {}, 0
    for k in PACKED:
        size = like[k].size
        out[k] = flat[off:off + size].reshape(like[k].shape)
        off += size + (-size % PACK_ALIGN)
    return out


def _as_rows(a):
    return a.reshape(-1, a.shape[-1])


def kernel(x, norm_mix_g, w_in, conv_w, conv_b, lru_w_a, lru_b_a, lru_w_x, lru_b_x, lru_lambda, sgu_ln_g, sgu_ln_b, sgu_w_s, sgu_b_s, w_branch_a, w_branch_b, w_out, norm_ffn_g, w_up, w_down, final_norm_g, loss_target, m_norm_mix_g, m_w_in, m_conv_w, m_conv_b, m_lru_w_a, m_lru_b_a, m_lru_w_x, m_lru_b_x, m_lru_lambda, m_sgu_ln_g, m_sgu_ln_b, m_sgu_w_s, m_sgu_b_s, m_w_branch_a, m_w_branch_b, m_w_out, m_norm_ffn_g, m_w_up, m_w_down, m_final_norm_g, v_norm_mix_g, v_w_in, v_conv_w, v_conv_b, v_lru_w_a, v_lru_b_a, v_lru_w_x, v_lru_b_x, v_lru_lambda, v_sgu_ln_g, v_sgu_ln_b, v_sgu_w_s, v_sgu_b_s, v_w_branch_a, v_w_branch_b, v_w_out, v_norm_ffn_g, v_w_up, v_w_down, v_final_norm_g):
    w = dict(norm_mix_g=norm_mix_g, w_in=w_in, conv_w=conv_w, conv_b=conv_b, lru_w_a=lru_w_a, lru_b_a=lru_b_a,
             lru_w_x=lru_w_x, lru_b_x=lru_b_x, lru_lambda=lru_lambda, sgu_ln_g=sgu_ln_g, sgu_ln_b=sgu_ln_b,
             sgu_w_s=sgu_w_s, sgu_b_s=sgu_b_s, w_branch_a=w_branch_a, w_branch_b=w_branch_b, w_out=w_out,
             norm_ffn_g=norm_ffn_g, w_up=w_up, w_down=w_down, final_norm_g=final_norm_g)
    m = dict(norm_mix_g=m_norm_mix_g, w_in=m_w_in, conv_w=m_conv_w, conv_b=m_conv_b, lru_w_a=m_lru_w_a,
             lru_b_a=m_lru_b_a, lru_w_x=m_lru_w_x, lru_b_x=m_lru_b_x, lru_lambda=m_lru_lambda,
             sgu_ln_g=m_sgu_ln_g, sgu_ln_b=m_sgu_ln_b, sgu_w_s=m_sgu_w_s, sgu_b_s=m_sgu_b_s,
             w_branch_a=m_w_branch_a, w_branch_b=m_w_branch_b, w_out=m_w_out, norm_ffn_g=m_norm_ffn_g,
             w_up=m_w_up, w_down=m_w_down, final_norm_g=m_final_norm_g)
    v = dict(norm_mix_g=v_norm_mix_g, w_in=v_w_in, conv_w=v_conv_w, conv_b=v_conv_b, lru_w_a=v_lru_w_a,
             lru_b_a=v_lru_b_a, lru_w_x=v_lru_w_x, lru_b_x=v_lru_b_x, lru_lambda=v_lru_lambda,
             sgu_ln_g=v_sgu_ln_g, sgu_ln_b=v_sgu_ln_b, sgu_w_s=v_sgu_w_s, sgu_b_s=v_sgu_b_s,
             w_branch_a=v_w_branch_a, w_branch_b=v_w_branch_b, w_out=v_w_out, norm_ffn_g=v_norm_ffn_g,
             w_up=v_w_up, w_down=v_w_down, final_norm_g=v_final_norm_g)
    core = lax.axis_index("c")
    chip = 2 * lax.axis_index("x") + lax.axis_index("y")
    sel = jnp.stack([core, 1 - core, chip, 2 * chip + core]).astype(jnp.int32)
    this_core, this_chip = ("sel", 0), ("sel", 2)
    sds = jax.ShapeDtypeStruct

    ts = TOKEN_TILE

    def after_all(arrays):
        return jnp.stack([a[(0,) * a.ndim].astype(F32) for a in arrays])

    halves = {k: (w[k].shape[1] // 2, w[k].shape[2]) for k in BIG}

    def half_view(k, a):
        return a.reshape((2 * N_QUARTERS,) + halves[k])

    def full_view(k, a):
        if k == "conv_w":
            return a.reshape(N_QUARTERS, DEPTH, CONV_WIDTH, -1).transpose(1, 2, 0, 3).reshape(DEPTH, CONV_WIDTH, D_RNN)
        r2, cols = halves[k]
        if k in ("w_in", "w_up"):
            return a.reshape(1, N_QUARTERS, 2 * r2, cols)
        return a.reshape(1, 2 * N_QUARTERS * r2, cols)

    layer_bufs = [{}, {}]

    def cast_weights(k, after):
        _, r, cols = w[k].shape
        w4 = w[k].reshape(DEPTH, 1, r, cols)
        outs = _ew_call(lambda a, b: (a, b), "cast_weights", [(w4, (0, 0)), (w4, (1, 0))],
                        [(sds((1, N_QUARTERS, r, cols), BF), (0, this_chip))] * DEPTH, 1, sel, after=after)
        for l in range(DEPTH):
            layer_bufs[l][k] = half_view(k, outs[l])

    conv_buf = lax.dynamic_update_slice_in_dim(
        jnp.zeros((N_QUARTERS, DEPTH) + conv_w.shape[1:], F32), conv_w[None], chip, axis=0)
    layer_bufs[0]["conv_w"] = conv_buf.reshape((2 * N_QUARTERS,) + conv_w.shape[1:])
    sm = {k: w[k] for k in SMALL if k != "conv_w"}

    def gather_start(tag, l, keys, after):
        bufs = [layer_bufs[l][k] for k in keys]
        return _exchange_start(f"gather_start_{tag}", bufs, _gather_copies, 3 * len(keys), after)

    def gather_finish(tag, keys, started, after):
        send_sems, recv_sems, thru, _ = started
        landed = _exchange_wait(f"gather_wait_{tag}", send_sems, recv_sems, thru, _gather_copies, after)
        landed = _sibling_inplace_call("gather_forward", landed, _gather_forward_slabs, 3 * len(keys))
        return {k: full_view(k, a) for k, a in zip(keys, landed)}

    first, rest = ("w_in",), tuple(k for k in BIG if k != "w_in")
    cast_weights("w_in", None)
    started_a = gather_start("0a", 0, first + ("conv_w",), sel)
    for k in rest:
        cast_weights(k, started_a[3])
    started_b, started_c, started_d = _exchange_start_many(
        "gather_start_rest",
        [([layer_bufs[l][k] for k in keys], _gather_copies, 3 * len(keys)) for l, keys in ((0, rest), (1, first), (1, rest))],
        started_a[3])

    def arrives(tag, keys, started):
        state = {}

        def hook(after):
            landed = _exchange_wait(f"gather_wait_{tag}", started[0], started[1], started[2], _gather_copies, after)
            state["forward"] = _exchange_start(f"forward_start_{tag}", landed, _forward_copies, 3 * len(keys), after)
            return state["forward"][3]

        def finish(after):
            send_sems, recv_sems, thru, _ = state["forward"]
            done = _exchange_wait(f"forward_wait_{tag}", send_sems, recv_sems, thru, _forward_copies, after)
            return {k: full_view(k, a) for k, a in zip(keys, done)}

        return hook, finish

    p0, p1 = _layer_small(sm, 0, sel[0:1]), _layer_small(sm, 1, sel[0:1])
    h0 = _norm_call(x[0], p0["g1"], ts)
    proj_own = _inproj_part_call(h0, full_view("w_in", started_a[2][0]), 2 * ts, sel[2:3], 0, 1)
    ready = after_all([started_d[3], proj_own] + [p[k] for p in (p0, p1) for k in ("wa", "wx", "wm")])
    big0 = gather_finish("0a", first + ("conv_w",), started_a, ready)
    for l, p in enumerate((p0, p1)):
        p["cw"] = big0["conv_w"][l]
    proj0 = _inproj_part_call(h0, big0["w_in"], 2 * ts, sel[2:3], 1, N_QUARTERS - 1, proj_own)
    hook, finish = arrives("0b", rest, started_b)
    sv0 = _layer_fwd_mix(x[0], big0, p0, ts, h0, hook, proj0)
    big0.update(finish(sv0["yb_pre"]))
    x_mid = _layer_fwd_out(sv0, big0, ts)
    hook, finish = arrives("1a", first, started_c)
    h1 = _norm_call(x_mid, p1["g1"], ts, hook(x_mid))
    big1 = finish(h1)
    hook, finish = arrives("1b", rest, started_d)
    sv1 = _layer_fwd_mix(x_mid, big1, p1, ts, h1, hook)
    big1.update(finish(sv1["yb_pre"]))
    x_out = _layer_fwd_out(sv1, big1, ts)
    dx, loss, dgf = _loss_call(x_out, loss_target[0], final_norm_g.reshape(1, -1), ts)

    def pair_start(tag, gb, after):
        sends = [gb[k][1] for k in gb]
        zones = [lax.empty(a.shape, BF) for a in sends]
        return _exchange_start(f"pair_start_{tag}", sends + zones, _sibling_copies, len(sends), after)

    def reduce_start(tag, gb, after, pair=None):
        keys = tuple(gb)
        if pair is None:
            from_sibling = _sibling_send_call([gb[k][1] for k in keys])
        else:
            done = _exchange_wait(f"pair_wait_{tag}", pair[0], pair[1], pair[2], _sibling_copies, after)
            from_sibling = done[len(keys):]
        sums = [
            _ew_call(lambda a, b: (a.astype(F32) + b.astype(F32),), "pair_sum", [(gb[k][0][None], (0, "g")), (r[None], (0, "g"))],
                     [(sds((1,) + r.shape, BF), (0, "g"))], N_QUARTERS)[0][0]
            for k, r in zip(keys, from_sibling)]
        zones = [lax.empty((3,) + a.shape[1:], BF) for a in sums]
        started = _exchange_start(f"reduce_start_{tag}", sums + zones, _owner_copies, 3 * len(keys), after)
        return keys, started

    def reduce_finish(tag, l, keys_started, after, reduced):
        keys, (send_sems, recv_sems, thru, _) = keys_started
        done = _exchange_wait(f"reduce_wait_{tag}", send_sems, recv_sems, thru, _owner_copies, after)
        sums, zones = done[:len(keys)], done[len(keys):]
        for i, k in enumerate(keys):
            r2, cols = halves[k]
            reduced[k] = _ew_call(
                lambda a, b, c, d: (((a.astype(F32) + b.astype(F32)) + c.astype(F32)) + d.astype(F32),),
                "quarter_sum", [(sums[i][None], (0, this_chip))] + [(zones[i][None], (0, j)) for j in range(3)],
                [(sds((DEPTH, 2, r2, cols), F32), (l, this_core))], 1, sel, into=reduced.get(k))[0]

    dx1, gb_ffn, gs1 = _layer_bwd_ffn(dx, sv1, big1, ts)
    merge_out, gb_merge = _layer_bwd_merge(dx1, sv1, big1, ts)
    dx_mid, gb_in, gs1_mix = _layer_bwd_branches(dx1, merge_out, sv1, big1, lru_lambda[1], ts)
    gb_1 = {**gb_ffn, **gb_merge, **gb_in}
    pair_1 = pair_start("1", gb_1, dx_mid)
    dx1, gb_ffn, gs0 = _layer_bwd_ffn(dx_mid, sv0, big0, ts, pair_1[3])
    exchange_1 = reduce_start("1", gb_1, dx1, pair_1)
    pair_0a = pair_start("0a", gb_ffn, exchange_1[1][3])
    merge_out, gb_merge = _layer_bwd_merge(dx1, sv0, big0, ts, pair_0a[3])
    exchange_0a = reduce_start("0a", gb_ffn, merge_out[0], pair_0a)
    pair_0b = pair_start("0b", gb_merge, exchange_0a[1][3])
    started_0b = {}

    def after_sgu(duv):
        started_0b["exchange"] = reduce_start("0b", gb_merge, duv, pair_0b)
        return started_0b["exchange"][1][3]

    grad_x, gb_in, gs0_mix = _layer_bwd_branches(dx1, merge_out, sv0, big0, lru_lambda[0], ts, pair_0b[3],
                                                 after_sgu)
    exchange_0b = started_0b["exchange"]
    exchange_0c = reduce_start("0c", gb_in, exchange_0b[1][3])
    layer_gs = [{**gs0, **gs0_mix}, {**gs1, **gs1_mix}]
    gs = {k: jnp.stack([g[k] for g in layer_gs]) for k in layer_gs[0]}
    gs["final_norm_g"] = dgf[0]
    gs["loss"] = loss[0, 0:1]

    me = ("sel", 3)
    piece = (1, N_DEVICES, SMALL_ROWS, 128)
    packed = _pack_small(gs).reshape(piece)
    scatter = _exchange_start("small_scatter_start", [packed[0], lax.empty(piece[1:], F32)], _small_scatter_copies,
                              N_DEVICES - 1, exchange_0c[1][3])
    reduced = {}
    reduce_finish("1", 1, exchange_1, scatter[3], reduced)
    reduce_finish("0a", 0, exchange_0a, reduced["w_in"], reduced)
    reduce_finish("0b", 0, exchange_0b, reduced["w_down"], reduced)

    def swap_slabs(ref, c, i):
        layers = (1,) if BIG[i] == "w_in" else range(DEPTH)
        return [(ref.at[l, c], ref.at[l, 1 - c]) for l in layers]

    swapped = dict(zip(BIG, _sibling_inplace_call("grads_swap_halves", [reduced[k] for k in BIG], swap_slabs,
                                                  DEPTH * len(BIG) - 1)))

    def adamw_layers(k, grad, layer, into, after=None):
        if layer is None:
            views = [_as4(_as_rows(a)) for a in (w[k], grad, m[k], v[k])]
            idx = (0, 0)
        else:
            views = [a.reshape((1,) + w[k].shape) for a in (w[k], grad, m[k], v[k])]
            idx = (0, layer)
        return _ew_call(_adamw, "adamw_big", [(a, idx) for a in views], [(sds(views[0].shape, F32), idx)] * 3,
                        into=into, after=after)

    updated, last_update = {}, None
    for k in BIG:
        updated[k] = adamw_layers(k, swapped[k], 1 if k == "w_in" else None, None, last_update)
        last_update = updated[k][0]
    scattered = _exchange_wait("small_scatter_wait", scatter[0], scatter[1], scatter[2], _small_scatter_copies,
                               last_update)
    summed = _ew_call(
        lambda *parts: (functools.reduce(lambda a, b: a + b, parts),), "small_sum",
        [(scattered[0][None], (0, me))]
        + [(scattered[1][None], (0, lambda g, s, k=k: s[3] ^ k)) for k in range(1, N_DEVICES)],
        [(sds(piece, F32), (0, me))], 1, sel)[0]
    spread = _exchange_start("small_spread_start", [summed[0]], _small_spread_copies, N_DEVICES - 1, summed)
    reduced["w_in"] = swapped["w_in"]
    reduce_finish("0c", 0, exchange_0c, spread[3], reduced)
    last = _sibling_inplace_call("grads_swap_last", [reduced["w_in"]],
                                 lambda ref, c, i: [(ref.at[0, c], ref.at[0, 1 - c])], 1)[0]
    swapped["w_in"] = last
    updated["w_in"] = adamw_layers("w_in", last, 0, updated["w_in"])
    grads_big = {k: swapped[k].reshape(w[k].shape) for k in BIG}
    delta, new_m, new_v = ({k: updated[k][j].reshape(w[k].shape) for k in BIG} for j in range(3))
    gathered_small = _exchange_wait("small_spread_wait", spread[0], spread[1], spread[2], _small_spread_copies,
                                    updated["w_in"][0])[0]

    like = {k: jax.ShapeDtypeStruct(gs[k].shape, F32) for k in SMALL}
    like["loss"] = jax.ShapeDtypeStruct((1,), F32)
    grads_small = _unpack_small(gathered_small, like)
    total = grads_small.pop("loss")[0]
    conv_q = grads_small["conv_w"].reshape(DEPTH, CONV_WIDTH, N_QUARTERS, D_RNN // N_QUARTERS)
    grads_small["conv_w"] = lax.dynamic_index_in_dim(conv_q, chip, axis=2, keepdims=False)
    at_least_2d = lambda a: a.reshape(1, -1) if a.ndim == 1 else a
    outs = _small_adamw_call(*[[at_least_2d(d[k]) for k in SMALL] for d in (w, grads_small, m, v)])
    for d, o in zip((delta, new_m, new_v), outs):
        for k, a in zip(SMALL, o):
            d[k] = a.reshape(w[k].shape)

    grads = {**grads_big, **grads_small}
    return (total, grad_x[None], *[grads[k] for k in WEIGHTS], *[delta[k] for k in WEIGHTS],
            *[new_m[k] for k in WEIGHTS], *[new_v[k] for k in WEIGHTS])
```

```python
import functools
import math

import jax
import jax.numpy as jnp
from jax import lax
from jax.experimental import pallas as pl
from jax.experimental.pallas import tpu as pltpu

F32 = jnp.float32
BF = jnp.bfloat16

DEPTH = 2
D_MODEL = 1024
D_RNN = 1280
D_SGU = 1024
D_FF = 4096
D_IN = 2 * D_RNN + 2 * D_SGU + 2 * D_MODEL
N_QUARTERS = 4
Q_IN = D_IN // N_QUARTERS
Q_FF = D_FF // N_QUARTERS
RNN_HEADS = 20
RNN_HEAD_DIM = 64
LRU_GROUP = 256
N_LRU_GROUPS = D_RNN // LRU_GROUP
HEADS_PER_GROUP = LRU_GROUP // RNN_HEAD_DIM
CONV_WIDTH = 4
LRU_C = 8.0
SGU_GROUPS = 8
SGU_BLOCK = 128
CHUNK = 64
EPS = 1e-6

ADAM_LR = 0.001
ADAM_B1 = 0.9
ADAM_B2 = 0.999
ADAM_EPS = 1e-08
ADAM_WD = 0.01
ADAM_STEP = 10

SUBLANES = 8
TOKEN_TILE = 512
VMEM_LIMIT_BYTES = 56 * 1024 * 1024

MESH = pl.DeviceIdType.MESH


def _params(semantics=None, vmem=True, **kw):
    return pltpu.CompilerParams(
        dimension_semantics=semantics,
        vmem_limit_bytes=VMEM_LIMIT_BYTES if vmem else None,
        **kw,
    )


def _dot(a, b):
    return jnp.dot(a, b, preferred_element_type=F32)


def _dot_nt(a, b):
    return lax.dot_general(a, b, (((1,), (1,)), ((), ())), preferred_element_type=F32)


def _dot_tn(a, b):
    return lax.dot_general(a, b, (((0,), (0,)), ((), ())), preferred_element_type=F32)


_GELU_C = math.sqrt(2.0 / math.pi)
_GELU_A = 0.044715


def _gelu(x):
    return 0.5 * x * (1.0 + jnp.tanh(_GELU_C * (x + _GELU_A * x * x * x)))


def _gelu_and_grad(x):
    x2 = x * x
    t = jnp.tanh(_GELU_C * (x + _GELU_A * x2 * x))
    du = _GELU_C * (1.0 + 3.0 * _GELU_A * x2)
    return 0.5 * x * (1.0 + t), 0.5 * (1.0 + t) + 0.5 * x * (1.0 - t * t) * du


def _rms_stats(x):
    return lax.rsqrt(jnp.mean(x * x, axis=-1, keepdims=True) + EPS)


def _rms_bwd(dy, x, g):
    rs = _rms_stats(x)
    n = x * rs
    dn = dy * g
    dx = rs * (dn - n * jnp.mean(dn * n, axis=-1, keepdims=True))
    return dx, dy * n


def _row_sum(x):
    return jnp.sum(x, axis=0, keepdims=True)


def _tile_spec(ts, width, col=0):
    return pl.BlockSpec((ts, width), lambda i, col=col: (i, col))


def _full_spec(shape):
    zeros = (0,) * len(shape)
    return pl.BlockSpec(shape, lambda *_: zeros)


def _layer_spec(w, layer):
    zeros = (0,) * (w.ndim - 1)
    return pl.BlockSpec((None,) + tuple(w.shape[1:]), lambda *_: (layer,) + zeros)


def _with_after(body, n_in, after):
    if after is None:
        return body, [], []

    def wrapped(*refs):
        return body(*refs[:n_in], *refs[n_in + 1:])

    return wrapped, [pl.BlockSpec(memory_space=pl.ANY)], [after]


def _norm_call(x, g, ts, after=None):
    s = x.shape[0]

    def body(x_ref, g_ref, h_ref):
        xv = x_ref[...]
        h_ref[...] = (xv * _rms_stats(xv) * g_ref[...]).astype(BF)

    body, more_specs, more = _with_after(body, 2, after)
    return pl.pallas_call(
        body, name="norm_fwd", grid=(s // ts,),
        in_specs=[_tile_spec(ts, D_MODEL), _full_spec((1, D_MODEL))] + more_specs,
        out_specs=_tile_spec(ts, D_MODEL),
        out_shape=jax.ShapeDtypeStruct((s, D_MODEL), BF),
        compiler_params=_params(("parallel",)),
    )(x, g, *more)


def _inproj_call(h, w_in, layer, ts):
    s = h.shape[0]

    def body(h_ref, w_ref, o_ref):
        o_ref[...] = _dot(h_ref[...], w_ref[...]).astype(BF)

    return pl.pallas_call(
        body, name="inproj_fwd", grid=(N_QUARTERS, s // ts),
        in_specs=[
            pl.BlockSpec((ts, D_MODEL), lambda q, i: (i, 0)),
            pl.BlockSpec((None, None, D_MODEL, Q_IN), lambda q, i: (layer, q, 0, 0)),
        ],
        out_specs=pl.BlockSpec((ts, Q_IN), lambda q, i: (i, q)),
        out_shape=jax.ShapeDtypeStruct((s, D_IN), BF),
        compiler_params=_params(("parallel", "parallel")),
    )(h, w_in)


def _inproj_part_call(h, w_in, ts, own, first, count, into=None):
    s = h.shape[0]

    def quarter(j, sel):
        return (sel[0] + first + j) % N_QUARTERS

    def body(sel_ref, h_ref, w_ref, *rest):
        rest[-1][...] = _dot(h_ref[...], w_ref[...]).astype(BF)

    in_specs = [pl.BlockSpec((ts, D_MODEL), lambda j, i, sel: (i, 0)),
                pl.BlockSpec((None, None, D_MODEL, Q_IN), lambda j, i, sel: (0, quarter(j, sel), 0, 0))]
    operands = [h, w_in]
    aliases = {}
    if into is not None:
        in_specs.append(pl.BlockSpec(memory_space=pl.ANY))
        operands.append(into)
        aliases = {3: 0}
    return pl.pallas_call(
        body, name="inproj_fwd_part", out_shape=jax.ShapeDtypeStruct((s, D_IN), BF),
        grid_spec=pltpu.PrefetchScalarGridSpec(
            num_scalar_prefetch=1, grid=(count, s // ts), in_specs=in_specs,
            out_specs=pl.BlockSpec((ts, Q_IN), lambda j, i, sel: (i, quarter(j, sel)))),
        input_output_aliases=aliases,
        compiler_params=_params(("parallel", "parallel")),
    )(own, *operands)


def _shift_down(x, tail, s):
    xr = pltpu.roll(x, s, 0)
    tr = pltpu.roll(tail, s, 0)
    row = lax.broadcasted_iota(jnp.int32, tail.shape, 0)
    top = jnp.where(row < s, tr, xr[0:SUBLANES])
    return jnp.concatenate([top, xr[SUBLANES:]], axis=0)


def _shift_up(x, head, s):
    t = x.shape[0]
    xr = pltpu.roll(x, t - s, 0)
    hr = pltpu.roll(head, SUBLANES - s, 0)
    row = lax.broadcasted_iota(jnp.int32, head.shape, 0)
    bottom = jnp.where(row >= SUBLANES - s, hr, xr[t - SUBLANES:])
    return jnp.concatenate([xr[: t - SUBLANES], bottom], axis=0)


def _conv_fwd(x, tail, cw_ref, cb_ref):
    out = cb_ref[...] + cw_ref[CONV_WIDTH - 1:CONV_WIDTH, :] * x
    for s in range(1, CONV_WIDTH):
        k = CONV_WIDTH - 1 - s
        out = out + cw_ref[k:k + 1, :] * _shift_down(x, tail, s)
    return out


def _group_dot(x_bf, w_ref, dot):
    cols = [dot(x_bf[:, g * LRU_GROUP:(g + 1) * LRU_GROUP], w_ref[g]) for g in range(N_LRU_GROUPS)]
    return jnp.concatenate(cols, axis=1)


def _lru_gates(xr, wa_ref, wx_ref, ba_ref, bx_ref, sp_ref):
    xb = xr.astype(BF)
    r = jax.nn.sigmoid(_group_dot(xb, wa_ref, _dot) + ba_ref[...])
    i = jax.nn.sigmoid(_group_dot(xb, wx_ref, _dot) + bx_ref[...])
    log_a = (-LRU_C * r) * sp_ref[...]
    a = jnp.exp(log_a)
    nrm2 = -jnp.tanh(log_a) * (a * a + 1.0)
    inv_nrm = lax.rsqrt(jnp.maximum(nrm2, 1e-36))
    return r, i, a, nrm2 * inv_nrm, inv_nrm


def _linear_scan(a, b, carry, al_ref, bl_ref, h_ref, reverse):
    t, c = a.shape
    rowm = lax.broadcasted_iota(jnp.int32, (t, c), 0) & (SUBLANES - 1)
    for d in (1, 2, 4):
        if reverse:
            keep, sh = rowm < SUBLANES - d, t - d
        else:
            keep, sh = rowm >= d, d
        a_sh = jnp.where(keep, pltpu.roll(a, sh, 0), 1.0)
        b_sh = jnp.where(keep, pltpu.roll(b, sh, 0), 0.0)
        b = a * b_sh + b
        a = a * a_sh
    al_ref[...] = a
    bl_ref[...] = b
    groups = t // SUBLANES

    def step(j, state):
        jj = groups - 1 - j if reverse else j
        off = pl.multiple_of(jj * SUBLANES, SUBLANES)
        rows = bl_ref[pl.ds(off, SUBLANES), :] + al_ref[pl.ds(off, SUBLANES), :] * state
        h_ref[pl.ds(off, SUBLANES), :] = rows
        last = rows[0:1, :] if reverse else rows[SUBLANES - 1:SUBLANES, :]
        return jnp.broadcast_to(last, (SUBLANES, c))

    out = lax.fori_loop(0, groups, step, jnp.broadcast_to(carry, (SUBLANES, c)))
    return out[0:1, :]


def _rnn_fwd_call(proj, wa, wx, ba, bx, sp, cw, cb, ts):
    s = proj.shape[0]

    def body(xg_ref, wa_ref, wx_ref, ba_ref, bx_ref, sp_ref, cw_ref, cb_ref, xr_ref, hr_ref, ya_ref,
             tail_sc, carry_sc, al_sc, bl_sc, h_sc):
        @pl.when(pl.program_id(0) == 0)
        def _():
            tail_sc[...] = jnp.zeros_like(tail_sc)
            carry_sc[...] = jnp.zeros_like(carry_sc)

        x = xg_ref[:, :D_RNN].astype(F32)
        g = xg_ref[:, D_RNN:]
        xr = _conv_fwd(x, tail_sc[...], cw_ref, cb_ref)
        tail_sc[...] = x[ts - SUBLANES:, :]
        xr_ref[...] = xr.astype(BF)
        _, i, a, nrm, _ = _lru_gates(xr, wa_ref, wx_ref, ba_ref, bx_ref, sp_ref)
        carry_sc[...] = _linear_scan(a, nrm * (i * xr), carry_sc[...], al_sc, bl_sc, h_sc, False)
        h = h_sc[...]
        hr_ref[...] = h.astype(BF)
        ya_ref[...] = (h * _gelu(g)).astype(BF)

    gw = (N_LRU_GROUPS, LRU_GROUP, LRU_GROUP)
    return pl.pallas_call(
        body, name="rnn_fwd", grid=(s // ts,),
        in_specs=[_tile_spec(ts, 2 * D_RNN), _full_spec(gw), _full_spec(gw),
                  _full_spec((1, D_RNN)), _full_spec((1, D_RNN)), _full_spec((1, D_RNN)),
                  _full_spec((CONV_WIDTH, D_RNN)), _full_spec((1, D_RNN))],
        out_specs=[_tile_spec(ts, D_RNN)] * 3,
        out_shape=[jax.ShapeDtypeStruct((s, D_RNN), BF)] * 3,
        scratch_shapes=[pltpu.VMEM((SUBLANES, D_RNN), F32), pltpu.VMEM((1, D_RNN), F32),
                        pltpu.VMEM((ts, D_RNN), F32), pltpu.VMEM((ts, D_RNN), F32),
                        pltpu.VMEM((ts, D_RNN), F32)],
        compiler_params=_params(("arbitrary",)),
    )(proj, wa, wx, ba, bx, sp, cw, cb)


def _layernorm_fwd(x):
    mu = jnp.mean(x, axis=-1, keepdims=True)
    xc = x - mu
    rstd = lax.rsqrt(jnp.mean(xc * xc, axis=-1, keepdims=True) + EPS)
    return xc * rstd, rstd


def _sgu_mix(vn_bf, wm_ref, bsb_ref, ts):
    rows = []
    for blk in range(ts // SGU_BLOCK):
        r0 = blk * SGU_BLOCK
        cols = [
            _dot(wm_ref[g], vn_bf[r0:r0 + SGU_BLOCK, g * SGU_BLOCK:(g + 1) * SGU_BLOCK]) + bsb_ref[g]
            for g in range(SGU_GROUPS)
        ]
        rows.append(jnp.concatenate(cols, axis=1))
    return jnp.concatenate(rows, axis=0)


def _sgu_fwd_call(proj, wm, bsb, lg, lb, ts, after=None):
    s = proj.shape[0]

    def body(uv_ref, wm_ref, bsb_ref, lg_ref, lb_ref, yb_ref):
        gu = _gelu(uv_ref[:, :D_SGU])
        gv = _gelu(uv_ref[:, D_SGU:2 * D_SGU]).astype(F32)
        nh, _ = _layernorm_fwd(gv)
        vn = (nh * lg_ref[...] + lb_ref[...]).astype(BF)
        yb_ref[...] = (gu * _sgu_mix(vn, wm_ref, bsb_ref, ts)).astype(BF)

    sw = (SGU_GROUPS, SGU_BLOCK, SGU_BLOCK)
    body, more_specs, more = _with_after(body, 5, after)
    return pl.pallas_call(
        body, name="sgu_fwd", grid=(s // ts,),
        in_specs=[_tile_spec(ts, 2 * D_RNN, 1), _full_spec(sw), _full_spec(sw),
                  _full_spec((1, D_SGU)), _full_spec((1, D_SGU))] + more_specs,
        out_specs=_tile_spec(ts, D_SGU),
        out_shape=jax.ShapeDtypeStruct((s, D_SGU), BF),
        compiler_params=_params(("parallel",)),
    )(proj, wm, bsb, lg, lb, *more)


_GATE_COL0 = (2 * D_RNN + 2 * D_SGU) // 512


def _gate_specs(ts):
    return [_tile_spec(ts, 512, _GATE_COL0 + j) for j in range(4)]


def _merge_call(x, proj, ya_pre, yb_pre, w_ba, w_bb, w_out, g2, layer, ts):
    s = x.shape[0]

    def body(x_ref, ga0, ga1, gb0, gb1, ya_ref, yb_ref, wa_ref, wb_ref, wo_ref, g2_ref,
             x1_ref, yao_ref, ybo_ref, mg_ref, h2_ref):
        ya = _dot(ya_ref[...], wa_ref[...])
        yb = _dot(yb_ref[...], wb_ref[...])
        sa = jax.nn.sigmoid(jnp.concatenate([ga0[...], ga1[...]], axis=1).astype(F32))
        sb = jax.nn.sigmoid(jnp.concatenate([gb0[...], gb1[...]], axis=1).astype(F32))
        merged = (sa * ya + sb * yb).astype(BF)
        x1 = x_ref[...] + _dot(merged, wo_ref[...])
        x1_ref[...] = x1
        yao_ref[...] = ya.astype(BF)
        ybo_ref[...] = yb.astype(BF)
        mg_ref[...] = merged
        h2_ref[...] = (x1 * _rms_stats(x1) * g2_ref[...]).astype(BF)

    act = jax.ShapeDtypeStruct((s, D_MODEL), BF)
    return pl.pallas_call(
        body, name="merge_fwd", grid=(s // ts,),
        in_specs=[_tile_spec(ts, D_MODEL)] + _gate_specs(ts) + [
            _tile_spec(ts, D_RNN), _tile_spec(ts, D_SGU),
            _layer_spec(w_ba, layer), _layer_spec(w_bb, layer), _layer_spec(w_out, layer),
            _full_spec((1, D_MODEL))],
        out_specs=[_tile_spec(ts, D_MODEL)] * 5,
        out_shape=[jax.ShapeDtypeStruct((s, D_MODEL), F32), act, act, act, act],
        compiler_params=_params(("parallel",)),
    )(x, proj, proj, proj, proj, ya_pre, yb_pre, w_ba, w_bb, w_out, g2)


def _ffn_call(x1, h2, w_up, w_down, layer, ts):
    s = x1.shape[0]

    def body(x1_ref, h2_ref, wu_ref, wd_ref, x2_ref, p_ref):
        h2v = h2_ref[...]
        acc = x1_ref[...]
        for q in range(N_QUARTERS):
            p = _dot(h2v, wu_ref[q])
            p_ref[:, q * Q_FF:(q + 1) * Q_FF] = p.astype(BF)
            f = jnp.square(jnp.maximum(p, 0.0)).astype(BF)
            acc = acc + _dot(f, wd_ref[q * Q_FF:(q + 1) * Q_FF, :])
        x2_ref[...] = acc

    return pl.pallas_call(
        body, name="ffn_fwd", grid=(s // ts,),
        in_specs=[_tile_spec(ts, D_MODEL), _tile_spec(ts, D_MODEL),
                  pl.BlockSpec((None, N_QUARTERS, D_MODEL, Q_FF), lambda i: (layer, 0, 0, 0)),
                  pl.BlockSpec((None, D_FF, D_MODEL), lambda i: (layer, 0, 0))],
        out_specs=[_tile_spec(ts, D_MODEL), _tile_spec(ts, D_FF)],
        out_shape=[jax.ShapeDtypeStruct((s, D_MODEL), F32), jax.ShapeDtypeStruct((s, D_FF), BF)],
        compiler_params=_params(("parallel",)),
    )(x1, h2, w_up, w_down)


def _loss_call(x, target, gf, ts):
    s = x.shape[0]

    def body(x_ref, t_ref, g_ref, dx_ref, loss_ref, dg_ref):
        @pl.when(pl.program_id(0) == 0)
        def _():
            loss_ref[...] = jnp.zeros_like(loss_ref)
            dg_ref[...] = jnp.zeros_like(dg_ref)

        xv = x_ref[...]
        gv = g_ref[...]
        err = xv * _rms_stats(xv) * gv - t_ref[...]
        part = 0.5 * jnp.sum(jnp.mean(err * err, axis=-1, keepdims=True), axis=0, keepdims=True)
        loss_ref[...] += jnp.broadcast_to(part, loss_ref.shape)
        dx, dg = _rms_bwd(err * (1.0 / D_MODEL), xv, gv)
        dx_ref[...] = dx
        dg_ref[...] += _row_sum(dg)

    return pl.pallas_call(
        body, name="loss_head", grid=(s // ts,),
        in_specs=[_tile_spec(ts, D_MODEL), _tile_spec(ts, D_MODEL), _full_spec((1, D_MODEL))],
        out_specs=[_tile_spec(ts, D_MODEL), _full_spec((1, 128)), _full_spec((1, D_MODEL))],
        out_shape=[jax.ShapeDtypeStruct((s, D_MODEL), F32), jax.ShapeDtypeStruct((1, 128), F32),
                   jax.ShapeDtypeStruct((1, D_MODEL), F32)],
        compiler_params=_params(("arbitrary",)),
    )(x, target, gf)


def _ffn_bwd_call(dx2, p, x1, g2, w_up, w_down, layer, ts, after=None):
    s = dx2.shape[0]

    def body(dx2_ref, p_ref, x1_ref, g2_ref, wu_ref, wd_ref, dx1_ref, dp_ref, dg_ref, dx2b_ref, dx1b_ref):
        @pl.when(pl.program_id(0) == 0)
        def _():
            dg_ref[...] = jnp.zeros_like(dg_ref)

        dx2v = dx2_ref[...]
        dyb = dx2v.astype(BF)
        dx2b_ref[...] = dyb
        dh2 = jnp.zeros((ts, D_MODEL), F32)
        for q in range(N_QUARTERS):
            cols = slice(q * Q_FF, (q + 1) * Q_FF)
            df = _dot_nt(dyb, wd_ref[cols, :])
            dp = (df * (2.0 * jnp.maximum(p_ref[:, cols].astype(F32), 0.0))).astype(BF)
            dp_ref[:, cols] = dp
            dh2 = dh2 + _dot_nt(dp, wu_ref[q])
        dx, dg = _rms_bwd(dh2, x1_ref[...], g2_ref[...])
        dx1 = dx2v + dx
        dx1_ref[...] = dx1
        dx1b_ref[...] = dx1.astype(BF)
        dg_ref[...] += _row_sum(dg)

    body, more_specs, more = _with_after(body, 6, after)
    return pl.pallas_call(
        body, name="ffn_bwd", grid=(s // ts,),
        in_specs=[_tile_spec(ts, D_MODEL), _tile_spec(ts, D_FF), _tile_spec(ts, D_MODEL),
                  _full_spec((1, D_MODEL)),
                  pl.BlockSpec((None, N_QUARTERS, D_MODEL, Q_FF), lambda i: (layer, 0, 0, 0)),
                  pl.BlockSpec((None, D_FF, D_MODEL), lambda i: (layer, 0, 0))] + more_specs,
        out_specs=[_tile_spec(ts, D_MODEL), _tile_spec(ts, D_FF), _full_spec((1, D_MODEL)),
                   _tile_spec(ts, D_MODEL), _tile_spec(ts, D_MODEL)],
        out_shape=[jax.ShapeDtypeStruct((s, D_MODEL), F32), jax.ShapeDtypeStruct((s, D_FF), BF),
                   jax.ShapeDtypeStruct((1, D_MODEL), F32),
                   jax.ShapeDtypeStruct((s, D_MODEL), BF), jax.ShapeDtypeStruct((s, D_MODEL), BF)],
        compiler_params=_params(("arbitrary",)),
    )(dx2, p, x1, g2, w_up, w_down, *more)


def _merge_bwd_call(dx1, proj, ya, yb, w_ba, w_bb, w_out, layer, ts, after=None):
    s = dx1.shape[0]

    def body(dx1_ref, ga0, ga1, gb0, gb1, ya_ref, yb_ref, wa_ref, wb_ref, wo_ref, *rest):
        dya_ref, dyb_ref, dgate_ref, dyap_ref, dybp_ref = rest[-5:]
        dm = _dot_nt(dx1_ref[...].astype(BF), wo_ref[...])
        sa = jax.nn.sigmoid(jnp.concatenate([ga0[...], ga1[...]], axis=1).astype(F32))
        sb = jax.nn.sigmoid(jnp.concatenate([gb0[...], gb1[...]], axis=1).astype(F32))
        dya = (dm * sa).astype(BF)
        dyb = (dm * sb).astype(BF)
        dya_ref[...] = dya
        dyb_ref[...] = dyb
        dgate_ref[:, :D_MODEL] = (dm * ya_ref[...].astype(F32) * sa * (1.0 - sa)).astype(BF)
        dgate_ref[:, D_MODEL:] = (dm * yb_ref[...].astype(F32) * sb * (1.0 - sb)).astype(BF)
        dyap_ref[...] = _dot_nt(dya, wa_ref[...]).astype(BF)
        dybp_ref[...] = _dot_nt(dyb, wb_ref[...]).astype(BF)

    act = jax.ShapeDtypeStruct((s, D_MODEL), BF)
    return pl.pallas_call(
        body, name="merge_bwd", grid=(s // ts,),
        in_specs=[_tile_spec(ts, D_MODEL)] + _gate_specs(ts) + [
            _tile_spec(ts, D_MODEL), _tile_spec(ts, D_MODEL),
            _layer_spec(w_ba, layer), _layer_spec(w_bb, layer), _layer_spec(w_out, layer)]
        + ([] if after is None else [pl.BlockSpec(memory_space=pl.ANY)]),
        out_specs=[_tile_spec(ts, D_MODEL), _tile_spec(ts, D_MODEL), _tile_spec(ts, 2 * D_MODEL),
                   _tile_spec(ts, D_RNN), _tile_spec(ts, D_SGU)],
        out_shape=[act, act, jax.ShapeDtypeStruct((s, 2 * D_MODEL), BF),
                   jax.ShapeDtypeStruct((s, D_RNN), BF), jax.ShapeDtypeStruct((s, D_SGU), BF)],
        compiler_params=_params(("parallel",)),
    )(dx1, proj, proj, proj, proj, ya, yb, w_ba, w_bb, w_out, *([] if after is None else [after]))


def _sgu_bwd_call(dyb_pre, proj, wm, bsb, mask, lg, lb, ts, after=None):
    s = proj.shape[0]

    def body(dy_ref, uv_ref, wm_ref, bsb_ref, mask_ref, lg_ref, lb_ref,
             duv_ref, dws_ref, dbs_ref, dlg_ref, dlb_ref, dm_sc):
        step = pl.program_id(0)

        @pl.when(step == 0)
        def _():
            dws_ref[...] = jnp.zeros_like(dws_ref)
            dlg_ref[...] = jnp.zeros_like(dlg_ref)
            dlb_ref[...] = jnp.zeros_like(dlb_ref)
            dm_sc[...] = jnp.zeros_like(dm_sc)

        gu, dgu_du = _gelu_and_grad(uv_ref[:, :D_SGU])
        gv, dgv_dv = _gelu_and_grad(uv_ref[:, D_SGU:2 * D_SGU])
        nh, rstd = _layernorm_fwd(gv.astype(F32))
        lgv = lg_ref[...]
        vn = (nh * lgv + lb_ref[...]).astype(BF)
        dy = dy_ref[...].astype(F32)
        du = dy * _sgu_mix(vn, wm_ref, bsb_ref, ts) * dgu_du
        dmix = dy * gu
        dmix_bf = dmix.astype(BF)
        dm_acc = dm_sc[...]
        rows = []
        for blk in range(ts // SGU_BLOCK):
            r0 = blk * SGU_BLOCK
            dm_acc = dm_acc + dmix[r0:r0 + SGU_BLOCK, :]
            cols = []
            for g in range(SGU_GROUPS):
                c0 = g * SGU_BLOCK
                dmg = dmix_bf[r0:r0 + SGU_BLOCK, c0:c0 + SGU_BLOCK]
                cols.append(_dot_tn(wm_ref[g], dmg))
                dws_ref[g] += _dot_nt(dmg, vn[r0:r0 + SGU_BLOCK, c0:c0 + SGU_BLOCK])
            rows.append(jnp.concatenate(cols, axis=1))
        dm_sc[...] = dm_acc
        dvn = jnp.concatenate(rows, axis=0)
        dlg_ref[...] += _row_sum(dvn * nh)
        dlb_ref[...] += _row_sum(dvn)
        dnh = dvn * lgv
        dgv = rstd * (dnh - jnp.mean(dnh, axis=-1, keepdims=True)
                      - nh * jnp.mean(dnh * nh, axis=-1, keepdims=True))
        duv_ref[:, :D_SGU] = du.astype(BF)
        duv_ref[:, D_SGU:] = (dgv * dgv_dv).astype(BF)

        @pl.when(step == pl.num_programs(0) - 1)
        def _():
            for g in range(SGU_GROUPS):
                dws_ref[g] = dws_ref[g] * mask_ref[...]
                dbs_ref[:, g:g + 1] = jnp.sum(
                    dm_acc[:, g * SGU_BLOCK:(g + 1) * SGU_BLOCK], axis=1, keepdims=True)

    sw = (SGU_GROUPS, SGU_BLOCK, SGU_BLOCK)
    body, more_specs, more = _with_after(body, 7, after)
    return pl.pallas_call(
        body, name="sgu_bwd", grid=(s // ts,),
        in_specs=[_tile_spec(ts, D_SGU), _tile_spec(ts, 2 * D_RNN, 1), _full_spec(sw), _full_spec(sw),
                  _full_spec((SGU_BLOCK, SGU_BLOCK)), _full_spec((1, D_SGU)), _full_spec((1, D_SGU))] + more_specs,
        out_specs=[_tile_spec(ts, 2 * D_SGU), _full_spec(sw), _full_spec((SGU_BLOCK, SGU_GROUPS)),
                   _full_spec((1, D_SGU)), _full_spec((1, D_SGU))],
        out_shape=[jax.ShapeDtypeStruct((s, 2 * D_SGU), BF), jax.ShapeDtypeStruct(sw, F32),
                   jax.ShapeDtypeStruct((SGU_BLOCK, SGU_GROUPS), F32),
                   jax.ShapeDtypeStruct((1, D_SGU), F32), jax.ShapeDtypeStruct((1, D_SGU), F32)],
        scratch_shapes=[pltpu.VMEM((SGU_BLOCK, D_SGU), F32)],
        compiler_params=_params(("arbitrary",)),
    )(dyb_pre, proj, wm, bsb, mask, lg, lb, *more)


_ROW_DBA, _ROW_DBX, _ROW_DSP, _ROW_DCB, _ROW_DCW = 0, 1, 2, 3, 4
_PREV_ROWS = 16


def _rnn_bwd_call(dya_pre, proj, xr_saved, hr, wa, wx, ba, bx, sp, cw, ts, after=None):
    s = proj.shape[0]
    nt = s // ts
    per = ts // _PREV_ROWS

    def tile(i):
        return nt - 1 - i

    def prev(i):
        return jnp.maximum(tile(i) * per - 1, 0)

    def body(dy_ref, xg_ref, xr_ref, hr_ref, hrp_ref, wa_ref, wx_ref, ba_ref, bx_ref, sp_ref,
             cw_ref, dxg_ref, dwa_ref, dwx_ref, vec_ref,
             lam_carry, a_first, dxr_head, al_sc, bl_sc, lam_sc):
        step = pl.program_id(0)

        @pl.when(step == 0)
        def _():
            dwa_ref[...] = jnp.zeros_like(dwa_ref)
            dwx_ref[...] = jnp.zeros_like(dwx_ref)
            vec_ref[...] = jnp.zeros_like(vec_ref)
            lam_carry[...] = jnp.zeros_like(lam_carry)
            a_first[...] = jnp.zeros_like(a_first)
            dxr_head[...] = jnp.zeros_like(dxr_head)

        has_prev = (step < nt - 1).astype(F32)
        x = xg_ref[:, :D_RNN].astype(F32)
        g = xg_ref[:, D_RNN:]
        h_tail =hrp_ref[_PREV_ROWS - SUBLANES:, :].astype(F32) * has_prev
        xr = xr_ref[...].astype(F32)
        r, i, a, nrm, inv_nrm = _lru_gates(xr, wa_ref, wx_ref, ba_ref, bx_ref, sp_ref)
        h = hr_ref[...].astype(F32)
        dy = dy_ref[...].astype(F32)
        gg, dgg = _gelu_and_grad(g)

        coef = _shift_up(a, jnp.broadcast_to(a_first[...], (SUBLANES, D_RNN)), 1)
        lam_carry[...] = _linear_scan(coef, dy * gg, lam_carry[...], al_sc, bl_sc, lam_sc, True)
        a_first[...] = a[0:1, :]
        lam = lam_sc[...]

        da = lam * _shift_down(h, h_tail, 1)
        dnrm = lam * (i * xr)
        di = lam * nrm * xr
        dlog_a = da * a - dnrm * (a * a) * inv_nrm
        spv = sp_ref[...]
        dza = (dlog_a * (-LRU_C * spv)) * (r * (1.0 - r))
        dzx = di * (i * (1.0 - i))
        vec_ref[_ROW_DSP:_ROW_DSP + 1, :] += _row_sum(dlog_a * (-LRU_C * r))
        vec_ref[_ROW_DBA:_ROW_DBA + 1, :] += _row_sum(dza)
        vec_ref[_ROW_DBX:_ROW_DBX + 1, :] += _row_sum(dzx)
        xb = xr.astype(BF)
        dza_bf = dza.astype(BF)
        dzx_bf = dzx.astype(BF)
        for grp in range(N_LRU_GROUPS):
            cols = slice(grp * LRU_GROUP, (grp + 1) * LRU_GROUP)
            dwa_ref[grp] += _dot_tn(xb[:, cols], dza_bf[:, cols])
            dwx_ref[grp] += _dot_tn(xb[:, cols], dzx_bf[:, cols])
        dxr = (lam * nrm * i + _group_dot(dza_bf, wa_ref, _dot_nt) + _group_dot(dzx_bf, wx_ref, _dot_nt))

        vec_ref[_ROW_DCB:_ROW_DCB + 1, :] += _row_sum(dxr)
        head = dxr_head[...]
        dx = cw_ref[CONV_WIDTH - 1:CONV_WIDTH, :] * dxr
        vec_ref[_ROW_DCW + 3:_ROW_DCW + 4, :] += _row_sum(dxr * x)
        for sft in range(1, CONV_WIDTH):
            k = CONV_WIDTH - 1 - sft
            ahead = _shift_up(dxr, head, sft)
            dx = dx + cw_ref[k:k + 1, :] * ahead
            vec_ref[_ROW_DCW + k:_ROW_DCW + k + 1, :] += _row_sum(ahead * x)
        dxr_head[...] = dxr[0:SUBLANES, :]
        dxg_ref[:, :D_RNN] = dx.astype(BF)
        dxg_ref[:, D_RNN:] = (dy * h * dgg).astype(BF)

    gw = (N_LRU_GROUPS, LRU_GROUP, LRU_GROUP)
    rev = lambda width: pl.BlockSpec((ts, width), lambda i: (tile(i), 0))
    body, more_specs, more = _with_after(body, 11, after)
    return pl.pallas_call(
        body, name="rnn_bwd", grid=(nt,),
        in_specs=[rev(D_RNN), rev(2 * D_RNN), rev(D_RNN), rev(D_RNN),
                  pl.BlockSpec((_PREV_ROWS, D_RNN), lambda i: (prev(i), 0)),
                  _full_spec(gw), _full_spec(gw),
                  _full_spec((1, D_RNN)), _full_spec((1, D_RNN)), _full_spec((1, D_RNN)),
                  _full_spec((CONV_WIDTH, D_RNN))] + more_specs,
        out_specs=[rev(2 * D_RNN), _full_spec(gw), _full_spec(gw), _full_spec((SUBLANES, D_RNN))],
        out_shape=[jax.ShapeDtypeStruct((s, 2 * D_RNN), BF), jax.ShapeDtypeStruct(gw, F32),
                   jax.ShapeDtypeStruct(gw, F32), jax.ShapeDtypeStruct((SUBLANES, D_RNN), F32)],
        scratch_shapes=[pltpu.VMEM((1, D_RNN), F32), pltpu.VMEM((1, D_RNN), F32),
                        pltpu.VMEM((SUBLANES, D_RNN), F32),
                        pltpu.VMEM((ts, D_RNN), F32), pltpu.VMEM((ts, D_RNN), F32),
                        pltpu.VMEM((ts, D_RNN), F32)],
        compiler_params=_params(("arbitrary",)),
    )(dya_pre, proj, xr_saved, hr, hr, wa, wx, ba, bx, sp, cw, *more)


def _inproj_bwd_call(dxg, duv, dgate, dx1, x, g1, w_in, layer, ts):
    s = x.shape[0]

    def body(dxg_ref, duv_ref, dgt_ref, dx1_ref, x_ref, g_ref, w_ref, dx_ref, dproj_ref, dg_ref):
        @pl.when(pl.program_id(0) == 0)
        def _():
            dg_ref[...] = jnp.zeros_like(dg_ref)

        dproj = jnp.concatenate([dxg_ref[...], duv_ref[...], dgt_ref[...]], axis=1)
        dproj_ref[...] = dproj
        dh = jnp.zeros((ts, D_MODEL), F32)
        for q in range(N_QUARTERS):
            dh = dh + _dot_nt(dproj[:, q * Q_IN:(q + 1) * Q_IN], w_ref[q])
        dx, dg = _rms_bwd(dh, x_ref[...], g_ref[...])
        dx_ref[...] = dx1_ref[...] + dx
        dg_ref[...] += _row_sum(dg)

    return pl.pallas_call(
        body, name="inproj_bwd", grid=(s // ts,),
        in_specs=[_tile_spec(ts, 2 * D_RNN), _tile_spec(ts, 2 * D_SGU), _tile_spec(ts, 2 * D_MODEL),
                  _tile_spec(ts, D_MODEL), _tile_spec(ts, D_MODEL), _full_spec((1, D_MODEL)),
                  pl.BlockSpec((None, N_QUARTERS, D_MODEL, Q_IN), lambda i: (layer, 0, 0, 0))],
        out_specs=[_tile_spec(ts, D_MODEL), _tile_spec(ts, D_IN), _full_spec((1, D_MODEL))],
        out_shape=[jax.ShapeDtypeStruct((s, D_MODEL), F32), jax.ShapeDtypeStruct((s, D_IN), BF),
                   jax.ShapeDtypeStruct((1, D_MODEL), F32)],
        compiler_params=_params(("arbitrary",)),
    )(dxg, duv, dgate, dx1, x, g1, w_in)


def _relu_sq(p):
    return jnp.square(jnp.maximum(p, 0))


def _wgrad_call(a, b, core, tm, tn, tk, col_blocked, name, a_fn=None):
    s, m = a.shape
    n = b.shape[1]
    r, cols = (m, n // N_QUARTERS) if col_blocked else (m // N_QUARTERS, n)
    r2 = r // 2
    per_tile = tm // r
    steps = s // tk
    assert per_tile > 0 or steps == 1

    def body(core_ref, a_ref, b_ref, keep_ref, send_ref, *acc):
        av = a_ref[...]
        if a_fn is not None:
            av = a_fn(av)
        prod = _dot_tn(av.astype(BF), b_ref[...].astype(BF))

        def emit(total):
            for h in range(2):
                @pl.when(core_ref[0] == h)
                def _():
                    for q in range(per_tile):
                        keep_ref[q] = total[q * r + h * r2:q * r + (h + 1) * r2].astype(BF)
                        send_ref[q] = total[q * r + (1 - h) * r2:q * r + (2 - h) * r2].astype(BF)

        if per_tile == 0:
            mine = pl.program_id(1) == core_ref[0]

            @pl.when(mine)
            def _():
                keep_ref[0] = prod.astype(BF)

            @pl.when(jnp.logical_not(mine))
            def _():
                send_ref[0] = prod.astype(BF)
        elif steps == 1:
            emit(prod)
        else:
            acc_ref, = acc
            step = pl.program_id(2)

            @pl.when(step == 0)
            def _():
                acc_ref[...] = prod

            @pl.when(jnp.logical_and(step > 0, step < steps - 1))
            def _():
                acc_ref[...] += prod

            @pl.when(step == steps - 1)
            def _():
                emit(acc_ref[...] + prod)

    if col_blocked:
        per_q = cols // tn
        out_spec = pl.BlockSpec((1, r2, tn), lambda j, i, k, c: (j // per_q, 0, j % per_q))
    else:
        out_spec = pl.BlockSpec((per_tile, r2, tn), lambda j, i, k, c: (i, 0, j))
    return pl.pallas_call(
        body, name=name,
        out_shape=[jax.ShapeDtypeStruct((N_QUARTERS, r2, cols), BF)] * 2,
        grid_spec=pltpu.PrefetchScalarGridSpec(
            num_scalar_prefetch=1, grid=(n // tn, m // tm, steps),
            in_specs=[pl.BlockSpec((tk, tm), lambda j, i, k, c: (k, i)),
                      pl.BlockSpec((tk, tn), lambda j, i, k, c: (k, j))],
            out_specs=[out_spec, out_spec],
            scratch_shapes=[] if steps == 1 else [pltpu.VMEM((tm, tn), F32)]),
        compiler_params=_params(("parallel", "parallel", "arbitrary")),
    )(core, a, b)


BIG = ("w_in", "w_up", "w_down", "w_branch_a", "w_branch_b", "w_out")


def _block_diag(w):
    w4 = w.reshape(N_LRU_GROUPS, HEADS_PER_GROUP, RNN_HEAD_DIM, RNN_HEAD_DIM)
    eye = jnp.eye(HEADS_PER_GROUP, dtype=w.dtype)
    return jnp.einsum("gjio,jk->gjiko", w4, eye).reshape(N_LRU_GROUPS, LRU_GROUP, LRU_GROUP)


def _block_diag_extract(d):
    d5 = d.reshape(N_LRU_GROUPS, HEADS_PER_GROUP, RNN_HEAD_DIM, HEADS_PER_GROUP, RNN_HEAD_DIM)
    blocks = [d5[:, j, :, j, :] for j in range(HEADS_PER_GROUP)]
    return jnp.stack(blocks, axis=1).reshape(RNN_HEADS, RNN_HEAD_DIM, RNN_HEAD_DIM)


def _sgu_mask():
    chunk = jnp.arange(SGU_BLOCK) // CHUNK
    return (chunk[:, None] >= chunk[None, :]).astype(F32)


def _layer_small(sm, l, core):
    row = lambda v: v.reshape(1, -1)
    return dict(
        core=core,
        g1=row(sm["norm_mix_g"][l]), g2=row(sm["norm_ffn_g"][l]),
        wa=_block_diag(sm["lru_w_a"][l]).astype(BF), wx=_block_diag(sm["lru_w_x"][l]).astype(BF),
        ba=row(sm["lru_b_a"][l]), bx=row(sm["lru_b_x"][l]),
        sp=row(jax.nn.softplus(-sm["lru_lambda"][l])),
        cw=sm["conv_w"][l] if "conv_w" in sm else None, cb=row(sm["conv_b"][l]),
        wm=(sm["sgu_w_s"][l] * _sgu_mask()).astype(BF),
        bsb=jnp.broadcast_to(sm["sgu_b_s"][l][:, :, None], (SGU_GROUPS, SGU_BLOCK, SGU_BLOCK)),
        lg=row(sm["sgu_ln_g"][l]), lb=row(sm["sgu_ln_b"][l]),
    )


def _layer_fwd_mix(x, big, p, ts, h=None, before_sgu=None, proj=None):
    if h is None:
        h = _norm_call(x, p["g1"], ts)
    if proj is None:
        proj = _inproj_call(h, big["w_in"], 0, 2 * ts)
    xr, hr, ya_pre = _rnn_fwd_call(proj, p["wa"], p["wx"], p["ba"], p["bx"], p["sp"], p["cw"], p["cb"], ts)
    yb_pre = _sgu_fwd_call(proj, p["wm"], p["bsb"], p["lg"], p["lb"], ts,
                           None if before_sgu is None else before_sgu(ya_pre))
    return dict(p=p, x=x, h=h, proj=proj, xr=xr, hr=hr, ya_pre=ya_pre, yb_pre=yb_pre)


def _layer_fwd_out(sv, big, ts):
    x1, ya, yb, merged, h2 = _merge_call(sv["x"], sv["proj"], sv["ya_pre"], sv["yb_pre"], big["w_branch_a"],
                                         big["w_branch_b"], big["w_out"], sv["p"]["g2"], 0, ts)
    x2, pre = _ffn_call(x1, h2, big["w_up"], big["w_down"], 0, ts)
    sv.update(x1=x1, ya=ya, yb=yb, merged=merged, h2=h2, pre=pre)
    return x2


def _layer_bwd_ffn(dx, sv, big, ts, after=None):
    p = sv["p"]
    dx1, dpre, dg2, dx_bf, sv["dx1_bf"] = _ffn_bwd_call(dx, sv["pre"], sv["x1"], p["g2"], big["w_up"],
                                                       big["w_down"], 0, ts, after)
    tk = dx.shape[0]
    gb = dict(
        w_down=_wgrad_call(sv["pre"], dx_bf, p["core"], Q_FF, D_MODEL, tk, False, "wgrad_down", a_fn=_relu_sq),
        w_up=_wgrad_call(sv["h2"], dpre, p["core"], D_MODEL, Q_FF, tk, True, "wgrad_up"))
    return dx1, gb, dict(norm_ffn_g=dg2[0])


def _layer_bwd_merge(dx1, sv, big, ts, after=None):
    tk = dx1.shape[0]
    core = sv["p"]["core"]
    dya, dyb, dgate, dya_pre, dyb_pre = _merge_bwd_call(
        dx1, sv["proj"], sv["ya"], sv["yb"], big["w_branch_a"], big["w_branch_b"], big["w_out"], 0, ts, after)
    gb = dict(
        w_out=_wgrad_call(sv["merged"], sv["dx1_bf"], core, D_MODEL, D_MODEL, tk, False, "wgrad_out"),
        w_branch_a=_wgrad_call(sv["ya_pre"], dya, core, D_RNN, D_MODEL // 2, tk, False, "wgrad_branch_a"),
        w_branch_b=_wgrad_call(sv["yb_pre"], dyb, core, D_SGU, D_MODEL, tk, False, "wgrad_branch_b"))
    return (dgate, dya_pre, dyb_pre), gb


def _layer_bwd_branches(dx1, merge_out, sv, big, lam, ts, after=None, after_sgu=None):
    p = sv["p"]
    tk = dx1.shape[0]
    dgate, dya_pre, dyb_pre = merge_out
    gb = {}
    duv, dws, dbs, dlg, dlb = _sgu_bwd_call(dyb_pre, sv["proj"], p["wm"], p["bsb"], _sgu_mask(), p["lg"], p["lb"],
                                            ts, after)
    dxg, dwa, dwx, vec = _rnn_bwd_call(dya_pre, sv["proj"], sv["xr"], sv["hr"], p["wa"], p["wx"], p["ba"], p["bx"],
                                       p["sp"], p["cw"], ts, None if after_sgu is None else after_sgu(duv))
    dx, dproj, dg1 = _inproj_bwd_call(dxg, duv, dgate, dx1, sv["x"], p["g1"], big["w_in"], 0, ts)
    gb["w_in"] = _wgrad_call(sv["h"], dproj, p["core"], D_MODEL // 2, Q_IN, tk, True, "wgrad_in")
    gs = dict(
        norm_mix_g=dg1[0], conv_w=vec[_ROW_DCW:_ROW_DCW + CONV_WIDTH], conv_b=vec[_ROW_DCB],
        lru_w_a=_block_diag_extract(dwa), lru_w_x=_block_diag_extract(dwx),
        lru_b_a=vec[_ROW_DBA].reshape(RNN_HEADS, RNN_HEAD_DIM), lru_b_x=vec[_ROW_DBX].reshape(RNN_HEADS, RNN_HEAD_DIM),
        lru_lambda=-vec[_ROW_DSP] * jax.nn.sigmoid(-lam),
        sgu_ln_g=dlg[0], sgu_ln_b=dlb[0], sgu_w_s=dws, sgu_b_s=dbs.T)
    return dx, gb, gs


def _local_step(x, target, big, sm, ts):
    saved = []
    core = jnp.zeros((1,), jnp.int32)
    for l in range(DEPTH):
        sv = _layer_fwd_mix(x, big[l], _layer_small(sm, l, core), ts)
        x = _layer_fwd_out(sv, big[l], ts)
        saved.append(sv)
    dx, loss, dgf = _loss_call(x, target, sm["final_norm_g"].reshape(1, -1), ts)
    gb, gs = [None] * DEPTH, [None] * DEPTH
    for l in reversed(range(DEPTH)):
        dx1, gb_ffn, gs_ffn = _layer_bwd_ffn(dx, saved[l], big[l], ts)
        merge_out, gb_merge = _layer_bwd_merge(dx1, saved[l], big[l], ts)
        dx, gb_mix, gs_mix = _layer_bwd_branches(dx1, merge_out, saved[l], big[l], sm["lru_lambda"][l], ts)
        gb[l] = {**gb_ffn, **gb_merge, **gb_mix}
        gs[l] = {**gs_ffn, **gs_mix}
    gs = {k: jnp.stack([g[k] for g in gs]) for k in gs[0]}
    gs["final_norm_g"] = dgf[0]
    return loss, dx, gb, gs


EW_VMEM_BYTES = 24 * 1024 * 1024


def _row_block(rows, cols, bytes_per_elem):
    for br in range(min(rows, EW_VMEM_BYTES // (2 * bytes_per_elem * cols)), 0, -1):
        if rows % br == 0 and br % 16 == 0:
            return br
    return rows


def _ew_call(fn, name, operands, outputs, slabs=1, sel=None, into=None, after=None):
    if into is not None and not isinstance(into, (list, tuple)):
        into = [into]
    rows, cols = outputs[0][0].shape[2:]
    br = _row_block(rows, cols, sum(jnp.dtype(a.dtype).itemsize for a, _ in operands + outputs))
    n_in = len(operands)

    def pick(tok, g, s):
        if callable(tok):
            return tok(g, s)
        if tok == "g":
            return g
        if isinstance(tok, tuple):
            return s[tok[1]]
        return tok

    def spec(idx):
        return pl.BlockSpec((None, None, br, cols),
                            lambda g, i, s, idx=idx: (pick(idx[0], g, s), pick(idx[1], g, s), i, 0))

    if sel is None:
        sel = jnp.zeros((1,), jnp.int32)
    in_specs = [spec(idx) for _, idx in operands]
    arrays = [a for a, _ in operands]
    aliases = {}
    for j, buf in enumerate(into or ()):
        in_specs.append(pl.BlockSpec(memory_space=pl.ANY))
        arrays.append(buf)
        aliases[1 + n_in + j] = j
    if after is not None:
        in_specs.append(pl.BlockSpec(memory_space=pl.ANY))
        arrays.append(after)

    def body(sel_ref, *refs):
        outs = fn(*[r[...] for r in refs[:n_in]])
        for o_ref, o in zip(refs[len(arrays):], outs):
            o_ref[...] = o.astype(o_ref.dtype)

    return pl.pallas_call(
        body, name=name, out_shape=[s for s, _ in outputs],
        grid_spec=pltpu.PrefetchScalarGridSpec(
            num_scalar_prefetch=1, grid=(slabs, rows // br),
            in_specs=in_specs,
            out_specs=[spec(idx) for _, idx in outputs]),
        input_output_aliases=aliases,
        compiler_params=_params(("parallel", "parallel")),
    )(sel, *arrays)


def _as4(a):
    return a.reshape((1,) * (4 - a.ndim) + a.shape)


def _adamw(w, g, m, v):
    m = ADAM_B1 * m + (1.0 - ADAM_B1) * g
    v = ADAM_B2 * v + (1.0 - ADAM_B2) * jnp.square(g)
    m_hat = m / (1.0 - ADAM_B1 ** ADAM_STEP)
    v_hat = v / (1.0 - ADAM_B2 ** ADAM_STEP)
    delta = -ADAM_LR * (m_hat / (jnp.sqrt(v_hat) + ADAM_EPS) + ADAM_WD * w)
    return delta, m, v


def _small_adamw_call(ws, gs, ms, vs):
    n = len(ws)

    def body(*refs):
        for k in range(n):
            w, g, m, v = (refs[j * n + k][...] for j in range(4))
            outs = _adamw(w, g, m, v)
            for j in range(3):
                refs[(4 + j) * n + k][...] = outs[j]

    shapes = [jax.ShapeDtypeStruct(w.shape, F32) for w in ws]
    outs = pl.pallas_call(
        body, name="adamw_small", out_shape=shapes * 3,
        in_specs=[pl.BlockSpec(memory_space=pltpu.VMEM)] * (4 * n),
        out_specs=[pl.BlockSpec(memory_space=pltpu.VMEM)] * (3 * n),
        compiler_params=_params(),
    )(*ws, *gs, *ms, *vs)
    return outs[:n], outs[n:2 * n], outs[2 * n:]


ANY = pl.BlockSpec(memory_space=pl.ANY)


def _place():
    x, y, c = lax.axis_index("x"), lax.axis_index("y"), lax.axis_index("c")
    chips = [(1 - x, y), (x, 1 - y), (1 - x, 1 - y)]
    return x, y, c, chips


def _remote(src, dst, send_sem, recv_sem, to):
    return pltpu.make_async_remote_copy(src_ref=src, dst_ref=dst, send_sem=send_sem, recv_sem=recv_sem,
                                        device_id=to, device_id_type=MESH)


def _sibling_send_call(items):
    n = len(items)

    def body(*refs):
        src, out = refs[:n], refs[n:2 * n]
        send_sems, recv_sems = refs[2 * n:]
        x, y, c, _ = _place()
        copies = [_remote(src[w], out[w], send_sems.at[w], recv_sems.at[w], (x, y, 1 - c)) for w in range(n)]
        for cp in copies:
            cp.start()
        for cp in copies:
            cp.wait()

    return pl.pallas_call(
        body, name="grads_to_sibling",
        out_shape=[jax.ShapeDtypeStruct(a.shape, a.dtype) for a in items],
        in_specs=[ANY] * n, out_specs=[ANY] * n,
        scratch_shapes=[pltpu.SemaphoreType.DMA((n,)), pltpu.SemaphoreType.DMA((n,))],
        compiler_params=_params(vmem=False, has_side_effects=True),
    )(*items)


def _sibling_inplace_call(name, bufs, slabs, n_pairs):
    n = len(bufs)

    def body(*refs):
        out = refs[n:2 * n]
        send_sems, recv_sems = refs[2 * n:]
        x, y, c, _ = _place()
        sibling = (x, y, 1 - c)
        pairs = [pair for w, ref in enumerate(out) for pair in slabs(ref, c, w)]
        sends = [_remote(s, s, send_sems.at[k], recv_sems.at[k], sibling) for k, (s, _) in enumerate(pairs)]
        for cp in sends:
            cp.start()
        for k, (_, r) in enumerate(pairs):
            _remote(r, r, send_sems.at[k], recv_sems.at[k], sibling).wait_recv()
        for cp in sends:
            cp.wait_send()

    return pl.pallas_call(
        body, name=name,
        out_shape=[jax.ShapeDtypeStruct(a.shape, a.dtype) for a in bufs],
        in_specs=[ANY] * n, out_specs=[ANY] * n,
        input_output_aliases={w: w for w in range(n)},
        scratch_shapes=[pltpu.SemaphoreType.DMA((n_pairs,)), pltpu.SemaphoreType.DMA((n_pairs,))],
        compiler_params=_params(vmem=False, has_side_effects=True),
    )(*bufs)


HBM_SPEC = pl.BlockSpec(memory_space=pltpu.HBM)
SEM_SPEC = pl.BlockSpec(memory_space=pltpu.SEMAPHORE)
DATAFLOW_EFFECT = pltpu.SideEffectType.DATAFLOW_SIDE_EFFECTING


def _exchange_start(name, bufs, copies, n_copies, after):
    return _exchange_start_many(name, [(bufs, copies, n_copies)], after)[0]


def _exchange_start_many(name, groups, after):
    sizes = [len(bufs) for bufs, _, _ in groups]
    starts = [sum(sizes[:g]) for g in range(len(groups))]
    n, n_sems = sum(sizes), 2 * len(groups)

    def body(*refs):
        ins, sems, token = refs[:n], refs[n + 1:n + 1 + n_sems], refs[-1]
        for g, (_, copies, _) in enumerate(groups):
            send_sems, recv_sems = sems[2 * g], sems[2 * g + 1]
            for k, (src, dst, to) in enumerate(copies(ins[starts[g]:starts[g] + sizes[g]])):
                _remote(src, dst, send_sems.at[k], recv_sems.at[k], to).start()
        token[...] = jnp.zeros_like(token)

    every = [b for bufs, _, _ in groups for b in bufs]
    outs = pl.pallas_call(
        body, name=name,
        out_shape=(*[pltpu.SemaphoreType.DMA((c,)) for _, _, c in groups for _ in range(2)],
                   *[pltpu.HBM(b.shape, b.dtype) for b in every], jax.ShapeDtypeStruct((SUBLANES, 128), F32)),
        in_specs=[HBM_SPEC] * n + [ANY],
        out_specs=(*[SEM_SPEC] * n_sems, *[HBM_SPEC] * n, pl.BlockSpec(memory_space=pltpu.VMEM)),
        input_output_aliases={w: w + n_sems for w in range(n)},
        compiler_params=pltpu.CompilerParams(has_side_effects=DATAFLOW_EFFECT),
    )(*[pltpu.with_memory_space_constraint(b, pltpu.HBM) for b in every], after)
    thru = outs[n_sems:n_sems + n]
    return [(outs[2 * g], outs[2 * g + 1], list(thru[starts[g]:starts[g] + sizes[g]]), outs[-1])
            for g in range(len(groups))]


def _exchange_wait(name, send_sems, recv_sems, bufs, copies, after):
    n = len(bufs)

    def body(*refs):
        ins, send_sems, recv_sems = refs[:n], refs[n], refs[n + 1]
        for k, (src, dst, to) in enumerate(copies(ins)):
            cp = _remote(src, dst, send_sems.at[k], recv_sems.at[k], to)
            cp.wait_send()
            cp.wait_recv()

    return pl.pallas_call(
        body, name=name,
        out_shape=[pltpu.HBM(b.shape, b.dtype) for b in bufs],
        in_specs=[HBM_SPEC] * n + [SEM_SPEC, SEM_SPEC, ANY],
        out_specs=[HBM_SPEC] * n,
        input_output_aliases={w: w for w in range(n)},
        compiler_params=pltpu.CompilerParams(has_side_effects=DATAFLOW_EFFECT),
    )(*bufs, send_sems, recv_sems, after)


def _gather_copies(refs):
    x, y, c, chips = _place()
    mine = 2 * (2 * x + y) + c
    return [(ref.at[mine], ref.at[mine], (qx, qy, c)) for ref in refs for qx, qy in chips]


def _forward_copies(refs):
    x, y, c, chips = _place()
    return [(ref.at[2 * (2 * qx + qy) + c], ref.at[2 * (2 * qx + qy) + c], (x, y, 1 - c))
            for ref in refs for qx, qy in chips]


def _gather_forward_slabs(ref, c, w):
    x, y, _, chips = _place()
    return [(ref.at[2 * (2 * qx + qy) + c], ref.at[2 * (2 * qx + qy) + 1 - c]) for qx, qy in chips]


def _device_peers():
    x, y, c, _ = _place()
    return 4 * x + 2 * y + c, [(k, (x ^ ((k >> 2) & 1), y ^ ((k >> 1) & 1), c ^ (k & 1))) for k in range(1, 8)]


def _small_scatter_copies(refs):
    me, peers = _device_peers()
    return [(refs[0].at[me ^ k], refs[1].at[me], to) for k, to in peers]


def _small_spread_copies(refs):
    me, peers = _device_peers()
    return [(refs[0].at[me], refs[0].at[me], to) for _, to in peers]


def _sibling_copies(refs):
    n = len(refs) // 2
    x, y, c, _ = _place()
    return [(refs[w], refs[n + w], (x, y, 1 - c)) for w in range(n)]


def _owner_copies(refs):
    n = len(refs) // 2
    x, y, c, chips = _place()
    return [(refs[w].at[2 * qx + qy], refs[n + w].at[j], (qx, qy, c))
            for w in range(n) for j, (qx, qy) in enumerate(chips)]


N_DEVICES = 8
SMALL_ROWS = 616


SMALL = ("norm_mix_g", "conv_w", "conv_b", "lru_w_a", "lru_b_a", "lru_w_x", "lru_b_x", "lru_lambda",
         "sgu_ln_g", "sgu_ln_b", "sgu_w_s", "sgu_b_s", "norm_ffn_g", "final_norm_g")
WEIGHTS = ("norm_mix_g", "w_in", "conv_w", "conv_b", "lru_w_a", "lru_b_a", "lru_w_x", "lru_b_x", "lru_lambda",
           "sgu_ln_g", "sgu_ln_b", "sgu_w_s", "sgu_b_s", "w_branch_a", "w_branch_b", "w_out", "norm_ffn_g",
           "w_up", "w_down", "final_norm_g")
PACK_ALIGN = SUBLANES * 128


PACKED = SMALL + ("loss",)


def _pack_small(gs):
    parts = []
    for k in PACKED:
        flat = gs[k].reshape(-1)
        parts.append(jnp.pad(flat, (0, -flat.size % PACK_ALIGN)))
    flat = jnp.concatenate(parts)
    flat = jnp.pad(flat, (0, N_DEVICES * SMALL_ROWS * 128 - flat.size))
    return flat.reshape(N_DEVICES, SMALL_ROWS, 128)


def _unpack_small(buf, like):
    flat = buf.reshape(-1)
    out, off = {}, 0
    for k in PACKED:
        size = like[k].size
        out[k] = flat[off:off + size].reshape(like[k].shape)
        off += size + (-size % PACK_ALIGN)
    return out


def _as_rows(a):
    return a.reshape(-1, a.shape[-1])


def kernel(x, norm_mix_g, w_in, conv_w, conv_b, lru_w_a, lru_b_a, lru_w_x, lru_b_x, lru_lambda, sgu_ln_g, sgu_ln_b, sgu_w_s, sgu_b_s, w_branch_a, w_branch_b, w_out, norm_ffn_g, w_up, w_down, final_norm_g, loss_target, m_norm_mix_g, m_w_in, m_conv_w, m_conv_b, m_lru_w_a, m_lru_b_a, m_lru_w_x, m_lru_b_x, m_lru_lambda, m_sgu_ln_g, m_sgu_ln_b, m_sgu_w_s, m_sgu_b_s, m_w_branch_a, m_w_branch_b, m_w_out, m_norm_ffn_g, m_w_up, m_w_down, m_final_norm_g, v_norm_mix_g, v_w_in, v_conv_w, v_conv_b, v_lru_w_a, v_lru_b_a, v_lru_w_x, v_lru_b_x, v_lru_lambda, v_sgu_ln_g, v_sgu_ln_b, v_sgu_w_s, v_sgu_b_s, v_w_branch_a, v_w_branch_b, v_w_out, v_norm_ffn_g, v_w_up, v_w_down, v_final_norm_g):
    w = dict(norm_mix_g=norm_mix_g, w_in=w_in, conv_w=conv_w, conv_b=conv_b, lru_w_a=lru_w_a, lru_b_a=lru_b_a,
             lru_w_x=lru_w_x, lru_b_x=lru_b_x, lru_lambda=lru_lambda, sgu_ln_g=sgu_ln_g, sgu_ln_b=sgu_ln_b,
             sgu_w_s=sgu_w_s, sgu_b_s=sgu_b_s, w_branch_a=w_branch_a, w_branch_b=w_branch_b, w_out=w_out,
             norm_ffn_g=norm_ffn_g, w_up=w_up, w_down=w_down, final_norm_g=final_norm_g)
    m = dict(norm_mix_g=m_norm_mix_g, w_in=m_w_in, conv_w=m_conv_w, conv_b=m_conv_b, lru_w_a=m_lru_w_a,
             lru_b_a=m_lru_b_a, lru_w_x=m_lru_w_x, lru_b_x=m_lru_b_x, lru_lambda=m_lru_lambda,
             sgu_ln_g=m_sgu_ln_g, sgu_ln_b=m_sgu_ln_b, sgu_w_s=m_sgu_w_s, sgu_b_s=m_sgu_b_s,
             w_branch_a=m_w_branch_a, w_branch_b=m_w_branch_b, w_out=m_w_out, norm_ffn_g=m_norm_ffn_g,
             w_up=m_w_up, w_down=m_w_down, final_norm_g=m_final_norm_g)
    v = dict(norm_mix_g=v_norm_mix_g, w_in=v_w_in, conv_w=v_conv_w, conv_b=v_conv_b, lru_w_a=v_lru_w_a,
             lru_b_a=v_lru_b_a, lru_w_x=v_lru_w_x, lru_b_x=v_lru_b_x, lru_lambda=v_lru_lambda,
             sgu_ln_g=v_sgu_ln_g, sgu_ln_b=v_sgu_ln_b, sgu_w_s=v_sgu_w_s, sgu_b_s=v_sgu_b_s,
             w_branch_a=v_w_branch_a, w_branch_b=v_w_branch_b, w_out=v_w_out, norm_ffn_g=v_norm_ffn_g,
             w_up=v_w_up, w_down=v_w_down, final_norm_g=v_final_norm_g)
    core = lax.axis_index("c")
    chip = 2 * lax.axis_index("x") + lax.axis_index("y")
    sel = jnp.stack([core, 1 - core, chip, 2 * chip + core]).astype(jnp.int32)
    this_core, this_chip = ("sel", 0), ("sel", 2)
    sds = jax.ShapeDtypeStruct

    ts = TOKEN_TILE

    def after_all(arrays):
        return jnp.stack([a[(0,) * a.ndim].astype(F32) for a in arrays])

    halves = {k: (w[k].shape[1] // 2, w[k].shape[2]) for k in BIG}

    def half_view(k, a):
        return a.reshape((2 * N_QUARTERS,) + halves[k])

    def full_view(k, a):
        if k == "conv_w":
            return a.reshape(N_QUARTERS, DEPTH, CONV_WIDTH, -1).transpose(1, 2, 0, 3).reshape(DEPTH, CONV_WIDTH, D_RNN)
        r2, cols = halves[k]
        if k in ("w_in", "w_up"):
            return a.reshape(1, N_QUARTERS, 2 * r2, cols)
        return a.reshape(1, 2 * N_QUARTERS * r2, cols)

    layer_bufs = [{}, {}]

    def cast_weights(k, after):
        _, r, cols = w[k].shape
        w4 = w[k].reshape(DEPTH, 1, r, cols)
        outs = _ew_call(lambda a, b: (a, b), "cast_weights", [(w4, (0, 0)), (w4, (1, 0))],
                        [(sds((1, N_QUARTERS, r, cols), BF), (0, this_chip))] * DEPTH, 1, sel, after=after)
        for l in range(DEPTH):
            layer_bufs[l][k] = half_view(k, outs[l])

    conv_buf = lax.dynamic_update_slice_in_dim(
        jnp.zeros((N_QUARTERS, DEPTH) + conv_w.shape[1:], F32), conv_w[None], chip, axis=0)
    layer_bufs[0]["conv_w"] = conv_buf.reshape((2 * N_QUARTERS,) + conv_w.shape[1:])
    sm = {k: w[k] for k in SMALL if k != "conv_w"}

    def gather_start(tag, l, keys, after):
        bufs = [layer_bufs[l][k] for k in keys]
        return _exchange_start(f"gather_start_{tag}", bufs, _gather_copies, 3 * len(keys), after)

    def gather_finish(tag, keys, started, after):
        send_sems, recv_sems, thru, _ = started
        landed = _exchange_wait(f"gather_wait_{tag}", send_sems, recv_sems, thru, _gather_copies, after)
        landed = _sibling_inplace_call("gather_forward", landed, _gather_forward_slabs, 3 * len(keys))
        return {k: full_view(k, a) for k, a in zip(keys, landed)}

    first, rest = ("w_in",), tuple(k for k in BIG if k != "w_in")
    cast_weights("w_in", None)
    started_a = gather_start("0a", 0, first + ("conv_w",), sel)
    for k in rest:
        cast_weights(k, started_a[3])
    started_b, started_c, started_d = _exchange_start_many(
        "gather_start_rest",
        [([layer_bufs[l][k] for k in keys], _gather_copies, 3 * len(keys)) for l, keys in ((0, rest), (1, first), (1, rest))],
        started_a[3])

    def arrives(tag, keys, started):
        state = {}

        def hook(after):
            landed = _exchange_wait(f"gather_wait_{tag}", started[0], started[1], started[2], _gather_copies, after)
            state["forward"] = _exchange_start(f"forward_start_{tag}", landed, _forward_copies, 3 * len(keys), after)
            return state["forward"][3]

        def finish(after):
            send_sems, recv_sems, thru, _ = state["forward"]
            done = _exchange_wait(f"forward_wait_{tag}", send_sems, recv_sems, thru, _forward_copies, after)
            return {k: full_view(k, a) for k, a in zip(keys, done)}

        return hook, finish

    p0, p1 = _layer_small(sm, 0, sel[0:1]), _layer_small(sm, 1, sel[0:1])
    h0 = _norm_call(x[0], p0["g1"], 2 * ts)
    proj_own = _inproj_part_call(h0, full_view("w_in", started_a[2][0]), 2 * ts, sel[2:3], 0, 1)
    ready = after_all([started_d[3], proj_own] + [p[k] for p in (p0, p1) for k in ("wa", "wx", "wm")])
    big0 = gather_finish("0a", first + ("conv_w",), started_a, ready)
    for l, p in enumerate((p0, p1)):
        p["cw"] = big0["conv_w"][l]
    proj0 = _inproj_part_call(h0, big0["w_in"], 2 * ts, sel[2:3], 1, N_QUARTERS - 1, proj_own)
    hook, finish = arrives("0b", rest, started_b)
    sv0 = _layer_fwd_mix(x[0], big0, p0, ts, h0, hook, proj0)
    big0.update(finish(sv0["yb_pre"]))
    x_mid = _layer_fwd_out(sv0, big0, ts)
    hook, finish = arrives("1a", first, started_c)
    h1 = _norm_call(x_mid, p1["g1"], 2 * ts, hook(x_mid))
    big1 = finish(h1)
    hook, finish = arrives("1b", rest, started_d)
    sv1 = _layer_fwd_mix(x_mid, big1, p1, ts, h1, hook)
    big1.update(finish(sv1["yb_pre"]))
    x_out = _layer_fwd_out(sv1, big1, ts)
    dx, loss, dgf = _loss_call(x_out, loss_target[0], final_norm_g.reshape(1, -1), 2 * ts)

    def pair_start(tag, gb, after):
        sends = [gb[k][1] for k in gb]
        zones = [lax.empty(a.shape, BF) for a in sends]
        return _exchange_start(f"pair_start_{tag}", sends + zones, _sibling_copies, len(sends), after)

    def reduce_start(tag, gb, after, pair=None):
        keys = tuple(gb)
        if pair is None:
            from_sibling = _sibling_send_call([gb[k][1] for k in keys])
        else:
            done = _exchange_wait(f"pair_wait_{tag}", pair[0], pair[1], pair[2], _sibling_copies, after)
            from_sibling = done[len(keys):]
        sums = [
            _ew_call(lambda a, b: (a.astype(F32) + b.astype(F32),), "pair_sum", [(gb[k][0][None], (0, "g")), (r[None], (0, "g"))],
                     [(sds((1,) + r.shape, BF), (0, "g"))], N_QUARTERS)[0][0]
            for k, r in zip(keys, from_sibling)]
        zones = [lax.empty((3,) + a.shape[1:], BF) for a in sums]
        started = _exchange_start(f"reduce_start_{tag}", sums + zones, _owner_copies, 3 * len(keys), after)
        return keys, started

    def reduce_finish(tag, l, keys_started, after, reduced):
        keys, (send_sems, recv_sems, thru, _) = keys_started
        done = _exchange_wait(f"reduce_wait_{tag}", send_sems, recv_sems, thru, _owner_copies, after)
        sums, zones = done[:len(keys)], done[len(keys):]
        for i, k in enumerate(keys):
            r2, cols = halves[k]
            reduced[k] = _ew_call(
                lambda a, b, c, d: (((a.astype(F32) + b.astype(F32)) + c.astype(F32)) + d.astype(F32),),
                "quarter_sum", [(sums[i][None], (0, this_chip))] + [(zones[i][None], (0, j)) for j in range(3)],
                [(sds((DEPTH, 2, r2, cols), F32), (l, this_core))], 1, sel, into=reduced.get(k))[0]

    dx1, gb_ffn, gs1 = _layer_bwd_ffn(dx, sv1, big1, ts)
    merge_out, gb_merge = _layer_bwd_merge(dx1, sv1, big1, ts)
    dx_mid, gb_in, gs1_mix = _layer_bwd_branches(dx1, merge_out, sv1, big1, lru_lambda[1], ts)
    gb_1 = {**gb_ffn, **gb_merge, **gb_in}
    pair_1 = pair_start("1", gb_1, dx_mid)
    dx1, gb_ffn, gs0 = _layer_bwd_ffn(dx_mid, sv0, big0, ts, pair_1[3])
    exchange_1 = reduce_start("1", gb_1, dx1, pair_1)
    pair_0a = pair_start("0a", gb_ffn, exchange_1[1][3])
    merge_out, gb_merge = _layer_bwd_merge(dx1, sv0, big0, ts, pair_0a[3])
    exchange_0a = reduce_start("0a", gb_ffn, merge_out[0], pair_0a)
    pair_0b = pair_start("0b", gb_merge, exchange_0a[1][3])
    started_0b = {}

    def after_sgu(duv):
        started_0b["exchange"] = reduce_start("0b", gb_merge, duv, pair_0b)
        return started_0b["exchange"][1][3]

    grad_x, gb_in, gs0_mix = _layer_bwd_branches(dx1, merge_out, sv0, big0, lru_lambda[0], ts, pair_0b[3],
                                                 after_sgu)
    exchange_0b = started_0b["exchange"]
    exchange_0c = reduce_start("0c", gb_in, exchange_0b[1][3])
    layer_gs = [{**gs0, **gs0_mix}, {**gs1, **gs1_mix}]
    gs = {k: jnp.stack([g[k] for g in layer_gs]) for k in layer_gs[0]}
    gs["final_norm_g"] = dgf[0]
    gs["loss"] = loss[0, 0:1]

    me = ("sel", 3)
    piece = (1, N_DEVICES, SMALL_ROWS, 128)
    packed = _pack_small(gs).reshape(piece)
    scatter = _exchange_start("small_scatter_start", [packed[0], lax.empty(piece[1:], F32)], _small_scatter_copies,
                              N_DEVICES - 1, exchange_0c[1][3])
    reduced = {}
    reduce_finish("1", 1, exchange_1, scatter[3], reduced)
    reduce_finish("0a", 0, exchange_0a, reduced["w_in"], reduced)
    reduce_finish("0b", 0, exchange_0b, reduced["w_down"], reduced)

    def swap_slabs(ref, c, i):
        layers = (1,) if BIG[i] == "w_in" else range(DEPTH)
        return [(ref.at[l, c], ref.at[l, 1 - c]) for l in layers]

    swapped = dict(zip(BIG, _sibling_inplace_call("grads_swap_halves", [reduced[k] for k in BIG], swap_slabs,
                                                  DEPTH * len(BIG) - 1)))

    def adamw_layers(k, grad, layer, into, after=None):
        if layer is None:
            views = [_as4(_as_rows(a)) for a in (w[k], grad, m[k], v[k])]
            idx = (0, 0)
        else:
            views = [a.reshape((1,) + w[k].shape) for a in (w[k], grad, m[k], v[k])]
            idx = (0, layer)
        return _ew_call(_adamw, "adamw_big", [(a, idx) for a in views], [(sds(views[0].shape, F32), idx)] * 3,
                        into=into, after=after)

    updated, last_update = {}, None
    for k in BIG:
        updated[k] = adamw_layers(k, swapped[k], 1 if k == "w_in" else None, None, last_update)
        last_update = updated[k][0]
    scattered = _exchange_wait("small_scatter_wait", scatter[0], scatter[1], scatter[2], _small_scatter_copies,
                               last_update)
    summed = _ew_call(
        lambda *parts: (functools.reduce(lambda a, b: a + b, parts),), "small_sum",
        [(scattered[0][None], (0, me))]
        + [(scattered[1][None], (0, lambda g, s, k=k: s[3] ^ k)) for k in range(1, N_DEVICES)],
        [(sds(piece, F32), (0, me))], 1, sel)[0]
    spread = _exchange_start("small_spread_start", [summed[0]], _small_spread_copies, N_DEVICES - 1, summed)
    reduced["w_in"] = swapped["w_in"]
    reduce_finish("0c", 0, exchange_0c, spread[3], reduced)
    last = _sibling_inplace_call("grads_swap_last", [reduced["w_in"]],
                                 lambda ref, c, i: [(ref.at[0, c], ref.at[0, 1 - c])], 1)[0]
    swapped["w_in"] = last
    updated["w_in"] = adamw_layers("w_in", last, 0, updated["w_in"])
    grads_big = {k: swapped[k].reshape(w[k].shape) for k in BIG}
    delta, new_m, new_v = ({k: updated[k][j].reshape(w[k].shape) for k in BIG} for j in range(3))
    gathered_small = _exchange_wait("small_spread_wait", spread[0], spread[1], spread[2], _small_spread_copies,
                                    updated["w_in"][0])[0]

    like = {k: jax.ShapeDtypeStruct(gs[k].shape, F32) for k in SMALL}
    like["loss"] = jax.ShapeDtypeStruct((1,), F32)
    grads_small = _unpack_small(gathered_small, like)
    total = grads_small.pop("loss")[0]
    conv_q = grads_small["conv_w"].reshape(DEPTH, CONV_WIDTH, N_QUARTERS, D_RNN // N_QUARTERS)
    grads_small["conv_w"] = lax.dynamic_index_in_dim(conv_q, chip, axis=2, keepdims=False)
    at_least_2d = lambda a: a.reshape(1, -1) if a.ndim == 1 else a
    outs = _small_adamw_call(*[[at_least_2d(d[k]) for k in SMALL] for d in (w, grads_small, m, v)])
    for d, o in zip((delta, new_m, new_v), outs):
        for k, a in zip(SMALL, o):
            d[k] = a.reshape(w[k].shape)

    grads = {**grads_big, **grads_small}
    return (total, grad_x[None], *[grads[k] for k in WEIGHTS], *[delta[k] for k in WEIGHTS],
            *[new_m[k] for k in WEIGHTS], *[new_v[k] for k in WEIGHTS])
```

```python
import functools
import math

import jax
import jax.numpy as jnp
from jax import lax
from jax.experimental import pallas as pl
from jax.experimental.pallas import tpu as pltpu

F32 = jnp.float32
BF = jnp.bfloat16

DEPTH = 2
D_MODEL = 1024
D_RNN = 1280
D_SGU = 1024
D_FF = 4096
D_IN = 2 * D_RNN + 2 * D_SGU + 2 * D_MODEL
N_QUARTERS = 4
Q_IN = D_IN // N_QUARTERS
Q_FF = D_FF // N_QUARTERS
RNN_HEADS = 20
RNN_HEAD_DIM = 64
LRU_GROUP = 256
N_LRU_GROUPS = D_RNN // LRU_GROUP
HEADS_PER_GROUP = LRU_GROUP // RNN_HEAD_DIM
CONV_WIDTH = 4
LRU_C = 8.0
SGU_GROUPS = 8
SGU_BLOCK = 128
CHUNK = 64
EPS = 1e-6

ADAM_LR = 0.001
ADAM_B1 = 0.9
ADAM_B2 = 0.999
ADAM_EPS = 1e-08
ADAM_WD = 0.01
ADAM_STEP = 10

SUBLANES = 8
TOKEN_TILE = 512
VMEM_LIMIT_BYTES = 56 * 1024 * 1024

MESH = pl.DeviceIdType.MESH


def _params(semantics=None, vmem=True, **kw):
    return pltpu.CompilerParams(
        dimension_semantics=semantics,
        vmem_limit_bytes=VMEM_LIMIT_BYTES if vmem else None,
        **kw,
    )


def _dot(a, b):
    return jnp.dot(a, b, preferred_element_type=F32)


def _dot_nt(a, b):
    return lax.dot_general(a, b, (((1,), (1,)), ((), ())), preferred_element_type=F32)


def _dot_tn(a, b):
    return lax.dot_general(a, b, (((0,), (0,)), ((), ())), preferred_element_type=F32)


_GELU_C = math.sqrt(2.0 / math.pi)
_GELU_A = 0.044715


def _gelu(x):
    return 0.5 * x * (1.0 + jnp.tanh(_GELU_C * (x + _GELU_A * x * x * x)))


def _gelu_and_grad(x):
    x2 = x * x
    t = jnp.tanh(_GELU_C * (x + _GELU_A * x2 * x))
    du = _GELU_C * (1.0 + 3.0 * _GELU_A * x2)
    return 0.5 * x * (1.0 + t), 0.5 * (1.0 + t) + 0.5 * x * (1.0 - t * t) * du


def _rms_stats(x):
    return lax.rsqrt(jnp.mean(x * x, axis=-1, keepdims=True) + EPS)


def _rms_bwd(dy, x, g):
    rs = _rms_stats(x)
    n = x * rs
    dn = dy * g
    dx = rs * (dn - n * jnp.mean(dn * n, axis=-1, keepdims=True))
    return dx, dy * n


def _row_sum(x):
    return jnp.sum(x, axis=0, keepdims=True)


def _tile_spec(ts, width, col=0):
    return pl.BlockSpec((ts, width), lambda i, col=col: (i, col))


def _full_spec(shape):
    zeros = (0,) * len(shape)
    return pl.BlockSpec(shape, lambda *_: zeros)


def _layer_spec(w, layer):
    zeros = (0,) * (w.ndim - 1)
    return pl.BlockSpec((None,) + tuple(w.shape[1:]), lambda *_: (layer,) + zeros)


def _with_after(body, n_in, after):
    if after is None:
        return body, [], []

    def wrapped(*refs):
        return body(*refs[:n_in], *refs[n_in + 1:])

    return wrapped, [pl.BlockSpec(memory_space=pl.ANY)], [after]


def _norm_call(x, g, ts, after=None):
    s = x.shape[0]

    def body(x_ref, g_ref, h_ref):
        xv = x_ref[...]
        h_ref[...] = (xv * _rms_stats(xv) * g_ref[...]).astype(BF)

    body, more_specs, more = _with_after(body, 2, after)
    return pl.pallas_call(
        body, name="norm_fwd", grid=(s // ts,),
        in_specs=[_tile_spec(ts, D_MODEL), _full_spec((1, D_MODEL))] + more_specs,
        out_specs=_tile_spec(ts, D_MODEL),
        out_shape=jax.ShapeDtypeStruct((s, D_MODEL), BF),
        compiler_params=_params(("parallel",)),
    )(x, g, *more)


def _inproj_call(h, w_in, layer, ts):
    s = h.shape[0]

    def body(h_ref, w_ref, o_ref):
        o_ref[...] = _dot(h_ref[...], w_ref[...]).astype(BF)

    return pl.pallas_call(
        body, name="inproj_fwd", grid=(N_QUARTERS, s // ts),
        in_specs=[
            pl.BlockSpec((ts, D_MODEL), lambda q, i: (i, 0)),
            pl.BlockSpec((None, None, D_MODEL, Q_IN), lambda q, i: (layer, q, 0, 0)),
        ],
        out_specs=pl.BlockSpec((ts, Q_IN), lambda q, i: (i, q)),
        out_shape=jax.ShapeDtypeStruct((s, D_IN), BF),
        compiler_params=_params(("parallel", "parallel")),
    )(h, w_in)


def _inproj_part_call(h, w_in, ts, own, first, count, into=None):
    s = h.shape[0]

    def quarter(j, sel):
        return (sel[0] + first + j) % N_QUARTERS

    def body(sel_ref, h_ref, w_ref, *rest):
        rest[-1][...] = _dot(h_ref[...], w_ref[...]).astype(BF)

    in_specs = [pl.BlockSpec((ts, D_MODEL), lambda j, i, sel: (i, 0)),
                pl.BlockSpec((None, None, D_MODEL, Q_IN), lambda j, i, sel: (0, quarter(j, sel), 0, 0))]
    operands = [h, w_in]
    aliases = {}
    if into is not None:
        in_specs.append(pl.BlockSpec(memory_space=pl.ANY))
        operands.append(into)
        aliases = {3: 0}
    return pl.pallas_call(
        body, name="inproj_fwd_part", out_shape=jax.ShapeDtypeStruct((s, D_IN), BF),
        grid_spec=pltpu.PrefetchScalarGridSpec(
            num_scalar_prefetch=1, grid=(count, s // ts), in_specs=in_specs,
            out_specs=pl.BlockSpec((ts, Q_IN), lambda j, i, sel: (i, quarter(j, sel)))),
        input_output_aliases=aliases,
        compiler_params=_params(("parallel", "parallel")),
    )(own, *operands)


def _shift_down(x, tail, s):
    xr = pltpu.roll(x, s, 0)
    tr = pltpu.roll(tail, s, 0)
    row = lax.broadcasted_iota(jnp.int32, tail.shape, 0)
    top = jnp.where(row < s, tr, xr[0:SUBLANES])
    return jnp.concatenate([top, xr[SUBLANES:]], axis=0)


def _shift_up(x, head, s):
    t = x.shape[0]
    xr = pltpu.roll(x, t - s, 0)
    hr = pltpu.roll(head, SUBLANES - s, 0)
    row = lax.broadcasted_iota(jnp.int32, head.shape, 0)
    bottom = jnp.where(row >= SUBLANES - s, hr, xr[t - SUBLANES:])
    return jnp.concatenate([xr[: t - SUBLANES], bottom], axis=0)


def _conv_fwd(x, tail, cw_ref, cb_ref):
    out = cb_ref[...] + cw_ref[CONV_WIDTH - 1:CONV_WIDTH, :] * x
    for s in range(1, CONV_WIDTH):
        k = CONV_WIDTH - 1 - s
        out = out + cw_ref[k:k + 1, :] * _shift_down(x, tail, s)
    return out


def _group_dot(x_bf, w_ref, dot):
    cols = [dot(x_bf[:, g * LRU_GROUP:(g + 1) * LRU_GROUP], w_ref[g]) for g in range(N_LRU_GROUPS)]
    return jnp.concatenate(cols, axis=1)


def _lru_gates(xr, wa_ref, wx_ref, ba_ref, bx_ref, sp_ref):
    xb = xr.astype(BF)
    r = jax.nn.sigmoid(_group_dot(xb, wa_ref, _dot) + ba_ref[...])
    i = jax.nn.sigmoid(_group_dot(xb, wx_ref, _dot) + bx_ref[...])
    log_a = (-LRU_C * r) * sp_ref[...]
    a = jnp.exp(log_a)
    nrm2 = -jnp.tanh(log_a) * (a * a + 1.0)
    inv_nrm = lax.rsqrt(jnp.maximum(nrm2, 1e-36))
    return r, i, a, nrm2 * inv_nrm, inv_nrm


def _linear_scan(a, b, carry, al_ref, bl_ref, h_ref, reverse):
    t, c = a.shape
    rowm = lax.broadcasted_iota(jnp.int32, (t, c), 0) & (SUBLANES - 1)
    for d in (1, 2, 4):
        if reverse:
            keep, sh = rowm < SUBLANES - d, t - d
        else:
            keep, sh = rowm >= d, d
        a_sh = jnp.where(keep, pltpu.roll(a, sh, 0), 1.0)
        b_sh = jnp.where(keep, pltpu.roll(b, sh, 0), 0.0)
        b = a * b_sh + b
        a = a * a_sh
    al_ref[...] = a
    bl_ref[...] = b
    groups = t // SUBLANES

    def step(j, state):
        jj = groups - 1 - j if reverse else j
        off = pl.multiple_of(jj * SUBLANES, SUBLANES)
        rows = bl_ref[pl.ds(off, SUBLANES), :] + al_ref[pl.ds(off, SUBLANES), :] * state
        h_ref[pl.ds(off, SUBLANES), :] = rows
        last = rows[0:1, :] if reverse else rows[SUBLANES - 1:SUBLANES, :]
        return jnp.broadcast_to(last, (SUBLANES, c))

    out = lax.fori_loop(0, groups, step, jnp.broadcast_to(carry, (SUBLANES, c)))
    return out[0:1, :]


def _rnn_fwd_call(proj, wa, wx, ba, bx, sp, cw, cb, ts):
    s = proj.shape[0]

    def body(xg_ref, wa_ref, wx_ref, ba_ref, bx_ref, sp_ref, cw_ref, cb_ref, xr_ref, hr_ref, ya_ref,
             tail_sc, carry_sc, al_sc, bl_sc, h_sc):
        @pl.when(pl.program_id(0) == 0)
        def _():
            tail_sc[...] = jnp.zeros_like(tail_sc)
            carry_sc[...] = jnp.zeros_like(carry_sc)

        x = xg_ref[:, :D_RNN].astype(F32)
        g = xg_ref[:, D_RNN:]
        xr = _conv_fwd(x, tail_sc[...], cw_ref, cb_ref)
        tail_sc[...] = x[ts - SUBLANES:, :]
        xr_ref[...] = xr.astype(BF)
        _, i, a, nrm, _ = _lru_gates(xr, wa_ref, wx_ref, ba_ref, bx_ref, sp_ref)
        carry_sc[...] = _linear_scan(a, nrm * (i * xr), carry_sc[...], al_sc, bl_sc, h_sc, False)
        h = h_sc[...]
        hr_ref[...] = h.astype(BF)
        ya_ref[...] = (h * _gelu(g)).astype(BF)

    gw = (N_LRU_GROUPS, LRU_GROUP, LRU_GROUP)
    return pl.pallas_call(
        body, name="rnn_fwd", grid=(s // ts,),
        in_specs=[_tile_spec(ts, 2 * D_RNN), _full_spec(gw), _full_spec(gw),
                  _full_spec((1, D_RNN)), _full_spec((1, D_RNN)), _full_spec((1, D_RNN)),
                  _full_spec((CONV_WIDTH, D_RNN)), _full_spec((1, D_RNN))],
        out_specs=[_tile_spec(ts, D_RNN)] * 3,
        out_shape=[jax.ShapeDtypeStruct((s, D_RNN), BF)] * 3,
        scratch_shapes=[pltpu.VMEM((SUBLANES, D_RNN), F32), pltpu.VMEM((1, D_RNN), F32),
                        pltpu.VMEM((ts, D_RNN), F32), pltpu.VMEM((ts, D_RNN), F32),
                        pltpu.VMEM((ts, D_RNN), F32)],
        compiler_params=_params(("arbitrary",)),
    )(proj, wa, wx, ba, bx, sp, cw, cb)


def _layernorm_fwd(x):
    mu = jnp.mean(x, axis=-1, keepdims=True)
    xc = x - mu
    rstd = lax.rsqrt(jnp.mean(xc * xc, axis=-1, keepdims=True) + EPS)
    return xc * rstd, rstd


def _sgu_mix(vn_bf, wm_ref, bsb_ref, ts):
    rows = []
    for blk in range(ts // SGU_BLOCK):
        r0 = blk * SGU_BLOCK
        cols = [
            _dot(wm_ref[g], vn_bf[r0:r0 + SGU_BLOCK, g * SGU_BLOCK:(g + 1) * SGU_BLOCK]) + bsb_ref[g]
            for g in range(SGU_GROUPS)
        ]
        rows.append(jnp.concatenate(cols, axis=1))
    return jnp.concatenate(rows, axis=0)


def _sgu_fwd_call(proj, wm, bsb, lg, lb, ts, after=None):
    s = proj.shape[0]

    def body(uv_ref, wm_ref, bsb_ref, lg_ref, lb_ref, yb_ref):
        gu = _gelu(uv_ref[:, :D_SGU])
        gv = _gelu(uv_ref[:, D_SGU:2 * D_SGU]).astype(F32)
        nh, _ = _layernorm_fwd(gv)
        vn = (nh * lg_ref[...] + lb_ref[...]).astype(BF)
        yb_ref[...] = (gu * _sgu_mix(vn, wm_ref, bsb_ref, ts)).astype(BF)

    sw = (SGU_GROUPS, SGU_BLOCK, SGU_BLOCK)
    body, more_specs, more = _with_after(body, 5, after)
    return pl.pallas_call(
        body, name="sgu_fwd", grid=(s // ts,),
        in_specs=[_tile_spec(ts, 2 * D_RNN, 1), _full_spec(sw), _full_spec(sw),
                  _full_spec((1, D_SGU)), _full_spec((1, D_SGU))] + more_specs,
        out_specs=_tile_spec(ts, D_SGU),
        out_shape=jax.ShapeDtypeStruct((s, D_SGU), BF),
        compiler_params=_params(("parallel",)),
    )(proj, wm, bsb, lg, lb, *more)


_GATE_COL0 = (2 * D_RNN + 2 * D_SGU) // 512


def _gate_specs(ts):
    return [_tile_spec(ts, 512, _GATE_COL0 + j) for j in range(4)]


def _merge_call(x, proj, ya_pre, yb_pre, w_ba, w_bb, w_out, g2, layer, ts):
    s = x.shape[0]

    def body(x_ref, ga0, ga1, gb0, gb1, ya_ref, yb_ref, wa_ref, wb_ref, wo_ref, g2_ref,
             x1_ref, yao_ref, ybo_ref, mg_ref, h2_ref):
        ya = _dot(ya_ref[...], wa_ref[...])
        yb = _dot(yb_ref[...], wb_ref[...])
        sa = jax.nn.sigmoid(jnp.concatenate([ga0[...], ga1[...]], axis=1).astype(F32))
        sb = jax.nn.sigmoid(jnp.concatenate([gb0[...], gb1[...]], axis=1).astype(F32))
        merged = (sa * ya + sb * yb).astype(BF)
        x1 = x_ref[...] + _dot(merged, wo_ref[...])
        x1_ref[...] = x1
        yao_ref[...] = ya.astype(BF)
        ybo_ref[...] = yb.astype(BF)
        mg_ref[...] = merged
        h2_ref[...] = (x1 * _rms_stats(x1) * g2_ref[...]).astype(BF)

    act = jax.ShapeDtypeStruct((s, D_MODEL), BF)
    return pl.pallas_call(
        body, name="merge_fwd", grid=(s // ts,),
        in_specs=[_tile_spec(ts, D_MODEL)] + _gate_specs(ts) + [
            _tile_spec(ts, D_RNN), _tile_spec(ts, D_SGU),
            _layer_spec(w_ba, layer), _layer_spec(w_bb, layer), _layer_spec(w_out, layer),
            _full_spec((1, D_MODEL))],
        out_specs=[_tile_spec(ts, D_MODEL)] * 5,
        out_shape=[jax.ShapeDtypeStruct((s, D_MODEL), F32), act, act, act, act],
        compiler_params=_params(("parallel",)),
    )(x, proj, proj, proj, proj, ya_pre, yb_pre, w_ba, w_bb, w_out, g2)


def _ffn_call(x1, h2, w_up, w_down, layer, ts):
    s = x1.shape[0]

    def body(x1_ref, h2_ref, wu_ref, wd_ref, x2_ref, p_ref):
        h2v = h2_ref[...]
        acc = x1_ref[...]
        for q in range(N_QUARTERS):
            p = _dot(h2v, wu_ref[q])
            p_ref[:, q * Q_FF:(q + 1) * Q_FF] = p.astype(BF)
            f = jnp.square(jnp.maximum(p, 0.0)).astype(BF)
            acc = acc + _dot(f, wd_ref[q * Q_FF:(q + 1) * Q_FF, :])
        x2_ref[...] = acc

    return pl.pallas_call(
        body, name="ffn_fwd", grid=(s // ts,),
        in_specs=[_tile_spec(ts, D_MODEL), _tile_spec(ts, D_MODEL),
                  pl.BlockSpec((None, N_QUARTERS, D_MODEL, Q_FF), lambda i: (layer, 0, 0, 0)),
                  pl.BlockSpec((None, D_FF, D_MODEL), lambda i: (layer, 0, 0))],
        out_specs=[_tile_spec(ts, D_MODEL), _tile_spec(ts, D_FF)],
        out_shape=[jax.ShapeDtypeStruct((s, D_MODEL), F32), jax.ShapeDtypeStruct((s, D_FF), BF)],
        compiler_params=_params(("parallel",)),
    )(x1, h2, w_up, w_down)


def _loss_call(x, target, gf, ts):
    s = x.shape[0]

    def body(x_ref, t_ref, g_ref, dx_ref, loss_ref, dg_ref):
        @pl.when(pl.program_id(0) == 0)
        def _():
            loss_ref[...] = jnp.zeros_like(loss_ref)
            dg_ref[...] = jnp.zeros_like(dg_ref)

        xv = x_ref[...]
        gv = g_ref[...]
        err = xv * _rms_stats(xv) * gv - t_ref[...]
        part = 0.5 * jnp.sum(jnp.mean(err * err, axis=-1, keepdims=True), axis=0, keepdims=True)
        loss_ref[...] += jnp.broadcast_to(part, loss_ref.shape)
        dx, dg = _rms_bwd(err * (1.0 / D_MODEL), xv, gv)
        dx_ref[...] = dx
        dg_ref[...] += _row_sum(dg)

    return pl.pallas_call(
        body, name="loss_head", grid=(s // ts,),
        in_specs=[_tile_spec(ts, D_MODEL), _tile_spec(ts, D_MODEL), _full_spec((1, D_MODEL))],
        out_specs=[_tile_spec(ts, D_MODEL), _full_spec((1, 128)), _full_spec((1, D_MODEL))],
        out_shape=[jax.ShapeDtypeStruct((s, D_MODEL), F32), jax.ShapeDtypeStruct((1, 128), F32),
                   jax.ShapeDtypeStruct((1, D_MODEL), F32)],
        compiler_params=_params(("arbitrary",)),
    )(x, target, gf)


def _ffn_bwd_call(dx2, p, x1, g2, w_up, w_down, layer, ts, after=None):
    s = dx2.shape[0]

    def body(dx2_ref, p_ref, x1_ref, g2_ref, wu_ref, wd_ref, dx1_ref, dp_ref, dg_ref, dx2b_ref, dx1b_ref):
        @pl.when(pl.program_id(0) == 0)
        def _():
            dg_ref[...] = jnp.zeros_like(dg_ref)

        dx2v = dx2_ref[...]
        dyb = dx2v.astype(BF)
        dx2b_ref[...] = dyb
        dh2 = jnp.zeros((ts, D_MODEL), F32)
        for q in range(N_QUARTERS):
            cols = slice(q * Q_FF, (q + 1) * Q_FF)
            df = _dot_nt(dyb, wd_ref[cols, :])
            dp = (df * (2.0 * jnp.maximum(p_ref[:, cols].astype(F32), 0.0))).astype(BF)
            dp_ref[:, cols] = dp
            dh2 = dh2 + _dot_nt(dp, wu_ref[q])
        dx, dg = _rms_bwd(dh2, x1_ref[...], g2_ref[...])
        dx1 = dx2v + dx
        dx1_ref[...] = dx1
        dx1b_ref[...] = dx1.astype(BF)
        dg_ref[...] += _row_sum(dg)

    body, more_specs, more = _with_after(body, 6, after)
    return pl.pallas_call(
        body, name="ffn_bwd", grid=(s // ts,),
        in_specs=[_tile_spec(ts, D_MODEL), _tile_spec(ts, D_FF), _tile_spec(ts, D_MODEL),
                  _full_spec((1, D_MODEL)),
                  pl.BlockSpec((None, N_QUARTERS, D_MODEL, Q_FF), lambda i: (layer, 0, 0, 0)),
                  pl.BlockSpec((None, D_FF, D_MODEL), lambda i: (layer, 0, 0))] + more_specs,
        out_specs=[_tile_spec(ts, D_MODEL), _tile_spec(ts, D_FF), _full_spec((1, D_MODEL)),
                   _tile_spec(ts, D_MODEL), _tile_spec(ts, D_MODEL)],
        out_shape=[jax.ShapeDtypeStruct((s, D_MODEL), F32), jax.ShapeDtypeStruct((s, D_FF), BF),
                   jax.ShapeDtypeStruct((1, D_MODEL), F32),
                   jax.ShapeDtypeStruct((s, D_MODEL), BF), jax.ShapeDtypeStruct((s, D_MODEL), BF)],
        compiler_params=_params(("arbitrary",)),
    )(dx2, p, x1, g2, w_up, w_down, *more)


def _merge_bwd_call(dx1, proj, ya, yb, w_ba, w_bb, w_out, layer, ts, after=None):
    s = dx1.shape[0]

    def body(dx1_ref, ga0, ga1, gb0, gb1, ya_ref, yb_ref, wa_ref, wb_ref, wo_ref, *rest):
        dya_ref, dyb_ref, dgate_ref, dyap_ref, dybp_ref = rest[-5:]
        dm = _dot_nt(dx1_ref[...].astype(BF), wo_ref[...])
        sa = jax.nn.sigmoid(jnp.concatenate([ga0[...], ga1[...]], axis=1).astype(F32))
        sb = jax.nn.sigmoid(jnp.concatenate([gb0[...], gb1[...]], axis=1).astype(F32))
        dya = (dm * sa).astype(BF)
        dyb = (dm * sb).astype(BF)
        dya_ref[...] = dya
        dyb_ref[...] = dyb
        dgate_ref[:, :D_MODEL] = (dm * ya_ref[...].astype(F32) * sa * (1.0 - sa)).astype(BF)
        dgate_ref[:, D_MODEL:] = (dm * yb_ref[...].astype(F32) * sb * (1.0 - sb)).astype(BF)
        dyap_ref[...] = _dot_nt(dya, wa_ref[...]).astype(BF)
        dybp_ref[...] = _dot_nt(dyb, wb_ref[...]).astype(BF)

    act = jax.ShapeDtypeStruct((s, D_MODEL), BF)
    return pl.pallas_call(
        body, name="merge_bwd", grid=(s // ts,),
        in_specs=[_tile_spec(ts, D_MODEL)] + _gate_specs(ts) + [
            _tile_spec(ts, D_MODEL), _tile_spec(ts, D_MODEL),
            _layer_spec(w_ba, layer), _layer_spec(w_bb, layer), _layer_spec(w_out, layer)]
        + ([] if after is None else [pl.BlockSpec(memory_space=pl.ANY)]),
        out_specs=[_tile_spec(ts, D_MODEL), _tile_spec(ts, D_MODEL), _tile_spec(ts, 2 * D_MODEL),
                   _tile_spec(ts, D_RNN), _tile_spec(ts, D_SGU)],
        out_shape=[act, act, jax.ShapeDtypeStruct((s, 2 * D_MODEL), BF),
                   jax.ShapeDtypeStruct((s, D_RNN), BF), jax.ShapeDtypeStruct((s, D_SGU), BF)],
        compiler_params=_params(("parallel",)),
    )(dx1, proj, proj, proj, proj, ya, yb, w_ba, w_bb, w_out, *([] if after is None else [after]))


def _sgu_bwd_call(dyb_pre, proj, wm, bsb, mask, lg, lb, ts, after=None):
    s = proj.shape[0]

    def body(dy_ref, uv_ref, wm_ref, bsb_ref, mask_ref, lg_ref, lb_ref,
             duv_ref, dws_ref, dbs_ref, dlg_ref, dlb_ref, dm_sc):
        step = pl.program_id(0)

        @pl.when(step == 0)
        def _():
            dws_ref[...] = jnp.zeros_like(dws_ref)
            dlg_ref[...] = jnp.zeros_like(dlg_ref)
            dlb_ref[...] = jnp.zeros_like(dlb_ref)
            dm_sc[...] = jnp.zeros_like(dm_sc)

        gu, dgu_du = _gelu_and_grad(uv_ref[:, :D_SGU])
        gv, dgv_dv = _gelu_and_grad(uv_ref[:, D_SGU:2 * D_SGU])
        nh, rstd = _layernorm_fwd(gv.astype(F32))
        lgv = lg_ref[...]
        vn = (nh * lgv + lb_ref[...]).astype(BF)
        dy = dy_ref[...].astype(F32)
        du = dy * _sgu_mix(vn, wm_ref, bsb_ref, ts) * dgu_du
        dmix = dy * gu
        dmix_bf = dmix.astype(BF)
        dm_acc = dm_sc[...]
        rows = []
        for blk in range(ts // SGU_BLOCK):
            r0 = blk * SGU_BLOCK
            dm_acc = dm_acc + dmix[r0:r0 + SGU_BLOCK, :]
            cols = []
            for g in range(SGU_GROUPS):
                c0 = g * SGU_BLOCK
                dmg = dmix_bf[r0:r0 + SGU_BLOCK, c0:c0 + SGU_BLOCK]
                cols.append(_dot_tn(wm_ref[g], dmg))
                dws_ref[g] += _dot_nt(dmg, vn[r0:r0 + SGU_BLOCK, c0:c0 + SGU_BLOCK])
            rows.append(jnp.concatenate(cols, axis=1))
        dm_sc[...] = dm_acc
        dvn = jnp.concatenate(rows, axis=0)
        dlg_ref[...] += _row_sum(dvn * nh)
        dlb_ref[...] += _row_sum(dvn)
        dnh = dvn * lgv
        dgv = rstd * (dnh - jnp.mean(dnh, axis=-1, keepdims=True)
                      - nh * jnp.mean(dnh * nh, axis=-1, keepdims=True))
        duv_ref[:, :D_SGU] = du.astype(BF)
        duv_ref[:, D_SGU:] = (dgv * dgv_dv).astype(BF)

        @pl.when(step == pl.num_programs(0) - 1)
        def _():
            for g in range(SGU_GROUPS):
                dws_ref[g] = dws_ref[g] * mask_ref[...]
                dbs_ref[:, g:g + 1] = jnp.sum(
                    dm_acc[:, g * SGU_BLOCK:(g + 1) * SGU_BLOCK], axis=1, keepdims=True)

    sw = (SGU_GROUPS, SGU_BLOCK, SGU_BLOCK)
    body, more_specs, more = _with_after(body, 7, after)
    return pl.pallas_call(
        body, name="sgu_bwd", grid=(s // ts,),
        in_specs=[_tile_spec(ts, D_SGU), _tile_spec(ts, 2 * D_RNN, 1), _full_spec(sw), _full_spec(sw),
                  _full_spec((SGU_BLOCK, SGU_BLOCK)), _full_spec((1, D_SGU)), _full_spec((1, D_SGU))] + more_specs,
        out_specs=[_tile_spec(ts, 2 * D_SGU), _full_spec(sw), _full_spec((SGU_BLOCK, SGU_GROUPS)),
                   _full_spec((1, D_SGU)), _full_spec((1, D_SGU))],
        out_shape=[jax.ShapeDtypeStruct((s, 2 * D_SGU), BF), jax.ShapeDtypeStruct(sw, F32),
                   jax.ShapeDtypeStruct((SGU_BLOCK, SGU_GROUPS), F32),
                   jax.ShapeDtypeStruct((1, D_SGU), F32), jax.ShapeDtypeStruct((1, D_SGU), F32)],
        scratch_shapes=[pltpu.VMEM((SGU_BLOCK, D_SGU), F32)],
        compiler_params=_params(("arbitrary",)),
    )(dyb_pre, proj, wm, bsb, mask, lg, lb, *more)


_ROW_DBA, _ROW_DBX, _ROW_DSP, _ROW_DCB, _ROW_DCW = 0, 1, 2, 3, 4
_PREV_ROWS = 16


def _rnn_bwd_call(dya_pre, proj, xr_saved, hr, wa, wx, ba, bx, sp, cw, ts, after=None):
    s = proj.shape[0]
    nt = s // ts
    per = ts // _PREV_ROWS

    def tile(i):
        return nt - 1 - i

    def prev(i):
        return jnp.maximum(tile(i) * per - 1, 0)

    def body(dy_ref, xg_ref, xr_ref, hr_ref, hrp_ref, wa_ref, wx_ref, ba_ref, bx_ref, sp_ref,
             cw_ref, dxg_ref, dwa_ref, dwx_ref, vec_ref,
             lam_carry, a_first, dxr_head, al_sc, bl_sc, lam_sc):
        step = pl.program_id(0)

        @pl.when(step == 0)
        def _():
            dwa_ref[...] = jnp.zeros_like(dwa_ref)
            dwx_ref[...] = jnp.zeros_like(dwx_ref)
            vec_ref[...] = jnp.zeros_like(vec_ref)
            lam_carry[...] = jnp.zeros_like(lam_carry)
            a_first[...] = jnp.zeros_like(a_first)
            dxr_head[...] = jnp.zeros_like(dxr_head)

        has_prev = (step < nt - 1).astype(F32)
        x = xg_ref[:, :D_RNN].astype(F32)
        g = xg_ref[:, D_RNN:]
        h_tail =hrp_ref[_PREV_ROWS - SUBLANES:, :].astype(F32) * has_prev
        xr = xr_ref[...].astype(F32)
        r, i, a, nrm, inv_nrm = _lru_gates(xr, wa_ref, wx_ref, ba_ref, bx_ref, sp_ref)
        h = hr_ref[...].astype(F32)
        dy = dy_ref[...].astype(F32)
        gg, dgg = _gelu_and_grad(g)

        coef = _shift_up(a, jnp.broadcast_to(a_first[...], (SUBLANES, D_RNN)), 1)
        lam_carry[...] = _linear_scan(coef, dy * gg, lam_carry[...], al_sc, bl_sc, lam_sc, True)
        a_first[...] = a[0:1, :]
        lam = lam_sc[...]

        da = lam * _shift_down(h, h_tail, 1)
        dnrm = lam * (i * xr)
        di = lam * nrm * xr
        dlog_a = da * a - dnrm * (a * a) * inv_nrm
        spv = sp_ref[...]
        dza = (dlog_a * (-LRU_C * spv)) * (r * (1.0 - r))
        dzx = di * (i * (1.0 - i))
        vec_ref[_ROW_DSP:_ROW_DSP + 1, :] += _row_sum(dlog_a * (-LRU_C * r))
        vec_ref[_ROW_DBA:_ROW_DBA + 1, :] += _row_sum(dza)
        vec_ref[_ROW_DBX:_ROW_DBX + 1, :] += _row_sum(dzx)
        xb = xr.astype(BF)
        dza_bf = dza.astype(BF)
        dzx_bf = dzx.astype(BF)
        for grp in range(N_LRU_GROUPS):
            cols = slice(grp * LRU_GROUP, (grp + 1) * LRU_GROUP)
            dwa_ref[grp] += _dot_tn(xb[:, cols], dza_bf[:, cols])
            dwx_ref[grp] += _dot_tn(xb[:, cols], dzx_bf[:, cols])
        dxr = (lam * nrm * i + _group_dot(dza_bf, wa_ref, _dot_nt) + _group_dot(dzx_bf, wx_ref, _dot_nt))

        vec_ref[_ROW_DCB:_ROW_DCB + 1, :] += _row_sum(dxr)
        head = dxr_head[...]
        dx = cw_ref[CONV_WIDTH - 1:CONV_WIDTH, :] * dxr
        vec_ref[_ROW_DCW + 3:_ROW_DCW + 4, :] += _row_sum(dxr * x)
        for sft in range(1, CONV_WIDTH):
            k = CONV_WIDTH - 1 - sft
            ahead = _shift_up(dxr, head, sft)
            dx = dx + cw_ref[k:k + 1, :] * ahead
            vec_ref[_ROW_DCW + k:_ROW_DCW + k + 1, :] += _row_sum(ahead * x)
        dxr_head[...] = dxr[0:SUBLANES, :]
        dxg_ref[:, :D_RNN] = dx.astype(BF)
        dxg_ref[:, D_RNN:] = (dy * h * dgg).astype(BF)

    gw = (N_LRU_GROUPS, LRU_GROUP, LRU_GROUP)
    rev = lambda width: pl.BlockSpec((ts, width), lambda i: (tile(i), 0))
    body, more_specs, more = _with_after(body, 11, after)
    return pl.pallas_call(
        body, name="rnn_bwd", grid=(nt,),
        in_specs=[rev(D_RNN), rev(2 * D_RNN), rev(D_RNN), rev(D_RNN),
                  pl.BlockSpec((_PREV_ROWS, D_RNN), lambda i: (prev(i), 0)),
                  _full_spec(gw), _full_spec(gw),
                  _full_spec((1, D_RNN)), _full_spec((1, D_RNN)), _full_spec((1, D_RNN)),
                  _full_spec((CONV_WIDTH, D_RNN))] + more_specs,
        out_specs=[rev(2 * D_RNN), _full_spec(gw), _full_spec(gw), _full_spec((SUBLANES, D_RNN))],
        out_shape=[jax.ShapeDtypeStruct((s, 2 * D_RNN), BF), jax.ShapeDtypeStruct(gw, F32),
                   jax.ShapeDtypeStruct(gw, F32), jax.ShapeDtypeStruct((SUBLANES, D_RNN), F32)],
        scratch_shapes=[pltpu.VMEM((1, D_RNN), F32), pltpu.VMEM((1, D_RNN), F32),
                        pltpu.VMEM((SUBLANES, D_RNN), F32),
                        pltpu.VMEM((ts, D_RNN), F32), pltpu.VMEM((ts, D_RNN), F32),
                        pltpu.VMEM((ts, D_RNN), F32)],
        compiler_params=_params(("arbitrary",)),
    )(dya_pre, proj, xr_saved, hr, hr, wa, wx, ba, bx, sp, cw, *more)


def _inproj_bwd_call(dxg, duv, dgate, dx1, x, g1, w_in, layer, ts):
    s = x.shape[0]

    def body(dxg_ref, duv_ref, dgt_ref, dx1_ref, x_ref, g_ref, w_ref, dx_ref, dproj_ref, dg_ref):
        @pl.when(pl.program_id(0) == 0)
        def _():
            dg_ref[...] = jnp.zeros_like(dg_ref)

        dproj = jnp.concatenate([dxg_ref[...], duv_ref[...], dgt_ref[...]], axis=1)
        dproj_ref[...] = dproj
        dh = jnp.zeros((ts, D_MODEL), F32)
        for q in range(N_QUARTERS):
            dh = dh + _dot_nt(dproj[:, q * Q_IN:(q + 1) * Q_IN], w_ref[q])
        dx, dg = _rms_bwd(dh, x_ref[...], g_ref[...])
        dx_ref[...] = dx1_ref[...] + dx
        dg_ref[...] += _row_sum(dg)

    return pl.pallas_call(
        body, name="inproj_bwd", grid=(s // ts,),
        in_specs=[_tile_spec(ts, 2 * D_RNN), _tile_spec(ts, 2 * D_SGU), _tile_spec(ts, 2 * D_MODEL),
                  _tile_spec(ts, D_MODEL), _tile_spec(ts, D_MODEL), _full_spec((1, D_MODEL)),
                  pl.BlockSpec((None, N_QUARTERS, D_MODEL, Q_IN), lambda i: (layer, 0, 0, 0))],
        out_specs=[_tile_spec(ts, D_MODEL), _tile_spec(ts, D_IN), _full_spec((1, D_MODEL))],
        out_shape=[jax.ShapeDtypeStruct((s, D_MODEL), F32), jax.ShapeDtypeStruct((s, D_IN), BF),
                   jax.ShapeDtypeStruct((1, D_MODEL), F32)],
        compiler_params=_params(("arbitrary",)),
    )(dxg, duv, dgate, dx1, x, g1, w_in)


def _relu_sq(p):
    return jnp.square(jnp.maximum(p, 0))


def _wgrad_call(a, b, core, tm, tn, tk, col_blocked, name, a_fn=None):
    s, m = a.shape
    n = b.shape[1]
    r, cols = (m, n // N_QUARTERS) if col_blocked else (m // N_QUARTERS, n)
    r2 = r // 2
    per_tile = tm // r
    steps = s // tk
    assert per_tile > 0 or steps == 1

    def body(core_ref, a_ref, b_ref, keep_ref, send_ref, *acc):
        av = a_ref[...]
        if a_fn is not None:
            av = a_fn(av)
        prod = _dot_tn(av.astype(BF), b_ref[...].astype(BF))

        def emit(total):
            for h in range(2):
                @pl.when(core_ref[0] == h)
                def _():
                    for q in range(per_tile):
                        keep_ref[q] = total[q * r + h * r2:q * r + (h + 1) * r2].astype(BF)
                        send_ref[q] = total[q * r + (1 - h) * r2:q * r + (2 - h) * r2].astype(BF)

        if per_tile == 0:
            mine = pl.program_id(1) == core_ref[0]

            @pl.when(mine)
            def _():
                keep_ref[0] = prod.astype(BF)

            @pl.when(jnp.logical_not(mine))
            def _():
                send_ref[0] = prod.astype(BF)
        elif steps == 1:
            emit(prod)
        else:
            acc_ref, = acc
            step = pl.program_id(2)

            @pl.when(step == 0)
            def _():
                acc_ref[...] = prod

            @pl.when(jnp.logical_and(step > 0, step < steps - 1))
            def _():
                acc_ref[...] += prod

            @pl.when(step == steps - 1)
            def _():
                emit(acc_ref[...] + prod)

    if col_blocked:
        per_q = cols // tn
        out_spec = pl.BlockSpec((1, r2, tn), lambda j, i, k, c: (j // per_q, 0, j % per_q))
    else:
        out_spec = pl.BlockSpec((per_tile, r2, tn), lambda j, i, k, c: (i, 0, j))
    return pl.pallas_call(
        body, name=name,
        out_shape=[jax.ShapeDtypeStruct((N_QUARTERS, r2, cols), BF)] * 2,
        grid_spec=pltpu.PrefetchScalarGridSpec(
            num_scalar_prefetch=1, grid=(n // tn, m // tm, steps),
            in_specs=[pl.BlockSpec((tk, tm), lambda j, i, k, c: (k, i)),
                      pl.BlockSpec((tk, tn), lambda j, i, k, c: (k, j))],
            out_specs=[out_spec, out_spec],
            scratch_shapes=[] if steps == 1 else [pltpu.VMEM((tm, tn), F32)]),
        compiler_params=_params(("parallel", "parallel", "arbitrary")),
    )(core, a, b)


BIG = ("w_in", "w_up", "w_down", "w_branch_a", "w_branch_b", "w_out")


def _block_diag(w):
    w4 = w.reshape(N_LRU_GROUPS, HEADS_PER_GROUP, RNN_HEAD_DIM, RNN_HEAD_DIM)
    eye = jnp.eye(HEADS_PER_GROUP, dtype=w.dtype)
    return jnp.einsum("gjio,jk->gjiko", w4, eye).reshape(N_LRU_GROUPS, LRU_GROUP, LRU_GROUP)


def _block_diag_extract(d):
    d5 = d.reshape(N_LRU_GROUPS, HEADS_PER_GROUP, RNN_HEAD_DIM, HEADS_PER_GROUP, RNN_HEAD_DIM)
    blocks = [d5[:, j, :, j, :] for j in range(HEADS_PER_GROUP)]
    return jnp.stack(blocks, axis=1).reshape(RNN_HEADS, RNN_HEAD_DIM, RNN_HEAD_DIM)


def _sgu_mask():
    chunk = jnp.arange(SGU_BLOCK) // CHUNK
    return (chunk[:, None] >= chunk[None, :]).astype(F32)


def _layer_small(sm, l, core):
    row = lambda v: v.reshape(1, -1)
    return dict(
        core=core,
        g1=row(sm["norm_mix_g"][l]), g2=row(sm["norm_ffn_g"][l]),
        wa=_block_diag(sm["lru_w_a"][l]).astype(BF), wx=_block_diag(sm["lru_w_x"][l]).astype(BF),
        ba=row(sm["lru_b_a"][l]), bx=row(sm["lru_b_x"][l]),
        sp=row(jax.nn.softplus(-sm["lru_lambda"][l])),
        cw=sm["conv_w"][l] if "conv_w" in sm else None, cb=row(sm["conv_b"][l]),
        wm=(sm["sgu_w_s"][l] * _sgu_mask()).astype(BF),
        bsb=jnp.broadcast_to(sm["sgu_b_s"][l][:, :, None], (SGU_GROUPS, SGU_BLOCK, SGU_BLOCK)),
        lg=row(sm["sgu_ln_g"][l]), lb=row(sm["sgu_ln_b"][l]),
    )


def _layer_fwd_mix(x, big, p, ts, h=None, before_sgu=None, proj=None):
    if h is None:
        h = _norm_call(x, p["g1"], ts)
    if proj is None:
        proj = _inproj_call(h, big["w_in"], 0, 2 * ts)
    xr, hr, ya_pre = _rnn_fwd_call(proj, p["wa"], p["wx"], p["ba"], p["bx"], p["sp"], p["cw"], p["cb"], ts)
    yb_pre = _sgu_fwd_call(proj, p["wm"], p["bsb"], p["lg"], p["lb"], ts,
                           None if before_sgu is None else before_sgu(ya_pre))
    return dict(p=p, x=x, h=h, proj=proj, xr=xr, hr=hr, ya_pre=ya_pre, yb_pre=yb_pre)


def _layer_fwd_out(sv, big, ts):
    x1, ya, yb, merged, h2 = _merge_call(sv["x"], sv["proj"], sv["ya_pre"], sv["yb_pre"], big["w_branch_a"],
                                         big["w_branch_b"], big["w_out"], sv["p"]["g2"], 0, ts)
    x2, pre = _ffn_call(x1, h2, big["w_up"], big["w_down"], 0, ts)
    sv.update(x1=x1, ya=ya, yb=yb, merged=merged, h2=h2, pre=pre)
    return x2


def _layer_bwd_ffn(dx, sv, big, ts, after=None):
    p = sv["p"]
    dx1, dpre, dg2, dx_bf, sv["dx1_bf"] = _ffn_bwd_call(dx, sv["pre"], sv["x1"], p["g2"], big["w_up"],
                                                       big["w_down"], 0, ts, after)
    tk = dx.shape[0]
    gb = dict(
        w_down=_wgrad_call(sv["pre"], dx_bf, p["core"], Q_FF, D_MODEL, tk, False, "wgrad_down", a_fn=_relu_sq),
        w_up=_wgrad_call(sv["h2"], dpre, p["core"], D_MODEL, Q_FF, tk, True, "wgrad_up"))
    return dx1, gb, dict(norm_ffn_g=dg2[0])


def _layer_bwd_merge(dx1, sv, big, ts, after=None):
    tk = dx1.shape[0]
    core = sv["p"]["core"]
    dya, dyb, dgate, dya_pre, dyb_pre = _merge_bwd_call(
        dx1, sv["proj"], sv["ya"], sv["yb"], big["w_branch_a"], big["w_branch_b"], big["w_out"], 0, ts, after)
    gb = dict(
        w_out=_wgrad_call(sv["merged"], sv["dx1_bf"], core, D_MODEL, D_MODEL, tk, False, "wgrad_out"),
        w_branch_a=_wgrad_call(sv["ya_pre"], dya, core, D_RNN, D_MODEL // 2, tk, False, "wgrad_branch_a"),
        w_branch_b=_wgrad_call(sv["yb_pre"], dyb, core, D_SGU, D_MODEL, tk, False, "wgrad_branch_b"))
    return (dgate, dya_pre, dyb_pre), gb


def _layer_bwd_branches(dx1, merge_out, sv, big, lam, ts, after=None, after_sgu=None):
    p = sv["p"]
    tk = dx1.shape[0]
    dgate, dya_pre, dyb_pre = merge_out
    gb = {}
    duv, dws, dbs, dlg, dlb = _sgu_bwd_call(dyb_pre, sv["proj"], p["wm"], p["bsb"], _sgu_mask(), p["lg"], p["lb"],
                                            ts, after)
    dxg, dwa, dwx, vec = _rnn_bwd_call(dya_pre, sv["proj"], sv["xr"], sv["hr"], p["wa"], p["wx"], p["ba"], p["bx"],
                                       p["sp"], p["cw"], ts, None if after_sgu is None else after_sgu(duv))
    dx, dproj, dg1 = _inproj_bwd_call(dxg, duv, dgate, dx1, sv["x"], p["g1"], big["w_in"], 0, ts)
    gb["w_in"] = _wgrad_call(sv["h"], dproj, p["core"], D_MODEL // 2, Q_IN, tk, True, "wgrad_in")
    gs = dict(
        norm_mix_g=dg1[0], conv_w=vec[_ROW_DCW:_ROW_DCW + CONV_WIDTH], conv_b=vec[_ROW_DCB],
        lru_w_a=_block_diag_extract(dwa), lru_w_x=_block_diag_extract(dwx),
        lru_b_a=vec[_ROW_DBA].reshape(RNN_HEADS, RNN_HEAD_DIM), lru_b_x=vec[_ROW_DBX].reshape(RNN_HEADS, RNN_HEAD_DIM),
        lru_lambda=-vec[_ROW_DSP] * jax.nn.sigmoid(-lam),
        sgu_ln_g=dlg[0], sgu_ln_b=dlb[0], sgu_w_s=dws, sgu_b_s=dbs.T)
    return dx, gb, gs


def _local_step(x, target, big, sm, ts):
    saved = []
    core = jnp.zeros((1,), jnp.int32)
    for l in range(DEPTH):
        sv = _layer_fwd_mix(x, big[l], _layer_small(sm, l, core), ts)
        x = _layer_fwd_out(sv, big[l], ts)
        saved.append(sv)
    dx, loss, dgf = _loss_call(x, target, sm["final_norm_g"].reshape(1, -1), ts)
    gb, gs = [None] * DEPTH, [None] * DEPTH
    for l in reversed(range(DEPTH)):
        dx1, gb_ffn, gs_ffn = _layer_bwd_ffn(dx, saved[l], big[l], ts)
        merge_out, gb_merge = _layer_bwd_merge(dx1, saved[l], big[l], ts)
        dx, gb_mix, gs_mix = _layer_bwd_branches(dx1, merge_out, saved[l], big[l], sm["lru_lambda"][l], ts)
        gb[l] = {**gb_ffn, **gb_merge, **gb_mix}
        gs[l] = {**gs_ffn, **gs_mix}
    gs = {k: jnp.stack([g[k] for g in gs]) for k in gs[0]}
    gs["final_norm_g"] = dgf[0]
    return loss, dx, gb, gs


EW_VMEM_BYTES = 24 * 1024 * 1024


def _row_block(rows, cols, bytes_per_elem):
    for br in range(min(rows, EW_VMEM_BYTES // (2 * bytes_per_elem * cols)), 0, -1):
        if rows % br == 0 and br % 16 == 0:
            return br
    return rows


def _ew_call(fn, name, operands, outputs, slabs=1, sel=None, into=None, after=None):
    if into is not None and not isinstance(into, (list, tuple)):
        into = [into]
    rows, cols = outputs[0][0].shape[2:]
    br = _row_block(rows, cols, sum(jnp.dtype(a.dtype).itemsize for a, _ in operands + outputs))
    n_in = len(operands)

    def pick(tok, g, s):
        if callable(tok):
            return tok(g, s)
        if tok == "g":
            return g
        if isinstance(tok, tuple):
            return s[tok[1]]
        return tok

    def spec(idx):
        return pl.BlockSpec((None, None, br, cols),
                            lambda g, i, s, idx=idx: (pick(idx[0], g, s), pick(idx[1], g, s), i, 0))

    if sel is None:
        sel = jnp.zeros((1,), jnp.int32)
    in_specs = [spec(idx) for _, idx in operands]
    arrays = [a for a, _ in operands]
    aliases = {}
    for j, buf in enumerate(into or ()):
        in_specs.append(pl.BlockSpec(memory_space=pl.ANY))
        arrays.append(buf)
        aliases[1 + n_in + j] = j
    if after is not None:
        in_specs.append(pl.BlockSpec(memory_space=pl.ANY))
        arrays.append(after)

    def body(sel_ref, *refs):
        outs = fn(*[r[...] for r in refs[:n_in]])
        for o_ref, o in zip(refs[len(arrays):], outs):
            o_ref[...] = o.astype(o_ref.dtype)

    return pl.pallas_call(
        body, name=name, out_shape=[s for s, _ in outputs],
        grid_spec=pltpu.PrefetchScalarGridSpec(
            num_scalar_prefetch=1, grid=(slabs, rows // br),
            in_specs=in_specs,
            out_specs=[spec(idx) for _, idx in outputs]),
        input_output_aliases=aliases,
        compiler_params=_params(("parallel", "parallel")),
    )(sel, *arrays)


def _as4(a):
    return a.reshape((1,) * (4 - a.ndim) + a.shape)


def _adamw(w, g, m, v):
    m = ADAM_B1 * m + (1.0 - ADAM_B1) * g
    v = ADAM_B2 * v + (1.0 - ADAM_B2) * jnp.square(g)
    m_hat = m / (1.0 - ADAM_B1 ** ADAM_STEP)
    v_hat = v / (1.0 - ADAM_B2 ** ADAM_STEP)
    delta = -ADAM_LR * (m_hat / (jnp.sqrt(v_hat) + ADAM_EPS) + ADAM_WD * w)
    return delta, m, v


def _small_adamw_call(ws, gs, ms, vs):
    n = len(ws)

    def body(*refs):
        for k in range(n):
            w, g, m, v = (refs[j * n + k][...] for j in range(4))
            outs = _adamw(w, g, m, v)
            for j in range(3):
                refs[(4 + j) * n + k][...] = outs[j]

    shapes = [jax.ShapeDtypeStruct(w.shape, F32) for w in ws]
    outs = pl.pallas_call(
        body, name="adamw_small", out_shape=shapes * 3,
        in_specs=[pl.BlockSpec(memory_space=pltpu.VMEM)] * (4 * n),
        out_specs=[pl.BlockSpec(memory_space=pltpu.VMEM)] * (3 * n),
        compiler_params=_params(),
    )(*ws, *gs, *ms, *vs)
    return outs[:n], outs[n:2 * n], outs[2 * n:]


ANY = pl.BlockSpec(memory_space=pl.ANY)


def _place():
    x, y, c = lax.axis_index("x"), lax.axis_index("y"), lax.axis_index("c")
    chips = [(1 - x, y), (x, 1 - y), (1 - x, 1 - y)]
    return x, y, c, chips


def _remote(src, dst, send_sem, recv_sem, to):
    return pltpu.make_async_remote_copy(src_ref=src, dst_ref=dst, send_sem=send_sem, recv_sem=recv_sem,
                                        device_id=to, device_id_type=MESH)


def _sibling_send_call(items):
    n = len(items)

    def body(*refs):
        src, out = refs[:n], refs[n:2 * n]
        send_sems, recv_sems = refs[2 * n:]
        x, y, c, _ = _place()
        copies = [_remote(src[w], out[w], send_sems.at[w], recv_sems.at[w], (x, y, 1 - c)) for w in range(n)]
        for cp in copies:
            cp.start()
        for cp in copies:
            cp.wait()

    return pl.pallas_call(
        body, name="grads_to_sibling",
        out_shape=[jax.ShapeDtypeStruct(a.shape, a.dtype) for a in items],
        in_specs=[ANY] * n, out_specs=[ANY] * n,
        scratch_shapes=[pltpu.SemaphoreType.DMA((n,)), pltpu.SemaphoreType.DMA((n,))],
        compiler_params=_params(vmem=False, has_side_effects=True),
    )(*items)


def _sibling_inplace_call(name, bufs, slabs, n_pairs):
    n = len(bufs)

    def body(*refs):
        out = refs[n:2 * n]
        send_sems, recv_sems = refs[2 * n:]
        x, y, c, _ = _place()
        sibling = (x, y, 1 - c)
        pairs = [pair for w, ref in enumerate(out) for pair in slabs(ref, c, w)]
        sends = [_remote(s, s, send_sems.at[k], recv_sems.at[k], sibling) for k, (s, _) in enumerate(pairs)]
        for cp in sends:
            cp.start()
        for k, (_, r) in enumerate(pairs):
            _remote(r, r, send_sems.at[k], recv_sems.at[k], sibling).wait_recv()
        for cp in sends:
            cp.wait_send()

    return pl.pallas_call(
        body, name=name,
        out_shape=[jax.ShapeDtypeStruct(a.shape, a.dtype) for a in bufs],
        in_specs=[ANY] * n, out_specs=[ANY] * n,
        input_output_aliases={w: w for w in range(n)},
        scratch_shapes=[pltpu.SemaphoreType.DMA((n_pairs,)), pltpu.SemaphoreType.DMA((n_pairs,))],
        compiler_params=_params(vmem=False, has_side_effects=True),
    )(*bufs)


HBM_SPEC = pl.BlockSpec(memory_space=pltpu.HBM)
SEM_SPEC = pl.BlockSpec(memory_space=pltpu.SEMAPHORE)
DATAFLOW_EFFECT = pltpu.SideEffectType.DATAFLOW_SIDE_EFFECTING


def _exchange_start(name, bufs, copies, n_copies, after):
    return _exchange_start_many(name, [(bufs, copies, n_copies)], after)[0]


def _exchange_start_many(name, groups, after):
    sizes = [len(bufs) for bufs, _, _ in groups]
    starts = [sum(sizes[:g]) for g in range(len(groups))]
    n, n_sems = sum(sizes), 2 * len(groups)

    def body(*refs):
        ins, sems, token = refs[:n], refs[n + 1:n + 1 + n_sems], refs[-1]
        for g, (_, copies, _) in enumerate(groups):
            send_sems, recv_sems = sems[2 * g], sems[2 * g + 1]
            for k, (src, dst, to) in enumerate(copies(ins[starts[g]:starts[g] + sizes[g]])):
                _remote(src, dst, send_sems.at[k], recv_sems.at[k], to).start()
        token[...] = jnp.zeros_like(token)

    every = [b for bufs, _, _ in groups for b in bufs]
    outs = pl.pallas_call(
        body, name=name,
        out_shape=(*[pltpu.SemaphoreType.DMA((c,)) for _, _, c in groups for _ in range(2)],
                   *[pltpu.HBM(b.shape, b.dtype) for b in every], jax.ShapeDtypeStruct((SUBLANES, 128), F32)),
        in_specs=[HBM_SPEC] * n + [ANY],
        out_specs=(*[SEM_SPEC] * n_sems, *[HBM_SPEC] * n, pl.BlockSpec(memory_space=pltpu.VMEM)),
        input_output_aliases={w: w + n_sems for w in range(n)},
        compiler_params=pltpu.CompilerParams(has_side_effects=DATAFLOW_EFFECT),
    )(*[pltpu.with_memory_space_constraint(b, pltpu.HBM) for b in every], after)
    thru = outs[n_sems:n_sems + n]
    return [(outs[2 * g], outs[2 * g + 1], list(thru[starts[g]:starts[g] + sizes[g]]), outs[-1])
            for g in range(len(groups))]


def _exchange_wait(name, send_sems, recv_sems, bufs, copies, after):
    n = len(bufs)

    def body(*refs):
        ins, send_sems, recv_sems = refs[:n], refs[n], refs[n + 1]
        for k, (src, dst, to) in enumerate(copies(ins)):
            cp = _remote(src, dst, send_sems.at[k], recv_sems.at[k], to)
            cp.wait_send()
            cp.wait_recv()

    return pl.pallas_call(
        body, name=name,
        out_shape=[pltpu.HBM(b.shape, b.dtype) for b in bufs],
        in_specs=[HBM_SPEC] * n + [SEM_SPEC, SEM_SPEC, ANY],
        out_specs=[HBM_SPEC] * n,
        input_output_aliases={w: w for w in range(n)},
        compiler_params=pltpu.CompilerParams(has_side_effects=DATAFLOW_EFFECT),
    )(*bufs, send_sems, recv_sems, after)


def _gather_copies(refs):
    x, y, c, chips = _place()
    mine = 2 * (2 * x + y) + c
    return [(ref.at[mine], ref.at[mine], (qx, qy, c)) for ref in refs for qx, qy in chips]


def _forward_copies(refs):
    x, y, c, chips = _place()
    return [(ref.at[2 * (2 * qx + qy) + c], ref.at[2 * (2 * qx + qy) + c], (x, y, 1 - c))
            for ref in refs for qx, qy in chips]


def _gather_forward_slabs(ref, c, w):
    x, y, _, chips = _place()
    return [(ref.at[2 * (2 * qx + qy) + c], ref.at[2 * (2 * qx + qy) + 1 - c]) for qx, qy in chips]


def _device_peers():
    x, y, c, _ = _place()
    return 4 * x + 2 * y + c, [(k, (x ^ ((k >> 2) & 1), y ^ ((k >> 1) & 1), c ^ (k & 1))) for k in range(1, 8)]


def _small_scatter_copies(refs):
    me, peers = _device_peers()
    return [(refs[0].at[me ^ k], refs[1].at[me], to) for k, to in peers]


def _small_spread_copies(refs):
    me, peers = _device_peers()
    return [(refs[0].at[me], refs[0].at[me], to) for _, to in peers]


def _sibling_copies(refs):
    n = len(refs) // 2
    x, y, c, _ = _place()
    return [(refs[w], refs[n + w], (x, y, 1 - c)) for w in range(n)]


def _owner_copies(refs):
    n = len(refs) // 2
    x, y, c, chips = _place()
    return [(refs[w].at[2 * qx + qy], refs[n + w].at[j], (qx, qy, c))
            for w in range(n) for j, (qx, qy) in enumerate(chips)]


N_DEVICES = 8
SMALL_ROWS = 616


SMALL = ("norm_mix_g", "conv_w", "conv_b", "lru_w_a", "lru_b_a", "lru_w_x", "lru_b_x", "lru_lambda",
         "sgu_ln_g", "sgu_ln_b", "sgu_w_s", "sgu_b_s", "norm_ffn_g", "final_norm_g")
WEIGHTS = ("norm_mix_g", "w_in", "conv_w", "conv_b", "lru_w_a", "lru_b_a", "lru_w_x", "lru_b_x", "lru_lambda",
           "sgu_ln_g", "sgu_ln_b", "sgu_w_s", "sgu_b_s", "w_branch_a", "w_branch_b", "w_out", "norm_ffn_g",
           "w_up", "w_down", "final_norm_g")
PACK_ALIGN = SUBLANES * 128


PACKED = SMALL + ("loss",)


def _pack_small(gs):
    parts = []
    for k in PACKED:
        flat = gs[k].reshape(-1)
        parts.append(jnp.pad(flat, (0, -flat.size % PACK_ALIGN)))
    flat = jnp.concatenate(parts)
    flat = jnp.pad(flat, (0, N_DEVICES * SMALL_ROWS * 128 - flat.size))
    return flat.reshape(N_DEVICES, SMALL_ROWS, 128)


def _unpack_small(buf, like):
    flat = buf.reshape(-1)
    out, off = {}, 0
    for k in PACKED:
        size = like[k].size
        out[k] = flat[off:off + size].reshape(like[k].shape)
        off += size + (-size % PACK_ALIGN)
    return out


def _as_rows(a):
    return a.reshape(-1, a.shape[-1])


def kernel(x, norm_mix_g, w_in, conv_w, conv_b, lru_w_a, lru_b_a, lru_w_x, lru_b_x, lru_lambda, sgu_ln_g, sgu_ln_b, sgu_w_s, sgu_b_s, w_branch_a, w_branch_b, w_out, norm_ffn_g, w_up, w_down, final_norm_g, loss_target, m_norm_mix_g, m_w_in, m_conv_w, m_conv_b, m_lru_w_a, m_lru_b_a, m_lru_w_x, m_lru_b_x, m_lru_lambda, m_sgu_ln_g, m_sgu_ln_b, m_sgu_w_s, m_sgu_b_s, m_w_branch_a, m_w_branch_b, m_w_out, m_norm_ffn_g, m_w_up, m_w_down, m_final_norm_g, v_norm_mix_g, v_w_in, v_conv_w, v_conv_b, v_lru_w_a, v_lru_b_a, v_lru_w_x, v_lru_b_x, v_lru_lambda, v_sgu_ln_g, v_sgu_ln_b, v_sgu_w_s, v_sgu_b_s, v_w_branch_a, v_w_branch_b, v_w_out, v_norm_ffn_g, v_w_up, v_w_down, v_final_norm_g):
    w = dict(norm_mix_g=norm_mix_g, w_in=w_in, conv_w=conv_w, conv_b=conv_b, lru_w_a=lru_w_a, lru_b_a=lru_b_a,
             lru_w_x=lru_w_x, lru_b_x=lru_b_x, lru_lambda=lru_lambda, sgu_ln_g=sgu_ln_g, sgu_ln_b=sgu_ln_b,
             sgu_w_s=sgu_w_s, sgu_b_s=sgu_b_s, w_branch_a=w_branch_a, w_branch_b=w_branch_b, w_out=w_out,
             norm_ffn_g=norm_ffn_g, w_up=w_up, w_down=w_down, final_norm_g=final_norm_g)
    m = dict(norm_mix_g=m_norm_mix_g, w_in=m_w_in, conv_w=m_conv_w, conv_b=m_conv_b, lru_w_a=m_lru_w_a,
             lru_b_a=m_lru_b_a, lru_w_x=m_lru_w_x, lru_b_x=m_lru_b_x, lru_lambda=m_lru_lambda,
             sgu_ln_g=m_sgu_ln_g, sgu_ln_b=m_sgu_ln_b, sgu_w_s=m_sgu_w_s, sgu_b_s=m_sgu_b_s,
             w_branch_a=m_w_branch_a, w_branch_b=m_w_branch_b, w_out=m_w_out, norm_ffn_g=m_norm_ffn_g,
             w_up=m_w_up, w_down=m_w_down, final_norm_g=m_final_norm_g)
    v = dict(norm_mix_g=v_norm_mix_g, w_in=v_w_in, conv_w=v_conv_w, conv_b=v_conv_b, lru_w_a=v_lru_w_a,
             lru_b_a=v_lru_b_a, lru_w_x=v_lru_w_x, lru_b_x=v_lru_b_x, lru_lambda=v_lru_lambda,
             sgu_ln_g=v_sgu_ln_g, sgu_ln_b=v_sgu_ln_b, sgu_w_s=v_sgu_w_s, sgu_b_s=v_sgu_b_s,
             w_branch_a=v_w_branch_a, w_branch_b=v_w_branch_b, w_out=v_w_out, norm_ffn_g=v_norm_ffn_g,
             w_up=v_w_up, w_down=v_w_down, final_norm_g=v_final_norm_g)
    core = lax.axis_index("c")
    chip = 2 * lax.axis_index("x") + lax.axis_index("y")
    sel = jnp.stack([core, 1 - core, chip, 2 * chip + core]).astype(jnp.int32)
    this_core, this_chip = ("sel", 0), ("sel", 2)
    sds = jax.ShapeDtypeStruct

    ts = TOKEN_TILE

    def after_all(arrays):
        return jnp.stack([a[(0,) * a.ndim].astype(F32) for a in arrays])

    halves = {k: (w[k].shape[1] // 2, w[k].shape[2]) for k in BIG}

    def same_shape(keys):
        groups = {}
        for k in keys:
            groups.setdefault(halves[k], []).append(k)
        return list(groups.values())

    def half_view(k, a):
        return a.reshape((2 * N_QUARTERS,) + halves[k])

    def full_view(k, a):
        if k == "conv_w":
            return a.reshape(N_QUARTERS, DEPTH, CONV_WIDTH, -1).transpose(1, 2, 0, 3).reshape(DEPTH, CONV_WIDTH, D_RNN)
        r2, cols = halves[k]
        if k in ("w_in", "w_up"):
            return a.reshape(1, N_QUARTERS, 2 * r2, cols)
        return a.reshape(1, 2 * N_QUARTERS * r2, cols)

    layer_bufs = [{}, {}]

    def cast_weights(k, after):
        _, r, cols = w[k].shape
        w4 = w[k].reshape(DEPTH, 1, r, cols)
        outs = _ew_call(lambda a, b: (a, b), "cast_weights", [(w4, (0, 0)), (w4, (1, 0))],
                        [(sds((1, N_QUARTERS, r, cols), BF), (0, this_chip))] * DEPTH, 1, sel, after=after)
        for l in range(DEPTH):
            layer_bufs[l][k] = half_view(k, outs[l])

    conv_buf = lax.dynamic_update_slice_in_dim(
        jnp.zeros((N_QUARTERS, DEPTH) + conv_w.shape[1:], F32), conv_w[None], chip, axis=0)
    layer_bufs[0]["conv_w"] = conv_buf.reshape((2 * N_QUARTERS,) + conv_w.shape[1:])
    sm = {k: w[k] for k in SMALL if k != "conv_w"}

    def gather_start(tag, l, keys, after):
        bufs = [layer_bufs[l][k] for k in keys]
        return _exchange_start(f"gather_start_{tag}", bufs, _gather_copies, 3 * len(keys), after)

    def gather_finish(tag, keys, started, after):
        send_sems, recv_sems, thru, _ = started
        landed = _exchange_wait(f"gather_wait_{tag}", send_sems, recv_sems, thru, _gather_copies, after)
        landed = _sibling_inplace_call("gather_forward", landed, _gather_forward_slabs, 3 * len(keys))
        return {k: full_view(k, a) for k, a in zip(keys, landed)}

    first, rest = ("w_in",), tuple(k for k in BIG if k != "w_in")
    cast_weights("w_in", None)
    started_a = gather_start("0a", 0, first + ("conv_w",), sel)
    for k in rest:
        cast_weights(k, started_a[3])
    started_b, started_c, started_d = _exchange_start_many(
        "gather_start_rest",
        [([layer_bufs[l][k] for k in keys], _gather_copies, 3 * len(keys)) for l, keys in ((0, rest), (1, first), (1, rest))],
        started_a[3])

    def arrives(tag, keys, started):
        state = {}

        def hook(after):
            landed = _exchange_wait(f"gather_wait_{tag}", started[0], started[1], started[2], _gather_copies, after)
            state["forward"] = _exchange_start(f"forward_start_{tag}", landed, _forward_copies, 3 * len(keys), after)
            return state["forward"][3]

        def finish(after):
            send_sems, recv_sems, thru, _ = state["forward"]
            done = _exchange_wait(f"forward_wait_{tag}", send_sems, recv_sems, thru, _forward_copies, after)
            return {k: full_view(k, a) for k, a in zip(keys, done)}

        return hook, finish

    p0, p1 = _layer_small(sm, 0, sel[0:1]), _layer_small(sm, 1, sel[0:1])
    h0 = _norm_call(x[0], p0["g1"], 2 * ts)
    proj_own = _inproj_part_call(h0, full_view("w_in", started_a[2][0]), 2 * ts, sel[2:3], 0, 1)
    ready = after_all([started_d[3], proj_own] + [p[k] for p in (p0, p1) for k in ("wa", "wx", "wm")])
    big0 = gather_finish("0a", first + ("conv_w",), started_a, ready)
    for l, p in enumerate((p0, p1)):
        p["cw"] = big0["conv_w"][l]
    proj0 = _inproj_part_call(h0, big0["w_in"], 2 * ts, sel[2:3], 1, N_QUARTERS - 1, proj_own)
    hook, finish = arrives("0b", rest, started_b)
    sv0 = _layer_fwd_mix(x[0], big0, p0, ts, h0, hook, proj0)
    big0.update(finish(sv0["yb_pre"]))
    x_mid = _layer_fwd_out(sv0, big0, ts)
    hook, finish = arrives("1a", first, started_c)
    h1 = _norm_call(x_mid, p1["g1"], 2 * ts, hook(x_mid))
    big1 = finish(h1)
    hook, finish = arrives("1b", rest, started_d)
    sv1 = _layer_fwd_mix(x_mid, big1, p1, ts, h1, hook)
    big1.update(finish(sv1["yb_pre"]))
    x_out = _layer_fwd_out(sv1, big1, ts)
    dx, loss, dgf = _loss_call(x_out, loss_target[0], final_norm_g.reshape(1, -1), 2 * ts)

    def pair_start(tag, gb, after):
        sends = [gb[k][1] for k in gb]
        zones = [lax.empty(a.shape, BF) for a in sends]
        return _exchange_start(f"pair_start_{tag}", sends + zones, _sibling_copies, len(sends), after)

    def reduce_start(tag, gb, after, pair=None):
        keys = tuple(gb)
        if pair is None:
            from_sibling = _sibling_send_call([gb[k][1] for k in keys])
        else:
            done = _exchange_wait(f"pair_wait_{tag}", pair[0], pair[1], pair[2], _sibling_copies, after)
            from_sibling = done[len(keys):]
        received = dict(zip(keys, from_sibling))
        sums = {}
        for ks in same_shape(keys):
            outs = _ew_call(
                lambda *t: tuple(a.astype(F32) + b.astype(F32) for a, b in zip(t[0::2], t[1::2])), "pair_sum",
                [(a[None], (0, "g")) for k in ks for a in (gb[k][0], received[k])],
                [(sds((1,) + received[k].shape, BF), (0, "g")) for k in ks], N_QUARTERS)
            sums.update({k: o[0] for k, o in zip(ks, outs)})
        sums = [sums[k] for k in keys]
        zones = [lax.empty((3,) + a.shape[1:], BF) for a in sums]
        started = _exchange_start(f"reduce_start_{tag}", sums + zones, _owner_copies, 3 * len(keys), after)
        return keys, started

    def reduce_finish(tag, l, keys_started, after, reduced):
        keys, (send_sems, recv_sems, thru, _) = keys_started
        done = _exchange_wait(f"reduce_wait_{tag}", send_sems, recv_sems, thru, _owner_copies, after)
        sums, zones = done[:len(keys)], done[len(keys):]
        sums, zones = dict(zip(keys, sums)), dict(zip(keys, zones))
        for ks in same_shape(keys):
            outs = _ew_call(
                lambda *t: tuple(((a.astype(F32) + b.astype(F32)) + c.astype(F32)) + d.astype(F32)
                                 for a, b, c, d in zip(t[0::4], t[1::4], t[2::4], t[3::4])),
                "quarter_sum",
                [op for k in ks for op in [(sums[k][None], (0, this_chip))] + [(zones[k][None], (0, j)) for j in range(3)]],
                [(sds((DEPTH, 2) + halves[k], F32), (l, this_core)) for k in ks], 1, sel,
                into=[reduced[k] for k in ks] if ks[0] in reduced else None)
            reduced.update(zip(ks, outs))

    dx1, gb_ffn, gs1 = _layer_bwd_ffn(dx, sv1, big1, ts)
    merge_out, gb_merge = _layer_bwd_merge(dx1, sv1, big1, ts)
    dx_mid, gb_in, gs1_mix = _layer_bwd_branches(dx1, merge_out, sv1, big1, lru_lambda[1], ts)
    gb_1 = {**gb_ffn, **gb_merge, **gb_in}
    pair_1 = pair_start("1", gb_1, dx_mid)
    dx1, gb_ffn, gs0 = _layer_bwd_ffn(dx_mid, sv0, big0, ts, pair_1[3])
    exchange_1 = reduce_start("1", gb_1, dx1, pair_1)
    pair_0a = pair_start("0a", gb_ffn, exchange_1[1][3])
    merge_out, gb_merge = _layer_bwd_merge(dx1, sv0, big0, ts, pair_0a[3])
    exchange_0a = reduce_start("0a", gb_ffn, merge_out[0], pair_0a)
    pair_0b = pair_start("0b", gb_merge, exchange_0a[1][3])
    started_0b = {}

    def after_sgu(duv):
        started_0b["exchange"] = reduce_start("0b", gb_merge, duv, pair_0b)
        return started_0b["exchange"][1][3]

    grad_x, gb_in, gs0_mix = _layer_bwd_branches(dx1, merge_out, sv0, big0, lru_lambda[0], ts, pair_0b[3],
                                                 after_sgu)
    exchange_0b = started_0b["exchange"]
    exchange_0c = reduce_start("0c", gb_in, exchange_0b[1][3])
    layer_gs = [{**gs0, **gs0_mix}, {**gs1, **gs1_mix}]
    gs = {k: jnp.stack([g[k] for g in layer_gs]) for k in layer_gs[0]}
    gs["final_norm_g"] = dgf[0]
    gs["loss"] = loss[0, 0:1]

    me = ("sel", 3)
    piece = (1, N_DEVICES, SMALL_ROWS, 128)
    packed = _pack_small(gs).reshape(piece)
    scatter = _exchange_start("small_scatter_start", [packed[0], lax.empty(piece[1:], F32)], _small_scatter_copies,
                              N_DEVICES - 1, exchange_0c[1][3])
    reduced = {}
    reduce_finish("1", 1, exchange_1, scatter[3], reduced)
    reduce_finish("0a", 0, exchange_0a, reduced["w_in"], reduced)
    reduce_finish("0b", 0, exchange_0b, reduced["w_down"], reduced)

    def swap_slabs(ref, c, i):
        layers = (1,) if BIG[i] == "w_in" else range(DEPTH)
        return [(ref.at[l, c], ref.at[l, 1 - c]) for l in layers]

    swapped = dict(zip(BIG, _sibling_inplace_call("grads_swap_halves", [reduced[k] for k in BIG], swap_slabs,
                                                  DEPTH * len(BIG) - 1)))

    def adamw_layers(k, grad, layer, into, after=None):
        if layer is None:
            views = [_as4(_as_rows(a)) for a in (w[k], grad, m[k], v[k])]
            idx = (0, 0)
        else:
            views = [a.reshape((1,) + w[k].shape) for a in (w[k], grad, m[k], v[k])]
            idx = (0, layer)
        return _ew_call(_adamw, "adamw_big", [(a, idx) for a in views], [(sds(views[0].shape, F32), idx)] * 3,
                        into=into, after=after)

    updated, last_update = {}, None
    for k in BIG:
        updated[k] = adamw_layers(k, swapped[k], 1 if k == "w_in" else None, None, last_update)
        last_update = updated[k][0]
    scattered = _exchange_wait("small_scatter_wait", scatter[0], scatter[1], scatter[2], _small_scatter_copies,
                               last_update)
    summed = _ew_call(
        lambda *parts: (functools.reduce(lambda a, b: a + b, parts),), "small_sum",
        [(scattered[0][None], (0, me))]
        + [(scattered[1][None], (0, lambda g, s, k=k: s[3] ^ k)) for k in range(1, N_DEVICES)],
        [(sds(piece, F32), (0, me))], 1, sel)[0]
    spread = _exchange_start("small_spread_start", [summed[0]], _small_spread_copies, N_DEVICES - 1, summed)
    reduced["w_in"] = swapped["w_in"]
    reduce_finish("0c", 0, exchange_0c, spread[3], reduced)
    last = _sibling_inplace_call("grads_swap_last", [reduced["w_in"]],
                                 lambda ref, c, i: [(ref.at[0, c], ref.at[0, 1 - c])], 1)[0]
    swapped["w_in"] = last
    updated["w_in"] = adamw_layers("w_in", last, 0, updated["w_in"])
    grads_big = {k: swapped[k].reshape(w[k].shape) for k in BIG}
    delta, new_m, new_v = ({k: updated[k][j].reshape(w[k].shape) for k in BIG} for j in range(3))
    gathered_small = _exchange_wait("small_spread_wait", spread[0], spread[1], spread[2], _small_spread_copies,
                                    updated["w_in"][0])[0]

    like = {k: jax.ShapeDtypeStruct(gs[k].shape, F32) for k in SMALL}
    like["loss"] = jax.ShapeDtypeStruct((1,), F32)
    grads_small = _unpack_small(gathered_small, like)
    total = grads_small.pop("loss")[0]
    conv_q = grads_small["conv_w"].reshape(DEPTH, CONV_WIDTH, N_QUARTERS, D_RNN // N_QUARTERS)
    grads_small["conv_w"] = lax.dynamic_index_in_dim(conv_q, chip, axis=2, keepdims=False)
    at_least_2d = lambda a: a.reshape(1, -1) if a.ndim == 1 else a
    outs = _small_adamw_call(*[[at_least_2d(d[k]) for k in SMALL] for d in (w, grads_small, m, v)])
    for d, o in zip((delta, new_m, new_v), outs):
        for k, a in zip(SMALL, o):
            d[k] = a.reshape(w[k].shape)

    grads = {**grads_big, **grads_small}
    return (total, grad_x[None], *[grads[k] for k in WEIGHTS], *[delta[k] for k in WEIGHTS],
            *[new_m[k] for k in WEIGHTS], *[new_v[k] for k in WEIGHTS])
```

```python
import functools
import math

import jax
import jax.numpy as jnp
from jax import lax
from jax.experimental import pallas as pl
from jax.experimental.pallas import tpu as pltpu

F32 = jnp.float32
BF = jnp.bfloat16

DEPTH = 2
D_MODEL = 1024
D_RNN = 1280
D_SGU = 1024
D_FF = 4096
D_IN = 2 * D_RNN + 2 * D_SGU + 2 * D_MODEL
N_QUARTERS = 4
Q_IN = D_IN // N_QUARTERS
Q_FF = D_FF // N_QUARTERS
RNN_HEADS = 20
RNN_HEAD_DIM = 64
LRU_GROUP = 256
N_LRU_GROUPS = D_RNN // LRU_GROUP
HEADS_PER_GROUP = LRU_GROUP // RNN_HEAD_DIM
CONV_WIDTH = 4
LRU_C = 8.0
SGU_GROUPS = 8
SGU_BLOCK = 128
CHUNK = 64
EPS = 1e-6

ADAM_LR = 0.001
ADAM_B1 = 0.9
ADAM_B2 = 0.999
ADAM_EPS = 1e-08
ADAM_WD = 0.01
ADAM_STEP = 10

SUBLANES = 8
TOKEN_TILE = 512
VMEM_LIMIT_BYTES = 56 * 1024 * 1024

MESH = pl.DeviceIdType.MESH


def _params(semantics=None, vmem=True, **kw):
    return pltpu.CompilerParams(
        dimension_semantics=semantics,
        vmem_limit_bytes=VMEM_LIMIT_BYTES if vmem else None,
        **kw,
    )


def _dot(a, b):
    return jnp.dot(a, b, preferred_element_type=F32)


def _dot_nt(a, b):
    return lax.dot_general(a, b, (((1,), (1,)), ((), ())), preferred_element_type=F32)


def _dot_tn(a, b):
    return lax.dot_general(a, b, (((0,), (0,)), ((), ())), preferred_element_type=F32)


_GELU_C = math.sqrt(2.0 / math.pi)
_GELU_A = 0.044715


def _gelu(x):
    return 0.5 * x * (1.0 + jnp.tanh(_GELU_C * (x + _GELU_A * x * x * x)))


def _gelu_and_grad(x):
    x2 = x * x
    t = jnp.tanh(_GELU_C * (x + _GELU_A * x2 * x))
    du = _GELU_C * (1.0 + 3.0 * _GELU_A * x2)
    return 0.5 * x * (1.0 + t), 0.5 * (1.0 + t) + 0.5 * x * (1.0 - t * t) * du


def _rms_stats(x):
    return lax.rsqrt(jnp.mean(x * x, axis=-1, keepdims=True) + EPS)


def _rms_bwd(dy, x, g):
    rs = _rms_stats(x)
    n = x * rs
    dn = dy * g
    dx = rs * (dn - n * jnp.mean(dn * n, axis=-1, keepdims=True))
    return dx, dy * n


def _row_sum(x):
    return jnp.sum(x, axis=0, keepdims=True)


def _tile_spec(ts, width, col=0):
    return pl.BlockSpec((ts, width), lambda i, col=col: (i, col))


def _full_spec(shape):
    zeros = (0,) * len(shape)
    return pl.BlockSpec(shape, lambda *_: zeros)


def _layer_spec(w, layer):
    zeros = (0,) * (w.ndim - 1)
    return pl.BlockSpec((None,) + tuple(w.shape[1:]), lambda *_: (layer,) + zeros)


def _with_after(body, n_in, after):
    if after is None:
        return body, [], []

    def wrapped(*refs):
        return body(*refs[:n_in], *refs[n_in + 1:])

    return wrapped, [pl.BlockSpec(memory_space=pl.ANY)], [after]


def _norm_call(x, g, ts, after=None):
    s = x.shape[0]

    def body(x_ref, g_ref, h_ref):
        xv = x_ref[...]
        h_ref[...] = (xv * _rms_stats(xv) * g_ref[...]).astype(BF)

    body, more_specs, more = _with_after(body, 2, after)
    return pl.pallas_call(
        body, name="norm_fwd", grid=(s // ts,),
        in_specs=[_tile_spec(ts, D_MODEL), _full_spec((1, D_MODEL))] + more_specs,
        out_specs=_tile_spec(ts, D_MODEL),
        out_shape=jax.ShapeDtypeStruct((s, D_MODEL), BF),
        compiler_params=_params(("parallel",)),
    )(x, g, *more)


def _inproj_call(h, w_in, layer, ts):
    s = h.shape[0]

    def body(h_ref, w_ref, o_ref):
        o_ref[...] = _dot(h_ref[...], w_ref[...]).astype(BF)

    return pl.pallas_call(
        body, name="inproj_fwd", grid=(N_QUARTERS, s // ts),
        in_specs=[
            pl.BlockSpec((ts, D_MODEL), lambda q, i: (i, 0)),
            pl.BlockSpec((None, None, D_MODEL, Q_IN), lambda q, i: (layer, q, 0, 0)),
        ],
        out_specs=pl.BlockSpec((ts, Q_IN), lambda q, i: (i, q)),
        out_shape=jax.ShapeDtypeStruct((s, D_IN), BF),
        compiler_params=_params(("parallel", "parallel")),
    )(h, w_in)


def _inproj_part_call(h, w_in, ts, own, first, count, into=None):
    s = h.shape[0]

    def quarter(j, sel):
        return (sel[0] + first + j) % N_QUARTERS

    def body(sel_ref, h_ref, w_ref, *rest):
        rest[-1][...] = _dot(h_ref[...], w_ref[...]).astype(BF)

    in_specs = [pl.BlockSpec((ts, D_MODEL), lambda j, i, sel: (i, 0)),
                pl.BlockSpec((None, None, D_MODEL, Q_IN), lambda j, i, sel: (0, quarter(j, sel), 0, 0))]
    operands = [h, w_in]
    aliases = {}
    if into is not None:
        in_specs.append(pl.BlockSpec(memory_space=pl.ANY))
        operands.append(into)
        aliases = {3: 0}
    return pl.pallas_call(
        body, name="inproj_fwd_part", out_shape=jax.ShapeDtypeStruct((s, D_IN), BF),
        grid_spec=pltpu.PrefetchScalarGridSpec(
            num_scalar_prefetch=1, grid=(count, s // ts), in_specs=in_specs,
            out_specs=pl.BlockSpec((ts, Q_IN), lambda j, i, sel: (i, quarter(j, sel)))),
        input_output_aliases=aliases,
        compiler_params=_params(("parallel", "parallel")),
    )(own, *operands)


def _shift_down(x, tail, s):
    xr = pltpu.roll(x, s, 0)
    tr = pltpu.roll(tail, s, 0)
    row = lax.broadcasted_iota(jnp.int32, tail.shape, 0)
    top = jnp.where(row < s, tr, xr[0:SUBLANES])
    return jnp.concatenate([top, xr[SUBLANES:]], axis=0)


def _shift_up(x, head, s):
    t = x.shape[0]
    xr = pltpu.roll(x, t - s, 0)
    hr = pltpu.roll(head, SUBLANES - s, 0)
    row = lax.broadcasted_iota(jnp.int32, head.shape, 0)
    bottom = jnp.where(row >= SUBLANES - s, hr, xr[t - SUBLANES:])
    return jnp.concatenate([xr[: t - SUBLANES], bottom], axis=0)


def _conv_fwd(x, tail, cw_ref, cb_ref):
    out = cb_ref[...] + cw_ref[CONV_WIDTH - 1:CONV_WIDTH, :] * x
    for s in range(1, CONV_WIDTH):
        k = CONV_WIDTH - 1 - s
        out = out + cw_ref[k:k + 1, :] * _shift_down(x, tail, s)
    return out


def _group_dot(x_bf, w_ref, dot):
    cols = [dot(x_bf[:, g * LRU_GROUP:(g + 1) * LRU_GROUP], w_ref[g]) for g in range(N_LRU_GROUPS)]
    return jnp.concatenate(cols, axis=1)


def _lru_gates(xr, wa_ref, wx_ref, ba_ref, bx_ref, sp_ref):
    xb = xr.astype(BF)
    r = jax.nn.sigmoid(_group_dot(xb, wa_ref, _dot) + ba_ref[...])
    i = jax.nn.sigmoid(_group_dot(xb, wx_ref, _dot) + bx_ref[...])
    log_a = (-LRU_C * r) * sp_ref[...]
    a = jnp.exp(log_a)
    nrm2 = -jnp.tanh(log_a) * (a * a + 1.0)
    inv_nrm = lax.rsqrt(jnp.maximum(nrm2, 1e-36))
    return r, i, a, nrm2 * inv_nrm, inv_nrm


def _linear_scan(a, b, carry, al_ref, bl_ref, h_ref, reverse):
    t, c = a.shape
    rowm = lax.broadcasted_iota(jnp.int32, (t, c), 0) & (SUBLANES - 1)
    for d in (1, 2, 4):
        if reverse:
            keep, sh = rowm < SUBLANES - d, t - d
        else:
            keep, sh = rowm >= d, d
        a_sh = jnp.where(keep, pltpu.roll(a, sh, 0), 1.0)
        b_sh = jnp.where(keep, pltpu.roll(b, sh, 0), 0.0)
        b = a * b_sh + b
        a = a * a_sh
    al_ref[...] = a
    bl_ref[...] = b
    groups = t // SUBLANES

    def step(j, state):
        jj = groups - 1 - j if reverse else j
        off = pl.multiple_of(jj * SUBLANES, SUBLANES)
        rows = bl_ref[pl.ds(off, SUBLANES), :] + al_ref[pl.ds(off, SUBLANES), :] * state
        h_ref[pl.ds(off, SUBLANES), :] = rows
        last = rows[0:1, :] if reverse else rows[SUBLANES - 1:SUBLANES, :]
        return jnp.broadcast_to(last, (SUBLANES, c))

    out = lax.fori_loop(0, groups, step, jnp.broadcast_to(carry, (SUBLANES, c)))
    return out[0:1, :]


def _rnn_fwd_call(proj, wa, wx, ba, bx, sp, cw, cb, ts):
    s = proj.shape[0]

    def body(xg_ref, wa_ref, wx_ref, ba_ref, bx_ref, sp_ref, cw_ref, cb_ref, xr_ref, hr_ref, ya_ref,
             tail_sc, carry_sc, al_sc, bl_sc, h_sc):
        @pl.when(pl.program_id(0) == 0)
        def _():
            tail_sc[...] = jnp.zeros_like(tail_sc)
            carry_sc[...] = jnp.zeros_like(carry_sc)

        x = xg_ref[:, :D_RNN].astype(F32)
        g = xg_ref[:, D_RNN:]
        xr = _conv_fwd(x, tail_sc[...], cw_ref, cb_ref)
        tail_sc[...] = x[ts - SUBLANES:, :]
        xr_ref[...] = xr.astype(BF)
        _, i, a, nrm, _ = _lru_gates(xr, wa_ref, wx_ref, ba_ref, bx_ref, sp_ref)
        carry_sc[...] = _linear_scan(a, nrm * (i * xr), carry_sc[...], al_sc, bl_sc, h_sc, False)
        h = h_sc[...]
        hr_ref[...] = h.astype(BF)
        ya_ref[...] = (h * _gelu(g)).astype(BF)

    gw = (N_LRU_GROUPS, LRU_GROUP, LRU_GROUP)
    return pl.pallas_call(
        body, name="rnn_fwd", grid=(s // ts,),
        in_specs=[_tile_spec(ts, 2 * D_RNN), _full_spec(gw), _full_spec(gw),
                  _full_spec((1, D_RNN)), _full_spec((1, D_RNN)), _full_spec((1, D_RNN)),
                  _full_spec((CONV_WIDTH, D_RNN)), _full_spec((1, D_RNN))],
        out_specs=[_tile_spec(ts, D_RNN)] * 3,
        out_shape=[jax.ShapeDtypeStruct((s, D_RNN), BF)] * 3,
        scratch_shapes=[pltpu.VMEM((SUBLANES, D_RNN), F32), pltpu.VMEM((1, D_RNN), F32),
                        pltpu.VMEM((ts, D_RNN), F32), pltpu.VMEM((ts, D_RNN), F32),
                        pltpu.VMEM((ts, D_RNN), F32)],
        compiler_params=_params(("arbitrary",)),
    )(proj, wa, wx, ba, bx, sp, cw, cb)


def _layernorm_fwd(x):
    mu = jnp.mean(x, axis=-1, keepdims=True)
    xc = x - mu
    rstd = lax.rsqrt(jnp.mean(xc * xc, axis=-1, keepdims=True) + EPS)
    return xc * rstd, rstd


def _sgu_mix(vn_bf, wm_ref, bsb_ref, ts):
    rows = []
    for blk in range(ts // SGU_BLOCK):
        r0 = blk * SGU_BLOCK
        cols = [
            _dot(wm_ref[g], vn_bf[r0:r0 + SGU_BLOCK, g * SGU_BLOCK:(g + 1) * SGU_BLOCK]) + bsb_ref[g]
            for g in range(SGU_GROUPS)
        ]
        rows.append(jnp.concatenate(cols, axis=1))
    return jnp.concatenate(rows, axis=0)


def _sgu_fwd_call(proj, wm, bsb, lg, lb, ts, after=None):
    s = proj.shape[0]

    def body(uv_ref, wm_ref, bsb_ref, lg_ref, lb_ref, yb_ref):
        gu = _gelu(uv_ref[:, :D_SGU])
        gv = _gelu(uv_ref[:, D_SGU:2 * D_SGU]).astype(F32)
        nh, _ = _layernorm_fwd(gv)
        vn = (nh * lg_ref[...] + lb_ref[...]).astype(BF)
        yb_ref[...] = (gu * _sgu_mix(vn, wm_ref, bsb_ref, ts)).astype(BF)

    sw = (SGU_GROUPS, SGU_BLOCK, SGU_BLOCK)
    body, more_specs, more = _with_after(body, 5, after)
    return pl.pallas_call(
        body, name="sgu_fwd", grid=(s // ts,),
        in_specs=[_tile_spec(ts, 2 * D_RNN, 1), _full_spec(sw), _full_spec(sw),
                  _full_spec((1, D_SGU)), _full_spec((1, D_SGU))] + more_specs,
        out_specs=_tile_spec(ts, D_SGU),
        out_shape=jax.ShapeDtypeStruct((s, D_SGU), BF),
        compiler_params=_params(("parallel",)),
    )(proj, wm, bsb, lg, lb, *more)


_GATE_COL0 = (2 * D_RNN + 2 * D_SGU) // 512


def _gate_specs(ts):
    return [_tile_spec(ts, 512, _GATE_COL0 + j) for j in range(4)]


def _merge_call(x, proj, ya_pre, yb_pre, w_ba, w_bb, w_out, g2, layer, ts):
    s = x.shape[0]

    def body(x_ref, ga0, ga1, gb0, gb1, ya_ref, yb_ref, wa_ref, wb_ref, wo_ref, g2_ref,
             x1_ref, yao_ref, ybo_ref, mg_ref, h2_ref):
        ya = _dot(ya_ref[...], wa_ref[...])
        yb = _dot(yb_ref[...], wb_ref[...])
        sa = jax.nn.sigmoid(jnp.concatenate([ga0[...], ga1[...]], axis=1).astype(F32))
        sb = jax.nn.sigmoid(jnp.concatenate([gb0[...], gb1[...]], axis=1).astype(F32))
        merged = (sa * ya + sb * yb).astype(BF)
        x1 = x_ref[...] + _dot(merged, wo_ref[...])
        x1_ref[...] = x1
        yao_ref[...] = ya.astype(BF)
        ybo_ref[...] = yb.astype(BF)
        mg_ref[...] = merged
        h2_ref[...] = (x1 * _rms_stats(x1) * g2_ref[...]).astype(BF)

    act = jax.ShapeDtypeStruct((s, D_MODEL), BF)
    return pl.pallas_call(
        body, name="merge_fwd", grid=(s // ts,),
        in_specs=[_tile_spec(ts, D_MODEL)] + _gate_specs(ts) + [
            _tile_spec(ts, D_RNN), _tile_spec(ts, D_SGU),
            _layer_spec(w_ba, layer), _layer_spec(w_bb, layer), _layer_spec(w_out, layer),
            _full_spec((1, D_MODEL))],
        out_specs=[_tile_spec(ts, D_MODEL)] * 5,
        out_shape=[jax.ShapeDtypeStruct((s, D_MODEL), F32), act, act, act, act],
        compiler_params=_params(("parallel",)),
    )(x, proj, proj, proj, proj, ya_pre, yb_pre, w_ba, w_bb, w_out, g2)


def _ffn_call(x1, h2, w_up, w_down, layer, ts):
    s = x1.shape[0]

    def body(x1_ref, h2_ref, wu_ref, wd_ref, x2_ref, p_ref):
        h2v = h2_ref[...]
        acc = x1_ref[...]
        for q in range(N_QUARTERS):
            p = _dot(h2v, wu_ref[q])
            p_ref[:, q * Q_FF:(q + 1) * Q_FF] = p.astype(BF)
            f = jnp.square(jnp.maximum(p, 0.0)).astype(BF)
            acc = acc + _dot(f, wd_ref[q * Q_FF:(q + 1) * Q_FF, :])
        x2_ref[...] = acc

    return pl.pallas_call(
        body, name="ffn_fwd", grid=(s // ts,),
        in_specs=[_tile_spec(ts, D_MODEL), _tile_spec(ts, D_MODEL),
                  pl.BlockSpec((None, N_QUARTERS, D_MODEL, Q_FF), lambda i: (layer, 0, 0, 0)),
                  pl.BlockSpec((None, D_FF, D_MODEL), lambda i: (layer, 0, 0))],
        out_specs=[_tile_spec(ts, D_MODEL), _tile_spec(ts, D_FF)],
        out_shape=[jax.ShapeDtypeStruct((s, D_MODEL), F32), jax.ShapeDtypeStruct((s, D_FF), BF)],
        compiler_params=_params(("parallel",)),
    )(x1, h2, w_up, w_down)


def _loss_call(x, target, gf, ts):
    s = x.shape[0]

    def body(x_ref, t_ref, g_ref, dx_ref, loss_ref, dg_ref):
        @pl.when(pl.program_id(0) == 0)
        def _():
            loss_ref[...] = jnp.zeros_like(loss_ref)
            dg_ref[...] = jnp.zeros_like(dg_ref)

        xv = x_ref[...]
        gv = g_ref[...]
        err = xv * _rms_stats(xv) * gv - t_ref[...]
        part = 0.5 * jnp.sum(jnp.mean(err * err, axis=-1, keepdims=True), axis=0, keepdims=True)
        loss_ref[...] += jnp.broadcast_to(part, loss_ref.shape)
        dx, dg = _rms_bwd(err * (1.0 / D_MODEL), xv, gv)
        dx_ref[...] = dx
        dg_ref[...] += _row_sum(dg)

    return pl.pallas_call(
        body, name="loss_head", grid=(s // ts,),
        in_specs=[_tile_spec(ts, D_MODEL), _tile_spec(ts, D_MODEL), _full_spec((1, D_MODEL))],
        out_specs=[_tile_spec(ts, D_MODEL), _full_spec((1, 128)), _full_spec((1, D_MODEL))],
        out_shape=[jax.ShapeDtypeStruct((s, D_MODEL), F32), jax.ShapeDtypeStruct((1, 128), F32),
                   jax.ShapeDtypeStruct((1, D_MODEL), F32)],
        compiler_params=_params(("arbitrary",)),
    )(x, target, gf)


def _ffn_bwd_call(dx2, p, x1, g2, w_up, w_down, layer, ts, after=None):
    s = dx2.shape[0]

    def body(dx2_ref, p_ref, x1_ref, g2_ref, wu_ref, wd_ref, dx1_ref, dp_ref, dg_ref, dx2b_ref, dx1b_ref):
        @pl.when(pl.program_id(0) == 0)
        def _():
            dg_ref[...] = jnp.zeros_like(dg_ref)

        dx2v = dx2_ref[...]
        dyb = dx2v.astype(BF)
        dx2b_ref[...] = dyb
        dh2 = jnp.zeros((ts, D_MODEL), F32)
        for q in range(N_QUARTERS):
            cols = slice(q * Q_FF, (q + 1) * Q_FF)
            df = _dot_nt(dyb, wd_ref[cols, :])
            dp = (df * (2.0 * jnp.maximum(p_ref[:, cols].astype(F32), 0.0))).astype(BF)
            dp_ref[:, cols] = dp
            dh2 = dh2 + _dot_nt(dp, wu_ref[q])
        dx, dg = _rms_bwd(dh2, x1_ref[...], g2_ref[...])
        dx1 = dx2v + dx
        dx1_ref[...] = dx1
        dx1b_ref[...] = dx1.astype(BF)
        dg_ref[...] += _row_sum(dg)

    body, more_specs, more = _with_after(body, 6, after)
    return pl.pallas_call(
        body, name="ffn_bwd", grid=(s // ts,),
        in_specs=[_tile_spec(ts, D_MODEL), _tile_spec(ts, D_FF), _tile_spec(ts, D_MODEL),
                  _full_spec((1, D_MODEL)),
                  pl.BlockSpec((None, N_QUARTERS, D_MODEL, Q_FF), lambda i: (layer, 0, 0, 0)),
                  pl.BlockSpec((None, D_FF, D_MODEL), lambda i: (layer, 0, 0))] + more_specs,
        out_specs=[_tile_spec(ts, D_MODEL), _tile_spec(ts, D_FF), _full_spec((1, D_MODEL)),
                   _tile_spec(ts, D_MODEL), _tile_spec(ts, D_MODEL)],
        out_shape=[jax.ShapeDtypeStruct((s, D_MODEL), F32), jax.ShapeDtypeStruct((s, D_FF), BF),
                   jax.ShapeDtypeStruct((1, D_MODEL), F32),
                   jax.ShapeDtypeStruct((s, D_MODEL), BF), jax.ShapeDtypeStruct((s, D_MODEL), BF)],
        compiler_params=_params(("arbitrary",)),
    )(dx2, p, x1, g2, w_up, w_down, *more)


def _merge_bwd_call(dx1, proj, ya, yb, w_ba, w_bb, w_out, layer, ts, after=None):
    s = dx1.shape[0]

    def body(dx1_ref, ga0, ga1, gb0, gb1, ya_ref, yb_ref, wa_ref, wb_ref, wo_ref, *rest):
        dya_ref, dyb_ref, dgate_ref, dyap_ref, dybp_ref = rest[-5:]
        dm = _dot_nt(dx1_ref[...].astype(BF), wo_ref[...])
        sa = jax.nn.sigmoid(jnp.concatenate([ga0[...], ga1[...]], axis=1).astype(F32))
        sb = jax.nn.sigmoid(jnp.concatenate([gb0[...], gb1[...]], axis=1).astype(F32))
        dya = (dm * sa).astype(BF)
        dyb = (dm * sb).astype(BF)
        dya_ref[...] = dya
        dyb_ref[...] = dyb
        dgate_ref[:, :D_MODEL] = (dm * ya_ref[...].astype(F32) * sa * (1.0 - sa)).astype(BF)
        dgate_ref[:, D_MODEL:] = (dm * yb_ref[...].astype(F32) * sb * (1.0 - sb)).astype(BF)
        dyap_ref[...] = _dot_nt(dya, wa_ref[...]).astype(BF)
        dybp_ref[...] = _dot_nt(dyb, wb_ref[...]).astype(BF)

    act = jax.ShapeDtypeStruct((s, D_MODEL), BF)
    return pl.pallas_call(
        body, name="merge_bwd", grid=(s // ts,),
        in_specs=[_tile_spec(ts, D_MODEL)] + _gate_specs(ts) + [
            _tile_spec(ts, D_MODEL), _tile_spec(ts, D_MODEL),
            _layer_spec(w_ba, layer), _layer_spec(w_bb, layer), _layer_spec(w_out, layer)]
        + ([] if after is None else [pl.BlockSpec(memory_space=pl.ANY)]),
        out_specs=[_tile_spec(ts, D_MODEL), _tile_spec(ts, D_MODEL), _tile_spec(ts, 2 * D_MODEL),
                   _tile_spec(ts, D_RNN), _tile_spec(ts, D_SGU)],
        out_shape=[act, act, jax.ShapeDtypeStruct((s, 2 * D_MODEL), BF),
                   jax.ShapeDtypeStruct((s, D_RNN), BF), jax.ShapeDtypeStruct((s, D_SGU), BF)],
        compiler_params=_params(("parallel",)),
    )(dx1, proj, proj, proj, proj, ya, yb, w_ba, w_bb, w_out, *([] if after is None else [after]))


def _sgu_bwd_call(dyb_pre, proj, wm, bsb, mask, lg, lb, ts, after=None):
    s = proj.shape[0]

    def body(dy_ref, uv_ref, wm_ref, bsb_ref, mask_ref, lg_ref, lb_ref,
             duv_ref, dws_ref, dbs_ref, dlg_ref, dlb_ref, dm_sc):
        step = pl.program_id(0)

        @pl.when(step == 0)
        def _():
            dws_ref[...] = jnp.zeros_like(dws_ref)
            dlg_ref[...] = jnp.zeros_like(dlg_ref)
            dlb_ref[...] = jnp.zeros_like(dlb_ref)
            dm_sc[...] = jnp.zeros_like(dm_sc)

        gu, dgu_du = _gelu_and_grad(uv_ref[:, :D_SGU])
        gv, dgv_dv = _gelu_and_grad(uv_ref[:, D_SGU:2 * D_SGU])
        nh, rstd = _layernorm_fwd(gv.astype(F32))
        lgv = lg_ref[...]
        vn = (nh * lgv + lb_ref[...]).astype(BF)
        dy = dy_ref[...].astype(F32)
        du = dy * _sgu_mix(vn, wm_ref, bsb_ref, ts) * dgu_du
        dmix = dy * gu
        dmix_bf = dmix.astype(BF)
        dm_acc = dm_sc[...]
        rows = []
        for blk in range(ts // SGU_BLOCK):
            r0 = blk * SGU_BLOCK
            dm_acc = dm_acc + dmix[r0:r0 + SGU_BLOCK, :]
            cols = []
            for g in range(SGU_GROUPS):
                c0 = g * SGU_BLOCK
                dmg = dmix_bf[r0:r0 + SGU_BLOCK, c0:c0 + SGU_BLOCK]
                cols.append(_dot_tn(wm_ref[g], dmg))
                dws_ref[g] += _dot_nt(dmg, vn[r0:r0 + SGU_BLOCK, c0:c0 + SGU_BLOCK])
            rows.append(jnp.concatenate(cols, axis=1))
        dm_sc[...] = dm_acc
        dvn = jnp.concatenate(rows, axis=0)
        dlg_ref[...] += _row_sum(dvn * nh)
        dlb_ref[...] += _row_sum(dvn)
        dnh = dvn * lgv
        dgv = rstd * (dnh - jnp.mean(dnh, axis=-1, keepdims=True)
                      - nh * jnp.mean(dnh * nh, axis=-1, keepdims=True))
        duv_ref[:, :D_SGU] = du.astype(BF)
        duv_ref[:, D_SGU:] = (dgv * dgv_dv).astype(BF)

        @pl.when(step == pl.num_programs(0) - 1)
        def _():
            for g in range(SGU_GROUPS):
                dws_ref[g] = dws_ref[g] * mask_ref[...]
                dbs_ref[:, g:g + 1] = jnp.sum(
                    dm_acc[:, g * SGU_BLOCK:(g + 1) * SGU_BLOCK], axis=1, keepdims=True)

    sw = (SGU_GROUPS, SGU_BLOCK, SGU_BLOCK)
    body, more_specs, more = _with_after(body, 7, after)
    return pl.pallas_call(
        body, name="sgu_bwd", grid=(s // ts,),
        in_specs=[_tile_spec(ts, D_SGU), _tile_spec(ts, 2 * D_RNN, 1), _full_spec(sw), _full_spec(sw),
                  _full_spec((SGU_BLOCK, SGU_BLOCK)), _full_spec((1, D_SGU)), _full_spec((1, D_SGU))] + more_specs,
        out_specs=[_tile_spec(ts, 2 * D_SGU), _full_spec(sw), _full_spec((SGU_BLOCK, SGU_GROUPS)),
                   _full_spec((1, D_SGU)), _full_spec((1, D_SGU))],
        out_shape=[jax.ShapeDtypeStruct((s, 2 * D_SGU), BF), jax.ShapeDtypeStruct(sw, F32),
                   jax.ShapeDtypeStruct((SGU_BLOCK, SGU_GROUPS), F32),
                   jax.ShapeDtypeStruct((1, D_SGU), F32), jax.ShapeDtypeStruct((1, D_SGU), F32)],
        scratch_shapes=[pltpu.VMEM((SGU_BLOCK, D_SGU), F32)],
        compiler_params=_params(("arbitrary",)),
    )(dyb_pre, proj, wm, bsb, mask, lg, lb, *more)


_ROW_DBA, _ROW_DBX, _ROW_DSP, _ROW_DCB, _ROW_DCW = 0, 1, 2, 3, 4
_PREV_ROWS = 16


def _rnn_bwd_call(dya_pre, proj, xr_saved, hr, wa, wx, ba, bx, sp, cw, ts, after=None):
    s = proj.shape[0]
    nt = s // ts
    per = ts // _PREV_ROWS

    def tile(i):
        return nt - 1 - i

    def prev(i):
        return jnp.maximum(tile(i) * per - 1, 0)

    def body(dy_ref, xg_ref, xr_ref, hr_ref, hrp_ref, wa_ref, wx_ref, ba_ref, bx_ref, sp_ref,
             cw_ref, dxg_ref, dwa_ref, dwx_ref, vec_ref,
             lam_carry, a_first, dxr_head, al_sc, bl_sc, lam_sc):
        step = pl.program_id(0)

        @pl.when(step == 0)
        def _():
            dwa_ref[...] = jnp.zeros_like(dwa_ref)
            dwx_ref[...] = jnp.zeros_like(dwx_ref)
            vec_ref[...] = jnp.zeros_like(vec_ref)
            lam_carry[...] = jnp.zeros_like(lam_carry)
            a_first[...] = jnp.zeros_like(a_first)
            dxr_head[...] = jnp.zeros_like(dxr_head)

        has_prev = (step < nt - 1).astype(F32)
        x = xg_ref[:, :D_RNN].astype(F32)
        g = xg_ref[:, D_RNN:]
        h_tail =hrp_ref[_PREV_ROWS - SUBLANES:, :].astype(F32) * has_prev
        xr = xr_ref[...].astype(F32)
        r, i, a, nrm, inv_nrm = _lru_gates(xr, wa_ref, wx_ref, ba_ref, bx_ref, sp_ref)
        h = hr_ref[...].astype(F32)
        dy = dy_ref[...].astype(F32)
        gg, dgg = _gelu_and_grad(g)

        coef = _shift_up(a, jnp.broadcast_to(a_first[...], (SUBLANES, D_RNN)), 1)
        lam_carry[...] = _linear_scan(coef, dy * gg, lam_carry[...], al_sc, bl_sc, lam_sc, True)
        a_first[...] = a[0:1, :]
        lam = lam_sc[...]

        da = lam * _shift_down(h, h_tail, 1)
        dnrm = lam * (i * xr)
        di = lam * nrm * xr
        dlog_a = da * a - dnrm * (a * a) * inv_nrm
        spv = sp_ref[...]
        dza = (dlog_a * (-LRU_C * spv)) * (r * (1.0 - r))
        dzx = di * (i * (1.0 - i))
        vec_ref[_ROW_DSP:_ROW_DSP + 1, :] += _row_sum(dlog_a * (-LRU_C * r))
        vec_ref[_ROW_DBA:_ROW_DBA + 1, :] += _row_sum(dza)
        vec_ref[_ROW_DBX:_ROW_DBX + 1, :] += _row_sum(dzx)
        xb = xr.astype(BF)
        dza_bf = dza.astype(BF)
        dzx_bf = dzx.astype(BF)
        for grp in range(N_LRU_GROUPS):
            cols = slice(grp * LRU_GROUP, (grp + 1) * LRU_GROUP)
            dwa_ref[grp] += _dot_tn(xb[:, cols], dza_bf[:, cols])
            dwx_ref[grp] += _dot_tn(xb[:, cols], dzx_bf[:, cols])
        dxr = (lam * nrm * i + _group_dot(dza_bf, wa_ref, _dot_nt) + _group_dot(dzx_bf, wx_ref, _dot_nt))

        vec_ref[_ROW_DCB:_ROW_DCB + 1, :] += _row_sum(dxr)
        head = dxr_head[...]
        dx = cw_ref[CONV_WIDTH - 1:CONV_WIDTH, :] * dxr
        vec_ref[_ROW_DCW + 3:_ROW_DCW + 4, :] += _row_sum(dxr * x)
        for sft in range(1, CONV_WIDTH):
            k = CONV_WIDTH - 1 - sft
            ahead = _shift_up(dxr, head, sft)
            dx = dx + cw_ref[k:k + 1, :] * ahead
            vec_ref[_ROW_DCW + k:_ROW_DCW + k + 1, :] += _row_sum(ahead * x)
        dxr_head[...] = dxr[0:SUBLANES, :]
        dxg_ref[:, :D_RNN] = dx.astype(BF)
        dxg_ref[:, D_RNN:] = (dy * h * dgg).astype(BF)

    gw = (N_LRU_GROUPS, LRU_GROUP, LRU_GROUP)
    rev = lambda width: pl.BlockSpec((ts, width), lambda i: (tile(i), 0))
    body, more_specs, more = _with_after(body, 11, after)
    return pl.pallas_call(
        body, name="rnn_bwd", grid=(nt,),
        in_specs=[rev(D_RNN), rev(2 * D_RNN), rev(D_RNN), rev(D_RNN),
                  pl.BlockSpec((_PREV_ROWS, D_RNN), lambda i: (prev(i), 0)),
                  _full_spec(gw), _full_spec(gw),
                  _full_spec((1, D_RNN)), _full_spec((1, D_RNN)), _full_spec((1, D_RNN)),
                  _full_spec((CONV_WIDTH, D_RNN))] + more_specs,
        out_specs=[rev(2 * D_RNN), _full_spec(gw), _full_spec(gw), _full_spec((SUBLANES, D_RNN))],
        out_shape=[jax.ShapeDtypeStruct((s, 2 * D_RNN), BF), jax.ShapeDtypeStruct(gw, F32),
                   jax.ShapeDtypeStruct(gw, F32), jax.ShapeDtypeStruct((SUBLANES, D_RNN), F32)],
        scratch_shapes=[pltpu.VMEM((1, D_RNN), F32), pltpu.VMEM((1, D_RNN), F32),
                        pltpu.VMEM((SUBLANES, D_RNN), F32),
                        pltpu.VMEM((ts, D_RNN), F32), pltpu.VMEM((ts, D_RNN), F32),
                        pltpu.VMEM((ts, D_RNN), F32)],
        compiler_params=_params(("arbitrary",)),
    )(dya_pre, proj, xr_saved, hr, hr, wa, wx, ba, bx, sp, cw, *more)


def _inproj_bwd_call(dxg, duv, dgate, dx1, x, g1, w_in, layer, ts):
    s = x.shape[0]

    def body(dxg_ref, duv_ref, dgt_ref, dx1_ref, x_ref, g_ref, w_ref, dx_ref, dproj_ref, dg_ref):
        @pl.when(pl.program_id(0) == 0)
        def _():
            dg_ref[...] = jnp.zeros_like(dg_ref)

        dproj = jnp.concatenate([dxg_ref[...], duv_ref[...], dgt_ref[...]], axis=1)
        dproj_ref[...] = dproj
        dh = jnp.zeros((ts, D_MODEL), F32)
        for q in range(N_QUARTERS):
            dh = dh + _dot_nt(dproj[:, q * Q_IN:(q + 1) * Q_IN], w_ref[q])
        dx, dg = _rms_bwd(dh, x_ref[...], g_ref[...])
        dx_ref[...] = dx1_ref[...] + dx
        dg_ref[...] += _row_sum(dg)

    return pl.pallas_call(
        body, name="inproj_bwd", grid=(s // ts,),
        in_specs=[_tile_spec(ts, 2 * D_RNN), _tile_spec(ts, 2 * D_SGU), _tile_spec(ts, 2 * D_MODEL),
                  _tile_spec(ts, D_MODEL), _tile_spec(ts, D_MODEL), _full_spec((1, D_MODEL)),
                  pl.BlockSpec((None, N_QUARTERS, D_MODEL, Q_IN), lambda i: (layer, 0, 0, 0))],
        out_specs=[_tile_spec(ts, D_MODEL), _tile_spec(ts, D_IN), _full_spec((1, D_MODEL))],
        out_shape=[jax.ShapeDtypeStruct((s, D_MODEL), F32), jax.ShapeDtypeStruct((s, D_IN), BF),
                   jax.ShapeDtypeStruct((1, D_MODEL), F32)],
        compiler_params=_params(("arbitrary",)),
    )(dxg, duv, dgate, dx1, x, g1, w_in)


def _relu_sq(p):
    return jnp.square(jnp.maximum(p, 0))


def _wgrad_call(a, b, core, tm, tn, tk, col_blocked, name, a_fn=None):
    s, m = a.shape
    n = b.shape[1]
    r, cols = (m, n // N_QUARTERS) if col_blocked else (m // N_QUARTERS, n)
    r2 = r // 2
    per_tile = tm // r
    steps = s // tk
    assert per_tile > 0 or steps == 1

    def body(core_ref, a_ref, b_ref, keep_ref, send_ref, *acc):
        av = a_ref[...]
        if a_fn is not None:
            av = a_fn(av)
        prod = _dot_tn(av.astype(BF), b_ref[...].astype(BF))

        def emit(total):
            for h in range(2):
                @pl.when(core_ref[0] == h)
                def _():
                    for q in range(per_tile):
                        keep_ref[q] = total[q * r + h * r2:q * r + (h + 1) * r2].astype(BF)
                        send_ref[q] = total[q * r + (1 - h) * r2:q * r + (2 - h) * r2].astype(BF)

        if per_tile == 0:
            mine = pl.program_id(1) == core_ref[0]

            @pl.when(mine)
            def _():
                keep_ref[0] = prod.astype(BF)

            @pl.when(jnp.logical_not(mine))
            def _():
                send_ref[0] = prod.astype(BF)
        elif steps == 1:
            emit(prod)
        else:
            acc_ref, = acc
            step = pl.program_id(2)

            @pl.when(step == 0)
            def _():
                acc_ref[...] = prod

            @pl.when(jnp.logical_and(step > 0, step < steps - 1))
            def _():
                acc_ref[...] += prod

            @pl.when(step == steps - 1)
            def _():
                emit(acc_ref[...] + prod)

    if col_blocked:
        per_q = cols // tn
        out_spec = pl.BlockSpec((1, r2, tn), lambda j, i, k, c: (j // per_q, 0, j % per_q))
    else:
        out_spec = pl.BlockSpec((per_tile, r2, tn), lambda j, i, k, c: (i, 0, j))
    return pl.pallas_call(
        body, name=name,
        out_shape=[jax.ShapeDtypeStruct((N_QUARTERS, r2, cols), BF)] * 2,
        grid_spec=pltpu.PrefetchScalarGridSpec(
            num_scalar_prefetch=1, grid=(n // tn, m // tm, steps),
            in_specs=[pl.BlockSpec((tk, tm), lambda j, i, k, c: (k, i)),
                      pl.BlockSpec((tk, tn), lambda j, i, k, c: (k, j))],
            out_specs=[out_spec, out_spec],
            scratch_shapes=[] if steps == 1 else [pltpu.VMEM((tm, tn), F32)]),
        compiler_params=_params(("parallel", "parallel", "arbitrary")),
    )(core, a, b)


BIG = ("w_in", "w_up", "w_down", "w_branch_a", "w_branch_b", "w_out")


def _block_diag(w):
    w4 = w.reshape(N_LRU_GROUPS, HEADS_PER_GROUP, RNN_HEAD_DIM, RNN_HEAD_DIM)
    eye = jnp.eye(HEADS_PER_GROUP, dtype=w.dtype)
    return jnp.einsum("gjio,jk->gjiko", w4, eye).reshape(N_LRU_GROUPS, LRU_GROUP, LRU_GROUP)


def _block_diag_extract(d):
    d5 = d.reshape(N_LRU_GROUPS, HEADS_PER_GROUP, RNN_HEAD_DIM, HEADS_PER_GROUP, RNN_HEAD_DIM)
    blocks = [d5[:, j, :, j, :] for j in range(HEADS_PER_GROUP)]
    return jnp.stack(blocks, axis=1).reshape(RNN_HEADS, RNN_HEAD_DIM, RNN_HEAD_DIM)


def _sgu_mask():
    chunk = jnp.arange(SGU_BLOCK) // CHUNK
    return (chunk[:, None] >= chunk[None, :]).astype(F32)


def _layer_small(sm, l, core):
    row = lambda v: v.reshape(1, -1)
    return dict(
        core=core,
        g1=row(sm["norm_mix_g"][l]), g2=row(sm["norm_ffn_g"][l]),
        wa=_block_diag(sm["lru_w_a"][l]).astype(BF), wx=_block_diag(sm["lru_w_x"][l]).astype(BF),
        ba=row(sm["lru_b_a"][l]), bx=row(sm["lru_b_x"][l]),
        sp=row(jax.nn.softplus(-sm["lru_lambda"][l])),
        cw=sm["conv_w"][l] if "conv_w" in sm else None, cb=row(sm["conv_b"][l]),
        wm=(sm["sgu_w_s"][l] * _sgu_mask()).astype(BF),
        bsb=jnp.broadcast_to(sm["sgu_b_s"][l][:, :, None], (SGU_GROUPS, SGU_BLOCK, SGU_BLOCK)),
        lg=row(sm["sgu_ln_g"][l]), lb=row(sm["sgu_ln_b"][l]),
    )


def _layer_fwd_mix(x, big, p, ts, h=None, before_sgu=None, proj=None):
    if h is None:
        h = _norm_call(x, p["g1"], ts)
    if proj is None:
        proj = _inproj_call(h, big["w_in"], 0, 2 * ts)
    xr, hr, ya_pre = _rnn_fwd_call(proj, p["wa"], p["wx"], p["ba"], p["bx"], p["sp"], p["cw"], p["cb"], ts)
    yb_pre = _sgu_fwd_call(proj, p["wm"], p["bsb"], p["lg"], p["lb"], ts,
                           None if before_sgu is None else before_sgu(ya_pre))
    return dict(p=p, x=x, h=h, proj=proj, xr=xr, hr=hr, ya_pre=ya_pre, yb_pre=yb_pre)


def _layer_fwd_out(sv, big, ts):
    x1, ya, yb, merged, h2 = _merge_call(sv["x"], sv["proj"], sv["ya_pre"], sv["yb_pre"], big["w_branch_a"],
                                         big["w_branch_b"], big["w_out"], sv["p"]["g2"], 0, ts)
    x2, pre = _ffn_call(x1, h2, big["w_up"], big["w_down"], 0, ts)
    sv.update(x1=x1, ya=ya, yb=yb, merged=merged, h2=h2, pre=pre)
    return x2


def _layer_bwd_ffn(dx, sv, big, ts, after=None):
    p = sv["p"]
    dx1, dpre, dg2, dx_bf, sv["dx1_bf"] = _ffn_bwd_call(dx, sv["pre"], sv["x1"], p["g2"], big["w_up"],
                                                       big["w_down"], 0, ts, after)
    tk = dx.shape[0]
    gb = dict(
        w_down=_wgrad_call(sv["pre"], dx_bf, p["core"], Q_FF, D_MODEL, tk, False, "wgrad_down", a_fn=_relu_sq),
        w_up=_wgrad_call(sv["h2"], dpre, p["core"], D_MODEL, Q_FF, tk, True, "wgrad_up"))
    return dx1, gb, dict(norm_ffn_g=dg2[0])


def _layer_bwd_merge(dx1, sv, big, ts, after=None):
    tk = dx1.shape[0]
    core = sv["p"]["core"]
    dya, dyb, dgate, dya_pre, dyb_pre = _merge_bwd_call(
        dx1, sv["proj"], sv["ya"], sv["yb"], big["w_branch_a"], big["w_branch_b"], big["w_out"], 0, ts, after)
    gb = dict(
        w_out=_wgrad_call(sv["merged"], sv["dx1_bf"], core, D_MODEL, D_MODEL, tk, False, "wgrad_out"),
        w_branch_a=_wgrad_call(sv["ya_pre"], dya, core, D_RNN, D_MODEL // 2, tk, False, "wgrad_branch_a"),
        w_branch_b=_wgrad_call(sv["yb_pre"], dyb, core, D_SGU, D_MODEL, tk, False, "wgrad_branch_b"))
    return (dgate, dya_pre, dyb_pre), gb


def _layer_bwd_branches(dx1, merge_out, sv, big, lam, ts, after=None, after_sgu=None):
    p = sv["p"]
    tk = dx1.shape[0]
    dgate, dya_pre, dyb_pre = merge_out
    gb = {}
    duv, dws, dbs, dlg, dlb = _sgu_bwd_call(dyb_pre, sv["proj"], p["wm"], p["bsb"], _sgu_mask(), p["lg"], p["lb"],
                                            ts, after)
    dxg, dwa, dwx, vec = _rnn_bwd_call(dya_pre, sv["proj"], sv["xr"], sv["hr"], p["wa"], p["wx"], p["ba"], p["bx"],
                                       p["sp"], p["cw"], ts, None if after_sgu is None else after_sgu(duv))
    dx, dproj, dg1 = _inproj_bwd_call(dxg, duv, dgate, dx1, sv["x"], p["g1"], big["w_in"], 0, ts)
    gb["w_in"] = _wgrad_call(sv["h"], dproj, p["core"], D_MODEL // 2, Q_IN, tk, True, "wgrad_in")
    gs = dict(
        norm_mix_g=dg1[0], conv_w=vec[_ROW_DCW:_ROW_DCW + CONV_WIDTH], conv_b=vec[_ROW_DCB],
        lru_w_a=_block_diag_extract(dwa), lru_w_x=_block_diag_extract(dwx),
        lru_b_a=vec[_ROW_DBA].reshape(RNN_HEADS, RNN_HEAD_DIM), lru_b_x=vec[_ROW_DBX].reshape(RNN_HEADS, RNN_HEAD_DIM),
        lru_lambda=-vec[_ROW_DSP] * jax.nn.sigmoid(-lam),
        sgu_ln_g=dlg[0], sgu_ln_b=dlb[0], sgu_w_s=dws, sgu_b_s=dbs.T)
    return dx, gb, gs


def _local_step(x, target, big, sm, ts):
    saved = []
    core = jnp.zeros((1,), jnp.int32)
    for l in range(DEPTH):
        sv = _layer_fwd_mix(x, big[l], _layer_small(sm, l, core), ts)
        x = _layer_fwd_out(sv, big[l], ts)
        saved.append(sv)
    dx, loss, dgf = _loss_call(x, target, sm["final_norm_g"].reshape(1, -1), ts)
    gb, gs = [None] * DEPTH, [None] * DEPTH
    for l in reversed(range(DEPTH)):
        dx1, gb_ffn, gs_ffn = _layer_bwd_ffn(dx, saved[l], big[l], ts)
        merge_out, gb_merge = _layer_bwd_merge(dx1, saved[l], big[l], ts)
        dx, gb_mix, gs_mix = _layer_bwd_branches(dx1, merge_out, saved[l], big[l], sm["lru_lambda"][l], ts)
        gb[l] = {**gb_ffn, **gb_merge, **gb_mix}
        gs[l] = {**gs_ffn, **gs_mix}
    gs = {k: jnp.stack([g[k] for g in gs]) for k in gs[0]}
    gs["final_norm_g"] = dgf[0]
    return loss, dx, gb, gs


EW_VMEM_BYTES = 24 * 1024 * 1024


def _row_block(rows, cols, bytes_per_elem):
    for br in range(min(rows, EW_VMEM_BYTES // (2 * bytes_per_elem * cols)), 0, -1):
        if rows % br == 0 and br % 16 == 0:
            return br
    return rows


def _ew_call(fn, name, operands, outputs, slabs=1, sel=None, into=None, after=None):
    if into is not None and not isinstance(into, (list, tuple)):
        into = [into]
    rows, cols = outputs[0][0].shape[2:]
    br = _row_block(rows, cols, sum(jnp.dtype(a.dtype).itemsize for a, _ in operands + outputs))
    n_in = len(operands)

    def pick(tok, g, s):
        if callable(tok):
            return tok(g, s)
        if tok == "g":
            return g
        if isinstance(tok, tuple):
            return s[tok[1]]
        return tok

    def spec(idx):
        return pl.BlockSpec((None, None, br, cols),
                            lambda g, i, s, idx=idx: (pick(idx[0], g, s), pick(idx[1], g, s), i, 0))

    if sel is None:
        sel = jnp.zeros((1,), jnp.int32)
    in_specs = [spec(idx) for _, idx in operands]
    arrays = [a for a, _ in operands]
    aliases = {}
    for j, buf in enumerate(into or ()):
        in_specs.append(pl.BlockSpec(memory_space=pl.ANY))
        arrays.append(buf)
        aliases[1 + n_in + j] = j
    if after is not None:
        in_specs.append(pl.BlockSpec(memory_space=pl.ANY))
        arrays.append(after)

    def body(sel_ref, *refs):
        outs = fn(*[r[...] for r in refs[:n_in]])
        for o_ref, o in zip(refs[len(arrays):], outs):
            o_ref[...] = o.astype(o_ref.dtype)

    return pl.pallas_call(
        body, name=name, out_shape=[s for s, _ in outputs],
        grid_spec=pltpu.PrefetchScalarGridSpec(
            num_scalar_prefetch=1, grid=(slabs, rows // br),
            in_specs=in_specs,
            out_specs=[spec(idx) for _, idx in outputs]),
        input_output_aliases=aliases,
        compiler_params=_params(("parallel", "parallel")),
    )(sel, *arrays)


def _as4(a):
    return a.reshape((1,) * (4 - a.ndim) + a.shape)


def _adamw(w, g, m, v):
    m = ADAM_B1 * m + (1.0 - ADAM_B1) * g
    v = ADAM_B2 * v + (1.0 - ADAM_B2) * jnp.square(g)
    m_hat = m / (1.0 - ADAM_B1 ** ADAM_STEP)
    v_hat = v / (1.0 - ADAM_B2 ** ADAM_STEP)
    delta = -ADAM_LR * (m_hat / (jnp.sqrt(v_hat) + ADAM_EPS) + ADAM_WD * w)
    return delta, m, v


def _small_adamw_call(ws, gs, ms, vs):
    n = len(ws)

    def body(*refs):
        for k in range(n):
            w, g, m, v = (refs[j * n + k][...] for j in range(4))
            outs = _adamw(w, g, m, v)
            for j in range(3):
                refs[(4 + j) * n + k][...] = outs[j]

    shapes = [jax.ShapeDtypeStruct(w.shape, F32) for w in ws]
    outs = pl.pallas_call(
        body, name="adamw_small", out_shape=shapes * 3,
        in_specs=[pl.BlockSpec(memory_space=pltpu.VMEM)] * (4 * n),
        out_specs=[pl.BlockSpec(memory_space=pltpu.VMEM)] * (3 * n),
        compiler_params=_params(),
    )(*ws, *gs, *ms, *vs)
    return outs[:n], outs[n:2 * n], outs[2 * n:]


ANY = pl.BlockSpec(memory_space=pl.ANY)


def _place():
    x, y, c = lax.axis_index("x"), lax.axis_index("y"), lax.axis_index("c")
    chips = [(1 - x, y), (x, 1 - y), (1 - x, 1 - y)]
    return x, y, c, chips


def _remote(src, dst, send_sem, recv_sem, to):
    return pltpu.make_async_remote_copy(src_ref=src, dst_ref=dst, send_sem=send_sem, recv_sem=recv_sem,
                                        device_id=to, device_id_type=MESH)


def _sibling_send_call(items):
    n = len(items)

    def body(*refs):
        src, out = refs[:n], refs[n:2 * n]
        send_sems, recv_sems = refs[2 * n:]
        x, y, c, _ = _place()
        copies = [_remote(src[w], out[w], send_sems.at[w], recv_sems.at[w], (x, y, 1 - c)) for w in range(n)]
        for cp in copies:
            cp.start()
        for cp in copies:
            cp.wait()

    return pl.pallas_call(
        body, name="grads_to_sibling",
        out_shape=[jax.ShapeDtypeStruct(a.shape, a.dtype) for a in items],
        in_specs=[ANY] * n, out_specs=[ANY] * n,
        scratch_shapes=[pltpu.SemaphoreType.DMA((n,)), pltpu.SemaphoreType.DMA((n,))],
        compiler_params=_params(vmem=False, has_side_effects=True),
    )(*items)


def _sibling_inplace_call(name, bufs, slabs, n_pairs):
    n = len(bufs)

    def body(*refs):
        out = refs[n:2 * n]
        send_sems, recv_sems = refs[2 * n:]
        x, y, c, _ = _place()
        sibling = (x, y, 1 - c)
        pairs = [pair for w, ref in enumerate(out) for pair in slabs(ref, c, w)]
        sends = [_remote(s, s, send_sems.at[k], recv_sems.at[k], sibling) for k, (s, _) in enumerate(pairs)]
        for cp in sends:
            cp.start()
        for k, (_, r) in enumerate(pairs):
            _remote(r, r, send_sems.at[k], recv_sems.at[k], sibling).wait_recv()
        for cp in sends:
            cp.wait_send()

    return pl.pallas_call(
        body, name=name,
        out_shape=[jax.ShapeDtypeStruct(a.shape, a.dtype) for a in bufs],
        in_specs=[ANY] * n, out_specs=[ANY] * n,
        input_output_aliases={w: w for w in range(n)},
        scratch_shapes=[pltpu.SemaphoreType.DMA((n_pairs,)), pltpu.SemaphoreType.DMA((n_pairs,))],
        compiler_params=_params(vmem=False, has_side_effects=True),
    )(*bufs)


HBM_SPEC = pl.BlockSpec(memory_space=pltpu.HBM)
SEM_SPEC = pl.BlockSpec(memory_space=pltpu.SEMAPHORE)
DATAFLOW_EFFECT = pltpu.SideEffectType.DATAFLOW_SIDE_EFFECTING


def _exchange_start(name, bufs, copies, n_copies, after):
    return _exchange_start_many(name, [(bufs, copies, n_copies)], after)[0]


def _exchange_start_many(name, groups, after):
    sizes = [len(bufs) for bufs, _, _ in groups]
    starts = [sum(sizes[:g]) for g in range(len(groups))]
    n, n_sems = sum(sizes), 2 * len(groups)

    def body(*refs):
        ins, sems, token = refs[:n], refs[n + 1:n + 1 + n_sems], refs[-1]
        for g, (_, copies, _) in enumerate(groups):
            send_sems, recv_sems = sems[2 * g], sems[2 * g + 1]
            for k, (src, dst, to) in enumerate(copies(ins[starts[g]:starts[g] + sizes[g]])):
                _remote(src, dst, send_sems.at[k], recv_sems.at[k], to).start()
        token[...] = jnp.zeros_like(token)

    every = [b for bufs, _, _ in groups for b in bufs]
    outs = pl.pallas_call(
        body, name=name,
        out_shape=(*[pltpu.SemaphoreType.DMA((c,)) for _, _, c in groups for _ in range(2)],
                   *[pltpu.HBM(b.shape, b.dtype) for b in every], jax.ShapeDtypeStruct((SUBLANES, 128), F32)),
        in_specs=[HBM_SPEC] * n + [ANY],
        out_specs=(*[SEM_SPEC] * n_sems, *[HBM_SPEC] * n, pl.BlockSpec(memory_space=pltpu.VMEM)),
        input_output_aliases={w: w + n_sems for w in range(n)},
        compiler_params=pltpu.CompilerParams(has_side_effects=DATAFLOW_EFFECT),
    )(*[pltpu.with_memory_space_constraint(b, pltpu.HBM) for b in every], after)
    thru = outs[n_sems:n_sems + n]
    return [(outs[2 * g], outs[2 * g + 1], list(thru[starts[g]:starts[g] + sizes[g]]), outs[-1])
            for g in range(len(groups))]


def _exchange_wait(name, send_sems, recv_sems, bufs, copies, after):
    n = len(bufs)

    def body(*refs):
        ins, send_sems, recv_sems = refs[:n], refs[n], refs[n + 1]
        for k, (src, dst, to) in enumerate(copies(ins)):
            cp = _remote(src, dst, send_sems.at[k], recv_sems.at[k], to)
            cp.wait_send()
            cp.wait_recv()

    return pl.pallas_call(
        body, name=name,
        out_shape=[pltpu.HBM(b.shape, b.dtype) for b in bufs],
        in_specs=[HBM_SPEC] * n + [SEM_SPEC, SEM_SPEC, ANY],
        out_specs=[HBM_SPEC] * n,
        input_output_aliases={w: w for w in range(n)},
        compiler_params=pltpu.CompilerParams(has_side_effects=DATAFLOW_EFFECT),
    )(*bufs, send_sems, recv_sems, after)


def _gather_copies(refs):
    x, y, c, chips = _place()
    mine = 2 * (2 * x + y) + c
    return [(ref.at[mine], ref.at[mine], (qx, qy, c)) for ref in refs for qx, qy in chips]


def _forward_copies(refs):
    x, y, c, chips = _place()
    return [(ref.at[2 * (2 * qx + qy) + c], ref.at[2 * (2 * qx + qy) + c], (x, y, 1 - c))
            for ref in refs for qx, qy in chips]


def _gather_forward_slabs(ref, c, w):
    x, y, _, chips = _place()
    return [(ref.at[2 * (2 * qx + qy) + c], ref.at[2 * (2 * qx + qy) + 1 - c]) for qx, qy in chips]


def _device_peers():
    x, y, c, _ = _place()
    return 4 * x + 2 * y + c, [(k, (x ^ ((k >> 2) & 1), y ^ ((k >> 1) & 1), c ^ (k & 1))) for k in range(1, 8)]


def _small_scatter_copies(refs):
    me, peers = _device_peers()
    return [(refs[0].at[me ^ k], refs[1].at[me], to) for k, to in peers]


def _small_spread_copies(refs):
    me, peers = _device_peers()
    return [(refs[0].at[me], refs[0].at[me], to) for _, to in peers]


def _sibling_copies(refs):
    n = len(refs) // 2
    x, y, c, _ = _place()
    return [(refs[w], refs[n + w], (x, y, 1 - c)) for w in range(n)]


def _owner_copies(refs):
    n = len(refs) // 2
    x, y, c, chips = _place()
    return [(refs[w].at[2 * qx + qy], refs[n + w].at[j], (qx, qy, c))
            for w in range(n) for j, (qx, qy) in enumerate(chips)]


N_DEVICES = 8
SMALL_ROWS = 616


SMALL = ("norm_mix_g", "conv_w", "conv_b", "lru_w_a", "lru_b_a", "lru_w_x", "lru_b_x", "lru_lambda",
         "sgu_ln_g", "sgu_ln_b", "sgu_w_s", "sgu_b_s", "norm_ffn_g", "final_norm_g")
WEIGHTS = ("norm_mix_g", "w_in", "conv_w", "conv_b", "lru_w_a", "lru_b_a", "lru_w_x", "lru_b_x", "lru_lambda",
           "sgu_ln_g", "sgu_ln_b", "sgu_w_s", "sgu_b_s", "w_branch_a", "w_branch_b", "w_out", "norm_ffn_g",
           "w_up", "w_down", "final_norm_g")
PACK_ALIGN = SUBLANES * 128


PACKED = SMALL + ("loss",)


def _pack_small(gs):
    parts = []
    for k in PACKED:
        flat = gs[k].reshape(-1)
        parts.append(jnp.pad(flat, (0, -flat.size % PACK_ALIGN)))
    flat = jnp.concatenate(parts)
    flat = jnp.pad(flat, (0, N_DEVICES * SMALL_ROWS * 128 - flat.size))
    return flat.reshape(N_DEVICES, SMALL_ROWS, 128)


def _unpack_small(buf, like):
    flat = buf.reshape(-1)
    out, off = {}, 0
    for k in PACKED:
        size = like[k].size
        out[k] = flat[off:off + size].reshape(like[k].shape)
        off += size + (-size % PACK_ALIGN)
    return out


def _as_rows(a):
    return a.reshape(-1, a.shape[-1])


def kernel(x, norm_mix_g, w_in, conv_w, conv_b, lru_w_a, lru_b_a, lru_w_x, lru_b_x, lru_lambda, sgu_ln_g, sgu_ln_b, sgu_w_s, sgu_b_s, w_branch_a, w_branch_b, w_out, norm_ffn_g, w_up, w_down, final_norm_g, loss_target, m_norm_mix_g, m_w_in, m_conv_w, m_conv_b, m_lru_w_a, m_lru_b_a, m_lru_w_x, m_lru_b_x, m_lru_lambda, m_sgu_ln_g, m_sgu_ln_b, m_sgu_w_s, m_sgu_b_s, m_w_branch_a, m_w_branch_b, m_w_out, m_norm_ffn_g, m_w_up, m_w_down, m_final_norm_g, v_norm_mix_g, v_w_in, v_conv_w, v_conv_b, v_lru_w_a, v_lru_b_a, v_lru_w_x, v_lru_b_x, v_lru_lambda, v_sgu_ln_g, v_sgu_ln_b, v_sgu_w_s, v_sgu_b_s, v_w_branch_a, v_w_branch_b, v_w_out, v_norm_ffn_g, v_w_up, v_w_down, v_final_norm_g):
    w = dict(norm_mix_g=norm_mix_g, w_in=w_in, conv_w=conv_w, conv_b=conv_b, lru_w_a=lru_w_a, lru_b_a=lru_b_a,
             lru_w_x=lru_w_x, lru_b_x=lru_b_x, lru_lambda=lru_lambda, sgu_ln_g=sgu_ln_g, sgu_ln_b=sgu_ln_b,
             sgu_w_s=sgu_w_s, sgu_b_s=sgu_b_s, w_branch_a=w_branch_a, w_branch_b=w_branch_b, w_out=w_out,
             norm_ffn_g=norm_ffn_g, w_up=w_up, w_down=w_down, final_norm_g=final_norm_g)
    m = dict(norm_mix_g=m_norm_mix_g, w_in=m_w_in, conv_w=m_conv_w, conv_b=m_conv_b, lru_w_a=m_lru_w_a,
             lru_b_a=m_lru_b_a, lru_w_x=m_lru_w_x, lru_b_x=m_lru_b_x, lru_lambda=m_lru_lambda,
             sgu_ln_g=m_sgu_ln_g, sgu_ln_b=m_sgu_ln_b, sgu_w_s=m_sgu_w_s, sgu_b_s=m_sgu_b_s,
             w_branch_a=m_w_branch_a, w_branch_b=m_w_branch_b, w_out=m_w_out, norm_ffn_g=m_norm_ffn_g,
             w_up=m_w_up, w_down=m_w_down, final_norm_g=m_final_norm_g)
    v = dict(norm_mix_g=v_norm_mix_g, w_in=v_w_in, conv_w=v_conv_w, conv_b=v_conv_b, lru_w_a=v_lru_w_a,
             lru_b_a=v_lru_b_a, lru_w_x=v_lru_w_x, lru_b_x=v_lru_b_x, lru_lambda=v_lru_lambda,
             sgu_ln_g=v_sgu_ln_g, sgu_ln_b=v_sgu_ln_b, sgu_w_s=v_sgu_w_s, sgu_b_s=v_sgu_b_s,
             w_branch_a=v_w_branch_a, w_branch_b=v_w_branch_b, w_out=v_w_out, norm_ffn_g=v_norm_ffn_g,
             w_up=v_w_up, w_down=v_w_down, final_norm_g=v_final_norm_g)
    core = lax.axis_index("c")
    chip = 2 * lax.axis_index("x") + lax.axis_index("y")
    sel = jnp.stack([core, 1 - core, chip, 2 * chip + core]).astype(jnp.int32)
    this_core, this_chip = ("sel", 0), ("sel", 2)
    sds = jax.ShapeDtypeStruct

    ts = TOKEN_TILE

    def after_all(arrays):
        return jnp.stack([a[(0,) * a.ndim].astype(F32) for a in arrays])

    halves = {k: (w[k].shape[1] // 2, w[k].shape[2]) for k in BIG}

    def same_shape(keys):
        groups = {}
        for k in keys:
            groups.setdefault(halves[k], []).append(k)
        return list(groups.values())

    def half_view(k, a):
        return a.reshape((2 * N_QUARTERS,) + halves[k])

    def full_view(k, a):
        if k == "conv_w":
            return a.reshape(N_QUARTERS, DEPTH, CONV_WIDTH, -1).transpose(1, 2, 0, 3).reshape(DEPTH, CONV_WIDTH, D_RNN)
        r2, cols = halves[k]
        if k in ("w_in", "w_up"):
            return a.reshape(1, N_QUARTERS, 2 * r2, cols)
        return a.reshape(1, 2 * N_QUARTERS * r2, cols)

    layer_bufs = [{}, {}]

    def cast_weights(keys, after):
        for ks in same_shape(keys):
            outs = _ew_call(
                lambda *t: t, "cast_weights",
                [(w[k].reshape((DEPTH, 1) + w[k].shape[1:]), (l, 0)) for k in ks for l in range(DEPTH)],
                [(sds((1, N_QUARTERS) + w[k].shape[1:], BF), (0, this_chip)) for k in ks for l in range(DEPTH)],
                1, sel, after=after)
            for i, k in enumerate(ks):
                for l in range(DEPTH):
                    layer_bufs[l][k] = half_view(k, outs[DEPTH * i + l])

    conv_buf = lax.dynamic_update_slice_in_dim(
        jnp.zeros((N_QUARTERS, DEPTH) + conv_w.shape[1:], F32), conv_w[None], chip, axis=0)
    layer_bufs[0]["conv_w"] = conv_buf.reshape((2 * N_QUARTERS,) + conv_w.shape[1:])
    sm = {k: w[k] for k in SMALL if k != "conv_w"}

    def gather_start(tag, l, keys, after):
        bufs = [layer_bufs[l][k] for k in keys]
        return _exchange_start(f"gather_start_{tag}", bufs, _gather_copies, 3 * len(keys), after)

    def gather_finish(tag, keys, started, after):
        send_sems, recv_sems, thru, _ = started
        landed = _exchange_wait(f"gather_wait_{tag}", send_sems, recv_sems, thru, _gather_copies, after)
        landed = _sibling_inplace_call("gather_forward", landed, _gather_forward_slabs, 3 * len(keys))
        return {k: full_view(k, a) for k, a in zip(keys, landed)}

    first, rest = ("w_in",), tuple(k for k in BIG if k != "w_in")
    cast_weights(first, None)
    started_a = gather_start("0a", 0, first + ("conv_w",), sel)
    cast_weights(rest, started_a[3])
    started_b, started_c, started_d = _exchange_start_many(
        "gather_start_rest",
        [([layer_bufs[l][k] for k in keys], _gather_copies, 3 * len(keys)) for l, keys in ((0, rest), (1, first), (1, rest))],
        started_a[3])

    def arrives(tag, keys, started):
        state = {}

        def hook(after):
            landed = _exchange_wait(f"gather_wait_{tag}", started[0], started[1], started[2], _gather_copies, after)
            state["forward"] = _exchange_start(f"forward_start_{tag}", landed, _forward_copies, 3 * len(keys), after)
            return state["forward"][3]

        def finish(after):
            send_sems, recv_sems, thru, _ = state["forward"]
            done = _exchange_wait(f"forward_wait_{tag}", send_sems, recv_sems, thru, _forward_copies, after)
            return {k: full_view(k, a) for k, a in zip(keys, done)}

        return hook, finish

    p0, p1 = _layer_small(sm, 0, sel[0:1]), _layer_small(sm, 1, sel[0:1])
    h0 = _norm_call(x[0], p0["g1"], 2 * ts)
    proj_own = _inproj_part_call(h0, full_view("w_in", started_a[2][0]), 2 * ts, sel[2:3], 0, 1)
    ready = after_all([started_d[3], proj_own] + [p[k] for p in (p0, p1) for k in ("wa", "wx", "wm")])
    big0 = gather_finish("0a", first + ("conv_w",), started_a, ready)
    for l, p in enumerate((p0, p1)):
        p["cw"] = big0["conv_w"][l]
    proj0 = _inproj_part_call(h0, big0["w_in"], 2 * ts, sel[2:3], 1, N_QUARTERS - 1, proj_own)
    hook, finish = arrives("0b", rest, started_b)
    sv0 = _layer_fwd_mix(x[0], big0, p0, ts, h0, hook, proj0)
    big0.update(finish(sv0["yb_pre"]))
    x_mid = _layer_fwd_out(sv0, big0, ts)
    hook, finish = arrives("1a", first, started_c)
    h1 = _norm_call(x_mid, p1["g1"], 2 * ts, hook(x_mid))
    big1 = finish(h1)
    hook, finish = arrives("1b", rest, started_d)
    sv1 = _layer_fwd_mix(x_mid, big1, p1, ts, h1, hook)
    big1.update(finish(sv1["yb_pre"]))
    x_out = _layer_fwd_out(sv1, big1, ts)
    dx, loss, dgf = _loss_call(x_out, loss_target[0], final_norm_g.reshape(1, -1), 2 * ts)

    def pair_start(tag, gb, after):
        sends = [gb[k][1] for k in gb]
        zones = [lax.empty(a.shape, BF) for a in sends]
        return _exchange_start(f"pair_start_{tag}", sends + zones, _sibling_copies, len(sends), after)

    def reduce_start(tag, gb, after, pair=None):
        keys = tuple(gb)
        if pair is None:
            from_sibling = _sibling_send_call([gb[k][1] for k in keys])
        else:
            done = _exchange_wait(f"pair_wait_{tag}", pair[0], pair[1], pair[2], _sibling_copies, after)
            from_sibling = done[len(keys):]
        received = dict(zip(keys, from_sibling))
        sums = {}
        for ks in same_shape(keys):
            outs = _ew_call(
                lambda *t: tuple(a.astype(F32) + b.astype(F32) for a, b in zip(t[0::2], t[1::2])), "pair_sum",
                [(a[None], (0, "g")) for k in ks for a in (gb[k][0], received[k])],
                [(sds((1,) + received[k].shape, BF), (0, "g")) for k in ks], N_QUARTERS)
            sums.update({k: o[0] for k, o in zip(ks, outs)})
        sums = [sums[k] for k in keys]
        zones = [lax.empty((3,) + a.shape[1:], BF) for a in sums]
        started = _exchange_start(f"reduce_start_{tag}", sums + zones, _owner_copies, 3 * len(keys), after)
        return keys, started

    def reduce_finish(tag, l, keys_started, after, reduced):
        keys, (send_sems, recv_sems, thru, _) = keys_started
        done = _exchange_wait(f"reduce_wait_{tag}", send_sems, recv_sems, thru, _owner_copies, after)
        sums, zones = done[:len(keys)], done[len(keys):]
        sums, zones = dict(zip(keys, sums)), dict(zip(keys, zones))
        for ks in same_shape(keys):
            outs = _ew_call(
                lambda *t: tuple(((a.astype(F32) + b.astype(F32)) + c.astype(F32)) + d.astype(F32)
                                 for a, b, c, d in zip(t[0::4], t[1::4], t[2::4], t[3::4])),
                "quarter_sum",
                [op for k in ks for op in [(sums[k][None], (0, this_chip))] + [(zones[k][None], (0, j)) for j in range(3)]],
                [(sds((DEPTH, 2) + halves[k], F32), (l, this_core)) for k in ks], 1, sel,
                into=[reduced[k] for k in ks] if ks[0] in reduced else None)
            reduced.update(zip(ks, outs))

    dx1, gb_ffn, gs1 = _layer_bwd_ffn(dx, sv1, big1, ts)
    merge_out, gb_merge = _layer_bwd_merge(dx1, sv1, big1, ts)
    dx_mid, gb_in, gs1_mix = _layer_bwd_branches(dx1, merge_out, sv1, big1, lru_lambda[1], ts)
    gb_1 = {**gb_ffn, **gb_merge, **gb_in}
    pair_1 = pair_start("1", gb_1, dx_mid)
    dx1, gb_ffn, gs0 = _layer_bwd_ffn(dx_mid, sv0, big0, ts, pair_1[3])
    exchange_1 = reduce_start("1", gb_1, dx1, pair_1)
    pair_0a = pair_start("0a", gb_ffn, exchange_1[1][3])
    merge_out, gb_merge = _layer_bwd_merge(dx1, sv0, big0, ts, pair_0a[3])
    exchange_0a = reduce_start("0a", gb_ffn, merge_out[0], pair_0a)
    pair_0b = pair_start("0b", gb_merge, exchange_0a[1][3])
    started_0b = {}

    def after_sgu(duv):
        started_0b["exchange"] = reduce_start("0b", gb_merge, duv, pair_0b)
        return started_0b["exchange"][1][3]

    grad_x, gb_in, gs0_mix = _layer_bwd_branches(dx1, merge_out, sv0, big0, lru_lambda[0], ts, pair_0b[3],
                                                 after_sgu)
    exchange_0b = started_0b["exchange"]
    exchange_0c = reduce_start("0c", gb_in, exchange_0b[1][3])
    layer_gs = [{**gs0, **gs0_mix}, {**gs1, **gs1_mix}]
    gs = {k: jnp.stack([g[k] for g in layer_gs]) for k in layer_gs[0]}
    gs["final_norm_g"] = dgf[0]
    gs["loss"] = loss[0, 0:1]

    me = ("sel", 3)
    piece = (1, N_DEVICES, SMALL_ROWS, 128)
    packed = _pack_small(gs).reshape(piece)
    scatter = _exchange_start("small_scatter_start", [packed[0], lax.empty(piece[1:], F32)], _small_scatter_copies,
                              N_DEVICES - 1, exchange_0c[1][3])
    reduced = {}
    reduce_finish("1", 1, exchange_1, scatter[3], reduced)
    reduce_finish("0a", 0, exchange_0a, reduced["w_in"], reduced)
    reduce_finish("0b", 0, exchange_0b, reduced["w_down"], reduced)

    def swap_slabs(ref, c, i):
        layers = (1,) if BIG[i] == "w_in" else range(DEPTH)
        return [(ref.at[l, c], ref.at[l, 1 - c]) for l in layers]

    swapped = dict(zip(BIG, _sibling_inplace_call("grads_swap_halves", [reduced[k] for k in BIG], swap_slabs,
                                                  DEPTH * len(BIG) - 1)))

    def adamw_layers(keys, grads, layer, into, after=None):
        if layer is None:
            views = [_as4(_as_rows(a)) for k in keys for a in (w[k], grads[k], m[k], v[k])]
            idx = (0, 0)
        else:
            views = [a.reshape((1,) + w[k].shape) for k in keys for a in (w[k], grads[k], m[k], v[k])]
            idx = (0, layer)
        outs = _ew_call(lambda *t: tuple(o for i in range(0, len(t), 4) for o in _adamw(*t[i:i + 4])), "adamw_big",
                        [(a, idx) for a in views], [(sds(views[0].shape, F32), idx)] * (3 * len(keys)),
                        into=into, after=after)
        return {k: outs[3 * i:3 * i + 3] for i, k in enumerate(keys)}

    updated = adamw_layers(("w_in",), swapped, 1, None)
    last_update = updated["w_in"][0]
    for ks in same_shape(k for k in BIG if k != "w_in"):
        updated.update(adamw_layers(ks, swapped, None, None, last_update))
        last_update = updated[ks[0]][0]
    scattered = _exchange_wait("small_scatter_wait", scatter[0], scatter[1], scatter[2], _small_scatter_copies,
                               last_update)
    summed = _ew_call(
        lambda *parts: (functools.reduce(lambda a, b: a + b, parts),), "small_sum",
        [(scattered[0][None], (0, me))]
        + [(scattered[1][None], (0, lambda g, s, k=k: s[3] ^ k)) for k in range(1, N_DEVICES)],
        [(sds(piece, F32), (0, me))], 1, sel)[0]
    spread = _exchange_start("small_spread_start", [summed[0]], _small_spread_copies, N_DEVICES - 1, summed)
    reduced["w_in"] = swapped["w_in"]
    reduce_finish("0c", 0, exchange_0c, spread[3], reduced)
    last = _sibling_inplace_call("grads_swap_last", [reduced["w_in"]],
                                 lambda ref, c, i: [(ref.at[0, c], ref.at[0, 1 - c])], 1)[0]
    swapped["w_in"] = last
    updated.update(adamw_layers(("w_in",), swapped, 0, updated["w_in"]))
    grads_big = {k: swapped[k].reshape(w[k].shape) for k in BIG}
    delta, new_m, new_v = ({k: updated[k][j].reshape(w[k].shape) for k in BIG} for j in range(3))
    gathered_small = _exchange_wait("small_spread_wait", spread[0], spread[1], spread[2], _small_spread_copies,
                                    updated["w_in"][0])[0]

    like = {k: jax.ShapeDtypeStruct(gs[k].shape, F32) for k in SMALL}
    like["loss"] = jax.ShapeDtypeStruct((1,), F32)
    grads_small = _unpack_small(gathered_small, like)
    total = grads_small.pop("loss")[0]
    conv_q = grads_small["conv_w"].reshape(DEPTH, CONV_WIDTH, N_QUARTERS, D_RNN // N_QUARTERS)
    grads_small["conv_w"] = lax.dynamic_index_in_dim(conv_q, chip, axis=2, keepdims=False)
    at_least_2d = lambda a: a.reshape(1, -1) if a.ndim == 1 else a
    outs = _small_adamw_call(*[[at_least_2d(d[k]) for k in SMALL] for d in (w, grads_small, m, v)])
    for d, o in zip((delta, new_m, new_v), outs):
        for k, a in zip(SMALL, o):
            d[k] = a.reshape(w[k].shape)

    grads = {**grads_big, **grads_small}
    return (total, grad_x[None], *[grads[k] for k in WEIGHTS], *[delta[k] for k in WEIGHTS],
            *[new_m[k] for k in WEIGHTS], *[new_v[k] for k in WEIGHTS])
```

```python
import functools
import math

import jax
import jax.numpy as jnp
from jax import lax
from jax.experimental import pallas as pl
from jax.experimental.pallas import tpu as pltpu

F32 = jnp.float32
BF = jnp.bfloat16

DEPTH = 2
D_MODEL = 1024
D_RNN = 1280
D_SGU = 1024
D_FF = 4096
D_IN = 2 * D_RNN + 2 * D_SGU + 2 * D_MODEL
N_QUARTERS = 4
Q_IN = D_IN // N_QUARTERS
Q_FF = D_FF // N_QUARTERS
RNN_HEADS = 20
RNN_HEAD_DIM = 64
LRU_GROUP = 256
N_LRU_GROUPS = D_RNN // LRU_GROUP
HEADS_PER_GROUP = LRU_GROUP // RNN_HEAD_DIM
CONV_WIDTH = 4
LRU_C = 8.0
SGU_GROUPS = 8
SGU_BLOCK = 128
CHUNK = 64
EPS = 1e-6

ADAM_LR = 0.001
ADAM_B1 = 0.9
ADAM_B2 = 0.999
ADAM_EPS = 1e-08
ADAM_WD = 0.01
ADAM_STEP = 10

SUBLANES = 8
TOKEN_TILE = 512
VMEM_LIMIT_BYTES = 56 * 1024 * 1024

MESH = pl.DeviceIdType.MESH


def _params(semantics=None, vmem=True, **kw):
    return pltpu.CompilerParams(
        dimension_semantics=semantics,
        vmem_limit_bytes=VMEM_LIMIT_BYTES if vmem else None,
        **kw,
    )


def _dot(a, b):
    return jnp.dot(a, b, preferred_element_type=F32)


def _dot_nt(a, b):
    return lax.dot_general(a, b, (((1,), (1,)), ((), ())), preferred_element_type=F32)


def _dot_tn(a, b):
    return lax.dot_general(a, b, (((0,), (0,)), ((), ())), preferred_element_type=F32)


_GELU_C = math.sqrt(2.0 / math.pi)
_GELU_A = 0.044715


def _gelu(x):
    return 0.5 * x * (1.0 + jnp.tanh(_GELU_C * (x + _GELU_A * x * x * x)))


def _gelu_and_grad(x):
    x2 = x * x
    t = jnp.tanh(_GELU_C * (x + _GELU_A * x2 * x))
    du = _GELU_C * (1.0 + 3.0 * _GELU_A * x2)
    return 0.5 * x * (1.0 + t), 0.5 * (1.0 + t) + 0.5 * x * (1.0 - t * t) * du


def _rms_stats(x):
    return lax.rsqrt(jnp.mean(x * x, axis=-1, keepdims=True) + EPS)


def _rms_bwd(dy, x, g):
    rs = _rms_stats(x)
    n = x * rs
    dn = dy * g
    dx = rs * (dn - n * jnp.mean(dn * n, axis=-1, keepdims=True))
    return dx, dy * n


def _row_sum(x):
    return jnp.sum(x, axis=0, keepdims=True)


def _tile_spec(ts, width, col=0):
    return pl.BlockSpec((ts, width), lambda i, col=col: (i, col))


def _full_spec(shape):
    zeros = (0,) * len(shape)
    return pl.BlockSpec(shape, lambda *_: zeros)


def _layer_spec(w, layer):
    zeros = (0,) * (w.ndim - 1)
    return pl.BlockSpec((None,) + tuple(w.shape[1:]), lambda *_: (layer,) + zeros)


def _with_after(body, n_in, after):
    if after is None:
        return body, [], []

    def wrapped(*refs):
        return body(*refs[:n_in], *refs[n_in + 1:])

    return wrapped, [pl.BlockSpec(memory_space=pl.ANY)], [after]


def _norm_call(x, g, ts, after=None):
    s = x.shape[0]

    def body(x_ref, g_ref, h_ref):
        xv = x_ref[...]
        h_ref[...] = (xv * _rms_stats(xv) * g_ref[...]).astype(BF)

    body, more_specs, more = _with_after(body, 2, after)
    return pl.pallas_call(
        body, name="norm_fwd", grid=(s // ts,),
        in_specs=[_tile_spec(ts, D_MODEL), _full_spec((1, D_MODEL))] + more_specs,
        out_specs=_tile_spec(ts, D_MODEL),
        out_shape=jax.ShapeDtypeStruct((s, D_MODEL), BF),
        compiler_params=_params(("parallel",)),
    )(x, g, *more)


def _inproj_call(h, w_in, layer, ts):
    s = h.shape[0]

    def body(h_ref, w_ref, o_ref):
        o_ref[...] = _dot(h_ref[...], w_ref[...]).astype(BF)

    return pl.pallas_call(
        body, name="inproj_fwd", grid=(N_QUARTERS, s // ts),
        in_specs=[
            pl.BlockSpec((ts, D_MODEL), lambda q, i: (i, 0)),
            pl.BlockSpec((None, None, D_MODEL, Q_IN), lambda q, i: (layer, q, 0, 0)),
        ],
        out_specs=pl.BlockSpec((ts, Q_IN), lambda q, i: (i, q)),
        out_shape=jax.ShapeDtypeStruct((s, D_IN), BF),
        compiler_params=_params(("parallel", "parallel")),
    )(h, w_in)


def _inproj_part_call(h, w_in, ts, own, first, count, into=None):
    s = h.shape[0]

    def quarter(j, sel):
        return (sel[0] + first + j) % N_QUARTERS

    def body(sel_ref, h_ref, w_ref, *rest):
        rest[-1][...] = _dot(h_ref[...], w_ref[...]).astype(BF)

    in_specs = [pl.BlockSpec((ts, D_MODEL), lambda j, i, sel: (i, 0)),
                pl.BlockSpec((None, None, D_MODEL, Q_IN), lambda j, i, sel: (0, quarter(j, sel), 0, 0))]
    operands = [h, w_in]
    aliases = {}
    if into is not None:
        in_specs.append(pl.BlockSpec(memory_space=pl.ANY))
        operands.append(into)
        aliases = {3: 0}
    return pl.pallas_call(
        body, name="inproj_fwd_part", out_shape=jax.ShapeDtypeStruct((s, D_IN), BF),
        grid_spec=pltpu.PrefetchScalarGridSpec(
            num_scalar_prefetch=1, grid=(count, s // ts), in_specs=in_specs,
            out_specs=pl.BlockSpec((ts, Q_IN), lambda j, i, sel: (i, quarter(j, sel)))),
        input_output_aliases=aliases,
        compiler_params=_params(("parallel", "parallel")),
    )(own, *operands)


def _shift_down(x, tail, s):
    xr = pltpu.roll(x, s, 0)
    tr = pltpu.roll(tail, s, 0)
    row = lax.broadcasted_iota(jnp.int32, tail.shape, 0)
    top = jnp.where(row < s, tr, xr[0:SUBLANES])
    return jnp.concatenate([top, xr[SUBLANES:]], axis=0)


def _shift_up(x, head, s):
    t = x.shape[0]
    xr = pltpu.roll(x, t - s, 0)
    hr = pltpu.roll(head, SUBLANES - s, 0)
    row = lax.broadcasted_iota(jnp.int32, head.shape, 0)
    bottom = jnp.where(row >= SUBLANES - s, hr, xr[t - SUBLANES:])
    return jnp.concatenate([xr[: t - SUBLANES], bottom], axis=0)


def _conv_fwd(x, tail, cw_ref, cb_ref):
    out = cb_ref[...] + cw_ref[CONV_WIDTH - 1:CONV_WIDTH, :] * x
    for s in range(1, CONV_WIDTH):
        k = CONV_WIDTH - 1 - s
        out = out + cw_ref[k:k + 1, :] * _shift_down(x, tail, s)
    return out


def _group_dot(x_bf, w_ref, dot):
    cols = [dot(x_bf[:, g * LRU_GROUP:(g + 1) * LRU_GROUP], w_ref[g]) for g in range(N_LRU_GROUPS)]
    return jnp.concatenate(cols, axis=1)


def _lru_gates(xr, wa_ref, wx_ref, ba_ref, bx_ref, sp_ref):
    xb = xr.astype(BF)
    r = jax.nn.sigmoid(_group_dot(xb, wa_ref, _dot) + ba_ref[...])
    i = jax.nn.sigmoid(_group_dot(xb, wx_ref, _dot) + bx_ref[...])
    log_a = (-LRU_C * r) * sp_ref[...]
    a = jnp.exp(log_a)
    nrm2 = -jnp.tanh(log_a) * (a * a + 1.0)
    inv_nrm = lax.rsqrt(jnp.maximum(nrm2, 1e-36))
    return r, i, a, nrm2 * inv_nrm, inv_nrm


def _linear_scan(a, b, carry, al_ref, bl_ref, h_ref, reverse):
    t, c = a.shape
    rowm = lax.broadcasted_iota(jnp.int32, (t, c), 0) & (SUBLANES - 1)
    for d in (1, 2, 4):
        if reverse:
            keep, sh = rowm < SUBLANES - d, t - d
        else:
            keep, sh = rowm >= d, d
        a_sh = jnp.where(keep, pltpu.roll(a, sh, 0), 1.0)
        b_sh = jnp.where(keep, pltpu.roll(b, sh, 0), 0.0)
        b = a * b_sh + b
        a = a * a_sh
    al_ref[...] = a
    bl_ref[...] = b
    groups = t // SUBLANES

    def step(j, state):
        jj = groups - 1 - j if reverse else j
        off = pl.multiple_of(jj * SUBLANES, SUBLANES)
        rows = bl_ref[pl.ds(off, SUBLANES), :] + al_ref[pl.ds(off, SUBLANES), :] * state
        h_ref[pl.ds(off, SUBLANES), :] = rows
        last = rows[0:1, :] if reverse else rows[SUBLANES - 1:SUBLANES, :]
        return jnp.broadcast_to(last, (SUBLANES, c))

    out = lax.fori_loop(0, groups, step, jnp.broadcast_to(carry, (SUBLANES, c)))
    return out[0:1, :]


def _rnn_fwd_call(proj, wa, wx, ba, bx, sp, cw, cb, ts):
    s = proj.shape[0]

    def body(xg_ref, wa_ref, wx_ref, ba_ref, bx_ref, sp_ref, cw_ref, cb_ref, xr_ref, hr_ref, ya_ref,
             tail_sc, carry_sc, al_sc, bl_sc, h_sc):
        @pl.when(pl.program_id(0) == 0)
        def _():
            tail_sc[...] = jnp.zeros_like(tail_sc)
            carry_sc[...] = jnp.zeros_like(carry_sc)

        x = xg_ref[:, :D_RNN].astype(F32)
        g = xg_ref[:, D_RNN:]
        xr = _conv_fwd(x, tail_sc[...], cw_ref, cb_ref)
        tail_sc[...] = x[ts - SUBLANES:, :]
        xr_ref[...] = xr.astype(BF)
        _, i, a, nrm, _ = _lru_gates(xr, wa_ref, wx_ref, ba_ref, bx_ref, sp_ref)
        carry_sc[...] = _linear_scan(a, nrm * (i * xr), carry_sc[...], al_sc, bl_sc, h_sc, False)
        h = h_sc[...]
        hr_ref[...] = h.astype(BF)
        ya_ref[...] = (h * _gelu(g)).astype(BF)

    gw = (N_LRU_GROUPS, LRU_GROUP, LRU_GROUP)
    return pl.pallas_call(
        body, name="rnn_fwd", grid=(s // ts,),
        in_specs=[_tile_spec(ts, 2 * D_RNN), _full_spec(gw), _full_spec(gw),
                  _full_spec((1, D_RNN)), _full_spec((1, D_RNN)), _full_spec((1, D_RNN)),
                  _full_spec((CONV_WIDTH, D_RNN)), _full_spec((1, D_RNN))],
        out_specs=[_tile_spec(ts, D_RNN)] * 3,
        out_shape=[jax.ShapeDtypeStruct((s, D_RNN), BF)] * 3,
        scratch_shapes=[pltpu.VMEM((SUBLANES, D_RNN), F32), pltpu.VMEM((1, D_RNN), F32),
                        pltpu.VMEM((ts, D_RNN), F32), pltpu.VMEM((ts, D_RNN), F32),
                        pltpu.VMEM((ts, D_RNN), F32)],
        compiler_params=_params(("arbitrary",)),
    )(proj, wa, wx, ba, bx, sp, cw, cb)


def _layernorm_fwd(x):
    mu = jnp.mean(x, axis=-1, keepdims=True)
    xc = x - mu
    rstd = lax.rsqrt(jnp.mean(xc * xc, axis=-1, keepdims=True) + EPS)
    return xc * rstd, rstd


def _sgu_mix(vn_bf, wm_ref, bsb_ref, ts):
    rows = []
    for blk in range(ts // SGU_BLOCK):
        r0 = blk * SGU_BLOCK
        cols = [
            _dot(wm_ref[g], vn_bf[r0:r0 + SGU_BLOCK, g * SGU_BLOCK:(g + 1) * SGU_BLOCK]) + bsb_ref[g]
            for g in range(SGU_GROUPS)
        ]
        rows.append(jnp.concatenate(cols, axis=1))
    return jnp.concatenate(rows, axis=0)


def _sgu_fwd_call(proj, wm, bsb, lg, lb, ts, after=None):
    s = proj.shape[0]

    def body(uv_ref, wm_ref, bsb_ref, lg_ref, lb_ref, yb_ref):
        gu = _gelu(uv_ref[:, :D_SGU])
        gv = _gelu(uv_ref[:, D_SGU:2 * D_SGU]).astype(F32)
        nh, _ = _layernorm_fwd(gv)
        vn = (nh * lg_ref[...] + lb_ref[...]).astype(BF)
        yb_ref[...] = (gu * _sgu_mix(vn, wm_ref, bsb_ref, ts)).astype(BF)

    sw = (SGU_GROUPS, SGU_BLOCK, SGU_BLOCK)
    body, more_specs, more = _with_after(body, 5, after)
    return pl.pallas_call(
        body, name="sgu_fwd", grid=(s // ts,),
        in_specs=[_tile_spec(ts, 2 * D_RNN, 1), _full_spec(sw), _full_spec(sw),
                  _full_spec((1, D_SGU)), _full_spec((1, D_SGU))] + more_specs,
        out_specs=_tile_spec(ts, D_SGU),
        out_shape=jax.ShapeDtypeStruct((s, D_SGU), BF),
        compiler_params=_params(("parallel",)),
    )(proj, wm, bsb, lg, lb, *more)


_GATE_COL0 = (2 * D_RNN + 2 * D_SGU) // 512


def _gate_specs(ts):
    return [_tile_spec(ts, 512, _GATE_COL0 + j) for j in range(4)]


def _merge_call(x, proj, ya_pre, yb_pre, w_ba, w_bb, w_out, g2, layer, ts):
    s = x.shape[0]

    def body(x_ref, ga0, ga1, gb0, gb1, ya_ref, yb_ref, wa_ref, wb_ref, wo_ref, g2_ref,
             x1_ref, yao_ref, ybo_ref, mg_ref, h2_ref):
        ya = _dot(ya_ref[...], wa_ref[...])
        yb = _dot(yb_ref[...], wb_ref[...])
        sa = jax.nn.sigmoid(jnp.concatenate([ga0[...], ga1[...]], axis=1).astype(F32))
        sb = jax.nn.sigmoid(jnp.concatenate([gb0[...], gb1[...]], axis=1).astype(F32))
        merged = (sa * ya + sb * yb).astype(BF)
        x1 = x_ref[...] + _dot(merged, wo_ref[...])
        x1_ref[...] = x1
        yao_ref[...] = ya.astype(BF)
        ybo_ref[...] = yb.astype(BF)
        mg_ref[...] = merged
        h2_ref[...] = (x1 * _rms_stats(x1) * g2_ref[...]).astype(BF)

    act = jax.ShapeDtypeStruct((s, D_MODEL), BF)
    return pl.pallas_call(
        body, name="merge_fwd", grid=(s // ts,),
        in_specs=[_tile_spec(ts, D_MODEL)] + _gate_specs(ts) + [
            _tile_spec(ts, D_RNN), _tile_spec(ts, D_SGU),
            _layer_spec(w_ba, layer), _layer_spec(w_bb, layer), _layer_spec(w_out, layer),
            _full_spec((1, D_MODEL))],
        out_specs=[_tile_spec(ts, D_MODEL)] * 5,
        out_shape=[jax.ShapeDtypeStruct((s, D_MODEL), F32), act, act, act, act],
        compiler_params=_params(("parallel",)),
    )(x, proj, proj, proj, proj, ya_pre, yb_pre, w_ba, w_bb, w_out, g2)


def _ffn_call(x1, h2, w_up, w_down, layer, ts):
    s = x1.shape[0]

    def body(x1_ref, h2_ref, wu_ref, wd_ref, x2_ref, p_ref):
        h2v = h2_ref[...]
        acc = x1_ref[...]
        for q in range(N_QUARTERS):
            p = _dot(h2v, wu_ref[q])
            p_ref[:, q * Q_FF:(q + 1) * Q_FF] = p.astype(BF)
            f = jnp.square(jnp.maximum(p, 0.0)).astype(BF)
            acc = acc + _dot(f, wd_ref[q * Q_FF:(q + 1) * Q_FF, :])
        x2_ref[...] = acc

    return pl.pallas_call(
        body, name="ffn_fwd", grid=(s // ts,),
        in_specs=[_tile_spec(ts, D_MODEL), _tile_spec(ts, D_MODEL),
                  pl.BlockSpec((None, N_QUARTERS, D_MODEL, Q_FF), lambda i: (layer, 0, 0, 0)),
                  pl.BlockSpec((None, D_FF, D_MODEL), lambda i: (layer, 0, 0))],
        out_specs=[_tile_spec(ts, D_MODEL), _tile_spec(ts, D_FF)],
        out_shape=[jax.ShapeDtypeStruct((s, D_MODEL), F32), jax.ShapeDtypeStruct((s, D_FF), BF)],
        compiler_params=_params(("parallel",)),
    )(x1, h2, w_up, w_down)


def _loss_call(x, target, gf, ts):
    s = x.shape[0]

    def body(x_ref, t_ref, g_ref, dx_ref, loss_ref, dg_ref):
        @pl.when(pl.program_id(0) == 0)
        def _():
            loss_ref[...] = jnp.zeros_like(loss_ref)
            dg_ref[...] = jnp.zeros_like(dg_ref)

        xv = x_ref[...]
        gv = g_ref[...]
        err = xv * _rms_stats(xv) * gv - t_ref[...]
        part = 0.5 * jnp.sum(jnp.mean(err * err, axis=-1, keepdims=True), axis=0, keepdims=True)
        loss_ref[...] += jnp.broadcast_to(part, loss_ref.shape)
        dx, dg = _rms_bwd(err * (1.0 / D_MODEL), xv, gv)
        dx_ref[...] = dx
        dg_ref[...] += _row_sum(dg)

    return pl.pallas_call(
        body, name="loss_head", grid=(s // ts,),
        in_specs=[_tile_spec(ts, D_MODEL), _tile_spec(ts, D_MODEL), _full_spec((1, D_MODEL))],
        out_specs=[_tile_spec(ts, D_MODEL), _full_spec((1, 128)), _full_spec((1, D_MODEL))],
        out_shape=[jax.ShapeDtypeStruct((s, D_MODEL), F32), jax.ShapeDtypeStruct((1, 128), F32),
                   jax.ShapeDtypeStruct((1, D_MODEL), F32)],
        compiler_params=_params(("arbitrary",)),
    )(x, target, gf)


def _ffn_bwd_call(dx2, p, x1, g2, w_up, w_down, layer, ts, after=None):
    s = dx2.shape[0]

    def body(dx2_ref, p_ref, x1_ref, g2_ref, wu_ref, wd_ref, dx1_ref, dp_ref, dg_ref, dx2b_ref, dx1b_ref):
        @pl.when(pl.program_id(0) == 0)
        def _():
            dg_ref[...] = jnp.zeros_like(dg_ref)

        dx2v = dx2_ref[...]
        dyb = dx2v.astype(BF)
        dx2b_ref[...] = dyb
        dh2 = jnp.zeros((ts, D_MODEL), F32)
        for q in range(N_QUARTERS):
            cols = slice(q * Q_FF, (q + 1) * Q_FF)
            df = _dot_nt(dyb, wd_ref[cols, :])
            dp = (df * (2.0 * jnp.maximum(p_ref[:, cols].astype(F32), 0.0))).astype(BF)
            dp_ref[:, cols] = dp
            dh2 = dh2 + _dot_nt(dp, wu_ref[q])
        dx, dg = _rms_bwd(dh2, x1_ref[...], g2_ref[...])
        dx1 = dx2v + dx
        dx1_ref[...] = dx1
        dx1b_ref[...] = dx1.astype(BF)
        dg_ref[...] += _row_sum(dg)

    body, more_specs, more = _with_after(body, 6, after)
    return pl.pallas_call(
        body, name="ffn_bwd", grid=(s // ts,),
        in_specs=[_tile_spec(ts, D_MODEL), _tile_spec(ts, D_FF), _tile_spec(ts, D_MODEL),
                  _full_spec((1, D_MODEL)),
                  pl.BlockSpec((None, N_QUARTERS, D_MODEL, Q_FF), lambda i: (layer, 0, 0, 0)),
                  pl.BlockSpec((None, D_FF, D_MODEL), lambda i: (layer, 0, 0))] + more_specs,
        out_specs=[_tile_spec(ts, D_MODEL), _tile_spec(ts, D_FF), _full_spec((1, D_MODEL)),
                   _tile_spec(ts, D_MODEL), _tile_spec(ts, D_MODEL)],
        out_shape=[jax.ShapeDtypeStruct((s, D_MODEL), F32), jax.ShapeDtypeStruct((s, D_FF), BF),
                   jax.ShapeDtypeStruct((1, D_MODEL), F32),
                   jax.ShapeDtypeStruct((s, D_MODEL), BF), jax.ShapeDtypeStruct((s, D_MODEL), BF)],
        compiler_params=_params(("arbitrary",)),
    )(dx2, p, x1, g2, w_up, w_down, *more)


def _merge_bwd_call(dx1, proj, ya, yb, w_ba, w_bb, w_out, layer, ts, after=None):
    s = dx1.shape[0]

    def body(dx1_ref, ga0, ga1, gb0, gb1, ya_ref, yb_ref, wa_ref, wb_ref, wo_ref, *rest):
        dya_ref, dyb_ref, dgate_ref, dyap_ref, dybp_ref = rest[-5:]
        dm = _dot_nt(dx1_ref[...].astype(BF), wo_ref[...])
        sa = jax.nn.sigmoid(jnp.concatenate([ga0[...], ga1[...]], axis=1).astype(F32))
        sb = jax.nn.sigmoid(jnp.concatenate([gb0[...], gb1[...]], axis=1).astype(F32))
        dya = (dm * sa).astype(BF)
        dyb = (dm * sb).astype(BF)
        dya_ref[...] = dya
        dyb_ref[...] = dyb
        dgate_ref[:, :D_MODEL] = (dm * ya_ref[...].astype(F32) * sa * (1.0 - sa)).astype(BF)
        dgate_ref[:, D_MODEL:] = (dm * yb_ref[...].astype(F32) * sb * (1.0 - sb)).astype(BF)
        dyap_ref[...] = _dot_nt(dya, wa_ref[...]).astype(BF)
        dybp_ref[...] = _dot_nt(dyb, wb_ref[...]).astype(BF)

    act = jax.ShapeDtypeStruct((s, D_MODEL), BF)
    return pl.pallas_call(
        body, name="merge_bwd", grid=(s // ts,),
        in_specs=[_tile_spec(ts, D_MODEL)] + _gate_specs(ts) + [
            _tile_spec(ts, D_MODEL), _tile_spec(ts, D_MODEL),
            _layer_spec(w_ba, layer), _layer_spec(w_bb, layer), _layer_spec(w_out, layer)]
        + ([] if after is None else [pl.BlockSpec(memory_space=pl.ANY)]),
        out_specs=[_tile_spec(ts, D_MODEL), _tile_spec(ts, D_MODEL), _tile_spec(ts, 2 * D_MODEL),
                   _tile_spec(ts, D_RNN), _tile_spec(ts, D_SGU)],
        out_shape=[act, act, jax.ShapeDtypeStruct((s, 2 * D_MODEL), BF),
                   jax.ShapeDtypeStruct((s, D_RNN), BF), jax.ShapeDtypeStruct((s, D_SGU), BF)],
        compiler_params=_params(("parallel",)),
    )(dx1, proj, proj, proj, proj, ya, yb, w_ba, w_bb, w_out, *([] if after is None else [after]))


def _sgu_bwd_call(dyb_pre, proj, wm, bsb, mask, lg, lb, ts, after=None):
    s = proj.shape[0]

    def body(dy_ref, uv_ref, wm_ref, bsb_ref, mask_ref, lg_ref, lb_ref,
             duv_ref, dws_ref, dbs_ref, dlg_ref, dlb_ref, dm_sc):
        step = pl.program_id(0)

        @pl.when(step == 0)
        def _():
            dws_ref[...] = jnp.zeros_like(dws_ref)
            dlg_ref[...] = jnp.zeros_like(dlg_ref)
            dlb_ref[...] = jnp.zeros_like(dlb_ref)
            dm_sc[...] = jnp.zeros_like(dm_sc)

        gu, dgu_du = _gelu_and_grad(uv_ref[:, :D_SGU])
        gv, dgv_dv = _gelu_and_grad(uv_ref[:, D_SGU:2 * D_SGU])
        nh, rstd = _layernorm_fwd(gv.astype(F32))
        lgv = lg_ref[...]
        vn = (nh * lgv + lb_ref[...]).astype(BF)
        dy = dy_ref[...].astype(F32)
        du = dy * _sgu_mix(vn, wm_ref, bsb_ref, ts) * dgu_du
        dmix = dy * gu
        dmix_bf = dmix.astype(BF)
        dm_acc = dm_sc[...]
        rows = []
        for blk in range(ts // SGU_BLOCK):
            r0 = blk * SGU_BLOCK
            dm_acc = dm_acc + dmix[r0:r0 + SGU_BLOCK, :]
            cols = []
            for g in range(SGU_GROUPS):
                c0 = g * SGU_BLOCK
                dmg = dmix_bf[r0:r0 + SGU_BLOCK, c0:c0 + SGU_BLOCK]
                cols.append(_dot_tn(wm_ref[g], dmg))
                dws_ref[g] += _dot_nt(dmg, vn[r0:r0 + SGU_BLOCK, c0:c0 + SGU_BLOCK])
            rows.append(jnp.concatenate(cols, axis=1))
        dm_sc[...] = dm_acc
        dvn = jnp.concatenate(rows, axis=0)
        dlg_ref[...] += _row_sum(dvn * nh)
        dlb_ref[...] += _row_sum(dvn)
        dnh = dvn * lgv
        dgv = rstd * (dnh - jnp.mean(dnh, axis=-1, keepdims=True)
                      - nh * jnp.mean(dnh * nh, axis=-1, keepdims=True))
        duv_ref[:, :D_SGU] = du.astype(BF)
        duv_ref[:, D_SGU:] = (dgv * dgv_dv).astype(BF)

        @pl.when(step == pl.num_programs(0) - 1)
        def _():
            for g in range(SGU_GROUPS):
                dws_ref[g] = dws_ref[g] * mask_ref[...]
                dbs_ref[:, g:g + 1] = jnp.sum(
                    dm_acc[:, g * SGU_BLOCK:(g + 1) * SGU_BLOCK], axis=1, keepdims=True)

    sw = (SGU_GROUPS, SGU_BLOCK, SGU_BLOCK)
    body, more_specs, more = _with_after(body, 7, after)
    return pl.pallas_call(
        body, name="sgu_bwd", grid=(s // ts,),
        in_specs=[_tile_spec(ts, D_SGU), _tile_spec(ts, 2 * D_RNN, 1), _full_spec(sw), _full_spec(sw),
                  _full_spec((SGU_BLOCK, SGU_BLOCK)), _full_spec((1, D_SGU)), _full_spec((1, D_SGU))] + more_specs,
        out_specs=[_tile_spec(ts, 2 * D_SGU), _full_spec(sw), _full_spec((SGU_BLOCK, SGU_GROUPS)),
                   _full_spec((1, D_SGU)), _full_spec((1, D_SGU))],
        out_shape=[jax.ShapeDtypeStruct((s, 2 * D_SGU), BF), jax.ShapeDtypeStruct(sw, F32),
                   jax.ShapeDtypeStruct((SGU_BLOCK, SGU_GROUPS), F32),
                   jax.ShapeDtypeStruct((1, D_SGU), F32), jax.ShapeDtypeStruct((1, D_SGU), F32)],
        scratch_shapes=[pltpu.VMEM((SGU_BLOCK, D_SGU), F32)],
        compiler_params=_params(("arbitrary",)),
    )(dyb_pre, proj, wm, bsb, mask, lg, lb, *more)


_ROW_DBA, _ROW_DBX, _ROW_DSP, _ROW_DCB, _ROW_DCW = 0, 1, 2, 3, 4
_PREV_ROWS = 16


def _rnn_bwd_call(dya_pre, proj, xr_saved, hr, wa, wx, ba, bx, sp, cw, ts, after=None):
    s = proj.shape[0]
    nt = s // ts
    per = ts // _PREV_ROWS

    def tile(i):
        return nt - 1 - i

    def prev(i):
        return jnp.maximum(tile(i) * per - 1, 0)

    def body(dy_ref, xg_ref, xr_ref, hr_ref, hrp_ref, wa_ref, wx_ref, ba_ref, bx_ref, sp_ref,
             cw_ref, dxg_ref, dwa_ref, dwx_ref, vec_ref,
             lam_carry, a_first, dxr_head, al_sc, bl_sc, lam_sc):
        step = pl.program_id(0)

        @pl.when(step == 0)
        def _():
            dwa_ref[...] = jnp.zeros_like(dwa_ref)
            dwx_ref[...] = jnp.zeros_like(dwx_ref)
            vec_ref[...] = jnp.zeros_like(vec_ref)
            lam_carry[...] = jnp.zeros_like(lam_carry)
            a_first[...] = jnp.zeros_like(a_first)
            dxr_head[...] = jnp.zeros_like(dxr_head)

        has_prev = (step < nt - 1).astype(F32)
        x = xg_ref[:, :D_RNN].astype(F32)
        g = xg_ref[:, D_RNN:]
        h_tail =hrp_ref[_PREV_ROWS - SUBLANES:, :].astype(F32) * has_prev
        xr = xr_ref[...].astype(F32)
        r, i, a, nrm, inv_nrm = _lru_gates(xr, wa_ref, wx_ref, ba_ref, bx_ref, sp_ref)
        h = hr_ref[...].astype(F32)
        dy = dy_ref[...].astype(F32)
        gg, dgg = _gelu_and_grad(g)

        coef = _shift_up(a, jnp.broadcast_to(a_first[...], (SUBLANES, D_RNN)), 1)
        lam_carry[...] = _linear_scan(coef, dy * gg, lam_carry[...], al_sc, bl_sc, lam_sc, True)
        a_first[...] = a[0:1, :]
        lam = lam_sc[...]

        da = lam * _shift_down(h, h_tail, 1)
        dnrm = lam * (i * xr)
        di = lam * nrm * xr
        dlog_a = da * a - dnrm * (a * a) * inv_nrm
        spv = sp_ref[...]
        dza = (dlog_a * (-LRU_C * spv)) * (r * (1.0 - r))
        dzx = di * (i * (1.0 - i))
        vec_ref[_ROW_DSP:_ROW_DSP + 1, :] += _row_sum(dlog_a * (-LRU_C * r))
        vec_ref[_ROW_DBA:_ROW_DBA + 1, :] += _row_sum(dza)
        vec_ref[_ROW_DBX:_ROW_DBX + 1, :] += _row_sum(dzx)
        xb = xr.astype(BF)
        dza_bf = dza.astype(BF)
        dzx_bf = dzx.astype(BF)
        for grp in range(N_LRU_GROUPS):
            cols = slice(grp * LRU_GROUP, (grp + 1) * LRU_GROUP)
            dwa_ref[grp] += _dot_tn(xb[:, cols], dza_bf[:, cols])
            dwx_ref[grp] += _dot_tn(xb[:, cols], dzx_bf[:, cols])
        dxr = (lam * nrm * i + _group_dot(dza_bf, wa_ref, _dot_nt) + _group_dot(dzx_bf, wx_ref, _dot_nt))

        vec_ref[_ROW_DCB:_ROW_DCB + 1, :] += _row_sum(dxr)
        head = dxr_head[...]
        dx = cw_ref[CONV_WIDTH - 1:CONV_WIDTH, :] * dxr
        vec_ref[_ROW_DCW + 3:_ROW_DCW + 4, :] += _row_sum(dxr * x)
        for sft in range(1, CONV_WIDTH):
            k = CONV_WIDTH - 1 - sft
            ahead = _shift_up(dxr, head, sft)
            dx = dx + cw_ref[k:k + 1, :] * ahead
            vec_ref[_ROW_DCW + k:_ROW_DCW + k + 1, :] += _row_sum(ahead * x)
        dxr_head[...] = dxr[0:SUBLANES, :]
        dxg_ref[:, :D_RNN] = dx.astype(BF)
        dxg_ref[:, D_RNN:] = (dy * h * dgg).astype(BF)

    gw = (N_LRU_GROUPS, LRU_GROUP, LRU_GROUP)
    rev = lambda width: pl.BlockSpec((ts, width), lambda i: (tile(i), 0))
    body, more_specs, more = _with_after(body, 11, after)
    return pl.pallas_call(
        body, name="rnn_bwd", grid=(nt,),
        in_specs=[rev(D_RNN), rev(2 * D_RNN), rev(D_RNN), rev(D_RNN),
                  pl.BlockSpec((_PREV_ROWS, D_RNN), lambda i: (prev(i), 0)),
                  _full_spec(gw), _full_spec(gw),
                  _full_spec((1, D_RNN)), _full_spec((1, D_RNN)), _full_spec((1, D_RNN)),
                  _full_spec((CONV_WIDTH, D_RNN))] + more_specs,
        out_specs=[rev(2 * D_RNN), _full_spec(gw), _full_spec(gw), _full_spec((SUBLANES, D_RNN))],
        out_shape=[jax.ShapeDtypeStruct((s, 2 * D_RNN), BF), jax.ShapeDtypeStruct(gw, F32),
                   jax.ShapeDtypeStruct(gw, F32), jax.ShapeDtypeStruct((SUBLANES, D_RNN), F32)],
        scratch_shapes=[pltpu.VMEM((1, D_RNN), F32), pltpu.VMEM((1, D_RNN), F32),
                        pltpu.VMEM((SUBLANES, D_RNN), F32),
                        pltpu.VMEM((ts, D_RNN), F32), pltpu.VMEM((ts, D_RNN), F32),
                        pltpu.VMEM((ts, D_RNN), F32)],
        compiler_params=_params(("arbitrary",)),
    )(dya_pre, proj, xr_saved, hr, hr, wa, wx, ba, bx, sp, cw, *more)


def _inproj_bwd_call(dxg, duv, dgate, dx1, x, g1, w_in, layer, ts):
    s = x.shape[0]

    def body(dxg_ref, duv_ref, dgt_ref, dx1_ref, x_ref, g_ref, w_ref, dx_ref, dproj_ref, dg_ref):
        @pl.when(pl.program_id(0) == 0)
        def _():
            dg_ref[...] = jnp.zeros_like(dg_ref)

        dproj = jnp.concatenate([dxg_ref[...], duv_ref[...], dgt_ref[...]], axis=1)
        dproj_ref[...] = dproj
        dh = jnp.zeros((ts, D_MODEL), F32)
        for q in range(N_QUARTERS):
            dh = dh + _dot_nt(dproj[:, q * Q_IN:(q + 1) * Q_IN], w_ref[q])
        dx, dg = _rms_bwd(dh, x_ref[...], g_ref[...])
        dx_ref[...] = dx1_ref[...] + dx
        dg_ref[...] += _row_sum(dg)

    return pl.pallas_call(
        body, name="inproj_bwd", grid=(s // ts,),
        in_specs=[_tile_spec(ts, 2 * D_RNN), _tile_spec(ts, 2 * D_SGU), _tile_spec(ts, 2 * D_MODEL),
                  _tile_spec(ts, D_MODEL), _tile_spec(ts, D_MODEL), _full_spec((1, D_MODEL)),
                  pl.BlockSpec((None, N_QUARTERS, D_MODEL, Q_IN), lambda i: (layer, 0, 0, 0))],
        out_specs=[_tile_spec(ts, D_MODEL), _tile_spec(ts, D_IN), _full_spec((1, D_MODEL))],
        out_shape=[jax.ShapeDtypeStruct((s, D_MODEL), F32), jax.ShapeDtypeStruct((s, D_IN), BF),
                   jax.ShapeDtypeStruct((1, D_MODEL), F32)],
        compiler_params=_params(("arbitrary",)),
    )(dxg, duv, dgate, dx1, x, g1, w_in)


def _relu_sq(p):
    return jnp.square(jnp.maximum(p, 0))


def _wgrad_call(a, b, core, tm, tn, tk, col_blocked, name, a_fn=None):
    s, m = a.shape
    n = b.shape[1]
    r, cols = (m, n // N_QUARTERS) if col_blocked else (m // N_QUARTERS, n)
    r2 = r // 2
    per_tile = tm // r
    steps = s // tk
    assert per_tile > 0 or steps == 1

    def body(core_ref, a_ref, b_ref, keep_ref, send_ref, *acc):
        av = a_ref[...]
        if a_fn is not None:
            av = a_fn(av)
        prod = _dot_tn(av.astype(BF), b_ref[...].astype(BF))

        def emit(total):
            for h in range(2):
                @pl.when(core_ref[0] == h)
                def _():
                    for q in range(per_tile):
                        keep_ref[q] = total[q * r + h * r2:q * r + (h + 1) * r2].astype(BF)
                        send_ref[q] = total[q * r + (1 - h) * r2:q * r + (2 - h) * r2].astype(BF)

        if per_tile == 0:
            mine = pl.program_id(1) == core_ref[0]

            @pl.when(mine)
            def _():
                keep_ref[0] = prod.astype(BF)

            @pl.when(jnp.logical_not(mine))
            def _():
                send_ref[0] = prod.astype(BF)
        elif steps == 1:
            emit(prod)
        else:
            acc_ref, = acc
            step = pl.program_id(2)

            @pl.when(step == 0)
            def _():
                acc_ref[...] = prod

            @pl.when(jnp.logical_and(step > 0, step < steps - 1))
            def _():
                acc_ref[...] += prod

            @pl.when(step == steps - 1)
            def _():
                emit(acc_ref[...] + prod)

    if col_blocked:
        per_q = cols // tn
        out_spec = pl.BlockSpec((1, r2, tn), lambda j, i, k, c: (j // per_q, 0, j % per_q))
    else:
        out_spec = pl.BlockSpec((per_tile, r2, tn), lambda j, i, k, c: (i, 0, j))
    return pl.pallas_call(
        body, name=name,
        out_shape=[jax.ShapeDtypeStruct((N_QUARTERS, r2, cols), BF)] * 2,
        grid_spec=pltpu.PrefetchScalarGridSpec(
            num_scalar_prefetch=1, grid=(n // tn, m // tm, steps),
            in_specs=[pl.BlockSpec((tk, tm), lambda j, i, k, c: (k, i)),
                      pl.BlockSpec((tk, tn), lambda j, i, k, c: (k, j))],
            out_specs=[out_spec, out_spec],
            scratch_shapes=[] if steps == 1 else [pltpu.VMEM((tm, tn), F32)]),
        compiler_params=_params(("parallel", "parallel", "arbitrary")),
    )(core, a, b)


BIG = ("w_in", "w_up", "w_down", "w_branch_a", "w_branch_b", "w_out")


def _block_diag(w):
    w4 = w.reshape(N_LRU_GROUPS, HEADS_PER_GROUP, RNN_HEAD_DIM, RNN_HEAD_DIM)
    eye = jnp.eye(HEADS_PER_GROUP, dtype=w.dtype)
    return jnp.einsum("gjio,jk->gjiko", w4, eye).reshape(N_LRU_GROUPS, LRU_GROUP, LRU_GROUP)


def _block_diag_extract(d):
    d5 = d.reshape(N_LRU_GROUPS, HEADS_PER_GROUP, RNN_HEAD_DIM, HEADS_PER_GROUP, RNN_HEAD_DIM)
    blocks = [d5[:, j, :, j, :] for j in range(HEADS_PER_GROUP)]
    return jnp.stack(blocks, axis=1).reshape(RNN_HEADS, RNN_HEAD_DIM, RNN_HEAD_DIM)


def _sgu_mask():
    chunk = jnp.arange(SGU_BLOCK) // CHUNK
    return (chunk[:, None] >= chunk[None, :]).astype(F32)


def _layer_small(sm, l, core):
    row = lambda v: v.reshape(1, -1)
    return dict(
        core=core,
        g1=row(sm["norm_mix_g"][l]), g2=row(sm["norm_ffn_g"][l]),
        wa=_block_diag(sm["lru_w_a"][l]).astype(BF), wx=_block_diag(sm["lru_w_x"][l]).astype(BF),
        ba=row(sm["lru_b_a"][l]), bx=row(sm["lru_b_x"][l]),
        sp=row(jax.nn.softplus(-sm["lru_lambda"][l])),
        cw=sm["conv_w"][l] if "conv_w" in sm else None, cb=row(sm["conv_b"][l]),
        wm=(sm["sgu_w_s"][l] * _sgu_mask()).astype(BF),
        bsb=jnp.broadcast_to(sm["sgu_b_s"][l][:, :, None], (SGU_GROUPS, SGU_BLOCK, SGU_BLOCK)),
        lg=row(sm["sgu_ln_g"][l]), lb=row(sm["sgu_ln_b"][l]),
    )


def _layer_fwd_mix(x, big, p, ts, h=None, before_sgu=None, proj=None):
    if h is None:
        h = _norm_call(x, p["g1"], ts)
    if proj is None:
        proj = _inproj_call(h, big["w_in"], 0, 2 * ts)
    xr, hr, ya_pre = _rnn_fwd_call(proj, p["wa"], p["wx"], p["ba"], p["bx"], p["sp"], p["cw"], p["cb"], ts)
    yb_pre = _sgu_fwd_call(proj, p["wm"], p["bsb"], p["lg"], p["lb"], ts,
                           None if before_sgu is None else before_sgu(ya_pre))
    return dict(p=p, x=x, h=h, proj=proj, xr=xr, hr=hr, ya_pre=ya_pre, yb_pre=yb_pre)


def _layer_fwd_out(sv, big, ts):
    x1, ya, yb, merged, h2 = _merge_call(sv["x"], sv["proj"], sv["ya_pre"], sv["yb_pre"], big["w_branch_a"],
                                         big["w_branch_b"], big["w_out"], sv["p"]["g2"], 0, ts)
    x2, pre = _ffn_call(x1, h2, big["w_up"], big["w_down"], 0, ts)
    sv.update(x1=x1, ya=ya, yb=yb, merged=merged, h2=h2, pre=pre)
    return x2


def _layer_bwd_ffn(dx, sv, big, ts, after=None):
    p = sv["p"]
    dx1, dpre, dg2, dx_bf, sv["dx1_bf"] = _ffn_bwd_call(dx, sv["pre"], sv["x1"], p["g2"], big["w_up"],
                                                       big["w_down"], 0, ts, after)
    tk = dx.shape[0]
    gb = dict(
        w_down=_wgrad_call(sv["pre"], dx_bf, p["core"], Q_FF, D_MODEL, tk, False, "wgrad_down", a_fn=_relu_sq),
        w_up=_wgrad_call(sv["h2"], dpre, p["core"], D_MODEL, Q_FF, tk, True, "wgrad_up"))
    return dx1, gb, dict(norm_ffn_g=dg2[0])


def _layer_bwd_merge(dx1, sv, big, ts, after=None):
    tk = dx1.shape[0]
    core = sv["p"]["core"]
    dya, dyb, dgate, dya_pre, dyb_pre = _merge_bwd_call(
        dx1, sv["proj"], sv["ya"], sv["yb"], big["w_branch_a"], big["w_branch_b"], big["w_out"], 0, ts, after)
    gb = dict(
        w_out=_wgrad_call(sv["merged"], sv["dx1_bf"], core, D_MODEL, D_MODEL, tk, False, "wgrad_out"),
        w_branch_a=_wgrad_call(sv["ya_pre"], dya, core, D_RNN, D_MODEL // 2, tk, False, "wgrad_branch_a"),
        w_branch_b=_wgrad_call(sv["yb_pre"], dyb, core, D_SGU, D_MODEL, tk, False, "wgrad_branch_b"))
    return (dgate, dya_pre, dyb_pre), gb


def _layer_bwd_branches(dx1, merge_out, sv, big, lam, ts, after=None, after_sgu=None):
    p = sv["p"]
    tk = dx1.shape[0]
    dgate, dya_pre, dyb_pre = merge_out
    gb = {}
    duv, dws, dbs, dlg, dlb = _sgu_bwd_call(dyb_pre, sv["proj"], p["wm"], p["bsb"], _sgu_mask(), p["lg"], p["lb"],
                                            ts, after)
    dxg, dwa, dwx, vec = _rnn_bwd_call(dya_pre, sv["proj"], sv["xr"], sv["hr"], p["wa"], p["wx"], p["ba"], p["bx"],
                                       p["sp"], p["cw"], ts, None if after_sgu is None else after_sgu(duv))
    dx, dproj, dg1 = _inproj_bwd_call(dxg, duv, dgate, dx1, sv["x"], p["g1"], big["w_in"], 0, ts)
    gb["w_in"] = _wgrad_call(sv["h"], dproj, p["core"], D_MODEL // 2, Q_IN, tk, True, "wgrad_in")
    gs = dict(
        norm_mix_g=dg1[0], conv_w=vec[_ROW_DCW:_ROW_DCW + CONV_WIDTH], conv_b=vec[_ROW_DCB],
        lru_w_a=_block_diag_extract(dwa), lru_w_x=_block_diag_extract(dwx),
        lru_b_a=vec[_ROW_DBA].reshape(RNN_HEADS, RNN_HEAD_DIM), lru_b_x=vec[_ROW_DBX].reshape(RNN_HEADS, RNN_HEAD_DIM),
        lru_lambda=-vec[_ROW_DSP] * jax.nn.sigmoid(-lam),
        sgu_ln_g=dlg[0], sgu_ln_b=dlb[0], sgu_w_s=dws, sgu_b_s=dbs.T)
    return dx, gb, gs


def _local_step(x, target, big, sm, ts):
    saved = []
    core = jnp.zeros((1,), jnp.int32)
    for l in range(DEPTH):
        sv = _layer_fwd_mix(x, big[l], _layer_small(sm, l, core), ts)
        x = _layer_fwd_out(sv, big[l], ts)
        saved.append(sv)
    dx, loss, dgf = _loss_call(x, target, sm["final_norm_g"].reshape(1, -1), ts)
    gb, gs = [None] * DEPTH, [None] * DEPTH
    for l in reversed(range(DEPTH)):
        dx1, gb_ffn, gs_ffn = _layer_bwd_ffn(dx, saved[l], big[l], ts)
        merge_out, gb_merge = _layer_bwd_merge(dx1, saved[l], big[l], ts)
        dx, gb_mix, gs_mix = _layer_bwd_branches(dx1, merge_out, saved[l], big[l], sm["lru_lambda"][l], ts)
        gb[l] = {**gb_ffn, **gb_merge, **gb_mix}
        gs[l] = {**gs_ffn, **gs_mix}
    gs = {k: jnp.stack([g[k] for g in gs]) for k in gs[0]}
    gs["final_norm_g"] = dgf[0]
    return loss, dx, gb, gs


EW_VMEM_BYTES = 24 * 1024 * 1024


def _row_block(rows, cols, bytes_per_elem):
    for br in range(min(rows, EW_VMEM_BYTES // (2 * bytes_per_elem * cols)), 0, -1):
        if rows % br == 0 and br % 16 == 0:
            return br
    return rows


def _ew_call(fn, name, operands, outputs, slabs=1, sel=None, into=None, after=None):
    if into is not None and not isinstance(into, (list, tuple)):
        into = [into]
    rows, cols = outputs[0][0].shape[2:]
    br = _row_block(rows, cols, sum(jnp.dtype(a.dtype).itemsize for a, _ in operands + outputs))
    n_in = len(operands)

    def pick(tok, g, s):
        if callable(tok):
            return tok(g, s)
        if tok == "g":
            return g
        if isinstance(tok, tuple):
            return s[tok[1]]
        return tok

    def spec(idx):
        return pl.BlockSpec((None, None, br, cols),
                            lambda g, i, s, idx=idx: (pick(idx[0], g, s), pick(idx[1], g, s), i, 0))

    if sel is None:
        sel = jnp.zeros((1,), jnp.int32)
    in_specs = [spec(idx) for _, idx in operands]
    arrays = [a for a, _ in operands]
    aliases = {}
    for j, buf in enumerate(into or ()):
        in_specs.append(pl.BlockSpec(memory_space=pl.ANY))
        arrays.append(buf)
        aliases[1 + n_in + j] = j
    if after is not None:
        in_specs.append(pl.BlockSpec(memory_space=pl.ANY))
        arrays.append(after)

    def body(sel_ref, *refs):
        outs = fn(*[r[...] for r in refs[:n_in]])
        for o_ref, o in zip(refs[len(arrays):], outs):
            o_ref[...] = o.astype(o_ref.dtype)

    return pl.pallas_call(
        body, name=name, out_shape=[s for s, _ in outputs],
        grid_spec=pltpu.PrefetchScalarGridSpec(
            num_scalar_prefetch=1, grid=(slabs, rows // br),
            in_specs=in_specs,
            out_specs=[spec(idx) for _, idx in outputs]),
        input_output_aliases=aliases,
        compiler_params=_params(("parallel", "parallel")),
    )(sel, *arrays)


def _ew_jobs_call(name, jobs, sel):
    arrays, out_shapes, plans = [], [], []
    for fn, operands, outputs, slabs, _ in jobs:
        rows, cols = outputs[0][0].shape[2:]
        br = _row_block(rows, cols, sum(jnp.dtype(a.dtype).itemsize for a, _ in operands + outputs))
        plans.append((fn, len(arrays), [idx for _, idx in operands], len(out_shapes), [idx for _, idx in outputs],
                      (slabs, rows // br), (None, None, br, cols)))
        arrays += [a for a, _ in operands]
        out_shapes += [s for s, _ in outputs]
    n_in = len(arrays)
    aliases = {}
    for job, plan in zip(jobs, plans):
        for j, buf in enumerate(job[4] or ()):
            aliases[1 + len(arrays)] = plan[3] + j
            arrays.append(buf)

    def pick(tok, g, s):
        if callable(tok):
            return tok(g, s)
        if tok == "g":
            return g
        if isinstance(tok, tuple):
            return s[tok[1]]
        return tok

    def body(sel_ref, *refs):
        ins, outs = refs[:n_in], refs[len(arrays):]
        for fn, i0, in_idx, o0, out_idx, grid, block in plans:
            def spec(idx, block=block):
                return pl.BlockSpec(block, lambda g, i, idx=idx: (pick(idx[0], g, sel_ref), pick(idx[1], g, sel_ref), i, 0))

            def step(*tiles, fn=fn, k=len(in_idx)):
                for o_ref, o in zip(tiles[k:], fn(*[t[...] for t in tiles[:k]])):
                    o_ref[...] = o.astype(o_ref.dtype)

            pltpu.emit_pipeline(step, grid=grid, in_specs=[spec(idx) for idx in in_idx],
                                out_specs=[spec(idx) for idx in out_idx])(
                *ins[i0:i0 + len(in_idx)], *outs[o0:o0 + len(out_idx)])

    return pl.pallas_call(
        body, name=name, out_shape=out_shapes,
        in_specs=[pl.BlockSpec(memory_space=pltpu.SMEM)] + [pl.BlockSpec(memory_space=pl.ANY)] * len(arrays),
        out_specs=[pl.BlockSpec(memory_space=pl.ANY)] * len(out_shapes),
        input_output_aliases=aliases, compiler_params=_params(),
    )(sel, *arrays)


def _as4(a):
    return a.reshape((1,) * (4 - a.ndim) + a.shape)


def _adamw(w, g, m, v):
    m = ADAM_B1 * m + (1.0 - ADAM_B1) * g
    v = ADAM_B2 * v + (1.0 - ADAM_B2) * jnp.square(g)
    m_hat = m / (1.0 - ADAM_B1 ** ADAM_STEP)
    v_hat = v / (1.0 - ADAM_B2 ** ADAM_STEP)
    delta = -ADAM_LR * (m_hat / (jnp.sqrt(v_hat) + ADAM_EPS) + ADAM_WD * w)
    return delta, m, v


def _small_adamw_call(ws, gs, ms, vs):
    n = len(ws)

    def body(*refs):
        for k in range(n):
            w, g, m, v = (refs[j * n + k][...] for j in range(4))
            outs = _adamw(w, g, m, v)
            for j in range(3):
                refs[(4 + j) * n + k][...] = outs[j]

    shapes = [jax.ShapeDtypeStruct(w.shape, F32) for w in ws]
    outs = pl.pallas_call(
        body, name="adamw_small", out_shape=shapes * 3,
        in_specs=[pl.BlockSpec(memory_space=pltpu.VMEM)] * (4 * n),
        out_specs=[pl.BlockSpec(memory_space=pltpu.VMEM)] * (3 * n),
        compiler_params=_params(),
    )(*ws, *gs, *ms, *vs)
    return outs[:n], outs[n:2 * n], outs[2 * n:]


ANY = pl.BlockSpec(memory_space=pl.ANY)


def _place():
    x, y, c = lax.axis_index("x"), lax.axis_index("y"), lax.axis_index("c")
    chips = [(1 - x, y), (x, 1 - y), (1 - x, 1 - y)]
    return x, y, c, chips


def _remote(src, dst, send_sem, recv_sem, to):
    return pltpu.make_async_remote_copy(src_ref=src, dst_ref=dst, send_sem=send_sem, recv_sem=recv_sem,
                                        device_id=to, device_id_type=MESH)


def _sibling_send_call(items):
    n = len(items)

    def body(*refs):
        src, out = refs[:n], refs[n:2 * n]
        send_sems, recv_sems = refs[2 * n:]
        x, y, c, _ = _place()
        copies = [_remote(src[w], out[w], send_sems.at[w], recv_sems.at[w], (x, y, 1 - c)) for w in range(n)]
        for cp in copies:
            cp.start()
        for cp in copies:
            cp.wait()

    return pl.pallas_call(
        body, name="grads_to_sibling",
        out_shape=[jax.ShapeDtypeStruct(a.shape, a.dtype) for a in items],
        in_specs=[ANY] * n, out_specs=[ANY] * n,
        scratch_shapes=[pltpu.SemaphoreType.DMA((n,)), pltpu.SemaphoreType.DMA((n,))],
        compiler_params=_params(vmem=False, has_side_effects=True),
    )(*items)


def _sibling_inplace_call(name, bufs, slabs, n_pairs):
    n = len(bufs)

    def body(*refs):
        out = refs[n:2 * n]
        send_sems, recv_sems = refs[2 * n:]
        x, y, c, _ = _place()
        sibling = (x, y, 1 - c)
        pairs = [pair for w, ref in enumerate(out) for pair in slabs(ref, c, w)]
        sends = [_remote(s, s, send_sems.at[k], recv_sems.at[k], sibling) for k, (s, _) in enumerate(pairs)]
        for cp in sends:
            cp.start()
        for k, (_, r) in enumerate(pairs):
            _remote(r, r, send_sems.at[k], recv_sems.at[k], sibling).wait_recv()
        for cp in sends:
            cp.wait_send()

    return pl.pallas_call(
        body, name=name,
        out_shape=[jax.ShapeDtypeStruct(a.shape, a.dtype) for a in bufs],
        in_specs=[ANY] * n, out_specs=[ANY] * n,
        input_output_aliases={w: w for w in range(n)},
        scratch_shapes=[pltpu.SemaphoreType.DMA((n_pairs,)), pltpu.SemaphoreType.DMA((n_pairs,))],
        compiler_params=_params(vmem=False, has_side_effects=True),
    )(*bufs)


HBM_SPEC = pl.BlockSpec(memory_space=pltpu.HBM)
SEM_SPEC = pl.BlockSpec(memory_space=pltpu.SEMAPHORE)
DATAFLOW_EFFECT = pltpu.SideEffectType.DATAFLOW_SIDE_EFFECTING


def _exchange_start(name, bufs, copies, n_copies, after):
    return _exchange_start_many(name, [(bufs, copies, n_copies)], after)[0]


def _exchange_start_many(name, groups, after):
    sizes = [len(bufs) for bufs, _, _ in groups]
    starts = [sum(sizes[:g]) for g in range(len(groups))]
    n, n_sems = sum(sizes), 2 * len(groups)

    def body(*refs):
        ins, sems, token = refs[:n], refs[n + 1:n + 1 + n_sems], refs[-1]
        for g, (_, copies, _) in enumerate(groups):
            send_sems, recv_sems = sems[2 * g], sems[2 * g + 1]
            for k, (src, dst, to) in enumerate(copies(ins[starts[g]:starts[g] + sizes[g]])):
                _remote(src, dst, send_sems.at[k], recv_sems.at[k], to).start()
        token[...] = jnp.zeros_like(token)

    every = [b for bufs, _, _ in groups for b in bufs]
    outs = pl.pallas_call(
        body, name=name,
        out_shape=(*[pltpu.SemaphoreType.DMA((c,)) for _, _, c in groups for _ in range(2)],
                   *[pltpu.HBM(b.shape, b.dtype) for b in every], jax.ShapeDtypeStruct((SUBLANES, 128), F32)),
        in_specs=[HBM_SPEC] * n + [ANY],
        out_specs=(*[SEM_SPEC] * n_sems, *[HBM_SPEC] * n, pl.BlockSpec(memory_space=pltpu.VMEM)),
        input_output_aliases={w: w + n_sems for w in range(n)},
        compiler_params=pltpu.CompilerParams(has_side_effects=DATAFLOW_EFFECT),
    )(*[pltpu.with_memory_space_constraint(b, pltpu.HBM) for b in every], after)
    thru = outs[n_sems:n_sems + n]
    return [(outs[2 * g], outs[2 * g + 1], list(thru[starts[g]:starts[g] + sizes[g]]), outs[-1])
            for g in range(len(groups))]


def _exchange_wait(name, send_sems, recv_sems, bufs, copies, after):
    n = len(bufs)

    def body(*refs):
        ins, send_sems, recv_sems = refs[:n], refs[n], refs[n + 1]
        for k, (src, dst, to) in enumerate(copies(ins)):
            cp = _remote(src, dst, send_sems.at[k], recv_sems.at[k], to)
            cp.wait_send()
            cp.wait_recv()

    return pl.pallas_call(
        body, name=name,
        out_shape=[pltpu.HBM(b.shape, b.dtype) for b in bufs],
        in_specs=[HBM_SPEC] * n + [SEM_SPEC, SEM_SPEC, ANY],
        out_specs=[HBM_SPEC] * n,
        input_output_aliases={w: w for w in range(n)},
        compiler_params=pltpu.CompilerParams(has_side_effects=DATAFLOW_EFFECT),
    )(*bufs, send_sems, recv_sems, after)


def _gather_copies(refs):
    x, y, c, chips = _place()
    mine = 2 * (2 * x + y) + c
    return [(ref.at[mine], ref.at[mine], (qx, qy, c)) for ref in refs for qx, qy in chips]


def _forward_copies(refs):
    x, y, c, chips = _place()
    return [(ref.at[2 * (2 * qx + qy) + c], ref.at[2 * (2 * qx + qy) + c], (x, y, 1 - c))
            for ref in refs for qx, qy in chips]


def _gather_forward_slabs(ref, c, w):
    x, y, _, chips = _place()
    return [(ref.at[2 * (2 * qx + qy) + c], ref.at[2 * (2 * qx + qy) + 1 - c]) for qx, qy in chips]


def _device_peers():
    x, y, c, _ = _place()
    return 4 * x + 2 * y + c, [(k, (x ^ ((k >> 2) & 1), y ^ ((k >> 1) & 1), c ^ (k & 1))) for k in range(1, 8)]


def _small_scatter_copies(refs):
    me, peers = _device_peers()
    return [(refs[0].at[me ^ k], refs[1].at[me], to) for k, to in peers]


def _small_spread_copies(refs):
    me, peers = _device_peers()
    return [(refs[0].at[me], refs[0].at[me], to) for _, to in peers]


def _sibling_copies(refs):
    n = len(refs) // 2
    x, y, c, _ = _place()
    return [(refs[w], refs[n + w], (x, y, 1 - c)) for w in range(n)]


def _owner_copies(refs):
    n = len(refs) // 2
    x, y, c, chips = _place()
    return [(refs[w].at[2 * qx + qy], refs[n + w].at[j], (qx, qy, c))
            for w in range(n) for j, (qx, qy) in enumerate(chips)]


N_DEVICES = 8
SMALL_ROWS = 616


SMALL = ("norm_mix_g", "conv_w", "conv_b", "lru_w_a", "lru_b_a", "lru_w_x", "lru_b_x", "lru_lambda",
         "sgu_ln_g", "sgu_ln_b", "sgu_w_s", "sgu_b_s", "norm_ffn_g", "final_norm_g")
WEIGHTS = ("norm_mix_g", "w_in", "conv_w", "conv_b", "lru_w_a", "lru_b_a", "lru_w_x", "lru_b_x", "lru_lambda",
           "sgu_ln_g", "sgu_ln_b", "sgu_w_s", "sgu_b_s", "w_branch_a", "w_branch_b", "w_out", "norm_ffn_g",
           "w_up", "w_down", "final_norm_g")
PACK_ALIGN = SUBLANES * 128


PACKED = SMALL + ("loss",)


def _pack_small(gs):
    parts = []
    for k in PACKED:
        flat = gs[k].reshape(-1)
        parts.append(jnp.pad(flat, (0, -flat.size % PACK_ALIGN)))
    flat = jnp.concatenate(parts)
    flat = jnp.pad(flat, (0, N_DEVICES * SMALL_ROWS * 128 - flat.size))
    return flat.reshape(N_DEVICES, SMALL_ROWS, 128)


def _unpack_small(buf, like):
    flat = buf.reshape(-1)
    out, off = {}, 0
    for k in PACKED:
        size = like[k].size
        out[k] = flat[off:off + size].reshape(like[k].shape)
        off += size + (-size % PACK_ALIGN)
    return out


def _as_rows(a):
    return a.reshape(-1, a.shape[-1])


def kernel(x, norm_mix_g, w_in, conv_w, conv_b, lru_w_a, lru_b_a, lru_w_x, lru_b_x, lru_lambda, sgu_ln_g, sgu_ln_b, sgu_w_s, sgu_b_s, w_branch_a, w_branch_b, w_out, norm_ffn_g, w_up, w_down, final_norm_g, loss_target, m_norm_mix_g, m_w_in, m_conv_w, m_conv_b, m_lru_w_a, m_lru_b_a, m_lru_w_x, m_lru_b_x, m_lru_lambda, m_sgu_ln_g, m_sgu_ln_b, m_sgu_w_s, m_sgu_b_s, m_w_branch_a, m_w_branch_b, m_w_out, m_norm_ffn_g, m_w_up, m_w_down, m_final_norm_g, v_norm_mix_g, v_w_in, v_conv_w, v_conv_b, v_lru_w_a, v_lru_b_a, v_lru_w_x, v_lru_b_x, v_lru_lambda, v_sgu_ln_g, v_sgu_ln_b, v_sgu_w_s, v_sgu_b_s, v_w_branch_a, v_w_branch_b, v_w_out, v_norm_ffn_g, v_w_up, v_w_down, v_final_norm_g):
    w = dict(norm_mix_g=norm_mix_g, w_in=w_in, conv_w=conv_w, conv_b=conv_b, lru_w_a=lru_w_a, lru_b_a=lru_b_a,
             lru_w_x=lru_w_x, lru_b_x=lru_b_x, lru_lambda=lru_lambda, sgu_ln_g=sgu_ln_g, sgu_ln_b=sgu_ln_b,
             sgu_w_s=sgu_w_s, sgu_b_s=sgu_b_s, w_branch_a=w_branch_a, w_branch_b=w_branch_b, w_out=w_out,
             norm_ffn_g=norm_ffn_g, w_up=w_up, w_down=w_down, final_norm_g=final_norm_g)
    m = dict(norm_mix_g=m_norm_mix_g, w_in=m_w_in, conv_w=m_conv_w, conv_b=m_conv_b, lru_w_a=m_lru_w_a,
             lru_b_a=m_lru_b_a, lru_w_x=m_lru_w_x, lru_b_x=m_lru_b_x, lru_lambda=m_lru_lambda,
             sgu_ln_g=m_sgu_ln_g, sgu_ln_b=m_sgu_ln_b, sgu_w_s=m_sgu_w_s, sgu_b_s=m_sgu_b_s,
             w_branch_a=m_w_branch_a, w_branch_b=m_w_branch_b, w_out=m_w_out, norm_ffn_g=m_norm_ffn_g,
             w_up=m_w_up, w_down=m_w_down, final_norm_g=m_final_norm_g)
    v = dict(norm_mix_g=v_norm_mix_g, w_in=v_w_in, conv_w=v_conv_w, conv_b=v_conv_b, lru_w_a=v_lru_w_a,
             lru_b_a=v_lru_b_a, lru_w_x=v_lru_w_x, lru_b_x=v_lru_b_x, lru_lambda=v_lru_lambda,
             sgu_ln_g=v_sgu_ln_g, sgu_ln_b=v_sgu_ln_b, sgu_w_s=v_sgu_w_s, sgu_b_s=v_sgu_b_s,
             w_branch_a=v_w_branch_a, w_branch_b=v_w_branch_b, w_out=v_w_out, norm_ffn_g=v_norm_ffn_g,
             w_up=v_w_up, w_down=v_w_down, final_norm_g=v_final_norm_g)
    core = lax.axis_index("c")
    chip = 2 * lax.axis_index("x") + lax.axis_index("y")
    sel = jnp.stack([core, 1 - core, chip, 2 * chip + core]).astype(jnp.int32)
    this_core, this_chip = ("sel", 0), ("sel", 2)
    sds = jax.ShapeDtypeStruct

    ts = TOKEN_TILE

    def after_all(arrays):
        return jnp.stack([a[(0,) * a.ndim].astype(F32) for a in arrays])

    halves = {k: (w[k].shape[1] // 2, w[k].shape[2]) for k in BIG}

    def same_shape(keys):
        groups = {}
        for k in keys:
            groups.setdefault(halves[k], []).append(k)
        return list(groups.values())

    def half_view(k, a):
        return a.reshape((2 * N_QUARTERS,) + halves[k])

    def full_view(k, a):
        if k == "conv_w":
            return a.reshape(N_QUARTERS, DEPTH, CONV_WIDTH, -1).transpose(1, 2, 0, 3).reshape(DEPTH, CONV_WIDTH, D_RNN)
        r2, cols = halves[k]
        if k in ("w_in", "w_up"):
            return a.reshape(1, N_QUARTERS, 2 * r2, cols)
        return a.reshape(1, 2 * N_QUARTERS * r2, cols)

    layer_bufs = [{}, {}]

    def cast_weights(keys, after):
        for ks in same_shape(keys):
            outs = _ew_call(
                lambda *t: t, "cast_weights",
                [(w[k].reshape((DEPTH, 1) + w[k].shape[1:]), (l, 0)) for k in ks for l in range(DEPTH)],
                [(sds((1, N_QUARTERS) + w[k].shape[1:], BF), (0, this_chip)) for k in ks for l in range(DEPTH)],
                1, sel, after=after)
            for i, k in enumerate(ks):
                for l in range(DEPTH):
                    layer_bufs[l][k] = half_view(k, outs[DEPTH * i + l])

    conv_buf = lax.dynamic_update_slice_in_dim(
        jnp.zeros((N_QUARTERS, DEPTH) + conv_w.shape[1:], F32), conv_w[None], chip, axis=0)
    layer_bufs[0]["conv_w"] = conv_buf.reshape((2 * N_QUARTERS,) + conv_w.shape[1:])
    sm = {k: w[k] for k in SMALL if k != "conv_w"}

    def gather_start(tag, l, keys, after):
        bufs = [layer_bufs[l][k] for k in keys]
        return _exchange_start(f"gather_start_{tag}", bufs, _gather_copies, 3 * len(keys), after)

    def gather_finish(tag, keys, started, after):
        send_sems, recv_sems, thru, _ = started
        landed = _exchange_wait(f"gather_wait_{tag}", send_sems, recv_sems, thru, _gather_copies, after)
        landed = _sibling_inplace_call("gather_forward", landed, _gather_forward_slabs, 3 * len(keys))
        return {k: full_view(k, a) for k, a in zip(keys, landed)}

    first, rest = ("w_in",), tuple(k for k in BIG if k != "w_in")
    cast_weights(first, None)
    started_a = gather_start("0a", 0, first + ("conv_w",), sel)
    cast_weights(rest, started_a[3])
    started_b, started_c, started_d = _exchange_start_many(
        "gather_start_rest",
        [([layer_bufs[l][k] for k in keys], _gather_copies, 3 * len(keys)) for l, keys in ((0, rest), (1, first), (1, rest))],
        started_a[3])

    def arrives(tag, keys, started):
        state = {}

        def hook(after):
            landed = _exchange_wait(f"gather_wait_{tag}", started[0], started[1], started[2], _gather_copies, after)
            state["forward"] = _exchange_start(f"forward_start_{tag}", landed, _forward_copies, 3 * len(keys), after)
            return state["forward"][3]

        def finish(after):
            send_sems, recv_sems, thru, _ = state["forward"]
            done = _exchange_wait(f"forward_wait_{tag}", send_sems, recv_sems, thru, _forward_copies, after)
            return {k: full_view(k, a) for k, a in zip(keys, done)}

        return hook, finish

    p0, p1 = _layer_small(sm, 0, sel[0:1]), _layer_small(sm, 1, sel[0:1])
    h0 = _norm_call(x[0], p0["g1"], 2 * ts)
    proj_own = _inproj_part_call(h0, full_view("w_in", started_a[2][0]), 2 * ts, sel[2:3], 0, 1)
    ready = after_all([started_d[3], proj_own] + [p[k] for p in (p0, p1) for k in ("wa", "wx", "wm")])
    big0 = gather_finish("0a", first + ("conv_w",), started_a, ready)
    for l, p in enumerate((p0, p1)):
        p["cw"] = big0["conv_w"][l]
    proj0 = _inproj_part_call(h0, big0["w_in"], 2 * ts, sel[2:3], 1, N_QUARTERS - 1, proj_own)
    hook, finish = arrives("0b", rest, started_b)
    sv0 = _layer_fwd_mix(x[0], big0, p0, ts, h0, hook, proj0)
    big0.update(finish(sv0["yb_pre"]))
    x_mid = _layer_fwd_out(sv0, big0, ts)
    hook, finish = arrives("1a", first, started_c)
    h1 = _norm_call(x_mid, p1["g1"], 2 * ts, hook(x_mid))
    big1 = finish(h1)
    hook, finish = arrives("1b", rest, started_d)
    sv1 = _layer_fwd_mix(x_mid, big1, p1, ts, h1, hook)
    big1.update(finish(sv1["yb_pre"]))
    x_out = _layer_fwd_out(sv1, big1, ts)
    dx, loss, dgf = _loss_call(x_out, loss_target[0], final_norm_g.reshape(1, -1), 2 * ts)

    def pair_start(tag, gb, after):
        sends = [gb[k][1] for k in gb]
        zones = [lax.empty(a.shape, BF) for a in sends]
        return _exchange_start(f"pair_start_{tag}", sends + zones, _sibling_copies, len(sends), after)

    def reduce_start(tag, gb, after, pair=None):
        keys = tuple(gb)
        if pair is None:
            from_sibling = _sibling_send_call([gb[k][1] for k in keys])
        else:
            done = _exchange_wait(f"pair_wait_{tag}", pair[0], pair[1], pair[2], _sibling_copies, after)
            from_sibling = done[len(keys):]
        received = dict(zip(keys, from_sibling))
        groups = same_shape(keys)
        outs = _ew_jobs_call("pair_sum", [
            (lambda *t: tuple(a.astype(F32) + b.astype(F32) for a, b in zip(t[0::2], t[1::2])),
             [(a[None], (0, "g")) for k in ks for a in (gb[k][0], received[k])],
             [(sds((1,) + received[k].shape, BF), (0, "g")) for k in ks], N_QUARTERS, None) for ks in groups], sel)
        sums = dict(zip([k for ks in groups for k in ks], [o[0] for o in outs]))
        sums = [sums[k] for k in keys]
        zones = [lax.empty((3,) + a.shape[1:], BF) for a in sums]
        started = _exchange_start(f"reduce_start_{tag}", sums + zones, _owner_copies, 3 * len(keys), after)
        return keys, started

    def reduce_finish(tag, l, keys_started, after, reduced):
        keys, (send_sems, recv_sems, thru, _) = keys_started
        done = _exchange_wait(f"reduce_wait_{tag}", send_sems, recv_sems, thru, _owner_copies, after)
        sums, zones = done[:len(keys)], done[len(keys):]
        sums, zones = dict(zip(keys, sums)), dict(zip(keys, zones))
        groups = same_shape(keys)
        outs = _ew_jobs_call("quarter_sum", [
            (lambda *t: tuple(((a.astype(F32) + b.astype(F32)) + c.astype(F32)) + d.astype(F32)
                              for a, b, c, d in zip(t[0::4], t[1::4], t[2::4], t[3::4])),
             [op for k in ks for op in [(sums[k][None], (0, this_chip))] + [(zones[k][None], (0, j)) for j in range(3)]],
             [(sds((DEPTH, 2) + halves[k], F32), (l, this_core)) for k in ks], 1,
             [reduced[k] for k in ks] if ks[0] in reduced else None) for ks in groups], sel)
        reduced.update(zip([k for ks in groups for k in ks], outs))

    dx1, gb_ffn, gs1 = _layer_bwd_ffn(dx, sv1, big1, ts)
    merge_out, gb_merge = _layer_bwd_merge(dx1, sv1, big1, ts)
    dx_mid, gb_in, gs1_mix = _layer_bwd_branches(dx1, merge_out, sv1, big1, lru_lambda[1], ts)
    gb_1 = {**gb_ffn, **gb_merge, **gb_in}
    pair_1 = pair_start("1", gb_1, dx_mid)
    dx1, gb_ffn, gs0 = _layer_bwd_ffn(dx_mid, sv0, big0, ts, pair_1[3])
    exchange_1 = reduce_start("1", gb_1, dx1, pair_1)
    pair_0a = pair_start("0a", gb_ffn, exchange_1[1][3])
    merge_out, gb_merge = _layer_bwd_merge(dx1, sv0, big0, ts, pair_0a[3])
    exchange_0a = reduce_start("0a", gb_ffn, merge_out[0], pair_0a)
    pair_0b = pair_start("0b", gb_merge, exchange_0a[1][3])
    started_0b = {}

    def after_sgu(duv):
        started_0b["exchange"] = reduce_start("0b", gb_merge, duv, pair_0b)
        return started_0b["exchange"][1][3]

    grad_x, gb_in, gs0_mix = _layer_bwd_branches(dx1, merge_out, sv0, big0, lru_lambda[0], ts, pair_0b[3],
                                                 after_sgu)
    exchange_0b = started_0b["exchange"]
    exchange_0c = reduce_start("0c", gb_in, exchange_0b[1][3])
    layer_gs = [{**gs0, **gs0_mix}, {**gs1, **gs1_mix}]
    gs = {k: jnp.stack([g[k] for g in layer_gs]) for k in layer_gs[0]}
    gs["final_norm_g"] = dgf[0]
    gs["loss"] = loss[0, 0:1]

    me = ("sel", 3)
    piece = (1, N_DEVICES, SMALL_ROWS, 128)
    packed = _pack_small(gs).reshape(piece)
    scatter = _exchange_start("small_scatter_start", [packed[0], lax.empty(piece[1:], F32)], _small_scatter_copies,
                              N_DEVICES - 1, exchange_0c[1][3])
    reduced = {}
    reduce_finish("1", 1, exchange_1, scatter[3], reduced)
    reduce_finish("0a", 0, exchange_0a, reduced["w_in"], reduced)
    reduce_finish("0b", 0, exchange_0b, reduced["w_down"], reduced)

    def swap_slabs(ref, c, i):
        layers = (1,) if BIG[i] == "w_in" else range(DEPTH)
        return [(ref.at[l, c], ref.at[l, 1 - c]) for l in layers]

    swapped = dict(zip(BIG, _sibling_inplace_call("grads_swap_halves", [reduced[k] for k in BIG], swap_slabs,
                                                  DEPTH * len(BIG) - 1)))

    def adamw_layers(keys, grads, layer, into, after=None):
        if layer is None:
            views = [_as4(_as_rows(a)) for k in keys for a in (w[k], grads[k], m[k], v[k])]
            idx = (0, 0)
        else:
            views = [a.reshape((1,) + w[k].shape) for k in keys for a in (w[k], grads[k], m[k], v[k])]
            idx = (0, layer)
        outs = _ew_call(lambda *t: tuple(o for i in range(0, len(t), 4) for o in _adamw(*t[i:i + 4])), "adamw_big",
                        [(a, idx) for a in views], [(sds(views[0].shape, F32), idx)] * (3 * len(keys)),
                        into=into, after=after)
        return {k: outs[3 * i:3 * i + 3] for i, k in enumerate(keys)}

    updated = adamw_layers(("w_in",), swapped, 1, None)
    last_update = updated["w_in"][0]
    for ks in same_shape(k for k in BIG if k != "w_in"):
        updated.update(adamw_layers(ks, swapped, None, None, last_update))
        last_update = updated[ks[0]][0]
    scattered = _exchange_wait("small_scatter_wait", scatter[0], scatter[1], scatter[2], _small_scatter_copies,
                               last_update)
    summed = _ew_call(
        lambda *parts: (functools.reduce(lambda a, b: a + b, parts),), "small_sum",
        [(scattered[0][None], (0, me))]
        + [(scattered[1][None], (0, lambda g, s, k=k: s[3] ^ k)) for k in range(1, N_DEVICES)],
        [(sds(piece, F32), (0, me))], 1, sel)[0]
    spread = _exchange_start("small_spread_start", [summed[0]], _small_spread_copies, N_DEVICES - 1, summed)
    reduced["w_in"] = swapped["w_in"]
    reduce_finish("0c", 0, exchange_0c, spread[3], reduced)
    last = _sibling_inplace_call("grads_swap_last", [reduced["w_in"]],
                                 lambda ref, c, i: [(ref.at[0, c], ref.at[0, 1 - c])], 1)[0]
    swapped["w_in"] = last
    updated.update(adamw_layers(("w_in",), swapped, 0, updated["w_in"]))
    grads_big = {k: swapped[k].reshape(w[k].shape) for k in BIG}
    delta, new_m, new_v = ({k: updated[k][j].reshape(w[k].shape) for k in BIG} for j in range(3))
    gathered_small = _exchange_wait("small_spread_wait", spread[0], spread[1], spread[2], _small_spread_copies,
                                    updated["w_in"][0])[0]

    like = {k: jax.ShapeDtypeStruct(gs[k].shape, F32) for k in SMALL}
    like["loss"] = jax.ShapeDtypeStruct((1,), F32)
    grads_small = _unpack_small(gathered_small, like)
    total = grads_small.pop("loss")[0]
    conv_q = grads_small["conv_w"].reshape(DEPTH, CONV_WIDTH, N_QUARTERS, D_RNN // N_QUARTERS)
    grads_small["conv_w"] = lax.dynamic_index_in_dim(conv_q, chip, axis=2, keepdims=False)
    at_least_2d = lambda a: a.reshape(1, -1) if a.ndim == 1 else a
    outs = _small_adamw_call(*[[at_least_2d(d[k]) for k in SMALL] for d in (w, grads_small, m, v)])
    for d, o in zip((delta, new_m, new_v), outs):
        for k, a in zip(SMALL, o):
            d[k] = a.reshape(w[k].shape)

    grads = {**grads_big, **grads_small}
    return (total, grad_x[None], *[grads[k] for k in WEIGHTS], *[delta[k] for k in WEIGHTS],
            *[new_m[k] for k in WEIGHTS], *[new_v[k] for k in WEIGHTS])
```
